```python
import jax, jax.numpy as jnp
from jax import lax
import numpy as np

D_MODEL = 1024
BATCH = 8
SEQ = 4096
DEPTH = 1

CHUNK = 64
MIX_WIDTH = D_MODEL
CONV_CH = MIX_WIDTH // 2
CONV_WIDTH = 31
CONV_GROUPS = 8
HGRN_WIDTH = MIX_WIDTH - CONV_CH
HGRN_HEADS = 4
HGRN_DK = HGRN_WIDTH // HGRN_HEADS
HGRN_DV = HGRN_WIDTH // HGRN_HEADS
IN_COLS = 2 * CONV_CH + 4 * HGRN_WIDTH
D_FF = 4 * D_MODEL
RMS_EPS = 1e-6
GN_EPS = 1e-5

kernel_name = "hymba_conformer_hgrn2_adaln_block"


def _rms(x, w):
    xf = x.astype(jnp.float32)
    y = xf * lax.rsqrt(jnp.mean(xf * xf, axis=-1, keepdims=True) + RMS_EPS)
    return (y * w.astype(jnp.float32)).astype(x.dtype)


def _conv_mixer(u_val, u_gate, w_dw, b_dw, gn_gain, gn_bias):
    B, S, _ = u_val.shape
    u = u_val * jax.nn.sigmoid(u_gate)
    y = lax.conv_general_dilated(
        u, w_dw[:, None, :].astype(u.dtype), window_strides=(1,),
        padding=[(CONV_WIDTH - 1, 0)],
        dimension_numbers=("NWC", "WIO", "NWC"),
        feature_group_count=CONV_CH) + b_dw
    yf = y.astype(jnp.float32).reshape(B, S, CONV_GROUPS, CONV_CH // CONV_GROUPS)
    mu = jnp.mean(yf, axis=-1, keepdims=True)
    var = jnp.mean(jnp.square(yf - mu), axis=-1, keepdims=True)
    yf = ((yf - mu) * lax.rsqrt(var + GN_EPS)).reshape(B, S, CONV_CH)
    yf = yf * gn_gain.astype(jnp.float32) + gn_bias.astype(jnp.float32)
    return jax.nn.silu(yf).astype(u_val.dtype)


def _hgrn2(q, fl, i, g, lb, g_out):
    B, S, _ = q.shape
    N = S // CHUNK
    f32 = jnp.float32
    qf = jax.nn.silu(q.astype(f32))
    f = lb + (1.0 - lb) * jax.nn.sigmoid(fl.astype(f32))
    logf = jnp.log(f)
    k = 1.0 - f
    v = i.astype(f32)

    def to_chunks(t, d):
        return t.reshape(B, N, CHUNK, HGRN_HEADS, d).transpose(1, 0, 3, 2, 4)

    qc, kc, lc = to_chunks(qf, HGRN_DK), to_chunks(k, HGRN_DK), to_chunks(logf, HGRN_DK)
    vc = to_chunks(v, HGRN_DV)
    causal = jnp.tril(jnp.ones((CHUNK, CHUNK), dtype=bool))

    def step(state, inp):
        qb, kb, vb, lb_ = inp
        cum = jnp.cumsum(lb_, axis=2)
        diff = cum[:, :, :, None, :] - cum[:, :, None, :, :]
        decay = jnp.exp(jnp.where(causal[None, None, :, :, None], diff, -jnp.inf))
        att = jnp.einsum("bhtk,bhsk,bhtsk->bhts", qb, kb, decay)
        o = jnp.einsum("bhts,bhsv->bhtv", att, vb) + \
            jnp.einsum("bhtk,bhkv->bhtv", qb * jnp.exp(cum), state)
        last = cum[:, :, -1:, :]
        state = jnp.exp(last[:, :, 0, :])[..., None] * state + \
            jnp.einsum("bhsk,bhsv->bhkv", kb * jnp.exp(last - cum), vb)
        return state, o

    s0 = jnp.zeros((B, HGRN_HEADS, HGRN_DK, HGRN_DV), f32)
    _, oc = lax.scan(step, s0, (qc, kc, vc, lc))
    o = oc.transpose(1, 0, 3, 2, 4).reshape(B, S, HGRN_HEADS, HGRN_DV)
    o = _rms(o, g_out.reshape(HGRN_HEADS, HGRN_DV))
    o = o * jax.nn.silu(g.astype(f32).reshape(B, S, HGRN_HEADS, HGRN_DV))
    return o.reshape(B, S, HGRN_WIDTH).astype(q.dtype)


def _fwd_setup_inputs(seed: int = 0) -> dict:
    key = jax.random.key(seed)
    ks = jax.random.split(key, 20)
    nrm = jax.random.normal
    L, D = DEPTH, D_MODEL
    return {
        "x": nrm(ks[0], (BATCH, SEQ, D), jnp.float32),
        "c": nrm(ks[1], (BATCH, D), jnp.float32),
        "w_ada": nrm(ks[2], (L, D, 6 * D), jnp.float32) * D ** -0.5,
        "b_ada": nrm(ks[3], (L, 6 * D), jnp.float32) * 0.02,
        "lb_logits": nrm(ks[4], (L + 1, HGRN_WIDTH), jnp.float32) * 0.1,
        "g_pre_mix": 1.0 + 0.02 * nrm(ks[5], (L, D), jnp.float32),
        "w_in": nrm(ks[6], (L, D, IN_COLS), jnp.float32) * D ** -0.5,
        "b_in": nrm(ks[7], (L, IN_COLS), jnp.float32) * 0.02,
        "w_dw": nrm(ks[8], (L, CONV_WIDTH, CONV_CH), jnp.float32) * CONV_WIDTH ** -0.5,
        "b_dw": nrm(ks[9], (L, CONV_CH), jnp.float32) * 0.02,
        "gn_gain": 1.0 + 0.02 * nrm(ks[10], (L, CONV_CH), jnp.float32),
        "gn_bias": nrm(ks[11], (L, CONV_CH), jnp.float32) * 0.02,
        "g_hgrn_out": 1.0 + 0.02 * nrm(ks[12], (L, HGRN_WIDTH), jnp.float32),
        "w_out": nrm(ks[13], (L, MIX_WIDTH, D), jnp.float32) * MIX_WIDTH ** -0.5,
        "g_post_mix": 1.0 + 0.02 * nrm(ks[14], (L, D), jnp.float32),
        "g_pre_ffn": 1.0 + 0.02 * nrm(ks[15], (L, D), jnp.float32),
        "w_up": nrm(ks[16], (L, D, D_FF), jnp.float32) * D ** -0.5,
        "w_down": nrm(ks[17], (L, D_FF, D), jnp.float32) * D_FF ** -0.5,
        "g_post_ffn": 1.0 + 0.02 * nrm(ks[18], (L, D), jnp.float32),
    }


def _fwd_reference(x, c, w_ada, b_ada, lb_logits, g_pre_mix, w_in, b_in, w_dw, b_dw, gn_gain,
              gn_bias, g_hgrn_out, w_out, g_post_mix, g_pre_ffn, w_up, w_down, g_post_ffn):
    lb_all = jnp.cumsum(jax.nn.softmax(lb_logits.astype(jnp.float32), axis=0), axis=0)
    c_act = jax.nn.silu(c)
    for l in range(DEPTH):
        mod = c_act @ w_ada[l] + b_ada[l]
        sh_m, sc_m, gt_m, sh_f, sc_f, gt_f = [m[:, None, :] for m in jnp.split(mod, 6, axis=-1)]

        h = _rms(x, g_pre_mix[l]) * (1.0 + sc_m) + sh_m
        p = h @ w_in[l] + b_in[l]
        o0 = 2 * CONV_CH
        cv = _conv_mixer(p[..., :CONV_CH], p[..., CONV_CH:o0],
                         w_dw[l], b_dw[l], gn_gain[l], gn_bias[l])
        W = HGRN_WIDTH
        hg = _hgrn2(p[..., o0:o0 + W], p[..., o0 + W:o0 + 2 * W],
                    p[..., o0 + 2 * W:o0 + 3 * W], p[..., o0 + 3 * W:o0 + 4 * W],
                    lb_all[l], g_hgrn_out[l])
        y = jnp.concatenate([cv, hg], axis=-1) @ w_out[l]
        x = x + gt_m * _rms(y, g_post_mix[l])

        h = _rms(x, g_pre_ffn[l]) * (1.0 + sc_f) + sh_f
        y = jnp.square(jax.nn.relu(h @ w_up[l])) @ w_down[l]
        x = x + gt_f * _rms(y, g_post_ffn[l])
    return x


import jax as _jax
import jax.numpy as _jnp

TWIN_FORMAT = 'train_step'
FWD_PARAMS = ['x', 'c', 'w_ada', 'b_ada', 'lb_logits', 'g_pre_mix', 'w_in', 'b_in', 'w_dw', 'b_dw', 'gn_gain', 'gn_bias', 'g_hgrn_out', 'w_out', 'g_post_mix', 'g_pre_ffn', 'w_up', 'w_down', 'g_post_ffn']
TWIN_WEIGHTS = ['w_ada', 'b_ada', 'lb_logits', 'g_pre_mix', 'w_in', 'b_in', 'w_dw', 'b_dw', 'gn_gain', 'gn_bias', 'g_hgrn_out', 'w_out', 'g_post_mix', 'g_pre_ffn', 'w_up', 'w_down', 'g_post_ffn']
TWIN_DIFF_INPUT = 'x'
TWIN_INPUTS = ['x', 'c', 'w_ada', 'b_ada', 'lb_logits', 'g_pre_mix', 'w_in', 'b_in', 'w_dw', 'b_dw', 'gn_gain', 'gn_bias', 'g_hgrn_out', 'w_out', 'g_post_mix', 'g_pre_ffn', 'w_up', 'w_down', 'g_post_ffn', 'loss_target', 'm_w_ada', 'm_b_ada', 'm_lb_logits', 'm_g_pre_mix', 'm_w_in', 'm_b_in', 'm_w_dw', 'm_b_dw', 'm_gn_gain', 'm_gn_bias', 'm_g_hgrn_out', 'm_w_out', 'm_g_post_mix', 'm_g_pre_ffn', 'm_w_up', 'm_w_down', 'm_g_post_ffn', 'v_w_ada', 'v_b_ada', 'v_lb_logits', 'v_g_pre_mix', 'v_w_in', 'v_b_in', 'v_w_dw', 'v_b_dw', 'v_gn_gain', 'v_gn_bias', 'v_g_hgrn_out', 'v_w_out', 'v_g_post_mix', 'v_g_pre_ffn', 'v_w_up', 'v_w_down', 'v_g_post_ffn']
TWIN_OUTPUTS = ['loss', 'grad_x', 'grad_w_ada', 'grad_b_ada', 'grad_lb_logits', 'grad_g_pre_mix', 'grad_w_in', 'grad_b_in', 'grad_w_dw', 'grad_b_dw', 'grad_gn_gain', 'grad_gn_bias', 'grad_g_hgrn_out', 'grad_w_out', 'grad_g_post_mix', 'grad_g_pre_ffn', 'grad_w_up', 'grad_w_down', 'grad_g_post_ffn', 'delta_w_ada', 'delta_b_ada', 'delta_lb_logits', 'delta_g_pre_mix', 'delta_w_in', 'delta_b_in', 'delta_w_dw', 'delta_b_dw', 'delta_gn_gain', 'delta_gn_bias', 'delta_g_hgrn_out', 'delta_w_out', 'delta_g_post_mix', 'delta_g_pre_ffn', 'delta_w_up', 'delta_w_down', 'delta_g_post_ffn', 'new_m_w_ada', 'new_m_b_ada', 'new_m_lb_logits', 'new_m_g_pre_mix', 'new_m_w_in', 'new_m_b_in', 'new_m_w_dw', 'new_m_b_dw', 'new_m_gn_gain', 'new_m_gn_bias', 'new_m_g_hgrn_out', 'new_m_w_out', 'new_m_g_post_mix', 'new_m_g_pre_ffn', 'new_m_w_up', 'new_m_w_down', 'new_m_g_post_ffn', 'new_v_w_ada', 'new_v_b_ada', 'new_v_lb_logits', 'new_v_g_pre_mix', 'new_v_w_in', 'new_v_b_in', 'new_v_w_dw', 'new_v_b_dw', 'new_v_gn_gain', 'new_v_gn_bias', 'new_v_g_hgrn_out', 'new_v_w_out', 'new_v_g_post_mix', 'new_v_g_pre_ffn', 'new_v_w_up', 'new_v_w_down', 'new_v_g_post_ffn']
TWIN_LEAF_KINDS = {'loss': 'loss', 'grad_x': 'grad_x', 'grad_w_ada': 'grad_w', 'grad_b_ada': 'grad_w', 'grad_lb_logits': 'grad_w', 'grad_g_pre_mix': 'grad_w', 'grad_w_in': 'grad_w', 'grad_b_in': 'grad_w', 'grad_w_dw': 'grad_w', 'grad_b_dw': 'grad_w', 'grad_gn_gain': 'grad_w', 'grad_gn_bias': 'grad_w', 'grad_g_hgrn_out': 'grad_w', 'grad_w_out': 'grad_w', 'grad_g_post_mix': 'grad_w', 'grad_g_pre_ffn': 'grad_w', 'grad_w_up': 'grad_w', 'grad_w_down': 'grad_w', 'grad_g_post_ffn': 'grad_w', 'delta_w_ada': 'delta_w', 'delta_b_ada': 'delta_w', 'delta_lb_logits': 'delta_w', 'delta_g_pre_mix': 'delta_w', 'delta_w_in': 'delta_w', 'delta_b_in': 'delta_w', 'delta_w_dw': 'delta_w', 'delta_b_dw': 'delta_w', 'delta_gn_gain': 'delta_w', 'delta_gn_bias': 'delta_w', 'delta_g_hgrn_out': 'delta_w', 'delta_w_out': 'delta_w', 'delta_g_post_mix': 'delta_w', 'delta_g_pre_ffn': 'delta_w', 'delta_w_up': 'delta_w', 'delta_w_down': 'delta_w', 'delta_g_post_ffn': 'delta_w', 'new_m_w_ada': 'new_m', 'new_m_b_ada': 'new_m', 'new_m_lb_logits': 'new_m', 'new_m_g_pre_mix': 'new_m', 'new_m_w_in': 'new_m', 'new_m_b_in': 'new_m', 'new_m_w_dw': 'new_m', 'new_m_b_dw': 'new_m', 'new_m_gn_gain': 'new_m', 'new_m_gn_bias': 'new_m', 'new_m_g_hgrn_out': 'new_m', 'new_m_w_out': 'new_m', 'new_m_g_post_mix': 'new_m', 'new_m_g_pre_ffn': 'new_m', 'new_m_w_up': 'new_m', 'new_m_w_down': 'new_m', 'new_m_g_post_ffn': 'new_m', 'new_v_w_ada': 'new_v', 'new_v_b_ada': 'new_v', 'new_v_lb_logits': 'new_v', 'new_v_g_pre_mix': 'new_v', 'new_v_w_in': 'new_v', 'new_v_b_in': 'new_v', 'new_v_w_dw': 'new_v', 'new_v_b_dw': 'new_v', 'new_v_gn_gain': 'new_v', 'new_v_gn_bias': 'new_v', 'new_v_g_hgrn_out': 'new_v', 'new_v_w_out': 'new_v', 'new_v_g_post_mix': 'new_v', 'new_v_g_pre_ffn': 'new_v', 'new_v_w_up': 'new_v', 'new_v_w_down': 'new_v', 'new_v_g_post_ffn': 'new_v'}


def _forward(args):
    return _fwd_reference(*[args[k] for k in FWD_PARAMS])


def _output_shape():
    out = _jax.eval_shape(lambda: _forward(_fwd_setup_inputs(0)))
    return out.shape, out.dtype

N_MICROBATCH = 1
ADAM_LR = 0.001
ADAM_B1 = 0.9
ADAM_B2 = 0.999
ADAM_EPS = 1e-08
ADAM_WD = 0.01
ADAM_STEP = 10
PER_EXAMPLE_BATCH_AXIS = {'x': 0, 'c': 0, 'loss_target': 0}
SHARED_INPUTS = []
_WEIGHT_DTYPES = {'w_ada': _jnp.float32, 'b_ada': _jnp.float32, 'lb_logits': _jnp.float32, 'g_pre_mix': _jnp.float32, 'w_in': _jnp.float32, 'b_in': _jnp.float32, 'w_dw': _jnp.float32, 'b_dw': _jnp.float32, 'gn_gain': _jnp.float32, 'gn_bias': _jnp.float32, 'g_hgrn_out': _jnp.float32, 'w_out': _jnp.float32, 'g_post_mix': _jnp.float32, 'g_pre_ffn': _jnp.float32, 'w_up': _jnp.float32, 'w_down': _jnp.float32, 'g_post_ffn': _jnp.float32}
MOMENT_SCALE = {'w_ada': 4.133096e+00, 'b_ada': 7.523187e+00, 'lb_logits': 1.820450e-02, 'g_pre_mix': 2.628823e-01, 'w_in': 1.518633e+00, 'b_in': 2.596121e+00, 'w_dw': 1.629336e+00, 'b_dw': 5.011236e+00, 'gn_gain': 3.193046e+00, 'gn_bias': 3.998560e+00, 'g_hgrn_out': 3.061093e+00, 'w_out': 2.693761e+00, 'g_post_mix': 1.558858e+01, 'g_pre_ffn': 9.196664e-01, 'w_up': 1.093726e+00, 'w_down': 3.100573e+00, 'g_post_ffn': 1.592174e+01}


def _to_microbatches(a, axis):
    t = _jnp.moveaxis(a, axis, 0)
    t = t.reshape((N_MICROBATCH, t.shape[0] // N_MICROBATCH) + t.shape[1:])
    return _jnp.moveaxis(t, 1, axis + 1)


def setup_inputs(seed: int = 0) -> dict:
    inp = _fwd_setup_inputs(seed)
    key = _jax.random.fold_in(_jax.random.key(seed), 7919)
    shape, _ = _output_shape()
    out = dict(inp)
    out["loss_target"] = _jax.random.normal(_jax.random.fold_in(key, 0), shape, _jnp.float32)
    for i, name in enumerate(TWIN_WEIGHTS):
        w = inp[name].astype(_jnp.float32)
        if MOMENT_SCALE is None:
            s = _jnp.sqrt(_jnp.mean(_jnp.square(w)) + 1e-30)
        else:
            s = MOMENT_SCALE[name]
        km, kv = _jax.random.split(_jax.random.fold_in(key, i + 1))
        out[name] = w
        out["m_" + name] = s * _jax.random.normal(km, w.shape, _jnp.float32)
        out["v_" + name] = (s * s) * _jax.random.uniform(kv, w.shape, _jnp.float32, 0.5, 1.5)
    if N_MICROBATCH > 1:
        for name, axis in PER_EXAMPLE_BATCH_AXIS.items():
            out[name] = _to_microbatches(out[name], axis)
    return {'x': out['x'], 'c': out['c'], 'w_ada': out['w_ada'], 'b_ada': out['b_ada'], 'lb_logits': out['lb_logits'], 'g_pre_mix': out['g_pre_mix'], 'w_in': out['w_in'], 'b_in': out['b_in'], 'w_dw': out['w_dw'], 'b_dw': out['b_dw'], 'gn_gain': out['gn_gain'], 'gn_bias': out['gn_bias'], 'g_hgrn_out': out['g_hgrn_out'], 'w_out': out['w_out'], 'g_post_mix': out['g_post_mix'], 'g_pre_ffn': out['g_pre_ffn'], 'w_up': out['w_up'], 'w_down': out['w_down'], 'g_post_ffn': out['g_post_ffn'], 'loss_target': out['loss_target'], 'm_w_ada': out['m_w_ada'], 'm_b_ada': out['m_b_ada'], 'm_lb_logits': out['m_lb_logits'], 'm_g_pre_mix': out['m_g_pre_mix'], 'm_w_in': out['m_w_in'], 'm_b_in': out['m_b_in'], 'm_w_dw': out['m_w_dw'], 'm_b_dw': out['m_b_dw'], 'm_gn_gain': out['m_gn_gain'], 'm_gn_bias': out['m_gn_bias'], 'm_g_hgrn_out': out['m_g_hgrn_out'], 'm_w_out': out['m_w_out'], 'm_g_post_mix': out['m_g_post_mix'], 'm_g_pre_ffn': out['m_g_pre_ffn'], 'm_w_up': out['m_w_up'], 'm_w_down': out['m_w_down'], 'm_g_post_ffn': out['m_g_post_ffn'], 'v_w_ada': out['v_w_ada'], 'v_b_ada': out['v_b_ada'], 'v_lb_logits': out['v_lb_logits'], 'v_g_pre_mix': out['v_g_pre_mix'], 'v_w_in': out['v_w_in'], 'v_b_in': out['v_b_in'], 'v_w_dw': out['v_w_dw'], 'v_b_dw': out['v_b_dw'], 'v_gn_gain': out['v_gn_gain'], 'v_gn_bias': out['v_gn_bias'], 'v_g_hgrn_out': out['v_g_hgrn_out'], 'v_w_out': out['v_w_out'], 'v_g_post_mix': out['v_g_post_mix'], 'v_g_pre_ffn': out['v_g_pre_ffn'], 'v_w_up': out['v_w_up'], 'v_w_down': out['v_w_down'], 'v_g_post_ffn': out['v_g_post_ffn']}


def _loss(weights, diff, rest, loss_target):
    with _jax.named_scope("forward"):
        args = {**rest, TWIN_DIFF_INPUT: diff, **{k: w.astype(_WEIGHT_DTYPES[k]) for k, w in weights.items()}}
        y = _forward(args)
    with _jax.named_scope("loss_head"):
        err = _jnp.square(y.astype(_jnp.float32) - loss_target)
        return 0.5 * _jnp.sum(_jnp.mean(err, axis=-1)) if err.ndim else 0.5 * err


def _adamw(w, g, m, v):
    m = ADAM_B1 * m + (1.0 - ADAM_B1) * g
    v = ADAM_B2 * v + (1.0 - ADAM_B2) * _jnp.square(g)
    m_hat = m / (1.0 - ADAM_B1 ** ADAM_STEP)
    v_hat = v / (1.0 - ADAM_B2 ** ADAM_STEP)
    delta = -ADAM_LR * (m_hat / (_jnp.sqrt(v_hat) + ADAM_EPS) + ADAM_WD * w)
    return delta, m, v


def reference(x, c, w_ada, b_ada, lb_logits, g_pre_mix, w_in, b_in, w_dw, b_dw, gn_gain, gn_bias, g_hgrn_out, w_out, g_post_mix, g_pre_ffn, w_up, w_down, g_post_ffn, loss_target, m_w_ada, m_b_ada, m_lb_logits, m_g_pre_mix, m_w_in, m_b_in, m_w_dw, m_b_dw, m_gn_gain, m_gn_bias, m_g_hgrn_out, m_w_out, m_g_post_mix, m_g_pre_ffn, m_w_up, m_w_down, m_g_post_ffn, v_w_ada, v_b_ada, v_lb_logits, v_g_pre_mix, v_w_in, v_b_in, v_w_dw, v_b_dw, v_gn_gain, v_gn_bias, v_g_hgrn_out, v_w_out, v_g_post_mix, v_g_pre_ffn, v_w_up, v_w_down, v_g_post_ffn):
    given = dict(x=x, c=c, w_ada=w_ada, b_ada=b_ada, lb_logits=lb_logits, g_pre_mix=g_pre_mix, w_in=w_in, b_in=b_in, w_dw=w_dw, b_dw=b_dw, gn_gain=gn_gain, gn_bias=gn_bias, g_hgrn_out=g_hgrn_out, w_out=w_out, g_post_mix=g_post_mix, g_pre_ffn=g_pre_ffn, w_up=w_up, w_down=w_down, g_post_ffn=g_post_ffn, loss_target=loss_target, m_w_ada=m_w_ada, m_b_ada=m_b_ada, m_lb_logits=m_lb_logits, m_g_pre_mix=m_g_pre_mix, m_w_in=m_w_in, m_b_in=m_b_in, m_w_dw=m_w_dw, m_b_dw=m_b_dw, m_gn_gain=m_gn_gain, m_gn_bias=m_gn_bias, m_g_hgrn_out=m_g_hgrn_out, m_w_out=m_w_out, m_g_post_mix=m_g_post_mix, m_g_pre_ffn=m_g_pre_ffn, m_w_up=m_w_up, m_w_down=m_w_down, m_g_post_ffn=m_g_post_ffn, v_w_ada=v_w_ada, v_b_ada=v_b_ada, v_lb_logits=v_lb_logits, v_g_pre_mix=v_g_pre_mix, v_w_in=v_w_in, v_b_in=v_b_in, v_w_dw=v_w_dw, v_b_dw=v_b_dw, v_gn_gain=v_gn_gain, v_gn_bias=v_gn_bias, v_g_hgrn_out=v_g_hgrn_out, v_w_out=v_w_out, v_g_post_mix=v_g_post_mix, v_g_pre_ffn=v_g_pre_ffn, v_w_up=v_w_up, v_w_down=v_w_down, v_g_post_ffn=v_g_post_ffn)
    weights = {n: given[n] for n in TWIN_WEIGHTS}
    shared = {n: given[n] for n in SHARED_INPUTS}
    per_example = {n: given[n] for n in ['x', 'c']}
    grad_fn = _jax.value_and_grad(_loss, argnums=(0, 1))

    def one_microbatch(ex, loss_target):
        ex = dict(ex)
        diff = ex.pop(TWIN_DIFF_INPUT)
        return grad_fn(weights, diff, {**shared, **ex}, loss_target)

    if N_MICROBATCH == 1:
        loss, (grad_w, grad_x) = one_microbatch(per_example, given["loss_target"])
    else:
        def body(carry, xs):
            loss_sum, grad_sum = carry
            l_k, (gw_k, gx_k) = one_microbatch(xs[0], xs[1])
            with _jax.named_scope("update"):
                return (loss_sum + l_k, _jax.tree.map(_jnp.add, grad_sum, gw_k)), gx_k

        init = (_jnp.zeros((), _jnp.float32), _jax.tree.map(_jnp.zeros_like, weights))
        (loss, grad_w), grad_x = _jax.lax.scan(body, init, (per_example, given["loss_target"]))
    with _jax.named_scope("update"):
        delta_w, new_m, new_v = {}, {}, {}
        for n in TWIN_WEIGHTS:
            delta_w[n], new_m[n], new_v[n] = _adamw(weights[n], grad_w[n], given["m_" + n], given["v_" + n])
    return (loss, grad_x, *[grad_w[n] for n in TWIN_WEIGHTS], *[delta_w[n] for n in TWIN_WEIGHTS],
            *[new_m[n] for n in TWIN_WEIGHTS], *[new_v[n] for n in TWIN_WEIGHTS])
```

```python
import jax
import jax.numpy as jnp
from jax import lax
from jax.experimental import pallas as pl
from jax.experimental.pallas import tpu as pltpu

F32, BF16 = jnp.float32, jnp.bfloat16
D_MODEL = 1024
CONV_CH = 512
HGRN_W = 512
N_HEADS = 4
HEAD_D = 128
CONV_K = 31
GN_GROUP = 64
GN_SHIFT = 6
IN_COLS = 3072
D_FF = 4096
CHUNK = 64
CHUNK_SHIFT = 6
N_CHIPS = 4
N_DEV = 8
RMS_EPS = 1e-6
GN_EPS = 1e-5
ADAM_LR, ADAM_B1, ADAM_B2, ADAM_EPS, ADAM_WD, ADAM_STEP = 0.001, 0.9, 0.999, 1e-08, 0.01, 10
TOK_TILE = 512
HALO = 32
CONV_ROWS = 64
MIB = 1 << 20
MESH = pl.DeviceIdType.MESH
OTHER_CHIPS = ((1, 0), (0, 1), (1, 1))


def _cp(sem=None, vmem_mib=48):
    return pltpu.CompilerParams(dimension_semantics=sem, vmem_limit_bytes=vmem_mib * MIB)


def _dot(a, b):
    return jnp.dot(a, b, preferred_element_type=F32)


def _dot_nt(a, b):
    return lax.dot_general(a, b, (((1,), (1,)), ((), ())), preferred_element_type=F32)


def _dot_tn(a, b):
    return lax.dot_general(a, b, (((0,), (0,)), ((), ())), preferred_element_type=F32)


def _sig(v):
    return jax.nn.sigmoid(v)


def _colsum(v):
    return jnp.sum(v, axis=0, keepdims=True)


def _flip(v, b):
    return 1 - v if b else v


def _rcopy(src, dst, ssem, rsem, dev):
    return pltpu.make_async_remote_copy(src_ref=src, dst_ref=dst, send_sem=ssem, recv_sem=rsem,
                                        device_id=dev, device_id_type=MESH)


def _place():
    return lax.axis_index("x"), lax.axis_index("y"), lax.axis_index("c")


def _full(shape):
    return pl.BlockSpec(shape, lambda *_: (0,) * len(shape))


def _split2(v):
    hi = v.astype(BF16)
    lo = (v - hi.astype(F32)).astype(BF16)
    return hi, lo


def _split3(v):
    h1 = v.astype(BF16)
    r1 = v - h1.astype(F32)
    h2 = r1.astype(BF16)
    h3 = (r1 - h2.astype(F32)).astype(BF16)
    return h1, h2, h3


def _mm3(mat, v):
    h1, h2, h3 = _split3(v)
    return _dot(mat, h1) + _dot(mat, h2) + _dot(mat, h3)


def _gn_matrix():
    r = lax.broadcasted_iota(jnp.int32, (CONV_CH, CONV_CH), 0) >> GN_SHIFT
    c = lax.broadcasted_iota(jnp.int32, (CONV_CH, CONV_CH), 1) >> GN_SHIFT
    return jnp.where(r == c, 1.0 / GN_GROUP, 0.0).astype(BF16)


def _gmean(v, gmat):
    hi, lo = _split2(v)
    return _dot(hi, gmat) + _dot(lo, gmat)


def _chunk_masks(tm):
    r = lax.broadcasted_iota(jnp.int32, (tm, tm), 0)
    c = lax.broadcasted_iota(jnp.int32, (tm, tm), 1)
    same = (r >> CHUNK_SHIFT) == (c >> CHUNK_SHIFT)
    one = lambda m: jnp.where(m, 1.0, 0.0).astype(BF16)
    return one(same & (c <= r)), one(same & (c >= r)), one(same)


def _tri():
    return lax.broadcasted_iota(jnp.int32, (CHUNK, CHUNK), 0) >= lax.broadcasted_iota(jnp.int32, (CHUNK, CHUNK), 1)


def _lower_bound(lbl_ref):
    l0, l1 = lbl_ref[0:1, :], lbl_ref[1:2, :]
    mx = jnp.maximum(l0, l1)
    e0, e1 = jnp.exp(l0 - mx), jnp.exp(l1 - mx)
    return e0 / (e0 + e1), e1 / (e0 + e1)


def _hgrn_prep(pq, pf, lb, lower, same):
    sq = _sig(pq)
    qf = pq * sq
    sf = _sig(pf)
    f = lb + (1.0 - lb) * sf
    logf = jnp.log(f)
    k = 1.0 - f
    G = _mm3(lower, logf)
    Gl = _mm3(same, logf)
    eG, enG, eGlG = jnp.exp(G), jnp.exp(-G), jnp.exp(Gl - G)
    return dict(sq=sq, sf=sf, f=f, Gl=Gl, eG=eG, enG=enG, eGlG=eGlG, qt=qf * eG, kt=k * enG, kh=k * eGlG)


def _ada_exchange(c_row, w_ada, b_sh, wdw_pad):
    ncol = w_ada.shape[1]

    def body(c_ref, w_ref, b_ref, wdw_ref, call_ref, c8_ref, modg_ref, wdwg_ref, rows_s, sa, ra, sw, rw, sm, rm):
        x, y, c = _place()
        me = 4 * x + 2 * y + c
        chip = 2 * x + y
        cv = c_ref[...]
        call_ref[me] = cv * _sig(cv)
        wdwg_ref[chip] = wdw_ref[...]
        sends = []
        for m in range(1, N_DEV):
            peer = (_flip(x, m >> 2), _flip(y, (m >> 1) & 1), _flip(c, m & 1))
            cp = _rcopy(call_ref.at[me], call_ref.at[me], sa.at[m - 1], ra.at[m - 1], peer)
            cp.start()
            sends.append(cp)
        for k, (fx, fy) in enumerate(OTHER_CHIPS):
            peer = (_flip(x, fx), _flip(y, fy), c)
            cp = _rcopy(wdwg_ref.at[chip], wdwg_ref.at[chip], sw.at[k], rw.at[k], peer)
            cp.start()
            sends.append(cp)
        for m in range(1, N_DEV):
            peer = (_flip(x, m >> 2), _flip(y, (m >> 1) & 1), _flip(c, m & 1))
            pid = 4 * peer[0] + 2 * peer[1] + peer[2]
            _rcopy(call_ref.at[pid], call_ref.at[pid], sa.at[m - 1], ra.at[m - 1], peer).wait_recv()
        for b in range(N_DEV):
            c8_ref[b:b + 1, :] = call_ref[b]
        mod_all = _dot(c8_ref[...].astype(BF16), w_ref[...].astype(BF16)) + b_ref[...]
        for b in range(N_DEV):
            rows_s[b] = mod_all[b:b + 1, :]
        modg_ref[chip] = rows_s[me]
        for k, (fx, fy) in enumerate(OTHER_CHIPS):
            peer = (_flip(x, fx), _flip(y, fy), c)
            pid = 4 * peer[0] + 2 * peer[1] + peer[2]
            cp = _rcopy(rows_s.at[pid], modg_ref.at[chip], sm.at[k], rm.at[k], peer)
            cp.start()
            sends.append(cp)
        for k, (fx, fy) in enumerate(OTHER_CHIPS):
            peer = (_flip(x, fx), _flip(y, fy), c)
            pchip = 2 * peer[0] + peer[1]
            _rcopy(rows_s.at[0], modg_ref.at[pchip], sm.at[k], rm.at[k], peer).wait_recv()
            _rcopy(wdwg_ref.at[pchip], wdwg_ref.at[pchip], sw.at[k], rw.at[k], peer).wait_recv()
        for cp in sends:
            cp.wait_send()

    vm = pl.BlockSpec(memory_space=pltpu.VMEM)
    return pl.pallas_call(
        body, name="ada_exchange",
        out_shape=[jax.ShapeDtypeStruct((N_DEV, 1, D_MODEL), F32), jax.ShapeDtypeStruct((N_DEV, D_MODEL), F32),
                   jax.ShapeDtypeStruct((N_CHIPS, 1, ncol), F32), jax.ShapeDtypeStruct((N_CHIPS, HALO, HEAD_D), F32)],
        in_specs=[vm] * 4, out_specs=[vm] * 4,
        scratch_shapes=[pltpu.VMEM((N_DEV, 1, ncol), F32),
                        pltpu.SemaphoreType.DMA((N_DEV - 1,)), pltpu.SemaphoreType.DMA((N_DEV - 1,)),
                        pltpu.SemaphoreType.DMA((3,)), pltpu.SemaphoreType.DMA((3,)),
                        pltpu.SemaphoreType.DMA((3,)), pltpu.SemaphoreType.DMA((3,))],
        compiler_params=_cp(None, 32),
    )(c_row, w_ada, b_sh, wdw_pad)


def _weight_gather(shards):
    nt = len(shards)

    def body(*refs):
        ins, outs = refs[:nt], refs[nt:2 * nt]
        ssem, rsem = refs[2 * nt], refs[2 * nt + 1]
        x, y, c = _place()
        chip = 2 * x + y
        sibling = (x, y, 1 - c)
        for t in range(nt):
            outs[t][chip] = ins[t][...].astype(BF16)
        copies = []

        def slab(t, ch, core):
            hs = ins[t].shape[0] // 2
            return outs[t].at[ch, pl.ds(core * hs, hs), :]

        for k, (fx, fy) in enumerate(OTHER_CHIPS):
            peer = (_flip(x, fx), _flip(y, fy), c)
            for t in range(nt):
                cp = _rcopy(slab(t, chip, c), slab(t, chip, c), ssem.at[t * 3 + k], rsem.at[t * 3 + k], peer)
                cp.start()
                copies.append(cp)
        for k, (fx, fy) in enumerate(OTHER_CHIPS):
            peer = (_flip(x, fx), _flip(y, fy), c)
            pchip = 2 * peer[0] + peer[1]
            for t in range(nt):
                _rcopy(slab(t, pchip, c), slab(t, pchip, c), ssem.at[t * 3 + k], rsem.at[t * 3 + k], peer).wait_recv()
                cp = _rcopy(slab(t, pchip, c), slab(t, pchip, c), ssem.at[3 * nt + t * 3 + k],
                            rsem.at[3 * nt + t * 3 + k], sibling)
                cp.start()
                copies.append(cp)
        for k, (fx, fy) in enumerate(OTHER_CHIPS):
            pchip = 2 * _flip(x, fx) + _flip(y, fy)
            for t in range(nt):
                _rcopy(slab(t, pchip, 1 - c), slab(t, pchip, 1 - c), ssem.at[3 * nt + t * 3 + k],
                       rsem.at[3 * nt + t * 3 + k], sibling).wait_recv()
        for cp in copies:
            cp.wait_send()

    vm = pl.BlockSpec(memory_space=pltpu.VMEM)
    return pl.pallas_call(
        body, name="weight_gather",
        out_shape=[jax.ShapeDtypeStruct((N_CHIPS,) + s.shape, BF16) for s in shards],
        in_specs=[vm] * nt, out_specs=[vm] * nt,
        scratch_shapes=[pltpu.SemaphoreType.DMA((6 * nt,)), pltpu.SemaphoreType.DMA((6 * nt,))],
        compiler_params=_cp(None, 52),
    )(*shards)


def _stats_gather(stats):
    rows, cols = stats.shape

    def body(s_ref, out_ref, ssem, rsem, lsem):
        x, y, c = _place()
        me, sibling = (x, y, c), (x, y, 1 - c)
        chips = [(_flip(x, fx), _flip(y, fy)) for fx, fy in OTHER_CHIPS]

        def blk(px, py, pc):
            return out_ref.at[4 * px + 2 * py + pc]

        def copy(k, block, to, src=None):
            return _rcopy(blk(*block) if src is None else src, blk(*block), ssem.at[k], rsem.at[k], to)

        mine = pltpu.make_async_copy(s_ref, blk(*me), lsem)
        mine.start()
        first = [copy(0, me, sibling, src=s_ref)]
        first += [copy(1 + j, me, (*chip, c), src=s_ref) for j, chip in enumerate(chips)]
        for cp in first:
            cp.start()
        passed = [copy(4 + j, (*chip, c), sibling) for j, chip in enumerate(chips)]
        for j, chip in enumerate(chips):
            copy(1 + j, (*chip, c), me).wait_recv()
            passed[j].start()
        copy(0, sibling, me).wait_recv()
        for j, chip in enumerate(chips):
            copy(4 + j, (*chip, 1 - c), me).wait_recv()
        for cp in first + passed:
            cp.wait_send()
        mine.wait()

    vm = pl.BlockSpec(memory_space=pltpu.VMEM)
    return pl.pallas_call(
        body, name="stats_gather",
        out_shape=jax.ShapeDtypeStruct((N_DEV, rows, cols), F32),
        in_specs=[vm], out_specs=vm,
        scratch_shapes=[pltpu.SemaphoreType.DMA((7,)), pltpu.SemaphoreType.DMA((7,)), pltpu.SemaphoreType.DMA],
        compiler_params=_cp(None, 32),
    )(stats)


def _pair_swap(grads):
    nt = len(grads)
    hbm = pl.BlockSpec(memory_space=pl.ANY)

    def body(*refs):
        ins, outs = refs[:nt], refs[nt:2 * nt]
        ssem, rsem = refs[2 * nt], refs[2 * nt + 1]
        x, y, c = _place()
        sibling = (x, y, 1 - c)
        copies = []
        for t in range(nt):
            hs = ins[t].shape[1] // 2
            cp = _rcopy(ins[t].at[:, pl.ds((1 - c) * hs, hs), :], outs[t], ssem.at[t], rsem.at[t], sibling)
            cp.start()
            copies.append(cp)
        for cp in copies:
            cp.wait_recv()
        for cp in copies:
            cp.wait_send()

    return pl.pallas_call(
        body, name="grad_pair_swap",
        out_shape=[jax.ShapeDtypeStruct((g.shape[0], g.shape[1] // 2, g.shape[2]), F32) for g in grads],
        in_specs=[hbm] * nt, out_specs=[hbm] * nt,
        scratch_shapes=[pltpu.SemaphoreType.DMA((nt,)), pltpu.SemaphoreType.DMA((nt,))],
    )(*grads)


def _chip_exchange(pairs_b):
    nt = len(pairs_b)
    hbm = pl.BlockSpec(memory_space=pl.ANY)

    def body(*refs):
        ins, outs = refs[:nt], refs[nt:2 * nt]
        ssem, rsem, lsem = refs[2 * nt], refs[2 * nt + 1], refs[2 * nt + 2]
        x, y, c = _place()
        chip = 2 * x + y
        copies, locals_ = [], []
        for t in range(nt):
            lc = pltpu.make_async_copy(ins[t].at[chip], outs[t].at[chip], lsem.at[t])
            lc.start()
            locals_.append(lc)
        for k, (fx, fy) in enumerate(OTHER_CHIPS):
            peer = (_flip(x, fx), _flip(y, fy), c)
            pchip = 2 * peer[0] + peer[1]
            for t in range(nt):
                cp = _rcopy(ins[t].at[pchip], outs[t].at[chip], ssem.at[t * 3 + k], rsem.at[t * 3 + k], peer)
                cp.start()
                copies.append(cp)
        for k, (fx, fy) in enumerate(OTHER_CHIPS):
            peer = (_flip(x, fx), _flip(y, fy), c)
            pchip = 2 * peer[0] + peer[1]
            for t in range(nt):
                _rcopy(ins[t].at[pchip], outs[t].at[pchip], ssem.at[t * 3 + k], rsem.at[t * 3 + k], peer).wait_recv()
        for cp in copies:
            cp.wait_send()
        for lc in locals_:
            lc.wait()

    return pl.pallas_call(
        body, name="grad_chip_exchange",
        out_shape=[jax.ShapeDtypeStruct(p.shape, BF16) for p in pairs_b],
        in_specs=[hbm] * nt, out_specs=[hbm] * nt,
        scratch_shapes=[pltpu.SemaphoreType.DMA((3 * nt,)), pltpu.SemaphoreType.DMA((3 * nt,)),
                        pltpu.SemaphoreType.DMA((nt,))],
    )(*pairs_b)


def _half_swap(halves):
    nt = len(halves)
    hbm = pl.BlockSpec(memory_space=pl.ANY)

    def body(*refs):
        ins, outs = refs[:nt], refs[nt:2 * nt]
        ssem, rsem, lsem = refs[2 * nt], refs[2 * nt + 1], refs[2 * nt + 2]
        x, y, c = _place()
        sibling = (x, y, 1 - c)
        copies, locals_ = [], []
        for t in range(nt):
            hs = ins[t].shape[0]
            dst = outs[t].at[pl.ds(c * hs, hs), :]
            lc = pltpu.make_async_copy(ins[t], dst, lsem.at[t])
            lc.start()
            locals_.append(lc)
            cp = _rcopy(ins[t], dst, ssem.at[t], rsem.at[t], sibling)
            cp.start()
            copies.append(cp)
        for t in range(nt):
            hs = ins[t].shape[0]
            _rcopy(ins[t], outs[t].at[pl.ds((1 - c) * hs, hs), :], ssem.at[t], rsem.at[t], sibling).wait_recv()
        for cp in copies:
            cp.wait_send()
        for lc in locals_:
            lc.wait()

    return pl.pallas_call(
        body, name="grad_half_swap",
        out_shape=[jax.ShapeDtypeStruct((2 * h.shape[0], h.shape[1]), F32) for h in halves],
        in_specs=[hbm] * nt, out_specs=[hbm] * nt,
        scratch_shapes=[pltpu.SemaphoreType.DMA((nt,)), pltpu.SemaphoreType.DMA((nt,)),
                        pltpu.SemaphoreType.DMA((nt,))],
    )(*halves)


def _row_tile(rows):
    return min(rows, 256)


def _pair_sum(core, grad, got, name):
    nb, hs, cols = got.shape
    tr = _row_tile(hs)
    nr = hs // tr

    def body(core_ref, g_ref, o_ref, pf_ref, pb_ref):
        s = g_ref[...] + o_ref[...]
        pf_ref[...] = s
        pb_ref[...] = s.astype(BF16)

    gs = pltpu.PrefetchScalarGridSpec(
        num_scalar_prefetch=1, grid=(nb, nr),
        in_specs=[pl.BlockSpec((1, tr, cols), lambda j, i, cr: (j, cr[0] * nr + i, 0)),
                  pl.BlockSpec((1, tr, cols), lambda j, i, cr: (j, i, 0))],
        out_specs=[pl.BlockSpec((1, tr, cols), lambda j, i, cr: (j, i, 0)),
                   pl.BlockSpec((1, tr, cols), lambda j, i, cr: (j, i, 0))])
    return pl.pallas_call(
        body, name=name, grid_spec=gs,
        out_shape=[jax.ShapeDtypeStruct(got.shape, F32), jax.ShapeDtypeStruct(got.shape, BF16)],
        compiler_params=_cp(("arbitrary", "arbitrary"), 32),
    )(core, grad, got)


def _chip_sum(chip, pair_f, got_b, name):
    nb, hs, cols = got_b.shape
    tr = _row_tile(hs)

    def body(chip_ref, pf_ref, gb_ref, o_ref):
        me = chip_ref[0]
        acc = None
        for k in range(nb):
            term = jnp.where(me == k, pf_ref[0], gb_ref[k].astype(F32))
            acc = term if acc is None else acc + term
        o_ref[...] = acc

    gs = pltpu.PrefetchScalarGridSpec(
        num_scalar_prefetch=1, grid=(hs // tr,),
        in_specs=[pl.BlockSpec((1, tr, cols), lambda i, ch: (ch[0], i, 0)),
                  pl.BlockSpec((nb, tr, cols), lambda i, ch: (0, i, 0))],
        out_specs=pl.BlockSpec((tr, cols), lambda i, ch: (i, 0)))
    return pl.pallas_call(
        body, name=name, grid_spec=gs,
        out_shape=jax.ShapeDtypeStruct((hs, cols), F32),
        compiler_params=_cp(("arbitrary",), 32),
    )(chip, pair_f, got_b)


def _adam_math(w, g, m, v):
    m2 = ADAM_B1 * m + (1.0 - ADAM_B1) * g
    v2 = ADAM_B2 * v + (1.0 - ADAM_B2) * (g * g)
    m_hat = m2 / (1.0 - ADAM_B1 ** ADAM_STEP)
    v_hat = v2 / (1.0 - ADAM_B2 ** ADAM_STEP)
    delta = -ADAM_LR * (m_hat / (jnp.sqrt(v_hat) + ADAM_EPS) + ADAM_WD * w)
    return delta, m2, v2


def _adam_big(w, g, m, v, name):
    rows, cols = w.shape
    tr = _row_tile(rows)

    def body(w_ref, g_ref, m_ref, v_ref, d_out, m_out, v_out):
        d, m2, v2 = _adam_math(w_ref[...], g_ref[...], m_ref[...], v_ref[...])
        d_out[...] = d
        m_out[...] = m2
        v_out[...] = v2

    spec = pl.BlockSpec((tr, cols), lambda i: (i, 0))
    return pl.pallas_call(
        body, name=name, grid=(rows // tr,), in_specs=[spec] * 4, out_specs=[spec] * 3,
        out_shape=[jax.ShapeDtypeStruct(w.shape, F32)] * 3,
        compiler_params=_cp(("arbitrary",), 32),
    )(w, g, m, v)


def _ada_grad_adam(c8t, dmod_sh, w, m, v):
    rows, cols = w.shape
    tr = _row_tile(rows)

    def body(ct_ref, dm_ref, w_ref, m_ref, v_ref, g_out, d_out, m_out, v_out):
        g = None
        for b in range(N_DEV):
            term = ct_ref[:, b:b + 1] * dm_ref[b:b + 1, :]
            g = term if g is None else g + term
        d, m2, v2 = _adam_math(w_ref[...], g, m_ref[...], v_ref[...])
        g_out[...] = g
        d_out[...] = d
        m_out[...] = m2
        v_out[...] = v2

    spec = pl.BlockSpec((tr, cols), lambda i: (i, 0))
    return pl.pallas_call(
        body, name="ada_grad_adam", grid=(rows // tr,),
        in_specs=[pl.BlockSpec((tr, N_DEV), lambda i: (i, 0)), _full((N_DEV, cols)), spec, spec, spec],
        out_specs=[spec] * 4, out_shape=[jax.ShapeDtypeStruct(w.shape, F32)] * 4,
        compiler_params=_cp(("arbitrary",), 32),
    )(c8t, dmod_sh, w, m, v)


def _tok_tile(t):
    return min(TOK_TILE, t)


def _mix_in_fwd(x, modr, g_pre, w_in_g, b_in):
    T = x.shape[0]
    tm = _tok_tile(T)
    nb = IN_COLS // N_CHIPS

    def body(x_ref, mod_ref, g_ref, w_ref, b_ref, p_ref, h_ref):
        xv = x_ref[...]
        rstd = lax.rsqrt(jnp.mean(xv * xv, axis=-1, keepdims=True) + RMS_EPS)
        h = (xv * rstd) * g_ref[...] * (1.0 + mod_ref[1:2, :]) + mod_ref[0:1, :]
        hb = h.astype(BF16)
        h_ref[...] = hb
        for j in range(N_CHIPS):
            p_ref[:, j * nb:(j + 1) * nb] = _dot(hb, w_ref[j]) + b_ref[:, j * nb:(j + 1) * nb]

    return pl.pallas_call(
        body, name="mix_in_fwd", grid=(T // tm,),
        in_specs=[pl.BlockSpec((tm, D_MODEL), lambda i: (i, 0)), _full((6, D_MODEL)), _full((1, D_MODEL)),
                  _full((N_CHIPS, D_MODEL, nb)), _full((1, IN_COLS))],
        out_specs=[pl.BlockSpec((tm, IN_COLS), lambda i: (i, 0)), pl.BlockSpec((tm, D_MODEL), lambda i: (i, 0))],
        out_shape=[jax.ShapeDtypeStruct((T, IN_COLS), F32), jax.ShapeDtypeStruct((T, D_MODEL), BF16)],
        compiler_params=_cp(("arbitrary",), 48),
    )(x, modr, g_pre, w_in_g, b_in)


def _mixers_fwd(p, wdw, vecs, lbl):
    T = p.shape[0]
    tm = _tok_tile(T)
    nch = tm // CHUNK

    def body(p_ref, wdw_ref, bdw_ref, gain_ref, bias_ref, gout_ref, lbl_ref, cat_ref, ys_ref, o_ref, st_ref,
             ubuf, state, qt_s, kt_s, kh_s, v_s, egl_s):
        i = pl.program_id(0)

        @pl.when(i == 0)
        def _():
            state[...] = jnp.zeros(state.shape, F32)
            ubuf[0:HALO, :] = jnp.zeros((HALO, CONV_CH), F32)

        @pl.when(i > 0)
        def _():
            ubuf[0:HALO, :] = ubuf[tm:tm + HALO, :]

        ubuf[HALO:HALO + tm, :] = p_ref[:, 0:CONV_CH] * _sig(p_ref[:, CONV_CH:2 * CONV_CH])
        for r in range(tm // CONV_ROWS):
            base = r * CONV_ROWS + HALO - (CONV_K - 1)
            acc = jnp.broadcast_to(bdw_ref[...], (CONV_ROWS, CONV_CH))
            for j in range(CONV_K):
                acc = acc + wdw_ref[j:j + 1, :] * ubuf[base + j:base + j + CONV_ROWS, :]
            ys_ref[r * CONV_ROWS:(r + 1) * CONV_ROWS, :] = acc
        gmat = _gn_matrix()
        yv = ys_ref[...]
        d = yv - _gmean(yv, gmat)
        rs = lax.rsqrt(_gmean(d * d, gmat) + GN_EPS)
        z = d * rs * gain_ref[...] + bias_ref[...]
        cat_ref[:, 0:CONV_CH] = (z * _sig(z)).astype(BF16)

        lb, _ = _lower_bound(lbl_ref)
        lower, _, same = _chunk_masks(tm)
        o0 = 2 * CONV_CH
        pr = _hgrn_prep(p_ref[:, o0:o0 + HGRN_W], p_ref[:, o0 + HGRN_W:o0 + 2 * HGRN_W], lb, lower, same)
        qt_s[...] = pr["qt"].astype(BF16)
        kt_s[...] = pr["kt"].astype(BF16)
        kh_s[...] = pr["kh"].astype(BF16)
        v_s[...] = p_ref[:, o0 + 2 * HGRN_W:o0 + 3 * HGRN_W].astype(BF16)
        egl_s[...] = jnp.exp(pr["Gl"])
        tri = _tri()

        def chunk(ci, carry):
            r0 = pl.multiple_of(ci * CHUNK, CHUNK)
            rows = pl.ds(r0, CHUNK)
            for h in range(N_HEADS):
                ls = pl.ds(h * HEAD_D, HEAD_D)
                qc, kc, hc, vc = qt_s[rows, ls], kt_s[rows, ls], kh_s[rows, ls], v_s[rows, ls]
                s0 = state[h]
                s0b = s0.astype(BF16)
                st_ref[ci, h] = s0b
                att = jnp.where(tri, _dot_nt(qc, kc), 0.0).astype(BF16)
                o_ref[rows, ls] = _dot(att, vc) + _dot_nt(qc, s0b)
                state[h] = s0 * egl_s[pl.ds(r0, 1), ls] + _dot_tn(vc, hc)
            return carry

        lax.fori_loop(0, nch, chunk, 0)
        for h in range(N_HEADS):
            sl = slice(h * HEAD_D, (h + 1) * HEAD_D)
            oh = o_ref[:, sl]
            gh = p_ref[:, o0 + 3 * HGRN_W + h * HEAD_D:o0 + 3 * HGRN_W + (h + 1) * HEAD_D]
            rsh = lax.rsqrt(jnp.mean(oh * oh, axis=-1, keepdims=True) + RMS_EPS)
            hg = (oh * rsh) * gout_ref[:, sl] * (gh * _sig(gh))
            cat_ref[:, CONV_CH + h * HEAD_D:CONV_CH + (h + 1) * HEAD_D] = hg.astype(BF16)

    tile = lambda cols: pl.BlockSpec((tm, cols), lambda i: (i, 0))
    return pl.pallas_call(
        body, name="mixers_fwd", grid=(T // tm,),
        in_specs=[tile(IN_COLS), _full((HALO, CONV_CH))] + [_full((1, CONV_CH))] * 4 + [_full((2, HGRN_W))],
        out_specs=[tile(D_MODEL), tile(CONV_CH), tile(HGRN_W),
                   pl.BlockSpec((nch, N_HEADS, HEAD_D, HEAD_D), lambda i: (i, 0, 0, 0))],
        out_shape=[jax.ShapeDtypeStruct((T, D_MODEL), BF16), jax.ShapeDtypeStruct((T, CONV_CH), F32),
                   jax.ShapeDtypeStruct((T, HGRN_W), F32),
                   jax.ShapeDtypeStruct((T // CHUNK, N_HEADS, HEAD_D, HEAD_D), BF16)],
        scratch_shapes=[pltpu.VMEM((tm + HALO, CONV_CH), F32), pltpu.VMEM((N_HEADS, HEAD_D, HEAD_D), F32),
                        pltpu.VMEM((tm, HGRN_W), BF16), pltpu.VMEM((tm, HGRN_W), BF16),
                        pltpu.VMEM((tm, HGRN_W), BF16), pltpu.VMEM((tm, HGRN_W), BF16),
                        pltpu.VMEM((tm, HGRN_W), F32)],
        compiler_params=_cp(("arbitrary",), 56),
    )(p, wdw, *vecs, lbl)


def _mix_out_fwd(cat, w_out, x, modr, g_post, g_ffn):
    T = x.shape[0]
    tm = _tok_tile(T)

    def body(cat_ref, w_ref, x_ref, mod_ref, gp_ref, gf_ref, y_ref, x1_ref, h2_ref):
        yv = _dot(cat_ref[...], w_ref[...])
        y_ref[...] = yv
        rs = lax.rsqrt(jnp.mean(yv * yv, axis=-1, keepdims=True) + RMS_EPS)
        x1 = x_ref[...] + mod_ref[2:3, :] * ((yv * rs) * gp_ref[...])
        x1_ref[...] = x1
        rs1 = lax.rsqrt(jnp.mean(x1 * x1, axis=-1, keepdims=True) + RMS_EPS)
        h2 = (x1 * rs1) * gf_ref[...] * (1.0 + mod_ref[4:5, :]) + mod_ref[3:4, :]
        h2_ref[...] = h2.astype(BF16)

    tile = pl.BlockSpec((tm, D_MODEL), lambda i: (i, 0))
    return pl.pallas_call(
        body, name="mix_out_fwd", grid=(T // tm,),
        in_specs=[tile, _full((D_MODEL, D_MODEL)), tile, _full((6, D_MODEL)), _full((1, D_MODEL)),
                  _full((1, D_MODEL))],
        out_specs=[tile, tile, tile],
        out_shape=[jax.ShapeDtypeStruct((T, D_MODEL), F32), jax.ShapeDtypeStruct((T, D_MODEL), F32),
                   jax.ShapeDtypeStruct((T, D_MODEL), BF16)],
        compiler_params=_cp(("arbitrary",), 40),
    )(cat, w_out, x, modr, g_post, g_ffn)


def _ffn_fwd(h2, w_up_g, w_down, x1, target, modr, g_post):
    T = h2.shape[0]
    tm = _tok_tile(T)
    fb = D_FF // N_CHIPS

    def body(h_ref, wu_ref, wd_ref, x1_ref, t_ref, mod_ref, g_ref, a_ref, y2_ref, dx2_ref, ls_ref, acc):
        i, j = pl.program_id(0), pl.program_id(1)
        a = _dot(h_ref[...], wu_ref[0])
        a_ref[...] = a
        ra = jnp.maximum(a, 0.0)
        part = _dot((ra * ra).astype(BF16), wd_ref[...])

        @pl.when(j == 0)
        def _():
            acc[...] = part

        @pl.when(j > 0)
        def _():
            acc[...] += part

        @pl.when((i == 0) & (j == 0))
        def _():
            ls_ref[...] = jnp.zeros(ls_ref.shape, F32)

        @pl.when(j == N_CHIPS - 1)
        def _():
            y2 = acc[...]
            y2_ref[...] = y2
            rs = lax.rsqrt(jnp.mean(y2 * y2, axis=-1, keepdims=True) + RMS_EPS)
            x2 = x1_ref[...] + mod_ref[5:6, :] * ((y2 * rs) * g_ref[...])
            err = x2 - t_ref[...]
            dx2_ref[...] = err * (1.0 / D_MODEL)
            ls_ref[0:1, :] += _colsum(err * err)

    tile = pl.BlockSpec((tm, D_MODEL), lambda i, j: (i, 0))
    return pl.pallas_call(
        body, name="ffn_fwd", grid=(T // tm, N_CHIPS),
        in_specs=[tile, pl.BlockSpec((1, D_MODEL, fb), lambda i, j: (j, 0, 0)),
                  pl.BlockSpec((fb, D_MODEL), lambda i, j: (j, 0)), tile, tile,
                  _full((6, D_MODEL)), _full((1, D_MODEL))],
        out_specs=[pl.BlockSpec((tm, fb), lambda i, j: (i, j)), tile, tile, _full((8, D_MODEL))],
        out_shape=[jax.ShapeDtypeStruct((T, D_FF), F32), jax.ShapeDtypeStruct((T, D_MODEL), F32),
                   jax.ShapeDtypeStruct((T, D_MODEL), F32), jax.ShapeDtypeStruct((8, D_MODEL), F32)],
        scratch_shapes=[pltpu.VMEM((tm, D_MODEL), F32)],
        compiler_params=_cp(("arbitrary", "arbitrary"), 48),
    )(h2, w_up_g, w_down, x1, target, modr, g_post)


def _rms_bwd(dxn, xn, rs):
    return rs * (dxn - xn * jnp.mean(dxn * xn, axis=-1, keepdims=True))


def _ffn_bwd(dx2, y2, a, x1, w_up_g, w_down, modr, g_post, g_ffn):
    T = dx2.shape[0]
    tm = _tok_tile(T)
    fb = D_FF // N_CHIPS

    def body(dx2_ref, y2_ref, a_ref, x1_ref, wu_ref, wd_ref, mod_ref, gp_ref, gf_ref,
             da_ref, r_ref, dy2_ref, dx1_ref, st_ref, dy2_s, dh_s):
        i, j = pl.program_id(0), pl.program_id(1)

        @pl.when((i == 0) & (j == 0))
        def _():
            st_ref[...] = jnp.zeros(st_ref.shape, F32)

        @pl.when(j == 0)
        def _():
            dx2v, y2v = dx2_ref[...], y2_ref[...]
            rs = lax.rsqrt(jnp.mean(y2v * y2v, axis=-1, keepdims=True) + RMS_EPS)
            nh = y2v * rs
            st_ref[0:1, :] += _colsum(dx2v * (nh * gp_ref[...]))
            dn = dx2v * mod_ref[5:6, :]
            st_ref[1:2, :] += _colsum(dn * nh)
            dy2 = _rms_bwd(dn * gp_ref[...], nh, rs).astype(BF16)
            dy2_s[...] = dy2
            dy2_ref[...] = dy2

        av = a_ref[...]
        ra = jnp.maximum(av, 0.0)
        r_ref[...] = (ra * ra).astype(BF16)
        da = (_dot_nt(dy2_s[...], wd_ref[...]) * (2.0 * ra)).astype(BF16)
        da_ref[...] = da
        part = _dot_nt(da, wu_ref[0])

        @pl.when(j == 0)
        def _():
            dh_s[...] = part

        @pl.when(j > 0)
        def _():
            dh_s[...] += part

        @pl.when(j == N_CHIPS - 1)
        def _():
            dh = dh_s[...]
            x1v = x1_ref[...]
            rs1 = lax.rsqrt(jnp.mean(x1v * x1v, axis=-1, keepdims=True) + RMS_EPS)
            xn = x1v * rs1
            st_ref[2:3, :] += _colsum(dh)
            st_ref[3:4, :] += _colsum(dh * (xn * gf_ref[...]))
            dsc = dh * (1.0 + mod_ref[4:5, :])
            st_ref[4:5, :] += _colsum(dsc * xn)
            dx1_ref[...] = dx2_ref[...] + _rms_bwd(dsc * gf_ref[...], xn, rs1)

    tile = pl.BlockSpec((tm, D_MODEL), lambda i, j: (i, 0))
    ftile = pl.BlockSpec((tm, fb), lambda i, j: (i, j))
    return pl.pallas_call(
        body, name="ffn_bwd", grid=(T // tm, N_CHIPS),
        in_specs=[tile, tile, ftile, tile, pl.BlockSpec((1, D_MODEL, fb), lambda i, j: (j, 0, 0)),
                  pl.BlockSpec((fb, D_MODEL), lambda i, j: (j, 0)), _full((6, D_MODEL)), _full((1, D_MODEL)),
                  _full((1, D_MODEL))],
        out_specs=[ftile, ftile, tile, tile, _full((8, D_MODEL))],
        out_shape=[jax.ShapeDtypeStruct((T, D_FF), BF16), jax.ShapeDtypeStruct((T, D_FF), BF16),
                   jax.ShapeDtypeStruct((T, D_MODEL), BF16), jax.ShapeDtypeStruct((T, D_MODEL), F32),
                   jax.ShapeDtypeStruct((8, D_MODEL), F32)],
        scratch_shapes=[pltpu.VMEM((tm, D_MODEL), BF16), pltpu.VMEM((tm, D_MODEL), F32)],
        compiler_params=_cp(("arbitrary", "arbitrary"), 48),
    )(dx2, y2, a, x1, w_up_g, w_down, modr, g_post, g_ffn)


def _mix_out_bwd(dx1, y, w_out, modr, g_post):
    T = dx1.shape[0]
    tm = _tok_tile(T)

    def body(dx1_ref, y_ref, w_ref, mod_ref, gp_ref, dy_ref, dcat_ref, st_ref):
        @pl.when(pl.program_id(0) == 0)
        def _():
            st_ref[...] = jnp.zeros(st_ref.shape, F32)

        dxv, yv = dx1_ref[...], y_ref[...]
        rs = lax.rsqrt(jnp.mean(yv * yv, axis=-1, keepdims=True) + RMS_EPS)
        nh = yv * rs
        st_ref[0:1, :] += _colsum(dxv * (nh * gp_ref[...]))
        dn = dxv * mod_ref[2:3, :]
        st_ref[1:2, :] += _colsum(dn * nh)
        dy = _rms_bwd(dn * gp_ref[...], nh, rs).astype(BF16)
        dy_ref[...] = dy
        dcat_ref[...] = _dot_nt(dy, w_ref[...])

    tile = pl.BlockSpec((tm, D_MODEL), lambda i: (i, 0))
    return pl.pallas_call(
        body, name="mix_out_bwd", grid=(T // tm,),
        in_specs=[tile, tile, _full((D_MODEL, D_MODEL)), _full((6, D_MODEL)), _full((1, D_MODEL))],
        out_specs=[tile, tile, _full((8, D_MODEL))],
        out_shape=[jax.ShapeDtypeStruct((T, D_MODEL), BF16), jax.ShapeDtypeStruct((T, D_MODEL), F32),
                   jax.ShapeDtypeStruct((8, D_MODEL), F32)],
        compiler_params=_cp(("arbitrary",), 40),
    )(dx1, y, w_out, modr, g_post)


def _mixers_bwd(p, dcat, ys, o, states, wdw, vecs, lbl):
    T = p.shape[0]
    tm = _tok_tile(T)
    nt = T // tm
    nch = tm // CHUNK
    hpt = tm // HALO

    def body(p_ref, ph_ref, dcat_ref, ys_ref, o_ref, st_ref, wdw_ref, bdw_ref, gain_ref, bias_ref, gout_ref, lbl_ref,
             dp_ref, sb_ref, s5_ref, dw_ref,
             ubuf, dybuf, carry, dstate, qt_s, kt_s, kh_s, v_s, do_s, egl_s, dqt_s, dkt_s, dkh_s, dv_s, dgl_s):
        i = pl.program_id(0)
        tile_idx = nt - 1 - i

        @pl.when(i == 0)
        def _():
            dstate[...] = jnp.zeros(dstate.shape, F32)
            carry[...] = jnp.zeros(carry.shape, F32)
            sb_ref[...] = jnp.zeros(sb_ref.shape, F32)
            s5_ref[...] = jnp.zeros(s5_ref.shape, F32)
            dw_ref[...] = jnp.zeros(dw_ref.shape, F32)

        uh = ph_ref[:, 0:CONV_CH] * _sig(ph_ref[:, CONV_CH:2 * CONV_CH])
        ubuf[0:HALO, :] = jnp.where(tile_idx > 0, uh, 0.0)
        ubuf[HALO:HALO + tm, :] = p_ref[:, 0:CONV_CH] * _sig(p_ref[:, CONV_CH:2 * CONV_CH])
        gmat = _gn_matrix()
        gain = gain_ref[...]
        yv = ys_ref[...]
        d = yv - _gmean(yv, gmat)
        rs = lax.rsqrt(_gmean(d * d, gmat) + GN_EPS)
        yn = d * rs
        z = yn * gain + bias_ref[...]
        sz = _sig(z)
        dz = dcat_ref[:, 0:CONV_CH] * (sz * (1.0 + z * (1.0 - sz)))
        dyn = dz * gain
        dyc = rs * (dyn - _gmean(dyn, gmat) - yn * _gmean(dyn * yn, gmat))
        s5_ref[0:1, :] += _colsum(dyc)
        s5_ref[1:2, :] += _colsum(dz * yn)
        s5_ref[2:3, :] += _colsum(dz)
        dybuf[tm:tm + HALO, :] = carry[...]
        dybuf[0:tm, :] = dyc
        carry[...] = dyc[0:HALO, :]
        for j in range(CONV_K):
            off = HALO - (CONV_K - 1) + j
            dw_ref[j:j + 1, :] += _colsum(dyc * ubuf[off:off + tm, :])
        for r in range(tm // CONV_ROWS):
            rows = slice(r * CONV_ROWS, (r + 1) * CONV_ROWS)
            acc = jnp.zeros((CONV_ROWS, CONV_CH), F32)
            for j in range(CONV_K):
                off = r * CONV_ROWS + (CONV_K - 1) - j
                acc = acc + wdw_ref[j:j + 1, :] * dybuf[off:off + CONV_ROWS, :]
            val = p_ref[rows, 0:CONV_CH]
            sg = _sig(p_ref[rows, CONV_CH:2 * CONV_CH])
            dval = acc * sg
            dgate = acc * val * (sg * (1.0 - sg))
            dp_ref[rows, 0:CONV_CH] = dval.astype(BF16)
            dp_ref[rows, CONV_CH:2 * CONV_CH] = dgate.astype(BF16)
            sb_ref[0:1, 0:CONV_CH] += _colsum(dval)
            sb_ref[0:1, CONV_CH:2 * CONV_CH] += _colsum(dgate)

        o0 = 2 * CONV_CH
        for h in range(N_HEADS):
            sl = slice(h * HEAD_D, (h + 1) * HEAD_D)
            gsl = slice(o0 + 3 * HGRN_W + h * HEAD_D, o0 + 3 * HGRN_W + (h + 1) * HEAD_D)
            oh = o_ref[:, sl]
            gh = p_ref[:, gsl]
            dh = dcat_ref[:, CONV_CH + h * HEAD_D:CONV_CH + (h + 1) * HEAD_D]
            gout = gout_ref[:, sl]
            rsh = lax.rsqrt(jnp.mean(oh * oh, axis=-1, keepdims=True) + RMS_EPS)
            on = oh * rsh
            sgg = _sig(gh)
            dgh = dh * (on * gout) * (sgg * (1.0 + gh * (1.0 - sgg)))
            dm = dh * (gh * sgg)
            s5_ref[3:4, sl] += _colsum(dm * on)
            do_s[:, sl] = _rms_bwd(dm * gout, on, rsh).astype(BF16)
            dp_ref[:, gsl] = dgh.astype(BF16)
            sb_ref[2:3, CONV_CH + h * HEAD_D:CONV_CH + (h + 1) * HEAD_D] += _colsum(dgh)

        lb, _ = _lower_bound(lbl_ref)
        lower, upper, same = _chunk_masks(tm)
        pq = p_ref[:, o0:o0 + HGRN_W]
        pr = _hgrn_prep(pq, p_ref[:, o0 + HGRN_W:o0 + 2 * HGRN_W], lb, lower, same)
        qt_s[...] = pr["qt"].astype(BF16)
        kt_s[...] = pr["kt"].astype(BF16)
        kh_s[...] = pr["kh"].astype(BF16)
        v_s[...] = p_ref[:, o0 + 2 * HGRN_W:o0 + 3 * HGRN_W].astype(BF16)
        egl_s[...] = jnp.exp(pr["Gl"])
        tri = _tri()

        def chunk(it, c_):
            ci = nch - 1 - it
            r0 = pl.multiple_of(ci * CHUNK, CHUNK)
            rows = pl.ds(r0, CHUNK)
            for h in range(N_HEADS):
                ls = pl.ds(h * HEAD_D, HEAD_D)
                qc, kc, hc, vc = qt_s[rows, ls], kt_s[rows, ls], kh_s[rows, ls], v_s[rows, ls]
                dob = do_s[rows, ls]
                s0b = st_ref[ci, h]
                ds1 = dstate[h]
                ds1b = ds1.astype(BF16)
                egl = egl_s[pl.ds(r0, 1), ls]
                att = jnp.where(tri, _dot_nt(qc, kc), 0.0).astype(BF16)
                datt = jnp.where(tri, _dot_nt(dob, vc), 0.0).astype(BF16)
                dv_s[rows, ls] = _dot_tn(att, dob) + _dot_nt(hc, ds1b)
                dqt_s[rows, ls] = _dot(datt, kc) + _dot(dob, s0b)
                dkt_s[rows, ls] = _dot_tn(datt, qc)
                dkh_s[rows, ls] = _dot(vc, ds1b)
                dgl = egl * _colsum(ds1 * s0b.astype(F32))
                dgl_s[rows, ls] = jnp.broadcast_to(dgl, (CHUNK, HEAD_D))
                dstate[h] = ds1 * egl + _dot_tn(dob, qc)
            return c_

        lax.fori_loop(0, nch, chunk, 0)
        dqt, dkt, dkh = dqt_s[...], dkt_s[...], dkh_s[...]
        dk = dkt * pr["enG"] + dkh * pr["eGlG"]
        khk = dkh * pr["kh"]
        dG = dqt * pr["qt"] - dkt * pr["kt"] - khk
        dlogf = _mm3(upper, dG) + _mm3(same, khk) + dgl_s[...]
        df = dlogf / pr["f"] - dk
        sf, sq = pr["sf"], pr["sq"]
        s5_ref[4:5, :] += _colsum(df * (1.0 - sf))
        dfl = df * (1.0 - lb) * (sf * (1.0 - sf))
        dq = (dqt * pr["eG"]) * (sq * (1.0 + pq * (1.0 - sq)))
        dvv = dv_s[...]
        dp_ref[:, o0:o0 + HGRN_W] = dq.astype(BF16)
        dp_ref[:, o0 + HGRN_W:o0 + 2 * HGRN_W] = dfl.astype(BF16)
        dp_ref[:, o0 + 2 * HGRN_W:o0 + 3 * HGRN_W] = dvv.astype(BF16)
        sb_ref[1:2, 0:HGRN_W] += _colsum(dq)
        sb_ref[1:2, HGRN_W:2 * HGRN_W] += _colsum(dfl)
        sb_ref[2:3, 0:HGRN_W] += _colsum(dvv)

    rev = lambda cols: pl.BlockSpec((tm, cols), lambda i: (nt - 1 - i, 0))
    halo = pl.BlockSpec((HALO, 2 * CONV_CH), lambda i: (jnp.maximum((nt - 1 - i) * hpt - 1, 0), 0))
    wide = lambda n: pltpu.VMEM((tm, HGRN_W), n)
    return pl.pallas_call(
        body, name="mixers_bwd", grid=(nt,),
        in_specs=[rev(IN_COLS), halo, rev(D_MODEL), rev(CONV_CH), rev(HGRN_W),
                  pl.BlockSpec((nch, N_HEADS, HEAD_D, HEAD_D), lambda i: (nt - 1 - i, 0, 0, 0)),
                  _full((HALO, CONV_CH))] + [_full((1, CONV_CH))] * 4 + [_full((2, HGRN_W))],
        out_specs=[rev(IN_COLS), _full((8, D_MODEL)), _full((8, CONV_CH)), _full((HALO, CONV_CH))],
        out_shape=[jax.ShapeDtypeStruct((T, IN_COLS), BF16), jax.ShapeDtypeStruct((8, D_MODEL), F32),
                   jax.ShapeDtypeStruct((8, CONV_CH), F32), jax.ShapeDtypeStruct((HALO, CONV_CH), F32)],
        scratch_shapes=[pltpu.VMEM((tm + HALO, CONV_CH), F32), pltpu.VMEM((tm + HALO, CONV_CH), F32),
                        pltpu.VMEM((HALO, CONV_CH), F32), pltpu.VMEM((N_HEADS, HEAD_D, HEAD_D), F32),
                        wide(BF16), wide(BF16), wide(BF16), wide(BF16), wide(BF16),
                        wide(F32), wide(F32), wide(F32), wide(F32), wide(F32), wide(F32)],
        compiler_params=_cp(("arbitrary",), 56),
    )(p, p, dcat, ys, o, states, wdw, *vecs, lbl)


def _mix_in_bwd(dp, w_in_g, x, dx1, modr, g_pre):
    T = x.shape[0]
    tm = _tok_tile(T)
    nb = IN_COLS // N_CHIPS

    def body(dp_ref, w_ref, x_ref, dx1_ref, mod_ref, g_ref, gx_ref, st_ref):
        @pl.when(pl.program_id(0) == 0)
        def _():
            st_ref[...] = jnp.zeros(st_ref.shape, F32)

        dh = None
        for j in range(N_CHIPS):
            part = _dot_nt(dp_ref[:, j * nb:(j + 1) * nb], w_ref[j])
            dh = part if dh is None else dh + part
        xv = x_ref[...]
        rs = lax.rsqrt(jnp.mean(xv * xv, axis=-1, keepdims=True) + RMS_EPS)
        xn = xv * rs
        st_ref[0:1, :] += _colsum(dh)
        st_ref[1:2, :] += _colsum(dh * (xn * g_ref[...]))
        dsc = dh * (1.0 + mod_ref[1:2, :])
        st_ref[2:3, :] += _colsum(dsc * xn)
        gx_ref[...] = dx1_ref[...] + _rms_bwd(dsc * g_ref[...], xn, rs)

    tile = pl.BlockSpec((tm, D_MODEL), lambda i: (i, 0))
    return pl.pallas_call(
        body, name="mix_in_bwd", grid=(T // tm,),
        in_specs=[pl.BlockSpec((tm, IN_COLS), lambda i: (i, 0)), _full((N_CHIPS, D_MODEL, nb)), tile, tile,
                  _full((6, D_MODEL)), _full((1, D_MODEL))],
        out_specs=[tile, _full((8, D_MODEL))],
        out_shape=[jax.ShapeDtypeStruct((T, D_MODEL), F32), jax.ShapeDtypeStruct((8, D_MODEL), F32)],
        compiler_params=_cp(("arbitrary",), 48),
    )(dp, w_in_g, x, dx1, modr, g_pre)


def _weight_grad(a, b, a_blocked, b_blocked, name):
    T = a.shape[0]
    tt = _tok_tile(T)
    ka = a.shape[1] // N_CHIPS if a_blocked else a.shape[1]
    nb = b.shape[1] // N_CHIPS if b_blocked else b.shape[1]

    def body(a_ref, b_ref, o_ref):
        prod = _dot_tn(a_ref[...], b_ref[...])

        @pl.when(pl.program_id(1) == 0)
        def _():
            o_ref[0] = prod

        @pl.when(pl.program_id(1) > 0)
        def _():
            o_ref[0] += prod

    return pl.pallas_call(
        body, name=name, grid=(N_CHIPS, T // tt),
        in_specs=[pl.BlockSpec((tt, ka), (lambda j, t: (t, j)) if a_blocked else (lambda j, t: (t, 0))),
                  pl.BlockSpec((tt, nb), (lambda j, t: (t, j)) if b_blocked else (lambda j, t: (t, 0)))],
        out_specs=pl.BlockSpec((1, ka, nb), lambda j, t: (j, 0, 0)),
        out_shape=jax.ShapeDtypeStruct((N_CHIPS, ka, nb), F32),
        compiler_params=_cp(("arbitrary", "arbitrary"), 40),
    )(a, b)


R_LOSS = 0
R_FFN = 8
R_OUT = 16
R_IN = 24
R_BIN = 32
R_512 = 40
R_DW = 48
N_STAT_ROWS = 80
MOD_ROWS = (R_IN + 0, R_IN + 1, R_OUT + 0, R_FFN + 2, R_FFN + 3, R_FFN + 0)


def _small_update(gath, params):
    names = ["b_ada", "lb_logits", "g_pre_mix", "b_in", "b_dw", "gn_gain", "gn_bias", "g_hgrn_out", "g_post_mix",
             "g_pre_ffn", "g_post_ffn"]
    flat = []
    for n in names:
        flat += list(params[n])
    n_in = 1 + len(flat)

    def body(*refs):
        g_ref = refs[0]
        prm = {n: refs[1 + 3 * k:4 + 3 * k] for k, n in enumerate(names)}
        outs = refs[n_in:]
        loss_ref, dmod_ref, dwdw_ref = outs[0], outs[1], outs[2]
        res = {n: outs[3 + 4 * k:7 + 4 * k] for k, n in enumerate(names)}
        red = g_ref[0]
        for dev in range(1, N_DEV):
            red = red + g_ref[dev]
        loss_ref[...] = jnp.broadcast_to(
            (0.5 / D_MODEL) * jnp.sum(red[R_LOSS:R_LOSS + 1, :], axis=-1, keepdims=True), loss_ref.shape)
        for dev in range(N_DEV):
            for k, r in enumerate(MOD_ROWS):
                dmod_ref[dev:dev + 1, k * D_MODEL:(k + 1) * D_MODEL] = g_ref[dev, r:r + 1, :]
        dwdw_ref[...] = red[R_DW:R_DW + HALO, 0:CONV_CH]

        def finish(name, pieces):
            w_ref, m_ref, v_ref = prm[name]
            g_out, d_out, m_out, v_out = res[name]
            for rsl, lsl, g in pieces:
                d, m2, v2 = _adam_math(w_ref[rsl, lsl], g, m_ref[rsl, lsl], v_ref[rsl, lsl])
                g_out[rsl, lsl] = g
                d_out[rsl, lsl] = d
                m_out[rsl, lsl] = m2
                v_out[rsl, lsl] = v2

        one = slice(0, 1)
        row = lambda r: red[r:r + 1, :]
        half = lambda r: red[r:r + 1, 0:CONV_CH]
        finish("b_ada", [(one, slice(k * D_MODEL, (k + 1) * D_MODEL), row(r)) for k, r in enumerate(MOD_ROWS)])
        finish("b_in", [(one, slice(k * D_MODEL, (k + 1) * D_MODEL), row(R_BIN + k)) for k in range(3)])
        finish("g_pre_mix", [(one, slice(None), row(R_IN + 2))])
        finish("g_post_mix", [(one, slice(None), row(R_OUT + 1))])
        finish("g_pre_ffn", [(one, slice(None), row(R_FFN + 4))])
        finish("g_post_ffn", [(one, slice(None), row(R_FFN + 1))])
        finish("b_dw", [(one, slice(None), half(R_512 + 0))])
        finish("gn_gain", [(one, slice(None), half(R_512 + 1))])
        finish("gn_bias", [(one, slice(None), half(R_512 + 2))])
        finish("g_hgrn_out", [(one, slice(None), half(R_512 + 3))])
        s0, s1 = _lower_bound(prm["lb_logits"][0])
        dlb = half(R_512 + 4)
        finish("lb_logits", [(slice(0, 1), slice(None), dlb * s0 * (1.0 - s0)),
                             (slice(1, 2), slice(None), -dlb * s0 * s1)])

    vm = pl.BlockSpec(memory_space=pltpu.VMEM)
    out_shape = [jax.ShapeDtypeStruct((8, 128), F32), jax.ShapeDtypeStruct((N_DEV, 6 * D_MODEL), F32),
                 jax.ShapeDtypeStruct((HALO, CONV_CH), F32)]
    for n in names:
        out_shape += [jax.ShapeDtypeStruct(params[n][0].shape, F32)] * 4
    outs = pl.pallas_call(
        body, name="small_update", out_shape=out_shape,
        in_specs=[vm] * n_in, out_specs=[vm] * len(out_shape),
        compiler_params=_cp(None, 32),
    )(gath, *flat)
    return outs[0], outs[1], outs[2], {n: outs[3 + 4 * k:7 + 4 * k] for k, n in enumerate(names)}


def _wdw_adam(w, g, m, v):
    def body(w_ref, g_ref, m_ref, v_ref, d_out, m_out, v_out):
        d, m2, v2 = _adam_math(w_ref[...], g_ref[...], m_ref[...], v_ref[...])
        d_out[...] = d
        m_out[...] = m2
        v_out[...] = v2

    vm = pl.BlockSpec(memory_space=pltpu.VMEM)
    return pl.pallas_call(
        body, name="wdw_adam", out_shape=[jax.ShapeDtypeStruct(w.shape, F32)] * 3,
        in_specs=[vm] * 4, out_specs=[vm] * 3, compiler_params=_cp(None, 16),
    )(w, g, m, v)


def kernel(x, c, w_ada, b_ada, lb_logits, g_pre_mix, w_in, b_in, w_dw, b_dw, gn_gain, gn_bias, g_hgrn_out, w_out, g_post_mix, g_pre_ffn, w_up, w_down, g_post_ffn, loss_target, m_w_ada, m_b_ada, m_lb_logits, m_g_pre_mix, m_w_in, m_b_in, m_w_dw, m_b_dw, m_gn_gain, m_gn_bias, m_g_hgrn_out, m_w_out, m_g_post_mix, m_g_pre_ffn, m_w_up, m_w_down, m_g_post_ffn, v_w_ada, v_b_ada, v_lb_logits, v_g_pre_mix, v_w_in, v_b_in, v_w_dw, v_b_dw, v_gn_gain, v_gn_bias, v_g_hgrn_out, v_w_out, v_g_post_mix, v_g_pre_ffn, v_w_up, v_w_down, v_g_post_ffn):
    ax, ay, ac = lax.axis_index("x"), lax.axis_index("y"), lax.axis_index("c")
    chip = 2 * ax + ay
    T = x.shape[1]
    xs, tgt = x[0], loss_target[0]
    ada_cols = w_ada.shape[2]

    b_sh = lax.dynamic_slice_in_dim(b_ada, chip * ada_cols, ada_cols, axis=1)
    wdw_pad = jnp.pad(w_dw[0], ((0, HALO - CONV_K), (0, 0)))
    _, c8, modg, wdwg = _ada_exchange(c, w_ada[0], b_sh, wdw_pad)
    modr = modg.reshape(6, D_MODEL)
    wdw_all = jnp.transpose(wdwg, (1, 0, 2)).reshape(HALO, CONV_CH)
    w_in_g, w_out_g, w_up_g, w_down_g = _weight_gather([w_in[0], w_out[0], w_up[0], w_down[0]])
    w_out_f = w_out_g.reshape(D_MODEL, D_MODEL)
    w_down_f = w_down_g.reshape(D_FF, D_MODEL)
    vec = (b_dw, gn_gain, gn_bias, g_hgrn_out)

    p, h1 = _mix_in_fwd(xs, modr, g_pre_mix, w_in_g, b_in)
    cat, ys, o, states = _mixers_fwd(p, wdw_all, vec, lb_logits)
    y, x1, h2 = _mix_out_fwd(cat, w_out_f, xs, modr, g_post_mix, g_pre_ffn)
    a, y2, dx2, st_loss = _ffn_fwd(h2, w_up_g, w_down_f, x1, tgt, modr, g_post_ffn)

    da, r, dy2, dx1, st_ffn = _ffn_bwd(dx2, y2, a, x1, w_up_g, w_down_f, modr, g_post_ffn, g_pre_ffn)
    dy, dcat, st_out = _mix_out_bwd(dx1, y, w_out_f, modr, g_post_mix)
    dp, st_bin, st_512, dwdw = _mixers_bwd(p, dcat, ys, o, states, wdw_all, vec, lb_logits)
    grad_x, st_in = _mix_in_bwd(dp, w_in_g, xs, dx1, modr, g_pre_mix)
    big = [_weight_grad(h1, dp, False, True, "grad_w_in"), _weight_grad(cat, dy, True, False, "grad_w_out"),
           _weight_grad(h2, da, False, True, "grad_w_up"), _weight_grad(r, dy2, True, False, "grad_w_down")]

    core = jnp.reshape(ac, (1,)).astype(jnp.int32)
    chip1 = jnp.reshape(chip, (1,)).astype(jnp.int32)
    tags = ["w_in", "w_out", "w_up", "w_down"]
    got = _pair_swap(big)
    pairs = [_pair_sum(core, g, o_, "pair_sum_" + t) for g, o_, t in zip(big, got, tags)]
    got_b = _chip_exchange([pb for _, pb in pairs])
    halves = [_chip_sum(chip1, pf, gb, "chip_sum_" + t) for (pf, _), gb, t in zip(pairs, got_b, tags)]
    g_w_in, g_w_out, g_w_up, g_w_down = _half_swap(halves)

    pad_lanes = lambda s: jnp.pad(s, ((0, 0), (0, D_MODEL - s.shape[1])))
    stats = jnp.concatenate([st_loss, st_ffn, st_out, st_in, st_bin, pad_lanes(st_512), pad_lanes(dwdw)], axis=0)
    gath = _stats_gather(stats)
    small = {"b_ada": (b_ada, m_b_ada, v_b_ada), "lb_logits": (lb_logits, m_lb_logits, v_lb_logits),
             "g_pre_mix": (g_pre_mix, m_g_pre_mix, v_g_pre_mix), "b_in": (b_in, m_b_in, v_b_in),
             "b_dw": (b_dw, m_b_dw, v_b_dw), "gn_gain": (gn_gain, m_gn_gain, v_gn_gain),
             "gn_bias": (gn_bias, m_gn_bias, v_gn_bias), "g_hgrn_out": (g_hgrn_out, m_g_hgrn_out, v_g_hgrn_out),
             "g_post_mix": (g_post_mix, m_g_post_mix, v_g_post_mix), "g_pre_ffn": (g_pre_ffn, m_g_pre_ffn, v_g_pre_ffn),
             "g_post_ffn": (g_post_ffn, m_g_post_ffn, v_g_post_ffn)}
    loss_t, dmod_all, dwdw_sum, sres = _small_update(gath, small)
    loss = loss_t[0, 0]

    res = dict(sres)
    dmod_sh = lax.dynamic_slice_in_dim(dmod_all, chip * ada_cols, ada_cols, axis=1)
    res["w_ada"] = [t[None] for t in _ada_grad_adam(jnp.transpose(c8), dmod_sh, w_ada[0], m_w_ada[0], v_w_ada[0])]
    g_wdw = lax.dynamic_slice_in_dim(dwdw_sum, chip * HEAD_D, HEAD_D, axis=1)[:CONV_K][None]
    res["w_dw"] = [g_wdw] + list(_wdw_adam(w_dw, g_wdw, m_w_dw, v_w_dw))
    for name, g, w, m, v in (("w_in", g_w_in, w_in, m_w_in, v_w_in), ("w_out", g_w_out, w_out, m_w_out, v_w_out),
                             ("w_up", g_w_up, w_up, m_w_up, v_w_up), ("w_down", g_w_down, w_down, m_w_down, v_w_down)):
        d, m2, v2 = _adam_big(w[0], g, m[0], v[0], "adam_" + name)
        res[name] = [g[None], d[None], m2[None], v2[None]]

    order = ["w_ada", "b_ada", "lb_logits", "g_pre_mix", "w_in", "b_in", "w_dw", "b_dw", "gn_gain", "gn_bias",
             "g_hgrn_out", "w_out", "g_post_mix", "g_pre_ffn", "w_up", "w_down", "g_post_ffn"]
    out = [loss, grad_x[None]]
    for k in range(4):
        out += [res[n][k] for n in order]
    return tuple(out)
```

```python
import jax
import jax.numpy as jnp
from jax import lax
from jax.experimental import pallas as pl
from jax.experimental.pallas import tpu as pltpu

F32, BF16 = jnp.float32, jnp.bfloat16
D_MODEL = 1024
CONV_CH = 512
HGRN_W = 512
N_HEADS = 4
HEAD_D = 128
CONV_K = 31
GN_GROUP = 64
GN_SHIFT = 6
IN_COLS = 3072
D_FF = 4096
CHUNK = 64
CHUNK_SHIFT = 6
N_CHIPS = 4
N_DEV = 8
RMS_EPS = 1e-6
GN_EPS = 1e-5
ADAM_LR, ADAM_B1, ADAM_B2, ADAM_EPS, ADAM_WD, ADAM_STEP = 0.001, 0.9, 0.999, 1e-08, 0.01, 10
TOK_TILE = 512
HALO = 32
CONV_ROWS = 64
MIB = 1 << 20
MESH = pl.DeviceIdType.MESH
OTHER_CHIPS = ((1, 0), (0, 1), (1, 1))


def _cp(sem=None, vmem_mib=48):
    return pltpu.CompilerParams(dimension_semantics=sem, vmem_limit_bytes=vmem_mib * MIB)


def _dot(a, b):
    return jnp.dot(a, b, preferred_element_type=F32)


def _dot_nt(a, b):
    return lax.dot_general(a, b, (((1,), (1,)), ((), ())), preferred_element_type=F32)


def _dot_tn(a, b):
    return lax.dot_general(a, b, (((0,), (0,)), ((), ())), preferred_element_type=F32)


def _sig(v):
    return jax.nn.sigmoid(v)


def _colsum(v):
    return jnp.sum(v, axis=0, keepdims=True)


def _flip(v, b):
    return 1 - v if b else v


def _rcopy(src, dst, ssem, rsem, dev):
    return pltpu.make_async_remote_copy(src_ref=src, dst_ref=dst, send_sem=ssem, recv_sem=rsem,
                                        device_id=dev, device_id_type=MESH)


def _place():
    return lax.axis_index("x"), lax.axis_index("y"), lax.axis_index("c")


def _full(shape):
    return pl.BlockSpec(shape, lambda *_: (0,) * len(shape))


def _big(shape, dtype):
    return pltpu.HBM(shape, dtype)


def _hbm(*arrays):
    out = [pltpu.with_memory_space_constraint(a, pltpu.HBM) for a in arrays]
    return out[0] if len(out) == 1 else out


def _split2(v):
    hi = v.astype(BF16)
    lo = (v - hi.astype(F32)).astype(BF16)
    return hi, lo


def _split3(v):
    h1 = v.astype(BF16)
    r1 = v - h1.astype(F32)
    h2 = r1.astype(BF16)
    h3 = (r1 - h2.astype(F32)).astype(BF16)
    return h1, h2, h3


def _mm3(mat, v):
    h1, h2, h3 = _split3(v)
    return _dot(mat, h1) + _dot(mat, h2) + _dot(mat, h3)


def _gn_matrix():
    r = lax.broadcasted_iota(jnp.int32, (CONV_CH, CONV_CH), 0) >> GN_SHIFT
    c = lax.broadcasted_iota(jnp.int32, (CONV_CH, CONV_CH), 1) >> GN_SHIFT
    return jnp.where(r == c, 1.0 / GN_GROUP, 0.0).astype(BF16)


def _gmean(v, gmat):
    hi, lo = _split2(v)
    return _dot(hi, gmat) + _dot(lo, gmat)


def _chunk_masks(tm):
    r = lax.broadcasted_iota(jnp.int32, (tm, tm), 0)
    c = lax.broadcasted_iota(jnp.int32, (tm, tm), 1)
    same = (r >> CHUNK_SHIFT) == (c >> CHUNK_SHIFT)
    one = lambda m: jnp.where(m, 1.0, 0.0).astype(BF16)
    return one(same & (c <= r)), one(same & (c >= r)), one(same)


def _tri():
    return lax.broadcasted_iota(jnp.int32, (CHUNK, CHUNK), 0) >= lax.broadcasted_iota(jnp.int32, (CHUNK, CHUNK), 1)


def _lower_bound(lbl_ref):
    l0, l1 = lbl_ref[0:1, :], lbl_ref[1:2, :]
    mx = jnp.maximum(l0, l1)
    e0, e1 = jnp.exp(l0 - mx), jnp.exp(l1 - mx)
    return e0 / (e0 + e1), e1 / (e0 + e1)


def _hgrn_prep(pq, pf, lb, lower, same):
    sq = _sig(pq)
    qf = pq * sq
    sf = _sig(pf)
    f = lb + (1.0 - lb) * sf
    logf = jnp.log(f)
    k = 1.0 - f
    G = _mm3(lower, logf)
    Gl = _mm3(same, logf)
    eG, enG, eGlG = jnp.exp(G), jnp.exp(-G), jnp.exp(Gl - G)
    return dict(sq=sq, sf=sf, f=f, Gl=Gl, eG=eG, enG=enG, eGlG=eGlG, qt=qf * eG, kt=k * enG, kh=k * eGlG)


def _ada_exchange(c_row, w_ada, b_sh, wdw_pad):
    ncol = w_ada.shape[1]

    def body(c_ref, w_ref, b_ref, wdw_ref, call_ref, c8_ref, modg_ref, wdwg_ref, rows_s, sa, ra, sw, rw, sm, rm):
        x, y, c = _place()
        me = 4 * x + 2 * y + c
        chip = 2 * x + y
        cv = c_ref[...]
        call_ref[me] = cv * _sig(cv)
        wdwg_ref[chip] = wdw_ref[...]
        sends = []
        for m in range(1, N_DEV):
            peer = (_flip(x, m >> 2), _flip(y, (m >> 1) & 1), _flip(c, m & 1))
            cp = _rcopy(call_ref.at[me], call_ref.at[me], sa.at[m - 1], ra.at[m - 1], peer)
            cp.start()
            sends.append(cp)
        for k, (fx, fy) in enumerate(OTHER_CHIPS):
            peer = (_flip(x, fx), _flip(y, fy), c)
            cp = _rcopy(wdwg_ref.at[chip], wdwg_ref.at[chip], sw.at[k], rw.at[k], peer)
            cp.start()
            sends.append(cp)
        for m in range(1, N_DEV):
            peer = (_flip(x, m >> 2), _flip(y, (m >> 1) & 1), _flip(c, m & 1))
            pid = 4 * peer[0] + 2 * peer[1] + peer[2]
            _rcopy(call_ref.at[pid], call_ref.at[pid], sa.at[m - 1], ra.at[m - 1], peer).wait_recv()
        for b in range(N_DEV):
            c8_ref[b:b + 1, :] = call_ref[b]
        mod_all = _dot(c8_ref[...].astype(BF16), w_ref[...].astype(BF16)) + b_ref[...]
        for b in range(N_DEV):
            rows_s[b] = mod_all[b:b + 1, :]
        modg_ref[chip] = rows_s[me]
        for k, (fx, fy) in enumerate(OTHER_CHIPS):
            peer = (_flip(x, fx), _flip(y, fy), c)
            pid = 4 * peer[0] + 2 * peer[1] + peer[2]
            cp = _rcopy(rows_s.at[pid], modg_ref.at[chip], sm.at[k], rm.at[k], peer)
            cp.start()
            sends.append(cp)
        for k, (fx, fy) in enumerate(OTHER_CHIPS):
            peer = (_flip(x, fx), _flip(y, fy), c)
            pchip = 2 * peer[0] + peer[1]
            _rcopy(rows_s.at[0], modg_ref.at[pchip], sm.at[k], rm.at[k], peer).wait_recv()
            _rcopy(wdwg_ref.at[pchip], wdwg_ref.at[pchip], sw.at[k], rw.at[k], peer).wait_recv()
        for cp in sends:
            cp.wait_send()

    vm = pl.BlockSpec(memory_space=pltpu.VMEM)
    return pl.pallas_call(
        body, name="ada_exchange",
        out_shape=[jax.ShapeDtypeStruct((N_DEV, 1, D_MODEL), F32), jax.ShapeDtypeStruct((N_DEV, D_MODEL), F32),
                   jax.ShapeDtypeStruct((N_CHIPS, 1, ncol), F32), jax.ShapeDtypeStruct((N_CHIPS, HALO, HEAD_D), F32)],
        in_specs=[vm] * 4, out_specs=[vm] * 4,
        scratch_shapes=[pltpu.VMEM((N_DEV, 1, ncol), F32),
                        pltpu.SemaphoreType.DMA((N_DEV - 1,)), pltpu.SemaphoreType.DMA((N_DEV - 1,)),
                        pltpu.SemaphoreType.DMA((3,)), pltpu.SemaphoreType.DMA((3,)),
                        pltpu.SemaphoreType.DMA((3,)), pltpu.SemaphoreType.DMA((3,))],
        compiler_params=_cp(None, 32),
    )(c_row, w_ada, b_sh, wdw_pad)


def _weight_gather(shards):
    nt = len(shards)

    def body(*refs):
        ins, outs = refs[:nt], refs[nt:2 * nt]
        ssem, rsem = refs[2 * nt], refs[2 * nt + 1]
        x, y, c = _place()
        chip = 2 * x + y
        sibling = (x, y, 1 - c)
        for t in range(nt):
            outs[t][chip] = ins[t][...].astype(BF16)
        copies = []

        def slab(t, ch, core):
            hs = ins[t].shape[0] // 2
            return outs[t].at[ch, pl.ds(core * hs, hs), :]

        for k, (fx, fy) in enumerate(OTHER_CHIPS):
            peer = (_flip(x, fx), _flip(y, fy), c)
            for t in range(nt):
                cp = _rcopy(slab(t, chip, c), slab(t, chip, c), ssem.at[t * 3 + k], rsem.at[t * 3 + k], peer)
                cp.start()
                copies.append(cp)
        for k, (fx, fy) in enumerate(OTHER_CHIPS):
            peer = (_flip(x, fx), _flip(y, fy), c)
            pchip = 2 * peer[0] + peer[1]
            for t in range(nt):
                _rcopy(slab(t, pchip, c), slab(t, pchip, c), ssem.at[t * 3 + k], rsem.at[t * 3 + k], peer).wait_recv()
                cp = _rcopy(slab(t, pchip, c), slab(t, pchip, c), ssem.at[3 * nt + t * 3 + k],
                            rsem.at[3 * nt + t * 3 + k], sibling)
                cp.start()
                copies.append(cp)
        for k, (fx, fy) in enumerate(OTHER_CHIPS):
            pchip = 2 * _flip(x, fx) + _flip(y, fy)
            for t in range(nt):
                _rcopy(slab(t, pchip, 1 - c), slab(t, pchip, 1 - c), ssem.at[3 * nt + t * 3 + k],
                       rsem.at[3 * nt + t * 3 + k], sibling).wait_recv()
        for cp in copies:
            cp.wait_send()

    vm = pl.BlockSpec(memory_space=pltpu.VMEM)
    return pl.pallas_call(
        body, name="weight_gather",
        out_shape=[jax.ShapeDtypeStruct((N_CHIPS,) + s.shape, BF16) for s in shards],
        in_specs=[vm] * nt, out_specs=[vm] * nt,
        scratch_shapes=[pltpu.SemaphoreType.DMA((6 * nt,)), pltpu.SemaphoreType.DMA((6 * nt,))],
        compiler_params=_cp(None, 52),
    )(*shards)


def _stats_gather(stats):
    rows, cols = stats.shape

    def body(s_ref, out_ref, ssem, rsem, lsem):
        x, y, c = _place()
        me, sibling = (x, y, c), (x, y, 1 - c)
        chips = [(_flip(x, fx), _flip(y, fy)) for fx, fy in OTHER_CHIPS]

        def blk(px, py, pc):
            return out_ref.at[4 * px + 2 * py + pc]

        def copy(k, block, to, src=None):
            return _rcopy(blk(*block) if src is None else src, blk(*block), ssem.at[k], rsem.at[k], to)

        mine = pltpu.make_async_copy(s_ref, blk(*me), lsem)
        mine.start()
        first = [copy(0, me, sibling, src=s_ref)]
        first += [copy(1 + j, me, (*chip, c), src=s_ref) for j, chip in enumerate(chips)]
        for cp in first:
            cp.start()
        passed = [copy(4 + j, (*chip, c), sibling) for j, chip in enumerate(chips)]
        for j, chip in enumerate(chips):
            copy(1 + j, (*chip, c), me).wait_recv()
            passed[j].start()
        copy(0, sibling, me).wait_recv()
        for j, chip in enumerate(chips):
            copy(4 + j, (*chip, 1 - c), me).wait_recv()
        for cp in first + passed:
            cp.wait_send()
        mine.wait()

    vm = pl.BlockSpec(memory_space=pltpu.VMEM)
    return pl.pallas_call(
        body, name="stats_gather",
        out_shape=jax.ShapeDtypeStruct((N_DEV, rows, cols), F32),
        in_specs=[vm], out_specs=vm,
        scratch_shapes=[pltpu.SemaphoreType.DMA((7,)), pltpu.SemaphoreType.DMA((7,)), pltpu.SemaphoreType.DMA],
        compiler_params=_cp(None, 32),
    )(stats)


def _pair_swap(grads):
    nt = len(grads)
    hbm = pl.BlockSpec(memory_space=pl.ANY)

    def body(*refs):
        ins, outs = refs[:nt], refs[nt:2 * nt]
        ssem, rsem = refs[2 * nt], refs[2 * nt + 1]
        x, y, c = _place()
        sibling = (x, y, 1 - c)
        copies = []
        for t in range(nt):
            hs = ins[t].shape[1] // 2
            cp = _rcopy(ins[t].at[:, pl.ds((1 - c) * hs, hs), :], outs[t], ssem.at[t], rsem.at[t], sibling)
            cp.start()
            copies.append(cp)
        for cp in copies:
            cp.wait_recv()
        for cp in copies:
            cp.wait_send()

    return pl.pallas_call(
        body, name="grad_pair_swap",
        out_shape=[jax.ShapeDtypeStruct((g.shape[0], g.shape[1] // 2, g.shape[2]), F32) for g in grads],
        in_specs=[hbm] * nt, out_specs=[hbm] * nt,
        scratch_shapes=[pltpu.SemaphoreType.DMA((nt,)), pltpu.SemaphoreType.DMA((nt,))],
    )(*[_hbm(g) for g in grads])


def _chip_exchange(pairs_b):
    nt = len(pairs_b)
    hbm = pl.BlockSpec(memory_space=pl.ANY)

    def body(*refs):
        ins, outs = refs[:nt], refs[nt:2 * nt]
        ssem, rsem, lsem = refs[2 * nt], refs[2 * nt + 1], refs[2 * nt + 2]
        x, y, c = _place()
        chip = 2 * x + y
        copies, locals_ = [], []
        for t in range(nt):
            lc = pltpu.make_async_copy(ins[t].at[chip], outs[t].at[chip], lsem.at[t])
            lc.start()
            locals_.append(lc)
        for k, (fx, fy) in enumerate(OTHER_CHIPS):
            peer = (_flip(x, fx), _flip(y, fy), c)
            pchip = 2 * peer[0] + peer[1]
            for t in range(nt):
                cp = _rcopy(ins[t].at[pchip], outs[t].at[chip], ssem.at[t * 3 + k], rsem.at[t * 3 + k], peer)
                cp.start()
                copies.append(cp)
        for k, (fx, fy) in enumerate(OTHER_CHIPS):
            peer = (_flip(x, fx), _flip(y, fy), c)
            pchip = 2 * peer[0] + peer[1]
            for t in range(nt):
                _rcopy(ins[t].at[pchip], outs[t].at[pchip], ssem.at[t * 3 + k], rsem.at[t * 3 + k], peer).wait_recv()
        for cp in copies:
            cp.wait_send()
        for lc in locals_:
            lc.wait()

    return pl.pallas_call(
        body, name="grad_chip_exchange",
        out_shape=[jax.ShapeDtypeStruct(p.shape, BF16) for p in pairs_b],
        in_specs=[hbm] * nt, out_specs=[hbm] * nt,
        scratch_shapes=[pltpu.SemaphoreType.DMA((3 * nt,)), pltpu.SemaphoreType.DMA((3 * nt,)),
                        pltpu.SemaphoreType.DMA((nt,))],
    )(*[_hbm(p) for p in pairs_b])


def _half_swap(halves):
    nt = len(halves)
    hbm = pl.BlockSpec(memory_space=pl.ANY)

    def body(*refs):
        ins, outs = refs[:nt], refs[nt:2 * nt]
        ssem, rsem, lsem = refs[2 * nt], refs[2 * nt + 1], refs[2 * nt + 2]
        x, y, c = _place()
        sibling = (x, y, 1 - c)
        copies, locals_ = [], []
        for t in range(nt):
            hs = ins[t].shape[0]
            dst = outs[t].at[pl.ds(c * hs, hs), :]
            lc = pltpu.make_async_copy(ins[t], dst, lsem.at[t])
            lc.start()
            locals_.append(lc)
            cp = _rcopy(ins[t], dst, ssem.at[t], rsem.at[t], sibling)
            cp.start()
            copies.append(cp)
        for t in range(nt):
            hs = ins[t].shape[0]
            _rcopy(ins[t], outs[t].at[pl.ds((1 - c) * hs, hs), :], ssem.at[t], rsem.at[t], sibling).wait_recv()
        for cp in copies:
            cp.wait_send()
        for lc in locals_:
            lc.wait()

    return pl.pallas_call(
        body, name="grad_half_swap",
        out_shape=[jax.ShapeDtypeStruct((2 * h.shape[0], h.shape[1]), F32) for h in halves],
        in_specs=[hbm] * nt, out_specs=[hbm] * nt,
        scratch_shapes=[pltpu.SemaphoreType.DMA((nt,)), pltpu.SemaphoreType.DMA((nt,)),
                        pltpu.SemaphoreType.DMA((nt,))],
    )(*[_hbm(h) for h in halves])


def _row_tile(rows):
    return min(rows, 256)


def _pair_sum(core, grad, got, name):
    nb, hs, cols = got.shape
    tr = _row_tile(hs)
    nr = hs // tr

    def body(core_ref, g_ref, o_ref, pf_ref, pb_ref):
        s = g_ref[...] + o_ref[...]
        pf_ref[...] = s
        pb_ref[...] = s.astype(BF16)

    gs = pltpu.PrefetchScalarGridSpec(
        num_scalar_prefetch=1, grid=(nb, nr),
        in_specs=[pl.BlockSpec((1, tr, cols), lambda j, i, cr: (j, cr[0] * nr + i, 0)),
                  pl.BlockSpec((1, tr, cols), lambda j, i, cr: (j, i, 0))],
        out_specs=[pl.BlockSpec((1, tr, cols), lambda j, i, cr: (j, i, 0)),
                   pl.BlockSpec((1, tr, cols), lambda j, i, cr: (j, i, 0))])
    return pl.pallas_call(
        body, name=name, grid_spec=gs,
        out_shape=[_big(got.shape, F32), _big(got.shape, BF16)],
        compiler_params=_cp(("arbitrary", "arbitrary"), 32),
    )(core, *_hbm(grad, got))


def _chip_sum(chip, pair_f, got_b, name):
    nb, hs, cols = got_b.shape
    tr = _row_tile(hs)

    def body(chip_ref, pf_ref, gb_ref, o_ref):
        me = chip_ref[0]
        acc = None
        for k in range(nb):
            term = jnp.where(me == k, pf_ref[0], gb_ref[k].astype(F32))
            acc = term if acc is None else acc + term
        o_ref[...] = acc

    gs = pltpu.PrefetchScalarGridSpec(
        num_scalar_prefetch=1, grid=(hs // tr,),
        in_specs=[pl.BlockSpec((1, tr, cols), lambda i, ch: (ch[0], i, 0)),
                  pl.BlockSpec((nb, tr, cols), lambda i, ch: (0, i, 0))],
        out_specs=pl.BlockSpec((tr, cols), lambda i, ch: (i, 0)))
    return pl.pallas_call(
        body, name=name, grid_spec=gs,
        out_shape=_big((hs, cols), F32),
        compiler_params=_cp(("arbitrary",), 32),
    )(chip, *_hbm(pair_f, got_b))


def _adam_math(w, g, m, v):
    m2 = ADAM_B1 * m + (1.0 - ADAM_B1) * g
    v2 = ADAM_B2 * v + (1.0 - ADAM_B2) * (g * g)
    m_hat = m2 / (1.0 - ADAM_B1 ** ADAM_STEP)
    v_hat = v2 / (1.0 - ADAM_B2 ** ADAM_STEP)
    delta = -ADAM_LR * (m_hat / (jnp.sqrt(v_hat) + ADAM_EPS) + ADAM_WD * w)
    return delta, m2, v2


def _adam_big(w, g, m, v, name):
    rows, cols = w.shape
    tr = _row_tile(rows)

    def body(w_ref, g_ref, m_ref, v_ref, d_out, m_out, v_out):
        d, m2, v2 = _adam_math(w_ref[...], g_ref[...], m_ref[...], v_ref[...])
        d_out[...] = d
        m_out[...] = m2
        v_out[...] = v2

    spec = pl.BlockSpec((tr, cols), lambda i: (i, 0))
    return pl.pallas_call(
        body, name=name, grid=(rows // tr,), in_specs=[spec] * 4, out_specs=[spec] * 3,
        out_shape=[_big(w.shape, F32)] * 3,
        compiler_params=_cp(("arbitrary",), 32),
    )(*_hbm(w, g, m, v))


def _ada_grad_adam(c8t, dmod_sh, w, m, v):
    rows, cols = w.shape
    tr = _row_tile(rows)

    def body(ct_ref, dm_ref, w_ref, m_ref, v_ref, g_out, d_out, m_out, v_out):
        g = None
        for b in range(N_DEV):
            term = ct_ref[:, b:b + 1] * dm_ref[b:b + 1, :]
            g = term if g is None else g + term
        d, m2, v2 = _adam_math(w_ref[...], g, m_ref[...], v_ref[...])
        g_out[...] = g
        d_out[...] = d
        m_out[...] = m2
        v_out[...] = v2

    spec = pl.BlockSpec((tr, cols), lambda i: (i, 0))
    return pl.pallas_call(
        body, name="ada_grad_adam", grid=(rows // tr,),
        in_specs=[pl.BlockSpec((tr, N_DEV), lambda i: (i, 0)), _full((N_DEV, cols)), spec, spec, spec],
        out_specs=[spec] * 4, out_shape=[_big(w.shape, F32)] * 4,
        compiler_params=_cp(("arbitrary",), 32),
    )(c8t, dmod_sh, *_hbm(w, m, v))


def _tok_tile(t):
    return min(TOK_TILE, t)


def _mix_in_fwd(x, modr, g_pre, w_in_g, b_in):
    T = x.shape[0]
    tm = _tok_tile(T)
    nb = IN_COLS // N_CHIPS

    def body(x_ref, mod_ref, g_ref, w_ref, b_ref, p_ref, h_ref):
        xv = x_ref[...]
        rstd = lax.rsqrt(jnp.mean(xv * xv, axis=-1, keepdims=True) + RMS_EPS)
        h = (xv * rstd) * g_ref[...] * (1.0 + mod_ref[1:2, :]) + mod_ref[0:1, :]
        hb = h.astype(BF16)
        h_ref[...] = hb
        for j in range(N_CHIPS):
            p_ref[:, j * nb:(j + 1) * nb] = _dot(hb, w_ref[j]) + b_ref[:, j * nb:(j + 1) * nb]

    return pl.pallas_call(
        body, name="mix_in_fwd", grid=(T // tm,),
        in_specs=[pl.BlockSpec((tm, D_MODEL), lambda i: (i, 0)), _full((6, D_MODEL)), _full((1, D_MODEL)),
                  _full((N_CHIPS, D_MODEL, nb)), _full((1, IN_COLS))],
        out_specs=[pl.BlockSpec((tm, IN_COLS), lambda i: (i, 0)), pl.BlockSpec((tm, D_MODEL), lambda i: (i, 0))],
        out_shape=[_big((T,IN_COLS), F32), _big((T,D_MODEL), BF16)],
        compiler_params=_cp(("arbitrary",), 48),
    )(_hbm(x), modr, g_pre, _hbm(w_in_g), b_in)


def _mixers_fwd(p, wdw, vecs, lbl):
    T = p.shape[0]
    tm = _tok_tile(T)
    nch = tm // CHUNK

    def body(p_ref, wdw_ref, bdw_ref, gain_ref, bias_ref, gout_ref, lbl_ref, cat_ref, ys_ref, o_ref, st_ref,
             ubuf, state, qt_s, kt_s, kh_s, v_s, egl_s):
        i = pl.program_id(0)

        @pl.when(i == 0)
        def _():
            state[...] = jnp.zeros(state.shape, F32)
            ubuf[0:HALO, :] = jnp.zeros((HALO, CONV_CH), F32)

        @pl.when(i > 0)
        def _():
            ubuf[0:HALO, :] = ubuf[tm:tm + HALO, :]

        ubuf[HALO:HALO + tm, :] = p_ref[:, 0:CONV_CH] * _sig(p_ref[:, CONV_CH:2 * CONV_CH])
        for r in range(tm // CONV_ROWS):
            base = r * CONV_ROWS + HALO - (CONV_K - 1)
            acc = jnp.broadcast_to(bdw_ref[...], (CONV_ROWS, CONV_CH))
            for j in range(CONV_K):
                acc = acc + wdw_ref[j:j + 1, :] * ubuf[base + j:base + j + CONV_ROWS, :]
            ys_ref[r * CONV_ROWS:(r + 1) * CONV_ROWS, :] = acc
        gmat = _gn_matrix()
        yv = ys_ref[...]
        d = yv - _gmean(yv, gmat)
        rs = lax.rsqrt(_gmean(d * d, gmat) + GN_EPS)
        z = d * rs * gain_ref[...] + bias_ref[...]
        cat_ref[:, 0:CONV_CH] = (z * _sig(z)).astype(BF16)

        lb, _ = _lower_bound(lbl_ref)
        lower, _, same = _chunk_masks(tm)
        o0 = 2 * CONV_CH
        pr = _hgrn_prep(p_ref[:, o0:o0 + HGRN_W], p_ref[:, o0 + HGRN_W:o0 + 2 * HGRN_W], lb, lower, same)
        qt_s[...] = pr["qt"].astype(BF16)
        kt_s[...] = pr["kt"].astype(BF16)
        kh_s[...] = pr["kh"].astype(BF16)
        v_s[...] = p_ref[:, o0 + 2 * HGRN_W:o0 + 3 * HGRN_W].astype(BF16)
        egl_s[...] = jnp.exp(pr["Gl"])
        tri = _tri()

        def chunk(ci, carry):
            r0 = pl.multiple_of(ci * CHUNK, CHUNK)
            rows = pl.ds(r0, CHUNK)
            for h in range(N_HEADS):
                ls = pl.ds(h * HEAD_D, HEAD_D)
                qc, kc, hc, vc = qt_s[rows, ls], kt_s[rows, ls], kh_s[rows, ls], v_s[rows, ls]
                s0 = state[h]
                s0b = s0.astype(BF16)
                st_ref[ci, h] = s0
                att =jnp.where(tri, _dot_nt(qc, kc), 0.0).astype(BF16)
                o_ref[rows, ls] = _dot(att, vc) + _dot_nt(qc, s0b)
                state[h] = s0 * egl_s[pl.ds(r0, 1), ls] + _dot_tn(vc, hc)
            return carry

        lax.fori_loop(0, nch, chunk, 0)
        for h in range(N_HEADS):
            sl = slice(h * HEAD_D, (h + 1) * HEAD_D)
            oh = o_ref[:, sl]
            gh = p_ref[:, o0 + 3 * HGRN_W + h * HEAD_D:o0 + 3 * HGRN_W + (h + 1) * HEAD_D]
            rsh = lax.rsqrt(jnp.mean(oh * oh, axis=-1, keepdims=True) + RMS_EPS)
            hg = (oh * rsh) * gout_ref[:, sl] * (gh * _sig(gh))
            cat_ref[:, CONV_CH + h * HEAD_D:CONV_CH + (h + 1) * HEAD_D] = hg.astype(BF16)

    tile = lambda cols: pl.BlockSpec((tm, cols), lambda i: (i, 0))
    return pl.pallas_call(
        body, name="mixers_fwd", grid=(T // tm,),
        in_specs=[tile(IN_COLS), _full((HALO, CONV_CH))] + [_full((1, CONV_CH))] * 4 + [_full((2, HGRN_W))],
        out_specs=[tile(D_MODEL), tile(CONV_CH), tile(HGRN_W),
                   pl.BlockSpec((nch, N_HEADS, HEAD_D, HEAD_D), lambda i: (i, 0, 0, 0))],
        out_shape=[_big((T,D_MODEL), BF16), _big((T,CONV_CH), F32),
                   _big((T,HGRN_W), F32),
                   _big((T // CHUNK, N_HEADS, HEAD_D, HEAD_D), F32)],
        scratch_shapes=[pltpu.VMEM((tm + HALO, CONV_CH), F32), pltpu.VMEM((N_HEADS, HEAD_D, HEAD_D), F32),
                        pltpu.VMEM((tm, HGRN_W), BF16), pltpu.VMEM((tm, HGRN_W), BF16),
                        pltpu.VMEM((tm, HGRN_W), BF16), pltpu.VMEM((tm, HGRN_W), BF16),
                        pltpu.VMEM((tm, HGRN_W), F32)],
        compiler_params=_cp(("arbitrary",), 56),
    )(_hbm(p), wdw, *vecs, lbl)


def _mix_out_fwd(cat, w_out, x, modr, g_post, g_ffn):
    T = x.shape[0]
    tm = _tok_tile(T)

    def body(cat_ref, w_ref, x_ref, mod_ref, gp_ref, gf_ref, y_ref, x1_ref, h2_ref):
        yv = _dot(cat_ref[...], w_ref[...])
        y_ref[...] = yv
        rs = lax.rsqrt(jnp.mean(yv * yv, axis=-1, keepdims=True) + RMS_EPS)
        x1 = x_ref[...] + mod_ref[2:3, :] * ((yv * rs) * gp_ref[...])
        x1_ref[...] = x1
        rs1 = lax.rsqrt(jnp.mean(x1 * x1, axis=-1, keepdims=True) + RMS_EPS)
        h2 = (x1 * rs1) * gf_ref[...] * (1.0 + mod_ref[4:5, :]) + mod_ref[3:4, :]
        h2_ref[...] = h2.astype(BF16)

    tile = pl.BlockSpec((tm, D_MODEL), lambda i: (i, 0))
    return pl.pallas_call(
        body, name="mix_out_fwd", grid=(T // tm,),
        in_specs=[tile, _full((D_MODEL, D_MODEL)), tile, _full((6, D_MODEL)), _full((1, D_MODEL)),
                  _full((1, D_MODEL))],
        out_specs=[tile, tile, tile],
        out_shape=[_big((T,D_MODEL), F32), _big((T,D_MODEL), F32),
                   _big((T,D_MODEL), BF16)],
        compiler_params=_cp(("arbitrary",), 40),
    )(*_hbm(cat, w_out, x), modr, g_post, g_ffn)


def _ffn_fwd(h2, w_up_g, w_down, x1, target, modr, g_post):
    T = h2.shape[0]
    tm = _tok_tile(T)
    fb = D_FF // N_CHIPS

    def body(h_ref, wu_ref, wd_ref, x1_ref, t_ref, mod_ref, g_ref, a_ref, y2_ref, dx2_ref, ls_ref, acc):
        i, j = pl.program_id(0), pl.program_id(1)
        a = _dot(h_ref[...], wu_ref[0])
        a_ref[...] = a
        ra = jnp.maximum(a, 0.0)
        part = _dot((ra * ra).astype(BF16), wd_ref[...])

        @pl.when(j == 0)
        def _():
            acc[...] = part

        @pl.when(j > 0)
        def _():
            acc[...] += part

        @pl.when((i == 0) & (j == 0))
        def _():
            ls_ref[...] = jnp.zeros(ls_ref.shape, F32)

        @pl.when(j == N_CHIPS - 1)
        def _():
            y2 = acc[...]
            y2_ref[...] = y2
            rs = lax.rsqrt(jnp.mean(y2 * y2, axis=-1, keepdims=True) + RMS_EPS)
            x2 = x1_ref[...] + mod_ref[5:6, :] * ((y2 * rs) * g_ref[...])
            err = x2 - t_ref[...]
            dx2_ref[...] = err * (1.0 / D_MODEL)
            ls_ref[0:1, :] += _colsum(err * err)

    tile = pl.BlockSpec((tm, D_MODEL), lambda i, j: (i, 0))
    return pl.pallas_call(
        body, name="ffn_fwd", grid=(T // tm, N_CHIPS),
        in_specs=[tile, pl.BlockSpec((1, D_MODEL, fb), lambda i, j: (j, 0, 0)),
                  pl.BlockSpec((fb, D_MODEL), lambda i, j: (j, 0)), tile, tile,
                  _full((6, D_MODEL)), _full((1, D_MODEL))],
        out_specs=[pl.BlockSpec((tm, fb), lambda i, j: (i, j)), tile, tile, _full((8, D_MODEL))],
        out_shape=[_big((T,D_FF), F32), _big((T,D_MODEL), F32),
                   _big((T,D_MODEL), F32), jax.ShapeDtypeStruct((8, D_MODEL), F32)],
        scratch_shapes=[pltpu.VMEM((tm, D_MODEL), F32)],
        compiler_params=_cp(("arbitrary", "arbitrary"), 48),
    )(*_hbm(h2, w_up_g, w_down, x1, target), modr, g_post)


def _rms_bwd(dxn, xn, rs):
    return rs * (dxn - xn * jnp.mean(dxn * xn, axis=-1, keepdims=True))


def _ffn_bwd(dx2, y2, a, x1, w_up_g, w_down, modr, g_post, g_ffn):
    T = dx2.shape[0]
    tm = _tok_tile(T)
    fb = D_FF // N_CHIPS

    def body(dx2_ref, y2_ref, a_ref, x1_ref, wu_ref, wd_ref, mod_ref, gp_ref, gf_ref,
             da_ref, r_ref, dy2_ref, dx1_ref, st_ref, dy2_s, dh_s):
        i, j = pl.program_id(0), pl.program_id(1)

        @pl.when((i == 0) & (j == 0))
        def _():
            st_ref[...] = jnp.zeros(st_ref.shape, F32)

        @pl.when(j == 0)
        def _():
            dx2v, y2v = dx2_ref[...], y2_ref[...]
            rs = lax.rsqrt(jnp.mean(y2v * y2v, axis=-1, keepdims=True) + RMS_EPS)
            nh = y2v * rs
            st_ref[0:1, :] += _colsum(dx2v * (nh * gp_ref[...]))
            dn = dx2v * mod_ref[5:6, :]
            st_ref[1:2, :] += _colsum(dn * nh)
            dy2 = _rms_bwd(dn * gp_ref[...], nh, rs).astype(BF16)
            dy2_s[...] = dy2
            dy2_ref[...] = dy2

        av = a_ref[...]
        ra = jnp.maximum(av, 0.0)
        r_ref[...] = (ra * ra).astype(BF16)
        da = (_dot_nt(dy2_s[...], wd_ref[...]) * (2.0 * ra)).astype(BF16)
        da_ref[...] = da
        part = _dot_nt(da, wu_ref[0])

        @pl.when(j == 0)
        def _():
            dh_s[...] = part

        @pl.when(j > 0)
        def _():
            dh_s[...] += part

        @pl.when(j == N_CHIPS - 1)
        def _():
            dh = dh_s[...]
            x1v = x1_ref[...]
            rs1 = lax.rsqrt(jnp.mean(x1v * x1v, axis=-1, keepdims=True) + RMS_EPS)
            xn = x1v * rs1
            st_ref[2:3, :] += _colsum(dh)
            st_ref[3:4, :] += _colsum(dh * (xn * gf_ref[...]))
            dsc = dh * (1.0 + mod_ref[4:5, :])
            st_ref[4:5, :] += _colsum(dsc * xn)
            dx1_ref[...] = dx2_ref[...] + _rms_bwd(dsc * gf_ref[...], xn, rs1)

    tile = pl.BlockSpec((tm, D_MODEL), lambda i, j: (i, 0))
    ftile = pl.BlockSpec((tm, fb), lambda i, j: (i, j))
    return pl.pallas_call(
        body, name="ffn_bwd", grid=(T // tm, N_CHIPS),
        in_specs=[tile, tile, ftile, tile, pl.BlockSpec((1, D_MODEL, fb), lambda i, j: (j, 0, 0)),
                  pl.BlockSpec((fb, D_MODEL), lambda i, j: (j, 0)), _full((6, D_MODEL)), _full((1, D_MODEL)),
                  _full((1, D_MODEL))],
        out_specs=[ftile, ftile, tile, tile, _full((8, D_MODEL))],
        out_shape=[_big((T,D_FF), BF16), _big((T,D_FF), BF16),
                   _big((T,D_MODEL), BF16), _big((T,D_MODEL), F32),
                   jax.ShapeDtypeStruct((8, D_MODEL), F32)],
        scratch_shapes=[pltpu.VMEM((tm, D_MODEL), BF16), pltpu.VMEM((tm, D_MODEL), F32)],
        compiler_params=_cp(("arbitrary", "arbitrary"), 48),
    )(*_hbm(dx2, y2, a, x1, w_up_g, w_down), modr, g_post, g_ffn)


def _mix_out_bwd(dx1, y, w_out, modr, g_post):
    T = dx1.shape[0]
    tm = _tok_tile(T)

    def body(dx1_ref, y_ref, w_ref, mod_ref, gp_ref, dy_ref, dcat_ref, st_ref):
        @pl.when(pl.program_id(0) == 0)
        def _():
            st_ref[...] = jnp.zeros(st_ref.shape, F32)

        dxv, yv = dx1_ref[...], y_ref[...]
        rs = lax.rsqrt(jnp.mean(yv * yv, axis=-1, keepdims=True) + RMS_EPS)
        nh = yv * rs
        st_ref[0:1, :] += _colsum(dxv * (nh * gp_ref[...]))
        dn = dxv * mod_ref[2:3, :]
        st_ref[1:2, :] += _colsum(dn * nh)
        dy = _rms_bwd(dn * gp_ref[...], nh, rs).astype(BF16)
        dy_ref[...] = dy
        dcat_ref[...] = _dot_nt(dy, w_ref[...])

    tile = pl.BlockSpec((tm, D_MODEL), lambda i: (i, 0))
    return pl.pallas_call(
        body, name="mix_out_bwd", grid=(T // tm,),
        in_specs=[tile, tile, _full((D_MODEL, D_MODEL)), _full((6, D_MODEL)), _full((1, D_MODEL))],
        out_specs=[tile, tile, _full((8, D_MODEL))],
        out_shape=[_big((T,D_MODEL), BF16), _big((T,D_MODEL), F32),
                   jax.ShapeDtypeStruct((8, D_MODEL), F32)],
        compiler_params=_cp(("arbitrary",), 40),
    )(*_hbm(dx1, y, w_out), modr, g_post)


def _mixers_bwd(p, dcat, ys, o, states, wdw, vecs, lbl):
    T = p.shape[0]
    tm = _tok_tile(T)
    nt = T // tm
    nch = tm // CHUNK
    hpt = tm // HALO

    def body(p_ref, ph_ref, dcat_ref, ys_ref, o_ref, st_ref, wdw_ref, bdw_ref, gain_ref, bias_ref, gout_ref, lbl_ref,
             dp_ref, sb_ref, s5_ref, dw_ref,
             ubuf, dybuf, carry, dstate, qt_s, kt_s, kh_s, v_s, do_s, egl_s, dqt_s, dkt_s, dkh_s, dv_s, dgl_s):
        i = pl.program_id(0)
        tile_idx = nt - 1 - i

        @pl.when(i == 0)
        def _():
            dstate[...] = jnp.zeros(dstate.shape, F32)
            carry[...] = jnp.zeros(carry.shape, F32)
            sb_ref[...] = jnp.zeros(sb_ref.shape, F32)
            s5_ref[...] = jnp.zeros(s5_ref.shape, F32)
            dw_ref[...] = jnp.zeros(dw_ref.shape, F32)

        uh = ph_ref[:, 0:CONV_CH] * _sig(ph_ref[:, CONV_CH:2 * CONV_CH])
        ubuf[0:HALO, :] = jnp.where(tile_idx > 0, uh, 0.0)
        ubuf[HALO:HALO + tm, :] = p_ref[:, 0:CONV_CH] * _sig(p_ref[:, CONV_CH:2 * CONV_CH])
        gmat = _gn_matrix()
        gain = gain_ref[...]
        yv = ys_ref[...]
        d = yv - _gmean(yv, gmat)
        rs = lax.rsqrt(_gmean(d * d, gmat) + GN_EPS)
        yn = d * rs
        z = yn * gain + bias_ref[...]
        sz = _sig(z)
        dz = dcat_ref[:, 0:CONV_CH] * (sz * (1.0 + z * (1.0 - sz)))
        dyn = dz * gain
        dyc = rs * (dyn - _gmean(dyn, gmat) - yn * _gmean(dyn * yn, gmat))
        s5_ref[0:1, :] += _colsum(dyc)
        s5_ref[1:2, :] += _colsum(dz * yn)
        s5_ref[2:3, :] += _colsum(dz)
        dybuf[tm:tm + HALO, :] = carry[...]
        dybuf[0:tm, :] = dyc
        carry[...] = dyc[0:HALO, :]
        for j in range(CONV_K):
            off = HALO - (CONV_K - 1) + j
            dw_ref[j:j + 1, :] += _colsum(dyc * ubuf[off:off + tm, :])
        for r in range(tm // CONV_ROWS):
            rows = slice(r * CONV_ROWS, (r + 1) * CONV_ROWS)
            acc = jnp.zeros((CONV_ROWS, CONV_CH), F32)
            for j in range(CONV_K):
                off = r * CONV_ROWS + (CONV_K - 1) - j
                acc = acc + wdw_ref[j:j + 1, :] * dybuf[off:off + CONV_ROWS, :]
            val = p_ref[rows, 0:CONV_CH]
            sg = _sig(p_ref[rows, CONV_CH:2 * CONV_CH])
            dval = acc * sg
            dgate = acc * val * (sg * (1.0 - sg))
            dp_ref[rows, 0:CONV_CH] = dval.astype(BF16)
            dp_ref[rows, CONV_CH:2 * CONV_CH] = dgate.astype(BF16)
            sb_ref[0:1, 0:CONV_CH] += _colsum(dval)
            sb_ref[0:1, CONV_CH:2 * CONV_CH] += _colsum(dgate)

        o0 = 2 * CONV_CH
        for h in range(N_HEADS):
            sl = slice(h * HEAD_D, (h + 1) * HEAD_D)
            gsl = slice(o0 + 3 * HGRN_W + h * HEAD_D, o0 + 3 * HGRN_W + (h + 1) * HEAD_D)
            oh = o_ref[:, sl]
            gh = p_ref[:, gsl]
            dh = dcat_ref[:, CONV_CH + h * HEAD_D:CONV_CH + (h + 1) * HEAD_D]
            gout = gout_ref[:, sl]
            rsh = lax.rsqrt(jnp.mean(oh * oh, axis=-1, keepdims=True) + RMS_EPS)
            on = oh * rsh
            sgg = _sig(gh)
            dgh = dh * (on * gout) * (sgg * (1.0 + gh * (1.0 - sgg)))
            dm = dh * (gh * sgg)
            s5_ref[3:4, sl] += _colsum(dm * on)
            do_s[:, sl] = _rms_bwd(dm * gout, on, rsh).astype(BF16)
            dp_ref[:, gsl] = dgh.astype(BF16)
            sb_ref[2:3, CONV_CH + h * HEAD_D:CONV_CH + (h + 1) * HEAD_D] += _colsum(dgh)

        lb, _ = _lower_bound(lbl_ref)
        lower, upper, same = _chunk_masks(tm)
        pq = p_ref[:, o0:o0 + HGRN_W]
        pr = _hgrn_prep(pq, p_ref[:, o0 + HGRN_W:o0 + 2 * HGRN_W], lb, lower, same)
        qt_s[...] = pr["qt"].astype(BF16)
        kt_s[...] = pr["kt"].astype(BF16)
        kh_s[...] = pr["kh"].astype(BF16)
        v_s[...] = p_ref[:, o0 + 2 * HGRN_W:o0 + 3 * HGRN_W].astype(BF16)
        egl_s[...] = jnp.exp(pr["Gl"])
        tri = _tri()

        def chunk(it, c_):
            ci = nch - 1 - it
            r0 = pl.multiple_of(ci * CHUNK, CHUNK)
            rows = pl.ds(r0, CHUNK)
            for h in range(N_HEADS):
                ls = pl.ds(h * HEAD_D, HEAD_D)
                qc, kc, hc, vc = qt_s[rows, ls], kt_s[rows, ls], kh_s[rows, ls], v_s[rows, ls]
                dob = do_s[rows, ls]
                s0 = st_ref[ci, h]
                s0b = s0.astype(BF16)
                ds1 = dstate[h]
                ds1b = ds1.astype(BF16)
                egl = egl_s[pl.ds(r0, 1), ls]
                att = jnp.where(tri, _dot_nt(qc, kc), 0.0).astype(BF16)
                datt = jnp.where(tri, _dot_nt(dob, vc), 0.0).astype(BF16)
                dv_s[rows, ls] = _dot_tn(att, dob) + _dot_nt(hc, ds1b)
                dqt_s[rows, ls] = _dot(datt, kc) + _dot(dob, s0b)
                dkt_s[rows, ls] = _dot_tn(datt, qc)
                dkh_s[rows, ls] = _dot(vc, ds1b)
                dgl = egl * _colsum(ds1 * s0)
                dgl_s[rows, ls] = jnp.broadcast_to(dgl, (CHUNK, HEAD_D))
                dstate[h] = ds1 * egl + _dot_tn(dob, qc)
            return c_

        lax.fori_loop(0, nch, chunk, 0)
        dqt, dkt, dkh = dqt_s[...], dkt_s[...], dkh_s[...]
        dk = dkt * pr["enG"] + dkh * pr["eGlG"]
        khk = dkh * kh_s[...].astype(F32)
        dG = dqt * qt_s[...].astype(F32) - dkt * kt_s[...].astype(F32) - khk
        dlogf = _mm3(upper, dG) + _mm3(same, khk) + dgl_s[...]
        df = dlogf / pr["f"] - dk
        sf, sq = pr["sf"], pr["sq"]
        s5_ref[4:5, :] += _colsum(df * (1.0 - sf))
        dfl = df * (1.0 - lb) * (sf * (1.0 - sf))
        dq = (dqt * pr["eG"]) * (sq * (1.0 + pq * (1.0 - sq)))
        dvv = dv_s[...]
        dp_ref[:, o0:o0 + HGRN_W] = dq.astype(BF16)
        dp_ref[:, o0 + HGRN_W:o0 + 2 * HGRN_W] = dfl.astype(BF16)
        dp_ref[:, o0 + 2 * HGRN_W:o0 + 3 * HGRN_W] = dvv.astype(BF16)
        sb_ref[1:2, 0:HGRN_W] += _colsum(dq)
        sb_ref[1:2, HGRN_W:2 * HGRN_W] += _colsum(dfl)
        sb_ref[2:3, 0:HGRN_W] += _colsum(dvv)

    rev = lambda cols: pl.BlockSpec((tm, cols), lambda i: (nt - 1 - i, 0))
    halo = pl.BlockSpec((HALO, 2 * CONV_CH), lambda i: (jnp.maximum((nt - 1 - i) * hpt - 1, 0), 0))
    wide = lambda n: pltpu.VMEM((tm, HGRN_W), n)
    return pl.pallas_call(
        body, name="mixers_bwd", grid=(nt,),
        in_specs=[rev(IN_COLS), halo, rev(D_MODEL), rev(CONV_CH), rev(HGRN_W),
                  pl.BlockSpec((nch, N_HEADS, HEAD_D, HEAD_D), lambda i: (nt - 1 - i, 0, 0, 0)),
                  _full((HALO, CONV_CH))] + [_full((1, CONV_CH))] * 4 + [_full((2, HGRN_W))],
        out_specs=[rev(IN_COLS), _full((8, D_MODEL)), _full((8, CONV_CH)), _full((HALO, CONV_CH))],
        out_shape=[_big((T,IN_COLS), BF16), jax.ShapeDtypeStruct((8, D_MODEL), F32),
                   jax.ShapeDtypeStruct((8, CONV_CH), F32), jax.ShapeDtypeStruct((HALO, CONV_CH), F32)],
        scratch_shapes=[pltpu.VMEM((tm + HALO, CONV_CH), F32), pltpu.VMEM((tm + HALO, CONV_CH), F32),
                        pltpu.VMEM((HALO, CONV_CH), F32), pltpu.VMEM((N_HEADS, HEAD_D, HEAD_D), F32),
                        wide(BF16), wide(BF16), wide(BF16), wide(BF16), wide(BF16),
                        wide(F32), wide(F32), wide(F32), wide(F32), wide(F32), wide(F32)],
        compiler_params=_cp(("arbitrary",), 56),
    )(*_hbm(p, p, dcat, ys, o, states), wdw, *vecs, lbl)


def _mix_in_bwd(dp, w_in_g, x, dx1, modr, g_pre):
    T = x.shape[0]
    tm = _tok_tile(T)
    nb = IN_COLS // N_CHIPS

    def body(dp_ref, w_ref, x_ref, dx1_ref, mod_ref, g_ref, gx_ref, st_ref):
        @pl.when(pl.program_id(0) == 0)
        def _():
            st_ref[...] = jnp.zeros(st_ref.shape, F32)

        dh = None
        for j in range(N_CHIPS):
            part = _dot_nt(dp_ref[:, j * nb:(j + 1) * nb], w_ref[j])
            dh = part if dh is None else dh + part
        xv = x_ref[...]
        rs = lax.rsqrt(jnp.mean(xv * xv, axis=-1, keepdims=True) + RMS_EPS)
        xn = xv * rs
        st_ref[0:1, :] += _colsum(dh)
        st_ref[1:2, :] += _colsum(dh * (xn * g_ref[...]))
        dsc = dh * (1.0 + mod_ref[1:2, :])
        st_ref[2:3, :] += _colsum(dsc * xn)
        gx_ref[...] = dx1_ref[...] + _rms_bwd(dsc * g_ref[...], xn, rs)

    tile = pl.BlockSpec((tm, D_MODEL), lambda i: (i, 0))
    return pl.pallas_call(
        body, name="mix_in_bwd", grid=(T // tm,),
        in_specs=[pl.BlockSpec((tm, IN_COLS), lambda i: (i, 0)), _full((N_CHIPS, D_MODEL, nb)), tile, tile,
                  _full((6, D_MODEL)), _full((1, D_MODEL))],
        out_specs=[tile, _full((8, D_MODEL))],
        out_shape=[_big((T,D_MODEL), F32), jax.ShapeDtypeStruct((8, D_MODEL), F32)],
        compiler_params=_cp(("arbitrary",), 48),
    )(*_hbm(dp, w_in_g, x, dx1), modr, g_pre)


def _weight_grad(a, b, a_blocked, b_blocked, name):
    T = a.shape[0]
    tt = _tok_tile(T)
    ka = a.shape[1] // N_CHIPS if a_blocked else a.shape[1]
    nb = b.shape[1] // N_CHIPS if b_blocked else b.shape[1]

    def body(a_ref, b_ref, o_ref):
        prod = _dot_tn(a_ref[...], b_ref[...])

        @pl.when(pl.program_id(1) == 0)
        def _():
            o_ref[0] = prod

        @pl.when(pl.program_id(1) > 0)
        def _():
            o_ref[0] += prod

    return pl.pallas_call(
        body, name=name, grid=(N_CHIPS, T // tt),
        in_specs=[pl.BlockSpec((tt, ka), (lambda j, t: (t, j)) if a_blocked else (lambda j, t: (t, 0))),
                  pl.BlockSpec((tt, nb), (lambda j, t: (t, j)) if b_blocked else (lambda j, t: (t, 0)))],
        out_specs=pl.BlockSpec((1, ka, nb), lambda j, t: (j, 0, 0)),
        out_shape=_big((N_CHIPS, ka, nb), F32),
        compiler_params=_cp(("arbitrary", "arbitrary"), 40),
    )(*_hbm(a, b))


R_LOSS = 0
R_FFN = 8
R_OUT = 16
R_IN = 24
R_BIN = 32
R_512 = 40
R_DW = 48
N_STAT_ROWS = 80
MOD_ROWS = (R_IN + 0, R_IN + 1, R_OUT + 0, R_FFN + 2, R_FFN + 3, R_FFN + 0)


def _small_update(gath, params):
    names = ["b_ada", "lb_logits", "g_pre_mix", "b_in", "b_dw", "gn_gain", "gn_bias", "g_hgrn_out", "g_post_mix",
             "g_pre_ffn", "g_post_ffn"]
    flat = []
    for n in names:
        flat += list(params[n])
    n_in = 1 + len(flat)

    def body(*refs):
        g_ref = refs[0]
        prm = {n: refs[1 + 3 * k:4 + 3 * k] for k, n in enumerate(names)}
        outs = refs[n_in:]
        loss_ref, dmod_ref, dwdw_ref = outs[0], outs[1], outs[2]
        res = {n: outs[3 + 4 * k:7 + 4 * k] for k, n in enumerate(names)}
        red = g_ref[0]
        for dev in range(1, N_DEV):
            red = red + g_ref[dev]
        loss_ref[...] = jnp.broadcast_to(
            (0.5 / D_MODEL) * jnp.sum(red[R_LOSS:R_LOSS + 1, :], axis=-1, keepdims=True), loss_ref.shape)
        for dev in range(N_DEV):
            for k, r in enumerate(MOD_ROWS):
                dmod_ref[dev:dev + 1, k * D_MODEL:(k + 1) * D_MODEL] = g_ref[dev, r:r + 1, :]
        dwdw_ref[...] = red[R_DW:R_DW + HALO, 0:CONV_CH]

        def finish(name, pieces):
            w_ref, m_ref, v_ref = prm[name]
            g_out, d_out, m_out, v_out = res[name]
            for rsl, lsl, g in pieces:
                d, m2, v2 = _adam_math(w_ref[rsl, lsl], g, m_ref[rsl, lsl], v_ref[rsl, lsl])
                g_out[rsl, lsl] = g
                d_out[rsl, lsl] = d
                m_out[rsl, lsl] = m2
                v_out[rsl, lsl] = v2

        one = slice(0, 1)
        row = lambda r: red[r:r + 1, :]
        half = lambda r: red[r:r + 1, 0:CONV_CH]
        finish("b_ada", [(one, slice(k * D_MODEL, (k + 1) * D_MODEL), row(r)) for k, r in enumerate(MOD_ROWS)])
        finish("b_in", [(one, slice(k * D_MODEL, (k + 1) * D_MODEL), row(R_BIN + k)) for k in range(3)])
        finish("g_pre_mix", [(one, slice(None), row(R_IN + 2))])
        finish("g_post_mix", [(one, slice(None), row(R_OUT + 1))])
        finish("g_pre_ffn", [(one, slice(None), row(R_FFN + 4))])
        finish("g_post_ffn", [(one, slice(None), row(R_FFN + 1))])
        finish("b_dw", [(one, slice(None), half(R_512 + 0))])
        finish("gn_gain", [(one, slice(None), half(R_512 + 1))])
        finish("gn_bias", [(one, slice(None), half(R_512 + 2))])
        finish("g_hgrn_out", [(one, slice(None), half(R_512 + 3))])
        s0, s1 = _lower_bound(prm["lb_logits"][0])
        dlb = half(R_512 + 4)
        finish("lb_logits", [(slice(0, 1), slice(None), dlb * s0 * (1.0 - s0)),
                             (slice(1, 2), slice(None), -dlb * s0 * s1)])

    vm = pl.BlockSpec(memory_space=pltpu.VMEM)
    out_shape = [jax.ShapeDtypeStruct((8, 128), F32), jax.ShapeDtypeStruct((N_DEV, 6 * D_MODEL), F32),
                 jax.ShapeDtypeStruct((HALO, CONV_CH), F32)]
    for n in names:
        out_shape += [jax.ShapeDtypeStruct(params[n][0].shape, F32)] * 4
    outs = pl.pallas_call(
        body, name="small_update", out_shape=out_shape,
        in_specs=[vm] * n_in, out_specs=[vm] * len(out_shape),
        compiler_params=_cp(None, 32),
    )(gath, *flat)
    return outs[0], outs[1], outs[2], {n: outs[3 + 4 * k:7 + 4 * k] for k, n in enumerate(names)}


def _wdw_adam(w, g, m, v):
    def body(w_ref, g_ref, m_ref, v_ref, d_out, m_out, v_out):
        d, m2, v2 = _adam_math(w_ref[...], g_ref[...], m_ref[...], v_ref[...])
        d_out[...] = d
        m_out[...] = m2
        v_out[...] = v2

    vm = pl.BlockSpec(memory_space=pltpu.VMEM)
    return pl.pallas_call(
        body, name="wdw_adam", out_shape=[jax.ShapeDtypeStruct(w.shape, F32)] * 3,
        in_specs=[vm] * 4, out_specs=[vm] * 3, compiler_params=_cp(None, 16),
    )(w, g, m, v)


def kernel(x, c, w_ada, b_ada, lb_logits, g_pre_mix, w_in, b_in, w_dw, b_dw, gn_gain, gn_bias, g_hgrn_out, w_out, g_post_mix, g_pre_ffn, w_up, w_down, g_post_ffn, loss_target, m_w_ada, m_b_ada, m_lb_logits, m_g_pre_mix, m_w_in, m_b_in, m_w_dw, m_b_dw, m_gn_gain, m_gn_bias, m_g_hgrn_out, m_w_out, m_g_post_mix, m_g_pre_ffn, m_w_up, m_w_down, m_g_post_ffn, v_w_ada, v_b_ada, v_lb_logits, v_g_pre_mix, v_w_in, v_b_in, v_w_dw, v_b_dw, v_gn_gain, v_gn_bias, v_g_hgrn_out, v_w_out, v_g_post_mix, v_g_pre_ffn, v_w_up, v_w_down, v_g_post_ffn):
    ax, ay, ac = lax.axis_index("x"), lax.axis_index("y"), lax.axis_index("c")
    chip = 2 * ax + ay
    T = x.shape[1]
    xs, tgt = x[0], loss_target[0]
    ada_cols = w_ada.shape[2]

    b_sh = lax.dynamic_slice_in_dim(b_ada, chip * ada_cols, ada_cols, axis=1)
    wdw_pad = jnp.pad(w_dw[0], ((0, HALO - CONV_K), (0, 0)))
    _, c8, modg, wdwg = _ada_exchange(c, w_ada[0], b_sh, wdw_pad)
    modr = modg.reshape(6, D_MODEL)
    wdw_all = jnp.transpose(wdwg, (1, 0, 2)).reshape(HALO, CONV_CH)
    w_in_g, w_out_g, w_up_g, w_down_g = _weight_gather([w_in[0], w_out[0], w_up[0], w_down[0]])
    w_out_f = w_out_g.reshape(D_MODEL, D_MODEL)
    w_down_f = w_down_g.reshape(D_FF, D_MODEL)
    vec = (b_dw, gn_gain, gn_bias, g_hgrn_out)

    p, h1 = _mix_in_fwd(xs, modr, g_pre_mix, w_in_g, b_in)
    cat, ys, o, states = _mixers_fwd(p, wdw_all, vec, lb_logits)
    y, x1, h2 = _mix_out_fwd(cat, w_out_f, xs, modr, g_post_mix, g_pre_ffn)
    a, y2, dx2, st_loss = _ffn_fwd(h2, w_up_g, w_down_f, x1, tgt, modr, g_post_ffn)

    da, r, dy2, dx1, st_ffn = _ffn_bwd(dx2, y2, a, x1, w_up_g, w_down_f, modr, g_post_ffn, g_pre_ffn)
    dy, dcat, st_out = _mix_out_bwd(dx1, y, w_out_f, modr, g_post_mix)
    dp, st_bin, st_512, dwdw = _mixers_bwd(p, dcat, ys, o, states, wdw_all, vec, lb_logits)
    grad_x, st_in = _mix_in_bwd(dp, w_in_g, xs, dx1, modr, g_pre_mix)
    big = [_weight_grad(h1, dp, False, True, "grad_w_in"), _weight_grad(cat, dy, True, False, "grad_w_out"),
           _weight_grad(h2, da, False, True, "grad_w_up"), _weight_grad(r, dy2, True, False, "grad_w_down")]

    core = jnp.reshape(ac, (1,)).astype(jnp.int32)
    chip1 = jnp.reshape(chip, (1,)).astype(jnp.int32)
    tags = ["w_in", "w_out", "w_up", "w_down"]
    got = _pair_swap(big)
    pairs = [_pair_sum(core, g, o_, "pair_sum_" + t) for g, o_, t in zip(big, got, tags)]
    got_b = _chip_exchange([pb for _, pb in pairs])
    halves = [_chip_sum(chip1, pf, gb, "chip_sum_" + t) for (pf, _), gb, t in zip(pairs, got_b, tags)]
    g_w_in, g_w_out, g_w_up, g_w_down = _half_swap(halves)

    pad_lanes = lambda s: jnp.pad(s, ((0, 0), (0, D_MODEL - s.shape[1])))
    stats = jnp.concatenate([st_loss, st_ffn, st_out, st_in, st_bin, pad_lanes(st_512), pad_lanes(dwdw)], axis=0)
    gath = _stats_gather(stats)
    small = {"b_ada": (b_ada, m_b_ada, v_b_ada), "lb_logits": (lb_logits, m_lb_logits, v_lb_logits),
             "g_pre_mix": (g_pre_mix, m_g_pre_mix, v_g_pre_mix), "b_in": (b_in, m_b_in, v_b_in),
             "b_dw": (b_dw, m_b_dw, v_b_dw), "gn_gain": (gn_gain, m_gn_gain, v_gn_gain),
             "gn_bias": (gn_bias, m_gn_bias, v_gn_bias), "g_hgrn_out": (g_hgrn_out, m_g_hgrn_out, v_g_hgrn_out),
             "g_post_mix": (g_post_mix, m_g_post_mix, v_g_post_mix), "g_pre_ffn": (g_pre_ffn, m_g_pre_ffn, v_g_pre_ffn),
             "g_post_ffn": (g_post_ffn, m_g_post_ffn, v_g_post_ffn)}
    loss_t, dmod_all, dwdw_sum, sres = _small_update(gath, small)
    loss = loss_t[0, 0]

    res = dict(sres)
    dmod_sh = lax.dynamic_slice_in_dim(dmod_all, chip * ada_cols, ada_cols, axis=1)
    res["w_ada"] = [t[None] for t in _ada_grad_adam(jnp.transpose(c8), dmod_sh, w_ada[0], m_w_ada[0], v_w_ada[0])]
    g_wdw = lax.dynamic_slice_in_dim(dwdw_sum, chip * HEAD_D, HEAD_D, axis=1)[:CONV_K][None]
    res["w_dw"] = [g_wdw] + list(_wdw_adam(w_dw, g_wdw, m_w_dw, v_w_dw))
    for name, g, w, m, v in (("w_in", g_w_in, w_in, m_w_in, v_w_in), ("w_out", g_w_out, w_out, m_w_out, v_w_out),
                             ("w_up", g_w_up, w_up, m_w_up, v_w_up), ("w_down", g_w_down, w_down, m_w_down, v_w_down)):
        d, m2, v2 = _adam_big(w[0], g, m[0], v[0], "adam_" + name)
        res[name] = [g[None], d[None], m2[None], v2[None]]

    order = ["w_ada", "b_ada", "lb_logits", "g_pre_mix", "w_in", "b_in", "w_dw", "b_dw", "gn_gain", "gn_bias",
             "g_hgrn_out", "w_out", "g_post_mix", "g_pre_ffn", "w_up", "w_down", "g_post_ffn"]
    out = [loss, grad_x[None]]
    for k in range(4):
        out += [res[n][k] for n in order]
    return tuple(out)
```

```python
import jax
import jax.numpy as jnp
from jax import lax
from jax.experimental import pallas as pl
from jax.experimental.pallas import tpu as pltpu

F32, BF16 = jnp.float32, jnp.bfloat16
D_MODEL = 1024
CONV_CH = 512
HGRN_W = 512
N_HEADS = 4
HEAD_D = 128
CONV_K = 31
GN_GROUP = 64
GN_SHIFT = 6
IN_COLS = 3072
D_FF = 4096
CHUNK = 64
CHUNK_SHIFT = 6
N_CHIPS = 4
N_DEV = 8
RMS_EPS = 1e-6
GN_EPS = 1e-5
ADAM_LR, ADAM_B1, ADAM_B2, ADAM_EPS, ADAM_WD, ADAM_STEP = 0.001, 0.9, 0.999, 1e-08, 0.01, 10
TOK_TILE = 512
HALO = 32
CONV_ROWS = 64
MIB = 1 << 20
MESH = pl.DeviceIdType.MESH
OTHER_CHIPS = ((0, 1), (1, 0), (1, 1))


def _cp(sem=None, vmem_mib=48):
    return pltpu.CompilerParams(dimension_semantics=sem, vmem_limit_bytes=vmem_mib * MIB)


def _dot(a, b):
    return jnp.dot(a, b, preferred_element_type=F32)


def _dot_nt(a, b):
    return lax.dot_general(a, b, (((1,), (1,)), ((), ())), preferred_element_type=F32)


def _dot_tn(a, b):
    return lax.dot_general(a, b, (((0,), (0,)), ((), ())), preferred_element_type=F32)


def _sig(v):
    return jax.nn.sigmoid(v)


def _colsum(v):
    return jnp.sum(v, axis=0, keepdims=True)


def _flip(v, b):
    return 1 - v if b else v


def _rcopy(src, dst, ssem, rsem, dev):
    return pltpu.make_async_remote_copy(src_ref=src, dst_ref=dst, send_sem=ssem, recv_sem=rsem,
                                        device_id=dev, device_id_type=MESH)


def _place():
    return lax.axis_index("x"), lax.axis_index("y"), lax.axis_index("c")


def _full(shape):
    return pl.BlockSpec(shape, lambda *_: (0,) * len(shape))


def _big(shape, dtype):
    return pltpu.HBM(shape, dtype)


def _hbm(*arrays):
    out = [pltpu.with_memory_space_constraint(a, pltpu.HBM) for a in arrays]
    return out[0] if len(out) == 1 else out


def _split2(v):
    hi = v.astype(BF16)
    lo = (v - hi.astype(F32)).astype(BF16)
    return hi, lo


def _split3(v):
    h1 = v.astype(BF16)
    r1 = v - h1.astype(F32)
    h2 = r1.astype(BF16)
    h3 = (r1 - h2.astype(F32)).astype(BF16)
    return h1, h2, h3


def _mm3(mat, v):
    h1, h2, h3 = _split3(v)
    return _dot(mat, h1) + _dot(mat, h2) + _dot(mat, h3)


def _gn_matrix():
    r = lax.broadcasted_iota(jnp.int32, (CONV_CH, CONV_CH), 0) >> GN_SHIFT
    c = lax.broadcasted_iota(jnp.int32, (CONV_CH, CONV_CH), 1) >> GN_SHIFT
    return jnp.where(r == c, 1.0 / GN_GROUP, 0.0).astype(BF16)


def _gmean(v, gmat):
    hi, lo = _split2(v)
    return _dot(hi, gmat) + _dot(lo, gmat)


def _chunk_masks(tm):
    r = lax.broadcasted_iota(jnp.int32, (tm, tm), 0)
    c = lax.broadcasted_iota(jnp.int32, (tm, tm), 1)
    same = (r >> CHUNK_SHIFT) == (c >> CHUNK_SHIFT)
    one = lambda m: jnp.where(m, 1.0, 0.0).astype(BF16)
    return one(same & (c <= r)), one(same & (c >= r)), one(same)


def _tri():
    return lax.broadcasted_iota(jnp.int32, (CHUNK, CHUNK), 0) >= lax.broadcasted_iota(jnp.int32, (CHUNK, CHUNK), 1)


def _lower_bound(lbl_ref):
    l0, l1 = lbl_ref[0:1, :], lbl_ref[1:2, :]
    mx = jnp.maximum(l0, l1)
    e0, e1 = jnp.exp(l0 - mx), jnp.exp(l1 - mx)
    return e0 / (e0 + e1), e1 / (e0 + e1)


def _hgrn_prep(pq, pf, lb, lower, same):
    sq = _sig(pq)
    qf = pq * sq
    sf = _sig(pf)
    f = lb + (1.0 - lb) * sf
    logf = jnp.log(f)
    k = 1.0 - f
    G = _mm3(lower, logf)
    Gl = _mm3(same, logf)
    eG, enG, eGlG = jnp.exp(G), jnp.exp(-G), jnp.exp(Gl - G)
    return dict(sq=sq, sf=sf, f=f, Gl=Gl, eG=eG, enG=enG, eGlG=eGlG, qt=qf * eG, kt=k * enG, kh=k * eGlG)


def _ada_exchange(c_row, w_ada, b_sh, wdw_pad):
    ncol = w_ada.shape[1]

    def body(c_ref, w_ref, b_ref, wdw_ref, call_ref, c8_ref, modg_ref, wdwg_ref, rows_s, sa, ra, sw, rw, sm, rm):
        x, y, c = _place()
        me = 4 * x + 2 * y + c
        chip = 2 * x + y
        cv = c_ref[...]
        call_ref[me] = cv * _sig(cv)
        wdwg_ref[chip] = wdw_ref[...]
        sends = []
        for m in range(1, N_DEV):
            peer = (_flip(x, m >> 2), _flip(y, (m >> 1) & 1), _flip(c, m & 1))
            cp = _rcopy(call_ref.at[me], call_ref.at[me], sa.at[m - 1], ra.at[m - 1], peer)
            cp.start()
            sends.append(cp)
        for k, (fx, fy) in enumerate(OTHER_CHIPS):
            peer = (_flip(x, fx), _flip(y, fy), c)
            cp = _rcopy(wdwg_ref.at[chip], wdwg_ref.at[chip], sw.at[k], rw.at[k], peer)
            cp.start()
            sends.append(cp)
        for m in range(1, N_DEV):
            peer = (_flip(x, m >> 2), _flip(y, (m >> 1) & 1), _flip(c, m & 1))
            pid = 4 * peer[0] + 2 * peer[1] + peer[2]
            _rcopy(call_ref.at[pid], call_ref.at[pid], sa.at[m - 1], ra.at[m - 1], peer).wait_recv()
        for b in range(N_DEV):
            c8_ref[b:b + 1, :] = call_ref[b]
        mod_all = _dot(c8_ref[...].astype(BF16), w_ref[...].astype(BF16)) + b_ref[...]
        for b in range(N_DEV):
            rows_s[b] = mod_all[b:b + 1, :]
        modg_ref[chip] = rows_s[me]
        for k, (fx, fy) in enumerate(OTHER_CHIPS):
            peer = (_flip(x, fx), _flip(y, fy), c)
            pid = 4 * peer[0] + 2 * peer[1] + peer[2]
            cp = _rcopy(rows_s.at[pid], modg_ref.at[chip], sm.at[k], rm.at[k], peer)
            cp.start()
            sends.append(cp)
        for k, (fx, fy) in enumerate(OTHER_CHIPS):
            peer = (_flip(x, fx), _flip(y, fy), c)
            pchip = 2 * peer[0] + peer[1]
            _rcopy(rows_s.at[0], modg_ref.at[pchip], sm.at[k], rm.at[k], peer).wait_recv()
            _rcopy(wdwg_ref.at[pchip], wdwg_ref.at[pchip], sw.at[k], rw.at[k], peer).wait_recv()
        for cp in sends:
            cp.wait_send()

    vm = pl.BlockSpec(memory_space=pltpu.VMEM)
    return pl.pallas_call(
        body, name="ada_exchange",
        out_shape=[jax.ShapeDtypeStruct((N_DEV, 1, D_MODEL), F32), jax.ShapeDtypeStruct((N_DEV, D_MODEL), F32),
                   jax.ShapeDtypeStruct((N_CHIPS, 1, ncol), F32), jax.ShapeDtypeStruct((N_CHIPS, HALO, HEAD_D), F32)],
        in_specs=[vm] * 4, out_specs=[vm] * 4,
        scratch_shapes=[pltpu.VMEM((N_DEV, 1, ncol), F32),
                        pltpu.SemaphoreType.DMA((N_DEV - 1,)), pltpu.SemaphoreType.DMA((N_DEV - 1,)),
                        pltpu.SemaphoreType.DMA((3,)), pltpu.SemaphoreType.DMA((3,)),
                        pltpu.SemaphoreType.DMA((3,)), pltpu.SemaphoreType.DMA((3,))],
        compiler_params=_cp(None, 32),
    )(c_row, w_ada, b_sh, wdw_pad)


def _cast_own(chip1, shard, name):
    rows, cols = shard.shape
    tr = _row_tile(rows)

    def body(ch_ref, s_ref, o_ref):
        o_ref[0] = s_ref[...].astype(BF16)

    gs = pltpu.PrefetchScalarGridSpec(
        num_scalar_prefetch=1, grid=(rows // tr,),
        in_specs=[pl.BlockSpec((tr, cols), lambda i, ch: (i, 0))],
        out_specs=pl.BlockSpec((1, tr, cols), lambda i, ch: (ch[0], i, 0)))
    return pl.pallas_call(
        body, name=name, grid_spec=gs, out_shape=_big((N_CHIPS, rows, cols), BF16),
        compiler_params=_cp(("arbitrary",), 32),
    )(chip1, _hbm(shard))


def _slab(buf, ch, core):
    hs = buf.shape[1] // 2
    return buf.at[ch, pl.ds(core * hs, hs), :]


def _gather_start(bufs, ssem, rsem):
    x, y, c = _place()
    chip = 2 * x + y
    for k, (fx, fy) in enumerate(OTHER_CHIPS):
        peer = (_flip(x, fx), _flip(y, fy), c)
        for t, buf in enumerate(bufs):
            _rcopy(_slab(buf, chip, c), _slab(buf, chip, c), ssem.at[t * 3 + k], rsem.at[t * 3 + k], peer).start()


def _gather_finish(bufs, ssem, rsem):
    nt = len(bufs)
    x, y, c = _place()
    chip = 2 * x + y
    sibling = (x, y, 1 - c)
    for k, (fx, fy) in enumerate(OTHER_CHIPS):
        peer = (_flip(x, fx), _flip(y, fy), c)
        pchip = 2 * peer[0] + peer[1]
        for t, buf in enumerate(bufs):
            _rcopy(_slab(buf, pchip, c), _slab(buf, pchip, c), ssem.at[t * 3 + k], rsem.at[t * 3 + k], peer).wait_recv()
            _rcopy(_slab(buf, pchip, c), _slab(buf, pchip, c), ssem.at[3 * nt + t * 3 + k],
                   rsem.at[3 * nt + t * 3 + k], sibling).start()
    for k, (fx, fy) in enumerate(OTHER_CHIPS):
        peer = (_flip(x, fx), _flip(y, fy), c)
        pchip = 2 * peer[0] + peer[1]
        for t, buf in enumerate(bufs):
            _rcopy(_slab(buf, pchip, 1 - c), _slab(buf, pchip, 1 - c), ssem.at[3 * nt + t * 3 + k],
                   rsem.at[3 * nt + t * 3 + k], sibling).wait_recv()
            _rcopy(_slab(buf, chip, c), _slab(buf, chip, c), ssem.at[t * 3 + k], rsem.at[t * 3 + k], peer).wait_send()
            _rcopy(_slab(buf, pchip, c), _slab(buf, pchip, c), ssem.at[3 * nt + t * 3 + k],
                   rsem.at[3 * nt + t * 3 + k], sibling).wait_send()


def _gather_sems(nt):
    return [pltpu.SemaphoreType.DMA((6 * nt,)), pltpu.SemaphoreType.DMA((6 * nt,))]


def _weight_gather(bufs, name):
    nt = len(bufs)
    hbm = pl.BlockSpec(memory_space=pl.ANY)

    def body(*refs):
        outs, ssem, rsem = refs[nt:2 * nt], refs[2 * nt], refs[2 * nt + 1]
        _gather_start(outs, ssem, rsem)
        _gather_finish(outs, ssem, rsem)

    return pl.pallas_call(
        body, name=name, out_shape=[_big(b.shape, BF16) for b in bufs],
        in_specs=[hbm] * nt, out_specs=[hbm] * nt, input_output_aliases={t: t for t in range(nt)},
        scratch_shapes=_gather_sems(nt),
    )(*[_hbm(b) for b in bufs])


def _stats_gather(stats):
    rows, cols = stats.shape

    def body(s_ref, out_ref, ssem, rsem):
        x, y, c = _place()
        me, sibling = (x, y, c), (x, y, 1 - c)
        chips = [(_flip(x, fx), _flip(y, fy)) for fx, fy in OTHER_CHIPS]

        def blk(px, py, pc):
            return out_ref.at[4 * px + 2 * py + pc]

        def copy(k, block, to, src=None):
            return _rcopy(blk(*block) if src is None else src, blk(*block), ssem.at[k], rsem.at[k], to)

        out_ref[4 * x + 2 * y + c] = s_ref[...]
        first = [copy(0, me, sibling, src=s_ref)]
        first += [copy(1 + j, me, (*chip, c), src=s_ref) for j, chip in enumerate(chips)]
        for cp in first:
            cp.start()
        passed = [copy(4 + j, (*chip, c), sibling) for j, chip in enumerate(chips)]
        for j, chip in enumerate(chips):
            copy(1 + j, (*chip, c), me).wait_recv()
            passed[j].start()
        copy(0, sibling, me).wait_recv()
        for j, chip in enumerate(chips):
            copy(4 + j, (*chip, 1 - c), me).wait_recv()
        for cp in first + passed:
            cp.wait_send()

    vm = pl.BlockSpec(memory_space=pltpu.VMEM)
    return pl.pallas_call(
        body, name="stats_gather",
        out_shape=jax.ShapeDtypeStruct((N_DEV, rows, cols), F32),
        in_specs=[vm], out_specs=vm,
        scratch_shapes=[pltpu.SemaphoreType.DMA((7,)), pltpu.SemaphoreType.DMA((7,))],
        compiler_params=_cp(None, 32),
    )(stats)


def _pair_swap(grads, name):
    nt = len(grads)
    hbm = pl.BlockSpec(memory_space=pl.ANY)

    def body(*refs):
        ins, outs = refs[:nt], refs[nt:2 * nt]
        ssem, rsem = refs[2 * nt], refs[2 * nt + 1]
        x, y, c = _place()
        sibling = (x, y, 1 - c)
        copies = []
        for t in range(nt):
            hs = ins[t].shape[1] // 2
            cp = _rcopy(ins[t].at[:, pl.ds((1 - c) * hs, hs), :], outs[t], ssem.at[t], rsem.at[t], sibling)
            cp.start()
            copies.append(cp)
        for cp in copies:
            cp.wait_recv()
        for cp in copies:
            cp.wait_send()

    return pl.pallas_call(
        body, name=name,
        out_shape=[jax.ShapeDtypeStruct((g.shape[0], g.shape[1] // 2, g.shape[2]), F32) for g in grads],
        in_specs=[hbm] * nt, out_specs=[hbm] * nt,
        scratch_shapes=[pltpu.SemaphoreType.DMA((nt,)), pltpu.SemaphoreType.DMA((nt,))],
    )(*[_hbm(g) for g in grads])


def _xchg_copies(ins, outs, ssem, rsem):
    x, y, c = _place()
    copies = []
    for k, (fx, fy) in enumerate(OTHER_CHIPS):
        peer = (_flip(x, fx), _flip(y, fy), c)
        for t in range(len(ins)):
            copies.append(_rcopy(ins[t].at[k], outs[t].at[k], ssem.at[t * 3 + k], rsem.at[t * 3 + k], peer))
    return copies


def _xchg_sems(nt):
    return [pltpu.SemaphoreType.DMA((3 * nt,)), pltpu.SemaphoreType.DMA((3 * nt,))]


def _chip_exchange(pairs_b, name):
    nt = len(pairs_b)
    hbm = pl.BlockSpec(memory_space=pl.ANY)

    def body(*refs):
        copies = _xchg_copies(refs[:nt], refs[nt:2 * nt], refs[2 * nt], refs[2 * nt + 1])
        for cp in copies:
            cp.start()
        for cp in copies:
            cp.wait_recv()
        for cp in copies:
            cp.wait_send()

    return pl.pallas_call(
        body, name=name, out_shape=[_big(p.shape, BF16) for p in pairs_b],
        in_specs=[hbm] * nt, out_specs=[hbm] * nt, scratch_shapes=_xchg_sems(nt),
    )(*[_hbm(p) for p in pairs_b])


def _half_swap(fulls):
    nt = len(fulls)
    hbm = pl.BlockSpec(memory_space=pl.ANY)

    def body(*refs):
        ins, outs, ssem, rsem = refs[:nt], refs[nt:2 * nt], refs[2 * nt], refs[2 * nt + 1]
        x, y, c = _place()
        sibling = (x, y, 1 - c)
        copies = []
        for t in range(nt):
            hs = ins[t].shape[0] // 2
            mine = pl.ds(c * hs, hs)
            cp = _rcopy(ins[t].at[mine, :], outs[t].at[mine, :], ssem.at[t], rsem.at[t], sibling)
            cp.start()
            copies.append(cp)
        for t in range(nt):
            hs = ins[t].shape[0] // 2
            other = pl.ds((1 - c) * hs, hs)
            _rcopy(ins[t].at[other, :], outs[t].at[other, :], ssem.at[t], rsem.at[t], sibling).wait_recv()
        for cp in copies:
            cp.wait_send()

    return pl.pallas_call(
        body, name="grad_half_swap", out_shape=[_big(f.shape, F32) for f in fulls],
        in_specs=[hbm] * nt, out_specs=[hbm] * nt, input_output_aliases={t: t for t in range(nt)},
        scratch_shapes=[pltpu.SemaphoreType.DMA((nt,)), pltpu.SemaphoreType.DMA((nt,))],
    )(*[_hbm(f) for f in fulls])


def _row_tile(rows):
    return min(rows, 256)


def _pair_sum(place, grad, got, name):
    nb, hs, cols = got.shape
    tr = _row_tile(hs)
    nr = hs // tr

    def body(pl_ref, g_ref, o_ref, pf_ref, pb_ref):
        j = pl.program_id(1)
        s = g_ref[0] + o_ref[0]

        @pl.when(j == 0)
        def _():
            pf_ref[...] = s

        @pl.when(j > 0)
        def _():
            pb_ref[0] = s.astype(BF16)

    gs = pltpu.PrefetchScalarGridSpec(
        num_scalar_prefetch=1, grid=(nr, nb),
        in_specs=[pl.BlockSpec((1, tr, cols), lambda i, j, p: (p[1] ^ j, p[0] * nr + i, 0)),
                  pl.BlockSpec((1, tr, cols), lambda i, j, p: (p[1] ^ j, i, 0))],
        out_specs=[pl.BlockSpec((tr, cols), lambda i, j, p: (i, 0)),
                   pl.BlockSpec((1, tr, cols), lambda i, j, p: (jnp.maximum(j - 1, 0), i, 0))])
    return pl.pallas_call(
        body, name=name, grid_spec=gs,
        out_shape=[_big((hs, cols), F32), _big((nb - 1, hs, cols), BF16)],
        compiler_params=_cp(("arbitrary", "arbitrary"), 32),
    )(place, *_hbm(grad, got))


def _chip_sum(place, pair_f, got_b, name):
    nb, hs, cols = got_b.shape
    tr = _row_tile(hs)
    nr = hs // tr

    def body(pl_ref, pf_ref, gb_ref, o_ref):
        acc = pf_ref[...]
        for k in range(nb):
            acc = acc + gb_ref[k].astype(F32)
        o_ref[...] = acc

    gs = pltpu.PrefetchScalarGridSpec(
        num_scalar_prefetch=1, grid=(nr,),
        in_specs=[pl.BlockSpec((tr, cols), lambda i, p: (i, 0)),
                  pl.BlockSpec((nb, tr, cols), lambda i, p: (0, i, 0))],
        out_specs=pl.BlockSpec((tr, cols), lambda i, p: (p[0] * nr + i, 0)))
    return pl.pallas_call(
        body, name=name, grid_spec=gs,
        out_shape=_big((2 * hs, cols), F32),
        compiler_params=_cp(("arbitrary",), 32),
    )(place, *_hbm(pair_f, got_b))


def _adam_math(w, g, m, v):
    m2 = ADAM_B1 * m + (1.0 - ADAM_B1) * g
    v2 = ADAM_B2 * v + (1.0 - ADAM_B2) * (g * g)
    m_hat = m2 / (1.0 - ADAM_B1 ** ADAM_STEP)
    v_hat = v2 / (1.0 - ADAM_B2 ** ADAM_STEP)
    delta = -ADAM_LR * (m_hat / (jnp.sqrt(v_hat) + ADAM_EPS) + ADAM_WD * w)
    return delta, m2, v2


def _adam_big(w, g, m, v, name):
    rows, cols = w.shape
    tr = _row_tile(rows)

    def body(w_ref, g_ref, m_ref, v_ref, d_out, m_out, v_out):
        d, m2, v2 = _adam_math(w_ref[...], g_ref[...], m_ref[...], v_ref[...])
        d_out[...] = d
        m_out[...] = m2
        v_out[...] = v2

    spec = pl.BlockSpec((tr, cols), lambda i: (i, 0))
    return pl.pallas_call(
        body, name=name, grid=(rows // tr,), in_specs=[spec] * 4, out_specs=[spec] * 3,
        out_shape=[_big(w.shape, F32)] * 3,
        compiler_params=_cp(("arbitrary",), 32),
    )(*_hbm(w, g, m, v))


def _ada_grad_adam(c8t, dmod_sh, w, m, v):
    rows, cols = w.shape
    tr = _row_tile(rows)

    def body(ct_ref, dm_ref, w_ref, m_ref, v_ref, g_out, d_out, m_out, v_out):
        g = None
        for b in range(N_DEV):
            term = ct_ref[:, b:b + 1] * dm_ref[b:b + 1, :]
            g = term if g is None else g + term
        d, m2, v2 = _adam_math(w_ref[...], g, m_ref[...], v_ref[...])
        g_out[...] = g
        d_out[...] = d
        m_out[...] = m2
        v_out[...] = v2

    spec = pl.BlockSpec((tr, cols), lambda i: (i, 0))
    return pl.pallas_call(
        body, name="ada_grad_adam", grid=(rows // tr,),
        in_specs=[pl.BlockSpec((tr, N_DEV), lambda i: (i, 0)), _full((N_DEV, cols)), spec, spec, spec],
        out_specs=[spec] * 4, out_shape=[_big(w.shape, F32)] * 4,
        compiler_params=_cp(("arbitrary",), 32),
    )(c8t, dmod_sh, *_hbm(w, m, v))


def _tok_tile(t):
    return min(TOK_TILE, t)


def _mix_in_fwd(x, modr, g_pre, w_in_g, b_in):
    T = x.shape[0]
    tm = _tok_tile(T)
    nb = IN_COLS // N_CHIPS

    def body(x_ref, mod_ref, g_ref, w_ref, b_ref, p_ref, h_ref):
        xv = x_ref[...]
        rstd = lax.rsqrt(jnp.mean(xv * xv, axis=-1, keepdims=True) + RMS_EPS)
        h = (xv * rstd) * g_ref[...] * (1.0 + mod_ref[1:2, :]) + mod_ref[0:1, :]
        hb = h.astype(BF16)
        h_ref[...] = hb
        for j in range(N_CHIPS):
            p_ref[:, j * nb:(j + 1) * nb] = _dot(hb, w_ref[j]) + b_ref[:, j * nb:(j + 1) * nb]

    return pl.pallas_call(
        body, name="mix_in_fwd", grid=(T // tm,),
        in_specs=[pl.BlockSpec((tm, D_MODEL), lambda i: (i, 0)), _full((6, D_MODEL)), _full((1, D_MODEL)),
                  _full((N_CHIPS, D_MODEL, nb)), _full((1, IN_COLS))],
        out_specs=[pl.BlockSpec((tm, IN_COLS), lambda i: (i, 0)), pl.BlockSpec((tm, D_MODEL), lambda i: (i, 0))],
        out_shape=[_big((T,IN_COLS), F32), _big((T,D_MODEL), BF16)],
        compiler_params=_cp(("arbitrary",), 48),
    )(_hbm(x), modr, g_pre, _hbm(w_in_g), b_in)


def _mixers_fwd(p, wdw, vecs, lbl, gbufs):
    T = p.shape[0]
    tm = _tok_tile(T)
    nt = T // tm
    nch = tm // CHUNK
    ng = len(gbufs)

    def body(*refs):
        p_ref, wdw_ref, bdw_ref, gain_ref, bias_ref, gout_ref, lbl_ref = refs[:7]
        cat_ref, ys_ref, o_ref, st_ref = refs[7 + ng:11 + ng]
        gout_bufs = refs[11 + ng:11 + 2 * ng]
        ubuf, state, qt_s, kt_s, kh_s, v_s, egl_s, gssem, grsem = refs[11 + 2 * ng:]
        i = pl.program_id(0)

        @pl.when(i == 0)
        def _():
            _gather_start(gout_bufs, gssem, grsem)
            state[...] = jnp.zeros(state.shape, F32)
            ubuf[0:HALO, :] = jnp.zeros((HALO, CONV_CH), F32)

        @pl.when(i > 0)
        def _():
            ubuf[0:HALO, :] = ubuf[tm:tm + HALO, :]

        ubuf[HALO:HALO + tm, :] = p_ref[:, 0:CONV_CH] * _sig(p_ref[:, CONV_CH:2 * CONV_CH])
        for r in range(tm // CONV_ROWS):
            base = r * CONV_ROWS + HALO - (CONV_K - 1)
            acc = jnp.broadcast_to(bdw_ref[...], (CONV_ROWS, CONV_CH))
            for j in range(CONV_K):
                acc = acc + wdw_ref[j:j + 1, :] * ubuf[base + j:base + j + CONV_ROWS, :]
            ys_ref[r * CONV_ROWS:(r + 1) * CONV_ROWS, :] = acc
        gmat = _gn_matrix()
        yv = ys_ref[...]
        d = yv - _gmean(yv, gmat)
        rs = lax.rsqrt(_gmean(d * d, gmat) + GN_EPS)
        z = d * rs * gain_ref[...] + bias_ref[...]
        cat_ref[:, 0:CONV_CH] = (z * _sig(z)).astype(BF16)

        lb, _ = _lower_bound(lbl_ref)
        lower, _, same = _chunk_masks(tm)
        o0 = 2 * CONV_CH
        pr = _hgrn_prep(p_ref[:, o0:o0 + HGRN_W], p_ref[:, o0 + HGRN_W:o0 + 2 * HGRN_W], lb, lower, same)
        qt_s[...] = pr["qt"].astype(BF16)
        kt_s[...] = pr["kt"].astype(BF16)
        kh_s[...] = pr["kh"].astype(BF16)
        v_s[...] = p_ref[:, o0 + 2 * HGRN_W:o0 + 3 * HGRN_W].astype(BF16)
        egl_s[...] = jnp.exp(pr["Gl"])
        tri = _tri()

        def chunk(ci, carry):
            r0 = pl.multiple_of(ci * CHUNK, CHUNK)
            rows = pl.ds(r0, CHUNK)
            for h in range(N_HEADS):
                ls = pl.ds(h * HEAD_D, HEAD_D)
                qc, kc, hc, vc = qt_s[rows, ls], kt_s[rows, ls], kh_s[rows, ls], v_s[rows, ls]
                s0 = state[h]
                s0b = s0.astype(BF16)
                st_ref[ci, h] = s0
                att =jnp.where(tri, _dot_nt(qc, kc), 0.0).astype(BF16)
                o_ref[rows, ls] = _dot(att, vc) + _dot_nt(qc, s0b)
                state[h] = s0 * egl_s[pl.ds(r0, 1), ls] + _dot_tn(vc, hc)
            return carry

        lax.fori_loop(0, nch, chunk, 0)
        for h in range(N_HEADS):
            sl = slice(h * HEAD_D, (h + 1) * HEAD_D)
            oh = o_ref[:, sl]
            gh = p_ref[:, o0 + 3 * HGRN_W + h * HEAD_D:o0 + 3 * HGRN_W + (h + 1) * HEAD_D]
            rsh = lax.rsqrt(jnp.mean(oh * oh, axis=-1, keepdims=True) + RMS_EPS)
            hg = (oh * rsh) * gout_ref[:, sl] * (gh * _sig(gh))
            cat_ref[:, CONV_CH + h * HEAD_D:CONV_CH + (h + 1) * HEAD_D] = hg.astype(BF16)

        @pl.when(i == nt - 1)
        def _():
            _gather_finish(gout_bufs, gssem, grsem)

    tile = lambda cols: pl.BlockSpec((tm, cols), lambda i: (i, 0))
    hbm = pl.BlockSpec(memory_space=pl.ANY)
    n_in = 7
    outs = pl.pallas_call(
        body, name="mixers_fwd", grid=(nt,),
        in_specs=[tile(IN_COLS), _full((HALO, CONV_CH))] + [_full((1, CONV_CH))] * 4 + [_full((2, HGRN_W))]
        + [hbm] * ng,
        out_specs=[tile(D_MODEL), tile(CONV_CH), tile(HGRN_W),
                   pl.BlockSpec((nch, N_HEADS, HEAD_D, HEAD_D), lambda i: (i, 0, 0, 0))] + [hbm] * ng,
        out_shape=[_big((T, D_MODEL), BF16), _big((T, CONV_CH), F32), _big((T, HGRN_W), F32),
                   _big((T // CHUNK, N_HEADS, HEAD_D, HEAD_D), F32)] + [_big(b.shape, BF16) for b in gbufs],
        input_output_aliases={n_in + t: 4 + t for t in range(ng)},
        scratch_shapes=[pltpu.VMEM((tm + HALO, CONV_CH), F32), pltpu.VMEM((N_HEADS, HEAD_D, HEAD_D), F32),
                        pltpu.VMEM((tm, HGRN_W), BF16), pltpu.VMEM((tm, HGRN_W), BF16),
                        pltpu.VMEM((tm, HGRN_W), BF16), pltpu.VMEM((tm, HGRN_W), BF16),
                        pltpu.VMEM((tm, HGRN_W), F32)] + _gather_sems(ng),
        compiler_params=_cp(("arbitrary",), 56),
    )(_hbm(p), wdw, *vecs, lbl, *[_hbm(b) for b in gbufs])
    return outs[:4], outs[4:]


def _mix_out_fwd(cat, w_out, x, modr, g_post, g_ffn):
    T = x.shape[0]
    tm = _tok_tile(T)

    def body(cat_ref, w_ref, x_ref, mod_ref, gp_ref, gf_ref, y_ref, x1_ref, h2_ref):
        yv = _dot(cat_ref[...], w_ref[...])
        y_ref[...] = yv
        rs = lax.rsqrt(jnp.mean(yv * yv, axis=-1, keepdims=True) + RMS_EPS)
        x1 = x_ref[...] + mod_ref[2:3, :] * ((yv * rs) * gp_ref[...])
        x1_ref[...] = x1
        rs1 = lax.rsqrt(jnp.mean(x1 * x1, axis=-1, keepdims=True) + RMS_EPS)
        h2 = (x1 * rs1) * gf_ref[...] * (1.0 + mod_ref[4:5, :]) + mod_ref[3:4, :]
        h2_ref[...] = h2.astype(BF16)

    tile = pl.BlockSpec((tm, D_MODEL), lambda i: (i, 0))
    return pl.pallas_call(
        body, name="mix_out_fwd", grid=(T // tm,),
        in_specs=[tile, _full((D_MODEL, D_MODEL)), tile, _full((6, D_MODEL)), _full((1, D_MODEL)),
                  _full((1, D_MODEL))],
        out_specs=[tile, tile, tile],
        out_shape=[_big((T,D_MODEL), F32), _big((T,D_MODEL), F32),
                   _big((T,D_MODEL), BF16)],
        compiler_params=_cp(("arbitrary",), 40),
    )(*_hbm(cat, w_out, x), modr, g_post, g_ffn)


def _ffn_fwd(h2, w_up_g, w_down, x1, target, modr, g_post):
    T = h2.shape[0]
    tm = _tok_tile(T)
    fb = D_FF // N_CHIPS

    def body(h_ref, wu_ref, wd_ref, x1_ref, t_ref, mod_ref, g_ref, a_ref, y2_ref, dx2_ref, ls_ref, acc):
        i, j = pl.program_id(0), pl.program_id(1)
        a = _dot(h_ref[...], wu_ref[0])
        a_ref[...] = a
        ra = jnp.maximum(a, 0.0)
        part = _dot((ra * ra).astype(BF16), wd_ref[...])

        @pl.when(j == 0)
        def _():
            acc[...] = part

        @pl.when(j > 0)
        def _():
            acc[...] += part

        @pl.when((i == 0) & (j == 0))
        def _():
            ls_ref[...] = jnp.zeros(ls_ref.shape, F32)

        @pl.when(j == N_CHIPS - 1)
        def _():
            y2 = acc[...]
            y2_ref[...] = y2
            rs = lax.rsqrt(jnp.mean(y2 * y2, axis=-1, keepdims=True) + RMS_EPS)
            x2 = x1_ref[...] + mod_ref[5:6, :] * ((y2 * rs) * g_ref[...])
            err = x2 - t_ref[...]
            dx2_ref[...] = err * (1.0 / D_MODEL)
            ls_ref[0:1, :] += _colsum(err * err)

    tile = pl.BlockSpec((tm, D_MODEL), lambda i, j: (i, 0))
    return pl.pallas_call(
        body, name="ffn_fwd", grid=(T // tm, N_CHIPS),
        in_specs=[tile, pl.BlockSpec((1, D_MODEL, fb), lambda i, j: (j, 0, 0)),
                  pl.BlockSpec((fb, D_MODEL), lambda i, j: (j, 0)), tile, tile,
                  _full((6, D_MODEL)), _full((1, D_MODEL))],
        out_specs=[pl.BlockSpec((tm, fb), lambda i, j: (i, j)), tile, tile, _full((8, D_MODEL))],
        out_shape=[_big((T,D_FF), F32), _big((T,D_MODEL), F32),
                   _big((T,D_MODEL), F32), jax.ShapeDtypeStruct((8, D_MODEL), F32)],
        scratch_shapes=[pltpu.VMEM((tm, D_MODEL), F32)],
        compiler_params=_cp(("arbitrary", "arbitrary"), 48),
    )(*_hbm(h2, w_up_g, w_down, x1, target), modr, g_post)


def _rms_bwd(dxn, xn, rs):
    return rs * (dxn - xn * jnp.mean(dxn * xn, axis=-1, keepdims=True))


def _ffn_bwd(dx2, y2, a, x1, w_up_g, w_down, modr, g_post, g_ffn):
    T = dx2.shape[0]
    tm = _tok_tile(T)
    fb = D_FF // N_CHIPS

    def body(dx2_ref, y2_ref, a_ref, x1_ref, wu_ref, wd_ref, mod_ref, gp_ref, gf_ref,
             da_ref, r_ref, dy2_ref, dx1_ref, st_ref, dy2_s, dh_s):
        i, j = pl.program_id(0), pl.program_id(1)

        @pl.when((i == 0) & (j == 0))
        def _():
            st_ref[...] = jnp.zeros(st_ref.shape, F32)

        @pl.when(j == 0)
        def _():
            dx2v, y2v = dx2_ref[...], y2_ref[...]
            rs = lax.rsqrt(jnp.mean(y2v * y2v, axis=-1, keepdims=True) + RMS_EPS)
            nh = y2v * rs
            st_ref[0:1, :] += _colsum(dx2v * (nh * gp_ref[...]))
            dn = dx2v * mod_ref[5:6, :]
            st_ref[1:2, :] += _colsum(dn * nh)
            dy2 = _rms_bwd(dn * gp_ref[...], nh, rs).astype(BF16)
            dy2_s[...] = dy2
            dy2_ref[...] = dy2

        av = a_ref[...]
        ra = jnp.maximum(av, 0.0)
        r_ref[...] = (ra * ra).astype(BF16)
        da = (_dot_nt(dy2_s[...], wd_ref[...]) * (2.0 * ra)).astype(BF16)
        da_ref[...] = da
        part = _dot_nt(da, wu_ref[0])

        @pl.when(j == 0)
        def _():
            dh_s[...] = part

        @pl.when(j > 0)
        def _():
            dh_s[...] += part

        @pl.when(j == N_CHIPS - 1)
        def _():
            dh = dh_s[...]
            x1v = x1_ref[...]
            rs1 = lax.rsqrt(jnp.mean(x1v * x1v, axis=-1, keepdims=True) + RMS_EPS)
            xn = x1v * rs1
            st_ref[2:3, :] += _colsum(dh)
            st_ref[3:4, :] += _colsum(dh * (xn * gf_ref[...]))
            dsc = dh * (1.0 + mod_ref[4:5, :])
            st_ref[4:5, :] += _colsum(dsc * xn)
            dx1_ref[...] = dx2_ref[...] + _rms_bwd(dsc * gf_ref[...], xn, rs1)

    tile = pl.BlockSpec((tm, D_MODEL), lambda i, j: (i, 0))
    ftile = pl.BlockSpec((tm, fb), lambda i, j: (i, j))
    return pl.pallas_call(
        body, name="ffn_bwd", grid=(T // tm, N_CHIPS),
        in_specs=[tile, tile, ftile, tile, pl.BlockSpec((1, D_MODEL, fb), lambda i, j: (j, 0, 0)),
                  pl.BlockSpec((fb, D_MODEL), lambda i, j: (j, 0)), _full((6, D_MODEL)), _full((1, D_MODEL)),
                  _full((1, D_MODEL))],
        out_specs=[ftile, ftile, tile, tile, _full((8, D_MODEL))],
        out_shape=[_big((T,D_FF), BF16), _big((T,D_FF), BF16),
                   _big((T,D_MODEL), BF16), _big((T,D_MODEL), F32),
                   jax.ShapeDtypeStruct((8, D_MODEL), F32)],
        scratch_shapes=[pltpu.VMEM((tm, D_MODEL), BF16), pltpu.VMEM((tm, D_MODEL), F32)],
        compiler_params=_cp(("arbitrary", "arbitrary"), 48),
    )(*_hbm(dx2, y2, a, x1, w_up_g, w_down), modr, g_post, g_ffn)


def _mix_out_bwd(dx1, y, w_out, modr, g_post):
    T = dx1.shape[0]
    tm = _tok_tile(T)

    def body(dx1_ref, y_ref, w_ref, mod_ref, gp_ref, dy_ref, dcat_ref, st_ref):
        @pl.when(pl.program_id(0) == 0)
        def _():
            st_ref[...] = jnp.zeros(st_ref.shape, F32)

        dxv, yv = dx1_ref[...], y_ref[...]
        rs = lax.rsqrt(jnp.mean(yv * yv, axis=-1, keepdims=True) + RMS_EPS)
        nh = yv * rs
        st_ref[0:1, :] += _colsum(dxv * (nh * gp_ref[...]))
        dn = dxv * mod_ref[2:3, :]
        st_ref[1:2, :] += _colsum(dn * nh)
        dy = _rms_bwd(dn * gp_ref[...], nh, rs).astype(BF16)
        dy_ref[...] = dy
        dcat_ref[...] = _dot_nt(dy, w_ref[...])

    tile = pl.BlockSpec((tm, D_MODEL), lambda i: (i, 0))
    return pl.pallas_call(
        body, name="mix_out_bwd", grid=(T // tm,),
        in_specs=[tile, tile, _full((D_MODEL, D_MODEL)), _full((6, D_MODEL)), _full((1, D_MODEL))],
        out_specs=[tile, tile, _full((8, D_MODEL))],
        out_shape=[_big((T,D_MODEL), BF16), _big((T,D_MODEL), F32),
                   jax.ShapeDtypeStruct((8, D_MODEL), F32)],
        compiler_params=_cp(("arbitrary",), 40),
    )(*_hbm(dx1, y, w_out), modr, g_post)


def _mixers_bwd(p, dcat, ys, o, states, wdw, vecs, lbl, pairs_b):
    T = p.shape[0]
    tm = _tok_tile(T)
    nt = T // tm
    nch = tm // CHUNK
    hpt = tm // HALO
    nx = len(pairs_b)

    def body(*refs):
        (p_ref, ph_ref, dcat_ref, ys_ref, o_ref, st_ref, wdw_ref, bdw_ref, gain_ref, bias_ref, gout_ref,
         lbl_ref) = refs[:12]
        x_ins = refs[12:12 + nx]
        dp_ref, sb_ref, s5_ref, dw_ref = refs[12 + nx:16 + nx]
        x_outs = refs[16 + nx:16 + 2 * nx]
        (ubuf, dybuf, carry, dstate, qt_s, kt_s, kh_s, v_s, do_s, egl_s, dqt_s, dkt_s, dkh_s, dv_s, dgl_s,
         xssem, xrsem) = refs[16 + 2 * nx:]
        i = pl.program_id(0)
        tile_idx = nt - 1 - i

        @pl.when(i == 0)
        def _():
            for cp in _xchg_copies(x_ins, x_outs, xssem, xrsem):
                cp.start()
            dstate[...] = jnp.zeros(dstate.shape, F32)
            carry[...] = jnp.zeros(carry.shape, F32)
            sb_ref[...] = jnp.zeros(sb_ref.shape, F32)
            s5_ref[...] = jnp.zeros(s5_ref.shape, F32)
            dw_ref[...] = jnp.zeros(dw_ref.shape, F32)

        uh = ph_ref[:, 0:CONV_CH] * _sig(ph_ref[:, CONV_CH:2 * CONV_CH])
        ubuf[0:HALO, :] = jnp.where(tile_idx > 0, uh, 0.0)
        ubuf[HALO:HALO + tm, :] = p_ref[:, 0:CONV_CH] * _sig(p_ref[:, CONV_CH:2 * CONV_CH])
        gmat = _gn_matrix()
        gain = gain_ref[...]
        yv = ys_ref[...]
        d = yv - _gmean(yv, gmat)
        rs = lax.rsqrt(_gmean(d * d, gmat) + GN_EPS)
        yn = d * rs
        z = yn * gain + bias_ref[...]
        sz = _sig(z)
        dz = dcat_ref[:, 0:CONV_CH] * (sz * (1.0 + z * (1.0 - sz)))
        dyn = dz * gain
        dyc = rs * (dyn - _gmean(dyn, gmat) - yn * _gmean(dyn * yn, gmat))
        s5_ref[0:1, :] += _colsum(dyc)
        s5_ref[1:2, :] += _colsum(dz * yn)
        s5_ref[2:3, :] += _colsum(dz)
        dybuf[tm:tm + HALO, :] = carry[...]
        dybuf[0:tm, :] = dyc
        carry[...] = dyc[0:HALO, :]
        for j in range(CONV_K):
            off = HALO - (CONV_K - 1) + j
            dw_ref[j:j + 1, :] += _colsum(dyc * ubuf[off:off + tm, :])
        for r in range(tm // CONV_ROWS):
            rows = slice(r * CONV_ROWS, (r + 1) * CONV_ROWS)
            acc = jnp.zeros((CONV_ROWS, CONV_CH), F32)
            for j in range(CONV_K):
                off = r * CONV_ROWS + (CONV_K - 1) - j
                acc = acc + wdw_ref[j:j + 1, :] * dybuf[off:off + CONV_ROWS, :]
            val = p_ref[rows, 0:CONV_CH]
            sg = _sig(p_ref[rows, CONV_CH:2 * CONV_CH])
            dval = acc * sg
            dgate = acc * val * (sg * (1.0 - sg))
            dp_ref[rows, 0:CONV_CH] = dval.astype(BF16)
            dp_ref[rows, CONV_CH:2 * CONV_CH] = dgate.astype(BF16)
            sb_ref[0:1, 0:CONV_CH] += _colsum(dval)
            sb_ref[0:1, CONV_CH:2 * CONV_CH] += _colsum(dgate)

        o0 = 2 * CONV_CH
        for h in range(N_HEADS):
            sl = slice(h * HEAD_D, (h + 1) * HEAD_D)
            gsl = slice(o0 + 3 * HGRN_W + h * HEAD_D, o0 + 3 * HGRN_W + (h + 1) * HEAD_D)
            oh = o_ref[:, sl]
            gh = p_ref[:, gsl]
            dh = dcat_ref[:, CONV_CH + h * HEAD_D:CONV_CH + (h + 1) * HEAD_D]
            gout = gout_ref[:, sl]
            rsh = lax.rsqrt(jnp.mean(oh * oh, axis=-1, keepdims=True) + RMS_EPS)
            on = oh * rsh
            sgg = _sig(gh)
            dgh = dh * (on * gout) * (sgg * (1.0 + gh * (1.0 - sgg)))
            dm = dh * (gh * sgg)
            s5_ref[3:4, sl] += _colsum(dm * on)
            do_s[:, sl] = _rms_bwd(dm * gout, on, rsh).astype(BF16)
            dp_ref[:, gsl] = dgh.astype(BF16)
            sb_ref[2:3, CONV_CH + h * HEAD_D:CONV_CH + (h + 1) * HEAD_D] += _colsum(dgh)

        lb, _ = _lower_bound(lbl_ref)
        lower, upper, same = _chunk_masks(tm)
        pq = p_ref[:, o0:o0 + HGRN_W]
        pr = _hgrn_prep(pq, p_ref[:, o0 + HGRN_W:o0 + 2 * HGRN_W], lb, lower, same)
        qt_s[...] = pr["qt"].astype(BF16)
        kt_s[...] = pr["kt"].astype(BF16)
        kh_s[...] = pr["kh"].astype(BF16)
        v_s[...] = p_ref[:, o0 + 2 * HGRN_W:o0 + 3 * HGRN_W].astype(BF16)
        egl_s[...] = jnp.exp(pr["Gl"])
        tri = _tri()

        def chunk(it, c_):
            ci = nch - 1 - it
            r0 = pl.multiple_of(ci * CHUNK, CHUNK)
            rows = pl.ds(r0, CHUNK)
            for h in range(N_HEADS):
                ls = pl.ds(h * HEAD_D, HEAD_D)
                qc, kc, hc, vc = qt_s[rows, ls], kt_s[rows, ls], kh_s[rows, ls], v_s[rows, ls]
                dob = do_s[rows, ls]
                s0 = st_ref[ci, h]
                s0b = s0.astype(BF16)
                ds1 = dstate[h]
                ds1b = ds1.astype(BF16)
                egl = egl_s[pl.ds(r0, 1), ls]
                att = jnp.where(tri, _dot_nt(qc, kc), 0.0).astype(BF16)
                datt = jnp.where(tri, _dot_nt(dob, vc), 0.0).astype(BF16)
                dv_s[rows, ls] = _dot_tn(att, dob) + _dot_nt(hc, ds1b)
                dqt_s[rows, ls] = _dot(datt, kc) + _dot(dob, s0b)
                dkt_s[rows, ls] = _dot_tn(datt, qc)
                dkh_s[rows, ls] = _dot(vc, ds1b)
                dgl = egl * _colsum(ds1 * s0)
                dgl_s[rows, ls] = jnp.broadcast_to(dgl, (CHUNK, HEAD_D))
                dstate[h] = ds1 * egl + _dot_tn(dob, qc)
            return c_

        lax.fori_loop(0, nch, chunk, 0)
        dqt, dkt, dkh = dqt_s[...], dkt_s[...], dkh_s[...]
        dk = dkt * pr["enG"] + dkh * pr["eGlG"]
        khk = dkh * kh_s[...].astype(F32)
        dG = dqt * qt_s[...].astype(F32) - dkt * kt_s[...].astype(F32) - khk
        dlogf = _mm3(upper, dG) + _mm3(same, khk) + dgl_s[...]
        df = dlogf / pr["f"] - dk
        sf, sq = pr["sf"], pr["sq"]
        s5_ref[4:5, :] += _colsum(df * (1.0 - sf))
        dfl = df * (1.0 - lb) * (sf * (1.0 - sf))
        dq = (dqt * pr["eG"]) * (sq * (1.0 + pq * (1.0 - sq)))
        dvv = dv_s[...]
        dp_ref[:, o0:o0 + HGRN_W] = dq.astype(BF16)
        dp_ref[:, o0 + HGRN_W:o0 + 2 * HGRN_W] = dfl.astype(BF16)
        dp_ref[:, o0 + 2 * HGRN_W:o0 + 3 * HGRN_W] = dvv.astype(BF16)
        sb_ref[1:2, 0:HGRN_W] += _colsum(dq)
        sb_ref[1:2, HGRN_W:2 * HGRN_W] += _colsum(dfl)
        sb_ref[2:3, 0:HGRN_W] += _colsum(dvv)

        @pl.when(i == nt - 1)
        def _():
            copies = _xchg_copies(x_ins, x_outs, xssem, xrsem)
            for cp in copies:
                cp.wait_recv()
            for cp in copies:
                cp.wait_send()

    rev = lambda cols: pl.BlockSpec((tm, cols), lambda i: (nt - 1 - i, 0))
    halo = pl.BlockSpec((HALO, 2 * CONV_CH), lambda i: (jnp.maximum((nt - 1 - i) * hpt - 1, 0), 0))
    wide = lambda n: pltpu.VMEM((tm, HGRN_W), n)
    hbm = pl.BlockSpec(memory_space=pl.ANY)
    outs = pl.pallas_call(
        body, name="mixers_bwd", grid=(nt,),
        in_specs=[rev(IN_COLS), halo, rev(D_MODEL), rev(CONV_CH), rev(HGRN_W),
                  pl.BlockSpec((nch, N_HEADS, HEAD_D, HEAD_D), lambda i: (nt - 1 - i, 0, 0, 0)),
                  _full((HALO, CONV_CH))] + [_full((1, CONV_CH))] * 4 + [_full((2, HGRN_W))] + [hbm] * nx,
        out_specs=[rev(IN_COLS), _full((8, D_MODEL)), _full((8, CONV_CH)), _full((HALO, CONV_CH))] + [hbm] * nx,
        out_shape=[_big((T, IN_COLS), BF16), jax.ShapeDtypeStruct((8, D_MODEL), F32),
                   jax.ShapeDtypeStruct((8, CONV_CH), F32), jax.ShapeDtypeStruct((HALO, CONV_CH), F32)]
        + [_big(pb.shape, BF16) for pb in pairs_b],
        scratch_shapes=[pltpu.VMEM((tm + HALO, CONV_CH), F32), pltpu.VMEM((tm + HALO, CONV_CH), F32),
                        pltpu.VMEM((HALO, CONV_CH), F32), pltpu.VMEM((N_HEADS, HEAD_D, HEAD_D), F32),
                        wide(BF16), wide(BF16), wide(BF16), wide(BF16), wide(BF16),
                        wide(F32), wide(F32), wide(F32), wide(F32), wide(F32), wide(F32)] + _xchg_sems(nx),
        compiler_params=_cp(("arbitrary",), 56),
    )(*_hbm(p, p, dcat, ys, o, states), wdw, *vecs, lbl, *[_hbm(pb) for pb in pairs_b])
    return outs[:4], outs[4:]


def _mix_in_bwd(dp, w_in_g, x, dx1, modr, g_pre):
    T = x.shape[0]
    tm = _tok_tile(T)
    nb = IN_COLS // N_CHIPS

    def body(dp_ref, w_ref, x_ref, dx1_ref, mod_ref, g_ref, gx_ref, st_ref):
        @pl.when(pl.program_id(0) == 0)
        def _():
            st_ref[...] = jnp.zeros(st_ref.shape, F32)

        dh = None
        for j in range(N_CHIPS):
            part = _dot_nt(dp_ref[:, j * nb:(j + 1) * nb], w_ref[j])
            dh = part if dh is None else dh + part
        xv = x_ref[...]
        rs = lax.rsqrt(jnp.mean(xv * xv, axis=-1, keepdims=True) + RMS_EPS)
        xn = xv * rs
        st_ref[0:1, :] += _colsum(dh)
        st_ref[1:2, :] += _colsum(dh * (xn * g_ref[...]))
        dsc = dh * (1.0 + mod_ref[1:2, :])
        st_ref[2:3, :] += _colsum(dsc * xn)
        gx_ref[...] = dx1_ref[...] + _rms_bwd(dsc * g_ref[...], xn, rs)

    tile = pl.BlockSpec((tm, D_MODEL), lambda i: (i, 0))
    return pl.pallas_call(
        body, name="mix_in_bwd", grid=(T // tm,),
        in_specs=[pl.BlockSpec((tm, IN_COLS), lambda i: (i, 0)), _full((N_CHIPS, D_MODEL, nb)), tile, tile,
                  _full((6, D_MODEL)), _full((1, D_MODEL))],
        out_specs=[tile, _full((8, D_MODEL))],
        out_shape=[_big((T,D_MODEL), F32), jax.ShapeDtypeStruct((8, D_MODEL), F32)],
        compiler_params=_cp(("arbitrary",), 48),
    )(*_hbm(dp, w_in_g, x, dx1), modr, g_pre)


def _weight_grad(a, b, a_blocked, b_blocked, name):
    T = a.shape[0]
    tt = _tok_tile(T)
    ka = a.shape[1] // N_CHIPS if a_blocked else a.shape[1]
    nb = b.shape[1] // N_CHIPS if b_blocked else b.shape[1]

    def body(a_ref, b_ref, o_ref):
        prod = _dot_tn(a_ref[...], b_ref[...])

        @pl.when(pl.program_id(1) == 0)
        def _():
            o_ref[0] = prod

        @pl.when(pl.program_id(1) > 0)
        def _():
            o_ref[0] += prod

    return pl.pallas_call(
        body, name=name, grid=(N_CHIPS, T // tt),
        in_specs=[pl.BlockSpec((tt, ka), (lambda j, t: (t, j)) if a_blocked else (lambda j, t: (t, 0))),
                  pl.BlockSpec((tt, nb), (lambda j, t: (t, j)) if b_blocked else (lambda j, t: (t, 0)))],
        out_specs=pl.BlockSpec((1, ka, nb), lambda j, t: (j, 0, 0)),
        out_shape=_big((N_CHIPS, ka, nb), F32),
        compiler_params=_cp(("arbitrary", "arbitrary"), 40),
    )(*_hbm(a, b))


R_LOSS = 0
R_FFN = 8
R_OUT = 16
R_IN = 24
R_BIN = 32
R_512 = 40
R_DW = 48
N_STAT_ROWS = 80
MOD_ROWS = (R_IN + 0, R_IN + 1, R_OUT + 0, R_FFN + 2, R_FFN + 3, R_FFN + 0)


def _small_update(gath, params):
    names = ["b_ada", "lb_logits", "g_pre_mix", "b_in", "b_dw", "gn_gain", "gn_bias", "g_hgrn_out", "g_post_mix",
             "g_pre_ffn", "g_post_ffn"]
    flat = []
    for n in names:
        flat += list(params[n])
    n_in = 1 + len(flat)

    def body(*refs):
        g_ref = refs[0]
        prm = {n: refs[1 + 3 * k:4 + 3 * k] for k, n in enumerate(names)}
        outs = refs[n_in:]
        loss_ref, dmod_ref, dwdw_ref = outs[0], outs[1], outs[2]
        res = {n: outs[3 + 4 * k:7 + 4 * k] for k, n in enumerate(names)}
        red = g_ref[0]
        for dev in range(1, N_DEV):
            red = red + g_ref[dev]
        loss_ref[...] = jnp.broadcast_to(
            (0.5 / D_MODEL) * jnp.sum(red[R_LOSS:R_LOSS + 1, :], axis=-1, keepdims=True), loss_ref.shape)
        for dev in range(N_DEV):
            for k, r in enumerate(MOD_ROWS):
                dmod_ref[dev:dev + 1, k * D_MODEL:(k + 1) * D_MODEL] = g_ref[dev, r:r + 1, :]
        dwdw_ref[...] = red[R_DW:R_DW + HALO, 0:CONV_CH]

        def finish(name, pieces):
            w_ref, m_ref, v_ref = prm[name]
            g_out, d_out, m_out, v_out = res[name]
            for rsl, lsl, g in pieces:
                d, m2, v2 = _adam_math(w_ref[rsl, lsl], g, m_ref[rsl, lsl], v_ref[rsl, lsl])
                g_out[rsl, lsl] = g
                d_out[rsl, lsl] = d
                m_out[rsl, lsl] = m2
                v_out[rsl, lsl] = v2

        one = slice(0, 1)
        row = lambda r: red[r:r + 1, :]
        half = lambda r: red[r:r + 1, 0:CONV_CH]
        finish("b_ada", [(one, slice(k * D_MODEL, (k + 1) * D_MODEL), row(r)) for k, r in enumerate(MOD_ROWS)])
        finish("b_in", [(one, slice(k * D_MODEL, (k + 1) * D_MODEL), row(R_BIN + k)) for k in range(3)])
        finish("g_pre_mix", [(one, slice(None), row(R_IN + 2))])
        finish("g_post_mix", [(one, slice(None), row(R_OUT + 1))])
        finish("g_pre_ffn", [(one, slice(None), row(R_FFN + 4))])
        finish("g_post_ffn", [(one, slice(None), row(R_FFN + 1))])
        finish("b_dw", [(one, slice(None), half(R_512 + 0))])
        finish("gn_gain", [(one, slice(None), half(R_512 + 1))])
        finish("gn_bias", [(one, slice(None), half(R_512 + 2))])
        finish("g_hgrn_out", [(one, slice(None), half(R_512 + 3))])
        s0, s1 = _lower_bound(prm["lb_logits"][0])
        dlb = half(R_512 + 4)
        finish("lb_logits", [(slice(0, 1), slice(None), dlb * s0 * (1.0 - s0)),
                             (slice(1, 2), slice(None), -dlb * s0 * s1)])

    vm = pl.BlockSpec(memory_space=pltpu.VMEM)
    out_shape = [jax.ShapeDtypeStruct((8, 128), F32), jax.ShapeDtypeStruct((N_DEV, 6 * D_MODEL), F32),
                 jax.ShapeDtypeStruct((HALO, CONV_CH), F32)]
    for n in names:
        out_shape += [jax.ShapeDtypeStruct(params[n][0].shape, F32)] * 4
    outs = pl.pallas_call(
        body, name="small_update", out_shape=out_shape,
        in_specs=[vm] * n_in, out_specs=[vm] * len(out_shape),
        compiler_params=_cp(None, 32),
    )(gath, *flat)
    return outs[0], outs[1], outs[2], {n: outs[3 + 4 * k:7 + 4 * k] for k, n in enumerate(names)}


def _wdw_adam(w, g, m, v):
    def body(w_ref, g_ref, m_ref, v_ref, d_out, m_out, v_out):
        d, m2, v2 = _adam_math(w_ref[...], g_ref[...], m_ref[...], v_ref[...])
        d_out[...] = d
        m_out[...] = m2
        v_out[...] = v2

    vm = pl.BlockSpec(memory_space=pltpu.VMEM)
    return pl.pallas_call(
        body, name="wdw_adam", out_shape=[jax.ShapeDtypeStruct(w.shape, F32)] * 3,
        in_specs=[vm] * 4, out_specs=[vm] * 3, compiler_params=_cp(None, 16),
    )(w, g, m, v)


def kernel(x, c, w_ada, b_ada, lb_logits, g_pre_mix, w_in, b_in, w_dw, b_dw, gn_gain, gn_bias, g_hgrn_out, w_out, g_post_mix, g_pre_ffn, w_up, w_down, g_post_ffn, loss_target, m_w_ada, m_b_ada, m_lb_logits, m_g_pre_mix, m_w_in, m_b_in, m_w_dw, m_b_dw, m_gn_gain, m_gn_bias, m_g_hgrn_out, m_w_out, m_g_post_mix, m_g_pre_ffn, m_w_up, m_w_down, m_g_post_ffn, v_w_ada, v_b_ada, v_lb_logits, v_g_pre_mix, v_w_in, v_b_in, v_w_dw, v_b_dw, v_gn_gain, v_gn_bias, v_g_hgrn_out, v_w_out, v_g_post_mix, v_g_pre_ffn, v_w_up, v_w_down, v_g_post_ffn):
    ax, ay, ac = lax.axis_index("x"), lax.axis_index("y"), lax.axis_index("c")
    chip = 2 * ax + ay
    T = x.shape[1]
    xs, tgt = x[0], loss_target[0]
    ada_cols = w_ada.shape[2]

    b_sh = lax.dynamic_slice_in_dim(b_ada, chip * ada_cols, ada_cols, axis=1)
    wdw_pad = jnp.pad(w_dw[0], ((0, HALO - CONV_K), (0, 0)))
    _, c8, modg, wdwg = _ada_exchange(c, w_ada[0], b_sh, wdw_pad)
    modr = modg.reshape(6, D_MODEL)
    wdw_all = jnp.transpose(wdwg, (1, 0, 2)).reshape(HALO, CONV_CH)
    chip1 = jnp.reshape(chip, (1,)).astype(jnp.int32)
    place = jnp.stack([ac, chip]).astype(jnp.int32)
    (w_in_g,) = _weight_gather([_cast_own(chip1, w_in[0], "cast_w_in")], "gather_w_in")
    late_bufs = [_cast_own(chip1, w[0], "cast_" + t) for w, t in ((w_out, "w_out"), (w_up, "w_up"), (w_down, "w_down"))]
    vec = (b_dw, gn_gain, gn_bias, g_hgrn_out)

    p, h1 = _mix_in_fwd(xs, modr, g_pre_mix, w_in_g, b_in)
    (cat, ys, o, states), (w_out_g, w_up_g, w_down_g) = _mixers_fwd(p, wdw_all, vec, lb_logits, late_bufs)
    w_out_f = w_out_g.reshape(D_MODEL, D_MODEL)
    w_down_f = w_down_g.reshape(D_FF, D_MODEL)
    y, x1, h2 = _mix_out_fwd(cat, w_out_f, xs, modr, g_post_mix, g_pre_ffn)
    a, y2, dx2, st_loss = _ffn_fwd(h2, w_up_g, w_down_f, x1, tgt, modr, g_post_ffn)

    def pair_stage(grads, tags, name):
        got = _pair_swap(grads, name)
        return [_pair_sum(place, g, o_, "pair_sum_" + t) for g, o_, t in zip(grads, got, tags)]

    da, r, dy2, dx1, st_ffn = _ffn_bwd(dx2, y2, a, x1, w_up_g, w_down_f, modr, g_post_ffn, g_pre_ffn)
    g_up = _weight_grad(h2, da, False, True, "grad_w_up")
    g_down = _weight_grad(r, dy2, True, False, "grad_w_down")
    dy, dcat, st_out = _mix_out_bwd(dx1, y, w_out_f, modr, g_post_mix)
    g_out = _weight_grad(cat, dy, True, False, "grad_w_out")
    early = pair_stage([g_out, g_up, g_down], ["w_out", "w_up", "w_down"], "pair_swap_early")
    (dp, st_bin, st_512, dwdw), got_early = _mixers_bwd(p, dcat, ys, o, states, wdw_all, vec, lb_logits,
                                                        [pb for _, pb in early])
    g_in = _weight_grad(h1, dp, False, True, "grad_w_in")
    grad_x, st_in = _mix_in_bwd(dp, w_in_g, xs, dx1, modr, g_pre_mix)
    late = pair_stage([g_in], ["w_in"], "pair_swap_late")
    got_late = _chip_exchange([late[0][1]], "chip_exchange_w_in")
    fulls = [_chip_sum(place, pf, gb, "chip_sum_" + t)
             for (pf, _), gb, t in zip(late + early, list(got_late) + list(got_early), ["w_in", "w_out", "w_up", "w_down"])]
    g_w_in, g_w_out, g_w_up, g_w_down = _half_swap(fulls)

    pad_lanes = lambda s: jnp.pad(s, ((0, 0), (0, D_MODEL - s.shape[1])))
    stats = jnp.concatenate([st_loss, st_ffn, st_out, st_in, st_bin, pad_lanes(st_512), pad_lanes(dwdw)], axis=0)
    gath = _stats_gather(stats)
    small = {"b_ada": (b_ada, m_b_ada, v_b_ada), "lb_logits": (lb_logits, m_lb_logits, v_lb_logits),
             "g_pre_mix": (g_pre_mix, m_g_pre_mix, v_g_pre_mix), "b_in": (b_in, m_b_in, v_b_in),
             "b_dw": (b_dw, m_b_dw, v_b_dw), "gn_gain": (gn_gain, m_gn_gain, v_gn_gain),
             "gn_bias": (gn_bias, m_gn_bias, v_gn_bias), "g_hgrn_out": (g_hgrn_out, m_g_hgrn_out, v_g_hgrn_out),
             "g_post_mix": (g_post_mix, m_g_post_mix, v_g_post_mix), "g_pre_ffn": (g_pre_ffn, m_g_pre_ffn, v_g_pre_ffn),
             "g_post_ffn": (g_post_ffn, m_g_post_ffn, v_g_post_ffn)}
    loss_t, dmod_all, dwdw_sum, sres = _small_update(gath, small)
    loss = loss_t[0, 0]

    res = dict(sres)
    dmod_sh = lax.dynamic_slice_in_dim(dmod_all, chip * ada_cols, ada_cols, axis=1)
    res["w_ada"] = [t[None] for t in _ada_grad_adam(jnp.transpose(c8), dmod_sh, w_ada[0], m_w_ada[0], v_w_ada[0])]
    g_wdw = lax.dynamic_slice_in_dim(dwdw_sum, chip * HEAD_D, HEAD_D, axis=1)[:CONV_K][None]
    res["w_dw"] = [g_wdw] + list(_wdw_adam(w_dw, g_wdw, m_w_dw, v_w_dw))
    for name, g, w, m, v in (("w_in", g_w_in, w_in, m_w_in, v_w_in), ("w_out", g_w_out, w_out, m_w_out, v_w_out),
                             ("w_up", g_w_up, w_up, m_w_up, v_w_up), ("w_down", g_w_down, w_down, m_w_down, v_w_down)):
        d, m2, v2 = _adam_big(w[0], g, m[0], v[0], "adam_" + name)
        res[name] = [g[None], d[None], m2[None], v2[None]]

    order = ["w_ada", "b_ada", "lb_logits", "g_pre_mix", "w_in", "b_in", "w_dw", "b_dw", "gn_gain", "gn_bias",
             "g_hgrn_out", "w_out", "g_post_mix", "g_pre_ffn", "w_up", "w_down", "g_post_ffn"]
    out = [loss, grad_x[None]]
    for k in range(4):
        out += [res[n][k] for n in order]
    return tuple(out)
```

```python
import jax
import jax.numpy as jnp
from jax import lax
from jax.experimental import pallas as pl
from jax.experimental.pallas import tpu as pltpu

F32, BF16 = jnp.float32, jnp.bfloat16
D_MODEL = 1024
CONV_CH = 512
HGRN_W = 512
N_HEADS = 4
HEAD_D = 128
CONV_K = 31
GN_GROUP = 64
GN_SHIFT = 6
IN_COLS = 3072
D_FF = 4096
CHUNK = 64
CHUNK_SHIFT = 6
N_CHIPS = 4
N_DEV = 8
RMS_EPS = 1e-6
GN_EPS = 1e-5
ADAM_LR, ADAM_B1, ADAM_B2, ADAM_EPS, ADAM_WD, ADAM_STEP = 0.001, 0.9, 0.999, 1e-08, 0.01, 10
TOK_TILE = 512
MIXB_TILE = 256
FFN_TILE = 1024
FFN_BLOCK = 512
GRAD_TILE = 2048
HALO = 32
SUB = 8
LANE = 128
CONV_ROWS = 128
MIB = 1 << 20
MESH = pl.DeviceIdType.MESH
OTHER_CHIPS = ((0, 1), (1, 0), (1, 1))


def _cp(sem=None, vmem_mib=48):
    return pltpu.CompilerParams(dimension_semantics=sem, vmem_limit_bytes=vmem_mib * MIB)


def _dot(a, b):
    return jnp.dot(a, b, preferred_element_type=F32)


def _dot_nt(a, b):
    return lax.dot_general(a, b, (((1,), (1,)), ((), ())), preferred_element_type=F32)


def _dot_tn(a, b):
    return lax.dot_general(a, b, (((0,), (0,)), ((), ())), preferred_element_type=F32)


def _sig(v):
    return jax.nn.sigmoid(v)


def _colsum(v):
    return jnp.sum(v, axis=0, keepdims=True)


def _flip(v, b):
    return 1 - v if b else v


def _rcopy(src, dst, ssem, rsem, dev):
    return pltpu.make_async_remote_copy(src_ref=src, dst_ref=dst, send_sem=ssem, recv_sem=rsem,
                                        device_id=dev, device_id_type=MESH)


def _place():
    return lax.axis_index("x"), lax.axis_index("y"), lax.axis_index("c")


def _full(shape):
    return pl.BlockSpec(shape, lambda *_: (0,) * len(shape))


def _big(shape, dtype):
    return pltpu.HBM(shape, dtype)


def _hbm(*arrays):
    out = [pltpu.with_memory_space_constraint(a, pltpu.HBM) for a in arrays]
    return out[0] if len(out) == 1 else out


def _split2(v):
    hi = v.astype(BF16)
    lo = (v - hi.astype(F32)).astype(BF16)
    return hi, lo


def _split3(v):
    h1 = v.astype(BF16)
    r1 = v - h1.astype(F32)
    h2 = r1.astype(BF16)
    h3 = (r1 - h2.astype(F32)).astype(BF16)
    return h1, h2, h3


def _mm3s(mat, parts):
    h1, h2, h3 = parts
    return _dot(mat, h1) + _dot(mat, h2) + _dot(mat, h3)


def _mm3(mat, v):
    return _mm3s(mat, _split3(v))


def _gn_matrix():
    r = lax.broadcasted_iota(jnp.int32, (CONV_CH, CONV_CH), 0) >> GN_SHIFT
    c = lax.broadcasted_iota(jnp.int32, (CONV_CH, CONV_CH), 1) >> GN_SHIFT
    return jnp.where(r == c, 1.0 / GN_GROUP, 0.0).astype(BF16)


def _gmean(v, gmat):
    hi, lo = _split2(v)
    return _dot(hi, gmat) + _dot(lo, gmat)


def _chunk_masks(tm):
    r = lax.broadcasted_iota(jnp.int32, (tm, tm), 0)
    c = lax.broadcasted_iota(jnp.int32, (tm, tm), 1)
    same = (r >> CHUNK_SHIFT) == (c >> CHUNK_SHIFT)
    one = lambda m: jnp.where(m, 1.0, 0.0).astype(BF16)
    return one(same & (c <= r)), one(same & (c >= r)), one(same)


def _tri():
    return lax.broadcasted_iota(jnp.int32, (CHUNK, CHUNK), 0) >= lax.broadcasted_iota(jnp.int32, (CHUNK, CHUNK), 1)


def _lower_bound(lbl_ref):
    l0, l1 = lbl_ref[0:1, :], lbl_ref[1:2, :]
    mx = jnp.maximum(l0, l1)
    e0, e1 = jnp.exp(l0 - mx), jnp.exp(l1 - mx)
    return e0 / (e0 + e1), e1 / (e0 + e1)


CONV_FWD_TAPS = tuple((j, HALO - (CONV_K - 1) + j) for j in range(CONV_K))
CONV_BWD_TAPS = tuple((j, (CONV_K - 1) - j) for j in range(CONV_K))


def _tap_conv(src_ref, w_ref, row0, taps, lanes):
    acc = None
    for b in range(SUB):
        pb = None
        for j, off in taps:
            if off % SUB == b:
                lo = row0 + off - b
                term = w_ref[j:j + 1, lanes] * src_ref[lo:lo + CONV_ROWS + SUB, lanes]
                pb = term if pb is None else pb + term
        if pb is not None:
            sh = pb[b:b + CONV_ROWS, :]
            acc = sh if acc is None else acc + sh
    return acc


def _hgrn_prep(pq, pf, lb, lower, same):
    sq = _sig(pq)
    qf = pq * sq
    sf = _sig(pf)
    f = lb + (1.0 - lb) * sf
    logf = jnp.log(f)
    k = 1.0 - f
    parts = _split3(logf)
    G = _mm3s(lower, parts)
    Gl = _mm3s(same, parts)
    eG, enG, eGlG = jnp.exp(G), jnp.exp(-G), jnp.exp(Gl - G)
    return dict(sq=sq, sf=sf, f=f, Gl=Gl, eG=eG, enG=enG, eGlG=eGlG, qt=qf * eG, kt=k * enG, kh=k * eGlG)


def _ada_exchange(c_row, w_ada, b_sh, wdw_pad):
    ncol = w_ada.shape[1]

    def body(c_ref, w_ref, b_ref, wdw_ref, call_ref, c8_ref, modg_ref, wdwg_ref, rows_s, sa, ra, sw, rw, sm, rm):
        x, y, c = _place()
        me = 4 * x + 2 * y + c
        chip = 2 * x + y
        cv = c_ref[...]
        call_ref[me] = cv * _sig(cv)
        wdwg_ref[chip] = wdw_ref[...]
        sends = []
        for m in range(1, N_DEV):
            peer = (_flip(x, m >> 2), _flip(y, (m >> 1) & 1), _flip(c, m & 1))
            cp = _rcopy(call_ref.at[me], call_ref.at[me], sa.at[m - 1], ra.at[m - 1], peer)
            cp.start()
            sends.append(cp)
        for k, (fx, fy) in enumerate(OTHER_CHIPS):
            peer = (_flip(x, fx), _flip(y, fy), c)
            cp = _rcopy(wdwg_ref.at[chip], wdwg_ref.at[chip], sw.at[k], rw.at[k], peer)
            cp.start()
            sends.append(cp)
        for m in range(1, N_DEV):
            peer = (_flip(x, m >> 2), _flip(y, (m >> 1) & 1), _flip(c, m & 1))
            pid = 4 * peer[0] + 2 * peer[1] + peer[2]
            _rcopy(call_ref.at[pid], call_ref.at[pid], sa.at[m - 1], ra.at[m - 1], peer).wait_recv()
        for b in range(N_DEV):
            c8_ref[b:b + 1, :] = call_ref[b]
        mod_all = _dot(c8_ref[...].astype(BF16), w_ref[...].astype(BF16)) + b_ref[...]
        for b in range(N_DEV):
            rows_s[b] = mod_all[b:b + 1, :]
        modg_ref[chip] = rows_s[me]
        for k, (fx, fy) in enumerate(OTHER_CHIPS):
            peer = (_flip(x, fx), _flip(y, fy), c)
            pid = 4 * peer[0] + 2 * peer[1] + peer[2]
            cp = _rcopy(rows_s.at[pid], modg_ref.at[chip], sm.at[k], rm.at[k], peer)
            cp.start()
            sends.append(cp)
        for k, (fx, fy) in enumerate(OTHER_CHIPS):
            peer = (_flip(x, fx), _flip(y, fy), c)
            pchip = 2 * peer[0] + peer[1]
            _rcopy(rows_s.at[0], modg_ref.at[pchip], sm.at[k], rm.at[k], peer).wait_recv()
            _rcopy(wdwg_ref.at[pchip], wdwg_ref.at[pchip], sw.at[k], rw.at[k], peer).wait_recv()
        for cp in sends:
            cp.wait_send()

    vm = pl.BlockSpec(memory_space=pltpu.VMEM)
    return pl.pallas_call(
        body, name="ada_exchange",
        out_shape=[jax.ShapeDtypeStruct((N_DEV, 1, D_MODEL), F32), jax.ShapeDtypeStruct((N_DEV, D_MODEL), F32),
                   jax.ShapeDtypeStruct((N_CHIPS, 1, ncol), F32), jax.ShapeDtypeStruct((N_CHIPS, HALO, HEAD_D), F32)],
        in_specs=[vm] * 4, out_specs=[vm] * 4,
        scratch_shapes=[pltpu.VMEM((N_DEV, 1, ncol), F32),
                        pltpu.SemaphoreType.DMA((N_DEV - 1,)), pltpu.SemaphoreType.DMA((N_DEV - 1,)),
                        pltpu.SemaphoreType.DMA((3,)), pltpu.SemaphoreType.DMA((3,)),
                        pltpu.SemaphoreType.DMA((3,)), pltpu.SemaphoreType.DMA((3,))],
        compiler_params=_cp(None, 32),
    )(c_row, w_ada, b_sh, wdw_pad)


def _cast_own(chip1, shard, name):
    rows, cols = shard.shape
    tr = _row_tile(rows)

    def body(ch_ref, s_ref, o_ref):
        o_ref[0] = s_ref[...].astype(BF16)

    gs = pltpu.PrefetchScalarGridSpec(
        num_scalar_prefetch=1, grid=(rows // tr,),
        in_specs=[pl.BlockSpec((tr, cols), lambda i, ch: (i, 0))],
        out_specs=pl.BlockSpec((1, tr, cols), lambda i, ch: (ch[0], i, 0)))
    return pl.pallas_call(
        body, name=name, grid_spec=gs, out_shape=_big((N_CHIPS, rows, cols), BF16),
        compiler_params=_cp(("arbitrary",), 32),
    )(chip1, _hbm(shard))


def _slab(buf, ch, core):
    hs = buf.shape[1] // 2
    return buf.at[ch, pl.ds(core * hs, hs), :]


def _gather_start(bufs, ssem, rsem):
    x, y, c = _place()
    chip = 2 * x + y
    for k, (fx, fy) in enumerate(OTHER_CHIPS):
        peer = (_flip(x, fx), _flip(y, fy), c)
        for t, buf in enumerate(bufs):
            _rcopy(_slab(buf, chip, c), _slab(buf, chip, c), ssem.at[t * 3 + k], rsem.at[t * 3 + k], peer).start()


def _gather_finish(bufs, ssem, rsem):
    nt = len(bufs)
    x, y, c = _place()
    chip = 2 * x + y
    sibling = (x, y, 1 - c)
    for k, (fx, fy) in enumerate(OTHER_CHIPS):
        peer = (_flip(x, fx), _flip(y, fy), c)
        pchip = 2 * peer[0] + peer[1]
        for t, buf in enumerate(bufs):
            _rcopy(_slab(buf, pchip, c), _slab(buf, pchip, c), ssem.at[t * 3 + k], rsem.at[t * 3 + k], peer).wait_recv()
            _rcopy(_slab(buf, pchip, c), _slab(buf, pchip, c), ssem.at[3 * nt + t * 3 + k],
                   rsem.at[3 * nt + t * 3 + k], sibling).start()
    for k, (fx, fy) in enumerate(OTHER_CHIPS):
        peer = (_flip(x, fx), _flip(y, fy), c)
        pchip = 2 * peer[0] + peer[1]
        for t, buf in enumerate(bufs):
            _rcopy(_slab(buf, pchip, 1 - c), _slab(buf, pchip, 1 - c), ssem.at[3 * nt + t * 3 + k],
                   rsem.at[3 * nt + t * 3 + k], sibling).wait_recv()
            _rcopy(_slab(buf, chip, c), _slab(buf, chip, c), ssem.at[t * 3 + k], rsem.at[t * 3 + k], peer).wait_send()
            _rcopy(_slab(buf, pchip, c), _slab(buf, pchip, c), ssem.at[3 * nt + t * 3 + k],
                   rsem.at[3 * nt + t * 3 + k], sibling).wait_send()


def _gather_sems(nt):
    return [pltpu.SemaphoreType.DMA((6 * nt,)), pltpu.SemaphoreType.DMA((6 * nt,))]


def _weight_gather(bufs, name):
    nt = len(bufs)
    hbm = pl.BlockSpec(memory_space=pl.ANY)

    def body(*refs):
        outs, ssem, rsem = refs[nt:2 * nt], refs[2 * nt], refs[2 * nt + 1]
        _gather_start(outs, ssem, rsem)
        _gather_finish(outs, ssem, rsem)

    return pl.pallas_call(
        body, name=name, out_shape=[_big(b.shape, BF16) for b in bufs],
        in_specs=[hbm] * nt, out_specs=[hbm] * nt, input_output_aliases={t: t for t in range(nt)},
        scratch_shapes=_gather_sems(nt),
    )(*[_hbm(b) for b in bufs])


def _stats_gather(stats):
    rows, cols = stats.shape

    def body(s_ref, out_ref, ssem, rsem):
        x, y, c = _place()
        me, sibling = (x, y, c), (x, y, 1 - c)
        chips = [(_flip(x, fx), _flip(y, fy)) for fx, fy in OTHER_CHIPS]

        def blk(px, py, pc):
            return out_ref.at[4 * px + 2 * py + pc]

        def copy(k, block, to, src=None):
            return _rcopy(blk(*block) if src is None else src, blk(*block), ssem.at[k], rsem.at[k], to)

        out_ref[4 * x + 2 * y + c] = s_ref[...]
        first = [copy(0, me, sibling, src=s_ref)]
        first += [copy(1 + j, me, (*chip, c), src=s_ref) for j, chip in enumerate(chips)]
        for cp in first:
            cp.start()
        passed = [copy(4 + j, (*chip, c), sibling) for j, chip in enumerate(chips)]
        for j, chip in enumerate(chips):
            copy(1 + j, (*chip, c), me).wait_recv()
            passed[j].start()
        copy(0, sibling, me).wait_recv()
        for j, chip in enumerate(chips):
            copy(4 + j, (*chip, 1 - c), me).wait_recv()
        for cp in first + passed:
            cp.wait_send()

    vm = pl.BlockSpec(memory_space=pltpu.VMEM)
    return pl.pallas_call(
        body, name="stats_gather",
        out_shape=jax.ShapeDtypeStruct((N_DEV, rows, cols), F32),
        in_specs=[vm], out_specs=vm,
        scratch_shapes=[pltpu.SemaphoreType.DMA((7,)), pltpu.SemaphoreType.DMA((7,))],
        compiler_params=_cp(None, 32),
    )(stats)


def _pair_swap(grads, name):
    nt = len(grads)
    hbm = pl.BlockSpec(memory_space=pl.ANY)

    def body(*refs):
        ins, outs = refs[:nt], refs[nt:2 * nt]
        ssem, rsem = refs[2 * nt], refs[2 * nt + 1]
        x, y, c = _place()
        sibling = (x, y, 1 - c)
        copies = []
        for t in range(nt):
            hs = ins[t].shape[1] // 2
            cp = _rcopy(ins[t].at[:, pl.ds((1 - c) * hs, hs), :], outs[t], ssem.at[t], rsem.at[t], sibling)
            cp.start()
            copies.append(cp)
        for cp in copies:
            cp.wait_recv()
        for cp in copies:
            cp.wait_send()

    return pl.pallas_call(
        body, name=name,
        out_shape=[jax.ShapeDtypeStruct((g.shape[0], g.shape[1] // 2, g.shape[2]), F32) for g in grads],
        in_specs=[hbm] * nt, out_specs=[hbm] * nt,
        scratch_shapes=[pltpu.SemaphoreType.DMA((nt,)), pltpu.SemaphoreType.DMA((nt,))],
    )(*[_hbm(g) for g in grads])


def _xchg_copies(ins, outs, ssem, rsem):
    x, y, c = _place()
    copies = []
    for k, (fx, fy) in enumerate(OTHER_CHIPS):
        peer = (_flip(x, fx), _flip(y, fy), c)
        for t in range(len(ins)):
            copies.append(_rcopy(ins[t].at[k], outs[t].at[k], ssem.at[t * 3 + k], rsem.at[t * 3 + k], peer))
    return copies


def _xchg_sems(nt):
    return [pltpu.SemaphoreType.DMA((3 * nt,)), pltpu.SemaphoreType.DMA((3 * nt,))]


def _chip_exchange(pairs_b, name):
    nt = len(pairs_b)
    hbm = pl.BlockSpec(memory_space=pl.ANY)

    def body(*refs):
        copies = _xchg_copies(refs[:nt], refs[nt:2 * nt], refs[2 * nt], refs[2 * nt + 1])
        for cp in copies:
            cp.start()
        for cp in copies:
            cp.wait_recv()
        for cp in copies:
            cp.wait_send()

    return pl.pallas_call(
        body, name=name, out_shape=[_big(p.shape, BF16) for p in pairs_b],
        in_specs=[hbm] * nt, out_specs=[hbm] * nt, scratch_shapes=_xchg_sems(nt),
    )(*[_hbm(p) for p in pairs_b])


def _half_swap(fulls):
    nt = len(fulls)
    hbm = pl.BlockSpec(memory_space=pl.ANY)

    def body(*refs):
        ins, outs, ssem, rsem = refs[:nt], refs[nt:2 * nt], refs[2 * nt], refs[2 * nt + 1]
        x, y, c = _place()
        sibling = (x, y, 1 - c)
        copies = []
        for t in range(nt):
            hs = ins[t].shape[0] // 2
            mine = pl.ds(c * hs, hs)
            cp = _rcopy(ins[t].at[mine, :], outs[t].at[mine, :], ssem.at[t], rsem.at[t], sibling)
            cp.start()
            copies.append(cp)
        for t in range(nt):
            hs = ins[t].shape[0] // 2
            other = pl.ds((1 - c) * hs, hs)
            _rcopy(ins[t].at[other, :], outs[t].at[other, :], ssem.at[t], rsem.at[t], sibling).wait_recv()
        for cp in copies:
            cp.wait_send()

    return pl.pallas_call(
        body, name="grad_half_swap", out_shape=[_big(f.shape, F32) for f in fulls],
        in_specs=[hbm] * nt, out_specs=[hbm] * nt, input_output_aliases={t: t for t in range(nt)},
        scratch_shapes=[pltpu.SemaphoreType.DMA((nt,)), pltpu.SemaphoreType.DMA((nt,))],
    )(*[_hbm(f) for f in fulls])


def _row_tile(rows):
    return min(rows, 256)


def _pair_sum(place, grad, got, name):
    nb, hs, cols = got.shape
    tr = _row_tile(hs)
    nr = hs // tr

    def body(pl_ref, g_ref, o_ref, pf_ref, pb_ref):
        j = pl.program_id(1)
        s = g_ref[0] + o_ref[0]

        @pl.when(j == 0)
        def _():
            pf_ref[...] = s

        @pl.when(j > 0)
        def _():
            pb_ref[0] = s.astype(BF16)

    gs = pltpu.PrefetchScalarGridSpec(
        num_scalar_prefetch=1, grid=(nr, nb),
        in_specs=[pl.BlockSpec((1, tr, cols), lambda i, j, p: (p[1] ^ j, p[0] * nr + i, 0)),
                  pl.BlockSpec((1, tr, cols), lambda i, j, p: (p[1] ^ j, i, 0))],
        out_specs=[pl.BlockSpec((tr, cols), lambda i, j, p: (i, 0)),
                   pl.BlockSpec((1, tr, cols), lambda i, j, p: (jnp.maximum(j - 1, 0), i, 0))])
    return pl.pallas_call(
        body, name=name, grid_spec=gs,
        out_shape=[_big((hs, cols), F32), _big((nb - 1, hs, cols), BF16)],
        compiler_params=_cp(("arbitrary", "arbitrary"), 32),
    )(place, *_hbm(grad, got))


def _chip_sum(place, pair_f, got_b, name):
    nb, hs, cols = got_b.shape
    tr = _row_tile(hs)
    nr = hs // tr

    def body(pl_ref, pf_ref, gb_ref, o_ref):
        acc = pf_ref[...]
        for k in range(nb):
            acc = acc + gb_ref[k].astype(F32)
        o_ref[...] = acc

    gs = pltpu.PrefetchScalarGridSpec(
        num_scalar_prefetch=1, grid=(nr,),
        in_specs=[pl.BlockSpec((tr, cols), lambda i, p: (i, 0)),
                  pl.BlockSpec((nb, tr, cols), lambda i, p: (0, i, 0))],
        out_specs=pl.BlockSpec((tr, cols), lambda i, p: (p[0] * nr + i, 0)))
    return pl.pallas_call(
        body, name=name, grid_spec=gs,
        out_shape=_big((2 * hs, cols), F32),
        compiler_params=_cp(("arbitrary",), 32),
    )(place, *_hbm(pair_f, got_b))


def _adam_math(w, g, m, v):
    m2 = ADAM_B1 * m + (1.0 - ADAM_B1) * g
    v2 = ADAM_B2 * v + (1.0 - ADAM_B2) * (g * g)
    m_hat = m2 / (1.0 - ADAM_B1 ** ADAM_STEP)
    v_hat = v2 / (1.0 - ADAM_B2 ** ADAM_STEP)
    delta = -ADAM_LR * (m_hat / (jnp.sqrt(v_hat) + ADAM_EPS) + ADAM_WD * w)
    return delta, m2, v2


def _adam_big(w, g, m, v, name):
    rows, cols = w.shape
    tr = _row_tile(rows)

    def body(w_ref, g_ref, m_ref, v_ref, d_out, m_out, v_out):
        d, m2, v2 = _adam_math(w_ref[...], g_ref[...], m_ref[...], v_ref[...])
        d_out[...] = d
        m_out[...] = m2
        v_out[...] = v2

    spec = pl.BlockSpec((tr, cols), lambda i: (i, 0))
    return pl.pallas_call(
        body, name=name, grid=(rows // tr,), in_specs=[spec] * 4, out_specs=[spec] * 3,
        out_shape=[_big(w.shape, F32)] * 3,
        compiler_params=_cp(("arbitrary",), 32),
    )(*_hbm(w, g, m, v))


def _ada_grad_adam(c8t, dmod_sh, w, m, v):
    rows, cols = w.shape
    tr = _row_tile(rows)

    def body(ct_ref, dm_ref, w_ref, m_ref, v_ref, g_out, d_out, m_out, v_out):
        g = None
        for b in range(N_DEV):
            term = ct_ref[:, b:b + 1] * dm_ref[b:b + 1, :]
            g = term if g is None else g + term
        d, m2, v2 = _adam_math(w_ref[...], g, m_ref[...], v_ref[...])
        g_out[...] = g
        d_out[...] = d
        m_out[...] = m2
        v_out[...] = v2

    spec = pl.BlockSpec((tr, cols), lambda i: (i, 0))
    return pl.pallas_call(
        body, name="ada_grad_adam", grid=(rows // tr,),
        in_specs=[pl.BlockSpec((tr, N_DEV), lambda i: (i, 0)), _full((N_DEV, cols)), spec, spec, spec],
        out_specs=[spec] * 4, out_shape=[_big(w.shape, F32)] * 4,
        compiler_params=_cp(("arbitrary",), 32),
    )(c8t, dmod_sh, *_hbm(w, m, v))


def _tok_tile(t):
    return min(TOK_TILE, t)


def _mix_in_fwd(x, modr, g_pre, w_in_g, b_in):
    T = x.shape[0]
    tm = _tok_tile(T)
    nb = IN_COLS // N_CHIPS

    def body(x_ref, mod_ref, g_ref, w_ref, b_ref, p_ref, h_ref):
        xv = x_ref[...]
        rstd = lax.rsqrt(jnp.mean(xv * xv, axis=-1, keepdims=True) + RMS_EPS)
        h = (xv * rstd) * g_ref[...] * (1.0 + mod_ref[1:2, :]) + mod_ref[0:1, :]
        hb = h.astype(BF16)
        h_ref[...] = hb
        for j in range(N_CHIPS):
            p_ref[:, j * nb:(j + 1) * nb] = _dot(hb, w_ref[j]) + b_ref[:, j * nb:(j + 1) * nb]

    return pl.pallas_call(
        body, name="mix_in_fwd", grid=(T // tm,),
        in_specs=[pl.BlockSpec((tm, D_MODEL), lambda i: (i, 0)), _full((6, D_MODEL)), _full((1, D_MODEL)),
                  _full((N_CHIPS, D_MODEL, nb)), _full((1, IN_COLS))],
        out_specs=[pl.BlockSpec((tm, IN_COLS), lambda i: (i, 0)), pl.BlockSpec((tm, D_MODEL), lambda i: (i, 0))],
        out_shape=[_big((T,IN_COLS), F32), _big((T,D_MODEL), BF16)],
        compiler_params=_cp(("arbitrary",), 48),
    )(_hbm(x), modr, g_pre, _hbm(w_in_g), b_in)


def _mixers_fwd(p, wdw, vecs, lbl, gbufs):
    T = p.shape[0]
    tm = _tok_tile(T)
    nt = T // tm
    nch = tm // CHUNK
    ng = len(gbufs)

    def body(*refs):
        p_ref, wdw_ref, bdw_ref, gain_ref, bias_ref, gout_ref, lbl_ref = refs[:7]
        cat_ref, ys_ref, o_ref, st_ref = refs[7 + ng:11 + ng]
        gout_bufs = refs[11 + ng:11 + 2 * ng]
        ubuf, state, qt_s, kt_s, kh_s, v_s, egl_s, lower_s, same_s, gmat_s, gssem, grsem = refs[11 + 2 * ng:]
        i = pl.program_id(0)

        @pl.when(i == 0)
        def _():
            _gather_start(gout_bufs, gssem, grsem)
            lower_s[...], _, same_s[...] = _chunk_masks(tm)
            gmat_s[...] = _gn_matrix()
            state[...] = jnp.zeros(state.shape, F32)
            ubuf[0:HALO, :] = jnp.zeros((HALO, CONV_CH), F32)
            ubuf[HALO + tm:HALO + tm + SUB, :] = jnp.zeros((SUB, CONV_CH), F32)

        @pl.when(i > 0)
        def _():
            ubuf[0:HALO, :] = ubuf[tm:tm + HALO, :]

        ubuf[HALO:HALO + tm, :] = p_ref[:, 0:CONV_CH] * _sig(p_ref[:, CONV_CH:2 * CONV_CH])
        for r in range(tm // CONV_ROWS):
            rows = slice(r * CONV_ROWS, (r + 1) * CONV_ROWS)
            for lb_ in range(CONV_CH // LANE):
                lanes = slice(lb_ * LANE, (lb_ + 1) * LANE)
                ys_ref[rows, lanes] = bdw_ref[:, lanes] + _tap_conv(ubuf, wdw_ref, r * CONV_ROWS, CONV_FWD_TAPS, lanes)
        gmat = gmat_s[...]
        yv = ys_ref[...]
        d = yv - _gmean(yv, gmat)
        rs = lax.rsqrt(_gmean(d * d, gmat) + GN_EPS)
        z = d * rs * gain_ref[...] + bias_ref[...]
        cat_ref[:, 0:CONV_CH] = (z * _sig(z)).astype(BF16)

        lb, _ = _lower_bound(lbl_ref)
        lower, same = lower_s[...], same_s[...]
        o0 = 2 * CONV_CH
        pr = _hgrn_prep(p_ref[:, o0:o0 + HGRN_W], p_ref[:, o0 + HGRN_W:o0 + 2 * HGRN_W], lb, lower, same)
        qt_s[...] = pr["qt"].astype(BF16)
        kt_s[...] = pr["kt"].astype(BF16)
        kh_s[...] = pr["kh"].astype(BF16)
        v_s[...] = p_ref[:, o0 + 2 * HGRN_W:o0 + 3 * HGRN_W].astype(BF16)
        egl_s[...] = jnp.exp(pr["Gl"])
        tri = _tri()

        def chunk(ci, carry):
            r0 = pl.multiple_of(ci * CHUNK, CHUNK)
            rows = pl.ds(r0, CHUNK)
            for h in range(N_HEADS):
                ls = pl.ds(h * HEAD_D, HEAD_D)
                qc, kc, hc, vc = qt_s[rows, ls], kt_s[rows, ls], kh_s[rows, ls], v_s[rows, ls]
                s0 = state[h]
                s0b = s0.astype(BF16)
                st_ref[ci, h] = s0
                att =jnp.where(tri, _dot_nt(qc, kc), 0.0).astype(BF16)
                o_ref[rows, ls] = _dot(att, vc) + _dot_nt(qc, s0b)
                state[h] = s0 * egl_s[pl.ds(r0, 1), ls] + _dot_tn(vc, hc)
            return carry

        lax.fori_loop(0, nch, chunk, 0)
        for h in range(N_HEADS):
            sl = slice(h * HEAD_D, (h + 1) * HEAD_D)
            oh = o_ref[:, sl]
            gh = p_ref[:, o0 + 3 * HGRN_W + h * HEAD_D:o0 + 3 * HGRN_W + (h + 1) * HEAD_D]
            rsh = lax.rsqrt(jnp.mean(oh * oh, axis=-1, keepdims=True) + RMS_EPS)
            hg = (oh * rsh) * gout_ref[:, sl] * (gh * _sig(gh))
            cat_ref[:, CONV_CH + h * HEAD_D:CONV_CH + (h + 1) * HEAD_D] = hg.astype(BF16)

        @pl.when(i == nt - 1)
        def _():
            _gather_finish(gout_bufs, gssem, grsem)

    tile = lambda cols: pl.BlockSpec((tm, cols), lambda i: (i, 0))
    hbm = pl.BlockSpec(memory_space=pl.ANY)
    n_in = 7
    outs = pl.pallas_call(
        body, name="mixers_fwd", grid=(nt,),
        in_specs=[tile(IN_COLS), _full((HALO, CONV_CH))] + [_full((1, CONV_CH))] * 4 + [_full((2, HGRN_W))]
        + [hbm] * ng,
        out_specs=[tile(D_MODEL), tile(CONV_CH), tile(HGRN_W),
                   pl.BlockSpec((nch, N_HEADS, HEAD_D, HEAD_D), lambda i: (i, 0, 0, 0))] + [hbm] * ng,
        out_shape=[_big((T, D_MODEL), BF16), _big((T, CONV_CH), F32), _big((T, HGRN_W), F32),
                   _big((T // CHUNK, N_HEADS, HEAD_D, HEAD_D), F32)] + [_big(b.shape, BF16) for b in gbufs],
        input_output_aliases={n_in + t: 4 + t for t in range(ng)},
        scratch_shapes=[pltpu.VMEM((tm + HALO + SUB, CONV_CH), F32), pltpu.VMEM((N_HEADS, HEAD_D, HEAD_D), F32),
                        pltpu.VMEM((tm, HGRN_W), BF16), pltpu.VMEM((tm, HGRN_W), BF16),
                        pltpu.VMEM((tm, HGRN_W), BF16), pltpu.VMEM((tm, HGRN_W), BF16),
                        pltpu.VMEM((tm, HGRN_W), F32), pltpu.VMEM((tm, tm), BF16), pltpu.VMEM((tm, tm), BF16),
                        pltpu.VMEM((CONV_CH, CONV_CH), BF16)] + _gather_sems(ng),
        compiler_params=_cp(("arbitrary",), 56),
    )(_hbm(p), wdw, *vecs, lbl, *[_hbm(b) for b in gbufs])
    return outs[:4], outs[4:]


def _mix_out_fwd(cat, w_out, x, modr, g_post, g_ffn):
    T = x.shape[0]
    tm = _tok_tile(T)

    def body(cat_ref, w_ref, x_ref, mod_ref, gp_ref, gf_ref, y_ref, x1_ref, h2_ref):
        yv = _dot(cat_ref[...], w_ref[...])
        y_ref[...] = yv
        rs = lax.rsqrt(jnp.mean(yv * yv, axis=-1, keepdims=True) + RMS_EPS)
        x1 = x_ref[...] + mod_ref[2:3, :] * ((yv * rs) * gp_ref[...])
        x1_ref[...] = x1
        rs1 = lax.rsqrt(jnp.mean(x1 * x1, axis=-1, keepdims=True) + RMS_EPS)
        h2 = (x1 * rs1) * gf_ref[...] * (1.0 + mod_ref[4:5, :]) + mod_ref[3:4, :]
        h2_ref[...] = h2.astype(BF16)

    tile = pl.BlockSpec((tm, D_MODEL), lambda i: (i, 0))
    return pl.pallas_call(
        body, name="mix_out_fwd", grid=(T // tm,),
        in_specs=[tile, _full((D_MODEL, D_MODEL)), tile, _full((6, D_MODEL)), _full((1, D_MODEL)),
                  _full((1, D_MODEL))],
        out_specs=[tile, tile, tile],
        out_shape=[_big((T,D_MODEL), F32), _big((T,D_MODEL), F32),
                   _big((T,D_MODEL), BF16)],
        compiler_params=_cp(("arbitrary",), 40),
    )(*_hbm(cat, w_out, x), modr, g_post, g_ffn)


def _ffn_blocks():
    return D_FF // FFN_BLOCK, (D_FF // N_CHIPS) // FFN_BLOCK


def _ffn_fwd(h2, w_up_g, w_down, x1, target, modr, g_post):
    T = h2.shape[0]
    tm = min(FFN_TILE, T)
    fb = FFN_BLOCK
    nj, per = _ffn_blocks()

    def body(h_ref, wu_ref, wd_ref, x1_ref, t_ref, mod_ref, g_ref, r_ref, dy2_ref, dx2_ref, st_ref, acc):
        i, j = pl.program_id(0), pl.program_id(1)
        ra = jnp.maximum(_dot(h_ref[...], wu_ref[0]), 0.0)
        rb = (ra * ra).astype(BF16)
        r_ref[...] = rb
        part = _dot(rb, wd_ref[...])

        @pl.when(j == 0)
        def _():
            acc[...] = part

        @pl.when(j > 0)
        def _():
            acc[...] += part

        @pl.when((i == 0) & (j == 0))
        def _():
            st_ref[...] = jnp.zeros(st_ref.shape, F32)

        @pl.when(j == nj - 1)
        def _():
            y2 = acc[...]
            rs = lax.rsqrt(jnp.mean(y2 * y2, axis=-1, keepdims=True) + RMS_EPS)
            nh = y2 * rs
            gp = g_ref[...]
            err = x1_ref[...] + mod_ref[5:6, :] * (nh * gp) - t_ref[...]
            dx2 = err * (1.0 / D_MODEL)
            dx2_ref[...] = dx2
            st_ref[0:1, :] += _colsum(err * err)
            st_ref[1:2, :] += _colsum(dx2 * (nh * gp))
            dn = dx2 * mod_ref[5:6, :]
            st_ref[2:3, :] += _colsum(dn * nh)
            dy2_ref[...] = _rms_bwd(dn * gp, nh, rs).astype(BF16)

    tile = pl.BlockSpec((tm, D_MODEL), lambda i, j: (i, 0))
    return pl.pallas_call(
        body, name="ffn_fwd", grid=(T // tm, nj),
        in_specs=[tile, pl.BlockSpec((1, D_MODEL, fb), lambda i, j: (j // per, 0, j % per)),
                  pl.BlockSpec((fb, D_MODEL), lambda i, j: (j, 0)), tile, tile,
                  _full((6, D_MODEL)), _full((1, D_MODEL))],
        out_specs=[pl.BlockSpec((tm, fb), lambda i, j: (i, j)), tile, tile, _full((8, D_MODEL))],
        out_shape=[_big((T, D_FF), BF16), _big((T, D_MODEL), BF16), _big((T, D_MODEL), F32),
                   jax.ShapeDtypeStruct((8, D_MODEL), F32)],
        scratch_shapes=[pltpu.VMEM((tm, D_MODEL), F32)],
        compiler_params=_cp(("arbitrary", "arbitrary"), 56),
    )(*_hbm(h2, w_up_g, w_down, x1, target), modr, g_post)


def _rms_bwd(dxn, xn, rs):
    return rs * (dxn - xn * jnp.mean(dxn * xn, axis=-1, keepdims=True))


def _ffn_bwd(dy2, r, x1, dx2, w_up_g, w_down, modr, g_ffn):
    T = dx2.shape[0]
    tm = min(FFN_TILE, T)
    fb = FFN_BLOCK
    nj, per = _ffn_blocks()

    def body(dy2_ref, r_ref, x1_ref, dx2_ref, wu_ref, wd_ref, mod_ref, gf_ref, da_ref, dx1_ref, st_ref, dh_s):
        i, j = pl.program_id(0), pl.program_id(1)

        @pl.when((i == 0) & (j == 0))
        def _():
            st_ref[...] = jnp.zeros(st_ref.shape, F32)

        ra = jnp.sqrt(r_ref[...].astype(F32))
        da = (_dot_nt(dy2_ref[...], wd_ref[...]) * (2.0 * ra)).astype(BF16)
        da_ref[...] = da
        part = _dot_nt(da, wu_ref[0])

        @pl.when(j == 0)
        def _():
            dh_s[...] = part

        @pl.when(j > 0)
        def _():
            dh_s[...] += part

        @pl.when(j == nj - 1)
        def _():
            dh = dh_s[...]
            x1v = x1_ref[...]
            rs1 = lax.rsqrt(jnp.mean(x1v * x1v, axis=-1, keepdims=True) + RMS_EPS)
            xn = x1v * rs1
            st_ref[0:1, :] += _colsum(dh)
            st_ref[1:2, :] += _colsum(dh * (xn * gf_ref[...]))
            dsc = dh * (1.0 + mod_ref[4:5, :])
            st_ref[2:3, :] += _colsum(dsc * xn)
            dx1_ref[...] = dx2_ref[...] + _rms_bwd(dsc * gf_ref[...], xn, rs1)

    tile = pl.BlockSpec((tm, D_MODEL), lambda i, j: (i, 0))
    ftile = pl.BlockSpec((tm, fb), lambda i, j: (i, j))
    return pl.pallas_call(
        body, name="ffn_bwd", grid=(T // tm, nj),
        in_specs=[tile, ftile, tile, tile, pl.BlockSpec((1, D_MODEL, fb), lambda i, j: (j // per, 0, j % per)),
                  pl.BlockSpec((fb, D_MODEL), lambda i, j: (j, 0)), _full((6, D_MODEL)), _full((1, D_MODEL))],
        out_specs=[ftile, tile, _full((8, D_MODEL))],
        out_shape=[_big((T, D_FF), BF16), _big((T, D_MODEL), F32), jax.ShapeDtypeStruct((8, D_MODEL), F32)],
        scratch_shapes=[pltpu.VMEM((tm, D_MODEL), F32)],
        compiler_params=_cp(("arbitrary", "arbitrary"), 56),
    )(*_hbm(dy2, r, x1, dx2, w_up_g, w_down), modr, g_ffn)


def _mix_out_bwd(dx1, y, w_out, modr, g_post):
    T = dx1.shape[0]
    tm = _tok_tile(T)

    def body(dx1_ref, y_ref, w_ref, mod_ref, gp_ref, dy_ref, dcat_ref, st_ref):
        @pl.when(pl.program_id(0) == 0)
        def _():
            st_ref[...] = jnp.zeros(st_ref.shape, F32)

        dxv, yv = dx1_ref[...], y_ref[...]
        rs = lax.rsqrt(jnp.mean(yv * yv, axis=-1, keepdims=True) + RMS_EPS)
        nh = yv * rs
        st_ref[0:1, :] += _colsum(dxv * (nh * gp_ref[...]))
        dn = dxv * mod_ref[2:3, :]
        st_ref[1:2, :] += _colsum(dn * nh)
        dy = _rms_bwd(dn * gp_ref[...], nh, rs).astype(BF16)
        dy_ref[...] = dy
        dcat_ref[...] = _dot_nt(dy, w_ref[...])

    tile = pl.BlockSpec((tm, D_MODEL), lambda i: (i, 0))
    return pl.pallas_call(
        body, name="mix_out_bwd", grid=(T // tm,),
        in_specs=[tile, tile, _full((D_MODEL, D_MODEL)), _full((6, D_MODEL)), _full((1, D_MODEL))],
        out_specs=[tile, tile, _full((8, D_MODEL))],
        out_shape=[_big((T,D_MODEL), BF16), _big((T,D_MODEL), F32),
                   jax.ShapeDtypeStruct((8, D_MODEL), F32)],
        compiler_params=_cp(("arbitrary",), 40),
    )(*_hbm(dx1, y, w_out), modr, g_post)


def _mixers_bwd(p, dcat, ys, o, states, wdw, vecs, lbl, pairs_b):
    T = p.shape[0]
    tm = min(MIXB_TILE, T)
    nt = T // tm
    nch = tm // CHUNK
    hpt = tm // HALO
    nx = len(pairs_b)

    def body(*refs):
        (p_ref, ph_ref, dcat_ref, ys_ref, o_ref, st_ref, wdw_ref, bdw_ref, gain_ref, bias_ref, gout_ref,
         lbl_ref) = refs[:12]
        x_ins = refs[12:12 + nx]
        dp_ref, sb_ref, s5_ref, dw_ref = refs[12 + nx:16 + nx]
        x_outs = refs[16 + nx:16 + 2 * nx]
        (ubuf, dybuf, carry, dstate, qt_s, kt_s, kh_s, v_s, do_s, egl_s, dqt_s, dkt_s, dkh_s, dv_s, dgl_s,
         dsh, dw8, dshift, lower_s, upper_s, same_s, gmat_s, xssem, xrsem) = refs[16 + 2 * nx:]
        i = pl.program_id(0)
        tile_idx = nt - 1 - i

        @pl.when(i == 0)
        def _():
            for cp in _xchg_copies(x_ins, x_outs, xssem, xrsem):
                cp.start()
            dstate[...] = jnp.zeros(dstate.shape, F32)
            carry[...] = jnp.zeros(carry.shape, F32)
            sb_ref[...] = jnp.zeros(sb_ref.shape, F32)
            s5_ref[...] = jnp.zeros(s5_ref.shape, F32)
            dw_ref[...] = jnp.zeros(dw_ref.shape, F32)
            dw8[...] = jnp.zeros(dw8.shape, F32)
            lower_s[...], upper_s[...], same_s[...] = _chunk_masks(tm)
            gmat_s[...] = _gn_matrix()
            dsh[0:SUB, :] = jnp.zeros((SUB, CONV_CH), F32)
            dsh[SUB + tm:2 * SUB + tm, :] = jnp.zeros((SUB, CONV_CH), F32)
            ubuf[HALO + tm:HALO + tm + SUB, :] = jnp.zeros((SUB, CONV_CH), F32)

        uh = ph_ref[:, 0:CONV_CH] * _sig(ph_ref[:, CONV_CH:2 * CONV_CH])
        ubuf[0:HALO, :] = jnp.where(tile_idx > 0, uh, 0.0)
        ubuf[HALO:HALO + tm, :] = p_ref[:, 0:CONV_CH] * _sig(p_ref[:, CONV_CH:2 * CONV_CH])
        gmat = gmat_s[...]
        gain = gain_ref[...]
        yv = ys_ref[...]
        d = yv - _gmean(yv, gmat)
        rs = lax.rsqrt(_gmean(d * d, gmat) + GN_EPS)
        yn = d * rs
        z = yn * gain + bias_ref[...]
        sz = _sig(z)
        dz = dcat_ref[:, 0:CONV_CH] * (sz * (1.0 + z * (1.0 - sz)))
        dyn = dz * gain
        dyc = rs * (dyn - _gmean(dyn, gmat) - yn * _gmean(dyn * yn, gmat))
        s5_ref[0:1, :] += _colsum(dyc)
        s5_ref[1:2, :] += _colsum(dz * yn)
        s5_ref[2:3, :] += _colsum(dz)
        dybuf[tm:tm + HALO, :] = carry[...]
        dybuf[0:tm, :] = dyc
        dsh[SUB:SUB + tm, :] = dyc
        carry[...] = dyc[0:HALO, :]
        for b in range(SUB):
            dshift[...] = dsh[SUB - b:2 * SUB - b + tm, :]
            for j, off in CONV_FWD_TAPS:
                if off % SUB == b:
                    prod = dshift[...] * ubuf[off - b:off - b + tm + SUB, :]
                    dw8[j] += jnp.sum(prod.reshape((tm + SUB) // SUB, SUB, CONV_CH), axis=0)
        for r in range(tm // CONV_ROWS):
            rows = slice(r * CONV_ROWS, (r + 1) * CONV_ROWS)
            for lb_ in range(CONV_CH // LANE):
                lanes = slice(lb_ * LANE, (lb_ + 1) * LANE)
                glanes = slice(CONV_CH + lb_ * LANE, CONV_CH + (lb_ + 1) * LANE)
                acc = _tap_conv(dybuf, wdw_ref, r * CONV_ROWS, CONV_BWD_TAPS, lanes)
                val = p_ref[rows, lanes]
                sg = _sig(p_ref[rows, glanes])
                dval = acc * sg
                dgate = acc * val * (sg * (1.0 - sg))
                dp_ref[rows, lanes] = dval.astype(BF16)
                dp_ref[rows, glanes] = dgate.astype(BF16)
                sb_ref[0:1, lanes] += _colsum(dval)
                sb_ref[0:1, glanes] += _colsum(dgate)

        o0 = 2 * CONV_CH
        for h in range(N_HEADS):
            sl = slice(h * HEAD_D, (h + 1) * HEAD_D)
            gsl = slice(o0 + 3 * HGRN_W + h * HEAD_D, o0 + 3 * HGRN_W + (h + 1) * HEAD_D)
            oh = o_ref[:, sl]
            gh = p_ref[:, gsl]
            dh = dcat_ref[:, CONV_CH + h * HEAD_D:CONV_CH + (h + 1) * HEAD_D]
            gout = gout_ref[:, sl]
            rsh = lax.rsqrt(jnp.mean(oh * oh, axis=-1, keepdims=True) + RMS_EPS)
            on = oh * rsh
            sgg = _sig(gh)
            dgh = dh * (on * gout) * (sgg * (1.0 + gh * (1.0 - sgg)))
            dm = dh * (gh * sgg)
            s5_ref[3:4, sl] += _colsum(dm * on)
            do_s[:, sl] = _rms_bwd(dm * gout, on, rsh).astype(BF16)
            dp_ref[:, gsl] = dgh.astype(BF16)
            sb_ref[2:3, CONV_CH + h * HEAD_D:CONV_CH + (h + 1) * HEAD_D] += _colsum(dgh)

        lb, _ = _lower_bound(lbl_ref)
        lower, upper, same = lower_s[...], upper_s[...], same_s[...]
        pq = p_ref[:, o0:o0 + HGRN_W]
        pr = _hgrn_prep(pq, p_ref[:, o0 + HGRN_W:o0 + 2 * HGRN_W], lb, lower, same)
        qt_s[...] = pr["qt"].astype(BF16)
        kt_s[...] = pr["kt"].astype(BF16)
        kh_s[...] = pr["kh"].astype(BF16)
        v_s[...] = p_ref[:, o0 + 2 * HGRN_W:o0 + 3 * HGRN_W].astype(BF16)
        egl_s[...] = jnp.exp(pr["Gl"])
        tri = _tri()

        def chunk(it, c_):
            ci = nch - 1 - it
            r0 = pl.multiple_of(ci * CHUNK, CHUNK)
            rows = pl.ds(r0, CHUNK)
            for h in range(N_HEADS):
                ls = pl.ds(h * HEAD_D, HEAD_D)
                qc, kc, hc, vc = qt_s[rows, ls], kt_s[rows, ls], kh_s[rows, ls], v_s[rows, ls]
                dob = do_s[rows, ls]
                s0 = st_ref[ci, h]
                s0b = s0.astype(BF16)
                ds1 = dstate[h]
                ds1b = ds1.astype(BF16)
                egl = egl_s[pl.ds(r0, 1), ls]
                att = jnp.where(tri, _dot_nt(qc, kc), 0.0).astype(BF16)
                datt = jnp.where(tri, _dot_nt(dob, vc), 0.0).astype(BF16)
                dv_s[rows, ls] = _dot_tn(att, dob) + _dot_nt(hc, ds1b)
                dqt_s[rows, ls] = _dot(datt, kc) + _dot(dob, s0b)
                dkt_s[rows, ls] = _dot_tn(datt, qc)
                dkh_s[rows, ls] = _dot(vc, ds1b)
                dgl = egl * _colsum(ds1 * s0)
                dgl_s[rows, ls] = jnp.broadcast_to(dgl, (CHUNK, HEAD_D))
                dstate[h] = ds1 * egl + _dot_tn(dob, qc)
            return c_

        lax.fori_loop(0, nch, chunk, 0)
        dqt, dkt, dkh = dqt_s[...], dkt_s[...], dkh_s[...]
        dk = dkt * pr["enG"] + dkh * pr["eGlG"]
        khk = dkh * kh_s[...].astype(F32)
        dG = dqt * qt_s[...].astype(F32) - dkt * kt_s[...].astype(F32) - khk
        dlogf = _mm3(upper, dG) + _mm3(same, khk) + dgl_s[...]
        df = dlogf / pr["f"] - dk
        sf, sq = pr["sf"], pr["sq"]
        s5_ref[4:5, :] += _colsum(df * (1.0 - sf))
        dfl = df * (1.0 - lb) * (sf * (1.0 - sf))
        dq = (dqt * pr["eG"]) * (sq * (1.0 + pq * (1.0 - sq)))
        dvv = dv_s[...]
        dp_ref[:, o0:o0 + HGRN_W] = dq.astype(BF16)
        dp_ref[:, o0 + HGRN_W:o0 + 2 * HGRN_W] = dfl.astype(BF16)
        dp_ref[:, o0 + 2 * HGRN_W:o0 + 3 * HGRN_W] = dvv.astype(BF16)
        sb_ref[1:2, 0:HGRN_W] += _colsum(dq)
        sb_ref[1:2, HGRN_W:2 * HGRN_W] += _colsum(dfl)
        sb_ref[2:3, 0:HGRN_W] += _colsum(dvv)

        @pl.when(i == nt - 1)
        def _():
            for j in range(CONV_K):
                dw_ref[j:j + 1, :] = _colsum(dw8[j])
            copies = _xchg_copies(x_ins, x_outs, xssem, xrsem)
            for cp in copies:
                cp.wait_recv()
            for cp in copies:
                cp.wait_send()

    rev = lambda cols: pl.BlockSpec((tm, cols), lambda i: (nt - 1 - i, 0))
    halo = pl.BlockSpec((HALO, 2 * CONV_CH), lambda i: (jnp.maximum((nt - 1 - i) * hpt - 1, 0), 0))
    wide = lambda n: pltpu.VMEM((tm, HGRN_W), n)
    hbm = pl.BlockSpec(memory_space=pl.ANY)
    outs = pl.pallas_call(
        body, name="mixers_bwd", grid=(nt,),
        in_specs=[rev(IN_COLS), halo, rev(D_MODEL), rev(CONV_CH), rev(HGRN_W),
                  pl.BlockSpec((nch, N_HEADS, HEAD_D, HEAD_D), lambda i: (nt - 1 - i, 0, 0, 0)),
                  _full((HALO, CONV_CH))] + [_full((1, CONV_CH))] * 4 + [_full((2, HGRN_W))] + [hbm] * nx,
        out_specs=[rev(IN_COLS), _full((8, D_MODEL)), _full((8, CONV_CH)), _full((HALO, CONV_CH))] + [hbm] * nx,
        out_shape=[_big((T, IN_COLS), BF16), jax.ShapeDtypeStruct((8, D_MODEL), F32),
                   jax.ShapeDtypeStruct((8, CONV_CH), F32), jax.ShapeDtypeStruct((HALO, CONV_CH), F32)]
        + [_big(pb.shape, BF16) for pb in pairs_b],
        scratch_shapes=[pltpu.VMEM((tm + HALO + SUB, CONV_CH), F32), pltpu.VMEM((tm + HALO, CONV_CH), F32),
                        pltpu.VMEM((HALO, CONV_CH), F32), pltpu.VMEM((N_HEADS, HEAD_D, HEAD_D), F32),
                        wide(BF16), wide(BF16), wide(BF16), wide(BF16), wide(BF16),
                        wide(F32), wide(F32), wide(F32), wide(F32), wide(F32), wide(F32),
                        pltpu.VMEM((tm + 2 * SUB, CONV_CH), F32), pltpu.VMEM((HALO, SUB, CONV_CH), F32),
                        pltpu.VMEM((tm + SUB, CONV_CH), F32), pltpu.VMEM((tm, tm), BF16), pltpu.VMEM((tm, tm), BF16),
                        pltpu.VMEM((tm, tm), BF16), pltpu.VMEM((CONV_CH, CONV_CH), BF16)]
        + _xchg_sems(nx),
        compiler_params=_cp(("arbitrary",), 56),
    )(*_hbm(p, p, dcat, ys, o, states), wdw, *vecs, lbl, *[_hbm(pb) for pb in pairs_b])
    return outs[:4], outs[4:]


def _mix_in_bwd(dp, w_in_g, x, dx1, modr, g_pre):
    T = x.shape[0]
    tm = _tok_tile(T)
    nb = IN_COLS // N_CHIPS

    def body(dp_ref, w_ref, x_ref, dx1_ref, mod_ref, g_ref, gx_ref, st_ref):
        @pl.when(pl.program_id(0) == 0)
        def _():
            st_ref[...] = jnp.zeros(st_ref.shape, F32)

        dh = None
        for j in range(N_CHIPS):
            part = _dot_nt(dp_ref[:, j * nb:(j + 1) * nb], w_ref[j])
            dh = part if dh is None else dh + part
        xv = x_ref[...]
        rs = lax.rsqrt(jnp.mean(xv * xv, axis=-1, keepdims=True) + RMS_EPS)
        xn = xv * rs
        st_ref[0:1, :] += _colsum(dh)
        st_ref[1:2, :] += _colsum(dh * (xn * g_ref[...]))
        dsc = dh * (1.0 + mod_ref[1:2, :])
        st_ref[2:3, :] += _colsum(dsc * xn)
        gx_ref[...] = dx1_ref[...] + _rms_bwd(dsc * g_ref[...], xn, rs)

    tile = pl.BlockSpec((tm, D_MODEL), lambda i: (i, 0))
    return pl.pallas_call(
        body, name="mix_in_bwd", grid=(T // tm,),
        in_specs=[pl.BlockSpec((tm, IN_COLS), lambda i: (i, 0)), _full((N_CHIPS, D_MODEL, nb)), tile, tile,
                  _full((6, D_MODEL)), _full((1, D_MODEL))],
        out_specs=[tile, _full((8, D_MODEL))],
        out_shape=[_big((T,D_MODEL), F32), jax.ShapeDtypeStruct((8, D_MODEL), F32)],
        compiler_params=_cp(("arbitrary",), 48),
    )(*_hbm(dp, w_in_g, x, dx1), modr, g_pre)


def _weight_grad(a, b, a_blocked, b_blocked, name):
    T = a.shape[0]
    tt = min(GRAD_TILE, T)
    ka = a.shape[1] // N_CHIPS if a_blocked else a.shape[1]
    nb = b.shape[1] // N_CHIPS if b_blocked else b.shape[1]

    def body(a_ref, b_ref, o_ref):
        prod = _dot_tn(a_ref[...], b_ref[...])

        @pl.when(pl.program_id(1) == 0)
        def _():
            o_ref[0] = prod

        @pl.when(pl.program_id(1) > 0)
        def _():
            o_ref[0] += prod

    return pl.pallas_call(
        body, name=name, grid=(N_CHIPS, T // tt),
        in_specs=[pl.BlockSpec((tt, ka), (lambda j, t: (t, j)) if a_blocked else (lambda j, t: (t, 0))),
                  pl.BlockSpec((tt, nb), (lambda j, t: (t, j)) if b_blocked else (lambda j, t: (t, 0)))],
        out_specs=pl.BlockSpec((1, ka, nb), lambda j, t: (j, 0, 0)),
        out_shape=_big((N_CHIPS, ka, nb), F32),
        compiler_params=_cp(("arbitrary", "arbitrary"), 40),
    )(*_hbm(a, b))


R_LOSS = 0
R_FFN = 8
R_OUT = 16
R_IN = 24
R_BIN = 32
R_512 = 40
R_DW = 48
N_STAT_ROWS = 80
MOD_ROWS = (R_IN + 0, R_IN + 1, R_OUT + 0, R_FFN + 0, R_FFN + 1, R_LOSS + 1)


def _small_update(gath, params):
    names = ["b_ada", "lb_logits", "g_pre_mix", "b_in", "b_dw", "gn_gain", "gn_bias", "g_hgrn_out", "g_post_mix",
             "g_pre_ffn", "g_post_ffn"]
    flat = []
    for n in names:
        flat += list(params[n])
    n_in = 1 + len(flat)

    def body(*refs):
        g_ref = refs[0]
        prm = {n: refs[1 + 3 * k:4 + 3 * k] for k, n in enumerate(names)}
        outs = refs[n_in:]
        loss_ref, dmod_ref, dwdw_ref = outs[0], outs[1], outs[2]
        res = {n: outs[3 + 4 * k:7 + 4 * k] for k, n in enumerate(names)}
        red = g_ref[0]
        for dev in range(1, N_DEV):
            red = red + g_ref[dev]
        loss_ref[...] = jnp.broadcast_to(
            (0.5 / D_MODEL) * jnp.sum(red[R_LOSS:R_LOSS + 1, :], axis=-1, keepdims=True), loss_ref.shape)
        for dev in range(N_DEV):
            for k, r in enumerate(MOD_ROWS):
                dmod_ref[dev:dev + 1, k * D_MODEL:(k + 1) * D_MODEL] = g_ref[dev, r:r + 1, :]
        dwdw_ref[...] = red[R_DW:R_DW + HALO, 0:CONV_CH]

        def finish(name, pieces):
            w_ref, m_ref, v_ref = prm[name]
            g_out, d_out, m_out, v_out = res[name]
            for rsl, lsl, g in pieces:
                d, m2, v2 = _adam_math(w_ref[rsl, lsl], g, m_ref[rsl, lsl], v_ref[rsl, lsl])
                g_out[rsl, lsl] = g
                d_out[rsl, lsl] = d
                m_out[rsl, lsl] = m2
                v_out[rsl, lsl] = v2

        one = slice(0, 1)
        row = lambda r: red[r:r + 1, :]
        half = lambda r: red[r:r + 1, 0:CONV_CH]
        finish("b_ada", [(one, slice(k * D_MODEL, (k + 1) * D_MODEL), row(r)) for k, r in enumerate(MOD_ROWS)])
        finish("b_in", [(one, slice(k * D_MODEL, (k + 1) * D_MODEL), row(R_BIN + k)) for k in range(3)])
        finish("g_pre_mix", [(one, slice(None), row(R_IN + 2))])
        finish("g_post_mix", [(one, slice(None), row(R_OUT + 1))])
        finish("g_pre_ffn", [(one, slice(None), row(R_FFN + 2))])
        finish("g_post_ffn", [(one, slice(None), row(R_LOSS + 2))])
        finish("b_dw", [(one, slice(None), half(R_512 + 0))])
        finish("gn_gain", [(one, slice(None), half(R_512 + 1))])
        finish("gn_bias", [(one, slice(None), half(R_512 + 2))])
        finish("g_hgrn_out", [(one, slice(None), half(R_512 + 3))])
        s0, s1 = _lower_bound(prm["lb_logits"][0])
        dlb = half(R_512 + 4)
        finish("lb_logits", [(slice(0, 1), slice(None), dlb * s0 * (1.0 - s0)),
                             (slice(1, 2), slice(None), -dlb * s0 * s1)])

    vm = pl.BlockSpec(memory_space=pltpu.VMEM)
    out_shape = [jax.ShapeDtypeStruct((8, 128), F32), jax.ShapeDtypeStruct((N_DEV, 6 * D_MODEL), F32),
                 jax.ShapeDtypeStruct((HALO, CONV_CH), F32)]
    for n in names:
        out_shape += [jax.ShapeDtypeStruct(params[n][0].shape, F32)] * 4
    outs = pl.pallas_call(
        body, name="small_update", out_shape=out_shape,
        in_specs=[vm] * n_in, out_specs=[vm] * len(out_shape),
        compiler_params=_cp(None, 32),
    )(gath, *flat)
    return outs[0], outs[1], outs[2], {n: outs[3 + 4 * k:7 + 4 * k] for k, n in enumerate(names)}


def _wdw_adam(w, g, m, v):
    def body(w_ref, g_ref, m_ref, v_ref, d_out, m_out, v_out):
        d, m2, v2 = _adam_math(w_ref[...], g_ref[...], m_ref[...], v_ref[...])
        d_out[...] = d
        m_out[...] = m2
        v_out[...] = v2

    vm = pl.BlockSpec(memory_space=pltpu.VMEM)
    return pl.pallas_call(
        body, name="wdw_adam", out_shape=[jax.ShapeDtypeStruct(w.shape, F32)] * 3,
        in_specs=[vm] * 4, out_specs=[vm] * 3, compiler_params=_cp(None, 16),
    )(w, g, m, v)


def kernel(x, c, w_ada, b_ada, lb_logits, g_pre_mix, w_in, b_in, w_dw, b_dw, gn_gain, gn_bias, g_hgrn_out, w_out, g_post_mix, g_pre_ffn, w_up, w_down, g_post_ffn, loss_target, m_w_ada, m_b_ada, m_lb_logits, m_g_pre_mix, m_w_in, m_b_in, m_w_dw, m_b_dw, m_gn_gain, m_gn_bias, m_g_hgrn_out, m_w_out, m_g_post_mix, m_g_pre_ffn, m_w_up, m_w_down, m_g_post_ffn, v_w_ada, v_b_ada, v_lb_logits, v_g_pre_mix, v_w_in, v_b_in, v_w_dw, v_b_dw, v_gn_gain, v_gn_bias, v_g_hgrn_out, v_w_out, v_g_post_mix, v_g_pre_ffn, v_w_up, v_w_down, v_g_post_ffn):
    ax, ay, ac = lax.axis_index("x"), lax.axis_index("y"), lax.axis_index("c")
    chip = 2 * ax + ay
    T = x.shape[1]
    xs, tgt = x[0], loss_target[0]
    ada_cols = w_ada.shape[2]

    b_sh = lax.dynamic_slice_in_dim(b_ada, chip * ada_cols, ada_cols, axis=1)
    wdw_pad = jnp.pad(w_dw[0], ((0, HALO - CONV_K), (0, 0)))
    _, c8, modg, wdwg = _ada_exchange(c, w_ada[0], b_sh, wdw_pad)
    modr = modg.reshape(6, D_MODEL)
    wdw_all = jnp.transpose(wdwg, (1, 0, 2)).reshape(HALO, CONV_CH)
    chip1 = jnp.reshape(chip, (1,)).astype(jnp.int32)
    place = jnp.stack([ac, chip]).astype(jnp.int32)
    (w_in_g,) = _weight_gather([_cast_own(chip1, w_in[0], "cast_w_in")], "gather_w_in")
    late_bufs = [_cast_own(chip1, w[0], "cast_" + t) for w, t in ((w_out, "w_out"), (w_up, "w_up"), (w_down, "w_down"))]
    vec = (b_dw, gn_gain, gn_bias, g_hgrn_out)

    p, h1 = _mix_in_fwd(xs, modr, g_pre_mix, w_in_g, b_in)
    (cat, ys, o, states), (w_out_g, w_up_g, w_down_g) = _mixers_fwd(p, wdw_all, vec, lb_logits, late_bufs)
    w_out_f = w_out_g.reshape(D_MODEL, D_MODEL)
    w_down_f = w_down_g.reshape(D_FF, D_MODEL)
    y, x1, h2 = _mix_out_fwd(cat, w_out_f, xs, modr, g_post_mix, g_pre_ffn)
    r, dy2, dx2, st_loss = _ffn_fwd(h2, w_up_g, w_down_f, x1, tgt, modr, g_post_ffn)

    def pair_stage(grads, tags, name):
        got = _pair_swap(grads, name)
        return [_pair_sum(place, g, o_, "pair_sum_" + t) for g, o_, t in zip(grads, got, tags)]

    da, dx1, st_ffn = _ffn_bwd(dy2, r, x1, dx2, w_up_g, w_down_f, modr, g_pre_ffn)
    g_up = _weight_grad(h2, da, False, True, "grad_w_up")
    g_down = _weight_grad(r, dy2, True, False, "grad_w_down")
    dy, dcat, st_out = _mix_out_bwd(dx1, y, w_out_f, modr, g_post_mix)
    g_out = _weight_grad(cat, dy, True, False, "grad_w_out")
    early = pair_stage([g_out, g_up, g_down], ["w_out", "w_up", "w_down"], "pair_swap_early")
    (dp, st_bin, st_512, dwdw), got_early = _mixers_bwd(p, dcat, ys, o, states, wdw_all, vec, lb_logits,
                                                        [pb for _, pb in early])
    g_in = _weight_grad(h1, dp, False, True, "grad_w_in")
    grad_x, st_in = _mix_in_bwd(dp, w_in_g, xs, dx1, modr, g_pre_mix)
    late = pair_stage([g_in], ["w_in"], "pair_swap_late")
    got_late = _chip_exchange([late[0][1]], "chip_exchange_w_in")
    fulls = [_chip_sum(place, pf, gb, "chip_sum_" + t)
             for (pf, _), gb, t in zip(late + early, list(got_late) + list(got_early), ["w_in", "w_out", "w_up", "w_down"])]
    g_w_in, g_w_out, g_w_up, g_w_down = _half_swap(fulls)

    pad_lanes = lambda s: jnp.pad(s, ((0, 0), (0, D_MODEL - s.shape[1])))
    stats = jnp.concatenate([st_loss, st_ffn, st_out, st_in, st_bin, pad_lanes(st_512), pad_lanes(dwdw)], axis=0)
    gath = _stats_gather(stats)
    small = {"b_ada": (b_ada, m_b_ada, v_b_ada), "lb_logits": (lb_logits, m_lb_logits, v_lb_logits),
             "g_pre_mix": (g_pre_mix, m_g_pre_mix, v_g_pre_mix), "b_in": (b_in, m_b_in, v_b_in),
             "b_dw": (b_dw, m_b_dw, v_b_dw), "gn_gain": (gn_gain, m_gn_gain, v_gn_gain),
             "gn_bias": (gn_bias, m_gn_bias, v_gn_bias), "g_hgrn_out": (g_hgrn_out, m_g_hgrn_out, v_g_hgrn_out),
             "g_post_mix": (g_post_mix, m_g_post_mix, v_g_post_mix), "g_pre_ffn": (g_pre_ffn, m_g_pre_ffn, v_g_pre_ffn),
             "g_post_ffn": (g_post_ffn, m_g_post_ffn, v_g_post_ffn)}
    loss_t, dmod_all, dwdw_sum, sres = _small_update(gath, small)
    loss = loss_t[0, 0]

    res = dict(sres)
    dmod_sh = lax.dynamic_slice_in_dim(dmod_all, chip * ada_cols, ada_cols, axis=1)
    res["w_ada"] = [t[None] for t in _ada_grad_adam(jnp.transpose(c8), dmod_sh, w_ada[0], m_w_ada[0], v_w_ada[0])]
    g_wdw = lax.dynamic_slice_in_dim(dwdw_sum, chip * HEAD_D, HEAD_D, axis=1)[:CONV_K][None]
    res["w_dw"] = [g_wdw] + list(_wdw_adam(w_dw, g_wdw, m_w_dw, v_w_dw))
    for name, g, w, m, v in (("w_in", g_w_in, w_in, m_w_in, v_w_in), ("w_out", g_w_out, w_out, m_w_out, v_w_out),
                             ("w_up", g_w_up, w_up, m_w_up, v_w_up), ("w_down", g_w_down, w_down, m_w_down, v_w_down)):
        d, m2, v2 = _adam_big(w[0], g, m[0], v[0], "adam_" + name)
        res[name] = [g[None], d[None], m2[None], v2[None]]

    order = ["w_ada", "b_ada", "lb_logits", "g_pre_mix", "w_in", "b_in", "w_dw", "b_dw", "gn_gain", "gn_bias",
             "g_hgrn_out", "w_out", "g_post_mix", "g_pre_ffn", "w_up", "w_down", "g_post_ffn"]
    out = [loss, grad_x[None]]
    for k in range(4):
        out += [res[n][k] for n in order]
    return tuple(out)
```

```python
import jax
import jax.numpy as jnp
from jax import lax
from jax.experimental import pallas as pl
from jax.experimental.pallas import tpu as pltpu

F32, BF16 = jnp.float32, jnp.bfloat16
D_MODEL = 1024
CONV_CH = 512
HGRN_W = 512
N_HEADS = 4
HEAD_D = 128
CONV_K = 31
GN_GROUP = 64
GN_SHIFT = 6
IN_COLS = 3072
D_FF = 4096
CHUNK = 64
CHUNK_SHIFT = 6
N_CHIPS = 4
N_DEV = 8
RMS_EPS = 1e-6
GN_EPS = 1e-5
ADAM_LR, ADAM_B1, ADAM_B2, ADAM_EPS, ADAM_WD, ADAM_STEP = 0.001, 0.9, 0.999, 1e-08, 0.01, 10
TOK_TILE = 512
MIXB_TILE = 256
FFN_TILE = 1024
FFN_BLOCK = 512
GRAD_TILE = 2048
HALO = 32
SUB = 8
LANE = 128
CONV_ROWS = 128
MIB = 1 << 20
MESH = pl.DeviceIdType.MESH
OTHER_CHIPS = ((0, 1), (1, 0), (1, 1))


def _cp(sem=None, vmem_mib=48):
    return pltpu.CompilerParams(dimension_semantics=sem, vmem_limit_bytes=vmem_mib * MIB)


def _dot(a, b):
    return jnp.dot(a, b, preferred_element_type=F32)


def _dot_nt(a, b):
    return lax.dot_general(a, b, (((1,), (1,)), ((), ())), preferred_element_type=F32)


def _dot_tn(a, b):
    return lax.dot_general(a, b, (((0,), (0,)), ((), ())), preferred_element_type=F32)


def _sig(v):
    return jax.nn.sigmoid(v)


def _colsum(v):
    return jnp.sum(v, axis=0, keepdims=True)


def _flip(v, b):
    return 1 - v if b else v


def _rcopy(src, dst, ssem, rsem, dev):
    return pltpu.make_async_remote_copy(src_ref=src, dst_ref=dst, send_sem=ssem, recv_sem=rsem,
                                        device_id=dev, device_id_type=MESH)


def _place():
    return lax.axis_index("x"), lax.axis_index("y"), lax.axis_index("c")


def _full(shape):
    return pl.BlockSpec(shape, lambda *_: (0,) * len(shape))


def _big(shape, dtype):
    return pltpu.HBM(shape, dtype)


def _hbm(*arrays):
    out = [pltpu.with_memory_space_constraint(a, pltpu.HBM) for a in arrays]
    return out[0] if len(out) == 1 else out


def _split2(v):
    hi = v.astype(BF16)
    lo = (v - hi.astype(F32)).astype(BF16)
    return hi, lo


def _split3(v):
    h1 = v.astype(BF16)
    r1 = v - h1.astype(F32)
    h2 = r1.astype(BF16)
    h3 = (r1 - h2.astype(F32)).astype(BF16)
    return h1, h2, h3


def _mm3s(mat, parts):
    h1, h2, h3 = parts
    return _dot(mat, h1) + _dot(mat, h2) + _dot(mat, h3)


def _mm3(mat, v):
    return _mm3s(mat, _split3(v))


def _gn_matrix():
    r = lax.broadcasted_iota(jnp.int32, (CONV_CH, CONV_CH), 0) >> GN_SHIFT
    c = lax.broadcasted_iota(jnp.int32, (CONV_CH, CONV_CH), 1) >> GN_SHIFT
    return jnp.where(r == c, 1.0 / GN_GROUP, 0.0).astype(BF16)


def _gmean(v, gmat):
    hi, lo = _split2(v)
    return _dot(hi, gmat) + _dot(lo, gmat)


def _chunk_masks(tm):
    r = lax.broadcasted_iota(jnp.int32, (tm, tm), 0)
    c = lax.broadcasted_iota(jnp.int32, (tm, tm), 1)
    same = (r >> CHUNK_SHIFT) == (c >> CHUNK_SHIFT)
    one = lambda m: jnp.where(m, 1.0, 0.0).astype(BF16)
    return one(same & (c <= r)), one(same & (c >= r)), one(same)


def _tri():
    return lax.broadcasted_iota(jnp.int32, (CHUNK, CHUNK), 0) >= lax.broadcasted_iota(jnp.int32, (CHUNK, CHUNK), 1)


def _lower_bound(lbl_ref):
    l0, l1 = lbl_ref[0:1, :], lbl_ref[1:2, :]
    mx = jnp.maximum(l0, l1)
    e0, e1 = jnp.exp(l0 - mx), jnp.exp(l1 - mx)
    return e0 / (e0 + e1), e1 / (e0 + e1)


CONV_FWD_TAPS = tuple((j, HALO - (CONV_K - 1) + j) for j in range(CONV_K))
CONV_BWD_TAPS = tuple((j, (CONV_K - 1) - j) for j in range(CONV_K))


def _tap_conv(src_ref, w_ref, row0, taps, lanes):
    acc = None
    for b in range(SUB):
        pb = None
        for j, off in taps:
            if off % SUB == b:
                lo = row0 + off - b
                term = w_ref[j:j + 1, lanes] * src_ref[lo:lo + CONV_ROWS + SUB, lanes]
                pb = term if pb is None else pb + term
        if pb is not None:
            sh = pb[b:b + CONV_ROWS, :]
            acc = sh if acc is None else acc + sh
    return acc


def _hgrn_prep(pq, pf, lb, lower, same):
    sq = _sig(pq)
    qf = pq * sq
    sf = _sig(pf)
    f = lb + (1.0 - lb) * sf
    logf = jnp.log(f)
    k = 1.0 - f
    parts = _split3(logf)
    G = _mm3s(lower, parts)
    Gl = _mm3s(same, parts)
    eG, enG, eGlG = jnp.exp(G), jnp.exp(-G), jnp.exp(Gl - G)
    return dict(sq=sq, sf=sf, f=f, Gl=Gl, eG=eG, enG=enG, eGlG=eGlG, qt=qf * eG, kt=k * enG, kh=k * eGlG)


def _ada_exchange(c_row, w_ada, b_sh, wdw_pad, w_in_buf):
    ncol = w_ada.shape[1]

    def body(c_ref, w_ref, b_ref, wdw_ref, win_in, call_ref, c8_ref, modg_ref, wdwg_ref, win_ref,
             rows_s, sa, ra, sw, rw, sm, rm, gssem, grsem):
        _gather_start([win_ref], gssem, grsem)
        x, y, c = _place()
        me = 4 * x + 2 * y + c
        chip = 2 * x + y
        cv = c_ref[...]
        call_ref[me] = cv * _sig(cv)
        wdwg_ref[chip] = wdw_ref[...]
        sends = []
        for m in range(1, N_DEV):
            peer = (_flip(x, m >> 2), _flip(y, (m >> 1) & 1), _flip(c, m & 1))
            cp = _rcopy(call_ref.at[me], call_ref.at[me], sa.at[m - 1], ra.at[m - 1], peer)
            cp.start()
            sends.append(cp)
        for k, (fx, fy) in enumerate(OTHER_CHIPS):
            peer = (_flip(x, fx), _flip(y, fy), c)
            cp = _rcopy(wdwg_ref.at[chip], wdwg_ref.at[chip], sw.at[k], rw.at[k], peer)
            cp.start()
            sends.append(cp)
        for m in range(1, N_DEV):
            peer = (_flip(x, m >> 2), _flip(y, (m >> 1) & 1), _flip(c, m & 1))
            pid = 4 * peer[0] + 2 * peer[1] + peer[2]
            _rcopy(call_ref.at[pid], call_ref.at[pid], sa.at[m - 1], ra.at[m - 1], peer).wait_recv()
        for b in range(N_DEV):
            c8_ref[b:b + 1, :] = call_ref[b]
        mod_all = _dot(c8_ref[...].astype(BF16), w_ref[...].astype(BF16)) + b_ref[...]
        for b in range(N_DEV):
            rows_s[b] = mod_all[b:b + 1, :]
        modg_ref[chip] = rows_s[me]
        for k, (fx, fy) in enumerate(OTHER_CHIPS):
            peer = (_flip(x, fx), _flip(y, fy), c)
            pid = 4 * peer[0] + 2 * peer[1] + peer[2]
            cp = _rcopy(rows_s.at[pid], modg_ref.at[chip], sm.at[k], rm.at[k], peer)
            cp.start()
            sends.append(cp)
        for k, (fx, fy) in enumerate(OTHER_CHIPS):
            peer = (_flip(x, fx), _flip(y, fy), c)
            pchip = 2 * peer[0] + peer[1]
            _rcopy(rows_s.at[0], modg_ref.at[pchip], sm.at[k], rm.at[k], peer).wait_recv()
            _rcopy(wdwg_ref.at[pchip], wdwg_ref.at[pchip], sw.at[k], rw.at[k], peer).wait_recv()
        for cp in sends:
            cp.wait_send()
        _gather_finish([win_ref], gssem, grsem)

    vm = pl.BlockSpec(memory_space=pltpu.VMEM)
    hbm = pl.BlockSpec(memory_space=pl.ANY)
    return pl.pallas_call(
        body, name="ada_exchange",
        out_shape=[jax.ShapeDtypeStruct((N_DEV, 1, D_MODEL), F32), jax.ShapeDtypeStruct((N_DEV, D_MODEL), F32),
                   jax.ShapeDtypeStruct((N_CHIPS, 1, ncol), F32), jax.ShapeDtypeStruct((N_CHIPS, HALO, HEAD_D), F32),
                   _big(w_in_buf.shape, BF16)],
        in_specs=[vm] * 4 + [hbm], out_specs=[vm] * 4 + [hbm], input_output_aliases={4: 4},
        scratch_shapes=[pltpu.VMEM((N_DEV, 1, ncol), F32),
                        pltpu.SemaphoreType.DMA((N_DEV - 1,)), pltpu.SemaphoreType.DMA((N_DEV - 1,)),
                        pltpu.SemaphoreType.DMA((3,)), pltpu.SemaphoreType.DMA((3,)),
                        pltpu.SemaphoreType.DMA((3,)), pltpu.SemaphoreType.DMA((3,))] + _gather_sems(1),
        compiler_params=_cp(None, 32),
    )(c_row, w_ada, b_sh, wdw_pad, _hbm(w_in_buf))


def _cast_own(chip1, shard, name):
    rows, cols = shard.shape
    tr = _row_tile(rows)

    def body(ch_ref, s_ref, o_ref):
        o_ref[0] = s_ref[...].astype(BF16)

    gs = pltpu.PrefetchScalarGridSpec(
        num_scalar_prefetch=1, grid=(rows // tr,),
        in_specs=[pl.BlockSpec((tr, cols), lambda i, ch: (i, 0))],
        out_specs=pl.BlockSpec((1, tr, cols), lambda i, ch: (ch[0], i, 0)))
    return pl.pallas_call(
        body, name=name, grid_spec=gs, out_shape=_big((N_CHIPS, rows, cols), BF16),
        compiler_params=_cp(("arbitrary",), 32),
    )(chip1, _hbm(shard))


def _slab(buf, ch, core):
    hs = buf.shape[1] // 2
    return buf.at[ch, pl.ds(core * hs, hs), :]


def _gather_start(bufs, ssem, rsem):
    x, y, c = _place()
    chip = 2 * x + y
    for k, (fx, fy) in enumerate(OTHER_CHIPS):
        peer = (_flip(x, fx), _flip(y, fy), c)
        for t, buf in enumerate(bufs):
            _rcopy(_slab(buf, chip, c), _slab(buf, chip, c), ssem.at[t * 3 + k], rsem.at[t * 3 + k], peer).start()


def _gather_finish(bufs, ssem, rsem):
    nt = len(bufs)
    x, y, c = _place()
    chip = 2 * x + y
    sibling = (x, y, 1 - c)
    for k, (fx, fy) in enumerate(OTHER_CHIPS):
        peer = (_flip(x, fx), _flip(y, fy), c)
        pchip = 2 * peer[0] + peer[1]
        for t, buf in enumerate(bufs):
            _rcopy(_slab(buf, pchip, c), _slab(buf, pchip, c), ssem.at[t * 3 + k], rsem.at[t * 3 + k], peer).wait_recv()
            _rcopy(_slab(buf, pchip, c), _slab(buf, pchip, c), ssem.at[3 * nt + t * 3 + k],
                   rsem.at[3 * nt + t * 3 + k], sibling).start()
    for k, (fx, fy) in enumerate(OTHER_CHIPS):
        peer = (_flip(x, fx), _flip(y, fy), c)
        pchip = 2 * peer[0] + peer[1]
        for t, buf in enumerate(bufs):
            _rcopy(_slab(buf, pchip, 1 - c), _slab(buf, pchip, 1 - c), ssem.at[3 * nt + t * 3 + k],
                   rsem.at[3 * nt + t * 3 + k], sibling).wait_recv()
            _rcopy(_slab(buf, chip, c), _slab(buf, chip, c), ssem.at[t * 3 + k], rsem.at[t * 3 + k], peer).wait_send()
            _rcopy(_slab(buf, pchip, c), _slab(buf, pchip, c), ssem.at[3 * nt + t * 3 + k],
                   rsem.at[3 * nt + t * 3 + k], sibling).wait_send()


def _gather_sems(nt):
    return [pltpu.SemaphoreType.DMA((6 * nt,)), pltpu.SemaphoreType.DMA((6 * nt,))]


def _pair_swap(grads, name):
    nt = len(grads)
    hbm = pl.BlockSpec(memory_space=pl.ANY)

    def body(*refs):
        ins, outs = refs[:nt], refs[nt:2 * nt]
        ssem, rsem = refs[2 * nt], refs[2 * nt + 1]
        x, y, c = _place()
        sibling = (x, y, 1 - c)
        copies = []
        for t in range(nt):
            hs = ins[t].shape[1] // 2
            cp = _rcopy(ins[t].at[:, pl.ds((1 - c) * hs, hs), :], outs[t], ssem.at[t], rsem.at[t], sibling)
            cp.start()
            copies.append(cp)
        for cp in copies:
            cp.wait_recv()
        for cp in copies:
            cp.wait_send()

    return pl.pallas_call(
        body, name=name,
        out_shape=[_big((g.shape[0], g.shape[1] // 2, g.shape[2]), g.dtype) for g in grads],
        in_specs=[hbm] * nt, out_specs=[hbm] * nt,
        scratch_shapes=[pltpu.SemaphoreType.DMA((nt,)), pltpu.SemaphoreType.DMA((nt,))],
    )(*[_hbm(g) for g in grads])


def _xchg_copies(ins, outs, ssem, rsem):
    x, y, c = _place()
    copies = []
    for k, (fx, fy) in enumerate(OTHER_CHIPS):
        peer = (_flip(x, fx), _flip(y, fy), c)
        for t in range(len(ins)):
            copies.append(_rcopy(ins[t].at[k], outs[t].at[k], ssem.at[t * 3 + k], rsem.at[t * 3 + k], peer))
    return copies


def _xchg_sems(nt):
    return [pltpu.SemaphoreType.DMA((3 * nt,)), pltpu.SemaphoreType.DMA((3 * nt,))]


def _final_exchange(fulls, stats):
    nt = len(fulls)
    rows, cols = stats.shape
    hbm = pl.BlockSpec(memory_space=pl.ANY)
    vm = pl.BlockSpec(memory_space=pltpu.VMEM)

    def body(*refs):
        ins, s_ref = refs[:nt], refs[nt]
        outs, g_ref = refs[nt + 1:2 * nt + 1], refs[2 * nt + 1]
        hssem, hrsem, ssem, rsem = refs[2 * nt + 2:]
        x, y, c = _place()
        me, sibling = (x, y, c), (x, y, 1 - c)
        halves = []
        for t in range(nt):
            hs = ins[t].shape[0] // 2
            mine = pl.ds(c * hs, hs)
            cp = _rcopy(ins[t].at[mine, :], outs[t].at[mine, :], hssem.at[t], hrsem.at[t], sibling)
            cp.start()
            halves.append(cp)

        chips = [(_flip(x, fx), _flip(y, fy)) for fx, fy in OTHER_CHIPS]

        def blk(px, py, pc):
            return g_ref.at[4 * px + 2 * py + pc]

        def copy(k, block, to, src=None):
            return _rcopy(blk(*block) if src is None else src, blk(*block), ssem.at[k], rsem.at[k], to)

        g_ref[4 * x + 2 * y + c] = s_ref[...]
        first = [copy(0, me, sibling, src=s_ref)]
        first += [copy(1 + j, me, (*chip, c), src=s_ref) for j, chip in enumerate(chips)]
        for cp in first:
            cp.start()
        passed = [copy(4 + j, (*chip, c), sibling) for j, chip in enumerate(chips)]
        for j, chip in enumerate(chips):
            copy(1 + j, (*chip, c), me).wait_recv()
            passed[j].start()
        copy(0, sibling, me).wait_recv()
        for j, chip in enumerate(chips):
            copy(4 + j, (*chip, 1 - c), me).wait_recv()
        for t in range(nt):
            hs = ins[t].shape[0] // 2
            other = pl.ds((1 - c) * hs, hs)
            _rcopy(ins[t].at[other, :], outs[t].at[other, :], hssem.at[t], hrsem.at[t], sibling).wait_recv()
        for cp in first + passed + halves:
            cp.wait_send()

    outs = pl.pallas_call(
        body, name="final_exchange",
        out_shape=[_big(f.shape, F32) for f in fulls] + [jax.ShapeDtypeStruct((N_DEV, rows, cols), F32)],
        in_specs=[hbm] * nt + [vm], out_specs=[hbm] * nt + [vm],
        input_output_aliases={t: t for t in range(nt)},
        scratch_shapes=[pltpu.SemaphoreType.DMA((nt,)), pltpu.SemaphoreType.DMA((nt,)),
                        pltpu.SemaphoreType.DMA((7,)), pltpu.SemaphoreType.DMA((7,))],
        compiler_params=_cp(None, 32),
    )(*[_hbm(f) for f in fulls], stats)
    return outs[:nt], outs[nt]


def _row_tile(rows):
    return min(rows, 256)


def _pair_sum(place, grad, got, name):
    nb, hs, cols = got.shape
    tr = _row_tile(hs)
    nr = hs // tr

    def body(pl_ref, g_ref, o_ref, pf_ref, pb_ref):
        j = pl.program_id(1)
        s = g_ref[0] + o_ref[0].astype(F32)

        @pl.when(j == 0)
        def _():
            pf_ref[...] = s

        @pl.when(j > 0)
        def _():
            pb_ref[0] = s.astype(BF16)

    gs = pltpu.PrefetchScalarGridSpec(
        num_scalar_prefetch=1, grid=(nr, nb),
        in_specs=[pl.BlockSpec((1, tr, cols), lambda i, j, p: (p[1] ^ j, p[0] * nr + i, 0)),
                  pl.BlockSpec((1, tr, cols), lambda i, j, p: (p[1] ^ j, i, 0))],
        out_specs=[pl.BlockSpec((tr, cols), lambda i, j, p: (i, 0)),
                   pl.BlockSpec((1, tr, cols), lambda i, j, p: (jnp.maximum(j - 1, 0), i, 0))])
    return pl.pallas_call(
        body, name=name, grid_spec=gs,
        out_shape=[_big((hs, cols), F32), _big((nb - 1, hs, cols), BF16)],
        compiler_params=_cp(("arbitrary", "arbitrary"), 32),
    )(place, *_hbm(grad, got))


def _chip_sum(place, pair_f, got_b, name):
    nb, hs, cols = got_b.shape
    tr = _row_tile(hs)
    nr = hs // tr

    def body(pl_ref, pf_ref, gb_ref, o_ref):
        acc = pf_ref[...]
        for k in range(nb):
            acc = acc + gb_ref[k].astype(F32)
        o_ref[...] = acc

    gs = pltpu.PrefetchScalarGridSpec(
        num_scalar_prefetch=1, grid=(nr,),
        in_specs=[pl.BlockSpec((tr, cols), lambda i, p: (i, 0)),
                  pl.BlockSpec((nb, tr, cols), lambda i, p: (0, i, 0))],
        out_specs=pl.BlockSpec((tr, cols), lambda i, p: (p[0] * nr + i, 0)))
    return pl.pallas_call(
        body, name=name, grid_spec=gs,
        out_shape=_big((2 * hs, cols), F32),
        compiler_params=_cp(("arbitrary",), 32),
    )(place, *_hbm(pair_f, got_b))


def _adam_math(w, g, m, v):
    m2 = ADAM_B1 * m + (1.0 - ADAM_B1) * g
    v2 = ADAM_B2 * v + (1.0 - ADAM_B2) * (g * g)
    m_hat = m2 / (1.0 - ADAM_B1 ** ADAM_STEP)
    v_hat = v2 / (1.0 - ADAM_B2 ** ADAM_STEP)
    delta = -ADAM_LR * (m_hat / (jnp.sqrt(v_hat) + ADAM_EPS) + ADAM_WD * w)
    return delta, m2, v2


def _adam_big(w, g, m, v, name):
    rows, cols = w.shape
    tr = _row_tile(rows)

    def body(w_ref, g_ref, m_ref, v_ref, d_out, m_out, v_out):
        d, m2, v2 = _adam_math(w_ref[...], g_ref[...], m_ref[...], v_ref[...])
        d_out[...] = d
        m_out[...] = m2
        v_out[...] = v2

    spec = pl.BlockSpec((tr, cols), lambda i: (i, 0))
    return pl.pallas_call(
        body, name=name, grid=(rows // tr,), in_specs=[spec] * 4, out_specs=[spec] * 3,
        out_shape=[_big(w.shape, F32)] * 3,
        compiler_params=_cp(("arbitrary",), 32),
    )(*_hbm(w, g, m, v))


def _ada_grad_adam(c8t, dmod_sh, w, m, v):
    rows, cols = w.shape
    tr = _row_tile(rows)

    def body(ct_ref, dm_ref, w_ref, m_ref, v_ref, g_out, d_out, m_out, v_out):
        g = None
        for b in range(N_DEV):
            term = ct_ref[:, b:b + 1] * dm_ref[b:b + 1, :]
            g = term if g is None else g + term
        d, m2, v2 = _adam_math(w_ref[...], g, m_ref[...], v_ref[...])
        g_out[...] = g
        d_out[...] = d
        m_out[...] = m2
        v_out[...] = v2

    spec = pl.BlockSpec((tr, cols), lambda i: (i, 0))
    return pl.pallas_call(
        body, name="ada_grad_adam", grid=(rows // tr,),
        in_specs=[pl.BlockSpec((tr, N_DEV), lambda i: (i, 0)), _full((N_DEV, cols)), spec, spec, spec],
        out_specs=[spec] * 4, out_shape=[_big(w.shape, F32)] * 4,
        compiler_params=_cp(("arbitrary",), 32),
    )(c8t, dmod_sh, *_hbm(w, m, v))


def _tok_tile(t):
    return min(TOK_TILE, t)


def _mix_in_fwd(x, modr, g_pre, w_in_g, b_in):
    T = x.shape[0]
    tm = _tok_tile(T)
    nb = IN_COLS // N_CHIPS

    def body(x_ref, mod_ref, g_ref, w_ref, b_ref, p_ref, h_ref):
        xv = x_ref[...]
        rstd = lax.rsqrt(jnp.mean(xv * xv, axis=-1, keepdims=True) + RMS_EPS)
        h = (xv * rstd) * g_ref[...] * (1.0 + mod_ref[1:2, :]) + mod_ref[0:1, :]
        hb = h.astype(BF16)
        h_ref[...] = hb
        for j in range(N_CHIPS):
            p_ref[:, j * nb:(j + 1) * nb] = _dot(hb, w_ref[j]) + b_ref[:, j * nb:(j + 1) * nb]

    return pl.pallas_call(
        body, name="mix_in_fwd", grid=(T // tm,),
        in_specs=[pl.BlockSpec((tm, D_MODEL), lambda i: (i, 0)), _full((6, D_MODEL)), _full((1, D_MODEL)),
                  _full((N_CHIPS, D_MODEL, nb)), _full((1, IN_COLS))],
        out_specs=[pl.BlockSpec((tm, IN_COLS), lambda i: (i, 0)), pl.BlockSpec((tm, D_MODEL), lambda i: (i, 0))],
        out_shape=[_big((T,IN_COLS), F32), _big((T,D_MODEL), BF16)],
        compiler_params=_cp(("arbitrary",), 48),
    )(_hbm(x), modr, g_pre, _hbm(w_in_g), b_in)


def _mixers_fwd(p, wdw, vecs, lbl, gbufs):
    T = p.shape[0]
    tm = _tok_tile(T)
    nt = T // tm
    nch = tm // CHUNK
    ng = len(gbufs)

    def body(*refs):
        p_ref, wdw_ref, bdw_ref, gain_ref, bias_ref, gout_ref, lbl_ref = refs[:7]
        cat_ref, ys_ref, o_ref, st_ref = refs[7 + ng:11 + ng]
        gout_bufs = refs[11 + ng:11 + 2 * ng]
        ubuf, state, qt_s, kt_s, kh_s, v_s, egl_s, lower_s, same_s, gmat_s, gssem, grsem = refs[11 + 2 * ng:]
        i = pl.program_id(0)

        @pl.when(i == 0)
        def _():
            _gather_start(gout_bufs, gssem, grsem)
            lower_s[...], _, same_s[...] = _chunk_masks(tm)
            gmat_s[...] = _gn_matrix()
            state[...] = jnp.zeros(state.shape, F32)
            ubuf[0:HALO, :] = jnp.zeros((HALO, CONV_CH), F32)
            ubuf[HALO + tm:HALO + tm + SUB, :] = jnp.zeros((SUB, CONV_CH), F32)

        @pl.when(i > 0)
        def _():
            ubuf[0:HALO, :] = ubuf[tm:tm + HALO, :]

        ubuf[HALO:HALO + tm, :] = p_ref[:, 0:CONV_CH] * _sig(p_ref[:, CONV_CH:2 * CONV_CH])
        for r in range(tm // CONV_ROWS):
            rows = slice(r * CONV_ROWS, (r + 1) * CONV_ROWS)
            for lb_ in range(CONV_CH // LANE):
                lanes = slice(lb_ * LANE, (lb_ + 1) * LANE)
                ys_ref[rows, lanes] = bdw_ref[:, lanes] + _tap_conv(ubuf, wdw_ref, r * CONV_ROWS, CONV_FWD_TAPS, lanes)
        gmat = gmat_s[...]
        yv = ys_ref[...]
        d = yv - _gmean(yv, gmat)
        rs = lax.rsqrt(_gmean(d * d, gmat) + GN_EPS)
        z = d * rs * gain_ref[...] + bias_ref[...]
        cat_ref[:, 0:CONV_CH] = (z * _sig(z)).astype(BF16)

        lb, _ = _lower_bound(lbl_ref)
        lower, same = lower_s[...], same_s[...]
        o0 = 2 * CONV_CH
        pr = _hgrn_prep(p_ref[:, o0:o0 + HGRN_W], p_ref[:, o0 + HGRN_W:o0 + 2 * HGRN_W], lb, lower, same)
        qt_s[...] = pr["qt"].astype(BF16)
        kt_s[...] = pr["kt"].astype(BF16)
        kh_s[...] = pr["kh"].astype(BF16)
        v_s[...] = p_ref[:, o0 + 2 * HGRN_W:o0 + 3 * HGRN_W].astype(BF16)
        egl_s[...] = jnp.exp(pr["Gl"])
        tri = _tri()

        def chunk(ci, carry):
            r0 = pl.multiple_of(ci * CHUNK, CHUNK)
            rows = pl.ds(r0, CHUNK)
            for h in range(N_HEADS):
                ls = pl.ds(h * HEAD_D, HEAD_D)
                qc, kc, hc, vc = qt_s[rows, ls], kt_s[rows, ls], kh_s[rows, ls], v_s[rows, ls]
                s0 = state[h]
                s0b = s0.astype(BF16)
                st_ref[ci, h] = s0
                att =jnp.where(tri, _dot_nt(qc, kc), 0.0).astype(BF16)
                o_ref[rows, ls] = _dot(att, vc) + _dot_nt(qc, s0b)
                state[h] = s0 * egl_s[pl.ds(r0, 1), ls] + _dot_tn(vc, hc)
            return carry

        lax.fori_loop(0, nch, chunk, 0)
        for h in range(N_HEADS):
            sl = slice(h * HEAD_D, (h + 1) * HEAD_D)
            oh = o_ref[:, sl]
            gh = p_ref[:, o0 + 3 * HGRN_W + h * HEAD_D:o0 + 3 * HGRN_W + (h + 1) * HEAD_D]
            rsh = lax.rsqrt(jnp.mean(oh * oh, axis=-1, keepdims=True) + RMS_EPS)
            hg = (oh * rsh) * gout_ref[:, sl] * (gh * _sig(gh))
            cat_ref[:, CONV_CH + h * HEAD_D:CONV_CH + (h + 1) * HEAD_D] = hg.astype(BF16)

        @pl.when(i == nt - 1)
        def _():
            _gather_finish(gout_bufs, gssem, grsem)

    tile = lambda cols: pl.BlockSpec((tm, cols), lambda i: (i, 0))
    hbm = pl.BlockSpec(memory_space=pl.ANY)
    n_in = 7
    outs = pl.pallas_call(
        body, name="mixers_fwd", grid=(nt,),
        in_specs=[tile(IN_COLS), _full((HALO, CONV_CH))] + [_full((1, CONV_CH))] * 4 + [_full((2, HGRN_W))]
        + [hbm] * ng,
        out_specs=[tile(D_MODEL), tile(CONV_CH), tile(HGRN_W),
                   pl.BlockSpec((nch, N_HEADS, HEAD_D, HEAD_D), lambda i: (i, 0, 0, 0))] + [hbm] * ng,
        out_shape=[_big((T, D_MODEL), BF16), _big((T, CONV_CH), F32), _big((T, HGRN_W), F32),
                   _big((T // CHUNK, N_HEADS, HEAD_D, HEAD_D), F32)] + [_big(b.shape, BF16) for b in gbufs],
        input_output_aliases={n_in + t: 4 + t for t in range(ng)},
        scratch_shapes=[pltpu.VMEM((tm + HALO + SUB, CONV_CH), F32), pltpu.VMEM((N_HEADS, HEAD_D, HEAD_D), F32),
                        pltpu.VMEM((tm, HGRN_W), BF16), pltpu.VMEM((tm, HGRN_W), BF16),
                        pltpu.VMEM((tm, HGRN_W), BF16), pltpu.VMEM((tm, HGRN_W), BF16),
                        pltpu.VMEM((tm, HGRN_W), F32), pltpu.VMEM((tm, tm), BF16), pltpu.VMEM((tm, tm), BF16),
                        pltpu.VMEM((CONV_CH, CONV_CH), BF16)] + _gather_sems(ng),
        compiler_params=_cp(("arbitrary",), 56),
    )(_hbm(p), wdw, *vecs, lbl, *[_hbm(b) for b in gbufs])
    return outs[:4], outs[4:]


def _mix_out_fwd(cat, w_out, x, modr, g_post, g_ffn):
    T = x.shape[0]
    tm = _tok_tile(T)

    def body(cat_ref, w_ref, x_ref, mod_ref, gp_ref, gf_ref, y_ref, x1_ref, h2_ref):
        yv = _dot(cat_ref[...], w_ref[...])
        y_ref[...] = yv
        rs = lax.rsqrt(jnp.mean(yv * yv, axis=-1, keepdims=True) + RMS_EPS)
        x1 = x_ref[...] + mod_ref[2:3, :] * ((yv * rs) * gp_ref[...])
        x1_ref[...] = x1
        rs1 = lax.rsqrt(jnp.mean(x1 * x1, axis=-1, keepdims=True) + RMS_EPS)
        h2 = (x1 * rs1) * gf_ref[...] * (1.0 + mod_ref[4:5, :]) + mod_ref[3:4, :]
        h2_ref[...] = h2.astype(BF16)

    tile = pl.BlockSpec((tm, D_MODEL), lambda i: (i, 0))
    return pl.pallas_call(
        body, name="mix_out_fwd", grid=(T // tm,),
        in_specs=[tile, _full((D_MODEL, D_MODEL)), tile, _full((6, D_MODEL)), _full((1, D_MODEL)),
                  _full((1, D_MODEL))],
        out_specs=[tile, tile, tile],
        out_shape=[_big((T,D_MODEL), F32), _big((T,D_MODEL), F32),
                   _big((T,D_MODEL), BF16)],
        compiler_params=_cp(("arbitrary",), 40),
    )(*_hbm(cat, w_out, x), modr, g_post, g_ffn)


def _ffn_blocks():
    return D_FF // FFN_BLOCK, (D_FF // N_CHIPS) // FFN_BLOCK


def _ffn_fwd(h2, w_up_g, w_down, x1, target, modr, g_post):
    T = h2.shape[0]
    tm = min(FFN_TILE, T)
    fb = FFN_BLOCK
    nj, per = _ffn_blocks()

    def body(h_ref, wu_ref, wd_ref, x1_ref, t_ref, mod_ref, g_ref, r_ref, dy2_ref, dx2_ref, st_ref, acc):
        i, j = pl.program_id(0), pl.program_id(1)
        ra = jnp.maximum(_dot(h_ref[...], wu_ref[0]), 0.0)
        rb = (ra * ra).astype(BF16)
        r_ref[...] = rb
        part = _dot(rb, wd_ref[...])

        @pl.when(j == 0)
        def _():
            acc[...] = part

        @pl.when(j > 0)
        def _():
            acc[...] += part

        @pl.when((i == 0) & (j == 0))
        def _():
            st_ref[...] = jnp.zeros(st_ref.shape, F32)

        @pl.when(j == nj - 1)
        def _():
            y2 = acc[...]
            rs = lax.rsqrt(jnp.mean(y2 * y2, axis=-1, keepdims=True) + RMS_EPS)
            nh = y2 * rs
            gp = g_ref[...]
            err = x1_ref[...] + mod_ref[5:6, :] * (nh * gp) - t_ref[...]
            dx2 = err * (1.0 / D_MODEL)
            dx2_ref[...] = dx2
            st_ref[0:1, :] += _colsum(err * err)
            st_ref[1:2, :] += _colsum(dx2 * (nh * gp))
            dn = dx2 * mod_ref[5:6, :]
            st_ref[2:3, :] += _colsum(dn * nh)
            dy2_ref[...] = _rms_bwd(dn * gp, nh, rs).astype(BF16)

    tile = pl.BlockSpec((tm, D_MODEL), lambda i, j: (i, 0))
    return pl.pallas_call(
        body, name="ffn_fwd", grid=(T // tm, nj),
        in_specs=[tile, pl.BlockSpec((1, D_MODEL, fb), lambda i, j: (j // per, 0, j % per)),
                  pl.BlockSpec((fb, D_MODEL), lambda i, j: (j, 0)), tile, tile,
                  _full((6, D_MODEL)), _full((1, D_MODEL))],
        out_specs=[pl.BlockSpec((tm, fb), lambda i, j: (i, j)), tile, tile, _full((8, D_MODEL))],
        out_shape=[_big((T, D_FF), BF16), _big((T, D_MODEL), BF16), _big((T, D_MODEL), F32),
                   jax.ShapeDtypeStruct((8, D_MODEL), F32)],
        scratch_shapes=[pltpu.VMEM((tm, D_MODEL), F32)],
        compiler_params=_cp(("arbitrary", "arbitrary"), 56),
    )(*_hbm(h2, w_up_g, w_down, x1, target), modr, g_post)


def _rms_bwd(dxn, xn, rs):
    return rs * (dxn - xn * jnp.mean(dxn * xn, axis=-1, keepdims=True))


def _ffn_bwd(dy2, r, x1, dx2, w_up_g, w_down, modr, g_ffn):
    T = dx2.shape[0]
    tm = min(FFN_TILE, T)
    fb = FFN_BLOCK
    nj, per = _ffn_blocks()

    def body(dy2_ref, r_ref, x1_ref, dx2_ref, wu_ref, wd_ref, mod_ref, gf_ref, da_ref, dx1_ref, st_ref, dh_s):
        i, j = pl.program_id(0), pl.program_id(1)

        @pl.when((i == 0) & (j == 0))
        def _():
            st_ref[...] = jnp.zeros(st_ref.shape, F32)

        ra = jnp.sqrt(r_ref[...].astype(F32))
        da = (_dot_nt(dy2_ref[...], wd_ref[...]) * (2.0 * ra)).astype(BF16)
        da_ref[...] = da
        part = _dot_nt(da, wu_ref[0])

        @pl.when(j == 0)
        def _():
            dh_s[...] = part

        @pl.when(j > 0)
        def _():
            dh_s[...] += part

        @pl.when(j == nj - 1)
        def _():
            dh = dh_s[...]
            x1v = x1_ref[...]
            rs1 = lax.rsqrt(jnp.mean(x1v * x1v, axis=-1, keepdims=True) + RMS_EPS)
            xn = x1v * rs1
            st_ref[0:1, :] += _colsum(dh)
            st_ref[1:2, :] += _colsum(dh * (xn * gf_ref[...]))
            dsc = dh * (1.0 + mod_ref[4:5, :])
            st_ref[2:3, :] += _colsum(dsc * xn)
            dx1_ref[...] = dx2_ref[...] + _rms_bwd(dsc * gf_ref[...], xn, rs1)

    tile = pl.BlockSpec((tm, D_MODEL), lambda i, j: (i, 0))
    ftile = pl.BlockSpec((tm, fb), lambda i, j: (i, j))
    return pl.pallas_call(
        body, name="ffn_bwd", grid=(T // tm, nj),
        in_specs=[tile, ftile, tile, tile, pl.BlockSpec((1, D_MODEL, fb), lambda i, j: (j // per, 0, j % per)),
                  pl.BlockSpec((fb, D_MODEL), lambda i, j: (j, 0)), _full((6, D_MODEL)), _full((1, D_MODEL))],
        out_specs=[ftile, tile, _full((8, D_MODEL))],
        out_shape=[_big((T, D_FF), BF16), _big((T, D_MODEL), F32), jax.ShapeDtypeStruct((8, D_MODEL), F32)],
        scratch_shapes=[pltpu.VMEM((tm, D_MODEL), F32)],
        compiler_params=_cp(("arbitrary", "arbitrary"), 56),
    )(*_hbm(dy2, r, x1, dx2, w_up_g, w_down), modr, g_ffn)


def _mix_out_bwd(dx1, y, w_out, modr, g_post):
    T = dx1.shape[0]
    tm = _tok_tile(T)

    def body(dx1_ref, y_ref, w_ref, mod_ref, gp_ref, dy_ref, dcat_ref, st_ref):
        @pl.when(pl.program_id(0) == 0)
        def _():
            st_ref[...] = jnp.zeros(st_ref.shape, F32)

        dxv, yv = dx1_ref[...], y_ref[...]
        rs = lax.rsqrt(jnp.mean(yv * yv, axis=-1, keepdims=True) + RMS_EPS)
        nh = yv * rs
        st_ref[0:1, :] += _colsum(dxv * (nh * gp_ref[...]))
        dn = dxv * mod_ref[2:3, :]
        st_ref[1:2, :] += _colsum(dn * nh)
        dy = _rms_bwd(dn * gp_ref[...], nh, rs).astype(BF16)
        dy_ref[...] = dy
        dcat_ref[...] = _dot_nt(dy, w_ref[...])

    tile = pl.BlockSpec((tm, D_MODEL), lambda i: (i, 0))
    return pl.pallas_call(
        body, name="mix_out_bwd", grid=(T // tm,),
        in_specs=[tile, tile, _full((D_MODEL, D_MODEL)), _full((6, D_MODEL)), _full((1, D_MODEL))],
        out_specs=[tile, tile, _full((8, D_MODEL))],
        out_shape=[_big((T,D_MODEL), BF16), _big((T,D_MODEL), F32),
                   jax.ShapeDtypeStruct((8, D_MODEL), F32)],
        compiler_params=_cp(("arbitrary",), 40),
    )(*_hbm(dx1, y, w_out), modr, g_post)


def _mixers_bwd(p, dcat, ys, o, states, wdw, vecs, lbl, pairs_b):
    T = p.shape[0]
    tm = min(MIXB_TILE, T)
    nt = T // tm
    nch = tm // CHUNK
    hpt = tm // HALO
    nx = len(pairs_b)

    def body(*refs):
        (p_ref, ph_ref, dcat_ref, ys_ref, o_ref, st_ref, wdw_ref, bdw_ref, gain_ref, bias_ref, gout_ref,
         lbl_ref) = refs[:12]
        x_ins = refs[12:12 + nx]
        dp_ref, sb_ref, s5_ref, dw_ref = refs[12 + nx:16 + nx]
        x_outs = refs[16 + nx:16 + 2 * nx]
        (ubuf, dybuf, carry, dstate, qt_s, kt_s, kh_s, v_s, do_s, egl_s, dqt_s, dkt_s, dkh_s, dv_s, dgl_s,
         dsh, dw8, dshift, lower_s, upper_s, same_s, gmat_s, xssem, xrsem) = refs[16 + 2 * nx:]
        i = pl.program_id(0)
        tile_idx = nt - 1 - i

        @pl.when(i == 0)
        def _():
            for cp in _xchg_copies(x_ins, x_outs, xssem, xrsem):
                cp.start()
            dstate[...] = jnp.zeros(dstate.shape, F32)
            carry[...] = jnp.zeros(carry.shape, F32)
            sb_ref[...] = jnp.zeros(sb_ref.shape, F32)
            s5_ref[...] = jnp.zeros(s5_ref.shape, F32)
            dw_ref[...] = jnp.zeros(dw_ref.shape, F32)
            dw8[...] = jnp.zeros(dw8.shape, F32)
            lower_s[...], upper_s[...], same_s[...] = _chunk_masks(tm)
            gmat_s[...] = _gn_matrix()
            dsh[0:SUB, :] = jnp.zeros((SUB, CONV_CH), F32)
            dsh[SUB + tm:2 * SUB + tm, :] = jnp.zeros((SUB, CONV_CH), F32)
            ubuf[HALO + tm:HALO + tm + SUB, :] = jnp.zeros((SUB, CONV_CH), F32)

        uh = ph_ref[:, 0:CONV_CH] * _sig(ph_ref[:, CONV_CH:2 * CONV_CH])
        ubuf[0:HALO, :] = jnp.where(tile_idx > 0, uh, 0.0)
        ubuf[HALO:HALO + tm, :] = p_ref[:, 0:CONV_CH] * _sig(p_ref[:, CONV_CH:2 * CONV_CH])
        gmat = gmat_s[...]
        gain = gain_ref[...]
        yv = ys_ref[...]
        d = yv - _gmean(yv, gmat)
        rs = lax.rsqrt(_gmean(d * d, gmat) + GN_EPS)
        yn = d * rs
        z = yn * gain + bias_ref[...]
        sz = _sig(z)
        dz = dcat_ref[:, 0:CONV_CH] * (sz * (1.0 + z * (1.0 - sz)))
        dyn = dz * gain
        dyc = rs * (dyn - _gmean(dyn, gmat) - yn * _gmean(dyn * yn, gmat))
        s5_ref[0:1, :] += _colsum(dyc)
        s5_ref[1:2, :] += _colsum(dz * yn)
        s5_ref[2:3, :] += _colsum(dz)
        dybuf[tm:tm + HALO, :] = carry[...]
        dybuf[0:tm, :] = dyc
        dsh[SUB:SUB + tm, :] = dyc
        carry[...] = dyc[0:HALO, :]
        for b in range(SUB):
            dshift[...] = dsh[SUB - b:2 * SUB - b + tm, :]
            for j, off in CONV_FWD_TAPS:
                if off % SUB == b:
                    prod = dshift[...] * ubuf[off - b:off - b + tm + SUB, :]
                    dw8[j] += jnp.sum(prod.reshape((tm + SUB) // SUB, SUB, CONV_CH), axis=0)
        for r in range(tm // CONV_ROWS):
            rows = slice(r * CONV_ROWS, (r + 1) * CONV_ROWS)
            for lb_ in range(CONV_CH // LANE):
                lanes = slice(lb_ * LANE, (lb_ + 1) * LANE)
                glanes = slice(CONV_CH + lb_ * LANE, CONV_CH + (lb_ + 1) * LANE)
                acc = _tap_conv(dybuf, wdw_ref, r * CONV_ROWS, CONV_BWD_TAPS, lanes)
                val = p_ref[rows, lanes]
                sg = _sig(p_ref[rows, glanes])
                dval = acc * sg
                dgate = acc * val * (sg * (1.0 - sg))
                dp_ref[rows, lanes] = dval.astype(BF16)
                dp_ref[rows, glanes] = dgate.astype(BF16)
                sb_ref[0:1, lanes] += _colsum(dval)
                sb_ref[0:1, glanes] += _colsum(dgate)

        o0 = 2 * CONV_CH
        for h in range(N_HEADS):
            sl = slice(h * HEAD_D, (h + 1) * HEAD_D)
            gsl = slice(o0 + 3 * HGRN_W + h * HEAD_D, o0 + 3 * HGRN_W + (h + 1) * HEAD_D)
            oh = o_ref[:, sl]
            gh = p_ref[:, gsl]
            dh = dcat_ref[:, CONV_CH + h * HEAD_D:CONV_CH + (h + 1) * HEAD_D]
            gout = gout_ref[:, sl]
            rsh = lax.rsqrt(jnp.mean(oh * oh, axis=-1, keepdims=True) + RMS_EPS)
            on = oh * rsh
            sgg = _sig(gh)
            dgh = dh * (on * gout) * (sgg * (1.0 + gh * (1.0 - sgg)))
            dm = dh * (gh * sgg)
            s5_ref[3:4, sl] += _colsum(dm * on)
            do_s[:, sl] = _rms_bwd(dm * gout, on, rsh).astype(BF16)
            dp_ref[:, gsl] = dgh.astype(BF16)
            sb_ref[2:3, CONV_CH + h * HEAD_D:CONV_CH + (h + 1) * HEAD_D] += _colsum(dgh)

        lb, _ = _lower_bound(lbl_ref)
        lower, upper, same = lower_s[...], upper_s[...], same_s[...]
        pq = p_ref[:, o0:o0 + HGRN_W]
        pr = _hgrn_prep(pq, p_ref[:, o0 + HGRN_W:o0 + 2 * HGRN_W], lb, lower, same)
        qt_s[...] = pr["qt"].astype(BF16)
        kt_s[...] = pr["kt"].astype(BF16)
        kh_s[...] = pr["kh"].astype(BF16)
        v_s[...] = p_ref[:, o0 + 2 * HGRN_W:o0 + 3 * HGRN_W].astype(BF16)
        egl_s[...] = jnp.exp(pr["Gl"])
        tri = _tri()

        def chunk(it, c_):
            ci = nch - 1 - it
            r0 = pl.multiple_of(ci * CHUNK, CHUNK)
            rows = pl.ds(r0, CHUNK)
            for h in range(N_HEADS):
                ls = pl.ds(h * HEAD_D, HEAD_D)
                qc, kc, hc, vc = qt_s[rows, ls], kt_s[rows, ls], kh_s[rows, ls], v_s[rows, ls]
                dob = do_s[rows, ls]
                s0 = st_ref[ci, h]
                s0b = s0.astype(BF16)
                ds1 = dstate[h]
                ds1b = ds1.astype(BF16)
                egl = egl_s[pl.ds(r0, 1), ls]
                att = jnp.where(tri, _dot_nt(qc, kc), 0.0).astype(BF16)
                datt = jnp.where(tri, _dot_nt(dob, vc), 0.0).astype(BF16)
                dv_s[rows, ls] = _dot_tn(att, dob) + _dot_nt(hc, ds1b)
                dqt_s[rows, ls] = _dot(datt, kc) + _dot(dob, s0b)
                dkt_s[rows, ls] = _dot_tn(datt, qc)
                dkh_s[rows, ls] = _dot(vc, ds1b)
                dgl = egl * _colsum(ds1 * s0)
                dgl_s[rows, ls] = jnp.broadcast_to(dgl, (CHUNK, HEAD_D))
                dstate[h] = ds1 * egl + _dot_tn(dob, qc)
            return c_

        lax.fori_loop(0, nch, chunk, 0)
        dqt, dkt, dkh = dqt_s[...], dkt_s[...], dkh_s[...]
        dk = dkt * pr["enG"] + dkh * pr["eGlG"]
        khk = dkh * kh_s[...].astype(F32)
        dG = dqt * qt_s[...].astype(F32) - dkt * kt_s[...].astype(F32) - khk
        dlogf = _mm3(upper, dG) + _mm3(same, khk) + dgl_s[...]
        df = dlogf / pr["f"] - dk
        sf, sq = pr["sf"], pr["sq"]
        s5_ref[4:5, :] += _colsum(df * (1.0 - sf))
        dfl = df * (1.0 - lb) * (sf * (1.0 - sf))
        dq = (dqt * pr["eG"]) * (sq * (1.0 + pq * (1.0 - sq)))
        dvv = dv_s[...]
        dp_ref[:, o0:o0 + HGRN_W] = dq.astype(BF16)
        dp_ref[:, o0 + HGRN_W:o0 + 2 * HGRN_W] = dfl.astype(BF16)
        dp_ref[:, o0 + 2 * HGRN_W:o0 + 3 * HGRN_W] = dvv.astype(BF16)
        sb_ref[1:2, 0:HGRN_W] += _colsum(dq)
        sb_ref[1:2, HGRN_W:2 * HGRN_W] += _colsum(dfl)
        sb_ref[2:3, 0:HGRN_W] += _colsum(dvv)

        @pl.when(i == nt - 1)
        def _():
            for j in range(CONV_K):
                dw_ref[j:j + 1, :] = _colsum(dw8[j])
            copies = _xchg_copies(x_ins, x_outs, xssem, xrsem)
            for cp in copies:
                cp.wait_recv()
            for cp in copies:
                cp.wait_send()

    rev = lambda cols: pl.BlockSpec((tm, cols), lambda i: (nt - 1 - i, 0))
    halo = pl.BlockSpec((HALO, 2 * CONV_CH), lambda i: (jnp.maximum((nt - 1 - i) * hpt - 1, 0), 0))
    wide = lambda n: pltpu.VMEM((tm, HGRN_W), n)
    hbm = pl.BlockSpec(memory_space=pl.ANY)
    outs = pl.pallas_call(
        body, name="mixers_bwd", grid=(nt,),
        in_specs=[rev(IN_COLS), halo, rev(D_MODEL), rev(CONV_CH), rev(HGRN_W),
                  pl.BlockSpec((nch, N_HEADS, HEAD_D, HEAD_D), lambda i: (nt - 1 - i, 0, 0, 0)),
                  _full((HALO, CONV_CH))] + [_full((1, CONV_CH))] * 4 + [_full((2, HGRN_W))] + [hbm] * nx,
        out_specs=[rev(IN_COLS), _full((8, D_MODEL)), _full((8, CONV_CH)), _full((HALO, CONV_CH))] + [hbm] * nx,
        out_shape=[_big((T, IN_COLS), BF16), jax.ShapeDtypeStruct((8, D_MODEL), F32),
                   jax.ShapeDtypeStruct((8, CONV_CH), F32), jax.ShapeDtypeStruct((HALO, CONV_CH), F32)]
        + [_big(pb.shape, BF16) for pb in pairs_b],
        scratch_shapes=[pltpu.VMEM((tm + HALO + SUB, CONV_CH), F32), pltpu.VMEM((tm + HALO, CONV_CH), F32),
                        pltpu.VMEM((HALO, CONV_CH), F32), pltpu.VMEM((N_HEADS, HEAD_D, HEAD_D), F32),
                        wide(BF16), wide(BF16), wide(BF16), wide(BF16), wide(BF16),
                        wide(F32), wide(F32), wide(F32), wide(F32), wide(F32), wide(F32),
                        pltpu.VMEM((tm + 2 * SUB, CONV_CH), F32), pltpu.VMEM((HALO, SUB, CONV_CH), F32),
                        pltpu.VMEM((tm + SUB, CONV_CH), F32), pltpu.VMEM((tm, tm), BF16), pltpu.VMEM((tm, tm), BF16),
                        pltpu.VMEM((tm, tm), BF16), pltpu.VMEM((CONV_CH, CONV_CH), BF16)]
        + _xchg_sems(nx),
        compiler_params=_cp(("arbitrary",), 56),
    )(*_hbm(p, p, dcat, ys, o, states), wdw, *vecs, lbl, *[_hbm(pb) for pb in pairs_b])
    return outs[:4], outs[4:]


def _mix_in_bwd(dp, w_in_g, x, dx1, modr, g_pre, pairs_b):
    T = x.shape[0]
    tm = _tok_tile(T)
    nt = T // tm
    nb = IN_COLS // N_CHIPS
    nx = len(pairs_b)

    def body(*refs):
        dp_ref, w_ref, x_ref, dx1_ref, mod_ref, g_ref = refs[:6]
        x_ins = refs[6:6 + nx]
        gx_ref, st_ref = refs[6 + nx:8 + nx]
        x_outs = refs[8 + nx:8 + 2 * nx]
        xssem, xrsem = refs[8 + 2 * nx:]
        i = pl.program_id(0)

        @pl.when(i == 0)
        def _():
            for cp in _xchg_copies(x_ins, x_outs, xssem, xrsem):
                cp.start()
            st_ref[...] = jnp.zeros(st_ref.shape, F32)

        dh = None
        for j in range(N_CHIPS):
            part = _dot_nt(dp_ref[:, j * nb:(j + 1) * nb], w_ref[j])
            dh = part if dh is None else dh + part
        xv = x_ref[...]
        rs = lax.rsqrt(jnp.mean(xv * xv, axis=-1, keepdims=True) + RMS_EPS)
        xn = xv * rs
        st_ref[0:1, :] += _colsum(dh)
        st_ref[1:2, :] += _colsum(dh * (xn * g_ref[...]))
        dsc = dh * (1.0 + mod_ref[1:2, :])
        st_ref[2:3, :] += _colsum(dsc * xn)
        gx_ref[...] = dx1_ref[...] + _rms_bwd(dsc * g_ref[...], xn, rs)

        @pl.when(i == nt - 1)
        def _():
            copies = _xchg_copies(x_ins, x_outs, xssem, xrsem)
            for cp in copies:
                cp.wait_recv()
            for cp in copies:
                cp.wait_send()

    tile = pl.BlockSpec((tm, D_MODEL), lambda i: (i, 0))
    hbm = pl.BlockSpec(memory_space=pl.ANY)
    outs = pl.pallas_call(
        body, name="mix_in_bwd", grid=(nt,),
        in_specs=[pl.BlockSpec((tm, IN_COLS), lambda i: (i, 0)), _full((N_CHIPS, D_MODEL, nb)), tile, tile,
                  _full((6, D_MODEL)), _full((1, D_MODEL))] + [hbm] * nx,
        out_specs=[tile, _full((8, D_MODEL))] + [hbm] * nx,
        out_shape=[_big((T, D_MODEL), F32), jax.ShapeDtypeStruct((8, D_MODEL), F32)]
        + [_big(pb.shape, BF16) for pb in pairs_b],
        scratch_shapes=_xchg_sems(nx),
        compiler_params=_cp(("arbitrary",), 48),
    )(*_hbm(dp, w_in_g, x, dx1), modr, g_pre, *[_hbm(pb) for pb in pairs_b])
    return outs[:2], outs[2:]


def _weight_grad(a, b, a_blocked, b_blocked, name):
    T = a.shape[0]
    tt = min(GRAD_TILE, T)
    nt = T // tt
    ka = a.shape[1] // N_CHIPS if a_blocked else a.shape[1]
    nb = b.shape[1] // N_CHIPS if b_blocked else b.shape[1]

    def body(a_ref, b_ref, o_ref, ob_ref):
        t = pl.program_id(1)
        prod = _dot_tn(a_ref[...], b_ref[...])

        @pl.when(t == 0)
        def _():
            o_ref[0] = prod

        @pl.when(t > 0)
        def _():
            o_ref[0] += prod

        @pl.when(t == nt - 1)
        def _():
            ob_ref[0] = o_ref[0].astype(BF16)

    blk = pl.BlockSpec((1, ka, nb), lambda j, t: (j, 0, 0))
    return pl.pallas_call(
        body, name=name, grid=(N_CHIPS, nt),
        in_specs=[pl.BlockSpec((tt, ka), (lambda j, t: (t, j)) if a_blocked else (lambda j, t: (t, 0))),
                  pl.BlockSpec((tt, nb), (lambda j, t: (t, j)) if b_blocked else (lambda j, t: (t, 0)))],
        out_specs=[blk, blk],
        out_shape=[_big((N_CHIPS, ka, nb), F32), _big((N_CHIPS, ka, nb), BF16)],
        compiler_params=_cp(("arbitrary", "arbitrary"), 48),
    )(*_hbm(a, b))


R_LOSS = 0
R_FFN = 8
R_OUT = 16
R_IN = 24
R_BIN = 32
R_512 = 40
R_DW = 48
N_STAT_ROWS = 80
MOD_ROWS = (R_IN + 0, R_IN + 1, R_OUT + 0, R_FFN + 0, R_FFN + 1, R_LOSS + 1)


def _small_update(gath, params):
    names = ["b_ada", "lb_logits", "g_pre_mix", "b_in", "b_dw", "gn_gain", "gn_bias", "g_hgrn_out", "g_post_mix",
             "g_pre_ffn", "g_post_ffn"]
    flat = []
    for n in names:
        flat += list(params[n])
    n_in = 1 + len(flat)

    def body(*refs):
        g_ref = refs[0]
        prm = {n: refs[1 + 3 * k:4 + 3 * k] for k, n in enumerate(names)}
        outs = refs[n_in:]
        loss_ref, dmod_ref, dwdw_ref = outs[0], outs[1], outs[2]
        res = {n: outs[3 + 4 * k:7 + 4 * k] for k, n in enumerate(names)}
        red = g_ref[0]
        for dev in range(1, N_DEV):
            red = red + g_ref[dev]
        loss_ref[...] = jnp.broadcast_to(
            (0.5 / D_MODEL) * jnp.sum(red[R_LOSS:R_LOSS + 1, :], axis=-1, keepdims=True), loss_ref.shape)
        for dev in range(N_DEV):
            for k, r in enumerate(MOD_ROWS):
                dmod_ref[dev:dev + 1, k * D_MODEL:(k + 1) * D_MODEL] = g_ref[dev, r:r + 1, :]
        dwdw_ref[...] = red[R_DW:R_DW + HALO, 0:CONV_CH]

        def finish(name, pieces):
            w_ref, m_ref, v_ref = prm[name]
            g_out, d_out, m_out, v_out = res[name]
            for rsl, lsl, g in pieces:
                d, m2, v2 = _adam_math(w_ref[rsl, lsl], g, m_ref[rsl, lsl], v_ref[rsl, lsl])
                g_out[rsl, lsl] = g
                d_out[rsl, lsl] = d
                m_out[rsl, lsl] = m2
                v_out[rsl, lsl] = v2

        one = slice(0, 1)
        row = lambda r: red[r:r + 1, :]
        half = lambda r: red[r:r + 1, 0:CONV_CH]
        finish("b_ada", [(one, slice(k * D_MODEL, (k + 1) * D_MODEL), row(r)) for k, r in enumerate(MOD_ROWS)])
        finish("b_in", [(one, slice(k * D_MODEL, (k + 1) * D_MODEL), row(R_BIN + k)) for k in range(3)])
        finish("g_pre_mix", [(one, slice(None), row(R_IN + 2))])
        finish("g_post_mix", [(one, slice(None), row(R_OUT + 1))])
        finish("g_pre_ffn", [(one, slice(None), row(R_FFN + 2))])
        finish("g_post_ffn", [(one, slice(None), row(R_LOSS + 2))])
        finish("b_dw", [(one, slice(None), half(R_512 + 0))])
        finish("gn_gain", [(one, slice(None), half(R_512 + 1))])
        finish("gn_bias", [(one, slice(None), half(R_512 + 2))])
        finish("g_hgrn_out", [(one, slice(None), half(R_512 + 3))])
        s0, s1 = _lower_bound(prm["lb_logits"][0])
        dlb = half(R_512 + 4)
        finish("lb_logits", [(slice(0, 1), slice(None), dlb * s0 * (1.0 - s0)),
                             (slice(1, 2), slice(None), -dlb * s0 * s1)])

    vm = pl.BlockSpec(memory_space=pltpu.VMEM)
    out_shape = [jax.ShapeDtypeStruct((8, 128), F32), jax.ShapeDtypeStruct((N_DEV, 6 * D_MODEL), F32),
                 jax.ShapeDtypeStruct((HALO, CONV_CH), F32)]
    for n in names:
        out_shape += [jax.ShapeDtypeStruct(params[n][0].shape, F32)] * 4
    outs = pl.pallas_call(
        body, name="small_update", out_shape=out_shape,
        in_specs=[vm] * n_in, out_specs=[vm] * len(out_shape),
        compiler_params=_cp(None, 32),
    )(gath, *flat)
    return outs[0], outs[1], outs[2], {n: outs[3 + 4 * k:7 + 4 * k] for k, n in enumerate(names)}


def _wdw_adam(w, g, m, v):
    def body(w_ref, g_ref, m_ref, v_ref, d_out, m_out, v_out):
        d, m2, v2 = _adam_math(w_ref[...], g_ref[...], m_ref[...], v_ref[...])
        d_out[...] = d
        m_out[...] = m2
        v_out[...] = v2

    vm = pl.BlockSpec(memory_space=pltpu.VMEM)
    return pl.pallas_call(
        body, name="wdw_adam", out_shape=[jax.ShapeDtypeStruct(w.shape, F32)] * 3,
        in_specs=[vm] * 4, out_specs=[vm] * 3, compiler_params=_cp(None, 16),
    )(w, g, m, v)


def kernel(x, c, w_ada, b_ada, lb_logits, g_pre_mix, w_in, b_in, w_dw, b_dw, gn_gain, gn_bias, g_hgrn_out, w_out, g_post_mix, g_pre_ffn, w_up, w_down, g_post_ffn, loss_target, m_w_ada, m_b_ada, m_lb_logits, m_g_pre_mix, m_w_in, m_b_in, m_w_dw, m_b_dw, m_gn_gain, m_gn_bias, m_g_hgrn_out, m_w_out, m_g_post_mix, m_g_pre_ffn, m_w_up, m_w_down, m_g_post_ffn, v_w_ada, v_b_ada, v_lb_logits, v_g_pre_mix, v_w_in, v_b_in, v_w_dw, v_b_dw, v_gn_gain, v_gn_bias, v_g_hgrn_out, v_w_out, v_g_post_mix, v_g_pre_ffn, v_w_up, v_w_down, v_g_post_ffn):
    ax, ay, ac = lax.axis_index("x"), lax.axis_index("y"), lax.axis_index("c")
    chip = 2 * ax + ay
    T = x.shape[1]
    xs, tgt = x[0], loss_target[0]
    ada_cols = w_ada.shape[2]

    b_sh = lax.dynamic_slice_in_dim(b_ada, chip * ada_cols, ada_cols, axis=1)
    wdw_pad = jnp.pad(w_dw[0], ((0, HALO - CONV_K), (0, 0)))
    chip1 = jnp.reshape(chip, (1,)).astype(jnp.int32)
    place = jnp.stack([ac, chip]).astype(jnp.int32)
    _, c8, modg, wdwg, w_in_g = _ada_exchange(c, w_ada[0], b_sh, wdw_pad, _cast_own(chip1, w_in[0], "cast_w_in"))
    modr = modg.reshape(6, D_MODEL)
    wdw_all = jnp.transpose(wdwg, (1, 0, 2)).reshape(HALO, CONV_CH)
    late_bufs = [_cast_own(chip1, w[0], "cast_" + t) for w, t in ((w_out, "w_out"), (w_up, "w_up"), (w_down, "w_down"))]
    vec = (b_dw, gn_gain, gn_bias, g_hgrn_out)

    p, h1 = _mix_in_fwd(xs, modr, g_pre_mix, w_in_g, b_in)
    (cat, ys, o, states), (w_out_g, w_up_g, w_down_g) = _mixers_fwd(p, wdw_all, vec, lb_logits, late_bufs)
    w_out_f = w_out_g.reshape(D_MODEL, D_MODEL)
    w_down_f = w_down_g.reshape(D_FF, D_MODEL)
    y, x1, h2 = _mix_out_fwd(cat, w_out_f, xs, modr, g_post_mix, g_pre_ffn)
    r, dy2, dx2, st_loss = _ffn_fwd(h2, w_up_g, w_down_f, x1, tgt, modr, g_post_ffn)

    def pair_stage(grads, tags, name):
        got = _pair_swap([gb for _, gb in grads], name)
        return [_pair_sum(place, g, o_, "pair_sum_" + t) for (g, _), o_, t in zip(grads, got, tags)]

    da, dx1, st_ffn = _ffn_bwd(dy2, r, x1, dx2, w_up_g, w_down_f, modr, g_pre_ffn)
    g_up = _weight_grad(h2, da, False, True, "grad_w_up")
    g_down = _weight_grad(r, dy2, True, False, "grad_w_down")
    dy, dcat, st_out = _mix_out_bwd(dx1, y, w_out_f, modr, g_post_mix)
    g_out = _weight_grad(cat, dy, True, False, "grad_w_out")
    early = pair_stage([g_out, g_up, g_down], ["w_out", "w_up", "w_down"], "pair_swap_early")
    (dp, st_bin, st_512, dwdw), got_early = _mixers_bwd(p, dcat, ys, o, states, wdw_all, vec, lb_logits,
                                                        [pb for _, pb in early])
    g_in = _weight_grad(h1, dp, False, True, "grad_w_in")
    late = pair_stage([g_in], ["w_in"], "pair_swap_late")
    (grad_x, st_in), got_late = _mix_in_bwd(dp, w_in_g, xs, dx1, modr, g_pre_mix, [late[0][1]])
    fulls = [_chip_sum(place, pf, gb, "chip_sum_" + t)
             for (pf, _), gb, t in zip(late + early, list(got_late) + list(got_early), ["w_in", "w_out", "w_up", "w_down"])]

    pad_lanes = lambda s: jnp.pad(s, ((0, 0), (0, D_MODEL - s.shape[1])))
    stats = jnp.concatenate([st_loss, st_ffn, st_out, st_in, st_bin, pad_lanes(st_512), pad_lanes(dwdw)], axis=0)
    (g_w_in, g_w_out, g_w_up, g_w_down), gath = _final_exchange(fulls, stats)
    small = {"b_ada": (b_ada, m_b_ada, v_b_ada), "lb_logits": (lb_logits, m_lb_logits, v_lb_logits),
             "g_pre_mix": (g_pre_mix, m_g_pre_mix, v_g_pre_mix), "b_in": (b_in, m_b_in, v_b_in),
             "b_dw": (b_dw, m_b_dw, v_b_dw), "gn_gain": (gn_gain, m_gn_gain, v_gn_gain),
             "gn_bias": (gn_bias, m_gn_bias, v_gn_bias), "g_hgrn_out": (g_hgrn_out, m_g_hgrn_out, v_g_hgrn_out),
             "g_post_mix": (g_post_mix, m_g_post_mix, v_g_post_mix), "g_pre_ffn": (g_pre_ffn, m_g_pre_ffn, v_g_pre_ffn),
             "g_post_ffn": (g_post_ffn, m_g_post_ffn, v_g_post_ffn)}
    loss_t, dmod_all, dwdw_sum, sres = _small_update(gath, small)
    loss = loss_t[0, 0]

    res = dict(sres)
    dmod_sh = lax.dynamic_slice_in_dim(dmod_all, chip * ada_cols, ada_cols, axis=1)
    res["w_ada"] = [t[None] for t in _ada_grad_adam(jnp.transpose(c8), dmod_sh, w_ada[0], m_w_ada[0], v_w_ada[0])]
    g_wdw = lax.dynamic_slice_in_dim(dwdw_sum, chip * HEAD_D, HEAD_D, axis=1)[:CONV_K][None]
    res["w_dw"] = [g_wdw] + list(_wdw_adam(w_dw, g_wdw, m_w_dw, v_w_dw))
    for name, g, w, m, v in (("w_in", g_w_in, w_in, m_w_in, v_w_in), ("w_out", g_w_out, w_out, m_w_out, v_w_out),
                             ("w_up", g_w_up, w_up, m_w_up, v_w_up), ("w_down", g_w_down, w_down, m_w_down, v_w_down)):
        d, m2, v2 = _adam_big(w[0], g, m[0], v[0], "adam_" + name)
        res[name] = [g[None], d[None], m2[None], v2[None]]

    order = ["w_ada", "b_ada", "lb_logits", "g_pre_mix", "w_in", "b_in", "w_dw", "b_dw", "gn_gain", "gn_bias",
             "g_hgrn_out", "w_out", "g_post_mix", "g_pre_ffn", "w_up", "w_down", "g_post_ffn"]
    out = [loss, grad_x[None]]
    for k in range(4):
        out += [res[n][k] for n in order]
    return tuple(out)
```

```python
import jax
import jax.numpy as jnp
from jax import lax
from jax.experimental import pallas as pl
from jax.experimental.pallas import tpu as pltpu

F32, BF16 = jnp.float32, jnp.bfloat16
D_MODEL = 1024
CONV_CH = 512
HGRN_W = 512
N_HEADS = 4
HEAD_D = 128
CONV_K = 31
GN_GROUP = 64
GN_SHIFT = 6
IN_COLS = 3072
D_FF = 4096
CHUNK = 64
CHUNK_SHIFT = 6
N_CHIPS = 4
N_DEV = 8
RMS_EPS = 1e-6
GN_EPS = 1e-5
ADAM_LR, ADAM_B1, ADAM_B2, ADAM_EPS, ADAM_WD, ADAM_STEP = 0.001, 0.9, 0.999, 1e-08, 0.01, 10
TOK_TILE = 512
MIXB_TILE = 256
FFN_TILE = 1024
FFN_BLOCK = 512
GRAD_TILE = 2048
HALO = 32
SUB = 8
LANE = 128
CONV_ROWS = 128
MIB = 1 << 20
MESH = pl.DeviceIdType.MESH
OTHER_CHIPS = ((0, 1), (1, 0), (1, 1))


def _cp(sem=None, vmem_mib=48):
    return pltpu.CompilerParams(dimension_semantics=sem, vmem_limit_bytes=vmem_mib * MIB)


def _dot(a, b):
    return jnp.dot(a, b, preferred_element_type=F32)


def _dot_nt(a, b):
    return lax.dot_general(a, b, (((1,), (1,)), ((), ())), preferred_element_type=F32)


def _dot_tn(a, b):
    return lax.dot_general(a, b, (((0,), (0,)), ((), ())), preferred_element_type=F32)


def _sig(v):
    return 0.5 * jnp.tanh(0.5 * v) + 0.5


def _colsum(v):
    return jnp.sum(v, axis=0, keepdims=True)


def _flip(v, b):
    return 1 - v if b else v


def _rcopy(src, dst, ssem, rsem, dev):
    return pltpu.make_async_remote_copy(src_ref=src, dst_ref=dst, send_sem=ssem, recv_sem=rsem,
                                        device_id=dev, device_id_type=MESH)


def _place():
    return lax.axis_index("x"), lax.axis_index("y"), lax.axis_index("c")


def _full(shape):
    return pl.BlockSpec(shape, lambda *_: (0,) * len(shape))


def _big(shape, dtype):
    return pltpu.HBM(shape, dtype)


def _hbm(*arrays):
    out = [pltpu.with_memory_space_constraint(a, pltpu.HBM) for a in arrays]
    return out[0] if len(out) == 1 else out


def _split2(v):
    hi = v.astype(BF16)
    lo = (v - hi.astype(F32)).astype(BF16)
    return hi, lo


def _split3(v):
    h1 = v.astype(BF16)
    r1 = v - h1.astype(F32)
    h2 = r1.astype(BF16)
    h3 = (r1 - h2.astype(F32)).astype(BF16)
    return h1, h2, h3


def _mm3s(mat, parts):
    h1, h2, h3 = parts
    return _dot(mat, h1) + _dot(mat, h2) + _dot(mat, h3)


def _mm3(mat, v):
    return _mm3s(mat, _split3(v))


def _gn_matrix():
    r = lax.broadcasted_iota(jnp.int32, (CONV_CH, CONV_CH), 0) >> GN_SHIFT
    c = lax.broadcasted_iota(jnp.int32, (CONV_CH, CONV_CH), 1) >> GN_SHIFT
    return jnp.where(r == c, 1.0 / GN_GROUP, 0.0).astype(BF16)


def _gmean(v, gmat):
    hi, lo = _split2(v)
    return _dot(hi, gmat) + _dot(lo, gmat)


def _chunk_masks(tm):
    r = lax.broadcasted_iota(jnp.int32, (tm, tm), 0)
    c = lax.broadcasted_iota(jnp.int32, (tm, tm), 1)
    same = (r >> CHUNK_SHIFT) == (c >> CHUNK_SHIFT)
    one = lambda m: jnp.where(m, 1.0, 0.0).astype(BF16)
    return one(same & (c <= r)), one(same & (c >= r)), one(same)


def _tri():
    return lax.broadcasted_iota(jnp.int32, (CHUNK, CHUNK), 0) >= lax.broadcasted_iota(jnp.int32, (CHUNK, CHUNK), 1)


def _lower_bound(lbl_ref):
    l0, l1 = lbl_ref[0:1, :], lbl_ref[1:2, :]
    mx = jnp.maximum(l0, l1)
    e0, e1 = jnp.exp(l0 - mx), jnp.exp(l1 - mx)
    return e0 / (e0 + e1), e1 / (e0 + e1)


CONV_FWD_TAPS = tuple((j, HALO - (CONV_K - 1) + j) for j in range(CONV_K))
CONV_BWD_TAPS = tuple((j, (CONV_K - 1) - j) for j in range(CONV_K))


def _tap_conv(src_ref, w_ref, row0, taps, lanes):
    acc = None
    for b in range(SUB):
        pb = None
        for j, off in taps:
            if off % SUB == b:
                lo = row0 + off - b
                term = w_ref[j:j + 1, lanes] * src_ref[lo:lo + CONV_ROWS + SUB, lanes]
                pb = term if pb is None else pb + term
        if pb is not None:
            sh = pb[b:b + CONV_ROWS, :]
            acc = sh if acc is None else acc + sh
    return acc


def _hgrn_prep(pq, pf, lb, lower, same):
    sq = _sig(pq)
    qf = pq * sq
    sf = _sig(pf)
    f = lb + (1.0 - lb) * sf
    logf = jnp.log(f)
    k = 1.0 - f
    parts = _split3(logf)
    G = _mm3s(lower, parts)
    Gl = _mm3s(same, parts)
    eG, enG, eGlG = jnp.exp(G), jnp.exp(-G), jnp.exp(Gl - G)
    return dict(sq=sq, sf=sf, f=f, Gl=Gl, eG=eG, enG=enG, eGlG=eGlG, qt=qf * eG, kt=k * enG, kh=k * eGlG)


def _ada_exchange(c_row, w_ada, b_sh, wdw_pad, w_in_buf):
    ncol = w_ada.shape[1]

    def body(c_ref, w_ref, b_ref, wdw_ref, win_in, call_ref, c8_ref, modg_ref, wdwg_ref, win_ref,
             rows_s, sa, ra, sw, rw, sm, rm, gssem, grsem):
        _gather_start([win_ref], gssem, grsem)
        x, y, c = _place()
        me = 4 * x + 2 * y + c
        chip = 2 * x + y
        cv = c_ref[...]
        call_ref[me] = cv * _sig(cv)
        wdwg_ref[chip] = wdw_ref[...]
        sends = []
        for m in range(1, N_DEV):
            peer = (_flip(x, m >> 2), _flip(y, (m >> 1) & 1), _flip(c, m & 1))
            cp = _rcopy(call_ref.at[me], call_ref.at[me], sa.at[m - 1], ra.at[m - 1], peer)
            cp.start()
            sends.append(cp)
        for k, (fx, fy) in enumerate(OTHER_CHIPS):
            peer = (_flip(x, fx), _flip(y, fy), c)
            cp = _rcopy(wdwg_ref.at[chip], wdwg_ref.at[chip], sw.at[k], rw.at[k], peer)
            cp.start()
            sends.append(cp)
        for m in range(1, N_DEV):
            peer = (_flip(x, m >> 2), _flip(y, (m >> 1) & 1), _flip(c, m & 1))
            pid = 4 * peer[0] + 2 * peer[1] + peer[2]
            _rcopy(call_ref.at[pid], call_ref.at[pid], sa.at[m - 1], ra.at[m - 1], peer).wait_recv()
        for b in range(N_DEV):
            c8_ref[b:b + 1, :] = call_ref[b]
        mod_all = _dot(c8_ref[...].astype(BF16), w_ref[...].astype(BF16)) + b_ref[...]
        for b in range(N_DEV):
            rows_s[b] = mod_all[b:b + 1, :]
        modg_ref[chip] = rows_s[me]
        for k, (fx, fy) in enumerate(OTHER_CHIPS):
            peer = (_flip(x, fx), _flip(y, fy), c)
            pid = 4 * peer[0] + 2 * peer[1] + peer[2]
            cp = _rcopy(rows_s.at[pid], modg_ref.at[chip], sm.at[k], rm.at[k], peer)
            cp.start()
            sends.append(cp)
        for k, (fx, fy) in enumerate(OTHER_CHIPS):
            peer = (_flip(x, fx), _flip(y, fy), c)
            pchip = 2 * peer[0] + peer[1]
            _rcopy(rows_s.at[0], modg_ref.at[pchip], sm.at[k], rm.at[k], peer).wait_recv()
            _rcopy(wdwg_ref.at[pchip], wdwg_ref.at[pchip], sw.at[k], rw.at[k], peer).wait_recv()
        for cp in sends:
            cp.wait_send()
        _gather_finish([win_ref], gssem, grsem)

    vm = pl.BlockSpec(memory_space=pltpu.VMEM)
    hbm = pl.BlockSpec(memory_space=pl.ANY)
    return pl.pallas_call(
        body, name="ada_exchange",
        out_shape=[jax.ShapeDtypeStruct((N_DEV, 1, D_MODEL), F32), jax.ShapeDtypeStruct((N_DEV, D_MODEL), F32),
                   jax.ShapeDtypeStruct((N_CHIPS, 1, ncol), F32), jax.ShapeDtypeStruct((N_CHIPS, HALO, HEAD_D), F32),
                   _big(w_in_buf.shape, BF16)],
        in_specs=[vm] * 4 + [hbm], out_specs=[vm] * 4 + [hbm], input_output_aliases={4: 4},
        scratch_shapes=[pltpu.VMEM((N_DEV, 1, ncol), F32),
                        pltpu.SemaphoreType.DMA((N_DEV - 1,)), pltpu.SemaphoreType.DMA((N_DEV - 1,)),
                        pltpu.SemaphoreType.DMA((3,)), pltpu.SemaphoreType.DMA((3,)),
                        pltpu.SemaphoreType.DMA((3,)), pltpu.SemaphoreType.DMA((3,))] + _gather_sems(1),
        compiler_params=_cp(None, 32),
    )(c_row, w_ada, b_sh, wdw_pad, _hbm(w_in_buf))


def _cast_own(chip1, shard, name):
    rows, cols = shard.shape
    tr = _row_tile(rows)

    def body(ch_ref, s_ref, o_ref):
        o_ref[0] = s_ref[...].astype(BF16)

    gs = pltpu.PrefetchScalarGridSpec(
        num_scalar_prefetch=1, grid=(rows // tr,),
        in_specs=[pl.BlockSpec((tr, cols), lambda i, ch: (i, 0))],
        out_specs=pl.BlockSpec((1, tr, cols), lambda i, ch: (ch[0], i, 0)))
    return pl.pallas_call(
        body, name=name, grid_spec=gs, out_shape=_big((N_CHIPS, rows, cols), BF16),
        compiler_params=_cp(("arbitrary",), 32),
    )(chip1, _hbm(shard))


def _slab(buf, ch, core):
    hs = buf.shape[1] // 2
    return buf.at[ch, pl.ds(core * hs, hs), :]


def _gather_start(bufs, ssem, rsem):
    x, y, c = _place()
    chip = 2 * x + y
    for k, (fx, fy) in enumerate(OTHER_CHIPS):
        peer = (_flip(x, fx), _flip(y, fy), c)
        for t, buf in enumerate(bufs):
            _rcopy(_slab(buf, chip, c), _slab(buf, chip, c), ssem.at[t * 3 + k], rsem.at[t * 3 + k], peer).start()


def _gather_finish(bufs, ssem, rsem):
    nt = len(bufs)
    x, y, c = _place()
    chip = 2 * x + y
    sibling = (x, y, 1 - c)
    for k, (fx, fy) in enumerate(OTHER_CHIPS):
        peer = (_flip(x, fx), _flip(y, fy), c)
        pchip = 2 * peer[0] + peer[1]
        for t, buf in enumerate(bufs):
            _rcopy(_slab(buf, pchip, c), _slab(buf, pchip, c), ssem.at[t * 3 + k], rsem.at[t * 3 + k], peer).wait_recv()
            _rcopy(_slab(buf, pchip, c), _slab(buf, pchip, c), ssem.at[3 * nt + t * 3 + k],
                   rsem.at[3 * nt + t * 3 + k], sibling).start()
    for k, (fx, fy) in enumerate(OTHER_CHIPS):
        peer = (_flip(x, fx), _flip(y, fy), c)
        pchip = 2 * peer[0] + peer[1]
        for t, buf in enumerate(bufs):
            _rcopy(_slab(buf, pchip, 1 - c), _slab(buf, pchip, 1 - c), ssem.at[3 * nt + t * 3 + k],
                   rsem.at[3 * nt + t * 3 + k], sibling).wait_recv()
            _rcopy(_slab(buf, chip, c), _slab(buf, chip, c), ssem.at[t * 3 + k], rsem.at[t * 3 + k], peer).wait_send()
            _rcopy(_slab(buf, pchip, c), _slab(buf, pchip, c), ssem.at[3 * nt + t * 3 + k],
                   rsem.at[3 * nt + t * 3 + k], sibling).wait_send()


def _gather_sems(nt):
    return [pltpu.SemaphoreType.DMA((6 * nt,)), pltpu.SemaphoreType.DMA((6 * nt,))]


def _pair_copies(ins, outs, ssem, rsem):
    x, y, c = _place()
    copies = []
    for t in range(len(ins)):
        hs = ins[t].shape[1] // 2
        copies.append(_rcopy(ins[t].at[:, pl.ds((1 - c) * hs, hs), :], outs[t], ssem.at[t], rsem.at[t], (x, y, 1 - c)))
    return copies


def _pair_shapes(grads):
    return [_big((g.shape[0], g.shape[1] // 2, g.shape[2]), g.dtype) for g in grads]


def _pair_sems(nt):
    return [pltpu.SemaphoreType.DMA((nt,)), pltpu.SemaphoreType.DMA((nt,))]


def _pair_swap(grads, name):
    nt = len(grads)
    hbm = pl.BlockSpec(memory_space=pl.ANY)

    def body(*refs):
        copies = _pair_copies(refs[:nt], refs[nt:2 * nt], refs[2 * nt], refs[2 * nt + 1])
        for cp in copies:
            cp.start()
        for cp in copies:
            cp.wait_recv()
        for cp in copies:
            cp.wait_send()

    return pl.pallas_call(
        body, name=name, out_shape=_pair_shapes(grads), in_specs=[hbm] * nt, out_specs=[hbm] * nt,
        scratch_shapes=_pair_sems(nt),
    )(*[_hbm(g) for g in grads])


def _xchg_copies(ins, outs, ssem, rsem):
    x, y, c = _place()
    copies = []
    for k, (fx, fy) in enumerate(OTHER_CHIPS):
        peer = (_flip(x, fx), _flip(y, fy), c)
        for t in range(len(ins)):
            copies.append(_rcopy(ins[t].at[k], outs[t].at[k], ssem.at[t * 3 + k], rsem.at[t * 3 + k], peer))
    return copies


def _xchg_sems(nt):
    return [pltpu.SemaphoreType.DMA((3 * nt,)), pltpu.SemaphoreType.DMA((3 * nt,))]


def _final_exchange(fulls, stats):
    nt = len(fulls)
    rows, cols = stats.shape
    hbm = pl.BlockSpec(memory_space=pl.ANY)
    vm = pl.BlockSpec(memory_space=pltpu.VMEM)

    def body(*refs):
        ins, s_ref = refs[:nt], refs[nt]
        outs, g_ref = refs[nt + 1:2 * nt + 1], refs[2 * nt + 1]
        hssem, hrsem, ssem, rsem = refs[2 * nt + 2:]
        x, y, c = _place()
        me, sibling = (x, y, c), (x, y, 1 - c)
        halves = []
        for t in range(nt):
            hs = ins[t].shape[0] // 2
            mine = pl.ds(c * hs, hs)
            cp = _rcopy(ins[t].at[mine, :], outs[t].at[mine, :], hssem.at[t], hrsem.at[t], sibling)
            cp.start()
            halves.append(cp)

        chips = [(_flip(x, fx), _flip(y, fy)) for fx, fy in OTHER_CHIPS]

        def blk(px, py, pc):
            return g_ref.at[4 * px + 2 * py + pc]

        def copy(k, block, to, src=None):
            return _rcopy(blk(*block) if src is None else src, blk(*block), ssem.at[k], rsem.at[k], to)

        g_ref[4 * x + 2 * y + c] = s_ref[...]
        first = [copy(0, me, sibling, src=s_ref)]
        first += [copy(1 + j, me, (*chip, c), src=s_ref) for j, chip in enumerate(chips)]
        for cp in first:
            cp.start()
        passed = [copy(4 + j, (*chip, c), sibling) for j, chip in enumerate(chips)]
        for j, chip in enumerate(chips):
            copy(1 + j, (*chip, c), me).wait_recv()
            passed[j].start()
        copy(0, sibling, me).wait_recv()
        for j, chip in enumerate(chips):
            copy(4 + j, (*chip, 1 - c), me).wait_recv()
        for t in range(nt):
            hs = ins[t].shape[0] // 2
            other = pl.ds((1 - c) * hs, hs)
            _rcopy(ins[t].at[other, :], outs[t].at[other, :], hssem.at[t], hrsem.at[t], sibling).wait_recv()
        for cp in first + passed + halves:
            cp.wait_send()

    outs = pl.pallas_call(
        body, name="final_exchange",
        out_shape=[_big(f.shape, F32) for f in fulls] + [jax.ShapeDtypeStruct((N_DEV, rows, cols), F32)],
        in_specs=[hbm] * nt + [vm], out_specs=[hbm] * nt + [vm],
        input_output_aliases={t: t for t in range(nt)},
        scratch_shapes=[pltpu.SemaphoreType.DMA((nt,)), pltpu.SemaphoreType.DMA((nt,)),
                        pltpu.SemaphoreType.DMA((7,)), pltpu.SemaphoreType.DMA((7,))],
        compiler_params=_cp(None, 32),
    )(*[_hbm(f) for f in fulls], stats)
    return outs[:nt], outs[nt]


def _row_tile(rows):
    return min(rows, 256)


def _pair_sum(place, grad, got, name):
    nb, hs, cols = got.shape
    tr = _row_tile(hs)
    nr = hs // tr

    def body(pl_ref, g_ref, o_ref, pf_ref, pb_ref):
        j = pl.program_id(1)
        s = g_ref[0] + o_ref[0].astype(F32)

        @pl.when(j == 0)
        def _():
            pf_ref[...] = s

        @pl.when(j > 0)
        def _():
            pb_ref[0] = s.astype(BF16)

    gs = pltpu.PrefetchScalarGridSpec(
        num_scalar_prefetch=1, grid=(nr, nb),
        in_specs=[pl.BlockSpec((1, tr, cols), lambda i, j, p: (p[1] ^ j, p[0] * nr + i, 0)),
                  pl.BlockSpec((1, tr, cols), lambda i, j, p: (p[1] ^ j, i, 0))],
        out_specs=[pl.BlockSpec((tr, cols), lambda i, j, p: (i, 0)),
                   pl.BlockSpec((1, tr, cols), lambda i, j, p: (jnp.maximum(j - 1, 0), i, 0))])
    return pl.pallas_call(
        body, name=name, grid_spec=gs,
        out_shape=[_big((hs, cols), F32), _big((nb - 1, hs, cols), BF16)],
        compiler_params=_cp(("arbitrary", "arbitrary"), 32),
    )(place, *_hbm(grad, got))


def _chip_sum(place, pair_f, got_b, name):
    nb, hs, cols = got_b.shape
    tr = _row_tile(hs)
    nr = hs // tr

    def body(pl_ref, pf_ref, gb_ref, o_ref):
        acc = pf_ref[...]
        for k in range(nb):
            acc = acc + gb_ref[k].astype(F32)
        o_ref[...] = acc

    gs = pltpu.PrefetchScalarGridSpec(
        num_scalar_prefetch=1, grid=(nr,),
        in_specs=[pl.BlockSpec((tr, cols), lambda i, p: (i, 0)),
                  pl.BlockSpec((nb, tr, cols), lambda i, p: (0, i, 0))],
        out_specs=pl.BlockSpec((tr, cols), lambda i, p: (p[0] * nr + i, 0)))
    return pl.pallas_call(
        body, name=name, grid_spec=gs,
        out_shape=_big((2 * hs, cols), F32),
        compiler_params=_cp(("arbitrary",), 32),
    )(place, *_hbm(pair_f, got_b))


def _adam_math(w, g, m, v):
    m2 = ADAM_B1 * m + (1.0 - ADAM_B1) * g
    v2 = ADAM_B2 * v + (1.0 - ADAM_B2) * (g * g)
    m_hat = m2 / (1.0 - ADAM_B1 ** ADAM_STEP)
    v_hat = v2 / (1.0 - ADAM_B2 ** ADAM_STEP)
    delta = -ADAM_LR * (m_hat / (jnp.sqrt(v_hat) + ADAM_EPS) + ADAM_WD * w)
    return delta, m2, v2


def _adam_big(w, g, m, v, name):
    rows, cols = w.shape
    tr = _row_tile(rows)

    def body(w_ref, g_ref, m_ref, v_ref, d_out, m_out, v_out):
        d, m2, v2 = _adam_math(w_ref[...], g_ref[...], m_ref[...], v_ref[...])
        d_out[...] = d
        m_out[...] = m2
        v_out[...] = v2

    spec = pl.BlockSpec((tr, cols), lambda i: (i, 0))
    return pl.pallas_call(
        body, name=name, grid=(rows // tr,), in_specs=[spec] * 4, out_specs=[spec] * 3,
        out_shape=[_big(w.shape, F32)] * 3,
        compiler_params=_cp(("arbitrary",), 32),
    )(*_hbm(w, g, m, v))


def _ada_grad_adam(c8t, dmod_sh, w, m, v):
    rows, cols = w.shape
    tr = _row_tile(rows)

    def body(ct_ref, dm_ref, w_ref, m_ref, v_ref, g_out, d_out, m_out, v_out):
        g = None
        for b in range(N_DEV):
            term = ct_ref[:, b:b + 1] * dm_ref[b:b + 1, :]
            g = term if g is None else g + term
        d, m2, v2 = _adam_math(w_ref[...], g, m_ref[...], v_ref[...])
        g_out[...] = g
        d_out[...] = d
        m_out[...] = m2
        v_out[...] = v2

    spec = pl.BlockSpec((tr, cols), lambda i: (i, 0))
    return pl.pallas_call(
        body, name="ada_grad_adam", grid=(rows // tr,),
        in_specs=[pl.BlockSpec((tr, N_DEV), lambda i: (i, 0)), _full((N_DEV, cols)), spec, spec, spec],
        out_specs=[spec] * 4, out_shape=[_big(w.shape, F32)] * 4,
        compiler_params=_cp(("arbitrary",), 32),
    )(c8t, dmod_sh, *_hbm(w, m, v))


def _tok_tile(t):
    return min(TOK_TILE, t)


def _mix_in_fwd(x, modr, g_pre, w_in_g, b_in):
    T = x.shape[0]
    tm = _tok_tile(T)
    nb = IN_COLS // N_CHIPS

    def body(x_ref, mod_ref, g_ref, w_ref, b_ref, p_ref, h_ref):
        xv = x_ref[...]
        rstd = lax.rsqrt(jnp.mean(xv * xv, axis=-1, keepdims=True) + RMS_EPS)
        h = (xv * rstd) * g_ref[...] * (1.0 + mod_ref[1:2, :]) + mod_ref[0:1, :]
        hb = h.astype(BF16)
        h_ref[...] = hb
        for j in range(N_CHIPS):
            p_ref[:, j * nb:(j + 1) * nb] = _dot(hb, w_ref[j]) + b_ref[:, j * nb:(j + 1) * nb]

    return pl.pallas_call(
        body, name="mix_in_fwd", grid=(T // tm,),
        in_specs=[pl.BlockSpec((tm, D_MODEL), lambda i: (i, 0)), _full((6, D_MODEL)), _full((1, D_MODEL)),
                  _full((N_CHIPS, D_MODEL, nb)), _full((1, IN_COLS))],
        out_specs=[pl.BlockSpec((tm, IN_COLS), lambda i: (i, 0)), pl.BlockSpec((tm, D_MODEL), lambda i: (i, 0))],
        out_shape=[_big((T,IN_COLS), F32), _big((T,D_MODEL), BF16)],
        compiler_params=_cp(("arbitrary",), 48),
    )(_hbm(x), modr, g_pre, _hbm(w_in_g), b_in)


def _mixers_fwd(p, wdw, vecs, lbl, gbufs):
    T = p.shape[0]
    tm = _tok_tile(T)
    nt = T // tm
    nch = tm // CHUNK
    ng = len(gbufs)

    def body(*refs):
        p_ref, wdw_ref, bdw_ref, gain_ref, bias_ref, gout_ref, lbl_ref = refs[:7]
        cat_ref, ys_ref, o_ref, st_ref = refs[7 + ng:11 + ng]
        gout_bufs = refs[11 + ng:11 + 2 * ng]
        ubuf, state, qt_s, kt_s, kh_s, v_s, egl_s, lower_s, same_s, gmat_s, gssem, grsem = refs[11 + 2 * ng:]
        i = pl.program_id(0)

        @pl.when(i == 0)
        def _():
            _gather_start(gout_bufs, gssem, grsem)
            lower_s[...], _, same_s[...] = _chunk_masks(tm)
            gmat_s[...] = _gn_matrix()
            state[...] = jnp.zeros(state.shape, F32)
            ubuf[0:HALO, :] = jnp.zeros((HALO, CONV_CH), F32)
            ubuf[HALO + tm:HALO + tm + SUB, :] = jnp.zeros((SUB, CONV_CH), F32)

        @pl.when(i > 0)
        def _():
            ubuf[0:HALO, :] = ubuf[tm:tm + HALO, :]

        ubuf[HALO:HALO + tm, :] = p_ref[:, 0:CONV_CH] * _sig(p_ref[:, CONV_CH:2 * CONV_CH])
        for r in range(tm // CONV_ROWS):
            rows = slice(r * CONV_ROWS, (r + 1) * CONV_ROWS)
            for lb_ in range(CONV_CH // LANE):
                lanes = slice(lb_ * LANE, (lb_ + 1) * LANE)
                ys_ref[rows, lanes] = bdw_ref[:, lanes] + _tap_conv(ubuf, wdw_ref, r * CONV_ROWS, CONV_FWD_TAPS, lanes)
        gmat = gmat_s[...]
        yv = ys_ref[...]
        d = yv - _gmean(yv, gmat)
        rs = lax.rsqrt(_gmean(d * d, gmat) + GN_EPS)
        z = d * rs * gain_ref[...] + bias_ref[...]
        cat_ref[:, 0:CONV_CH] = (z * _sig(z)).astype(BF16)

        lb, _ = _lower_bound(lbl_ref)
        lower, same = lower_s[...], same_s[...]
        o0 = 2 * CONV_CH
        pr = _hgrn_prep(p_ref[:, o0:o0 + HGRN_W], p_ref[:, o0 + HGRN_W:o0 + 2 * HGRN_W], lb, lower, same)
        qt_s[...] = pr["qt"].astype(BF16)
        kt_s[...] = pr["kt"].astype(BF16)
        kh_s[...] = pr["kh"].astype(BF16)
        v_s[...] = p_ref[:, o0 + 2 * HGRN_W:o0 + 3 * HGRN_W].astype(BF16)
        egl_s[...] = jnp.exp(pr["Gl"])
        tri = _tri()

        def chunk(ci, carry):
            r0 = pl.multiple_of(ci * CHUNK, CHUNK)
            rows = pl.ds(r0, CHUNK)
            for h in range(N_HEADS):
                ls = pl.ds(h * HEAD_D, HEAD_D)
                qc, kc, hc, vc = qt_s[rows, ls], kt_s[rows, ls], kh_s[rows, ls], v_s[rows, ls]
                s0 = state[h]
                s0b = s0.astype(BF16)
                st_ref[ci, h] = s0
                att =jnp.where(tri, _dot_nt(qc, kc), 0.0).astype(BF16)
                o_ref[rows, ls] = _dot(att, vc) + _dot_nt(qc, s0b)
                state[h] = s0 * egl_s[pl.ds(r0, 1), ls] + _dot_tn(vc, hc)
            return carry

        lax.fori_loop(0, nch, chunk, 0)
        for h in range(N_HEADS):
            sl = slice(h * HEAD_D, (h + 1) * HEAD_D)
            oh = o_ref[:, sl]
            gh = p_ref[:, o0 + 3 * HGRN_W + h * HEAD_D:o0 + 3 * HGRN_W + (h + 1) * HEAD_D]
            rsh = lax.rsqrt(jnp.mean(oh * oh, axis=-1, keepdims=True) + RMS_EPS)
            hg = (oh * rsh) * gout_ref[:, sl] * (gh * _sig(gh))
            cat_ref[:, CONV_CH + h * HEAD_D:CONV_CH + (h + 1) * HEAD_D] = hg.astype(BF16)

        @pl.when(i == nt - 1)
        def _():
            _gather_finish(gout_bufs, gssem, grsem)

    tile = lambda cols: pl.BlockSpec((tm, cols), lambda i: (i, 0))
    hbm = pl.BlockSpec(memory_space=pl.ANY)
    n_in = 7
    outs = pl.pallas_call(
        body, name="mixers_fwd", grid=(nt,),
        in_specs=[tile(IN_COLS), _full((HALO, CONV_CH))] + [_full((1, CONV_CH))] * 4 + [_full((2, HGRN_W))]
        + [hbm] * ng,
        out_specs=[tile(D_MODEL), tile(CONV_CH), tile(HGRN_W),
                   pl.BlockSpec((nch, N_HEADS, HEAD_D, HEAD_D), lambda i: (i, 0, 0, 0))] + [hbm] * ng,
        out_shape=[_big((T, D_MODEL), BF16), _big((T, CONV_CH), F32), _big((T, HGRN_W), F32),
                   _big((T // CHUNK, N_HEADS, HEAD_D, HEAD_D), F32)] + [_big(b.shape, BF16) for b in gbufs],
        input_output_aliases={n_in + t: 4 + t for t in range(ng)},
        scratch_shapes=[pltpu.VMEM((tm + HALO + SUB, CONV_CH), F32), pltpu.VMEM((N_HEADS, HEAD_D, HEAD_D), F32),
                        pltpu.VMEM((tm, HGRN_W), BF16), pltpu.VMEM((tm, HGRN_W), BF16),
                        pltpu.VMEM((tm, HGRN_W), BF16), pltpu.VMEM((tm, HGRN_W), BF16),
                        pltpu.VMEM((tm, HGRN_W), F32), pltpu.VMEM((tm, tm), BF16), pltpu.VMEM((tm, tm), BF16),
                        pltpu.VMEM((CONV_CH, CONV_CH), BF16)] + _gather_sems(ng),
        compiler_params=_cp(("arbitrary",), 56),
    )(_hbm(p), wdw, *vecs, lbl, *[_hbm(b) for b in gbufs])
    return outs[:4], outs[4:]


def _mix_out_fwd(cat, w_out, x, modr, g_post, g_ffn):
    T = x.shape[0]
    tm = _tok_tile(T)

    def body(cat_ref, w_ref, x_ref, mod_ref, gp_ref, gf_ref, y_ref, x1_ref, h2_ref):
        yv = _dot(cat_ref[...], w_ref[...])
        y_ref[...] = yv
        rs = lax.rsqrt(jnp.mean(yv * yv, axis=-1, keepdims=True) + RMS_EPS)
        x1 = x_ref[...] + mod_ref[2:3, :] * ((yv * rs) * gp_ref[...])
        x1_ref[...] = x1
        rs1 = lax.rsqrt(jnp.mean(x1 * x1, axis=-1, keepdims=True) + RMS_EPS)
        h2 = (x1 * rs1) * gf_ref[...] * (1.0 + mod_ref[4:5, :]) + mod_ref[3:4, :]
        h2_ref[...] = h2.astype(BF16)

    tile = pl.BlockSpec((tm, D_MODEL), lambda i: (i, 0))
    return pl.pallas_call(
        body, name="mix_out_fwd", grid=(T // tm,),
        in_specs=[tile, _full((D_MODEL, D_MODEL)), tile, _full((6, D_MODEL)), _full((1, D_MODEL)),
                  _full((1, D_MODEL))],
        out_specs=[tile, tile, tile],
        out_shape=[_big((T,D_MODEL), F32), _big((T,D_MODEL), F32),
                   _big((T,D_MODEL), BF16)],
        compiler_params=_cp(("arbitrary",), 40),
    )(*_hbm(cat, w_out, x), modr, g_post, g_ffn)


def _row_chains(rows, n=2):
    step = rows // n
    return [slice(k * step, (k + 1) * step) for k in range(n)]


def _ffn_blocks():
    return D_FF // FFN_BLOCK, (D_FF // N_CHIPS) // FFN_BLOCK


def _ffn_fwd(h2, w_up_g, w_down, x1, target, modr, g_post):
    T = h2.shape[0]
    tm = min(FFN_TILE, T)
    fb = FFN_BLOCK
    nj, per = _ffn_blocks()

    def body(h_ref, wu_ref, wd_ref, x1_ref, t_ref, mod_ref, g_ref, r_ref, dy2_ref, dx2_ref, st_ref, acc):
        i, j = pl.program_id(0), pl.program_id(1)

        @pl.when((i == 0) & (j == 0))
        def _():
            st_ref[...] = jnp.zeros(st_ref.shape, F32)

        @pl.when(j == 0)
        def _():
            acc[...] = jnp.zeros(acc.shape, F32)

        for rows in _row_chains(tm):
            ra = jnp.maximum(_dot(h_ref[rows, :], wu_ref[0]), 0.0)
            rb = (ra * ra).astype(BF16)
            r_ref[rows, :] = rb
            acc[rows, :] += _dot(rb, wd_ref[...])

        @pl.when(j == nj - 1)
        def _():
            y2 = acc[...]
            rs = lax.rsqrt(jnp.mean(y2 * y2, axis=-1, keepdims=True) + RMS_EPS)
            nh = y2 * rs
            gp = g_ref[...]
            err = x1_ref[...] + mod_ref[5:6, :] * (nh * gp) - t_ref[...]
            dx2 = err * (1.0 / D_MODEL)
            dx2_ref[...] = dx2
            st_ref[0:1, :] += _colsum(err * err)
            st_ref[1:2, :] += _colsum(dx2 * (nh * gp))
            dn = dx2 * mod_ref[5:6, :]
            st_ref[2:3, :] += _colsum(dn * nh)
            dy2_ref[...] = _rms_bwd(dn * gp, nh, rs).astype(BF16)

    tile = pl.BlockSpec((tm, D_MODEL), lambda i, j: (i, 0))
    return pl.pallas_call(
        body, name="ffn_fwd", grid=(T // tm, nj),
        in_specs=[tile, pl.BlockSpec((1, D_MODEL, fb), lambda i, j: (j // per, 0, j % per)),
                  pl.BlockSpec((fb, D_MODEL), lambda i, j: (j, 0)), tile, tile,
                  _full((6, D_MODEL)), _full((1, D_MODEL))],
        out_specs=[pl.BlockSpec((tm, fb), lambda i, j: (i, j)), tile, tile, _full((8, D_MODEL))],
        out_shape=[_big((T, D_FF), BF16), _big((T, D_MODEL), BF16), _big((T, D_MODEL), F32),
                   jax.ShapeDtypeStruct((8, D_MODEL), F32)],
        scratch_shapes=[pltpu.VMEM((tm, D_MODEL), F32)],
        compiler_params=_cp(("arbitrary", "arbitrary"), 56),
    )(*_hbm(h2, w_up_g, w_down, x1, target), modr, g_post)


def _rms_bwd(dxn, xn, rs):
    return rs * (dxn - xn * jnp.mean(dxn * xn, axis=-1, keepdims=True))


def _ffn_bwd(dy2, r, x1, dx2, w_up_g, w_down, modr, g_ffn):
    T = dx2.shape[0]
    tm = min(FFN_TILE, T)
    fb = FFN_BLOCK
    nj, per = _ffn_blocks()

    def body(dy2_ref, r_ref, x1_ref, dx2_ref, wu_ref, wd_ref, mod_ref, gf_ref, da_ref, dx1_ref, st_ref, dh_s):
        i, j = pl.program_id(0), pl.program_id(1)

        @pl.when((i == 0) & (j == 0))
        def _():
            st_ref[...] = jnp.zeros(st_ref.shape, F32)

        @pl.when(j == 0)
        def _():
            dh_s[...] = jnp.zeros(dh_s.shape, F32)

        for rows in _row_chains(tm):
            ra = jnp.sqrt(r_ref[rows, :].astype(F32))
            da = (_dot_nt(dy2_ref[rows, :], wd_ref[...]) * (2.0 * ra)).astype(BF16)
            da_ref[rows, :] = da
            dh_s[rows, :] += _dot_nt(da, wu_ref[0])

        @pl.when(j == nj - 1)
        def _():
            dh = dh_s[...]
            x1v = x1_ref[...]
            rs1 = lax.rsqrt(jnp.mean(x1v * x1v, axis=-1, keepdims=True) + RMS_EPS)
            xn = x1v * rs1
            st_ref[0:1, :] += _colsum(dh)
            st_ref[1:2, :] += _colsum(dh * (xn * gf_ref[...]))
            dsc = dh * (1.0 + mod_ref[4:5, :])
            st_ref[2:3, :] += _colsum(dsc * xn)
            dx1_ref[...] = dx2_ref[...] + _rms_bwd(dsc * gf_ref[...], xn, rs1)

    tile = pl.BlockSpec((tm, D_MODEL), lambda i, j: (i, 0))
    ftile = pl.BlockSpec((tm, fb), lambda i, j: (i, j))
    return pl.pallas_call(
        body, name="ffn_bwd", grid=(T // tm, nj),
        in_specs=[tile, ftile, tile, tile, pl.BlockSpec((1, D_MODEL, fb), lambda i, j: (j // per, 0, j % per)),
                  pl.BlockSpec((fb, D_MODEL), lambda i, j: (j, 0)), _full((6, D_MODEL)), _full((1, D_MODEL))],
        out_specs=[ftile, tile, _full((8, D_MODEL))],
        out_shape=[_big((T, D_FF), BF16), _big((T, D_MODEL), F32), jax.ShapeDtypeStruct((8, D_MODEL), F32)],
        scratch_shapes=[pltpu.VMEM((tm, D_MODEL), F32)],
        compiler_params=_cp(("arbitrary", "arbitrary"), 56),
    )(*_hbm(dy2, r, x1, dx2, w_up_g, w_down), modr, g_ffn)


def _mix_out_bwd(dx1, y, w_out, modr, g_post, swap):
    T = dx1.shape[0]
    tm = _tok_tile(T)
    nt = T // tm
    ns = len(swap)

    def body(*refs):
        dx1_ref, y_ref, w_ref, mod_ref, gp_ref = refs[:5]
        s_ins = refs[5:5 + ns]
        dy_ref, dcat_ref, st_ref = refs[5 + ns:8 + ns]
        s_outs = refs[8 + ns:8 + 2 * ns]
        pssem, prsem = refs[8 + 2 * ns:]
        i = pl.program_id(0)

        @pl.when(i == 0)
        def _():
            for cp in _pair_copies(s_ins, s_outs, pssem, prsem):
                cp.start()
            st_ref[...] = jnp.zeros(st_ref.shape, F32)

        dxv, yv = dx1_ref[...], y_ref[...]
        rs = lax.rsqrt(jnp.mean(yv * yv, axis=-1, keepdims=True) + RMS_EPS)
        nh = yv * rs
        st_ref[0:1, :] += _colsum(dxv * (nh * gp_ref[...]))
        dn = dxv * mod_ref[2:3, :]
        st_ref[1:2, :] += _colsum(dn * nh)
        dy = _rms_bwd(dn * gp_ref[...], nh, rs).astype(BF16)
        dy_ref[...] = dy
        dcat_ref[...] = _dot_nt(dy, w_ref[...])

        @pl.when(i == nt - 1)
        def _():
            copies = _pair_copies(s_ins, s_outs, pssem, prsem)
            for cp in copies:
                cp.wait_recv()
            for cp in copies:
                cp.wait_send()

    tile = pl.BlockSpec((tm, D_MODEL), lambda i: (i, 0))
    hbm = pl.BlockSpec(memory_space=pl.ANY)
    outs = pl.pallas_call(
        body, name="mix_out_bwd", grid=(nt,),
        in_specs=[tile, tile, _full((D_MODEL, D_MODEL)), _full((6, D_MODEL)), _full((1, D_MODEL))] + [hbm] * ns,
        out_specs=[tile, tile, _full((8, D_MODEL))] + [hbm] * ns,
        out_shape=[_big((T, D_MODEL), BF16), _big((T, D_MODEL), F32), jax.ShapeDtypeStruct((8, D_MODEL), F32)]
        + _pair_shapes(swap),
        scratch_shapes=_pair_sems(ns),
        compiler_params=_cp(("arbitrary",), 40),
    )(*_hbm(dx1, y, w_out), modr, g_post, *[_hbm(g) for g in swap])
    return outs[:3], outs[3:]


def _mixers_bwd(p, dcat, ys, o, states, wdw, vecs, lbl, pairs_b):
    T = p.shape[0]
    tm = min(MIXB_TILE, T)
    nt = T // tm
    nch = tm // CHUNK
    hpt = tm // HALO
    nx = len(pairs_b)

    def body(*refs):
        (p_ref, ph_ref, dcat_ref, ys_ref, o_ref, st_ref, wdw_ref, bdw_ref, gain_ref, bias_ref, gout_ref,
         lbl_ref) = refs[:12]
        x_ins = refs[12:12 + nx]
        dp_ref, sb_ref, s5_ref, dw_ref = refs[12 + nx:16 + nx]
        x_outs = refs[16 + nx:16 + 2 * nx]
        (ubuf, dybuf, carry, dstate, qt_s, kt_s, kh_s, v_s, do_s, egl_s, dqt_s, dkt_s, dkh_s, dv_s, dgl_s,
         dsh, dw8, dshift, lower_s, upper_s, same_s, gmat_s, xssem, xrsem) = refs[16 + 2 * nx:]
        i = pl.program_id(0)
        tile_idx = nt - 1 - i

        @pl.when(i == 0)
        def _():
            for cp in _xchg_copies(x_ins, x_outs, xssem, xrsem):
                cp.start()
            dstate[...] = jnp.zeros(dstate.shape, F32)
            carry[...] = jnp.zeros(carry.shape, F32)
            sb_ref[...] = jnp.zeros(sb_ref.shape, F32)
            s5_ref[...] = jnp.zeros(s5_ref.shape, F32)
            dw_ref[...] = jnp.zeros(dw_ref.shape, F32)
            dw8[...] = jnp.zeros(dw8.shape, F32)
            lower_s[...], upper_s[...], same_s[...] = _chunk_masks(tm)
            gmat_s[...] = _gn_matrix()
            dsh[0:SUB, :] = jnp.zeros((SUB, CONV_CH), F32)
            dsh[SUB + tm:2 * SUB + tm, :] = jnp.zeros((SUB, CONV_CH), F32)
            ubuf[HALO + tm:HALO + tm + SUB, :] = jnp.zeros((SUB, CONV_CH), F32)

        uh = ph_ref[:, 0:CONV_CH] * _sig(ph_ref[:, CONV_CH:2 * CONV_CH])
        ubuf[0:HALO, :] = jnp.where(tile_idx > 0, uh, 0.0)
        ubuf[HALO:HALO + tm, :] = p_ref[:, 0:CONV_CH] * _sig(p_ref[:, CONV_CH:2 * CONV_CH])
        gmat = gmat_s[...]
        gain = gain_ref[...]
        yv = ys_ref[...]
        d = yv - _gmean(yv, gmat)
        rs = lax.rsqrt(_gmean(d * d, gmat) + GN_EPS)
        yn = d * rs
        z = yn * gain + bias_ref[...]
        sz = _sig(z)
        dz = dcat_ref[:, 0:CONV_CH] * (sz * (1.0 + z * (1.0 - sz)))
        dyn = dz * gain
        dyc = rs * (dyn - _gmean(dyn, gmat) - yn * _gmean(dyn * yn, gmat))
        s5_ref[0:1, :] += _colsum(dyc)
        s5_ref[1:2, :] += _colsum(dz * yn)
        s5_ref[2:3, :] += _colsum(dz)
        dybuf[tm:tm + HALO, :] = carry[...]
        dybuf[0:tm, :] = dyc
        dsh[SUB:SUB + tm, :] = dyc
        carry[...] = dyc[0:HALO, :]
        for b in range(SUB):
            dshift[...] = dsh[SUB - b:2 * SUB - b + tm, :]
            for j, off in CONV_FWD_TAPS:
                if off % SUB == b:
                    prod = dshift[...] * ubuf[off - b:off - b + tm + SUB, :]
                    dw8[j] += jnp.sum(prod.reshape((tm + SUB) // SUB, SUB, CONV_CH), axis=0)
        for r in range(tm // CONV_ROWS):
            rows = slice(r * CONV_ROWS, (r + 1) * CONV_ROWS)
            for lb_ in range(CONV_CH // LANE):
                lanes = slice(lb_ * LANE, (lb_ + 1) * LANE)
                glanes = slice(CONV_CH + lb_ * LANE, CONV_CH + (lb_ + 1) * LANE)
                acc = _tap_conv(dybuf, wdw_ref, r * CONV_ROWS, CONV_BWD_TAPS, lanes)
                val = p_ref[rows, lanes]
                sg = _sig(p_ref[rows, glanes])
                dval = acc * sg
                dgate = acc * val * (sg * (1.0 - sg))
                dp_ref[rows, lanes] = dval.astype(BF16)
                dp_ref[rows, glanes] = dgate.astype(BF16)
                sb_ref[0:1, lanes] += _colsum(dval)
                sb_ref[0:1, glanes] += _colsum(dgate)

        o0 = 2 * CONV_CH
        for h in range(N_HEADS):
            sl = slice(h * HEAD_D, (h + 1) * HEAD_D)
            gsl = slice(o0 + 3 * HGRN_W + h * HEAD_D, o0 + 3 * HGRN_W + (h + 1) * HEAD_D)
            oh = o_ref[:, sl]
            gh = p_ref[:, gsl]
            dh = dcat_ref[:, CONV_CH + h * HEAD_D:CONV_CH + (h + 1) * HEAD_D]
            gout = gout_ref[:, sl]
            rsh = lax.rsqrt(jnp.mean(oh * oh, axis=-1, keepdims=True) + RMS_EPS)
            on = oh * rsh
            sgg = _sig(gh)
            dgh = dh * (on * gout) * (sgg * (1.0 + gh * (1.0 - sgg)))
            dm = dh * (gh * sgg)
            s5_ref[3:4, sl] += _colsum(dm * on)
            do_s[:, sl] = _rms_bwd(dm * gout, on, rsh).astype(BF16)
            dp_ref[:, gsl] = dgh.astype(BF16)
            sb_ref[2:3, CONV_CH + h * HEAD_D:CONV_CH + (h + 1) * HEAD_D] += _colsum(dgh)

        lb, _ = _lower_bound(lbl_ref)
        lower, upper, same = lower_s[...], upper_s[...], same_s[...]
        pq = p_ref[:, o0:o0 + HGRN_W]
        pr = _hgrn_prep(pq, p_ref[:, o0 + HGRN_W:o0 + 2 * HGRN_W], lb, lower, same)
        qt_s[...] = pr["qt"].astype(BF16)
        kt_s[...] = pr["kt"].astype(BF16)
        kh_s[...] = pr["kh"].astype(BF16)
        v_s[...] = p_ref[:, o0 + 2 * HGRN_W:o0 + 3 * HGRN_W].astype(BF16)
        egl_s[...] = jnp.exp(pr["Gl"])
        tri = _tri()

        def chunk(it, c_):
            ci = nch - 1 - it
            r0 = pl.multiple_of(ci * CHUNK, CHUNK)
            rows = pl.ds(r0, CHUNK)
            for h in range(N_HEADS):
                ls = pl.ds(h * HEAD_D, HEAD_D)
                qc, kc, hc, vc = qt_s[rows, ls], kt_s[rows, ls], kh_s[rows, ls], v_s[rows, ls]
                dob = do_s[rows, ls]
                s0 = st_ref[ci, h]
                s0b = s0.astype(BF16)
                ds1 = dstate[h]
                ds1b = ds1.astype(BF16)
                egl = egl_s[pl.ds(r0, 1), ls]
                att = jnp.where(tri, _dot_nt(qc, kc), 0.0).astype(BF16)
                datt = jnp.where(tri, _dot_nt(dob, vc), 0.0).astype(BF16)
                dv_s[rows, ls] = _dot_tn(att, dob) + _dot_nt(hc, ds1b)
                dqt_s[rows, ls] = _dot(datt, kc) + _dot(dob, s0b)
                dkt_s[rows, ls] = _dot_tn(datt, qc)
                dkh_s[rows, ls] = _dot(vc, ds1b)
                dgl = egl * _colsum(ds1 * s0)
                dgl_s[rows, ls] = jnp.broadcast_to(dgl, (CHUNK, HEAD_D))
                dstate[h] = ds1 * egl + _dot_tn(dob, qc)
            return c_

        lax.fori_loop(0, nch, chunk, 0)
        dqt, dkt, dkh = dqt_s[...], dkt_s[...], dkh_s[...]
        dk = dkt * pr["enG"] + dkh * pr["eGlG"]
        khk = dkh * kh_s[...].astype(F32)
        dG = dqt * qt_s[...].astype(F32) - dkt * kt_s[...].astype(F32) - khk
        dlogf = _mm3(upper, dG) + _mm3(same, khk) + dgl_s[...]
        df = dlogf / pr["f"] - dk
        sf, sq = pr["sf"], pr["sq"]
        s5_ref[4:5, :] += _colsum(df * (1.0 - sf))
        dfl = df * (1.0 - lb) * (sf * (1.0 - sf))
        dq = (dqt * pr["eG"]) * (sq * (1.0 + pq * (1.0 - sq)))
        dvv = dv_s[...]
        dp_ref[:, o0:o0 + HGRN_W] = dq.astype(BF16)
        dp_ref[:, o0 + HGRN_W:o0 + 2 * HGRN_W] = dfl.astype(BF16)
        dp_ref[:, o0 + 2 * HGRN_W:o0 + 3 * HGRN_W] = dvv.astype(BF16)
        sb_ref[1:2, 0:HGRN_W] += _colsum(dq)
        sb_ref[1:2, HGRN_W:2 * HGRN_W] += _colsum(dfl)
        sb_ref[2:3, 0:HGRN_W] += _colsum(dvv)

        @pl.when(i == nt - 1)
        def _():
            for j in range(CONV_K):
                dw_ref[j:j + 1, :] = _colsum(dw8[j])
            copies = _xchg_copies(x_ins, x_outs, xssem, xrsem)
            for cp in copies:
                cp.wait_recv()
            for cp in copies:
                cp.wait_send()

    rev = lambda cols: pl.BlockSpec((tm, cols), lambda i: (nt - 1 - i, 0))
    halo = pl.BlockSpec((HALO, 2 * CONV_CH), lambda i: (jnp.maximum((nt - 1 - i) * hpt - 1, 0), 0))
    wide = lambda n: pltpu.VMEM((tm, HGRN_W), n)
    hbm = pl.BlockSpec(memory_space=pl.ANY)
    outs = pl.pallas_call(
        body, name="mixers_bwd", grid=(nt,),
        in_specs=[rev(IN_COLS), halo, rev(D_MODEL), rev(CONV_CH), rev(HGRN_W),
                  pl.BlockSpec((nch, N_HEADS, HEAD_D, HEAD_D), lambda i: (nt - 1 - i, 0, 0, 0)),
                  _full((HALO, CONV_CH))] + [_full((1, CONV_CH))] * 4 + [_full((2, HGRN_W))] + [hbm] * nx,
        out_specs=[rev(IN_COLS), _full((8, D_MODEL)), _full((8, CONV_CH)), _full((HALO, CONV_CH))] + [hbm] * nx,
        out_shape=[_big((T, IN_COLS), BF16), jax.ShapeDtypeStruct((8, D_MODEL), F32),
                   jax.ShapeDtypeStruct((8, CONV_CH), F32), jax.ShapeDtypeStruct((HALO, CONV_CH), F32)]
        + [_big(pb.shape, BF16) for pb in pairs_b],
        scratch_shapes=[pltpu.VMEM((tm + HALO + SUB, CONV_CH), F32), pltpu.VMEM((tm + HALO, CONV_CH), F32),
                        pltpu.VMEM((HALO, CONV_CH), F32), pltpu.VMEM((N_HEADS, HEAD_D, HEAD_D), F32),
                        wide(BF16), wide(BF16), wide(BF16), wide(BF16), wide(BF16),
                        wide(F32), wide(F32), wide(F32), wide(F32), wide(F32), wide(F32),
                        pltpu.VMEM((tm + 2 * SUB, CONV_CH), F32), pltpu.VMEM((HALO, SUB, CONV_CH), F32),
                        pltpu.VMEM((tm + SUB, CONV_CH), F32), pltpu.VMEM((tm, tm), BF16), pltpu.VMEM((tm, tm), BF16),
                        pltpu.VMEM((tm, tm), BF16), pltpu.VMEM((CONV_CH, CONV_CH), BF16)]
        + _xchg_sems(nx),
        compiler_params=_cp(("arbitrary",), 56),
    )(*_hbm(p, p, dcat, ys, o, states), wdw, *vecs, lbl, *[_hbm(pb) for pb in pairs_b])
    return outs[:4], outs[4:]


def _mix_in_bwd(dp, w_in_g, x, dx1, modr, g_pre, pairs_b):
    T = x.shape[0]
    tm = _tok_tile(T)
    nt = T // tm
    nb = IN_COLS // N_CHIPS
    nx = len(pairs_b)

    def body(*refs):
        dp_ref, w_ref, x_ref, dx1_ref, mod_ref, g_ref = refs[:6]
        x_ins = refs[6:6 + nx]
        gx_ref, st_ref = refs[6 + nx:8 + nx]
        x_outs = refs[8 + nx:8 + 2 * nx]
        xssem, xrsem = refs[8 + 2 * nx:]
        i = pl.program_id(0)

        @pl.when(i == 0)
        def _():
            for cp in _xchg_copies(x_ins, x_outs, xssem, xrsem):
                cp.start()
            st_ref[...] = jnp.zeros(st_ref.shape, F32)

        dh = None
        for j in range(N_CHIPS):
            part = _dot_nt(dp_ref[:, j * nb:(j + 1) * nb], w_ref[j])
            dh = part if dh is None else dh + part
        xv = x_ref[...]
        rs = lax.rsqrt(jnp.mean(xv * xv, axis=-1, keepdims=True) + RMS_EPS)
        xn = xv * rs
        st_ref[0:1, :] += _colsum(dh)
        st_ref[1:2, :] += _colsum(dh * (xn * g_ref[...]))
        dsc = dh * (1.0 + mod_ref[1:2, :])
        st_ref[2:3, :] += _colsum(dsc * xn)
        gx_ref[...] = dx1_ref[...] + _rms_bwd(dsc * g_ref[...], xn, rs)

        @pl.when(i == nt - 1)
        def _():
            copies = _xchg_copies(x_ins, x_outs, xssem, xrsem)
            for cp in copies:
                cp.wait_recv()
            for cp in copies:
                cp.wait_send()

    tile = pl.BlockSpec((tm, D_MODEL), lambda i: (i, 0))
    hbm = pl.BlockSpec(memory_space=pl.ANY)
    outs = pl.pallas_call(
        body, name="mix_in_bwd", grid=(nt,),
        in_specs=[pl.BlockSpec((tm, IN_COLS), lambda i: (i, 0)), _full((N_CHIPS, D_MODEL, nb)), tile, tile,
                  _full((6, D_MODEL)), _full((1, D_MODEL))] + [hbm] * nx,
        out_specs=[tile, _full((8, D_MODEL))] + [hbm] * nx,
        out_shape=[_big((T, D_MODEL), F32), jax.ShapeDtypeStruct((8, D_MODEL), F32)]
        + [_big(pb.shape, BF16) for pb in pairs_b],
        scratch_shapes=_xchg_sems(nx),
        compiler_params=_cp(("arbitrary",), 48),
    )(*_hbm(dp, w_in_g, x, dx1), modr, g_pre, *[_hbm(pb) for pb in pairs_b])
    return outs[:2], outs[2:]


def _weight_grad(a, b, a_blocked, b_blocked, name):
    T = a.shape[0]
    tt = min(GRAD_TILE, T)
    nt = T // tt
    ka = a.shape[1] // N_CHIPS if a_blocked else a.shape[1]
    nb = b.shape[1] // N_CHIPS if b_blocked else b.shape[1]

    def body(a_ref, b_ref, o_ref, ob_ref):
        t = pl.program_id(1)

        @pl.when(t == 0)
        def _():
            o_ref[...] = jnp.zeros(o_ref.shape, F32)

        for cols in _row_chains(nb):
            o_ref[0, :, cols] += _dot_tn(a_ref[...], b_ref[:, cols])

        @pl.when(t == nt - 1)
        def _():
            ob_ref[0] = o_ref[0].astype(BF16)

    blk = pl.BlockSpec((1, ka, nb), lambda j, t: (j, 0, 0))
    return pl.pallas_call(
        body, name=name, grid=(N_CHIPS, nt),
        in_specs=[pl.BlockSpec((tt, ka), (lambda j, t: (t, j)) if a_blocked else (lambda j, t: (t, 0))),
                  pl.BlockSpec((tt, nb), (lambda j, t: (t, j)) if b_blocked else (lambda j, t: (t, 0)))],
        out_specs=[blk, blk],
        out_shape=[_big((N_CHIPS, ka, nb), F32), _big((N_CHIPS, ka, nb), BF16)],
        compiler_params=_cp(("arbitrary", "arbitrary"), 48),
    )(*_hbm(a, b))


R_LOSS = 0
R_FFN = 8
R_OUT = 16
R_IN = 24
R_BIN = 32
R_512 = 40
R_DW = 48
N_STAT_ROWS = 80
MOD_ROWS = (R_IN + 0, R_IN + 1, R_OUT + 0, R_FFN + 0, R_FFN + 1, R_LOSS + 1)


def _small_update(gath, params):
    names = ["b_ada", "lb_logits", "g_pre_mix", "b_in", "b_dw", "gn_gain", "gn_bias", "g_hgrn_out", "g_post_mix",
             "g_pre_ffn", "g_post_ffn"]
    flat = []
    for n in names:
        flat += list(params[n])
    n_in = 1 + len(flat)

    def body(*refs):
        g_ref = refs[0]
        prm = {n: refs[1 + 3 * k:4 + 3 * k] for k, n in enumerate(names)}
        outs = refs[n_in:]
        loss_ref, dmod_ref, dwdw_ref = outs[0], outs[1], outs[2]
        res = {n: outs[3 + 4 * k:7 + 4 * k] for k, n in enumerate(names)}
        red = g_ref[0]
        for dev in range(1, N_DEV):
            red = red + g_ref[dev]
        loss_ref[...] = jnp.broadcast_to(
            (0.5 / D_MODEL) * jnp.sum(red[R_LOSS:R_LOSS + 1, :], axis=-1, keepdims=True), loss_ref.shape)
        for dev in range(N_DEV):
            for k, r in enumerate(MOD_ROWS):
                dmod_ref[dev:dev + 1, k * D_MODEL:(k + 1) * D_MODEL] = g_ref[dev, r:r + 1, :]
        dwdw_ref[...] = red[R_DW:R_DW + HALO, 0:CONV_CH]

        def finish(name, pieces):
            w_ref, m_ref, v_ref = prm[name]
            g_out, d_out, m_out, v_out = res[name]
            for rsl, lsl, g in pieces:
                d, m2, v2 = _adam_math(w_ref[rsl, lsl], g, m_ref[rsl, lsl], v_ref[rsl, lsl])
                g_out[rsl, lsl] = g
                d_out[rsl, lsl] = d
                m_out[rsl, lsl] = m2
                v_out[rsl, lsl] = v2

        one = slice(0, 1)
        row = lambda r: red[r:r + 1, :]
        half = lambda r: red[r:r + 1, 0:CONV_CH]
        finish("b_ada", [(one, slice(k * D_MODEL, (k + 1) * D_MODEL), row(r)) for k, r in enumerate(MOD_ROWS)])
        finish("b_in", [(one, slice(k * D_MODEL, (k + 1) * D_MODEL), row(R_BIN + k)) for k in range(3)])
        finish("g_pre_mix", [(one, slice(None), row(R_IN + 2))])
        finish("g_post_mix", [(one, slice(None), row(R_OUT + 1))])
        finish("g_pre_ffn", [(one, slice(None), row(R_FFN + 2))])
        finish("g_post_ffn", [(one, slice(None), row(R_LOSS + 2))])
        finish("b_dw", [(one, slice(None), half(R_512 + 0))])
        finish("gn_gain", [(one, slice(None), half(R_512 + 1))])
        finish("gn_bias", [(one, slice(None), half(R_512 + 2))])
        finish("g_hgrn_out", [(one, slice(None), half(R_512 + 3))])
        s0, s1 = _lower_bound(prm["lb_logits"][0])
        dlb = half(R_512 + 4)
        finish("lb_logits", [(slice(0, 1), slice(None), dlb * s0 * (1.0 - s0)),
                             (slice(1, 2), slice(None), -dlb * s0 * s1)])

    vm = pl.BlockSpec(memory_space=pltpu.VMEM)
    out_shape = [jax.ShapeDtypeStruct((8, 128), F32), jax.ShapeDtypeStruct((N_DEV, 6 * D_MODEL), F32),
                 jax.ShapeDtypeStruct((HALO, CONV_CH), F32)]
    for n in names:
        out_shape += [jax.ShapeDtypeStruct(params[n][0].shape, F32)] * 4
    outs = pl.pallas_call(
        body, name="small_update", out_shape=out_shape,
        in_specs=[vm] * n_in, out_specs=[vm] * len(out_shape),
        compiler_params=_cp(None, 32),
    )(gath, *flat)
    return outs[0], outs[1], outs[2], {n: outs[3 + 4 * k:7 + 4 * k] for k, n in enumerate(names)}


def _wdw_adam(w, g, m, v):
    def body(w_ref, g_ref, m_ref, v_ref, d_out, m_out, v_out):
        d, m2, v2 = _adam_math(w_ref[...], g_ref[...], m_ref[...], v_ref[...])
        d_out[...] = d
        m_out[...] = m2
        v_out[...] = v2

    vm = pl.BlockSpec(memory_space=pltpu.VMEM)
    return pl.pallas_call(
        body, name="wdw_adam", out_shape=[jax.ShapeDtypeStruct(w.shape, F32)] * 3,
        in_specs=[vm] * 4, out_specs=[vm] * 3, compiler_params=_cp(None, 16),
    )(w, g, m, v)


def kernel(x, c, w_ada, b_ada, lb_logits, g_pre_mix, w_in, b_in, w_dw, b_dw, gn_gain, gn_bias, g_hgrn_out, w_out, g_post_mix, g_pre_ffn, w_up, w_down, g_post_ffn, loss_target, m_w_ada, m_b_ada, m_lb_logits, m_g_pre_mix, m_w_in, m_b_in, m_w_dw, m_b_dw, m_gn_gain, m_gn_bias, m_g_hgrn_out, m_w_out, m_g_post_mix, m_g_pre_ffn, m_w_up, m_w_down, m_g_post_ffn, v_w_ada, v_b_ada, v_lb_logits, v_g_pre_mix, v_w_in, v_b_in, v_w_dw, v_b_dw, v_gn_gain, v_gn_bias, v_g_hgrn_out, v_w_out, v_g_post_mix, v_g_pre_ffn, v_w_up, v_w_down, v_g_post_ffn):
    ax, ay, ac = lax.axis_index("x"), lax.axis_index("y"), lax.axis_index("c")
    chip = 2 * ax + ay
    T = x.shape[1]
    xs, tgt = x[0], loss_target[0]
    ada_cols = w_ada.shape[2]

    b_sh = lax.dynamic_slice_in_dim(b_ada, chip * ada_cols, ada_cols, axis=1)
    wdw_pad = jnp.pad(w_dw[0], ((0, HALO - CONV_K), (0, 0)))
    chip1 = jnp.reshape(chip, (1,)).astype(jnp.int32)
    place = jnp.stack([ac, chip]).astype(jnp.int32)
    _, c8, modg, wdwg, w_in_g = _ada_exchange(c, w_ada[0], b_sh, wdw_pad, _cast_own(chip1, w_in[0], "cast_w_in"))
    modr = modg.reshape(6, D_MODEL)
    wdw_all = jnp.transpose(wdwg, (1, 0, 2)).reshape(HALO, CONV_CH)
    late_bufs = [_cast_own(chip1, w[0], "cast_" + t) for w, t in ((w_out, "w_out"), (w_up, "w_up"), (w_down, "w_down"))]
    vec = (b_dw, gn_gain, gn_bias, g_hgrn_out)

    p, h1 = _mix_in_fwd(xs, modr, g_pre_mix, w_in_g, b_in)
    (cat, ys, o, states), (w_out_g, w_up_g, w_down_g) = _mixers_fwd(p, wdw_all, vec, lb_logits, late_bufs)
    w_out_f = w_out_g.reshape(D_MODEL, D_MODEL)
    w_down_f = w_down_g.reshape(D_FF, D_MODEL)
    y, x1, h2 = _mix_out_fwd(cat, w_out_f, xs, modr, g_post_mix, g_pre_ffn)
    r, dy2, dx2, st_loss = _ffn_fwd(h2, w_up_g, w_down_f, x1, tgt, modr, g_post_ffn)

    def pair_sums(grads, got, tags):
        return [_pair_sum(place, g, o_, "pair_sum_" + t) for (g, _), o_, t in zip(grads, got, tags)]

    da, dx1, st_ffn = _ffn_bwd(dy2, r, x1, dx2, w_up_g, w_down_f, modr, g_pre_ffn)
    g_up = _weight_grad(h2, da, False, True, "grad_w_up")
    g_down = _weight_grad(r, dy2, True, False, "grad_w_down")
    (dy, dcat, st_out), got_ud = _mix_out_bwd(dx1, y, w_out_f, modr, g_post_mix, [g_up[1], g_down[1]])
    g_out = _weight_grad(cat, dy, True, False, "grad_w_out")
    got_o = _pair_swap([g_out[1]], "pair_swap_w_out")
    early = pair_sums([g_out, g_up, g_down], list(got_o) + list(got_ud), ["w_out", "w_up", "w_down"])
    (dp, st_bin, st_512, dwdw), got_early = _mixers_bwd(p, dcat, ys, o, states, wdw_all, vec, lb_logits,
                                                        [pb for _, pb in early])
    g_in = _weight_grad(h1, dp, False, True, "grad_w_in")
    late = pair_sums([g_in], _pair_swap([g_in[1]], "pair_swap_w_in"), ["w_in"])
    (grad_x, st_in), got_late = _mix_in_bwd(dp, w_in_g, xs, dx1, modr, g_pre_mix, [late[0][1]])
    fulls = [_chip_sum(place, pf, gb, "chip_sum_" + t)
             for (pf, _), gb, t in zip(late + early, list(got_late) + list(got_early), ["w_in", "w_out", "w_up", "w_down"])]

    pad_lanes = lambda s: jnp.pad(s, ((0, 0), (0, D_MODEL - s.shape[1])))
    stats = jnp.concatenate([st_loss, st_ffn, st_out, st_in, st_bin, pad_lanes(st_512), pad_lanes(dwdw)], axis=0)
    (g_w_in, g_w_out, g_w_up, g_w_down), gath = _final_exchange(fulls, stats)
    small = {"b_ada": (b_ada, m_b_ada, v_b_ada), "lb_logits": (lb_logits, m_lb_logits, v_lb_logits),
             "g_pre_mix": (g_pre_mix, m_g_pre_mix, v_g_pre_mix), "b_in": (b_in, m_b_in, v_b_in),
             "b_dw": (b_dw, m_b_dw, v_b_dw), "gn_gain": (gn_gain, m_gn_gain, v_gn_gain),
             "gn_bias": (gn_bias, m_gn_bias, v_gn_bias), "g_hgrn_out": (g_hgrn_out, m_g_hgrn_out, v_g_hgrn_out),
             "g_post_mix": (g_post_mix, m_g_post_mix, v_g_post_mix), "g_pre_ffn": (g_pre_ffn, m_g_pre_ffn, v_g_pre_ffn),
             "g_post_ffn": (g_post_ffn, m_g_post_ffn, v_g_post_ffn)}
    loss_t, dmod_all, dwdw_sum, sres = _small_update(gath, small)
    loss = loss_t[0, 0]

    res = dict(sres)
    dmod_sh = lax.dynamic_slice_in_dim(dmod_all, chip * ada_cols, ada_cols, axis=1)
    res["w_ada"] = [t[None] for t in _ada_grad_adam(jnp.transpose(c8), dmod_sh, w_ada[0], m_w_ada[0], v_w_ada[0])]
    g_wdw = lax.dynamic_slice_in_dim(dwdw_sum, chip * HEAD_D, HEAD_D, axis=1)[:CONV_K][None]
    res["w_dw"] = [g_wdw] + list(_wdw_adam(w_dw, g_wdw, m_w_dw, v_w_dw))
    for name, g, w, m, v in (("w_in", g_w_in, w_in, m_w_in, v_w_in), ("w_out", g_w_out, w_out, m_w_out, v_w_out),
                             ("w_up", g_w_up, w_up, m_w_up, v_w_up), ("w_down", g_w_down, w_down, m_w_down, v_w_down)):
        d, m2, v2 = _adam_big(w[0], g, m[0], v[0], "adam_" + name)
        res[name] = [g[None], d[None], m2[None], v2[None]]

    order = ["w_ada", "b_ada", "lb_logits", "g_pre_mix", "w_in", "b_in", "w_dw", "b_dw", "gn_gain", "gn_bias",
             "g_hgrn_out", "w_out", "g_post_mix", "g_pre_ffn", "w_up", "w_down", "g_post_ffn"]
    out = [loss, grad_x[None]]
    for k in range(4):
        out += [res[n][k] for n in order]
    return tuple(out)
```

```python
import jax
import jax.numpy as jnp
from jax import lax
from jax.experimental import pallas as pl
from jax.experimental.pallas import tpu as pltpu

F32, BF16 = jnp.float32, jnp.bfloat16
D_MODEL = 1024
CONV_CH = 512
HGRN_W = 512
N_HEADS = 4
HEAD_D = 128
CONV_K = 31
GN_GROUP = 64
GN_SHIFT = 6
IN_COLS = 3072
D_FF = 4096
CHUNK = 64
CHUNK_SHIFT = 6
N_CHIPS = 4
N_DEV = 8
RMS_EPS = 1e-6
GN_EPS = 1e-5
ADAM_LR, ADAM_B1, ADAM_B2, ADAM_EPS, ADAM_WD, ADAM_STEP = 0.001, 0.9, 0.999, 1e-08, 0.01, 10
TOK_TILE = 512
MIXIN_TILE = 1024
MIXB_TILE = 256
FFN_TILE = 1024
FFN_BLOCK = 512
GRAD_TILE = 2048
HALO = 32
SUB = 8
LANE = 128
CONV_ROWS = 128
MIB = 1 << 20
MESH = pl.DeviceIdType.MESH
OTHER_CHIPS = ((0, 1), (1, 0), (1, 1))


def _cp(sem=None, vmem_mib=48):
    return pltpu.CompilerParams(dimension_semantics=sem, vmem_limit_bytes=vmem_mib * MIB)


def _dot(a, b):
    return jnp.dot(a, b, preferred_element_type=F32)


def _dot_nt(a, b):
    return lax.dot_general(a, b, (((1,), (1,)), ((), ())), preferred_element_type=F32)


def _dot_tn(a, b):
    return lax.dot_general(a, b, (((0,), (0,)), ((), ())), preferred_element_type=F32)


def _sig(v):
    return 0.5 * jnp.tanh(0.5 * v) + 0.5


def _colsum(v):
    return jnp.sum(v, axis=0, keepdims=True)


def _flip(v, b):
    return 1 - v if b else v


def _rcopy(src, dst, ssem, rsem, dev):
    return pltpu.make_async_remote_copy(src_ref=src, dst_ref=dst, send_sem=ssem, recv_sem=rsem,
                                        device_id=dev, device_id_type=MESH)


def _place():
    return lax.axis_index("x"), lax.axis_index("y"), lax.axis_index("c")


def _full(shape):
    return pl.BlockSpec(shape, lambda *_: (0,) * len(shape))


def _big(shape, dtype):
    return pltpu.HBM(shape, dtype)


def _hbm(*arrays):
    out = [pltpu.with_memory_space_constraint(a, pltpu.HBM) for a in arrays]
    return out[0] if len(out) == 1 else out


def _split2(v):
    hi = v.astype(BF16)
    lo = (v - hi.astype(F32)).astype(BF16)
    return hi, lo


def _split3(v):
    h1 = v.astype(BF16)
    r1 = v - h1.astype(F32)
    h2 = r1.astype(BF16)
    h3 = (r1 - h2.astype(F32)).astype(BF16)
    return h1, h2, h3


def _mm3s(mat, parts):
    h1, h2, h3 = parts
    return _dot(mat, h1) + _dot(mat, h2) + _dot(mat, h3)


def _mm3(mat, v):
    return _mm3s(mat, _split3(v))


def _gn_matrix():
    r = lax.broadcasted_iota(jnp.int32, (CONV_CH, CONV_CH), 0) >> GN_SHIFT
    c = lax.broadcasted_iota(jnp.int32, (CONV_CH, CONV_CH), 1) >> GN_SHIFT
    return jnp.where(r == c, 1.0 / GN_GROUP, 0.0).astype(BF16)


def _gmean(v, gmat):
    hi, lo = _split2(v)
    return _dot(hi, gmat) + _dot(lo, gmat)


def _chunk_masks(tm):
    r = lax.broadcasted_iota(jnp.int32, (tm, tm), 0)
    c = lax.broadcasted_iota(jnp.int32, (tm, tm), 1)
    same = (r >> CHUNK_SHIFT) == (c >> CHUNK_SHIFT)
    one = lambda m: jnp.where(m, 1.0, 0.0).astype(BF16)
    return one(same & (c <= r)), one(same & (c >= r)), one(same)


def _tri():
    return lax.broadcasted_iota(jnp.int32, (CHUNK, CHUNK), 0) >= lax.broadcasted_iota(jnp.int32, (CHUNK, CHUNK), 1)


def _lower_bound(lbl_ref):
    l0, l1 = lbl_ref[0:1, :], lbl_ref[1:2, :]
    mx = jnp.maximum(l0, l1)
    e0, e1 = jnp.exp(l0 - mx), jnp.exp(l1 - mx)
    return e0 / (e0 + e1), e1 / (e0 + e1)


CONV_FWD_TAPS = tuple((j, HALO - (CONV_K - 1) + j) for j in range(CONV_K))
CONV_BWD_TAPS = tuple((j, (CONV_K - 1) - j) for j in range(CONV_K))


def _tap_conv(src_ref, w_ref, row0, taps, lanes):
    acc = None
    for b in range(SUB):
        pb = None
        for j, off in taps:
            if off % SUB == b:
                lo = row0 + off - b
                term = w_ref[j:j + 1, lanes] * src_ref[lo:lo + CONV_ROWS + SUB, lanes]
                pb = term if pb is None else pb + term
        if pb is not None:
            sh = pb[b:b + CONV_ROWS, :]
            acc = sh if acc is None else acc + sh
    return acc


def _hgrn_prep(pq, pf, lb, lower, same):
    sq = _sig(pq)
    qf = pq * sq
    sf = _sig(pf)
    f = lb + (1.0 - lb) * sf
    logf = jnp.log(f)
    k = 1.0 - f
    parts = _split3(logf)
    G = _mm3s(lower, parts)
    Gl = _mm3s(same, parts)
    eG, enG, eGlG = jnp.exp(G), jnp.exp(-G), jnp.exp(Gl - G)
    return dict(sq=sq, sf=sf, f=f, Gl=Gl, eG=eG, enG=enG, eGlG=eGlG, qt=qf * eG, kt=k * enG, kh=k * eGlG)


def _ada_exchange(c_row, w_ada, b_sh, wdw_pad):
    ncol = w_ada.shape[1]

    def body(c_ref, w_ref, b_ref, wdw_ref, call_ref, c8_ref, modg_ref, wdwg_ref, rows_s, sa, ra, sw, rw, sm, rm):
        x, y, c = _place()
        me = 4 * x + 2 * y + c
        chip = 2 * x + y
        cv = c_ref[...]
        call_ref[me] = cv * _sig(cv)
        wdwg_ref[chip] = wdw_ref[...]
        sends = []
        for m in range(1, N_DEV):
            peer = (_flip(x, m >> 2), _flip(y, (m >> 1) & 1), _flip(c, m & 1))
            cp = _rcopy(call_ref.at[me], call_ref.at[me], sa.at[m - 1], ra.at[m - 1], peer)
            cp.start()
            sends.append(cp)
        for k, (fx, fy) in enumerate(OTHER_CHIPS):
            peer = (_flip(x, fx), _flip(y, fy), c)
            cp = _rcopy(wdwg_ref.at[chip], wdwg_ref.at[chip], sw.at[k], rw.at[k], peer)
            cp.start()
            sends.append(cp)
        for m in range(1, N_DEV):
            peer = (_flip(x, m >> 2), _flip(y, (m >> 1) & 1), _flip(c, m & 1))
            pid = 4 * peer[0] + 2 * peer[1] + peer[2]
            _rcopy(call_ref.at[pid], call_ref.at[pid], sa.at[m - 1], ra.at[m - 1], peer).wait_recv()
        for b in range(N_DEV):
            c8_ref[b:b + 1, :] = call_ref[b]
        mod_all = _dot(c8_ref[...].astype(BF16), w_ref[...].astype(BF16)) + b_ref[...]
        for b in range(N_DEV):
            rows_s[b] = mod_all[b:b + 1, :]
        modg_ref[chip] = rows_s[me]
        for k, (fx, fy) in enumerate(OTHER_CHIPS):
            peer = (_flip(x, fx), _flip(y, fy), c)
            pid = 4 * peer[0] + 2 * peer[1] + peer[2]
            cp = _rcopy(rows_s.at[pid], modg_ref.at[chip], sm.at[k], rm.at[k], peer)
            cp.start()
            sends.append(cp)
        for k, (fx, fy) in enumerate(OTHER_CHIPS):
            peer = (_flip(x, fx), _flip(y, fy), c)
            pchip = 2 * peer[0] + peer[1]
            _rcopy(rows_s.at[0], modg_ref.at[pchip], sm.at[k], rm.at[k], peer).wait_recv()
            _rcopy(wdwg_ref.at[pchip], wdwg_ref.at[pchip], sw.at[k], rw.at[k], peer).wait_recv()
        for cp in sends:
            cp.wait_send()

    vm = pl.BlockSpec(memory_space=pltpu.VMEM)
    return pl.pallas_call(
        body, name="ada_exchange",
        out_shape=[jax.ShapeDtypeStruct((N_DEV, 1, D_MODEL), F32), jax.ShapeDtypeStruct((N_DEV, D_MODEL), F32),
                   jax.ShapeDtypeStruct((N_CHIPS, 1, ncol), F32), jax.ShapeDtypeStruct((N_CHIPS, HALO, HEAD_D), F32)],
        in_specs=[vm] * 4, out_specs=[vm] * 4,
        scratch_shapes=[pltpu.VMEM((N_DEV, 1, ncol), F32),
                        pltpu.SemaphoreType.DMA((N_DEV - 1,)), pltpu.SemaphoreType.DMA((N_DEV - 1,)),
                        pltpu.SemaphoreType.DMA((3,)), pltpu.SemaphoreType.DMA((3,)),
                        pltpu.SemaphoreType.DMA((3,)), pltpu.SemaphoreType.DMA((3,))],
        compiler_params=_cp(None, 32),
    )(c_row, w_ada, b_sh, wdw_pad)


def _cast_own(chip1, shard, name):
    rows, cols = shard.shape
    tr = _row_tile(rows)

    def body(ch_ref, s_ref, o_ref):
        o_ref[0] = s_ref[...].astype(BF16)

    gs = pltpu.PrefetchScalarGridSpec(
        num_scalar_prefetch=1, grid=(rows // tr,),
        in_specs=[pl.BlockSpec((tr, cols), lambda i, ch: (i, 0))],
        out_specs=pl.BlockSpec((1, tr, cols), lambda i, ch: (ch[0], i, 0)))
    return pl.pallas_call(
        body, name=name, grid_spec=gs, out_shape=_big((N_CHIPS, rows, cols), BF16),
        compiler_params=_cp(("arbitrary",), 32),
    )(chip1, _hbm(shard))


def _slab(buf, ch, core):
    hs = buf.shape[1] // 2
    return buf.at[ch, pl.ds(core * hs, hs), :]


def _gather_start(bufs, ssem, rsem):
    x, y, c = _place()
    chip = 2 * x + y
    for k, (fx, fy) in enumerate(OTHER_CHIPS):
        peer = (_flip(x, fx), _flip(y, fy), c)
        for t, buf in enumerate(bufs):
            _rcopy(_slab(buf, chip, c), _slab(buf, chip, c), ssem.at[t * 3 + k], rsem.at[t * 3 + k], peer).start()


def _gather_finish(bufs, ssem, rsem):
    nt = len(bufs)
    x, y, c = _place()
    chip = 2 * x + y
    sibling = (x, y, 1 - c)
    for k, (fx, fy) in enumerate(OTHER_CHIPS):
        peer = (_flip(x, fx), _flip(y, fy), c)
        pchip = 2 * peer[0] + peer[1]
        for t, buf in enumerate(bufs):
            _rcopy(_slab(buf, pchip, c), _slab(buf, pchip, c), ssem.at[t * 3 + k], rsem.at[t * 3 + k], peer).wait_recv()
            _rcopy(_slab(buf, pchip, c), _slab(buf, pchip, c), ssem.at[3 * nt + t * 3 + k],
                   rsem.at[3 * nt + t * 3 + k], sibling).start()
    for k, (fx, fy) in enumerate(OTHER_CHIPS):
        peer = (_flip(x, fx), _flip(y, fy), c)
        pchip = 2 * peer[0] + peer[1]
        for t, buf in enumerate(bufs):
            _rcopy(_slab(buf, pchip, 1 - c), _slab(buf, pchip, 1 - c), ssem.at[3 * nt + t * 3 + k],
                   rsem.at[3 * nt + t * 3 + k], sibling).wait_recv()
            _rcopy(_slab(buf, chip, c), _slab(buf, chip, c), ssem.at[t * 3 + k], rsem.at[t * 3 + k], peer).wait_send()
            _rcopy(_slab(buf, pchip, c), _slab(buf, pchip, c), ssem.at[3 * nt + t * 3 + k],
                   rsem.at[3 * nt + t * 3 + k], sibling).wait_send()


def _gather_arrive(bufs, k, ssem, rsem):
    nt = len(bufs)
    x, y, c = _place()
    fx, fy = OTHER_CHIPS[k]
    peer = (_flip(x, fx), _flip(y, fy), c)
    pchip = 2 * peer[0] + peer[1]
    for t, buf in enumerate(bufs):
        _rcopy(_slab(buf, pchip, c), _slab(buf, pchip, c), ssem.at[t * 3 + k], rsem.at[t * 3 + k], peer).wait_recv()
        _rcopy(_slab(buf, pchip, c), _slab(buf, pchip, c), ssem.at[3 * nt + t * 3 + k],
               rsem.at[3 * nt + t * 3 + k], (x, y, 1 - c)).start()
    for t, buf in enumerate(bufs):
        _rcopy(_slab(buf, pchip, 1 - c), _slab(buf, pchip, 1 - c), ssem.at[3 * nt + t * 3 + k],
               rsem.at[3 * nt + t * 3 + k], (x, y, 1 - c)).wait_recv()


def _gather_sends_done(bufs, ssem, rsem):
    nt = len(bufs)
    x, y, c = _place()
    chip = 2 * x + y
    for k, (fx, fy) in enumerate(OTHER_CHIPS):
        peer = (_flip(x, fx), _flip(y, fy), c)
        pchip = 2 * peer[0] + peer[1]
        for t, buf in enumerate(bufs):
            _rcopy(_slab(buf, chip, c), _slab(buf, chip, c), ssem.at[t * 3 + k], rsem.at[t * 3 + k], peer).wait_send()
            _rcopy(_slab(buf, pchip, c), _slab(buf, pchip, c), ssem.at[3 * nt + t * 3 + k],
                   rsem.at[3 * nt + t * 3 + k], (x, y, 1 - c)).wait_send()


def _gather_sems(nt):
    return [pltpu.SemaphoreType.DMA((6 * nt,)), pltpu.SemaphoreType.DMA((6 * nt,))]


def _pair_copies(ins, outs, ssem, rsem):
    x, y, c = _place()
    copies = []
    for t in range(len(ins)):
        hs = ins[t].shape[1] // 2
        copies.append(_rcopy(ins[t].at[:, pl.ds((1 - c) * hs, hs), :], outs[t], ssem.at[t], rsem.at[t], (x, y, 1 - c)))
    return copies


def _pair_shapes(grads):
    return [_big((g.shape[0], g.shape[1] // 2, g.shape[2]), g.dtype) for g in grads]


def _pair_sems(nt):
    return [pltpu.SemaphoreType.DMA((nt,)), pltpu.SemaphoreType.DMA((nt,))]


def _pair_swap(grads, name):
    nt = len(grads)
    hbm = pl.BlockSpec(memory_space=pl.ANY)

    def body(*refs):
        copies = _pair_copies(refs[:nt], refs[nt:2 * nt], refs[2 * nt], refs[2 * nt + 1])
        for cp in copies:
            cp.start()
        for cp in copies:
            cp.wait_recv()
        for cp in copies:
            cp.wait_send()

    return pl.pallas_call(
        body, name=name, out_shape=_pair_shapes(grads), in_specs=[hbm] * nt, out_specs=[hbm] * nt,
        scratch_shapes=_pair_sems(nt),
    )(*[_hbm(g) for g in grads])


def _xchg_copies(ins, outs, ssem, rsem):
    x, y, c = _place()
    copies = []
    for k, (fx, fy) in enumerate(OTHER_CHIPS):
        peer = (_flip(x, fx), _flip(y, fy), c)
        for t in range(len(ins)):
            copies.append(_rcopy(ins[t].at[k], outs[t].at[k], ssem.at[t * 3 + k], rsem.at[t * 3 + k], peer))
    return copies


def _xchg_sems(nt):
    return [pltpu.SemaphoreType.DMA((3 * nt,)), pltpu.SemaphoreType.DMA((3 * nt,))]


def _final_exchange(fulls, stats):
    nt = len(fulls)
    rows, cols = stats.shape
    hbm = pl.BlockSpec(memory_space=pl.ANY)
    vm = pl.BlockSpec(memory_space=pltpu.VMEM)

    def body(*refs):
        ins, s_ref = refs[:nt], refs[nt]
        outs, g_ref = refs[nt + 1:2 * nt + 1], refs[2 * nt + 1]
        hssem, hrsem, ssem, rsem = refs[2 * nt + 2:]
        x, y, c = _place()
        me, sibling = (x, y, c), (x, y, 1 - c)
        halves = []
        for t in range(nt):
            hs = ins[t].shape[0] // 2
            mine = pl.ds(c * hs, hs)
            cp = _rcopy(ins[t].at[mine, :], outs[t].at[mine, :], hssem.at[t], hrsem.at[t], sibling)
            cp.start()
            halves.append(cp)

        chips = [(_flip(x, fx), _flip(y, fy)) for fx, fy in OTHER_CHIPS]

        def blk(px, py, pc):
            return g_ref.at[4 * px + 2 * py + pc]

        def copy(k, block, to, src=None):
            return _rcopy(blk(*block) if src is None else src, blk(*block), ssem.at[k], rsem.at[k], to)

        g_ref[4 * x + 2 * y + c] = s_ref[...]
        first = [copy(0, me, sibling, src=s_ref)]
        first += [copy(1 + j, me, (*chip, c), src=s_ref) for j, chip in enumerate(chips)]
        for cp in first:
            cp.start()
        passed = [copy(4 + j, (*chip, c), sibling) for j, chip in enumerate(chips)]
        for j, chip in enumerate(chips):
            copy(1 + j, (*chip, c), me).wait_recv()
            passed[j].start()
        copy(0, sibling, me).wait_recv()
        for j, chip in enumerate(chips):
            copy(4 + j, (*chip, 1 - c), me).wait_recv()
        for t in range(nt):
            hs = ins[t].shape[0] // 2
            other = pl.ds((1 - c) * hs, hs)
            _rcopy(ins[t].at[other, :], outs[t].at[other, :], hssem.at[t], hrsem.at[t], sibling).wait_recv()
        for cp in first + passed + halves:
            cp.wait_send()

    outs = pl.pallas_call(
        body, name="final_exchange",
        out_shape=[_big(f.shape, F32) for f in fulls] + [jax.ShapeDtypeStruct((N_DEV, rows, cols), F32)],
        in_specs=[hbm] * nt + [vm], out_specs=[hbm] * nt + [vm],
        input_output_aliases={t: t for t in range(nt)},
        scratch_shapes=[pltpu.SemaphoreType.DMA((nt,)), pltpu.SemaphoreType.DMA((nt,)),
                        pltpu.SemaphoreType.DMA((7,)), pltpu.SemaphoreType.DMA((7,))],
        compiler_params=_cp(None, 32),
    )(*[_hbm(f) for f in fulls], stats)
    return outs[:nt], outs[nt]


def _row_tile(rows):
    return min(rows, 256)


def _pair_sum(place, grad, got, name):
    nb, hs, cols = got.shape
    tr = _row_tile(hs)
    nr = hs // tr

    def body(pl_ref, g_ref, o_ref, pf_ref, pb_ref):
        j = pl.program_id(1)
        s = g_ref[0] + o_ref[0].astype(F32)

        @pl.when(j == 0)
        def _():
            pf_ref[...] = s

        @pl.when(j > 0)
        def _():
            pb_ref[0] = s.astype(BF16)

    gs = pltpu.PrefetchScalarGridSpec(
        num_scalar_prefetch=1, grid=(nr, nb),
        in_specs=[pl.BlockSpec((1, tr, cols), lambda i, j, p: (p[1] ^ j, p[0] * nr + i, 0)),
                  pl.BlockSpec((1, tr, cols), lambda i, j, p: (p[1] ^ j, i, 0))],
        out_specs=[pl.BlockSpec((tr, cols), lambda i, j, p: (i, 0)),
                   pl.BlockSpec((1, tr, cols), lambda i, j, p: (jnp.maximum(j - 1, 0), i, 0))])
    return pl.pallas_call(
        body, name=name, grid_spec=gs,
        out_shape=[_big((hs, cols), F32), _big((nb - 1, hs, cols), BF16)],
        compiler_params=_cp(("arbitrary", "arbitrary"), 32),
    )(place, *_hbm(grad, got))


def _chip_sum(place, pair_f, got_b, name):
    nb, hs, cols = got_b.shape
    tr = _row_tile(hs)
    nr = hs // tr

    def body(pl_ref, pf_ref, gb_ref, o_ref):
        acc = pf_ref[...]
        for k in range(nb):
            acc = acc + gb_ref[k].astype(F32)
        o_ref[...] = acc

    gs = pltpu.PrefetchScalarGridSpec(
        num_scalar_prefetch=1, grid=(nr,),
        in_specs=[pl.BlockSpec((tr, cols), lambda i, p: (i, 0)),
                  pl.BlockSpec((nb, tr, cols), lambda i, p: (0, i, 0))],
        out_specs=pl.BlockSpec((tr, cols), lambda i, p: (p[0] * nr + i, 0)))
    return pl.pallas_call(
        body, name=name, grid_spec=gs,
        out_shape=_big((2 * hs, cols), F32),
        compiler_params=_cp(("arbitrary",), 32),
    )(place, *_hbm(pair_f, got_b))


def _adam_math(w, g, m, v):
    m2 = ADAM_B1 * m + (1.0 - ADAM_B1) * g
    v2 = ADAM_B2 * v + (1.0 - ADAM_B2) * (g * g)
    m_hat = m2 / (1.0 - ADAM_B1 ** ADAM_STEP)
    v_hat = v2 / (1.0 - ADAM_B2 ** ADAM_STEP)
    delta = -ADAM_LR * (m_hat / (jnp.sqrt(v_hat) + ADAM_EPS) + ADAM_WD * w)
    return delta, m2, v2


def _adam_big(w, g, m, v, name):
    rows, cols = w.shape
    tr = _row_tile(rows)

    def body(w_ref, g_ref, m_ref, v_ref, d_out, m_out, v_out):
        d, m2, v2 = _adam_math(w_ref[...], g_ref[...], m_ref[...], v_ref[...])
        d_out[...] = d
        m_out[...] = m2
        v_out[...] = v2

    spec = pl.BlockSpec((tr, cols), lambda i: (i, 0))
    return pl.pallas_call(
        body, name=name, grid=(rows // tr,), in_specs=[spec] * 4, out_specs=[spec] * 3,
        out_shape=[_big(w.shape, F32)] * 3,
        compiler_params=_cp(("arbitrary",), 32),
    )(*_hbm(w, g, m, v))


def _ada_grad_adam(c8t, dmod_sh, w, m, v):
    rows, cols = w.shape
    tr = _row_tile(rows)

    def body(ct_ref, dm_ref, w_ref, m_ref, v_ref, g_out, d_out, m_out, v_out):
        g = None
        for b in range(N_DEV):
            term = ct_ref[:, b:b + 1] * dm_ref[b:b + 1, :]
            g = term if g is None else g + term
        d, m2, v2 = _adam_math(w_ref[...], g, m_ref[...], v_ref[...])
        g_out[...] = g
        d_out[...] = d
        m_out[...] = m2
        v_out[...] = v2

    spec = pl.BlockSpec((tr, cols), lambda i: (i, 0))
    return pl.pallas_call(
        body, name="ada_grad_adam", grid=(rows // tr,),
        in_specs=[pl.BlockSpec((tr, N_DEV), lambda i: (i, 0)), _full((N_DEV, cols)), spec, spec, spec],
        out_specs=[spec] * 4, out_shape=[_big(w.shape, F32)] * 4,
        compiler_params=_cp(("arbitrary",), 32),
    )(c8t, dmod_sh, *_hbm(w, m, v))


def _tok_tile(t):
    return min(TOK_TILE, t)


def _mix_in_fwd(chip1, x, modr, g_pre, b_in4, w_in_buf, w_out_buf):
    T = x.shape[0]
    tm = min(MIXIN_TILE, T)
    nt = T // tm
    nb = IN_COLS // N_CHIPS

    def body(ch_ref, x_ref, mod_ref, g_ref, b_ref, win_in, wout_in, p_ref, h_ref, win_ref, wout_ref,
             h_all, wblk, lsem, is_sem, ir_sem, os_sem, or_sem):
        k, i = pl.program_id(0), pl.program_id(1)
        chip = ch_ref[0]

        def load_block(blk):
            cp = pltpu.make_async_copy(win_ref.at[blk], wblk, lsem)
            cp.start()
            cp.wait()

        @pl.when((k == 0) & (i == 0))
        def _():
            _gather_start([win_ref], is_sem, ir_sem)
            _gather_start([wout_ref], os_sem, or_sem)
            load_block(chip)

        for r in range(N_CHIPS - 1):
            @pl.when((k == r + 1) & (i == 0))
            def _(r=r):
                _gather_arrive([win_ref], r, is_sem, ir_sem)
                load_block(chip ^ (r + 1))

        rows = pl.ds(pl.multiple_of(i * tm, tm), tm)

        @pl.when(k == 0)
        def _():
            xv = x_ref[...]
            rstd = lax.rsqrt(jnp.mean(xv * xv, axis=-1, keepdims=True) + RMS_EPS)
            h = (xv * rstd) * g_ref[...] * (1.0 + mod_ref[1:2, :]) + mod_ref[0:1, :]
            hb = h.astype(BF16)
            h_ref[...] = hb
            h_all[rows, :] = hb

        p_ref[...] = _dot(h_all[rows, :], wblk[...]) + b_ref[chip ^ k]

        @pl.when((k == N_CHIPS - 1) & (i == nt - 1))
        def _():
            _gather_sends_done([win_ref], is_sem, ir_sem)
            _gather_finish([wout_ref], os_sem, or_sem)

    hbm = pl.BlockSpec(memory_space=pl.ANY)
    first_pass = lambda k, i, ch: (jnp.where(k == 0, i, nt - 1), 0)
    gs = pltpu.PrefetchScalarGridSpec(
        num_scalar_prefetch=1, grid=(N_CHIPS, nt),
        in_specs=[pl.BlockSpec((tm, D_MODEL), first_pass), pl.BlockSpec((6, D_MODEL), lambda k, i, ch: (0, 0)),
                  pl.BlockSpec((1, D_MODEL), lambda k, i, ch: (0, 0)),
                  pl.BlockSpec((N_CHIPS, 1, nb), lambda k, i, ch: (0, 0, 0)), hbm, hbm],
        out_specs=[pl.BlockSpec((tm, nb), lambda k, i, ch: (i, ch[0] ^ k)), pl.BlockSpec((tm, D_MODEL), first_pass),
                   hbm, hbm],
        scratch_shapes=[pltpu.VMEM((T, D_MODEL), BF16), pltpu.VMEM((D_MODEL, nb), BF16), pltpu.SemaphoreType.DMA]
        + _gather_sems(1) + _gather_sems(1))
    return pl.pallas_call(
        body, name="mix_in_fwd", grid_spec=gs,
        out_shape=[_big((T, IN_COLS), F32), _big((T, D_MODEL), BF16), _big(w_in_buf.shape, BF16),
                   _big(w_out_buf.shape, BF16)],
        input_output_aliases={5: 2, 6: 3},
        compiler_params=_cp(("arbitrary", "arbitrary"), 48),
    )(chip1, _hbm(x), modr, g_pre, b_in4, _hbm(w_in_buf), _hbm(w_out_buf))


def _mixers_fwd(p, wdw, vecs, lbl, gbufs):
    T = p.shape[0]
    tm = _tok_tile(T)
    nt = T // tm
    nch = tm // CHUNK
    ng = len(gbufs)

    def body(*refs):
        p_ref, wdw_ref, bdw_ref, gain_ref, bias_ref, gout_ref, lbl_ref = refs[:7]
        cat_ref, ys_ref, o_ref, st_ref = refs[7 + ng:11 + ng]
        gout_bufs = refs[11 + ng:11 + 2 * ng]
        ubuf, state, qt_s, kt_s, kh_s, v_s, egl_s, lower_s, same_s, gmat_s, gssem, grsem = refs[11 + 2 * ng:]
        i = pl.program_id(0)

        @pl.when(i == 0)
        def _():
            _gather_start(gout_bufs, gssem, grsem)
            lower_s[...], _, same_s[...] = _chunk_masks(tm)
            gmat_s[...] = _gn_matrix()
            state[...] = jnp.zeros(state.shape, F32)
            ubuf[0:HALO, :] = jnp.zeros((HALO, CONV_CH), F32)
            ubuf[HALO + tm:HALO + tm + SUB, :] = jnp.zeros((SUB, CONV_CH), F32)

        @pl.when(i > 0)
        def _():
            ubuf[0:HALO, :] = ubuf[tm:tm + HALO, :]

        ubuf[HALO:HALO + tm, :] = p_ref[:, 0:CONV_CH] * _sig(p_ref[:, CONV_CH:2 * CONV_CH])
        for r in range(tm // CONV_ROWS):
            rows = slice(r * CONV_ROWS, (r + 1) * CONV_ROWS)
            for lb_ in range(CONV_CH // LANE):
                lanes = slice(lb_ * LANE, (lb_ + 1) * LANE)
                ys_ref[rows, lanes] = bdw_ref[:, lanes] + _tap_conv(ubuf, wdw_ref, r * CONV_ROWS, CONV_FWD_TAPS, lanes)
        gmat = gmat_s[...]
        yv = ys_ref[...]
        d = yv - _gmean(yv, gmat)
        rs = lax.rsqrt(_gmean(d * d, gmat) + GN_EPS)
        z = d * rs * gain_ref[...] + bias_ref[...]
        cat_ref[:, 0:CONV_CH] = (z * _sig(z)).astype(BF16)

        lb, _ = _lower_bound(lbl_ref)
        lower, same = lower_s[...], same_s[...]
        o0 = 2 * CONV_CH
        pr = _hgrn_prep(p_ref[:, o0:o0 + HGRN_W], p_ref[:, o0 + HGRN_W:o0 + 2 * HGRN_W], lb, lower, same)
        qt_s[...] = pr["qt"].astype(BF16)
        kt_s[...] = pr["kt"].astype(BF16)
        kh_s[...] = pr["kh"].astype(BF16)
        v_s[...] = p_ref[:, o0 + 2 * HGRN_W:o0 + 3 * HGRN_W].astype(BF16)
        egl_s[...] = jnp.exp(pr["Gl"])
        tri = _tri()

        def chunk(ci, carry):
            r0 = pl.multiple_of(ci * CHUNK, CHUNK)
            rows = pl.ds(r0, CHUNK)
            for h in range(N_HEADS):
                ls = pl.ds(h * HEAD_D, HEAD_D)
                qc, kc, hc, vc = qt_s[rows, ls], kt_s[rows, ls], kh_s[rows, ls], v_s[rows, ls]
                s0 = state[h]
                s0b = s0.astype(BF16)
                st_ref[ci, h] = s0
                att =jnp.where(tri, _dot_nt(qc, kc), 0.0).astype(BF16)
                o_ref[rows, ls] = _dot(att, vc) + _dot_nt(qc, s0b)
                state[h] = s0 * egl_s[pl.ds(r0, 1), ls] + _dot_tn(vc, hc)
            return carry

        lax.fori_loop(0, nch, chunk, 0)
        for h in range(N_HEADS):
            sl = slice(h * HEAD_D, (h + 1) * HEAD_D)
            oh = o_ref[:, sl]
            gh = p_ref[:, o0 + 3 * HGRN_W + h * HEAD_D:o0 + 3 * HGRN_W + (h + 1) * HEAD_D]
            rsh = lax.rsqrt(jnp.mean(oh * oh, axis=-1, keepdims=True) + RMS_EPS)
            hg = (oh * rsh) * gout_ref[:, sl] * (gh * _sig(gh))
            cat_ref[:, CONV_CH + h * HEAD_D:CONV_CH + (h + 1) * HEAD_D] = hg.astype(BF16)

        @pl.when(i == nt - 1)
        def _():
            _gather_finish(gout_bufs, gssem, grsem)

    tile = lambda cols: pl.BlockSpec((tm, cols), lambda i: (i, 0))
    hbm = pl.BlockSpec(memory_space=pl.ANY)
    n_in = 7
    outs = pl.pallas_call(
        body, name="mixers_fwd", grid=(nt,),
        in_specs=[tile(IN_COLS), _full((HALO, CONV_CH))] + [_full((1, CONV_CH))] * 4 + [_full((2, HGRN_W))]
        + [hbm] * ng,
        out_specs=[tile(D_MODEL), tile(CONV_CH), tile(HGRN_W),
                   pl.BlockSpec((nch, N_HEADS, HEAD_D, HEAD_D), lambda i: (i, 0, 0, 0))] + [hbm] * ng,
        out_shape=[_big((T, D_MODEL), BF16), _big((T, CONV_CH), F32), _big((T, HGRN_W), F32),
                   _big((T // CHUNK, N_HEADS, HEAD_D, HEAD_D), F32)] + [_big(b.shape, BF16) for b in gbufs],
        input_output_aliases={n_in + t: 4 + t for t in range(ng)},
        scratch_shapes=[pltpu.VMEM((tm + HALO + SUB, CONV_CH), F32), pltpu.VMEM((N_HEADS, HEAD_D, HEAD_D), F32),
                        pltpu.VMEM((tm, HGRN_W), BF16), pltpu.VMEM((tm, HGRN_W), BF16),
                        pltpu.VMEM((tm, HGRN_W), BF16), pltpu.VMEM((tm, HGRN_W), BF16),
                        pltpu.VMEM((tm, HGRN_W), F32), pltpu.VMEM((tm, tm), BF16), pltpu.VMEM((tm, tm), BF16),
                        pltpu.VMEM((CONV_CH, CONV_CH), BF16)] + _gather_sems(ng),
        compiler_params=_cp(("arbitrary",), 56),
    )(_hbm(p), wdw, *vecs, lbl, *[_hbm(b) for b in gbufs])
    return outs[:4], outs[4:]


def _mix_out_fwd(cat, w_out, x, modr, g_post, g_ffn):
    T = x.shape[0]
    tm = _tok_tile(T)

    def body(cat_ref, w_ref, x_ref, mod_ref, gp_ref, gf_ref, y_ref, x1_ref, h2_ref):
        yv = _dot(cat_ref[...], w_ref[...])
        y_ref[...] = yv
        rs = lax.rsqrt(jnp.mean(yv * yv, axis=-1, keepdims=True) + RMS_EPS)
        x1 = x_ref[...] + mod_ref[2:3, :] * ((yv * rs) * gp_ref[...])
        x1_ref[...] = x1
        rs1 = lax.rsqrt(jnp.mean(x1 * x1, axis=-1, keepdims=True) + RMS_EPS)
        h2 = (x1 * rs1) * gf_ref[...] * (1.0 + mod_ref[4:5, :]) + mod_ref[3:4, :]
        h2_ref[...] = h2.astype(BF16)

    tile = pl.BlockSpec((tm, D_MODEL), lambda i: (i, 0))
    return pl.pallas_call(
        body, name="mix_out_fwd", grid=(T // tm,),
        in_specs=[tile, _full((D_MODEL, D_MODEL)), tile, _full((6, D_MODEL)), _full((1, D_MODEL)),
                  _full((1, D_MODEL))],
        out_specs=[tile, tile, tile],
        out_shape=[_big((T,D_MODEL), F32), _big((T,D_MODEL), F32),
                   _big((T,D_MODEL), BF16)],
        compiler_params=_cp(("arbitrary",), 40),
    )(*_hbm(cat, w_out, x), modr, g_post, g_ffn)


def _row_chains(rows, n=2):
    step = rows // n
    return [slice(k * step, (k + 1) * step) for k in range(n)]


def _ffn_blocks():
    return D_FF // FFN_BLOCK, (D_FF // N_CHIPS) // FFN_BLOCK


def _ffn_fwd(h2, w_up_g, w_down, x1, target, modr, g_post):
    T = h2.shape[0]
    tm = min(FFN_TILE, T)
    fb = FFN_BLOCK
    nj, per = _ffn_blocks()

    def body(h_ref, wu_ref, wd_ref, x1_ref, t_ref, mod_ref, g_ref, r_ref, dy2_ref, dx2_ref, st_ref, acc):
        i, j = pl.program_id(0), pl.program_id(1)

        @pl.when((i == 0) & (j == 0))
        def _():
            st_ref[...] = jnp.zeros(st_ref.shape, F32)

        @pl.when(j == 0)
        def _():
            acc[...] = jnp.zeros(acc.shape, F32)

        for rows in _row_chains(tm):
            ra = jnp.maximum(_dot(h_ref[rows, :], wu_ref[0]), 0.0)
            rb = (ra * ra).astype(BF16)
            r_ref[rows, :] = rb
            acc[rows, :] += _dot(rb, wd_ref[...])

        @pl.when(j == nj - 1)
        def _():
            y2 = acc[...]
            rs = lax.rsqrt(jnp.mean(y2 * y2, axis=-1, keepdims=True) + RMS_EPS)
            nh = y2 * rs
            gp = g_ref[...]
            err = x1_ref[...] + mod_ref[5:6, :] * (nh * gp) - t_ref[...]
            dx2 = err * (1.0 / D_MODEL)
            dx2_ref[...] = dx2
            st_ref[0:1, :] += _colsum(err * err)
            st_ref[1:2, :] += _colsum(dx2 * (nh * gp))
            dn = dx2 * mod_ref[5:6, :]
            st_ref[2:3, :] += _colsum(dn * nh)
            dy2_ref[...] = _rms_bwd(dn * gp, nh, rs).astype(BF16)

    tile = pl.BlockSpec((tm, D_MODEL), lambda i, j: (i, 0))
    return pl.pallas_call(
        body, name="ffn_fwd", grid=(T // tm, nj),
        in_specs=[tile, pl.BlockSpec((1, D_MODEL, fb), lambda i, j: (j // per, 0, j % per)),
                  pl.BlockSpec((fb, D_MODEL), lambda i, j: (j, 0)), tile, tile,
                  _full((6, D_MODEL)), _full((1, D_MODEL))],
        out_specs=[pl.BlockSpec((tm, fb), lambda i, j: (i, j)), tile, tile, _full((8, D_MODEL))],
        out_shape=[_big((T, D_FF), BF16), _big((T, D_MODEL), BF16), _big((T, D_MODEL), F32),
                   jax.ShapeDtypeStruct((8, D_MODEL), F32)],
        scratch_shapes=[pltpu.VMEM((tm, D_MODEL), F32)],
        compiler_params=_cp(("arbitrary", "arbitrary"), 56),
    )(*_hbm(h2, w_up_g, w_down, x1, target), modr, g_post)


def _rms_bwd(dxn, xn, rs):
    return rs * (dxn - xn * jnp.mean(dxn * xn, axis=-1, keepdims=True))


def _ffn_bwd(dy2, r, x1, dx2, w_up_g, w_down, modr, g_ffn):
    T = dx2.shape[0]
    tm = min(FFN_TILE, T)
    fb = FFN_BLOCK
    nj, per = _ffn_blocks()

    def body(dy2_ref, r_ref, x1_ref, dx2_ref, wu_ref, wd_ref, mod_ref, gf_ref, da_ref, dx1_ref, st_ref, dh_s):
        i, j = pl.program_id(0), pl.program_id(1)

        @pl.when((i == 0) & (j == 0))
        def _():
            st_ref[...] = jnp.zeros(st_ref.shape, F32)

        @pl.when(j == 0)
        def _():
            dh_s[...] = jnp.zeros(dh_s.shape, F32)

        for rows in _row_chains(tm):
            ra = jnp.sqrt(r_ref[rows, :].astype(F32))
            da = (_dot_nt(dy2_ref[rows, :], wd_ref[...]) * (2.0 * ra)).astype(BF16)
            da_ref[rows, :] = da
            dh_s[rows, :] += _dot_nt(da, wu_ref[0])

        @pl.when(j == nj - 1)
        def _():
            dh = dh_s[...]
            x1v = x1_ref[...]
            rs1 = lax.rsqrt(jnp.mean(x1v * x1v, axis=-1, keepdims=True) + RMS_EPS)
            xn = x1v * rs1
            st_ref[0:1, :] += _colsum(dh)
            st_ref[1:2, :] += _colsum(dh * (xn * gf_ref[...]))
            dsc = dh * (1.0 + mod_ref[4:5, :])
            st_ref[2:3, :] += _colsum(dsc * xn)
            dx1_ref[...] = dx2_ref[...] + _rms_bwd(dsc * gf_ref[...], xn, rs1)

    tile = pl.BlockSpec((tm, D_MODEL), lambda i, j: (i, 0))
    ftile = pl.BlockSpec((tm, fb), lambda i, j: (i, j))
    return pl.pallas_call(
        body, name="ffn_bwd", grid=(T // tm, nj),
        in_specs=[tile, ftile, tile, tile, pl.BlockSpec((1, D_MODEL, fb), lambda i, j: (j // per, 0, j % per)),
                  pl.BlockSpec((fb, D_MODEL), lambda i, j: (j, 0)), _full((6, D_MODEL)), _full((1, D_MODEL))],
        out_specs=[ftile, tile, _full((8, D_MODEL))],
        out_shape=[_big((T, D_FF), BF16), _big((T, D_MODEL), F32), jax.ShapeDtypeStruct((8, D_MODEL), F32)],
        scratch_shapes=[pltpu.VMEM((tm, D_MODEL), F32)],
        compiler_params=_cp(("arbitrary", "arbitrary"), 56),
    )(*_hbm(dy2, r, x1, dx2, w_up_g, w_down), modr, g_ffn)


def _mix_out_bwd(dx1, y, w_out, modr, g_post, swap):
    T = dx1.shape[0]
    tm = _tok_tile(T)
    nt = T // tm
    ns = len(swap)

    def body(*refs):
        dx1_ref, y_ref, w_ref, mod_ref, gp_ref = refs[:5]
        s_ins = refs[5:5 + ns]
        dy_ref, dcat_ref, st_ref = refs[5 + ns:8 + ns]
        s_outs = refs[8 + ns:8 + 2 * ns]
        pssem, prsem = refs[8 + 2 * ns:]
        i = pl.program_id(0)

        @pl.when(i == 0)
        def _():
            for cp in _pair_copies(s_ins, s_outs, pssem, prsem):
                cp.start()
            st_ref[...] = jnp.zeros(st_ref.shape, F32)

        dxv, yv = dx1_ref[...], y_ref[...]
        rs = lax.rsqrt(jnp.mean(yv * yv, axis=-1, keepdims=True) + RMS_EPS)
        nh = yv * rs
        st_ref[0:1, :] += _colsum(dxv * (nh * gp_ref[...]))
        dn = dxv * mod_ref[2:3, :]
        st_ref[1:2, :] += _colsum(dn * nh)
        dy = _rms_bwd(dn * gp_ref[...], nh, rs).astype(BF16)
        dy_ref[...] = dy
        dcat_ref[...] = _dot_nt(dy, w_ref[...])

        @pl.when(i == nt - 1)
        def _():
            copies = _pair_copies(s_ins, s_outs, pssem, prsem)
            for cp in copies:
                cp.wait_recv()
            for cp in copies:
                cp.wait_send()

    tile = pl.BlockSpec((tm, D_MODEL), lambda i: (i, 0))
    hbm = pl.BlockSpec(memory_space=pl.ANY)
    outs = pl.pallas_call(
        body, name="mix_out_bwd", grid=(nt,),
        in_specs=[tile, tile, _full((D_MODEL, D_MODEL)), _full((6, D_MODEL)), _full((1, D_MODEL))] + [hbm] * ns,
        out_specs=[tile, tile, _full((8, D_MODEL))] + [hbm] * ns,
        out_shape=[_big((T, D_MODEL), BF16), _big((T, D_MODEL), F32), jax.ShapeDtypeStruct((8, D_MODEL), F32)]
        + _pair_shapes(swap),
        scratch_shapes=_pair_sems(ns),
        compiler_params=_cp(("arbitrary",), 40),
    )(*_hbm(dx1, y, w_out), modr, g_post, *[_hbm(g) for g in swap])
    return outs[:3], outs[3:]


def _mixers_bwd(p, dcat, ys, o, states, wdw, vecs, lbl, pairs_b):
    T = p.shape[0]
    tm = min(MIXB_TILE, T)
    nt = T // tm
    nch = tm // CHUNK
    hpt = tm // HALO
    nx = len(pairs_b)

    def body(*refs):
        (p_ref, ph_ref, dcat_ref, ys_ref, o_ref, st_ref, wdw_ref, bdw_ref, gain_ref, bias_ref, gout_ref,
         lbl_ref) = refs[:12]
        x_ins = refs[12:12 + nx]
        dp_ref, sb_ref, s5_ref, dw_ref = refs[12 + nx:16 + nx]
        x_outs = refs[16 + nx:16 + 2 * nx]
        (ubuf, dybuf, carry, dstate, qt_s, kt_s, kh_s, v_s, do_s, egl_s, dqt_s, dkt_s, dkh_s, dv_s, dgl_s,
         dsh, dw8, dshift, lower_s, upper_s, same_s, gmat_s, xssem, xrsem) = refs[16 + 2 * nx:]
        i = pl.program_id(0)
        tile_idx = nt - 1 - i

        @pl.when(i == 0)
        def _():
            for cp in _xchg_copies(x_ins, x_outs, xssem, xrsem):
                cp.start()
            dstate[...] = jnp.zeros(dstate.shape, F32)
            carry[...] = jnp.zeros(carry.shape, F32)
            sb_ref[...] = jnp.zeros(sb_ref.shape, F32)
            s5_ref[...] = jnp.zeros(s5_ref.shape, F32)
            dw_ref[...] = jnp.zeros(dw_ref.shape, F32)
            dw8[...] = jnp.zeros(dw8.shape, F32)
            lower_s[...], upper_s[...], same_s[...] = _chunk_masks(tm)
            gmat_s[...] = _gn_matrix()
            dsh[0:SUB, :] = jnp.zeros((SUB, CONV_CH), F32)
            dsh[SUB + tm:2 * SUB + tm, :] = jnp.zeros((SUB, CONV_CH), F32)
            ubuf[HALO + tm:HALO + tm + SUB, :] = jnp.zeros((SUB, CONV_CH), F32)

        uh = ph_ref[:, 0:CONV_CH] * _sig(ph_ref[:, CONV_CH:2 * CONV_CH])
        ubuf[0:HALO, :] = jnp.where(tile_idx > 0, uh, 0.0)
        ubuf[HALO:HALO + tm, :] = p_ref[:, 0:CONV_CH] * _sig(p_ref[:, CONV_CH:2 * CONV_CH])
        gmat = gmat_s[...]
        gain = gain_ref[...]
        yv = ys_ref[...]
        d = yv - _gmean(yv, gmat)
        rs = lax.rsqrt(_gmean(d * d, gmat) + GN_EPS)
        yn = d * rs
        z = yn * gain + bias_ref[...]
        sz = _sig(z)
        dz = dcat_ref[:, 0:CONV_CH] * (sz * (1.0 + z * (1.0 - sz)))
        dyn = dz * gain
        dyc = rs * (dyn - _gmean(dyn, gmat) - yn * _gmean(dyn * yn, gmat))
        s5_ref[0:1, :] += _colsum(dyc)
        s5_ref[1:2, :] += _colsum(dz * yn)
        s5_ref[2:3, :] += _colsum(dz)
        dybuf[tm:tm + HALO, :] = carry[...]
        dybuf[0:tm, :] = dyc
        dsh[SUB:SUB + tm, :] = dyc
        carry[...] = dyc[0:HALO, :]
        for b in range(SUB):
            dshift[...] = dsh[SUB - b:2 * SUB - b + tm, :]
            for j, off in CONV_FWD_TAPS:
                if off % SUB == b:
                    prod = dshift[...] * ubuf[off - b:off - b + tm + SUB, :]
                    dw8[j] += jnp.sum(prod.reshape((tm + SUB) // SUB, SUB, CONV_CH), axis=0)
        for r in range(tm // CONV_ROWS):
            rows = slice(r * CONV_ROWS, (r + 1) * CONV_ROWS)
            for lb_ in range(CONV_CH // LANE):
                lanes = slice(lb_ * LANE, (lb_ + 1) * LANE)
                glanes = slice(CONV_CH + lb_ * LANE, CONV_CH + (lb_ + 1) * LANE)
                acc = _tap_conv(dybuf, wdw_ref, r * CONV_ROWS, CONV_BWD_TAPS, lanes)
                val = p_ref[rows, lanes]
                sg = _sig(p_ref[rows, glanes])
                dval = acc * sg
                dgate = acc * val * (sg * (1.0 - sg))
                dp_ref[rows, lanes] = dval.astype(BF16)
                dp_ref[rows, glanes] = dgate.astype(BF16)
                sb_ref[0:1, lanes] += _colsum(dval)
                sb_ref[0:1, glanes] += _colsum(dgate)

        o0 = 2 * CONV_CH
        for h in range(N_HEADS):
            sl = slice(h * HEAD_D, (h + 1) * HEAD_D)
            gsl = slice(o0 + 3 * HGRN_W + h * HEAD_D, o0 + 3 * HGRN_W + (h + 1) * HEAD_D)
            oh = o_ref[:, sl]
            gh = p_ref[:, gsl]
            dh = dcat_ref[:, CONV_CH + h * HEAD_D:CONV_CH + (h + 1) * HEAD_D]
            gout = gout_ref[:, sl]
            rsh = lax.rsqrt(jnp.mean(oh * oh, axis=-1, keepdims=True) + RMS_EPS)
            on = oh * rsh
            sgg = _sig(gh)
            dgh = dh * (on * gout) * (sgg * (1.0 + gh * (1.0 - sgg)))
            dm = dh * (gh * sgg)
            s5_ref[3:4, sl] += _colsum(dm * on)
            do_s[:, sl] = _rms_bwd(dm * gout, on, rsh).astype(BF16)
            dp_ref[:, gsl] = dgh.astype(BF16)
            sb_ref[2:3, CONV_CH + h * HEAD_D:CONV_CH + (h + 1) * HEAD_D] += _colsum(dgh)

        lb, _ = _lower_bound(lbl_ref)
        lower, upper, same = lower_s[...], upper_s[...], same_s[...]
        pq = p_ref[:, o0:o0 + HGRN_W]
        pr = _hgrn_prep(pq, p_ref[:, o0 + HGRN_W:o0 + 2 * HGRN_W], lb, lower, same)
        qt_s[...] = pr["qt"].astype(BF16)
        kt_s[...] = pr["kt"].astype(BF16)
        kh_s[...] = pr["kh"].astype(BF16)
        v_s[...] = p_ref[:, o0 + 2 * HGRN_W:o0 + 3 * HGRN_W].astype(BF16)
        egl_s[...] = jnp.exp(pr["Gl"])
        tri = _tri()

        def chunk(it, c_):
            ci = nch - 1 - it
            r0 = pl.multiple_of(ci * CHUNK, CHUNK)
            rows = pl.ds(r0, CHUNK)
            for h in range(N_HEADS):
                ls = pl.ds(h * HEAD_D, HEAD_D)
                qc, kc, hc, vc = qt_s[rows, ls], kt_s[rows, ls], kh_s[rows, ls], v_s[rows, ls]
                dob = do_s[rows, ls]
                s0 = st_ref[ci, h]
                s0b = s0.astype(BF16)
                ds1 = dstate[h]
                ds1b = ds1.astype(BF16)
                egl = egl_s[pl.ds(r0, 1), ls]
                att = jnp.where(tri, _dot_nt(qc, kc), 0.0).astype(BF16)
                datt = jnp.where(tri, _dot_nt(dob, vc), 0.0).astype(BF16)
                dv_s[rows, ls] = _dot_tn(att, dob) + _dot_nt(hc, ds1b)
                dqt_s[rows, ls] = _dot(datt, kc) + _dot(dob, s0b)
                dkt_s[rows, ls] = _dot_tn(datt, qc)
                dkh_s[rows, ls] = _dot(vc, ds1b)
                dgl = egl * _colsum(ds1 * s0)
                dgl_s[rows, ls] = jnp.broadcast_to(dgl, (CHUNK, HEAD_D))
                dstate[h] = ds1 * egl + _dot_tn(dob, qc)
            return c_

        lax.fori_loop(0, nch, chunk, 0)
        dqt, dkt, dkh = dqt_s[...], dkt_s[...], dkh_s[...]
        dk = dkt * pr["enG"] + dkh * pr["eGlG"]
        khk = dkh * kh_s[...].astype(F32)
        dG = dqt * qt_s[...].astype(F32) - dkt * kt_s[...].astype(F32) - khk
        dlogf = _mm3(upper, dG) + _mm3(same, khk) + dgl_s[...]
        df = dlogf / pr["f"] - dk
        sf, sq = pr["sf"], pr["sq"]
        s5_ref[4:5, :] += _colsum(df * (1.0 - sf))
        dfl = df * (1.0 - lb) * (sf * (1.0 - sf))
        dq = (dqt * pr["eG"]) * (sq * (1.0 + pq * (1.0 - sq)))
        dvv = dv_s[...]
        dp_ref[:, o0:o0 + HGRN_W] = dq.astype(BF16)
        dp_ref[:, o0 + HGRN_W:o0 + 2 * HGRN_W] = dfl.astype(BF16)
        dp_ref[:, o0 + 2 * HGRN_W:o0 + 3 * HGRN_W] = dvv.astype(BF16)
        sb_ref[1:2, 0:HGRN_W] += _colsum(dq)
        sb_ref[1:2, HGRN_W:2 * HGRN_W] += _colsum(dfl)
        sb_ref[2:3, 0:HGRN_W] += _colsum(dvv)

        @pl.when(i == nt - 1)
        def _():
            for j in range(CONV_K):
                dw_ref[j:j + 1, :] = _colsum(dw8[j])
            copies = _xchg_copies(x_ins, x_outs, xssem, xrsem)
            for cp in copies:
                cp.wait_recv()
            for cp in copies:
                cp.wait_send()

    rev = lambda cols: pl.BlockSpec((tm, cols), lambda i: (nt - 1 - i, 0))
    halo = pl.BlockSpec((HALO, 2 * CONV_CH), lambda i: (jnp.maximum((nt - 1 - i) * hpt - 1, 0), 0))
    wide = lambda n: pltpu.VMEM((tm, HGRN_W), n)
    hbm = pl.BlockSpec(memory_space=pl.ANY)
    outs = pl.pallas_call(
        body, name="mixers_bwd", grid=(nt,),
        in_specs=[rev(IN_COLS), halo, rev(D_MODEL), rev(CONV_CH), rev(HGRN_W),
                  pl.BlockSpec((nch, N_HEADS, HEAD_D, HEAD_D), lambda i: (nt - 1 - i, 0, 0, 0)),
                  _full((HALO, CONV_CH))] + [_full((1, CONV_CH))] * 4 + [_full((2, HGRN_W))] + [hbm] * nx,
        out_specs=[rev(IN_COLS), _full((8, D_MODEL)), _full((8, CONV_CH)), _full((HALO, CONV_CH))] + [hbm] * nx,
        out_shape=[_big((T, IN_COLS), BF16), jax.ShapeDtypeStruct((8, D_MODEL), F32),
                   jax.ShapeDtypeStruct((8, CONV_CH), F32), jax.ShapeDtypeStruct((HALO, CONV_CH), F32)]
        + [_big(pb.shape, BF16) for pb in pairs_b],
        scratch_shapes=[pltpu.VMEM((tm + HALO + SUB, CONV_CH), F32), pltpu.VMEM((tm + HALO, CONV_CH), F32),
                        pltpu.VMEM((HALO, CONV_CH), F32), pltpu.VMEM((N_HEADS, HEAD_D, HEAD_D), F32),
                        wide(BF16), wide(BF16), wide(BF16), wide(BF16), wide(BF16),
                        wide(F32), wide(F32), wide(F32), wide(F32), wide(F32), wide(F32),
                        pltpu.VMEM((tm + 2 * SUB, CONV_CH), F32), pltpu.VMEM((HALO, SUB, CONV_CH), F32),
                        pltpu.VMEM((tm + SUB, CONV_CH), F32), pltpu.VMEM((tm, tm), BF16), pltpu.VMEM((tm, tm), BF16),
                        pltpu.VMEM((tm, tm), BF16), pltpu.VMEM((CONV_CH, CONV_CH), BF16)]
        + _xchg_sems(nx),
        compiler_params=_cp(("arbitrary",), 56),
    )(*_hbm(p, p, dcat, ys, o, states), wdw, *vecs, lbl, *[_hbm(pb) for pb in pairs_b])
    return outs[:4], outs[4:]


def _mix_in_bwd(dp, w_in_g, x, dx1, modr, g_pre, pairs_b):
    T = x.shape[0]
    tm = _tok_tile(T)
    nt = T // tm
    nb = IN_COLS // N_CHIPS
    nx = len(pairs_b)

    def body(*refs):
        dp_ref, w_ref, x_ref, dx1_ref, mod_ref, g_ref = refs[:6]
        x_ins = refs[6:6 + nx]
        gx_ref, st_ref = refs[6 + nx:8 + nx]
        x_outs = refs[8 + nx:8 + 2 * nx]
        xssem, xrsem = refs[8 + 2 * nx:]
        i = pl.program_id(0)

        @pl.when(i == 0)
        def _():
            for cp in _xchg_copies(x_ins, x_outs, xssem, xrsem):
                cp.start()
            st_ref[...] = jnp.zeros(st_ref.shape, F32)

        dh = None
        for j in range(N_CHIPS):
            part = _dot_nt(dp_ref[:, j * nb:(j + 1) * nb], w_ref[j])
            dh = part if dh is None else dh + part
        xv = x_ref[...]
        rs = lax.rsqrt(jnp.mean(xv * xv, axis=-1, keepdims=True) + RMS_EPS)
        xn = xv * rs
        st_ref[0:1, :] += _colsum(dh)
        st_ref[1:2, :] += _colsum(dh * (xn * g_ref[...]))
        dsc = dh * (1.0 + mod_ref[1:2, :])
        st_ref[2:3, :] += _colsum(dsc * xn)
        gx_ref[...] = dx1_ref[...] + _rms_bwd(dsc * g_ref[...], xn, rs)

        @pl.when(i == nt - 1)
        def _():
            copies = _xchg_copies(x_ins, x_outs, xssem, xrsem)
            for cp in copies:
                cp.wait_recv()
            for cp in copies:
                cp.wait_send()

    tile = pl.BlockSpec((tm, D_MODEL), lambda i: (i, 0))
    hbm = pl.BlockSpec(memory_space=pl.ANY)
    outs = pl.pallas_call(
        body, name="mix_in_bwd", grid=(nt,),
        in_specs=[pl.BlockSpec((tm, IN_COLS), lambda i: (i, 0)), _full((N_CHIPS, D_MODEL, nb)), tile, tile,
                  _full((6, D_MODEL)), _full((1, D_MODEL))] + [hbm] * nx,
        out_specs=[tile, _full((8, D_MODEL))] + [hbm] * nx,
        out_shape=[_big((T, D_MODEL), F32), jax.ShapeDtypeStruct((8, D_MODEL), F32)]
        + [_big(pb.shape, BF16) for pb in pairs_b],
        scratch_shapes=_xchg_sems(nx),
        compiler_params=_cp(("arbitrary",), 48),
    )(*_hbm(dp, w_in_g, x, dx1), modr, g_pre, *[_hbm(pb) for pb in pairs_b])
    return outs[:2], outs[2:]


def _weight_grad(a, b, a_blocked, b_blocked, name):
    T = a.shape[0]
    tt = min(GRAD_TILE, T)
    nt = T // tt
    ka = a.shape[1] // N_CHIPS if a_blocked else a.shape[1]
    nb = b.shape[1] // N_CHIPS if b_blocked else b.shape[1]

    def body(a_ref, b_ref, o_ref, ob_ref):
        t = pl.program_id(1)

        @pl.when(t == 0)
        def _():
            o_ref[...] = jnp.zeros(o_ref.shape, F32)

        for cols in _row_chains(nb):
            o_ref[0, :, cols] += _dot_tn(a_ref[...], b_ref[:, cols])

        @pl.when(t == nt - 1)
        def _():
            ob_ref[0] = o_ref[0].astype(BF16)

    blk = pl.BlockSpec((1, ka, nb), lambda j, t: (j, 0, 0))
    return pl.pallas_call(
        body, name=name, grid=(N_CHIPS, nt),
        in_specs=[pl.BlockSpec((tt, ka), (lambda j, t: (t, j)) if a_blocked else (lambda j, t: (t, 0))),
                  pl.BlockSpec((tt, nb), (lambda j, t: (t, j)) if b_blocked else (lambda j, t: (t, 0)))],
        out_specs=[blk, blk],
        out_shape=[_big((N_CHIPS, ka, nb), F32), _big((N_CHIPS, ka, nb), BF16)],
        compiler_params=_cp(("arbitrary", "arbitrary"), 48),
    )(*_hbm(a, b))


R_LOSS = 0
R_FFN = 8
R_OUT = 16
R_IN = 24
R_BIN = 32
R_512 = 40
R_DW = 48
N_STAT_ROWS = 80
MOD_ROWS = (R_IN + 0, R_IN + 1, R_OUT + 0, R_FFN + 0, R_FFN + 1, R_LOSS + 1)


def _small_update(gath, params):
    names = ["b_ada", "lb_logits", "g_pre_mix", "b_in", "b_dw", "gn_gain", "gn_bias", "g_hgrn_out", "g_post_mix",
             "g_pre_ffn", "g_post_ffn"]
    flat = []
    for n in names:
        flat += list(params[n])
    n_in = 1 + len(flat)

    def body(*refs):
        g_ref = refs[0]
        prm = {n: refs[1 + 3 * k:4 + 3 * k] for k, n in enumerate(names)}
        outs = refs[n_in:]
        loss_ref, dmod_ref, dwdw_ref = outs[0], outs[1], outs[2]
        res = {n: outs[3 + 4 * k:7 + 4 * k] for k, n in enumerate(names)}
        red = g_ref[0]
        for dev in range(1, N_DEV):
            red = red + g_ref[dev]
        loss_ref[...] = jnp.broadcast_to(
            (0.5 / D_MODEL) * jnp.sum(red[R_LOSS:R_LOSS + 1, :], axis=-1, keepdims=True), loss_ref.shape)
        for dev in range(N_DEV):
            for k, r in enumerate(MOD_ROWS):
                dmod_ref[dev:dev + 1, k * D_MODEL:(k + 1) * D_MODEL] = g_ref[dev, r:r + 1, :]
        dwdw_ref[...] = red[R_DW:R_DW + HALO, 0:CONV_CH]

        def finish(name, pieces):
            w_ref, m_ref, v_ref = prm[name]
            g_out, d_out, m_out, v_out = res[name]
            for rsl, lsl, g in pieces:
                d, m2, v2 = _adam_math(w_ref[rsl, lsl], g, m_ref[rsl, lsl], v_ref[rsl, lsl])
                g_out[rsl, lsl] = g
                d_out[rsl, lsl] = d
                m_out[rsl, lsl] = m2
                v_out[rsl, lsl] = v2

        one = slice(0, 1)
        row = lambda r: red[r:r + 1, :]
        half = lambda r: red[r:r + 1, 0:CONV_CH]
        finish("b_ada", [(one, slice(k * D_MODEL, (k + 1) * D_MODEL), row(r)) for k, r in enumerate(MOD_ROWS)])
        finish("b_in", [(one, slice(k * D_MODEL, (k + 1) * D_MODEL), row(R_BIN + k)) for k in range(3)])
        finish("g_pre_mix", [(one, slice(None), row(R_IN + 2))])
        finish("g_post_mix", [(one, slice(None), row(R_OUT + 1))])
        finish("g_pre_ffn", [(one, slice(None), row(R_FFN + 2))])
        finish("g_post_ffn", [(one, slice(None), row(R_LOSS + 2))])
        finish("b_dw", [(one, slice(None), half(R_512 + 0))])
        finish("gn_gain", [(one, slice(None), half(R_512 + 1))])
        finish("gn_bias", [(one, slice(None), half(R_512 + 2))])
        finish("g_hgrn_out", [(one, slice(None), half(R_512 + 3))])
        s0, s1 = _lower_bound(prm["lb_logits"][0])
        dlb = half(R_512 + 4)
        finish("lb_logits", [(slice(0, 1), slice(None), dlb * s0 * (1.0 - s0)),
                             (slice(1, 2), slice(None), -dlb * s0 * s1)])

    vm = pl.BlockSpec(memory_space=pltpu.VMEM)
    out_shape = [jax.ShapeDtypeStruct((8, 128), F32), jax.ShapeDtypeStruct((N_DEV, 6 * D_MODEL), F32),
                 jax.ShapeDtypeStruct((HALO, CONV_CH), F32)]
    for n in names:
        out_shape += [jax.ShapeDtypeStruct(params[n][0].shape, F32)] * 4
    outs = pl.pallas_call(
        body, name="small_update", out_shape=out_shape,
        in_specs=[vm] * n_in, out_specs=[vm] * len(out_shape),
        compiler_params=_cp(None, 32),
    )(gath, *flat)
    return outs[0], outs[1], outs[2], {n: outs[3 + 4 * k:7 + 4 * k] for k, n in enumerate(names)}


def _wdw_adam(w, g, m, v):
    def body(w_ref, g_ref, m_ref, v_ref, d_out, m_out, v_out):
        d, m2, v2 = _adam_math(w_ref[...], g_ref[...], m_ref[...], v_ref[...])
        d_out[...] = d
        m_out[...] = m2
        v_out[...] = v2

    vm = pl.BlockSpec(memory_space=pltpu.VMEM)
    return pl.pallas_call(
        body, name="wdw_adam", out_shape=[jax.ShapeDtypeStruct(w.shape, F32)] * 3,
        in_specs=[vm] * 4, out_specs=[vm] * 3, compiler_params=_cp(None, 16),
    )(w, g, m, v)


def kernel(x, c, w_ada, b_ada, lb_logits, g_pre_mix, w_in, b_in, w_dw, b_dw, gn_gain, gn_bias, g_hgrn_out, w_out, g_post_mix, g_pre_ffn, w_up, w_down, g_post_ffn, loss_target, m_w_ada, m_b_ada, m_lb_logits, m_g_pre_mix, m_w_in, m_b_in, m_w_dw, m_b_dw, m_gn_gain, m_gn_bias, m_g_hgrn_out, m_w_out, m_g_post_mix, m_g_pre_ffn, m_w_up, m_w_down, m_g_post_ffn, v_w_ada, v_b_ada, v_lb_logits, v_g_pre_mix, v_w_in, v_b_in, v_w_dw, v_b_dw, v_gn_gain, v_gn_bias, v_g_hgrn_out, v_w_out, v_g_post_mix, v_g_pre_ffn, v_w_up, v_w_down, v_g_post_ffn):
    ax, ay, ac = lax.axis_index("x"), lax.axis_index("y"), lax.axis_index("c")
    chip = 2 * ax + ay
    T = x.shape[1]
    xs, tgt = x[0], loss_target[0]
    ada_cols = w_ada.shape[2]

    b_sh = lax.dynamic_slice_in_dim(b_ada, chip * ada_cols, ada_cols, axis=1)
    wdw_pad = jnp.pad(w_dw[0], ((0, HALO - CONV_K), (0, 0)))
    chip1 = jnp.reshape(chip, (1,)).astype(jnp.int32)
    place = jnp.stack([ac, chip]).astype(jnp.int32)
    _, c8, modg, wdwg = _ada_exchange(c, w_ada[0], b_sh, wdw_pad)
    modr = modg.reshape(6, D_MODEL)
    wdw_all = jnp.transpose(wdwg, (1, 0, 2)).reshape(HALO, CONV_CH)
    bufs = {t: _cast_own(chip1, w[0], "cast_" + t)
            for w, t in ((w_in, "w_in"), (w_out, "w_out"), (w_up, "w_up"), (w_down, "w_down"))}
    vec = (b_dw, gn_gain, gn_bias, g_hgrn_out)

    p, h1, w_in_g, w_out_g = _mix_in_fwd(chip1, xs, modr, g_pre_mix, b_in.reshape(N_CHIPS, 1, IN_COLS // N_CHIPS),
                                          bufs["w_in"], bufs["w_out"])
    (cat, ys, o, states), (w_up_g, w_down_g) = _mixers_fwd(p, wdw_all, vec, lb_logits, [bufs["w_up"], bufs["w_down"]])
    w_out_f = w_out_g.reshape(D_MODEL, D_MODEL)
    w_down_f = w_down_g.reshape(D_FF, D_MODEL)
    y, x1, h2 = _mix_out_fwd(cat, w_out_f, xs, modr, g_post_mix, g_pre_ffn)
    r, dy2, dx2, st_loss = _ffn_fwd(h2, w_up_g, w_down_f, x1, tgt, modr, g_post_ffn)

    def pair_sums(grads, got, tags):
        return [_pair_sum(place, g, o_, "pair_sum_" + t) for (g, _), o_, t in zip(grads, got, tags)]

    da, dx1, st_ffn = _ffn_bwd(dy2, r, x1, dx2, w_up_g, w_down_f, modr, g_pre_ffn)
    g_up = _weight_grad(h2, da, False, True, "grad_w_up")
    g_down = _weight_grad(r, dy2, True, False, "grad_w_down")
    (dy, dcat, st_out), got_ud = _mix_out_bwd(dx1, y, w_out_f, modr, g_post_mix, [g_up[1], g_down[1]])
    g_out = _weight_grad(cat, dy, True, False, "grad_w_out")
    got_o = _pair_swap([g_out[1]], "pair_swap_w_out")
    early = pair_sums([g_out, g_up, g_down], list(got_o) + list(got_ud), ["w_out", "w_up", "w_down"])
    (dp, st_bin, st_512, dwdw), got_early = _mixers_bwd(p, dcat, ys, o, states, wdw_all, vec, lb_logits,
                                                        [pb for _, pb in early])
    g_in = _weight_grad(h1, dp, False, True, "grad_w_in")
    late = pair_sums([g_in], _pair_swap([g_in[1]], "pair_swap_w_in"), ["w_in"])
    (grad_x, st_in), got_late = _mix_in_bwd(dp, w_in_g, xs, dx1, modr, g_pre_mix, [late[0][1]])
    fulls = [_chip_sum(place, pf, gb, "chip_sum_" + t)
             for (pf, _), gb, t in zip(late + early, list(got_late) + list(got_early), ["w_in", "w_out", "w_up", "w_down"])]

    pad_lanes = lambda s: jnp.pad(s, ((0, 0), (0, D_MODEL - s.shape[1])))
    stats = jnp.concatenate([st_loss, st_ffn, st_out, st_in, st_bin, pad_lanes(st_512), pad_lanes(dwdw)], axis=0)
    (g_w_in, g_w_out, g_w_up, g_w_down), gath = _final_exchange(fulls, stats)
    small = {"b_ada": (b_ada, m_b_ada, v_b_ada), "lb_logits": (lb_logits, m_lb_logits, v_lb_logits),
             "g_pre_mix": (g_pre_mix, m_g_pre_mix, v_g_pre_mix), "b_in": (b_in, m_b_in, v_b_in),
             "b_dw": (b_dw, m_b_dw, v_b_dw), "gn_gain": (gn_gain, m_gn_gain, v_gn_gain),
             "gn_bias": (gn_bias, m_gn_bias, v_gn_bias), "g_hgrn_out": (g_hgrn_out, m_g_hgrn_out, v_g_hgrn_out),
             "g_post_mix": (g_post_mix, m_g_post_mix, v_g_post_mix), "g_pre_ffn": (g_pre_ffn, m_g_pre_ffn, v_g_pre_ffn),
             "g_post_ffn": (g_post_ffn, m_g_post_ffn, v_g_post_ffn)}
    loss_t, dmod_all, dwdw_sum, sres = _small_update(gath, small)
    loss = loss_t[0, 0]

    res = dict(sres)
    dmod_sh = lax.dynamic_slice_in_dim(dmod_all, chip * ada_cols, ada_cols, axis=1)
    res["w_ada"] = [t[None] for t in _ada_grad_adam(jnp.transpose(c8), dmod_sh, w_ada[0], m_w_ada[0], v_w_ada[0])]
    g_wdw = lax.dynamic_slice_in_dim(dwdw_sum, chip * HEAD_D, HEAD_D, axis=1)[:CONV_K][None]
    res["w_dw"] = [g_wdw] + list(_wdw_adam(w_dw, g_wdw, m_w_dw, v_w_dw))
    for name, g, w, m, v in (("w_in", g_w_in, w_in, m_w_in, v_w_in), ("w_out", g_w_out, w_out, m_w_out, v_w_out),
                             ("w_up", g_w_up, w_up, m_w_up, v_w_up), ("w_down", g_w_down, w_down, m_w_down, v_w_down)):
        d, m2, v2 = _adam_big(w[0], g, m[0], v[0], "adam_" + name)
        res[name] = [g[None], d[None], m2[None], v2[None]]

    order = ["w_ada", "b_ada", "lb_logits", "g_pre_mix", "w_in", "b_in", "w_dw", "b_dw", "gn_gain", "gn_bias",
             "g_hgrn_out", "w_out", "g_post_mix", "g_pre_ffn", "w_up", "w_down", "g_post_ffn"]
    out = [loss, grad_x[None]]
    for k in range(4):
        out += [res[n][k] for n in order]
    return tuple(out)
```

```python
import jax
import jax.numpy as jnp
from jax import lax
from jax.experimental import pallas as pl
from jax.experimental.pallas import tpu as pltpu

F32, BF16 = jnp.float32, jnp.bfloat16
D_MODEL = 1024
CONV_CH = 512
HGRN_W = 512
N_HEADS = 4
HEAD_D = 128
CONV_K = 31
GN_GROUP = 64
GN_SHIFT = 6
IN_COLS = 3072
D_FF = 4096
CHUNK = 64
CHUNK_SHIFT = 6
N_CHIPS = 4
N_DEV = 8
RMS_EPS = 1e-6
GN_EPS = 1e-5
ADAM_LR, ADAM_B1, ADAM_B2, ADAM_EPS, ADAM_WD, ADAM_STEP = 0.001, 0.9, 0.999, 1e-08, 0.01, 10
TOK_TILE = 512
MIXIN_TILE = 1024
MIXB_TILE = 256
FFN_TILE = 1024
FFN_BLOCK = 512
GRAD_TILE = 2048
HALO = 32
SUB = 8
LANE = 128
CONV_ROWS = 128
MIB = 1 << 20
MESH = pl.DeviceIdType.MESH
OTHER_CHIPS = ((0, 1), (1, 0), (1, 1))


def _cp(sem=None, vmem_mib=48):
    return pltpu.CompilerParams(dimension_semantics=sem, vmem_limit_bytes=vmem_mib * MIB)


def _dot(a, b):
    return jnp.dot(a, b, preferred_element_type=F32)


def _dot_nt(a, b):
    return lax.dot_general(a, b, (((1,), (1,)), ((), ())), preferred_element_type=F32)


def _dot_tn(a, b):
    return lax.dot_general(a, b, (((0,), (0,)), ((), ())), preferred_element_type=F32)


def _sig(v):
    return 0.5 * jnp.tanh(0.5 * v) + 0.5


def _colsum(v):
    return jnp.sum(v, axis=0, keepdims=True)


def _flip(v, b):
    return 1 - v if b else v


def _rcopy(src, dst, ssem, rsem, dev):
    return pltpu.make_async_remote_copy(src_ref=src, dst_ref=dst, send_sem=ssem, recv_sem=rsem,
                                        device_id=dev, device_id_type=MESH)


def _place():
    return lax.axis_index("x"), lax.axis_index("y"), lax.axis_index("c")


def _full(shape):
    return pl.BlockSpec(shape, lambda *_: (0,) * len(shape))


def _big(shape, dtype):
    return pltpu.HBM(shape, dtype)


def _hbm(*arrays):
    out = [pltpu.with_memory_space_constraint(a, pltpu.HBM) for a in arrays]
    return out[0] if len(out) == 1 else out


def _split2(v):
    hi = v.astype(BF16)
    lo = (v - hi.astype(F32)).astype(BF16)
    return hi, lo


def _split3(v):
    h1 = v.astype(BF16)
    r1 = v - h1.astype(F32)
    h2 = r1.astype(BF16)
    h3 = (r1 - h2.astype(F32)).astype(BF16)
    return h1, h2, h3


def _mm3s(mat, parts):
    h1, h2, h3 = parts
    return _dot(mat, h1) + _dot(mat, h2) + _dot(mat, h3)


def _mm3(mat, v):
    return _mm3s(mat, _split3(v))


def _gn_matrix():
    r = lax.broadcasted_iota(jnp.int32, (CONV_CH, CONV_CH), 0) >> GN_SHIFT
    c = lax.broadcasted_iota(jnp.int32, (CONV_CH, CONV_CH), 1) >> GN_SHIFT
    return jnp.where(r == c, 1.0 / GN_GROUP, 0.0).astype(BF16)


def _gmean(v, gmat):
    hi, lo = _split2(v)
    return _dot(hi, gmat) + _dot(lo, gmat)


def _chunk_masks(tm):
    r = lax.broadcasted_iota(jnp.int32, (tm, tm), 0)
    c = lax.broadcasted_iota(jnp.int32, (tm, tm), 1)
    same = (r >> CHUNK_SHIFT) == (c >> CHUNK_SHIFT)
    one = lambda m: jnp.where(m, 1.0, 0.0).astype(BF16)
    return one(same & (c <= r)), one(same & (c >= r)), one(same)


def _tri():
    return lax.broadcasted_iota(jnp.int32, (CHUNK, CHUNK), 0) >= lax.broadcasted_iota(jnp.int32, (CHUNK, CHUNK), 1)


def _lower_bound(lbl_ref):
    l0, l1 = lbl_ref[0:1, :], lbl_ref[1:2, :]
    mx = jnp.maximum(l0, l1)
    e0, e1 = jnp.exp(l0 - mx), jnp.exp(l1 - mx)
    return e0 / (e0 + e1), e1 / (e0 + e1)


CONV_FWD_TAPS = tuple((j, HALO - (CONV_K - 1) + j) for j in range(CONV_K))
CONV_BWD_TAPS = tuple((j, (CONV_K - 1) - j) for j in range(CONV_K))


def _tap_conv(src_ref, w_ref, row0, taps, lanes):
    acc = None
    for b in range(SUB):
        pb = None
        for j, off in taps:
            if off % SUB == b:
                lo = row0 + off - b
                term = w_ref[j:j + 1, lanes] * src_ref[lo:lo + CONV_ROWS + SUB, lanes]
                pb = term if pb is None else pb + term
        if pb is not None:
            sh = pb[b:b + CONV_ROWS, :]
            acc = sh if acc is None else acc + sh
    return acc


def _hgrn_prep(pq, pf, lb, lower, same):
    sq = _sig(pq)
    qf = pq * sq
    sf = _sig(pf)
    f = lb + (1.0 - lb) * sf
    logf = jnp.log(f)
    k = 1.0 - f
    parts = _split3(logf)
    G = _mm3s(lower, parts)
    Gl = _mm3s(same, parts)
    eG, enG, eGlG = jnp.exp(G), jnp.exp(-G), jnp.exp(Gl - G)
    return dict(sq=sq, sf=sf, f=f, Gl=Gl, eG=eG, enG=enG, eGlG=eGlG, qt=qf * eG, kt=k * enG, kh=k * eGlG)


def _ada_exchange(c_row, w_ada, b_sh, wdw_pad):
    ncol = w_ada.shape[1]

    def body(c_ref, w_ref, b_ref, wdw_ref, call_ref, c8_ref, modg_ref, wdwg_ref, rows_s, sa, ra, sw, rw, sm, rm):
        x, y, c = _place()
        me = 4 * x + 2 * y + c
        chip = 2 * x + y
        cv = c_ref[...]
        call_ref[me] = cv * _sig(cv)
        wdwg_ref[chip] = wdw_ref[...]
        sends = []
        for m in range(1, N_DEV):
            peer = (_flip(x, m >> 2), _flip(y, (m >> 1) & 1), _flip(c, m & 1))
            cp = _rcopy(call_ref.at[me], call_ref.at[me], sa.at[m - 1], ra.at[m - 1], peer)
            cp.start()
            sends.append(cp)
        for k, (fx, fy) in enumerate(OTHER_CHIPS):
            peer = (_flip(x, fx), _flip(y, fy), c)
            cp = _rcopy(wdwg_ref.at[chip], wdwg_ref.at[chip], sw.at[k], rw.at[k], peer)
            cp.start()
            sends.append(cp)
        for m in range(1, N_DEV):
            peer = (_flip(x, m >> 2), _flip(y, (m >> 1) & 1), _flip(c, m & 1))
            pid = 4 * peer[0] + 2 * peer[1] + peer[2]
            _rcopy(call_ref.at[pid], call_ref.at[pid], sa.at[m - 1], ra.at[m - 1], peer).wait_recv()
        for b in range(N_DEV):
            c8_ref[b:b + 1, :] = call_ref[b]
        mod_all = _dot(c8_ref[...].astype(BF16), w_ref[...].astype(BF16)) + b_ref[...]
        for b in range(N_DEV):
            rows_s[b] = mod_all[b:b + 1, :]
        modg_ref[chip] = rows_s[me]
        for k, (fx, fy) in enumerate(OTHER_CHIPS):
            peer = (_flip(x, fx), _flip(y, fy), c)
            pid = 4 * peer[0] + 2 * peer[1] + peer[2]
            cp = _rcopy(rows_s.at[pid], modg_ref.at[chip], sm.at[k], rm.at[k], peer)
            cp.start()
            sends.append(cp)
        for k, (fx, fy) in enumerate(OTHER_CHIPS):
            peer = (_flip(x, fx), _flip(y, fy), c)
            pchip = 2 * peer[0] + peer[1]
            _rcopy(rows_s.at[0], modg_ref.at[pchip], sm.at[k], rm.at[k], peer).wait_recv()
            _rcopy(wdwg_ref.at[pchip], wdwg_ref.at[pchip], sw.at[k], rw.at[k], peer).wait_recv()
        for cp in sends:
            cp.wait_send()

    vm = pl.BlockSpec(memory_space=pltpu.VMEM)
    return pl.pallas_call(
        body, name="ada_exchange",
        out_shape=[jax.ShapeDtypeStruct((N_DEV, 1, D_MODEL), F32), jax.ShapeDtypeStruct((N_DEV, D_MODEL), F32),
                   jax.ShapeDtypeStruct((N_CHIPS, 1, ncol), F32), jax.ShapeDtypeStruct((N_CHIPS, HALO, HEAD_D), F32)],
        in_specs=[vm] * 4, out_specs=[vm] * 4,
        scratch_shapes=[pltpu.VMEM((N_DEV, 1, ncol), F32),
                        pltpu.SemaphoreType.DMA((N_DEV - 1,)), pltpu.SemaphoreType.DMA((N_DEV - 1,)),
                        pltpu.SemaphoreType.DMA((3,)), pltpu.SemaphoreType.DMA((3,)),
                        pltpu.SemaphoreType.DMA((3,)), pltpu.SemaphoreType.DMA((3,))],
        compiler_params=_cp(None, 32),
    )(c_row, w_ada, b_sh, wdw_pad)


def _cast_own(chip1, shard, name):
    rows, cols = shard.shape
    tr = _row_tile(rows)

    def body(ch_ref, s_ref, o_ref):
        o_ref[0] = s_ref[...].astype(BF16)

    gs = pltpu.PrefetchScalarGridSpec(
        num_scalar_prefetch=1, grid=(rows // tr,),
        in_specs=[pl.BlockSpec((tr, cols), lambda i, ch: (i, 0))],
        out_specs=pl.BlockSpec((1, tr, cols), lambda i, ch: (ch[0], i, 0)))
    return pl.pallas_call(
        body, name=name, grid_spec=gs, out_shape=_big((N_CHIPS, rows, cols), BF16),
        compiler_params=_cp(("arbitrary",), 32),
    )(chip1, _hbm(shard))


def _slab(buf, ch, core):
    hs = buf.shape[1] // 2
    return buf.at[ch, pl.ds(core * hs, hs), :]


def _gather_start(bufs, ssem, rsem, relations=(0, 1, 2)):
    x, y, c = _place()
    chip = 2 * x + y
    for k in relations:
        fx, fy = OTHER_CHIPS[k]
        peer = (_flip(x, fx), _flip(y, fy), c)
        for t, buf in enumerate(bufs):
            _rcopy(_slab(buf, chip, c), _slab(buf, chip, c), ssem.at[t * 3 + k], rsem.at[t * 3 + k], peer).start()


def _gather_finish(bufs, ssem, rsem):
    nt = len(bufs)
    x, y, c = _place()
    chip = 2 * x + y
    sibling = (x, y, 1 - c)
    for k, (fx, fy) in enumerate(OTHER_CHIPS):
        peer = (_flip(x, fx), _flip(y, fy), c)
        pchip = 2 * peer[0] + peer[1]
        for t, buf in enumerate(bufs):
            _rcopy(_slab(buf, pchip, c), _slab(buf, pchip, c), ssem.at[t * 3 + k], rsem.at[t * 3 + k], peer).wait_recv()
            _rcopy(_slab(buf, pchip, c), _slab(buf, pchip, c), ssem.at[3 * nt + t * 3 + k],
                   rsem.at[3 * nt + t * 3 + k], sibling).start()
    for k, (fx, fy) in enumerate(OTHER_CHIPS):
        peer = (_flip(x, fx), _flip(y, fy), c)
        pchip = 2 * peer[0] + peer[1]
        for t, buf in enumerate(bufs):
            _rcopy(_slab(buf, pchip, 1 - c), _slab(buf, pchip, 1 - c), ssem.at[3 * nt + t * 3 + k],
                   rsem.at[3 * nt + t * 3 + k], sibling).wait_recv()
            _rcopy(_slab(buf, chip, c), _slab(buf, chip, c), ssem.at[t * 3 + k], rsem.at[t * 3 + k], peer).wait_send()
            _rcopy(_slab(buf, pchip, c), _slab(buf, pchip, c), ssem.at[3 * nt + t * 3 + k],
                   rsem.at[3 * nt + t * 3 + k], sibling).wait_send()


def _gather_arrive(bufs, k, ssem, rsem):
    nt = len(bufs)
    x, y, c = _place()
    fx, fy = OTHER_CHIPS[k]
    peer = (_flip(x, fx), _flip(y, fy), c)
    pchip = 2 * peer[0] + peer[1]
    for t, buf in enumerate(bufs):
        _rcopy(_slab(buf, pchip, c), _slab(buf, pchip, c), ssem.at[t * 3 + k], rsem.at[t * 3 + k], peer).wait_recv()
        _rcopy(_slab(buf, pchip, c), _slab(buf, pchip, c), ssem.at[3 * nt + t * 3 + k],
               rsem.at[3 * nt + t * 3 + k], (x, y, 1 - c)).start()
    for t, buf in enumerate(bufs):
        _rcopy(_slab(buf, pchip, 1 - c), _slab(buf, pchip, 1 - c), ssem.at[3 * nt + t * 3 + k],
               rsem.at[3 * nt + t * 3 + k], (x, y, 1 - c)).wait_recv()


def _gather_sends_done(bufs, ssem, rsem):
    nt = len(bufs)
    x, y, c = _place()
    chip = 2 * x + y
    for k, (fx, fy) in enumerate(OTHER_CHIPS):
        peer = (_flip(x, fx), _flip(y, fy), c)
        pchip = 2 * peer[0] + peer[1]
        for t, buf in enumerate(bufs):
            _rcopy(_slab(buf, chip, c), _slab(buf, chip, c), ssem.at[t * 3 + k], rsem.at[t * 3 + k], peer).wait_send()
            _rcopy(_slab(buf, pchip, c), _slab(buf, pchip, c), ssem.at[3 * nt + t * 3 + k],
                   rsem.at[3 * nt + t * 3 + k], (x, y, 1 - c)).wait_send()


def _gather_sems(nt):
    return [pltpu.SemaphoreType.DMA((6 * nt,)), pltpu.SemaphoreType.DMA((6 * nt,))]


def _pair_copies(ins, outs, ssem, rsem):
    x, y, c = _place()
    copies = []
    for t in range(len(ins)):
        hs = ins[t].shape[1] // 2
        copies.append(_rcopy(ins[t].at[:, pl.ds((1 - c) * hs, hs), :], outs[t], ssem.at[t], rsem.at[t], (x, y, 1 - c)))
    return copies


def _pair_shapes(grads):
    return [_big((g.shape[0], g.shape[1] // 2, g.shape[2]), g.dtype) for g in grads]


def _pair_sems(nt):
    return [pltpu.SemaphoreType.DMA((nt,)), pltpu.SemaphoreType.DMA((nt,))]


def _pair_swap(grads, name):
    nt = len(grads)
    hbm = pl.BlockSpec(memory_space=pl.ANY)

    def body(*refs):
        copies = _pair_copies(refs[:nt], refs[nt:2 * nt], refs[2 * nt], refs[2 * nt + 1])
        for cp in copies:
            cp.start()
        for cp in copies:
            cp.wait_recv()
        for cp in copies:
            cp.wait_send()

    return pl.pallas_call(
        body, name=name, out_shape=_pair_shapes(grads), in_specs=[hbm] * nt, out_specs=[hbm] * nt,
        scratch_shapes=_pair_sems(nt),
    )(*[_hbm(g) for g in grads])


def _xchg_copies(ins, outs, ssem, rsem):
    x, y, c = _place()
    copies = []
    for k, (fx, fy) in enumerate(OTHER_CHIPS):
        peer = (_flip(x, fx), _flip(y, fy), c)
        for t in range(len(ins)):
            copies.append(_rcopy(ins[t].at[k], outs[t].at[k], ssem.at[t * 3 + k], rsem.at[t * 3 + k], peer))
    return copies


def _xchg_sems(nt):
    return [pltpu.SemaphoreType.DMA((3 * nt,)), pltpu.SemaphoreType.DMA((3 * nt,))]


def _final_exchange(fulls, stats):
    nt = len(fulls)
    rows, cols = stats.shape
    hbm = pl.BlockSpec(memory_space=pl.ANY)
    vm = pl.BlockSpec(memory_space=pltpu.VMEM)

    def body(*refs):
        ins, s_ref = refs[:nt], refs[nt]
        outs, g_ref = refs[nt + 1:2 * nt + 1], refs[2 * nt + 1]
        hssem, hrsem, ssem, rsem = refs[2 * nt + 2:]
        x, y, c = _place()
        me, sibling = (x, y, c), (x, y, 1 - c)
        halves = []
        for t in range(nt):
            hs = ins[t].shape[0] // 2
            mine = pl.ds(c * hs, hs)
            cp = _rcopy(ins[t].at[mine, :], outs[t].at[mine, :], hssem.at[t], hrsem.at[t], sibling)
            cp.start()
            halves.append(cp)

        chips = [(_flip(x, fx), _flip(y, fy)) for fx, fy in OTHER_CHIPS]

        def blk(px, py, pc):
            return g_ref.at[4 * px + 2 * py + pc]

        def copy(k, block, to, src=None):
            return _rcopy(blk(*block) if src is None else src, blk(*block), ssem.at[k], rsem.at[k], to)

        g_ref[4 * x + 2 * y + c] = s_ref[...]
        first = [copy(0, me, sibling, src=s_ref)]
        first += [copy(1 + j, me, (*chip, c), src=s_ref) for j, chip in enumerate(chips)]
        for cp in first:
            cp.start()
        passed = [copy(4 + j, (*chip, c), sibling) for j, chip in enumerate(chips)]
        for j, chip in enumerate(chips):
            copy(1 + j, (*chip, c), me).wait_recv()
            passed[j].start()
        copy(0, sibling, me).wait_recv()
        for j, chip in enumerate(chips):
            copy(4 + j, (*chip, 1 - c), me).wait_recv()
        for t in range(nt):
            hs = ins[t].shape[0] // 2
            other = pl.ds((1 - c) * hs, hs)
            _rcopy(ins[t].at[other, :], outs[t].at[other, :], hssem.at[t], hrsem.at[t], sibling).wait_recv()
        for cp in first + passed + halves:
            cp.wait_send()

    outs = pl.pallas_call(
        body, name="final_exchange",
        out_shape=[_big(f.shape, F32) for f in fulls] + [jax.ShapeDtypeStruct((N_DEV, rows, cols), F32)],
        in_specs=[hbm] * nt + [vm], out_specs=[hbm] * nt + [vm],
        input_output_aliases={t: t for t in range(nt)},
        scratch_shapes=[pltpu.SemaphoreType.DMA((nt,)), pltpu.SemaphoreType.DMA((nt,)),
                        pltpu.SemaphoreType.DMA((7,)), pltpu.SemaphoreType.DMA((7,))],
        compiler_params=_cp(None, 32),
    )(*[_hbm(f) for f in fulls], stats)
    return outs[:nt], outs[nt]


def _row_tile(rows):
    return min(rows, 256)


def _pair_sum(place, grad, got, name):
    nb, hs, cols = got.shape
    tr = _row_tile(hs)
    nr = hs // tr

    def body(pl_ref, g_ref, o_ref, pf_ref, pb_ref):
        j = pl.program_id(1)
        s = g_ref[0] + o_ref[0].astype(F32)

        @pl.when(j == 0)
        def _():
            pf_ref[...] = s

        @pl.when(j > 0)
        def _():
            pb_ref[0] = s.astype(BF16)

    gs = pltpu.PrefetchScalarGridSpec(
        num_scalar_prefetch=1, grid=(nr, nb),
        in_specs=[pl.BlockSpec((1, tr, cols), lambda i, j, p: (p[1] ^ j, p[0] * nr + i, 0)),
                  pl.BlockSpec((1, tr, cols), lambda i, j, p: (p[1] ^ j, i, 0))],
        out_specs=[pl.BlockSpec((tr, cols), lambda i, j, p: (i, 0)),
                   pl.BlockSpec((1, tr, cols), lambda i, j, p: (jnp.maximum(j - 1, 0), i, 0))])
    return pl.pallas_call(
        body, name=name, grid_spec=gs,
        out_shape=[_big((hs, cols), F32), _big((nb - 1, hs, cols), BF16)],
        compiler_params=_cp(("arbitrary", "arbitrary"), 32),
    )(place, *_hbm(grad, got))


def _chip_sum(place, pair_f, got_b, name):
    nb, hs, cols = got_b.shape
    tr = _row_tile(hs)
    nr = hs // tr

    def body(pl_ref, pf_ref, gb_ref, o_ref):
        acc = pf_ref[...]
        for k in range(nb):
            acc = acc + gb_ref[k].astype(F32)
        o_ref[...] = acc

    gs = pltpu.PrefetchScalarGridSpec(
        num_scalar_prefetch=1, grid=(nr,),
        in_specs=[pl.BlockSpec((tr, cols), lambda i, p: (i, 0)),
                  pl.BlockSpec((nb, tr, cols), lambda i, p: (0, i, 0))],
        out_specs=pl.BlockSpec((tr, cols), lambda i, p: (p[0] * nr + i, 0)))
    return pl.pallas_call(
        body, name=name, grid_spec=gs,
        out_shape=_big((2 * hs, cols), F32),
        compiler_params=_cp(("arbitrary",), 32),
    )(place, *_hbm(pair_f, got_b))


def _adam_math(w, g, m, v):
    m2 = ADAM_B1 * m + (1.0 - ADAM_B1) * g
    v2 = ADAM_B2 * v + (1.0 - ADAM_B2) * (g * g)
    m_hat = m2 / (1.0 - ADAM_B1 ** ADAM_STEP)
    v_hat = v2 / (1.0 - ADAM_B2 ** ADAM_STEP)
    delta = -ADAM_LR * (m_hat / (jnp.sqrt(v_hat) + ADAM_EPS) + ADAM_WD * w)
    return delta, m2, v2


def _adam_big(w, g, m, v, name):
    rows, cols = w.shape
    tr = _row_tile(rows)

    def body(w_ref, g_ref, m_ref, v_ref, d_out, m_out, v_out):
        d, m2, v2 = _adam_math(w_ref[...], g_ref[...], m_ref[...], v_ref[...])
        d_out[...] = d
        m_out[...] = m2
        v_out[...] = v2

    spec = pl.BlockSpec((tr, cols), lambda i: (i, 0))
    return pl.pallas_call(
        body, name=name, grid=(rows // tr,), in_specs=[spec] * 4, out_specs=[spec] * 3,
        out_shape=[_big(w.shape, F32)] * 3,
        compiler_params=_cp(("arbitrary",), 32),
    )(*_hbm(w, g, m, v))


def _ada_grad_adam(c8t, dmod_sh, w, m, v):
    rows, cols = w.shape
    tr = _row_tile(rows)

    def body(ct_ref, dm_ref, w_ref, m_ref, v_ref, g_out, d_out, m_out, v_out):
        g = None
        for b in range(N_DEV):
            term = ct_ref[:, b:b + 1] * dm_ref[b:b + 1, :]
            g = term if g is None else g + term
        d, m2, v2 = _adam_math(w_ref[...], g, m_ref[...], v_ref[...])
        g_out[...] = g
        d_out[...] = d
        m_out[...] = m2
        v_out[...] = v2

    spec = pl.BlockSpec((tr, cols), lambda i: (i, 0))
    return pl.pallas_call(
        body, name="ada_grad_adam", grid=(rows // tr,),
        in_specs=[pl.BlockSpec((tr, N_DEV), lambda i: (i, 0)), _full((N_DEV, cols)), spec, spec, spec],
        out_specs=[spec] * 4, out_shape=[_big(w.shape, F32)] * 4,
        compiler_params=_cp(("arbitrary",), 32),
    )(c8t, dmod_sh, *_hbm(w, m, v))


def _tok_tile(t):
    return min(TOK_TILE, t)


def _mix_in_fwd(chip1, x, modr, g_pre, b_in4, w_in_buf, w_out_buf):
    T = x.shape[0]
    tm = min(MIXIN_TILE, T)
    nt = T // tm
    nb = IN_COLS // N_CHIPS

    def body(ch_ref, x_ref, mod_ref, g_ref, b_ref, win_in, wout_in, p_ref, h_ref, win_ref, wout_ref,
             h_all, wblk, lsem, is_sem, ir_sem, os_sem, or_sem):
        k, i = pl.program_id(0), pl.program_id(1)
        chip = ch_ref[0]

        def load_block(blk):
            cp = pltpu.make_async_copy(win_ref.at[blk], wblk, lsem)
            cp.start()
            cp.wait()

        @pl.when((k == 0) & (i == 0))
        def _():
            _gather_start([win_ref], is_sem, ir_sem, relations=(0, 1))
            load_block(chip)

        for r in range(N_CHIPS - 1):
            @pl.when((k == r + 1) & (i == 0))
            def _(r=r):
                _gather_arrive([win_ref], r, is_sem, ir_sem)
                if r == 0:
                    _gather_start([win_ref], is_sem, ir_sem, relations=(2,))
                if r == 1:
                    _gather_start([wout_ref], os_sem, or_sem)
                load_block(chip ^ (r + 1))

        rows = pl.ds(pl.multiple_of(i * tm, tm), tm)

        @pl.when(k == 0)
        def _():
            xv = x_ref[...]
            rstd = lax.rsqrt(jnp.mean(xv * xv, axis=-1, keepdims=True) + RMS_EPS)
            h = (xv * rstd) * g_ref[...] * (1.0 + mod_ref[1:2, :]) + mod_ref[0:1, :]
            hb = h.astype(BF16)
            h_ref[...] = hb
            h_all[rows, :] = hb

        p_ref[...] = _dot(h_all[rows, :], wblk[...]) + b_ref[chip ^ k]

        @pl.when((k == N_CHIPS - 1) & (i == nt - 1))
        def _():
            _gather_sends_done([win_ref], is_sem, ir_sem)
            _gather_finish([wout_ref], os_sem, or_sem)

    hbm = pl.BlockSpec(memory_space=pl.ANY)
    first_pass = lambda k, i, ch: (jnp.where(k == 0, i, nt - 1), 0)
    gs = pltpu.PrefetchScalarGridSpec(
        num_scalar_prefetch=1, grid=(N_CHIPS, nt),
        in_specs=[pl.BlockSpec((tm, D_MODEL), first_pass), pl.BlockSpec((6, D_MODEL), lambda k, i, ch: (0, 0)),
                  pl.BlockSpec((1, D_MODEL), lambda k, i, ch: (0, 0)),
                  pl.BlockSpec((N_CHIPS, 1, nb), lambda k, i, ch: (0, 0, 0)), hbm, hbm],
        out_specs=[pl.BlockSpec((tm, nb), lambda k, i, ch: (i, ch[0] ^ k)), pl.BlockSpec((tm, D_MODEL), first_pass),
                   hbm, hbm],
        scratch_shapes=[pltpu.VMEM((T, D_MODEL), BF16), pltpu.VMEM((D_MODEL, nb), BF16), pltpu.SemaphoreType.DMA]
        + _gather_sems(1) + _gather_sems(1))
    return pl.pallas_call(
        body, name="mix_in_fwd", grid_spec=gs,
        out_shape=[_big((T, IN_COLS), F32), _big((T, D_MODEL), BF16), _big(w_in_buf.shape, BF16),
                   _big(w_out_buf.shape, BF16)],
        input_output_aliases={5: 2, 6: 3},
        compiler_params=_cp(("arbitrary", "arbitrary"), 48),
    )(chip1, _hbm(x), modr, g_pre, b_in4, _hbm(w_in_buf), _hbm(w_out_buf))


def _mixers_fwd(p, wdw, vecs, lbl, gbufs):
    T = p.shape[0]
    tm = _tok_tile(T)
    nt = T // tm
    nch = tm // CHUNK
    ng = len(gbufs)

    def body(*refs):
        p_ref, wdw_ref, bdw_ref, gain_ref, bias_ref, gout_ref, lbl_ref = refs[:7]
        cat_ref, ys_ref, o_ref, st_ref = refs[7 + ng:11 + ng]
        gout_bufs = refs[11 + ng:11 + 2 * ng]
        ubuf, state, qt_s, kt_s, kh_s, v_s, egl_s, lower_s, same_s, gmat_s, gssem, grsem = refs[11 + 2 * ng:]
        i = pl.program_id(0)

        @pl.when(i == 0)
        def _():
            _gather_start(gout_bufs, gssem, grsem)
            lower_s[...], _, same_s[...] = _chunk_masks(tm)
            gmat_s[...] = _gn_matrix()
            state[...] = jnp.zeros(state.shape, F32)
            ubuf[0:HALO, :] = jnp.zeros((HALO, CONV_CH), F32)
            ubuf[HALO + tm:HALO + tm + SUB, :] = jnp.zeros((SUB, CONV_CH), F32)

        @pl.when(i > 0)
        def _():
            ubuf[0:HALO, :] = ubuf[tm:tm + HALO, :]

        ubuf[HALO:HALO + tm, :] = p_ref[:, 0:CONV_CH] * _sig(p_ref[:, CONV_CH:2 * CONV_CH])
        for r in range(tm // CONV_ROWS):
            rows = slice(r * CONV_ROWS, (r + 1) * CONV_ROWS)
            for lb_ in range(CONV_CH // LANE):
                lanes = slice(lb_ * LANE, (lb_ + 1) * LANE)
                ys_ref[rows, lanes] = bdw_ref[:, lanes] + _tap_conv(ubuf, wdw_ref, r * CONV_ROWS, CONV_FWD_TAPS, lanes)
        gmat = gmat_s[...]
        yv = ys_ref[...]
        d = yv - _gmean(yv, gmat)
        rs = lax.rsqrt(_gmean(d * d, gmat) + GN_EPS)
        z = d * rs * gain_ref[...] + bias_ref[...]
        cat_ref[:, 0:CONV_CH] = (z * _sig(z)).astype(BF16)

        lb, _ = _lower_bound(lbl_ref)
        lower, same = lower_s[...], same_s[...]
        o0 = 2 * CONV_CH
        pr = _hgrn_prep(p_ref[:, o0:o0 + HGRN_W], p_ref[:, o0 + HGRN_W:o0 + 2 * HGRN_W], lb, lower, same)
        qt_s[...] = pr["qt"].astype(BF16)
        kt_s[...] = pr["kt"].astype(BF16)
        kh_s[...] = pr["kh"].astype(BF16)
        v_s[...] = p_ref[:, o0 + 2 * HGRN_W:o0 + 3 * HGRN_W].astype(BF16)
        egl_s[...] = jnp.exp(pr["Gl"])
        tri = _tri()

        def chunk(ci, carry):
            r0 = pl.multiple_of(ci * CHUNK, CHUNK)
            rows = pl.ds(r0, CHUNK)
            for h in range(N_HEADS):
                ls = pl.ds(h * HEAD_D, HEAD_D)
                qc, kc, hc, vc = qt_s[rows, ls], kt_s[rows, ls], kh_s[rows, ls], v_s[rows, ls]
                s0 = state[h]
                s0b = s0.astype(BF16)
                st_ref[ci, h] = s0
                att =jnp.where(tri, _dot_nt(qc, kc), 0.0).astype(BF16)
                o_ref[rows, ls] = _dot(att, vc) + _dot_nt(qc, s0b)
                state[h] = s0 * egl_s[pl.ds(r0, 1), ls] + _dot_tn(vc, hc)
            return carry

        lax.fori_loop(0, nch, chunk, 0)
        for h in range(N_HEADS):
            sl = slice(h * HEAD_D, (h + 1) * HEAD_D)
            oh = o_ref[:, sl]
            gh = p_ref[:, o0 + 3 * HGRN_W + h * HEAD_D:o0 + 3 * HGRN_W + (h + 1) * HEAD_D]
            rsh = lax.rsqrt(jnp.mean(oh * oh, axis=-1, keepdims=True) + RMS_EPS)
            hg = (oh * rsh) * gout_ref[:, sl] * (gh * _sig(gh))
            cat_ref[:, CONV_CH + h * HEAD_D:CONV_CH + (h + 1) * HEAD_D] = hg.astype(BF16)

        @pl.when(i == nt - 1)
        def _():
            _gather_finish(gout_bufs, gssem, grsem)

    tile = lambda cols: pl.BlockSpec((tm, cols), lambda i: (i, 0))
    hbm = pl.BlockSpec(memory_space=pl.ANY)
    n_in = 7
    outs = pl.pallas_call(
        body, name="mixers_fwd", grid=(nt,),
        in_specs=[tile(IN_COLS), _full((HALO, CONV_CH))] + [_full((1, CONV_CH))] * 4 + [_full((2, HGRN_W))]
        + [hbm] * ng,
        out_specs=[tile(D_MODEL), tile(CONV_CH), tile(HGRN_W),
                   pl.BlockSpec((nch, N_HEADS, HEAD_D, HEAD_D), lambda i: (i, 0, 0, 0))] + [hbm] * ng,
        out_shape=[_big((T, D_MODEL), BF16), _big((T, CONV_CH), F32), _big((T, HGRN_W), F32),
                   _big((T // CHUNK, N_HEADS, HEAD_D, HEAD_D), F32)] + [_big(b.shape, BF16) for b in gbufs],
        input_output_aliases={n_in + t: 4 + t for t in range(ng)},
        scratch_shapes=[pltpu.VMEM((tm + HALO + SUB, CONV_CH), F32), pltpu.VMEM((N_HEADS, HEAD_D, HEAD_D), F32),
                        pltpu.VMEM((tm, HGRN_W), BF16), pltpu.VMEM((tm, HGRN_W), BF16),
                        pltpu.VMEM((tm, HGRN_W), BF16), pltpu.VMEM((tm, HGRN_W), BF16),
                        pltpu.VMEM((tm, HGRN_W), F32), pltpu.VMEM((tm, tm), BF16), pltpu.VMEM((tm, tm), BF16),
                        pltpu.VMEM((CONV_CH, CONV_CH), BF16)] + _gather_sems(ng),
        compiler_params=_cp(("arbitrary",), 56),
    )(_hbm(p), wdw, *vecs, lbl, *[_hbm(b) for b in gbufs])
    return outs[:4], outs[4:]


def _mix_out_fwd(cat, w_out, x, modr, g_post, g_ffn):
    T = x.shape[0]
    tm = _tok_tile(T)

    def body(cat_ref, w_ref, x_ref, mod_ref, gp_ref, gf_ref, y_ref, x1_ref, h2_ref):
        yv = _dot(cat_ref[...], w_ref[...])
        y_ref[...] = yv
        rs = lax.rsqrt(jnp.mean(yv * yv, axis=-1, keepdims=True) + RMS_EPS)
        x1 = x_ref[...] + mod_ref[2:3, :] * ((yv * rs) * gp_ref[...])
        x1_ref[...] = x1
        rs1 = lax.rsqrt(jnp.mean(x1 * x1, axis=-1, keepdims=True) + RMS_EPS)
        h2 = (x1 * rs1) * gf_ref[...] * (1.0 + mod_ref[4:5, :]) + mod_ref[3:4, :]
        h2_ref[...] = h2.astype(BF16)

    tile = pl.BlockSpec((tm, D_MODEL), lambda i: (i, 0))
    return pl.pallas_call(
        body, name="mix_out_fwd", grid=(T // tm,),
        in_specs=[tile, _full((D_MODEL, D_MODEL)), tile, _full((6, D_MODEL)), _full((1, D_MODEL)),
                  _full((1, D_MODEL))],
        out_specs=[tile, tile, tile],
        out_shape=[_big((T,D_MODEL), F32), _big((T,D_MODEL), F32),
                   _big((T,D_MODEL), BF16)],
        compiler_params=_cp(("arbitrary",), 40),
    )(*_hbm(cat, w_out, x), modr, g_post, g_ffn)


def _row_chains(rows, n=2):
    step = rows // n
    return [slice(k * step, (k + 1) * step) for k in range(n)]


def _ffn_blocks():
    return D_FF // FFN_BLOCK, (D_FF // N_CHIPS) // FFN_BLOCK


def _ffn_fwd(h2, w_up_g, w_down, x1, target, modr, g_post):
    T = h2.shape[0]
    tm = min(FFN_TILE, T)
    fb = FFN_BLOCK
    nj, per = _ffn_blocks()

    def body(h_ref, wu_ref, wd_ref, x1_ref, t_ref, mod_ref, g_ref, r_ref, dy2_ref, dx2_ref, st_ref, acc):
        i, j = pl.program_id(0), pl.program_id(1)

        @pl.when((i == 0) & (j == 0))
        def _():
            st_ref[...] = jnp.zeros(st_ref.shape, F32)

        @pl.when(j == 0)
        def _():
            acc[...] = jnp.zeros(acc.shape, F32)

        for rows in _row_chains(tm):
            ra = jnp.maximum(_dot(h_ref[rows, :], wu_ref[0]), 0.0)
            rb = (ra * ra).astype(BF16)
            r_ref[rows, :] = rb
            acc[rows, :] += _dot(rb, wd_ref[...])

        @pl.when(j == nj - 1)
        def _():
            y2 = acc[...]
            rs = lax.rsqrt(jnp.mean(y2 * y2, axis=-1, keepdims=True) + RMS_EPS)
            nh = y2 * rs
            gp = g_ref[...]
            err = x1_ref[...] + mod_ref[5:6, :] * (nh * gp) - t_ref[...]
            dx2 = err * (1.0 / D_MODEL)
            dx2_ref[...] = dx2
            st_ref[0:1, :] += _colsum(err * err)
            st_ref[1:2, :] += _colsum(dx2 * (nh * gp))
            dn = dx2 * mod_ref[5:6, :]
            st_ref[2:3, :] += _colsum(dn * nh)
            dy2_ref[...] = _rms_bwd(dn * gp, nh, rs).astype(BF16)

    tile = pl.BlockSpec((tm, D_MODEL), lambda i, j: (i, 0))
    return pl.pallas_call(
        body, name="ffn_fwd", grid=(T // tm, nj),
        in_specs=[tile, pl.BlockSpec((1, D_MODEL, fb), lambda i, j: (j // per, 0, j % per)),
                  pl.BlockSpec((fb, D_MODEL), lambda i, j: (j, 0)), tile, tile,
                  _full((6, D_MODEL)), _full((1, D_MODEL))],
        out_specs=[pl.BlockSpec((tm, fb), lambda i, j: (i, j)), tile, tile, _full((8, D_MODEL))],
        out_shape=[_big((T, D_FF), BF16), _big((T, D_MODEL), BF16), _big((T, D_MODEL), F32),
                   jax.ShapeDtypeStruct((8, D_MODEL), F32)],
        scratch_shapes=[pltpu.VMEM((tm, D_MODEL), F32)],
        compiler_params=_cp(("arbitrary", "arbitrary"), 56),
    )(*_hbm(h2, w_up_g, w_down, x1, target), modr, g_post)


def _rms_bwd(dxn, xn, rs):
    return rs * (dxn - xn * jnp.mean(dxn * xn, axis=-1, keepdims=True))


def _ffn_bwd(dy2, r, x1, dx2, w_up_g, w_down, modr, g_ffn):
    T = dx2.shape[0]
    tm = min(FFN_TILE, T)
    fb = FFN_BLOCK
    nj, per = _ffn_blocks()

    def body(dy2_ref, r_ref, x1_ref, dx2_ref, wu_ref, wd_ref, mod_ref, gf_ref, da_ref, dx1_ref, st_ref, dh_s):
        i, j = pl.program_id(0), pl.program_id(1)

        @pl.when((i == 0) & (j == 0))
        def _():
            st_ref[...] = jnp.zeros(st_ref.shape, F32)

        @pl.when(j == 0)
        def _():
            dh_s[...] = jnp.zeros(dh_s.shape, F32)

        for rows in _row_chains(tm):
            ra = jnp.sqrt(r_ref[rows, :].astype(F32))
            da = (_dot_nt(dy2_ref[rows, :], wd_ref[...]) * (2.0 * ra)).astype(BF16)
            da_ref[rows, :] = da
            dh_s[rows, :] += _dot_nt(da, wu_ref[0])

        @pl.when(j == nj - 1)
        def _():
            dh = dh_s[...]
            x1v = x1_ref[...]
            rs1 = lax.rsqrt(jnp.mean(x1v * x1v, axis=-1, keepdims=True) + RMS_EPS)
            xn = x1v * rs1
            st_ref[0:1, :] += _colsum(dh)
            st_ref[1:2, :] += _colsum(dh * (xn * gf_ref[...]))
            dsc = dh * (1.0 + mod_ref[4:5, :])
            st_ref[2:3, :] += _colsum(dsc * xn)
            dx1_ref[...] = dx2_ref[...] + _rms_bwd(dsc * gf_ref[...], xn, rs1)

    tile = pl.BlockSpec((tm, D_MODEL), lambda i, j: (i, 0))
    ftile = pl.BlockSpec((tm, fb), lambda i, j: (i, j))
    return pl.pallas_call(
        body, name="ffn_bwd", grid=(T // tm, nj),
        in_specs=[tile, ftile, tile, tile, pl.BlockSpec((1, D_MODEL, fb), lambda i, j: (j // per, 0, j % per)),
                  pl.BlockSpec((fb, D_MODEL), lambda i, j: (j, 0)), _full((6, D_MODEL)), _full((1, D_MODEL))],
        out_specs=[ftile, tile, _full((8, D_MODEL))],
        out_shape=[_big((T, D_FF), BF16), _big((T, D_MODEL), F32), jax.ShapeDtypeStruct((8, D_MODEL), F32)],
        scratch_shapes=[pltpu.VMEM((tm, D_MODEL), F32)],
        compiler_params=_cp(("arbitrary", "arbitrary"), 56),
    )(*_hbm(dy2, r, x1, dx2, w_up_g, w_down), modr, g_ffn)


def _mix_out_bwd(dx1, y, w_out, modr, g_post, swap):
    T = dx1.shape[0]
    tm = _tok_tile(T)
    nt = T // tm
    ns = len(swap)

    def body(*refs):
        dx1_ref, y_ref, w_ref, mod_ref, gp_ref = refs[:5]
        s_ins = refs[5:5 + ns]
        dy_ref, dcat_ref, st_ref = refs[5 + ns:8 + ns]
        s_outs = refs[8 + ns:8 + 2 * ns]
        pssem, prsem = refs[8 + 2 * ns:]
        i = pl.program_id(0)

        @pl.when(i == 0)
        def _():
            for cp in _pair_copies(s_ins, s_outs, pssem, prsem):
                cp.start()
            st_ref[...] = jnp.zeros(st_ref.shape, F32)

        dxv, yv = dx1_ref[...], y_ref[...]
        rs = lax.rsqrt(jnp.mean(yv * yv, axis=-1, keepdims=True) + RMS_EPS)
        nh = yv * rs
        st_ref[0:1, :] += _colsum(dxv * (nh * gp_ref[...]))
        dn = dxv * mod_ref[2:3, :]
        st_ref[1:2, :] += _colsum(dn * nh)
        dy = _rms_bwd(dn * gp_ref[...], nh, rs).astype(BF16)
        dy_ref[...] = dy
        dcat_ref[...] = _dot_nt(dy, w_ref[...])

        @pl.when(i == nt - 1)
        def _():
            copies = _pair_copies(s_ins, s_outs, pssem, prsem)
            for cp in copies:
                cp.wait_recv()
            for cp in copies:
                cp.wait_send()

    tile = pl.BlockSpec((tm, D_MODEL), lambda i: (i, 0))
    hbm = pl.BlockSpec(memory_space=pl.ANY)
    outs = pl.pallas_call(
        body, name="mix_out_bwd", grid=(nt,),
        in_specs=[tile, tile, _full((D_MODEL, D_MODEL)), _full((6, D_MODEL)), _full((1, D_MODEL))] + [hbm] * ns,
        out_specs=[tile, tile, _full((8, D_MODEL))] + [hbm] * ns,
        out_shape=[_big((T, D_MODEL), BF16), _big((T, D_MODEL), F32), jax.ShapeDtypeStruct((8, D_MODEL), F32)]
        + _pair_shapes(swap),
        scratch_shapes=_pair_sems(ns),
        compiler_params=_cp(("arbitrary",), 40),
    )(*_hbm(dx1, y, w_out), modr, g_post, *[_hbm(g) for g in swap])
    return outs[:3], outs[3:]


def _mixers_bwd(p, dcat, ys, o, states, wdw, vecs, lbl, pairs_b):
    T = p.shape[0]
    tm = min(MIXB_TILE, T)
    nt = T // tm
    nch = tm // CHUNK
    hpt = tm // HALO
    nx = len(pairs_b)

    def body(*refs):
        (p_ref, ph_ref, dcat_ref, ys_ref, o_ref, st_ref, wdw_ref, bdw_ref, gain_ref, bias_ref, gout_ref,
         lbl_ref) = refs[:12]
        x_ins = refs[12:12 + nx]
        dp_ref, sb_ref, s5_ref, dw_ref = refs[12 + nx:16 + nx]
        x_outs = refs[16 + nx:16 + 2 * nx]
        (ubuf, dybuf, carry, dstate, qt_s, kt_s, kh_s, v_s, do_s, egl_s, dqt_s, dkt_s, dkh_s, dv_s, dgl_s,
         dsh, dw8, dshift, lower_s, upper_s, same_s, gmat_s, xssem, xrsem) = refs[16 + 2 * nx:]
        i = pl.program_id(0)
        tile_idx = nt - 1 - i

        @pl.when(i == 0)
        def _():
            for cp in _xchg_copies(x_ins, x_outs, xssem, xrsem):
                cp.start()
            dstate[...] = jnp.zeros(dstate.shape, F32)
            carry[...] = jnp.zeros(carry.shape, F32)
            sb_ref[...] = jnp.zeros(sb_ref.shape, F32)
            s5_ref[...] = jnp.zeros(s5_ref.shape, F32)
            dw_ref[...] = jnp.zeros(dw_ref.shape, F32)
            dw8[...] = jnp.zeros(dw8.shape, F32)
            lower_s[...], upper_s[...], same_s[...] = _chunk_masks(tm)
            gmat_s[...] = _gn_matrix()
            dsh[0:SUB, :] = jnp.zeros((SUB, CONV_CH), F32)
            dsh[SUB + tm:2 * SUB + tm, :] = jnp.zeros((SUB, CONV_CH), F32)
            ubuf[HALO + tm:HALO + tm + SUB, :] = jnp.zeros((SUB, CONV_CH), F32)

        uh = ph_ref[:, 0:CONV_CH] * _sig(ph_ref[:, CONV_CH:2 * CONV_CH])
        ubuf[0:HALO, :] = jnp.where(tile_idx > 0, uh, 0.0)
        ubuf[HALO:HALO + tm, :] = p_ref[:, 0:CONV_CH] * _sig(p_ref[:, CONV_CH:2 * CONV_CH])
        gmat = gmat_s[...]
        gain = gain_ref[...]
        yv = ys_ref[...]
        d = yv - _gmean(yv, gmat)
        rs = lax.rsqrt(_gmean(d * d, gmat) + GN_EPS)
        yn = d * rs
        z = yn * gain + bias_ref[...]
        sz = _sig(z)
        dz = dcat_ref[:, 0:CONV_CH] * (sz * (1.0 + z * (1.0 - sz)))
        dyn = dz * gain
        dyc = rs * (dyn - _gmean(dyn, gmat) - yn * _gmean(dyn * yn, gmat))
        s5_ref[0:1, :] += _colsum(dyc)
        s5_ref[1:2, :] += _colsum(dz * yn)
        s5_ref[2:3, :] += _colsum(dz)
        dybuf[tm:tm + HALO, :] = carry[...]
        dybuf[0:tm, :] = dyc
        dsh[SUB:SUB + tm, :] = dyc
        carry[...] = dyc[0:HALO, :]
        for b in range(SUB):
            dshift[...] = dsh[SUB - b:2 * SUB - b + tm, :]
            for j, off in CONV_FWD_TAPS:
                if off % SUB == b:
                    prod = dshift[...] * ubuf[off - b:off - b + tm + SUB, :]
                    dw8[j] += jnp.sum(prod.reshape((tm + SUB) // SUB, SUB, CONV_CH), axis=0)
        for r in range(tm // CONV_ROWS):
            rows = slice(r * CONV_ROWS, (r + 1) * CONV_ROWS)
            for lb_ in range(CONV_CH // LANE):
                lanes = slice(lb_ * LANE, (lb_ + 1) * LANE)
                glanes = slice(CONV_CH + lb_ * LANE, CONV_CH + (lb_ + 1) * LANE)
                acc = _tap_conv(dybuf, wdw_ref, r * CONV_ROWS, CONV_BWD_TAPS, lanes)
                val = p_ref[rows, lanes]
                sg = _sig(p_ref[rows, glanes])
                dval = acc * sg
                dgate = acc * val * (sg * (1.0 - sg))
                dp_ref[rows, lanes] = dval.astype(BF16)
                dp_ref[rows, glanes] = dgate.astype(BF16)
                sb_ref[0:1, lanes] += _colsum(dval)
                sb_ref[0:1, glanes] += _colsum(dgate)

        o0 = 2 * CONV_CH
        for h in range(N_HEADS):
            sl = slice(h * HEAD_D, (h + 1) * HEAD_D)
            gsl = slice(o0 + 3 * HGRN_W + h * HEAD_D, o0 + 3 * HGRN_W + (h + 1) * HEAD_D)
            oh = o_ref[:, sl]
            gh = p_ref[:, gsl]
            dh = dcat_ref[:, CONV_CH + h * HEAD_D:CONV_CH + (h + 1) * HEAD_D]
            gout = gout_ref[:, sl]
            rsh = lax.rsqrt(jnp.mean(oh * oh, axis=-1, keepdims=True) + RMS_EPS)
            on = oh * rsh
            sgg = _sig(gh)
            dgh = dh * (on * gout) * (sgg * (1.0 + gh * (1.0 - sgg)))
            dm = dh * (gh * sgg)
            s5_ref[3:4, sl] += _colsum(dm * on)
            do_s[:, sl] = _rms_bwd(dm * gout, on, rsh).astype(BF16)
            dp_ref[:, gsl] = dgh.astype(BF16)
            sb_ref[2:3, CONV_CH + h * HEAD_D:CONV_CH + (h + 1) * HEAD_D] += _colsum(dgh)

        lb, _ = _lower_bound(lbl_ref)
        lower, upper, same = lower_s[...], upper_s[...], same_s[...]
        pq = p_ref[:, o0:o0 + HGRN_W]
        pr = _hgrn_prep(pq, p_ref[:, o0 + HGRN_W:o0 + 2 * HGRN_W], lb, lower, same)
        qt_s[...] = pr["qt"].astype(BF16)
        kt_s[...] = pr["kt"].astype(BF16)
        kh_s[...] = pr["kh"].astype(BF16)
        v_s[...] = p_ref[:, o0 + 2 * HGRN_W:o0 + 3 * HGRN_W].astype(BF16)
        egl_s[...] = jnp.exp(pr["Gl"])
        tri = _tri()

        def chunk(it, c_):
            ci = nch - 1 - it
            r0 = pl.multiple_of(ci * CHUNK, CHUNK)
            rows = pl.ds(r0, CHUNK)
            for h in range(N_HEADS):
                ls = pl.ds(h * HEAD_D, HEAD_D)
                qc, kc, hc, vc = qt_s[rows, ls], kt_s[rows, ls], kh_s[rows, ls], v_s[rows, ls]
                dob = do_s[rows, ls]
                s0 = st_ref[ci, h]
                s0b = s0.astype(BF16)
                ds1 = dstate[h]
                ds1b = ds1.astype(BF16)
                egl = egl_s[pl.ds(r0, 1), ls]
                att = jnp.where(tri, _dot_nt(qc, kc), 0.0).astype(BF16)
                datt = jnp.where(tri, _dot_nt(dob, vc), 0.0).astype(BF16)
                dv_s[rows, ls] = _dot_tn(att, dob) + _dot_nt(hc, ds1b)
                dqt_s[rows, ls] = _dot(datt, kc) + _dot(dob, s0b)
                dkt_s[rows, ls] = _dot_tn(datt, qc)
                dkh_s[rows, ls] = _dot(vc, ds1b)
                dgl = egl * _colsum(ds1 * s0)
                dgl_s[rows, ls] = jnp.broadcast_to(dgl, (CHUNK, HEAD_D))
                dstate[h] = ds1 * egl + _dot_tn(dob, qc)
            return c_

        lax.fori_loop(0, nch, chunk, 0)
        dqt, dkt, dkh = dqt_s[...], dkt_s[...], dkh_s[...]
        dk = dkt * pr["enG"] + dkh * pr["eGlG"]
        khk = dkh * kh_s[...].astype(F32)
        dG = dqt * qt_s[...].astype(F32) - dkt * kt_s[...].astype(F32) - khk
        dlogf = _mm3(upper, dG) + _mm3(same, khk) + dgl_s[...]
        df = dlogf / pr["f"] - dk
        sf, sq = pr["sf"], pr["sq"]
        s5_ref[4:5, :] += _colsum(df * (1.0 - sf))
        dfl = df * (1.0 - lb) * (sf * (1.0 - sf))
        dq = (dqt * pr["eG"]) * (sq * (1.0 + pq * (1.0 - sq)))
        dvv = dv_s[...]
        dp_ref[:, o0:o0 + HGRN_W] = dq.astype(BF16)
        dp_ref[:, o0 + HGRN_W:o0 + 2 * HGRN_W] = dfl.astype(BF16)
        dp_ref[:, o0 + 2 * HGRN_W:o0 + 3 * HGRN_W] = dvv.astype(BF16)
        sb_ref[1:2, 0:HGRN_W] += _colsum(dq)
        sb_ref[1:2, HGRN_W:2 * HGRN_W] += _colsum(dfl)
        sb_ref[2:3, 0:HGRN_W] += _colsum(dvv)

        @pl.when(i == nt - 1)
        def _():
            for j in range(CONV_K):
                dw_ref[j:j + 1, :] = _colsum(dw8[j])
            copies = _xchg_copies(x_ins, x_outs, xssem, xrsem)
            for cp in copies:
                cp.wait_recv()
            for cp in copies:
                cp.wait_send()

    rev = lambda cols: pl.BlockSpec((tm, cols), lambda i: (nt - 1 - i, 0))
    halo = pl.BlockSpec((HALO, 2 * CONV_CH), lambda i: (jnp.maximum((nt - 1 - i) * hpt - 1, 0), 0))
    wide = lambda n: pltpu.VMEM((tm, HGRN_W), n)
    hbm = pl.BlockSpec(memory_space=pl.ANY)
    outs = pl.pallas_call(
        body, name="mixers_bwd", grid=(nt,),
        in_specs=[rev(IN_COLS), halo, rev(D_MODEL), rev(CONV_CH), rev(HGRN_W),
                  pl.BlockSpec((nch, N_HEADS, HEAD_D, HEAD_D), lambda i: (nt - 1 - i, 0, 0, 0)),
                  _full((HALO, CONV_CH))] + [_full((1, CONV_CH))] * 4 + [_full((2, HGRN_W))] + [hbm] * nx,
        out_specs=[rev(IN_COLS), _full((8, D_MODEL)), _full((8, CONV_CH)), _full((HALO, CONV_CH))] + [hbm] * nx,
        out_shape=[_big((T, IN_COLS), BF16), jax.ShapeDtypeStruct((8, D_MODEL), F32),
                   jax.ShapeDtypeStruct((8, CONV_CH), F32), jax.ShapeDtypeStruct((HALO, CONV_CH), F32)]
        + [_big(pb.shape, BF16) for pb in pairs_b],
        scratch_shapes=[pltpu.VMEM((tm + HALO + SUB, CONV_CH), F32), pltpu.VMEM((tm + HALO, CONV_CH), F32),
                        pltpu.VMEM((HALO, CONV_CH), F32), pltpu.VMEM((N_HEADS, HEAD_D, HEAD_D), F32),
                        wide(BF16), wide(BF16), wide(BF16), wide(BF16), wide(BF16),
                        wide(F32), wide(F32), wide(F32), wide(F32), wide(F32), wide(F32),
                        pltpu.VMEM((tm + 2 * SUB, CONV_CH), F32), pltpu.VMEM((HALO, SUB, CONV_CH), F32),
                        pltpu.VMEM((tm + SUB, CONV_CH), F32), pltpu.VMEM((tm, tm), BF16), pltpu.VMEM((tm, tm), BF16),
                        pltpu.VMEM((tm, tm), BF16), pltpu.VMEM((CONV_CH, CONV_CH), BF16)]
        + _xchg_sems(nx),
        compiler_params=_cp(("arbitrary",), 56),
    )(*_hbm(p, p, dcat, ys, o, states), wdw, *vecs, lbl, *[_hbm(pb) for pb in pairs_b])
    return outs[:4], outs[4:]


def _mix_in_bwd(dp, w_in_g, x, dx1, modr, g_pre, pairs_b):
    T = x.shape[0]
    tm = _tok_tile(T)
    nt = T // tm
    nb = IN_COLS // N_CHIPS
    nx = len(pairs_b)

    def body(*refs):
        dp_ref, w_ref, x_ref, dx1_ref, mod_ref, g_ref = refs[:6]
        x_ins = refs[6:6 + nx]
        gx_ref, st_ref = refs[6 + nx:8 + nx]
        x_outs = refs[8 + nx:8 + 2 * nx]
        xssem, xrsem = refs[8 + 2 * nx:]
        i = pl.program_id(0)

        @pl.when(i == 0)
        def _():
            for cp in _xchg_copies(x_ins, x_outs, xssem, xrsem):
                cp.start()
            st_ref[...] = jnp.zeros(st_ref.shape, F32)

        dh = None
        for j in range(N_CHIPS):
            part = _dot_nt(dp_ref[:, j * nb:(j + 1) * nb], w_ref[j])
            dh = part if dh is None else dh + part
        xv = x_ref[...]
        rs = lax.rsqrt(jnp.mean(xv * xv, axis=-1, keepdims=True) + RMS_EPS)
        xn = xv * rs
        st_ref[0:1, :] += _colsum(dh)
        st_ref[1:2, :] += _colsum(dh * (xn * g_ref[...]))
        dsc = dh * (1.0 + mod_ref[1:2, :])
        st_ref[2:3, :] += _colsum(dsc * xn)
        gx_ref[...] = dx1_ref[...] + _rms_bwd(dsc * g_ref[...], xn, rs)

        @pl.when(i == nt - 1)
        def _():
            copies = _xchg_copies(x_ins, x_outs, xssem, xrsem)
            for cp in copies:
                cp.wait_recv()
            for cp in copies:
                cp.wait_send()

    tile = pl.BlockSpec((tm, D_MODEL), lambda i: (i, 0))
    hbm = pl.BlockSpec(memory_space=pl.ANY)
    outs = pl.pallas_call(
        body, name="mix_in_bwd", grid=(nt,),
        in_specs=[pl.BlockSpec((tm, IN_COLS), lambda i: (i, 0)), _full((N_CHIPS, D_MODEL, nb)), tile, tile,
                  _full((6, D_MODEL)), _full((1, D_MODEL))] + [hbm] * nx,
        out_specs=[tile, _full((8, D_MODEL))] + [hbm] * nx,
        out_shape=[_big((T, D_MODEL), F32), jax.ShapeDtypeStruct((8, D_MODEL), F32)]
        + [_big(pb.shape, BF16) for pb in pairs_b],
        scratch_shapes=_xchg_sems(nx),
        compiler_params=_cp(("arbitrary",), 48),
    )(*_hbm(dp, w_in_g, x, dx1), modr, g_pre, *[_hbm(pb) for pb in pairs_b])
    return outs[:2], outs[2:]


def _weight_grad(a, b, a_blocked, b_blocked, name):
    T = a.shape[0]
    tt = min(GRAD_TILE, T)
    nt = T // tt
    ka = a.shape[1] // N_CHIPS if a_blocked else a.shape[1]
    nb = b.shape[1] // N_CHIPS if b_blocked else b.shape[1]

    def body(a_ref, b_ref, o_ref, ob_ref):
        t = pl.program_id(1)

        @pl.when(t == 0)
        def _():
            o_ref[...] = jnp.zeros(o_ref.shape, F32)

        for cols in _row_chains(nb):
            o_ref[0, :, cols] += _dot_tn(a_ref[...], b_ref[:, cols])

        @pl.when(t == nt - 1)
        def _():
            ob_ref[0] = o_ref[0].astype(BF16)

    blk = pl.BlockSpec((1, ka, nb), lambda j, t: (j, 0, 0))
    return pl.pallas_call(
        body, name=name, grid=(N_CHIPS, nt),
        in_specs=[pl.BlockSpec((tt, ka), (lambda j, t: (t, j)) if a_blocked else (lambda j, t: (t, 0))),
                  pl.BlockSpec((tt, nb), (lambda j, t: (t, j)) if b_blocked else (lambda j, t: (t, 0)))],
        out_specs=[blk, blk],
        out_shape=[_big((N_CHIPS, ka, nb), F32), _big((N_CHIPS, ka, nb), BF16)],
        compiler_params=_cp(("arbitrary", "arbitrary"), 48),
    )(*_hbm(a, b))


R_LOSS = 0
R_FFN = 8
R_OUT = 16
R_IN = 24
R_BIN = 32
R_512 = 40
R_DW = 48
N_STAT_ROWS = 80
MOD_ROWS = (R_IN + 0, R_IN + 1, R_OUT + 0, R_FFN + 0, R_FFN + 1, R_LOSS + 1)


def _small_update(gath, params):
    names = ["b_ada", "lb_logits", "g_pre_mix", "b_in", "b_dw", "gn_gain", "gn_bias", "g_hgrn_out", "g_post_mix",
             "g_pre_ffn", "g_post_ffn"]
    flat = []
    for n in names:
        flat += list(params[n])
    n_in = 1 + len(flat)

    def body(*refs):
        g_ref = refs[0]
        prm = {n: refs[1 + 3 * k:4 + 3 * k] for k, n in enumerate(names)}
        outs = refs[n_in:]
        loss_ref, dmod_ref, dwdw_ref = outs[0], outs[1], outs[2]
        res = {n: outs[3 + 4 * k:7 + 4 * k] for k, n in enumerate(names)}
        red = g_ref[0]
        for dev in range(1, N_DEV):
            red = red + g_ref[dev]
        loss_ref[...] = jnp.broadcast_to(
            (0.5 / D_MODEL) * jnp.sum(red[R_LOSS:R_LOSS + 1, :], axis=-1, keepdims=True), loss_ref.shape)
        for dev in range(N_DEV):
            for k, r in enumerate(MOD_ROWS):
                dmod_ref[dev:dev + 1, k * D_MODEL:(k + 1) * D_MODEL] = g_ref[dev, r:r + 1, :]
        dwdw_ref[...] = red[R_DW:R_DW + HALO, 0:CONV_CH]

        def finish(name, pieces):
            w_ref, m_ref, v_ref = prm[name]
            g_out, d_out, m_out, v_out = res[name]
            for rsl, lsl, g in pieces:
                d, m2, v2 = _adam_math(w_ref[rsl, lsl], g, m_ref[rsl, lsl], v_ref[rsl, lsl])
                g_out[rsl, lsl] = g
                d_out[rsl, lsl] = d
                m_out[rsl, lsl] = m2
                v_out[rsl, lsl] = v2

        one = slice(0, 1)
        row = lambda r: red[r:r + 1, :]
        half = lambda r: red[r:r + 1, 0:CONV_CH]
        finish("b_ada", [(one, slice(k * D_MODEL, (k + 1) * D_MODEL), row(r)) for k, r in enumerate(MOD_ROWS)])
        finish("b_in", [(one, slice(k * D_MODEL, (k + 1) * D_MODEL), row(R_BIN + k)) for k in range(3)])
        finish("g_pre_mix", [(one, slice(None), row(R_IN + 2))])
        finish("g_post_mix", [(one, slice(None), row(R_OUT + 1))])
        finish("g_pre_ffn", [(one, slice(None), row(R_FFN + 2))])
        finish("g_post_ffn", [(one, slice(None), row(R_LOSS + 2))])
        finish("b_dw", [(one, slice(None), half(R_512 + 0))])
        finish("gn_gain", [(one, slice(None), half(R_512 + 1))])
        finish("gn_bias", [(one, slice(None), half(R_512 + 2))])
        finish("g_hgrn_out", [(one, slice(None), half(R_512 + 3))])
        s0, s1 = _lower_bound(prm["lb_logits"][0])
        dlb = half(R_512 + 4)
        finish("lb_logits", [(slice(0, 1), slice(None), dlb * s0 * (1.0 - s0)),
                             (slice(1, 2), slice(None), -dlb * s0 * s1)])

    vm = pl.BlockSpec(memory_space=pltpu.VMEM)
    out_shape = [jax.ShapeDtypeStruct((8, 128), F32), jax.ShapeDtypeStruct((N_DEV, 6 * D_MODEL), F32),
                 jax.ShapeDtypeStruct((HALO, CONV_CH), F32)]
    for n in names:
        out_shape += [jax.ShapeDtypeStruct(params[n][0].shape, F32)] * 4
    outs = pl.pallas_call(
        body, name="small_update", out_shape=out_shape,
        in_specs=[vm] * n_in, out_specs=[vm] * len(out_shape),
        compiler_params=_cp(None, 32),
    )(gath, *flat)
    return outs[0], outs[1], outs[2], {n: outs[3 + 4 * k:7 + 4 * k] for k, n in enumerate(names)}


def _wdw_adam(w, g, m, v):
    def body(w_ref, g_ref, m_ref, v_ref, d_out, m_out, v_out):
        d, m2, v2 = _adam_math(w_ref[...], g_ref[...], m_ref[...], v_ref[...])
        d_out[...] = d
        m_out[...] = m2
        v_out[...] = v2

    vm = pl.BlockSpec(memory_space=pltpu.VMEM)
    return pl.pallas_call(
        body, name="wdw_adam", out_shape=[jax.ShapeDtypeStruct(w.shape, F32)] * 3,
        in_specs=[vm] * 4, out_specs=[vm] * 3, compiler_params=_cp(None, 16),
    )(w, g, m, v)


def kernel(x, c, w_ada, b_ada, lb_logits, g_pre_mix, w_in, b_in, w_dw, b_dw, gn_gain, gn_bias, g_hgrn_out, w_out, g_post_mix, g_pre_ffn, w_up, w_down, g_post_ffn, loss_target, m_w_ada, m_b_ada, m_lb_logits, m_g_pre_mix, m_w_in, m_b_in, m_w_dw, m_b_dw, m_gn_gain, m_gn_bias, m_g_hgrn_out, m_w_out, m_g_post_mix, m_g_pre_ffn, m_w_up, m_w_down, m_g_post_ffn, v_w_ada, v_b_ada, v_lb_logits, v_g_pre_mix, v_w_in, v_b_in, v_w_dw, v_b_dw, v_gn_gain, v_gn_bias, v_g_hgrn_out, v_w_out, v_g_post_mix, v_g_pre_ffn, v_w_up, v_w_down, v_g_post_ffn):
    ax, ay, ac = lax.axis_index("x"), lax.axis_index("y"), lax.axis_index("c")
    chip = 2 * ax + ay
    T = x.shape[1]
    xs, tgt = x[0], loss_target[0]
    ada_cols = w_ada.shape[2]

    b_sh = lax.dynamic_slice_in_dim(b_ada, chip * ada_cols, ada_cols, axis=1)
    wdw_pad = jnp.pad(w_dw[0], ((0, HALO - CONV_K), (0, 0)))
    chip1 = jnp.reshape(chip, (1,)).astype(jnp.int32)
    place = jnp.stack([ac, chip]).astype(jnp.int32)
    _, c8, modg, wdwg = _ada_exchange(c, w_ada[0], b_sh, wdw_pad)
    modr = modg.reshape(6, D_MODEL)
    wdw_all = jnp.transpose(wdwg, (1, 0, 2)).reshape(HALO, CONV_CH)
    bufs = {t: _cast_own(chip1, w[0], "cast_" + t)
            for w, t in ((w_in, "w_in"), (w_out, "w_out"), (w_up, "w_up"), (w_down, "w_down"))}
    vec = (b_dw, gn_gain, gn_bias, g_hgrn_out)

    p, h1, w_in_g, w_out_g = _mix_in_fwd(chip1, xs, modr, g_pre_mix, b_in.reshape(N_CHIPS, 1, IN_COLS // N_CHIPS),
                                          bufs["w_in"], bufs["w_out"])
    (cat, ys, o, states), (w_up_g, w_down_g) = _mixers_fwd(p, wdw_all, vec, lb_logits, [bufs["w_up"], bufs["w_down"]])
    w_out_f = w_out_g.reshape(D_MODEL, D_MODEL)
    w_down_f = w_down_g.reshape(D_FF, D_MODEL)
    y, x1, h2 = _mix_out_fwd(cat, w_out_f, xs, modr, g_post_mix, g_pre_ffn)
    r, dy2, dx2, st_loss = _ffn_fwd(h2, w_up_g, w_down_f, x1, tgt, modr, g_post_ffn)

    def pair_sums(grads, got, tags):
        return [_pair_sum(place, g, o_, "pair_sum_" + t) for (g, _), o_, t in zip(grads, got, tags)]

    da, dx1, st_ffn = _ffn_bwd(dy2, r, x1, dx2, w_up_g, w_down_f, modr, g_pre_ffn)
    g_up = _weight_grad(h2, da, False, True, "grad_w_up")
    g_down = _weight_grad(r, dy2, True, False, "grad_w_down")
    (dy, dcat, st_out), got_ud = _mix_out_bwd(dx1, y, w_out_f, modr, g_post_mix, [g_up[1], g_down[1]])
    g_out = _weight_grad(cat, dy, True, False, "grad_w_out")
    got_o = _pair_swap([g_out[1]], "pair_swap_w_out")
    early = pair_sums([g_out, g_up, g_down], list(got_o) + list(got_ud), ["w_out", "w_up", "w_down"])
    (dp, st_bin, st_512, dwdw), got_early = _mixers_bwd(p, dcat, ys, o, states, wdw_all, vec, lb_logits,
                                                        [pb for _, pb in early])
    g_in = _weight_grad(h1, dp, False, True, "grad_w_in")
    late = pair_sums([g_in], _pair_swap([g_in[1]], "pair_swap_w_in"), ["w_in"])
    (grad_x, st_in), got_late = _mix_in_bwd(dp, w_in_g, xs, dx1, modr, g_pre_mix, [late[0][1]])
    fulls = [_chip_sum(place, pf, gb, "chip_sum_" + t)
             for (pf, _), gb, t in zip(late + early, list(got_late) + list(got_early), ["w_in", "w_out", "w_up", "w_down"])]

    pad_lanes = lambda s: jnp.pad(s, ((0, 0), (0, D_MODEL - s.shape[1])))
    stats = jnp.concatenate([st_loss, st_ffn, st_out, st_in, st_bin, pad_lanes(st_512), pad_lanes(dwdw)], axis=0)
    (g_w_in, g_w_out, g_w_up, g_w_down), gath = _final_exchange(fulls, stats)
    small = {"b_ada": (b_ada, m_b_ada, v_b_ada), "lb_logits": (lb_logits, m_lb_logits, v_lb_logits),
             "g_pre_mix": (g_pre_mix, m_g_pre_mix, v_g_pre_mix), "b_in": (b_in, m_b_in, v_b_in),
             "b_dw": (b_dw, m_b_dw, v_b_dw), "gn_gain": (gn_gain, m_gn_gain, v_gn_gain),
             "gn_bias": (gn_bias, m_gn_bias, v_gn_bias), "g_hgrn_out": (g_hgrn_out, m_g_hgrn_out, v_g_hgrn_out),
             "g_post_mix": (g_post_mix, m_g_post_mix, v_g_post_mix), "g_pre_ffn": (g_pre_ffn, m_g_pre_ffn, v_g_pre_ffn),
             "g_post_ffn": (g_post_ffn, m_g_post_ffn, v_g_post_ffn)}
    loss_t, dmod_all, dwdw_sum, sres = _small_update(gath, small)
    loss = loss_t[0, 0]

    res = dict(sres)
    dmod_sh = lax.dynamic_slice_in_dim(dmod_all, chip * ada_cols, ada_cols, axis=1)
    res["w_ada"] = [t[None] for t in _ada_grad_adam(jnp.transpose(c8), dmod_sh, w_ada[0], m_w_ada[0], v_w_ada[0])]
    g_wdw = lax.dynamic_slice_in_dim(dwdw_sum, chip * HEAD_D, HEAD_D, axis=1)[:CONV_K][None]
    res["w_dw"] = [g_wdw] + list(_wdw_adam(w_dw, g_wdw, m_w_dw, v_w_dw))
    for name, g, w, m, v in (("w_in", g_w_in, w_in, m_w_in, v_w_in), ("w_out", g_w_out, w_out, m_w_out, v_w_out),
                             ("w_up", g_w_up, w_up, m_w_up, v_w_up), ("w_down", g_w_down, w_down, m_w_down, v_w_down)):
        d, m2, v2 = _adam_big(w[0], g, m[0], v[0], "adam_" + name)
        res[name] = [g[None], d[None], m2[None], v2[None]]

    order = ["w_ada", "b_ada", "lb_logits", "g_pre_mix", "w_in", "b_in", "w_dw", "b_dw", "gn_gain", "gn_bias",
             "g_hgrn_out", "w_out", "g_post_mix", "g_pre_ffn", "w_up", "w_down", "g_post_ffn"]
    out = [loss, grad_x[None]]
    for k in range(4):
        out += [res[n][k] for n in order]
    return tuple(out)
```

```python
import jax
import jax.numpy as jnp
from jax import lax
from jax.experimental import pallas as pl
from jax.experimental.pallas import tpu as pltpu

F32, BF16 = jnp.float32, jnp.bfloat16
D_MODEL = 1024
CONV_CH = 512
HGRN_W = 512
N_HEADS = 4
HEAD_D = 128
CONV_K = 31
GN_GROUP = 64
GN_SHIFT = 6
IN_COLS = 3072
D_FF = 4096
CHUNK = 64
CHUNK_SHIFT = 6
N_CHIPS = 4
N_DEV = 8
RMS_EPS = 1e-6
GN_EPS = 1e-5
ADAM_LR, ADAM_B1, ADAM_B2, ADAM_EPS, ADAM_WD, ADAM_STEP = 0.001, 0.9, 0.999, 1e-08, 0.01, 10
TOK_TILE = 512
MIXIN_TILE = 1024
MIXB_TILE = 256
FFN_TILE = 1024
FFN_BLOCK = 512
GRAD_TILE = 2048
HALO = 32
SUB = 8
LANE = 128
CONV_ROWS = 128
MIB = 1 << 20
MESH = pl.DeviceIdType.MESH
OTHER_CHIPS = ((0, 1), (1, 0), (1, 1))


def _cp(sem=None, vmem_mib=48):
    return pltpu.CompilerParams(dimension_semantics=sem, vmem_limit_bytes=vmem_mib * MIB)


def _dot(a, b):
    return jnp.dot(a, b, preferred_element_type=F32)


def _dot_nt(a, b):
    return lax.dot_general(a, b, (((1,), (1,)), ((), ())), preferred_element_type=F32)


def _dot_tn(a, b):
    return lax.dot_general(a, b, (((0,), (0,)), ((), ())), preferred_element_type=F32)


def _sig(v):
    return 0.5 * jnp.tanh(0.5 * v) + 0.5


def _colsum(v):
    return jnp.sum(v, axis=0, keepdims=True)


def _flip(v, b):
    return 1 - v if b else v


def _rcopy(src, dst, ssem, rsem, dev):
    return pltpu.make_async_remote_copy(src_ref=src, dst_ref=dst, send_sem=ssem, recv_sem=rsem,
                                        device_id=dev, device_id_type=MESH)


def _place():
    return lax.axis_index("x"), lax.axis_index("y"), lax.axis_index("c")


def _full(shape):
    return pl.BlockSpec(shape, lambda *_: (0,) * len(shape))


def _big(shape, dtype):
    return pltpu.HBM(shape, dtype)


def _hbm(*arrays):
    out = [pltpu.with_memory_space_constraint(a, pltpu.HBM) for a in arrays]
    return out[0] if len(out) == 1 else out


def _split2(v):
    hi = v.astype(BF16)
    lo = (v - hi.astype(F32)).astype(BF16)
    return hi, lo


def _split3(v):
    h1 = v.astype(BF16)
    r1 = v - h1.astype(F32)
    h2 = r1.astype(BF16)
    h3 = (r1 - h2.astype(F32)).astype(BF16)
    return h1, h2, h3


def _mm3s(mat, parts):
    h1, h2, h3 = parts
    return _dot(mat, h1) + _dot(mat, h2) + _dot(mat, h3)


def _mm3(mat, v):
    return _mm3s(mat, _split3(v))


def _gn_matrix():
    r = lax.broadcasted_iota(jnp.int32, (CONV_CH, CONV_CH), 0) >> GN_SHIFT
    c = lax.broadcasted_iota(jnp.int32, (CONV_CH, CONV_CH), 1) >> GN_SHIFT
    return jnp.where(r == c, 1.0 / GN_GROUP, 0.0).astype(BF16)


def _gmean(v, gmat):
    hi, lo = _split2(v)
    return _dot(hi, gmat) + _dot(lo, gmat)


def _chunk_masks(tm):
    r = lax.broadcasted_iota(jnp.int32, (tm, tm), 0)
    c = lax.broadcasted_iota(jnp.int32, (tm, tm), 1)
    same = (r >> CHUNK_SHIFT) == (c >> CHUNK_SHIFT)
    one = lambda m: jnp.where(m, 1.0, 0.0).astype(BF16)
    return one(same & (c <= r)), one(same & (c >= r)), one(same)


def _tri():
    return lax.broadcasted_iota(jnp.int32, (CHUNK, CHUNK), 0) >= lax.broadcasted_iota(jnp.int32, (CHUNK, CHUNK), 1)


def _lower_bound(lbl_ref):
    l0, l1 = lbl_ref[0:1, :], lbl_ref[1:2, :]
    mx = jnp.maximum(l0, l1)
    e0, e1 = jnp.exp(l0 - mx), jnp.exp(l1 - mx)
    return e0 / (e0 + e1), e1 / (e0 + e1)


CONV_FWD_TAPS = tuple((j, HALO - (CONV_K - 1) + j) for j in range(CONV_K))
CONV_BWD_TAPS = tuple((j, (CONV_K - 1) - j) for j in range(CONV_K))


def _tap_conv(src_ref, w_ref, row0, taps, lanes):
    acc = None
    for b in range(SUB):
        pb = None
        for j, off in taps:
            if off % SUB == b:
                lo = row0 + off - b
                term = w_ref[j:j + 1, lanes] * src_ref[lo:lo + CONV_ROWS + SUB, lanes]
                pb = term if pb is None else pb + term
        if pb is not None:
            sh = pb[b:b + CONV_ROWS, :]
            acc = sh if acc is None else acc + sh
    return acc


def _hgrn_prep(pq, pf, lb, lower, same):
    sq = _sig(pq)
    qf = pq * sq
    sf = _sig(pf)
    f = lb + (1.0 - lb) * sf
    logf = jnp.log(f)
    k = 1.0 - f
    parts = _split3(logf)
    G = _mm3s(lower, parts)
    Gl = _mm3s(same, parts)
    eG, enG, eGlG = jnp.exp(G), jnp.exp(-G), jnp.exp(Gl - G)
    return dict(sq=sq, sf=sf, f=f, Gl=Gl, eG=eG, enG=enG, eGlG=eGlG, qt=qf * eG, kt=k * enG, kh=k * eGlG)


def _ada_exchange(c_row, w_ada, b_sh, wdw_pad):
    ncol = w_ada.shape[1]

    def body(c_ref, w_ref, b_ref, wdw_ref, call_ref, c8_ref, modg_ref, wdwg_ref, rows_s, sa, ra, sw, rw, sm, rm):
        x, y, c = _place()
        me = 4 * x + 2 * y + c
        chip = 2 * x + y
        cv = c_ref[...]
        call_ref[me] = cv * _sig(cv)
        wdwg_ref[chip] = wdw_ref[...]
        sends = []
        for m in range(1, N_DEV):
            peer = (_flip(x, m >> 2), _flip(y, (m >> 1) & 1), _flip(c, m & 1))
            cp = _rcopy(call_ref.at[me], call_ref.at[me], sa.at[m - 1], ra.at[m - 1], peer)
            cp.start()
            sends.append(cp)
        for k, (fx, fy) in enumerate(OTHER_CHIPS):
            peer = (_flip(x, fx), _flip(y, fy), c)
            cp = _rcopy(wdwg_ref.at[chip], wdwg_ref.at[chip], sw.at[k], rw.at[k], peer)
            cp.start()
            sends.append(cp)
        for m in range(1, N_DEV):
            peer = (_flip(x, m >> 2), _flip(y, (m >> 1) & 1), _flip(c, m & 1))
            pid = 4 * peer[0] + 2 * peer[1] + peer[2]
            _rcopy(call_ref.at[pid], call_ref.at[pid], sa.at[m - 1], ra.at[m - 1], peer).wait_recv()
        for b in range(N_DEV):
            c8_ref[b:b + 1, :] = call_ref[b]
        mod_all = _dot(c8_ref[...].astype(BF16), w_ref[...].astype(BF16)) + b_ref[...]
        for b in range(N_DEV):
            rows_s[b] = mod_all[b:b + 1, :]
        modg_ref[chip] = rows_s[me]
        for k, (fx, fy) in enumerate(OTHER_CHIPS):
            peer = (_flip(x, fx), _flip(y, fy), c)
            pid = 4 * peer[0] + 2 * peer[1] + peer[2]
            cp = _rcopy(rows_s.at[pid], modg_ref.at[chip], sm.at[k], rm.at[k], peer)
            cp.start()
            sends.append(cp)
        for k, (fx, fy) in enumerate(OTHER_CHIPS):
            peer = (_flip(x, fx), _flip(y, fy), c)
            pchip = 2 * peer[0] + peer[1]
            _rcopy(rows_s.at[0], modg_ref.at[pchip], sm.at[k], rm.at[k], peer).wait_recv()
            _rcopy(wdwg_ref.at[pchip], wdwg_ref.at[pchip], sw.at[k], rw.at[k], peer).wait_recv()
        for cp in sends:
            cp.wait_send()

    vm = pl.BlockSpec(memory_space=pltpu.VMEM)
    return pl.pallas_call(
        body, name="ada_exchange",
        out_shape=[jax.ShapeDtypeStruct((N_DEV, 1, D_MODEL), F32), jax.ShapeDtypeStruct((N_DEV, D_MODEL), F32),
                   jax.ShapeDtypeStruct((N_CHIPS, 1, ncol), F32), jax.ShapeDtypeStruct((N_CHIPS, HALO, HEAD_D), F32)],
        in_specs=[vm] * 4, out_specs=[vm] * 4,
        scratch_shapes=[pltpu.VMEM((N_DEV, 1, ncol), F32),
                        pltpu.SemaphoreType.DMA((N_DEV - 1,)), pltpu.SemaphoreType.DMA((N_DEV - 1,)),
                        pltpu.SemaphoreType.DMA((3,)), pltpu.SemaphoreType.DMA((3,)),
                        pltpu.SemaphoreType.DMA((3,)), pltpu.SemaphoreType.DMA((3,))],
        compiler_params=_cp(None, 32),
    )(c_row, w_ada, b_sh, wdw_pad)


def _cast_own(chip1, shard, name):
    rows, cols = shard.shape
    tr = _row_tile(rows)

    def body(ch_ref, s_ref, o_ref):
        o_ref[0] = s_ref[...].astype(BF16)

    gs = pltpu.PrefetchScalarGridSpec(
        num_scalar_prefetch=1, grid=(rows // tr,),
        in_specs=[pl.BlockSpec((tr, cols), lambda i, ch: (i, 0))],
        out_specs=pl.BlockSpec((1, tr, cols), lambda i, ch: (ch[0], i, 0)))
    return pl.pallas_call(
        body, name=name, grid_spec=gs, out_shape=_big((N_CHIPS, rows, cols), BF16),
        compiler_params=_cp(("arbitrary",), 32),
    )(chip1, _hbm(shard))


def _slab(buf, ch, core):
    hs = buf.shape[1] // 2
    return buf.at[ch, pl.ds(core * hs, hs), :]


def _gather_start(bufs, ssem, rsem, relations=(0, 1, 2)):
    x, y, c = _place()
    chip = 2 * x + y
    for k in relations:
        fx, fy = OTHER_CHIPS[k]
        peer = (_flip(x, fx), _flip(y, fy), c)
        for t, buf in enumerate(bufs):
            _rcopy(_slab(buf, chip, c), _slab(buf, chip, c), ssem.at[t * 3 + k], rsem.at[t * 3 + k], peer).start()


def _gather_finish(bufs, ssem, rsem):
    nt = len(bufs)
    x, y, c = _place()
    chip = 2 * x + y
    sibling = (x, y, 1 - c)
    for k, (fx, fy) in enumerate(OTHER_CHIPS):
        peer = (_flip(x, fx), _flip(y, fy), c)
        pchip = 2 * peer[0] + peer[1]
        for t, buf in enumerate(bufs):
            _rcopy(_slab(buf, pchip, c), _slab(buf, pchip, c), ssem.at[t * 3 + k], rsem.at[t * 3 + k], peer).wait_recv()
            _rcopy(_slab(buf, pchip, c), _slab(buf, pchip, c), ssem.at[3 * nt + t * 3 + k],
                   rsem.at[3 * nt + t * 3 + k], sibling).start()
    for k, (fx, fy) in enumerate(OTHER_CHIPS):
        peer = (_flip(x, fx), _flip(y, fy), c)
        pchip = 2 * peer[0] + peer[1]
        for t, buf in enumerate(bufs):
            _rcopy(_slab(buf, pchip, 1 - c), _slab(buf, pchip, 1 - c), ssem.at[3 * nt + t * 3 + k],
                   rsem.at[3 * nt + t * 3 + k], sibling).wait_recv()
            _rcopy(_slab(buf, chip, c), _slab(buf, chip, c), ssem.at[t * 3 + k], rsem.at[t * 3 + k], peer).wait_send()
            _rcopy(_slab(buf, pchip, c), _slab(buf, pchip, c), ssem.at[3 * nt + t * 3 + k],
                   rsem.at[3 * nt + t * 3 + k], sibling).wait_send()


def _gather_arrive(bufs, k, ssem, rsem):
    nt = len(bufs)
    x, y, c = _place()
    fx, fy = OTHER_CHIPS[k]
    peer = (_flip(x, fx), _flip(y, fy), c)
    pchip = 2 * peer[0] + peer[1]
    for t, buf in enumerate(bufs):
        _rcopy(_slab(buf, pchip, c), _slab(buf, pchip, c), ssem.at[t * 3 + k], rsem.at[t * 3 + k], peer).wait_recv()
        _rcopy(_slab(buf, pchip, c), _slab(buf, pchip, c), ssem.at[3 * nt + t * 3 + k],
               rsem.at[3 * nt + t * 3 + k], (x, y, 1 - c)).start()
    for t, buf in enumerate(bufs):
        _rcopy(_slab(buf, pchip, 1 - c), _slab(buf, pchip, 1 - c), ssem.at[3 * nt + t * 3 + k],
               rsem.at[3 * nt + t * 3 + k], (x, y, 1 - c)).wait_recv()


def _gather_sends_done(bufs, ssem, rsem):
    nt = len(bufs)
    x, y, c = _place()
    chip = 2 * x + y
    for k, (fx, fy) in enumerate(OTHER_CHIPS):
        peer = (_flip(x, fx), _flip(y, fy), c)
        pchip = 2 * peer[0] + peer[1]
        for t, buf in enumerate(bufs):
            _rcopy(_slab(buf, chip, c), _slab(buf, chip, c), ssem.at[t * 3 + k], rsem.at[t * 3 + k], peer).wait_send()
            _rcopy(_slab(buf, pchip, c), _slab(buf, pchip, c), ssem.at[3 * nt + t * 3 + k],
                   rsem.at[3 * nt + t * 3 + k], (x, y, 1 - c)).wait_send()


def _gather_sems(nt):
    return [pltpu.SemaphoreType.DMA((6 * nt,)), pltpu.SemaphoreType.DMA((6 * nt,))]


def _pair_copies(ins, outs, ssem, rsem):
    x, y, c = _place()
    copies = []
    for t in range(len(ins)):
        hs = ins[t].shape[1] // 2
        copies.append(_rcopy(ins[t].at[:, pl.ds((1 - c) * hs, hs), :], outs[t], ssem.at[t], rsem.at[t], (x, y, 1 - c)))
    return copies


def _pair_shapes(grads):
    return [_big((g.shape[0], g.shape[1] // 2, g.shape[2]), g.dtype) for g in grads]


def _pair_sems(nt):
    return [pltpu.SemaphoreType.DMA((nt,)), pltpu.SemaphoreType.DMA((nt,))]


def _pair_swap(grads, name):
    nt = len(grads)
    hbm = pl.BlockSpec(memory_space=pl.ANY)

    def body(*refs):
        copies = _pair_copies(refs[:nt], refs[nt:2 * nt], refs[2 * nt], refs[2 * nt + 1])
        for cp in copies:
            cp.start()
        for cp in copies:
            cp.wait_recv()
        for cp in copies:
            cp.wait_send()

    return pl.pallas_call(
        body, name=name, out_shape=_pair_shapes(grads), in_specs=[hbm] * nt, out_specs=[hbm] * nt,
        scratch_shapes=_pair_sems(nt),
    )(*[_hbm(g) for g in grads])


def _xchg_copies(ins, outs, ssem, rsem):
    x, y, c = _place()
    copies = []
    for k, (fx, fy) in enumerate(OTHER_CHIPS):
        peer = (_flip(x, fx), _flip(y, fy), c)
        for t in range(len(ins)):
            copies.append(_rcopy(ins[t].at[k], outs[t].at[k], ssem.at[t * 3 + k], rsem.at[t * 3 + k], peer))
    return copies


def _xchg_sems(nt):
    return [pltpu.SemaphoreType.DMA((3 * nt,)), pltpu.SemaphoreType.DMA((3 * nt,))]


def _final_exchange(fulls, stats):
    nt = len(fulls)
    rows, cols = stats.shape
    hbm = pl.BlockSpec(memory_space=pl.ANY)
    vm = pl.BlockSpec(memory_space=pltpu.VMEM)

    def body(*refs):
        ins, s_ref = refs[:nt], refs[nt]
        outs, g_ref = refs[nt + 1:2 * nt + 1], refs[2 * nt + 1]
        hssem, hrsem, ssem, rsem = refs[2 * nt + 2:]
        x, y, c = _place()
        me, sibling = (x, y, c), (x, y, 1 - c)
        halves = []
        for t in range(nt):
            hs = ins[t].shape[0] // 2
            mine = pl.ds(c * hs, hs)
            cp = _rcopy(ins[t].at[mine, :], outs[t].at[mine, :], hssem.at[t], hrsem.at[t], sibling)
            cp.start()
            halves.append(cp)

        chips = [(_flip(x, fx), _flip(y, fy)) for fx, fy in OTHER_CHIPS]

        def blk(px, py, pc):
            return g_ref.at[4 * px + 2 * py + pc]

        def copy(k, block, to, src=None):
            return _rcopy(blk(*block) if src is None else src, blk(*block), ssem.at[k], rsem.at[k], to)

        g_ref[4 * x + 2 * y + c] = s_ref[...]
        first = [copy(0, me, sibling, src=s_ref)]
        first += [copy(1 + j, me, (*chip, c), src=s_ref) for j, chip in enumerate(chips)]
        for cp in first:
            cp.start()
        passed = [copy(4 + j, (*chip, c), sibling) for j, chip in enumerate(chips)]
        for j, chip in enumerate(chips):
            copy(1 + j, (*chip, c), me).wait_recv()
            passed[j].start()
        copy(0, sibling, me).wait_recv()
        for j, chip in enumerate(chips):
            copy(4 + j, (*chip, 1 - c), me).wait_recv()
        for t in range(nt):
            hs = ins[t].shape[0] // 2
            other = pl.ds((1 - c) * hs, hs)
            _rcopy(ins[t].at[other, :], outs[t].at[other, :], hssem.at[t], hrsem.at[t], sibling).wait_recv()
        for cp in first + passed + halves:
            cp.wait_send()

    outs = pl.pallas_call(
        body, name="final_exchange",
        out_shape=[_big(f.shape, F32) for f in fulls] + [jax.ShapeDtypeStruct((N_DEV, rows, cols), F32)],
        in_specs=[hbm] * nt + [vm], out_specs=[hbm] * nt + [vm],
        input_output_aliases={t: t for t in range(nt)},
        scratch_shapes=[pltpu.SemaphoreType.DMA((nt,)), pltpu.SemaphoreType.DMA((nt,)),
                        pltpu.SemaphoreType.DMA((7,)), pltpu.SemaphoreType.DMA((7,))],
        compiler_params=_cp(None, 32),
    )(*[_hbm(f) for f in fulls], stats)
    return outs[:nt], outs[nt]


def _row_tile(rows):
    return min(rows, 512)


def _pair_sum(place, grad, got, name):
    nb, hs, cols = got.shape
    tr = _row_tile(hs)
    nr = hs // tr

    def body(pl_ref, g_ref, o_ref, pf_ref, pb_ref):
        j = pl.program_id(1)
        s = g_ref[0] + o_ref[0].astype(F32)

        @pl.when(j == 0)
        def _():
            pf_ref[...] = s

        @pl.when(j > 0)
        def _():
            pb_ref[0] = s.astype(BF16)

    gs = pltpu.PrefetchScalarGridSpec(
        num_scalar_prefetch=1, grid=(nr, nb),
        in_specs=[pl.BlockSpec((1, tr, cols), lambda i, j, p: (p[1] ^ j, p[0] * nr + i, 0)),
                  pl.BlockSpec((1, tr, cols), lambda i, j, p: (p[1] ^ j, i, 0))],
        out_specs=[pl.BlockSpec((tr, cols), lambda i, j, p: (i, 0)),
                   pl.BlockSpec((1, tr, cols), lambda i, j, p: (jnp.maximum(j - 1, 0), i, 0))])
    return pl.pallas_call(
        body, name=name, grid_spec=gs,
        out_shape=[_big((hs, cols), F32), _big((nb - 1, hs, cols), BF16)],
        compiler_params=_cp(("arbitrary", "arbitrary"), 32),
    )(place, *_hbm(grad, got))


def _chip_sum(place, pair_f, got_b, name):
    nb, hs, cols = got_b.shape
    tr = _row_tile(hs)
    nr = hs // tr

    def body(pl_ref, pf_ref, gb_ref, o_ref):
        acc = pf_ref[...]
        for k in range(nb):
            acc = acc + gb_ref[k].astype(F32)
        o_ref[...] = acc

    gs = pltpu.PrefetchScalarGridSpec(
        num_scalar_prefetch=1, grid=(nr,),
        in_specs=[pl.BlockSpec((tr, cols), lambda i, p: (i, 0)),
                  pl.BlockSpec((nb, tr, cols), lambda i, p: (0, i, 0))],
        out_specs=pl.BlockSpec((tr, cols), lambda i, p: (p[0] * nr + i, 0)))
    return pl.pallas_call(
        body, name=name, grid_spec=gs,
        out_shape=_big((2 * hs, cols), F32),
        compiler_params=_cp(("arbitrary",), 32),
    )(place, *_hbm(pair_f, got_b))


def _adam_math(w, g, m, v):
    m2 = ADAM_B1 * m + (1.0 - ADAM_B1) * g
    v2 = ADAM_B2 * v + (1.0 - ADAM_B2) * (g * g)
    m_hat = m2 / (1.0 - ADAM_B1 ** ADAM_STEP)
    v_hat = v2 / (1.0 - ADAM_B2 ** ADAM_STEP)
    delta = -ADAM_LR * (m_hat / (jnp.sqrt(v_hat) + ADAM_EPS) + ADAM_WD * w)
    return delta, m2, v2


def _adam_big(w, g, m, v, name):
    rows, cols = w.shape
    tr = _row_tile(rows)

    def body(w_ref, g_ref, m_ref, v_ref, d_out, m_out, v_out):
        d, m2, v2 = _adam_math(w_ref[...], g_ref[...], m_ref[...], v_ref[...])
        d_out[...] = d
        m_out[...] = m2
        v_out[...] = v2

    spec = pl.BlockSpec((tr, cols), lambda i: (i, 0))
    return pl.pallas_call(
        body, name=name, grid=(rows // tr,), in_specs=[spec] * 4, out_specs=[spec] * 3,
        out_shape=[_big(w.shape, F32)] * 3,
        compiler_params=_cp(("arbitrary",), 48),
    )(*_hbm(w, g, m, v))


def _ada_grad_adam(c8t, dmod_sh, w, m, v):
    rows, cols = w.shape
    tr = _row_tile(rows) // 2

    def body(ct_ref, dm_ref, w_ref, m_ref, v_ref, g_out, d_out, m_out, v_out):
        g = None
        for b in range(N_DEV):
            term = ct_ref[:, b:b + 1] * dm_ref[b:b + 1, :]
            g = term if g is None else g + term
        d, m2, v2 = _adam_math(w_ref[...], g, m_ref[...], v_ref[...])
        g_out[...] = g
        d_out[...] = d
        m_out[...] = m2
        v_out[...] = v2

    spec = pl.BlockSpec((tr, cols), lambda i: (i, 0))
    return pl.pallas_call(
        body, name="ada_grad_adam", grid=(rows // tr,),
        in_specs=[pl.BlockSpec((tr, N_DEV), lambda i: (i, 0)), _full((N_DEV, cols)), spec, spec, spec],
        out_specs=[spec] * 4, out_shape=[_big(w.shape, F32)] * 4,
        compiler_params=_cp(("arbitrary",), 32),
    )(c8t, dmod_sh, *_hbm(w, m, v))


def _tok_tile(t):
    return min(TOK_TILE, t)


def _mix_in_fwd(chip1, x, modr, g_pre, b_in4, w_in_buf, w_out_buf):
    T = x.shape[0]
    tm = min(MIXIN_TILE, T)
    nt = T // tm
    nb = IN_COLS // N_CHIPS

    def body(ch_ref, x_ref, mod_ref, g_ref, b_ref, win_in, wout_in, p_ref, h_ref, win_ref, wout_ref,
             h_all, wblk, lsem, is_sem, ir_sem, os_sem, or_sem):
        k, i = pl.program_id(0), pl.program_id(1)
        chip = ch_ref[0]

        def load_block(blk):
            cp = pltpu.make_async_copy(win_ref.at[blk], wblk, lsem)
            cp.start()
            cp.wait()

        @pl.when((k == 0) & (i == 0))
        def _():
            _gather_start([win_ref], is_sem, ir_sem, relations=(0, 1))
            load_block(chip)

        for r in range(N_CHIPS - 1):
            @pl.when((k == r + 1) & (i == 0))
            def _(r=r):
                _gather_arrive([win_ref], r, is_sem, ir_sem)
                if r == 0:
                    _gather_start([win_ref], is_sem, ir_sem, relations=(2,))
                if r == 1:
                    _gather_start([wout_ref], os_sem, or_sem)
                load_block(chip ^ (r + 1))

        rows = pl.ds(pl.multiple_of(i * tm, tm), tm)

        @pl.when(k == 0)
        def _():
            xv = x_ref[...]
            rstd = lax.rsqrt(jnp.mean(xv * xv, axis=-1, keepdims=True) + RMS_EPS)
            h = (xv * rstd) * g_ref[...] * (1.0 + mod_ref[1:2, :]) + mod_ref[0:1, :]
            hb = h.astype(BF16)
            h_ref[...] = hb
            h_all[rows, :] = hb

        p_ref[...] = _dot(h_all[rows, :], wblk[...]) + b_ref[chip ^ k]

        @pl.when((k == N_CHIPS - 1) & (i == nt - 1))
        def _():
            _gather_sends_done([win_ref], is_sem, ir_sem)
            _gather_finish([wout_ref], os_sem, or_sem)

    hbm = pl.BlockSpec(memory_space=pl.ANY)
    first_pass = lambda k, i, ch: (jnp.where(k == 0, i, nt - 1), 0)
    gs = pltpu.PrefetchScalarGridSpec(
        num_scalar_prefetch=1, grid=(N_CHIPS, nt),
        in_specs=[pl.BlockSpec((tm, D_MODEL), first_pass), pl.BlockSpec((6, D_MODEL), lambda k, i, ch: (0, 0)),
                  pl.BlockSpec((1, D_MODEL), lambda k, i, ch: (0, 0)),
                  pl.BlockSpec((N_CHIPS, 1, nb), lambda k, i, ch: (0, 0, 0)), hbm, hbm],
        out_specs=[pl.BlockSpec((tm, nb), lambda k, i, ch: (i, ch[0] ^ k)), pl.BlockSpec((tm, D_MODEL), first_pass),
                   hbm, hbm],
        scratch_shapes=[pltpu.VMEM((T, D_MODEL), BF16), pltpu.VMEM((D_MODEL, nb), BF16), pltpu.SemaphoreType.DMA]
        + _gather_sems(1) + _gather_sems(1))
    return pl.pallas_call(
        body, name="mix_in_fwd", grid_spec=gs,
        out_shape=[_big((T, IN_COLS), F32), _big((T, D_MODEL), BF16), _big(w_in_buf.shape, BF16),
                   _big(w_out_buf.shape, BF16)],
        input_output_aliases={5: 2, 6: 3},
        compiler_params=_cp(("arbitrary", "arbitrary"), 48),
    )(chip1, _hbm(x), modr, g_pre, b_in4, _hbm(w_in_buf), _hbm(w_out_buf))


def _mixers_fwd(p, wdw, vecs, lbl, gbufs):
    T = p.shape[0]
    tm = _tok_tile(T)
    nt = T // tm
    nch = tm // CHUNK
    ng = len(gbufs)

    def body(*refs):
        p_ref, wdw_ref, bdw_ref, gain_ref, bias_ref, gout_ref, lbl_ref = refs[:7]
        cat_ref, ys_ref, o_ref, st_ref = refs[7 + ng:11 + ng]
        gout_bufs = refs[11 + ng:11 + 2 * ng]
        ubuf, state, qt_s, kt_s, kh_s, v_s, egl_s, lower_s, same_s, gmat_s, gssem, grsem = refs[11 + 2 * ng:]
        i = pl.program_id(0)

        @pl.when(i == 0)
        def _():
            _gather_start(gout_bufs, gssem, grsem)
            lower_s[...], _, same_s[...] = _chunk_masks(tm)
            gmat_s[...] = _gn_matrix()
            state[...] = jnp.zeros(state.shape, F32)
            ubuf[0:HALO, :] = jnp.zeros((HALO, CONV_CH), F32)
            ubuf[HALO + tm:HALO + tm + SUB, :] = jnp.zeros((SUB, CONV_CH), F32)

        @pl.when(i > 0)
        def _():
            ubuf[0:HALO, :] = ubuf[tm:tm + HALO, :]

        ubuf[HALO:HALO + tm, :] = p_ref[:, 0:CONV_CH] * _sig(p_ref[:, CONV_CH:2 * CONV_CH])
        for r in range(tm // CONV_ROWS):
            rows = slice(r * CONV_ROWS, (r + 1) * CONV_ROWS)
            for lb_ in range(CONV_CH // LANE):
                lanes = slice(lb_ * LANE, (lb_ + 1) * LANE)
                ys_ref[rows, lanes] = bdw_ref[:, lanes] + _tap_conv(ubuf, wdw_ref, r * CONV_ROWS, CONV_FWD_TAPS, lanes)
        gmat = gmat_s[...]
        yv = ys_ref[...]
        d = yv - _gmean(yv, gmat)
        rs = lax.rsqrt(_gmean(d * d, gmat) + GN_EPS)
        z = d * rs * gain_ref[...] + bias_ref[...]
        cat_ref[:, 0:CONV_CH] = (z * _sig(z)).astype(BF16)

        lb, _ = _lower_bound(lbl_ref)
        lower, same = lower_s[...], same_s[...]
        o0 = 2 * CONV_CH
        pr = _hgrn_prep(p_ref[:, o0:o0 + HGRN_W], p_ref[:, o0 + HGRN_W:o0 + 2 * HGRN_W], lb, lower, same)
        qt_s[...] = pr["qt"].astype(BF16)
        kt_s[...] = pr["kt"].astype(BF16)
        kh_s[...] = pr["kh"].astype(BF16)
        v_s[...] = p_ref[:, o0 + 2 * HGRN_W:o0 + 3 * HGRN_W].astype(BF16)
        egl_s[...] = jnp.exp(pr["Gl"])
        tri = _tri()

        def chunk(ci, carry):
            r0 = pl.multiple_of(ci * CHUNK, CHUNK)
            rows = pl.ds(r0, CHUNK)
            for h in range(N_HEADS):
                ls = pl.ds(h * HEAD_D, HEAD_D)
                qc, kc, hc, vc = qt_s[rows, ls], kt_s[rows, ls], kh_s[rows, ls], v_s[rows, ls]
                s0 = state[h]
                s0b = s0.astype(BF16)
                st_ref[ci, h] = s0
                att =jnp.where(tri, _dot_nt(qc, kc), 0.0).astype(BF16)
                o_ref[rows, ls] = _dot(att, vc) + _dot_nt(qc, s0b)
                state[h] = s0 * egl_s[pl.ds(r0, 1), ls] + _dot_tn(vc, hc)
            return carry

        lax.fori_loop(0, nch, chunk, 0)
        for h in range(N_HEADS):
            sl = slice(h * HEAD_D, (h + 1) * HEAD_D)
            oh = o_ref[:, sl]
            gh = p_ref[:, o0 + 3 * HGRN_W + h * HEAD_D:o0 + 3 * HGRN_W + (h + 1) * HEAD_D]
            rsh = lax.rsqrt(jnp.mean(oh * oh, axis=-1, keepdims=True) + RMS_EPS)
            hg = (oh * rsh) * gout_ref[:, sl] * (gh * _sig(gh))
            cat_ref[:, CONV_CH + h * HEAD_D:CONV_CH + (h + 1) * HEAD_D] = hg.astype(BF16)

        @pl.when(i == max(nt - 2, 0))
        def _():
            _gather_arrive(gout_bufs, 0, gssem, grsem)
            _gather_arrive(gout_bufs, 1, gssem, grsem)

        @pl.when(i == nt - 1)
        def _():
            _gather_arrive(gout_bufs, 2, gssem, grsem)
            _gather_sends_done(gout_bufs, gssem, grsem)

    tile = lambda cols: pl.BlockSpec((tm, cols), lambda i: (i, 0))
    hbm = pl.BlockSpec(memory_space=pl.ANY)
    n_in = 7
    outs = pl.pallas_call(
        body, name="mixers_fwd", grid=(nt,),
        in_specs=[tile(IN_COLS), _full((HALO, CONV_CH))] + [_full((1, CONV_CH))] * 4 + [_full((2, HGRN_W))]
        + [hbm] * ng,
        out_specs=[tile(D_MODEL), tile(CONV_CH), tile(HGRN_W),
                   pl.BlockSpec((nch, N_HEADS, HEAD_D, HEAD_D), lambda i: (i, 0, 0, 0))] + [hbm] * ng,
        out_shape=[_big((T, D_MODEL), BF16), _big((T, CONV_CH), F32), _big((T, HGRN_W), F32),
                   _big((T // CHUNK, N_HEADS, HEAD_D, HEAD_D), F32)] + [_big(b.shape, BF16) for b in gbufs],
        input_output_aliases={n_in + t: 4 + t for t in range(ng)},
        scratch_shapes=[pltpu.VMEM((tm + HALO + SUB, CONV_CH), F32), pltpu.VMEM((N_HEADS, HEAD_D, HEAD_D), F32),
                        pltpu.VMEM((tm, HGRN_W), BF16), pltpu.VMEM((tm, HGRN_W), BF16),
                        pltpu.VMEM((tm, HGRN_W), BF16), pltpu.VMEM((tm, HGRN_W), BF16),
                        pltpu.VMEM((tm, HGRN_W), F32), pltpu.VMEM((tm, tm), BF16), pltpu.VMEM((tm, tm), BF16),
                        pltpu.VMEM((CONV_CH, CONV_CH), BF16)] + _gather_sems(ng),
        compiler_params=_cp(("arbitrary",), 56),
    )(_hbm(p), wdw, *vecs, lbl, *[_hbm(b) for b in gbufs])
    return outs[:4], outs[4:]


def _mix_out_fwd(cat, w_out, x, modr, g_post, g_ffn):
    T = x.shape[0]
    tm = _tok_tile(T)

    def body(cat_ref, w_ref, x_ref, mod_ref, gp_ref, gf_ref, y_ref, x1_ref, h2_ref):
        yv = _dot(cat_ref[...], w_ref[...])
        y_ref[...] = yv
        rs = lax.rsqrt(jnp.mean(yv * yv, axis=-1, keepdims=True) + RMS_EPS)
        x1 = x_ref[...] + mod_ref[2:3, :] * ((yv * rs) * gp_ref[...])
        x1_ref[...] = x1
        rs1 = lax.rsqrt(jnp.mean(x1 * x1, axis=-1, keepdims=True) + RMS_EPS)
        h2 = (x1 * rs1) * gf_ref[...] * (1.0 + mod_ref[4:5, :]) + mod_ref[3:4, :]
        h2_ref[...] = h2.astype(BF16)

    tile = pl.BlockSpec((tm, D_MODEL), lambda i: (i, 0))
    return pl.pallas_call(
        body, name="mix_out_fwd", grid=(T // tm,),
        in_specs=[tile, _full((D_MODEL, D_MODEL)), tile, _full((6, D_MODEL)), _full((1, D_MODEL)),
                  _full((1, D_MODEL))],
        out_specs=[tile, tile, tile],
        out_shape=[_big((T,D_MODEL), F32), _big((T,D_MODEL), F32),
                   _big((T,D_MODEL), BF16)],
        compiler_params=_cp(("arbitrary",), 40),
    )(*_hbm(cat, w_out, x), modr, g_post, g_ffn)


def _row_chains(rows, n=2):
    step = rows // n
    return [slice(k * step, (k + 1) * step) for k in range(n)]


def _ffn_blocks():
    return D_FF // FFN_BLOCK, (D_FF // N_CHIPS) // FFN_BLOCK


def _ffn_fwd(h2, w_up_g, w_down, x1, target, modr, g_post):
    T = h2.shape[0]
    tm = min(FFN_TILE, T)
    fb = FFN_BLOCK
    nj, per = _ffn_blocks()

    def body(h_ref, wu_ref, wd_ref, x1_ref, t_ref, mod_ref, g_ref, r_ref, dy2_ref, dx2_ref, st_ref, acc):
        i, j = pl.program_id(0), pl.program_id(1)

        @pl.when((i == 0) & (j == 0))
        def _():
            st_ref[...] = jnp.zeros(st_ref.shape, F32)

        @pl.when(j == 0)
        def _():
            acc[...] = jnp.zeros(acc.shape, F32)

        for rows in _row_chains(tm):
            ra = jnp.maximum(_dot(h_ref[rows, :], wu_ref[0]), 0.0)
            rb = (ra * ra).astype(BF16)
            r_ref[rows, :] = rb
            acc[rows, :] += _dot(rb, wd_ref[...])

        @pl.when(j == nj - 1)
        def _():
            y2 = acc[...]
            rs = lax.rsqrt(jnp.mean(y2 * y2, axis=-1, keepdims=True) + RMS_EPS)
            nh = y2 * rs
            gp = g_ref[...]
            err = x1_ref[...] + mod_ref[5:6, :] * (nh * gp) - t_ref[...]
            dx2 = err * (1.0 / D_MODEL)
            dx2_ref[...] = dx2
            st_ref[0:1, :] += _colsum(err * err)
            st_ref[1:2, :] += _colsum(dx2 * (nh * gp))
            dn = dx2 * mod_ref[5:6, :]
            st_ref[2:3, :] += _colsum(dn * nh)
            dy2_ref[...] = _rms_bwd(dn * gp, nh, rs).astype(BF16)

    tile = pl.BlockSpec((tm, D_MODEL), lambda i, j: (i, 0))
    return pl.pallas_call(
        body, name="ffn_fwd", grid=(T // tm, nj),
        in_specs=[tile, pl.BlockSpec((1, D_MODEL, fb), lambda i, j: (j // per, 0, j % per)),
                  pl.BlockSpec((fb, D_MODEL), lambda i, j: (j, 0)), tile, tile,
                  _full((6, D_MODEL)), _full((1, D_MODEL))],
        out_specs=[pl.BlockSpec((tm, fb), lambda i, j: (i, j)), tile, tile, _full((8, D_MODEL))],
        out_shape=[_big((T, D_FF), BF16), _big((T, D_MODEL), BF16), _big((T, D_MODEL), F32),
                   jax.ShapeDtypeStruct((8, D_MODEL), F32)],
        scratch_shapes=[pltpu.VMEM((tm, D_MODEL), F32)],
        compiler_params=_cp(("arbitrary", "arbitrary"), 56),
    )(*_hbm(h2, w_up_g, w_down, x1, target), modr, g_post)


def _rms_bwd(dxn, xn, rs):
    return rs * (dxn - xn * jnp.mean(dxn * xn, axis=-1, keepdims=True))


def _ffn_bwd(dy2, r, x1, dx2, w_up_g, w_down, modr, g_ffn):
    T = dx2.shape[0]
    tm = min(FFN_TILE, T)
    fb = FFN_BLOCK
    nj, per = _ffn_blocks()

    def body(dy2_ref, r_ref, x1_ref, dx2_ref, wu_ref, wd_ref, mod_ref, gf_ref, da_ref, dx1_ref, st_ref, dh_s):
        i, j = pl.program_id(0), pl.program_id(1)

        @pl.when((i == 0) & (j == 0))
        def _():
            st_ref[...] = jnp.zeros(st_ref.shape, F32)

        @pl.when(j == 0)
        def _():
            dh_s[...] = jnp.zeros(dh_s.shape, F32)

        for rows in _row_chains(tm):
            ra = jnp.sqrt(r_ref[rows, :].astype(F32))
            da = (_dot_nt(dy2_ref[rows, :], wd_ref[...]) * (2.0 * ra)).astype(BF16)
            da_ref[rows, :] = da
            dh_s[rows, :] += _dot_nt(da, wu_ref[0])

        @pl.when(j == nj - 1)
        def _():
            dh = dh_s[...]
            x1v = x1_ref[...]
            rs1 = lax.rsqrt(jnp.mean(x1v * x1v, axis=-1, keepdims=True) + RMS_EPS)
            xn = x1v * rs1
            st_ref[0:1, :] += _colsum(dh)
            st_ref[1:2, :] += _colsum(dh * (xn * gf_ref[...]))
            dsc = dh * (1.0 + mod_ref[4:5, :])
            st_ref[2:3, :] += _colsum(dsc * xn)
            dx1_ref[...] = dx2_ref[...] + _rms_bwd(dsc * gf_ref[...], xn, rs1)

    tile = pl.BlockSpec((tm, D_MODEL), lambda i, j: (i, 0))
    ftile = pl.BlockSpec((tm, fb), lambda i, j: (i, j))
    return pl.pallas_call(
        body, name="ffn_bwd", grid=(T // tm, nj),
        in_specs=[tile, ftile, tile, tile, pl.BlockSpec((1, D_MODEL, fb), lambda i, j: (j // per, 0, j % per)),
                  pl.BlockSpec((fb, D_MODEL), lambda i, j: (j, 0)), _full((6, D_MODEL)), _full((1, D_MODEL))],
        out_specs=[ftile, tile, _full((8, D_MODEL))],
        out_shape=[_big((T, D_FF), BF16), _big((T, D_MODEL), F32), jax.ShapeDtypeStruct((8, D_MODEL), F32)],
        scratch_shapes=[pltpu.VMEM((tm, D_MODEL), F32)],
        compiler_params=_cp(("arbitrary", "arbitrary"), 56),
    )(*_hbm(dy2, r, x1, dx2, w_up_g, w_down), modr, g_ffn)


def _mix_out_bwd(dx1, y, w_out, modr, g_post, swap):
    T = dx1.shape[0]
    tm = _tok_tile(T)
    nt = T // tm
    ns = len(swap)

    def body(*refs):
        dx1_ref, y_ref, w_ref, mod_ref, gp_ref = refs[:5]
        s_ins = refs[5:5 + ns]
        dy_ref, dcat_ref, st_ref = refs[5 + ns:8 + ns]
        s_outs = refs[8 + ns:8 + 2 * ns]
        pssem, prsem = refs[8 + 2 * ns:]
        i = pl.program_id(0)

        @pl.when(i == 0)
        def _():
            for cp in _pair_copies(s_ins, s_outs, pssem, prsem):
                cp.start()
            st_ref[...] = jnp.zeros(st_ref.shape, F32)

        dxv, yv = dx1_ref[...], y_ref[...]
        rs = lax.rsqrt(jnp.mean(yv * yv, axis=-1, keepdims=True) + RMS_EPS)
        nh = yv * rs
        st_ref[0:1, :] += _colsum(dxv * (nh * gp_ref[...]))
        dn = dxv * mod_ref[2:3, :]
        st_ref[1:2, :] += _colsum(dn * nh)
        dy = _rms_bwd(dn * gp_ref[...], nh, rs).astype(BF16)
        dy_ref[...] = dy
        dcat_ref[...] = _dot_nt(dy, w_ref[...])

        @pl.when(i == nt - 1)
        def _():
            copies = _pair_copies(s_ins, s_outs, pssem, prsem)
            for cp in copies:
                cp.wait_recv()
            for cp in copies:
                cp.wait_send()

    tile = pl.BlockSpec((tm, D_MODEL), lambda i: (i, 0))
    hbm = pl.BlockSpec(memory_space=pl.ANY)
    outs = pl.pallas_call(
        body, name="mix_out_bwd", grid=(nt,),
        in_specs=[tile, tile, _full((D_MODEL, D_MODEL)), _full((6, D_MODEL)), _full((1, D_MODEL))] + [hbm] * ns,
        out_specs=[tile, tile, _full((8, D_MODEL))] + [hbm] * ns,
        out_shape=[_big((T, D_MODEL), BF16), _big((T, D_MODEL), F32), jax.ShapeDtypeStruct((8, D_MODEL), F32)]
        + _pair_shapes(swap),
        scratch_shapes=_pair_sems(ns),
        compiler_params=_cp(("arbitrary",), 40),
    )(*_hbm(dx1, y, w_out), modr, g_post, *[_hbm(g) for g in swap])
    return outs[:3], outs[3:]


def _mixers_bwd(p, dcat, ys, o, states, wdw, vecs, lbl, pairs_b):
    T = p.shape[0]
    tm = min(MIXB_TILE, T)
    nt = T // tm
    nch = tm // CHUNK
    hpt = tm // HALO
    nx = len(pairs_b)

    def body(*refs):
        (p_ref, ph_ref, dcat_ref, ys_ref, o_ref, st_ref, wdw_ref, bdw_ref, gain_ref, bias_ref, gout_ref,
         lbl_ref) = refs[:12]
        x_ins = refs[12:12 + nx]
        dp_ref, sb_ref, s5_ref, dw_ref = refs[12 + nx:16 + nx]
        x_outs = refs[16 + nx:16 + 2 * nx]
        (ubuf, dybuf, carry, dstate, qt_s, kt_s, kh_s, v_s, do_s, egl_s, dqt_s, dkt_s, dkh_s, dv_s, dgl_s,
         dsh, dw8, dshift, lower_s, upper_s, same_s, gmat_s, xssem, xrsem) = refs[16 + 2 * nx:]
        i = pl.program_id(0)
        tile_idx = nt - 1 - i

        @pl.when(i == 0)
        def _():
            for cp in _xchg_copies(x_ins, x_outs, xssem, xrsem):
                cp.start()
            dstate[...] = jnp.zeros(dstate.shape, F32)
            carry[...] = jnp.zeros(carry.shape, F32)
            sb_ref[...] = jnp.zeros(sb_ref.shape, F32)
            s5_ref[...] = jnp.zeros(s5_ref.shape, F32)
            dw_ref[...] = jnp.zeros(dw_ref.shape, F32)
            dw8[...] = jnp.zeros(dw8.shape, F32)
            lower_s[...], upper_s[...], same_s[...] = _chunk_masks(tm)
            gmat_s[...] = _gn_matrix()
            dsh[0:SUB, :] = jnp.zeros((SUB, CONV_CH), F32)
            dsh[SUB + tm:2 * SUB + tm, :] = jnp.zeros((SUB, CONV_CH), F32)
            ubuf[HALO + tm:HALO + tm + SUB, :] = jnp.zeros((SUB, CONV_CH), F32)

        uh = ph_ref[:, 0:CONV_CH] * _sig(ph_ref[:, CONV_CH:2 * CONV_CH])
        ubuf[0:HALO, :] = jnp.where(tile_idx > 0, uh, 0.0)
        ubuf[HALO:HALO + tm, :] = p_ref[:, 0:CONV_CH] * _sig(p_ref[:, CONV_CH:2 * CONV_CH])
        gmat = gmat_s[...]
        gain = gain_ref[...]
        yv = ys_ref[...]
        d = yv - _gmean(yv, gmat)
        rs = lax.rsqrt(_gmean(d * d, gmat) + GN_EPS)
        yn = d * rs
        z = yn * gain + bias_ref[...]
        sz = _sig(z)
        dz = dcat_ref[:, 0:CONV_CH] * (sz * (1.0 + z * (1.0 - sz)))
        dyn = dz * gain
        dyc = rs * (dyn - _gmean(dyn, gmat) - yn * _gmean(dyn * yn, gmat))
        s5_ref[0:1, :] += _colsum(dyc)
        s5_ref[1:2, :] += _colsum(dz * yn)
        s5_ref[2:3, :] += _colsum(dz)
        dybuf[tm:tm + HALO, :] = carry[...]
        dybuf[0:tm, :] = dyc
        dsh[SUB:SUB + tm, :] = dyc
        carry[...] = dyc[0:HALO, :]
        for b in range(SUB):
            dshift[...] = dsh[SUB - b:2 * SUB - b + tm, :]
            for j, off in CONV_FWD_TAPS:
                if off % SUB == b:
                    prod = dshift[...] * ubuf[off - b:off - b + tm + SUB, :]
                    dw8[j] += jnp.sum(prod.reshape((tm + SUB) // SUB, SUB, CONV_CH), axis=0)
        for r in range(tm // CONV_ROWS):
            rows = slice(r * CONV_ROWS, (r + 1) * CONV_ROWS)
            for lb_ in range(CONV_CH // LANE):
                lanes = slice(lb_ * LANE, (lb_ + 1) * LANE)
                glanes = slice(CONV_CH + lb_ * LANE, CONV_CH + (lb_ + 1) * LANE)
                acc = _tap_conv(dybuf, wdw_ref, r * CONV_ROWS, CONV_BWD_TAPS, lanes)
                val = p_ref[rows, lanes]
                sg = _sig(p_ref[rows, glanes])
                dval = acc * sg
                dgate = acc * val * (sg * (1.0 - sg))
                dp_ref[rows, lanes] = dval.astype(BF16)
                dp_ref[rows, glanes] = dgate.astype(BF16)
                sb_ref[0:1, lanes] += _colsum(dval)
                sb_ref[0:1, glanes] += _colsum(dgate)

        o0 = 2 * CONV_CH
        for h in range(N_HEADS):
            sl = slice(h * HEAD_D, (h + 1) * HEAD_D)
            gsl = slice(o0 + 3 * HGRN_W + h * HEAD_D, o0 + 3 * HGRN_W + (h + 1) * HEAD_D)
            oh = o_ref[:, sl]
            gh = p_ref[:, gsl]
            dh = dcat_ref[:, CONV_CH + h * HEAD_D:CONV_CH + (h + 1) * HEAD_D]
            gout = gout_ref[:, sl]
            rsh = lax.rsqrt(jnp.mean(oh * oh, axis=-1, keepdims=True) + RMS_EPS)
            on = oh * rsh
            sgg = _sig(gh)
            dgh = dh * (on * gout) * (sgg * (1.0 + gh * (1.0 - sgg)))
            dm = dh * (gh * sgg)
            s5_ref[3:4, sl] += _colsum(dm * on)
            do_s[:, sl] = _rms_bwd(dm * gout, on, rsh).astype(BF16)
            dp_ref[:, gsl] = dgh.astype(BF16)
            sb_ref[2:3, CONV_CH + h * HEAD_D:CONV_CH + (h + 1) * HEAD_D] += _colsum(dgh)

        lb, _ = _lower_bound(lbl_ref)
        lower, upper, same = lower_s[...], upper_s[...], same_s[...]
        pq = p_ref[:, o0:o0 + HGRN_W]
        pr = _hgrn_prep(pq, p_ref[:, o0 + HGRN_W:o0 + 2 * HGRN_W], lb, lower, same)
        qt_s[...] = pr["qt"].astype(BF16)
        kt_s[...] = pr["kt"].astype(BF16)
        kh_s[...] = pr["kh"].astype(BF16)
        v_s[...] = p_ref[:, o0 + 2 * HGRN_W:o0 + 3 * HGRN_W].astype(BF16)
        egl_s[...] = jnp.exp(pr["Gl"])
        tri = _tri()

        def chunk(it, c_):
            ci = nch - 1 - it
            r0 = pl.multiple_of(ci * CHUNK, CHUNK)
            rows = pl.ds(r0, CHUNK)
            for h in range(N_HEADS):
                ls = pl.ds(h * HEAD_D, HEAD_D)
                qc, kc, hc, vc = qt_s[rows, ls], kt_s[rows, ls], kh_s[rows, ls], v_s[rows, ls]
                dob = do_s[rows, ls]
                s0 = st_ref[ci, h]
                s0b = s0.astype(BF16)
                ds1 = dstate[h]
                ds1b = ds1.astype(BF16)
                egl = egl_s[pl.ds(r0, 1), ls]
                att = jnp.where(tri, _dot_nt(qc, kc), 0.0).astype(BF16)
                datt = jnp.where(tri, _dot_nt(dob, vc), 0.0).astype(BF16)
                dv_s[rows, ls] = _dot_tn(att, dob) + _dot_nt(hc, ds1b)
                dqt_s[rows, ls] = _dot(datt, kc) + _dot(dob, s0b)
                dkt_s[rows, ls] = _dot_tn(datt, qc)
                dkh_s[rows, ls] = _dot(vc, ds1b)
                dgl = egl * _colsum(ds1 * s0)
                dgl_s[rows, ls] = jnp.broadcast_to(dgl, (CHUNK, HEAD_D))
                dstate[h] = ds1 * egl + _dot_tn(dob, qc)
            return c_

        lax.fori_loop(0, nch, chunk, 0)
        dqt, dkt, dkh = dqt_s[...], dkt_s[...], dkh_s[...]
        dk = dkt * pr["enG"] + dkh * pr["eGlG"]
        khk = dkh * kh_s[...].astype(F32)
        dG = dqt * qt_s[...].astype(F32) - dkt * kt_s[...].astype(F32) - khk
        dlogf = _mm3(upper, dG) + _mm3(same, khk) + dgl_s[...]
        df = dlogf / pr["f"] - dk
        sf, sq = pr["sf"], pr["sq"]
        s5_ref[4:5, :] += _colsum(df * (1.0 - sf))
        dfl = df * (1.0 - lb) * (sf * (1.0 - sf))
        dq = (dqt * pr["eG"]) * (sq * (1.0 + pq * (1.0 - sq)))
        dvv = dv_s[...]
        dp_ref[:, o0:o0 + HGRN_W] = dq.astype(BF16)
        dp_ref[:, o0 + HGRN_W:o0 + 2 * HGRN_W] = dfl.astype(BF16)
        dp_ref[:, o0 + 2 * HGRN_W:o0 + 3 * HGRN_W] = dvv.astype(BF16)
        sb_ref[1:2, 0:HGRN_W] += _colsum(dq)
        sb_ref[1:2, HGRN_W:2 * HGRN_W] += _colsum(dfl)
        sb_ref[2:3, 0:HGRN_W] += _colsum(dvv)

        @pl.when(i == nt - 1)
        def _():
            for j in range(CONV_K):
                dw_ref[j:j + 1, :] = _colsum(dw8[j])
            copies = _xchg_copies(x_ins, x_outs, xssem, xrsem)
            for cp in copies:
                cp.wait_recv()
            for cp in copies:
                cp.wait_send()

    rev = lambda cols: pl.BlockSpec((tm, cols), lambda i: (nt - 1 - i, 0))
    halo = pl.BlockSpec((HALO, 2 * CONV_CH), lambda i: (jnp.maximum((nt - 1 - i) * hpt - 1, 0), 0))
    wide = lambda n: pltpu.VMEM((tm, HGRN_W), n)
    hbm = pl.BlockSpec(memory_space=pl.ANY)
    outs = pl.pallas_call(
        body, name="mixers_bwd", grid=(nt,),
        in_specs=[rev(IN_COLS), halo, rev(D_MODEL), rev(CONV_CH), rev(HGRN_W),
                  pl.BlockSpec((nch, N_HEADS, HEAD_D, HEAD_D), lambda i: (nt - 1 - i, 0, 0, 0)),
                  _full((HALO, CONV_CH))] + [_full((1, CONV_CH))] * 4 + [_full((2, HGRN_W))] + [hbm] * nx,
        out_specs=[rev(IN_COLS), _full((8, D_MODEL)), _full((8, CONV_CH)), _full((HALO, CONV_CH))] + [hbm] * nx,
        out_shape=[_big((T, IN_COLS), BF16), jax.ShapeDtypeStruct((8, D_MODEL), F32),
                   jax.ShapeDtypeStruct((8, CONV_CH), F32), jax.ShapeDtypeStruct((HALO, CONV_CH), F32)]
        + [_big(pb.shape, BF16) for pb in pairs_b],
        scratch_shapes=[pltpu.VMEM((tm + HALO + SUB, CONV_CH), F32), pltpu.VMEM((tm + HALO, CONV_CH), F32),
                        pltpu.VMEM((HALO, CONV_CH), F32), pltpu.VMEM((N_HEADS, HEAD_D, HEAD_D), F32),
                        wide(BF16), wide(BF16), wide(BF16), wide(BF16), wide(BF16),
                        wide(F32), wide(F32), wide(F32), wide(F32), wide(F32), wide(F32),
                        pltpu.VMEM((tm + 2 * SUB, CONV_CH), F32), pltpu.VMEM((HALO, SUB, CONV_CH), F32),
                        pltpu.VMEM((tm + SUB, CONV_CH), F32), pltpu.VMEM((tm, tm), BF16), pltpu.VMEM((tm, tm), BF16),
                        pltpu.VMEM((tm, tm), BF16), pltpu.VMEM((CONV_CH, CONV_CH), BF16)]
        + _xchg_sems(nx),
        compiler_params=_cp(("arbitrary",), 56),
    )(*_hbm(p, p, dcat, ys, o, states), wdw, *vecs, lbl, *[_hbm(pb) for pb in pairs_b])
    return outs[:4], outs[4:]


def _mix_in_bwd(dp, w_in_g, x, dx1, modr, g_pre, pairs_b):
    T = x.shape[0]
    tm = _tok_tile(T)
    nt = T // tm
    nb = IN_COLS // N_CHIPS
    nx = len(pairs_b)

    def body(*refs):
        dp_ref, w_ref, x_ref, dx1_ref, mod_ref, g_ref = refs[:6]
        x_ins = refs[6:6 + nx]
        gx_ref, st_ref = refs[6 + nx:8 + nx]
        x_outs = refs[8 + nx:8 + 2 * nx]
        xssem, xrsem = refs[8 + 2 * nx:]
        i = pl.program_id(0)

        @pl.when(i == 0)
        def _():
            for cp in _xchg_copies(x_ins, x_outs, xssem, xrsem):
                cp.start()
            st_ref[...] = jnp.zeros(st_ref.shape, F32)

        dh = None
        for j in range(N_CHIPS):
            part = _dot_nt(dp_ref[:, j * nb:(j + 1) * nb], w_ref[j])
            dh = part if dh is None else dh + part
        xv = x_ref[...]
        rs = lax.rsqrt(jnp.mean(xv * xv, axis=-1, keepdims=True) + RMS_EPS)
        xn = xv * rs
        st_ref[0:1, :] += _colsum(dh)
        st_ref[1:2, :] += _colsum(dh * (xn * g_ref[...]))
        dsc = dh * (1.0 + mod_ref[1:2, :])
        st_ref[2:3, :] += _colsum(dsc * xn)
        gx_ref[...] = dx1_ref[...] + _rms_bwd(dsc * g_ref[...], xn, rs)

        @pl.when(i == nt - 1)
        def _():
            copies = _xchg_copies(x_ins, x_outs, xssem, xrsem)
            for cp in copies:
                cp.wait_recv()
            for cp in copies:
                cp.wait_send()

    tile = pl.BlockSpec((tm, D_MODEL), lambda i: (i, 0))
    hbm = pl.BlockSpec(memory_space=pl.ANY)
    outs = pl.pallas_call(
        body, name="mix_in_bwd", grid=(nt,),
        in_specs=[pl.BlockSpec((tm, IN_COLS), lambda i: (i, 0)), _full((N_CHIPS, D_MODEL, nb)), tile, tile,
                  _full((6, D_MODEL)), _full((1, D_MODEL))] + [hbm] * nx,
        out_specs=[tile, _full((8, D_MODEL))] + [hbm] * nx,
        out_shape=[_big((T, D_MODEL), F32), jax.ShapeDtypeStruct((8, D_MODEL), F32)]
        + [_big(pb.shape, BF16) for pb in pairs_b],
        scratch_shapes=_xchg_sems(nx),
        compiler_params=_cp(("arbitrary",), 48),
    )(*_hbm(dp, w_in_g, x, dx1), modr, g_pre, *[_hbm(pb) for pb in pairs_b])
    return outs[:2], outs[2:]


def _weight_grad(a, b, a_blocked, b_blocked, name):
    T = a.shape[0]
    tt = min(GRAD_TILE, T)
    nt = T // tt
    ka = a.shape[1] // N_CHIPS if a_blocked else a.shape[1]
    nb = b.shape[1] // N_CHIPS if b_blocked else b.shape[1]

    def body(a_ref, b_ref, o_ref, ob_ref):
        t = pl.program_id(1)

        @pl.when(t == 0)
        def _():
            o_ref[...] = jnp.zeros(o_ref.shape, F32)

        for cols in _row_chains(nb):
            o_ref[0, :, cols] += _dot_tn(a_ref[...], b_ref[:, cols])

        @pl.when(t == nt - 1)
        def _():
            ob_ref[0] = o_ref[0].astype(BF16)

    blk = pl.BlockSpec((1, ka, nb), lambda j, t: (j, 0, 0))
    return pl.pallas_call(
        body, name=name, grid=(N_CHIPS, nt),
        in_specs=[pl.BlockSpec((tt, ka), (lambda j, t: (t, j)) if a_blocked else (lambda j, t: (t, 0))),
                  pl.BlockSpec((tt, nb), (lambda j, t: (t, j)) if b_blocked else (lambda j, t: (t, 0)))],
        out_specs=[blk, blk],
        out_shape=[_big((N_CHIPS, ka, nb), F32), _big((N_CHIPS, ka, nb), BF16)],
        compiler_params=_cp(("arbitrary", "arbitrary"), 48),
    )(*_hbm(a, b))


R_LOSS = 0
R_FFN = 8
R_OUT = 16
R_IN = 24
R_BIN = 32
R_512 = 40
R_DW = 48
N_STAT_ROWS = 80
MOD_ROWS = (R_IN + 0, R_IN + 1, R_OUT + 0, R_FFN + 0, R_FFN + 1, R_LOSS + 1)


def _small_update(gath, params):
    names = ["b_ada", "lb_logits", "g_pre_mix", "b_in", "b_dw", "gn_gain", "gn_bias", "g_hgrn_out", "g_post_mix",
             "g_pre_ffn", "g_post_ffn"]
    flat = []
    for n in names:
        flat += list(params[n])
    n_in = 1 + len(flat)

    def body(*refs):
        g_ref = refs[0]
        prm = {n: refs[1 + 3 * k:4 + 3 * k] for k, n in enumerate(names)}
        outs = refs[n_in:]
        loss_ref, dmod_ref, dwdw_ref = outs[0], outs[1], outs[2]
        res = {n: outs[3 + 4 * k:7 + 4 * k] for k, n in enumerate(names)}
        red = g_ref[0]
        for dev in range(1, N_DEV):
            red = red + g_ref[dev]
        loss_ref[...] = jnp.broadcast_to(
            (0.5 / D_MODEL) * jnp.sum(red[R_LOSS:R_LOSS + 1, :], axis=-1, keepdims=True), loss_ref.shape)
        for dev in range(N_DEV):
            for k, r in enumerate(MOD_ROWS):
                dmod_ref[dev:dev + 1, k * D_MODEL:(k + 1) * D_MODEL] = g_ref[dev, r:r + 1, :]
        dwdw_ref[...] = red[R_DW:R_DW + HALO, 0:CONV_CH]

        def finish(name, pieces):
            w_ref, m_ref, v_ref = prm[name]
            g_out, d_out, m_out, v_out = res[name]
            for rsl, lsl, g in pieces:
                d, m2, v2 = _adam_math(w_ref[rsl, lsl], g, m_ref[rsl, lsl], v_ref[rsl, lsl])
                g_out[rsl, lsl] = g
                d_out[rsl, lsl] = d
                m_out[rsl, lsl] = m2
                v_out[rsl, lsl] = v2

        one = slice(0, 1)
        row = lambda r: red[r:r + 1, :]
        half = lambda r: red[r:r + 1, 0:CONV_CH]
        finish("b_ada", [(one, slice(k * D_MODEL, (k + 1) * D_MODEL), row(r)) for k, r in enumerate(MOD_ROWS)])
        finish("b_in", [(one, slice(k * D_MODEL, (k + 1) * D_MODEL), row(R_BIN + k)) for k in range(3)])
        finish("g_pre_mix", [(one, slice(None), row(R_IN + 2))])
        finish("g_post_mix", [(one, slice(None), row(R_OUT + 1))])
        finish("g_pre_ffn", [(one, slice(None), row(R_FFN + 2))])
        finish("g_post_ffn", [(one, slice(None), row(R_LOSS + 2))])
        finish("b_dw", [(one, slice(None), half(R_512 + 0))])
        finish("gn_gain", [(one, slice(None), half(R_512 + 1))])
        finish("gn_bias", [(one, slice(None), half(R_512 + 2))])
        finish("g_hgrn_out", [(one, slice(None), half(R_512 + 3))])
        s0, s1 = _lower_bound(prm["lb_logits"][0])
        dlb = half(R_512 + 4)
        finish("lb_logits", [(slice(0, 1), slice(None), dlb * s0 * (1.0 - s0)),
                             (slice(1, 2), slice(None), -dlb * s0 * s1)])

    vm = pl.BlockSpec(memory_space=pltpu.VMEM)
    out_shape = [jax.ShapeDtypeStruct((8, 128), F32), jax.ShapeDtypeStruct((N_DEV, 6 * D_MODEL), F32),
                 jax.ShapeDtypeStruct((HALO, CONV_CH), F32)]
    for n in names:
        out_shape += [jax.ShapeDtypeStruct(params[n][0].shape, F32)] * 4
    outs = pl.pallas_call(
        body, name="small_update", out_shape=out_shape,
        in_specs=[vm] * n_in, out_specs=[vm] * len(out_shape),
        compiler_params=_cp(None, 32),
    )(gath, *flat)
    return outs[0], outs[1], outs[2], {n: outs[3 + 4 * k:7 + 4 * k] for k, n in enumerate(names)}


def _wdw_adam(w, g, m, v):
    def body(w_ref, g_ref, m_ref, v_ref, d_out, m_out, v_out):
        d, m2, v2 = _adam_math(w_ref[...], g_ref[...], m_ref[...], v_ref[...])
        d_out[...] = d
        m_out[...] = m2
        v_out[...] = v2

    vm = pl.BlockSpec(memory_space=pltpu.VMEM)
    return pl.pallas_call(
        body, name="wdw_adam", out_shape=[jax.ShapeDtypeStruct(w.shape, F32)] * 3,
        in_specs=[vm] * 4, out_specs=[vm] * 3, compiler_params=_cp(None, 16),
    )(w, g, m, v)


def kernel(x, c, w_ada, b_ada, lb_logits, g_pre_mix, w_in, b_in, w_dw, b_dw, gn_gain, gn_bias, g_hgrn_out, w_out, g_post_mix, g_pre_ffn, w_up, w_down, g_post_ffn, loss_target, m_w_ada, m_b_ada, m_lb_logits, m_g_pre_mix, m_w_in, m_b_in, m_w_dw, m_b_dw, m_gn_gain, m_gn_bias, m_g_hgrn_out, m_w_out, m_g_post_mix, m_g_pre_ffn, m_w_up, m_w_down, m_g_post_ffn, v_w_ada, v_b_ada, v_lb_logits, v_g_pre_mix, v_w_in, v_b_in, v_w_dw, v_b_dw, v_gn_gain, v_gn_bias, v_g_hgrn_out, v_w_out, v_g_post_mix, v_g_pre_ffn, v_w_up, v_w_down, v_g_post_ffn):
    ax, ay, ac = lax.axis_index("x"), lax.axis_index("y"), lax.axis_index("c")
    chip = 2 * ax + ay
    T = x.shape[1]
    xs, tgt = x[0], loss_target[0]
    ada_cols = w_ada.shape[2]

    b_sh = lax.dynamic_slice_in_dim(b_ada, chip * ada_cols, ada_cols, axis=1)
    wdw_pad = jnp.pad(w_dw[0], ((0, HALO - CONV_K), (0, 0)))
    chip1 = jnp.reshape(chip, (1,)).astype(jnp.int32)
    place = jnp.stack([ac, chip]).astype(jnp.int32)
    _, c8, modg, wdwg = _ada_exchange(c, w_ada[0], b_sh, wdw_pad)
    modr = modg.reshape(6, D_MODEL)
    wdw_all = jnp.transpose(wdwg, (1, 0, 2)).reshape(HALO, CONV_CH)
    bufs = {t: _cast_own(chip1, w[0], "cast_" + t)
            for w, t in ((w_in, "w_in"), (w_out, "w_out"), (w_up, "w_up"), (w_down, "w_down"))}
    vec = (b_dw, gn_gain, gn_bias, g_hgrn_out)

    p, h1, w_in_g, w_out_g = _mix_in_fwd(chip1, xs, modr, g_pre_mix, b_in.reshape(N_CHIPS, 1, IN_COLS // N_CHIPS),
                                          bufs["w_in"], bufs["w_out"])
    (cat, ys, o, states), (w_up_g, w_down_g) = _mixers_fwd(p, wdw_all, vec, lb_logits, [bufs["w_up"], bufs["w_down"]])
    w_out_f = w_out_g.reshape(D_MODEL, D_MODEL)
    w_down_f = w_down_g.reshape(D_FF, D_MODEL)
    y, x1, h2 = _mix_out_fwd(cat, w_out_f, xs, modr, g_post_mix, g_pre_ffn)
    r, dy2, dx2, st_loss = _ffn_fwd(h2, w_up_g, w_down_f, x1, tgt, modr, g_post_ffn)

    def pair_sums(grads, got, tags):
        return [_pair_sum(place, g, o_, "pair_sum_" + t) for (g, _), o_, t in zip(grads, got, tags)]

    da, dx1, st_ffn = _ffn_bwd(dy2, r, x1, dx2, w_up_g, w_down_f, modr, g_pre_ffn)
    g_up = _weight_grad(h2, da, False, True, "grad_w_up")
    g_down = _weight_grad(r, dy2, True, False, "grad_w_down")
    (dy, dcat, st_out), got_ud = _mix_out_bwd(dx1, y, w_out_f, modr, g_post_mix, [g_up[1], g_down[1]])
    g_out = _weight_grad(cat, dy, True, False, "grad_w_out")
    got_o = _pair_swap([g_out[1]], "pair_swap_w_out")
    early = pair_sums([g_out, g_up, g_down], list(got_o) + list(got_ud), ["w_out", "w_up", "w_down"])
    (dp, st_bin, st_512, dwdw), got_early = _mixers_bwd(p, dcat, ys, o, states, wdw_all, vec, lb_logits,
                                                        [pb for _, pb in early])
    g_in = _weight_grad(h1, dp, False, True, "grad_w_in")
    late = pair_sums([g_in], _pair_swap([g_in[1]], "pair_swap_w_in"), ["w_in"])
    (grad_x, st_in), got_late = _mix_in_bwd(dp, w_in_g, xs, dx1, modr, g_pre_mix, [late[0][1]])
    fulls = [_chip_sum(place, pf, gb, "chip_sum_" + t)
             for (pf, _), gb, t in zip(late + early, list(got_late) + list(got_early), ["w_in", "w_out", "w_up", "w_down"])]

    pad_lanes = lambda s: jnp.pad(s, ((0, 0), (0, D_MODEL - s.shape[1])))
    stats = jnp.concatenate([st_loss, st_ffn, st_out, st_in, st_bin, pad_lanes(st_512), pad_lanes(dwdw)], axis=0)
    (g_w_in, g_w_out, g_w_up, g_w_down), gath = _final_exchange(fulls, stats)
    small = {"b_ada": (b_ada, m_b_ada, v_b_ada), "lb_logits": (lb_logits, m_lb_logits, v_lb_logits),
             "g_pre_mix": (g_pre_mix, m_g_pre_mix, v_g_pre_mix), "b_in": (b_in, m_b_in, v_b_in),
             "b_dw": (b_dw, m_b_dw, v_b_dw), "gn_gain": (gn_gain, m_gn_gain, v_gn_gain),
             "gn_bias": (gn_bias, m_gn_bias, v_gn_bias), "g_hgrn_out": (g_hgrn_out, m_g_hgrn_out, v_g_hgrn_out),
             "g_post_mix": (g_post_mix, m_g_post_mix, v_g_post_mix), "g_pre_ffn": (g_pre_ffn, m_g_pre_ffn, v_g_pre_ffn),
             "g_post_ffn": (g_post_ffn, m_g_post_ffn, v_g_post_ffn)}
    loss_t, dmod_all, dwdw_sum, sres = _small_update(gath, small)
    loss = loss_t[0, 0]

    res = dict(sres)
    dmod_sh = lax.dynamic_slice_in_dim(dmod_all, chip * ada_cols, ada_cols, axis=1)
    res["w_ada"] = [t[None] for t in _ada_grad_adam(jnp.transpose(c8), dmod_sh, w_ada[0], m_w_ada[0], v_w_ada[0])]
    g_wdw = lax.dynamic_slice_in_dim(dwdw_sum, chip * HEAD_D, HEAD_D, axis=1)[:CONV_K][None]
    res["w_dw"] = [g_wdw] + list(_wdw_adam(w_dw, g_wdw, m_w_dw, v_w_dw))
    for name, g, w, m, v in (("w_in", g_w_in, w_in, m_w_in, v_w_in), ("w_out", g_w_out, w_out, m_w_out, v_w_out),
                             ("w_up", g_w_up, w_up, m_w_up, v_w_up), ("w_down", g_w_down, w_down, m_w_down, v_w_down)):
        d, m2, v2 = _adam_big(w[0], g, m[0], v[0], "adam_" + name)
        res[name] = [g[None], d[None], m2[None], v2[None]]

    order = ["w_ada", "b_ada", "lb_logits", "g_pre_mix", "w_in", "b_in", "w_dw", "b_dw", "gn_gain", "gn_bias",
             "g_hgrn_out", "w_out", "g_post_mix", "g_pre_ffn", "w_up", "w_down", "g_post_ffn"]
    out = [loss, grad_x[None]]
    for k in range(4):
        out += [res[n][k] for n in order]
    return tuple(out)
```

```python
import jax
import jax.numpy as jnp
from jax import lax
from jax.experimental import pallas as pl
from jax.experimental.pallas import tpu as pltpu

F32, BF16 = jnp.float32, jnp.bfloat16
D_MODEL = 1024
CONV_CH = 512
HGRN_W = 512
N_HEADS = 4
HEAD_D = 128
CONV_K = 31
GN_GROUP = 64
GN_SHIFT = 6
IN_COLS = 3072
D_FF = 4096
CHUNK = 64
CHUNK_SHIFT = 6
N_CHIPS = 4
N_DEV = 8
RMS_EPS = 1e-6
GN_EPS = 1e-5
ADAM_LR, ADAM_B1, ADAM_B2, ADAM_EPS, ADAM_WD, ADAM_STEP = 0.001, 0.9, 0.999, 1e-08, 0.01, 10
TOK_TILE = 512
MIXIN_TILE = 1024
MIXB_TILE = 256
FFN_TILE = 1024
FFN_BLOCK = 512
GRAD_TILE = 2048
HALO = 32
SUB = 8
LANE = 128
CONV_ROWS = 128
MIB = 1 << 20
MESH = pl.DeviceIdType.MESH
OTHER_CHIPS = ((0, 1), (1, 0), (1, 1))


def _cp(sem=None, vmem_mib=48):
    return pltpu.CompilerParams(dimension_semantics=sem, vmem_limit_bytes=vmem_mib * MIB)


def _dot(a, b):
    return jnp.dot(a, b, preferred_element_type=F32)


def _dot_nt(a, b):
    return lax.dot_general(a, b, (((1,), (1,)), ((), ())), preferred_element_type=F32)


def _dot_tn(a, b):
    return lax.dot_general(a, b, (((0,), (0,)), ((), ())), preferred_element_type=F32)


def _sig(v):
    return 0.5 * jnp.tanh(0.5 * v) + 0.5


def _colsum(v):
    return jnp.sum(v, axis=0, keepdims=True)


def _flip(v, b):
    return 1 - v if b else v


def _rcopy(src, dst, ssem, rsem, dev):
    return pltpu.make_async_remote_copy(src_ref=src, dst_ref=dst, send_sem=ssem, recv_sem=rsem,
                                        device_id=dev, device_id_type=MESH)


def _place():
    return lax.axis_index("x"), lax.axis_index("y"), lax.axis_index("c")


def _full(shape):
    return pl.BlockSpec(shape, lambda *_: (0,) * len(shape))


def _big(shape, dtype):
    return pltpu.HBM(shape, dtype)


def _hbm(*arrays):
    out = [pltpu.with_memory_space_constraint(a, pltpu.HBM) for a in arrays]
    return out[0] if len(out) == 1 else out


def _split2(v):
    hi = v.astype(BF16)
    lo = (v - hi.astype(F32)).astype(BF16)
    return hi, lo


def _split3(v):
    h1 = v.astype(BF16)
    r1 = v - h1.astype(F32)
    h2 = r1.astype(BF16)
    h3 = (r1 - h2.astype(F32)).astype(BF16)
    return h1, h2, h3


def _mm3s(mat, parts):
    h1, h2, h3 = parts
    return _dot(mat, h1) + _dot(mat, h2) + _dot(mat, h3)


def _mm3(mat, v):
    return _mm3s(mat, _split3(v))


def _gn_matrix():
    r = lax.broadcasted_iota(jnp.int32, (CONV_CH, CONV_CH), 0) >> GN_SHIFT
    c = lax.broadcasted_iota(jnp.int32, (CONV_CH, CONV_CH), 1) >> GN_SHIFT
    return jnp.where(r == c, 1.0 / GN_GROUP, 0.0).astype(BF16)


def _gmean(v, gmat):
    hi, lo = _split2(v)
    return _dot(hi, gmat) + _dot(lo, gmat)


def _chunk_masks(tm):
    r = lax.broadcasted_iota(jnp.int32, (tm, tm), 0)
    c = lax.broadcasted_iota(jnp.int32, (tm, tm), 1)
    same = (r >> CHUNK_SHIFT) == (c >> CHUNK_SHIFT)
    one = lambda m: jnp.where(m, 1.0, 0.0).astype(BF16)
    return one(same & (c <= r)), one(same & (c >= r)), one(same)


def _tri():
    return lax.broadcasted_iota(jnp.int32, (CHUNK, CHUNK), 0) >= lax.broadcasted_iota(jnp.int32, (CHUNK, CHUNK), 1)


def _lower_bound(lbl_ref):
    l0, l1 = lbl_ref[0:1, :], lbl_ref[1:2, :]
    mx = jnp.maximum(l0, l1)
    e0, e1 = jnp.exp(l0 - mx), jnp.exp(l1 - mx)
    return e0 / (e0 + e1), e1 / (e0 + e1)


CONV_FWD_TAPS = tuple((j, HALO - (CONV_K - 1) + j) for j in range(CONV_K))
CONV_BWD_TAPS = tuple((j, (CONV_K - 1) - j) for j in range(CONV_K))


def _tap_conv(src_ref, w_ref, row0, taps, lanes):
    acc = None
    for b in range(SUB):
        pb = None
        for j, off in taps:
            if off % SUB == b:
                lo = row0 + off - b
                term = w_ref[j:j + 1, lanes] * src_ref[lo:lo + CONV_ROWS + SUB, lanes]
                pb = term if pb is None else pb + term
        if pb is not None:
            sh = pb[b:b + CONV_ROWS, :]
            acc = sh if acc is None else acc + sh
    return acc


def _hgrn_prep(pq, pf, lb, lower, same):
    sq = _sig(pq)
    qf = pq * sq
    sf = _sig(pf)
    f = lb + (1.0 - lb) * sf
    logf = jnp.log(f)
    k = 1.0 - f
    parts = _split3(logf)
    G = _mm3s(lower, parts)
    Gl = _mm3s(same, parts)
    eG, enG, eGlG = jnp.exp(G), jnp.exp(-G), jnp.exp(Gl - G)
    return dict(sq=sq, sf=sf, f=f, Gl=Gl, eG=eG, enG=enG, eGlG=eGlG, qt=qf * eG, kt=k * enG, kh=k * eGlG)


def _ada_exchange(c_row, w_ada, b_sh, wdw_pad):
    ncol = w_ada.shape[1]

    def body(c_ref, w_ref, b_ref, wdw_ref, call_ref, c8_ref, modg_ref, wdwg_ref, rows_s, sa, ra, sw, rw, sm, rm):
        x, y, c = _place()
        me = 4 * x + 2 * y + c
        chip = 2 * x + y
        cv = c_ref[...]
        call_ref[me] = cv * _sig(cv)
        wdwg_ref[chip] = wdw_ref[...]
        sends = []
        for m in range(1, N_DEV):
            peer = (_flip(x, m >> 2), _flip(y, (m >> 1) & 1), _flip(c, m & 1))
            cp = _rcopy(call_ref.at[me], call_ref.at[me], sa.at[m - 1], ra.at[m - 1], peer)
            cp.start()
            sends.append(cp)
        for k, (fx, fy) in enumerate(OTHER_CHIPS):
            peer = (_flip(x, fx), _flip(y, fy), c)
            cp = _rcopy(wdwg_ref.at[chip], wdwg_ref.at[chip], sw.at[k], rw.at[k], peer)
            cp.start()
            sends.append(cp)
        for m in range(1, N_DEV):
            peer = (_flip(x, m >> 2), _flip(y, (m >> 1) & 1), _flip(c, m & 1))
            pid = 4 * peer[0] + 2 * peer[1] + peer[2]
            _rcopy(call_ref.at[pid], call_ref.at[pid], sa.at[m - 1], ra.at[m - 1], peer).wait_recv()
        for b in range(N_DEV):
            c8_ref[b:b + 1, :] = call_ref[b]
        mod_all = _dot(c8_ref[...].astype(BF16), w_ref[...].astype(BF16)) + b_ref[...]
        for b in range(N_DEV):
            rows_s[b] = mod_all[b:b + 1, :]
        modg_ref[chip] = rows_s[me]
        for k, (fx, fy) in enumerate(OTHER_CHIPS):
            peer = (_flip(x, fx), _flip(y, fy), c)
            pid = 4 * peer[0] + 2 * peer[1] + peer[2]
            cp = _rcopy(rows_s.at[pid], modg_ref.at[chip], sm.at[k], rm.at[k], peer)
            cp.start()
            sends.append(cp)
        for k, (fx, fy) in enumerate(OTHER_CHIPS):
            peer = (_flip(x, fx), _flip(y, fy), c)
            pchip = 2 * peer[0] + peer[1]
            _rcopy(rows_s.at[0], modg_ref.at[pchip], sm.at[k], rm.at[k], peer).wait_recv()
            _rcopy(wdwg_ref.at[pchip], wdwg_ref.at[pchip], sw.at[k], rw.at[k], peer).wait_recv()
        for cp in sends:
            cp.wait_send()

    vm = pl.BlockSpec(memory_space=pltpu.VMEM)
    return pl.pallas_call(
        body, name="ada_exchange",
        out_shape=[jax.ShapeDtypeStruct((N_DEV, 1, D_MODEL), F32), jax.ShapeDtypeStruct((N_DEV, D_MODEL), F32),
                   jax.ShapeDtypeStruct((N_CHIPS, 1, ncol), F32), jax.ShapeDtypeStruct((N_CHIPS, HALO, HEAD_D), F32)],
        in_specs=[vm] * 4, out_specs=[vm] * 4,
        scratch_shapes=[pltpu.VMEM((N_DEV, 1, ncol), F32),
                        pltpu.SemaphoreType.DMA((N_DEV - 1,)), pltpu.SemaphoreType.DMA((N_DEV - 1,)),
                        pltpu.SemaphoreType.DMA((3,)), pltpu.SemaphoreType.DMA((3,)),
                        pltpu.SemaphoreType.DMA((3,)), pltpu.SemaphoreType.DMA((3,))],
        compiler_params=_cp(None, 32),
    )(c_row, w_ada, b_sh, wdw_pad)


def _cast_own(chip1, shard, name):
    rows, cols = shard.shape
    tr = _row_tile(rows)

    def body(ch_ref, s_ref, o_ref):
        o_ref[0] = s_ref[...].astype(BF16)

    gs = pltpu.PrefetchScalarGridSpec(
        num_scalar_prefetch=1, grid=(rows // tr,),
        in_specs=[pl.BlockSpec((tr, cols), lambda i, ch: (i, 0))],
        out_specs=pl.BlockSpec((1, tr, cols), lambda i, ch: (ch[0], i, 0)))
    return pl.pallas_call(
        body, name=name, grid_spec=gs, out_shape=_big((N_CHIPS, rows, cols), BF16),
        compiler_params=_cp(("arbitrary",), 32),
    )(chip1, _hbm(shard))


def _slab(buf, ch, core):
    hs = buf.shape[1] // 2
    return buf.at[ch, pl.ds(core * hs, hs), :]


def _gather_start(bufs, ssem, rsem, relations=(0, 1, 2)):
    x, y, c = _place()
    chip = 2 * x + y
    for k in relations:
        fx, fy = OTHER_CHIPS[k]
        peer = (_flip(x, fx), _flip(y, fy), c)
        for t, buf in enumerate(bufs):
            _rcopy(_slab(buf, chip, c), _slab(buf, chip, c), ssem.at[t * 3 + k], rsem.at[t * 3 + k], peer).start()


def _gather_finish(bufs, ssem, rsem):
    nt = len(bufs)
    x, y, c = _place()
    chip = 2 * x + y
    sibling = (x, y, 1 - c)
    for k, (fx, fy) in enumerate(OTHER_CHIPS):
        peer = (_flip(x, fx), _flip(y, fy), c)
        pchip = 2 * peer[0] + peer[1]
        for t, buf in enumerate(bufs):
            _rcopy(_slab(buf, pchip, c), _slab(buf, pchip, c), ssem.at[t * 3 + k], rsem.at[t * 3 + k], peer).wait_recv()
            _rcopy(_slab(buf, pchip, c), _slab(buf, pchip, c), ssem.at[3 * nt + t * 3 + k],
                   rsem.at[3 * nt + t * 3 + k], sibling).start()
    for k, (fx, fy) in enumerate(OTHER_CHIPS):
        peer = (_flip(x, fx), _flip(y, fy), c)
        pchip = 2 * peer[0] + peer[1]
        for t, buf in enumerate(bufs):
            _rcopy(_slab(buf, pchip, 1 - c), _slab(buf, pchip, 1 - c), ssem.at[3 * nt + t * 3 + k],
                   rsem.at[3 * nt + t * 3 + k], sibling).wait_recv()
            _rcopy(_slab(buf, chip, c), _slab(buf, chip, c), ssem.at[t * 3 + k], rsem.at[t * 3 + k], peer).wait_send()
            _rcopy(_slab(buf, pchip, c), _slab(buf, pchip, c), ssem.at[3 * nt + t * 3 + k],
                   rsem.at[3 * nt + t * 3 + k], sibling).wait_send()


def _gather_arrive(bufs, k, ssem, rsem):
    nt = len(bufs)
    x, y, c = _place()
    fx, fy = OTHER_CHIPS[k]
    peer = (_flip(x, fx), _flip(y, fy), c)
    pchip = 2 * peer[0] + peer[1]
    for t, buf in enumerate(bufs):
        _rcopy(_slab(buf, pchip, c), _slab(buf, pchip, c), ssem.at[t * 3 + k], rsem.at[t * 3 + k], peer).wait_recv()
        _rcopy(_slab(buf, pchip, c), _slab(buf, pchip, c), ssem.at[3 * nt + t * 3 + k],
               rsem.at[3 * nt + t * 3 + k], (x, y, 1 - c)).start()
    for t, buf in enumerate(bufs):
        _rcopy(_slab(buf, pchip, 1 - c), _slab(buf, pchip, 1 - c), ssem.at[3 * nt + t * 3 + k],
               rsem.at[3 * nt + t * 3 + k], (x, y, 1 - c)).wait_recv()


def _gather_sends_done(bufs, ssem, rsem):
    nt = len(bufs)
    x, y, c = _place()
    chip = 2 * x + y
    for k, (fx, fy) in enumerate(OTHER_CHIPS):
        peer = (_flip(x, fx), _flip(y, fy), c)
        pchip = 2 * peer[0] + peer[1]
        for t, buf in enumerate(bufs):
            _rcopy(_slab(buf, chip, c), _slab(buf, chip, c), ssem.at[t * 3 + k], rsem.at[t * 3 + k], peer).wait_send()
            _rcopy(_slab(buf, pchip, c), _slab(buf, pchip, c), ssem.at[3 * nt + t * 3 + k],
                   rsem.at[3 * nt + t * 3 + k], (x, y, 1 - c)).wait_send()


def _gather_sems(nt):
    return [pltpu.SemaphoreType.DMA((6 * nt,)), pltpu.SemaphoreType.DMA((6 * nt,))]


def _pair_copies(ins, outs, ssem, rsem):
    x, y, c = _place()
    copies = []
    for t in range(len(ins)):
        hs = ins[t].shape[1] // 2
        copies.append(_rcopy(ins[t].at[:, pl.ds((1 - c) * hs, hs), :], outs[t], ssem.at[t], rsem.at[t], (x, y, 1 - c)))
    return copies


def _pair_shapes(grads):
    return [_big((g.shape[0], g.shape[1] // 2, g.shape[2]), g.dtype) for g in grads]


def _pair_sems(nt):
    return [pltpu.SemaphoreType.DMA((nt,)), pltpu.SemaphoreType.DMA((nt,))]


def _pair_swap(grads, name):
    nt = len(grads)
    hbm = pl.BlockSpec(memory_space=pl.ANY)

    def body(*refs):
        copies = _pair_copies(refs[:nt], refs[nt:2 * nt], refs[2 * nt], refs[2 * nt + 1])
        for cp in copies:
            cp.start()
        for cp in copies:
            cp.wait_recv()
        for cp in copies:
            cp.wait_send()

    return pl.pallas_call(
        body, name=name, out_shape=_pair_shapes(grads), in_specs=[hbm] * nt, out_specs=[hbm] * nt,
        scratch_shapes=_pair_sems(nt),
    )(*[_hbm(g) for g in grads])


def _xchg_copies(ins, outs, ssem, rsem):
    x, y, c = _place()
    copies = []
    for k, (fx, fy) in enumerate(OTHER_CHIPS):
        peer = (_flip(x, fx), _flip(y, fy), c)
        for t in range(len(ins)):
            copies.append(_rcopy(ins[t].at[k], outs[t].at[k], ssem.at[t * 3 + k], rsem.at[t * 3 + k], peer))
    return copies


def _xchg_sems(nt):
    return [pltpu.SemaphoreType.DMA((3 * nt,)), pltpu.SemaphoreType.DMA((3 * nt,))]


def _final_exchange(fulls, stats):
    nt = len(fulls)
    rows, cols = stats.shape
    hbm = pl.BlockSpec(memory_space=pl.ANY)
    vm = pl.BlockSpec(memory_space=pltpu.VMEM)

    def body(*refs):
        ins, s_ref = refs[:nt], refs[nt]
        outs, g_ref = refs[nt + 1:2 * nt + 1], refs[2 * nt + 1]
        hssem, hrsem, ssem, rsem = refs[2 * nt + 2:]
        x, y, c = _place()
        me, sibling = (x, y, c), (x, y, 1 - c)
        halves = []
        for t in range(nt):
            hs = ins[t].shape[0] // 2
            mine = pl.ds(c * hs, hs)
            cp = _rcopy(ins[t].at[mine, :], outs[t].at[mine, :], hssem.at[t], hrsem.at[t], sibling)
            cp.start()
            halves.append(cp)

        chips = [(_flip(x, fx), _flip(y, fy)) for fx, fy in OTHER_CHIPS]

        def blk(px, py, pc):
            return g_ref.at[4 * px + 2 * py + pc]

        def copy(k, block, to, src=None):
            return _rcopy(blk(*block) if src is None else src, blk(*block), ssem.at[k], rsem.at[k], to)

        g_ref[4 * x + 2 * y + c] = s_ref[...]
        first = [copy(0, me, sibling, src=s_ref)]
        first += [copy(1 + j, me, (*chip, c), src=s_ref) for j, chip in enumerate(chips)]
        for cp in first:
            cp.start()
        passed = [copy(4 + j, (*chip, c), sibling) for j, chip in enumerate(chips)]
        for j, chip in enumerate(chips):
            copy(1 + j, (*chip, c), me).wait_recv()
            passed[j].start()
        copy(0, sibling, me).wait_recv()
        for j, chip in enumerate(chips):
            copy(4 + j, (*chip, 1 - c), me).wait_recv()
        for t in range(nt):
            hs = ins[t].shape[0] // 2
            other = pl.ds((1 - c) * hs, hs)
            _rcopy(ins[t].at[other, :], outs[t].at[other, :], hssem.at[t], hrsem.at[t], sibling).wait_recv()
        for cp in first + passed + halves:
            cp.wait_send()

    outs = pl.pallas_call(
        body, name="final_exchange",
        out_shape=[_big(f.shape, F32) for f in fulls] + [jax.ShapeDtypeStruct((N_DEV, rows, cols), F32)],
        in_specs=[hbm] * nt + [vm], out_specs=[hbm] * nt + [vm],
        input_output_aliases={t: t for t in range(nt)},
        scratch_shapes=[pltpu.SemaphoreType.DMA((nt,)), pltpu.SemaphoreType.DMA((nt,)),
                        pltpu.SemaphoreType.DMA((7,)), pltpu.SemaphoreType.DMA((7,))],
        compiler_params=_cp(None, 32),
    )(*[_hbm(f) for f in fulls], stats)
    return outs[:nt], outs[nt]


def _row_tile(rows):
    return min(rows, 512)


def _pair_sum(place, grad, got, name):
    nb, hs, cols = got.shape
    tr = _row_tile(hs)
    nr = hs // tr

    def body(pl_ref, g_ref, o_ref, pf_ref, pb_ref):
        j = pl.program_id(1)
        s = g_ref[0] + o_ref[0].astype(F32)

        @pl.when(j == 0)
        def _():
            pf_ref[...] = s

        @pl.when(j > 0)
        def _():
            pb_ref[0] = s.astype(BF16)

    gs = pltpu.PrefetchScalarGridSpec(
        num_scalar_prefetch=1, grid=(nr, nb),
        in_specs=[pl.BlockSpec((1, tr, cols), lambda i, j, p: (p[1] ^ j, p[0] * nr + i, 0)),
                  pl.BlockSpec((1, tr, cols), lambda i, j, p: (p[1] ^ j, i, 0))],
        out_specs=[pl.BlockSpec((tr, cols), lambda i, j, p: (i, 0)),
                   pl.BlockSpec((1, tr, cols), lambda i, j, p: (jnp.maximum(j - 1, 0), i, 0))])
    return pl.pallas_call(
        body, name=name, grid_spec=gs,
        out_shape=[_big((hs, cols), F32), _big((nb - 1, hs, cols), BF16)],
        compiler_params=_cp(("arbitrary", "arbitrary"), 32),
    )(place, *_hbm(grad, got))


def _chip_sum(place, pair_f, got_b, name):
    nb, hs, cols = got_b.shape
    tr = _row_tile(hs)
    nr = hs // tr

    def body(pl_ref, pf_ref, gb_ref, o_ref):
        acc = pf_ref[...]
        for k in range(nb):
            acc = acc + gb_ref[k].astype(F32)
        o_ref[...] = acc

    gs = pltpu.PrefetchScalarGridSpec(
        num_scalar_prefetch=1, grid=(nr,),
        in_specs=[pl.BlockSpec((tr, cols), lambda i, p: (i, 0)),
                  pl.BlockSpec((nb, tr, cols), lambda i, p: (0, i, 0))],
        out_specs=pl.BlockSpec((tr, cols), lambda i, p: (p[0] * nr + i, 0)))
    return pl.pallas_call(
        body, name=name, grid_spec=gs,
        out_shape=_big((2 * hs, cols), F32),
        compiler_params=_cp(("arbitrary",), 32),
    )(place, *_hbm(pair_f, got_b))


def _adam_math(w, g, m, v):
    m2 = ADAM_B1 * m + (1.0 - ADAM_B1) * g
    v2 = ADAM_B2 * v + (1.0 - ADAM_B2) * (g * g)
    m_hat = m2 / (1.0 - ADAM_B1 ** ADAM_STEP)
    v_hat = v2 / (1.0 - ADAM_B2 ** ADAM_STEP)
    delta = -ADAM_LR * (m_hat / (jnp.sqrt(v_hat) + ADAM_EPS) + ADAM_WD * w)
    return delta, m2, v2


def _adam_big(w, g, m, v, name):
    rows, cols = w.shape
    tr = _row_tile(rows)

    def body(w_ref, g_ref, m_ref, v_ref, d_out, m_out, v_out):
        d, m2, v2 = _adam_math(w_ref[...], g_ref[...], m_ref[...], v_ref[...])
        d_out[...] = d
        m_out[...] = m2
        v_out[...] = v2

    spec = pl.BlockSpec((tr, cols), lambda i: (i, 0))
    return pl.pallas_call(
        body, name=name, grid=(rows // tr,), in_specs=[spec] * 4, out_specs=[spec] * 3,
        out_shape=[_big(w.shape, F32)] * 3,
        compiler_params=_cp(("arbitrary",), 48),
    )(*_hbm(w, g, m, v))


def _ada_grad_adam(c8t, dmod_sh, w, m, v):
    rows, cols = w.shape
    tr = _row_tile(rows) // 2

    def body(ct_ref, dm_ref, w_ref, m_ref, v_ref, g_out, d_out, m_out, v_out):
        g = None
        for b in range(N_DEV):
            term = ct_ref[:, b:b + 1] * dm_ref[b:b + 1, :]
            g = term if g is None else g + term
        d, m2, v2 = _adam_math(w_ref[...], g, m_ref[...], v_ref[...])
        g_out[...] = g
        d_out[...] = d
        m_out[...] = m2
        v_out[...] = v2

    spec = pl.BlockSpec((tr, cols), lambda i: (i, 0))
    return pl.pallas_call(
        body, name="ada_grad_adam", grid=(rows // tr,),
        in_specs=[pl.BlockSpec((tr, N_DEV), lambda i: (i, 0)), _full((N_DEV, cols)), spec, spec, spec],
        out_specs=[spec] * 4, out_shape=[_big(w.shape, F32)] * 4,
        compiler_params=_cp(("arbitrary",), 32),
    )(c8t, dmod_sh, *_hbm(w, m, v))


def _tok_tile(t):
    return min(TOK_TILE, t)


def _mix_in_fwd(chip1, x, modr, g_pre, b_in4, w_in_buf, w_out_buf):
    T = x.shape[0]
    tm = min(MIXIN_TILE, T)
    nt = T // tm
    nb = IN_COLS // N_CHIPS

    def body(ch_ref, x_ref, mod_ref, g_ref, b_ref, win_in, wout_in, p_ref, h_ref, win_ref, wout_ref,
             h_all, wblk, lsem, is_sem, ir_sem, os_sem, or_sem):
        k, i = pl.program_id(0), pl.program_id(1)
        chip = ch_ref[0]

        def load_block(blk):
            cp = pltpu.make_async_copy(win_ref.at[blk], wblk, lsem)
            cp.start()
            cp.wait()

        @pl.when((k == 0) & (i == 0))
        def _():
            _gather_start([win_ref], is_sem, ir_sem, relations=(0, 1))
            load_block(chip)

        for r in range(N_CHIPS - 1):
            @pl.when((k == r + 1) & (i == 0))
            def _(r=r):
                _gather_arrive([win_ref], r, is_sem, ir_sem)
                if r == 0:
                    _gather_start([win_ref], is_sem, ir_sem, relations=(2,))
                if r == 1:
                    _gather_start([wout_ref], os_sem, or_sem)
                load_block(chip ^ (r + 1))

        rows = pl.ds(pl.multiple_of(i * tm, tm), tm)

        @pl.when(k == 0)
        def _():
            xv = x_ref[...]
            rstd = lax.rsqrt(jnp.mean(xv * xv, axis=-1, keepdims=True) + RMS_EPS)
            h = (xv * rstd) * g_ref[...] * (1.0 + mod_ref[1:2, :]) + mod_ref[0:1, :]
            hb = h.astype(BF16)
            h_ref[...] = hb
            h_all[rows, :] = hb

        p_ref[...] = _dot(h_all[rows, :], wblk[...]) + b_ref[chip ^ k]

        @pl.when((k == N_CHIPS - 1) & (i == nt - 1))
        def _():
            _gather_sends_done([win_ref], is_sem, ir_sem)
            _gather_finish([wout_ref], os_sem, or_sem)

    hbm = pl.BlockSpec(memory_space=pl.ANY)
    first_pass = lambda k, i, ch: (jnp.where(k == 0, i, nt - 1), 0)
    gs = pltpu.PrefetchScalarGridSpec(
        num_scalar_prefetch=1, grid=(N_CHIPS, nt),
        in_specs=[pl.BlockSpec((tm, D_MODEL), first_pass), pl.BlockSpec((6, D_MODEL), lambda k, i, ch: (0, 0)),
                  pl.BlockSpec((1, D_MODEL), lambda k, i, ch: (0, 0)),
                  pl.BlockSpec((N_CHIPS, 1, nb), lambda k, i, ch: (0, 0, 0)), hbm, hbm],
        out_specs=[pl.BlockSpec((tm, nb), lambda k, i, ch: (i, ch[0] ^ k)), pl.BlockSpec((tm, D_MODEL), first_pass),
                   hbm, hbm],
        scratch_shapes=[pltpu.VMEM((T, D_MODEL), BF16), pltpu.VMEM((D_MODEL, nb), BF16), pltpu.SemaphoreType.DMA]
        + _gather_sems(1) + _gather_sems(1))
    return pl.pallas_call(
        body, name="mix_in_fwd", grid_spec=gs,
        out_shape=[_big((T, IN_COLS), F32), _big((T, D_MODEL), BF16), _big(w_in_buf.shape, BF16),
                   _big(w_out_buf.shape, BF16)],
        input_output_aliases={5: 2, 6: 3},
        compiler_params=_cp(("arbitrary", "arbitrary"), 48),
    )(chip1, _hbm(x), modr, g_pre, b_in4, _hbm(w_in_buf), _hbm(w_out_buf))


def _mixers_fwd(p, wdw, vecs, lbl, gbufs):
    T = p.shape[0]
    tm = _tok_tile(T)
    nt = T // tm
    nch = tm // CHUNK
    ng = len(gbufs)

    def body(*refs):
        p_ref, wdw_ref, bdw_ref, gain_ref, bias_ref, gout_ref, lbl_ref = refs[:7]
        cat_ref, ys_ref, o_ref, st_ref = refs[7 + ng:11 + ng]
        gout_bufs = refs[11 + ng:11 + 2 * ng]
        ubuf, state, qt_s, kt_s, kh_s, v_s, egl_s, lower_s, same_s, gmat_s, gssem, grsem = refs[11 + 2 * ng:]
        i = pl.program_id(0)

        @pl.when(i == 0)
        def _():
            _gather_start(gout_bufs, gssem, grsem)
            lower_s[...], _, same_s[...] = _chunk_masks(tm)
            gmat_s[...] = _gn_matrix()
            state[...] = jnp.zeros(state.shape, F32)
            ubuf[0:HALO, :] = jnp.zeros((HALO, CONV_CH), F32)
            ubuf[HALO + tm:HALO + tm + SUB, :] = jnp.zeros((SUB, CONV_CH), F32)

        @pl.when(i > 0)
        def _():
            ubuf[0:HALO, :] = ubuf[tm:tm + HALO, :]

        ubuf[HALO:HALO + tm, :] = p_ref[:, 0:CONV_CH] * _sig(p_ref[:, CONV_CH:2 * CONV_CH])
        for r in range(tm // CONV_ROWS):
            rows = slice(r * CONV_ROWS, (r + 1) * CONV_ROWS)
            for lb_ in range(CONV_CH // LANE):
                lanes = slice(lb_ * LANE, (lb_ + 1) * LANE)
                ys_ref[rows, lanes] = bdw_ref[:, lanes] + _tap_conv(ubuf, wdw_ref, r * CONV_ROWS, CONV_FWD_TAPS, lanes)
        gmat = gmat_s[...]
        yv = ys_ref[...]
        d = yv - _gmean(yv, gmat)
        rs = lax.rsqrt(_gmean(d * d, gmat) + GN_EPS)
        z = d * rs * gain_ref[...] + bias_ref[...]
        cat_ref[:, 0:CONV_CH] = (z * _sig(z)).astype(BF16)

        lb, _ = _lower_bound(lbl_ref)
        lower, same = lower_s[...], same_s[...]
        o0 = 2 * CONV_CH
        pr = _hgrn_prep(p_ref[:, o0:o0 + HGRN_W], p_ref[:, o0 + HGRN_W:o0 + 2 * HGRN_W], lb, lower, same)
        qt_s[...] = pr["qt"].astype(BF16)
        kt_s[...] = pr["kt"].astype(BF16)
        kh_s[...] = pr["kh"].astype(BF16)
        v_s[...] = p_ref[:, o0 + 2 * HGRN_W:o0 + 3 * HGRN_W].astype(BF16)
        egl_s[...] = jnp.exp(pr["Gl"])
        tri = _tri()

        def chunk(ci, carry):
            r0 = pl.multiple_of(ci * CHUNK, CHUNK)
            rows = pl.ds(r0, CHUNK)
            for h in range(N_HEADS):
                ls = pl.ds(h * HEAD_D, HEAD_D)
                qc, kc, hc, vc = qt_s[rows, ls], kt_s[rows, ls], kh_s[rows, ls], v_s[rows, ls]
                s0 = state[h]
                s0b = s0.astype(BF16)
                st_ref[ci, h] = s0
                att =jnp.where(tri, _dot_nt(qc, kc), 0.0).astype(BF16)
                o_ref[rows, ls] = _dot(att, vc) + _dot_nt(qc, s0b)
                state[h] = s0 * egl_s[pl.ds(r0, 1), ls] + _dot_tn(vc, hc)
            return carry

        lax.fori_loop(0, nch, chunk, 0)
        for h in range(N_HEADS):
            sl = slice(h * HEAD_D, (h + 1) * HEAD_D)
            oh = o_ref[:, sl]
            gh = p_ref[:, o0 + 3 * HGRN_W + h * HEAD_D:o0 + 3 * HGRN_W + (h + 1) * HEAD_D]
            rsh = lax.rsqrt(jnp.mean(oh * oh, axis=-1, keepdims=True) + RMS_EPS)
            hg = (oh * rsh) * gout_ref[:, sl] * (gh * _sig(gh))
            cat_ref[:, CONV_CH + h * HEAD_D:CONV_CH + (h + 1) * HEAD_D] = hg.astype(BF16)

        @pl.when(i == nt - 1)
        def _():
            _gather_finish(gout_bufs, gssem, grsem)

    tile = lambda cols: pl.BlockSpec((tm, cols), lambda i: (i, 0))
    hbm = pl.BlockSpec(memory_space=pl.ANY)
    n_in = 7
    outs = pl.pallas_call(
        body, name="mixers_fwd", grid=(nt,),
        in_specs=[tile(IN_COLS), _full((HALO, CONV_CH))] + [_full((1, CONV_CH))] * 4 + [_full((2, HGRN_W))]
        + [hbm] * ng,
        out_specs=[tile(D_MODEL), tile(CONV_CH), tile(HGRN_W),
                   pl.BlockSpec((nch, N_HEADS, HEAD_D, HEAD_D), lambda i: (i, 0, 0, 0))] + [hbm] * ng,
        out_shape=[_big((T, D_MODEL), BF16), _big((T, CONV_CH), F32), _big((T, HGRN_W), F32),
                   _big((T // CHUNK, N_HEADS, HEAD_D, HEAD_D), F32)] + [_big(b.shape, BF16) for b in gbufs],
        input_output_aliases={n_in + t: 4 + t for t in range(ng)},
        scratch_shapes=[pltpu.VMEM((tm + HALO + SUB, CONV_CH), F32), pltpu.VMEM((N_HEADS, HEAD_D, HEAD_D), F32),
                        pltpu.VMEM((tm, HGRN_W), BF16), pltpu.VMEM((tm, HGRN_W), BF16),
                        pltpu.VMEM((tm, HGRN_W), BF16), pltpu.VMEM((tm, HGRN_W), BF16),
                        pltpu.VMEM((tm, HGRN_W), F32), pltpu.VMEM((tm, tm), BF16), pltpu.VMEM((tm, tm), BF16),
                        pltpu.VMEM((CONV_CH, CONV_CH), BF16)] + _gather_sems(ng),
        compiler_params=_cp(("arbitrary",), 56),
    )(_hbm(p), wdw, *vecs, lbl, *[_hbm(b) for b in gbufs])
    return outs[:4], outs[4:]


def _mix_out_fwd(cat, w_out, x, modr, g_post, g_ffn):
    T = x.shape[0]
    tm = _tok_tile(T)

    def body(cat_ref, w_ref, x_ref, mod_ref, gp_ref, gf_ref, y_ref, x1_ref, h2_ref):
        yv = _dot(cat_ref[...], w_ref[...])
        y_ref[...] = yv
        rs = lax.rsqrt(jnp.mean(yv * yv, axis=-1, keepdims=True) + RMS_EPS)
        x1 = x_ref[...] + mod_ref[2:3, :] * ((yv * rs) * gp_ref[...])
        x1_ref[...] = x1
        rs1 = lax.rsqrt(jnp.mean(x1 * x1, axis=-1, keepdims=True) + RMS_EPS)
        h2 = (x1 * rs1) * gf_ref[...] * (1.0 + mod_ref[4:5, :]) + mod_ref[3:4, :]
        h2_ref[...] = h2.astype(BF16)

    tile = pl.BlockSpec((tm, D_MODEL), lambda i: (i, 0))
    return pl.pallas_call(
        body, name="mix_out_fwd", grid=(T // tm,),
        in_specs=[tile, _full((D_MODEL, D_MODEL)), tile, _full((6, D_MODEL)), _full((1, D_MODEL)),
                  _full((1, D_MODEL))],
        out_specs=[tile, tile, tile],
        out_shape=[_big((T,D_MODEL), F32), _big((T,D_MODEL), F32),
                   _big((T,D_MODEL), BF16)],
        compiler_params=_cp(("arbitrary",), 40),
    )(*_hbm(cat, w_out, x), modr, g_post, g_ffn)


def _row_chains(rows, n=2):
    step = rows // n
    return [slice(k * step, (k + 1) * step) for k in range(n)]


def _ffn_blocks():
    return D_FF // FFN_BLOCK, (D_FF // N_CHIPS) // FFN_BLOCK


def _ffn_fwd(h2, w_up_g, w_down, x1, target, modr, g_post):
    T = h2.shape[0]
    tm = min(FFN_TILE, T)
    fb = FFN_BLOCK
    nj, per = _ffn_blocks()

    def body(h_ref, wu_ref, wd_ref, x1_ref, t_ref, mod_ref, g_ref, r_ref, dy2_ref, dx2_ref, st_ref, acc):
        i, j = pl.program_id(0), pl.program_id(1)

        @pl.when((i == 0) & (j == 0))
        def _():
            st_ref[...] = jnp.zeros(st_ref.shape, F32)

        @pl.when(j == 0)
        def _():
            acc[...] = jnp.zeros(acc.shape, F32)

        for rows in _row_chains(tm):
            ra = jnp.maximum(_dot(h_ref[rows, :], wu_ref[0]), 0.0)
            rb = (ra * ra).astype(BF16)
            r_ref[rows, :] = rb
            acc[rows, :] += _dot(rb, wd_ref[...])

        @pl.when(j == nj - 1)
        def _():
            y2 = acc[...]
            rs = lax.rsqrt(jnp.mean(y2 * y2, axis=-1, keepdims=True) + RMS_EPS)
            nh = y2 * rs
            gp = g_ref[...]
            err = x1_ref[...] + mod_ref[5:6, :] * (nh * gp) - t_ref[...]
            dx2 = err * (1.0 / D_MODEL)
            dx2_ref[...] = dx2
            st_ref[0:1, :] += _colsum(err * err)
            st_ref[1:2, :] += _colsum(dx2 * (nh * gp))
            dn = dx2 * mod_ref[5:6, :]
            st_ref[2:3, :] += _colsum(dn * nh)
            dy2_ref[...] = _rms_bwd(dn * gp, nh, rs).astype(BF16)

    tile = pl.BlockSpec((tm, D_MODEL), lambda i, j: (i, 0))
    return pl.pallas_call(
        body, name="ffn_fwd", grid=(T // tm, nj),
        in_specs=[tile, pl.BlockSpec((1, D_MODEL, fb), lambda i, j: (j // per, 0, j % per)),
                  pl.BlockSpec((fb, D_MODEL), lambda i, j: (j, 0)), tile, tile,
                  _full((6, D_MODEL)), _full((1, D_MODEL))],
        out_specs=[pl.BlockSpec((tm, fb), lambda i, j: (i, j)), tile, tile, _full((8, D_MODEL))],
        out_shape=[_big((T, D_FF), BF16), _big((T, D_MODEL), BF16), _big((T, D_MODEL), F32),
                   jax.ShapeDtypeStruct((8, D_MODEL), F32)],
        scratch_shapes=[pltpu.VMEM((tm, D_MODEL), F32)],
        compiler_params=_cp(("arbitrary", "arbitrary"), 56),
    )(*_hbm(h2, w_up_g, w_down, x1, target), modr, g_post)


def _rms_bwd(dxn, xn, rs):
    return rs * (dxn - xn * jnp.mean(dxn * xn, axis=-1, keepdims=True))


def _ffn_bwd(dy2, r, x1, dx2, w_up_g, w_down, modr, g_ffn):
    T = dx2.shape[0]
    tm = min(FFN_TILE, T)
    fb = FFN_BLOCK
    nj, per = _ffn_blocks()

    def body(dy2_ref, r_ref, x1_ref, dx2_ref, wu_ref, wd_ref, mod_ref, gf_ref, da_ref, dx1_ref, st_ref, dh_s):
        i, j = pl.program_id(0), pl.program_id(1)

        @pl.when((i == 0) & (j == 0))
        def _():
            st_ref[...] = jnp.zeros(st_ref.shape, F32)

        @pl.when(j == 0)
        def _():
            dh_s[...] = jnp.zeros(dh_s.shape, F32)

        for rows in _row_chains(tm):
            ra = jnp.sqrt(r_ref[rows, :].astype(F32))
            da = (_dot_nt(dy2_ref[rows, :], wd_ref[...]) * (2.0 * ra)).astype(BF16)
            da_ref[rows, :] = da
            dh_s[rows, :] += _dot_nt(da, wu_ref[0])

        @pl.when(j == nj - 1)
        def _():
            dh = dh_s[...]
            x1v = x1_ref[...]
            rs1 = lax.rsqrt(jnp.mean(x1v * x1v, axis=-1, keepdims=True) + RMS_EPS)
            xn = x1v * rs1
            st_ref[0:1, :] += _colsum(dh)
            st_ref[1:2, :] += _colsum(dh * (xn * gf_ref[...]))
            dsc = dh * (1.0 + mod_ref[4:5, :])
            st_ref[2:3, :] += _colsum(dsc * xn)
            dx1_ref[...] = dx2_ref[...] + _rms_bwd(dsc * gf_ref[...], xn, rs1)

    tile = pl.BlockSpec((tm, D_MODEL), lambda i, j: (i, 0))
    ftile = pl.BlockSpec((tm, fb), lambda i, j: (i, j))
    return pl.pallas_call(
        body, name="ffn_bwd", grid=(T // tm, nj),
        in_specs=[tile, ftile, tile, tile, pl.BlockSpec((1, D_MODEL, fb), lambda i, j: (j // per, 0, j % per)),
                  pl.BlockSpec((fb, D_MODEL), lambda i, j: (j, 0)), _full((6, D_MODEL)), _full((1, D_MODEL))],
        out_specs=[ftile, tile, _full((8, D_MODEL))],
        out_shape=[_big((T, D_FF), BF16), _big((T, D_MODEL), F32), jax.ShapeDtypeStruct((8, D_MODEL), F32)],
        scratch_shapes=[pltpu.VMEM((tm, D_MODEL), F32)],
        compiler_params=_cp(("arbitrary", "arbitrary"), 56),
    )(*_hbm(dy2, r, x1, dx2, w_up_g, w_down), modr, g_ffn)


def _mix_out_bwd(dx1, y, w_out, modr, g_post, swap):
    T = dx1.shape[0]
    tm = _tok_tile(T)
    nt = T // tm
    ns = len(swap)

    def body(*refs):
        dx1_ref, y_ref, w_ref, mod_ref, gp_ref = refs[:5]
        s_ins = refs[5:5 + ns]
        dy_ref, dcat_ref, st_ref = refs[5 + ns:8 + ns]
        s_outs = refs[8 + ns:8 + 2 * ns]
        pssem, prsem = refs[8 + 2 * ns:]
        i = pl.program_id(0)

        @pl.when(i == 0)
        def _():
            for cp in _pair_copies(s_ins, s_outs, pssem, prsem):
                cp.start()
            st_ref[...] = jnp.zeros(st_ref.shape, F32)

        dxv, yv = dx1_ref[...], y_ref[...]
        rs = lax.rsqrt(jnp.mean(yv * yv, axis=-1, keepdims=True) + RMS_EPS)
        nh = yv * rs
        st_ref[0:1, :] += _colsum(dxv * (nh * gp_ref[...]))
        dn = dxv * mod_ref[2:3, :]
        st_ref[1:2, :] += _colsum(dn * nh)
        dy = _rms_bwd(dn * gp_ref[...], nh, rs).astype(BF16)
        dy_ref[...] = dy
        dcat_ref[...] = _dot_nt(dy, w_ref[...])

        @pl.when(i == nt - 1)
        def _():
            copies = _pair_copies(s_ins, s_outs, pssem, prsem)
            for cp in copies:
                cp.wait_recv()
            for cp in copies:
                cp.wait_send()

    tile = pl.BlockSpec((tm, D_MODEL), lambda i: (i, 0))
    hbm = pl.BlockSpec(memory_space=pl.ANY)
    outs = pl.pallas_call(
        body, name="mix_out_bwd", grid=(nt,),
        in_specs=[tile, tile, _full((D_MODEL, D_MODEL)), _full((6, D_MODEL)), _full((1, D_MODEL))] + [hbm] * ns,
        out_specs=[tile, tile, _full((8, D_MODEL))] + [hbm] * ns,
        out_shape=[_big((T, D_MODEL), BF16), _big((T, D_MODEL), F32), jax.ShapeDtypeStruct((8, D_MODEL), F32)]
        + _pair_shapes(swap),
        scratch_shapes=_pair_sems(ns),
        compiler_params=_cp(("arbitrary",), 40),
    )(*_hbm(dx1, y, w_out), modr, g_post, *[_hbm(g) for g in swap])
    return outs[:3], outs[3:]


def _mixers_bwd(p, dcat, ys, o, states, h1, wdw, vecs, lbl, pairs_b):
    T = p.shape[0]
    tm = min(MIXB_TILE, T)
    nt = T // tm
    nch = tm // CHUNK
    hpt = tm // HALO
    nx = len(pairs_b)
    nb = IN_COLS // N_CHIPS

    def body(*refs):
        (p_ref, ph_ref, dcat_ref, ys_ref, o_ref, st_ref, h1_ref, wdw_ref, bdw_ref, gain_ref, bias_ref, gout_ref,
         lbl_ref) = refs[:13]
        x_ins = refs[13:13 + nx]
        dp_ref, sb_ref, s5_ref, dw_ref = refs[13 + nx:17 + nx]
        x_outs = refs[17 + nx:17 + 2 * nx]
        gin_ref = refs[17 + 2 * nx]
        (ubuf, dybuf, carry, dstate, qt_s, kt_s, kh_s, v_s, do_s, egl_s, dqt_s, dkt_s, dkh_s, dv_s, dgl_s,
         dsh, dw8, dshift, lower_s, upper_s, same_s, gmat_s, gacc, dp_prev, h1_prev, gsem, xssem,
         xrsem) = refs[18 + 2 * nx:]
        i = pl.program_id(0)
        tile_idx = nt - 1 - i

        @pl.when(i == 0)
        def _():
            for cp in _xchg_copies(x_ins, x_outs, xssem, xrsem):
                cp.start()
            gacc[...] = jnp.zeros(gacc.shape, F32)
            dstate[...] = jnp.zeros(dstate.shape, F32)
            carry[...] = jnp.zeros(carry.shape, F32)
            sb_ref[...] = jnp.zeros(sb_ref.shape, F32)
            s5_ref[...] = jnp.zeros(s5_ref.shape, F32)
            dw_ref[...] = jnp.zeros(dw_ref.shape, F32)
            dw8[...] = jnp.zeros(dw8.shape, F32)
            lower_s[...], upper_s[...], same_s[...] = _chunk_masks(tm)
            gmat_s[...] = _gn_matrix()
            dsh[0:SUB, :] = jnp.zeros((SUB, CONV_CH), F32)
            dsh[SUB + tm:2 * SUB + tm, :] = jnp.zeros((SUB, CONV_CH), F32)
            ubuf[HALO + tm:HALO + tm + SUB, :] = jnp.zeros((SUB, CONV_CH), F32)
            dp_prev[...] = jnp.zeros(dp_prev.shape, BF16)
            h1_prev[...] = jnp.zeros(h1_prev.shape, BF16)

        n_pieces = (tm // CONV_ROWS) * (CONV_CH // LANE)
        per_block = n_pieces // N_CHIPS
        prow = D_MODEL // per_block

        def w_in_grad_piece(k):
            j, part = k // per_block, k % per_block
            rows_k = slice(part * prow, (part + 1) * prow)
            gacc[j, rows_k, :] += _dot_tn(h1_prev[:, rows_k], dp_prev[:, j * nb:(j + 1) * nb])

        uh = ph_ref[:, 0:CONV_CH] * _sig(ph_ref[:, CONV_CH:2 * CONV_CH])
        ubuf[0:HALO, :] = jnp.where(tile_idx > 0, uh, 0.0)
        ubuf[HALO:HALO + tm, :] = p_ref[:, 0:CONV_CH] * _sig(p_ref[:, CONV_CH:2 * CONV_CH])
        gmat = gmat_s[...]
        gain = gain_ref[...]
        yv = ys_ref[...]
        d = yv - _gmean(yv, gmat)
        rs = lax.rsqrt(_gmean(d * d, gmat) + GN_EPS)
        yn = d * rs
        z = yn * gain + bias_ref[...]
        sz = _sig(z)
        dz = dcat_ref[:, 0:CONV_CH] * (sz * (1.0 + z * (1.0 - sz)))
        dyn = dz * gain
        dyc = rs * (dyn - _gmean(dyn, gmat) - yn * _gmean(dyn * yn, gmat))
        s5_ref[0:1, :] += _colsum(dyc)
        s5_ref[1:2, :] += _colsum(dz * yn)
        s5_ref[2:3, :] += _colsum(dz)
        dybuf[tm:tm + HALO, :] = carry[...]
        dybuf[0:tm, :] = dyc
        dsh[SUB:SUB + tm, :] = dyc
        carry[...] = dyc[0:HALO, :]
        for b in range(SUB):
            dshift[...] = dsh[SUB - b:2 * SUB - b + tm, :]
            for j, off in CONV_FWD_TAPS:
                if off % SUB == b:
                    prod = dshift[...] * ubuf[off - b:off - b + tm + SUB, :]
                    dw8[j] += jnp.sum(prod.reshape((tm + SUB) // SUB, SUB, CONV_CH), axis=0)
        for r in range(tm // CONV_ROWS):
            rows = slice(r * CONV_ROWS, (r + 1) * CONV_ROWS)
            for lb_ in range(CONV_CH // LANE):
                lanes = slice(lb_ * LANE, (lb_ + 1) * LANE)
                glanes = slice(CONV_CH + lb_ * LANE, CONV_CH + (lb_ + 1) * LANE)
                w_in_grad_piece(r * (CONV_CH // LANE) + lb_)
                acc = _tap_conv(dybuf, wdw_ref, r * CONV_ROWS, CONV_BWD_TAPS, lanes)
                val = p_ref[rows, lanes]
                sg = _sig(p_ref[rows, glanes])
                dval = acc * sg
                dgate = acc * val * (sg * (1.0 - sg))
                dp_ref[rows, lanes] = dval.astype(BF16)
                dp_ref[rows, glanes] = dgate.astype(BF16)
                sb_ref[0:1, lanes] += _colsum(dval)
                sb_ref[0:1, glanes] += _colsum(dgate)

        o0 = 2 * CONV_CH
        for h in range(N_HEADS):
            sl = slice(h * HEAD_D, (h + 1) * HEAD_D)
            gsl = slice(o0 + 3 * HGRN_W + h * HEAD_D, o0 + 3 * HGRN_W + (h + 1) * HEAD_D)
            oh = o_ref[:, sl]
            gh = p_ref[:, gsl]
            dh = dcat_ref[:, CONV_CH + h * HEAD_D:CONV_CH + (h + 1) * HEAD_D]
            gout = gout_ref[:, sl]
            rsh = lax.rsqrt(jnp.mean(oh * oh, axis=-1, keepdims=True) + RMS_EPS)
            on = oh * rsh
            sgg = _sig(gh)
            dgh = dh * (on * gout) * (sgg * (1.0 + gh * (1.0 - sgg)))
            dm = dh * (gh * sgg)
            s5_ref[3:4, sl] += _colsum(dm * on)
            do_s[:, sl] = _rms_bwd(dm * gout, on, rsh).astype(BF16)
            dp_ref[:, gsl] = dgh.astype(BF16)
            sb_ref[2:3, CONV_CH + h * HEAD_D:CONV_CH + (h + 1) * HEAD_D] += _colsum(dgh)

        lb, _ = _lower_bound(lbl_ref)
        lower, upper, same = lower_s[...], upper_s[...], same_s[...]
        pq = p_ref[:, o0:o0 + HGRN_W]
        pr = _hgrn_prep(pq, p_ref[:, o0 + HGRN_W:o0 + 2 * HGRN_W], lb, lower, same)
        qt_s[...] = pr["qt"].astype(BF16)
        kt_s[...] = pr["kt"].astype(BF16)
        kh_s[...] = pr["kh"].astype(BF16)
        v_s[...] = p_ref[:, o0 + 2 * HGRN_W:o0 + 3 * HGRN_W].astype(BF16)
        egl_s[...] = jnp.exp(pr["Gl"])
        tri = _tri()

        def chunk(it, c_):
            ci = nch - 1 - it
            r0 = pl.multiple_of(ci * CHUNK, CHUNK)
            rows = pl.ds(r0, CHUNK)
            for h in range(N_HEADS):
                ls = pl.ds(h * HEAD_D, HEAD_D)
                qc, kc, hc, vc = qt_s[rows, ls], kt_s[rows, ls], kh_s[rows, ls], v_s[rows, ls]
                dob = do_s[rows, ls]
                s0 = st_ref[ci, h]
                s0b = s0.astype(BF16)
                ds1 = dstate[h]
                ds1b = ds1.astype(BF16)
                egl = egl_s[pl.ds(r0, 1), ls]
                att = jnp.where(tri, _dot_nt(qc, kc), 0.0).astype(BF16)
                datt = jnp.where(tri, _dot_nt(dob, vc), 0.0).astype(BF16)
                dv_s[rows, ls] = _dot_tn(att, dob) + _dot_nt(hc, ds1b)
                dqt_s[rows, ls] = _dot(datt, kc) + _dot(dob, s0b)
                dkt_s[rows, ls] = _dot_tn(datt, qc)
                dkh_s[rows, ls] = _dot(vc, ds1b)
                dgl = egl * _colsum(ds1 * s0)
                dgl_s[rows, ls] = jnp.broadcast_to(dgl, (CHUNK, HEAD_D))
                dstate[h] = ds1 * egl + _dot_tn(dob, qc)
            return c_

        lax.fori_loop(0, nch, chunk, 0)
        dqt, dkt, dkh = dqt_s[...], dkt_s[...], dkh_s[...]
        dk = dkt * pr["enG"] + dkh * pr["eGlG"]
        khk = dkh * kh_s[...].astype(F32)
        dG = dqt * qt_s[...].astype(F32) - dkt * kt_s[...].astype(F32) - khk
        dlogf = _mm3(upper, dG) + _mm3(same, khk) + dgl_s[...]
        df = dlogf / pr["f"] - dk
        sf, sq = pr["sf"], pr["sq"]
        s5_ref[4:5, :] += _colsum(df * (1.0 - sf))
        dfl = df * (1.0 - lb) * (sf * (1.0 - sf))
        dq = (dqt * pr["eG"]) * (sq * (1.0 + pq * (1.0 - sq)))
        dvv = dv_s[...]
        dp_ref[:, o0:o0 + HGRN_W] = dq.astype(BF16)
        dp_ref[:, o0 + HGRN_W:o0 + 2 * HGRN_W] = dfl.astype(BF16)
        dp_ref[:, o0 + 2 * HGRN_W:o0 + 3 * HGRN_W] = dvv.astype(BF16)
        sb_ref[1:2, 0:HGRN_W] += _colsum(dq)
        sb_ref[1:2, HGRN_W:2 * HGRN_W] += _colsum(dfl)
        sb_ref[2:3, 0:HGRN_W] += _colsum(dvv)

        dp_prev[...] = dp_ref[...]
        h1_prev[...] = h1_ref[...]

        @pl.when(i == nt - 1)
        def _():
            for k in range(n_pieces):
                w_in_grad_piece(k)
            out = pltpu.make_async_copy(gacc, gin_ref, gsem)
            out.start()
            for j in range(CONV_K):
                dw_ref[j:j + 1, :] = _colsum(dw8[j])
            copies = _xchg_copies(x_ins, x_outs, xssem, xrsem)
            for cp in copies:
                cp.wait_recv()
            for cp in copies:
                cp.wait_send()
            out.wait()

    rev = lambda cols: pl.BlockSpec((tm, cols), lambda i: (nt - 1 - i, 0))
    halo = pl.BlockSpec((HALO, 2 * CONV_CH), lambda i: (jnp.maximum((nt - 1 - i) * hpt - 1, 0), 0))
    wide = lambda n: pltpu.VMEM((tm, HGRN_W), n)
    hbm = pl.BlockSpec(memory_space=pl.ANY)
    outs = pl.pallas_call(
        body, name="mixers_bwd", grid=(nt,),
        in_specs=[rev(IN_COLS), halo, rev(D_MODEL), rev(CONV_CH), rev(HGRN_W),
                  pl.BlockSpec((nch, N_HEADS, HEAD_D, HEAD_D), lambda i: (nt - 1 - i, 0, 0, 0)), rev(D_MODEL),
                  _full((HALO, CONV_CH))] + [_full((1, CONV_CH))] * 4 + [_full((2, HGRN_W))] + [hbm] * nx,
        out_specs=[rev(IN_COLS), _full((8, D_MODEL)), _full((8, CONV_CH)), _full((HALO, CONV_CH))]
        + [hbm] * (nx + 1),
        out_shape=[_big((T, IN_COLS), BF16), jax.ShapeDtypeStruct((8, D_MODEL), F32),
                   jax.ShapeDtypeStruct((8, CONV_CH), F32), jax.ShapeDtypeStruct((HALO, CONV_CH), F32)]
        + [_big(pb.shape, BF16) for pb in pairs_b] + [_big((N_CHIPS, D_MODEL, nb), F32)],
        scratch_shapes=[pltpu.VMEM((tm + HALO + SUB, CONV_CH), F32), pltpu.VMEM((tm + HALO, CONV_CH), F32),
                        pltpu.VMEM((HALO, CONV_CH), F32), pltpu.VMEM((N_HEADS, HEAD_D, HEAD_D), F32),
                        wide(BF16), wide(BF16), wide(BF16), wide(BF16), wide(BF16),
                        wide(F32), wide(F32), wide(F32), wide(F32), wide(F32), wide(F32),
                        pltpu.VMEM((tm + 2 * SUB, CONV_CH), F32), pltpu.VMEM((HALO, SUB, CONV_CH), F32),
                        pltpu.VMEM((tm + SUB, CONV_CH), F32), pltpu.VMEM((tm, tm), BF16), pltpu.VMEM((tm, tm), BF16),
                        pltpu.VMEM((tm, tm), BF16), pltpu.VMEM((CONV_CH, CONV_CH), BF16),
                        pltpu.VMEM((N_CHIPS, D_MODEL, nb), F32), pltpu.VMEM((tm, IN_COLS), BF16),
                        pltpu.VMEM((tm, D_MODEL), BF16), pltpu.SemaphoreType.DMA]
        + _xchg_sems(nx),
        compiler_params=_cp(("arbitrary",), 56),
    )(*_hbm(p, p, dcat, ys, o, states, h1), wdw, *vecs, lbl, *[_hbm(pb) for pb in pairs_b])
    return outs[:4], outs[4:4 + nx], outs[4 + nx]


def _mix_in_bwd(dp, w_in_g, x, dx1, modr, g_pre, pairs_b):
    T = x.shape[0]
    tm = _tok_tile(T)
    nt = T // tm
    nb = IN_COLS // N_CHIPS
    nx = len(pairs_b)

    def body(*refs):
        dp_ref, w_ref, x_ref, dx1_ref, mod_ref, g_ref = refs[:6]
        x_ins = refs[6:6 + nx]
        gx_ref, st_ref = refs[6 + nx:8 + nx]
        x_outs = refs[8 + nx:8 + 2 * nx]
        xssem, xrsem = refs[8 + 2 * nx:]
        i = pl.program_id(0)

        @pl.when(i == 0)
        def _():
            for cp in _xchg_copies(x_ins, x_outs, xssem, xrsem):
                cp.start()
            st_ref[...] = jnp.zeros(st_ref.shape, F32)

        dh = None
        for j in range(N_CHIPS):
            part = _dot_nt(dp_ref[:, j * nb:(j + 1) * nb], w_ref[j])
            dh = part if dh is None else dh + part
        xv = x_ref[...]
        rs = lax.rsqrt(jnp.mean(xv * xv, axis=-1, keepdims=True) + RMS_EPS)
        xn = xv * rs
        st_ref[0:1, :] += _colsum(dh)
        st_ref[1:2, :] += _colsum(dh * (xn * g_ref[...]))
        dsc = dh * (1.0 + mod_ref[1:2, :])
        st_ref[2:3, :] += _colsum(dsc * xn)
        gx_ref[...] = dx1_ref[...] + _rms_bwd(dsc * g_ref[...], xn, rs)

        @pl.when(i == nt - 1)
        def _():
            copies = _xchg_copies(x_ins, x_outs, xssem, xrsem)
            for cp in copies:
                cp.wait_recv()
            for cp in copies:
                cp.wait_send()

    tile = pl.BlockSpec((tm, D_MODEL), lambda i: (i, 0))
    hbm = pl.BlockSpec(memory_space=pl.ANY)
    outs = pl.pallas_call(
        body, name="mix_in_bwd", grid=(nt,),
        in_specs=[pl.BlockSpec((tm, IN_COLS), lambda i: (i, 0)), _full((N_CHIPS, D_MODEL, nb)), tile, tile,
                  _full((6, D_MODEL)), _full((1, D_MODEL))] + [hbm] * nx,
        out_specs=[tile, _full((8, D_MODEL))] + [hbm] * nx,
        out_shape=[_big((T, D_MODEL), F32), jax.ShapeDtypeStruct((8, D_MODEL), F32)]
        + [_big(pb.shape, BF16) for pb in pairs_b],
        scratch_shapes=_xchg_sems(nx),
        compiler_params=_cp(("arbitrary",), 48),
    )(*_hbm(dp, w_in_g, x, dx1), modr, g_pre, *[_hbm(pb) for pb in pairs_b])
    return outs[:2], outs[2:]


def _weight_grad(a, b, a_blocked, b_blocked, name):
    T = a.shape[0]
    tt = min(GRAD_TILE, T)
    nt = T // tt
    ka = a.shape[1] // N_CHIPS if a_blocked else a.shape[1]
    nb = b.shape[1] // N_CHIPS if b_blocked else b.shape[1]

    def body(a_ref, b_ref, o_ref, ob_ref):
        t = pl.program_id(1)

        @pl.when(t == 0)
        def _():
            o_ref[...] = jnp.zeros(o_ref.shape, F32)

        for cols in _row_chains(nb):
            o_ref[0, :, cols] += _dot_tn(a_ref[...], b_ref[:, cols])

        @pl.when(t == nt - 1)
        def _():
            ob_ref[0] = o_ref[0].astype(BF16)

    blk = pl.BlockSpec((1, ka, nb), lambda j, t: (j, 0, 0))
    return pl.pallas_call(
        body, name=name, grid=(N_CHIPS, nt),
        in_specs=[pl.BlockSpec((tt, ka), (lambda j, t: (t, j)) if a_blocked else (lambda j, t: (t, 0))),
                  pl.BlockSpec((tt, nb), (lambda j, t: (t, j)) if b_blocked else (lambda j, t: (t, 0)))],
        out_specs=[blk, blk],
        out_shape=[_big((N_CHIPS, ka, nb), F32), _big((N_CHIPS, ka, nb), BF16)],
        compiler_params=_cp(("arbitrary", "arbitrary"), 48),
    )(*_hbm(a, b))


R_LOSS = 0
R_FFN = 8
R_OUT = 16
R_IN = 24
R_BIN = 32
R_512 = 40
R_DW = 48
N_STAT_ROWS = 80
MOD_ROWS = (R_IN + 0, R_IN + 1, R_OUT + 0, R_FFN + 0, R_FFN + 1, R_LOSS + 1)


def _small_update(gath, params):
    names = ["b_ada", "lb_logits", "g_pre_mix", "b_in", "b_dw", "gn_gain", "gn_bias", "g_hgrn_out", "g_post_mix",
             "g_pre_ffn", "g_post_ffn"]
    flat = []
    for n in names:
        flat += list(params[n])
    n_in = 1 + len(flat)

    def body(*refs):
        g_ref = refs[0]
        prm = {n: refs[1 + 3 * k:4 + 3 * k] for k, n in enumerate(names)}
        outs = refs[n_in:]
        loss_ref, dmod_ref, dwdw_ref = outs[0], outs[1], outs[2]
        res = {n: outs[3 + 4 * k:7 + 4 * k] for k, n in enumerate(names)}
        red = g_ref[0]
        for dev in range(1, N_DEV):
            red = red + g_ref[dev]
        loss_ref[...] = jnp.broadcast_to(
            (0.5 / D_MODEL) * jnp.sum(red[R_LOSS:R_LOSS + 1, :], axis=-1, keepdims=True), loss_ref.shape)
        for dev in range(N_DEV):
            for k, r in enumerate(MOD_ROWS):
                dmod_ref[dev:dev + 1, k * D_MODEL:(k + 1) * D_MODEL] = g_ref[dev, r:r + 1, :]
        dwdw_ref[...] = red[R_DW:R_DW + HALO, 0:CONV_CH]

        def finish(name, pieces):
            w_ref, m_ref, v_ref = prm[name]
            g_out, d_out, m_out, v_out = res[name]
            for rsl, lsl, g in pieces:
                d, m2, v2 = _adam_math(w_ref[rsl, lsl], g, m_ref[rsl, lsl], v_ref[rsl, lsl])
                g_out[rsl, lsl] = g
                d_out[rsl, lsl] = d
                m_out[rsl, lsl] = m2
                v_out[rsl, lsl] = v2

        one = slice(0, 1)
        row = lambda r: red[r:r + 1, :]
        half = lambda r: red[r:r + 1, 0:CONV_CH]
        finish("b_ada", [(one, slice(k * D_MODEL, (k + 1) * D_MODEL), row(r)) for k, r in enumerate(MOD_ROWS)])
        finish("b_in", [(one, slice(k * D_MODEL, (k + 1) * D_MODEL), row(R_BIN + k)) for k in range(3)])
        finish("g_pre_mix", [(one, slice(None), row(R_IN + 2))])
        finish("g_post_mix", [(one, slice(None), row(R_OUT + 1))])
        finish("g_pre_ffn", [(one, slice(None), row(R_FFN + 2))])
        finish("g_post_ffn", [(one, slice(None), row(R_LOSS + 2))])
        finish("b_dw", [(one, slice(None), half(R_512 + 0))])
        finish("gn_gain", [(one, slice(None), half(R_512 + 1))])
        finish("gn_bias", [(one, slice(None), half(R_512 + 2))])
        finish("g_hgrn_out", [(one, slice(None), half(R_512 + 3))])
        s0, s1 = _lower_bound(prm["lb_logits"][0])
        dlb = half(R_512 + 4)
        finish("lb_logits", [(slice(0, 1), slice(None), dlb * s0 * (1.0 - s0)),
                             (slice(1, 2), slice(None), -dlb * s0 * s1)])

    vm = pl.BlockSpec(memory_space=pltpu.VMEM)
    out_shape = [jax.ShapeDtypeStruct((8, 128), F32), jax.ShapeDtypeStruct((N_DEV, 6 * D_MODEL), F32),
                 jax.ShapeDtypeStruct((HALO, CONV_CH), F32)]
    for n in names:
        out_shape += [jax.ShapeDtypeStruct(params[n][0].shape, F32)] * 4
    outs = pl.pallas_call(
        body, name="small_update", out_shape=out_shape,
        in_specs=[vm] * n_in, out_specs=[vm] * len(out_shape),
        compiler_params=_cp(None, 32),
    )(gath, *flat)
    return outs[0], outs[1], outs[2], {n: outs[3 + 4 * k:7 + 4 * k] for k, n in enumerate(names)}


def _wdw_adam(w, g, m, v):
    def body(w_ref, g_ref, m_ref, v_ref, d_out, m_out, v_out):
        d, m2, v2 = _adam_math(w_ref[...], g_ref[...], m_ref[...], v_ref[...])
        d_out[...] = d
        m_out[...] = m2
        v_out[...] = v2

    vm = pl.BlockSpec(memory_space=pltpu.VMEM)
    return pl.pallas_call(
        body, name="wdw_adam", out_shape=[jax.ShapeDtypeStruct(w.shape, F32)] * 3,
        in_specs=[vm] * 4, out_specs=[vm] * 3, compiler_params=_cp(None, 16),
    )(w, g, m, v)


def kernel(x, c, w_ada, b_ada, lb_logits, g_pre_mix, w_in, b_in, w_dw, b_dw, gn_gain, gn_bias, g_hgrn_out, w_out, g_post_mix, g_pre_ffn, w_up, w_down, g_post_ffn, loss_target, m_w_ada, m_b_ada, m_lb_logits, m_g_pre_mix, m_w_in, m_b_in, m_w_dw, m_b_dw, m_gn_gain, m_gn_bias, m_g_hgrn_out, m_w_out, m_g_post_mix, m_g_pre_ffn, m_w_up, m_w_down, m_g_post_ffn, v_w_ada, v_b_ada, v_lb_logits, v_g_pre_mix, v_w_in, v_b_in, v_w_dw, v_b_dw, v_gn_gain, v_gn_bias, v_g_hgrn_out, v_w_out, v_g_post_mix, v_g_pre_ffn, v_w_up, v_w_down, v_g_post_ffn):
    ax, ay, ac = lax.axis_index("x"), lax.axis_index("y"), lax.axis_index("c")
    chip = 2 * ax + ay
    T = x.shape[1]
    xs, tgt = x[0], loss_target[0]
    ada_cols = w_ada.shape[2]

    b_sh = lax.dynamic_slice_in_dim(b_ada, chip * ada_cols, ada_cols, axis=1)
    wdw_pad = jnp.pad(w_dw[0], ((0, HALO - CONV_K), (0, 0)))
    chip1 = jnp.reshape(chip, (1,)).astype(jnp.int32)
    place = jnp.stack([ac, chip]).astype(jnp.int32)
    _, c8, modg, wdwg = _ada_exchange(c, w_ada[0], b_sh, wdw_pad)
    modr = modg.reshape(6, D_MODEL)
    wdw_all = jnp.transpose(wdwg, (1, 0, 2)).reshape(HALO, CONV_CH)
    bufs = {t: _cast_own(chip1, w[0], "cast_" + t)
            for w, t in ((w_in, "w_in"), (w_out, "w_out"), (w_up, "w_up"), (w_down, "w_down"))}
    vec = (b_dw, gn_gain, gn_bias, g_hgrn_out)

    p, h1, w_in_g, w_out_g = _mix_in_fwd(chip1, xs, modr, g_pre_mix, b_in.reshape(N_CHIPS, 1, IN_COLS // N_CHIPS),
                                          bufs["w_in"], bufs["w_out"])
    (cat, ys, o, states), (w_up_g, w_down_g) = _mixers_fwd(p, wdw_all, vec, lb_logits, [bufs["w_up"], bufs["w_down"]])
    w_out_f = w_out_g.reshape(D_MODEL, D_MODEL)
    w_down_f = w_down_g.reshape(D_FF, D_MODEL)
    y, x1, h2 = _mix_out_fwd(cat, w_out_f, xs, modr, g_post_mix, g_pre_ffn)
    r, dy2, dx2, st_loss = _ffn_fwd(h2, w_up_g, w_down_f, x1, tgt, modr, g_post_ffn)

    def pair_sums(grads, got, tags):
        return [_pair_sum(place, g, o_, "pair_sum_" + t) for (g, _), o_, t in zip(grads, got, tags)]

    da, dx1, st_ffn = _ffn_bwd(dy2, r, x1, dx2, w_up_g, w_down_f, modr, g_pre_ffn)
    g_up = _weight_grad(h2, da, False, True, "grad_w_up")
    g_down = _weight_grad(r, dy2, True, False, "grad_w_down")
    (dy, dcat, st_out), got_ud = _mix_out_bwd(dx1, y, w_out_f, modr, g_post_mix, [g_up[1], g_down[1]])
    g_out = _weight_grad(cat, dy, True, False, "grad_w_out")
    got_o = _pair_swap([g_out[1]], "pair_swap_w_out")
    early = pair_sums([g_out, g_up, g_down], list(got_o) + list(got_ud), ["w_out", "w_up", "w_down"])
    (dp, st_bin, st_512, dwdw), got_early, g_in = _mixers_bwd(p, dcat, ys, o, states, h1, wdw_all, vec, lb_logits,
                                                              [pb for _, pb in early])
    late = pair_sums([(g_in, None)], _pair_swap([g_in], "pair_swap_w_in"), ["w_in"])
    (grad_x, st_in), got_late = _mix_in_bwd(dp, w_in_g, xs, dx1, modr, g_pre_mix, [late[0][1]])
    fulls = [_chip_sum(place, pf, gb, "chip_sum_" + t)
             for (pf, _), gb, t in zip(late + early, list(got_late) + list(got_early), ["w_in", "w_out", "w_up", "w_down"])]

    pad_lanes = lambda s: jnp.pad(s, ((0, 0), (0, D_MODEL - s.shape[1])))
    stats = jnp.concatenate([st_loss, st_ffn, st_out, st_in, st_bin, pad_lanes(st_512), pad_lanes(dwdw)], axis=0)
    (g_w_in, g_w_out, g_w_up, g_w_down), gath = _final_exchange(fulls, stats)
    small = {"b_ada": (b_ada, m_b_ada, v_b_ada), "lb_logits": (lb_logits, m_lb_logits, v_lb_logits),
             "g_pre_mix": (g_pre_mix, m_g_pre_mix, v_g_pre_mix), "b_in": (b_in, m_b_in, v_b_in),
             "b_dw": (b_dw, m_b_dw, v_b_dw), "gn_gain": (gn_gain, m_gn_gain, v_gn_gain),
             "gn_bias": (gn_bias, m_gn_bias, v_gn_bias), "g_hgrn_out": (g_hgrn_out, m_g_hgrn_out, v_g_hgrn_out),
             "g_post_mix": (g_post_mix, m_g_post_mix, v_g_post_mix), "g_pre_ffn": (g_pre_ffn, m_g_pre_ffn, v_g_pre_ffn),
             "g_post_ffn": (g_post_ffn, m_g_post_ffn, v_g_post_ffn)}
    loss_t, dmod_all, dwdw_sum, sres = _small_update(gath, small)
    loss = loss_t[0, 0]

    res = dict(sres)
    dmod_sh = lax.dynamic_slice_in_dim(dmod_all, chip * ada_cols, ada_cols, axis=1)
    res["w_ada"] = [t[None] for t in _ada_grad_adam(jnp.transpose(c8), dmod_sh, w_ada[0], m_w_ada[0], v_w_ada[0])]
    g_wdw = lax.dynamic_slice_in_dim(dwdw_sum, chip * HEAD_D, HEAD_D, axis=1)[:CONV_K][None]
    res["w_dw"] = [g_wdw] + list(_wdw_adam(w_dw, g_wdw, m_w_dw, v_w_dw))
    for name, g, w, m, v in (("w_in", g_w_in, w_in, m_w_in, v_w_in), ("w_out", g_w_out, w_out, m_w_out, v_w_out),
                             ("w_up", g_w_up, w_up, m_w_up, v_w_up), ("w_down", g_w_down, w_down, m_w_down, v_w_down)):
        d, m2, v2 = _adam_big(w[0], g, m[0], v[0], "adam_" + name)
        res[name] = [g[None], d[None], m2[None], v2[None]]

    order = ["w_ada", "b_ada", "lb_logits", "g_pre_mix", "w_in", "b_in", "w_dw", "b_dw", "gn_gain", "gn_bias",
             "g_hgrn_out", "w_out", "g_post_mix", "g_pre_ffn", "w_up", "w_down", "g_post_ffn"]
    out = [loss, grad_x[None]]
    for k in range(4):
        out += [res[n][k] for n in order]
    return tuple(out)
```

```python
import jax
import jax.numpy as jnp
from jax import lax
from jax.experimental import pallas as pl
from jax.experimental.pallas import tpu as pltpu

F32, BF16 = jnp.float32, jnp.bfloat16
D_MODEL = 1024
CONV_CH = 512
HGRN_W = 512
N_HEADS = 4
HEAD_D = 128
CONV_K = 31
GN_GROUP = 64
GN_SHIFT = 6
IN_COLS = 3072
D_FF = 4096
CHUNK = 64
CHUNK_SHIFT = 6
N_CHIPS = 4
N_DEV = 8
RMS_EPS = 1e-6
GN_EPS = 1e-5
ADAM_LR, ADAM_B1, ADAM_B2, ADAM_EPS, ADAM_WD, ADAM_STEP = 0.001, 0.9, 0.999, 1e-08, 0.01, 10
TOK_TILE = 512
MIXIN_TILE = 1024
MIXB_TILE = 256
FFN_TILE = 1024
FFN_BLOCK = 512
GRAD_TILE = 2048
HALO = 32
SUB = 8
LANE = 128
CONV_ROWS = 128
MIB = 1 << 20
MESH = pl.DeviceIdType.MESH
OTHER_CHIPS = ((0, 1), (1, 0), (1, 1))


def _cp(sem=None, vmem_mib=48):
    return pltpu.CompilerParams(dimension_semantics=sem, vmem_limit_bytes=vmem_mib * MIB)


def _dot(a, b):
    return jnp.dot(a, b, preferred_element_type=F32)


def _dot_nt(a, b):
    return lax.dot_general(a, b, (((1,), (1,)), ((), ())), preferred_element_type=F32)


def _dot_tn(a, b):
    return lax.dot_general(a, b, (((0,), (0,)), ((), ())), preferred_element_type=F32)


def _sig(v):
    return 0.5 * jnp.tanh(0.5 * v) + 0.5


def _colsum(v):
    return jnp.sum(v, axis=0, keepdims=True)


def _flip(v, b):
    return 1 - v if b else v


def _rcopy(src, dst, ssem, rsem, dev):
    return pltpu.make_async_remote_copy(src_ref=src, dst_ref=dst, send_sem=ssem, recv_sem=rsem,
                                        device_id=dev, device_id_type=MESH)


def _place():
    return lax.axis_index("x"), lax.axis_index("y"), lax.axis_index("c")


def _full(shape):
    return pl.BlockSpec(shape, lambda *_: (0,) * len(shape))


def _big(shape, dtype):
    return pltpu.HBM(shape, dtype)


def _hbm(*arrays):
    out = [pltpu.with_memory_space_constraint(a, pltpu.HBM) for a in arrays]
    return out[0] if len(out) == 1 else out


def _split2(v):
    hi = v.astype(BF16)
    lo = (v - hi.astype(F32)).astype(BF16)
    return hi, lo


def _split3(v):
    h1 = v.astype(BF16)
    r1 = v - h1.astype(F32)
    h2 = r1.astype(BF16)
    h3 = (r1 - h2.astype(F32)).astype(BF16)
    return h1, h2, h3


def _mm3s(mat, parts):
    h1, h2, h3 = parts
    return _dot(mat, h1) + _dot(mat, h2) + _dot(mat, h3)


def _mm3(mat, v):
    return _mm3s(mat, _split3(v))


def _gn_matrix():
    r = lax.broadcasted_iota(jnp.int32, (CONV_CH, CONV_CH), 0) >> GN_SHIFT
    c = lax.broadcasted_iota(jnp.int32, (CONV_CH, CONV_CH), 1) >> GN_SHIFT
    return jnp.where(r == c, 1.0 / GN_GROUP, 0.0).astype(BF16)


def _gmean(v, gmat):
    hi, lo = _split2(v)
    return _dot(hi, gmat) + _dot(lo, gmat)


def _chunk_masks(tm):
    r = lax.broadcasted_iota(jnp.int32, (tm, tm), 0)
    c = lax.broadcasted_iota(jnp.int32, (tm, tm), 1)
    same = (r >> CHUNK_SHIFT) == (c >> CHUNK_SHIFT)
    one = lambda m: jnp.where(m, 1.0, 0.0).astype(BF16)
    return one(same & (c <= r)), one(same & (c >= r)), one(same)


def _tri():
    return lax.broadcasted_iota(jnp.int32, (CHUNK, CHUNK), 0) >= lax.broadcasted_iota(jnp.int32, (CHUNK, CHUNK), 1)


def _lower_bound(lbl_ref):
    l0, l1 = lbl_ref[0:1, :], lbl_ref[1:2, :]
    mx = jnp.maximum(l0, l1)
    e0, e1 = jnp.exp(l0 - mx), jnp.exp(l1 - mx)
    return e0 / (e0 + e1), e1 / (e0 + e1)


CONV_FWD_TAPS = tuple((j, HALO - (CONV_K - 1) + j) for j in range(CONV_K))
CONV_BWD_TAPS = tuple((j, (CONV_K - 1) - j) for j in range(CONV_K))


def _tap_conv(src_ref, w_ref, row0, taps, lanes):
    acc = None
    for b in range(SUB):
        pb = None
        for j, off in taps:
            if off % SUB == b:
                lo = row0 + off - b
                term = w_ref[j:j + 1, lanes] * src_ref[lo:lo + CONV_ROWS + SUB, lanes]
                pb = term if pb is None else pb + term
        if pb is not None:
            sh = pb[b:b + CONV_ROWS, :]
            acc = sh if acc is None else acc + sh
    return acc


def _hgrn_prep(pq, pf, lb, lower, same):
    sq = _sig(pq)
    qf = pq * sq
    sf = _sig(pf)
    f = lb + (1.0 - lb) * sf
    logf = jnp.log(f)
    k = 1.0 - f
    parts = _split3(logf)
    G = _mm3s(lower, parts)
    Gl = _mm3s(same, parts)
    eG, enG, eGlG = jnp.exp(G), jnp.exp(-G), jnp.exp(Gl - G)
    return dict(sq=sq, sf=sf, f=f, Gl=Gl, eG=eG, enG=enG, eGlG=eGlG, qt=qf * eG, kt=k * enG, kh=k * eGlG)


def _ada_exchange(c_row, w_ada, b_sh, wdw_pad):
    ncol = w_ada.shape[1]

    def body(c_ref, w_ref, b_ref, wdw_ref, call_ref, c8_ref, modg_ref, wdwg_ref, rows_s, sa, ra, sw, rw, sm, rm):
        x, y, c = _place()
        me = 4 * x + 2 * y + c
        chip = 2 * x + y
        cv = c_ref[...]
        call_ref[me] = cv * _sig(cv)
        wdwg_ref[chip] = wdw_ref[...]
        sends = []
        for m in range(1, N_DEV):
            peer = (_flip(x, m >> 2), _flip(y, (m >> 1) & 1), _flip(c, m & 1))
            cp = _rcopy(call_ref.at[me], call_ref.at[me], sa.at[m - 1], ra.at[m - 1], peer)
            cp.start()
            sends.append(cp)
        for k, (fx, fy) in enumerate(OTHER_CHIPS):
            peer = (_flip(x, fx), _flip(y, fy), c)
            cp = _rcopy(wdwg_ref.at[chip], wdwg_ref.at[chip], sw.at[k], rw.at[k], peer)
            cp.start()
            sends.append(cp)
        for m in range(1, N_DEV):
            peer = (_flip(x, m >> 2), _flip(y, (m >> 1) & 1), _flip(c, m & 1))
            pid = 4 * peer[0] + 2 * peer[1] + peer[2]
            _rcopy(call_ref.at[pid], call_ref.at[pid], sa.at[m - 1], ra.at[m - 1], peer).wait_recv()
        for b in range(N_DEV):
            c8_ref[b:b + 1, :] = call_ref[b]
        mod_all = _dot(c8_ref[...].astype(BF16), w_ref[...].astype(BF16)) + b_ref[...]
        for b in range(N_DEV):
            rows_s[b] = mod_all[b:b + 1, :]
        modg_ref[chip] = rows_s[me]
        for k, (fx, fy) in enumerate(OTHER_CHIPS):
            peer = (_flip(x, fx), _flip(y, fy), c)
            pid = 4 * peer[0] + 2 * peer[1] + peer[2]
            cp = _rcopy(rows_s.at[pid], modg_ref.at[chip], sm.at[k], rm.at[k], peer)
            cp.start()
            sends.append(cp)
        for k, (fx, fy) in enumerate(OTHER_CHIPS):
            peer = (_flip(x, fx), _flip(y, fy), c)
            pchip = 2 * peer[0] + peer[1]
            _rcopy(rows_s.at[0], modg_ref.at[pchip], sm.at[k], rm.at[k], peer).wait_recv()
            _rcopy(wdwg_ref.at[pchip], wdwg_ref.at[pchip], sw.at[k], rw.at[k], peer).wait_recv()
        for cp in sends:
            cp.wait_send()

    vm = pl.BlockSpec(memory_space=pltpu.VMEM)
    return pl.pallas_call(
        body, name="ada_exchange",
        out_shape=[jax.ShapeDtypeStruct((N_DEV, 1, D_MODEL), F32), jax.ShapeDtypeStruct((N_DEV, D_MODEL), F32),
                   jax.ShapeDtypeStruct((N_CHIPS, 1, ncol), F32), jax.ShapeDtypeStruct((N_CHIPS, HALO, HEAD_D), F32)],
        in_specs=[vm] * 4, out_specs=[vm] * 4,
        scratch_shapes=[pltpu.VMEM((N_DEV, 1, ncol), F32),
                        pltpu.SemaphoreType.DMA((N_DEV - 1,)), pltpu.SemaphoreType.DMA((N_DEV - 1,)),
                        pltpu.SemaphoreType.DMA((3,)), pltpu.SemaphoreType.DMA((3,)),
                        pltpu.SemaphoreType.DMA((3,)), pltpu.SemaphoreType.DMA((3,))],
        compiler_params=_cp(None, 32),
    )(c_row, w_ada, b_sh, wdw_pad)


def _cast_own(chip1, shard, name):
    rows, cols = shard.shape
    tr = _row_tile(rows)

    def body(ch_ref, s_ref, o_ref):
        o_ref[0] = s_ref[...].astype(BF16)

    gs = pltpu.PrefetchScalarGridSpec(
        num_scalar_prefetch=1, grid=(rows // tr,),
        in_specs=[pl.BlockSpec((tr, cols), lambda i, ch: (i, 0))],
        out_specs=pl.BlockSpec((1, tr, cols), lambda i, ch: (ch[0], i, 0)))
    return pl.pallas_call(
        body, name=name, grid_spec=gs, out_shape=_big((N_CHIPS, rows, cols), BF16),
        compiler_params=_cp(("arbitrary",), 32),
    )(chip1, _hbm(shard))


def _slab(buf, ch, core):
    hs = buf.shape[1] // 2
    return buf.at[ch, pl.ds(core * hs, hs), :]


def _gather_start(bufs, ssem, rsem, relations=(0, 1, 2)):
    x, y, c = _place()
    chip = 2 * x + y
    for k in relations:
        fx, fy = OTHER_CHIPS[k]
        peer = (_flip(x, fx), _flip(y, fy), c)
        for t, buf in enumerate(bufs):
            _rcopy(_slab(buf, chip, c), _slab(buf, chip, c), ssem.at[t * 3 + k], rsem.at[t * 3 + k], peer).start()


def _gather_finish(bufs, ssem, rsem):
    nt = len(bufs)
    x, y, c = _place()
    chip = 2 * x + y
    sibling = (x, y, 1 - c)
    for k, (fx, fy) in enumerate(OTHER_CHIPS):
        peer = (_flip(x, fx), _flip(y, fy), c)
        pchip = 2 * peer[0] + peer[1]
        for t, buf in enumerate(bufs):
            _rcopy(_slab(buf, pchip, c), _slab(buf, pchip, c), ssem.at[t * 3 + k], rsem.at[t * 3 + k], peer).wait_recv()
            _rcopy(_slab(buf, pchip, c), _slab(buf, pchip, c), ssem.at[3 * nt + t * 3 + k],
                   rsem.at[3 * nt + t * 3 + k], sibling).start()
    for k, (fx, fy) in enumerate(OTHER_CHIPS):
        peer = (_flip(x, fx), _flip(y, fy), c)
        pchip = 2 * peer[0] + peer[1]
        for t, buf in enumerate(bufs):
            _rcopy(_slab(buf, pchip, 1 - c), _slab(buf, pchip, 1 - c), ssem.at[3 * nt + t * 3 + k],
                   rsem.at[3 * nt + t * 3 + k], sibling).wait_recv()
            _rcopy(_slab(buf, chip, c), _slab(buf, chip, c), ssem.at[t * 3 + k], rsem.at[t * 3 + k], peer).wait_send()
            _rcopy(_slab(buf, pchip, c), _slab(buf, pchip, c), ssem.at[3 * nt + t * 3 + k],
                   rsem.at[3 * nt + t * 3 + k], sibling).wait_send()


def _gather_arrive(bufs, k, ssem, rsem):
    nt = len(bufs)
    x, y, c = _place()
    fx, fy = OTHER_CHIPS[k]
    peer = (_flip(x, fx), _flip(y, fy), c)
    pchip = 2 * peer[0] + peer[1]
    for t, buf in enumerate(bufs):
        _rcopy(_slab(buf, pchip, c), _slab(buf, pchip, c), ssem.at[t * 3 + k], rsem.at[t * 3 + k], peer).wait_recv()
        _rcopy(_slab(buf, pchip, c), _slab(buf, pchip, c), ssem.at[3 * nt + t * 3 + k],
               rsem.at[3 * nt + t * 3 + k], (x, y, 1 - c)).start()
    for t, buf in enumerate(bufs):
        _rcopy(_slab(buf, pchip, 1 - c), _slab(buf, pchip, 1 - c), ssem.at[3 * nt + t * 3 + k],
               rsem.at[3 * nt + t * 3 + k], (x, y, 1 - c)).wait_recv()


def _gather_sends_done(bufs, ssem, rsem):
    nt = len(bufs)
    x, y, c = _place()
    chip = 2 * x + y
    for k, (fx, fy) in enumerate(OTHER_CHIPS):
        peer = (_flip(x, fx), _flip(y, fy), c)
        pchip = 2 * peer[0] + peer[1]
        for t, buf in enumerate(bufs):
            _rcopy(_slab(buf, chip, c), _slab(buf, chip, c), ssem.at[t * 3 + k], rsem.at[t * 3 + k], peer).wait_send()
            _rcopy(_slab(buf, pchip, c), _slab(buf, pchip, c), ssem.at[3 * nt + t * 3 + k],
                   rsem.at[3 * nt + t * 3 + k], (x, y, 1 - c)).wait_send()


def _gather_sems(nt):
    return [pltpu.SemaphoreType.DMA((6 * nt,)), pltpu.SemaphoreType.DMA((6 * nt,))]


def _pair_copies(ins, outs, ssem, rsem):
    x, y, c = _place()
    copies = []
    for t in range(len(ins)):
        hs = ins[t].shape[1] // 2
        copies.append(_rcopy(ins[t].at[:, pl.ds((1 - c) * hs, hs), :], outs[t], ssem.at[t], rsem.at[t], (x, y, 1 - c)))
    return copies


def _pair_shapes(grads):
    return [_big((g.shape[0], g.shape[1] // 2, g.shape[2]), g.dtype) for g in grads]


def _pair_sems(nt):
    return [pltpu.SemaphoreType.DMA((nt,)), pltpu.SemaphoreType.DMA((nt,))]


def _pair_swap(grads, name):
    nt = len(grads)
    hbm = pl.BlockSpec(memory_space=pl.ANY)

    def body(*refs):
        copies = _pair_copies(refs[:nt], refs[nt:2 * nt], refs[2 * nt], refs[2 * nt + 1])
        for cp in copies:
            cp.start()
        for cp in copies:
            cp.wait_recv()
        for cp in copies:
            cp.wait_send()

    return pl.pallas_call(
        body, name=name, out_shape=_pair_shapes(grads), in_specs=[hbm] * nt, out_specs=[hbm] * nt,
        scratch_shapes=_pair_sems(nt),
    )(*[_hbm(g) for g in grads])


def _xchg_copies(ins, outs, ssem, rsem):
    x, y, c = _place()
    copies = []
    for k, (fx, fy) in enumerate(OTHER_CHIPS):
        peer = (_flip(x, fx), _flip(y, fy), c)
        for t in range(len(ins)):
            copies.append(_rcopy(ins[t].at[k], outs[t].at[k], ssem.at[t * 3 + k], rsem.at[t * 3 + k], peer))
    return copies


def _xchg_sems(nt):
    return [pltpu.SemaphoreType.DMA((3 * nt,)), pltpu.SemaphoreType.DMA((3 * nt,))]


def _final_exchange(fulls, stats):
    nt = len(fulls)
    rows, cols = stats.shape
    hbm = pl.BlockSpec(memory_space=pl.ANY)
    vm = pl.BlockSpec(memory_space=pltpu.VMEM)

    def body(*refs):
        ins, s_ref = refs[:nt], refs[nt]
        outs, g_ref = refs[nt + 1:2 * nt + 1], refs[2 * nt + 1]
        hssem, hrsem, ssem, rsem = refs[2 * nt + 2:]
        x, y, c = _place()
        me, sibling = (x, y, c), (x, y, 1 - c)
        halves = []
        for t in range(nt):
            hs = ins[t].shape[0] // 2
            mine = pl.ds(c * hs, hs)
            cp = _rcopy(ins[t].at[mine, :], outs[t].at[mine, :], hssem.at[t], hrsem.at[t], sibling)
            cp.start()
            halves.append(cp)

        chips = [(_flip(x, fx), _flip(y, fy)) for fx, fy in OTHER_CHIPS]

        def blk(px, py, pc):
            return g_ref.at[4 * px + 2 * py + pc]

        def copy(k, block, to, src=None):
            return _rcopy(blk(*block) if src is None else src, blk(*block), ssem.at[k], rsem.at[k], to)

        g_ref[4 * x + 2 * y + c] = s_ref[...]
        first = [copy(0, me, sibling, src=s_ref)]
        first += [copy(1 + j, me, (*chip, c), src=s_ref) for j, chip in enumerate(chips)]
        for cp in first:
            cp.start()
        passed = [copy(4 + j, (*chip, c), sibling) for j, chip in enumerate(chips)]
        for j, chip in enumerate(chips):
            copy(1 + j, (*chip, c), me).wait_recv()
            passed[j].start()
        copy(0, sibling, me).wait_recv()
        for j, chip in enumerate(chips):
            copy(4 + j, (*chip, 1 - c), me).wait_recv()
        for t in range(nt):
            hs = ins[t].shape[0] // 2
            other = pl.ds((1 - c) * hs, hs)
            _rcopy(ins[t].at[other, :], outs[t].at[other, :], hssem.at[t], hrsem.at[t], sibling).wait_recv()
        for cp in first + passed + halves:
            cp.wait_send()

    outs = pl.pallas_call(
        body, name="final_exchange",
        out_shape=[_big(f.shape, F32) for f in fulls] + [jax.ShapeDtypeStruct((N_DEV, rows, cols), F32)],
        in_specs=[hbm] * nt + [vm], out_specs=[hbm] * nt + [vm],
        input_output_aliases={t: t for t in range(nt)},
        scratch_shapes=[pltpu.SemaphoreType.DMA((nt,)), pltpu.SemaphoreType.DMA((nt,)),
                        pltpu.SemaphoreType.DMA((7,)), pltpu.SemaphoreType.DMA((7,))],
        compiler_params=_cp(None, 32),
    )(*[_hbm(f) for f in fulls], stats)
    return outs[:nt], outs[nt]


def _row_tile(rows):
    return min(rows, 512)


def _pair_sum(place, grad, got, name):
    nb, hs, cols = got.shape
    tr = _row_tile(hs)
    nr = hs // tr

    def body(pl_ref, g_ref, o_ref, pf_ref, pb_ref):
        j = pl.program_id(1)
        s = g_ref[0] + o_ref[0].astype(F32)

        @pl.when(j == 0)
        def _():
            pf_ref[...] = s

        @pl.when(j > 0)
        def _():
            pb_ref[0] = s.astype(BF16)

    gs = pltpu.PrefetchScalarGridSpec(
        num_scalar_prefetch=1, grid=(nr, nb),
        in_specs=[pl.BlockSpec((1, tr, cols), lambda i, j, p: (p[1] ^ j, p[0] * nr + i, 0)),
                  pl.BlockSpec((1, tr, cols), lambda i, j, p: (p[1] ^ j, i, 0))],
        out_specs=[pl.BlockSpec((tr, cols), lambda i, j, p: (i, 0)),
                   pl.BlockSpec((1, tr, cols), lambda i, j, p: (jnp.maximum(j - 1, 0), i, 0))])
    return pl.pallas_call(
        body, name=name, grid_spec=gs,
        out_shape=[_big((hs, cols), F32), _big((nb - 1, hs, cols), BF16)],
        compiler_params=_cp(("arbitrary", "arbitrary"), 32),
    )(place, *_hbm(grad, got))


def _chip_sum(place, pair_f, got_b, name):
    nb, hs, cols = got_b.shape
    tr = _row_tile(hs)
    nr = hs // tr

    def body(pl_ref, pf_ref, gb_ref, o_ref):
        acc = pf_ref[...]
        for k in range(nb):
            acc = acc + gb_ref[k].astype(F32)
        o_ref[...] = acc

    gs = pltpu.PrefetchScalarGridSpec(
        num_scalar_prefetch=1, grid=(nr,),
        in_specs=[pl.BlockSpec((tr, cols), lambda i, p: (i, 0)),
                  pl.BlockSpec((nb, tr, cols), lambda i, p: (0, i, 0))],
        out_specs=pl.BlockSpec((tr, cols), lambda i, p: (p[0] * nr + i, 0)))
    return pl.pallas_call(
        body, name=name, grid_spec=gs,
        out_shape=_big((2 * hs, cols), F32),
        compiler_params=_cp(("arbitrary",), 32),
    )(place, *_hbm(pair_f, got_b))


def _adam_math(w, g, m, v):
    m2 = ADAM_B1 * m + (1.0 - ADAM_B1) * g
    v2 = ADAM_B2 * v + (1.0 - ADAM_B2) * (g * g)
    m_hat = m2 / (1.0 - ADAM_B1 ** ADAM_STEP)
    v_hat = v2 / (1.0 - ADAM_B2 ** ADAM_STEP)
    delta = -ADAM_LR * (m_hat / (jnp.sqrt(v_hat) + ADAM_EPS) + ADAM_WD * w)
    return delta, m2, v2


def _adam_big(w, g, m, v, name):
    rows, cols = w.shape
    tr = _row_tile(rows)

    def body(w_ref, g_ref, m_ref, v_ref, d_out, m_out, v_out):
        d, m2, v2 = _adam_math(w_ref[...], g_ref[...], m_ref[...], v_ref[...])
        d_out[...] = d
        m_out[...] = m2
        v_out[...] = v2

    spec = pl.BlockSpec((tr, cols), lambda i: (i, 0))
    return pl.pallas_call(
        body, name=name, grid=(rows // tr,), in_specs=[spec] * 4, out_specs=[spec] * 3,
        out_shape=[_big(w.shape, F32)] * 3,
        compiler_params=_cp(("arbitrary",), 48),
    )(*_hbm(w, g, m, v))


def _ada_grad_adam(c8t, dmod_sh, w, m, v):
    rows, cols = w.shape
    tr = _row_tile(rows) // 2

    def body(ct_ref, dm_ref, w_ref, m_ref, v_ref, g_out, d_out, m_out, v_out):
        g = None
        for b in range(N_DEV):
            term = ct_ref[:, b:b + 1] * dm_ref[b:b + 1, :]
            g = term if g is None else g + term
        d, m2, v2 = _adam_math(w_ref[...], g, m_ref[...], v_ref[...])
        g_out[...] = g
        d_out[...] = d
        m_out[...] = m2
        v_out[...] = v2

    spec = pl.BlockSpec((tr, cols), lambda i: (i, 0))
    return pl.pallas_call(
        body, name="ada_grad_adam", grid=(rows // tr,),
        in_specs=[pl.BlockSpec((tr, N_DEV), lambda i: (i, 0)), _full((N_DEV, cols)), spec, spec, spec],
        out_specs=[spec] * 4, out_shape=[_big(w.shape, F32)] * 4,
        compiler_params=_cp(("arbitrary",), 32),
    )(c8t, dmod_sh, *_hbm(w, m, v))


def _tok_tile(t):
    return min(TOK_TILE, t)


def _mix_in_fwd(chip1, x, modr, g_pre, b_in4, w_in_buf, w_out_buf):
    T = x.shape[0]
    tm = min(MIXIN_TILE, T)
    nt = T // tm
    nb = IN_COLS // N_CHIPS

    def body(ch_ref, x_ref, mod_ref, g_ref, b_ref, win_in, wout_in, p_ref, h_ref, win_ref, wout_ref,
             h_all, wblk, lsem, is_sem, ir_sem, os_sem, or_sem):
        k, i = pl.program_id(0), pl.program_id(1)
        chip = ch_ref[0]

        def load_block(blk):
            cp = pltpu.make_async_copy(win_ref.at[blk], wblk, lsem)
            cp.start()
            cp.wait()

        @pl.when((k == 0) & (i == 0))
        def _():
            _gather_start([win_ref], is_sem, ir_sem, relations=(0, 1))
            load_block(chip)

        for r in range(N_CHIPS - 1):
            @pl.when((k == r + 1) & (i == 0))
            def _(r=r):
                _gather_arrive([win_ref], r, is_sem, ir_sem)
                if r == 0:
                    _gather_start([win_ref], is_sem, ir_sem, relations=(2,))
                if r == 1:
                    _gather_start([wout_ref], os_sem, or_sem)
                load_block(chip ^ (r + 1))

        rows = pl.ds(pl.multiple_of(i * tm, tm), tm)

        @pl.when(k == 0)
        def _():
            xv = x_ref[...]
            rstd = lax.rsqrt(jnp.mean(xv * xv, axis=-1, keepdims=True) + RMS_EPS)
            h = (xv * rstd) * g_ref[...] * (1.0 + mod_ref[1:2, :]) + mod_ref[0:1, :]
            hb = h.astype(BF16)
            h_ref[...] = hb
            h_all[rows, :] = hb

        p_ref[...] = _dot(h_all[rows, :], wblk[...]) + b_ref[chip ^ k]

        @pl.when((k == N_CHIPS - 1) & (i == nt - 1))
        def _():
            _gather_sends_done([win_ref], is_sem, ir_sem)
            _gather_finish([wout_ref], os_sem, or_sem)

    hbm = pl.BlockSpec(memory_space=pl.ANY)
    first_pass = lambda k, i, ch: (jnp.where(k == 0, i, nt - 1), 0)
    gs = pltpu.PrefetchScalarGridSpec(
        num_scalar_prefetch=1, grid=(N_CHIPS, nt),
        in_specs=[pl.BlockSpec((tm, D_MODEL), first_pass), pl.BlockSpec((6, D_MODEL), lambda k, i, ch: (0, 0)),
                  pl.BlockSpec((1, D_MODEL), lambda k, i, ch: (0, 0)),
                  pl.BlockSpec((N_CHIPS, 1, nb), lambda k, i, ch: (0, 0, 0)), hbm, hbm],
        out_specs=[pl.BlockSpec((tm, nb), lambda k, i, ch: (i, ch[0] ^ k)), pl.BlockSpec((tm, D_MODEL), first_pass),
                   hbm, hbm],
        scratch_shapes=[pltpu.VMEM((T, D_MODEL), BF16), pltpu.VMEM((D_MODEL, nb), BF16), pltpu.SemaphoreType.DMA]
        + _gather_sems(1) + _gather_sems(1))
    return pl.pallas_call(
        body, name="mix_in_fwd", grid_spec=gs,
        out_shape=[_big((T, IN_COLS), F32), _big((T, D_MODEL), BF16), _big(w_in_buf.shape, BF16),
                   _big(w_out_buf.shape, BF16)],
        input_output_aliases={5: 2, 6: 3},
        compiler_params=_cp(("arbitrary", "arbitrary"), 48),
    )(chip1, _hbm(x), modr, g_pre, b_in4, _hbm(w_in_buf), _hbm(w_out_buf))


def _mixers_fwd(p, wdw, vecs, lbl, gbufs):
    T = p.shape[0]
    tm = _tok_tile(T)
    nt = T // tm
    nch = tm // CHUNK
    ng = len(gbufs)

    def body(*refs):
        p_ref, wdw_ref, bdw_ref, gain_ref, bias_ref, gout_ref, lbl_ref = refs[:7]
        cat_ref, ys_ref, o_ref, st_ref = refs[7 + ng:11 + ng]
        gout_bufs = refs[11 + ng:11 + 2 * ng]
        ubuf, state, qt_s, kt_s, kh_s, v_s, egl_s, lower_s, same_s, gmat_s, gssem, grsem = refs[11 + 2 * ng:]
        i = pl.program_id(0)

        @pl.when(i == 0)
        def _():
            _gather_start(gout_bufs, gssem, grsem)
            lower_s[...], _, same_s[...] = _chunk_masks(tm)
            gmat_s[...] = _gn_matrix()
            state[...] = jnp.zeros(state.shape, F32)
            ubuf[0:HALO, :] = jnp.zeros((HALO, CONV_CH), F32)
            ubuf[HALO + tm:HALO + tm + SUB, :] = jnp.zeros((SUB, CONV_CH), F32)

        @pl.when(i > 0)
        def _():
            ubuf[0:HALO, :] = ubuf[tm:tm + HALO, :]

        ubuf[HALO:HALO + tm, :] = p_ref[:, 0:CONV_CH] * _sig(p_ref[:, CONV_CH:2 * CONV_CH])
        for r in range(tm // CONV_ROWS):
            rows = slice(r * CONV_ROWS, (r + 1) * CONV_ROWS)
            for lb_ in range(CONV_CH // LANE):
                lanes = slice(lb_ * LANE, (lb_ + 1) * LANE)
                ys_ref[rows, lanes] = bdw_ref[:, lanes] + _tap_conv(ubuf, wdw_ref, r * CONV_ROWS, CONV_FWD_TAPS, lanes)
        gmat = gmat_s[...]
        yv = ys_ref[...]
        d = yv - _gmean(yv, gmat)
        rs = lax.rsqrt(_gmean(d * d, gmat) + GN_EPS)
        z = d * rs * gain_ref[...] + bias_ref[...]
        cat_ref[:, 0:CONV_CH] = (z * _sig(z)).astype(BF16)

        lb, _ = _lower_bound(lbl_ref)
        lower, same = lower_s[...], same_s[...]
        o0 = 2 * CONV_CH
        pr = _hgrn_prep(p_ref[:, o0:o0 + HGRN_W], p_ref[:, o0 + HGRN_W:o0 + 2 * HGRN_W], lb, lower, same)
        qt_s[...] = pr["qt"].astype(BF16)
        kt_s[...] = pr["kt"].astype(BF16)
        kh_s[...] = pr["kh"].astype(BF16)
        v_s[...] = p_ref[:, o0 + 2 * HGRN_W:o0 + 3 * HGRN_W].astype(BF16)
        egl_s[...] = jnp.exp(pr["Gl"])
        tri = _tri()

        def chunk(ci, carry):
            r0 = pl.multiple_of(ci * CHUNK, CHUNK)
            rows = pl.ds(r0, CHUNK)
            for h in range(N_HEADS):
                ls = pl.ds(h * HEAD_D, HEAD_D)
                qc, kc, hc, vc = qt_s[rows, ls], kt_s[rows, ls], kh_s[rows, ls], v_s[rows, ls]
                s0 = state[h]
                s0b = s0.astype(BF16)
                st_ref[ci, h] = s0
                att =jnp.where(tri, _dot_nt(qc, kc), 0.0).astype(BF16)
                o_ref[rows, ls] = _dot(att, vc) + _dot_nt(qc, s0b)
                state[h] = s0 * egl_s[pl.ds(r0, 1), ls] + _dot_tn(vc, hc)
            return carry

        lax.fori_loop(0, nch, chunk, 0)
        for h in range(N_HEADS):
            sl = slice(h * HEAD_D, (h + 1) * HEAD_D)
            oh = o_ref[:, sl]
            gh = p_ref[:, o0 + 3 * HGRN_W + h * HEAD_D:o0 + 3 * HGRN_W + (h + 1) * HEAD_D]
            rsh = lax.rsqrt(jnp.mean(oh * oh, axis=-1, keepdims=True) + RMS_EPS)
            hg = (oh * rsh) * gout_ref[:, sl] * (gh * _sig(gh))
            cat_ref[:, CONV_CH + h * HEAD_D:CONV_CH + (h + 1) * HEAD_D] = hg.astype(BF16)

        @pl.when(i == nt - 1)
        def _():
            _gather_finish(gout_bufs, gssem, grsem)

    tile = lambda cols: pl.BlockSpec((tm, cols), lambda i: (i, 0))
    hbm = pl.BlockSpec(memory_space=pl.ANY)
    n_in = 7
    outs = pl.pallas_call(
        body, name="mixers_fwd", grid=(nt,),
        in_specs=[tile(IN_COLS), _full((HALO, CONV_CH))] + [_full((1, CONV_CH))] * 4 + [_full((2, HGRN_W))]
        + [hbm] * ng,
        out_specs=[tile(D_MODEL), tile(CONV_CH), tile(HGRN_W),
                   pl.BlockSpec((nch, N_HEADS, HEAD_D, HEAD_D), lambda i: (i, 0, 0, 0))] + [hbm] * ng,
        out_shape=[_big((T, D_MODEL), BF16), _big((T, CONV_CH), F32), _big((T, HGRN_W), F32),
                   _big((T // CHUNK, N_HEADS, HEAD_D, HEAD_D), F32)] + [_big(b.shape, BF16) for b in gbufs],
        input_output_aliases={n_in + t: 4 + t for t in range(ng)},
        scratch_shapes=[pltpu.VMEM((tm + HALO + SUB, CONV_CH), F32), pltpu.VMEM((N_HEADS, HEAD_D, HEAD_D), F32),
                        pltpu.VMEM((tm, HGRN_W), BF16), pltpu.VMEM((tm, HGRN_W), BF16),
                        pltpu.VMEM((tm, HGRN_W), BF16), pltpu.VMEM((tm, HGRN_W), BF16),
                        pltpu.VMEM((tm, HGRN_W), F32), pltpu.VMEM((tm, tm), BF16), pltpu.VMEM((tm, tm), BF16),
                        pltpu.VMEM((CONV_CH, CONV_CH), BF16)] + _gather_sems(ng),
        compiler_params=_cp(("arbitrary",), 56),
    )(_hbm(p), wdw, *vecs, lbl, *[_hbm(b) for b in gbufs])
    return outs[:4], outs[4:]


def _mix_out_fwd(cat, w_out, x, modr, g_post, g_ffn):
    T = x.shape[0]
    tm = _tok_tile(T)

    def body(cat_ref, w_ref, x_ref, mod_ref, gp_ref, gf_ref, y_ref, x1_ref, h2_ref):
        yv = _dot(cat_ref[...], w_ref[...])
        y_ref[...] = yv
        rs = lax.rsqrt(jnp.mean(yv * yv, axis=-1, keepdims=True) + RMS_EPS)
        x1 = x_ref[...] + mod_ref[2:3, :] * ((yv * rs) * gp_ref[...])
        x1_ref[...] = x1
        rs1 = lax.rsqrt(jnp.mean(x1 * x1, axis=-1, keepdims=True) + RMS_EPS)
        h2 = (x1 * rs1) * gf_ref[...] * (1.0 + mod_ref[4:5, :]) + mod_ref[3:4, :]
        h2_ref[...] = h2.astype(BF16)

    tile = pl.BlockSpec((tm, D_MODEL), lambda i: (i, 0))
    return pl.pallas_call(
        body, name="mix_out_fwd", grid=(T // tm,),
        in_specs=[tile, _full((D_MODEL, D_MODEL)), tile, _full((6, D_MODEL)), _full((1, D_MODEL)),
                  _full((1, D_MODEL))],
        out_specs=[tile, tile, tile],
        out_shape=[_big((T,D_MODEL), F32), _big((T,D_MODEL), F32),
                   _big((T,D_MODEL), BF16)],
        compiler_params=_cp(("arbitrary",), 40),
    )(*_hbm(cat, w_out, x), modr, g_post, g_ffn)


def _row_chains(rows, n=2):
    step = rows // n
    return [slice(k * step, (k + 1) * step) for k in range(n)]


def _ffn_blocks():
    return D_FF // FFN_BLOCK, (D_FF // N_CHIPS) // FFN_BLOCK


def _ffn_fwd(h2, w_up_g, w_down, x1, target, modr, g_post):
    T = h2.shape[0]
    tm = min(FFN_TILE, T)
    fb = FFN_BLOCK
    nj, per = _ffn_blocks()

    def body(h_ref, wu_ref, wd_ref, x1_ref, t_ref, mod_ref, g_ref, r_ref, dy2_ref, dx2_ref, st_ref, acc):
        i, j = pl.program_id(0), pl.program_id(1)

        @pl.when((i == 0) & (j == 0))
        def _():
            st_ref[...] = jnp.zeros(st_ref.shape, F32)

        @pl.when(j == 0)
        def _():
            acc[...] = jnp.zeros(acc.shape, F32)

        for rows in _row_chains(tm):
            ra = jnp.maximum(_dot(h_ref[rows, :], wu_ref[0]), 0.0)
            rb = (ra * ra).astype(BF16)
            r_ref[rows, :] = rb
            acc[rows, :] += _dot(rb, wd_ref[...])

        @pl.when(j == nj - 1)
        def _():
            y2 = acc[...]
            rs = lax.rsqrt(jnp.mean(y2 * y2, axis=-1, keepdims=True) + RMS_EPS)
            nh = y2 * rs
            gp = g_ref[...]
            err = x1_ref[...] + mod_ref[5:6, :] * (nh * gp) - t_ref[...]
            dx2 = err * (1.0 / D_MODEL)
            dx2_ref[...] = dx2
            st_ref[0:1, :] += _colsum(err * err)
            st_ref[1:2, :] += _colsum(dx2 * (nh * gp))
            dn = dx2 * mod_ref[5:6, :]
            st_ref[2:3, :] += _colsum(dn * nh)
            dy2_ref[...] = _rms_bwd(dn * gp, nh, rs).astype(BF16)

    tile = pl.BlockSpec((tm, D_MODEL), lambda i, j: (i, 0))
    return pl.pallas_call(
        body, name="ffn_fwd", grid=(T // tm, nj),
        in_specs=[tile, pl.BlockSpec((1, D_MODEL, fb), lambda i, j: (j // per, 0, j % per)),
                  pl.BlockSpec((fb, D_MODEL), lambda i, j: (j, 0)), tile, tile,
                  _full((6, D_MODEL)), _full((1, D_MODEL))],
        out_specs=[pl.BlockSpec((tm, fb), lambda i, j: (i, j)), tile, tile, _full((8, D_MODEL))],
        out_shape=[_big((T, D_FF), BF16), _big((T, D_MODEL), BF16), _big((T, D_MODEL), F32),
                   jax.ShapeDtypeStruct((8, D_MODEL), F32)],
        scratch_shapes=[pltpu.VMEM((tm, D_MODEL), F32)],
        compiler_params=_cp(("arbitrary", "arbitrary"), 56),
    )(*_hbm(h2, w_up_g, w_down, x1, target), modr, g_post)


def _rms_bwd(dxn, xn, rs):
    return rs * (dxn - xn * jnp.mean(dxn * xn, axis=-1, keepdims=True))


def _ffn_bwd(dy2, r, x1, dx2, w_up_g, w_down, modr, g_ffn):
    T = dx2.shape[0]
    tm = min(FFN_TILE, T)
    fb = FFN_BLOCK
    nj, per = _ffn_blocks()

    def body(dy2_ref, r_ref, x1_ref, dx2_ref, wu_ref, wd_ref, mod_ref, gf_ref, da_ref, dx1_ref, st_ref, dh_s):
        i, j = pl.program_id(0), pl.program_id(1)

        @pl.when((i == 0) & (j == 0))
        def _():
            st_ref[...] = jnp.zeros(st_ref.shape, F32)

        @pl.when(j == 0)
        def _():
            dh_s[...] = jnp.zeros(dh_s.shape, F32)

        for rows in _row_chains(tm):
            ra = jnp.sqrt(r_ref[rows, :].astype(F32))
            da = (_dot_nt(dy2_ref[rows, :], wd_ref[...]) * (2.0 * ra)).astype(BF16)
            da_ref[rows, :] = da
            dh_s[rows, :] += _dot_nt(da, wu_ref[0])

        @pl.when(j == nj - 1)
        def _():
            dh = dh_s[...]
            x1v = x1_ref[...]
            rs1 = lax.rsqrt(jnp.mean(x1v * x1v, axis=-1, keepdims=True) + RMS_EPS)
            xn = x1v * rs1
            st_ref[0:1, :] += _colsum(dh)
            st_ref[1:2, :] += _colsum(dh * (xn * gf_ref[...]))
            dsc = dh * (1.0 + mod_ref[4:5, :])
            st_ref[2:3, :] += _colsum(dsc * xn)
            dx1_ref[...] = dx2_ref[...] + _rms_bwd(dsc * gf_ref[...], xn, rs1)

    tile = pl.BlockSpec((tm, D_MODEL), lambda i, j: (i, 0))
    ftile = pl.BlockSpec((tm, fb), lambda i, j: (i, j))
    return pl.pallas_call(
        body, name="ffn_bwd", grid=(T // tm, nj),
        in_specs=[tile, ftile, tile, tile, pl.BlockSpec((1, D_MODEL, fb), lambda i, j: (j // per, 0, j % per)),
                  pl.BlockSpec((fb, D_MODEL), lambda i, j: (j, 0)), _full((6, D_MODEL)), _full((1, D_MODEL))],
        out_specs=[ftile, tile, _full((8, D_MODEL))],
        out_shape=[_big((T, D_FF), BF16), _big((T, D_MODEL), F32), jax.ShapeDtypeStruct((8, D_MODEL), F32)],
        scratch_shapes=[pltpu.VMEM((tm, D_MODEL), F32)],
        compiler_params=_cp(("arbitrary", "arbitrary"), 56),
    )(*_hbm(dy2, r, x1, dx2, w_up_g, w_down), modr, g_ffn)


def _mix_out_bwd(dx1, y, cat, w_out, modr, g_post, swap):
    T = dx1.shape[0]
    tm = _tok_tile(T)
    nt = T // tm
    ns = len(swap)

    def body(*refs):
        dx1_ref, y_ref, cat_ref, w_ref, mod_ref, gp_ref = refs[:6]
        s_ins = refs[6:6 + ns]
        dcat_ref, st_ref, gw_ref = refs[6 + ns:9 + ns]
        s_outs = refs[9 + ns:9 + 2 * ns]
        gacc, gsem, pssem, prsem = refs[9 + 2 * ns:]
        i = pl.program_id(0)

        @pl.when(i == 0)
        def _():
            for cp in _pair_copies(s_ins, s_outs, pssem, prsem):
                cp.start()
            st_ref[...] = jnp.zeros(st_ref.shape, F32)
            gacc[...] = jnp.zeros(gacc.shape, F32)

        dxv, yv = dx1_ref[...], y_ref[...]
        rs = lax.rsqrt(jnp.mean(yv * yv, axis=-1, keepdims=True) + RMS_EPS)
        nh = yv * rs
        st_ref[0:1, :] += _colsum(dxv * (nh * gp_ref[...]))
        dn = dxv * mod_ref[2:3, :]
        st_ref[1:2, :] += _colsum(dn * nh)
        dy = _rms_bwd(dn * gp_ref[...], nh, rs).astype(BF16)
        dcat_ref[...] = _dot_nt(dy, w_ref[...])
        for cols in _row_chains(D_MODEL):
            gacc[:, cols] += _dot_tn(cat_ref[...], dy[:, cols])

        @pl.when(i == nt - 1)
        def _():
            out = pltpu.make_async_copy(gacc, gw_ref, gsem)
            out.start()
            copies = _pair_copies(s_ins, s_outs, pssem, prsem)
            for cp in copies:
                cp.wait_recv()
            for cp in copies:
                cp.wait_send()
            out.wait()

    tile = pl.BlockSpec((tm, D_MODEL), lambda i: (i, 0))
    hbm = pl.BlockSpec(memory_space=pl.ANY)
    outs = pl.pallas_call(
        body, name="mix_out_bwd", grid=(nt,),
        in_specs=[tile, tile, tile, _full((D_MODEL, D_MODEL)), _full((6, D_MODEL)), _full((1, D_MODEL))]
        + [hbm] * ns,
        out_specs=[tile, _full((8, D_MODEL)), hbm] + [hbm] * ns,
        out_shape=[_big((T, D_MODEL), F32), jax.ShapeDtypeStruct((8, D_MODEL), F32), _big((D_MODEL, D_MODEL), F32)]
        + _pair_shapes(swap),
        scratch_shapes=[pltpu.VMEM((D_MODEL, D_MODEL), F32), pltpu.SemaphoreType.DMA] + _pair_sems(ns),
        compiler_params=_cp(("arbitrary",), 48),
    )(*_hbm(dx1, y, cat, w_out), modr, g_post, *[_hbm(g) for g in swap])
    return outs[:3], outs[3:]


def _mixers_bwd(p, dcat, ys, o, states, h1, wdw, vecs, lbl, pairs_b):
    T = p.shape[0]
    tm = min(MIXB_TILE, T)
    nt = T // tm
    nch = tm // CHUNK
    hpt = tm // HALO
    nx = len(pairs_b)
    nb = IN_COLS // N_CHIPS

    def body(*refs):
        (p_ref, ph_ref, dcat_ref, ys_ref, o_ref, st_ref, h1_ref, wdw_ref, bdw_ref, gain_ref, bias_ref, gout_ref,
         lbl_ref) = refs[:13]
        x_ins = refs[13:13 + nx]
        dp_ref, sb_ref, s5_ref, dw_ref = refs[13 + nx:17 + nx]
        x_outs = refs[17 + nx:17 + 2 * nx]
        gin_ref = refs[17 + 2 * nx]
        (ubuf, dybuf, carry, dstate, qt_s, kt_s, kh_s, v_s, do_s, egl_s, dqt_s, dkt_s, dkh_s, dv_s, dgl_s,
         dsh, dw8, dshift, lower_s, upper_s, same_s, gmat_s, gacc, dp_prev, h1_prev, gsem, xssem,
         xrsem) = refs[18 + 2 * nx:]
        i = pl.program_id(0)
        tile_idx = nt - 1 - i

        @pl.when(i == 0)
        def _():
            for cp in _xchg_copies(x_ins, x_outs, xssem, xrsem):
                cp.start()
            gacc[...] = jnp.zeros(gacc.shape, F32)
            dstate[...] = jnp.zeros(dstate.shape, F32)
            carry[...] = jnp.zeros(carry.shape, F32)
            sb_ref[...] = jnp.zeros(sb_ref.shape, F32)
            s5_ref[...] = jnp.zeros(s5_ref.shape, F32)
            dw_ref[...] = jnp.zeros(dw_ref.shape, F32)
            dw8[...] = jnp.zeros(dw8.shape, F32)
            lower_s[...], upper_s[...], same_s[...] = _chunk_masks(tm)
            gmat_s[...] = _gn_matrix()
            dsh[0:SUB, :] = jnp.zeros((SUB, CONV_CH), F32)
            dsh[SUB + tm:2 * SUB + tm, :] = jnp.zeros((SUB, CONV_CH), F32)
            ubuf[HALO + tm:HALO + tm + SUB, :] = jnp.zeros((SUB, CONV_CH), F32)
            dp_prev[...] = jnp.zeros(dp_prev.shape, BF16)
            h1_prev[...] = jnp.zeros(h1_prev.shape, BF16)

        n_pieces = (tm // CONV_ROWS) * (CONV_CH // LANE)
        per_block = n_pieces // N_CHIPS
        prow = D_MODEL // per_block

        def w_in_grad_piece(k):
            j, part = k // per_block, k % per_block
            rows_k = slice(part * prow, (part + 1) * prow)
            gacc[j, rows_k, :] += _dot_tn(h1_prev[:, rows_k], dp_prev[:, j * nb:(j + 1) * nb])

        uh = ph_ref[:, 0:CONV_CH] * _sig(ph_ref[:, CONV_CH:2 * CONV_CH])
        ubuf[0:HALO, :] = jnp.where(tile_idx > 0, uh, 0.0)
        ubuf[HALO:HALO + tm, :] = p_ref[:, 0:CONV_CH] * _sig(p_ref[:, CONV_CH:2 * CONV_CH])
        gmat = gmat_s[...]
        gain = gain_ref[...]
        yv = ys_ref[...]
        d = yv - _gmean(yv, gmat)
        rs = lax.rsqrt(_gmean(d * d, gmat) + GN_EPS)
        yn = d * rs
        z = yn * gain + bias_ref[...]
        sz = _sig(z)
        dz = dcat_ref[:, 0:CONV_CH] * (sz * (1.0 + z * (1.0 - sz)))
        dyn = dz * gain
        dyc = rs * (dyn - _gmean(dyn, gmat) - yn * _gmean(dyn * yn, gmat))
        s5_ref[0:1, :] += _colsum(dyc)
        s5_ref[1:2, :] += _colsum(dz * yn)
        s5_ref[2:3, :] += _colsum(dz)
        dybuf[tm:tm + HALO, :] = carry[...]
        dybuf[0:tm, :] = dyc
        dsh[SUB:SUB + tm, :] = dyc
        carry[...] = dyc[0:HALO, :]
        for b in range(SUB):
            dshift[...] = dsh[SUB - b:2 * SUB - b + tm, :]
            for j, off in CONV_FWD_TAPS:
                if off % SUB == b:
                    prod = dshift[...] * ubuf[off - b:off - b + tm + SUB, :]
                    dw8[j] += jnp.sum(prod.reshape((tm + SUB) // SUB, SUB, CONV_CH), axis=0)
        for r in range(tm // CONV_ROWS):
            rows = slice(r * CONV_ROWS, (r + 1) * CONV_ROWS)
            for lb_ in range(CONV_CH // LANE):
                lanes = slice(lb_ * LANE, (lb_ + 1) * LANE)
                glanes = slice(CONV_CH + lb_ * LANE, CONV_CH + (lb_ + 1) * LANE)
                w_in_grad_piece(r * (CONV_CH // LANE) + lb_)
                acc = _tap_conv(dybuf, wdw_ref, r * CONV_ROWS, CONV_BWD_TAPS, lanes)
                val = p_ref[rows, lanes]
                sg = _sig(p_ref[rows, glanes])
                dval = acc * sg
                dgate = acc * val * (sg * (1.0 - sg))
                dp_ref[rows, lanes] = dval.astype(BF16)
                dp_ref[rows, glanes] = dgate.astype(BF16)
                sb_ref[0:1, lanes] += _colsum(dval)
                sb_ref[0:1, glanes] += _colsum(dgate)

        o0 = 2 * CONV_CH
        for h in range(N_HEADS):
            sl = slice(h * HEAD_D, (h + 1) * HEAD_D)
            gsl = slice(o0 + 3 * HGRN_W + h * HEAD_D, o0 + 3 * HGRN_W + (h + 1) * HEAD_D)
            oh = o_ref[:, sl]
            gh = p_ref[:, gsl]
            dh = dcat_ref[:, CONV_CH + h * HEAD_D:CONV_CH + (h + 1) * HEAD_D]
            gout = gout_ref[:, sl]
            rsh = lax.rsqrt(jnp.mean(oh * oh, axis=-1, keepdims=True) + RMS_EPS)
            on = oh * rsh
            sgg = _sig(gh)
            dgh = dh * (on * gout) * (sgg * (1.0 + gh * (1.0 - sgg)))
            dm = dh * (gh * sgg)
            s5_ref[3:4, sl] += _colsum(dm * on)
            do_s[:, sl] = _rms_bwd(dm * gout, on, rsh).astype(BF16)
            dp_ref[:, gsl] = dgh.astype(BF16)
            sb_ref[2:3, CONV_CH + h * HEAD_D:CONV_CH + (h + 1) * HEAD_D] += _colsum(dgh)

        lb, _ = _lower_bound(lbl_ref)
        lower, upper, same = lower_s[...], upper_s[...], same_s[...]
        pq = p_ref[:, o0:o0 + HGRN_W]
        pr = _hgrn_prep(pq, p_ref[:, o0 + HGRN_W:o0 + 2 * HGRN_W], lb, lower, same)
        qt_s[...] = pr["qt"].astype(BF16)
        kt_s[...] = pr["kt"].astype(BF16)
        kh_s[...] = pr["kh"].astype(BF16)
        v_s[...] = p_ref[:, o0 + 2 * HGRN_W:o0 + 3 * HGRN_W].astype(BF16)
        egl_s[...] = jnp.exp(pr["Gl"])
        tri = _tri()

        def chunk(it, c_):
            ci = nch - 1 - it
            r0 = pl.multiple_of(ci * CHUNK, CHUNK)
            rows = pl.ds(r0, CHUNK)
            for h in range(N_HEADS):
                ls = pl.ds(h * HEAD_D, HEAD_D)
                qc, kc, hc, vc = qt_s[rows, ls], kt_s[rows, ls], kh_s[rows, ls], v_s[rows, ls]
                dob = do_s[rows, ls]
                s0 = st_ref[ci, h]
                s0b = s0.astype(BF16)
                ds1 = dstate[h]
                ds1b = ds1.astype(BF16)
                egl = egl_s[pl.ds(r0, 1), ls]
                att = jnp.where(tri, _dot_nt(qc, kc), 0.0).astype(BF16)
                datt = jnp.where(tri, _dot_nt(dob, vc), 0.0).astype(BF16)
                dv_s[rows, ls] = _dot_tn(att, dob) + _dot_nt(hc, ds1b)
                dqt_s[rows, ls] = _dot(datt, kc) + _dot(dob, s0b)
                dkt_s[rows, ls] = _dot_tn(datt, qc)
                dkh_s[rows, ls] = _dot(vc, ds1b)
                dgl = egl * _colsum(ds1 * s0)
                dgl_s[rows, ls] = jnp.broadcast_to(dgl, (CHUNK, HEAD_D))
                dstate[h] = ds1 * egl + _dot_tn(dob, qc)
            return c_

        lax.fori_loop(0, nch, chunk, 0)
        dqt, dkt, dkh = dqt_s[...], dkt_s[...], dkh_s[...]
        dk = dkt * pr["enG"] + dkh * pr["eGlG"]
        khk = dkh * kh_s[...].astype(F32)
        dG = dqt * qt_s[...].astype(F32) - dkt * kt_s[...].astype(F32) - khk
        dlogf = _mm3(upper, dG) + _mm3(same, khk) + dgl_s[...]
        df = dlogf / pr["f"] - dk
        sf, sq = pr["sf"], pr["sq"]
        s5_ref[4:5, :] += _colsum(df * (1.0 - sf))
        dfl = df * (1.0 - lb) * (sf * (1.0 - sf))
        dq = (dqt * pr["eG"]) * (sq * (1.0 + pq * (1.0 - sq)))
        dvv = dv_s[...]
        dp_ref[:, o0:o0 + HGRN_W] = dq.astype(BF16)
        dp_ref[:, o0 + HGRN_W:o0 + 2 * HGRN_W] = dfl.astype(BF16)
        dp_ref[:, o0 + 2 * HGRN_W:o0 + 3 * HGRN_W] = dvv.astype(BF16)
        sb_ref[1:2, 0:HGRN_W] += _colsum(dq)
        sb_ref[1:2, HGRN_W:2 * HGRN_W] += _colsum(dfl)
        sb_ref[2:3, 0:HGRN_W] += _colsum(dvv)

        dp_prev[...] = dp_ref[...]
        h1_prev[...] = h1_ref[...]

        @pl.when(i == nt - 1)
        def _():
            for k in range(n_pieces):
                w_in_grad_piece(k)
            out = pltpu.make_async_copy(gacc, gin_ref, gsem)
            out.start()
            for j in range(CONV_K):
                dw_ref[j:j + 1, :] = _colsum(dw8[j])
            copies = _xchg_copies(x_ins, x_outs, xssem, xrsem)
            for cp in copies:
                cp.wait_recv()
            for cp in copies:
                cp.wait_send()
            out.wait()

    rev = lambda cols: pl.BlockSpec((tm, cols), lambda i: (nt - 1 - i, 0))
    halo = pl.BlockSpec((HALO, 2 * CONV_CH), lambda i: (jnp.maximum((nt - 1 - i) * hpt - 1, 0), 0))
    wide = lambda n: pltpu.VMEM((tm, HGRN_W), n)
    hbm = pl.BlockSpec(memory_space=pl.ANY)
    outs = pl.pallas_call(
        body, name="mixers_bwd", grid=(nt,),
        in_specs=[rev(IN_COLS), halo, rev(D_MODEL), rev(CONV_CH), rev(HGRN_W),
                  pl.BlockSpec((nch, N_HEADS, HEAD_D, HEAD_D), lambda i: (nt - 1 - i, 0, 0, 0)), rev(D_MODEL),
                  _full((HALO, CONV_CH))] + [_full((1, CONV_CH))] * 4 + [_full((2, HGRN_W))] + [hbm] * nx,
        out_specs=[rev(IN_COLS), _full((8, D_MODEL)), _full((8, CONV_CH)), _full((HALO, CONV_CH))]
        + [hbm] * (nx + 1),
        out_shape=[_big((T, IN_COLS), BF16), jax.ShapeDtypeStruct((8, D_MODEL), F32),
                   jax.ShapeDtypeStruct((8, CONV_CH), F32), jax.ShapeDtypeStruct((HALO, CONV_CH), F32)]
        + [_big(pb.shape, BF16) for pb in pairs_b] + [_big((N_CHIPS, D_MODEL, nb), F32)],
        scratch_shapes=[pltpu.VMEM((tm + HALO + SUB, CONV_CH), F32), pltpu.VMEM((tm + HALO, CONV_CH), F32),
                        pltpu.VMEM((HALO, CONV_CH), F32), pltpu.VMEM((N_HEADS, HEAD_D, HEAD_D), F32),
                        wide(BF16), wide(BF16), wide(BF16), wide(BF16), wide(BF16),
                        wide(F32), wide(F32), wide(F32), wide(F32), wide(F32), wide(F32),
                        pltpu.VMEM((tm + 2 * SUB, CONV_CH), F32), pltpu.VMEM((HALO, SUB, CONV_CH), F32),
                        pltpu.VMEM((tm + SUB, CONV_CH), F32), pltpu.VMEM((tm, tm), BF16), pltpu.VMEM((tm, tm), BF16),
                        pltpu.VMEM((tm, tm), BF16), pltpu.VMEM((CONV_CH, CONV_CH), BF16),
                        pltpu.VMEM((N_CHIPS, D_MODEL, nb), F32), pltpu.VMEM((tm, IN_COLS), BF16),
                        pltpu.VMEM((tm, D_MODEL), BF16), pltpu.SemaphoreType.DMA]
        + _xchg_sems(nx),
        compiler_params=_cp(("arbitrary",), 56),
    )(*_hbm(p, p, dcat, ys, o, states, h1), wdw, *vecs, lbl, *[_hbm(pb) for pb in pairs_b])
    return outs[:4], outs[4:4 + nx], outs[4 + nx]


def _mix_in_bwd(dp, w_in_g, x, dx1, modr, g_pre, pairs_b):
    T = x.shape[0]
    tm = _tok_tile(T)
    nt = T // tm
    nb = IN_COLS // N_CHIPS
    nx = len(pairs_b)

    def body(*refs):
        dp_ref, w_ref, x_ref, dx1_ref, mod_ref, g_ref = refs[:6]
        x_ins = refs[6:6 + nx]
        gx_ref, st_ref = refs[6 + nx:8 + nx]
        x_outs = refs[8 + nx:8 + 2 * nx]
        xssem, xrsem = refs[8 + 2 * nx:]
        i = pl.program_id(0)

        @pl.when(i == 0)
        def _():
            for cp in _xchg_copies(x_ins, x_outs, xssem, xrsem):
                cp.start()
            st_ref[...] = jnp.zeros(st_ref.shape, F32)

        dh = None
        for j in range(N_CHIPS):
            part = _dot_nt(dp_ref[:, j * nb:(j + 1) * nb], w_ref[j])
            dh = part if dh is None else dh + part
        xv = x_ref[...]
        rs = lax.rsqrt(jnp.mean(xv * xv, axis=-1, keepdims=True) + RMS_EPS)
        xn = xv * rs
        st_ref[0:1, :] += _colsum(dh)
        st_ref[1:2, :] += _colsum(dh * (xn * g_ref[...]))
        dsc = dh * (1.0 + mod_ref[1:2, :])
        st_ref[2:3, :] += _colsum(dsc * xn)
        gx_ref[...] = dx1_ref[...] + _rms_bwd(dsc * g_ref[...], xn, rs)

        @pl.when(i == nt - 1)
        def _():
            copies = _xchg_copies(x_ins, x_outs, xssem, xrsem)
            for cp in copies:
                cp.wait_recv()
            for cp in copies:
                cp.wait_send()

    tile = pl.BlockSpec((tm, D_MODEL), lambda i: (i, 0))
    hbm = pl.BlockSpec(memory_space=pl.ANY)
    outs = pl.pallas_call(
        body, name="mix_in_bwd", grid=(nt,),
        in_specs=[pl.BlockSpec((tm, IN_COLS), lambda i: (i, 0)), _full((N_CHIPS, D_MODEL, nb)), tile, tile,
                  _full((6, D_MODEL)), _full((1, D_MODEL))] + [hbm] * nx,
        out_specs=[tile, _full((8, D_MODEL))] + [hbm] * nx,
        out_shape=[_big((T, D_MODEL), F32), jax.ShapeDtypeStruct((8, D_MODEL), F32)]
        + [_big(pb.shape, BF16) for pb in pairs_b],
        scratch_shapes=_xchg_sems(nx),
        compiler_params=_cp(("arbitrary",), 48),
    )(*_hbm(dp, w_in_g, x, dx1), modr, g_pre, *[_hbm(pb) for pb in pairs_b])
    return outs[:2], outs[2:]


def _weight_grad(a, b, a_blocked, b_blocked, name):
    T = a.shape[0]
    tt = min(GRAD_TILE, T)
    nt = T // tt
    ka = a.shape[1] // N_CHIPS if a_blocked else a.shape[1]
    nb = b.shape[1] // N_CHIPS if b_blocked else b.shape[1]

    def body(a_ref, b_ref, o_ref, ob_ref):
        t = pl.program_id(1)

        @pl.when(t == 0)
        def _():
            o_ref[...] = jnp.zeros(o_ref.shape, F32)

        for cols in _row_chains(nb):
            o_ref[0, :, cols] += _dot_tn(a_ref[...], b_ref[:, cols])

        @pl.when(t == nt - 1)
        def _():
            ob_ref[0] = o_ref[0].astype(BF16)

    blk = pl.BlockSpec((1, ka, nb), lambda j, t: (j, 0, 0))
    return pl.pallas_call(
        body, name=name, grid=(N_CHIPS, nt),
        in_specs=[pl.BlockSpec((tt, ka), (lambda j, t: (t, j)) if a_blocked else (lambda j, t: (t, 0))),
                  pl.BlockSpec((tt, nb), (lambda j, t: (t, j)) if b_blocked else (lambda j, t: (t, 0)))],
        out_specs=[blk, blk],
        out_shape=[_big((N_CHIPS, ka, nb), F32), _big((N_CHIPS, ka, nb), BF16)],
        compiler_params=_cp(("arbitrary", "arbitrary"), 48),
    )(*_hbm(a, b))


R_LOSS = 0
R_FFN = 8
R_OUT = 16
R_IN = 24
R_BIN = 32
R_512 = 40
R_DW = 48
N_STAT_ROWS = 80
MOD_ROWS = (R_IN + 0, R_IN + 1, R_OUT + 0, R_FFN + 0, R_FFN + 1, R_LOSS + 1)


def _small_update(gath, params):
    names = ["b_ada", "lb_logits", "g_pre_mix", "b_in", "b_dw", "gn_gain", "gn_bias", "g_hgrn_out", "g_post_mix",
             "g_pre_ffn", "g_post_ffn"]
    flat = []
    for n in names:
        flat += list(params[n])
    n_in = 1 + len(flat)

    def body(*refs):
        g_ref = refs[0]
        prm = {n: refs[1 + 3 * k:4 + 3 * k] for k, n in enumerate(names)}
        outs = refs[n_in:]
        loss_ref, dmod_ref, dwdw_ref = outs[0], outs[1], outs[2]
        res = {n: outs[3 + 4 * k:7 + 4 * k] for k, n in enumerate(names)}
        red = g_ref[0]
        for dev in range(1, N_DEV):
            red = red + g_ref[dev]
        loss_ref[...] = jnp.broadcast_to(
            (0.5 / D_MODEL) * jnp.sum(red[R_LOSS:R_LOSS + 1, :], axis=-1, keepdims=True), loss_ref.shape)
        for dev in range(N_DEV):
            for k, r in enumerate(MOD_ROWS):
                dmod_ref[dev:dev + 1, k * D_MODEL:(k + 1) * D_MODEL] = g_ref[dev, r:r + 1, :]
        dwdw_ref[...] = red[R_DW:R_DW + HALO, 0:CONV_CH]

        def finish(name, pieces):
            w_ref, m_ref, v_ref = prm[name]
            g_out, d_out, m_out, v_out = res[name]
            for rsl, lsl, g in pieces:
                d, m2, v2 = _adam_math(w_ref[rsl, lsl], g, m_ref[rsl, lsl], v_ref[rsl, lsl])
                g_out[rsl, lsl] = g
                d_out[rsl, lsl] = d
                m_out[rsl, lsl] = m2
                v_out[rsl, lsl] = v2

        one = slice(0, 1)
        row = lambda r: red[r:r + 1, :]
        half = lambda r: red[r:r + 1, 0:CONV_CH]
        finish("b_ada", [(one, slice(k * D_MODEL, (k + 1) * D_MODEL), row(r)) for k, r in enumerate(MOD_ROWS)])
        finish("b_in", [(one, slice(k * D_MODEL, (k + 1) * D_MODEL), row(R_BIN + k)) for k in range(3)])
        finish("g_pre_mix", [(one, slice(None), row(R_IN + 2))])
        finish("g_post_mix", [(one, slice(None), row(R_OUT + 1))])
        finish("g_pre_ffn", [(one, slice(None), row(R_FFN + 2))])
        finish("g_post_ffn", [(one, slice(None), row(R_LOSS + 2))])
        finish("b_dw", [(one, slice(None), half(R_512 + 0))])
        finish("gn_gain", [(one, slice(None), half(R_512 + 1))])
        finish("gn_bias", [(one, slice(None), half(R_512 + 2))])
        finish("g_hgrn_out", [(one, slice(None), half(R_512 + 3))])
        s0, s1 = _lower_bound(prm["lb_logits"][0])
        dlb = half(R_512 + 4)
        finish("lb_logits", [(slice(0, 1), slice(None), dlb * s0 * (1.0 - s0)),
                             (slice(1, 2), slice(None), -dlb * s0 * s1)])

    vm = pl.BlockSpec(memory_space=pltpu.VMEM)
    out_shape = [jax.ShapeDtypeStruct((8, 128), F32), jax.ShapeDtypeStruct((N_DEV, 6 * D_MODEL), F32),
                 jax.ShapeDtypeStruct((HALO, CONV_CH), F32)]
    for n in names:
        out_shape += [jax.ShapeDtypeStruct(params[n][0].shape, F32)] * 4
    outs = pl.pallas_call(
        body, name="small_update", out_shape=out_shape,
        in_specs=[vm] * n_in, out_specs=[vm] * len(out_shape),
        compiler_params=_cp(None, 32),
    )(gath, *flat)
    return outs[0], outs[1], outs[2], {n: outs[3 + 4 * k:7 + 4 * k] for k, n in enumerate(names)}


def _wdw_adam(w, g, m, v):
    def body(w_ref, g_ref, m_ref, v_ref, d_out, m_out, v_out):
        d, m2, v2 = _adam_math(w_ref[...], g_ref[...], m_ref[...], v_ref[...])
        d_out[...] = d
        m_out[...] = m2
        v_out[...] = v2

    vm = pl.BlockSpec(memory_space=pltpu.VMEM)
    return pl.pallas_call(
        body, name="wdw_adam", out_shape=[jax.ShapeDtypeStruct(w.shape, F32)] * 3,
        in_specs=[vm] * 4, out_specs=[vm] * 3, compiler_params=_cp(None, 16),
    )(w, g, m, v)


def kernel(x, c, w_ada, b_ada, lb_logits, g_pre_mix, w_in, b_in, w_dw, b_dw, gn_gain, gn_bias, g_hgrn_out, w_out, g_post_mix, g_pre_ffn, w_up, w_down, g_post_ffn, loss_target, m_w_ada, m_b_ada, m_lb_logits, m_g_pre_mix, m_w_in, m_b_in, m_w_dw, m_b_dw, m_gn_gain, m_gn_bias, m_g_hgrn_out, m_w_out, m_g_post_mix, m_g_pre_ffn, m_w_up, m_w_down, m_g_post_ffn, v_w_ada, v_b_ada, v_lb_logits, v_g_pre_mix, v_w_in, v_b_in, v_w_dw, v_b_dw, v_gn_gain, v_gn_bias, v_g_hgrn_out, v_w_out, v_g_post_mix, v_g_pre_ffn, v_w_up, v_w_down, v_g_post_ffn):
    ax, ay, ac = lax.axis_index("x"), lax.axis_index("y"), lax.axis_index("c")
    chip = 2 * ax + ay
    T = x.shape[1]
    xs, tgt = x[0], loss_target[0]
    ada_cols = w_ada.shape[2]

    b_sh = lax.dynamic_slice_in_dim(b_ada, chip * ada_cols, ada_cols, axis=1)
    wdw_pad = jnp.pad(w_dw[0], ((0, HALO - CONV_K), (0, 0)))
    chip1 = jnp.reshape(chip, (1,)).astype(jnp.int32)
    place = jnp.stack([ac, chip]).astype(jnp.int32)
    _, c8, modg, wdwg = _ada_exchange(c, w_ada[0], b_sh, wdw_pad)
    modr = modg.reshape(6, D_MODEL)
    wdw_all = jnp.transpose(wdwg, (1, 0, 2)).reshape(HALO, CONV_CH)
    bufs = {t: _cast_own(chip1, w[0], "cast_" + t)
            for w, t in ((w_in, "w_in"), (w_out, "w_out"), (w_up, "w_up"), (w_down, "w_down"))}
    vec = (b_dw, gn_gain, gn_bias, g_hgrn_out)

    p, h1, w_in_g, w_out_g = _mix_in_fwd(chip1, xs, modr, g_pre_mix, b_in.reshape(N_CHIPS, 1, IN_COLS // N_CHIPS),
                                          bufs["w_in"], bufs["w_out"])
    (cat, ys, o, states), (w_up_g, w_down_g) = _mixers_fwd(p, wdw_all, vec, lb_logits, [bufs["w_up"], bufs["w_down"]])
    w_out_f = w_out_g.reshape(D_MODEL, D_MODEL)
    w_down_f = w_down_g.reshape(D_FF, D_MODEL)
    y, x1, h2 = _mix_out_fwd(cat, w_out_f, xs, modr, g_post_mix, g_pre_ffn)
    r, dy2, dx2, st_loss = _ffn_fwd(h2, w_up_g, w_down_f, x1, tgt, modr, g_post_ffn)

    def pair_sums(grads, got, tags):
        return [_pair_sum(place, g, o_, "pair_sum_" + t) for (g, _), o_, t in zip(grads, got, tags)]

    da, dx1, st_ffn = _ffn_bwd(dy2, r, x1, dx2, w_up_g, w_down_f, modr, g_pre_ffn)
    g_up = _weight_grad(h2, da, False, True, "grad_w_up")
    g_down = _weight_grad(r, dy2, True, False, "grad_w_down")
    (dcat, st_out, g_out), got_ud = _mix_out_bwd(dx1, y, cat, w_out_f, modr, g_post_mix, [g_up[1], g_down[1]])
    g_out = g_out.reshape(N_CHIPS, D_MODEL // N_CHIPS, D_MODEL)
    got_o = _pair_swap([g_out], "pair_swap_w_out")
    early = pair_sums([(g_out, None), g_up, g_down], list(got_o) + list(got_ud), ["w_out", "w_up", "w_down"])
    (dp, st_bin, st_512, dwdw), got_early, g_in = _mixers_bwd(p, dcat, ys, o, states, h1, wdw_all, vec, lb_logits,
                                                              [pb for _, pb in early])
    late = pair_sums([(g_in, None)], _pair_swap([g_in], "pair_swap_w_in"), ["w_in"])
    (grad_x, st_in), got_late = _mix_in_bwd(dp, w_in_g, xs, dx1, modr, g_pre_mix, [late[0][1]])
    fulls = [_chip_sum(place, pf, gb, "chip_sum_" + t)
             for (pf, _), gb, t in zip(late + early, list(got_late) + list(got_early), ["w_in", "w_out", "w_up", "w_down"])]

    pad_lanes = lambda s: jnp.pad(s, ((0, 0), (0, D_MODEL - s.shape[1])))
    stats = jnp.concatenate([st_loss, st_ffn, st_out, st_in, st_bin, pad_lanes(st_512), pad_lanes(dwdw)], axis=0)
    (g_w_in, g_w_out, g_w_up, g_w_down), gath = _final_exchange(fulls, stats)
    small = {"b_ada": (b_ada, m_b_ada, v_b_ada), "lb_logits": (lb_logits, m_lb_logits, v_lb_logits),
             "g_pre_mix": (g_pre_mix, m_g_pre_mix, v_g_pre_mix), "b_in": (b_in, m_b_in, v_b_in),
             "b_dw": (b_dw, m_b_dw, v_b_dw), "gn_gain": (gn_gain, m_gn_gain, v_gn_gain),
             "gn_bias": (gn_bias, m_gn_bias, v_gn_bias), "g_hgrn_out": (g_hgrn_out, m_g_hgrn_out, v_g_hgrn_out),
             "g_post_mix": (g_post_mix, m_g_post_mix, v_g_post_mix), "g_pre_ffn": (g_pre_ffn, m_g_pre_ffn, v_g_pre_ffn),
             "g_post_ffn": (g_post_ffn, m_g_post_ffn, v_g_post_ffn)}
    loss_t, dmod_all, dwdw_sum, sres = _small_update(gath, small)
    loss = loss_t[0, 0]

    res = dict(sres)
    dmod_sh = lax.dynamic_slice_in_dim(dmod_all, chip * ada_cols, ada_cols, axis=1)
    res["w_ada"] = [t[None] for t in _ada_grad_adam(jnp.transpose(c8), dmod_sh, w_ada[0], m_w_ada[0], v_w_ada[0])]
    g_wdw = lax.dynamic_slice_in_dim(dwdw_sum, chip * HEAD_D, HEAD_D, axis=1)[:CONV_K][None]
    res["w_dw"] = [g_wdw] + list(_wdw_adam(w_dw, g_wdw, m_w_dw, v_w_dw))
    for name, g, w, m, v in (("w_in", g_w_in, w_in, m_w_in, v_w_in), ("w_out", g_w_out, w_out, m_w_out, v_w_out),
                             ("w_up", g_w_up, w_up, m_w_up, v_w_up), ("w_down", g_w_down, w_down, m_w_down, v_w_down)):
        d, m2, v2 = _adam_big(w[0], g, m[0], v[0], "adam_" + name)
        res[name] = [g[None], d[None], m2[None], v2[None]]

    order = ["w_ada", "b_ada", "lb_logits", "g_pre_mix", "w_in", "b_in", "w_dw", "b_dw", "gn_gain", "gn_bias",
             "g_hgrn_out", "w_out", "g_post_mix", "g_pre_ffn", "w_up", "w_down", "g_post_ffn"]
    out = [loss, grad_x[None]]
    for k in range(4):
        out += [res[n][k] for n in order]
    return tuple(out)
```

```python
import jax
import jax.numpy as jnp
from jax import lax
from jax.experimental import pallas as pl
from jax.experimental.pallas import tpu as pltpu

F32, BF16 = jnp.float32, jnp.bfloat16
D_MODEL = 1024
CONV_CH = 512
HGRN_W = 512
N_HEADS = 4
HEAD_D = 128
CONV_K = 31
GN_GROUP = 64
GN_SHIFT = 6
IN_COLS = 3072
D_FF = 4096
CHUNK = 64
CHUNK_SHIFT = 6
N_CHIPS = 4
N_DEV = 8
RMS_EPS = 1e-6
GN_EPS = 1e-5
ADAM_LR, ADAM_B1, ADAM_B2, ADAM_EPS, ADAM_WD, ADAM_STEP = 0.001, 0.9, 0.999, 1e-08, 0.01, 10
TOK_TILE = 512
MIXIN_TILE = 1024
MIXB_TILE = 256
FFN_TILE = 1024
FFN_BLOCK = 512
GRAD_TILE = 2048
HALO = 32
SUB = 8
LANE = 128
CONV_ROWS = 128
MIB = 1 << 20
MESH = pl.DeviceIdType.MESH
OTHER_CHIPS = ((0, 1), (1, 0), (1, 1))


def _cp(sem=None, vmem_mib=48):
    return pltpu.CompilerParams(dimension_semantics=sem, vmem_limit_bytes=vmem_mib * MIB)


def _dot(a, b):
    return jnp.dot(a, b, preferred_element_type=F32)


def _dot_nt(a, b):
    return lax.dot_general(a, b, (((1,), (1,)), ((), ())), preferred_element_type=F32)


def _dot_tn(a, b):
    return lax.dot_general(a, b, (((0,), (0,)), ((), ())), preferred_element_type=F32)


def _sig(v):
    return 0.5 * jnp.tanh(0.5 * v) + 0.5


def _colsum(v):
    return jnp.sum(v, axis=0, keepdims=True)


def _flip(v, b):
    return 1 - v if b else v


def _rcopy(src, dst, ssem, rsem, dev):
    return pltpu.make_async_remote_copy(src_ref=src, dst_ref=dst, send_sem=ssem, recv_sem=rsem,
                                        device_id=dev, device_id_type=MESH)


def _place():
    return lax.axis_index("x"), lax.axis_index("y"), lax.axis_index("c")


def _full(shape):
    return pl.BlockSpec(shape, lambda *_: (0,) * len(shape))


def _big(shape, dtype):
    return pltpu.HBM(shape, dtype)


def _hbm(*arrays):
    out = [pltpu.with_memory_space_constraint(a, pltpu.HBM) for a in arrays]
    return out[0] if len(out) == 1 else out


def _split2(v):
    hi = v.astype(BF16)
    lo = (v - hi.astype(F32)).astype(BF16)
    return hi, lo


def _split3(v):
    h1 = v.astype(BF16)
    r1 = v - h1.astype(F32)
    h2 = r1.astype(BF16)
    h3 = (r1 - h2.astype(F32)).astype(BF16)
    return h1, h2, h3


def _mm3s(mat, parts):
    h1, h2, h3 = parts
    return _dot(mat, h1) + _dot(mat, h2) + _dot(mat, h3)


def _mm3(mat, v):
    return _mm3s(mat, _split3(v))


def _gn_matrix():
    r = lax.broadcasted_iota(jnp.int32, (CONV_CH, CONV_CH), 0) >> GN_SHIFT
    c = lax.broadcasted_iota(jnp.int32, (CONV_CH, CONV_CH), 1) >> GN_SHIFT
    return jnp.where(r == c, 1.0 / GN_GROUP, 0.0).astype(BF16)


def _gmean(v, gmat):
    hi, lo = _split2(v)
    return _dot(hi, gmat) + _dot(lo, gmat)


def _chunk_masks(tm):
    r = lax.broadcasted_iota(jnp.int32, (tm, tm), 0)
    c = lax.broadcasted_iota(jnp.int32, (tm, tm), 1)
    same = (r >> CHUNK_SHIFT) == (c >> CHUNK_SHIFT)
    one = lambda m: jnp.where(m, 1.0, 0.0).astype(BF16)
    return one(same & (c <= r)), one(same & (c >= r)), one(same)


def _tri():
    return lax.broadcasted_iota(jnp.int32, (CHUNK, CHUNK), 0) >= lax.broadcasted_iota(jnp.int32, (CHUNK, CHUNK), 1)


def _lower_bound(lbl_ref):
    l0, l1 = lbl_ref[0:1, :], lbl_ref[1:2, :]
    mx = jnp.maximum(l0, l1)
    e0, e1 = jnp.exp(l0 - mx), jnp.exp(l1 - mx)
    return e0 / (e0 + e1), e1 / (e0 + e1)


CONV_FWD_TAPS = tuple((j, HALO - (CONV_K - 1) + j) for j in range(CONV_K))
CONV_BWD_TAPS = tuple((j, (CONV_K - 1) - j) for j in range(CONV_K))


def _tap_conv(src_ref, w_ref, row0, taps, lanes):
    acc = None
    for b in range(SUB):
        pb = None
        for j, off in taps:
            if off % SUB == b:
                lo = row0 + off - b
                term = w_ref[j:j + 1, lanes] * src_ref[lo:lo + CONV_ROWS + SUB, lanes]
                pb = term if pb is None else pb + term
        if pb is not None:
            sh = pb[b:b + CONV_ROWS, :]
            acc = sh if acc is None else acc + sh
    return acc


def _hgrn_prep(pq, pf, lb, lower, same):
    sq = _sig(pq)
    qf = pq * sq
    sf = _sig(pf)
    f = lb + (1.0 - lb) * sf
    logf = jnp.log(f)
    k = 1.0 - f
    parts = _split3(logf)
    G = _mm3s(lower, parts)
    Gl = _mm3s(same, parts)
    eG, enG, eGlG = jnp.exp(G), jnp.exp(-G), jnp.exp(Gl - G)
    return dict(sq=sq, sf=sf, f=f, Gl=Gl, eG=eG, enG=enG, eGlG=eGlG, qt=qf * eG, kt=k * enG, kh=k * eGlG)


def _ada_exchange(c_row, w_ada, b_sh, wdw_pad):
    ncol = w_ada.shape[1]

    def body(c_ref, w_ref, b_ref, wdw_ref, call_ref, c8_ref, modg_ref, wdwg_ref, rows_s, sa, ra, sw, rw, sm, rm):
        x, y, c = _place()
        me = 4 * x + 2 * y + c
        chip = 2 * x + y
        cv = c_ref[...]
        call_ref[me] = cv * _sig(cv)
        wdwg_ref[chip] = wdw_ref[...]
        sends = []
        for m in range(1, N_DEV):
            peer = (_flip(x, m >> 2), _flip(y, (m >> 1) & 1), _flip(c, m & 1))
            cp = _rcopy(call_ref.at[me], call_ref.at[me], sa.at[m - 1], ra.at[m - 1], peer)
            cp.start()
            sends.append(cp)
        for k, (fx, fy) in enumerate(OTHER_CHIPS):
            peer = (_flip(x, fx), _flip(y, fy), c)
            cp = _rcopy(wdwg_ref.at[chip], wdwg_ref.at[chip], sw.at[k], rw.at[k], peer)
            cp.start()
            sends.append(cp)
        for m in range(1, N_DEV):
            peer = (_flip(x, m >> 2), _flip(y, (m >> 1) & 1), _flip(c, m & 1))
            pid = 4 * peer[0] + 2 * peer[1] + peer[2]
            _rcopy(call_ref.at[pid], call_ref.at[pid], sa.at[m - 1], ra.at[m - 1], peer).wait_recv()
        for b in range(N_DEV):
            c8_ref[b:b + 1, :] = call_ref[b]
        mod_all = _dot(c8_ref[...].astype(BF16), w_ref[...].astype(BF16)) + b_ref[...]
        for b in range(N_DEV):
            rows_s[b] = mod_all[b:b + 1, :]
        modg_ref[chip] = rows_s[me]
        for k, (fx, fy) in enumerate(OTHER_CHIPS):
            peer = (_flip(x, fx), _flip(y, fy), c)
            pid = 4 * peer[0] + 2 * peer[1] + peer[2]
            cp = _rcopy(rows_s.at[pid], modg_ref.at[chip], sm.at[k], rm.at[k], peer)
            cp.start()
            sends.append(cp)
        for k, (fx, fy) in enumerate(OTHER_CHIPS):
            peer = (_flip(x, fx), _flip(y, fy), c)
            pchip = 2 * peer[0] + peer[1]
            _rcopy(rows_s.at[0], modg_ref.at[pchip], sm.at[k], rm.at[k], peer).wait_recv()
            _rcopy(wdwg_ref.at[pchip], wdwg_ref.at[pchip], sw.at[k], rw.at[k], peer).wait_recv()
        for cp in sends:
            cp.wait_send()

    vm = pl.BlockSpec(memory_space=pltpu.VMEM)
    return pl.pallas_call(
        body, name="ada_exchange",
        out_shape=[jax.ShapeDtypeStruct((N_DEV, 1, D_MODEL), F32), jax.ShapeDtypeStruct((N_DEV, D_MODEL), F32),
                   jax.ShapeDtypeStruct((N_CHIPS, 1, ncol), F32), jax.ShapeDtypeStruct((N_CHIPS, HALO, HEAD_D), F32)],
        in_specs=[vm] * 4, out_specs=[vm] * 4,
        scratch_shapes=[pltpu.VMEM((N_DEV, 1, ncol), F32),
                        pltpu.SemaphoreType.DMA((N_DEV - 1,)), pltpu.SemaphoreType.DMA((N_DEV - 1,)),
                        pltpu.SemaphoreType.DMA((3,)), pltpu.SemaphoreType.DMA((3,)),
                        pltpu.SemaphoreType.DMA((3,)), pltpu.SemaphoreType.DMA((3,))],
        compiler_params=_cp(None, 32),
    )(c_row, w_ada, b_sh, wdw_pad)


def _cast_own(chip1, shard, name):
    rows, cols = shard.shape
    tr = _row_tile(rows)

    def body(ch_ref, s_ref, o_ref):
        o_ref[0] = s_ref[...].astype(BF16)

    gs = pltpu.PrefetchScalarGridSpec(
        num_scalar_prefetch=1, grid=(rows // tr,),
        in_specs=[pl.BlockSpec((tr, cols), lambda i, ch: (i, 0))],
        out_specs=pl.BlockSpec((1, tr, cols), lambda i, ch: (ch[0], i, 0)))
    return pl.pallas_call(
        body, name=name, grid_spec=gs, out_shape=_big((N_CHIPS, rows, cols), BF16),
        compiler_params=_cp(("arbitrary",), 32),
    )(chip1, _hbm(shard))


def _slab(buf, ch, core):
    hs = buf.shape[1] // 2
    return buf.at[ch, pl.ds(core * hs, hs), :]


def _gather_start(bufs, ssem, rsem, relations=(0, 1, 2)):
    x, y, c = _place()
    chip = 2 * x + y
    for k in relations:
        fx, fy = OTHER_CHIPS[k]
        peer = (_flip(x, fx), _flip(y, fy), c)
        for t, buf in enumerate(bufs):
            _rcopy(_slab(buf, chip, c), _slab(buf, chip, c), ssem.at[t * 3 + k], rsem.at[t * 3 + k], peer).start()


def _gather_finish(bufs, ssem, rsem):
    nt = len(bufs)
    x, y, c = _place()
    chip = 2 * x + y
    sibling = (x, y, 1 - c)
    for k, (fx, fy) in enumerate(OTHER_CHIPS):
        peer = (_flip(x, fx), _flip(y, fy), c)
        pchip = 2 * peer[0] + peer[1]
        for t, buf in enumerate(bufs):
            _rcopy(_slab(buf, pchip, c), _slab(buf, pchip, c), ssem.at[t * 3 + k], rsem.at[t * 3 + k], peer).wait_recv()
            _rcopy(_slab(buf, pchip, c), _slab(buf, pchip, c), ssem.at[3 * nt + t * 3 + k],
                   rsem.at[3 * nt + t * 3 + k], sibling).start()
    for k, (fx, fy) in enumerate(OTHER_CHIPS):
        peer = (_flip(x, fx), _flip(y, fy), c)
        pchip = 2 * peer[0] + peer[1]
        for t, buf in enumerate(bufs):
            _rcopy(_slab(buf, pchip, 1 - c), _slab(buf, pchip, 1 - c), ssem.at[3 * nt + t * 3 + k],
                   rsem.at[3 * nt + t * 3 + k], sibling).wait_recv()
            _rcopy(_slab(buf, chip, c), _slab(buf, chip, c), ssem.at[t * 3 + k], rsem.at[t * 3 + k], peer).wait_send()
            _rcopy(_slab(buf, pchip, c), _slab(buf, pchip, c), ssem.at[3 * nt + t * 3 + k],
                   rsem.at[3 * nt + t * 3 + k], sibling).wait_send()


def _gather_arrive(bufs, k, ssem, rsem):
    nt = len(bufs)
    x, y, c = _place()
    fx, fy = OTHER_CHIPS[k]
    peer = (_flip(x, fx), _flip(y, fy), c)
    pchip = 2 * peer[0] + peer[1]
    for t, buf in enumerate(bufs):
        _rcopy(_slab(buf, pchip, c), _slab(buf, pchip, c), ssem.at[t * 3 + k], rsem.at[t * 3 + k], peer).wait_recv()
        _rcopy(_slab(buf, pchip, c), _slab(buf, pchip, c), ssem.at[3 * nt + t * 3 + k],
               rsem.at[3 * nt + t * 3 + k], (x, y, 1 - c)).start()
    for t, buf in enumerate(bufs):
        _rcopy(_slab(buf, pchip, 1 - c), _slab(buf, pchip, 1 - c), ssem.at[3 * nt + t * 3 + k],
               rsem.at[3 * nt + t * 3 + k], (x, y, 1 - c)).wait_recv()


def _gather_sends_done(bufs, ssem, rsem):
    nt = len(bufs)
    x, y, c = _place()
    chip = 2 * x + y
    for k, (fx, fy) in enumerate(OTHER_CHIPS):
        peer = (_flip(x, fx), _flip(y, fy), c)
        pchip = 2 * peer[0] + peer[1]
        for t, buf in enumerate(bufs):
            _rcopy(_slab(buf, chip, c), _slab(buf, chip, c), ssem.at[t * 3 + k], rsem.at[t * 3 + k], peer).wait_send()
            _rcopy(_slab(buf, pchip, c), _slab(buf, pchip, c), ssem.at[3 * nt + t * 3 + k],
                   rsem.at[3 * nt + t * 3 + k], (x, y, 1 - c)).wait_send()


def _gather_sems(nt):
    return [pltpu.SemaphoreType.DMA((6 * nt,)), pltpu.SemaphoreType.DMA((6 * nt,))]


def _pair_copies(ins, outs, ssem, rsem):
    x, y, c = _place()
    copies = []
    for t in range(len(ins)):
        hs = ins[t].shape[1] // 2
        copies.append(_rcopy(ins[t].at[:, pl.ds((1 - c) * hs, hs), :], outs[t], ssem.at[t], rsem.at[t], (x, y, 1 - c)))
    return copies


def _pair_shapes(grads):
    return [_big((g.shape[0], g.shape[1] // 2, g.shape[2]), g.dtype) for g in grads]


def _pair_sems(nt):
    return [pltpu.SemaphoreType.DMA((nt,)), pltpu.SemaphoreType.DMA((nt,))]


def _pair_swap(grads, name):
    nt = len(grads)
    hbm = pl.BlockSpec(memory_space=pl.ANY)

    def body(*refs):
        copies = _pair_copies(refs[:nt], refs[nt:2 * nt], refs[2 * nt], refs[2 * nt + 1])
        for cp in copies:
            cp.start()
        for cp in copies:
            cp.wait_recv()
        for cp in copies:
            cp.wait_send()

    return pl.pallas_call(
        body, name=name, out_shape=_pair_shapes(grads), in_specs=[hbm] * nt, out_specs=[hbm] * nt,
        scratch_shapes=_pair_sems(nt),
    )(*[_hbm(g) for g in grads])


def _xchg_copies(ins, outs, ssem, rsem):
    x, y, c = _place()
    copies = []
    for k, (fx, fy) in enumerate(OTHER_CHIPS):
        peer = (_flip(x, fx), _flip(y, fy), c)
        for t in range(len(ins)):
            copies.append(_rcopy(ins[t].at[k], outs[t].at[k], ssem.at[t * 3 + k], rsem.at[t * 3 + k], peer))
    return copies


def _xchg_sems(nt):
    return [pltpu.SemaphoreType.DMA((3 * nt,)), pltpu.SemaphoreType.DMA((3 * nt,))]


def _final_exchange(fulls, stats):
    nt = len(fulls)
    rows, cols = stats.shape
    hbm = pl.BlockSpec(memory_space=pl.ANY)
    vm = pl.BlockSpec(memory_space=pltpu.VMEM)

    def body(*refs):
        ins, s_ref = refs[:nt], refs[nt]
        outs, g_ref = refs[nt + 1:2 * nt + 1], refs[2 * nt + 1]
        hssem, hrsem, ssem, rsem = refs[2 * nt + 2:]
        x, y, c = _place()
        me, sibling = (x, y, c), (x, y, 1 - c)
        halves = []
        for t in range(nt):
            hs = ins[t].shape[0] // 2
            mine = pl.ds(c * hs, hs)
            cp = _rcopy(ins[t].at[mine, :], outs[t].at[mine, :], hssem.at[t], hrsem.at[t], sibling)
            cp.start()
            halves.append(cp)

        chips = [(_flip(x, fx), _flip(y, fy)) for fx, fy in OTHER_CHIPS]

        def blk(px, py, pc):
            return g_ref.at[4 * px + 2 * py + pc]

        def copy(k, block, to, src=None):
            return _rcopy(blk(*block) if src is None else src, blk(*block), ssem.at[k], rsem.at[k], to)

        g_ref[4 * x + 2 * y + c] = s_ref[...]
        first = [copy(0, me, sibling, src=s_ref)]
        first += [copy(1 + j, me, (*chip, c), src=s_ref) for j, chip in enumerate(chips)]
        for cp in first:
            cp.start()
        passed = [copy(4 + j, (*chip, c), sibling) for j, chip in enumerate(chips)]
        for j, chip in enumerate(chips):
            copy(1 + j, (*chip, c), me).wait_recv()
            passed[j].start()
        copy(0, sibling, me).wait_recv()
        for j, chip in enumerate(chips):
            copy(4 + j, (*chip, 1 - c), me).wait_recv()
        for t in range(nt):
            hs = ins[t].shape[0] // 2
            other = pl.ds((1 - c) * hs, hs)
            _rcopy(ins[t].at[other, :], outs[t].at[other, :], hssem.at[t], hrsem.at[t], sibling).wait_recv()
        for cp in first + passed + halves:
            cp.wait_send()

    outs = pl.pallas_call(
        body, name="final_exchange",
        out_shape=[_big(f.shape, F32) for f in fulls] + [jax.ShapeDtypeStruct((N_DEV, rows, cols), F32)],
        in_specs=[hbm] * nt + [vm], out_specs=[hbm] * nt + [vm],
        input_output_aliases={t: t for t in range(nt)},
        scratch_shapes=[pltpu.SemaphoreType.DMA((nt,)), pltpu.SemaphoreType.DMA((nt,)),
                        pltpu.SemaphoreType.DMA((7,)), pltpu.SemaphoreType.DMA((7,))],
        compiler_params=_cp(None, 32),
    )(*[_hbm(f) for f in fulls], stats)
    return outs[:nt], outs[nt]


def _row_tile(rows):
    return min(rows, 512)


def _pair_sum(place, grad, got, name):
    nb, hs, cols = got.shape
    tr = _row_tile(hs)
    nr = hs // tr

    def body(pl_ref, g_ref, o_ref, pf_ref, pb_ref):
        j = pl.program_id(1)
        s = g_ref[0] + o_ref[0].astype(F32)

        @pl.when(j == 0)
        def _():
            pf_ref[...] = s

        @pl.when(j > 0)
        def _():
            pb_ref[0] = s.astype(BF16)

    gs = pltpu.PrefetchScalarGridSpec(
        num_scalar_prefetch=1, grid=(nr, nb),
        in_specs=[pl.BlockSpec((1, tr, cols), lambda i, j, p: (p[1] ^ j, p[0] * nr + i, 0)),
                  pl.BlockSpec((1, tr, cols), lambda i, j, p: (p[1] ^ j, i, 0))],
        out_specs=[pl.BlockSpec((tr, cols), lambda i, j, p: (i, 0)),
                   pl.BlockSpec((1, tr, cols), lambda i, j, p: (jnp.maximum(j - 1, 0), i, 0))])
    return pl.pallas_call(
        body, name=name, grid_spec=gs,
        out_shape=[_big((hs, cols), F32), _big((nb - 1, hs, cols), BF16)],
        compiler_params=_cp(("arbitrary", "arbitrary"), 32),
    )(place, *_hbm(grad, got))


def _chip_sum(place, pair_f, got_b, name):
    nb, hs, cols = got_b.shape
    tr = _row_tile(hs)
    nr = hs // tr

    def body(pl_ref, pf_ref, gb_ref, o_ref):
        acc = pf_ref[...]
        for k in range(nb):
            acc = acc + gb_ref[k].astype(F32)
        o_ref[...] = acc

    gs = pltpu.PrefetchScalarGridSpec(
        num_scalar_prefetch=1, grid=(nr,),
        in_specs=[pl.BlockSpec((tr, cols), lambda i, p: (i, 0)),
                  pl.BlockSpec((nb, tr, cols), lambda i, p: (0, i, 0))],
        out_specs=pl.BlockSpec((tr, cols), lambda i, p: (p[0] * nr + i, 0)))
    return pl.pallas_call(
        body, name=name, grid_spec=gs,
        out_shape=_big((2 * hs, cols), F32),
        compiler_params=_cp(("arbitrary",), 32),
    )(place, *_hbm(pair_f, got_b))


def _adam_math(w, g, m, v):
    m2 = ADAM_B1 * m + (1.0 - ADAM_B1) * g
    v2 = ADAM_B2 * v + (1.0 - ADAM_B2) * (g * g)
    m_hat = m2 / (1.0 - ADAM_B1 ** ADAM_STEP)
    v_hat = v2 / (1.0 - ADAM_B2 ** ADAM_STEP)
    delta = -ADAM_LR * (m_hat / (jnp.sqrt(v_hat) + ADAM_EPS) + ADAM_WD * w)
    return delta, m2, v2


def _adam_big(w, g, m, v, name):
    rows, cols = w.shape
    tr = _row_tile(rows)

    def body(w_ref, g_ref, m_ref, v_ref, d_out, m_out, v_out):
        d, m2, v2 = _adam_math(w_ref[...], g_ref[...], m_ref[...], v_ref[...])
        d_out[...] = d
        m_out[...] = m2
        v_out[...] = v2

    spec = pl.BlockSpec((tr, cols), lambda i: (i, 0))
    return pl.pallas_call(
        body, name=name, grid=(rows // tr,), in_specs=[spec] * 4, out_specs=[spec] * 3,
        out_shape=[_big(w.shape, F32)] * 3,
        compiler_params=_cp(("arbitrary",), 48),
    )(*_hbm(w, g, m, v))


def _ada_grad_adam(c8t, dmod_sh, w, m, v):
    rows, cols = w.shape
    tr = _row_tile(rows) // 2

    def body(ct_ref, dm_ref, w_ref, m_ref, v_ref, g_out, d_out, m_out, v_out):
        g = None
        for b in range(N_DEV):
            term = ct_ref[:, b:b + 1] * dm_ref[b:b + 1, :]
            g = term if g is None else g + term
        d, m2, v2 = _adam_math(w_ref[...], g, m_ref[...], v_ref[...])
        g_out[...] = g
        d_out[...] = d
        m_out[...] = m2
        v_out[...] = v2

    spec = pl.BlockSpec((tr, cols), lambda i: (i, 0))
    return pl.pallas_call(
        body, name="ada_grad_adam", grid=(rows // tr,),
        in_specs=[pl.BlockSpec((tr, N_DEV), lambda i: (i, 0)), _full((N_DEV, cols)), spec, spec, spec],
        out_specs=[spec] * 4, out_shape=[_big(w.shape, F32)] * 4,
        compiler_params=_cp(("arbitrary",), 32),
    )(c8t, dmod_sh, *_hbm(w, m, v))


def _tok_tile(t):
    return min(TOK_TILE, t)


def _mix_in_fwd(chip1, x, modr, g_pre, b_in4, w_in_buf, w_out_buf):
    T = x.shape[0]
    tm = min(MIXIN_TILE, T)
    nt = T // tm
    nb = IN_COLS // N_CHIPS

    def body(ch_ref, x_ref, mod_ref, g_ref, b_ref, win_in, wout_in, p_ref, h_ref, win_ref, wout_ref,
             h_all, wblk, lsem, is_sem, ir_sem, os_sem, or_sem):
        k, i = pl.program_id(0), pl.program_id(1)
        chip = ch_ref[0]

        def load_block(blk):
            cp = pltpu.make_async_copy(win_ref.at[blk], wblk, lsem)
            cp.start()
            cp.wait()

        @pl.when((k == 0) & (i == 0))
        def _():
            _gather_start([win_ref], is_sem, ir_sem, relations=(0, 1))
            load_block(chip)

        for r in range(N_CHIPS - 1):
            @pl.when((k == r + 1) & (i == 0))
            def _(r=r):
                _gather_arrive([win_ref], r, is_sem, ir_sem)
                if r == 0:
                    _gather_start([win_ref], is_sem, ir_sem, relations=(2,))
                if r == 1:
                    _gather_start([wout_ref], os_sem, or_sem)
                load_block(chip ^ (r + 1))

        rows = pl.ds(pl.multiple_of(i * tm, tm), tm)

        @pl.when(k == 0)
        def _():
            xv = x_ref[...]
            rstd = lax.rsqrt(jnp.mean(xv * xv, axis=-1, keepdims=True) + RMS_EPS)
            h = (xv * rstd) * g_ref[...] * (1.0 + mod_ref[1:2, :]) + mod_ref[0:1, :]
            hb = h.astype(BF16)
            h_ref[...] = hb
            h_all[rows, :] = hb

        p_ref[...] = _dot(h_all[rows, :], wblk[...]) + b_ref[chip ^ k]

        @pl.when((k == N_CHIPS - 1) & (i == nt - 1))
        def _():
            _gather_sends_done([win_ref], is_sem, ir_sem)
            _gather_finish([wout_ref], os_sem, or_sem)

    hbm = pl.BlockSpec(memory_space=pl.ANY)
    first_pass = lambda k, i, ch: (jnp.where(k == 0, i, nt - 1), 0)
    gs = pltpu.PrefetchScalarGridSpec(
        num_scalar_prefetch=1, grid=(N_CHIPS, nt),
        in_specs=[pl.BlockSpec((tm, D_MODEL), first_pass), pl.BlockSpec((6, D_MODEL), lambda k, i, ch: (0, 0)),
                  pl.BlockSpec((1, D_MODEL), lambda k, i, ch: (0, 0)),
                  pl.BlockSpec((N_CHIPS, 1, nb), lambda k, i, ch: (0, 0, 0)), hbm, hbm],
        out_specs=[pl.BlockSpec((tm, nb), lambda k, i, ch: (i, ch[0] ^ k)), pl.BlockSpec((tm, D_MODEL), first_pass),
                   hbm, hbm],
        scratch_shapes=[pltpu.VMEM((T, D_MODEL), BF16), pltpu.VMEM((D_MODEL, nb), BF16), pltpu.SemaphoreType.DMA]
        + _gather_sems(1) + _gather_sems(1))
    return pl.pallas_call(
        body, name="mix_in_fwd", grid_spec=gs,
        out_shape=[_big((T, IN_COLS), F32), _big((T, D_MODEL), BF16), _big(w_in_buf.shape, BF16),
                   _big(w_out_buf.shape, BF16)],
        input_output_aliases={5: 2, 6: 3},
        compiler_params=_cp(("arbitrary", "arbitrary"), 48),
    )(chip1, _hbm(x), modr, g_pre, b_in4, _hbm(w_in_buf), _hbm(w_out_buf))


def _mixers_fwd(p, wdw, vecs, lbl, w_out, x, modr, g_post, g_ffn, gbufs):
    T = p.shape[0]
    tm = _tok_tile(T)
    nt = T // tm
    nch = tm // CHUNK
    ng = len(gbufs)
    n_in, n_out = 12, 7

    def body(*refs):
        (p_ref, wdw_ref, bdw_ref, gain_ref, bias_ref, gout_ref, lbl_ref, wout_ref, x_ref, mod_ref, gp_ref,
         gf_ref) = refs[:n_in]
        cat_ref, ys_ref, o_ref, st_ref, y_ref, x1_ref, h2_ref = refs[n_in + ng:n_in + ng + n_out]
        gout_bufs = refs[n_in + ng + n_out:n_in + 2 * ng + n_out]
        (ubuf, state, qt_s, kt_s, kh_s, v_s, egl_s, lower_s, same_s, gmat_s, gssem,
         grsem) = refs[n_in + 2 * ng + n_out:]
        i = pl.program_id(0)

        @pl.when(i == 0)
        def _():
            _gather_start(gout_bufs, gssem, grsem)
            lower_s[...], _, same_s[...] = _chunk_masks(tm)
            gmat_s[...] = _gn_matrix()
            state[...] = jnp.zeros(state.shape, F32)
            ubuf[0:HALO, :] = jnp.zeros((HALO, CONV_CH), F32)
            ubuf[HALO + tm:HALO + tm + SUB, :] = jnp.zeros((SUB, CONV_CH), F32)

        @pl.when(i > 0)
        def _():
            ubuf[0:HALO, :] = ubuf[tm:tm + HALO, :]

        ubuf[HALO:HALO + tm, :] = p_ref[:, 0:CONV_CH] * _sig(p_ref[:, CONV_CH:2 * CONV_CH])
        for r in range(tm // CONV_ROWS):
            rows = slice(r * CONV_ROWS, (r + 1) * CONV_ROWS)
            for lb_ in range(CONV_CH // LANE):
                lanes = slice(lb_ * LANE, (lb_ + 1) * LANE)
                ys_ref[rows, lanes] = bdw_ref[:, lanes] + _tap_conv(ubuf, wdw_ref, r * CONV_ROWS, CONV_FWD_TAPS, lanes)
        gmat = gmat_s[...]
        yv = ys_ref[...]
        d = yv - _gmean(yv, gmat)
        rs = lax.rsqrt(_gmean(d * d, gmat) + GN_EPS)
        z = d * rs * gain_ref[...] + bias_ref[...]
        cat_ref[:, 0:CONV_CH] = (z * _sig(z)).astype(BF16)

        lb, _ = _lower_bound(lbl_ref)
        lower, same = lower_s[...], same_s[...]
        o0 = 2 * CONV_CH
        pr = _hgrn_prep(p_ref[:, o0:o0 + HGRN_W], p_ref[:, o0 + HGRN_W:o0 + 2 * HGRN_W], lb, lower, same)
        qt_s[...] = pr["qt"].astype(BF16)
        kt_s[...] = pr["kt"].astype(BF16)
        kh_s[...] = pr["kh"].astype(BF16)
        v_s[...] = p_ref[:, o0 + 2 * HGRN_W:o0 + 3 * HGRN_W].astype(BF16)
        egl_s[...] = jnp.exp(pr["Gl"])
        tri = _tri()

        def chunk(ci, carry):
            r0 = pl.multiple_of(ci * CHUNK, CHUNK)
            rows = pl.ds(r0, CHUNK)
            for h in range(N_HEADS):
                ls = pl.ds(h * HEAD_D, HEAD_D)
                qc, kc, hc, vc = qt_s[rows, ls], kt_s[rows, ls], kh_s[rows, ls], v_s[rows, ls]
                s0 = state[h]
                s0b = s0.astype(BF16)
                st_ref[ci, h] = s0
                att =jnp.where(tri, _dot_nt(qc, kc), 0.0).astype(BF16)
                o_ref[rows, ls] = _dot(att, vc) + _dot_nt(qc, s0b)
                state[h] = s0 * egl_s[pl.ds(r0, 1), ls] + _dot_tn(vc, hc)
            return carry

        lax.fori_loop(0, nch, chunk, 0)
        for h in range(N_HEADS):
            sl = slice(h * HEAD_D, (h + 1) * HEAD_D)
            oh = o_ref[:, sl]
            gh = p_ref[:, o0 + 3 * HGRN_W + h * HEAD_D:o0 + 3 * HGRN_W + (h + 1) * HEAD_D]
            rsh = lax.rsqrt(jnp.mean(oh * oh, axis=-1, keepdims=True) + RMS_EPS)
            hg = (oh * rsh) * gout_ref[:, sl] * (gh * _sig(gh))
            cat_ref[:, CONV_CH + h * HEAD_D:CONV_CH + (h + 1) * HEAD_D] = hg.astype(BF16)

        yv = _dot(cat_ref[...], wout_ref[...])
        y_ref[...] = yv
        rsy = lax.rsqrt(jnp.mean(yv * yv, axis=-1, keepdims=True) + RMS_EPS)
        x1 = x_ref[...] + mod_ref[2:3, :] * ((yv * rsy) * gp_ref[...])
        x1_ref[...] = x1
        rs1 = lax.rsqrt(jnp.mean(x1 * x1, axis=-1, keepdims=True) + RMS_EPS)
        h2 = (x1 * rs1) * gf_ref[...] * (1.0 + mod_ref[4:5, :]) + mod_ref[3:4, :]
        h2_ref[...] = h2.astype(BF16)

        @pl.when(i == nt - 1)
        def _():
            _gather_finish(gout_bufs, gssem, grsem)

    tile = lambda cols: pl.BlockSpec((tm, cols), lambda i: (i, 0))
    hbm = pl.BlockSpec(memory_space=pl.ANY)
    outs = pl.pallas_call(
        body, name="mixers_fwd", grid=(nt,),
        in_specs=[tile(IN_COLS), _full((HALO, CONV_CH))] + [_full((1, CONV_CH))] * 4 + [_full((2, HGRN_W))]
        + [_full((D_MODEL, D_MODEL)), tile(D_MODEL), _full((6, D_MODEL)), _full((1, D_MODEL)), _full((1, D_MODEL))]
        + [hbm] * ng,
        out_specs=[tile(D_MODEL), tile(CONV_CH), tile(HGRN_W),
                   pl.BlockSpec((nch, N_HEADS, HEAD_D, HEAD_D), lambda i: (i, 0, 0, 0)),
                   tile(D_MODEL), tile(D_MODEL), tile(D_MODEL)] + [hbm] * ng,
        out_shape=[_big((T, D_MODEL), BF16), _big((T, CONV_CH), F32), _big((T, HGRN_W), F32),
                   _big((T // CHUNK, N_HEADS, HEAD_D, HEAD_D), F32), _big((T, D_MODEL), F32),
                   _big((T, D_MODEL), F32), _big((T, D_MODEL), BF16)] + [_big(b.shape, BF16) for b in gbufs],
        input_output_aliases={n_in + t: n_out + t for t in range(ng)},
        scratch_shapes=[pltpu.VMEM((tm + HALO + SUB, CONV_CH), F32), pltpu.VMEM((N_HEADS, HEAD_D, HEAD_D), F32),
                        pltpu.VMEM((tm, HGRN_W), BF16), pltpu.VMEM((tm, HGRN_W), BF16),
                        pltpu.VMEM((tm, HGRN_W), BF16), pltpu.VMEM((tm, HGRN_W), BF16),
                        pltpu.VMEM((tm, HGRN_W), F32), pltpu.VMEM((tm, tm), BF16), pltpu.VMEM((tm, tm), BF16),
                        pltpu.VMEM((CONV_CH, CONV_CH), BF16)] + _gather_sems(ng),
        compiler_params=_cp(("arbitrary",), 56),
    )(_hbm(p), wdw, *vecs, lbl, *_hbm(w_out, x), modr, g_post, g_ffn, *[_hbm(b) for b in gbufs])
    return outs[:n_out], outs[n_out:]


def _row_chains(rows, n=2):
    step = rows // n
    return [slice(k * step, (k + 1) * step) for k in range(n)]


def _ffn_blocks():
    return D_FF // FFN_BLOCK, (D_FF // N_CHIPS) // FFN_BLOCK


def _ffn_fwd(h2, w_up_g, w_down, x1, target, modr, g_post):
    T = h2.shape[0]
    tm = min(FFN_TILE, T)
    fb = FFN_BLOCK
    nj, per = _ffn_blocks()

    def body(h_ref, wu_ref, wd_ref, x1_ref, t_ref, mod_ref, g_ref, r_ref, dy2_ref, dx2_ref, st_ref, acc):
        i, j = pl.program_id(0), pl.program_id(1)

        @pl.when((i == 0) & (j == 0))
        def _():
            st_ref[...] = jnp.zeros(st_ref.shape, F32)

        @pl.when(j == 0)
        def _():
            acc[...] = jnp.zeros(acc.shape, F32)

        for rows in _row_chains(tm):
            ra = jnp.maximum(_dot(h_ref[rows, :], wu_ref[0]), 0.0)
            rb = (ra * ra).astype(BF16)
            r_ref[rows, :] = rb
            acc[rows, :] += _dot(rb, wd_ref[...])

        @pl.when(j == nj - 1)
        def _():
            y2 = acc[...]
            rs = lax.rsqrt(jnp.mean(y2 * y2, axis=-1, keepdims=True) + RMS_EPS)
            nh = y2 * rs
            gp = g_ref[...]
            err = x1_ref[...] + mod_ref[5:6, :] * (nh * gp) - t_ref[...]
            dx2 = err * (1.0 / D_MODEL)
            dx2_ref[...] = dx2
            st_ref[0:1, :] += _colsum(err * err)
            st_ref[1:2, :] += _colsum(dx2 * (nh * gp))
            dn = dx2 * mod_ref[5:6, :]
            st_ref[2:3, :] += _colsum(dn * nh)
            dy2_ref[...] = _rms_bwd(dn * gp, nh, rs).astype(BF16)

    tile = pl.BlockSpec((tm, D_MODEL), lambda i, j: (i, 0))
    return pl.pallas_call(
        body, name="ffn_fwd", grid=(T // tm, nj),
        in_specs=[tile, pl.BlockSpec((1, D_MODEL, fb), lambda i, j: (j // per, 0, j % per)),
                  pl.BlockSpec((fb, D_MODEL), lambda i, j: (j, 0)), tile, tile,
                  _full((6, D_MODEL)), _full((1, D_MODEL))],
        out_specs=[pl.BlockSpec((tm, fb), lambda i, j: (i, j)), tile, tile, _full((8, D_MODEL))],
        out_shape=[_big((T, D_FF), BF16), _big((T, D_MODEL), BF16), _big((T, D_MODEL), F32),
                   jax.ShapeDtypeStruct((8, D_MODEL), F32)],
        scratch_shapes=[pltpu.VMEM((tm, D_MODEL), F32)],
        compiler_params=_cp(("arbitrary", "arbitrary"), 56),
    )(*_hbm(h2, w_up_g, w_down, x1, target), modr, g_post)


def _rms_bwd(dxn, xn, rs):
    return rs * (dxn - xn * jnp.mean(dxn * xn, axis=-1, keepdims=True))


def _ffn_bwd(dy2, r, x1, dx2, w_up_g, w_down, modr, g_ffn):
    T = dx2.shape[0]
    tm = min(FFN_TILE, T)
    fb = FFN_BLOCK
    nj, per = _ffn_blocks()

    def body(dy2_ref, r_ref, x1_ref, dx2_ref, wu_ref, wd_ref, mod_ref, gf_ref, da_ref, dx1_ref, st_ref, dh_s):
        i, j = pl.program_id(0), pl.program_id(1)

        @pl.when((i == 0) & (j == 0))
        def _():
            st_ref[...] = jnp.zeros(st_ref.shape, F32)

        @pl.when(j == 0)
        def _():
            dh_s[...] = jnp.zeros(dh_s.shape, F32)

        for rows in _row_chains(tm):
            ra = jnp.sqrt(r_ref[rows, :].astype(F32))
            da = (_dot_nt(dy2_ref[rows, :], wd_ref[...]) * (2.0 * ra)).astype(BF16)
            da_ref[rows, :] = da
            dh_s[rows, :] += _dot_nt(da, wu_ref[0])

        @pl.when(j == nj - 1)
        def _():
            dh = dh_s[...]
            x1v = x1_ref[...]
            rs1 = lax.rsqrt(jnp.mean(x1v * x1v, axis=-1, keepdims=True) + RMS_EPS)
            xn = x1v * rs1
            st_ref[0:1, :] += _colsum(dh)
            st_ref[1:2, :] += _colsum(dh * (xn * gf_ref[...]))
            dsc = dh * (1.0 + mod_ref[4:5, :])
            st_ref[2:3, :] += _colsum(dsc * xn)
            dx1_ref[...] = dx2_ref[...] + _rms_bwd(dsc * gf_ref[...], xn, rs1)

    tile = pl.BlockSpec((tm, D_MODEL), lambda i, j: (i, 0))
    ftile = pl.BlockSpec((tm, fb), lambda i, j: (i, j))
    return pl.pallas_call(
        body, name="ffn_bwd", grid=(T // tm, nj),
        in_specs=[tile, ftile, tile, tile, pl.BlockSpec((1, D_MODEL, fb), lambda i, j: (j // per, 0, j % per)),
                  pl.BlockSpec((fb, D_MODEL), lambda i, j: (j, 0)), _full((6, D_MODEL)), _full((1, D_MODEL))],
        out_specs=[ftile, tile, _full((8, D_MODEL))],
        out_shape=[_big((T, D_FF), BF16), _big((T, D_MODEL), F32), jax.ShapeDtypeStruct((8, D_MODEL), F32)],
        scratch_shapes=[pltpu.VMEM((tm, D_MODEL), F32)],
        compiler_params=_cp(("arbitrary", "arbitrary"), 56),
    )(*_hbm(dy2, r, x1, dx2, w_up_g, w_down), modr, g_ffn)


def _mix_out_bwd(dx1, y, cat, w_out, modr, g_post, swap):
    T = dx1.shape[0]
    tm = _tok_tile(T)
    nt = T // tm
    ns = len(swap)

    def body(*refs):
        dx1_ref, y_ref, cat_ref, w_ref, mod_ref, gp_ref = refs[:6]
        s_ins = refs[6:6 + ns]
        dcat_ref, st_ref, gw_ref = refs[6 + ns:9 + ns]
        s_outs = refs[9 + ns:9 + 2 * ns]
        gacc, gsem, pssem, prsem = refs[9 + 2 * ns:]
        i = pl.program_id(0)

        @pl.when(i == 0)
        def _():
            for cp in _pair_copies(s_ins, s_outs, pssem, prsem):
                cp.start()
            st_ref[...] = jnp.zeros(st_ref.shape, F32)
            gacc[...] = jnp.zeros(gacc.shape, F32)

        dxv, yv = dx1_ref[...], y_ref[...]
        rs = lax.rsqrt(jnp.mean(yv * yv, axis=-1, keepdims=True) + RMS_EPS)
        nh = yv * rs
        st_ref[0:1, :] += _colsum(dxv * (nh * gp_ref[...]))
        dn = dxv * mod_ref[2:3, :]
        st_ref[1:2, :] += _colsum(dn * nh)
        dy = _rms_bwd(dn * gp_ref[...], nh, rs).astype(BF16)
        dcat_ref[...] = _dot_nt(dy, w_ref[...])
        for cols in _row_chains(D_MODEL):
            gacc[:, cols] += _dot_tn(cat_ref[...], dy[:, cols])

        @pl.when(i == nt - 1)
        def _():
            out = pltpu.make_async_copy(gacc, gw_ref, gsem)
            out.start()
            copies = _pair_copies(s_ins, s_outs, pssem, prsem)
            for cp in copies:
                cp.wait_recv()
            for cp in copies:
                cp.wait_send()
            out.wait()

    tile = pl.BlockSpec((tm, D_MODEL), lambda i: (i, 0))
    hbm = pl.BlockSpec(memory_space=pl.ANY)
    outs = pl.pallas_call(
        body, name="mix_out_bwd", grid=(nt,),
        in_specs=[tile, tile, tile, _full((D_MODEL, D_MODEL)), _full((6, D_MODEL)), _full((1, D_MODEL))]
        + [hbm] * ns,
        out_specs=[tile, _full((8, D_MODEL)), hbm] + [hbm] * ns,
        out_shape=[_big((T, D_MODEL), F32), jax.ShapeDtypeStruct((8, D_MODEL), F32), _big((D_MODEL, D_MODEL), F32)]
        + _pair_shapes(swap),
        scratch_shapes=[pltpu.VMEM((D_MODEL, D_MODEL), F32), pltpu.SemaphoreType.DMA] + _pair_sems(ns),
        compiler_params=_cp(("arbitrary",), 48),
    )(*_hbm(dx1, y, cat, w_out), modr, g_post, *[_hbm(g) for g in swap])
    return outs[:3], outs[3:]


def _mixers_bwd(p, dcat, ys, o, states, h1, wdw, vecs, lbl, pairs_b):
    T = p.shape[0]
    tm = min(MIXB_TILE, T)
    nt = T // tm
    nch = tm // CHUNK
    hpt = tm // HALO
    nx = len(pairs_b)
    nb = IN_COLS // N_CHIPS

    def body(*refs):
        (p_ref, ph_ref, dcat_ref, ys_ref, o_ref, st_ref, h1_ref, wdw_ref, bdw_ref, gain_ref, bias_ref, gout_ref,
         lbl_ref) = refs[:13]
        x_ins = refs[13:13 + nx]
        dp_ref, sb_ref, s5_ref, dw_ref = refs[13 + nx:17 + nx]
        x_outs = refs[17 + nx:17 + 2 * nx]
        gin_ref = refs[17 + 2 * nx]
        (ubuf, dybuf, carry, dstate, qt_s, kt_s, kh_s, v_s, do_s, egl_s, dqt_s, dkt_s, dkh_s, dv_s, dgl_s,
         dsh, dw8, dshift, lower_s, upper_s, same_s, gmat_s, gacc, dp_prev, h1_prev, gsem, xssem,
         xrsem) = refs[18 + 2 * nx:]
        i = pl.program_id(0)
        tile_idx = nt - 1 - i

        @pl.when(i == 0)
        def _():
            for cp in _xchg_copies(x_ins, x_outs, xssem, xrsem):
                cp.start()
            gacc[...] = jnp.zeros(gacc.shape, F32)
            dstate[...] = jnp.zeros(dstate.shape, F32)
            carry[...] = jnp.zeros(carry.shape, F32)
            sb_ref[...] = jnp.zeros(sb_ref.shape, F32)
            s5_ref[...] = jnp.zeros(s5_ref.shape, F32)
            dw_ref[...] = jnp.zeros(dw_ref.shape, F32)
            dw8[...] = jnp.zeros(dw8.shape, F32)
            lower_s[...], upper_s[...], same_s[...] = _chunk_masks(tm)
            gmat_s[...] = _gn_matrix()
            dsh[0:SUB, :] = jnp.zeros((SUB, CONV_CH), F32)
            dsh[SUB + tm:2 * SUB + tm, :] = jnp.zeros((SUB, CONV_CH), F32)
            ubuf[HALO + tm:HALO + tm + SUB, :] = jnp.zeros((SUB, CONV_CH), F32)
            dp_prev[...] = jnp.zeros(dp_prev.shape, BF16)
            h1_prev[...] = jnp.zeros(h1_prev.shape, BF16)

        n_pieces = (tm // CONV_ROWS) * (CONV_CH // LANE)
        per_block = n_pieces // N_CHIPS
        prow = D_MODEL // per_block

        def w_in_grad_piece(k):
            j, part = k // per_block, k % per_block
            rows_k = slice(part * prow, (part + 1) * prow)
            gacc[j, rows_k, :] += _dot_tn(h1_prev[:, rows_k], dp_prev[:, j * nb:(j + 1) * nb])

        uh = ph_ref[:, 0:CONV_CH] * _sig(ph_ref[:, CONV_CH:2 * CONV_CH])
        ubuf[0:HALO, :] = jnp.where(tile_idx > 0, uh, 0.0)
        ubuf[HALO:HALO + tm, :] = p_ref[:, 0:CONV_CH] * _sig(p_ref[:, CONV_CH:2 * CONV_CH])
        gmat = gmat_s[...]
        gain = gain_ref[...]
        yv = ys_ref[...]
        d = yv - _gmean(yv, gmat)
        rs = lax.rsqrt(_gmean(d * d, gmat) + GN_EPS)
        yn = d * rs
        z = yn * gain + bias_ref[...]
        sz = _sig(z)
        dz = dcat_ref[:, 0:CONV_CH] * (sz * (1.0 + z * (1.0 - sz)))
        dyn = dz * gain
        dyc = rs * (dyn - _gmean(dyn, gmat) - yn * _gmean(dyn * yn, gmat))
        s5_ref[0:1, :] += _colsum(dyc)
        s5_ref[1:2, :] += _colsum(dz * yn)
        s5_ref[2:3, :] += _colsum(dz)
        dybuf[tm:tm + HALO, :] = carry[...]
        dybuf[0:tm, :] = dyc
        dsh[SUB:SUB + tm, :] = dyc
        carry[...] = dyc[0:HALO, :]
        for b in range(SUB):
            dshift[...] = dsh[SUB - b:2 * SUB - b + tm, :]
            for j, off in CONV_FWD_TAPS:
                if off % SUB == b:
                    prod = dshift[...] * ubuf[off - b:off - b + tm + SUB, :]
                    dw8[j] += jnp.sum(prod.reshape((tm + SUB) // SUB, SUB, CONV_CH), axis=0)
        for r in range(tm // CONV_ROWS):
            rows = slice(r * CONV_ROWS, (r + 1) * CONV_ROWS)
            for lb_ in range(CONV_CH // LANE):
                lanes = slice(lb_ * LANE, (lb_ + 1) * LANE)
                glanes = slice(CONV_CH + lb_ * LANE, CONV_CH + (lb_ + 1) * LANE)
                w_in_grad_piece(r * (CONV_CH // LANE) + lb_)
                acc = _tap_conv(dybuf, wdw_ref, r * CONV_ROWS, CONV_BWD_TAPS, lanes)
                val = p_ref[rows, lanes]
                sg = _sig(p_ref[rows, glanes])
                dval = acc * sg
                dgate = acc * val * (sg * (1.0 - sg))
                dp_ref[rows, lanes] = dval.astype(BF16)
                dp_ref[rows, glanes] = dgate.astype(BF16)
                sb_ref[0:1, lanes] += _colsum(dval)
                sb_ref[0:1, glanes] += _colsum(dgate)

        o0 = 2 * CONV_CH
        for h in range(N_HEADS):
            sl = slice(h * HEAD_D, (h + 1) * HEAD_D)
            gsl = slice(o0 + 3 * HGRN_W + h * HEAD_D, o0 + 3 * HGRN_W + (h + 1) * HEAD_D)
            oh = o_ref[:, sl]
            gh = p_ref[:, gsl]
            dh = dcat_ref[:, CONV_CH + h * HEAD_D:CONV_CH + (h + 1) * HEAD_D]
            gout = gout_ref[:, sl]
            rsh = lax.rsqrt(jnp.mean(oh * oh, axis=-1, keepdims=True) + RMS_EPS)
            on = oh * rsh
            sgg = _sig(gh)
            dgh = dh * (on * gout) * (sgg * (1.0 + gh * (1.0 - sgg)))
            dm = dh * (gh * sgg)
            s5_ref[3:4, sl] += _colsum(dm * on)
            do_s[:, sl] = _rms_bwd(dm * gout, on, rsh).astype(BF16)
            dp_ref[:, gsl] = dgh.astype(BF16)
            sb_ref[2:3, CONV_CH + h * HEAD_D:CONV_CH + (h + 1) * HEAD_D] += _colsum(dgh)

        lb, _ = _lower_bound(lbl_ref)
        lower, upper, same = lower_s[...], upper_s[...], same_s[...]
        pq = p_ref[:, o0:o0 + HGRN_W]
        pr = _hgrn_prep(pq, p_ref[:, o0 + HGRN_W:o0 + 2 * HGRN_W], lb, lower, same)
        qt_s[...] = pr["qt"].astype(BF16)
        kt_s[...] = pr["kt"].astype(BF16)
        kh_s[...] = pr["kh"].astype(BF16)
        v_s[...] = p_ref[:, o0 + 2 * HGRN_W:o0 + 3 * HGRN_W].astype(BF16)
        egl_s[...] = jnp.exp(pr["Gl"])
        tri = _tri()

        def chunk(it, c_):
            ci = nch - 1 - it
            r0 = pl.multiple_of(ci * CHUNK, CHUNK)
            rows = pl.ds(r0, CHUNK)
            for h in range(N_HEADS):
                ls = pl.ds(h * HEAD_D, HEAD_D)
                qc, kc, hc, vc = qt_s[rows, ls], kt_s[rows, ls], kh_s[rows, ls], v_s[rows, ls]
                dob = do_s[rows, ls]
                s0 = st_ref[ci, h]
                s0b = s0.astype(BF16)
                ds1 = dstate[h]
                ds1b = ds1.astype(BF16)
                egl = egl_s[pl.ds(r0, 1), ls]
                att = jnp.where(tri, _dot_nt(qc, kc), 0.0).astype(BF16)
                datt = jnp.where(tri, _dot_nt(dob, vc), 0.0).astype(BF16)
                dv_s[rows, ls] = _dot_tn(att, dob) + _dot_nt(hc, ds1b)
                dqt_s[rows, ls] = _dot(datt, kc) + _dot(dob, s0b)
                dkt_s[rows, ls] = _dot_tn(datt, qc)
                dkh_s[rows, ls] = _dot(vc, ds1b)
                dgl = egl * _colsum(ds1 * s0)
                dgl_s[rows, ls] = jnp.broadcast_to(dgl, (CHUNK, HEAD_D))
                dstate[h] = ds1 * egl + _dot_tn(dob, qc)
            return c_

        lax.fori_loop(0, nch, chunk, 0)
        dqt, dkt, dkh = dqt_s[...], dkt_s[...], dkh_s[...]
        dk = dkt * pr["enG"] + dkh * pr["eGlG"]
        khk = dkh * kh_s[...].astype(F32)
        dG = dqt * qt_s[...].astype(F32) - dkt * kt_s[...].astype(F32) - khk
        dlogf = _mm3(upper, dG) + _mm3(same, khk) + dgl_s[...]
        df = dlogf / pr["f"] - dk
        sf, sq = pr["sf"], pr["sq"]
        s5_ref[4:5, :] += _colsum(df * (1.0 - sf))
        dfl = df * (1.0 - lb) * (sf * (1.0 - sf))
        dq = (dqt * pr["eG"]) * (sq * (1.0 + pq * (1.0 - sq)))
        dvv = dv_s[...]
        dp_ref[:, o0:o0 + HGRN_W] = dq.astype(BF16)
        dp_ref[:, o0 + HGRN_W:o0 + 2 * HGRN_W] = dfl.astype(BF16)
        dp_ref[:, o0 + 2 * HGRN_W:o0 + 3 * HGRN_W] = dvv.astype(BF16)
        sb_ref[1:2, 0:HGRN_W] += _colsum(dq)
        sb_ref[1:2, HGRN_W:2 * HGRN_W] += _colsum(dfl)
        sb_ref[2:3, 0:HGRN_W] += _colsum(dvv)

        dp_prev[...] = dp_ref[...]
        h1_prev[...] = h1_ref[...]

        @pl.when(i == nt - 1)
        def _():
            for k in range(n_pieces):
                w_in_grad_piece(k)
            out = pltpu.make_async_copy(gacc, gin_ref, gsem)
            out.start()
            for j in range(CONV_K):
                dw_ref[j:j + 1, :] = _colsum(dw8[j])
            copies = _xchg_copies(x_ins, x_outs, xssem, xrsem)
            for cp in copies:
                cp.wait_recv()
            for cp in copies:
                cp.wait_send()
            out.wait()

    rev = lambda cols: pl.BlockSpec((tm, cols), lambda i: (nt - 1 - i, 0))
    halo = pl.BlockSpec((HALO, 2 * CONV_CH), lambda i: (jnp.maximum((nt - 1 - i) * hpt - 1, 0), 0))
    wide = lambda n: pltpu.VMEM((tm, HGRN_W), n)
    hbm = pl.BlockSpec(memory_space=pl.ANY)
    outs = pl.pallas_call(
        body, name="mixers_bwd", grid=(nt,),
        in_specs=[rev(IN_COLS), halo, rev(D_MODEL), rev(CONV_CH), rev(HGRN_W),
                  pl.BlockSpec((nch, N_HEADS, HEAD_D, HEAD_D), lambda i: (nt - 1 - i, 0, 0, 0)), rev(D_MODEL),
                  _full((HALO, CONV_CH))] + [_full((1, CONV_CH))] * 4 + [_full((2, HGRN_W))] + [hbm] * nx,
        out_specs=[rev(IN_COLS), _full((8, D_MODEL)), _full((8, CONV_CH)), _full((HALO, CONV_CH))]
        + [hbm] * (nx + 1),
        out_shape=[_big((T, IN_COLS), BF16), jax.ShapeDtypeStruct((8, D_MODEL), F32),
                   jax.ShapeDtypeStruct((8, CONV_CH), F32), jax.ShapeDtypeStruct((HALO, CONV_CH), F32)]
        + [_big(pb.shape, BF16) for pb in pairs_b] + [_big((N_CHIPS, D_MODEL, nb), F32)],
        scratch_shapes=[pltpu.VMEM((tm + HALO + SUB, CONV_CH), F32), pltpu.VMEM((tm + HALO, CONV_CH), F32),
                        pltpu.VMEM((HALO, CONV_CH), F32), pltpu.VMEM((N_HEADS, HEAD_D, HEAD_D), F32),
                        wide(BF16), wide(BF16), wide(BF16), wide(BF16), wide(BF16),
                        wide(F32), wide(F32), wide(F32), wide(F32), wide(F32), wide(F32),
                        pltpu.VMEM((tm + 2 * SUB, CONV_CH), F32), pltpu.VMEM((HALO, SUB, CONV_CH), F32),
                        pltpu.VMEM((tm + SUB, CONV_CH), F32), pltpu.VMEM((tm, tm), BF16), pltpu.VMEM((tm, tm), BF16),
                        pltpu.VMEM((tm, tm), BF16), pltpu.VMEM((CONV_CH, CONV_CH), BF16),
                        pltpu.VMEM((N_CHIPS, D_MODEL, nb), F32), pltpu.VMEM((tm, IN_COLS), BF16),
                        pltpu.VMEM((tm, D_MODEL), BF16), pltpu.SemaphoreType.DMA]
        + _xchg_sems(nx),
        compiler_params=_cp(("arbitrary",), 56),
    )(*_hbm(p, p, dcat, ys, o, states, h1), wdw, *vecs, lbl, *[_hbm(pb) for pb in pairs_b])
    return outs[:4], outs[4:4 + nx], outs[4 + nx]


def _mix_in_bwd(dp, w_in_g, x, dx1, modr, g_pre, pairs_b):
    T = x.shape[0]
    tm = _tok_tile(T)
    nt = T // tm
    nb = IN_COLS // N_CHIPS
    nx = len(pairs_b)

    def body(*refs):
        dp_ref, w_ref, x_ref, dx1_ref, mod_ref, g_ref = refs[:6]
        x_ins = refs[6:6 + nx]
        gx_ref, st_ref = refs[6 + nx:8 + nx]
        x_outs = refs[8 + nx:8 + 2 * nx]
        xssem, xrsem = refs[8 + 2 * nx:]
        i = pl.program_id(0)

        @pl.when(i == 0)
        def _():
            for cp in _xchg_copies(x_ins, x_outs, xssem, xrsem):
                cp.start()
            st_ref[...] = jnp.zeros(st_ref.shape, F32)

        dh = None
        for j in range(N_CHIPS):
            part = _dot_nt(dp_ref[:, j * nb:(j + 1) * nb], w_ref[j])
            dh = part if dh is None else dh + part
        xv = x_ref[...]
        rs = lax.rsqrt(jnp.mean(xv * xv, axis=-1, keepdims=True) + RMS_EPS)
        xn = xv * rs
        st_ref[0:1, :] += _colsum(dh)
        st_ref[1:2, :] += _colsum(dh * (xn * g_ref[...]))
        dsc = dh * (1.0 + mod_ref[1:2, :])
        st_ref[2:3, :] += _colsum(dsc * xn)
        gx_ref[...] = dx1_ref[...] + _rms_bwd(dsc * g_ref[...], xn, rs)

        @pl.when(i == nt - 1)
        def _():
            copies = _xchg_copies(x_ins, x_outs, xssem, xrsem)
            for cp in copies:
                cp.wait_recv()
            for cp in copies:
                cp.wait_send()

    tile = pl.BlockSpec((tm, D_MODEL), lambda i: (i, 0))
    hbm = pl.BlockSpec(memory_space=pl.ANY)
    outs = pl.pallas_call(
        body, name="mix_in_bwd", grid=(nt,),
        in_specs=[pl.BlockSpec((tm, IN_COLS), lambda i: (i, 0)), _full((N_CHIPS, D_MODEL, nb)), tile, tile,
                  _full((6, D_MODEL)), _full((1, D_MODEL))] + [hbm] * nx,
        out_specs=[tile, _full((8, D_MODEL))] + [hbm] * nx,
        out_shape=[_big((T, D_MODEL), F32), jax.ShapeDtypeStruct((8, D_MODEL), F32)]
        + [_big(pb.shape, BF16) for pb in pairs_b],
        scratch_shapes=_xchg_sems(nx),
        compiler_params=_cp(("arbitrary",), 48),
    )(*_hbm(dp, w_in_g, x, dx1), modr, g_pre, *[_hbm(pb) for pb in pairs_b])
    return outs[:2], outs[2:]


def _weight_grad(a, b, a_blocked, b_blocked, name):
    T = a.shape[0]
    tt = min(GRAD_TILE, T)
    nt = T // tt
    ka = a.shape[1] // N_CHIPS if a_blocked else a.shape[1]
    nb = b.shape[1] // N_CHIPS if b_blocked else b.shape[1]

    def body(a_ref, b_ref, o_ref, ob_ref):
        t = pl.program_id(1)

        @pl.when(t == 0)
        def _():
            o_ref[...] = jnp.zeros(o_ref.shape, F32)

        for cols in _row_chains(nb):
            o_ref[0, :, cols] += _dot_tn(a_ref[...], b_ref[:, cols])

        @pl.when(t == nt - 1)
        def _():
            ob_ref[0] = o_ref[0].astype(BF16)

    blk = pl.BlockSpec((1, ka, nb), lambda j, t: (j, 0, 0))
    return pl.pallas_call(
        body, name=name, grid=(N_CHIPS, nt),
        in_specs=[pl.BlockSpec((tt, ka), (lambda j, t: (t, j)) if a_blocked else (lambda j, t: (t, 0))),
                  pl.BlockSpec((tt, nb), (lambda j, t: (t, j)) if b_blocked else (lambda j, t: (t, 0)))],
        out_specs=[blk, blk],
        out_shape=[_big((N_CHIPS, ka, nb), F32), _big((N_CHIPS, ka, nb), BF16)],
        compiler_params=_cp(("arbitrary", "arbitrary"), 48),
    )(*_hbm(a, b))


R_LOSS = 0
R_FFN = 8
R_OUT = 16
R_IN = 24
R_BIN = 32
R_512 = 40
R_DW = 48
N_STAT_ROWS = 80
MOD_ROWS = (R_IN + 0, R_IN + 1, R_OUT + 0, R_FFN + 0, R_FFN + 1, R_LOSS + 1)


def _small_update(gath, params):
    names = ["b_ada", "lb_logits", "g_pre_mix", "b_in", "b_dw", "gn_gain", "gn_bias", "g_hgrn_out", "g_post_mix",
             "g_pre_ffn", "g_post_ffn"]
    flat = []
    for n in names:
        flat += list(params[n])
    n_in = 1 + len(flat)

    def body(*refs):
        g_ref = refs[0]
        prm = {n: refs[1 + 3 * k:4 + 3 * k] for k, n in enumerate(names)}
        outs = refs[n_in:]
        loss_ref, dmod_ref, dwdw_ref = outs[0], outs[1], outs[2]
        res = {n: outs[3 + 4 * k:7 + 4 * k] for k, n in enumerate(names)}
        red = g_ref[0]
        for dev in range(1, N_DEV):
            red = red + g_ref[dev]
        loss_ref[...] = jnp.broadcast_to(
            (0.5 / D_MODEL) * jnp.sum(red[R_LOSS:R_LOSS + 1, :], axis=-1, keepdims=True), loss_ref.shape)
        for dev in range(N_DEV):
            for k, r in enumerate(MOD_ROWS):
                dmod_ref[dev:dev + 1, k * D_MODEL:(k + 1) * D_MODEL] = g_ref[dev, r:r + 1, :]
        dwdw_ref[...] = red[R_DW:R_DW + HALO, 0:CONV_CH]

        def finish(name, pieces):
            w_ref, m_ref, v_ref = prm[name]
            g_out, d_out, m_out, v_out = res[name]
            for rsl, lsl, g in pieces:
                d, m2, v2 = _adam_math(w_ref[rsl, lsl], g, m_ref[rsl, lsl], v_ref[rsl, lsl])
                g_out[rsl, lsl] = g
                d_out[rsl, lsl] = d
                m_out[rsl, lsl] = m2
                v_out[rsl, lsl] = v2

        one = slice(0, 1)
        row = lambda r: red[r:r + 1, :]
        half = lambda r: red[r:r + 1, 0:CONV_CH]
        finish("b_ada", [(one, slice(k * D_MODEL, (k + 1) * D_MODEL), row(r)) for k, r in enumerate(MOD_ROWS)])
        finish("b_in", [(one, slice(k * D_MODEL, (k + 1) * D_MODEL), row(R_BIN + k)) for k in range(3)])
        finish("g_pre_mix", [(one, slice(None), row(R_IN + 2))])
        finish("g_post_mix", [(one, slice(None), row(R_OUT + 1))])
        finish("g_pre_ffn", [(one, slice(None), row(R_FFN + 2))])
        finish("g_post_ffn", [(one, slice(None), row(R_LOSS + 2))])
        finish("b_dw", [(one, slice(None), half(R_512 + 0))])
        finish("gn_gain", [(one, slice(None), half(R_512 + 1))])
        finish("gn_bias", [(one, slice(None), half(R_512 + 2))])
        finish("g_hgrn_out", [(one, slice(None), half(R_512 + 3))])
        s0, s1 = _lower_bound(prm["lb_logits"][0])
        dlb = half(R_512 + 4)
        finish("lb_logits", [(slice(0, 1), slice(None), dlb * s0 * (1.0 - s0)),
                             (slice(1, 2), slice(None), -dlb * s0 * s1)])

    vm = pl.BlockSpec(memory_space=pltpu.VMEM)
    out_shape = [jax.ShapeDtypeStruct((8, 128), F32), jax.ShapeDtypeStruct((N_DEV, 6 * D_MODEL), F32),
                 jax.ShapeDtypeStruct((HALO, CONV_CH), F32)]
    for n in names:
        out_shape += [jax.ShapeDtypeStruct(params[n][0].shape, F32)] * 4
    outs = pl.pallas_call(
        body, name="small_update", out_shape=out_shape,
        in_specs=[vm] * n_in, out_specs=[vm] * len(out_shape),
        compiler_params=_cp(None, 32),
    )(gath, *flat)
    return outs[0], outs[1], outs[2], {n: outs[3 + 4 * k:7 + 4 * k] for k, n in enumerate(names)}


def _wdw_adam(w, g, m, v):
    def body(w_ref, g_ref, m_ref, v_ref, d_out, m_out, v_out):
        d, m2, v2 = _adam_math(w_ref[...], g_ref[...], m_ref[...], v_ref[...])
        d_out[...] = d
        m_out[...] = m2
        v_out[...] = v2

    vm = pl.BlockSpec(memory_space=pltpu.VMEM)
    return pl.pallas_call(
        body, name="wdw_adam", out_shape=[jax.ShapeDtypeStruct(w.shape, F32)] * 3,
        in_specs=[vm] * 4, out_specs=[vm] * 3, compiler_params=_cp(None, 16),
    )(w, g, m, v)


def kernel(x, c, w_ada, b_ada, lb_logits, g_pre_mix, w_in, b_in, w_dw, b_dw, gn_gain, gn_bias, g_hgrn_out, w_out, g_post_mix, g_pre_ffn, w_up, w_down, g_post_ffn, loss_target, m_w_ada, m_b_ada, m_lb_logits, m_g_pre_mix, m_w_in, m_b_in, m_w_dw, m_b_dw, m_gn_gain, m_gn_bias, m_g_hgrn_out, m_w_out, m_g_post_mix, m_g_pre_ffn, m_w_up, m_w_down, m_g_post_ffn, v_w_ada, v_b_ada, v_lb_logits, v_g_pre_mix, v_w_in, v_b_in, v_w_dw, v_b_dw, v_gn_gain, v_gn_bias, v_g_hgrn_out, v_w_out, v_g_post_mix, v_g_pre_ffn, v_w_up, v_w_down, v_g_post_ffn):
    ax, ay, ac = lax.axis_index("x"), lax.axis_index("y"), lax.axis_index("c")
    chip = 2 * ax + ay
    T = x.shape[1]
    xs, tgt = x[0], loss_target[0]
    ada_cols = w_ada.shape[2]

    b_sh = lax.dynamic_slice_in_dim(b_ada, chip * ada_cols, ada_cols, axis=1)
    wdw_pad = jnp.pad(w_dw[0], ((0, HALO - CONV_K), (0, 0)))
    chip1 = jnp.reshape(chip, (1,)).astype(jnp.int32)
    place = jnp.stack([ac, chip]).astype(jnp.int32)
    _, c8, modg, wdwg = _ada_exchange(c, w_ada[0], b_sh, wdw_pad)
    modr = modg.reshape(6, D_MODEL)
    wdw_all = jnp.transpose(wdwg, (1, 0, 2)).reshape(HALO, CONV_CH)
    bufs = {t: _cast_own(chip1, w[0], "cast_" + t)
            for w, t in ((w_in, "w_in"), (w_out, "w_out"), (w_up, "w_up"), (w_down, "w_down"))}
    vec = (b_dw, gn_gain, gn_bias, g_hgrn_out)

    p, h1, w_in_g, w_out_g = _mix_in_fwd(chip1, xs, modr, g_pre_mix, b_in.reshape(N_CHIPS, 1, IN_COLS // N_CHIPS),
                                          bufs["w_in"], bufs["w_out"])
    w_out_f = w_out_g.reshape(D_MODEL, D_MODEL)
    (cat, ys, o, states, y, x1, h2), (w_up_g, w_down_g) = _mixers_fwd(
        p, wdw_all, vec, lb_logits, w_out_f, xs, modr, g_post_mix, g_pre_ffn, [bufs["w_up"], bufs["w_down"]])
    w_down_f = w_down_g.reshape(D_FF, D_MODEL)
    r, dy2, dx2, st_loss = _ffn_fwd(h2, w_up_g, w_down_f, x1, tgt, modr, g_post_ffn)

    def pair_sums(grads, got, tags):
        return [_pair_sum(place, g, o_, "pair_sum_" + t) for (g, _), o_, t in zip(grads, got, tags)]

    da, dx1, st_ffn = _ffn_bwd(dy2, r, x1, dx2, w_up_g, w_down_f, modr, g_pre_ffn)
    g_up = _weight_grad(h2, da, False, True, "grad_w_up")
    g_down = _weight_grad(r, dy2, True, False, "grad_w_down")
    (dcat, st_out, g_out), got_ud = _mix_out_bwd(dx1, y, cat, w_out_f, modr, g_post_mix, [g_up[1], g_down[1]])
    g_out = g_out.reshape(N_CHIPS, D_MODEL // N_CHIPS, D_MODEL)
    got_o = _pair_swap([g_out], "pair_swap_w_out")
    early = pair_sums([(g_out, None), g_up, g_down], list(got_o) + list(got_ud), ["w_out", "w_up", "w_down"])
    (dp, st_bin, st_512, dwdw), got_early, g_in = _mixers_bwd(p, dcat, ys, o, states, h1, wdw_all, vec, lb_logits,
                                                              [pb for _, pb in early])
    late = pair_sums([(g_in, None)], _pair_swap([g_in], "pair_swap_w_in"), ["w_in"])
    (grad_x, st_in), got_late = _mix_in_bwd(dp, w_in_g, xs, dx1, modr, g_pre_mix, [late[0][1]])
    fulls = [_chip_sum(place, pf, gb, "chip_sum_" + t)
             for (pf, _), gb, t in zip(late + early, list(got_late) + list(got_early), ["w_in", "w_out", "w_up", "w_down"])]

    pad_lanes = lambda s: jnp.pad(s, ((0, 0), (0, D_MODEL - s.shape[1])))
    stats = jnp.concatenate([st_loss, st_ffn, st_out, st_in, st_bin, pad_lanes(st_512), pad_lanes(dwdw)], axis=0)
    (g_w_in, g_w_out, g_w_up, g_w_down), gath = _final_exchange(fulls, stats)
    small = {"b_ada": (b_ada, m_b_ada, v_b_ada), "lb_logits": (lb_logits, m_lb_logits, v_lb_logits),
             "g_pre_mix": (g_pre_mix, m_g_pre_mix, v_g_pre_mix), "b_in": (b_in, m_b_in, v_b_in),
             "b_dw": (b_dw, m_b_dw, v_b_dw), "gn_gain": (gn_gain, m_gn_gain, v_gn_gain),
             "gn_bias": (gn_bias, m_gn_bias, v_gn_bias), "g_hgrn_out": (g_hgrn_out, m_g_hgrn_out, v_g_hgrn_out),
             "g_post_mix": (g_post_mix, m_g_post_mix, v_g_post_mix), "g_pre_ffn": (g_pre_ffn, m_g_pre_ffn, v_g_pre_ffn),
             "g_post_ffn": (g_post_ffn, m_g_post_ffn, v_g_post_ffn)}
    loss_t, dmod_all, dwdw_sum, sres = _small_update(gath, small)
    loss = loss_t[0, 0]

    res = dict(sres)
    dmod_sh = lax.dynamic_slice_in_dim(dmod_all, chip * ada_cols, ada_cols, axis=1)
    res["w_ada"] = [t[None] for t in _ada_grad_adam(jnp.transpose(c8), dmod_sh, w_ada[0], m_w_ada[0], v_w_ada[0])]
    g_wdw = lax.dynamic_slice_in_dim(dwdw_sum, chip * HEAD_D, HEAD_D, axis=1)[:CONV_K][None]
    res["w_dw"] = [g_wdw] + list(_wdw_adam(w_dw, g_wdw, m_w_dw, v_w_dw))
    for name, g, w, m, v in (("w_in", g_w_in, w_in, m_w_in, v_w_in), ("w_out", g_w_out, w_out, m_w_out, v_w_out),
                             ("w_up", g_w_up, w_up, m_w_up, v_w_up), ("w_down", g_w_down, w_down, m_w_down, v_w_down)):
        d, m2, v2 = _adam_big(w[0], g, m[0], v[0], "adam_" + name)
        res[name] = [g[None], d[None], m2[None], v2[None]]

    order = ["w_ada", "b_ada", "lb_logits", "g_pre_mix", "w_in", "b_in", "w_dw", "b_dw", "gn_gain", "gn_bias",
             "g_hgrn_out", "w_out", "g_post_mix", "g_pre_ffn", "w_up", "w_down", "g_post_ffn"]
    out = [loss, grad_x[None]]
    for k in range(4):
        out += [res[n][k] for n in order]
    return tuple(out)
```

```python
import jax
import jax.numpy as jnp
from jax import lax
from jax.experimental import pallas as pl
from jax.experimental.pallas import tpu as pltpu

F32, BF16 = jnp.float32, jnp.bfloat16
D_MODEL = 1024
CONV_CH = 512
HGRN_W = 512
N_HEADS = 4
HEAD_D = 128
CONV_K = 31
GN_GROUP = 64
GN_SHIFT = 6
IN_COLS = 3072
D_FF = 4096
CHUNK = 64
CHUNK_SHIFT = 6
N_CHIPS = 4
N_DEV = 8
RMS_EPS = 1e-6
GN_EPS = 1e-5
ADAM_LR, ADAM_B1, ADAM_B2, ADAM_EPS, ADAM_WD, ADAM_STEP = 0.001, 0.9, 0.999, 1e-08, 0.01, 10
TOK_TILE = 512
MIXIN_TILE = 1024
MIXB_TILE = 256
FFN_TILE = 1024
FFN_BLOCK = 512
GRAD_TILE = 2048
HALO = 32
SUB = 8
LANE = 128
CONV_ROWS = 128
MIB = 1 << 20
MESH = pl.DeviceIdType.MESH
OTHER_CHIPS = ((0, 1), (1, 0), (1, 1))


def _cp(sem=None, vmem_mib=48):
    return pltpu.CompilerParams(dimension_semantics=sem, vmem_limit_bytes=vmem_mib * MIB)


def _dot(a, b):
    return jnp.dot(a, b, preferred_element_type=F32)


def _dot_nt(a, b):
    return lax.dot_general(a, b, (((1,), (1,)), ((), ())), preferred_element_type=F32)


def _dot_tn(a, b):
    return lax.dot_general(a, b, (((0,), (0,)), ((), ())), preferred_element_type=F32)


def _sig(v):
    return 0.5 * jnp.tanh(0.5 * v) + 0.5


def _colsum(v):
    return jnp.sum(v, axis=0, keepdims=True)


def _flip(v, b):
    return 1 - v if b else v


def _rcopy(src, dst, ssem, rsem, dev):
    return pltpu.make_async_remote_copy(src_ref=src, dst_ref=dst, send_sem=ssem, recv_sem=rsem,
                                        device_id=dev, device_id_type=MESH)


def _place():
    return lax.axis_index("x"), lax.axis_index("y"), lax.axis_index("c")


def _full(shape):
    return pl.BlockSpec(shape, lambda *_: (0,) * len(shape))


def _big(shape, dtype):
    return pltpu.HBM(shape, dtype)


def _hbm(*arrays):
    out = [pltpu.with_memory_space_constraint(a, pltpu.HBM) for a in arrays]
    return out[0] if len(out) == 1 else out


def _split2(v):
    hi = v.astype(BF16)
    lo = (v - hi.astype(F32)).astype(BF16)
    return hi, lo


def _split3(v):
    h1 = v.astype(BF16)
    r1 = v - h1.astype(F32)
    h2 = r1.astype(BF16)
    h3 = (r1 - h2.astype(F32)).astype(BF16)
    return h1, h2, h3


def _mm3s(mat, parts):
    h1, h2, h3 = parts
    return _dot(mat, h1) + _dot(mat, h2) + _dot(mat, h3)


def _mm3(mat, v):
    return _mm3s(mat, _split3(v))


def _gn_matrix():
    r = lax.broadcasted_iota(jnp.int32, (CONV_CH, CONV_CH), 0) >> GN_SHIFT
    c = lax.broadcasted_iota(jnp.int32, (CONV_CH, CONV_CH), 1) >> GN_SHIFT
    return jnp.where(r == c, 1.0 / GN_GROUP, 0.0).astype(BF16)


def _gmean(v, gmat):
    hi, lo = _split2(v)
    return _dot(hi, gmat) + _dot(lo, gmat)


def _chunk_masks(tm):
    r = lax.broadcasted_iota(jnp.int32, (tm, tm), 0)
    c = lax.broadcasted_iota(jnp.int32, (tm, tm), 1)
    same = (r >> CHUNK_SHIFT) == (c >> CHUNK_SHIFT)
    one = lambda m: jnp.where(m, 1.0, 0.0).astype(BF16)
    return one(same & (c <= r)), one(same & (c >= r)), one(same)


def _tri():
    return lax.broadcasted_iota(jnp.int32, (CHUNK, CHUNK), 0) >= lax.broadcasted_iota(jnp.int32, (CHUNK, CHUNK), 1)


def _lower_bound(lbl_ref):
    l0, l1 = lbl_ref[0:1, :], lbl_ref[1:2, :]
    mx = jnp.maximum(l0, l1)
    e0, e1 = jnp.exp(l0 - mx), jnp.exp(l1 - mx)
    return e0 / (e0 + e1), e1 / (e0 + e1)


CONV_FWD_TAPS = tuple((j, HALO - (CONV_K - 1) + j) for j in range(CONV_K))
CONV_BWD_TAPS = tuple((j, (CONV_K - 1) - j) for j in range(CONV_K))


def _tap_conv(src_ref, w_ref, row0, taps, lanes):
    acc = None
    for b in range(SUB):
        pb = None
        for j, off in taps:
            if off % SUB == b:
                lo = row0 + off - b
                term = w_ref[j:j + 1, lanes] * src_ref[lo:lo + CONV_ROWS + SUB, lanes]
                pb = term if pb is None else pb + term
        if pb is not None:
            sh = pb[b:b + CONV_ROWS, :]
            acc = sh if acc is None else acc + sh
    return acc


def _hgrn_prep(pq, pf, lb, lower, same):
    sq = _sig(pq)
    qf = pq * sq
    sf = _sig(pf)
    f = lb + (1.0 - lb) * sf
    logf = jnp.log(f)
    k = 1.0 - f
    parts = _split3(logf)
    G = _mm3s(lower, parts)
    Gl = _mm3s(same, parts)
    eG, enG, eGlG = jnp.exp(G), jnp.exp(-G), jnp.exp(Gl - G)
    return dict(sq=sq, sf=sf, f=f, Gl=Gl, eG=eG, enG=enG, eGlG=eGlG, qt=qf * eG, kt=k * enG, kh=k * eGlG)


def _ada_exchange(c_row, w_ada, b_sh, wdw_pad):
    ncol = w_ada.shape[1]

    def body(c_ref, w_ref, b_ref, wdw_ref, call_ref, c8_ref, modg_ref, wdwg_ref, rows_s, sa, ra, sw, rw, sm, rm):
        x, y, c = _place()
        me = 4 * x + 2 * y + c
        chip = 2 * x + y
        cv = c_ref[...]
        call_ref[me] = cv * _sig(cv)
        wdwg_ref[chip] = wdw_ref[...]
        sends = []
        for m in range(1, N_DEV):
            peer = (_flip(x, m >> 2), _flip(y, (m >> 1) & 1), _flip(c, m & 1))
            cp = _rcopy(call_ref.at[me], call_ref.at[me], sa.at[m - 1], ra.at[m - 1], peer)
            cp.start()
            sends.append(cp)
        for k, (fx, fy) in enumerate(OTHER_CHIPS):
            peer = (_flip(x, fx), _flip(y, fy), c)
            cp = _rcopy(wdwg_ref.at[chip], wdwg_ref.at[chip], sw.at[k], rw.at[k], peer)
            cp.start()
            sends.append(cp)
        for m in range(1, N_DEV):
            peer = (_flip(x, m >> 2), _flip(y, (m >> 1) & 1), _flip(c, m & 1))
            pid = 4 * peer[0] + 2 * peer[1] + peer[2]
            _rcopy(call_ref.at[pid], call_ref.at[pid], sa.at[m - 1], ra.at[m - 1], peer).wait_recv()
        for b in range(N_DEV):
            c8_ref[b:b + 1, :] = call_ref[b]
        mod_all = _dot(c8_ref[...].astype(BF16), w_ref[...].astype(BF16)) + b_ref[...]
        for b in range(N_DEV):
            rows_s[b] = mod_all[b:b + 1, :]
        modg_ref[chip] = rows_s[me]
        for k, (fx, fy) in enumerate(OTHER_CHIPS):
            peer = (_flip(x, fx), _flip(y, fy), c)
            pid = 4 * peer[0] + 2 * peer[1] + peer[2]
            cp = _rcopy(rows_s.at[pid], modg_ref.at[chip], sm.at[k], rm.at[k], peer)
            cp.start()
            sends.append(cp)
        for k, (fx, fy) in enumerate(OTHER_CHIPS):
            peer = (_flip(x, fx), _flip(y, fy), c)
            pchip = 2 * peer[0] + peer[1]
            _rcopy(rows_s.at[0], modg_ref.at[pchip], sm.at[k], rm.at[k], peer).wait_recv()
            _rcopy(wdwg_ref.at[pchip], wdwg_ref.at[pchip], sw.at[k], rw.at[k], peer).wait_recv()
        for cp in sends:
            cp.wait_send()

    vm = pl.BlockSpec(memory_space=pltpu.VMEM)
    return pl.pallas_call(
        body, name="ada_exchange",
        out_shape=[jax.ShapeDtypeStruct((N_DEV, 1, D_MODEL), F32), jax.ShapeDtypeStruct((N_DEV, D_MODEL), F32),
                   jax.ShapeDtypeStruct((N_CHIPS, 1, ncol), F32), jax.ShapeDtypeStruct((N_CHIPS, HALO, HEAD_D), F32)],
        in_specs=[vm] * 4, out_specs=[vm] * 4,
        scratch_shapes=[pltpu.VMEM((N_DEV, 1, ncol), F32),
                        pltpu.SemaphoreType.DMA((N_DEV - 1,)), pltpu.SemaphoreType.DMA((N_DEV - 1,)),
                        pltpu.SemaphoreType.DMA((3,)), pltpu.SemaphoreType.DMA((3,)),
                        pltpu.SemaphoreType.DMA((3,)), pltpu.SemaphoreType.DMA((3,))],
        compiler_params=_cp(None, 32),
    )(c_row, w_ada, b_sh, wdw_pad)


def _cast_own(chip1, shard, name):
    rows, cols = shard.shape
    tr = _row_tile(rows)

    def body(ch_ref, s_ref, o_ref):
        o_ref[0] = s_ref[...].astype(BF16)

    gs = pltpu.PrefetchScalarGridSpec(
        num_scalar_prefetch=1, grid=(rows // tr,),
        in_specs=[pl.BlockSpec((tr, cols), lambda i, ch: (i, 0))],
        out_specs=pl.BlockSpec((1, tr, cols), lambda i, ch: (ch[0], i, 0)))
    return pl.pallas_call(
        body, name=name, grid_spec=gs, out_shape=_big((N_CHIPS, rows, cols), BF16),
        compiler_params=_cp(("arbitrary",), 32),
    )(chip1, _hbm(shard))


def _slab(buf, ch, core):
    hs = buf.shape[1] // 2
    return buf.at[ch, pl.ds(core * hs, hs), :]


def _gather_start(bufs, ssem, rsem, relations=(0, 1, 2)):
    x, y, c = _place()
    chip = 2 * x + y
    for k in relations:
        fx, fy = OTHER_CHIPS[k]
        peer = (_flip(x, fx), _flip(y, fy), c)
        for t, buf in enumerate(bufs):
            _rcopy(_slab(buf, chip, c), _slab(buf, chip, c), ssem.at[t * 3 + k], rsem.at[t * 3 + k], peer).start()


def _gather_pass_on(bufs, ssem, rsem):
    nt = len(bufs)
    x, y, c = _place()
    sibling = (x, y, 1 - c)
    for k, (fx, fy) in enumerate(OTHER_CHIPS):
        peer = (_flip(x, fx), _flip(y, fy), c)
        pchip = 2 * peer[0] + peer[1]
        for t, buf in enumerate(bufs):
            _rcopy(_slab(buf, pchip, c), _slab(buf, pchip, c), ssem.at[t * 3 + k], rsem.at[t * 3 + k], peer).wait_recv()
            _rcopy(_slab(buf, pchip, c), _slab(buf, pchip, c), ssem.at[3 * nt + t * 3 + k],
                   rsem.at[3 * nt + t * 3 + k], sibling).start()


def _gather_drain(bufs, ssem, rsem):
    nt = len(bufs)
    x, y, c = _place()
    chip = 2 * x + y
    sibling = (x, y, 1 - c)
    for k, (fx, fy) in enumerate(OTHER_CHIPS):
        peer = (_flip(x, fx), _flip(y, fy), c)
        pchip = 2 * peer[0] + peer[1]
        for t, buf in enumerate(bufs):
            _rcopy(_slab(buf, pchip, 1 - c), _slab(buf, pchip, 1 - c), ssem.at[3 * nt + t * 3 + k],
                   rsem.at[3 * nt + t * 3 + k], sibling).wait_recv()
            _rcopy(_slab(buf, chip, c), _slab(buf, chip, c), ssem.at[t * 3 + k], rsem.at[t * 3 + k], peer).wait_send()
            _rcopy(_slab(buf, pchip, c), _slab(buf, pchip, c), ssem.at[3 * nt + t * 3 + k],
                   rsem.at[3 * nt + t * 3 + k], sibling).wait_send()


def _gather_finish(bufs, ssem, rsem):
    _gather_pass_on(bufs, ssem, rsem)
    _gather_drain(bufs, ssem, rsem)


def _gather_arrive(bufs, k, ssem, rsem):
    nt = len(bufs)
    x, y, c = _place()
    fx, fy = OTHER_CHIPS[k]
    peer = (_flip(x, fx), _flip(y, fy), c)
    pchip = 2 * peer[0] + peer[1]
    for t, buf in enumerate(bufs):
        _rcopy(_slab(buf, pchip, c), _slab(buf, pchip, c), ssem.at[t * 3 + k], rsem.at[t * 3 + k], peer).wait_recv()
        _rcopy(_slab(buf, pchip, c), _slab(buf, pchip, c), ssem.at[3 * nt + t * 3 + k],
               rsem.at[3 * nt + t * 3 + k], (x, y, 1 - c)).start()
    for t, buf in enumerate(bufs):
        _rcopy(_slab(buf, pchip, 1 - c), _slab(buf, pchip, 1 - c), ssem.at[3 * nt + t * 3 + k],
               rsem.at[3 * nt + t * 3 + k], (x, y, 1 - c)).wait_recv()


def _gather_sends_done(bufs, ssem, rsem):
    nt = len(bufs)
    x, y, c = _place()
    chip = 2 * x + y
    for k, (fx, fy) in enumerate(OTHER_CHIPS):
        peer = (_flip(x, fx), _flip(y, fy), c)
        pchip = 2 * peer[0] + peer[1]
        for t, buf in enumerate(bufs):
            _rcopy(_slab(buf, chip, c), _slab(buf, chip, c), ssem.at[t * 3 + k], rsem.at[t * 3 + k], peer).wait_send()
            _rcopy(_slab(buf, pchip, c), _slab(buf, pchip, c), ssem.at[3 * nt + t * 3 + k],
                   rsem.at[3 * nt + t * 3 + k], (x, y, 1 - c)).wait_send()


def _gather_sems(nt):
    return [pltpu.SemaphoreType.DMA((6 * nt,)), pltpu.SemaphoreType.DMA((6 * nt,))]


def _pair_copies(ins, outs, ssem, rsem):
    x, y, c = _place()
    copies = []
    for t in range(len(ins)):
        hs = ins[t].shape[1] // 2
        copies.append(_rcopy(ins[t].at[:, pl.ds((1 - c) * hs, hs), :], outs[t], ssem.at[t], rsem.at[t], (x, y, 1 - c)))
    return copies


def _pair_shapes(grads):
    return [_big((g.shape[0], g.shape[1] // 2, g.shape[2]), g.dtype) for g in grads]


def _pair_sems(nt):
    return [pltpu.SemaphoreType.DMA((nt,)), pltpu.SemaphoreType.DMA((nt,))]


def _pair_swap(grads, name):
    nt = len(grads)
    hbm = pl.BlockSpec(memory_space=pl.ANY)

    def body(*refs):
        copies = _pair_copies(refs[:nt], refs[nt:2 * nt], refs[2 * nt], refs[2 * nt + 1])
        for cp in copies:
            cp.start()
        for cp in copies:
            cp.wait_recv()
        for cp in copies:
            cp.wait_send()

    return pl.pallas_call(
        body, name=name, out_shape=_pair_shapes(grads), in_specs=[hbm] * nt, out_specs=[hbm] * nt,
        scratch_shapes=_pair_sems(nt),
    )(*[_hbm(g) for g in grads])


def _xchg_copies(ins, outs, ssem, rsem):
    x, y, c = _place()
    copies = []
    for k, (fx, fy) in enumerate(OTHER_CHIPS):
        peer = (_flip(x, fx), _flip(y, fy), c)
        for t in range(len(ins)):
            copies.append(_rcopy(ins[t].at[k], outs[t].at[k], ssem.at[t * 3 + k], rsem.at[t * 3 + k], peer))
    return copies


def _xchg_sems(nt):
    return [pltpu.SemaphoreType.DMA((3 * nt,)), pltpu.SemaphoreType.DMA((3 * nt,))]


def _final_exchange(fulls, stats):
    nt = len(fulls)
    rows, cols = stats.shape
    hbm = pl.BlockSpec(memory_space=pl.ANY)
    vm = pl.BlockSpec(memory_space=pltpu.VMEM)

    def body(*refs):
        ins, s_ref = refs[:nt], refs[nt]
        outs, g_ref = refs[nt + 1:2 * nt + 1], refs[2 * nt + 1]
        hssem, hrsem, ssem, rsem = refs[2 * nt + 2:]
        x, y, c = _place()
        me, sibling = (x, y, c), (x, y, 1 - c)
        halves = []
        for t in range(nt):
            hs = ins[t].shape[0] // 2
            mine = pl.ds(c * hs, hs)
            cp = _rcopy(ins[t].at[mine, :], outs[t].at[mine, :], hssem.at[t], hrsem.at[t], sibling)
            cp.start()
            halves.append(cp)

        chips = [(_flip(x, fx), _flip(y, fy)) for fx, fy in OTHER_CHIPS]

        def blk(px, py, pc):
            return g_ref.at[4 * px + 2 * py + pc]

        def copy(k, block, to, src=None):
            return _rcopy(blk(*block) if src is None else src, blk(*block), ssem.at[k], rsem.at[k], to)

        g_ref[4 * x + 2 * y + c] = s_ref[...]
        first = [copy(0, me, sibling, src=s_ref)]
        first += [copy(1 + j, me, (*chip, c), src=s_ref) for j, chip in enumerate(chips)]
        for cp in first:
            cp.start()
        passed = [copy(4 + j, (*chip, c), sibling) for j, chip in enumerate(chips)]
        for j, chip in enumerate(chips):
            copy(1 + j, (*chip, c), me).wait_recv()
            passed[j].start()
        copy(0, sibling, me).wait_recv()
        for j, chip in enumerate(chips):
            copy(4 + j, (*chip, 1 - c), me).wait_recv()
        for t in range(nt):
            hs = ins[t].shape[0] // 2
            other = pl.ds((1 - c) * hs, hs)
            _rcopy(ins[t].at[other, :], outs[t].at[other, :], hssem.at[t], hrsem.at[t], sibling).wait_recv()
        for cp in first + passed + halves:
            cp.wait_send()

    outs = pl.pallas_call(
        body, name="final_exchange",
        out_shape=[_big(f.shape, F32) for f in fulls] + [jax.ShapeDtypeStruct((N_DEV, rows, cols), F32)],
        in_specs=[hbm] * nt + [vm], out_specs=[hbm] * nt + [vm],
        input_output_aliases={t: t for t in range(nt)},
        scratch_shapes=[pltpu.SemaphoreType.DMA((nt,)), pltpu.SemaphoreType.DMA((nt,)),
                        pltpu.SemaphoreType.DMA((7,)), pltpu.SemaphoreType.DMA((7,))],
        compiler_params=_cp(None, 32),
    )(*[_hbm(f) for f in fulls], stats)
    return outs[:nt], outs[nt]


def _row_tile(rows):
    return min(rows, 512)


def _pair_sum(place, grad, got, name):
    nb, hs, cols = got.shape
    tr = _row_tile(hs)
    nr = hs // tr

    def body(pl_ref, g_ref, o_ref, pf_ref, pb_ref):
        j = pl.program_id(1)
        s = g_ref[0] + o_ref[0].astype(F32)

        @pl.when(j == 0)
        def _():
            pf_ref[...] = s

        @pl.when(j > 0)
        def _():
            pb_ref[0] = s.astype(BF16)

    gs = pltpu.PrefetchScalarGridSpec(
        num_scalar_prefetch=1, grid=(nr, nb),
        in_specs=[pl.BlockSpec((1, tr, cols), lambda i, j, p: (p[1] ^ j, p[0] * nr + i, 0)),
                  pl.BlockSpec((1, tr, cols), lambda i, j, p: (p[1] ^ j, i, 0))],
        out_specs=[pl.BlockSpec((tr, cols), lambda i, j, p: (i, 0)),
                   pl.BlockSpec((1, tr, cols), lambda i, j, p: (jnp.maximum(j - 1, 0), i, 0))])
    return pl.pallas_call(
        body, name=name, grid_spec=gs,
        out_shape=[_big((hs, cols), F32), _big((nb - 1, hs, cols), BF16)],
        compiler_params=_cp(("arbitrary", "arbitrary"), 32),
    )(place, *_hbm(grad, got))


def _chip_sum(place, pair_f, got_b, name):
    nb, hs, cols = got_b.shape
    tr = _row_tile(hs)
    nr = hs // tr

    def body(pl_ref, pf_ref, gb_ref, o_ref):
        acc = pf_ref[...]
        for k in range(nb):
            acc = acc + gb_ref[k].astype(F32)
        o_ref[...] = acc

    gs = pltpu.PrefetchScalarGridSpec(
        num_scalar_prefetch=1, grid=(nr,),
        in_specs=[pl.BlockSpec((tr, cols), lambda i, p: (i, 0)),
                  pl.BlockSpec((nb, tr, cols), lambda i, p: (0, i, 0))],
        out_specs=pl.BlockSpec((tr, cols), lambda i, p: (p[0] * nr + i, 0)))
    return pl.pallas_call(
        body, name=name, grid_spec=gs,
        out_shape=_big((2 * hs, cols), F32),
        compiler_params=_cp(("arbitrary",), 32),
    )(place, *_hbm(pair_f, got_b))


def _adam_math(w, g, m, v):
    m2 = ADAM_B1 * m + (1.0 - ADAM_B1) * g
    v2 = ADAM_B2 * v + (1.0 - ADAM_B2) * (g * g)
    m_hat = m2 / (1.0 - ADAM_B1 ** ADAM_STEP)
    v_hat = v2 / (1.0 - ADAM_B2 ** ADAM_STEP)
    delta = -ADAM_LR * (m_hat / (jnp.sqrt(v_hat) + ADAM_EPS) + ADAM_WD * w)
    return delta, m2, v2


def _adam_big(w, g, m, v, name):
    rows, cols = w.shape
    tr = _row_tile(rows)

    def body(w_ref, g_ref, m_ref, v_ref, d_out, m_out, v_out):
        d, m2, v2 = _adam_math(w_ref[...], g_ref[...], m_ref[...], v_ref[...])
        d_out[...] = d
        m_out[...] = m2
        v_out[...] = v2

    spec = pl.BlockSpec((tr, cols), lambda i: (i, 0))
    return pl.pallas_call(
        body, name=name, grid=(rows // tr,), in_specs=[spec] * 4, out_specs=[spec] * 3,
        out_shape=[_big(w.shape, F32)] * 3,
        compiler_params=_cp(("arbitrary",), 48),
    )(*_hbm(w, g, m, v))


def _ada_grad_adam(c8t, dmod_sh, w, m, v):
    rows, cols = w.shape
    tr = _row_tile(rows) // 2

    def body(ct_ref, dm_ref, w_ref, m_ref, v_ref, g_out, d_out, m_out, v_out):
        g = None
        for b in range(N_DEV):
            term = ct_ref[:, b:b + 1] * dm_ref[b:b + 1, :]
            g = term if g is None else g + term
        d, m2, v2 = _adam_math(w_ref[...], g, m_ref[...], v_ref[...])
        g_out[...] = g
        d_out[...] = d
        m_out[...] = m2
        v_out[...] = v2

    spec = pl.BlockSpec((tr, cols), lambda i: (i, 0))
    return pl.pallas_call(
        body, name="ada_grad_adam", grid=(rows // tr,),
        in_specs=[pl.BlockSpec((tr, N_DEV), lambda i: (i, 0)), _full((N_DEV, cols)), spec, spec, spec],
        out_specs=[spec] * 4, out_shape=[_big(w.shape, F32)] * 4,
        compiler_params=_cp(("arbitrary",), 32),
    )(c8t, dmod_sh, *_hbm(w, m, v))


def _tok_tile(t):
    return min(TOK_TILE, t)


def _mix_in_fwd(chip1, x, modr, g_pre, b_in4, w_in_buf, w_out_buf):
    T = x.shape[0]
    tm = min(MIXIN_TILE, T)
    nt = T // tm
    nb = IN_COLS // N_CHIPS

    def body(ch_ref, x_ref, mod_ref, g_ref, b_ref, win_in, wout_in, p_ref, h_ref, win_ref, wout_ref,
             h_all, wblk, lsem, is_sem, ir_sem, os_sem, or_sem):
        k, i = pl.program_id(0), pl.program_id(1)
        chip = ch_ref[0]

        def load_block(blk):
            cp = pltpu.make_async_copy(win_ref.at[blk], wblk, lsem)
            cp.start()
            cp.wait()

        @pl.when((k == 0) & (i == 0))
        def _():
            _gather_start([win_ref], is_sem, ir_sem, relations=(0, 1))
            load_block(chip)

        for r in range(N_CHIPS - 1):
            @pl.when((k == r + 1) & (i == 0))
            def _(r=r):
                _gather_arrive([win_ref], r, is_sem, ir_sem)
                if r == 0:
                    _gather_start([win_ref], is_sem, ir_sem, relations=(2,))
                if r == 1:
                    _gather_start([wout_ref], os_sem, or_sem)
                load_block(chip ^ (r + 1))

        rows = pl.ds(pl.multiple_of(i * tm, tm), tm)

        @pl.when(k == 0)
        def _():
            xv = x_ref[...]
            rstd = lax.rsqrt(jnp.mean(xv * xv, axis=-1, keepdims=True) + RMS_EPS)
            h = (xv * rstd) * g_ref[...] * (1.0 + mod_ref[1:2, :]) + mod_ref[0:1, :]
            hb = h.astype(BF16)
            h_ref[...] = hb
            h_all[rows, :] = hb

        p_ref[...] = _dot(h_all[rows, :], wblk[...]) + b_ref[chip ^ k]

        @pl.when((k == N_CHIPS - 1) & (i == nt - 1))
        def _():
            _gather_sends_done([win_ref], is_sem, ir_sem)
            _gather_finish([wout_ref], os_sem, or_sem)

    hbm = pl.BlockSpec(memory_space=pl.ANY)
    first_pass = lambda k, i, ch: (jnp.where(k == 0, i, nt - 1), 0)
    gs = pltpu.PrefetchScalarGridSpec(
        num_scalar_prefetch=1, grid=(N_CHIPS, nt),
        in_specs=[pl.BlockSpec((tm, D_MODEL), first_pass), pl.BlockSpec((6, D_MODEL), lambda k, i, ch: (0, 0)),
                  pl.BlockSpec((1, D_MODEL), lambda k, i, ch: (0, 0)),
                  pl.BlockSpec((N_CHIPS, 1, nb), lambda k, i, ch: (0, 0, 0)), hbm, hbm],
        out_specs=[pl.BlockSpec((tm, nb), lambda k, i, ch: (i, ch[0] ^ k)), pl.BlockSpec((tm, D_MODEL), first_pass),
                   hbm, hbm],
        scratch_shapes=[pltpu.VMEM((T, D_MODEL), BF16), pltpu.VMEM((D_MODEL, nb), BF16), pltpu.SemaphoreType.DMA]
        + _gather_sems(1) + _gather_sems(1))
    return pl.pallas_call(
        body, name="mix_in_fwd", grid_spec=gs,
        out_shape=[_big((T, IN_COLS), F32), _big((T, D_MODEL), BF16), _big(w_in_buf.shape, BF16),
                   _big(w_out_buf.shape, BF16)],
        input_output_aliases={5: 2, 6: 3},
        compiler_params=_cp(("arbitrary", "arbitrary"), 48),
    )(chip1, _hbm(x), modr, g_pre, b_in4, _hbm(w_in_buf), _hbm(w_out_buf))


def _mixers_fwd(p, wdw, vecs, lbl, w_out, x, modr, g_post, g_ffn, gbufs):
    T = p.shape[0]
    tm = _tok_tile(T)
    nt = T // tm
    nch = tm // CHUNK
    ng = len(gbufs)
    n_in, n_out = 12, 7

    def body(*refs):
        (p_ref, wdw_ref, bdw_ref, gain_ref, bias_ref, gout_ref, lbl_ref, wout_ref, x_ref, mod_ref, gp_ref,
         gf_ref) = refs[:n_in]
        cat_ref, ys_ref, o_ref, st_ref, y_ref, x1_ref, h2_ref = refs[n_in + ng:n_in + ng + n_out]
        gout_bufs = refs[n_in + ng + n_out:n_in + 2 * ng + n_out]
        (ubuf, state, qt_s, kt_s, kh_s, v_s, egl_s, lower_s, same_s, gmat_s, gssem,
         grsem) = refs[n_in + 2 * ng + n_out:]
        i = pl.program_id(0)

        @pl.when(i == 0)
        def _():
            _gather_start(gout_bufs, gssem, grsem)
            lower_s[...], _, same_s[...] = _chunk_masks(tm)
            gmat_s[...] = _gn_matrix()
            state[...] = jnp.zeros(state.shape, F32)
            ubuf[0:HALO, :] = jnp.zeros((HALO, CONV_CH), F32)
            ubuf[HALO + tm:HALO + tm + SUB, :] = jnp.zeros((SUB, CONV_CH), F32)

        @pl.when(i > 0)
        def _():
            ubuf[0:HALO, :] = ubuf[tm:tm + HALO, :]

        ubuf[HALO:HALO + tm, :] = p_ref[:, 0:CONV_CH] * _sig(p_ref[:, CONV_CH:2 * CONV_CH])
        for r in range(tm // CONV_ROWS):
            rows = slice(r * CONV_ROWS, (r + 1) * CONV_ROWS)
            for lb_ in range(CONV_CH // LANE):
                lanes = slice(lb_ * LANE, (lb_ + 1) * LANE)
                ys_ref[rows, lanes] = bdw_ref[:, lanes] + _tap_conv(ubuf, wdw_ref, r * CONV_ROWS, CONV_FWD_TAPS, lanes)
        gmat = gmat_s[...]
        yv = ys_ref[...]
        d = yv - _gmean(yv, gmat)
        rs = lax.rsqrt(_gmean(d * d, gmat) + GN_EPS)
        z = d * rs * gain_ref[...] + bias_ref[...]
        cat_ref[:, 0:CONV_CH] = (z * _sig(z)).astype(BF16)

        lb, _ = _lower_bound(lbl_ref)
        lower, same = lower_s[...], same_s[...]
        o0 = 2 * CONV_CH
        pr = _hgrn_prep(p_ref[:, o0:o0 + HGRN_W], p_ref[:, o0 + HGRN_W:o0 + 2 * HGRN_W], lb, lower, same)
        qt_s[...] = pr["qt"].astype(BF16)
        kt_s[...] = pr["kt"].astype(BF16)
        kh_s[...] = pr["kh"].astype(BF16)
        v_s[...] = p_ref[:, o0 + 2 * HGRN_W:o0 + 3 * HGRN_W].astype(BF16)
        egl_s[...] = jnp.exp(pr["Gl"])
        tri = _tri()

        def chunk(ci, carry):
            r0 = pl.multiple_of(ci * CHUNK, CHUNK)
            rows = pl.ds(r0, CHUNK)
            for h in range(N_HEADS):
                ls = pl.ds(h * HEAD_D, HEAD_D)
                qc, kc, hc, vc = qt_s[rows, ls], kt_s[rows, ls], kh_s[rows, ls], v_s[rows, ls]
                s0 = state[h]
                s0b = s0.astype(BF16)
                st_ref[ci, h] = s0
                att =jnp.where(tri, _dot_nt(qc, kc), 0.0).astype(BF16)
                o_ref[rows, ls] = _dot(att, vc) + _dot_nt(qc, s0b)
                state[h] = s0 * egl_s[pl.ds(r0, 1), ls] + _dot_tn(vc, hc)
            return carry

        lax.fori_loop(0, nch, chunk, 0)
        for h in range(N_HEADS):
            sl = slice(h * HEAD_D, (h + 1) * HEAD_D)
            oh = o_ref[:, sl]
            gh = p_ref[:, o0 + 3 * HGRN_W + h * HEAD_D:o0 + 3 * HGRN_W + (h + 1) * HEAD_D]
            rsh = lax.rsqrt(jnp.mean(oh * oh, axis=-1, keepdims=True) + RMS_EPS)
            hg = (oh * rsh) * gout_ref[:, sl] * (gh * _sig(gh))
            cat_ref[:, CONV_CH + h * HEAD_D:CONV_CH + (h + 1) * HEAD_D] = hg.astype(BF16)

        yv = _dot(cat_ref[...], wout_ref[...])
        y_ref[...] = yv
        rsy = lax.rsqrt(jnp.mean(yv * yv, axis=-1, keepdims=True) + RMS_EPS)
        x1 = x_ref[...] + mod_ref[2:3, :] * ((yv * rsy) * gp_ref[...])
        x1_ref[...] = x1
        rs1 = lax.rsqrt(jnp.mean(x1 * x1, axis=-1, keepdims=True) + RMS_EPS)
        h2 = (x1 * rs1) * gf_ref[...] * (1.0 + mod_ref[4:5, :]) + mod_ref[3:4, :]
        h2_ref[...] = h2.astype(BF16)

        @pl.when(i == max(nt - 2, 0))
        def _():
            _gather_pass_on(gout_bufs, gssem, grsem)

        @pl.when(i == nt - 1)
        def _():
            _gather_drain(gout_bufs, gssem, grsem)

    tile = lambda cols: pl.BlockSpec((tm, cols), lambda i: (i, 0))
    hbm = pl.BlockSpec(memory_space=pl.ANY)
    outs = pl.pallas_call(
        body, name="mixers_fwd", grid=(nt,),
        in_specs=[tile(IN_COLS), _full((HALO, CONV_CH))] + [_full((1, CONV_CH))] * 4 + [_full((2, HGRN_W))]
        + [_full((D_MODEL, D_MODEL)), tile(D_MODEL), _full((6, D_MODEL)), _full((1, D_MODEL)), _full((1, D_MODEL))]
        + [hbm] * ng,
        out_specs=[tile(D_MODEL), tile(CONV_CH), tile(HGRN_W),
                   pl.BlockSpec((nch, N_HEADS, HEAD_D, HEAD_D), lambda i: (i, 0, 0, 0)),
                   tile(D_MODEL), tile(D_MODEL), tile(D_MODEL)] + [hbm] * ng,
        out_shape=[_big((T, D_MODEL), BF16), _big((T, CONV_CH), F32), _big((T, HGRN_W), F32),
                   _big((T // CHUNK, N_HEADS, HEAD_D, HEAD_D), F32), _big((T, D_MODEL), F32),
                   _big((T, D_MODEL), F32), _big((T, D_MODEL), BF16)] + [_big(b.shape, BF16) for b in gbufs],
        input_output_aliases={n_in + t: n_out + t for t in range(ng)},
        scratch_shapes=[pltpu.VMEM((tm + HALO + SUB, CONV_CH), F32), pltpu.VMEM((N_HEADS, HEAD_D, HEAD_D), F32),
                        pltpu.VMEM((tm, HGRN_W), BF16), pltpu.VMEM((tm, HGRN_W), BF16),
                        pltpu.VMEM((tm, HGRN_W), BF16), pltpu.VMEM((tm, HGRN_W), BF16),
                        pltpu.VMEM((tm, HGRN_W), F32), pltpu.VMEM((tm, tm), BF16), pltpu.VMEM((tm, tm), BF16),
                        pltpu.VMEM((CONV_CH, CONV_CH), BF16)] + _gather_sems(ng),
        compiler_params=_cp(("arbitrary",), 56),
    )(_hbm(p), wdw, *vecs, lbl, *_hbm(w_out, x), modr, g_post, g_ffn, *[_hbm(b) for b in gbufs])
    return outs[:n_out], outs[n_out:]


def _row_chains(rows, n=2):
    step = rows // n
    return [slice(k * step, (k + 1) * step) for k in range(n)]


def _ffn_blocks():
    return D_FF // FFN_BLOCK, (D_FF // N_CHIPS) // FFN_BLOCK


def _ffn_fwd(h2, w_up_g, w_down, x1, target, modr, g_post):
    T = h2.shape[0]
    tm = min(FFN_TILE, T)
    fb = FFN_BLOCK
    nj, per = _ffn_blocks()

    def body(h_ref, wu_ref, wd_ref, x1_ref, t_ref, mod_ref, g_ref, r_ref, dy2_ref, dx2_ref, st_ref, acc):
        i, j = pl.program_id(0), pl.program_id(1)

        @pl.when((i == 0) & (j == 0))
        def _():
            st_ref[...] = jnp.zeros(st_ref.shape, F32)

        @pl.when(j == 0)
        def _():
            acc[...] = jnp.zeros(acc.shape, F32)

        for rows in _row_chains(tm):
            ra = jnp.maximum(_dot(h_ref[rows, :], wu_ref[0]), 0.0)
            rb = (ra * ra).astype(BF16)
            r_ref[rows, :] = rb
            acc[rows, :] += _dot(rb, wd_ref[...])

        @pl.when(j == nj - 1)
        def _():
            y2 = acc[...]
            rs = lax.rsqrt(jnp.mean(y2 * y2, axis=-1, keepdims=True) + RMS_EPS)
            nh = y2 * rs
            gp = g_ref[...]
            err = x1_ref[...] + mod_ref[5:6, :] * (nh * gp) - t_ref[...]
            dx2 = err * (1.0 / D_MODEL)
            dx2_ref[...] = dx2
            st_ref[0:1, :] += _colsum(err * err)
            st_ref[1:2, :] += _colsum(dx2 * (nh * gp))
            dn = dx2 * mod_ref[5:6, :]
            st_ref[2:3, :] += _colsum(dn * nh)
            dy2_ref[...] = _rms_bwd(dn * gp, nh, rs).astype(BF16)

    tile = pl.BlockSpec((tm, D_MODEL), lambda i, j: (i, 0))
    return pl.pallas_call(
        body, name="ffn_fwd", grid=(T // tm, nj),
        in_specs=[tile, pl.BlockSpec((1, D_MODEL, fb), lambda i, j: (j // per, 0, j % per)),
                  pl.BlockSpec((fb, D_MODEL), lambda i, j: (j, 0)), tile, tile,
                  _full((6, D_MODEL)), _full((1, D_MODEL))],
        out_specs=[pl.BlockSpec((tm, fb), lambda i, j: (i, j)), tile, tile, _full((8, D_MODEL))],
        out_shape=[_big((T, D_FF), BF16), _big((T, D_MODEL), BF16), _big((T, D_MODEL), F32),
                   jax.ShapeDtypeStruct((8, D_MODEL), F32)],
        scratch_shapes=[pltpu.VMEM((tm, D_MODEL), F32)],
        compiler_params=_cp(("arbitrary", "arbitrary"), 56),
    )(*_hbm(h2, w_up_g, w_down, x1, target), modr, g_post)


def _rms_bwd(dxn, xn, rs):
    return rs * (dxn - xn * jnp.mean(dxn * xn, axis=-1, keepdims=True))


def _ffn_bwd(dy2, r, x1, dx2, w_up_g, w_down, modr, g_ffn):
    T = dx2.shape[0]
    tm = min(FFN_TILE, T)
    fb = FFN_BLOCK
    nj, per = _ffn_blocks()

    def body(dy2_ref, r_ref, x1_ref, dx2_ref, wu_ref, wd_ref, mod_ref, gf_ref, da_ref, dx1_ref, st_ref, dh_s):
        i, j = pl.program_id(0), pl.program_id(1)

        @pl.when((i == 0) & (j == 0))
        def _():
            st_ref[...] = jnp.zeros(st_ref.shape, F32)

        @pl.when(j == 0)
        def _():
            dh_s[...] = jnp.zeros(dh_s.shape, F32)

        for rows in _row_chains(tm):
            ra = jnp.sqrt(r_ref[rows, :].astype(F32))
            da = (_dot_nt(dy2_ref[rows, :], wd_ref[...]) * (2.0 * ra)).astype(BF16)
            da_ref[rows, :] = da
            dh_s[rows, :] += _dot_nt(da, wu_ref[0])

        @pl.when(j == nj - 1)
        def _():
            dh = dh_s[...]
            x1v = x1_ref[...]
            rs1 = lax.rsqrt(jnp.mean(x1v * x1v, axis=-1, keepdims=True) + RMS_EPS)
            xn = x1v * rs1
            st_ref[0:1, :] += _colsum(dh)
            st_ref[1:2, :] += _colsum(dh * (xn * gf_ref[...]))
            dsc = dh * (1.0 + mod_ref[4:5, :])
            st_ref[2:3, :] += _colsum(dsc * xn)
            dx1_ref[...] = dx2_ref[...] + _rms_bwd(dsc * gf_ref[...], xn, rs1)

    tile = pl.BlockSpec((tm, D_MODEL), lambda i, j: (i, 0))
    ftile = pl.BlockSpec((tm, fb), lambda i, j: (i, j))
    return pl.pallas_call(
        body, name="ffn_bwd", grid=(T // tm, nj),
        in_specs=[tile, ftile, tile, tile, pl.BlockSpec((1, D_MODEL, fb), lambda i, j: (j // per, 0, j % per)),
                  pl.BlockSpec((fb, D_MODEL), lambda i, j: (j, 0)), _full((6, D_MODEL)), _full((1, D_MODEL))],
        out_specs=[ftile, tile, _full((8, D_MODEL))],
        out_shape=[_big((T, D_FF), BF16), _big((T, D_MODEL), F32), jax.ShapeDtypeStruct((8, D_MODEL), F32)],
        scratch_shapes=[pltpu.VMEM((tm, D_MODEL), F32)],
        compiler_params=_cp(("arbitrary", "arbitrary"), 56),
    )(*_hbm(dy2, r, x1, dx2, w_up_g, w_down), modr, g_ffn)


def _mix_out_bwd(dx1, y, cat, w_out, modr, g_post, swap):
    T = dx1.shape[0]
    tm = _tok_tile(T)
    nt = T // tm
    ns = len(swap)

    def body(*refs):
        dx1_ref, y_ref, cat_ref, w_ref, mod_ref, gp_ref = refs[:6]
        s_ins = refs[6:6 + ns]
        dcat_ref, st_ref, gw_ref = refs[6 + ns:9 + ns]
        s_outs = refs[9 + ns:9 + 2 * ns]
        gacc, gsem, pssem, prsem = refs[9 + 2 * ns:]
        i = pl.program_id(0)

        @pl.when(i == 0)
        def _():
            for cp in _pair_copies(s_ins, s_outs, pssem, prsem):
                cp.start()
            st_ref[...] = jnp.zeros(st_ref.shape, F32)
            gacc[...] = jnp.zeros(gacc.shape, F32)

        dxv, yv = dx1_ref[...], y_ref[...]
        rs = lax.rsqrt(jnp.mean(yv * yv, axis=-1, keepdims=True) + RMS_EPS)
        nh = yv * rs
        st_ref[0:1, :] += _colsum(dxv * (nh * gp_ref[...]))
        dn = dxv * mod_ref[2:3, :]
        st_ref[1:2, :] += _colsum(dn * nh)
        dy = _rms_bwd(dn * gp_ref[...], nh, rs).astype(BF16)
        dcat_ref[...] = _dot_nt(dy, w_ref[...])
        for cols in _row_chains(D_MODEL):
            gacc[:, cols] += _dot_tn(cat_ref[...], dy[:, cols])

        @pl.when(i == nt - 1)
        def _():
            out = pltpu.make_async_copy(gacc, gw_ref, gsem)
            out.start()
            copies = _pair_copies(s_ins, s_outs, pssem, prsem)
            for cp in copies:
                cp.wait_recv()
            for cp in copies:
                cp.wait_send()
            out.wait()

    tile = pl.BlockSpec((tm, D_MODEL), lambda i: (i, 0))
    hbm = pl.BlockSpec(memory_space=pl.ANY)
    outs = pl.pallas_call(
        body, name="mix_out_bwd", grid=(nt,),
        in_specs=[tile, tile, tile, _full((D_MODEL, D_MODEL)), _full((6, D_MODEL)), _full((1, D_MODEL))]
        + [hbm] * ns,
        out_specs=[tile, _full((8, D_MODEL)), hbm] + [hbm] * ns,
        out_shape=[_big((T, D_MODEL), F32), jax.ShapeDtypeStruct((8, D_MODEL), F32), _big((D_MODEL, D_MODEL), F32)]
        + _pair_shapes(swap),
        scratch_shapes=[pltpu.VMEM((D_MODEL, D_MODEL), F32), pltpu.SemaphoreType.DMA] + _pair_sems(ns),
        compiler_params=_cp(("arbitrary",), 48),
    )(*_hbm(dx1, y, cat, w_out), modr, g_post, *[_hbm(g) for g in swap])
    return outs[:3], outs[3:]


def _mixers_bwd(p, dcat, ys, o, states, h1, wdw, vecs, lbl, pairs_b):
    T = p.shape[0]
    tm = min(MIXB_TILE, T)
    nt = T // tm
    nch = tm // CHUNK
    hpt = tm // HALO
    nx = len(pairs_b)
    nb = IN_COLS // N_CHIPS

    def body(*refs):
        (p_ref, ph_ref, dcat_ref, ys_ref, o_ref, st_ref, h1_ref, wdw_ref, bdw_ref, gain_ref, bias_ref, gout_ref,
         lbl_ref) = refs[:13]
        x_ins = refs[13:13 + nx]
        dp_ref, sb_ref, s5_ref, dw_ref = refs[13 + nx:17 + nx]
        x_outs = refs[17 + nx:17 + 2 * nx]
        gin_ref = refs[17 + 2 * nx]
        (ubuf, dybuf, carry, dstate, qt_s, kt_s, kh_s, v_s, do_s, egl_s, dqt_s, dkt_s, dkh_s, dv_s, dgl_s,
         dsh, dw8, dshift, lower_s, upper_s, same_s, gmat_s, gacc, dp_prev, h1_prev, gsem, xssem,
         xrsem) = refs[18 + 2 * nx:]
        i = pl.program_id(0)
        tile_idx = nt - 1 - i

        @pl.when(i == 0)
        def _():
            for cp in _xchg_copies(x_ins, x_outs, xssem, xrsem):
                cp.start()
            gacc[...] = jnp.zeros(gacc.shape, F32)
            dstate[...] = jnp.zeros(dstate.shape, F32)
            carry[...] = jnp.zeros(carry.shape, F32)
            sb_ref[...] = jnp.zeros(sb_ref.shape, F32)
            s5_ref[...] = jnp.zeros(s5_ref.shape, F32)
            dw_ref[...] = jnp.zeros(dw_ref.shape, F32)
            dw8[...] = jnp.zeros(dw8.shape, F32)
            lower_s[...], upper_s[...], same_s[...] = _chunk_masks(tm)
            gmat_s[...] = _gn_matrix()
            dsh[0:SUB, :] = jnp.zeros((SUB, CONV_CH), F32)
            dsh[SUB + tm:2 * SUB + tm, :] = jnp.zeros((SUB, CONV_CH), F32)
            ubuf[HALO + tm:HALO + tm + SUB, :] = jnp.zeros((SUB, CONV_CH), F32)
            dp_prev[...] = jnp.zeros(dp_prev.shape, BF16)
            h1_prev[...] = jnp.zeros(h1_prev.shape, BF16)

        n_pieces = (tm // CONV_ROWS) * (CONV_CH // LANE)
        per_block = n_pieces // N_CHIPS
        prow = D_MODEL // per_block

        def w_in_grad_piece(k):
            j, part = k // per_block, k % per_block
            rows_k = slice(part * prow, (part + 1) * prow)
            gacc[j, rows_k, :] += _dot_tn(h1_prev[:, rows_k], dp_prev[:, j * nb:(j + 1) * nb])

        uh = ph_ref[:, 0:CONV_CH] * _sig(ph_ref[:, CONV_CH:2 * CONV_CH])
        ubuf[0:HALO, :] = jnp.where(tile_idx > 0, uh, 0.0)
        ubuf[HALO:HALO + tm, :] = p_ref[:, 0:CONV_CH] * _sig(p_ref[:, CONV_CH:2 * CONV_CH])
        gmat = gmat_s[...]
        gain = gain_ref[...]
        yv = ys_ref[...]
        d = yv - _gmean(yv, gmat)
        rs = lax.rsqrt(_gmean(d * d, gmat) + GN_EPS)
        yn = d * rs
        z = yn * gain + bias_ref[...]
        sz = _sig(z)
        dz = dcat_ref[:, 0:CONV_CH] * (sz * (1.0 + z * (1.0 - sz)))
        dyn = dz * gain
        dyc = rs * (dyn - _gmean(dyn, gmat) - yn * _gmean(dyn * yn, gmat))
        s5_ref[0:1, :] += _colsum(dyc)
        s5_ref[1:2, :] += _colsum(dz * yn)
        s5_ref[2:3, :] += _colsum(dz)
        dybuf[tm:tm + HALO, :] = carry[...]
        dybuf[0:tm, :] = dyc
        dsh[SUB:SUB + tm, :] = dyc
        carry[...] = dyc[0:HALO, :]
        for b in range(SUB):
            dshift[...] = dsh[SUB - b:2 * SUB - b + tm, :]
            for j, off in CONV_FWD_TAPS:
                if off % SUB == b:
                    prod = dshift[...] * ubuf[off - b:off - b + tm + SUB, :]
                    dw8[j] += jnp.sum(prod.reshape((tm + SUB) // SUB, SUB, CONV_CH), axis=0)
        for r in range(tm // CONV_ROWS):
            rows = slice(r * CONV_ROWS, (r + 1) * CONV_ROWS)
            for lb_ in range(CONV_CH // LANE):
                lanes = slice(lb_ * LANE, (lb_ + 1) * LANE)
                glanes = slice(CONV_CH + lb_ * LANE, CONV_CH + (lb_ + 1) * LANE)
                w_in_grad_piece(r * (CONV_CH // LANE) + lb_)
                acc = _tap_conv(dybuf, wdw_ref, r * CONV_ROWS, CONV_BWD_TAPS, lanes)
                val = p_ref[rows, lanes]
                sg = _sig(p_ref[rows, glanes])
                dval = acc * sg
                dgate = acc * val * (sg * (1.0 - sg))
                dp_ref[rows, lanes] = dval.astype(BF16)
                dp_ref[rows, glanes] = dgate.astype(BF16)
                sb_ref[0:1, lanes] += _colsum(dval)
                sb_ref[0:1, glanes] += _colsum(dgate)

        o0 = 2 * CONV_CH
        for h in range(N_HEADS):
            sl = slice(h * HEAD_D, (h + 1) * HEAD_D)
            gsl = slice(o0 + 3 * HGRN_W + h * HEAD_D, o0 + 3 * HGRN_W + (h + 1) * HEAD_D)
            oh = o_ref[:, sl]
            gh = p_ref[:, gsl]
            dh = dcat_ref[:, CONV_CH + h * HEAD_D:CONV_CH + (h + 1) * HEAD_D]
            gout = gout_ref[:, sl]
            rsh = lax.rsqrt(jnp.mean(oh * oh, axis=-1, keepdims=True) + RMS_EPS)
            on = oh * rsh
            sgg = _sig(gh)
            dgh = dh * (on * gout) * (sgg * (1.0 + gh * (1.0 - sgg)))
            dm = dh * (gh * sgg)
            s5_ref[3:4, sl] += _colsum(dm * on)
            do_s[:, sl] = _rms_bwd(dm * gout, on, rsh).astype(BF16)
            dp_ref[:, gsl] = dgh.astype(BF16)
            sb_ref[2:3, CONV_CH + h * HEAD_D:CONV_CH + (h + 1) * HEAD_D] += _colsum(dgh)

        lb, _ = _lower_bound(lbl_ref)
        lower, upper, same = lower_s[...], upper_s[...], same_s[...]
        pq = p_ref[:, o0:o0 + HGRN_W]
        pr = _hgrn_prep(pq, p_ref[:, o0 + HGRN_W:o0 + 2 * HGRN_W], lb, lower, same)
        qt_s[...] = pr["qt"].astype(BF16)
        kt_s[...] = pr["kt"].astype(BF16)
        kh_s[...] = pr["kh"].astype(BF16)
        v_s[...] = p_ref[:, o0 + 2 * HGRN_W:o0 + 3 * HGRN_W].astype(BF16)
        egl_s[...] = jnp.exp(pr["Gl"])
        tri = _tri()

        def chunk(it, c_):
            ci = nch - 1 - it
            r0 = pl.multiple_of(ci * CHUNK, CHUNK)
            rows = pl.ds(r0, CHUNK)
            for h in range(N_HEADS):
                ls = pl.ds(h * HEAD_D, HEAD_D)
                qc, kc, hc, vc = qt_s[rows, ls], kt_s[rows, ls], kh_s[rows, ls], v_s[rows, ls]
                dob = do_s[rows, ls]
                s0 = st_ref[ci, h]
                s0b = s0.astype(BF16)
                ds1 = dstate[h]
                ds1b = ds1.astype(BF16)
                egl = egl_s[pl.ds(r0, 1), ls]
                att = jnp.where(tri, _dot_nt(qc, kc), 0.0).astype(BF16)
                datt = jnp.where(tri, _dot_nt(dob, vc), 0.0).astype(BF16)
                dv_s[rows, ls] = _dot_tn(att, dob) + _dot_nt(hc, ds1b)
                dqt_s[rows, ls] = _dot(datt, kc) + _dot(dob, s0b)
                dkt_s[rows, ls] = _dot_tn(datt, qc)
                dkh_s[rows, ls] = _dot(vc, ds1b)
                dgl = egl * _colsum(ds1 * s0)
                dgl_s[rows, ls] = jnp.broadcast_to(dgl, (CHUNK, HEAD_D))
                dstate[h] = ds1 * egl + _dot_tn(dob, qc)
            return c_

        lax.fori_loop(0, nch, chunk, 0)
        dqt, dkt, dkh = dqt_s[...], dkt_s[...], dkh_s[...]
        dk = dkt * pr["enG"] + dkh * pr["eGlG"]
        khk = dkh * kh_s[...].astype(F32)
        dG = dqt * qt_s[...].astype(F32) - dkt * kt_s[...].astype(F32) - khk
        dlogf = _mm3(upper, dG) + _mm3(same, khk) + dgl_s[...]
        df = dlogf / pr["f"] - dk
        sf, sq = pr["sf"], pr["sq"]
        s5_ref[4:5, :] += _colsum(df * (1.0 - sf))
        dfl = df * (1.0 - lb) * (sf * (1.0 - sf))
        dq = (dqt * pr["eG"]) * (sq * (1.0 + pq * (1.0 - sq)))
        dvv = dv_s[...]
        dp_ref[:, o0:o0 + HGRN_W] = dq.astype(BF16)
        dp_ref[:, o0 + HGRN_W:o0 + 2 * HGRN_W] = dfl.astype(BF16)
        dp_ref[:, o0 + 2 * HGRN_W:o0 + 3 * HGRN_W] = dvv.astype(BF16)
        sb_ref[1:2, 0:HGRN_W] += _colsum(dq)
        sb_ref[1:2, HGRN_W:2 * HGRN_W] += _colsum(dfl)
        sb_ref[2:3, 0:HGRN_W] += _colsum(dvv)

        dp_prev[...] = dp_ref[...]
        h1_prev[...] = h1_ref[...]

        @pl.when(i == nt - 1)
        def _():
            for k in range(n_pieces):
                w_in_grad_piece(k)
            out = pltpu.make_async_copy(gacc, gin_ref, gsem)
            out.start()
            for j in range(CONV_K):
                dw_ref[j:j + 1, :] = _colsum(dw8[j])
            copies = _xchg_copies(x_ins, x_outs, xssem, xrsem)
            for cp in copies:
                cp.wait_recv()
            for cp in copies:
                cp.wait_send()
            out.wait()

    rev = lambda cols: pl.BlockSpec((tm, cols), lambda i: (nt - 1 - i, 0))
    halo = pl.BlockSpec((HALO, 2 * CONV_CH), lambda i: (jnp.maximum((nt - 1 - i) * hpt - 1, 0), 0))
    wide = lambda n: pltpu.VMEM((tm, HGRN_W), n)
    hbm = pl.BlockSpec(memory_space=pl.ANY)
    outs = pl.pallas_call(
        body, name="mixers_bwd", grid=(nt,),
        in_specs=[rev(IN_COLS), halo, rev(D_MODEL), rev(CONV_CH), rev(HGRN_W),
                  pl.BlockSpec((nch, N_HEADS, HEAD_D, HEAD_D), lambda i: (nt - 1 - i, 0, 0, 0)), rev(D_MODEL),
                  _full((HALO, CONV_CH))] + [_full((1, CONV_CH))] * 4 + [_full((2, HGRN_W))] + [hbm] * nx,
        out_specs=[rev(IN_COLS), _full((8, D_MODEL)), _full((8, CONV_CH)), _full((HALO, CONV_CH))]
        + [hbm] * (nx + 1),
        out_shape=[_big((T, IN_COLS), BF16), jax.ShapeDtypeStruct((8, D_MODEL), F32),
                   jax.ShapeDtypeStruct((8, CONV_CH), F32), jax.ShapeDtypeStruct((HALO, CONV_CH), F32)]
        + [_big(pb.shape, BF16) for pb in pairs_b] + [_big((N_CHIPS, D_MODEL, nb), F32)],
        scratch_shapes=[pltpu.VMEM((tm + HALO + SUB, CONV_CH), F32), pltpu.VMEM((tm + HALO, CONV_CH), F32),
                        pltpu.VMEM((HALO, CONV_CH), F32), pltpu.VMEM((N_HEADS, HEAD_D, HEAD_D), F32),
                        wide(BF16), wide(BF16), wide(BF16), wide(BF16), wide(BF16),
                        wide(F32), wide(F32), wide(F32), wide(F32), wide(F32), wide(F32),
                        pltpu.VMEM((tm + 2 * SUB, CONV_CH), F32), pltpu.VMEM((HALO, SUB, CONV_CH), F32),
                        pltpu.VMEM((tm + SUB, CONV_CH), F32), pltpu.VMEM((tm, tm), BF16), pltpu.VMEM((tm, tm), BF16),
                        pltpu.VMEM((tm, tm), BF16), pltpu.VMEM((CONV_CH, CONV_CH), BF16),
                        pltpu.VMEM((N_CHIPS, D_MODEL, nb), F32), pltpu.VMEM((tm, IN_COLS), BF16),
                        pltpu.VMEM((tm, D_MODEL), BF16), pltpu.SemaphoreType.DMA]
        + _xchg_sems(nx),
        compiler_params=_cp(("arbitrary",), 56),
    )(*_hbm(p, p, dcat, ys, o, states, h1), wdw, *vecs, lbl, *[_hbm(pb) for pb in pairs_b])
    return outs[:4], outs[4:4 + nx], outs[4 + nx]


def _mix_in_bwd(dp, w_in_g, x, dx1, modr, g_pre, pairs_b):
    T = x.shape[0]
    tm = _tok_tile(T)
    nt = T // tm
    nb = IN_COLS // N_CHIPS
    nx = len(pairs_b)

    def body(*refs):
        dp_ref, w_ref, x_ref, dx1_ref, mod_ref, g_ref = refs[:6]
        x_ins = refs[6:6 + nx]
        gx_ref, st_ref = refs[6 + nx:8 + nx]
        x_outs = refs[8 + nx:8 + 2 * nx]
        xssem, xrsem = refs[8 + 2 * nx:]
        i = pl.program_id(0)

        @pl.when(i == 0)
        def _():
            for cp in _xchg_copies(x_ins, x_outs, xssem, xrsem):
                cp.start()
            st_ref[...] = jnp.zeros(st_ref.shape, F32)

        dh = None
        for j in range(N_CHIPS):
            part = _dot_nt(dp_ref[:, j * nb:(j + 1) * nb], w_ref[j])
            dh = part if dh is None else dh + part
        xv = x_ref[...]
        rs = lax.rsqrt(jnp.mean(xv * xv, axis=-1, keepdims=True) + RMS_EPS)
        xn = xv * rs
        st_ref[0:1, :] += _colsum(dh)
        st_ref[1:2, :] += _colsum(dh * (xn * g_ref[...]))
        dsc = dh * (1.0 + mod_ref[1:2, :])
        st_ref[2:3, :] += _colsum(dsc * xn)
        gx_ref[...] = dx1_ref[...] + _rms_bwd(dsc * g_ref[...], xn, rs)

        @pl.when(i == nt - 1)
        def _():
            copies = _xchg_copies(x_ins, x_outs, xssem, xrsem)
            for cp in copies:
                cp.wait_recv()
            for cp in copies:
                cp.wait_send()

    tile = pl.BlockSpec((tm, D_MODEL), lambda i: (i, 0))
    hbm = pl.BlockSpec(memory_space=pl.ANY)
    outs = pl.pallas_call(
        body, name="mix_in_bwd", grid=(nt,),
        in_specs=[pl.BlockSpec((tm, IN_COLS), lambda i: (i, 0)), _full((N_CHIPS, D_MODEL, nb)), tile, tile,
                  _full((6, D_MODEL)), _full((1, D_MODEL))] + [hbm] * nx,
        out_specs=[tile, _full((8, D_MODEL))] + [hbm] * nx,
        out_shape=[_big((T, D_MODEL), F32), jax.ShapeDtypeStruct((8, D_MODEL), F32)]
        + [_big(pb.shape, BF16) for pb in pairs_b],
        scratch_shapes=_xchg_sems(nx),
        compiler_params=_cp(("arbitrary",), 48),
    )(*_hbm(dp, w_in_g, x, dx1), modr, g_pre, *[_hbm(pb) for pb in pairs_b])
    return outs[:2], outs[2:]


def _weight_grad(a, b, a_blocked, b_blocked, name):
    T = a.shape[0]
    tt = min(GRAD_TILE, T)
    nt = T // tt
    ka = a.shape[1] // N_CHIPS if a_blocked else a.shape[1]
    nb = b.shape[1] // N_CHIPS if b_blocked else b.shape[1]

    def body(a_ref, b_ref, o_ref, ob_ref):
        t = pl.program_id(1)

        @pl.when(t == 0)
        def _():
            o_ref[...] = jnp.zeros(o_ref.shape, F32)

        for cols in _row_chains(nb):
            o_ref[0, :, cols] += _dot_tn(a_ref[...], b_ref[:, cols])

        @pl.when(t == nt - 1)
        def _():
            ob_ref[0] = o_ref[0].astype(BF16)

    blk = pl.BlockSpec((1, ka, nb), lambda j, t: (j, 0, 0))
    return pl.pallas_call(
        body, name=name, grid=(N_CHIPS, nt),
        in_specs=[pl.BlockSpec((tt, ka), (lambda j, t: (t, j)) if a_blocked else (lambda j, t: (t, 0))),
                  pl.BlockSpec((tt, nb), (lambda j, t: (t, j)) if b_blocked else (lambda j, t: (t, 0)))],
        out_specs=[blk, blk],
        out_shape=[_big((N_CHIPS, ka, nb), F32), _big((N_CHIPS, ka, nb), BF16)],
        compiler_params=_cp(("arbitrary", "arbitrary"), 48),
    )(*_hbm(a, b))


R_LOSS = 0
R_FFN = 8
R_OUT = 16
R_IN = 24
R_BIN = 32
R_512 = 40
R_DW = 48
N_STAT_ROWS = 80
MOD_ROWS = (R_IN + 0, R_IN + 1, R_OUT + 0, R_FFN + 0, R_FFN + 1, R_LOSS + 1)


def _small_update(gath, params):
    names = ["b_ada", "lb_logits", "g_pre_mix", "b_in", "b_dw", "gn_gain", "gn_bias", "g_hgrn_out", "g_post_mix",
             "g_pre_ffn", "g_post_ffn"]
    flat = []
    for n in names:
        flat += list(params[n])
    n_in = 1 + len(flat)

    def body(*refs):
        g_ref = refs[0]
        prm = {n: refs[1 + 3 * k:4 + 3 * k] for k, n in enumerate(names)}
        outs = refs[n_in:]
        loss_ref, dmod_ref, dwdw_ref = outs[0], outs[1], outs[2]
        res = {n: outs[3 + 4 * k:7 + 4 * k] for k, n in enumerate(names)}
        red = g_ref[0]
        for dev in range(1, N_DEV):
            red = red + g_ref[dev]
        loss_ref[...] = jnp.broadcast_to(
            (0.5 / D_MODEL) * jnp.sum(red[R_LOSS:R_LOSS + 1, :], axis=-1, keepdims=True), loss_ref.shape)
        for dev in range(N_DEV):
            for k, r in enumerate(MOD_ROWS):
                dmod_ref[dev:dev + 1, k * D_MODEL:(k + 1) * D_MODEL] = g_ref[dev, r:r + 1, :]
        dwdw_ref[...] = red[R_DW:R_DW + HALO, 0:CONV_CH]

        def finish(name, pieces):
            w_ref, m_ref, v_ref = prm[name]
            g_out, d_out, m_out, v_out = res[name]
            for rsl, lsl, g in pieces:
                d, m2, v2 = _adam_math(w_ref[rsl, lsl], g, m_ref[rsl, lsl], v_ref[rsl, lsl])
                g_out[rsl, lsl] = g
                d_out[rsl, lsl] = d
                m_out[rsl, lsl] = m2
                v_out[rsl, lsl] = v2

        one = slice(0, 1)
        row = lambda r: red[r:r + 1, :]
        half = lambda r: red[r:r + 1, 0:CONV_CH]
        finish("b_ada", [(one, slice(k * D_MODEL, (k + 1) * D_MODEL), row(r)) for k, r in enumerate(MOD_ROWS)])
        finish("b_in", [(one, slice(k * D_MODEL, (k + 1) * D_MODEL), row(R_BIN + k)) for k in range(3)])
        finish("g_pre_mix", [(one, slice(None), row(R_IN + 2))])
        finish("g_post_mix", [(one, slice(None), row(R_OUT + 1))])
        finish("g_pre_ffn", [(one, slice(None), row(R_FFN + 2))])
        finish("g_post_ffn", [(one, slice(None), row(R_LOSS + 2))])
        finish("b_dw", [(one, slice(None), half(R_512 + 0))])
        finish("gn_gain", [(one, slice(None), half(R_512 + 1))])
        finish("gn_bias", [(one, slice(None), half(R_512 + 2))])
        finish("g_hgrn_out", [(one, slice(None), half(R_512 + 3))])
        s0, s1 = _lower_bound(prm["lb_logits"][0])
        dlb = half(R_512 + 4)
        finish("lb_logits", [(slice(0, 1), slice(None), dlb * s0 * (1.0 - s0)),
                             (slice(1, 2), slice(None), -dlb * s0 * s1)])

    vm = pl.BlockSpec(memory_space=pltpu.VMEM)
    out_shape = [jax.ShapeDtypeStruct((8, 128), F32), jax.ShapeDtypeStruct((N_DEV, 6 * D_MODEL), F32),
                 jax.ShapeDtypeStruct((HALO, CONV_CH), F32)]
    for n in names:
        out_shape += [jax.ShapeDtypeStruct(params[n][0].shape, F32)] * 4
    outs = pl.pallas_call(
        body, name="small_update", out_shape=out_shape,
        in_specs=[vm] * n_in, out_specs=[vm] * len(out_shape),
        compiler_params=_cp(None, 32),
    )(gath, *flat)
    return outs[0], outs[1], outs[2], {n: outs[3 + 4 * k:7 + 4 * k] for k, n in enumerate(names)}


def _wdw_adam(w, g, m, v):
    def body(w_ref, g_ref, m_ref, v_ref, d_out, m_out, v_out):
        d, m2, v2 = _adam_math(w_ref[...], g_ref[...], m_ref[...], v_ref[...])
        d_out[...] = d
        m_out[...] = m2
        v_out[...] = v2

    vm = pl.BlockSpec(memory_space=pltpu.VMEM)
    return pl.pallas_call(
        body, name="wdw_adam", out_shape=[jax.ShapeDtypeStruct(w.shape, F32)] * 3,
        in_specs=[vm] * 4, out_specs=[vm] * 3, compiler_params=_cp(None, 16),
    )(w, g, m, v)


def kernel(x, c, w_ada, b_ada, lb_logits, g_pre_mix, w_in, b_in, w_dw, b_dw, gn_gain, gn_bias, g_hgrn_out, w_out, g_post_mix, g_pre_ffn, w_up, w_down, g_post_ffn, loss_target, m_w_ada, m_b_ada, m_lb_logits, m_g_pre_mix, m_w_in, m_b_in, m_w_dw, m_b_dw, m_gn_gain, m_gn_bias, m_g_hgrn_out, m_w_out, m_g_post_mix, m_g_pre_ffn, m_w_up, m_w_down, m_g_post_ffn, v_w_ada, v_b_ada, v_lb_logits, v_g_pre_mix, v_w_in, v_b_in, v_w_dw, v_b_dw, v_gn_gain, v_gn_bias, v_g_hgrn_out, v_w_out, v_g_post_mix, v_g_pre_ffn, v_w_up, v_w_down, v_g_post_ffn):
    ax, ay, ac = lax.axis_index("x"), lax.axis_index("y"), lax.axis_index("c")
    chip = 2 * ax + ay
    T = x.shape[1]
    xs, tgt = x[0], loss_target[0]
    ada_cols = w_ada.shape[2]

    b_sh = lax.dynamic_slice_in_dim(b_ada, chip * ada_cols, ada_cols, axis=1)
    wdw_pad = jnp.pad(w_dw[0], ((0, HALO - CONV_K), (0, 0)))
    chip1 = jnp.reshape(chip, (1,)).astype(jnp.int32)
    place = jnp.stack([ac, chip]).astype(jnp.int32)
    _, c8, modg, wdwg = _ada_exchange(c, w_ada[0], b_sh, wdw_pad)
    modr = modg.reshape(6, D_MODEL)
    wdw_all = jnp.transpose(wdwg, (1, 0, 2)).reshape(HALO, CONV_CH)
    bufs = {t: _cast_own(chip1, w[0], "cast_" + t)
            for w, t in ((w_in, "w_in"), (w_out, "w_out"), (w_up, "w_up"), (w_down, "w_down"))}
    vec = (b_dw, gn_gain, gn_bias, g_hgrn_out)

    p, h1, w_in_g, w_out_g = _mix_in_fwd(chip1, xs, modr, g_pre_mix, b_in.reshape(N_CHIPS, 1, IN_COLS // N_CHIPS),
                                          bufs["w_in"], bufs["w_out"])
    w_out_f = w_out_g.reshape(D_MODEL, D_MODEL)
    (cat, ys, o, states, y, x1, h2), (w_up_g, w_down_g) = _mixers_fwd(
        p, wdw_all, vec, lb_logits, w_out_f, xs, modr, g_post_mix, g_pre_ffn, [bufs["w_up"], bufs["w_down"]])
    w_down_f = w_down_g.reshape(D_FF, D_MODEL)
    r, dy2, dx2, st_loss = _ffn_fwd(h2, w_up_g, w_down_f, x1, tgt, modr, g_post_ffn)

    def pair_sums(grads, got, tags):
        return [_pair_sum(place, g, o_, "pair_sum_" + t) for (g, _), o_, t in zip(grads, got, tags)]

    da, dx1, st_ffn = _ffn_bwd(dy2, r, x1, dx2, w_up_g, w_down_f, modr, g_pre_ffn)
    g_up = _weight_grad(h2, da, False, True, "grad_w_up")
    g_down = _weight_grad(r, dy2, True, False, "grad_w_down")
    (dcat, st_out, g_out), got_ud = _mix_out_bwd(dx1, y, cat, w_out_f, modr, g_post_mix, [g_up[1], g_down[1]])
    g_out = g_out.reshape(N_CHIPS, D_MODEL // N_CHIPS, D_MODEL)
    got_o = _pair_swap([g_out], "pair_swap_w_out")
    early = pair_sums([(g_out, None), g_up, g_down], list(got_o) + list(got_ud), ["w_out", "w_up", "w_down"])
    (dp, st_bin, st_512, dwdw), got_early, g_in = _mixers_bwd(p, dcat, ys, o, states, h1, wdw_all, vec, lb_logits,
                                                              [pb for _, pb in early])
    late = pair_sums([(g_in, None)], _pair_swap([g_in], "pair_swap_w_in"), ["w_in"])
    (grad_x, st_in), got_late = _mix_in_bwd(dp, w_in_g, xs, dx1, modr, g_pre_mix, [late[0][1]])
    fulls = [_chip_sum(place, pf, gb, "chip_sum_" + t)
             for (pf, _), gb, t in zip(late + early, list(got_late) + list(got_early), ["w_in", "w_out", "w_up", "w_down"])]

    pad_lanes = lambda s: jnp.pad(s, ((0, 0), (0, D_MODEL - s.shape[1])))
    stats = jnp.concatenate([st_loss, st_ffn, st_out, st_in, st_bin, pad_lanes(st_512), pad_lanes(dwdw)], axis=0)
    (g_w_in, g_w_out, g_w_up, g_w_down), gath = _final_exchange(fulls, stats)
    small = {"b_ada": (b_ada, m_b_ada, v_b_ada), "lb_logits": (lb_logits, m_lb_logits, v_lb_logits),
             "g_pre_mix": (g_pre_mix, m_g_pre_mix, v_g_pre_mix), "b_in": (b_in, m_b_in, v_b_in),
             "b_dw": (b_dw, m_b_dw, v_b_dw), "gn_gain": (gn_gain, m_gn_gain, v_gn_gain),
             "gn_bias": (gn_bias, m_gn_bias, v_gn_bias), "g_hgrn_out": (g_hgrn_out, m_g_hgrn_out, v_g_hgrn_out),
             "g_post_mix": (g_post_mix, m_g_post_mix, v_g_post_mix), "g_pre_ffn": (g_pre_ffn, m_g_pre_ffn, v_g_pre_ffn),
             "g_post_ffn": (g_post_ffn, m_g_post_ffn, v_g_post_ffn)}
    loss_t, dmod_all, dwdw_sum, sres = _small_update(gath, small)
    loss = loss_t[0, 0]

    res = dict(sres)
    dmod_sh = lax.dynamic_slice_in_dim(dmod_all, chip * ada_cols, ada_cols, axis=1)
    res["w_ada"] = [t[None] for t in _ada_grad_adam(jnp.transpose(c8), dmod_sh, w_ada[0], m_w_ada[0], v_w_ada[0])]
    g_wdw = lax.dynamic_slice_in_dim(dwdw_sum, chip * HEAD_D, HEAD_D, axis=1)[:CONV_K][None]
    res["w_dw"] = [g_wdw] + list(_wdw_adam(w_dw, g_wdw, m_w_dw, v_w_dw))
    for name, g, w, m, v in (("w_in", g_w_in, w_in, m_w_in, v_w_in), ("w_out", g_w_out, w_out, m_w_out, v_w_out),
                             ("w_up", g_w_up, w_up, m_w_up, v_w_up), ("w_down", g_w_down, w_down, m_w_down, v_w_down)):
        d, m2, v2 = _adam_big(w[0], g, m[0], v[0], "adam_" + name)
        res[name] = [g[None], d[None], m2[None], v2[None]]

    order = ["w_ada", "b_ada", "lb_logits", "g_pre_mix", "w_in", "b_in", "w_dw", "b_dw", "gn_gain", "gn_bias",
             "g_hgrn_out", "w_out", "g_post_mix", "g_pre_ffn", "w_up", "w_down", "g_post_ffn"]
    out = [loss, grad_x[None]]
    for k in range(4):
        out += [res[n][k] for n in order]
    return tuple(out)
```

```python
import jax
import jax.numpy as jnp
from jax import lax
from jax.experimental import pallas as pl
from jax.experimental.pallas import tpu as pltpu

F32, BF16 = jnp.float32, jnp.bfloat16
D_MODEL = 1024
CONV_CH = 512
HGRN_W = 512
N_HEADS = 4
HEAD_D = 128
CONV_K = 31
GN_GROUP = 64
GN_SHIFT = 6
IN_COLS = 3072
D_FF = 4096
CHUNK = 64
CHUNK_SHIFT = 6
N_CHIPS = 4
N_DEV = 8
RMS_EPS = 1e-6
GN_EPS = 1e-5
ADAM_LR, ADAM_B1, ADAM_B2, ADAM_EPS, ADAM_WD, ADAM_STEP = 0.001, 0.9, 0.999, 1e-08, 0.01, 10
TOK_TILE = 512
MIXIN_TILE = 1024
MIXB_TILE = 256
FFN_TILE = 1024
FFN_BLOCK = 512
GRAD_TILE = 2048
HALO = 32
SUB = 8
LANE = 128
CONV_ROWS = 128
CHUNK_UNROLL = 8
MIB = 1 << 20
MESH = pl.DeviceIdType.MESH
OTHER_CHIPS = ((0, 1), (1, 0), (1, 1))


def _cp(sem=None, vmem_mib=48):
    return pltpu.CompilerParams(dimension_semantics=sem, vmem_limit_bytes=vmem_mib * MIB)


def _dot(a, b):
    return jnp.dot(a, b, preferred_element_type=F32)


def _dot_nt(a, b):
    return lax.dot_general(a, b, (((1,), (1,)), ((), ())), preferred_element_type=F32)


def _dot_tn(a, b):
    return lax.dot_general(a, b, (((0,), (0,)), ((), ())), preferred_element_type=F32)


def _sig(v):
    return 0.5 * jnp.tanh(0.5 * v) + 0.5


def _colsum(v):
    return jnp.sum(v, axis=0, keepdims=True)


def _flip(v, b):
    return 1 - v if b else v


def _rcopy(src, dst, ssem, rsem, dev):
    return pltpu.make_async_remote_copy(src_ref=src, dst_ref=dst, send_sem=ssem, recv_sem=rsem,
                                        device_id=dev, device_id_type=MESH)


def _place():
    return lax.axis_index("x"), lax.axis_index("y"), lax.axis_index("c")


def _full(shape):
    return pl.BlockSpec(shape, lambda *_: (0,) * len(shape))


def _big(shape, dtype):
    return pltpu.HBM(shape, dtype)


def _hbm(*arrays):
    out = [pltpu.with_memory_space_constraint(a, pltpu.HBM) for a in arrays]
    return out[0] if len(out) == 1 else out


def _split2(v):
    hi = v.astype(BF16)
    lo = (v - hi.astype(F32)).astype(BF16)
    return hi, lo


def _split3(v):
    h1 = v.astype(BF16)
    r1 = v - h1.astype(F32)
    h2 = r1.astype(BF16)
    h3 = (r1 - h2.astype(F32)).astype(BF16)
    return h1, h2, h3


def _mm3s(mat, parts):
    h1, h2, h3 = parts
    return _dot(mat, h1) + _dot(mat, h2) + _dot(mat, h3)


def _mm3(mat, v):
    return _mm3s(mat, _split3(v))


def _gn_matrix():
    r = lax.broadcasted_iota(jnp.int32, (CONV_CH, CONV_CH), 0) >> GN_SHIFT
    c = lax.broadcasted_iota(jnp.int32, (CONV_CH, CONV_CH), 1) >> GN_SHIFT
    return jnp.where(r == c, 1.0 / GN_GROUP, 0.0).astype(BF16)


def _gmean(v, gmat):
    hi, lo = _split2(v)
    return _dot(hi, gmat) + _dot(lo, gmat)


def _chunk_masks(tm):
    r = lax.broadcasted_iota(jnp.int32, (tm, tm), 0)
    c = lax.broadcasted_iota(jnp.int32, (tm, tm), 1)
    same = (r >> CHUNK_SHIFT) == (c >> CHUNK_SHIFT)
    one = lambda m: jnp.where(m, 1.0, 0.0).astype(BF16)
    return one(same & (c <= r)), one(same & (c >= r)), one(same)


def _tri():
    return lax.broadcasted_iota(jnp.int32, (CHUNK, CHUNK), 0) >= lax.broadcasted_iota(jnp.int32, (CHUNK, CHUNK), 1)


def _lower_bound(lbl_ref):
    l0, l1 = lbl_ref[0:1, :], lbl_ref[1:2, :]
    mx = jnp.maximum(l0, l1)
    e0, e1 = jnp.exp(l0 - mx), jnp.exp(l1 - mx)
    return e0 / (e0 + e1), e1 / (e0 + e1)


CONV_FWD_TAPS = tuple((j, HALO - (CONV_K - 1) + j) for j in range(CONV_K))
CONV_BWD_TAPS = tuple((j, (CONV_K - 1) - j) for j in range(CONV_K))


def _tap_conv(src_ref, w_ref, row0, taps, lanes):
    acc = None
    for b in range(SUB):
        pb = None
        for j, off in taps:
            if off % SUB == b:
                lo = row0 + off - b
                term = w_ref[j:j + 1, lanes] * src_ref[lo:lo + CONV_ROWS + SUB, lanes]
                pb = term if pb is None else pb + term
        if pb is not None:
            sh = pb[b:b + CONV_ROWS, :]
            acc = sh if acc is None else acc + sh
    return acc


def _hgrn_prep(pq, pf, lb, lower, same):
    sq = _sig(pq)
    qf = pq * sq
    sf = _sig(pf)
    f = lb + (1.0 - lb) * sf
    logf = jnp.log(f)
    k = 1.0 - f
    parts = _split3(logf)
    G = _mm3s(lower, parts)
    Gl = _mm3s(same, parts)
    eG, enG, eGlG = jnp.exp(G), jnp.exp(-G), jnp.exp(Gl - G)
    return dict(sq=sq, sf=sf, f=f, Gl=Gl, eG=eG, enG=enG, eGlG=eGlG, qt=qf * eG, kt=k * enG, kh=k * eGlG)


def _ada_exchange(c_row, w_ada, b_sh, wdw_pad):
    ncol = w_ada.shape[1]

    def body(c_ref, w_ref, b_ref, wdw_ref, call_ref, c8_ref, modg_ref, wdwg_ref, rows_s, sa, ra, sw, rw, sm, rm):
        x, y, c = _place()
        me = 4 * x + 2 * y + c
        chip = 2 * x + y
        cv = c_ref[...]
        call_ref[me] = cv * _sig(cv)
        wdwg_ref[chip] = wdw_ref[...]
        sends = []
        for m in range(1, N_DEV):
            peer = (_flip(x, m >> 2), _flip(y, (m >> 1) & 1), _flip(c, m & 1))
            cp = _rcopy(call_ref.at[me], call_ref.at[me], sa.at[m - 1], ra.at[m - 1], peer)
            cp.start()
            sends.append(cp)
        for k, (fx, fy) in enumerate(OTHER_CHIPS):
            peer = (_flip(x, fx), _flip(y, fy), c)
            cp = _rcopy(wdwg_ref.at[chip], wdwg_ref.at[chip], sw.at[k], rw.at[k], peer)
            cp.start()
            sends.append(cp)
        for m in range(1, N_DEV):
            peer = (_flip(x, m >> 2), _flip(y, (m >> 1) & 1), _flip(c, m & 1))
            pid = 4 * peer[0] + 2 * peer[1] + peer[2]
            _rcopy(call_ref.at[pid], call_ref.at[pid], sa.at[m - 1], ra.at[m - 1], peer).wait_recv()
        for b in range(N_DEV):
            c8_ref[b:b + 1, :] = call_ref[b]
        mod_all = _dot(c8_ref[...].astype(BF16), w_ref[...].astype(BF16)) + b_ref[...]
        for b in range(N_DEV):
            rows_s[b] = mod_all[b:b + 1, :]
        modg_ref[chip] = rows_s[me]
        for k, (fx, fy) in enumerate(OTHER_CHIPS):
            peer = (_flip(x, fx), _flip(y, fy), c)
            pid = 4 * peer[0] + 2 * peer[1] + peer[2]
            cp = _rcopy(rows_s.at[pid], modg_ref.at[chip], sm.at[k], rm.at[k], peer)
            cp.start()
            sends.append(cp)
        for k, (fx, fy) in enumerate(OTHER_CHIPS):
            peer = (_flip(x, fx), _flip(y, fy), c)
            pchip = 2 * peer[0] + peer[1]
            _rcopy(rows_s.at[0], modg_ref.at[pchip], sm.at[k], rm.at[k], peer).wait_recv()
            _rcopy(wdwg_ref.at[pchip], wdwg_ref.at[pchip], sw.at[k], rw.at[k], peer).wait_recv()
        for cp in sends:
            cp.wait_send()

    vm = pl.BlockSpec(memory_space=pltpu.VMEM)
    return pl.pallas_call(
        body, name="ada_exchange",
        out_shape=[jax.ShapeDtypeStruct((N_DEV, 1, D_MODEL), F32), jax.ShapeDtypeStruct((N_DEV, D_MODEL), F32),
                   jax.ShapeDtypeStruct((N_CHIPS, 1, ncol), F32), jax.ShapeDtypeStruct((N_CHIPS, HALO, HEAD_D), F32)],
        in_specs=[vm] * 4, out_specs=[vm] * 4,
        scratch_shapes=[pltpu.VMEM((N_DEV, 1, ncol), F32),
                        pltpu.SemaphoreType.DMA((N_DEV - 1,)), pltpu.SemaphoreType.DMA((N_DEV - 1,)),
                        pltpu.SemaphoreType.DMA((3,)), pltpu.SemaphoreType.DMA((3,)),
                        pltpu.SemaphoreType.DMA((3,)), pltpu.SemaphoreType.DMA((3,))],
        compiler_params=_cp(None, 32),
    )(c_row, w_ada, b_sh, wdw_pad)


def _cast_own(chip1, shard, name):
    rows, cols = shard.shape
    tr = _row_tile(rows)

    def body(ch_ref, s_ref, o_ref):
        o_ref[0] = s_ref[...].astype(BF16)

    gs = pltpu.PrefetchScalarGridSpec(
        num_scalar_prefetch=1, grid=(rows // tr,),
        in_specs=[pl.BlockSpec((tr, cols), lambda i, ch: (i, 0))],
        out_specs=pl.BlockSpec((1, tr, cols), lambda i, ch: (ch[0], i, 0)))
    return pl.pallas_call(
        body, name=name, grid_spec=gs, out_shape=_big((N_CHIPS, rows, cols), BF16),
        compiler_params=_cp(("arbitrary",), 32),
    )(chip1, _hbm(shard))


def _slab(buf, ch, core):
    hs = buf.shape[1] // 2
    return buf.at[ch, pl.ds(core * hs, hs), :]


def _gather_start(bufs, ssem, rsem, relations=(0, 1, 2)):
    x, y, c = _place()
    chip = 2 * x + y
    for k in relations:
        fx, fy = OTHER_CHIPS[k]
        peer = (_flip(x, fx), _flip(y, fy), c)
        for t, buf in enumerate(bufs):
            _rcopy(_slab(buf, chip, c), _slab(buf, chip, c), ssem.at[t * 3 + k], rsem.at[t * 3 + k], peer).start()


def _gather_pass_on(bufs, ssem, rsem):
    nt = len(bufs)
    x, y, c = _place()
    sibling = (x, y, 1 - c)
    for k, (fx, fy) in enumerate(OTHER_CHIPS):
        peer = (_flip(x, fx), _flip(y, fy), c)
        pchip = 2 * peer[0] + peer[1]
        for t, buf in enumerate(bufs):
            _rcopy(_slab(buf, pchip, c), _slab(buf, pchip, c), ssem.at[t * 3 + k], rsem.at[t * 3 + k], peer).wait_recv()
            _rcopy(_slab(buf, pchip, c), _slab(buf, pchip, c), ssem.at[3 * nt + t * 3 + k],
                   rsem.at[3 * nt + t * 3 + k], sibling).start()


def _gather_drain(bufs, ssem, rsem):
    nt = len(bufs)
    x, y, c = _place()
    chip = 2 * x + y
    sibling = (x, y, 1 - c)
    for k, (fx, fy) in enumerate(OTHER_CHIPS):
        peer = (_flip(x, fx), _flip(y, fy), c)
        pchip = 2 * peer[0] + peer[1]
        for t, buf in enumerate(bufs):
            _rcopy(_slab(buf, pchip, 1 - c), _slab(buf, pchip, 1 - c), ssem.at[3 * nt + t * 3 + k],
                   rsem.at[3 * nt + t * 3 + k], sibling).wait_recv()
            _rcopy(_slab(buf, chip, c), _slab(buf, chip, c), ssem.at[t * 3 + k], rsem.at[t * 3 + k], peer).wait_send()
            _rcopy(_slab(buf, pchip, c), _slab(buf, pchip, c), ssem.at[3 * nt + t * 3 + k],
                   rsem.at[3 * nt + t * 3 + k], sibling).wait_send()


def _gather_finish(bufs, ssem, rsem):
    _gather_pass_on(bufs, ssem, rsem)
    _gather_drain(bufs, ssem, rsem)


def _gather_arrive(bufs, k, ssem, rsem):
    nt = len(bufs)
    x, y, c = _place()
    fx, fy = OTHER_CHIPS[k]
    peer = (_flip(x, fx), _flip(y, fy), c)
    pchip = 2 * peer[0] + peer[1]
    for t, buf in enumerate(bufs):
        _rcopy(_slab(buf, pchip, c), _slab(buf, pchip, c), ssem.at[t * 3 + k], rsem.at[t * 3 + k], peer).wait_recv()
        _rcopy(_slab(buf, pchip, c), _slab(buf, pchip, c), ssem.at[3 * nt + t * 3 + k],
               rsem.at[3 * nt + t * 3 + k], (x, y, 1 - c)).start()
    for t, buf in enumerate(bufs):
        _rcopy(_slab(buf, pchip, 1 - c), _slab(buf, pchip, 1 - c), ssem.at[3 * nt + t * 3 + k],
               rsem.at[3 * nt + t * 3 + k], (x, y, 1 - c)).wait_recv()


def _gather_sends_done(bufs, ssem, rsem):
    nt = len(bufs)
    x, y, c = _place()
    chip = 2 * x + y
    for k, (fx, fy) in enumerate(OTHER_CHIPS):
        peer = (_flip(x, fx), _flip(y, fy), c)
        pchip = 2 * peer[0] + peer[1]
        for t, buf in enumerate(bufs):
            _rcopy(_slab(buf, chip, c), _slab(buf, chip, c), ssem.at[t * 3 + k], rsem.at[t * 3 + k], peer).wait_send()
            _rcopy(_slab(buf, pchip, c), _slab(buf, pchip, c), ssem.at[3 * nt + t * 3 + k],
                   rsem.at[3 * nt + t * 3 + k], (x, y, 1 - c)).wait_send()


def _gather_sems(nt):
    return [pltpu.SemaphoreType.DMA((6 * nt,)), pltpu.SemaphoreType.DMA((6 * nt,))]


def _pair_copies(ins, outs, ssem, rsem):
    x, y, c = _place()
    copies = []
    for t in range(len(ins)):
        hs = ins[t].shape[1] // 2
        copies.append(_rcopy(ins[t].at[:, pl.ds((1 - c) * hs, hs), :], outs[t], ssem.at[t], rsem.at[t], (x, y, 1 - c)))
    return copies


def _pair_shapes(grads):
    return [_big((g.shape[0], g.shape[1] // 2, g.shape[2]), g.dtype) for g in grads]


def _pair_sems(nt):
    return [pltpu.SemaphoreType.DMA((nt,)), pltpu.SemaphoreType.DMA((nt,))]


def _pair_swap(grads, name):
    nt = len(grads)
    hbm = pl.BlockSpec(memory_space=pl.ANY)

    def body(*refs):
        copies = _pair_copies(refs[:nt], refs[nt:2 * nt], refs[2 * nt], refs[2 * nt + 1])
        for cp in copies:
            cp.start()
        for cp in copies:
            cp.wait_recv()
        for cp in copies:
            cp.wait_send()

    return pl.pallas_call(
        body, name=name, out_shape=_pair_shapes(grads), in_specs=[hbm] * nt, out_specs=[hbm] * nt,
        scratch_shapes=_pair_sems(nt),
    )(*[_hbm(g) for g in grads])


def _xchg_copies(ins, outs, ssem, rsem):
    x, y, c = _place()
    copies = []
    for k, (fx, fy) in enumerate(OTHER_CHIPS):
        peer = (_flip(x, fx), _flip(y, fy), c)
        for t in range(len(ins)):
            copies.append(_rcopy(ins[t].at[k], outs[t].at[k], ssem.at[t * 3 + k], rsem.at[t * 3 + k], peer))
    return copies


def _xchg_sems(nt):
    return [pltpu.SemaphoreType.DMA((3 * nt,)), pltpu.SemaphoreType.DMA((3 * nt,))]


def _final_exchange(fulls, stats):
    nt = len(fulls)
    rows, cols = stats.shape
    hbm = pl.BlockSpec(memory_space=pl.ANY)
    vm = pl.BlockSpec(memory_space=pltpu.VMEM)

    def body(*refs):
        ins, s_ref = refs[:nt], refs[nt]
        outs, g_ref = refs[nt + 1:2 * nt + 1], refs[2 * nt + 1]
        hssem, hrsem, ssem, rsem = refs[2 * nt + 2:]
        x, y, c = _place()
        me, sibling = (x, y, c), (x, y, 1 - c)
        halves = []
        for t in range(nt):
            hs = ins[t].shape[0] // 2
            mine = pl.ds(c * hs, hs)
            cp = _rcopy(ins[t].at[mine, :], outs[t].at[mine, :], hssem.at[t], hrsem.at[t], sibling)
            cp.start()
            halves.append(cp)

        chips = [(_flip(x, fx), _flip(y, fy)) for fx, fy in OTHER_CHIPS]

        def blk(px, py, pc):
            return g_ref.at[4 * px + 2 * py + pc]

        def copy(k, block, to, src=None):
            return _rcopy(blk(*block) if src is None else src, blk(*block), ssem.at[k], rsem.at[k], to)

        g_ref[4 * x + 2 * y + c] = s_ref[...]
        first = [copy(0, me, sibling, src=s_ref)]
        first += [copy(1 + j, me, (*chip, c), src=s_ref) for j, chip in enumerate(chips)]
        for cp in first:
            cp.start()
        passed = [copy(4 + j, (*chip, c), sibling) for j, chip in enumerate(chips)]
        for j, chip in enumerate(chips):
            copy(1 + j, (*chip, c), me).wait_recv()
            passed[j].start()
        copy(0, sibling, me).wait_recv()
        for j, chip in enumerate(chips):
            copy(4 + j, (*chip, 1 - c), me).wait_recv()
        for t in range(nt):
            hs = ins[t].shape[0] // 2
            other = pl.ds((1 - c) * hs, hs)
            _rcopy(ins[t].at[other, :], outs[t].at[other, :], hssem.at[t], hrsem.at[t], sibling).wait_recv()
        for cp in first + passed + halves:
            cp.wait_send()

    outs = pl.pallas_call(
        body, name="final_exchange",
        out_shape=[_big(f.shape, F32) for f in fulls] + [jax.ShapeDtypeStruct((N_DEV, rows, cols), F32)],
        in_specs=[hbm] * nt + [vm], out_specs=[hbm] * nt + [vm],
        input_output_aliases={t: t for t in range(nt)},
        scratch_shapes=[pltpu.SemaphoreType.DMA((nt,)), pltpu.SemaphoreType.DMA((nt,)),
                        pltpu.SemaphoreType.DMA((7,)), pltpu.SemaphoreType.DMA((7,))],
        compiler_params=_cp(None, 32),
    )(*[_hbm(f) for f in fulls], stats)
    return outs[:nt], outs[nt]


def _row_tile(rows):
    return min(rows, 512)


def _pair_sum(place, grad, got, name):
    nb, hs, cols = got.shape
    tr = _row_tile(hs)
    nr = hs // tr

    def body(pl_ref, g_ref, o_ref, pf_ref, pb_ref):
        j = pl.program_id(1)
        s = g_ref[0] + o_ref[0].astype(F32)

        @pl.when(j == 0)
        def _():
            pf_ref[...] = s

        @pl.when(j > 0)
        def _():
            pb_ref[0] = s.astype(BF16)

    gs = pltpu.PrefetchScalarGridSpec(
        num_scalar_prefetch=1, grid=(nr, nb),
        in_specs=[pl.BlockSpec((1, tr, cols), lambda i, j, p: (p[1] ^ j, p[0] * nr + i, 0)),
                  pl.BlockSpec((1, tr, cols), lambda i, j, p: (p[1] ^ j, i, 0))],
        out_specs=[pl.BlockSpec((tr, cols), lambda i, j, p: (i, 0)),
                   pl.BlockSpec((1, tr, cols), lambda i, j, p: (jnp.maximum(j - 1, 0), i, 0))])
    return pl.pallas_call(
        body, name=name, grid_spec=gs,
        out_shape=[_big((hs, cols), F32), _big((nb - 1, hs, cols), BF16)],
        compiler_params=_cp(("arbitrary", "arbitrary"), 32),
    )(place, *_hbm(grad, got))


def _chip_sum(place, pair_f, got_b, name):
    nb, hs, cols = got_b.shape
    tr = _row_tile(hs)
    nr = hs // tr

    def body(pl_ref, pf_ref, gb_ref, o_ref):
        acc = pf_ref[...]
        for k in range(nb):
            acc = acc + gb_ref[k].astype(F32)
        o_ref[...] = acc

    gs = pltpu.PrefetchScalarGridSpec(
        num_scalar_prefetch=1, grid=(nr,),
        in_specs=[pl.BlockSpec((tr, cols), lambda i, p: (i, 0)),
                  pl.BlockSpec((nb, tr, cols), lambda i, p: (0, i, 0))],
        out_specs=pl.BlockSpec((tr, cols), lambda i, p: (p[0] * nr + i, 0)))
    return pl.pallas_call(
        body, name=name, grid_spec=gs,
        out_shape=_big((2 * hs, cols), F32),
        compiler_params=_cp(("arbitrary",), 32),
    )(place, *_hbm(pair_f, got_b))


def _adam_math(w, g, m, v):
    m2 = ADAM_B1 * m + (1.0 - ADAM_B1) * g
    v2 = ADAM_B2 * v + (1.0 - ADAM_B2) * (g * g)
    m_hat = m2 / (1.0 - ADAM_B1 ** ADAM_STEP)
    v_hat = v2 / (1.0 - ADAM_B2 ** ADAM_STEP)
    delta = -ADAM_LR * (m_hat / (jnp.sqrt(v_hat) + ADAM_EPS) + ADAM_WD * w)
    return delta, m2, v2


def _adam_big(w, g, m, v, name):
    rows, cols = w.shape
    tr = _row_tile(rows)

    def body(w_ref, g_ref, m_ref, v_ref, d_out, m_out, v_out):
        d, m2, v2 = _adam_math(w_ref[...], g_ref[...], m_ref[...], v_ref[...])
        d_out[...] = d
        m_out[...] = m2
        v_out[...] = v2

    spec = pl.BlockSpec((tr, cols), lambda i: (i, 0))
    return pl.pallas_call(
        body, name=name, grid=(rows // tr,), in_specs=[spec] * 4, out_specs=[spec] * 3,
        out_shape=[_big(w.shape, F32)] * 3,
        compiler_params=_cp(("arbitrary",), 48),
    )(*_hbm(w, g, m, v))


def _ada_grad_adam(c8t, dmod_sh, w, m, v):
    rows, cols = w.shape
    tr = _row_tile(rows) // 2

    def body(ct_ref, dm_ref, w_ref, m_ref, v_ref, g_out, d_out, m_out, v_out):
        g = None
        for b in range(N_DEV):
            term = ct_ref[:, b:b + 1] * dm_ref[b:b + 1, :]
            g = term if g is None else g + term
        d, m2, v2 = _adam_math(w_ref[...], g, m_ref[...], v_ref[...])
        g_out[...] = g
        d_out[...] = d
        m_out[...] = m2
        v_out[...] = v2

    spec = pl.BlockSpec((tr, cols), lambda i: (i, 0))
    return pl.pallas_call(
        body, name="ada_grad_adam", grid=(rows // tr,),
        in_specs=[pl.BlockSpec((tr, N_DEV), lambda i: (i, 0)), _full((N_DEV, cols)), spec, spec, spec],
        out_specs=[spec] * 4, out_shape=[_big(w.shape, F32)] * 4,
        compiler_params=_cp(("arbitrary",), 32),
    )(c8t, dmod_sh, *_hbm(w, m, v))


def _tok_tile(t):
    return min(TOK_TILE, t)


def _mix_in_fwd(chip1, x, modr, g_pre, b_in4, w_in_buf, w_out_buf):
    T = x.shape[0]
    tm = min(MIXIN_TILE, T)
    nt = T // tm
    nb = IN_COLS // N_CHIPS

    def body(ch_ref, x_ref, mod_ref, g_ref, b_ref, win_in, wout_in, p_ref, h_ref, win_ref, wout_ref,
             h_all, wblk, lsem, is_sem, ir_sem, os_sem, or_sem):
        k, i = pl.program_id(0), pl.program_id(1)
        chip = ch_ref[0]

        def load_block(blk):
            cp = pltpu.make_async_copy(win_ref.at[blk], wblk, lsem)
            cp.start()
            cp.wait()

        @pl.when((k == 0) & (i == 0))
        def _():
            _gather_start([win_ref], is_sem, ir_sem, relations=(0, 1))
            load_block(chip)

        for r in range(N_CHIPS - 1):
            @pl.when((k == r + 1) & (i == 0))
            def _(r=r):
                _gather_arrive([win_ref], r, is_sem, ir_sem)
                if r == 0:
                    _gather_start([win_ref], is_sem, ir_sem, relations=(2,))
                if r == 1:
                    _gather_start([wout_ref], os_sem, or_sem)
                load_block(chip ^ (r + 1))

        rows = pl.ds(pl.multiple_of(i * tm, tm), tm)

        @pl.when(k == 0)
        def _():
            xv = x_ref[...]
            rstd = lax.rsqrt(jnp.mean(xv * xv, axis=-1, keepdims=True) + RMS_EPS)
            h = (xv * rstd) * g_ref[...] * (1.0 + mod_ref[1:2, :]) + mod_ref[0:1, :]
            hb = h.astype(BF16)
            h_ref[...] = hb
            h_all[rows, :] = hb

        p_ref[...] = _dot(h_all[rows, :], wblk[...]) + b_ref[chip ^ k]

        @pl.when((k == N_CHIPS - 1) & (i == nt - 1))
        def _():
            _gather_sends_done([win_ref], is_sem, ir_sem)
            _gather_finish([wout_ref], os_sem, or_sem)

    hbm = pl.BlockSpec(memory_space=pl.ANY)
    first_pass = lambda k, i, ch: (jnp.where(k == 0, i, nt - 1), 0)
    gs = pltpu.PrefetchScalarGridSpec(
        num_scalar_prefetch=1, grid=(N_CHIPS, nt),
        in_specs=[pl.BlockSpec((tm, D_MODEL), first_pass), pl.BlockSpec((6, D_MODEL), lambda k, i, ch: (0, 0)),
                  pl.BlockSpec((1, D_MODEL), lambda k, i, ch: (0, 0)),
                  pl.BlockSpec((N_CHIPS, 1, nb), lambda k, i, ch: (0, 0, 0)), hbm, hbm],
        out_specs=[pl.BlockSpec((tm, nb), lambda k, i, ch: (i, ch[0] ^ k)), pl.BlockSpec((tm, D_MODEL), first_pass),
                   hbm, hbm],
        scratch_shapes=[pltpu.VMEM((T, D_MODEL), BF16), pltpu.VMEM((D_MODEL, nb), BF16), pltpu.SemaphoreType.DMA]
        + _gather_sems(1) + _gather_sems(1))
    return pl.pallas_call(
        body, name="mix_in_fwd", grid_spec=gs,
        out_shape=[_big((T, IN_COLS), F32), _big((T, D_MODEL), BF16), _big(w_in_buf.shape, BF16),
                   _big(w_out_buf.shape, BF16)],
        input_output_aliases={5: 2, 6: 3},
        compiler_params=_cp(("arbitrary", "arbitrary"), 48),
    )(chip1, _hbm(x), modr, g_pre, b_in4, _hbm(w_in_buf), _hbm(w_out_buf))


def _mixers_fwd(p, wdw, vecs, lbl, w_out, x, modr, g_post, g_ffn, gbufs):
    T = p.shape[0]
    tm = _tok_tile(T)
    nt = T // tm
    nch = tm // CHUNK
    ng = len(gbufs)
    n_in, n_out = 12, 7

    def body(*refs):
        (p_ref, wdw_ref, bdw_ref, gain_ref, bias_ref, gout_ref, lbl_ref, wout_ref, x_ref, mod_ref, gp_ref,
         gf_ref) = refs[:n_in]
        cat_ref, ys_ref, o_ref, st_ref, y_ref, x1_ref, h2_ref = refs[n_in + ng:n_in + ng + n_out]
        gout_bufs = refs[n_in + ng + n_out:n_in + 2 * ng + n_out]
        (ubuf, state, qt_s, kt_s, kh_s, v_s, egl_s, lower_s, same_s, gmat_s, gssem,
         grsem) = refs[n_in + 2 * ng + n_out:]
        i = pl.program_id(0)

        @pl.when(i == 0)
        def _():
            _gather_start(gout_bufs, gssem, grsem)
            lower_s[...], _, same_s[...] = _chunk_masks(tm)
            gmat_s[...] = _gn_matrix()
            state[...] = jnp.zeros(state.shape, F32)
            ubuf[0:HALO, :] = jnp.zeros((HALO, CONV_CH), F32)
            ubuf[HALO + tm:HALO + tm + SUB, :] = jnp.zeros((SUB, CONV_CH), F32)

        @pl.when(i > 0)
        def _():
            ubuf[0:HALO, :] = ubuf[tm:tm + HALO, :]

        ubuf[HALO:HALO + tm, :] = p_ref[:, 0:CONV_CH] * _sig(p_ref[:, CONV_CH:2 * CONV_CH])
        for r in range(tm // CONV_ROWS):
            rows = slice(r * CONV_ROWS, (r + 1) * CONV_ROWS)
            for lb_ in range(CONV_CH // LANE):
                lanes = slice(lb_ * LANE, (lb_ + 1) * LANE)
                ys_ref[rows, lanes] = bdw_ref[:, lanes] + _tap_conv(ubuf, wdw_ref, r * CONV_ROWS, CONV_FWD_TAPS, lanes)
        gmat = gmat_s[...]
        yv = ys_ref[...]
        d = yv - _gmean(yv, gmat)
        rs = lax.rsqrt(_gmean(d * d, gmat) + GN_EPS)
        z = d * rs * gain_ref[...] + bias_ref[...]
        cat_ref[:, 0:CONV_CH] = (z * _sig(z)).astype(BF16)

        lb, _ = _lower_bound(lbl_ref)
        lower, same = lower_s[...], same_s[...]
        o0 = 2 * CONV_CH
        pr = _hgrn_prep(p_ref[:, o0:o0 + HGRN_W], p_ref[:, o0 + HGRN_W:o0 + 2 * HGRN_W], lb, lower, same)
        qt_s[...] = pr["qt"].astype(BF16)
        kt_s[...] = pr["kt"].astype(BF16)
        kh_s[...] = pr["kh"].astype(BF16)
        v_s[...] = p_ref[:, o0 + 2 * HGRN_W:o0 + 3 * HGRN_W].astype(BF16)
        egl_s[...] = jnp.exp(pr["Gl"])
        tri = _tri()

        def chunk(ci, carry):
            r0 = pl.multiple_of(ci * CHUNK, CHUNK)
            rows = pl.ds(r0, CHUNK)
            for h in range(N_HEADS):
                ls = pl.ds(h * HEAD_D, HEAD_D)
                qc, kc, hc, vc = qt_s[rows, ls], kt_s[rows, ls], kh_s[rows, ls], v_s[rows, ls]
                s0 = state[h]
                s0b = s0.astype(BF16)
                st_ref[ci, h] = s0
                att =jnp.where(tri, _dot_nt(qc, kc), 0.0).astype(BF16)
                o_ref[rows, ls] = _dot(att, vc) + _dot_nt(qc, s0b)
                state[h] = s0 * egl_s[pl.ds(r0, 1), ls] + _dot_tn(vc, hc)
            return carry

        lax.fori_loop(0, nch, chunk, 0, unroll=min(CHUNK_UNROLL, nch))
        for h in range(N_HEADS):
            sl = slice(h * HEAD_D, (h + 1) * HEAD_D)
            oh = o_ref[:, sl]
            gh = p_ref[:, o0 + 3 * HGRN_W + h * HEAD_D:o0 + 3 * HGRN_W + (h + 1) * HEAD_D]
            rsh = lax.rsqrt(jnp.mean(oh * oh, axis=-1, keepdims=True) + RMS_EPS)
            hg = (oh * rsh) * gout_ref[:, sl] * (gh * _sig(gh))
            cat_ref[:, CONV_CH + h * HEAD_D:CONV_CH + (h + 1) * HEAD_D] = hg.astype(BF16)

        yv = _dot(cat_ref[...], wout_ref[...])
        y_ref[...] = yv
        rsy = lax.rsqrt(jnp.mean(yv * yv, axis=-1, keepdims=True) + RMS_EPS)
        x1 = x_ref[...] + mod_ref[2:3, :] * ((yv * rsy) * gp_ref[...])
        x1_ref[...] = x1
        rs1 = lax.rsqrt(jnp.mean(x1 * x1, axis=-1, keepdims=True) + RMS_EPS)
        h2 = (x1 * rs1) * gf_ref[...] * (1.0 + mod_ref[4:5, :]) + mod_ref[3:4, :]
        h2_ref[...] = h2.astype(BF16)

        @pl.when(i == max(nt - 2, 0))
        def _():
            _gather_pass_on(gout_bufs, gssem, grsem)

        @pl.when(i == nt - 1)
        def _():
            _gather_drain(gout_bufs, gssem, grsem)

    tile = lambda cols: pl.BlockSpec((tm, cols), lambda i: (i, 0))
    hbm = pl.BlockSpec(memory_space=pl.ANY)
    outs = pl.pallas_call(
        body, name="mixers_fwd", grid=(nt,),
        in_specs=[tile(IN_COLS), _full((HALO, CONV_CH))] + [_full((1, CONV_CH))] * 4 + [_full((2, HGRN_W))]
        + [_full((D_MODEL, D_MODEL)), tile(D_MODEL), _full((6, D_MODEL)), _full((1, D_MODEL)), _full((1, D_MODEL))]
        + [hbm] * ng,
        out_specs=[tile(D_MODEL), tile(CONV_CH), tile(HGRN_W),
                   pl.BlockSpec((nch, N_HEADS, HEAD_D, HEAD_D), lambda i: (i, 0, 0, 0)),
                   tile(D_MODEL), tile(D_MODEL), tile(D_MODEL)] + [hbm] * ng,
        out_shape=[_big((T, D_MODEL), BF16), _big((T, CONV_CH), F32), _big((T, HGRN_W), F32),
                   _big((T // CHUNK, N_HEADS, HEAD_D, HEAD_D), F32), _big((T, D_MODEL), F32),
                   _big((T, D_MODEL), F32), _big((T, D_MODEL), BF16)] + [_big(b.shape, BF16) for b in gbufs],
        input_output_aliases={n_in + t: n_out + t for t in range(ng)},
        scratch_shapes=[pltpu.VMEM((tm + HALO + SUB, CONV_CH), F32), pltpu.VMEM((N_HEADS, HEAD_D, HEAD_D), F32),
                        pltpu.VMEM((tm, HGRN_W), BF16), pltpu.VMEM((tm, HGRN_W), BF16),
                        pltpu.VMEM((tm, HGRN_W), BF16), pltpu.VMEM((tm, HGRN_W), BF16),
                        pltpu.VMEM((tm, HGRN_W), F32), pltpu.VMEM((tm, tm), BF16), pltpu.VMEM((tm, tm), BF16),
                        pltpu.VMEM((CONV_CH, CONV_CH), BF16)] + _gather_sems(ng),
        compiler_params=_cp(("arbitrary",), 56),
    )(_hbm(p), wdw, *vecs, lbl, *_hbm(w_out, x), modr, g_post, g_ffn, *[_hbm(b) for b in gbufs])
    return outs[:n_out], outs[n_out:]


def _row_chains(rows, n=2):
    step = rows // n
    return [slice(k * step, (k + 1) * step) for k in range(n)]


def _ffn_blocks():
    return D_FF // FFN_BLOCK, (D_FF // N_CHIPS) // FFN_BLOCK


def _ffn_fwd(h2, w_up_g, w_down, x1, target, modr, g_post):
    T = h2.shape[0]
    tm = min(FFN_TILE, T)
    fb = FFN_BLOCK
    nj, per = _ffn_blocks()

    def body(h_ref, wu_ref, wd_ref, x1_ref, t_ref, mod_ref, g_ref, r_ref, dy2_ref, dx2_ref, st_ref, acc):
        i, j = pl.program_id(0), pl.program_id(1)

        @pl.when((i == 0) & (j == 0))
        def _():
            st_ref[...] = jnp.zeros(st_ref.shape, F32)

        @pl.when(j == 0)
        def _():
            acc[...] = jnp.zeros(acc.shape, F32)

        for rows in _row_chains(tm):
            ra = jnp.maximum(_dot(h_ref[rows, :], wu_ref[0]), 0.0)
            rb = (ra * ra).astype(BF16)
            r_ref[rows, :] = rb
            acc[rows, :] += _dot(rb, wd_ref[...])

        @pl.when(j == nj - 1)
        def _():
            y2 = acc[...]
            rs = lax.rsqrt(jnp.mean(y2 * y2, axis=-1, keepdims=True) + RMS_EPS)
            nh = y2 * rs
            gp = g_ref[...]
            err = x1_ref[...] + mod_ref[5:6, :] * (nh * gp) - t_ref[...]
            dx2 = err * (1.0 / D_MODEL)
            dx2_ref[...] = dx2
            st_ref[0:1, :] += _colsum(err * err)
            st_ref[1:2, :] += _colsum(dx2 * (nh * gp))
            dn = dx2 * mod_ref[5:6, :]
            st_ref[2:3, :] += _colsum(dn * nh)
            dy2_ref[...] = _rms_bwd(dn * gp, nh, rs).astype(BF16)

    tile = pl.BlockSpec((tm, D_MODEL), lambda i, j: (i, 0))
    return pl.pallas_call(
        body, name="ffn_fwd", grid=(T // tm, nj),
        in_specs=[tile, pl.BlockSpec((1, D_MODEL, fb), lambda i, j: (j // per, 0, j % per)),
                  pl.BlockSpec((fb, D_MODEL), lambda i, j: (j, 0)), tile, tile,
                  _full((6, D_MODEL)), _full((1, D_MODEL))],
        out_specs=[pl.BlockSpec((tm, fb), lambda i, j: (i, j)), tile, tile, _full((8, D_MODEL))],
        out_shape=[_big((T, D_FF), BF16), _big((T, D_MODEL), BF16), _big((T, D_MODEL), F32),
                   jax.ShapeDtypeStruct((8, D_MODEL), F32)],
        scratch_shapes=[pltpu.VMEM((tm, D_MODEL), F32)],
        compiler_params=_cp(("arbitrary", "arbitrary"), 56),
    )(*_hbm(h2, w_up_g, w_down, x1, target), modr, g_post)


def _rms_bwd(dxn, xn, rs):
    return rs * (dxn - xn * jnp.mean(dxn * xn, axis=-1, keepdims=True))


def _ffn_bwd(dy2, r, x1, dx2, w_up_g, w_down, modr, g_ffn):
    T = dx2.shape[0]
    tm = min(FFN_TILE, T)
    fb = FFN_BLOCK
    nj, per = _ffn_blocks()

    def body(dy2_ref, r_ref, x1_ref, dx2_ref, wu_ref, wd_ref, mod_ref, gf_ref, da_ref, dx1_ref, st_ref, dh_s):
        i, j = pl.program_id(0), pl.program_id(1)

        @pl.when((i == 0) & (j == 0))
        def _():
            st_ref[...] = jnp.zeros(st_ref.shape, F32)

        @pl.when(j == 0)
        def _():
            dh_s[...] = jnp.zeros(dh_s.shape, F32)

        for rows in _row_chains(tm):
            ra = jnp.sqrt(r_ref[rows, :].astype(F32))
            da = (_dot_nt(dy2_ref[rows, :], wd_ref[...]) * (2.0 * ra)).astype(BF16)
            da_ref[rows, :] = da
            dh_s[rows, :] += _dot_nt(da, wu_ref[0])

        @pl.when(j == nj - 1)
        def _():
            dh = dh_s[...]
            x1v = x1_ref[...]
            rs1 = lax.rsqrt(jnp.mean(x1v * x1v, axis=-1, keepdims=True) + RMS_EPS)
            xn = x1v * rs1
            st_ref[0:1, :] += _colsum(dh)
            st_ref[1:2, :] += _colsum(dh * (xn * gf_ref[...]))
            dsc = dh * (1.0 + mod_ref[4:5, :])
            st_ref[2:3, :] += _colsum(dsc * xn)
            dx1_ref[...] = dx2_ref[...] + _rms_bwd(dsc * gf_ref[...], xn, rs1)

    tile = pl.BlockSpec((tm, D_MODEL), lambda i, j: (i, 0))
    ftile = pl.BlockSpec((tm, fb), lambda i, j: (i, j))
    return pl.pallas_call(
        body, name="ffn_bwd", grid=(T // tm, nj),
        in_specs=[tile, ftile, tile, tile, pl.BlockSpec((1, D_MODEL, fb), lambda i, j: (j // per, 0, j % per)),
                  pl.BlockSpec((fb, D_MODEL), lambda i, j: (j, 0)), _full((6, D_MODEL)), _full((1, D_MODEL))],
        out_specs=[ftile, tile, _full((8, D_MODEL))],
        out_shape=[_big((T, D_FF), BF16), _big((T, D_MODEL), F32), jax.ShapeDtypeStruct((8, D_MODEL), F32)],
        scratch_shapes=[pltpu.VMEM((tm, D_MODEL), F32)],
        compiler_params=_cp(("arbitrary", "arbitrary"), 56),
    )(*_hbm(dy2, r, x1, dx2, w_up_g, w_down), modr, g_ffn)


def _mix_out_bwd(dx1, y, cat, w_out, modr, g_post, swap):
    T = dx1.shape[0]
    tm = _tok_tile(T)
    nt = T // tm
    ns = len(swap)

    def body(*refs):
        dx1_ref, y_ref, cat_ref, w_ref, mod_ref, gp_ref = refs[:6]
        s_ins = refs[6:6 + ns]
        dcat_ref, st_ref, gw_ref = refs[6 + ns:9 + ns]
        s_outs = refs[9 + ns:9 + 2 * ns]
        gacc, gsem, pssem, prsem = refs[9 + 2 * ns:]
        i = pl.program_id(0)

        @pl.when(i == 0)
        def _():
            for cp in _pair_copies(s_ins, s_outs, pssem, prsem):
                cp.start()
            st_ref[...] = jnp.zeros(st_ref.shape, F32)
            gacc[...] = jnp.zeros(gacc.shape, F32)

        dxv, yv = dx1_ref[...], y_ref[...]
        rs = lax.rsqrt(jnp.mean(yv * yv, axis=-1, keepdims=True) + RMS_EPS)
        nh = yv * rs
        st_ref[0:1, :] += _colsum(dxv * (nh * gp_ref[...]))
        dn = dxv * mod_ref[2:3, :]
        st_ref[1:2, :] += _colsum(dn * nh)
        dy = _rms_bwd(dn * gp_ref[...], nh, rs).astype(BF16)
        dcat_ref[...] = _dot_nt(dy, w_ref[...])
        for cols in _row_chains(D_MODEL):
            gacc[:, cols] += _dot_tn(cat_ref[...], dy[:, cols])

        @pl.when(i == nt - 1)
        def _():
            out = pltpu.make_async_copy(gacc, gw_ref, gsem)
            out.start()
            copies = _pair_copies(s_ins, s_outs, pssem, prsem)
            for cp in copies:
                cp.wait_recv()
            for cp in copies:
                cp.wait_send()
            out.wait()

    tile = pl.BlockSpec((tm, D_MODEL), lambda i: (i, 0))
    hbm = pl.BlockSpec(memory_space=pl.ANY)
    outs = pl.pallas_call(
        body, name="mix_out_bwd", grid=(nt,),
        in_specs=[tile, tile, tile, _full((D_MODEL, D_MODEL)), _full((6, D_MODEL)), _full((1, D_MODEL))]
        + [hbm] * ns,
        out_specs=[tile, _full((8, D_MODEL)), hbm] + [hbm] * ns,
        out_shape=[_big((T, D_MODEL), F32), jax.ShapeDtypeStruct((8, D_MODEL), F32), _big((D_MODEL, D_MODEL), F32)]
        + _pair_shapes(swap),
        scratch_shapes=[pltpu.VMEM((D_MODEL, D_MODEL), F32), pltpu.SemaphoreType.DMA] + _pair_sems(ns),
        compiler_params=_cp(("arbitrary",), 48),
    )(*_hbm(dx1, y, cat, w_out), modr, g_post, *[_hbm(g) for g in swap])
    return outs[:3], outs[3:]


def _mixers_bwd(p, dcat, ys, o, states, h1, wdw, vecs, lbl, pairs_b):
    T = p.shape[0]
    tm = min(MIXB_TILE, T)
    nt = T // tm
    nch = tm // CHUNK
    hpt = tm // HALO
    nx = len(pairs_b)
    nb = IN_COLS // N_CHIPS

    def body(*refs):
        (p_ref, ph_ref, dcat_ref, ys_ref, o_ref, st_ref, h1_ref, wdw_ref, bdw_ref, gain_ref, bias_ref, gout_ref,
         lbl_ref) = refs[:13]
        x_ins = refs[13:13 + nx]
        dp_ref, sb_ref, s5_ref, dw_ref = refs[13 + nx:17 + nx]
        x_outs = refs[17 + nx:17 + 2 * nx]
        gin_ref = refs[17 + 2 * nx]
        (ubuf, dybuf, carry, dstate, qt_s, kt_s, kh_s, v_s, do_s, egl_s, dqt_s, dkt_s, dkh_s, dv_s, dgl_s,
         dsh, dw8, dshift, lower_s, upper_s, same_s, gmat_s, gacc, dp_prev, h1_prev, gsem, xssem,
         xrsem) = refs[18 + 2 * nx:]
        i = pl.program_id(0)
        tile_idx = nt - 1 - i

        @pl.when(i == 0)
        def _():
            for cp in _xchg_copies(x_ins, x_outs, xssem, xrsem):
                cp.start()
            gacc[...] = jnp.zeros(gacc.shape, F32)
            dstate[...] = jnp.zeros(dstate.shape, F32)
            carry[...] = jnp.zeros(carry.shape, F32)
            sb_ref[...] = jnp.zeros(sb_ref.shape, F32)
            s5_ref[...] = jnp.zeros(s5_ref.shape, F32)
            dw_ref[...] = jnp.zeros(dw_ref.shape, F32)
            dw8[...] = jnp.zeros(dw8.shape, F32)
            lower_s[...], upper_s[...], same_s[...] = _chunk_masks(tm)
            gmat_s[...] = _gn_matrix()
            dsh[0:SUB, :] = jnp.zeros((SUB, CONV_CH), F32)
            dsh[SUB + tm:2 * SUB + tm, :] = jnp.zeros((SUB, CONV_CH), F32)
            ubuf[HALO + tm:HALO + tm + SUB, :] = jnp.zeros((SUB, CONV_CH), F32)
            dp_prev[...] = jnp.zeros(dp_prev.shape, BF16)
            h1_prev[...] = jnp.zeros(h1_prev.shape, BF16)

        n_pieces = (tm // CONV_ROWS) * (CONV_CH // LANE)
        per_block = n_pieces // N_CHIPS
        prow = D_MODEL // per_block

        def w_in_grad_piece(k):
            j, part = k // per_block, k % per_block
            rows_k = slice(part * prow, (part + 1) * prow)
            gacc[j, rows_k, :] += _dot_tn(h1_prev[:, rows_k], dp_prev[:, j * nb:(j + 1) * nb])

        uh = ph_ref[:, 0:CONV_CH] * _sig(ph_ref[:, CONV_CH:2 * CONV_CH])
        ubuf[0:HALO, :] = jnp.where(tile_idx > 0, uh, 0.0)
        ubuf[HALO:HALO + tm, :] = p_ref[:, 0:CONV_CH] * _sig(p_ref[:, CONV_CH:2 * CONV_CH])
        gmat = gmat_s[...]
        gain = gain_ref[...]
        yv = ys_ref[...]
        d = yv - _gmean(yv, gmat)
        rs = lax.rsqrt(_gmean(d * d, gmat) + GN_EPS)
        yn = d * rs
        z = yn * gain + bias_ref[...]
        sz = _sig(z)
        dz = dcat_ref[:, 0:CONV_CH] * (sz * (1.0 + z * (1.0 - sz)))
        dyn = dz * gain
        dyc = rs * (dyn - _gmean(dyn, gmat) - yn * _gmean(dyn * yn, gmat))
        s5_ref[0:1, :] += _colsum(dyc)
        s5_ref[1:2, :] += _colsum(dz * yn)
        s5_ref[2:3, :] += _colsum(dz)
        dybuf[tm:tm + HALO, :] = carry[...]
        dybuf[0:tm, :] = dyc
        dsh[SUB:SUB + tm, :] = dyc
        carry[...] = dyc[0:HALO, :]
        for b in range(SUB):
            dshift[...] = dsh[SUB - b:2 * SUB - b + tm, :]
            for j, off in CONV_FWD_TAPS:
                if off % SUB == b:
                    prod = dshift[...] * ubuf[off - b:off - b + tm + SUB, :]
                    dw8[j] += jnp.sum(prod.reshape((tm + SUB) // SUB, SUB, CONV_CH), axis=0)
        for r in range(tm // CONV_ROWS):
            rows = slice(r * CONV_ROWS, (r + 1) * CONV_ROWS)
            for lb_ in range(CONV_CH // LANE):
                lanes = slice(lb_ * LANE, (lb_ + 1) * LANE)
                glanes = slice(CONV_CH + lb_ * LANE, CONV_CH + (lb_ + 1) * LANE)
                w_in_grad_piece(r * (CONV_CH // LANE) + lb_)
                acc = _tap_conv(dybuf, wdw_ref, r * CONV_ROWS, CONV_BWD_TAPS, lanes)
                val = p_ref[rows, lanes]
                sg = _sig(p_ref[rows, glanes])
                dval = acc * sg
                dgate = acc * val * (sg * (1.0 - sg))
                dp_ref[rows, lanes] = dval.astype(BF16)
                dp_ref[rows, glanes] = dgate.astype(BF16)
                sb_ref[0:1, lanes] += _colsum(dval)
                sb_ref[0:1, glanes] += _colsum(dgate)

        o0 = 2 * CONV_CH
        for h in range(N_HEADS):
            sl = slice(h * HEAD_D, (h + 1) * HEAD_D)
            gsl = slice(o0 + 3 * HGRN_W + h * HEAD_D, o0 + 3 * HGRN_W + (h + 1) * HEAD_D)
            oh = o_ref[:, sl]
            gh = p_ref[:, gsl]
            dh = dcat_ref[:, CONV_CH + h * HEAD_D:CONV_CH + (h + 1) * HEAD_D]
            gout = gout_ref[:, sl]
            rsh = lax.rsqrt(jnp.mean(oh * oh, axis=-1, keepdims=True) + RMS_EPS)
            on = oh * rsh
            sgg = _sig(gh)
            dgh = dh * (on * gout) * (sgg * (1.0 + gh * (1.0 - sgg)))
            dm = dh * (gh * sgg)
            s5_ref[3:4, sl] += _colsum(dm * on)
            do_s[:, sl] = _rms_bwd(dm * gout, on, rsh).astype(BF16)
            dp_ref[:, gsl] = dgh.astype(BF16)
            sb_ref[2:3, CONV_CH + h * HEAD_D:CONV_CH + (h + 1) * HEAD_D] += _colsum(dgh)

        lb, _ = _lower_bound(lbl_ref)
        lower, upper, same = lower_s[...], upper_s[...], same_s[...]
        pq = p_ref[:, o0:o0 + HGRN_W]
        pr = _hgrn_prep(pq, p_ref[:, o0 + HGRN_W:o0 + 2 * HGRN_W], lb, lower, same)
        qt_s[...] = pr["qt"].astype(BF16)
        kt_s[...] = pr["kt"].astype(BF16)
        kh_s[...] = pr["kh"].astype(BF16)
        v_s[...] = p_ref[:, o0 + 2 * HGRN_W:o0 + 3 * HGRN_W].astype(BF16)
        egl_s[...] = jnp.exp(pr["Gl"])
        tri = _tri()

        def chunk(it, c_):
            ci = nch - 1 - it
            r0 = pl.multiple_of(ci * CHUNK, CHUNK)
            rows = pl.ds(r0, CHUNK)
            for h in range(N_HEADS):
                ls = pl.ds(h * HEAD_D, HEAD_D)
                qc, kc, hc, vc = qt_s[rows, ls], kt_s[rows, ls], kh_s[rows, ls], v_s[rows, ls]
                dob = do_s[rows, ls]
                s0 = st_ref[ci, h]
                s0b = s0.astype(BF16)
                ds1 = dstate[h]
                ds1b = ds1.astype(BF16)
                egl = egl_s[pl.ds(r0, 1), ls]
                att = jnp.where(tri, _dot_nt(qc, kc), 0.0).astype(BF16)
                datt = jnp.where(tri, _dot_nt(dob, vc), 0.0).astype(BF16)
                dv_s[rows, ls] = _dot_tn(att, dob) + _dot_nt(hc, ds1b)
                dqt_s[rows, ls] = _dot(datt, kc) + _dot(dob, s0b)
                dkt_s[rows, ls] = _dot_tn(datt, qc)
                dkh_s[rows, ls] = _dot(vc, ds1b)
                dgl = egl * _colsum(ds1 * s0)
                dgl_s[rows, ls] = jnp.broadcast_to(dgl, (CHUNK, HEAD_D))
                dstate[h] = ds1 * egl + _dot_tn(dob, qc)
            return c_

        lax.fori_loop(0, nch, chunk, 0, unroll=min(CHUNK_UNROLL, nch))
        dqt, dkt, dkh = dqt_s[...], dkt_s[...], dkh_s[...]
        dk = dkt * pr["enG"] + dkh * pr["eGlG"]
        khk = dkh * kh_s[...].astype(F32)
        dG = dqt * qt_s[...].astype(F32) - dkt * kt_s[...].astype(F32) - khk
        dlogf = _mm3(upper, dG) + _mm3(same, khk) + dgl_s[...]
        df = dlogf / pr["f"] - dk
        sf, sq = pr["sf"], pr["sq"]
        s5_ref[4:5, :] += _colsum(df * (1.0 - sf))
        dfl = df * (1.0 - lb) * (sf * (1.0 - sf))
        dq = (dqt * pr["eG"]) * (sq * (1.0 + pq * (1.0 - sq)))
        dvv = dv_s[...]
        dp_ref[:, o0:o0 + HGRN_W] = dq.astype(BF16)
        dp_ref[:, o0 + HGRN_W:o0 + 2 * HGRN_W] = dfl.astype(BF16)
        dp_ref[:, o0 + 2 * HGRN_W:o0 + 3 * HGRN_W] = dvv.astype(BF16)
        sb_ref[1:2, 0:HGRN_W] += _colsum(dq)
        sb_ref[1:2, HGRN_W:2 * HGRN_W] += _colsum(dfl)
        sb_ref[2:3, 0:HGRN_W] += _colsum(dvv)

        dp_prev[...] = dp_ref[...]
        h1_prev[...] = h1_ref[...]

        @pl.when(i == nt - 1)
        def _():
            for k in range(n_pieces):
                w_in_grad_piece(k)
            out = pltpu.make_async_copy(gacc, gin_ref, gsem)
            out.start()
            for j in range(CONV_K):
                dw_ref[j:j + 1, :] = _colsum(dw8[j])
            copies = _xchg_copies(x_ins, x_outs, xssem, xrsem)
            for cp in copies:
                cp.wait_recv()
            for cp in copies:
                cp.wait_send()
            out.wait()

    rev = lambda cols: pl.BlockSpec((tm, cols), lambda i: (nt - 1 - i, 0))
    halo = pl.BlockSpec((HALO, 2 * CONV_CH), lambda i: (jnp.maximum((nt - 1 - i) * hpt - 1, 0), 0))
    wide = lambda n: pltpu.VMEM((tm, HGRN_W), n)
    hbm = pl.BlockSpec(memory_space=pl.ANY)
    outs = pl.pallas_call(
        body, name="mixers_bwd", grid=(nt,),
        in_specs=[rev(IN_COLS), halo, rev(D_MODEL), rev(CONV_CH), rev(HGRN_W),
                  pl.BlockSpec((nch, N_HEADS, HEAD_D, HEAD_D), lambda i: (nt - 1 - i, 0, 0, 0)), rev(D_MODEL),
                  _full((HALO, CONV_CH))] + [_full((1, CONV_CH))] * 4 + [_full((2, HGRN_W))] + [hbm] * nx,
        out_specs=[rev(IN_COLS), _full((8, D_MODEL)), _full((8, CONV_CH)), _full((HALO, CONV_CH))]
        + [hbm] * (nx + 1),
        out_shape=[_big((T, IN_COLS), BF16), jax.ShapeDtypeStruct((8, D_MODEL), F32),
                   jax.ShapeDtypeStruct((8, CONV_CH), F32), jax.ShapeDtypeStruct((HALO, CONV_CH), F32)]
        + [_big(pb.shape, BF16) for pb in pairs_b] + [_big((N_CHIPS, D_MODEL, nb), F32)],
        scratch_shapes=[pltpu.VMEM((tm + HALO + SUB, CONV_CH), F32), pltpu.VMEM((tm + HALO, CONV_CH), F32),
                        pltpu.VMEM((HALO, CONV_CH), F32), pltpu.VMEM((N_HEADS, HEAD_D, HEAD_D), F32),
                        wide(BF16), wide(BF16), wide(BF16), wide(BF16), wide(BF16),
                        wide(F32), wide(F32), wide(F32), wide(F32), wide(F32), wide(F32),
                        pltpu.VMEM((tm + 2 * SUB, CONV_CH), F32), pltpu.VMEM((HALO, SUB, CONV_CH), F32),
                        pltpu.VMEM((tm + SUB, CONV_CH), F32), pltpu.VMEM((tm, tm), BF16), pltpu.VMEM((tm, tm), BF16),
                        pltpu.VMEM((tm, tm), BF16), pltpu.VMEM((CONV_CH, CONV_CH), BF16),
                        pltpu.VMEM((N_CHIPS, D_MODEL, nb), F32), pltpu.VMEM((tm, IN_COLS), BF16),
                        pltpu.VMEM((tm, D_MODEL), BF16), pltpu.SemaphoreType.DMA]
        + _xchg_sems(nx),
        compiler_params=_cp(("arbitrary",), 56),
    )(*_hbm(p, p, dcat, ys, o, states, h1), wdw, *vecs, lbl, *[_hbm(pb) for pb in pairs_b])
    return outs[:4], outs[4:4 + nx], outs[4 + nx]


def _mix_in_bwd(dp, w_in_g, x, dx1, modr, g_pre, pairs_b):
    T = x.shape[0]
    tm = _tok_tile(T)
    nt = T // tm
    nb = IN_COLS // N_CHIPS
    nx = len(pairs_b)

    def body(*refs):
        dp_ref, w_ref, x_ref, dx1_ref, mod_ref, g_ref = refs[:6]
        x_ins = refs[6:6 + nx]
        gx_ref, st_ref = refs[6 + nx:8 + nx]
        x_outs = refs[8 + nx:8 + 2 * nx]
        xssem, xrsem = refs[8 + 2 * nx:]
        i = pl.program_id(0)

        @pl.when(i == 0)
        def _():
            for cp in _xchg_copies(x_ins, x_outs, xssem, xrsem):
                cp.start()
            st_ref[...] = jnp.zeros(st_ref.shape, F32)

        dh = None
        for j in range(N_CHIPS):
            part = _dot_nt(dp_ref[:, j * nb:(j + 1) * nb], w_ref[j])
            dh = part if dh is None else dh + part
        xv = x_ref[...]
        rs = lax.rsqrt(jnp.mean(xv * xv, axis=-1, keepdims=True) + RMS_EPS)
        xn = xv * rs
        st_ref[0:1, :] += _colsum(dh)
        st_ref[1:2, :] += _colsum(dh * (xn * g_ref[...]))
        dsc = dh * (1.0 + mod_ref[1:2, :])
        st_ref[2:3, :] += _colsum(dsc * xn)
        gx_ref[...] = dx1_ref[...] + _rms_bwd(dsc * g_ref[...], xn, rs)

        @pl.when(i == nt - 1)
        def _():
            copies = _xchg_copies(x_ins, x_outs, xssem, xrsem)
            for cp in copies:
                cp.wait_recv()
            for cp in copies:
                cp.wait_send()

    tile = pl.BlockSpec((tm, D_MODEL), lambda i: (i, 0))
    hbm = pl.BlockSpec(memory_space=pl.ANY)
    outs = pl.pallas_call(
        body, name="mix_in_bwd", grid=(nt,),
        in_specs=[pl.BlockSpec((tm, IN_COLS), lambda i: (i, 0)), _full((N_CHIPS, D_MODEL, nb)), tile, tile,
                  _full((6, D_MODEL)), _full((1, D_MODEL))] + [hbm] * nx,
        out_specs=[tile, _full((8, D_MODEL))] + [hbm] * nx,
        out_shape=[_big((T, D_MODEL), F32), jax.ShapeDtypeStruct((8, D_MODEL), F32)]
        + [_big(pb.shape, BF16) for pb in pairs_b],
        scratch_shapes=_xchg_sems(nx),
        compiler_params=_cp(("arbitrary",), 48),
    )(*_hbm(dp, w_in_g, x, dx1), modr, g_pre, *[_hbm(pb) for pb in pairs_b])
    return outs[:2], outs[2:]


def _weight_grad(a, b, a_blocked, b_blocked, name):
    T = a.shape[0]
    tt = min(GRAD_TILE, T)
    nt = T // tt
    ka = a.shape[1] // N_CHIPS if a_blocked else a.shape[1]
    nb = b.shape[1] // N_CHIPS if b_blocked else b.shape[1]

    def body(a_ref, b_ref, o_ref, ob_ref):
        t = pl.program_id(1)

        @pl.when(t == 0)
        def _():
            o_ref[...] = jnp.zeros(o_ref.shape, F32)

        for cols in _row_chains(nb):
            o_ref[0, :, cols] += _dot_tn(a_ref[...], b_ref[:, cols])

        @pl.when(t == nt - 1)
        def _():
            ob_ref[0] = o_ref[0].astype(BF16)

    blk = pl.BlockSpec((1, ka, nb), lambda j, t: (j, 0, 0))
    return pl.pallas_call(
        body, name=name, grid=(N_CHIPS, nt),
        in_specs=[pl.BlockSpec((tt, ka), (lambda j, t: (t, j)) if a_blocked else (lambda j, t: (t, 0))),
                  pl.BlockSpec((tt, nb), (lambda j, t: (t, j)) if b_blocked else (lambda j, t: (t, 0)))],
        out_specs=[blk, blk],
        out_shape=[_big((N_CHIPS, ka, nb), F32), _big((N_CHIPS, ka, nb), BF16)],
        compiler_params=_cp(("arbitrary", "arbitrary"), 48),
    )(*_hbm(a, b))


R_LOSS = 0
R_FFN = 8
R_OUT = 16
R_IN = 24
R_BIN = 32
R_512 = 40
R_DW = 48
N_STAT_ROWS = 80
MOD_ROWS = (R_IN + 0, R_IN + 1, R_OUT + 0, R_FFN + 0, R_FFN + 1, R_LOSS + 1)


def _small_update(gath, params):
    names = ["b_ada", "lb_logits", "g_pre_mix", "b_in", "b_dw", "gn_gain", "gn_bias", "g_hgrn_out", "g_post_mix",
             "g_pre_ffn", "g_post_ffn"]
    flat = []
    for n in names:
        flat += list(params[n])
    n_in = 1 + len(flat)

    def body(*refs):
        g_ref = refs[0]
        prm = {n: refs[1 + 3 * k:4 + 3 * k] for k, n in enumerate(names)}
        outs = refs[n_in:]
        loss_ref, dmod_ref, dwdw_ref = outs[0], outs[1], outs[2]
        res = {n: outs[3 + 4 * k:7 + 4 * k] for k, n in enumerate(names)}
        red = g_ref[0]
        for dev in range(1, N_DEV):
            red = red + g_ref[dev]
        loss_ref[...] = jnp.broadcast_to(
            (0.5 / D_MODEL) * jnp.sum(red[R_LOSS:R_LOSS + 1, :], axis=-1, keepdims=True), loss_ref.shape)
        for dev in range(N_DEV):
            for k, r in enumerate(MOD_ROWS):
                dmod_ref[dev:dev + 1, k * D_MODEL:(k + 1) * D_MODEL] = g_ref[dev, r:r + 1, :]
        dwdw_ref[...] = red[R_DW:R_DW + HALO, 0:CONV_CH]

        def finish(name, pieces):
            w_ref, m_ref, v_ref = prm[name]
            g_out, d_out, m_out, v_out = res[name]
            for rsl, lsl, g in pieces:
                d, m2, v2 = _adam_math(w_ref[rsl, lsl], g, m_ref[rsl, lsl], v_ref[rsl, lsl])
                g_out[rsl, lsl] = g
                d_out[rsl, lsl] = d
                m_out[rsl, lsl] = m2
                v_out[rsl, lsl] = v2

        one = slice(0, 1)
        row = lambda r: red[r:r + 1, :]
        half = lambda r: red[r:r + 1, 0:CONV_CH]
        finish("b_ada", [(one, slice(k * D_MODEL, (k + 1) * D_MODEL), row(r)) for k, r in enumerate(MOD_ROWS)])
        finish("b_in", [(one, slice(k * D_MODEL, (k + 1) * D_MODEL), row(R_BIN + k)) for k in range(3)])
        finish("g_pre_mix", [(one, slice(None), row(R_IN + 2))])
        finish("g_post_mix", [(one, slice(None), row(R_OUT + 1))])
        finish("g_pre_ffn", [(one, slice(None), row(R_FFN + 2))])
        finish("g_post_ffn", [(one, slice(None), row(R_LOSS + 2))])
        finish("b_dw", [(one, slice(None), half(R_512 + 0))])
        finish("gn_gain", [(one, slice(None), half(R_512 + 1))])
        finish("gn_bias", [(one, slice(None), half(R_512 + 2))])
        finish("g_hgrn_out", [(one, slice(None), half(R_512 + 3))])
        s0, s1 = _lower_bound(prm["lb_logits"][0])
        dlb = half(R_512 + 4)
        finish("lb_logits", [(slice(0, 1), slice(None), dlb * s0 * (1.0 - s0)),
                             (slice(1, 2), slice(None), -dlb * s0 * s1)])

    vm = pl.BlockSpec(memory_space=pltpu.VMEM)
    out_shape = [jax.ShapeDtypeStruct((8, 128), F32), jax.ShapeDtypeStruct((N_DEV, 6 * D_MODEL), F32),
                 jax.ShapeDtypeStruct((HALO, CONV_CH), F32)]
    for n in names:
        out_shape += [jax.ShapeDtypeStruct(params[n][0].shape, F32)] * 4
    outs = pl.pallas_call(
        body, name="small_update", out_shape=out_shape,
        in_specs=[vm] * n_in, out_specs=[vm] * len(out_shape),
        compiler_params=_cp(None, 32),
    )(gath, *flat)
    return outs[0], outs[1], outs[2], {n: outs[3 + 4 * k:7 + 4 * k] for k, n in enumerate(names)}


def _wdw_adam(w, g, m, v):
    def body(w_ref, g_ref, m_ref, v_ref, d_out, m_out, v_out):
        d, m2, v2 = _adam_math(w_ref[...], g_ref[...], m_ref[...], v_ref[...])
        d_out[...] = d
        m_out[...] = m2
        v_out[...] = v2

    vm = pl.BlockSpec(memory_space=pltpu.VMEM)
    return pl.pallas_call(
        body, name="wdw_adam", out_shape=[jax.ShapeDtypeStruct(w.shape, F32)] * 3,
        in_specs=[vm] * 4, out_specs=[vm] * 3, compiler_params=_cp(None, 16),
    )(w, g, m, v)


def kernel(x, c, w_ada, b_ada, lb_logits, g_pre_mix, w_in, b_in, w_dw, b_dw, gn_gain, gn_bias, g_hgrn_out, w_out, g_post_mix, g_pre_ffn, w_up, w_down, g_post_ffn, loss_target, m_w_ada, m_b_ada, m_lb_logits, m_g_pre_mix, m_w_in, m_b_in, m_w_dw, m_b_dw, m_gn_gain, m_gn_bias, m_g_hgrn_out, m_w_out, m_g_post_mix, m_g_pre_ffn, m_w_up, m_w_down, m_g_post_ffn, v_w_ada, v_b_ada, v_lb_logits, v_g_pre_mix, v_w_in, v_b_in, v_w_dw, v_b_dw, v_gn_gain, v_gn_bias, v_g_hgrn_out, v_w_out, v_g_post_mix, v_g_pre_ffn, v_w_up, v_w_down, v_g_post_ffn):
    ax, ay, ac = lax.axis_index("x"), lax.axis_index("y"), lax.axis_index("c")
    chip = 2 * ax + ay
    T = x.shape[1]
    xs, tgt = x[0], loss_target[0]
    ada_cols = w_ada.shape[2]

    b_sh = lax.dynamic_slice_in_dim(b_ada, chip * ada_cols, ada_cols, axis=1)
    wdw_pad = jnp.pad(w_dw[0], ((0, HALO - CONV_K), (0, 0)))
    chip1 = jnp.reshape(chip, (1,)).astype(jnp.int32)
    place = jnp.stack([ac, chip]).astype(jnp.int32)
    _, c8, modg, wdwg = _ada_exchange(c, w_ada[0], b_sh, wdw_pad)
    modr = modg.reshape(6, D_MODEL)
    wdw_all = jnp.transpose(wdwg, (1, 0, 2)).reshape(HALO, CONV_CH)
    bufs = {t: _cast_own(chip1, w[0], "cast_" + t)
            for w, t in ((w_in, "w_in"), (w_out, "w_out"), (w_up, "w_up"), (w_down, "w_down"))}
    vec = (b_dw, gn_gain, gn_bias, g_hgrn_out)

    p, h1, w_in_g, w_out_g = _mix_in_fwd(chip1, xs, modr, g_pre_mix, b_in.reshape(N_CHIPS, 1, IN_COLS // N_CHIPS),
                                          bufs["w_in"], bufs["w_out"])
    w_out_f = w_out_g.reshape(D_MODEL, D_MODEL)
    (cat, ys, o, states, y, x1, h2), (w_up_g, w_down_g) = _mixers_fwd(
        p, wdw_all, vec, lb_logits, w_out_f, xs, modr, g_post_mix, g_pre_ffn, [bufs["w_up"], bufs["w_down"]])
    w_down_f = w_down_g.reshape(D_FF, D_MODEL)
    r, dy2, dx2, st_loss = _ffn_fwd(h2, w_up_g, w_down_f, x1, tgt, modr, g_post_ffn)

    def pair_sums(grads, got, tags):
        return [_pair_sum(place, g, o_, "pair_sum_" + t) for (g, _), o_, t in zip(grads, got, tags)]

    da, dx1, st_ffn = _ffn_bwd(dy2, r, x1, dx2, w_up_g, w_down_f, modr, g_pre_ffn)
    g_up = _weight_grad(h2, da, False, True, "grad_w_up")
    g_down = _weight_grad(r, dy2, True, False, "grad_w_down")
    (dcat, st_out, g_out), got_ud = _mix_out_bwd(dx1, y, cat, w_out_f, modr, g_post_mix, [g_up[1], g_down[1]])
    g_out = g_out.reshape(N_CHIPS, D_MODEL // N_CHIPS, D_MODEL)
    got_o = _pair_swap([g_out], "pair_swap_w_out")
    early = pair_sums([(g_out, None), g_up, g_down], list(got_o) + list(got_ud), ["w_out", "w_up", "w_down"])
    (dp, st_bin, st_512, dwdw), got_early, g_in = _mixers_bwd(p, dcat, ys, o, states, h1, wdw_all, vec, lb_logits,
                                                              [pb for _, pb in early])
    late = pair_sums([(g_in, None)], _pair_swap([g_in], "pair_swap_w_in"), ["w_in"])
    (grad_x, st_in), got_late = _mix_in_bwd(dp, w_in_g, xs, dx1, modr, g_pre_mix, [late[0][1]])
    fulls = [_chip_sum(place, pf, gb, "chip_sum_" + t)
             for (pf, _), gb, t in zip(late + early, list(got_late) + list(got_early), ["w_in", "w_out", "w_up", "w_down"])]

    pad_lanes = lambda s: jnp.pad(s, ((0, 0), (0, D_MODEL - s.shape[1])))
    stats = jnp.concatenate([st_loss, st_ffn, st_out, st_in, st_bin, pad_lanes(st_512), pad_lanes(dwdw)], axis=0)
    (g_w_in, g_w_out, g_w_up, g_w_down), gath = _final_exchange(fulls, stats)
    small = {"b_ada": (b_ada, m_b_ada, v_b_ada), "lb_logits": (lb_logits, m_lb_logits, v_lb_logits),
             "g_pre_mix": (g_pre_mix, m_g_pre_mix, v_g_pre_mix), "b_in": (b_in, m_b_in, v_b_in),
             "b_dw": (b_dw, m_b_dw, v_b_dw), "gn_gain": (gn_gain, m_gn_gain, v_gn_gain),
             "gn_bias": (gn_bias, m_gn_bias, v_gn_bias), "g_hgrn_out": (g_hgrn_out, m_g_hgrn_out, v_g_hgrn_out),
             "g_post_mix": (g_post_mix, m_g_post_mix, v_g_post_mix), "g_pre_ffn": (g_pre_ffn, m_g_pre_ffn, v_g_pre_ffn),
             "g_post_ffn": (g_post_ffn, m_g_post_ffn, v_g_post_ffn)}
    loss_t, dmod_all, dwdw_sum, sres = _small_update(gath, small)
    loss = loss_t[0, 0]

    res = dict(sres)
    dmod_sh = lax.dynamic_slice_in_dim(dmod_all, chip * ada_cols, ada_cols, axis=1)
    res["w_ada"] = [t[None] for t in _ada_grad_adam(jnp.transpose(c8), dmod_sh, w_ada[0], m_w_ada[0], v_w_ada[0])]
    g_wdw = lax.dynamic_slice_in_dim(dwdw_sum, chip * HEAD_D, HEAD_D, axis=1)[:CONV_K][None]
    res["w_dw"] = [g_wdw] + list(_wdw_adam(w_dw, g_wdw, m_w_dw, v_w_dw))
    for name, g, w, m, v in (("w_in", g_w_in, w_in, m_w_in, v_w_in), ("w_out", g_w_out, w_out, m_w_out, v_w_out),
                             ("w_up", g_w_up, w_up, m_w_up, v_w_up), ("w_down", g_w_down, w_down, m_w_down, v_w_down)):
        d, m2, v2 = _adam_big(w[0], g, m[0], v[0], "adam_" + name)
        res[name] = [g[None], d[None], m2[None], v2[None]]

    order = ["w_ada", "b_ada", "lb_logits", "g_pre_mix", "w_in", "b_in", "w_dw", "b_dw", "gn_gain", "gn_bias",
             "g_hgrn_out", "w_out", "g_post_mix", "g_pre_ffn", "w_up", "w_down", "g_post_ffn"]
    out = [loss, grad_x[None]]
    for k in range(4):
        out += [res[n][k] for n in order]
    return tuple(out)
```

```python
import jax
import jax.numpy as jnp
from jax import lax
from jax.experimental import pallas as pl
from jax.experimental.pallas import tpu as pltpu

F32, BF16 = jnp.float32, jnp.bfloat16
D_MODEL = 1024
CONV_CH = 512
HGRN_W = 512
N_HEADS = 4
HEAD_D = 128
CONV_K = 31
GN_GROUP = 64
GN_SHIFT = 6
IN_COLS = 3072
D_FF = 4096
CHUNK = 64
CHUNK_SHIFT = 6
N_CHIPS = 4
N_DEV = 8
RMS_EPS = 1e-6
GN_EPS = 1e-5
ADAM_LR, ADAM_B1, ADAM_B2, ADAM_EPS, ADAM_WD, ADAM_STEP = 0.001, 0.9, 0.999, 1e-08, 0.01, 10
TOK_TILE = 512
MIXIN_TILE = 1024
MIXB_TILE = 256
FFN_TILE = 1024
FFN_BLOCK = 512
GRAD_TILE = 2048
HALO = 32
SUB = 8
LANE = 128
CONV_ROWS = 128
CHUNK_UNROLL = 8
MIB = 1 << 20
MESH = pl.DeviceIdType.MESH
OTHER_CHIPS = ((0, 1), (1, 0), (1, 1))


def _cp(sem=None, vmem_mib=48):
    return pltpu.CompilerParams(dimension_semantics=sem, vmem_limit_bytes=vmem_mib * MIB)


def _dot(a, b):
    return jnp.dot(a, b, preferred_element_type=F32)


def _dot_nt(a, b):
    return lax.dot_general(a, b, (((1,), (1,)), ((), ())), preferred_element_type=F32)


def _dot_tn(a, b):
    return lax.dot_general(a, b, (((0,), (0,)), ((), ())), preferred_element_type=F32)


def _sig(v):
    return 0.5 * jnp.tanh(0.5 * v) + 0.5


def _colsum(v):
    return jnp.sum(v, axis=0, keepdims=True)


def _flip(v, b):
    return 1 - v if b else v


def _rcopy(src, dst, ssem, rsem, dev):
    return pltpu.make_async_remote_copy(src_ref=src, dst_ref=dst, send_sem=ssem, recv_sem=rsem,
                                        device_id=dev, device_id_type=MESH)


def _place():
    return lax.axis_index("x"), lax.axis_index("y"), lax.axis_index("c")


def _full(shape):
    return pl.BlockSpec(shape, lambda *_: (0,) * len(shape))


def _big(shape, dtype):
    return pltpu.HBM(shape, dtype)


def _hbm(*arrays):
    out = [pltpu.with_memory_space_constraint(a, pltpu.HBM) for a in arrays]
    return out[0] if len(out) == 1 else out


def _split2(v):
    hi = v.astype(BF16)
    lo = (v - hi.astype(F32)).astype(BF16)
    return hi, lo


def _split3(v):
    h1 = v.astype(BF16)
    r1 = v - h1.astype(F32)
    h2 = r1.astype(BF16)
    h3 = (r1 - h2.astype(F32)).astype(BF16)
    return h1, h2, h3


def _mm3s(mat, parts):
    h1, h2, h3 = parts
    return _dot(mat, h1) + _dot(mat, h2) + _dot(mat, h3)


def _mm3(mat, v):
    return _mm3s(mat, _split3(v))


def _gn_matrix():
    r = lax.broadcasted_iota(jnp.int32, (CONV_CH, CONV_CH), 0) >> GN_SHIFT
    c = lax.broadcasted_iota(jnp.int32, (CONV_CH, CONV_CH), 1) >> GN_SHIFT
    return jnp.where(r == c, 1.0 / GN_GROUP, 0.0).astype(BF16)


def _gmean(v, gmat):
    hi, lo = _split2(v)
    return _dot(hi, gmat) + _dot(lo, gmat)


def _chunk_masks(tm):
    r = lax.broadcasted_iota(jnp.int32, (tm, tm), 0)
    c = lax.broadcasted_iota(jnp.int32, (tm, tm), 1)
    same = (r >> CHUNK_SHIFT) == (c >> CHUNK_SHIFT)
    one = lambda m: jnp.where(m, 1.0, 0.0).astype(BF16)
    return one(same & (c <= r)), one(same & (c >= r)), one(same)


def _tri():
    return lax.broadcasted_iota(jnp.int32, (CHUNK, CHUNK), 0) >= lax.broadcasted_iota(jnp.int32, (CHUNK, CHUNK), 1)


def _lower_bound(lbl_ref):
    l0, l1 = lbl_ref[0:1, :], lbl_ref[1:2, :]
    mx = jnp.maximum(l0, l1)
    e0, e1 = jnp.exp(l0 - mx), jnp.exp(l1 - mx)
    return e0 / (e0 + e1), e1 / (e0 + e1)


CONV_FWD_TAPS = tuple((j, HALO - (CONV_K - 1) + j) for j in range(CONV_K))
CONV_BWD_TAPS = tuple((j, (CONV_K - 1) - j) for j in range(CONV_K))


def _tap_conv(src_ref, w_ref, row0, taps, lanes):
    acc = None
    for b in range(SUB):
        pb = None
        for j, off in taps:
            if off % SUB == b:
                lo = row0 + off - b
                term = w_ref[j:j + 1, lanes] * src_ref[lo:lo + CONV_ROWS + SUB, lanes]
                pb = term if pb is None else pb + term
        if pb is not None:
            sh = pb[b:b + CONV_ROWS, :]
            acc = sh if acc is None else acc + sh
    return acc


def _hgrn_prep(pq, pf, lb, lower, same):
    sq = _sig(pq)
    qf = pq * sq
    sf = _sig(pf)
    f = lb + (1.0 - lb) * sf
    logf = jnp.log(f)
    k = 1.0 - f
    parts = _split3(logf)
    G = _mm3s(lower, parts)
    Gl = _mm3s(same, parts)
    eG, enG, eGlG = jnp.exp(G), jnp.exp(-G), jnp.exp(Gl - G)
    return dict(sq=sq, sf=sf, f=f, Gl=Gl, eG=eG, enG=enG, eGlG=eGlG, qt=qf * eG, kt=k * enG, kh=k * eGlG)


def _ada_exchange(c_row, w_ada, b_sh, wdw_pad):
    ncol = w_ada.shape[1]

    def body(c_ref, w_ref, b_ref, wdw_ref, call_ref, c8_ref, modg_ref, wdwg_ref, rows_s, sa, ra, sw, rw, sm, rm):
        x, y, c = _place()
        me = 4 * x + 2 * y + c
        chip = 2 * x + y
        cv = c_ref[...]
        call_ref[me] = cv * _sig(cv)
        wdwg_ref[chip] = wdw_ref[...]
        sends = []
        for m in range(1, N_DEV):
            peer = (_flip(x, m >> 2), _flip(y, (m >> 1) & 1), _flip(c, m & 1))
            cp = _rcopy(call_ref.at[me], call_ref.at[me], sa.at[m - 1], ra.at[m - 1], peer)
            cp.start()
            sends.append(cp)
        for k, (fx, fy) in enumerate(OTHER_CHIPS):
            peer = (_flip(x, fx), _flip(y, fy), c)
            cp = _rcopy(wdwg_ref.at[chip], wdwg_ref.at[chip], sw.at[k], rw.at[k], peer)
            cp.start()
            sends.append(cp)
        for m in range(1, N_DEV):
            peer = (_flip(x, m >> 2), _flip(y, (m >> 1) & 1), _flip(c, m & 1))
            pid = 4 * peer[0] + 2 * peer[1] + peer[2]
            _rcopy(call_ref.at[pid], call_ref.at[pid], sa.at[m - 1], ra.at[m - 1], peer).wait_recv()
        for b in range(N_DEV):
            c8_ref[b:b + 1, :] = call_ref[b]
        mod_all = _dot(c8_ref[...].astype(BF16), w_ref[...].astype(BF16)) + b_ref[...]
        for b in range(N_DEV):
            rows_s[b] = mod_all[b:b + 1, :]
        modg_ref[chip] = rows_s[me]
        for k, (fx, fy) in enumerate(OTHER_CHIPS):
            peer = (_flip(x, fx), _flip(y, fy), c)
            pid = 4 * peer[0] + 2 * peer[1] + peer[2]
            cp = _rcopy(rows_s.at[pid], modg_ref.at[chip], sm.at[k], rm.at[k], peer)
            cp.start()
            sends.append(cp)
        for k, (fx, fy) in enumerate(OTHER_CHIPS):
            peer = (_flip(x, fx), _flip(y, fy), c)
            pchip = 2 * peer[0] + peer[1]
            _rcopy(rows_s.at[0], modg_ref.at[pchip], sm.at[k], rm.at[k], peer).wait_recv()
            _rcopy(wdwg_ref.at[pchip], wdwg_ref.at[pchip], sw.at[k], rw.at[k], peer).wait_recv()
        for cp in sends:
            cp.wait_send()

    vm = pl.BlockSpec(memory_space=pltpu.VMEM)
    return pl.pallas_call(
        body, name="ada_exchange",
        out_shape=[jax.ShapeDtypeStruct((N_DEV, 1, D_MODEL), F32), jax.ShapeDtypeStruct((N_DEV, D_MODEL), F32),
                   jax.ShapeDtypeStruct((N_CHIPS, 1, ncol), F32), jax.ShapeDtypeStruct((N_CHIPS, HALO, HEAD_D), F32)],
        in_specs=[vm] * 4, out_specs=[vm] * 4,
        scratch_shapes=[pltpu.VMEM((N_DEV, 1, ncol), F32),
                        pltpu.SemaphoreType.DMA((N_DEV - 1,)), pltpu.SemaphoreType.DMA((N_DEV - 1,)),
                        pltpu.SemaphoreType.DMA((3,)), pltpu.SemaphoreType.DMA((3,)),
                        pltpu.SemaphoreType.DMA((3,)), pltpu.SemaphoreType.DMA((3,))],
        compiler_params=_cp(None, 32),
    )(c_row, w_ada, b_sh, wdw_pad)


def _cast_own(chip1, shard, name):
    rows, cols = shard.shape
    tr = _row_tile(rows)

    def body(ch_ref, s_ref, o_ref):
        o_ref[0] = s_ref[...].astype(BF16)

    gs = pltpu.PrefetchScalarGridSpec(
        num_scalar_prefetch=1, grid=(rows // tr,),
        in_specs=[pl.BlockSpec((tr, cols), lambda i, ch: (i, 0))],
        out_specs=pl.BlockSpec((1, tr, cols), lambda i, ch: (ch[0], i, 0)))
    return pl.pallas_call(
        body, name=name, grid_spec=gs, out_shape=_big((N_CHIPS, rows, cols), BF16),
        compiler_params=_cp(("arbitrary",), 32),
    )(chip1, _hbm(shard))


def _slab(buf, ch, core):
    hs = buf.shape[1] // 2
    return buf.at[ch, pl.ds(core * hs, hs), :]


def _gather_start(bufs, ssem, rsem, relations=(0, 1, 2)):
    x, y, c = _place()
    chip = 2 * x + y
    for k in relations:
        fx, fy = OTHER_CHIPS[k]
        peer = (_flip(x, fx), _flip(y, fy), c)
        for t, buf in enumerate(bufs):
            _rcopy(_slab(buf, chip, c), _slab(buf, chip, c), ssem.at[t * 3 + k], rsem.at[t * 3 + k], peer).start()


def _gather_pass_on(bufs, ssem, rsem):
    nt = len(bufs)
    x, y, c = _place()
    sibling = (x, y, 1 - c)
    for k, (fx, fy) in enumerate(OTHER_CHIPS):
        peer = (_flip(x, fx), _flip(y, fy), c)
        pchip = 2 * peer[0] + peer[1]
        for t, buf in enumerate(bufs):
            _rcopy(_slab(buf, pchip, c), _slab(buf, pchip, c), ssem.at[t * 3 + k], rsem.at[t * 3 + k], peer).wait_recv()
            _rcopy(_slab(buf, pchip, c), _slab(buf, pchip, c), ssem.at[3 * nt + t * 3 + k],
                   rsem.at[3 * nt + t * 3 + k], sibling).start()


def _gather_drain(bufs, ssem, rsem):
    nt = len(bufs)
    x, y, c = _place()
    chip = 2 * x + y
    sibling = (x, y, 1 - c)
    for k, (fx, fy) in enumerate(OTHER_CHIPS):
        peer = (_flip(x, fx), _flip(y, fy), c)
        pchip = 2 * peer[0] + peer[1]
        for t, buf in enumerate(bufs):
            _rcopy(_slab(buf, pchip, 1 - c), _slab(buf, pchip, 1 - c), ssem.at[3 * nt + t * 3 + k],
                   rsem.at[3 * nt + t * 3 + k], sibling).wait_recv()
            _rcopy(_slab(buf, chip, c), _slab(buf, chip, c), ssem.at[t * 3 + k], rsem.at[t * 3 + k], peer).wait_send()
            _rcopy(_slab(buf, pchip, c), _slab(buf, pchip, c), ssem.at[3 * nt + t * 3 + k],
                   rsem.at[3 * nt + t * 3 + k], sibling).wait_send()


def _gather_finish(bufs, ssem, rsem):
    _gather_pass_on(bufs, ssem, rsem)
    _gather_drain(bufs, ssem, rsem)


def _gather_arrive(bufs, k, ssem, rsem):
    nt = len(bufs)
    x, y, c = _place()
    fx, fy = OTHER_CHIPS[k]
    peer = (_flip(x, fx), _flip(y, fy), c)
    pchip = 2 * peer[0] + peer[1]
    for t, buf in enumerate(bufs):
        _rcopy(_slab(buf, pchip, c), _slab(buf, pchip, c), ssem.at[t * 3 + k], rsem.at[t * 3 + k], peer).wait_recv()
        _rcopy(_slab(buf, pchip, c), _slab(buf, pchip, c), ssem.at[3 * nt + t * 3 + k],
               rsem.at[3 * nt + t * 3 + k], (x, y, 1 - c)).start()
    for t, buf in enumerate(bufs):
        _rcopy(_slab(buf, pchip, 1 - c), _slab(buf, pchip, 1 - c), ssem.at[3 * nt + t * 3 + k],
               rsem.at[3 * nt + t * 3 + k], (x, y, 1 - c)).wait_recv()


def _gather_sends_done(bufs, ssem, rsem):
    nt = len(bufs)
    x, y, c = _place()
    chip = 2 * x + y
    for k, (fx, fy) in enumerate(OTHER_CHIPS):
        peer = (_flip(x, fx), _flip(y, fy), c)
        pchip = 2 * peer[0] + peer[1]
        for t, buf in enumerate(bufs):
            _rcopy(_slab(buf, chip, c), _slab(buf, chip, c), ssem.at[t * 3 + k], rsem.at[t * 3 + k], peer).wait_send()
            _rcopy(_slab(buf, pchip, c), _slab(buf, pchip, c), ssem.at[3 * nt + t * 3 + k],
                   rsem.at[3 * nt + t * 3 + k], (x, y, 1 - c)).wait_send()


def _ring_parts(buf):
    x, y, c = _place()
    ynb, xnb = (x, 1 - y, c), (1 - x, y, c)
    ychip, xchip, dchip = 2 * x + (1 - y), 2 * (1 - x) + y, 2 * (1 - x) + (1 - y)
    hs = buf.shape[1] // 2

    def piece(ch, q):
        return buf.at[ch, pl.ds(c * hs + q * (hs // 2), hs // 2), :]

    return ynb, xnb, ychip, xchip, dchip, piece


def _ring_start(bufs, ssem, rsem):
    x, y, c = _place()
    chip = 2 * x + y
    for t, buf in enumerate(bufs):
        ynb, xnb, _, _, _, _ = _ring_parts(buf)
        _rcopy(_slab(buf, chip, c), _slab(buf, chip, c), ssem.at[2 * t], rsem.at[2 * t], ynb).start()
        _rcopy(_slab(buf, chip, c), _slab(buf, chip, c), ssem.at[2 * t + 1], rsem.at[2 * t + 1], xnb).start()


def _ring_forward(bufs, ssem, rsem):
    nt = len(bufs)
    _, _, c = _place()
    for t, buf in enumerate(bufs):
        ynb, xnb, ychip, xchip, _, piece = _ring_parts(buf)
        _rcopy(_slab(buf, ychip, c), _slab(buf, ychip, c), ssem.at[2 * t], rsem.at[2 * t], ynb).wait_recv()
        _rcopy(piece(ychip, 0), piece(ychip, 0), ssem.at[2 * nt + 2 * t], rsem.at[2 * nt + 2 * t], xnb).start()
        _rcopy(_slab(buf, xchip, c), _slab(buf, xchip, c), ssem.at[2 * t + 1], rsem.at[2 * t + 1], xnb).wait_recv()
        _rcopy(piece(xchip, 1), piece(xchip, 1), ssem.at[2 * nt + 2 * t + 1], rsem.at[2 * nt + 2 * t + 1], ynb).start()


def _ring_finish(bufs, ssem, rsem):
    nt = len(bufs)
    x, y, c = _place()
    chip = 2 * x + y
    sibling = (x, y, 1 - c)
    for t, buf in enumerate(bufs):
        ynb, xnb, ychip, xchip, dchip, piece = _ring_parts(buf)
        _rcopy(piece(dchip, 0), piece(dchip, 0), ssem.at[2 * nt + 2 * t], rsem.at[2 * nt + 2 * t], xnb).wait_recv()
        _rcopy(piece(dchip, 1), piece(dchip, 1), ssem.at[2 * nt + 2 * t + 1], rsem.at[2 * nt + 2 * t + 1],
               ynb).wait_recv()
        for k, ch in enumerate((ychip, xchip, dchip)):
            _rcopy(_slab(buf, ch, c), _slab(buf, ch, c), ssem.at[4 * nt + 3 * t + k], rsem.at[4 * nt + 3 * t + k],
                   sibling).start()
    for t, buf in enumerate(bufs):
        ynb, xnb, ychip, xchip, dchip, piece = _ring_parts(buf)
        for k, ch in enumerate((ychip, xchip, dchip)):
            _rcopy(_slab(buf, ch, 1 - c), _slab(buf, ch, 1 - c), ssem.at[4 * nt + 3 * t + k],
                   rsem.at[4 * nt + 3 * t + k], sibling).wait_recv()
            _rcopy(_slab(buf, ch, c), _slab(buf, ch, c), ssem.at[4 * nt + 3 * t + k], rsem.at[4 * nt + 3 * t + k],
                   sibling).wait_send()
        _rcopy(_slab(buf, chip, c), _slab(buf, chip, c), ssem.at[2 * t], rsem.at[2 * t], ynb).wait_send()
        _rcopy(_slab(buf, chip, c), _slab(buf, chip, c), ssem.at[2 * t + 1], rsem.at[2 * t + 1], xnb).wait_send()
        _rcopy(piece(ychip, 0), piece(ychip, 0), ssem.at[2 * nt + 2 * t], rsem.at[2 * nt + 2 * t], xnb).wait_send()
        _rcopy(piece(xchip, 1), piece(xchip, 1), ssem.at[2 * nt + 2 * t + 1], rsem.at[2 * nt + 2 * t + 1],
               ynb).wait_send()


def _ring_sems(nt):
    return [pltpu.SemaphoreType.DMA((7 * nt,)), pltpu.SemaphoreType.DMA((7 * nt,))]


def _gather_sems(nt):
    return [pltpu.SemaphoreType.DMA((6 * nt,)), pltpu.SemaphoreType.DMA((6 * nt,))]


def _pair_copies(ins, outs, ssem, rsem):
    x, y, c = _place()
    copies = []
    for t in range(len(ins)):
        hs = ins[t].shape[1] // 2
        copies.append(_rcopy(ins[t].at[:, pl.ds((1 - c) * hs, hs), :], outs[t], ssem.at[t], rsem.at[t], (x, y, 1 - c)))
    return copies


def _pair_shapes(grads):
    return [_big((g.shape[0], g.shape[1] // 2, g.shape[2]), g.dtype) for g in grads]


def _pair_sems(nt):
    return [pltpu.SemaphoreType.DMA((nt,)), pltpu.SemaphoreType.DMA((nt,))]


def _pair_swap(grads, name):
    nt = len(grads)
    hbm = pl.BlockSpec(memory_space=pl.ANY)

    def body(*refs):
        copies = _pair_copies(refs[:nt], refs[nt:2 * nt], refs[2 * nt], refs[2 * nt + 1])
        for cp in copies:
            cp.start()
        for cp in copies:
            cp.wait_recv()
        for cp in copies:
            cp.wait_send()

    return pl.pallas_call(
        body, name=name, out_shape=_pair_shapes(grads), in_specs=[hbm] * nt, out_specs=[hbm] * nt,
        scratch_shapes=_pair_sems(nt),
    )(*[_hbm(g) for g in grads])


def _xchg_copies(ins, outs, ssem, rsem):
    x, y, c = _place()
    copies = []
    for k, (fx, fy) in enumerate(OTHER_CHIPS):
        peer = (_flip(x, fx), _flip(y, fy), c)
        for t in range(len(ins)):
            copies.append(_rcopy(ins[t].at[k], outs[t].at[k], ssem.at[t * 3 + k], rsem.at[t * 3 + k], peer))
    return copies


def _xchg_sems(nt):
    return [pltpu.SemaphoreType.DMA((3 * nt,)), pltpu.SemaphoreType.DMA((3 * nt,))]


def _final_exchange(fulls, stats):
    nt = len(fulls)
    rows, cols = stats.shape
    hbm = pl.BlockSpec(memory_space=pl.ANY)
    vm = pl.BlockSpec(memory_space=pltpu.VMEM)

    def body(*refs):
        ins, s_ref = refs[:nt], refs[nt]
        outs, g_ref = refs[nt + 1:2 * nt + 1], refs[2 * nt + 1]
        hssem, hrsem, ssem, rsem = refs[2 * nt + 2:]
        x, y, c = _place()
        me, sibling = (x, y, c), (x, y, 1 - c)
        halves = []
        for t in range(nt):
            hs = ins[t].shape[0] // 2
            mine = pl.ds(c * hs, hs)
            cp = _rcopy(ins[t].at[mine, :], outs[t].at[mine, :], hssem.at[t], hrsem.at[t], sibling)
            cp.start()
            halves.append(cp)

        chips = [(_flip(x, fx), _flip(y, fy)) for fx, fy in OTHER_CHIPS]

        def blk(px, py, pc):
            return g_ref.at[4 * px + 2 * py + pc]

        def copy(k, block, to, src=None):
            return _rcopy(blk(*block) if src is None else src, blk(*block), ssem.at[k], rsem.at[k], to)

        g_ref[4 * x + 2 * y + c] = s_ref[...]
        first = [copy(0, me, sibling, src=s_ref)]
        first += [copy(1 + j, me, (*chip, c), src=s_ref) for j, chip in enumerate(chips)]
        for cp in first:
            cp.start()
        passed = [copy(4 + j, (*chip, c), sibling) for j, chip in enumerate(chips)]
        for j, chip in enumerate(chips):
            copy(1 + j, (*chip, c), me).wait_recv()
            passed[j].start()
        copy(0, sibling, me).wait_recv()
        for j, chip in enumerate(chips):
            copy(4 + j, (*chip, 1 - c), me).wait_recv()
        for t in range(nt):
            hs = ins[t].shape[0] // 2
            other = pl.ds((1 - c) * hs, hs)
            _rcopy(ins[t].at[other, :], outs[t].at[other, :], hssem.at[t], hrsem.at[t], sibling).wait_recv()
        for cp in first + passed + halves:
            cp.wait_send()

    outs = pl.pallas_call(
        body, name="final_exchange",
        out_shape=[_big(f.shape, F32) for f in fulls] + [jax.ShapeDtypeStruct((N_DEV, rows, cols), F32)],
        in_specs=[hbm] * nt + [vm], out_specs=[hbm] * nt + [vm],
        input_output_aliases={t: t for t in range(nt)},
        scratch_shapes=[pltpu.SemaphoreType.DMA((nt,)), pltpu.SemaphoreType.DMA((nt,)),
                        pltpu.SemaphoreType.DMA((7,)), pltpu.SemaphoreType.DMA((7,))],
        compiler_params=_cp(None, 32),
    )(*[_hbm(f) for f in fulls], stats)
    return outs[:nt], outs[nt]


def _row_tile(rows):
    return min(rows, 512)


def _pair_sum(place, grad, got, name):
    nb, hs, cols = got.shape
    tr = _row_tile(hs)
    nr = hs // tr

    def body(pl_ref, g_ref, o_ref, pf_ref, pb_ref):
        j = pl.program_id(1)
        s = g_ref[0] + o_ref[0].astype(F32)

        @pl.when(j == 0)
        def _():
            pf_ref[...] = s

        @pl.when(j > 0)
        def _():
            pb_ref[0] = s.astype(BF16)

    gs = pltpu.PrefetchScalarGridSpec(
        num_scalar_prefetch=1, grid=(nr, nb),
        in_specs=[pl.BlockSpec((1, tr, cols), lambda i, j, p: (p[1] ^ j, p[0] * nr + i, 0)),
                  pl.BlockSpec((1, tr, cols), lambda i, j, p: (p[1] ^ j, i, 0))],
        out_specs=[pl.BlockSpec((tr, cols), lambda i, j, p: (i, 0)),
                   pl.BlockSpec((1, tr, cols), lambda i, j, p: (jnp.maximum(j - 1, 0), i, 0))])
    return pl.pallas_call(
        body, name=name, grid_spec=gs,
        out_shape=[_big((hs, cols), F32), _big((nb - 1, hs, cols), BF16)],
        compiler_params=_cp(("arbitrary", "arbitrary"), 32),
    )(place, *_hbm(grad, got))


def _chip_sum(place, pair_f, got_b, name):
    nb, hs, cols = got_b.shape
    tr = _row_tile(hs)
    nr = hs // tr

    def body(pl_ref, pf_ref, gb_ref, o_ref):
        acc = pf_ref[...]
        for k in range(nb):
            acc = acc + gb_ref[k].astype(F32)
        o_ref[...] = acc

    gs = pltpu.PrefetchScalarGridSpec(
        num_scalar_prefetch=1, grid=(nr,),
        in_specs=[pl.BlockSpec((tr, cols), lambda i, p: (i, 0)),
                  pl.BlockSpec((nb, tr, cols), lambda i, p: (0, i, 0))],
        out_specs=pl.BlockSpec((tr, cols), lambda i, p: (p[0] * nr + i, 0)))
    return pl.pallas_call(
        body, name=name, grid_spec=gs,
        out_shape=_big((2 * hs, cols), F32),
        compiler_params=_cp(("arbitrary",), 32),
    )(place, *_hbm(pair_f, got_b))


def _adam_math(w, g, m, v):
    m2 = ADAM_B1 * m + (1.0 - ADAM_B1) * g
    v2 = ADAM_B2 * v + (1.0 - ADAM_B2) * (g * g)
    m_hat = m2 / (1.0 - ADAM_B1 ** ADAM_STEP)
    v_hat = v2 / (1.0 - ADAM_B2 ** ADAM_STEP)
    delta = -ADAM_LR * (m_hat / (jnp.sqrt(v_hat) + ADAM_EPS) + ADAM_WD * w)
    return delta, m2, v2


def _adam_big(w, g, m, v, name):
    rows, cols = w.shape
    tr = _row_tile(rows)

    def body(w_ref, g_ref, m_ref, v_ref, d_out, m_out, v_out):
        d, m2, v2 = _adam_math(w_ref[...], g_ref[...], m_ref[...], v_ref[...])
        d_out[...] = d
        m_out[...] = m2
        v_out[...] = v2

    spec = pl.BlockSpec((tr, cols), lambda i: (i, 0))
    return pl.pallas_call(
        body, name=name, grid=(rows // tr,), in_specs=[spec] * 4, out_specs=[spec] * 3,
        out_shape=[_big(w.shape, F32)] * 3,
        compiler_params=_cp(("arbitrary",), 48),
    )(*_hbm(w, g, m, v))


def _ada_grad_adam(c8t, dmod_sh, w, m, v):
    rows, cols = w.shape
    tr = _row_tile(rows) // 2

    def body(ct_ref, dm_ref, w_ref, m_ref, v_ref, g_out, d_out, m_out, v_out):
        g = None
        for b in range(N_DEV):
            term = ct_ref[:, b:b + 1] * dm_ref[b:b + 1, :]
            g = term if g is None else g + term
        d, m2, v2 = _adam_math(w_ref[...], g, m_ref[...], v_ref[...])
        g_out[...] = g
        d_out[...] = d
        m_out[...] = m2
        v_out[...] = v2

    spec = pl.BlockSpec((tr, cols), lambda i: (i, 0))
    return pl.pallas_call(
        body, name="ada_grad_adam", grid=(rows // tr,),
        in_specs=[pl.BlockSpec((tr, N_DEV), lambda i: (i, 0)), _full((N_DEV, cols)), spec, spec, spec],
        out_specs=[spec] * 4, out_shape=[_big(w.shape, F32)] * 4,
        compiler_params=_cp(("arbitrary",), 32),
    )(c8t, dmod_sh, *_hbm(w, m, v))


def _tok_tile(t):
    return min(TOK_TILE, t)


def _mix_in_fwd(chip1, x, modr, g_pre, b_in4, w_in_buf, w_out_buf):
    T = x.shape[0]
    tm = min(MIXIN_TILE, T)
    nt = T // tm
    nb = IN_COLS // N_CHIPS

    def body(ch_ref, x_ref, mod_ref, g_ref, b_ref, win_in, wout_in, p_ref, h_ref, win_ref, wout_ref,
             h_all, wblk, lsem, is_sem, ir_sem, os_sem, or_sem):
        k, i = pl.program_id(0), pl.program_id(1)
        chip = ch_ref[0]

        def load_block(blk):
            cp = pltpu.make_async_copy(win_ref.at[blk], wblk, lsem)
            cp.start()
            cp.wait()

        @pl.when((k == 0) & (i == 0))
        def _():
            _gather_start([win_ref], is_sem, ir_sem, relations=(0, 1))
            load_block(chip)

        for r in range(N_CHIPS - 1):
            @pl.when((k == r + 1) & (i == 0))
            def _(r=r):
                _gather_arrive([win_ref], r, is_sem, ir_sem)
                if r == 0:
                    _gather_start([win_ref], is_sem, ir_sem, relations=(2,))
                if r == 1:
                    _gather_start([wout_ref], os_sem, or_sem)
                load_block(chip ^ (r + 1))

        rows = pl.ds(pl.multiple_of(i * tm, tm), tm)

        @pl.when(k == 0)
        def _():
            xv = x_ref[...]
            rstd = lax.rsqrt(jnp.mean(xv * xv, axis=-1, keepdims=True) + RMS_EPS)
            h = (xv * rstd) * g_ref[...] * (1.0 + mod_ref[1:2, :]) + mod_ref[0:1, :]
            hb = h.astype(BF16)
            h_ref[...] = hb
            h_all[rows, :] = hb

        p_ref[...] = _dot(h_all[rows, :], wblk[...]) + b_ref[chip ^ k]

        @pl.when((k == N_CHIPS - 1) & (i == nt - 1))
        def _():
            _gather_sends_done([win_ref], is_sem, ir_sem)
            _gather_finish([wout_ref], os_sem, or_sem)

    hbm = pl.BlockSpec(memory_space=pl.ANY)
    first_pass = lambda k, i, ch: (jnp.where(k == 0, i, nt - 1), 0)
    gs = pltpu.PrefetchScalarGridSpec(
        num_scalar_prefetch=1, grid=(N_CHIPS, nt),
        in_specs=[pl.BlockSpec((tm, D_MODEL), first_pass), pl.BlockSpec((6, D_MODEL), lambda k, i, ch: (0, 0)),
                  pl.BlockSpec((1, D_MODEL), lambda k, i, ch: (0, 0)),
                  pl.BlockSpec((N_CHIPS, 1, nb), lambda k, i, ch: (0, 0, 0)), hbm, hbm],
        out_specs=[pl.BlockSpec((tm, nb), lambda k, i, ch: (i, ch[0] ^ k)), pl.BlockSpec((tm, D_MODEL), first_pass),
                   hbm, hbm],
        scratch_shapes=[pltpu.VMEM((T, D_MODEL), BF16), pltpu.VMEM((D_MODEL, nb), BF16), pltpu.SemaphoreType.DMA]
        + _gather_sems(1) + _gather_sems(1))
    return pl.pallas_call(
        body, name="mix_in_fwd", grid_spec=gs,
        out_shape=[_big((T, IN_COLS), F32), _big((T, D_MODEL), BF16), _big(w_in_buf.shape, BF16),
                   _big(w_out_buf.shape, BF16)],
        input_output_aliases={5: 2, 6: 3},
        compiler_params=_cp(("arbitrary", "arbitrary"), 48),
    )(chip1, _hbm(x), modr, g_pre, b_in4, _hbm(w_in_buf), _hbm(w_out_buf))


def _mixers_fwd(p, wdw, vecs, lbl, w_out, x, modr, g_post, g_ffn, gbufs):
    T = p.shape[0]
    tm = _tok_tile(T)
    nt = T // tm
    nch = tm // CHUNK
    ng = len(gbufs)
    n_in, n_out = 12, 7

    def body(*refs):
        (p_ref, wdw_ref, bdw_ref, gain_ref, bias_ref, gout_ref, lbl_ref, wout_ref, x_ref, mod_ref, gp_ref,
         gf_ref) = refs[:n_in]
        cat_ref, ys_ref, o_ref, st_ref, y_ref, x1_ref, h2_ref = refs[n_in + ng:n_in + ng + n_out]
        gout_bufs = refs[n_in + ng + n_out:n_in + 2 * ng + n_out]
        (ubuf, state, qt_s, kt_s, kh_s, v_s, egl_s, lower_s, same_s, gmat_s, gssem,
         grsem) = refs[n_in + 2 * ng + n_out:]
        i = pl.program_id(0)

        @pl.when(i == 0)
        def _():
            _ring_start(gout_bufs, gssem, grsem)
            lower_s[...], _, same_s[...] = _chunk_masks(tm)
            gmat_s[...] = _gn_matrix()
            state[...] = jnp.zeros(state.shape, F32)
            ubuf[0:HALO, :] = jnp.zeros((HALO, CONV_CH), F32)
            ubuf[HALO + tm:HALO + tm + SUB, :] = jnp.zeros((SUB, CONV_CH), F32)

        @pl.when(i > 0)
        def _():
            ubuf[0:HALO, :] = ubuf[tm:tm + HALO, :]

        ubuf[HALO:HALO + tm, :] = p_ref[:, 0:CONV_CH] * _sig(p_ref[:, CONV_CH:2 * CONV_CH])
        for r in range(tm // CONV_ROWS):
            rows = slice(r * CONV_ROWS, (r + 1) * CONV_ROWS)
            for lb_ in range(CONV_CH // LANE):
                lanes = slice(lb_ * LANE, (lb_ + 1) * LANE)
                ys_ref[rows, lanes] = bdw_ref[:, lanes] + _tap_conv(ubuf, wdw_ref, r * CONV_ROWS, CONV_FWD_TAPS, lanes)
        gmat = gmat_s[...]
        yv = ys_ref[...]
        d = yv - _gmean(yv, gmat)
        rs = lax.rsqrt(_gmean(d * d, gmat) + GN_EPS)
        z = d * rs * gain_ref[...] + bias_ref[...]
        cat_ref[:, 0:CONV_CH] = (z * _sig(z)).astype(BF16)

        lb, _ = _lower_bound(lbl_ref)
        lower, same = lower_s[...], same_s[...]
        o0 = 2 * CONV_CH
        pr = _hgrn_prep(p_ref[:, o0:o0 + HGRN_W], p_ref[:, o0 + HGRN_W:o0 + 2 * HGRN_W], lb, lower, same)
        qt_s[...] = pr["qt"].astype(BF16)
        kt_s[...] = pr["kt"].astype(BF16)
        kh_s[...] = pr["kh"].astype(BF16)
        v_s[...] = p_ref[:, o0 + 2 * HGRN_W:o0 + 3 * HGRN_W].astype(BF16)
        egl_s[...] = jnp.exp(pr["Gl"])
        tri = _tri()

        def chunk(ci, carry):
            r0 = pl.multiple_of(ci * CHUNK, CHUNK)
            rows = pl.ds(r0, CHUNK)
            for h in range(N_HEADS):
                ls = pl.ds(h * HEAD_D, HEAD_D)
                qc, kc, hc, vc = qt_s[rows, ls], kt_s[rows, ls], kh_s[rows, ls], v_s[rows, ls]
                s0 = state[h]
                s0b = s0.astype(BF16)
                st_ref[ci, h] = s0
                att =jnp.where(tri, _dot_nt(qc, kc), 0.0).astype(BF16)
                o_ref[rows, ls] = _dot(att, vc) + _dot_nt(qc, s0b)
                state[h] = s0 * egl_s[pl.ds(r0, 1), ls] + _dot_tn(vc, hc)
            return carry

        lax.fori_loop(0, nch, chunk, 0, unroll=min(CHUNK_UNROLL, nch))
        for h in range(N_HEADS):
            sl = slice(h * HEAD_D, (h + 1) * HEAD_D)
            oh = o_ref[:, sl]
            gh = p_ref[:, o0 + 3 * HGRN_W + h * HEAD_D:o0 + 3 * HGRN_W + (h + 1) * HEAD_D]
            rsh = lax.rsqrt(jnp.mean(oh * oh, axis=-1, keepdims=True) + RMS_EPS)
            hg = (oh * rsh) * gout_ref[:, sl] * (gh * _sig(gh))
            cat_ref[:, CONV_CH + h * HEAD_D:CONV_CH + (h + 1) * HEAD_D] = hg.astype(BF16)

        yv = _dot(cat_ref[...], wout_ref[...])
        y_ref[...] = yv
        rsy = lax.rsqrt(jnp.mean(yv * yv, axis=-1, keepdims=True) + RMS_EPS)
        x1 = x_ref[...] + mod_ref[2:3, :] * ((yv * rsy) * gp_ref[...])
        x1_ref[...] = x1
        rs1 = lax.rsqrt(jnp.mean(x1 * x1, axis=-1, keepdims=True) + RMS_EPS)
        h2 = (x1 * rs1) * gf_ref[...] * (1.0 + mod_ref[4:5, :]) + mod_ref[3:4, :]
        h2_ref[...] = h2.astype(BF16)

        @pl.when(i == min(nt - 1, nt // 2 + 1))
        def _():
            _ring_forward(gout_bufs, gssem, grsem)

        @pl.when(i == nt - 1)
        def _():
            _ring_finish(gout_bufs, gssem, grsem)

    tile = lambda cols: pl.BlockSpec((tm, cols), lambda i: (i, 0))
    hbm = pl.BlockSpec(memory_space=pl.ANY)
    outs = pl.pallas_call(
        body, name="mixers_fwd", grid=(nt,),
        in_specs=[tile(IN_COLS), _full((HALO, CONV_CH))] + [_full((1, CONV_CH))] * 4 + [_full((2, HGRN_W))]
        + [_full((D_MODEL, D_MODEL)), tile(D_MODEL), _full((6, D_MODEL)), _full((1, D_MODEL)), _full((1, D_MODEL))]
        + [hbm] * ng,
        out_specs=[tile(D_MODEL), tile(CONV_CH), tile(HGRN_W),
                   pl.BlockSpec((nch, N_HEADS, HEAD_D, HEAD_D), lambda i: (i, 0, 0, 0)),
                   tile(D_MODEL), tile(D_MODEL), tile(D_MODEL)] + [hbm] * ng,
        out_shape=[_big((T, D_MODEL), BF16), _big((T, CONV_CH), F32), _big((T, HGRN_W), F32),
                   _big((T // CHUNK, N_HEADS, HEAD_D, HEAD_D), F32), _big((T, D_MODEL), F32),
                   _big((T, D_MODEL), F32), _big((T, D_MODEL), BF16)] + [_big(b.shape, BF16) for b in gbufs],
        input_output_aliases={n_in + t: n_out + t for t in range(ng)},
        scratch_shapes=[pltpu.VMEM((tm + HALO + SUB, CONV_CH), F32), pltpu.VMEM((N_HEADS, HEAD_D, HEAD_D), F32),
                        pltpu.VMEM((tm, HGRN_W), BF16), pltpu.VMEM((tm, HGRN_W), BF16),
                        pltpu.VMEM((tm, HGRN_W), BF16), pltpu.VMEM((tm, HGRN_W), BF16),
                        pltpu.VMEM((tm, HGRN_W), F32), pltpu.VMEM((tm, tm), BF16), pltpu.VMEM((tm, tm), BF16),
                        pltpu.VMEM((CONV_CH, CONV_CH), BF16)] + _ring_sems(ng),
        compiler_params=_cp(("arbitrary",), 56),
    )(_hbm(p), wdw, *vecs, lbl, *_hbm(w_out, x), modr, g_post, g_ffn, *[_hbm(b) for b in gbufs])
    return outs[:n_out], outs[n_out:]


def _row_chains(rows, n=2):
    step = rows // n
    return [slice(k * step, (k + 1) * step) for k in range(n)]


def _ffn_blocks():
    return D_FF // FFN_BLOCK, (D_FF // N_CHIPS) // FFN_BLOCK


def _ffn_fwd(h2, w_up_g, w_down, x1, target, modr, g_post):
    T = h2.shape[0]
    tm = min(FFN_TILE, T)
    fb = FFN_BLOCK
    nj, per = _ffn_blocks()

    def body(h_ref, wu_ref, wd_ref, x1_ref, t_ref, mod_ref, g_ref, r_ref, dy2_ref, dx2_ref, st_ref, acc):
        i, j = pl.program_id(0), pl.program_id(1)

        @pl.when((i == 0) & (j == 0))
        def _():
            st_ref[...] = jnp.zeros(st_ref.shape, F32)

        @pl.when(j == 0)
        def _():
            acc[...] = jnp.zeros(acc.shape, F32)

        for rows in _row_chains(tm):
            ra = jnp.maximum(_dot(h_ref[rows, :], wu_ref[0]), 0.0)
            rb = (ra * ra).astype(BF16)
            r_ref[rows, :] = rb
            acc[rows, :] += _dot(rb, wd_ref[...])

        @pl.when(j == nj - 1)
        def _():
            y2 = acc[...]
            rs = lax.rsqrt(jnp.mean(y2 * y2, axis=-1, keepdims=True) + RMS_EPS)
            nh = y2 * rs
            gp = g_ref[...]
            err = x1_ref[...] + mod_ref[5:6, :] * (nh * gp) - t_ref[...]
            dx2 = err * (1.0 / D_MODEL)
            dx2_ref[...] = dx2
            st_ref[0:1, :] += _colsum(err * err)
            st_ref[1:2, :] += _colsum(dx2 * (nh * gp))
            dn = dx2 * mod_ref[5:6, :]
            st_ref[2:3, :] += _colsum(dn * nh)
            dy2_ref[...] = _rms_bwd(dn * gp, nh, rs).astype(BF16)

    tile = pl.BlockSpec((tm, D_MODEL), lambda i, j: (i, 0))
    return pl.pallas_call(
        body, name="ffn_fwd", grid=(T // tm, nj),
        in_specs=[tile, pl.BlockSpec((1, D_MODEL, fb), lambda i, j: (j // per, 0, j % per)),
                  pl.BlockSpec((fb, D_MODEL), lambda i, j: (j, 0)), tile, tile,
                  _full((6, D_MODEL)), _full((1, D_MODEL))],
        out_specs=[pl.BlockSpec((tm, fb), lambda i, j: (i, j)), tile, tile, _full((8, D_MODEL))],
        out_shape=[_big((T, D_FF), BF16), _big((T, D_MODEL), BF16), _big((T, D_MODEL), F32),
                   jax.ShapeDtypeStruct((8, D_MODEL), F32)],
        scratch_shapes=[pltpu.VMEM((tm, D_MODEL), F32)],
        compiler_params=_cp(("arbitrary", "arbitrary"), 56),
    )(*_hbm(h2, w_up_g, w_down, x1, target), modr, g_post)


def _rms_bwd(dxn, xn, rs):
    return rs * (dxn - xn * jnp.mean(dxn * xn, axis=-1, keepdims=True))


def _ffn_bwd(dy2, r, x1, dx2, w_up_g, w_down, modr, g_ffn):
    T = dx2.shape[0]
    tm = min(FFN_TILE, T)
    fb = FFN_BLOCK
    nj, per = _ffn_blocks()

    def body(dy2_ref, r_ref, x1_ref, dx2_ref, wu_ref, wd_ref, mod_ref, gf_ref, da_ref, dx1_ref, st_ref, dh_s):
        i, j = pl.program_id(0), pl.program_id(1)

        @pl.when((i == 0) & (j == 0))
        def _():
            st_ref[...] = jnp.zeros(st_ref.shape, F32)

        @pl.when(j == 0)
        def _():
            dh_s[...] = jnp.zeros(dh_s.shape, F32)

        for rows in _row_chains(tm):
            ra = jnp.sqrt(r_ref[rows, :].astype(F32))
            da = (_dot_nt(dy2_ref[rows, :], wd_ref[...]) * (2.0 * ra)).astype(BF16)
            da_ref[rows, :] = da
            dh_s[rows, :] += _dot_nt(da, wu_ref[0])

        @pl.when(j == nj - 1)
        def _():
            dh = dh_s[...]
            x1v = x1_ref[...]
            rs1 = lax.rsqrt(jnp.mean(x1v * x1v, axis=-1, keepdims=True) + RMS_EPS)
            xn = x1v * rs1
            st_ref[0:1, :] += _colsum(dh)
            st_ref[1:2, :] += _colsum(dh * (xn * gf_ref[...]))
            dsc = dh * (1.0 + mod_ref[4:5, :])
            st_ref[2:3, :] += _colsum(dsc * xn)
            dx1_ref[...] = dx2_ref[...] + _rms_bwd(dsc * gf_ref[...], xn, rs1)

    tile = pl.BlockSpec((tm, D_MODEL), lambda i, j: (i, 0))
    ftile = pl.BlockSpec((tm, fb), lambda i, j: (i, j))
    return pl.pallas_call(
        body, name="ffn_bwd", grid=(T // tm, nj),
        in_specs=[tile, ftile, tile, tile, pl.BlockSpec((1, D_MODEL, fb), lambda i, j: (j // per, 0, j % per)),
                  pl.BlockSpec((fb, D_MODEL), lambda i, j: (j, 0)), _full((6, D_MODEL)), _full((1, D_MODEL))],
        out_specs=[ftile, tile, _full((8, D_MODEL))],
        out_shape=[_big((T, D_FF), BF16), _big((T, D_MODEL), F32), jax.ShapeDtypeStruct((8, D_MODEL), F32)],
        scratch_shapes=[pltpu.VMEM((tm, D_MODEL), F32)],
        compiler_params=_cp(("arbitrary", "arbitrary"), 56),
    )(*_hbm(dy2, r, x1, dx2, w_up_g, w_down), modr, g_ffn)


def _mix_out_bwd(dx1, y, cat, w_out, modr, g_post, swap):
    T = dx1.shape[0]
    tm = _tok_tile(T)
    nt = T // tm
    ns = len(swap)

    def body(*refs):
        dx1_ref, y_ref, cat_ref, w_ref, mod_ref, gp_ref = refs[:6]
        s_ins = refs[6:6 + ns]
        dcat_ref, st_ref, gw_ref = refs[6 + ns:9 + ns]
        s_outs = refs[9 + ns:9 + 2 * ns]
        gacc, gsem, pssem, prsem = refs[9 + 2 * ns:]
        i = pl.program_id(0)

        @pl.when(i == 0)
        def _():
            for cp in _pair_copies(s_ins, s_outs, pssem, prsem):
                cp.start()
            st_ref[...] = jnp.zeros(st_ref.shape, F32)
            gacc[...] = jnp.zeros(gacc.shape, F32)

        dxv, yv = dx1_ref[...], y_ref[...]
        rs = lax.rsqrt(jnp.mean(yv * yv, axis=-1, keepdims=True) + RMS_EPS)
        nh = yv * rs
        st_ref[0:1, :] += _colsum(dxv * (nh * gp_ref[...]))
        dn = dxv * mod_ref[2:3, :]
        st_ref[1:2, :] += _colsum(dn * nh)
        dy = _rms_bwd(dn * gp_ref[...], nh, rs).astype(BF16)
        dcat_ref[...] = _dot_nt(dy, w_ref[...])
        for cols in _row_chains(D_MODEL):
            gacc[:, cols] += _dot_tn(cat_ref[...], dy[:, cols])

        @pl.when(i == nt - 1)
        def _():
            out = pltpu.make_async_copy(gacc, gw_ref, gsem)
            out.start()
            copies = _pair_copies(s_ins, s_outs, pssem, prsem)
            for cp in copies:
                cp.wait_recv()
            for cp in copies:
                cp.wait_send()
            out.wait()

    tile = pl.BlockSpec((tm, D_MODEL), lambda i: (i, 0))
    hbm = pl.BlockSpec(memory_space=pl.ANY)
    outs = pl.pallas_call(
        body, name="mix_out_bwd", grid=(nt,),
        in_specs=[tile, tile, tile, _full((D_MODEL, D_MODEL)), _full((6, D_MODEL)), _full((1, D_MODEL))]
        + [hbm] * ns,
        out_specs=[tile, _full((8, D_MODEL)), hbm] + [hbm] * ns,
        out_shape=[_big((T, D_MODEL), F32), jax.ShapeDtypeStruct((8, D_MODEL), F32), _big((D_MODEL, D_MODEL), F32)]
        + _pair_shapes(swap),
        scratch_shapes=[pltpu.VMEM((D_MODEL, D_MODEL), F32), pltpu.SemaphoreType.DMA] + _pair_sems(ns),
        compiler_params=_cp(("arbitrary",), 48),
    )(*_hbm(dx1, y, cat, w_out), modr, g_post, *[_hbm(g) for g in swap])
    return outs[:3], outs[3:]


def _mixers_bwd(p, dcat, ys, o, states, h1, wdw, vecs, lbl, pairs_b):
    T = p.shape[0]
    tm = min(MIXB_TILE, T)
    nt = T // tm
    nch = tm // CHUNK
    hpt = tm // HALO
    nx = len(pairs_b)
    nb = IN_COLS // N_CHIPS

    def body(*refs):
        (p_ref, ph_ref, dcat_ref, ys_ref, o_ref, st_ref, h1_ref, wdw_ref, bdw_ref, gain_ref, bias_ref, gout_ref,
         lbl_ref) = refs[:13]
        x_ins = refs[13:13 + nx]
        dp_ref, sb_ref, s5_ref, dw_ref = refs[13 + nx:17 + nx]
        x_outs = refs[17 + nx:17 + 2 * nx]
        gin_ref = refs[17 + 2 * nx]
        (ubuf, dybuf, carry, dstate, qt_s, kt_s, kh_s, v_s, do_s, egl_s, dqt_s, dkt_s, dkh_s, dv_s, dgl_s,
         dsh, dw8, dshift, lower_s, upper_s, same_s, gmat_s, gacc, dp_prev, h1_prev, gsem, xssem,
         xrsem) = refs[18 + 2 * nx:]
        i = pl.program_id(0)
        tile_idx = nt - 1 - i

        @pl.when(i == 0)
        def _():
            for cp in _xchg_copies(x_ins, x_outs, xssem, xrsem):
                cp.start()
            gacc[...] = jnp.zeros(gacc.shape, F32)
            dstate[...] = jnp.zeros(dstate.shape, F32)
            carry[...] = jnp.zeros(carry.shape, F32)
            sb_ref[...] = jnp.zeros(sb_ref.shape, F32)
            s5_ref[...] = jnp.zeros(s5_ref.shape, F32)
            dw_ref[...] = jnp.zeros(dw_ref.shape, F32)
            dw8[...] = jnp.zeros(dw8.shape, F32)
            lower_s[...], upper_s[...], same_s[...] = _chunk_masks(tm)
            gmat_s[...] = _gn_matrix()
            dsh[0:SUB, :] = jnp.zeros((SUB, CONV_CH), F32)
            dsh[SUB + tm:2 * SUB + tm, :] = jnp.zeros((SUB, CONV_CH), F32)
            ubuf[HALO + tm:HALO + tm + SUB, :] = jnp.zeros((SUB, CONV_CH), F32)
            dp_prev[...] = jnp.zeros(dp_prev.shape, BF16)
            h1_prev[...] = jnp.zeros(h1_prev.shape, BF16)

        n_pieces = (tm // CONV_ROWS) * (CONV_CH // LANE)
        per_block = n_pieces // N_CHIPS
        prow = D_MODEL // per_block

        def w_in_grad_piece(k):
            j, part = k // per_block, k % per_block
            rows_k = slice(part * prow, (part + 1) * prow)
            gacc[j, rows_k, :] += _dot_tn(h1_prev[:, rows_k], dp_prev[:, j * nb:(j + 1) * nb])

        uh = ph_ref[:, 0:CONV_CH] * _sig(ph_ref[:, CONV_CH:2 * CONV_CH])
        ubuf[0:HALO, :] = jnp.where(tile_idx > 0, uh, 0.0)
        ubuf[HALO:HALO + tm, :] = p_ref[:, 0:CONV_CH] * _sig(p_ref[:, CONV_CH:2 * CONV_CH])
        gmat = gmat_s[...]
        gain = gain_ref[...]
        yv = ys_ref[...]
        d = yv - _gmean(yv, gmat)
        rs = lax.rsqrt(_gmean(d * d, gmat) + GN_EPS)
        yn = d * rs
        z = yn * gain + bias_ref[...]
        sz = _sig(z)
        dz = dcat_ref[:, 0:CONV_CH] * (sz * (1.0 + z * (1.0 - sz)))
        dyn = dz * gain
        dyc = rs * (dyn - _gmean(dyn, gmat) - yn * _gmean(dyn * yn, gmat))
        s5_ref[0:1, :] += _colsum(dyc)
        s5_ref[1:2, :] += _colsum(dz * yn)
        s5_ref[2:3, :] += _colsum(dz)
        dybuf[tm:tm + HALO, :] = carry[...]
        dybuf[0:tm, :] = dyc
        dsh[SUB:SUB + tm, :] = dyc
        carry[...] = dyc[0:HALO, :]
        for b in range(SUB):
            dshift[...] = dsh[SUB - b:2 * SUB - b + tm, :]
            for j, off in CONV_FWD_TAPS:
                if off % SUB == b:
                    prod = dshift[...] * ubuf[off - b:off - b + tm + SUB, :]
                    dw8[j] += jnp.sum(prod.reshape((tm + SUB) // SUB, SUB, CONV_CH), axis=0)
        for r in range(tm // CONV_ROWS):
            rows = slice(r * CONV_ROWS, (r + 1) * CONV_ROWS)
            for lb_ in range(CONV_CH // LANE):
                lanes = slice(lb_ * LANE, (lb_ + 1) * LANE)
                glanes = slice(CONV_CH + lb_ * LANE, CONV_CH + (lb_ + 1) * LANE)
                w_in_grad_piece(r * (CONV_CH // LANE) + lb_)
                acc = _tap_conv(dybuf, wdw_ref, r * CONV_ROWS, CONV_BWD_TAPS, lanes)
                val = p_ref[rows, lanes]
                sg = _sig(p_ref[rows, glanes])
                dval = acc * sg
                dgate = acc * val * (sg * (1.0 - sg))
                dp_ref[rows, lanes] = dval.astype(BF16)
                dp_ref[rows, glanes] = dgate.astype(BF16)
                sb_ref[0:1, lanes] += _colsum(dval)
                sb_ref[0:1, glanes] += _colsum(dgate)

        o0 = 2 * CONV_CH
        for h in range(N_HEADS):
            sl = slice(h * HEAD_D, (h + 1) * HEAD_D)
            gsl = slice(o0 + 3 * HGRN_W + h * HEAD_D, o0 + 3 * HGRN_W + (h + 1) * HEAD_D)
            oh = o_ref[:, sl]
            gh = p_ref[:, gsl]
            dh = dcat_ref[:, CONV_CH + h * HEAD_D:CONV_CH + (h + 1) * HEAD_D]
            gout = gout_ref[:, sl]
            rsh = lax.rsqrt(jnp.mean(oh * oh, axis=-1, keepdims=True) + RMS_EPS)
            on = oh * rsh
            sgg = _sig(gh)
            dgh = dh * (on * gout) * (sgg * (1.0 + gh * (1.0 - sgg)))
            dm = dh * (gh * sgg)
            s5_ref[3:4, sl] += _colsum(dm * on)
            do_s[:, sl] = _rms_bwd(dm * gout, on, rsh).astype(BF16)
            dp_ref[:, gsl] = dgh.astype(BF16)
            sb_ref[2:3, CONV_CH + h * HEAD_D:CONV_CH + (h + 1) * HEAD_D] += _colsum(dgh)

        lb, _ = _lower_bound(lbl_ref)
        lower, upper, same = lower_s[...], upper_s[...], same_s[...]
        pq = p_ref[:, o0:o0 + HGRN_W]
        pr = _hgrn_prep(pq, p_ref[:, o0 + HGRN_W:o0 + 2 * HGRN_W], lb, lower, same)
        qt_s[...] = pr["qt"].astype(BF16)
        kt_s[...] = pr["kt"].astype(BF16)
        kh_s[...] = pr["kh"].astype(BF16)
        v_s[...] = p_ref[:, o0 + 2 * HGRN_W:o0 + 3 * HGRN_W].astype(BF16)
        egl_s[...] = jnp.exp(pr["Gl"])
        tri = _tri()

        def chunk(it, c_):
            ci = nch - 1 - it
            r0 = pl.multiple_of(ci * CHUNK, CHUNK)
            rows = pl.ds(r0, CHUNK)
            for h in range(N_HEADS):
                ls = pl.ds(h * HEAD_D, HEAD_D)
                qc, kc, hc, vc = qt_s[rows, ls], kt_s[rows, ls], kh_s[rows, ls], v_s[rows, ls]
                dob = do_s[rows, ls]
                s0 = st_ref[ci, h]
                s0b = s0.astype(BF16)
                ds1 = dstate[h]
                ds1b = ds1.astype(BF16)
                egl = egl_s[pl.ds(r0, 1), ls]
                att = jnp.where(tri, _dot_nt(qc, kc), 0.0).astype(BF16)
                datt = jnp.where(tri, _dot_nt(dob, vc), 0.0).astype(BF16)
                dv_s[rows, ls] = _dot_tn(att, dob) + _dot_nt(hc, ds1b)
                dqt_s[rows, ls] = _dot(datt, kc) + _dot(dob, s0b)
                dkt_s[rows, ls] = _dot_tn(datt, qc)
                dkh_s[rows, ls] = _dot(vc, ds1b)
                dgl = egl * _colsum(ds1 * s0)
                dgl_s[rows, ls] = jnp.broadcast_to(dgl, (CHUNK, HEAD_D))
                dstate[h] = ds1 * egl + _dot_tn(dob, qc)
            return c_

        lax.fori_loop(0, nch, chunk, 0, unroll=min(CHUNK_UNROLL, nch))
        dqt, dkt, dkh = dqt_s[...], dkt_s[...], dkh_s[...]
        dk = dkt * pr["enG"] + dkh * pr["eGlG"]
        khk = dkh * kh_s[...].astype(F32)
        dG = dqt * qt_s[...].astype(F32) - dkt * kt_s[...].astype(F32) - khk
        dlogf = _mm3(upper, dG) + _mm3(same, khk) + dgl_s[...]
        df = dlogf / pr["f"] - dk
        sf, sq = pr["sf"], pr["sq"]
        s5_ref[4:5, :] += _colsum(df * (1.0 - sf))
        dfl = df * (1.0 - lb) * (sf * (1.0 - sf))
        dq = (dqt * pr["eG"]) * (sq * (1.0 + pq * (1.0 - sq)))
        dvv = dv_s[...]
        dp_ref[:, o0:o0 + HGRN_W] = dq.astype(BF16)
        dp_ref[:, o0 + HGRN_W:o0 + 2 * HGRN_W] = dfl.astype(BF16)
        dp_ref[:, o0 + 2 * HGRN_W:o0 + 3 * HGRN_W] = dvv.astype(BF16)
        sb_ref[1:2, 0:HGRN_W] += _colsum(dq)
        sb_ref[1:2, HGRN_W:2 * HGRN_W] += _colsum(dfl)
        sb_ref[2:3, 0:HGRN_W] += _colsum(dvv)

        dp_prev[...] = dp_ref[...]
        h1_prev[...] = h1_ref[...]

        @pl.when(i == nt - 1)
        def _():
            for k in range(n_pieces):
                w_in_grad_piece(k)
            out = pltpu.make_async_copy(gacc, gin_ref, gsem)
            out.start()
            for j in range(CONV_K):
                dw_ref[j:j + 1, :] = _colsum(dw8[j])
            copies = _xchg_copies(x_ins, x_outs, xssem, xrsem)
            for cp in copies:
                cp.wait_recv()
            for cp in copies:
                cp.wait_send()
            out.wait()

    rev = lambda cols: pl.BlockSpec((tm, cols), lambda i: (nt - 1 - i, 0))
    halo = pl.BlockSpec((HALO, 2 * CONV_CH), lambda i: (jnp.maximum((nt - 1 - i) * hpt - 1, 0), 0))
    wide = lambda n: pltpu.VMEM((tm, HGRN_W), n)
    hbm = pl.BlockSpec(memory_space=pl.ANY)
    outs = pl.pallas_call(
        body, name="mixers_bwd", grid=(nt,),
        in_specs=[rev(IN_COLS), halo, rev(D_MODEL), rev(CONV_CH), rev(HGRN_W),
                  pl.BlockSpec((nch, N_HEADS, HEAD_D, HEAD_D), lambda i: (nt - 1 - i, 0, 0, 0)), rev(D_MODEL),
                  _full((HALO, CONV_CH))] + [_full((1, CONV_CH))] * 4 + [_full((2, HGRN_W))] + [hbm] * nx,
        out_specs=[rev(IN_COLS), _full((8, D_MODEL)), _full((8, CONV_CH)), _full((HALO, CONV_CH))]
        + [hbm] * (nx + 1),
        out_shape=[_big((T, IN_COLS), BF16), jax.ShapeDtypeStruct((8, D_MODEL), F32),
                   jax.ShapeDtypeStruct((8, CONV_CH), F32), jax.ShapeDtypeStruct((HALO, CONV_CH), F32)]
        + [_big(pb.shape, BF16) for pb in pairs_b] + [_big((N_CHIPS, D_MODEL, nb), F32)],
        scratch_shapes=[pltpu.VMEM((tm + HALO + SUB, CONV_CH), F32), pltpu.VMEM((tm + HALO, CONV_CH), F32),
                        pltpu.VMEM((HALO, CONV_CH), F32), pltpu.VMEM((N_HEADS, HEAD_D, HEAD_D), F32),
                        wide(BF16), wide(BF16), wide(BF16), wide(BF16), wide(BF16),
                        wide(F32), wide(F32), wide(F32), wide(F32), wide(F32), wide(F32),
                        pltpu.VMEM((tm + 2 * SUB, CONV_CH), F32), pltpu.VMEM((HALO, SUB, CONV_CH), F32),
                        pltpu.VMEM((tm + SUB, CONV_CH), F32), pltpu.VMEM((tm, tm), BF16), pltpu.VMEM((tm, tm), BF16),
                        pltpu.VMEM((tm, tm), BF16), pltpu.VMEM((CONV_CH, CONV_CH), BF16),
                        pltpu.VMEM((N_CHIPS, D_MODEL, nb), F32), pltpu.VMEM((tm, IN_COLS), BF16),
                        pltpu.VMEM((tm, D_MODEL), BF16), pltpu.SemaphoreType.DMA]
        + _xchg_sems(nx),
        compiler_params=_cp(("arbitrary",), 56),
    )(*_hbm(p, p, dcat, ys, o, states, h1), wdw, *vecs, lbl, *[_hbm(pb) for pb in pairs_b])
    return outs[:4], outs[4:4 + nx], outs[4 + nx]


def _mix_in_bwd(dp, w_in_g, x, dx1, modr, g_pre, pairs_b):
    T = x.shape[0]
    tm = _tok_tile(T)
    nt = T // tm
    nb = IN_COLS // N_CHIPS
    nx = len(pairs_b)

    def body(*refs):
        dp_ref, w_ref, x_ref, dx1_ref, mod_ref, g_ref = refs[:6]
        x_ins = refs[6:6 + nx]
        gx_ref, st_ref = refs[6 + nx:8 + nx]
        x_outs = refs[8 + nx:8 + 2 * nx]
        xssem, xrsem = refs[8 + 2 * nx:]
        i = pl.program_id(0)

        @pl.when(i == 0)
        def _():
            for cp in _xchg_copies(x_ins, x_outs, xssem, xrsem):
                cp.start()
            st_ref[...] = jnp.zeros(st_ref.shape, F32)

        dh = None
        for j in range(N_CHIPS):
            part = _dot_nt(dp_ref[:, j * nb:(j + 1) * nb], w_ref[j])
            dh = part if dh is None else dh + part
        xv = x_ref[...]
        rs = lax.rsqrt(jnp.mean(xv * xv, axis=-1, keepdims=True) + RMS_EPS)
        xn = xv * rs
        st_ref[0:1, :] += _colsum(dh)
        st_ref[1:2, :] += _colsum(dh * (xn * g_ref[...]))
        dsc = dh * (1.0 + mod_ref[1:2, :])
        st_ref[2:3, :] += _colsum(dsc * xn)
        gx_ref[...] = dx1_ref[...] + _rms_bwd(dsc * g_ref[...], xn, rs)

        @pl.when(i == nt - 1)
        def _():
            copies = _xchg_copies(x_ins, x_outs, xssem, xrsem)
            for cp in copies:
                cp.wait_recv()
            for cp in copies:
                cp.wait_send()

    tile = pl.BlockSpec((tm, D_MODEL), lambda i: (i, 0))
    hbm = pl.BlockSpec(memory_space=pl.ANY)
    outs = pl.pallas_call(
        body, name="mix_in_bwd", grid=(nt,),
        in_specs=[pl.BlockSpec((tm, IN_COLS), lambda i: (i, 0)), _full((N_CHIPS, D_MODEL, nb)), tile, tile,
                  _full((6, D_MODEL)), _full((1, D_MODEL))] + [hbm] * nx,
        out_specs=[tile, _full((8, D_MODEL))] + [hbm] * nx,
        out_shape=[_big((T, D_MODEL), F32), jax.ShapeDtypeStruct((8, D_MODEL), F32)]
        + [_big(pb.shape, BF16) for pb in pairs_b],
        scratch_shapes=_xchg_sems(nx),
        compiler_params=_cp(("arbitrary",), 48),
    )(*_hbm(dp, w_in_g, x, dx1), modr, g_pre, *[_hbm(pb) for pb in pairs_b])
    return outs[:2], outs[2:]


def _weight_grad(a, b, a_blocked, b_blocked, name):
    T = a.shape[0]
    tt = min(GRAD_TILE, T)
    nt = T // tt
    ka = a.shape[1] // N_CHIPS if a_blocked else a.shape[1]
    nb = b.shape[1] // N_CHIPS if b_blocked else b.shape[1]

    def body(a_ref, b_ref, o_ref, ob_ref):
        t = pl.program_id(1)

        @pl.when(t == 0)
        def _():
            o_ref[...] = jnp.zeros(o_ref.shape, F32)

        for cols in _row_chains(nb):
            o_ref[0, :, cols] += _dot_tn(a_ref[...], b_ref[:, cols])

        @pl.when(t == nt - 1)
        def _():
            ob_ref[0] = o_ref[0].astype(BF16)

    blk = pl.BlockSpec((1, ka, nb), lambda j, t: (j, 0, 0))
    return pl.pallas_call(
        body, name=name, grid=(N_CHIPS, nt),
        in_specs=[pl.BlockSpec((tt, ka), (lambda j, t: (t, j)) if a_blocked else (lambda j, t: (t, 0))),
                  pl.BlockSpec((tt, nb), (lambda j, t: (t, j)) if b_blocked else (lambda j, t: (t, 0)))],
        out_specs=[blk, blk],
        out_shape=[_big((N_CHIPS, ka, nb), F32), _big((N_CHIPS, ka, nb), BF16)],
        compiler_params=_cp(("arbitrary", "arbitrary"), 48),
    )(*_hbm(a, b))


R_LOSS = 0
R_FFN = 8
R_OUT = 16
R_IN = 24
R_BIN = 32
R_512 = 40
R_DW = 48
N_STAT_ROWS = 80
MOD_ROWS = (R_IN + 0, R_IN + 1, R_OUT + 0, R_FFN + 0, R_FFN + 1, R_LOSS + 1)


def _small_update(gath, params):
    names = ["b_ada", "lb_logits", "g_pre_mix", "b_in", "b_dw", "gn_gain", "gn_bias", "g_hgrn_out", "g_post_mix",
             "g_pre_ffn", "g_post_ffn"]
    flat = []
    for n in names:
        flat += list(params[n])
    n_in = 1 + len(flat)

    def body(*refs):
        g_ref = refs[0]
        prm = {n: refs[1 + 3 * k:4 + 3 * k] for k, n in enumerate(names)}
        outs = refs[n_in:]
        loss_ref, dmod_ref, dwdw_ref = outs[0], outs[1], outs[2]
        res = {n: outs[3 + 4 * k:7 + 4 * k] for k, n in enumerate(names)}
        red = g_ref[0]
        for dev in range(1, N_DEV):
            red = red + g_ref[dev]
        loss_ref[...] = jnp.broadcast_to(
            (0.5 / D_MODEL) * jnp.sum(red[R_LOSS:R_LOSS + 1, :], axis=-1, keepdims=True), loss_ref.shape)
        for dev in range(N_DEV):
            for k, r in enumerate(MOD_ROWS):
                dmod_ref[dev:dev + 1, k * D_MODEL:(k + 1) * D_MODEL] = g_ref[dev, r:r + 1, :]
        dwdw_ref[...] = red[R_DW:R_DW + HALO, 0:CONV_CH]

        def finish(name, pieces):
            w_ref, m_ref, v_ref = prm[name]
            g_out, d_out, m_out, v_out = res[name]
            for rsl, lsl, g in pieces:
                d, m2, v2 = _adam_math(w_ref[rsl, lsl], g, m_ref[rsl, lsl], v_ref[rsl, lsl])
                g_out[rsl, lsl] = g
                d_out[rsl, lsl] = d
                m_out[rsl, lsl] = m2
                v_out[rsl, lsl] = v2

        one = slice(0, 1)
        row = lambda r: red[r:r + 1, :]
        half = lambda r: red[r:r + 1, 0:CONV_CH]
        finish("b_ada", [(one, slice(k * D_MODEL, (k + 1) * D_MODEL), row(r)) for k, r in enumerate(MOD_ROWS)])
        finish("b_in", [(one, slice(k * D_MODEL, (k + 1) * D_MODEL), row(R_BIN + k)) for k in range(3)])
        finish("g_pre_mix", [(one, slice(None), row(R_IN + 2))])
        finish("g_post_mix", [(one, slice(None), row(R_OUT + 1))])
        finish("g_pre_ffn", [(one, slice(None), row(R_FFN + 2))])
        finish("g_post_ffn", [(one, slice(None), row(R_LOSS + 2))])
        finish("b_dw", [(one, slice(None), half(R_512 + 0))])
        finish("gn_gain", [(one, slice(None), half(R_512 + 1))])
        finish("gn_bias", [(one, slice(None), half(R_512 + 2))])
        finish("g_hgrn_out", [(one, slice(None), half(R_512 + 3))])
        s0, s1 = _lower_bound(prm["lb_logits"][0])
        dlb = half(R_512 + 4)
        finish("lb_logits", [(slice(0, 1), slice(None), dlb * s0 * (1.0 - s0)),
                             (slice(1, 2), slice(None), -dlb * s0 * s1)])

    vm = pl.BlockSpec(memory_space=pltpu.VMEM)
    out_shape = [jax.ShapeDtypeStruct((8, 128), F32), jax.ShapeDtypeStruct((N_DEV, 6 * D_MODEL), F32),
                 jax.ShapeDtypeStruct((HALO, CONV_CH), F32)]
    for n in names:
        out_shape += [jax.ShapeDtypeStruct(params[n][0].shape, F32)] * 4
    outs = pl.pallas_call(
        body, name="small_update", out_shape=out_shape,
        in_specs=[vm] * n_in, out_specs=[vm] * len(out_shape),
        compiler_params=_cp(None, 32),
    )(gath, *flat)
    return outs[0], outs[1], outs[2], {n: outs[3 + 4 * k:7 + 4 * k] for k, n in enumerate(names)}


def _wdw_adam(w, g, m, v):
    def body(w_ref, g_ref, m_ref, v_ref, d_out, m_out, v_out):
        d, m2, v2 = _adam_math(w_ref[...], g_ref[...], m_ref[...], v_ref[...])
        d_out[...] = d
        m_out[...] = m2
        v_out[...] = v2

    vm = pl.BlockSpec(memory_space=pltpu.VMEM)
    return pl.pallas_call(
        body, name="wdw_adam", out_shape=[jax.ShapeDtypeStruct(w.shape, F32)] * 3,
        in_specs=[vm] * 4, out_specs=[vm] * 3, compiler_params=_cp(None, 16),
    )(w, g, m, v)


def kernel(x, c, w_ada, b_ada, lb_logits, g_pre_mix, w_in, b_in, w_dw, b_dw, gn_gain, gn_bias, g_hgrn_out, w_out, g_post_mix, g_pre_ffn, w_up, w_down, g_post_ffn, loss_target, m_w_ada, m_b_ada, m_lb_logits, m_g_pre_mix, m_w_in, m_b_in, m_w_dw, m_b_dw, m_gn_gain, m_gn_bias, m_g_hgrn_out, m_w_out, m_g_post_mix, m_g_pre_ffn, m_w_up, m_w_down, m_g_post_ffn, v_w_ada, v_b_ada, v_lb_logits, v_g_pre_mix, v_w_in, v_b_in, v_w_dw, v_b_dw, v_gn_gain, v_gn_bias, v_g_hgrn_out, v_w_out, v_g_post_mix, v_g_pre_ffn, v_w_up, v_w_down, v_g_post_ffn):
    ax, ay, ac = lax.axis_index("x"), lax.axis_index("y"), lax.axis_index("c")
    chip = 2 * ax + ay
    T = x.shape[1]
    xs, tgt = x[0], loss_target[0]
    ada_cols = w_ada.shape[2]

    b_sh = lax.dynamic_slice_in_dim(b_ada, chip * ada_cols, ada_cols, axis=1)
    wdw_pad = jnp.pad(w_dw[0], ((0, HALO - CONV_K), (0, 0)))
    chip1 = jnp.reshape(chip, (1,)).astype(jnp.int32)
    place = jnp.stack([ac, chip]).astype(jnp.int32)
    _, c8, modg, wdwg = _ada_exchange(c, w_ada[0], b_sh, wdw_pad)
    modr = modg.reshape(6, D_MODEL)
    wdw_all = jnp.transpose(wdwg, (1, 0, 2)).reshape(HALO, CONV_CH)
    bufs = {t: _cast_own(chip1, w[0], "cast_" + t)
            for w, t in ((w_in, "w_in"), (w_out, "w_out"), (w_up, "w_up"), (w_down, "w_down"))}
    vec = (b_dw, gn_gain, gn_bias, g_hgrn_out)

    p, h1, w_in_g, w_out_g = _mix_in_fwd(chip1, xs, modr, g_pre_mix, b_in.reshape(N_CHIPS, 1, IN_COLS // N_CHIPS),
                                          bufs["w_in"], bufs["w_out"])
    w_out_f = w_out_g.reshape(D_MODEL, D_MODEL)
    (cat, ys, o, states, y, x1, h2), (w_up_g, w_down_g) = _mixers_fwd(
        p, wdw_all, vec, lb_logits, w_out_f, xs, modr, g_post_mix, g_pre_ffn, [bufs["w_up"], bufs["w_down"]])
    w_down_f = w_down_g.reshape(D_FF, D_MODEL)
    r, dy2, dx2, st_loss = _ffn_fwd(h2, w_up_g, w_down_f, x1, tgt, modr, g_post_ffn)

    def pair_sums(grads, got, tags):
        return [_pair_sum(place, g, o_, "pair_sum_" + t) for (g, _), o_, t in zip(grads, got, tags)]

    da, dx1, st_ffn = _ffn_bwd(dy2, r, x1, dx2, w_up_g, w_down_f, modr, g_pre_ffn)
    g_up = _weight_grad(h2, da, False, True, "grad_w_up")
    g_down = _weight_grad(r, dy2, True, False, "grad_w_down")
    (dcat, st_out, g_out), got_ud = _mix_out_bwd(dx1, y, cat, w_out_f, modr, g_post_mix, [g_up[1], g_down[1]])
    g_out = g_out.reshape(N_CHIPS, D_MODEL // N_CHIPS, D_MODEL)
    got_o = _pair_swap([g_out], "pair_swap_w_out")
    early = pair_sums([(g_out, None), g_up, g_down], list(got_o) + list(got_ud), ["w_out", "w_up", "w_down"])
    (dp, st_bin, st_512, dwdw), got_early, g_in = _mixers_bwd(p, dcat, ys, o, states, h1, wdw_all, vec, lb_logits,
                                                              [pb for _, pb in early])
    late = pair_sums([(g_in, None)], _pair_swap([g_in], "pair_swap_w_in"), ["w_in"])
    (grad_x, st_in), got_late = _mix_in_bwd(dp, w_in_g, xs, dx1, modr, g_pre_mix, [late[0][1]])
    fulls = [_chip_sum(place, pf, gb, "chip_sum_" + t)
             for (pf, _), gb, t in zip(late + early, list(got_late) + list(got_early), ["w_in", "w_out", "w_up", "w_down"])]

    pad_lanes = lambda s: jnp.pad(s, ((0, 0), (0, D_MODEL - s.shape[1])))
    stats = jnp.concatenate([st_loss, st_ffn, st_out, st_in, st_bin, pad_lanes(st_512), pad_lanes(dwdw)], axis=0)
    (g_w_in, g_w_out, g_w_up, g_w_down), gath = _final_exchange(fulls, stats)
    small = {"b_ada": (b_ada, m_b_ada, v_b_ada), "lb_logits": (lb_logits, m_lb_logits, v_lb_logits),
             "g_pre_mix": (g_pre_mix, m_g_pre_mix, v_g_pre_mix), "b_in": (b_in, m_b_in, v_b_in),
             "b_dw": (b_dw, m_b_dw, v_b_dw), "gn_gain": (gn_gain, m_gn_gain, v_gn_gain),
             "gn_bias": (gn_bias, m_gn_bias, v_gn_bias), "g_hgrn_out": (g_hgrn_out, m_g_hgrn_out, v_g_hgrn_out),
             "g_post_mix": (g_post_mix, m_g_post_mix, v_g_post_mix), "g_pre_ffn": (g_pre_ffn, m_g_pre_ffn, v_g_pre_ffn),
             "g_post_ffn": (g_post_ffn, m_g_post_ffn, v_g_post_ffn)}
    loss_t, dmod_all, dwdw_sum, sres = _small_update(gath, small)
    loss = loss_t[0, 0]

    res = dict(sres)
    dmod_sh = lax.dynamic_slice_in_dim(dmod_all, chip * ada_cols, ada_cols, axis=1)
    res["w_ada"] = [t[None] for t in _ada_grad_adam(jnp.transpose(c8), dmod_sh, w_ada[0], m_w_ada[0], v_w_ada[0])]
    g_wdw = lax.dynamic_slice_in_dim(dwdw_sum, chip * HEAD_D, HEAD_D, axis=1)[:CONV_K][None]
    res["w_dw"] = [g_wdw] + list(_wdw_adam(w_dw, g_wdw, m_w_dw, v_w_dw))
    for name, g, w, m, v in (("w_in", g_w_in, w_in, m_w_in, v_w_in), ("w_out", g_w_out, w_out, m_w_out, v_w_out),
                             ("w_up", g_w_up, w_up, m_w_up, v_w_up), ("w_down", g_w_down, w_down, m_w_down, v_w_down)):
        d, m2, v2 = _adam_big(w[0], g, m[0], v[0], "adam_" + name)
        res[name] = [g[None], d[None], m2[None], v2[None]]

    order = ["w_ada", "b_ada", "lb_logits", "g_pre_mix", "w_in", "b_in", "w_dw", "b_dw", "gn_gain", "gn_bias",
             "g_hgrn_out", "w_out", "g_post_mix", "g_pre_ffn", "w_up", "w_down", "g_post_ffn"]
    out = [loss, grad_x[None]]
    for k in range(4):
        out += [res[n][k] for n in order]
    return tuple(out)
```

```python
import jax
import jax.numpy as jnp
from jax import lax
from jax.experimental import pallas as pl
from jax.experimental.pallas import tpu as pltpu

F32, BF16 = jnp.float32, jnp.bfloat16
D_MODEL = 1024
CONV_CH = 512
HGRN_W = 512
N_HEADS = 4
HEAD_D = 128
CONV_K = 31
GN_GROUP = 64
GN_SHIFT = 6
IN_COLS = 3072
D_FF = 4096
CHUNK = 64
CHUNK_SHIFT = 6
N_CHIPS = 4
N_DEV = 8
RMS_EPS = 1e-6
GN_EPS = 1e-5
ADAM_LR, ADAM_B1, ADAM_B2, ADAM_EPS, ADAM_WD, ADAM_STEP = 0.001, 0.9, 0.999, 1e-08, 0.01, 10
TOK_TILE = 512
MIXIN_TILE = 1024
MIXB_TILE = 256
FFN_TILE = 1024
FFN_BLOCK = 512
GRAD_TILE = 2048
HALO = 32
SUB = 8
LANE = 128
CONV_ROWS = 128
CHUNK_UNROLL = 8
MIB = 1 << 20
MESH = pl.DeviceIdType.MESH
OTHER_CHIPS = ((0, 1), (1, 0), (1, 1))


def _cp(sem=None, vmem_mib=48):
    return pltpu.CompilerParams(dimension_semantics=sem, vmem_limit_bytes=vmem_mib * MIB)


def _dot(a, b):
    return jnp.dot(a, b, preferred_element_type=F32)


def _dot_nt(a, b):
    return lax.dot_general(a, b, (((1,), (1,)), ((), ())), preferred_element_type=F32)


def _dot_tn(a, b):
    return lax.dot_general(a, b, (((0,), (0,)), ((), ())), preferred_element_type=F32)


def _sig(v):
    return 0.5 * jnp.tanh(0.5 * v) + 0.5


def _colsum(v):
    return jnp.sum(v, axis=0, keepdims=True)


def _flip(v, b):
    return 1 - v if b else v


def _rcopy(src, dst, ssem, rsem, dev):
    return pltpu.make_async_remote_copy(src_ref=src, dst_ref=dst, send_sem=ssem, recv_sem=rsem,
                                        device_id=dev, device_id_type=MESH)


def _place():
    return lax.axis_index("x"), lax.axis_index("y"), lax.axis_index("c")


def _full(shape):
    return pl.BlockSpec(shape, lambda *_: (0,) * len(shape))


def _big(shape, dtype):
    return pltpu.HBM(shape, dtype)


def _hbm(*arrays):
    out = [pltpu.with_memory_space_constraint(a, pltpu.HBM) for a in arrays]
    return out[0] if len(out) == 1 else out


def _split2(v):
    hi = v.astype(BF16)
    lo = (v - hi.astype(F32)).astype(BF16)
    return hi, lo


def _split3(v):
    h1 = v.astype(BF16)
    r1 = v - h1.astype(F32)
    h2 = r1.astype(BF16)
    h3 = (r1 - h2.astype(F32)).astype(BF16)
    return h1, h2, h3


def _mm3(mat, v):
    h1, h2, h3 = _split3(v)
    return _dot(mat, h1) + _dot(mat, h2) + _dot(mat, h3)


def _gn_matrix():
    r = lax.broadcasted_iota(jnp.int32, (CONV_CH, CONV_CH), 0) >> GN_SHIFT
    c = lax.broadcasted_iota(jnp.int32, (CONV_CH, CONV_CH), 1) >> GN_SHIFT
    return jnp.where(r == c, 1.0 / GN_GROUP, 0.0).astype(BF16)


def _gmean(v, gmat):
    hi, lo = _split2(v)
    return _dot(hi, gmat) + _dot(lo, gmat)


def _chunk_masks(tm):
    r = lax.broadcasted_iota(jnp.int32, (tm, tm), 0)
    c = lax.broadcasted_iota(jnp.int32, (tm, tm), 1)
    same = (r >> CHUNK_SHIFT) == (c >> CHUNK_SHIFT)
    one = lambda m: jnp.where(m, 1.0, 0.0).astype(BF16)
    return one(same & (c <= r)), one(same & (c >= r))


def _tri():
    return lax.broadcasted_iota(jnp.int32, (CHUNK, CHUNK), 0) >= lax.broadcasted_iota(jnp.int32, (CHUNK, CHUNK), 1)


def _lower_bound(lbl_ref):
    l0, l1 = lbl_ref[0:1, :], lbl_ref[1:2, :]
    mx = jnp.maximum(l0, l1)
    e0, e1 = jnp.exp(l0 - mx), jnp.exp(l1 - mx)
    return e0 / (e0 + e1), e1 / (e0 + e1)


CONV_FWD_TAPS = tuple((j, HALO - (CONV_K - 1) + j) for j in range(CONV_K))
CONV_BWD_TAPS = tuple((j, (CONV_K - 1) - j) for j in range(CONV_K))


def _tap_conv(src_ref, w_ref, row0, taps, lanes):
    acc = None
    for b in range(SUB):
        pb = None
        for j, off in taps:
            if off % SUB == b:
                lo = row0 + off - b
                term = w_ref[j:j + 1, lanes] * src_ref[lo:lo + CONV_ROWS + SUB, lanes]
                pb = term if pb is None else pb + term
        if pb is not None:
            sh = pb[b:b + CONV_ROWS, :]
            acc = sh if acc is None else acc + sh
    return acc


def _chunk_total(v):
    rows, cols = v.shape
    t = jnp.sum(v.reshape(rows // CHUNK, CHUNK, cols), axis=1, keepdims=True)
    return jnp.broadcast_to(t, (rows // CHUNK, CHUNK, cols)).reshape(rows, cols)


def _hgrn_prep(pq, pf, lb, lower):
    sq = _sig(pq)
    qf = pq * sq
    sf = _sig(pf)
    f = lb + (1.0 - lb) * sf
    logf = jnp.log(f)
    k = 1.0 - f
    G = _mm3(lower, logf)
    Gl = _chunk_total(logf)
    eG, enG, eGlG = jnp.exp(G), jnp.exp(-G), jnp.exp(Gl - G)
    return dict(sq=sq, sf=sf, f=f, Gl=Gl, eG=eG, enG=enG, eGlG=eGlG, qt=qf * eG, kt=k * enG, kh=k * eGlG)


def _ada_exchange(c_row, w_ada, b_sh, wdw_pad):
    ncol = w_ada.shape[1]

    def body(c_ref, w_ref, b_ref, wdw_ref, call_ref, c8_ref, modg_ref, wdwg_ref, rows_s, sa, ra, sw, rw, sm, rm):
        x, y, c = _place()
        me = 4 * x + 2 * y + c
        chip = 2 * x + y
        cv = c_ref[...]
        call_ref[me] = cv * _sig(cv)
        wdwg_ref[chip] = wdw_ref[...]
        sends = []
        for m in range(1, N_DEV):
            peer = (_flip(x, m >> 2), _flip(y, (m >> 1) & 1), _flip(c, m & 1))
            cp = _rcopy(call_ref.at[me], call_ref.at[me], sa.at[m - 1], ra.at[m - 1], peer)
            cp.start()
            sends.append(cp)
        for k, (fx, fy) in enumerate(OTHER_CHIPS):
            peer = (_flip(x, fx), _flip(y, fy), c)
            cp = _rcopy(wdwg_ref.at[chip], wdwg_ref.at[chip], sw.at[k], rw.at[k], peer)
            cp.start()
            sends.append(cp)
        for m in range(1, N_DEV):
            peer = (_flip(x, m >> 2), _flip(y, (m >> 1) & 1), _flip(c, m & 1))
            pid = 4 * peer[0] + 2 * peer[1] + peer[2]
            _rcopy(call_ref.at[pid], call_ref.at[pid], sa.at[m - 1], ra.at[m - 1], peer).wait_recv()
        for b in range(N_DEV):
            c8_ref[b:b + 1, :] = call_ref[b]
        mod_all = _dot(c8_ref[...].astype(BF16), w_ref[...].astype(BF16)) + b_ref[...]
        for b in range(N_DEV):
            rows_s[b] = mod_all[b:b + 1, :]
        modg_ref[chip] = rows_s[me]
        for k, (fx, fy) in enumerate(OTHER_CHIPS):
            peer = (_flip(x, fx), _flip(y, fy), c)
            pid = 4 * peer[0] + 2 * peer[1] + peer[2]
            cp = _rcopy(rows_s.at[pid], modg_ref.at[chip], sm.at[k], rm.at[k], peer)
            cp.start()
            sends.append(cp)
        for k, (fx, fy) in enumerate(OTHER_CHIPS):
            peer = (_flip(x, fx), _flip(y, fy), c)
            pchip = 2 * peer[0] + peer[1]
            _rcopy(rows_s.at[0], modg_ref.at[pchip], sm.at[k], rm.at[k], peer).wait_recv()
            _rcopy(wdwg_ref.at[pchip], wdwg_ref.at[pchip], sw.at[k], rw.at[k], peer).wait_recv()
        for cp in sends:
            cp.wait_send()

    vm = pl.BlockSpec(memory_space=pltpu.VMEM)
    return pl.pallas_call(
        body, name="ada_exchange",
        out_shape=[jax.ShapeDtypeStruct((N_DEV, 1, D_MODEL), F32), jax.ShapeDtypeStruct((N_DEV, D_MODEL), F32),
                   jax.ShapeDtypeStruct((N_CHIPS, 1, ncol), F32), jax.ShapeDtypeStruct((N_CHIPS, HALO, HEAD_D), F32)],
        in_specs=[vm] * 4, out_specs=[vm] * 4,
        scratch_shapes=[pltpu.VMEM((N_DEV, 1, ncol), F32),
                        pltpu.SemaphoreType.DMA((N_DEV - 1,)), pltpu.SemaphoreType.DMA((N_DEV - 1,)),
                        pltpu.SemaphoreType.DMA((3,)), pltpu.SemaphoreType.DMA((3,)),
                        pltpu.SemaphoreType.DMA((3,)), pltpu.SemaphoreType.DMA((3,))],
        compiler_params=_cp(None, 32),
    )(c_row, w_ada, b_sh, wdw_pad)


def _cast_own(chip1, shard, name):
    rows, cols = shard.shape
    tr = _row_tile(rows)

    def body(ch_ref, s_ref, o_ref):
        o_ref[0] = s_ref[...].astype(BF16)

    gs = pltpu.PrefetchScalarGridSpec(
        num_scalar_prefetch=1, grid=(rows // tr,),
        in_specs=[pl.BlockSpec((tr, cols), lambda i, ch: (i, 0))],
        out_specs=pl.BlockSpec((1, tr, cols), lambda i, ch: (ch[0], i, 0)))
    return pl.pallas_call(
        body, name=name, grid_spec=gs, out_shape=_big((N_CHIPS, rows, cols), BF16),
        compiler_params=_cp(("arbitrary",), 32),
    )(chip1, _hbm(shard))


def _slab(buf, ch, core):
    hs = buf.shape[1] // 2
    return buf.at[ch, pl.ds(core * hs, hs), :]


def _gather_start(bufs, ssem, rsem, relations=(0, 1, 2)):
    x, y, c = _place()
    chip = 2 * x + y
    for k in relations:
        fx, fy = OTHER_CHIPS[k]
        peer = (_flip(x, fx), _flip(y, fy), c)
        for t, buf in enumerate(bufs):
            _rcopy(_slab(buf, chip, c), _slab(buf, chip, c), ssem.at[t * 3 + k], rsem.at[t * 3 + k], peer).start()


def _gather_pass_on(bufs, ssem, rsem):
    nt = len(bufs)
    x, y, c = _place()
    sibling = (x, y, 1 - c)
    for k, (fx, fy) in enumerate(OTHER_CHIPS):
        peer = (_flip(x, fx), _flip(y, fy), c)
        pchip = 2 * peer[0] + peer[1]
        for t, buf in enumerate(bufs):
            _rcopy(_slab(buf, pchip, c), _slab(buf, pchip, c), ssem.at[t * 3 + k], rsem.at[t * 3 + k], peer).wait_recv()
            _rcopy(_slab(buf, pchip, c), _slab(buf, pchip, c), ssem.at[3 * nt + t * 3 + k],
                   rsem.at[3 * nt + t * 3 + k], sibling).start()


def _gather_drain(bufs, ssem, rsem):
    nt = len(bufs)
    x, y, c = _place()
    chip = 2 * x + y
    sibling = (x, y, 1 - c)
    for k, (fx, fy) in enumerate(OTHER_CHIPS):
        peer = (_flip(x, fx), _flip(y, fy), c)
        pchip = 2 * peer[0] + peer[1]
        for t, buf in enumerate(bufs):
            _rcopy(_slab(buf, pchip, 1 - c), _slab(buf, pchip, 1 - c), ssem.at[3 * nt + t * 3 + k],
                   rsem.at[3 * nt + t * 3 + k], sibling).wait_recv()
            _rcopy(_slab(buf, chip, c), _slab(buf, chip, c), ssem.at[t * 3 + k], rsem.at[t * 3 + k], peer).wait_send()
            _rcopy(_slab(buf, pchip, c), _slab(buf, pchip, c), ssem.at[3 * nt + t * 3 + k],
                   rsem.at[3 * nt + t * 3 + k], sibling).wait_send()


def _gather_finish(bufs, ssem, rsem):
    _gather_pass_on(bufs, ssem, rsem)
    _gather_drain(bufs, ssem, rsem)


def _gather_arrive(bufs, k, ssem, rsem):
    nt = len(bufs)
    x, y, c = _place()
    fx, fy = OTHER_CHIPS[k]
    peer = (_flip(x, fx), _flip(y, fy), c)
    pchip = 2 * peer[0] + peer[1]
    for t, buf in enumerate(bufs):
        _rcopy(_slab(buf, pchip, c), _slab(buf, pchip, c), ssem.at[t * 3 + k], rsem.at[t * 3 + k], peer).wait_recv()
        _rcopy(_slab(buf, pchip, c), _slab(buf, pchip, c), ssem.at[3 * nt + t * 3 + k],
               rsem.at[3 * nt + t * 3 + k], (x, y, 1 - c)).start()
    for t, buf in enumerate(bufs):
        _rcopy(_slab(buf, pchip, 1 - c), _slab(buf, pchip, 1 - c), ssem.at[3 * nt + t * 3 + k],
               rsem.at[3 * nt + t * 3 + k], (x, y, 1 - c)).wait_recv()


def _gather_sends_done(bufs, ssem, rsem):
    nt = len(bufs)
    x, y, c = _place()
    chip = 2 * x + y
    for k, (fx, fy) in enumerate(OTHER_CHIPS):
        peer = (_flip(x, fx), _flip(y, fy), c)
        pchip = 2 * peer[0] + peer[1]
        for t, buf in enumerate(bufs):
            _rcopy(_slab(buf, chip, c), _slab(buf, chip, c), ssem.at[t * 3 + k], rsem.at[t * 3 + k], peer).wait_send()
            _rcopy(_slab(buf, pchip, c), _slab(buf, pchip, c), ssem.at[3 * nt + t * 3 + k],
                   rsem.at[3 * nt + t * 3 + k], (x, y, 1 - c)).wait_send()


def _ring_parts(buf):
    x, y, c = _place()
    ynb, xnb = (x, 1 - y, c), (1 - x, y, c)
    ychip, xchip, dchip = 2 * x + (1 - y), 2 * (1 - x) + y, 2 * (1 - x) + (1 - y)
    hs = buf.shape[1] // 2

    def piece(ch, q):
        return buf.at[ch, pl.ds(c * hs + q * (hs // 2), hs // 2), :]

    return ynb, xnb, ychip, xchip, dchip, piece


def _ring_start(bufs, ssem, rsem):
    x, y, c = _place()
    chip = 2 * x + y
    for t, buf in enumerate(bufs):
        ynb, xnb, _, _, _, _ = _ring_parts(buf)
        _rcopy(_slab(buf, chip, c), _slab(buf, chip, c), ssem.at[2 * t], rsem.at[2 * t], ynb).start()
        _rcopy(_slab(buf, chip, c), _slab(buf, chip, c), ssem.at[2 * t + 1], rsem.at[2 * t + 1], xnb).start()


def _ring_forward(bufs, ssem, rsem):
    nt = len(bufs)
    _, _, c = _place()
    for t, buf in enumerate(bufs):
        ynb, xnb, ychip, xchip, _, piece = _ring_parts(buf)
        _rcopy(_slab(buf, ychip, c), _slab(buf, ychip, c), ssem.at[2 * t], rsem.at[2 * t], ynb).wait_recv()
        _rcopy(piece(ychip, 0), piece(ychip, 0), ssem.at[2 * nt + 2 * t], rsem.at[2 * nt + 2 * t], xnb).start()
        _rcopy(_slab(buf, xchip, c), _slab(buf, xchip, c), ssem.at[2 * t + 1], rsem.at[2 * t + 1], xnb).wait_recv()
        _rcopy(piece(xchip, 1), piece(xchip, 1), ssem.at[2 * nt + 2 * t + 1], rsem.at[2 * nt + 2 * t + 1], ynb).start()


def _ring_finish(bufs, ssem, rsem):
    nt = len(bufs)
    x, y, c = _place()
    chip = 2 * x + y
    sibling = (x, y, 1 - c)
    for t, buf in enumerate(bufs):
        ynb, xnb, ychip, xchip, dchip, piece = _ring_parts(buf)
        _rcopy(piece(dchip, 0), piece(dchip, 0), ssem.at[2 * nt + 2 * t], rsem.at[2 * nt + 2 * t], xnb).wait_recv()
        _rcopy(piece(dchip, 1), piece(dchip, 1), ssem.at[2 * nt + 2 * t + 1], rsem.at[2 * nt + 2 * t + 1],
               ynb).wait_recv()
        for k, ch in enumerate((ychip, xchip, dchip)):
            _rcopy(_slab(buf, ch, c), _slab(buf, ch, c), ssem.at[4 * nt + 3 * t + k], rsem.at[4 * nt + 3 * t + k],
                   sibling).start()
    for t, buf in enumerate(bufs):
        ynb, xnb, ychip, xchip, dchip, piece = _ring_parts(buf)
        for k, ch in enumerate((ychip, xchip, dchip)):
            _rcopy(_slab(buf, ch, 1 - c), _slab(buf, ch, 1 - c), ssem.at[4 * nt + 3 * t + k],
                   rsem.at[4 * nt + 3 * t + k], sibling).wait_recv()
            _rcopy(_slab(buf, ch, c), _slab(buf, ch, c), ssem.at[4 * nt + 3 * t + k], rsem.at[4 * nt + 3 * t + k],
                   sibling).wait_send()
        _rcopy(_slab(buf, chip, c), _slab(buf, chip, c), ssem.at[2 * t], rsem.at[2 * t], ynb).wait_send()
        _rcopy(_slab(buf, chip, c), _slab(buf, chip, c), ssem.at[2 * t + 1], rsem.at[2 * t + 1], xnb).wait_send()
        _rcopy(piece(ychip, 0), piece(ychip, 0), ssem.at[2 * nt + 2 * t], rsem.at[2 * nt + 2 * t], xnb).wait_send()
        _rcopy(piece(xchip, 1), piece(xchip, 1), ssem.at[2 * nt + 2 * t + 1], rsem.at[2 * nt + 2 * t + 1],
               ynb).wait_send()


def _ring_sems(nt):
    return [pltpu.SemaphoreType.DMA((7 * nt,)), pltpu.SemaphoreType.DMA((7 * nt,))]


def _gather_sems(nt):
    return [pltpu.SemaphoreType.DMA((6 * nt,)), pltpu.SemaphoreType.DMA((6 * nt,))]


def _pair_copies(ins, outs, ssem, rsem):
    x, y, c = _place()
    copies = []
    for t in range(len(ins)):
        hs = ins[t].shape[1] // 2
        copies.append(_rcopy(ins[t].at[:, pl.ds((1 - c) * hs, hs), :], outs[t], ssem.at[t], rsem.at[t], (x, y, 1 - c)))
    return copies


def _pair_shapes(grads):
    return [_big((g.shape[0], g.shape[1] // 2, g.shape[2]), g.dtype) for g in grads]


def _pair_sems(nt):
    return [pltpu.SemaphoreType.DMA((nt,)), pltpu.SemaphoreType.DMA((nt,))]


def _pair_swap(grads, name):
    nt = len(grads)
    hbm = pl.BlockSpec(memory_space=pl.ANY)

    def body(*refs):
        copies = _pair_copies(refs[:nt], refs[nt:2 * nt], refs[2 * nt], refs[2 * nt + 1])
        for cp in copies:
            cp.start()
        for cp in copies:
            cp.wait_recv()
        for cp in copies:
            cp.wait_send()

    return pl.pallas_call(
        body, name=name, out_shape=_pair_shapes(grads), in_specs=[hbm] * nt, out_specs=[hbm] * nt,
        scratch_shapes=_pair_sems(nt),
    )(*[_hbm(g) for g in grads])


def _xchg_copies(ins, outs, ssem, rsem):
    x, y, c = _place()
    copies = []
    for k, (fx, fy) in enumerate(OTHER_CHIPS):
        peer = (_flip(x, fx), _flip(y, fy), c)
        for t in range(len(ins)):
            copies.append(_rcopy(ins[t].at[k], outs[t].at[k], ssem.at[t * 3 + k], rsem.at[t * 3 + k], peer))
    return copies


def _xchg_sems(nt):
    return [pltpu.SemaphoreType.DMA((3 * nt,)), pltpu.SemaphoreType.DMA((3 * nt,))]


def _final_exchange(fulls, stats):
    nt = len(fulls)
    rows, cols = stats.shape
    hbm = pl.BlockSpec(memory_space=pl.ANY)
    vm = pl.BlockSpec(memory_space=pltpu.VMEM)

    def body(*refs):
        ins, s_ref = refs[:nt], refs[nt]
        outs, g_ref = refs[nt + 1:2 * nt + 1], refs[2 * nt + 1]
        hssem, hrsem, ssem, rsem = refs[2 * nt + 2:]
        x, y, c = _place()
        me, sibling = (x, y, c), (x, y, 1 - c)
        halves = []
        for t in range(nt):
            hs = ins[t].shape[0] // 2
            mine = pl.ds(c * hs, hs)
            cp = _rcopy(ins[t].at[mine, :], outs[t].at[mine, :], hssem.at[t], hrsem.at[t], sibling)
            cp.start()
            halves.append(cp)

        chips = [(_flip(x, fx), _flip(y, fy)) for fx, fy in OTHER_CHIPS]

        def blk(px, py, pc):
            return g_ref.at[4 * px + 2 * py + pc]

        def copy(k, block, to, src=None):
            return _rcopy(blk(*block) if src is None else src, blk(*block), ssem.at[k], rsem.at[k], to)

        g_ref[4 * x + 2 * y + c] = s_ref[...]
        first = [copy(0, me, sibling, src=s_ref)]
        first += [copy(1 + j, me, (*chip, c), src=s_ref) for j, chip in enumerate(chips)]
        for cp in first:
            cp.start()
        passed = [copy(4 + j, (*chip, c), sibling) for j, chip in enumerate(chips)]
        for j, chip in enumerate(chips):
            copy(1 + j, (*chip, c), me).wait_recv()
            passed[j].start()
        copy(0, sibling, me).wait_recv()
        for j, chip in enumerate(chips):
            copy(4 + j, (*chip, 1 - c), me).wait_recv()
        for t in range(nt):
            hs = ins[t].shape[0] // 2
            other = pl.ds((1 - c) * hs, hs)
            _rcopy(ins[t].at[other, :], outs[t].at[other, :], hssem.at[t], hrsem.at[t], sibling).wait_recv()
        for cp in first + passed + halves:
            cp.wait_send()

    outs = pl.pallas_call(
        body, name="final_exchange",
        out_shape=[_big(f.shape, F32) for f in fulls] + [jax.ShapeDtypeStruct((N_DEV, rows, cols), F32)],
        in_specs=[hbm] * nt + [vm], out_specs=[hbm] * nt + [vm],
        input_output_aliases={t: t for t in range(nt)},
        scratch_shapes=[pltpu.SemaphoreType.DMA((nt,)), pltpu.SemaphoreType.DMA((nt,)),
                        pltpu.SemaphoreType.DMA((7,)), pltpu.SemaphoreType.DMA((7,))],
        compiler_params=_cp(None, 32),
    )(*[_hbm(f) for f in fulls], stats)
    return outs[:nt], outs[nt]


def _row_tile(rows):
    return min(rows, 512)


def _pair_sum(place, grad, got, name):
    nb, hs, cols = got.shape
    tr = _row_tile(hs)
    nr = hs // tr

    def body(pl_ref, g_ref, o_ref, pf_ref, pb_ref):
        j = pl.program_id(1)
        s = g_ref[0] + o_ref[0].astype(F32)

        @pl.when(j == 0)
        def _():
            pf_ref[...] = s

        @pl.when(j > 0)
        def _():
            pb_ref[0] = s.astype(BF16)

    gs = pltpu.PrefetchScalarGridSpec(
        num_scalar_prefetch=1, grid=(nr, nb),
        in_specs=[pl.BlockSpec((1, tr, cols), lambda i, j, p: (p[1] ^ j, p[0] * nr + i, 0)),
                  pl.BlockSpec((1, tr, cols), lambda i, j, p: (p[1] ^ j, i, 0))],
        out_specs=[pl.BlockSpec((tr, cols), lambda i, j, p: (i, 0)),
                   pl.BlockSpec((1, tr, cols), lambda i, j, p: (jnp.maximum(j - 1, 0), i, 0))])
    return pl.pallas_call(
        body, name=name, grid_spec=gs,
        out_shape=[_big((hs, cols), F32), _big((nb - 1, hs, cols), BF16)],
        compiler_params=_cp(("arbitrary", "arbitrary"), 32),
    )(place, *_hbm(grad, got))


def _chip_sum(place, pair_f, got_b, name):
    nb, hs, cols = got_b.shape
    tr = _row_tile(hs)
    nr = hs // tr

    def body(pl_ref, pf_ref, gb_ref, o_ref):
        acc = pf_ref[...]
        for k in range(nb):
            acc = acc + gb_ref[k].astype(F32)
        o_ref[...] = acc

    gs = pltpu.PrefetchScalarGridSpec(
        num_scalar_prefetch=1, grid=(nr,),
        in_specs=[pl.BlockSpec((tr, cols), lambda i, p: (i, 0)),
                  pl.BlockSpec((nb, tr, cols), lambda i, p: (0, i, 0))],
        out_specs=pl.BlockSpec((tr, cols), lambda i, p: (p[0] * nr + i, 0)))
    return pl.pallas_call(
        body, name=name, grid_spec=gs,
        out_shape=_big((2 * hs, cols), F32),
        compiler_params=_cp(("arbitrary",), 32),
    )(place, *_hbm(pair_f, got_b))


def _adam_math(w, g, m, v):
    m2 = ADAM_B1 * m + (1.0 - ADAM_B1) * g
    v2 = ADAM_B2 * v + (1.0 - ADAM_B2) * (g * g)
    m_hat = m2 / (1.0 - ADAM_B1 ** ADAM_STEP)
    v_hat = v2 / (1.0 - ADAM_B2 ** ADAM_STEP)
    delta = -ADAM_LR * (m_hat / (jnp.sqrt(v_hat) + ADAM_EPS) + ADAM_WD * w)
    return delta, m2, v2


def _adam_big(w, g, m, v, name):
    rows, cols = w.shape
    tr = _row_tile(rows)

    def body(w_ref, g_ref, m_ref, v_ref, d_out, m_out, v_out):
        d, m2, v2 = _adam_math(w_ref[...], g_ref[...], m_ref[...], v_ref[...])
        d_out[...] = d
        m_out[...] = m2
        v_out[...] = v2

    spec = pl.BlockSpec((tr, cols), lambda i: (i, 0))
    return pl.pallas_call(
        body, name=name, grid=(rows // tr,), in_specs=[spec] * 4, out_specs=[spec] * 3,
        out_shape=[_big(w.shape, F32)] * 3,
        compiler_params=_cp(("arbitrary",), 48),
    )(*_hbm(w, g, m, v))


def _ada_grad_adam(c8t, dmod_sh, w, m, v):
    rows, cols = w.shape
    tr = _row_tile(rows) // 2

    def body(ct_ref, dm_ref, w_ref, m_ref, v_ref, g_out, d_out, m_out, v_out):
        g = None
        for b in range(N_DEV):
            term = ct_ref[:, b:b + 1] * dm_ref[b:b + 1, :]
            g = term if g is None else g + term
        d, m2, v2 = _adam_math(w_ref[...], g, m_ref[...], v_ref[...])
        g_out[...] = g
        d_out[...] = d
        m_out[...] = m2
        v_out[...] = v2

    spec = pl.BlockSpec((tr, cols), lambda i: (i, 0))
    return pl.pallas_call(
        body, name="ada_grad_adam", grid=(rows // tr,),
        in_specs=[pl.BlockSpec((tr, N_DEV), lambda i: (i, 0)), _full((N_DEV, cols)), spec, spec, spec],
        out_specs=[spec] * 4, out_shape=[_big(w.shape, F32)] * 4,
        compiler_params=_cp(("arbitrary",), 32),
    )(c8t, dmod_sh, *_hbm(w, m, v))


def _tok_tile(t):
    return min(TOK_TILE, t)


def _mix_in_fwd(chip1, x, modr, g_pre, b_in4, w_in_buf, w_out_buf):
    T = x.shape[0]
    tm = min(MIXIN_TILE, T)
    nt = T // tm
    nb = IN_COLS // N_CHIPS

    def body(ch_ref, x_ref, mod_ref, g_ref, b_ref, win_in, wout_in, p_ref, h_ref, win_ref, wout_ref,
             h_all, wblk, lsem, is_sem, ir_sem, os_sem, or_sem):
        k, i = pl.program_id(0), pl.program_id(1)
        chip = ch_ref[0]

        def load_block(blk):
            cp = pltpu.make_async_copy(win_ref.at[blk], wblk, lsem)
            cp.start()
            cp.wait()

        @pl.when((k == 0) & (i == 0))
        def _():
            _gather_start([win_ref], is_sem, ir_sem, relations=(0, 1))
            load_block(chip)

        for r in range(N_CHIPS - 1):
            @pl.when((k == r + 1) & (i == 0))
            def _(r=r):
                _gather_arrive([win_ref], r, is_sem, ir_sem)
                if r == 0:
                    _gather_start([win_ref], is_sem, ir_sem, relations=(2,))
                if r == 1:
                    _gather_start([wout_ref], os_sem, or_sem)
                load_block(chip ^ (r + 1))

        rows = pl.ds(pl.multiple_of(i * tm, tm), tm)

        @pl.when(k == 0)
        def _():
            xv = x_ref[...]
            rstd = lax.rsqrt(jnp.mean(xv * xv, axis=-1, keepdims=True) + RMS_EPS)
            h = (xv * rstd) * g_ref[...] * (1.0 + mod_ref[1:2, :]) + mod_ref[0:1, :]
            hb = h.astype(BF16)
            h_ref[...] = hb
            h_all[rows, :] = hb

        p_ref[...] = _dot(h_all[rows, :], wblk[...]) + b_ref[chip ^ k]

        @pl.when((k == N_CHIPS - 1) & (i == nt - 1))
        def _():
            _gather_sends_done([win_ref], is_sem, ir_sem)
            _gather_finish([wout_ref], os_sem, or_sem)

    hbm = pl.BlockSpec(memory_space=pl.ANY)
    first_pass = lambda k, i, ch: (jnp.where(k == 0, i, nt - 1), 0)
    gs = pltpu.PrefetchScalarGridSpec(
        num_scalar_prefetch=1, grid=(N_CHIPS, nt),
        in_specs=[pl.BlockSpec((tm, D_MODEL), first_pass), pl.BlockSpec((6, D_MODEL), lambda k, i, ch: (0, 0)),
                  pl.BlockSpec((1, D_MODEL), lambda k, i, ch: (0, 0)),
                  pl.BlockSpec((N_CHIPS, 1, nb), lambda k, i, ch: (0, 0, 0)), hbm, hbm],
        out_specs=[pl.BlockSpec((tm, nb), lambda k, i, ch: (i, ch[0] ^ k)), pl.BlockSpec((tm, D_MODEL), first_pass),
                   hbm, hbm],
        scratch_shapes=[pltpu.VMEM((T, D_MODEL), BF16), pltpu.VMEM((D_MODEL, nb), BF16), pltpu.SemaphoreType.DMA]
        + _gather_sems(1) + _gather_sems(1))
    return pl.pallas_call(
        body, name="mix_in_fwd", grid_spec=gs,
        out_shape=[_big((T, IN_COLS), F32), _big((T, D_MODEL), BF16), _big(w_in_buf.shape, BF16),
                   _big(w_out_buf.shape, BF16)],
        input_output_aliases={5: 2, 6: 3},
        compiler_params=_cp(("arbitrary", "arbitrary"), 48),
    )(chip1, _hbm(x), modr, g_pre, b_in4, _hbm(w_in_buf), _hbm(w_out_buf))


def _mixers_fwd(p, wdw, vecs, lbl, w_out, x, modr, g_post, g_ffn, gbufs):
    T = p.shape[0]
    tm = _tok_tile(T)
    nt = T // tm
    nch = tm // CHUNK
    ng = len(gbufs)
    n_in, n_out = 12, 7

    def body(*refs):
        (p_ref, wdw_ref, bdw_ref, gain_ref, bias_ref, gout_ref, lbl_ref, wout_ref, x_ref, mod_ref, gp_ref,
         gf_ref) = refs[:n_in]
        cat_ref, ys_ref, o_ref, st_ref, y_ref, x1_ref, h2_ref = refs[n_in + ng:n_in + ng + n_out]
        gout_bufs = refs[n_in + ng + n_out:n_in + 2 * ng + n_out]
        (ubuf, state, qt_s, kt_s, kh_s, v_s, egl_s, lower_s, gmat_s, gssem,
         grsem) = refs[n_in + 2 * ng + n_out:]
        i = pl.program_id(0)

        @pl.when(i == 0)
        def _():
            _ring_start(gout_bufs, gssem, grsem)
            lower_s[...], _ = _chunk_masks(tm)
            gmat_s[...] = _gn_matrix()
            state[...] = jnp.zeros(state.shape, F32)
            ubuf[0:HALO, :] = jnp.zeros((HALO, CONV_CH), F32)
            ubuf[HALO + tm:HALO + tm + SUB, :] = jnp.zeros((SUB, CONV_CH), F32)

        @pl.when(i > 0)
        def _():
            ubuf[0:HALO, :] = ubuf[tm:tm + HALO, :]

        ubuf[HALO:HALO + tm, :] = p_ref[:, 0:CONV_CH] * _sig(p_ref[:, CONV_CH:2 * CONV_CH])
        for r in range(tm // CONV_ROWS):
            rows = slice(r * CONV_ROWS, (r + 1) * CONV_ROWS)
            for lb_ in range(CONV_CH // LANE):
                lanes = slice(lb_ * LANE, (lb_ + 1) * LANE)
                ys_ref[rows, lanes] = bdw_ref[:, lanes] + _tap_conv(ubuf, wdw_ref, r * CONV_ROWS, CONV_FWD_TAPS, lanes)
        gmat = gmat_s[...]
        yv = ys_ref[...]
        d = yv - _gmean(yv, gmat)
        rs = lax.rsqrt(_gmean(d * d, gmat) + GN_EPS)
        z = d * rs * gain_ref[...] + bias_ref[...]
        cat_ref[:, 0:CONV_CH] = (z * _sig(z)).astype(BF16)

        lb, _ = _lower_bound(lbl_ref)
        o0 = 2 * CONV_CH
        pr = _hgrn_prep(p_ref[:, o0:o0 + HGRN_W], p_ref[:, o0 + HGRN_W:o0 + 2 * HGRN_W], lb, lower_s[...])
        qt_s[...] = pr["qt"].astype(BF16)
        kt_s[...] = pr["kt"].astype(BF16)
        kh_s[...] = pr["kh"].astype(BF16)
        v_s[...] = p_ref[:, o0 + 2 * HGRN_W:o0 + 3 * HGRN_W].astype(BF16)
        egl_s[...] = jnp.exp(pr["Gl"])
        tri = _tri()

        def chunk(ci, carry):
            r0 = pl.multiple_of(ci * CHUNK, CHUNK)
            rows = pl.ds(r0, CHUNK)
            for h in range(N_HEADS):
                ls = pl.ds(h * HEAD_D, HEAD_D)
                qc, kc, hc, vc = qt_s[rows, ls], kt_s[rows, ls], kh_s[rows, ls], v_s[rows, ls]
                s0 = state[h]
                s0b = s0.astype(BF16)
                st_ref[ci, h] = s0
                att =jnp.where(tri, _dot_nt(qc, kc), 0.0).astype(BF16)
                o_ref[rows, ls] = _dot(att, vc) + _dot_nt(qc, s0b)
                state[h] = s0 * egl_s[pl.ds(r0, 1), ls] + _dot_tn(vc, hc)
            return carry

        lax.fori_loop(0, nch, chunk, 0, unroll=min(CHUNK_UNROLL, nch))
        for h in range(N_HEADS):
            sl = slice(h * HEAD_D, (h + 1) * HEAD_D)
            oh = o_ref[:, sl]
            gh = p_ref[:, o0 + 3 * HGRN_W + h * HEAD_D:o0 + 3 * HGRN_W + (h + 1) * HEAD_D]
            rsh = lax.rsqrt(jnp.mean(oh * oh, axis=-1, keepdims=True) + RMS_EPS)
            hg = (oh * rsh) * gout_ref[:, sl] * (gh * _sig(gh))
            cat_ref[:, CONV_CH + h * HEAD_D:CONV_CH + (h + 1) * HEAD_D] = hg.astype(BF16)

        yv = _dot(cat_ref[...], wout_ref[...])
        y_ref[...] = yv
        rsy = lax.rsqrt(jnp.mean(yv * yv, axis=-1, keepdims=True) + RMS_EPS)
        x1 = x_ref[...] + mod_ref[2:3, :] * ((yv * rsy) * gp_ref[...])
        x1_ref[...] = x1
        rs1 = lax.rsqrt(jnp.mean(x1 * x1, axis=-1, keepdims=True) + RMS_EPS)
        h2 = (x1 * rs1) * gf_ref[...] * (1.0 + mod_ref[4:5, :]) + mod_ref[3:4, :]
        h2_ref[...] = h2.astype(BF16)

        @pl.when(i == min(nt - 1, nt // 2 + 1))
        def _():
            _ring_forward(gout_bufs, gssem, grsem)

        @pl.when(i == nt - 1)
        def _():
            _ring_finish(gout_bufs, gssem, grsem)

    tile = lambda cols: pl.BlockSpec((tm, cols), lambda i: (i, 0))
    hbm = pl.BlockSpec(memory_space=pl.ANY)
    outs = pl.pallas_call(
        body, name="mixers_fwd", grid=(nt,),
        in_specs=[tile(IN_COLS), _full((HALO, CONV_CH))] + [_full((1, CONV_CH))] * 4 + [_full((2, HGRN_W))]
        + [_full((D_MODEL, D_MODEL)), tile(D_MODEL), _full((6, D_MODEL)), _full((1, D_MODEL)), _full((1, D_MODEL))]
        + [hbm] * ng,
        out_specs=[tile(D_MODEL), tile(CONV_CH), tile(HGRN_W),
                   pl.BlockSpec((nch, N_HEADS, HEAD_D, HEAD_D), lambda i: (i, 0, 0, 0)),
                   tile(D_MODEL), tile(D_MODEL), tile(D_MODEL)] + [hbm] * ng,
        out_shape=[_big((T, D_MODEL), BF16), _big((T, CONV_CH), F32), _big((T, HGRN_W), F32),
                   _big((T // CHUNK, N_HEADS, HEAD_D, HEAD_D), F32), _big((T, D_MODEL), F32),
                   _big((T, D_MODEL), F32), _big((T, D_MODEL), BF16)] + [_big(b.shape, BF16) for b in gbufs],
        input_output_aliases={n_in + t: n_out + t for t in range(ng)},
        scratch_shapes=[pltpu.VMEM((tm + HALO + SUB, CONV_CH), F32), pltpu.VMEM((N_HEADS, HEAD_D, HEAD_D), F32),
                        pltpu.VMEM((tm, HGRN_W), BF16), pltpu.VMEM((tm, HGRN_W), BF16),
                        pltpu.VMEM((tm, HGRN_W), BF16), pltpu.VMEM((tm, HGRN_W), BF16),
                        pltpu.VMEM((tm, HGRN_W), F32), pltpu.VMEM((tm, tm), BF16),
                        pltpu.VMEM((CONV_CH, CONV_CH), BF16)] + _ring_sems(ng),
        compiler_params=_cp(("arbitrary",), 56),
    )(_hbm(p), wdw, *vecs, lbl, *_hbm(w_out, x), modr, g_post, g_ffn, *[_hbm(b) for b in gbufs])
    return outs[:n_out], outs[n_out:]


def _row_chains(rows, n=2):
    step = rows // n
    return [slice(k * step, (k + 1) * step) for k in range(n)]


def _ffn_blocks():
    return D_FF // FFN_BLOCK, (D_FF // N_CHIPS) // FFN_BLOCK


def _ffn_fwd(h2, w_up_g, w_down, x1, target, modr, g_post):
    T = h2.shape[0]
    tm = min(FFN_TILE, T)
    fb = FFN_BLOCK
    nj, per = _ffn_blocks()

    def body(h_ref, wu_ref, wd_ref, x1_ref, t_ref, mod_ref, g_ref, r_ref, dy2_ref, dx2_ref, st_ref, acc):
        i, j = pl.program_id(0), pl.program_id(1)

        @pl.when((i == 0) & (j == 0))
        def _():
            st_ref[...] = jnp.zeros(st_ref.shape, F32)

        @pl.when(j == 0)
        def _():
            acc[...] = jnp.zeros(acc.shape, F32)

        for rows in _row_chains(tm):
            ra = jnp.maximum(_dot(h_ref[rows, :], wu_ref[0]), 0.0)
            rb = (ra * ra).astype(BF16)
            r_ref[rows, :] = rb
            acc[rows, :] += _dot(rb, wd_ref[...])

        @pl.when(j == nj - 1)
        def _():
            y2 = acc[...]
            rs = lax.rsqrt(jnp.mean(y2 * y2, axis=-1, keepdims=True) + RMS_EPS)
            nh = y2 * rs
            gp = g_ref[...]
            err = x1_ref[...] + mod_ref[5:6, :] * (nh * gp) - t_ref[...]
            dx2 = err * (1.0 / D_MODEL)
            dx2_ref[...] = dx2
            st_ref[0:1, :] += _colsum(err * err)
            st_ref[1:2, :] += _colsum(dx2 * (nh * gp))
            dn = dx2 * mod_ref[5:6, :]
            st_ref[2:3, :] += _colsum(dn * nh)
            dy2_ref[...] = _rms_bwd(dn * gp, nh, rs).astype(BF16)

    tile = pl.BlockSpec((tm, D_MODEL), lambda i, j: (i, 0))
    return pl.pallas_call(
        body, name="ffn_fwd", grid=(T // tm, nj),
        in_specs=[tile, pl.BlockSpec((1, D_MODEL, fb), lambda i, j: (j // per, 0, j % per)),
                  pl.BlockSpec((fb, D_MODEL), lambda i, j: (j, 0)), tile, tile,
                  _full((6, D_MODEL)), _full((1, D_MODEL))],
        out_specs=[pl.BlockSpec((tm, fb), lambda i, j: (i, j)), tile, tile, _full((8, D_MODEL))],
        out_shape=[_big((T, D_FF), BF16), _big((T, D_MODEL), BF16), _big((T, D_MODEL), F32),
                   jax.ShapeDtypeStruct((8, D_MODEL), F32)],
        scratch_shapes=[pltpu.VMEM((tm, D_MODEL), F32)],
        compiler_params=_cp(("arbitrary", "arbitrary"), 56),
    )(*_hbm(h2, w_up_g, w_down, x1, target), modr, g_post)


def _rms_bwd(dxn, xn, rs):
    return rs * (dxn - xn * jnp.mean(dxn * xn, axis=-1, keepdims=True))


def _ffn_bwd(dy2, r, x1, dx2, w_up_g, w_down, modr, g_ffn):
    T = dx2.shape[0]
    tm = min(FFN_TILE, T)
    fb = FFN_BLOCK
    nj, per = _ffn_blocks()

    def body(dy2_ref, r_ref, x1_ref, dx2_ref, wu_ref, wd_ref, mod_ref, gf_ref, da_ref, dx1_ref, st_ref, dh_s):
        i, j = pl.program_id(0), pl.program_id(1)

        @pl.when((i == 0) & (j == 0))
        def _():
            st_ref[...] = jnp.zeros(st_ref.shape, F32)

        @pl.when(j == 0)
        def _():
            dh_s[...] = jnp.zeros(dh_s.shape, F32)

        for rows in _row_chains(tm):
            ra = jnp.sqrt(r_ref[rows, :].astype(F32))
            da = (_dot_nt(dy2_ref[rows, :], wd_ref[...]) * (2.0 * ra)).astype(BF16)
            da_ref[rows, :] = da
            dh_s[rows, :] += _dot_nt(da, wu_ref[0])

        @pl.when(j == nj - 1)
        def _():
            dh = dh_s[...]
            x1v = x1_ref[...]
            rs1 = lax.rsqrt(jnp.mean(x1v * x1v, axis=-1, keepdims=True) + RMS_EPS)
            xn = x1v * rs1
            st_ref[0:1, :] += _colsum(dh)
            st_ref[1:2, :] += _colsum(dh * (xn * gf_ref[...]))
            dsc = dh * (1.0 + mod_ref[4:5, :])
            st_ref[2:3, :] += _colsum(dsc * xn)
            dx1_ref[...] = dx2_ref[...] + _rms_bwd(dsc * gf_ref[...], xn, rs1)

    tile = pl.BlockSpec((tm, D_MODEL), lambda i, j: (i, 0))
    ftile = pl.BlockSpec((tm, fb), lambda i, j: (i, j))
    return pl.pallas_call(
        body, name="ffn_bwd", grid=(T // tm, nj),
        in_specs=[tile, ftile, tile, tile, pl.BlockSpec((1, D_MODEL, fb), lambda i, j: (j // per, 0, j % per)),
                  pl.BlockSpec((fb, D_MODEL), lambda i, j: (j, 0)), _full((6, D_MODEL)), _full((1, D_MODEL))],
        out_specs=[ftile, tile, _full((8, D_MODEL))],
        out_shape=[_big((T, D_FF), BF16), _big((T, D_MODEL), F32), jax.ShapeDtypeStruct((8, D_MODEL), F32)],
        scratch_shapes=[pltpu.VMEM((tm, D_MODEL), F32)],
        compiler_params=_cp(("arbitrary", "arbitrary"), 56),
    )(*_hbm(dy2, r, x1, dx2, w_up_g, w_down), modr, g_ffn)


def _mix_out_bwd(dx1, y, cat, w_out, modr, g_post, swap):
    T = dx1.shape[0]
    tm = _tok_tile(T)
    nt = T // tm
    ns = len(swap)

    def body(*refs):
        dx1_ref, y_ref, cat_ref, w_ref, mod_ref, gp_ref = refs[:6]
        s_ins = refs[6:6 + ns]
        dcat_ref, st_ref, gw_ref = refs[6 + ns:9 + ns]
        s_outs = refs[9 + ns:9 + 2 * ns]
        gacc, gsem, pssem, prsem = refs[9 + 2 * ns:]
        i = pl.program_id(0)

        @pl.when(i == 0)
        def _():
            for cp in _pair_copies(s_ins, s_outs, pssem, prsem):
                cp.start()
            st_ref[...] = jnp.zeros(st_ref.shape, F32)
            gacc[...] = jnp.zeros(gacc.shape, F32)

        dxv, yv = dx1_ref[...], y_ref[...]
        rs = lax.rsqrt(jnp.mean(yv * yv, axis=-1, keepdims=True) + RMS_EPS)
        nh = yv * rs
        st_ref[0:1, :] += _colsum(dxv * (nh * gp_ref[...]))
        dn = dxv * mod_ref[2:3, :]
        st_ref[1:2, :] += _colsum(dn * nh)
        dy = _rms_bwd(dn * gp_ref[...], nh, rs).astype(BF16)
        dcat_ref[...] = _dot_nt(dy, w_ref[...])
        for cols in _row_chains(D_MODEL):
            gacc[:, cols] += _dot_tn(cat_ref[...], dy[:, cols])

        @pl.when(i == nt - 1)
        def _():
            out = pltpu.make_async_copy(gacc, gw_ref, gsem)
            out.start()
            copies = _pair_copies(s_ins, s_outs, pssem, prsem)
            for cp in copies:
                cp.wait_recv()
            for cp in copies:
                cp.wait_send()
            out.wait()

    tile = pl.BlockSpec((tm, D_MODEL), lambda i: (i, 0))
    hbm = pl.BlockSpec(memory_space=pl.ANY)
    outs = pl.pallas_call(
        body, name="mix_out_bwd", grid=(nt,),
        in_specs=[tile, tile, tile, _full((D_MODEL, D_MODEL)), _full((6, D_MODEL)), _full((1, D_MODEL))]
        + [hbm] * ns,
        out_specs=[tile, _full((8, D_MODEL)), hbm] + [hbm] * ns,
        out_shape=[_big((T, D_MODEL), F32), jax.ShapeDtypeStruct((8, D_MODEL), F32), _big((D_MODEL, D_MODEL), F32)]
        + _pair_shapes(swap),
        scratch_shapes=[pltpu.VMEM((D_MODEL, D_MODEL), F32), pltpu.SemaphoreType.DMA] + _pair_sems(ns),
        compiler_params=_cp(("arbitrary",), 48),
    )(*_hbm(dx1, y, cat, w_out), modr, g_post, *[_hbm(g) for g in swap])
    return outs[:3], outs[3:]


def _mixers_bwd(p, dcat, ys, o, states, h1, wdw, vecs, lbl, pairs_b):
    T = p.shape[0]
    tm = min(MIXB_TILE, T)
    nt = T // tm
    nch = tm // CHUNK
    hpt = tm // HALO
    nx = len(pairs_b)
    nb = IN_COLS // N_CHIPS

    def body(*refs):
        (p_ref, ph_ref, dcat_ref, ys_ref, o_ref, st_ref, h1_ref, wdw_ref, bdw_ref, gain_ref, bias_ref, gout_ref,
         lbl_ref) = refs[:13]
        x_ins = refs[13:13 + nx]
        dp_ref, sb_ref, s5_ref, dw_ref = refs[13 + nx:17 + nx]
        x_outs = refs[17 + nx:17 + 2 * nx]
        gin_ref = refs[17 + 2 * nx]
        (ubuf, dybuf, carry, dstate, qt_s, kt_s, kh_s, v_s, do_s, egl_s, dqt_s, dkt_s, dkh_s, dv_s, dgl_s,
         dsh, dw8, dshift, lower_s, upper_s, gmat_s, gacc, dp_prev, h1_prev, gsem, xssem,
         xrsem) = refs[18 + 2 * nx:]
        i = pl.program_id(0)
        tile_idx = nt - 1 - i

        @pl.when(i == 0)
        def _():
            for cp in _xchg_copies(x_ins, x_outs, xssem, xrsem):
                cp.start()
            gacc[...] = jnp.zeros(gacc.shape, F32)
            dstate[...] = jnp.zeros(dstate.shape, F32)
            carry[...] = jnp.zeros(carry.shape, F32)
            sb_ref[...] = jnp.zeros(sb_ref.shape, F32)
            s5_ref[...] = jnp.zeros(s5_ref.shape, F32)
            dw_ref[...] = jnp.zeros(dw_ref.shape, F32)
            dw8[...] = jnp.zeros(dw8.shape, F32)
            lower_s[...], upper_s[...] = _chunk_masks(tm)
            gmat_s[...] = _gn_matrix()
            dsh[0:SUB, :] = jnp.zeros((SUB, CONV_CH), F32)
            dsh[SUB + tm:2 * SUB + tm, :] = jnp.zeros((SUB, CONV_CH), F32)
            ubuf[HALO + tm:HALO + tm + SUB, :] = jnp.zeros((SUB, CONV_CH), F32)
            dp_prev[...] = jnp.zeros(dp_prev.shape, BF16)
            h1_prev[...] = jnp.zeros(h1_prev.shape, BF16)

        n_pieces = (tm // CONV_ROWS) * (CONV_CH // LANE)
        per_block = n_pieces // N_CHIPS
        prow = D_MODEL // per_block

        def w_in_grad_piece(k):
            j, part = k // per_block, k % per_block
            rows_k = slice(part * prow, (part + 1) * prow)
            gacc[j, rows_k, :] += _dot_tn(h1_prev[:, rows_k], dp_prev[:, j * nb:(j + 1) * nb])

        uh = ph_ref[:, 0:CONV_CH] * _sig(ph_ref[:, CONV_CH:2 * CONV_CH])
        ubuf[0:HALO, :] = jnp.where(tile_idx > 0, uh, 0.0)
        ubuf[HALO:HALO + tm, :] = p_ref[:, 0:CONV_CH] * _sig(p_ref[:, CONV_CH:2 * CONV_CH])
        gmat = gmat_s[...]
        gain = gain_ref[...]
        yv = ys_ref[...]
        d = yv - _gmean(yv, gmat)
        rs = lax.rsqrt(_gmean(d * d, gmat) + GN_EPS)
        yn = d * rs
        z = yn * gain + bias_ref[...]
        sz = _sig(z)
        dz = dcat_ref[:, 0:CONV_CH] * (sz * (1.0 + z * (1.0 - sz)))
        dyn = dz * gain
        dyc = rs * (dyn - _gmean(dyn, gmat) - yn * _gmean(dyn * yn, gmat))
        s5_ref[0:1, :] += _colsum(dyc)
        s5_ref[1:2, :] += _colsum(dz * yn)
        s5_ref[2:3, :] += _colsum(dz)
        dybuf[tm:tm + HALO, :] = carry[...]
        dybuf[0:tm, :] = dyc
        dsh[SUB:SUB + tm, :] = dyc
        carry[...] = dyc[0:HALO, :]
        for b in range(SUB):
            dshift[...] = dsh[SUB - b:2 * SUB - b + tm, :]
            for j, off in CONV_FWD_TAPS:
                if off % SUB == b:
                    prod = dshift[...] * ubuf[off - b:off - b + tm + SUB, :]
                    dw8[j] += jnp.sum(prod.reshape((tm + SUB) // SUB, SUB, CONV_CH), axis=0)
        for r in range(tm // CONV_ROWS):
            rows = slice(r * CONV_ROWS, (r + 1) * CONV_ROWS)
            for lb_ in range(CONV_CH // LANE):
                lanes = slice(lb_ * LANE, (lb_ + 1) * LANE)
                glanes = slice(CONV_CH + lb_ * LANE, CONV_CH + (lb_ + 1) * LANE)
                w_in_grad_piece(r * (CONV_CH // LANE) + lb_)
                acc = _tap_conv(dybuf, wdw_ref, r * CONV_ROWS, CONV_BWD_TAPS, lanes)
                val = p_ref[rows, lanes]
                sg = _sig(p_ref[rows, glanes])
                dval = acc * sg
                dgate = acc * val * (sg * (1.0 - sg))
                dp_ref[rows, lanes] = dval.astype(BF16)
                dp_ref[rows, glanes] = dgate.astype(BF16)
                sb_ref[0:1, lanes] += _colsum(dval)
                sb_ref[0:1, glanes] += _colsum(dgate)

        o0 = 2 * CONV_CH
        for h in range(N_HEADS):
            sl = slice(h * HEAD_D, (h + 1) * HEAD_D)
            gsl = slice(o0 + 3 * HGRN_W + h * HEAD_D, o0 + 3 * HGRN_W + (h + 1) * HEAD_D)
            oh = o_ref[:, sl]
            gh = p_ref[:, gsl]
            dh = dcat_ref[:, CONV_CH + h * HEAD_D:CONV_CH + (h + 1) * HEAD_D]
            gout = gout_ref[:, sl]
            rsh = lax.rsqrt(jnp.mean(oh * oh, axis=-1, keepdims=True) + RMS_EPS)
            on = oh * rsh
            sgg = _sig(gh)
            dgh = dh * (on * gout) * (sgg * (1.0 + gh * (1.0 - sgg)))
            dm = dh * (gh * sgg)
            s5_ref[3:4, sl] += _colsum(dm * on)
            do_s[:, sl] = _rms_bwd(dm * gout, on, rsh).astype(BF16)
            dp_ref[:, gsl] = dgh.astype(BF16)
            sb_ref[2:3, CONV_CH + h * HEAD_D:CONV_CH + (h + 1) * HEAD_D] += _colsum(dgh)

        lb, _ = _lower_bound(lbl_ref)
        pq = p_ref[:, o0:o0 + HGRN_W]
        pr = _hgrn_prep(pq, p_ref[:, o0 + HGRN_W:o0 + 2 * HGRN_W], lb, lower_s[...])
        qt_s[...] = pr["qt"].astype(BF16)
        kt_s[...] = pr["kt"].astype(BF16)
        kh_s[...] = pr["kh"].astype(BF16)
        v_s[...] = p_ref[:, o0 + 2 * HGRN_W:o0 + 3 * HGRN_W].astype(BF16)
        egl_s[...] = jnp.exp(pr["Gl"])
        tri = _tri()

        def chunk(it, c_):
            ci = nch - 1 - it
            r0 = pl.multiple_of(ci * CHUNK, CHUNK)
            rows = pl.ds(r0, CHUNK)
            for h in range(N_HEADS):
                ls = pl.ds(h * HEAD_D, HEAD_D)
                qc, kc, hc, vc = qt_s[rows, ls], kt_s[rows, ls], kh_s[rows, ls], v_s[rows, ls]
                dob = do_s[rows, ls]
                s0 = st_ref[ci, h]
                s0b = s0.astype(BF16)
                ds1 = dstate[h]
                ds1b = ds1.astype(BF16)
                egl = egl_s[pl.ds(r0, 1), ls]
                att = jnp.where(tri, _dot_nt(qc, kc), 0.0).astype(BF16)
                datt = jnp.where(tri, _dot_nt(dob, vc), 0.0).astype(BF16)
                dv_s[rows, ls] = _dot_tn(att, dob) + _dot_nt(hc, ds1b)
                dqt_s[rows, ls] = _dot(datt, kc) + _dot(dob, s0b)
                dkt_s[rows, ls] = _dot_tn(datt, qc)
                dkh_s[rows, ls] = _dot(vc, ds1b)
                dgl = egl * _colsum(ds1 * s0)
                dgl_s[rows, ls] = jnp.broadcast_to(dgl, (CHUNK, HEAD_D))
                dstate[h] = ds1 * egl + _dot_tn(dob, qc)
            return c_

        lax.fori_loop(0, nch, chunk, 0, unroll=min(CHUNK_UNROLL, nch))
        dqt, dkt, dkh = dqt_s[...], dkt_s[...], dkh_s[...]
        dk = dkt * pr["enG"] + dkh * pr["eGlG"]
        khk = dkh * kh_s[...].astype(F32)
        dG = dqt * qt_s[...].astype(F32) - dkt * kt_s[...].astype(F32) - khk
        dlogf = _mm3(upper_s[...], dG) + _chunk_total(khk) + dgl_s[...]
        df = dlogf / pr["f"] - dk
        sf, sq = pr["sf"], pr["sq"]
        s5_ref[4:5, :] += _colsum(df * (1.0 - sf))
        dfl = df * (1.0 - lb) * (sf * (1.0 - sf))
        dq = (dqt * pr["eG"]) * (sq * (1.0 + pq * (1.0 - sq)))
        dvv = dv_s[...]
        dp_ref[:, o0:o0 + HGRN_W] = dq.astype(BF16)
        dp_ref[:, o0 + HGRN_W:o0 + 2 * HGRN_W] = dfl.astype(BF16)
        dp_ref[:, o0 + 2 * HGRN_W:o0 + 3 * HGRN_W] = dvv.astype(BF16)
        sb_ref[1:2, 0:HGRN_W] += _colsum(dq)
        sb_ref[1:2, HGRN_W:2 * HGRN_W] += _colsum(dfl)
        sb_ref[2:3, 0:HGRN_W] += _colsum(dvv)

        dp_prev[...] = dp_ref[...]
        h1_prev[...] = h1_ref[...]

        @pl.when(i == nt - 1)
        def _():
            for k in range(n_pieces):
                w_in_grad_piece(k)
            out = pltpu.make_async_copy(gacc, gin_ref, gsem)
            out.start()
            for j in range(CONV_K):
                dw_ref[j:j + 1, :] = _colsum(dw8[j])
            copies = _xchg_copies(x_ins, x_outs, xssem, xrsem)
            for cp in copies:
                cp.wait_recv()
            for cp in copies:
                cp.wait_send()
            out.wait()

    rev = lambda cols: pl.BlockSpec((tm, cols), lambda i: (nt - 1 - i, 0))
    halo = pl.BlockSpec((HALO, 2 * CONV_CH), lambda i: (jnp.maximum((nt - 1 - i) * hpt - 1, 0), 0))
    wide = lambda n: pltpu.VMEM((tm, HGRN_W), n)
    hbm = pl.BlockSpec(memory_space=pl.ANY)
    outs = pl.pallas_call(
        body, name="mixers_bwd", grid=(nt,),
        in_specs=[rev(IN_COLS), halo, rev(D_MODEL), rev(CONV_CH), rev(HGRN_W),
                  pl.BlockSpec((nch, N_HEADS, HEAD_D, HEAD_D), lambda i: (nt - 1 - i, 0, 0, 0)), rev(D_MODEL),
                  _full((HALO, CONV_CH))] + [_full((1, CONV_CH))] * 4 + [_full((2, HGRN_W))] + [hbm] * nx,
        out_specs=[rev(IN_COLS), _full((8, D_MODEL)), _full((8, CONV_CH)), _full((HALO, CONV_CH))]
        + [hbm] * (nx + 1),
        out_shape=[_big((T, IN_COLS), BF16), jax.ShapeDtypeStruct((8, D_MODEL), F32),
                   jax.ShapeDtypeStruct((8, CONV_CH), F32), jax.ShapeDtypeStruct((HALO, CONV_CH), F32)]
        + [_big(pb.shape, BF16) for pb in pairs_b] + [_big((N_CHIPS, D_MODEL, nb), F32)],
        scratch_shapes=[pltpu.VMEM((tm + HALO + SUB, CONV_CH), F32), pltpu.VMEM((tm + HALO, CONV_CH), F32),
                        pltpu.VMEM((HALO, CONV_CH), F32), pltpu.VMEM((N_HEADS, HEAD_D, HEAD_D), F32),
                        wide(BF16), wide(BF16), wide(BF16), wide(BF16), wide(BF16),
                        wide(F32), wide(F32), wide(F32), wide(F32), wide(F32), wide(F32),
                        pltpu.VMEM((tm + 2 * SUB, CONV_CH), F32), pltpu.VMEM((HALO, SUB, CONV_CH), F32),
                        pltpu.VMEM((tm + SUB, CONV_CH), F32), pltpu.VMEM((tm, tm), BF16), pltpu.VMEM((tm, tm), BF16),
                        pltpu.VMEM((CONV_CH, CONV_CH), BF16),
                        pltpu.VMEM((N_CHIPS, D_MODEL, nb), F32), pltpu.VMEM((tm, IN_COLS), BF16),
                        pltpu.VMEM((tm, D_MODEL), BF16), pltpu.SemaphoreType.DMA]
        + _xchg_sems(nx),
        compiler_params=_cp(("arbitrary",), 56),
    )(*_hbm(p, p, dcat, ys, o, states, h1), wdw, *vecs, lbl, *[_hbm(pb) for pb in pairs_b])
    return outs[:4], outs[4:4 + nx], outs[4 + nx]


def _mix_in_bwd(dp, w_in_g, x, dx1, modr, g_pre, pairs_b):
    T = x.shape[0]
    tm = _tok_tile(T)
    nt = T // tm
    nb = IN_COLS // N_CHIPS
    nx = len(pairs_b)

    def body(*refs):
        dp_ref, w_ref, x_ref, dx1_ref, mod_ref, g_ref = refs[:6]
        x_ins = refs[6:6 + nx]
        gx_ref, st_ref = refs[6 + nx:8 + nx]
        x_outs = refs[8 + nx:8 + 2 * nx]
        xssem, xrsem = refs[8 + 2 * nx:]
        i = pl.program_id(0)

        @pl.when(i == 0)
        def _():
            for cp in _xchg_copies(x_ins, x_outs, xssem, xrsem):
                cp.start()
            st_ref[...] = jnp.zeros(st_ref.shape, F32)

        dh = None
        for j in range(N_CHIPS):
            part = _dot_nt(dp_ref[:, j * nb:(j + 1) * nb], w_ref[j])
            dh = part if dh is None else dh + part
        xv = x_ref[...]
        rs = lax.rsqrt(jnp.mean(xv * xv, axis=-1, keepdims=True) + RMS_EPS)
        xn = xv * rs
        st_ref[0:1, :] += _colsum(dh)
        st_ref[1:2, :] += _colsum(dh * (xn * g_ref[...]))
        dsc = dh * (1.0 + mod_ref[1:2, :])
        st_ref[2:3, :] += _colsum(dsc * xn)
        gx_ref[...] = dx1_ref[...] + _rms_bwd(dsc * g_ref[...], xn, rs)

        @pl.when(i == nt - 1)
        def _():
            copies = _xchg_copies(x_ins, x_outs, xssem, xrsem)
            for cp in copies:
                cp.wait_recv()
            for cp in copies:
                cp.wait_send()

    tile = pl.BlockSpec((tm, D_MODEL), lambda i: (i, 0))
    hbm = pl.BlockSpec(memory_space=pl.ANY)
    outs = pl.pallas_call(
        body, name="mix_in_bwd", grid=(nt,),
        in_specs=[pl.BlockSpec((tm, IN_COLS), lambda i: (i, 0)), _full((N_CHIPS, D_MODEL, nb)), tile, tile,
                  _full((6, D_MODEL)), _full((1, D_MODEL))] + [hbm] * nx,
        out_specs=[tile, _full((8, D_MODEL))] + [hbm] * nx,
        out_shape=[_big((T, D_MODEL), F32), jax.ShapeDtypeStruct((8, D_MODEL), F32)]
        + [_big(pb.shape, BF16) for pb in pairs_b],
        scratch_shapes=_xchg_sems(nx),
        compiler_params=_cp(("arbitrary",), 48),
    )(*_hbm(dp, w_in_g, x, dx1), modr, g_pre, *[_hbm(pb) for pb in pairs_b])
    return outs[:2], outs[2:]


def _weight_grad(a, b, a_blocked, b_blocked, name):
    T = a.shape[0]
    tt = min(GRAD_TILE, T)
    nt = T // tt
    ka = a.shape[1] // N_CHIPS if a_blocked else a.shape[1]
    nb = b.shape[1] // N_CHIPS if b_blocked else b.shape[1]

    def body(a_ref, b_ref, o_ref, ob_ref):
        t = pl.program_id(1)

        @pl.when(t == 0)
        def _():
            o_ref[...] = jnp.zeros(o_ref.shape, F32)

        for cols in _row_chains(nb):
            o_ref[0, :, cols] += _dot_tn(a_ref[...], b_ref[:, cols])

        @pl.when(t == nt - 1)
        def _():
            ob_ref[0] = o_ref[0].astype(BF16)

    blk = pl.BlockSpec((1, ka, nb), lambda j, t: (j, 0, 0))
    return pl.pallas_call(
        body, name=name, grid=(N_CHIPS, nt),
        in_specs=[pl.BlockSpec((tt, ka), (lambda j, t: (t, j)) if a_blocked else (lambda j, t: (t, 0))),
                  pl.BlockSpec((tt, nb), (lambda j, t: (t, j)) if b_blocked else (lambda j, t: (t, 0)))],
        out_specs=[blk, blk],
        out_shape=[_big((N_CHIPS, ka, nb), F32), _big((N_CHIPS, ka, nb), BF16)],
        compiler_params=_cp(("arbitrary", "arbitrary"), 48),
    )(*_hbm(a, b))


R_LOSS = 0
R_FFN = 8
R_OUT = 16
R_IN = 24
R_BIN = 32
R_512 = 40
R_DW = 48
N_STAT_ROWS = 80
MOD_ROWS = (R_IN + 0, R_IN + 1, R_OUT + 0, R_FFN + 0, R_FFN + 1, R_LOSS + 1)


def _small_update(gath, params):
    names = ["b_ada", "lb_logits", "g_pre_mix", "b_in", "b_dw", "gn_gain", "gn_bias", "g_hgrn_out", "g_post_mix",
             "g_pre_ffn", "g_post_ffn"]
    flat = []
    for n in names:
        flat += list(params[n])
    n_in = 1 + len(flat)

    def body(*refs):
        g_ref = refs[0]
        prm = {n: refs[1 + 3 * k:4 + 3 * k] for k, n in enumerate(names)}
        outs = refs[n_in:]
        loss_ref, dmod_ref, dwdw_ref = outs[0], outs[1], outs[2]
        res = {n: outs[3 + 4 * k:7 + 4 * k] for k, n in enumerate(names)}
        red = g_ref[0]
        for dev in range(1, N_DEV):
            red = red + g_ref[dev]
        loss_ref[...] = jnp.broadcast_to(
            (0.5 / D_MODEL) * jnp.sum(red[R_LOSS:R_LOSS + 1, :], axis=-1, keepdims=True), loss_ref.shape)
        for dev in range(N_DEV):
            for k, r in enumerate(MOD_ROWS):
                dmod_ref[dev:dev + 1, k * D_MODEL:(k + 1) * D_MODEL] = g_ref[dev, r:r + 1, :]
        dwdw_ref[...] = red[R_DW:R_DW + HALO, 0:CONV_CH]

        def finish(name, pieces):
            w_ref, m_ref, v_ref = prm[name]
            g_out, d_out, m_out, v_out = res[name]
            for rsl, lsl, g in pieces:
                d, m2, v2 = _adam_math(w_ref[rsl, lsl], g, m_ref[rsl, lsl], v_ref[rsl, lsl])
                g_out[rsl, lsl] = g
                d_out[rsl, lsl] = d
                m_out[rsl, lsl] = m2
                v_out[rsl, lsl] = v2

        one = slice(0, 1)
        row = lambda r: red[r:r + 1, :]
        half = lambda r: red[r:r + 1, 0:CONV_CH]
        finish("b_ada", [(one, slice(k * D_MODEL, (k + 1) * D_MODEL), row(r)) for k, r in enumerate(MOD_ROWS)])
        finish("b_in", [(one, slice(k * D_MODEL, (k + 1) * D_MODEL), row(R_BIN + k)) for k in range(3)])
        finish("g_pre_mix", [(one, slice(None), row(R_IN + 2))])
        finish("g_post_mix", [(one, slice(None), row(R_OUT + 1))])
        finish("g_pre_ffn", [(one, slice(None), row(R_FFN + 2))])
        finish("g_post_ffn", [(one, slice(None), row(R_LOSS + 2))])
        finish("b_dw", [(one, slice(None), half(R_512 + 0))])
        finish("gn_gain", [(one, slice(None), half(R_512 + 1))])
        finish("gn_bias", [(one, slice(None), half(R_512 + 2))])
        finish("g_hgrn_out", [(one, slice(None), half(R_512 + 3))])
        s0, s1 = _lower_bound(prm["lb_logits"][0])
        dlb = half(R_512 + 4)
        finish("lb_logits", [(slice(0, 1), slice(None), dlb * s0 * (1.0 - s0)),
                             (slice(1, 2), slice(None), -dlb * s0 * s1)])

    vm = pl.BlockSpec(memory_space=pltpu.VMEM)
    out_shape = [jax.ShapeDtypeStruct((8, 128), F32), jax.ShapeDtypeStruct((N_DEV, 6 * D_MODEL), F32),
                 jax.ShapeDtypeStruct((HALO, CONV_CH), F32)]
    for n in names:
        out_shape += [jax.ShapeDtypeStruct(params[n][0].shape, F32)] * 4
    outs = pl.pallas_call(
        body, name="small_update", out_shape=out_shape,
        in_specs=[vm] * n_in, out_specs=[vm] * len(out_shape),
        compiler_params=_cp(None, 32),
    )(gath, *flat)
    return outs[0], outs[1], outs[2], {n: outs[3 + 4 * k:7 + 4 * k] for k, n in enumerate(names)}


def _wdw_adam(w, g, m, v):
    def body(w_ref, g_ref, m_ref, v_ref, d_out, m_out, v_out):
        d, m2, v2 = _adam_math(w_ref[...], g_ref[...], m_ref[...], v_ref[...])
        d_out[...] = d
        m_out[...] = m2
        v_out[...] = v2

    vm = pl.BlockSpec(memory_space=pltpu.VMEM)
    return pl.pallas_call(
        body, name="wdw_adam", out_shape=[jax.ShapeDtypeStruct(w.shape, F32)] * 3,
        in_specs=[vm] * 4, out_specs=[vm] * 3, compiler_params=_cp(None, 16),
    )(w, g, m, v)


def kernel(x, c, w_ada, b_ada, lb_logits, g_pre_mix, w_in, b_in, w_dw, b_dw, gn_gain, gn_bias, g_hgrn_out, w_out, g_post_mix, g_pre_ffn, w_up, w_down, g_post_ffn, loss_target, m_w_ada, m_b_ada, m_lb_logits, m_g_pre_mix, m_w_in, m_b_in, m_w_dw, m_b_dw, m_gn_gain, m_gn_bias, m_g_hgrn_out, m_w_out, m_g_post_mix, m_g_pre_ffn, m_w_up, m_w_down, m_g_post_ffn, v_w_ada, v_b_ada, v_lb_logits, v_g_pre_mix, v_w_in, v_b_in, v_w_dw, v_b_dw, v_gn_gain, v_gn_bias, v_g_hgrn_out, v_w_out, v_g_post_mix, v_g_pre_ffn, v_w_up, v_w_down, v_g_post_ffn):
    ax, ay, ac = lax.axis_index("x"), lax.axis_index("y"), lax.axis_index("c")
    chip = 2 * ax + ay
    T = x.shape[1]
    xs, tgt = x[0], loss_target[0]
    ada_cols = w_ada.shape[2]

    b_sh = lax.dynamic_slice_in_dim(b_ada, chip * ada_cols, ada_cols, axis=1)
    wdw_pad = jnp.pad(w_dw[0], ((0, HALO - CONV_K), (0, 0)))
    chip1 = jnp.reshape(chip, (1,)).astype(jnp.int32)
    place = jnp.stack([ac, chip]).astype(jnp.int32)
    _, c8, modg, wdwg = _ada_exchange(c, w_ada[0], b_sh, wdw_pad)
    modr = modg.reshape(6, D_MODEL)
    wdw_all = jnp.transpose(wdwg, (1, 0, 2)).reshape(HALO, CONV_CH)
    bufs = {t: _cast_own(chip1, w[0], "cast_" + t)
            for w, t in ((w_in, "w_in"), (w_out, "w_out"), (w_up, "w_up"), (w_down, "w_down"))}
    vec = (b_dw, gn_gain, gn_bias, g_hgrn_out)

    p, h1, w_in_g, w_out_g = _mix_in_fwd(chip1, xs, modr, g_pre_mix, b_in.reshape(N_CHIPS, 1, IN_COLS // N_CHIPS),
                                          bufs["w_in"], bufs["w_out"])
    w_out_f = w_out_g.reshape(D_MODEL, D_MODEL)
    (cat, ys, o, states, y, x1, h2), (w_up_g, w_down_g) = _mixers_fwd(
        p, wdw_all, vec, lb_logits, w_out_f, xs, modr, g_post_mix, g_pre_ffn, [bufs["w_up"], bufs["w_down"]])
    w_down_f = w_down_g.reshape(D_FF, D_MODEL)
    r, dy2, dx2, st_loss = _ffn_fwd(h2, w_up_g, w_down_f, x1, tgt, modr, g_post_ffn)

    def pair_sums(grads, got, tags):
        return [_pair_sum(place, g, o_, "pair_sum_" + t) for (g, _), o_, t in zip(grads, got, tags)]

    da, dx1, st_ffn = _ffn_bwd(dy2, r, x1, dx2, w_up_g, w_down_f, modr, g_pre_ffn)
    g_up = _weight_grad(h2, da, False, True, "grad_w_up")
    g_down = _weight_grad(r, dy2, True, False, "grad_w_down")
    (dcat, st_out, g_out), got_ud = _mix_out_bwd(dx1, y, cat, w_out_f, modr, g_post_mix, [g_up[1], g_down[1]])
    g_out = g_out.reshape(N_CHIPS, D_MODEL // N_CHIPS, D_MODEL)
    got_o = _pair_swap([g_out], "pair_swap_w_out")
    early = pair_sums([(g_out, None), g_up, g_down], list(got_o) + list(got_ud), ["w_out", "w_up", "w_down"])
    (dp, st_bin, st_512, dwdw), got_early, g_in = _mixers_bwd(p, dcat, ys, o, states, h1, wdw_all, vec, lb_logits,
                                                              [pb for _, pb in early])
    late = pair_sums([(g_in, None)], _pair_swap([g_in], "pair_swap_w_in"), ["w_in"])
    (grad_x, st_in), got_late = _mix_in_bwd(dp, w_in_g, xs, dx1, modr, g_pre_mix, [late[0][1]])
    fulls = [_chip_sum(place, pf, gb, "chip_sum_" + t)
             for (pf, _), gb, t in zip(late + early, list(got_late) + list(got_early), ["w_in", "w_out", "w_up", "w_down"])]

    pad_lanes = lambda s: jnp.pad(s, ((0, 0), (0, D_MODEL - s.shape[1])))
    stats = jnp.concatenate([st_loss, st_ffn, st_out, st_in, st_bin, pad_lanes(st_512), pad_lanes(dwdw)], axis=0)
    (g_w_in, g_w_out, g_w_up, g_w_down), gath = _final_exchange(fulls, stats)
    small = {"b_ada": (b_ada, m_b_ada, v_b_ada), "lb_logits": (lb_logits, m_lb_logits, v_lb_logits),
             "g_pre_mix": (g_pre_mix, m_g_pre_mix, v_g_pre_mix), "b_in": (b_in, m_b_in, v_b_in),
             "b_dw": (b_dw, m_b_dw, v_b_dw), "gn_gain": (gn_gain, m_gn_gain, v_gn_gain),
             "gn_bias": (gn_bias, m_gn_bias, v_gn_bias), "g_hgrn_out": (g_hgrn_out, m_g_hgrn_out, v_g_hgrn_out),
             "g_post_mix": (g_post_mix, m_g_post_mix, v_g_post_mix), "g_pre_ffn": (g_pre_ffn, m_g_pre_ffn, v_g_pre_ffn),
             "g_post_ffn": (g_post_ffn, m_g_post_ffn, v_g_post_ffn)}
    loss_t, dmod_all, dwdw_sum, sres = _small_update(gath, small)
    loss = loss_t[0, 0]

    res = dict(sres)
    dmod_sh = lax.dynamic_slice_in_dim(dmod_all, chip * ada_cols, ada_cols, axis=1)
    res["w_ada"] = [t[None] for t in _ada_grad_adam(jnp.transpose(c8), dmod_sh, w_ada[0], m_w_ada[0], v_w_ada[0])]
    g_wdw = lax.dynamic_slice_in_dim(dwdw_sum, chip * HEAD_D, HEAD_D, axis=1)[:CONV_K][None]
    res["w_dw"] = [g_wdw] + list(_wdw_adam(w_dw, g_wdw, m_w_dw, v_w_dw))
    for name, g, w, m, v in (("w_in", g_w_in, w_in, m_w_in, v_w_in), ("w_out", g_w_out, w_out, m_w_out, v_w_out),
                             ("w_up", g_w_up, w_up, m_w_up, v_w_up), ("w_down", g_w_down, w_down, m_w_down, v_w_down)):
        d, m2, v2 = _adam_big(w[0], g, m[0], v[0], "adam_" + name)
        res[name] = [g[None], d[None], m2[None], v2[None]]

    order = ["w_ada", "b_ada", "lb_logits", "g_pre_mix", "w_in", "b_in", "w_dw", "b_dw", "gn_gain", "gn_bias",
             "g_hgrn_out", "w_out", "g_post_mix", "g_pre_ffn", "w_up", "w_down", "g_post_ffn"]
    out = [loss, grad_x[None]]
    for k in range(4):
        out += [res[n][k] for n in order]
    return tuple(out)
```

```python
import jax
import jax.numpy as jnp
from jax import lax
from jax.experimental import pallas as pl
from jax.experimental.pallas import tpu as pltpu

F32, BF16 = jnp.float32, jnp.bfloat16
D_MODEL = 1024
CONV_CH = 512
HGRN_W = 512
N_HEADS = 4
HEAD_D = 128
CONV_K = 31
GN_GROUP = 64
GN_SHIFT = 6
IN_COLS = 3072
D_FF = 4096
CHUNK = 64
CHUNK_SHIFT = 6
N_CHIPS = 4
N_DEV = 8
RMS_EPS = 1e-6
GN_EPS = 1e-5
ADAM_LR, ADAM_B1, ADAM_B2, ADAM_EPS, ADAM_WD, ADAM_STEP = 0.001, 0.9, 0.999, 1e-08, 0.01, 10
TOK_TILE = 512
MIXIN_TILE = 1024
MIXB_TILE = 256
FFN_TILE = 1024
FFN_BLOCK = 512
GRAD_TILE = 2048
HALO = 32
SUB = 8
LANE = 128
CONV_ROWS = 128
CHUNK_UNROLL = 8
MIB = 1 << 20
MESH = pl.DeviceIdType.MESH
OTHER_CHIPS = ((0, 1), (1, 0), (1, 1))


def _cp(sem=None, vmem_mib=48):
    return pltpu.CompilerParams(dimension_semantics=sem, vmem_limit_bytes=vmem_mib * MIB)


def _dot(a, b):
    return jnp.dot(a, b, preferred_element_type=F32)


def _dot_nt(a, b):
    return lax.dot_general(a, b, (((1,), (1,)), ((), ())), preferred_element_type=F32)


def _dot_tn(a, b):
    return lax.dot_general(a, b, (((0,), (0,)), ((), ())), preferred_element_type=F32)


def _sig(v):
    return 0.5 * jnp.tanh(0.5 * v) + 0.5


def _colsum(v):
    return jnp.sum(v, axis=0, keepdims=True)


def _flip(v, b):
    return 1 - v if b else v


def _rcopy(src, dst, ssem, rsem, dev):
    return pltpu.make_async_remote_copy(src_ref=src, dst_ref=dst, send_sem=ssem, recv_sem=rsem,
                                        device_id=dev, device_id_type=MESH)


def _place():
    return lax.axis_index("x"), lax.axis_index("y"), lax.axis_index("c")


def _full(shape):
    return pl.BlockSpec(shape, lambda *_: (0,) * len(shape))


def _big(shape, dtype):
    return pltpu.HBM(shape, dtype)


def _hbm(*arrays):
    out = [pltpu.with_memory_space_constraint(a, pltpu.HBM) for a in arrays]
    return out[0] if len(out) == 1 else out


def _split2(v):
    hi = v.astype(BF16)
    lo = (v - hi.astype(F32)).astype(BF16)
    return hi, lo


def _split3(v):
    h1 = v.astype(BF16)
    r1 = v - h1.astype(F32)
    h2 = r1.astype(BF16)
    h3 = (r1 - h2.astype(F32)).astype(BF16)
    return h1, h2, h3


def _mm3(mat, v):
    h1, h2, h3 = _split3(v)
    return _dot(mat, h1) + _dot(mat, h2) + _dot(mat, h3)


def _gn_matrix():
    r = lax.broadcasted_iota(jnp.int32, (CONV_CH, CONV_CH), 0) >> GN_SHIFT
    c = lax.broadcasted_iota(jnp.int32, (CONV_CH, CONV_CH), 1) >> GN_SHIFT
    return jnp.where(r == c, 1.0 / GN_GROUP, 0.0).astype(BF16)


def _gmean(v, gmat):
    hi, lo = _split2(v)
    return _dot(hi, gmat) + _dot(lo, gmat)


def _chunk_masks(tm):
    r = lax.broadcasted_iota(jnp.int32, (tm, tm), 0)
    c = lax.broadcasted_iota(jnp.int32, (tm, tm), 1)
    same = (r >> CHUNK_SHIFT) == (c >> CHUNK_SHIFT)
    one = lambda m: jnp.where(m, 1.0, 0.0).astype(BF16)
    return one(same & (c <= r)), one(same & (c >= r)), one(same)


def _tri():
    return lax.broadcasted_iota(jnp.int32, (CHUNK, CHUNK), 0) >= lax.broadcasted_iota(jnp.int32, (CHUNK, CHUNK), 1)


def _lower_bound(lbl_ref):
    l0, l1 = lbl_ref[0:1, :], lbl_ref[1:2, :]
    mx = jnp.maximum(l0, l1)
    e0, e1 = jnp.exp(l0 - mx), jnp.exp(l1 - mx)
    return e0 / (e0 + e1), e1 / (e0 + e1)


CONV_FWD_TAPS = tuple((j, HALO - (CONV_K - 1) + j) for j in range(CONV_K))
CONV_BWD_TAPS = tuple((j, (CONV_K - 1) - j) for j in range(CONV_K))


def _tap_conv(src_ref, w_ref, row0, taps, lanes):
    acc = None
    for b in range(SUB):
        pb = None
        for j, off in taps:
            if off % SUB == b:
                lo = row0 + off - b
                term = w_ref[j:j + 1, lanes] * src_ref[lo:lo + CONV_ROWS + SUB, lanes]
                pb = term if pb is None else pb + term
        if pb is not None:
            sh = pb[b:b + CONV_ROWS, :]
            acc = sh if acc is None else acc + sh
    return acc


def _hgrn_prep(pq, pf, lb, lower):
    sq = _sig(pq)
    qf = pq * sq
    sf = _sig(pf)
    f = lb + (1.0 - lb) * sf
    logf = jnp.log(f)
    k = 1.0 - f
    G = _mm3(lower, logf)
    rows, cols = G.shape
    g3 = G.reshape(rows // CHUNK, CHUNK, cols)
    Gl = jnp.broadcast_to(g3[:, CHUNK - 1:CHUNK, :], g3.shape).reshape(rows, cols)
    eG, enG, eGlG = jnp.exp(G), jnp.exp(-G), jnp.exp(Gl - G)
    return dict(sq=sq, sf=sf, f=f, Gl=Gl, eG=eG, enG=enG, eGlG=eGlG, qt=qf * eG, kt=k * enG, kh=k * eGlG)


def _ada_exchange(c_row, w_ada, b_sh, wdw_pad):
    ncol = w_ada.shape[1]

    def body(c_ref, w_ref, b_ref, wdw_ref, call_ref, c8_ref, modg_ref, wdwg_ref, rows_s, sa, ra, sw, rw, sm, rm):
        x, y, c = _place()
        me = 4 * x + 2 * y + c
        chip = 2 * x + y
        cv = c_ref[...]
        call_ref[me] = cv * _sig(cv)
        wdwg_ref[chip] = wdw_ref[...]
        sends = []
        for m in range(1, N_DEV):
            peer = (_flip(x, m >> 2), _flip(y, (m >> 1) & 1), _flip(c, m & 1))
            cp = _rcopy(call_ref.at[me], call_ref.at[me], sa.at[m - 1], ra.at[m - 1], peer)
            cp.start()
            sends.append(cp)
        for k, (fx, fy) in enumerate(OTHER_CHIPS):
            peer = (_flip(x, fx), _flip(y, fy), c)
            cp = _rcopy(wdwg_ref.at[chip], wdwg_ref.at[chip], sw.at[k], rw.at[k], peer)
            cp.start()
            sends.append(cp)
        for m in range(1, N_DEV):
            peer = (_flip(x, m >> 2), _flip(y, (m >> 1) & 1), _flip(c, m & 1))
            pid = 4 * peer[0] + 2 * peer[1] + peer[2]
            _rcopy(call_ref.at[pid], call_ref.at[pid], sa.at[m - 1], ra.at[m - 1], peer).wait_recv()
        for b in range(N_DEV):
            c8_ref[b:b + 1, :] = call_ref[b]
        mod_all = _dot(c8_ref[...].astype(BF16), w_ref[...].astype(BF16)) + b_ref[...]
        for b in range(N_DEV):
            rows_s[b] = mod_all[b:b + 1, :]
        modg_ref[chip] = rows_s[me]
        for k, (fx, fy) in enumerate(OTHER_CHIPS):
            peer = (_flip(x, fx), _flip(y, fy), c)
            pid = 4 * peer[0] + 2 * peer[1] + peer[2]
            cp = _rcopy(rows_s.at[pid], modg_ref.at[chip], sm.at[k], rm.at[k], peer)
            cp.start()
            sends.append(cp)
        for k, (fx, fy) in enumerate(OTHER_CHIPS):
            peer = (_flip(x, fx), _flip(y, fy), c)
            pchip = 2 * peer[0] + peer[1]
            _rcopy(rows_s.at[0], modg_ref.at[pchip], sm.at[k], rm.at[k], peer).wait_recv()
            _rcopy(wdwg_ref.at[pchip], wdwg_ref.at[pchip], sw.at[k], rw.at[k], peer).wait_recv()
        for cp in sends:
            cp.wait_send()

    vm = pl.BlockSpec(memory_space=pltpu.VMEM)
    return pl.pallas_call(
        body, name="ada_exchange",
        out_shape=[jax.ShapeDtypeStruct((N_DEV, 1, D_MODEL), F32), jax.ShapeDtypeStruct((N_DEV, D_MODEL), F32),
                   jax.ShapeDtypeStruct((N_CHIPS, 1, ncol), F32), jax.ShapeDtypeStruct((N_CHIPS, HALO, HEAD_D), F32)],
        in_specs=[vm] * 4, out_specs=[vm] * 4,
        scratch_shapes=[pltpu.VMEM((N_DEV, 1, ncol), F32),
                        pltpu.SemaphoreType.DMA((N_DEV - 1,)), pltpu.SemaphoreType.DMA((N_DEV - 1,)),
                        pltpu.SemaphoreType.DMA((3,)), pltpu.SemaphoreType.DMA((3,)),
                        pltpu.SemaphoreType.DMA((3,)), pltpu.SemaphoreType.DMA((3,))],
        compiler_params=_cp(None, 32),
    )(c_row, w_ada, b_sh, wdw_pad)


def _cast_own(chip1, shard, name):
    rows, cols = shard.shape
    tr = _row_tile(rows)

    def body(ch_ref, s_ref, o_ref):
        o_ref[0] = s_ref[...].astype(BF16)

    gs = pltpu.PrefetchScalarGridSpec(
        num_scalar_prefetch=1, grid=(rows // tr,),
        in_specs=[pl.BlockSpec((tr, cols), lambda i, ch: (i, 0))],
        out_specs=pl.BlockSpec((1, tr, cols), lambda i, ch: (ch[0], i, 0)))
    return pl.pallas_call(
        body, name=name, grid_spec=gs, out_shape=_big((N_CHIPS, rows, cols), BF16),
        compiler_params=_cp(("arbitrary",), 32),
    )(chip1, _hbm(shard))


def _slab(buf, ch, core):
    hs = buf.shape[1] // 2
    return buf.at[ch, pl.ds(core * hs, hs), :]


def _gather_start(bufs, ssem, rsem, relations=(0, 1, 2)):
    x, y, c = _place()
    chip = 2 * x + y
    for k in relations:
        fx, fy = OTHER_CHIPS[k]
        peer = (_flip(x, fx), _flip(y, fy), c)
        for t, buf in enumerate(bufs):
            _rcopy(_slab(buf, chip, c), _slab(buf, chip, c), ssem.at[t * 3 + k], rsem.at[t * 3 + k], peer).start()


def _gather_pass_on(bufs, ssem, rsem):
    nt = len(bufs)
    x, y, c = _place()
    sibling = (x, y, 1 - c)
    for k, (fx, fy) in enumerate(OTHER_CHIPS):
        peer = (_flip(x, fx), _flip(y, fy), c)
        pchip = 2 * peer[0] + peer[1]
        for t, buf in enumerate(bufs):
            _rcopy(_slab(buf, pchip, c), _slab(buf, pchip, c), ssem.at[t * 3 + k], rsem.at[t * 3 + k], peer).wait_recv()
            _rcopy(_slab(buf, pchip, c), _slab(buf, pchip, c), ssem.at[3 * nt + t * 3 + k],
                   rsem.at[3 * nt + t * 3 + k], sibling).start()


def _gather_drain(bufs, ssem, rsem):
    nt = len(bufs)
    x, y, c = _place()
    chip = 2 * x + y
    sibling = (x, y, 1 - c)
    for k, (fx, fy) in enumerate(OTHER_CHIPS):
        peer = (_flip(x, fx), _flip(y, fy), c)
        pchip = 2 * peer[0] + peer[1]
        for t, buf in enumerate(bufs):
            _rcopy(_slab(buf, pchip, 1 - c), _slab(buf, pchip, 1 - c), ssem.at[3 * nt + t * 3 + k],
                   rsem.at[3 * nt + t * 3 + k], sibling).wait_recv()
            _rcopy(_slab(buf, chip, c), _slab(buf, chip, c), ssem.at[t * 3 + k], rsem.at[t * 3 + k], peer).wait_send()
            _rcopy(_slab(buf, pchip, c), _slab(buf, pchip, c), ssem.at[3 * nt + t * 3 + k],
                   rsem.at[3 * nt + t * 3 + k], sibling).wait_send()


def _gather_finish(bufs, ssem, rsem):
    _gather_pass_on(bufs, ssem, rsem)
    _gather_drain(bufs, ssem, rsem)


def _gather_arrive(bufs, k, ssem, rsem):
    nt = len(bufs)
    x, y, c = _place()
    fx, fy = OTHER_CHIPS[k]
    peer = (_flip(x, fx), _flip(y, fy), c)
    pchip = 2 * peer[0] + peer[1]
    for t, buf in enumerate(bufs):
        _rcopy(_slab(buf, pchip, c), _slab(buf, pchip, c), ssem.at[t * 3 + k], rsem.at[t * 3 + k], peer).wait_recv()
        _rcopy(_slab(buf, pchip, c), _slab(buf, pchip, c), ssem.at[3 * nt + t * 3 + k],
               rsem.at[3 * nt + t * 3 + k], (x, y, 1 - c)).start()
    for t, buf in enumerate(bufs):
        _rcopy(_slab(buf, pchip, 1 - c), _slab(buf, pchip, 1 - c), ssem.at[3 * nt + t * 3 + k],
               rsem.at[3 * nt + t * 3 + k], (x, y, 1 - c)).wait_recv()


def _gather_sends_done(bufs, ssem, rsem):
    nt = len(bufs)
    x, y, c = _place()
    chip = 2 * x + y
    for k, (fx, fy) in enumerate(OTHER_CHIPS):
        peer = (_flip(x, fx), _flip(y, fy), c)
        pchip = 2 * peer[0] + peer[1]
        for t, buf in enumerate(bufs):
            _rcopy(_slab(buf, chip, c), _slab(buf, chip, c), ssem.at[t * 3 + k], rsem.at[t * 3 + k], peer).wait_send()
            _rcopy(_slab(buf, pchip, c), _slab(buf, pchip, c), ssem.at[3 * nt + t * 3 + k],
                   rsem.at[3 * nt + t * 3 + k], (x, y, 1 - c)).wait_send()


def _ring_parts(buf):
    x, y, c = _place()
    ynb, xnb = (x, 1 - y, c), (1 - x, y, c)
    ychip, xchip, dchip = 2 * x + (1 - y), 2 * (1 - x) + y, 2 * (1 - x) + (1 - y)
    hs = buf.shape[1] // 2

    def piece(ch, q):
        return buf.at[ch, pl.ds(c * hs + q * (hs // 2), hs // 2), :]

    return ynb, xnb, ychip, xchip, dchip, piece


def _ring_start(bufs, ssem, rsem):
    x, y, c = _place()
    chip = 2 * x + y
    for t, buf in enumerate(bufs):
        ynb, xnb, _, _, _, _ = _ring_parts(buf)
        _rcopy(_slab(buf, chip, c), _slab(buf, chip, c), ssem.at[2 * t], rsem.at[2 * t], ynb).start()
        _rcopy(_slab(buf, chip, c), _slab(buf, chip, c), ssem.at[2 * t + 1], rsem.at[2 * t + 1], xnb).start()


def _ring_forward(bufs, ssem, rsem):
    nt = len(bufs)
    _, _, c = _place()
    for t, buf in enumerate(bufs):
        ynb, xnb, ychip, xchip, _, piece = _ring_parts(buf)
        _rcopy(_slab(buf, ychip, c), _slab(buf, ychip, c), ssem.at[2 * t], rsem.at[2 * t], ynb).wait_recv()
        _rcopy(piece(ychip, 0), piece(ychip, 0), ssem.at[2 * nt + 2 * t], rsem.at[2 * nt + 2 * t], xnb).start()
        _rcopy(_slab(buf, xchip, c), _slab(buf, xchip, c), ssem.at[2 * t + 1], rsem.at[2 * t + 1], xnb).wait_recv()
        _rcopy(piece(xchip, 1), piece(xchip, 1), ssem.at[2 * nt + 2 * t + 1], rsem.at[2 * nt + 2 * t + 1], ynb).start()


def _ring_finish(bufs, ssem, rsem):
    nt = len(bufs)
    x, y, c = _place()
    chip = 2 * x + y
    sibling = (x, y, 1 - c)
    for t, buf in enumerate(bufs):
        ynb, xnb, ychip, xchip, dchip, piece = _ring_parts(buf)
        _rcopy(piece(dchip, 0), piece(dchip, 0), ssem.at[2 * nt + 2 * t], rsem.at[2 * nt + 2 * t], xnb).wait_recv()
        _rcopy(piece(dchip, 1), piece(dchip, 1), ssem.at[2 * nt + 2 * t + 1], rsem.at[2 * nt + 2 * t + 1],
               ynb).wait_recv()
        for k, ch in enumerate((ychip, xchip, dchip)):
            _rcopy(_slab(buf, ch, c), _slab(buf, ch, c), ssem.at[4 * nt + 3 * t + k], rsem.at[4 * nt + 3 * t + k],
                   sibling).start()
    for t, buf in enumerate(bufs):
        ynb, xnb, ychip, xchip, dchip, piece = _ring_parts(buf)
        for k, ch in enumerate((ychip, xchip, dchip)):
            _rcopy(_slab(buf, ch, 1 - c), _slab(buf, ch, 1 - c), ssem.at[4 * nt + 3 * t + k],
                   rsem.at[4 * nt + 3 * t + k], sibling).wait_recv()
            _rcopy(_slab(buf, ch, c), _slab(buf, ch, c), ssem.at[4 * nt + 3 * t + k], rsem.at[4 * nt + 3 * t + k],
                   sibling).wait_send()
        _rcopy(_slab(buf, chip, c), _slab(buf, chip, c), ssem.at[2 * t], rsem.at[2 * t], ynb).wait_send()
        _rcopy(_slab(buf, chip, c), _slab(buf, chip, c), ssem.at[2 * t + 1], rsem.at[2 * t + 1], xnb).wait_send()
        _rcopy(piece(ychip, 0), piece(ychip, 0), ssem.at[2 * nt + 2 * t], rsem.at[2 * nt + 2 * t], xnb).wait_send()
        _rcopy(piece(xchip, 1), piece(xchip, 1), ssem.at[2 * nt + 2 * t + 1], rsem.at[2 * nt + 2 * t + 1],
               ynb).wait_send()


def _ring_sems(nt):
    return [pltpu.SemaphoreType.DMA((7 * nt,)), pltpu.SemaphoreType.DMA((7 * nt,))]


def _gather_sems(nt):
    return [pltpu.SemaphoreType.DMA((6 * nt,)), pltpu.SemaphoreType.DMA((6 * nt,))]


def _pair_copies(ins, outs, ssem, rsem):
    x, y, c = _place()
    copies = []
    for t in range(len(ins)):
        hs = ins[t].shape[1] // 2
        copies.append(_rcopy(ins[t].at[:, pl.ds((1 - c) * hs, hs), :], outs[t], ssem.at[t], rsem.at[t], (x, y, 1 - c)))
    return copies


def _pair_shapes(grads):
    return [_big((g.shape[0], g.shape[1] // 2, g.shape[2]), g.dtype) for g in grads]


def _pair_sems(nt):
    return [pltpu.SemaphoreType.DMA((nt,)), pltpu.SemaphoreType.DMA((nt,))]


def _pair_swap(grads, name):
    nt = len(grads)
    hbm = pl.BlockSpec(memory_space=pl.ANY)

    def body(*refs):
        copies = _pair_copies(refs[:nt], refs[nt:2 * nt], refs[2 * nt], refs[2 * nt + 1])
        for cp in copies:
            cp.start()
        for cp in copies:
            cp.wait_recv()
        for cp in copies:
            cp.wait_send()

    return pl.pallas_call(
        body, name=name, out_shape=_pair_shapes(grads), in_specs=[hbm] * nt, out_specs=[hbm] * nt,
        scratch_shapes=_pair_sems(nt),
    )(*[_hbm(g) for g in grads])


def _xchg_copies(ins, outs, ssem, rsem):
    x, y, c = _place()
    copies = []
    for k, (fx, fy) in enumerate(OTHER_CHIPS):
        peer = (_flip(x, fx), _flip(y, fy), c)
        for t in range(len(ins)):
            copies.append(_rcopy(ins[t].at[k], outs[t].at[k], ssem.at[t * 3 + k], rsem.at[t * 3 + k], peer))
    return copies


def _xchg_sems(nt):
    return [pltpu.SemaphoreType.DMA((3 * nt,)), pltpu.SemaphoreType.DMA((3 * nt,))]


def _final_exchange(fulls, stats):
    nt = len(fulls)
    rows, cols = stats.shape
    hbm = pl.BlockSpec(memory_space=pl.ANY)
    vm = pl.BlockSpec(memory_space=pltpu.VMEM)

    def body(*refs):
        ins, s_ref = refs[:nt], refs[nt]
        outs, g_ref = refs[nt + 1:2 * nt + 1], refs[2 * nt + 1]
        hssem, hrsem, ssem, rsem = refs[2 * nt + 2:]
        x, y, c = _place()
        me, sibling = (x, y, c), (x, y, 1 - c)
        halves = []
        for t in range(nt):
            hs = ins[t].shape[0] // 2
            mine = pl.ds(c * hs, hs)
            cp = _rcopy(ins[t].at[mine, :], outs[t].at[mine, :], hssem.at[t], hrsem.at[t], sibling)
            cp.start()
            halves.append(cp)

        chips = [(_flip(x, fx), _flip(y, fy)) for fx, fy in OTHER_CHIPS]

        def blk(px, py, pc):
            return g_ref.at[4 * px + 2 * py + pc]

        def copy(k, block, to, src=None):
            return _rcopy(blk(*block) if src is None else src, blk(*block), ssem.at[k], rsem.at[k], to)

        g_ref[4 * x + 2 * y + c] = s_ref[...]
        first = [copy(0, me, sibling, src=s_ref)]
        first += [copy(1 + j, me, (*chip, c), src=s_ref) for j, chip in enumerate(chips)]
        for cp in first:
            cp.start()
        passed = [copy(4 + j, (*chip, c), sibling) for j, chip in enumerate(chips)]
        for j, chip in enumerate(chips):
            copy(1 + j, (*chip, c), me).wait_recv()
            passed[j].start()
        copy(0, sibling, me).wait_recv()
        for j, chip in enumerate(chips):
            copy(4 + j, (*chip, 1 - c), me).wait_recv()
        for t in range(nt):
            hs = ins[t].shape[0] // 2
            other = pl.ds((1 - c) * hs, hs)
            _rcopy(ins[t].at[other, :], outs[t].at[other, :], hssem.at[t], hrsem.at[t], sibling).wait_recv()
        for cp in first + passed + halves:
            cp.wait_send()

    outs = pl.pallas_call(
        body, name="final_exchange",
        out_shape=[_big(f.shape, F32) for f in fulls] + [jax.ShapeDtypeStruct((N_DEV, rows, cols), F32)],
        in_specs=[hbm] * nt + [vm], out_specs=[hbm] * nt + [vm],
        input_output_aliases={t: t for t in range(nt)},
        scratch_shapes=[pltpu.SemaphoreType.DMA((nt,)), pltpu.SemaphoreType.DMA((nt,)),
                        pltpu.SemaphoreType.DMA((7,)), pltpu.SemaphoreType.DMA((7,))],
        compiler_params=_cp(None, 32),
    )(*[_hbm(f) for f in fulls], stats)
    return outs[:nt], outs[nt]


def _row_tile(rows):
    return min(rows, 512)


def _pair_sum(place, grad, got, name):
    nb, hs, cols = got.shape
    tr = _row_tile(hs)
    nr = hs // tr

    def body(pl_ref, g_ref, o_ref, pf_ref, pb_ref):
        j = pl.program_id(1)
        s = g_ref[0] + o_ref[0].astype(F32)

        @pl.when(j == 0)
        def _():
            pf_ref[...] = s

        @pl.when(j > 0)
        def _():
            pb_ref[0] = s.astype(BF16)

    gs = pltpu.PrefetchScalarGridSpec(
        num_scalar_prefetch=1, grid=(nr, nb),
        in_specs=[pl.BlockSpec((1, tr, cols), lambda i, j, p: (p[1] ^ j, p[0] * nr + i, 0)),
                  pl.BlockSpec((1, tr, cols), lambda i, j, p: (p[1] ^ j, i, 0))],
        out_specs=[pl.BlockSpec((tr, cols), lambda i, j, p: (i, 0)),
                   pl.BlockSpec((1, tr, cols), lambda i, j, p: (jnp.maximum(j - 1, 0), i, 0))])
    return pl.pallas_call(
        body, name=name, grid_spec=gs,
        out_shape=[_big((hs, cols), F32), _big((nb - 1, hs, cols), BF16)],
        compiler_params=_cp(("arbitrary", "arbitrary"), 32),
    )(place, *_hbm(grad, got))


def _chip_sum(place, pair_f, got_b, name):
    nb, hs, cols = got_b.shape
    tr = _row_tile(hs)
    nr = hs // tr

    def body(pl_ref, pf_ref, gb_ref, o_ref):
        acc = pf_ref[...]
        for k in range(nb):
            acc = acc + gb_ref[k].astype(F32)
        o_ref[...] = acc

    gs = pltpu.PrefetchScalarGridSpec(
        num_scalar_prefetch=1, grid=(nr,),
        in_specs=[pl.BlockSpec((tr, cols), lambda i, p: (i, 0)),
                  pl.BlockSpec((nb, tr, cols), lambda i, p: (0, i, 0))],
        out_specs=pl.BlockSpec((tr, cols), lambda i, p: (p[0] * nr + i, 0)))
    return pl.pallas_call(
        body, name=name, grid_spec=gs,
        out_shape=_big((2 * hs, cols), F32),
        compiler_params=_cp(("arbitrary",), 32),
    )(place, *_hbm(pair_f, got_b))


def _adam_math(w, g, m, v):
    m2 = ADAM_B1 * m + (1.0 - ADAM_B1) * g
    v2 = ADAM_B2 * v + (1.0 - ADAM_B2) * (g * g)
    m_hat = m2 / (1.0 - ADAM_B1 ** ADAM_STEP)
    v_hat = v2 / (1.0 - ADAM_B2 ** ADAM_STEP)
    delta = -ADAM_LR * (m_hat / (jnp.sqrt(v_hat) + ADAM_EPS) + ADAM_WD * w)
    return delta, m2, v2


def _adam_big(w, g, m, v, name):
    rows, cols = w.shape
    tr = _row_tile(rows)

    def body(w_ref, g_ref, m_ref, v_ref, d_out, m_out, v_out):
        d, m2, v2 = _adam_math(w_ref[...], g_ref[...], m_ref[...], v_ref[...])
        d_out[...] = d
        m_out[...] = m2
        v_out[...] = v2

    spec = pl.BlockSpec((tr, cols), lambda i: (i, 0))
    return pl.pallas_call(
        body, name=name, grid=(rows // tr,), in_specs=[spec] * 4, out_specs=[spec] * 3,
        out_shape=[_big(w.shape, F32)] * 3,
        compiler_params=_cp(("arbitrary",), 48),
    )(*_hbm(w, g, m, v))


def _ada_grad_adam(c8t, dmod_sh, w, m, v):
    rows, cols = w.shape
    tr = _row_tile(rows) // 2

    def body(ct_ref, dm_ref, w_ref, m_ref, v_ref, g_out, d_out, m_out, v_out):
        g = None
        for b in range(N_DEV):
            term = ct_ref[:, b:b + 1] * dm_ref[b:b + 1, :]
            g = term if g is None else g + term
        d, m2, v2 = _adam_math(w_ref[...], g, m_ref[...], v_ref[...])
        g_out[...] = g
        d_out[...] = d
        m_out[...] = m2
        v_out[...] = v2

    spec = pl.BlockSpec((tr, cols), lambda i: (i, 0))
    return pl.pallas_call(
        body, name="ada_grad_adam", grid=(rows // tr,),
        in_specs=[pl.BlockSpec((tr, N_DEV), lambda i: (i, 0)), _full((N_DEV, cols)), spec, spec, spec],
        out_specs=[spec] * 4, out_shape=[_big(w.shape, F32)] * 4,
        compiler_params=_cp(("arbitrary",), 32),
    )(c8t, dmod_sh, *_hbm(w, m, v))


def _tok_tile(t):
    return min(TOK_TILE, t)


def _mix_in_fwd(chip1, x, modr, g_pre, b_in4, w_in_buf, w_out_buf):
    T = x.shape[0]
    tm = min(MIXIN_TILE, T)
    nt = T // tm
    nb = IN_COLS // N_CHIPS

    def body(ch_ref, x_ref, mod_ref, g_ref, b_ref, win_in, wout_in, p_ref, h_ref, win_ref, wout_ref,
             h_all, wblk, lsem, is_sem, ir_sem, os_sem, or_sem):
        k, i = pl.program_id(0), pl.program_id(1)
        chip = ch_ref[0]

        def load_block(blk):
            cp = pltpu.make_async_copy(win_ref.at[blk], wblk, lsem)
            cp.start()
            cp.wait()

        @pl.when((k == 0) & (i == 0))
        def _():
            _gather_start([win_ref], is_sem, ir_sem, relations=(0, 1))
            load_block(chip)

        for r in range(N_CHIPS - 1):
            @pl.when((k == r + 1) & (i == 0))
            def _(r=r):
                _gather_arrive([win_ref], r, is_sem, ir_sem)
                if r == 0:
                    _gather_start([win_ref], is_sem, ir_sem, relations=(2,))
                if r == 1:
                    _gather_start([wout_ref], os_sem, or_sem)
                load_block(chip ^ (r + 1))

        rows = pl.ds(pl.multiple_of(i * tm, tm), tm)

        @pl.when(k == 0)
        def _():
            xv = x_ref[...]
            rstd = lax.rsqrt(jnp.mean(xv * xv, axis=-1, keepdims=True) + RMS_EPS)
            h = (xv * rstd) * g_ref[...] * (1.0 + mod_ref[1:2, :]) + mod_ref[0:1, :]
            hb = h.astype(BF16)
            h_ref[...] = hb
            h_all[rows, :] = hb

        p_ref[...] = _dot(h_all[rows, :], wblk[...]) + b_ref[chip ^ k]

        @pl.when((k == N_CHIPS - 1) & (i == nt - 1))
        def _():
            _gather_sends_done([win_ref], is_sem, ir_sem)
            _gather_finish([wout_ref], os_sem, or_sem)

    hbm = pl.BlockSpec(memory_space=pl.ANY)
    first_pass = lambda k, i, ch: (jnp.where(k == 0, i, nt - 1), 0)
    gs = pltpu.PrefetchScalarGridSpec(
        num_scalar_prefetch=1, grid=(N_CHIPS, nt),
        in_specs=[pl.BlockSpec((tm, D_MODEL), first_pass), pl.BlockSpec((6, D_MODEL), lambda k, i, ch: (0, 0)),
                  pl.BlockSpec((1, D_MODEL), lambda k, i, ch: (0, 0)),
                  pl.BlockSpec((N_CHIPS, 1, nb), lambda k, i, ch: (0, 0, 0)), hbm, hbm],
        out_specs=[pl.BlockSpec((tm, nb), lambda k, i, ch: (i, ch[0] ^ k)), pl.BlockSpec((tm, D_MODEL), first_pass),
                   hbm, hbm],
        scratch_shapes=[pltpu.VMEM((T, D_MODEL), BF16), pltpu.VMEM((D_MODEL, nb), BF16), pltpu.SemaphoreType.DMA]
        + _gather_sems(1) + _gather_sems(1))
    return pl.pallas_call(
        body, name="mix_in_fwd", grid_spec=gs,
        out_shape=[_big((T, IN_COLS), F32), _big((T, D_MODEL), BF16), _big(w_in_buf.shape, BF16),
                   _big(w_out_buf.shape, BF16)],
        input_output_aliases={5: 2, 6: 3},
        compiler_params=_cp(("arbitrary", "arbitrary"), 48),
    )(chip1, _hbm(x), modr, g_pre, b_in4, _hbm(w_in_buf), _hbm(w_out_buf))


def _mixers_fwd(p, wdw, vecs, lbl, w_out, x, modr, g_post, g_ffn, gbufs):
    T = p.shape[0]
    tm = _tok_tile(T)
    nt = T // tm
    nch = tm // CHUNK
    ng = len(gbufs)
    n_in, n_out = 12, 7

    def body(*refs):
        (p_ref, wdw_ref, bdw_ref, gain_ref, bias_ref, gout_ref, lbl_ref, wout_ref, x_ref, mod_ref, gp_ref,
         gf_ref) = refs[:n_in]
        cat_ref, ys_ref, o_ref, st_ref, y_ref, x1_ref, h2_ref = refs[n_in + ng:n_in + ng + n_out]
        gout_bufs = refs[n_in + ng + n_out:n_in + 2 * ng + n_out]
        (ubuf, state, qt_s, kt_s, kh_s, v_s, egl_s, lower_s, gmat_s, gssem,
         grsem) = refs[n_in + 2 * ng + n_out:]
        i = pl.program_id(0)

        @pl.when(i == 0)
        def _():
            _ring_start(gout_bufs, gssem, grsem)
            lower_s[...], _, _ = _chunk_masks(tm)
            gmat_s[...] = _gn_matrix()
            state[...] = jnp.zeros(state.shape, F32)
            ubuf[0:HALO, :] = jnp.zeros((HALO, CONV_CH), F32)
            ubuf[HALO + tm:HALO + tm + SUB, :] = jnp.zeros((SUB, CONV_CH), F32)

        @pl.when(i > 0)
        def _():
            ubuf[0:HALO, :] = ubuf[tm:tm + HALO, :]

        ubuf[HALO:HALO + tm, :] = p_ref[:, 0:CONV_CH] * _sig(p_ref[:, CONV_CH:2 * CONV_CH])
        for r in range(tm // CONV_ROWS):
            rows = slice(r * CONV_ROWS, (r + 1) * CONV_ROWS)
            for lb_ in range(CONV_CH // LANE):
                lanes = slice(lb_ * LANE, (lb_ + 1) * LANE)
                ys_ref[rows, lanes] = bdw_ref[:, lanes] + _tap_conv(ubuf, wdw_ref, r * CONV_ROWS, CONV_FWD_TAPS, lanes)
        gmat = gmat_s[...]
        yv = ys_ref[...]
        d = yv - _gmean(yv, gmat)
        rs = lax.rsqrt(_gmean(d * d, gmat) + GN_EPS)
        z = d * rs * gain_ref[...] + bias_ref[...]
        cat_ref[:, 0:CONV_CH] = (z * _sig(z)).astype(BF16)

        lb, _ = _lower_bound(lbl_ref)
        o0 = 2 * CONV_CH
        pr = _hgrn_prep(p_ref[:, o0:o0 + HGRN_W], p_ref[:, o0 + HGRN_W:o0 + 2 * HGRN_W], lb, lower_s[...])
        qt_s[...] = pr["qt"].astype(BF16)
        kt_s[...] = pr["kt"].astype(BF16)
        kh_s[...] = pr["kh"].astype(BF16)
        v_s[...] = p_ref[:, o0 + 2 * HGRN_W:o0 + 3 * HGRN_W].astype(BF16)
        egl_s[...] = jnp.exp(pr["Gl"])
        tri = _tri()

        def chunk(ci, carry):
            r0 = pl.multiple_of(ci * CHUNK, CHUNK)
            rows = pl.ds(r0, CHUNK)
            for h in range(N_HEADS):
                ls = pl.ds(h * HEAD_D, HEAD_D)
                qc, kc, hc, vc = qt_s[rows, ls], kt_s[rows, ls], kh_s[rows, ls], v_s[rows, ls]
                s0 = state[h]
                s0b = s0.astype(BF16)
                st_ref[ci, h] = s0
                att =jnp.where(tri, _dot_nt(qc, kc), 0.0).astype(BF16)
                o_ref[rows, ls] = _dot(att, vc) + _dot_nt(qc, s0b)
                state[h] = s0 * egl_s[pl.ds(r0, 1), ls] + _dot_tn(vc, hc)
            return carry

        lax.fori_loop(0, nch, chunk, 0, unroll=min(CHUNK_UNROLL, nch))
        for h in range(N_HEADS):
            sl = slice(h * HEAD_D, (h + 1) * HEAD_D)
            oh = o_ref[:, sl]
            gh = p_ref[:, o0 + 3 * HGRN_W + h * HEAD_D:o0 + 3 * HGRN_W + (h + 1) * HEAD_D]
            rsh = lax.rsqrt(jnp.mean(oh * oh, axis=-1, keepdims=True) + RMS_EPS)
            hg = (oh * rsh) * gout_ref[:, sl] * (gh * _sig(gh))
            cat_ref[:, CONV_CH + h * HEAD_D:CONV_CH + (h + 1) * HEAD_D] = hg.astype(BF16)

        yv = _dot(cat_ref[...], wout_ref[...])
        y_ref[...] = yv
        rsy = lax.rsqrt(jnp.mean(yv * yv, axis=-1, keepdims=True) + RMS_EPS)
        x1 = x_ref[...] + mod_ref[2:3, :] * ((yv * rsy) * gp_ref[...])
        x1_ref[...] = x1
        rs1 = lax.rsqrt(jnp.mean(x1 * x1, axis=-1, keepdims=True) + RMS_EPS)
        h2 = (x1 * rs1) * gf_ref[...] * (1.0 + mod_ref[4:5, :]) + mod_ref[3:4, :]
        h2_ref[...] = h2.astype(BF16)

        @pl.when(i == min(nt - 1, nt // 2 + 1))
        def _():
            _ring_forward(gout_bufs, gssem, grsem)

        @pl.when(i == nt - 1)
        def _():
            _ring_finish(gout_bufs, gssem, grsem)

    tile = lambda cols: pl.BlockSpec((tm, cols), lambda i: (i, 0))
    hbm = pl.BlockSpec(memory_space=pl.ANY)
    outs = pl.pallas_call(
        body, name="mixers_fwd", grid=(nt,),
        in_specs=[tile(IN_COLS), _full((HALO, CONV_CH))] + [_full((1, CONV_CH))] * 4 + [_full((2, HGRN_W))]
        + [_full((D_MODEL, D_MODEL)), tile(D_MODEL), _full((6, D_MODEL)), _full((1, D_MODEL)), _full((1, D_MODEL))]
        + [hbm] * ng,
        out_specs=[tile(D_MODEL), tile(CONV_CH), tile(HGRN_W),
                   pl.BlockSpec((nch, N_HEADS, HEAD_D, HEAD_D), lambda i: (i, 0, 0, 0)),
                   tile(D_MODEL), tile(D_MODEL), tile(D_MODEL)] + [hbm] * ng,
        out_shape=[_big((T, D_MODEL), BF16), _big((T, CONV_CH), F32), _big((T, HGRN_W), F32),
                   _big((T // CHUNK, N_HEADS, HEAD_D, HEAD_D), F32), _big((T, D_MODEL), F32),
                   _big((T, D_MODEL), F32), _big((T, D_MODEL), BF16)] + [_big(b.shape, BF16) for b in gbufs],
        input_output_aliases={n_in + t: n_out + t for t in range(ng)},
        scratch_shapes=[pltpu.VMEM((tm + HALO + SUB, CONV_CH), F32), pltpu.VMEM((N_HEADS, HEAD_D, HEAD_D), F32),
                        pltpu.VMEM((tm, HGRN_W), BF16), pltpu.VMEM((tm, HGRN_W), BF16),
                        pltpu.VMEM((tm, HGRN_W), BF16), pltpu.VMEM((tm, HGRN_W), BF16),
                        pltpu.VMEM((tm, HGRN_W), F32), pltpu.VMEM((tm, tm), BF16),
                        pltpu.VMEM((CONV_CH, CONV_CH), BF16)] + _ring_sems(ng),
        compiler_params=_cp(("arbitrary",), 56),
    )(_hbm(p), wdw, *vecs, lbl, *_hbm(w_out, x), modr, g_post, g_ffn, *[_hbm(b) for b in gbufs])
    return outs[:n_out], outs[n_out:]


def _row_chains(rows, n=2):
    step = rows // n
    return [slice(k * step, (k + 1) * step) for k in range(n)]


def _ffn_blocks():
    return D_FF // FFN_BLOCK, (D_FF // N_CHIPS) // FFN_BLOCK


def _ffn_fwd(h2, w_up_g, w_down, x1, target, modr, g_post):
    T = h2.shape[0]
    tm = min(FFN_TILE, T)
    fb = FFN_BLOCK
    nj, per = _ffn_blocks()

    def body(h_ref, wu_ref, wd_ref, x1_ref, t_ref, mod_ref, g_ref, r_ref, dy2_ref, dx2_ref, st_ref, acc):
        i, j = pl.program_id(0), pl.program_id(1)

        @pl.when((i == 0) & (j == 0))
        def _():
            st_ref[...] = jnp.zeros(st_ref.shape, F32)

        @pl.when(j == 0)
        def _():
            acc[...] = jnp.zeros(acc.shape, F32)

        for rows in _row_chains(tm):
            ra = jnp.maximum(_dot(h_ref[rows, :], wu_ref[0]), 0.0)
            rb = (ra * ra).astype(BF16)
            r_ref[rows, :] = rb
            acc[rows, :] += _dot(rb, wd_ref[...])

        @pl.when(j == nj - 1)
        def _():
            y2 = acc[...]
            rs = lax.rsqrt(jnp.mean(y2 * y2, axis=-1, keepdims=True) + RMS_EPS)
            nh = y2 * rs
            gp = g_ref[...]
            err = x1_ref[...] + mod_ref[5:6, :] * (nh * gp) - t_ref[...]
            dx2 = err * (1.0 / D_MODEL)
            dx2_ref[...] = dx2
            st_ref[0:1, :] += _colsum(err * err)
            st_ref[1:2, :] += _colsum(dx2 * (nh * gp))
            dn = dx2 * mod_ref[5:6, :]
            st_ref[2:3, :] += _colsum(dn * nh)
            dy2_ref[...] = _rms_bwd(dn * gp, nh, rs).astype(BF16)

    tile = pl.BlockSpec((tm, D_MODEL), lambda i, j: (i, 0))
    return pl.pallas_call(
        body, name="ffn_fwd", grid=(T // tm, nj),
        in_specs=[tile, pl.BlockSpec((1, D_MODEL, fb), lambda i, j: (j // per, 0, j % per)),
                  pl.BlockSpec((fb, D_MODEL), lambda i, j: (j, 0)), tile, tile,
                  _full((6, D_MODEL)), _full((1, D_MODEL))],
        out_specs=[pl.BlockSpec((tm, fb), lambda i, j: (i, j)), tile, tile, _full((8, D_MODEL))],
        out_shape=[_big((T, D_FF), BF16), _big((T, D_MODEL), BF16), _big((T, D_MODEL), F32),
                   jax.ShapeDtypeStruct((8, D_MODEL), F32)],
        scratch_shapes=[pltpu.VMEM((tm, D_MODEL), F32)],
        compiler_params=_cp(("arbitrary", "arbitrary"), 56),
    )(*_hbm(h2, w_up_g, w_down, x1, target), modr, g_post)


def _rms_bwd(dxn, xn, rs):
    return rs * (dxn - xn * jnp.mean(dxn * xn, axis=-1, keepdims=True))


def _ffn_bwd(dy2, r, x1, dx2, w_up_g, w_down, modr, g_ffn):
    T = dx2.shape[0]
    tm = min(FFN_TILE, T)
    fb = FFN_BLOCK
    nj, per = _ffn_blocks()

    def body(dy2_ref, r_ref, x1_ref, dx2_ref, wu_ref, wd_ref, mod_ref, gf_ref, da_ref, dx1_ref, st_ref, dh_s):
        i, j = pl.program_id(0), pl.program_id(1)

        @pl.when((i == 0) & (j == 0))
        def _():
            st_ref[...] = jnp.zeros(st_ref.shape, F32)

        @pl.when(j == 0)
        def _():
            dh_s[...] = jnp.zeros(dh_s.shape, F32)

        for rows in _row_chains(tm):
            ra = jnp.sqrt(r_ref[rows, :].astype(F32))
            da = (_dot_nt(dy2_ref[rows, :], wd_ref[...]) * (2.0 * ra)).astype(BF16)
            da_ref[rows, :] = da
            dh_s[rows, :] += _dot_nt(da, wu_ref[0])

        @pl.when(j == nj - 1)
        def _():
            dh = dh_s[...]
            x1v = x1_ref[...]
            rs1 = lax.rsqrt(jnp.mean(x1v * x1v, axis=-1, keepdims=True) + RMS_EPS)
            xn = x1v * rs1
            st_ref[0:1, :] += _colsum(dh)
            st_ref[1:2, :] += _colsum(dh * (xn * gf_ref[...]))
            dsc = dh * (1.0 + mod_ref[4:5, :])
            st_ref[2:3, :] += _colsum(dsc * xn)
            dx1_ref[...] = dx2_ref[...] + _rms_bwd(dsc * gf_ref[...], xn, rs1)

    tile = pl.BlockSpec((tm, D_MODEL), lambda i, j: (i, 0))
    ftile = pl.BlockSpec((tm, fb), lambda i, j: (i, j))
    return pl.pallas_call(
        body, name="ffn_bwd", grid=(T // tm, nj),
        in_specs=[tile, ftile, tile, tile, pl.BlockSpec((1, D_MODEL, fb), lambda i, j: (j // per, 0, j % per)),
                  pl.BlockSpec((fb, D_MODEL), lambda i, j: (j, 0)), _full((6, D_MODEL)), _full((1, D_MODEL))],
        out_specs=[ftile, tile, _full((8, D_MODEL))],
        out_shape=[_big((T, D_FF), BF16), _big((T, D_MODEL), F32), jax.ShapeDtypeStruct((8, D_MODEL), F32)],
        scratch_shapes=[pltpu.VMEM((tm, D_MODEL), F32)],
        compiler_params=_cp(("arbitrary", "arbitrary"), 56),
    )(*_hbm(dy2, r, x1, dx2, w_up_g, w_down), modr, g_ffn)


def _mix_out_bwd(dx1, y, cat, w_out, modr, g_post, swap):
    T = dx1.shape[0]
    tm = _tok_tile(T)
    nt = T // tm
    ns = len(swap)

    def body(*refs):
        dx1_ref, y_ref, cat_ref, w_ref, mod_ref, gp_ref = refs[:6]
        s_ins = refs[6:6 + ns]
        dcat_ref, st_ref, gw_ref = refs[6 + ns:9 + ns]
        s_outs = refs[9 + ns:9 + 2 * ns]
        gacc, gsem, pssem, prsem = refs[9 + 2 * ns:]
        i = pl.program_id(0)

        @pl.when(i == 0)
        def _():
            for cp in _pair_copies(s_ins, s_outs, pssem, prsem):
                cp.start()
            st_ref[...] = jnp.zeros(st_ref.shape, F32)
            gacc[...] = jnp.zeros(gacc.shape, F32)

        dxv, yv = dx1_ref[...], y_ref[...]
        rs = lax.rsqrt(jnp.mean(yv * yv, axis=-1, keepdims=True) + RMS_EPS)
        nh = yv * rs
        st_ref[0:1, :] += _colsum(dxv * (nh * gp_ref[...]))
        dn = dxv * mod_ref[2:3, :]
        st_ref[1:2, :] += _colsum(dn * nh)
        dy = _rms_bwd(dn * gp_ref[...], nh, rs).astype(BF16)
        dcat_ref[...] = _dot_nt(dy, w_ref[...])
        for cols in _row_chains(D_MODEL):
            gacc[:, cols] += _dot_tn(cat_ref[...], dy[:, cols])

        @pl.when(i == nt - 1)
        def _():
            out = pltpu.make_async_copy(gacc, gw_ref, gsem)
            out.start()
            copies = _pair_copies(s_ins, s_outs, pssem, prsem)
            for cp in copies:
                cp.wait_recv()
            for cp in copies:
                cp.wait_send()
            out.wait()

    tile = pl.BlockSpec((tm, D_MODEL), lambda i: (i, 0))
    hbm = pl.BlockSpec(memory_space=pl.ANY)
    outs = pl.pallas_call(
        body, name="mix_out_bwd", grid=(nt,),
        in_specs=[tile, tile, tile, _full((D_MODEL, D_MODEL)), _full((6, D_MODEL)), _full((1, D_MODEL))]
        + [hbm] * ns,
        out_specs=[tile, _full((8, D_MODEL)), hbm] + [hbm] * ns,
        out_shape=[_big((T, D_MODEL), F32), jax.ShapeDtypeStruct((8, D_MODEL), F32), _big((D_MODEL, D_MODEL), F32)]
        + _pair_shapes(swap),
        scratch_shapes=[pltpu.VMEM((D_MODEL, D_MODEL), F32), pltpu.SemaphoreType.DMA] + _pair_sems(ns),
        compiler_params=_cp(("arbitrary",), 48),
    )(*_hbm(dx1, y, cat, w_out), modr, g_post, *[_hbm(g) for g in swap])
    return outs[:3], outs[3:]


def _mixers_bwd(p, dcat, ys, o, states, h1, wdw, vecs, lbl, pairs_b):
    T = p.shape[0]
    tm = min(MIXB_TILE, T)
    nt = T // tm
    nch = tm // CHUNK
    hpt = tm // HALO
    nx = len(pairs_b)
    nb = IN_COLS // N_CHIPS

    def body(*refs):
        (p_ref, ph_ref, dcat_ref, ys_ref, o_ref, st_ref, h1_ref, wdw_ref, bdw_ref, gain_ref, bias_ref, gout_ref,
         lbl_ref) = refs[:13]
        x_ins = refs[13:13 + nx]
        dp_ref, sb_ref, s5_ref, dw_ref = refs[13 + nx:17 + nx]
        x_outs = refs[17 + nx:17 + 2 * nx]
        gin_ref = refs[17 + 2 * nx]
        (ubuf, dybuf, carry, dstate, qt_s, kt_s, kh_s, v_s, do_s, egl_s, dqt_s, dkt_s, dkh_s, dv_s, dgl_s,
         dsh, dw8, dshift, lower_s, upper_s, same_s, gmat_s, gacc, dp_prev, h1_prev, gsem, xssem,
         xrsem) = refs[18 + 2 * nx:]
        i = pl.program_id(0)
        tile_idx = nt - 1 - i

        @pl.when(i == 0)
        def _():
            for cp in _xchg_copies(x_ins, x_outs, xssem, xrsem):
                cp.start()
            gacc[...] = jnp.zeros(gacc.shape, F32)
            dstate[...] = jnp.zeros(dstate.shape, F32)
            carry[...] = jnp.zeros(carry.shape, F32)
            sb_ref[...] = jnp.zeros(sb_ref.shape, F32)
            s5_ref[...] = jnp.zeros(s5_ref.shape, F32)
            dw_ref[...] = jnp.zeros(dw_ref.shape, F32)
            dw8[...] = jnp.zeros(dw8.shape, F32)
            lower_s[...], upper_s[...], same_s[...] = _chunk_masks(tm)
            gmat_s[...] = _gn_matrix()
            dsh[0:SUB, :] = jnp.zeros((SUB, CONV_CH), F32)
            dsh[SUB + tm:2 * SUB + tm, :] = jnp.zeros((SUB, CONV_CH), F32)
            ubuf[HALO + tm:HALO + tm + SUB, :] = jnp.zeros((SUB, CONV_CH), F32)
            dp_prev[...] = jnp.zeros(dp_prev.shape, BF16)
            h1_prev[...] = jnp.zeros(h1_prev.shape, BF16)

        n_pieces = (tm // CONV_ROWS) * (CONV_CH // LANE)
        per_block = n_pieces // N_CHIPS
        prow = D_MODEL // per_block

        def w_in_grad_piece(k):
            j, part = k // per_block, k % per_block
            rows_k = slice(part * prow, (part + 1) * prow)
            gacc[j, rows_k, :] += _dot_tn(h1_prev[:, rows_k], dp_prev[:, j * nb:(j + 1) * nb])

        uh = ph_ref[:, 0:CONV_CH] * _sig(ph_ref[:, CONV_CH:2 * CONV_CH])
        ubuf[0:HALO, :] = jnp.where(tile_idx > 0, uh, 0.0)
        ubuf[HALO:HALO + tm, :] = p_ref[:, 0:CONV_CH] * _sig(p_ref[:, CONV_CH:2 * CONV_CH])
        gmat = gmat_s[...]
        gain = gain_ref[...]
        yv = ys_ref[...]
        d = yv - _gmean(yv, gmat)
        rs = lax.rsqrt(_gmean(d * d, gmat) + GN_EPS)
        yn = d * rs
        z = yn * gain + bias_ref[...]
        sz = _sig(z)
        dz = dcat_ref[:, 0:CONV_CH] * (sz * (1.0 + z * (1.0 - sz)))
        dyn = dz * gain
        dyc = rs * (dyn - _gmean(dyn, gmat) - yn * _gmean(dyn * yn, gmat))
        s5_ref[0:1, :] += _colsum(dyc)
        s5_ref[1:2, :] += _colsum(dz * yn)
        s5_ref[2:3, :] += _colsum(dz)
        dybuf[tm:tm + HALO, :] = carry[...]
        dybuf[0:tm, :] = dyc
        dsh[SUB:SUB + tm, :] = dyc
        carry[...] = dyc[0:HALO, :]
        for b in range(SUB):
            dshift[...] = dsh[SUB - b:2 * SUB - b + tm, :]
            for j, off in CONV_FWD_TAPS:
                if off % SUB == b:
                    prod = dshift[...] * ubuf[off - b:off - b + tm + SUB, :]
                    dw8[j] += jnp.sum(prod.reshape((tm + SUB) // SUB, SUB, CONV_CH), axis=0)
        for r in range(tm // CONV_ROWS):
            rows = slice(r * CONV_ROWS, (r + 1) * CONV_ROWS)
            for lb_ in range(CONV_CH // LANE):
                lanes = slice(lb_ * LANE, (lb_ + 1) * LANE)
                glanes = slice(CONV_CH + lb_ * LANE, CONV_CH + (lb_ + 1) * LANE)
                w_in_grad_piece(r * (CONV_CH // LANE) + lb_)
                acc = _tap_conv(dybuf, wdw_ref, r * CONV_ROWS, CONV_BWD_TAPS, lanes)
                val = p_ref[rows, lanes]
                sg = _sig(p_ref[rows, glanes])
                dval = acc * sg
                dgate = acc * val * (sg * (1.0 - sg))
                dp_ref[rows, lanes] = dval.astype(BF16)
                dp_ref[rows, glanes] = dgate.astype(BF16)
                sb_ref[0:1, lanes] += _colsum(dval)
                sb_ref[0:1, glanes] += _colsum(dgate)

        o0 = 2 * CONV_CH
        for h in range(N_HEADS):
            sl = slice(h * HEAD_D, (h + 1) * HEAD_D)
            gsl = slice(o0 + 3 * HGRN_W + h * HEAD_D, o0 + 3 * HGRN_W + (h + 1) * HEAD_D)
            oh = o_ref[:, sl]
            gh = p_ref[:, gsl]
            dh = dcat_ref[:, CONV_CH + h * HEAD_D:CONV_CH + (h + 1) * HEAD_D]
            gout = gout_ref[:, sl]
            rsh = lax.rsqrt(jnp.mean(oh * oh, axis=-1, keepdims=True) + RMS_EPS)
            on = oh * rsh
            sgg = _sig(gh)
            dgh = dh * (on * gout) * (sgg * (1.0 + gh * (1.0 - sgg)))
            dm = dh * (gh * sgg)
            s5_ref[3:4, sl] += _colsum(dm * on)
            do_s[:, sl] = _rms_bwd(dm * gout, on, rsh).astype(BF16)
            dp_ref[:, gsl] = dgh.astype(BF16)
            sb_ref[2:3, CONV_CH + h * HEAD_D:CONV_CH + (h + 1) * HEAD_D] += _colsum(dgh)

        lb, _ = _lower_bound(lbl_ref)
        pq = p_ref[:, o0:o0 + HGRN_W]
        pr = _hgrn_prep(pq, p_ref[:, o0 + HGRN_W:o0 + 2 * HGRN_W], lb, lower_s[...])
        qt_s[...] = pr["qt"].astype(BF16)
        kt_s[...] = pr["kt"].astype(BF16)
        kh_s[...] = pr["kh"].astype(BF16)
        v_s[...] = p_ref[:, o0 + 2 * HGRN_W:o0 + 3 * HGRN_W].astype(BF16)
        egl_s[...] = jnp.exp(pr["Gl"])
        tri = _tri()

        def chunk(it, c_):
            ci = nch - 1 - it
            r0 = pl.multiple_of(ci * CHUNK, CHUNK)
            rows = pl.ds(r0, CHUNK)
            for h in range(N_HEADS):
                ls = pl.ds(h * HEAD_D, HEAD_D)
                qc, kc, hc, vc = qt_s[rows, ls], kt_s[rows, ls], kh_s[rows, ls], v_s[rows, ls]
                dob = do_s[rows, ls]
                s0 = st_ref[ci, h]
                s0b = s0.astype(BF16)
                ds1 = dstate[h]
                ds1b = ds1.astype(BF16)
                egl = egl_s[pl.ds(r0, 1), ls]
                att = jnp.where(tri, _dot_nt(qc, kc), 0.0).astype(BF16)
                datt = jnp.where(tri, _dot_nt(dob, vc), 0.0).astype(BF16)
                dv_s[rows, ls] = _dot_tn(att, dob) + _dot_nt(hc, ds1b)
                dqt_s[rows, ls] = _dot(datt, kc) + _dot(dob, s0b)
                dkt_s[rows, ls] = _dot_tn(datt, qc)
                dkh_s[rows, ls] = _dot(vc, ds1b)
                dgl = egl * _colsum(ds1 * s0)
                dgl_s[rows, ls] = jnp.broadcast_to(dgl, (CHUNK, HEAD_D))
                dstate[h] = ds1 * egl + _dot_tn(dob, qc)
            return c_

        lax.fori_loop(0, nch, chunk, 0, unroll=min(CHUNK_UNROLL, nch))
        dqt, dkt, dkh = dqt_s[...], dkt_s[...], dkh_s[...]
        dk = dkt * pr["enG"] + dkh * pr["eGlG"]
        khk = dkh * kh_s[...].astype(F32)
        dG = dqt * qt_s[...].astype(F32) - dkt * kt_s[...].astype(F32) - khk
        dlogf = _mm3(upper_s[...], dG) + _mm3(same_s[...], khk) + dgl_s[...]
        df = dlogf / pr["f"] - dk
        sf, sq = pr["sf"], pr["sq"]
        s5_ref[4:5, :] += _colsum(df * (1.0 - sf))
        dfl = df * (1.0 - lb) * (sf * (1.0 - sf))
        dq = (dqt * pr["eG"]) * (sq * (1.0 + pq * (1.0 - sq)))
        dvv = dv_s[...]
        dp_ref[:, o0:o0 + HGRN_W] = dq.astype(BF16)
        dp_ref[:, o0 + HGRN_W:o0 + 2 * HGRN_W] = dfl.astype(BF16)
        dp_ref[:, o0 + 2 * HGRN_W:o0 + 3 * HGRN_W] = dvv.astype(BF16)
        sb_ref[1:2, 0:HGRN_W] += _colsum(dq)
        sb_ref[1:2, HGRN_W:2 * HGRN_W] += _colsum(dfl)
        sb_ref[2:3, 0:HGRN_W] += _colsum(dvv)

        dp_prev[...] = dp_ref[...]
        h1_prev[...] = h1_ref[...]

        @pl.when(i == nt - 1)
        def _():
            for k in range(n_pieces):
                w_in_grad_piece(k)
            out = pltpu.make_async_copy(gacc, gin_ref, gsem)
            out.start()
            for j in range(CONV_K):
                dw_ref[j:j + 1, :] = _colsum(dw8[j])
            copies = _xchg_copies(x_ins, x_outs, xssem, xrsem)
            for cp in copies:
                cp.wait_recv()
            for cp in copies:
                cp.wait_send()
            out.wait()

    rev = lambda cols: pl.BlockSpec((tm, cols), lambda i: (nt - 1 - i, 0))
    halo = pl.BlockSpec((HALO, 2 * CONV_CH), lambda i: (jnp.maximum((nt - 1 - i) * hpt - 1, 0), 0))
    wide = lambda n: pltpu.VMEM((tm, HGRN_W), n)
    hbm = pl.BlockSpec(memory_space=pl.ANY)
    outs = pl.pallas_call(
        body, name="mixers_bwd", grid=(nt,),
        in_specs=[rev(IN_COLS), halo, rev(D_MODEL), rev(CONV_CH), rev(HGRN_W),
                  pl.BlockSpec((nch, N_HEADS, HEAD_D, HEAD_D), lambda i: (nt - 1 - i, 0, 0, 0)), rev(D_MODEL),
                  _full((HALO, CONV_CH))] + [_full((1, CONV_CH))] * 4 + [_full((2, HGRN_W))] + [hbm] * nx,
        out_specs=[rev(IN_COLS), _full((8, D_MODEL)), _full((8, CONV_CH)), _full((HALO, CONV_CH))]
        + [hbm] * (nx + 1),
        out_shape=[_big((T, IN_COLS), BF16), jax.ShapeDtypeStruct((8, D_MODEL), F32),
                   jax.ShapeDtypeStruct((8, CONV_CH), F32), jax.ShapeDtypeStruct((HALO, CONV_CH), F32)]
        + [_big(pb.shape, BF16) for pb in pairs_b] + [_big((N_CHIPS, D_MODEL, nb), F32)],
        scratch_shapes=[pltpu.VMEM((tm + HALO + SUB, CONV_CH), F32), pltpu.VMEM((tm + HALO, CONV_CH), F32),
                        pltpu.VMEM((HALO, CONV_CH), F32), pltpu.VMEM((N_HEADS, HEAD_D, HEAD_D), F32),
                        wide(BF16), wide(BF16), wide(BF16), wide(BF16), wide(BF16),
                        wide(F32), wide(F32), wide(F32), wide(F32), wide(F32), wide(F32),
                        pltpu.VMEM((tm + 2 * SUB, CONV_CH), F32), pltpu.VMEM((HALO, SUB, CONV_CH), F32),
                        pltpu.VMEM((tm + SUB, CONV_CH), F32), pltpu.VMEM((tm, tm), BF16), pltpu.VMEM((tm, tm), BF16),
                        pltpu.VMEM((tm, tm), BF16), pltpu.VMEM((CONV_CH, CONV_CH), BF16),
                        pltpu.VMEM((N_CHIPS, D_MODEL, nb), F32), pltpu.VMEM((tm, IN_COLS), BF16),
                        pltpu.VMEM((tm, D_MODEL), BF16), pltpu.SemaphoreType.DMA]
        + _xchg_sems(nx),
        compiler_params=_cp(("arbitrary",), 56),
    )(*_hbm(p, p, dcat, ys, o, states, h1), wdw, *vecs, lbl, *[_hbm(pb) for pb in pairs_b])
    return outs[:4], outs[4:4 + nx], outs[4 + nx]


def _mix_in_bwd(dp, w_in_g, x, dx1, modr, g_pre, pairs_b):
    T = x.shape[0]
    tm = _tok_tile(T)
    nt = T // tm
    nb = IN_COLS // N_CHIPS
    nx = len(pairs_b)

    def body(*refs):
        dp_ref, w_ref, x_ref, dx1_ref, mod_ref, g_ref = refs[:6]
        x_ins = refs[6:6 + nx]
        gx_ref, st_ref = refs[6 + nx:8 + nx]
        x_outs = refs[8 + nx:8 + 2 * nx]
        xssem, xrsem = refs[8 + 2 * nx:]
        i = pl.program_id(0)

        @pl.when(i == 0)
        def _():
            for cp in _xchg_copies(x_ins, x_outs, xssem, xrsem):
                cp.start()
            st_ref[...] = jnp.zeros(st_ref.shape, F32)

        dh = None
        for j in range(N_CHIPS):
            part = _dot_nt(dp_ref[:, j * nb:(j + 1) * nb], w_ref[j])
            dh = part if dh is None else dh + part
        xv = x_ref[...]
        rs = lax.rsqrt(jnp.mean(xv * xv, axis=-1, keepdims=True) + RMS_EPS)
        xn = xv * rs
        st_ref[0:1, :] += _colsum(dh)
        st_ref[1:2, :] += _colsum(dh * (xn * g_ref[...]))
        dsc = dh * (1.0 + mod_ref[1:2, :])
        st_ref[2:3, :] += _colsum(dsc * xn)
        gx_ref[...] = dx1_ref[...] + _rms_bwd(dsc * g_ref[...], xn, rs)

        @pl.when(i == nt - 1)
        def _():
            copies = _xchg_copies(x_ins, x_outs, xssem, xrsem)
            for cp in copies:
                cp.wait_recv()
            for cp in copies:
                cp.wait_send()

    tile = pl.BlockSpec((tm, D_MODEL), lambda i: (i, 0))
    hbm = pl.BlockSpec(memory_space=pl.ANY)
    outs = pl.pallas_call(
        body, name="mix_in_bwd", grid=(nt,),
        in_specs=[pl.BlockSpec((tm, IN_COLS), lambda i: (i, 0)), _full((N_CHIPS, D_MODEL, nb)), tile, tile,
                  _full((6, D_MODEL)), _full((1, D_MODEL))] + [hbm] * nx,
        out_specs=[tile, _full((8, D_MODEL))] + [hbm] * nx,
        out_shape=[_big((T, D_MODEL), F32), jax.ShapeDtypeStruct((8, D_MODEL), F32)]
        + [_big(pb.shape, BF16) for pb in pairs_b],
        scratch_shapes=_xchg_sems(nx),
        compiler_params=_cp(("arbitrary",), 48),
    )(*_hbm(dp, w_in_g, x, dx1), modr, g_pre, *[_hbm(pb) for pb in pairs_b])
    return outs[:2], outs[2:]


def _weight_grad(a, b, a_blocked, b_blocked, name):
    T = a.shape[0]
    tt = min(GRAD_TILE, T)
    nt = T // tt
    ka = a.shape[1] // N_CHIPS if a_blocked else a.shape[1]
    nb = b.shape[1] // N_CHIPS if b_blocked else b.shape[1]

    def body(a_ref, b_ref, o_ref, ob_ref):
        t = pl.program_id(1)

        @pl.when(t == 0)
        def _():
            o_ref[...] = jnp.zeros(o_ref.shape, F32)

        for cols in _row_chains(nb):
            o_ref[0, :, cols] += _dot_tn(a_ref[...], b_ref[:, cols])

        @pl.when(t == nt - 1)
        def _():
            ob_ref[0] = o_ref[0].astype(BF16)

    blk = pl.BlockSpec((1, ka, nb), lambda j, t: (j, 0, 0))
    return pl.pallas_call(
        body, name=name, grid=(N_CHIPS, nt),
        in_specs=[pl.BlockSpec((tt, ka), (lambda j, t: (t, j)) if a_blocked else (lambda j, t: (t, 0))),
                  pl.BlockSpec((tt, nb), (lambda j, t: (t, j)) if b_blocked else (lambda j, t: (t, 0)))],
        out_specs=[blk, blk],
        out_shape=[_big((N_CHIPS, ka, nb), F32), _big((N_CHIPS, ka, nb), BF16)],
        compiler_params=_cp(("arbitrary", "arbitrary"), 48),
    )(*_hbm(a, b))


R_LOSS = 0
R_FFN = 8
R_OUT = 16
R_IN = 24
R_BIN = 32
R_512 = 40
R_DW = 48
N_STAT_ROWS = 80
MOD_ROWS = (R_IN + 0, R_IN + 1, R_OUT + 0, R_FFN + 0, R_FFN + 1, R_LOSS + 1)


def _small_update(gath, params):
    names = ["b_ada", "lb_logits", "g_pre_mix", "b_in", "b_dw", "gn_gain", "gn_bias", "g_hgrn_out", "g_post_mix",
             "g_pre_ffn", "g_post_ffn"]
    flat = []
    for n in names:
        flat += list(params[n])
    n_in = 1 + len(flat)

    def body(*refs):
        g_ref = refs[0]
        prm = {n: refs[1 + 3 * k:4 + 3 * k] for k, n in enumerate(names)}
        outs = refs[n_in:]
        loss_ref, dmod_ref, dwdw_ref = outs[0], outs[1], outs[2]
        res = {n: outs[3 + 4 * k:7 + 4 * k] for k, n in enumerate(names)}
        red = g_ref[0]
        for dev in range(1, N_DEV):
            red = red + g_ref[dev]
        loss_ref[...] = jnp.broadcast_to(
            (0.5 / D_MODEL) * jnp.sum(red[R_LOSS:R_LOSS + 1, :], axis=-1, keepdims=True), loss_ref.shape)
        for dev in range(N_DEV):
            for k, r in enumerate(MOD_ROWS):
                dmod_ref[dev:dev + 1, k * D_MODEL:(k + 1) * D_MODEL] = g_ref[dev, r:r + 1, :]
        dwdw_ref[...] = red[R_DW:R_DW + HALO, 0:CONV_CH]

        def finish(name, pieces):
            w_ref, m_ref, v_ref = prm[name]
            g_out, d_out, m_out, v_out = res[name]
            for rsl, lsl, g in pieces:
                d, m2, v2 = _adam_math(w_ref[rsl, lsl], g, m_ref[rsl, lsl], v_ref[rsl, lsl])
                g_out[rsl, lsl] = g
                d_out[rsl, lsl] = d
                m_out[rsl, lsl] = m2
                v_out[rsl, lsl] = v2

        one = slice(0, 1)
        row = lambda r: red[r:r + 1, :]
        half = lambda r: red[r:r + 1, 0:CONV_CH]
        finish("b_ada", [(one, slice(k * D_MODEL, (k + 1) * D_MODEL), row(r)) for k, r in enumerate(MOD_ROWS)])
        finish("b_in", [(one, slice(k * D_MODEL, (k + 1) * D_MODEL), row(R_BIN + k)) for k in range(3)])
        finish("g_pre_mix", [(one, slice(None), row(R_IN + 2))])
        finish("g_post_mix", [(one, slice(None), row(R_OUT + 1))])
        finish("g_pre_ffn", [(one, slice(None), row(R_FFN + 2))])
        finish("g_post_ffn", [(one, slice(None), row(R_LOSS + 2))])
        finish("b_dw", [(one, slice(None), half(R_512 + 0))])
        finish("gn_gain", [(one, slice(None), half(R_512 + 1))])
        finish("gn_bias", [(one, slice(None), half(R_512 + 2))])
        finish("g_hgrn_out", [(one, slice(None), half(R_512 + 3))])
        s0, s1 = _lower_bound(prm["lb_logits"][0])
        dlb = half(R_512 + 4)
        finish("lb_logits", [(slice(0, 1), slice(None), dlb * s0 * (1.0 - s0)),
                             (slice(1, 2), slice(None), -dlb * s0 * s1)])

    vm = pl.BlockSpec(memory_space=pltpu.VMEM)
    out_shape = [jax.ShapeDtypeStruct((8, 128), F32), jax.ShapeDtypeStruct((N_DEV, 6 * D_MODEL), F32),
                 jax.ShapeDtypeStruct((HALO, CONV_CH), F32)]
    for n in names:
        out_shape += [jax.ShapeDtypeStruct(params[n][0].shape, F32)] * 4
    outs = pl.pallas_call(
        body, name="small_update", out_shape=out_shape,
        in_specs=[vm] * n_in, out_specs=[vm] * len(out_shape),
        compiler_params=_cp(None, 32),
    )(gath, *flat)
    return outs[0], outs[1], outs[2], {n: outs[3 + 4 * k:7 + 4 * k] for k, n in enumerate(names)}


def _wdw_adam(w, g, m, v):
    def body(w_ref, g_ref, m_ref, v_ref, d_out, m_out, v_out):
        d, m2, v2 = _adam_math(w_ref[...], g_ref[...], m_ref[...], v_ref[...])
        d_out[...] = d
        m_out[...] = m2
        v_out[...] = v2

    vm = pl.BlockSpec(memory_space=pltpu.VMEM)
    return pl.pallas_call(
        body, name="wdw_adam", out_shape=[jax.ShapeDtypeStruct(w.shape, F32)] * 3,
        in_specs=[vm] * 4, out_specs=[vm] * 3, compiler_params=_cp(None, 16),
    )(w, g, m, v)


def kernel(x, c, w_ada, b_ada, lb_logits, g_pre_mix, w_in, b_in, w_dw, b_dw, gn_gain, gn_bias, g_hgrn_out, w_out, g_post_mix, g_pre_ffn, w_up, w_down, g_post_ffn, loss_target, m_w_ada, m_b_ada, m_lb_logits, m_g_pre_mix, m_w_in, m_b_in, m_w_dw, m_b_dw, m_gn_gain, m_gn_bias, m_g_hgrn_out, m_w_out, m_g_post_mix, m_g_pre_ffn, m_w_up, m_w_down, m_g_post_ffn, v_w_ada, v_b_ada, v_lb_logits, v_g_pre_mix, v_w_in, v_b_in, v_w_dw, v_b_dw, v_gn_gain, v_gn_bias, v_g_hgrn_out, v_w_out, v_g_post_mix, v_g_pre_ffn, v_w_up, v_w_down, v_g_post_ffn):
    ax, ay, ac = lax.axis_index("x"), lax.axis_index("y"), lax.axis_index("c")
    chip = 2 * ax + ay
    T = x.shape[1]
    xs, tgt = x[0], loss_target[0]
    ada_cols = w_ada.shape[2]

    b_sh = lax.dynamic_slice_in_dim(b_ada, chip * ada_cols, ada_cols, axis=1)
    wdw_pad = jnp.pad(w_dw[0], ((0, HALO - CONV_K), (0, 0)))
    chip1 = jnp.reshape(chip, (1,)).astype(jnp.int32)
    place = jnp.stack([ac, chip]).astype(jnp.int32)
    _, c8, modg, wdwg = _ada_exchange(c, w_ada[0], b_sh, wdw_pad)
    modr = modg.reshape(6, D_MODEL)
    wdw_all = jnp.transpose(wdwg, (1, 0, 2)).reshape(HALO, CONV_CH)
    bufs = {t: _cast_own(chip1, w[0], "cast_" + t)
            for w, t in ((w_in, "w_in"), (w_out, "w_out"), (w_up, "w_up"), (w_down, "w_down"))}
    vec = (b_dw, gn_gain, gn_bias, g_hgrn_out)

    p, h1, w_in_g, w_out_g = _mix_in_fwd(chip1, xs, modr, g_pre_mix, b_in.reshape(N_CHIPS, 1, IN_COLS // N_CHIPS),
                                          bufs["w_in"], bufs["w_out"])
    w_out_f = w_out_g.reshape(D_MODEL, D_MODEL)
    (cat, ys, o, states, y, x1, h2), (w_up_g, w_down_g) = _mixers_fwd(
        p, wdw_all, vec, lb_logits, w_out_f, xs, modr, g_post_mix, g_pre_ffn, [bufs["w_up"], bufs["w_down"]])
    w_down_f = w_down_g.reshape(D_FF, D_MODEL)
    r, dy2, dx2, st_loss = _ffn_fwd(h2, w_up_g, w_down_f, x1, tgt, modr, g_post_ffn)

    def pair_sums(grads, got, tags):
        return [_pair_sum(place, g, o_, "pair_sum_" + t) for (g, _), o_, t in zip(grads, got, tags)]

    da, dx1, st_ffn = _ffn_bwd(dy2, r, x1, dx2, w_up_g, w_down_f, modr, g_pre_ffn)
    g_up = _weight_grad(h2, da, False, True, "grad_w_up")
    g_down = _weight_grad(r, dy2, True, False, "grad_w_down")
    (dcat, st_out, g_out), got_ud = _mix_out_bwd(dx1, y, cat, w_out_f, modr, g_post_mix, [g_up[1], g_down[1]])
    g_out = g_out.reshape(N_CHIPS, D_MODEL // N_CHIPS, D_MODEL)
    got_o = _pair_swap([g_out], "pair_swap_w_out")
    early = pair_sums([(g_out, None), g_up, g_down], list(got_o) + list(got_ud), ["w_out", "w_up", "w_down"])
    (dp, st_bin, st_512, dwdw), got_early, g_in = _mixers_bwd(p, dcat, ys, o, states, h1, wdw_all, vec, lb_logits,
                                                              [pb for _, pb in early])
    late = pair_sums([(g_in, None)], _pair_swap([g_in], "pair_swap_w_in"), ["w_in"])
    (grad_x, st_in), got_late = _mix_in_bwd(dp, w_in_g, xs, dx1, modr, g_pre_mix, [late[0][1]])
    fulls = [_chip_sum(place, pf, gb, "chip_sum_" + t)
             for (pf, _), gb, t in zip(late + early, list(got_late) + list(got_early), ["w_in", "w_out", "w_up", "w_down"])]

    pad_lanes = lambda s: jnp.pad(s, ((0, 0), (0, D_MODEL - s.shape[1])))
    stats = jnp.concatenate([st_loss, st_ffn, st_out, st_in, st_bin, pad_lanes(st_512), pad_lanes(dwdw)], axis=0)
    (g_w_in, g_w_out, g_w_up, g_w_down), gath = _final_exchange(fulls, stats)
    small = {"b_ada": (b_ada, m_b_ada, v_b_ada), "lb_logits": (lb_logits, m_lb_logits, v_lb_logits),
             "g_pre_mix": (g_pre_mix, m_g_pre_mix, v_g_pre_mix), "b_in": (b_in, m_b_in, v_b_in),
             "b_dw": (b_dw, m_b_dw, v_b_dw), "gn_gain": (gn_gain, m_gn_gain, v_gn_gain),
             "gn_bias": (gn_bias, m_gn_bias, v_gn_bias), "g_hgrn_out": (g_hgrn_out, m_g_hgrn_out, v_g_hgrn_out),
             "g_post_mix": (g_post_mix, m_g_post_mix, v_g_post_mix), "g_pre_ffn": (g_pre_ffn, m_g_pre_ffn, v_g_pre_ffn),
             "g_post_ffn": (g_post_ffn, m_g_post_ffn, v_g_post_ffn)}
    loss_t, dmod_all, dwdw_sum, sres = _small_update(gath, small)
    loss = loss_t[0, 0]

    res = dict(sres)
    dmod_sh = lax.dynamic_slice_in_dim(dmod_all, chip * ada_cols, ada_cols, axis=1)
    res["w_ada"] = [t[None] for t in _ada_grad_adam(jnp.transpose(c8), dmod_sh, w_ada[0], m_w_ada[0], v_w_ada[0])]
    g_wdw = lax.dynamic_slice_in_dim(dwdw_sum, chip * HEAD_D, HEAD_D, axis=1)[:CONV_K][None]
    res["w_dw"] = [g_wdw] + list(_wdw_adam(w_dw, g_wdw, m_w_dw, v_w_dw))
    for name, g, w, m, v in (("w_in", g_w_in, w_in, m_w_in, v_w_in), ("w_out", g_w_out, w_out, m_w_out, v_w_out),
                             ("w_up", g_w_up, w_up, m_w_up, v_w_up), ("w_down", g_w_down, w_down, m_w_down, v_w_down)):
        d, m2, v2 = _adam_big(w[0], g, m[0], v[0], "adam_" + name)
        res[name] = [g[None], d[None], m2[None], v2[None]]

    order = ["w_ada", "b_ada", "lb_logits", "g_pre_mix", "w_in", "b_in", "w_dw", "b_dw", "gn_gain", "gn_bias",
             "g_hgrn_out", "w_out", "g_post_mix", "g_pre_ffn", "w_up", "w_down", "g_post_ffn"]
    out = [loss, grad_x[None]]
    for k in range(4):
        out += [res[n][k] for n in order]
    return tuple(out)
```

```python
import jax
import jax.numpy as jnp
from jax import lax
from jax.experimental import pallas as pl
from jax.experimental.pallas import tpu as pltpu

F32, BF16 = jnp.float32, jnp.bfloat16
D_MODEL = 1024
CONV_CH = 512
HGRN_W = 512
N_HEADS = 4
HEAD_D = 128
CONV_K = 31
GN_GROUP = 64
GN_SHIFT = 6
IN_COLS = 3072
D_FF = 4096
CHUNK = 64
CHUNK_SHIFT = 6
N_CHIPS = 4
N_DEV = 8
RMS_EPS = 1e-6
GN_EPS = 1e-5
ADAM_LR, ADAM_B1, ADAM_B2, ADAM_EPS, ADAM_WD, ADAM_STEP = 0.001, 0.9, 0.999, 1e-08, 0.01, 10
TOK_TILE = 512
MIXIN_TILE = 1024
MIXB_TILE = 256
FFN_TILE = 1024
FFN_BLOCK = 512
GRAD_TILE = 2048
HALO = 32
SUB = 8
LANE = 128
CONV_ROWS = 128
CHUNK_UNROLL = 8
MIB = 1 << 20
MESH = pl.DeviceIdType.MESH
OTHER_CHIPS = ((0, 1), (1, 0), (1, 1))


def _cp(sem=None, vmem_mib=48):
    return pltpu.CompilerParams(dimension_semantics=sem, vmem_limit_bytes=vmem_mib * MIB)


def _dot(a, b):
    return jnp.dot(a, b, preferred_element_type=F32)


def _dot_nt(a, b):
    return lax.dot_general(a, b, (((1,), (1,)), ((), ())), preferred_element_type=F32)


def _dot_tn(a, b):
    return lax.dot_general(a, b, (((0,), (0,)), ((), ())), preferred_element_type=F32)


def _sig(v):
    return 0.5 * jnp.tanh(0.5 * v) + 0.5


def _colsum(v):
    return jnp.sum(v, axis=0, keepdims=True)


def _flip(v, b):
    return 1 - v if b else v


def _rcopy(src, dst, ssem, rsem, dev):
    return pltpu.make_async_remote_copy(src_ref=src, dst_ref=dst, send_sem=ssem, recv_sem=rsem,
                                        device_id=dev, device_id_type=MESH)


def _place():
    return lax.axis_index("x"), lax.axis_index("y"), lax.axis_index("c")


def _full(shape):
    return pl.BlockSpec(shape, lambda *_: (0,) * len(shape))


def _big(shape, dtype):
    return pltpu.HBM(shape, dtype)


def _hbm(*arrays):
    out = [pltpu.with_memory_space_constraint(a, pltpu.HBM) for a in arrays]
    return out[0] if len(out) == 1 else out


def _split2(v):
    hi = v.astype(BF16)
    lo = (v - hi.astype(F32)).astype(BF16)
    return hi, lo


def _split3(v):
    h1 = v.astype(BF16)
    r1 = v - h1.astype(F32)
    h2 = r1.astype(BF16)
    h3 = (r1 - h2.astype(F32)).astype(BF16)
    return h1, h2, h3


def _mm3(mat, v):
    h1, h2, h3 = _split3(v)
    return _dot(mat, h1) + _dot(mat, h2) + _dot(mat, h3)


def _gn_matrix():
    r = lax.broadcasted_iota(jnp.int32, (CONV_CH, CONV_CH), 0) >> GN_SHIFT
    c = lax.broadcasted_iota(jnp.int32, (CONV_CH, CONV_CH), 1) >> GN_SHIFT
    return jnp.where(r == c, 1.0 / GN_GROUP, 0.0).astype(BF16)


def _gmean(v, gmat):
    hi, lo = _split2(v)
    return _dot(hi, gmat) + _dot(lo, gmat)


def _chunk_masks(tm):
    r = lax.broadcasted_iota(jnp.int32, (tm, tm), 0)
    c = lax.broadcasted_iota(jnp.int32, (tm, tm), 1)
    same = (r >> CHUNK_SHIFT) == (c >> CHUNK_SHIFT)
    one = lambda m: jnp.where(m, 1.0, 0.0).astype(BF16)
    return one(same & (c <= r)), one(same & (c >= r)), one(same)


def _tri():
    return lax.broadcasted_iota(jnp.int32, (CHUNK, CHUNK), 0) >= lax.broadcasted_iota(jnp.int32, (CHUNK, CHUNK), 1)


def _lower_bound(lbl_ref):
    l0, l1 = lbl_ref[0:1, :], lbl_ref[1:2, :]
    mx = jnp.maximum(l0, l1)
    e0, e1 = jnp.exp(l0 - mx), jnp.exp(l1 - mx)
    return e0 / (e0 + e1), e1 / (e0 + e1)


CONV_FWD_TAPS = tuple((j, HALO - (CONV_K - 1) + j) for j in range(CONV_K))
CONV_BWD_TAPS = tuple((j, (CONV_K - 1) - j) for j in range(CONV_K))


def _tap_conv(src_ref, w_ref, row0, taps, lanes):
    acc = None
    for b in range(SUB):
        pb = None
        for j, off in taps:
            if off % SUB == b:
                lo = row0 + off - b
                term = w_ref[j:j + 1, lanes] * src_ref[lo:lo + CONV_ROWS + SUB, lanes]
                pb = term if pb is None else pb + term
        if pb is not None:
            sh = pb[b:b + CONV_ROWS, :]
            acc = sh if acc is None else acc + sh
    return acc


def _hgrn_prep(pq, pf, lb, lower):
    sq = _sig(pq)
    qf = pq * sq
    sf = _sig(pf)
    f = lb + (1.0 - lb) * sf
    logf = jnp.log(f)
    k = 1.0 - f
    G = _mm3(lower, logf)
    rows, cols = G.shape
    g3 = G.reshape(rows // CHUNK, CHUNK, cols)
    Gl = jnp.broadcast_to(g3[:, CHUNK - 1:CHUNK, :], g3.shape).reshape(rows, cols)
    eG, enG, eGlG = jnp.exp(G), jnp.exp(-G), jnp.exp(Gl - G)
    return dict(sq=sq, sf=sf, f=f, Gl=Gl, eG=eG, enG=enG, eGlG=eGlG, qt=qf * eG, kt=k * enG, kh=k * eGlG)


def _ada_exchange(c_row, w_ada, b_sh, wdw_pad):
    ncol = w_ada.shape[1]

    def body(c_ref, w_ref, b_ref, wdw_ref, call_ref, c8_ref, modg_ref, wdwg_ref, rows_s, sa, ra, sw, rw, sm, rm):
        x, y, c = _place()
        me = 4 * x + 2 * y + c
        chip = 2 * x + y
        cv = c_ref[...]
        call_ref[me] = cv * _sig(cv)
        wdwg_ref[chip] = wdw_ref[...]
        sends = []
        for m in range(1, N_DEV):
            peer = (_flip(x, m >> 2), _flip(y, (m >> 1) & 1), _flip(c, m & 1))
            cp = _rcopy(call_ref.at[me], call_ref.at[me], sa.at[m - 1], ra.at[m - 1], peer)
            cp.start()
            sends.append(cp)
        for k, (fx, fy) in enumerate(OTHER_CHIPS):
            peer = (_flip(x, fx), _flip(y, fy), c)
            cp = _rcopy(wdwg_ref.at[chip], wdwg_ref.at[chip], sw.at[k], rw.at[k], peer)
            cp.start()
            sends.append(cp)
        for m in range(1, N_DEV):
            peer = (_flip(x, m >> 2), _flip(y, (m >> 1) & 1), _flip(c, m & 1))
            pid = 4 * peer[0] + 2 * peer[1] + peer[2]
            _rcopy(call_ref.at[pid], call_ref.at[pid], sa.at[m - 1], ra.at[m - 1], peer).wait_recv()
        for b in range(N_DEV):
            c8_ref[b:b + 1, :] = call_ref[b]
        mod_all = _dot(c8_ref[...].astype(BF16), w_ref[...].astype(BF16)) + b_ref[...]
        for b in range(N_DEV):
            rows_s[b] = mod_all[b:b + 1, :]
        modg_ref[chip] = rows_s[me]
        for k, (fx, fy) in enumerate(OTHER_CHIPS):
            peer = (_flip(x, fx), _flip(y, fy), c)
            pid = 4 * peer[0] + 2 * peer[1] + peer[2]
            cp = _rcopy(rows_s.at[pid], modg_ref.at[chip], sm.at[k], rm.at[k], peer)
            cp.start()
            sends.append(cp)
        for k, (fx, fy) in enumerate(OTHER_CHIPS):
            peer = (_flip(x, fx), _flip(y, fy), c)
            pchip = 2 * peer[0] + peer[1]
            _rcopy(rows_s.at[0], modg_ref.at[pchip], sm.at[k], rm.at[k], peer).wait_recv()
            _rcopy(wdwg_ref.at[pchip], wdwg_ref.at[pchip], sw.at[k], rw.at[k], peer).wait_recv()
        for cp in sends:
            cp.wait_send()

    vm = pl.BlockSpec(memory_space=pltpu.VMEM)
    return pl.pallas_call(
        body, name="ada_exchange",
        out_shape=[jax.ShapeDtypeStruct((N_DEV, 1, D_MODEL), F32), jax.ShapeDtypeStruct((N_DEV, D_MODEL), F32),
                   jax.ShapeDtypeStruct((N_CHIPS, 1, ncol), F32), jax.ShapeDtypeStruct((N_CHIPS, HALO, HEAD_D), F32)],
        in_specs=[vm] * 4, out_specs=[vm] * 4,
        scratch_shapes=[pltpu.VMEM((N_DEV, 1, ncol), F32),
                        pltpu.SemaphoreType.DMA((N_DEV - 1,)), pltpu.SemaphoreType.DMA((N_DEV - 1,)),
                        pltpu.SemaphoreType.DMA((3,)), pltpu.SemaphoreType.DMA((3,)),
                        pltpu.SemaphoreType.DMA((3,)), pltpu.SemaphoreType.DMA((3,))],
        compiler_params=_cp(None, 32),
    )(c_row, w_ada, b_sh, wdw_pad)


def _cast_own(chip1, shard, name):
    rows, cols = shard.shape
    tr = _row_tile(rows)

    def body(ch_ref, s_ref, o_ref):
        o_ref[0] = s_ref[...].astype(BF16)

    gs = pltpu.PrefetchScalarGridSpec(
        num_scalar_prefetch=1, grid=(rows // tr,),
        in_specs=[pl.BlockSpec((tr, cols), lambda i, ch: (i, 0))],
        out_specs=pl.BlockSpec((1, tr, cols), lambda i, ch: (ch[0], i, 0)))
    return pl.pallas_call(
        body, name=name, grid_spec=gs, out_shape=_big((N_CHIPS, rows, cols), BF16),
        compiler_params=_cp(("arbitrary",), 32),
    )(chip1, _hbm(shard))


def _slab(buf, ch, core):
    hs = buf.shape[1] // 2
    return buf.at[ch, pl.ds(core * hs, hs), :]


def _gather_start(bufs, ssem, rsem, relations=(0, 1, 2)):
    x, y, c = _place()
    chip = 2 * x + y
    for k in relations:
        fx, fy = OTHER_CHIPS[k]
        peer = (_flip(x, fx), _flip(y, fy), c)
        for t, buf in enumerate(bufs):
            _rcopy(_slab(buf, chip, c), _slab(buf, chip, c), ssem.at[t * 3 + k], rsem.at[t * 3 + k], peer).start()


def _gather_pass_on(bufs, ssem, rsem):
    nt = len(bufs)
    x, y, c = _place()
    sibling = (x, y, 1 - c)
    for k, (fx, fy) in enumerate(OTHER_CHIPS):
        peer = (_flip(x, fx), _flip(y, fy), c)
        pchip = 2 * peer[0] + peer[1]
        for t, buf in enumerate(bufs):
            _rcopy(_slab(buf, pchip, c), _slab(buf, pchip, c), ssem.at[t * 3 + k], rsem.at[t * 3 + k], peer).wait_recv()
            _rcopy(_slab(buf, pchip, c), _slab(buf, pchip, c), ssem.at[3 * nt + t * 3 + k],
                   rsem.at[3 * nt + t * 3 + k], sibling).start()


def _gather_drain(bufs, ssem, rsem):
    nt = len(bufs)
    x, y, c = _place()
    chip = 2 * x + y
    sibling = (x, y, 1 - c)
    for k, (fx, fy) in enumerate(OTHER_CHIPS):
        peer = (_flip(x, fx), _flip(y, fy), c)
        pchip = 2 * peer[0] + peer[1]
        for t, buf in enumerate(bufs):
            _rcopy(_slab(buf, pchip, 1 - c), _slab(buf, pchip, 1 - c), ssem.at[3 * nt + t * 3 + k],
                   rsem.at[3 * nt + t * 3 + k], sibling).wait_recv()
            _rcopy(_slab(buf, chip, c), _slab(buf, chip, c), ssem.at[t * 3 + k], rsem.at[t * 3 + k], peer).wait_send()
            _rcopy(_slab(buf, pchip, c), _slab(buf, pchip, c), ssem.at[3 * nt + t * 3 + k],
                   rsem.at[3 * nt + t * 3 + k], sibling).wait_send()


def _gather_finish(bufs, ssem, rsem):
    _gather_pass_on(bufs, ssem, rsem)
    _gather_drain(bufs, ssem, rsem)


def _gather_arrive(bufs, k, ssem, rsem):
    nt = len(bufs)
    x, y, c = _place()
    fx, fy = OTHER_CHIPS[k]
    peer = (_flip(x, fx), _flip(y, fy), c)
    pchip = 2 * peer[0] + peer[1]
    for t, buf in enumerate(bufs):
        _rcopy(_slab(buf, pchip, c), _slab(buf, pchip, c), ssem.at[t * 3 + k], rsem.at[t * 3 + k], peer).wait_recv()
        _rcopy(_slab(buf, pchip, c), _slab(buf, pchip, c), ssem.at[3 * nt + t * 3 + k],
               rsem.at[3 * nt + t * 3 + k], (x, y, 1 - c)).start()
    for t, buf in enumerate(bufs):
        _rcopy(_slab(buf, pchip, 1 - c), _slab(buf, pchip, 1 - c), ssem.at[3 * nt + t * 3 + k],
               rsem.at[3 * nt + t * 3 + k], (x, y, 1 - c)).wait_recv()


def _gather_sends_done(bufs, ssem, rsem):
    nt = len(bufs)
    x, y, c = _place()
    chip = 2 * x + y
    for k, (fx, fy) in enumerate(OTHER_CHIPS):
        peer = (_flip(x, fx), _flip(y, fy), c)
        pchip = 2 * peer[0] + peer[1]
        for t, buf in enumerate(bufs):
            _rcopy(_slab(buf, chip, c), _slab(buf, chip, c), ssem.at[t * 3 + k], rsem.at[t * 3 + k], peer).wait_send()
            _rcopy(_slab(buf, pchip, c), _slab(buf, pchip, c), ssem.at[3 * nt + t * 3 + k],
                   rsem.at[3 * nt + t * 3 + k], (x, y, 1 - c)).wait_send()


def _ring_parts(buf):
    x, y, c = _place()
    ynb, xnb = (x, 1 - y, c), (1 - x, y, c)
    ychip, xchip, dchip = 2 * x + (1 - y), 2 * (1 - x) + y, 2 * (1 - x) + (1 - y)
    hs = buf.shape[1] // 2

    def piece(ch, q):
        return buf.at[ch, pl.ds(c * hs + q * (hs // 2), hs // 2), :]

    return ynb, xnb, ychip, xchip, dchip, piece


def _ring_start(bufs, ssem, rsem):
    x, y, c = _place()
    chip = 2 * x + y
    for t, buf in enumerate(bufs):
        ynb, xnb, _, _, _, _ = _ring_parts(buf)
        _rcopy(_slab(buf, chip, c), _slab(buf, chip, c), ssem.at[2 * t], rsem.at[2 * t], ynb).start()
        _rcopy(_slab(buf, chip, c), _slab(buf, chip, c), ssem.at[2 * t + 1], rsem.at[2 * t + 1], xnb).start()


def _ring_forward(bufs, ssem, rsem):
    nt = len(bufs)
    _, _, c = _place()
    for t, buf in enumerate(bufs):
        ynb, xnb, ychip, xchip, _, piece = _ring_parts(buf)
        _rcopy(_slab(buf, ychip, c), _slab(buf, ychip, c), ssem.at[2 * t], rsem.at[2 * t], ynb).wait_recv()
        _rcopy(piece(ychip, 0), piece(ychip, 0), ssem.at[2 * nt + 2 * t], rsem.at[2 * nt + 2 * t], xnb).start()
        _rcopy(_slab(buf, xchip, c), _slab(buf, xchip, c), ssem.at[2 * t + 1], rsem.at[2 * t + 1], xnb).wait_recv()
        _rcopy(piece(xchip, 1), piece(xchip, 1), ssem.at[2 * nt + 2 * t + 1], rsem.at[2 * nt + 2 * t + 1], ynb).start()


def _ring_finish(bufs, ssem, rsem):
    nt = len(bufs)
    x, y, c = _place()
    chip = 2 * x + y
    sibling = (x, y, 1 - c)
    for t, buf in enumerate(bufs):
        ynb, xnb, ychip, xchip, dchip, piece = _ring_parts(buf)
        _rcopy(piece(dchip, 0), piece(dchip, 0), ssem.at[2 * nt + 2 * t], rsem.at[2 * nt + 2 * t], xnb).wait_recv()
        _rcopy(piece(dchip, 1), piece(dchip, 1), ssem.at[2 * nt + 2 * t + 1], rsem.at[2 * nt + 2 * t + 1],
               ynb).wait_recv()
        for k, ch in enumerate((ychip, xchip, dchip)):
            _rcopy(_slab(buf, ch, c), _slab(buf, ch, c), ssem.at[4 * nt + 3 * t + k], rsem.at[4 * nt + 3 * t + k],
                   sibling).start()
    for t, buf in enumerate(bufs):
        ynb, xnb, ychip, xchip, dchip, piece = _ring_parts(buf)
        for k, ch in enumerate((ychip, xchip, dchip)):
            _rcopy(_slab(buf, ch, 1 - c), _slab(buf, ch, 1 - c), ssem.at[4 * nt + 3 * t + k],
                   rsem.at[4 * nt + 3 * t + k], sibling).wait_recv()
            _rcopy(_slab(buf, ch, c), _slab(buf, ch, c), ssem.at[4 * nt + 3 * t + k], rsem.at[4 * nt + 3 * t + k],
                   sibling).wait_send()
        _rcopy(_slab(buf, chip, c), _slab(buf, chip, c), ssem.at[2 * t], rsem.at[2 * t], ynb).wait_send()
        _rcopy(_slab(buf, chip, c), _slab(buf, chip, c), ssem.at[2 * t + 1], rsem.at[2 * t + 1], xnb).wait_send()
        _rcopy(piece(ychip, 0), piece(ychip, 0), ssem.at[2 * nt + 2 * t], rsem.at[2 * nt + 2 * t], xnb).wait_send()
        _rcopy(piece(xchip, 1), piece(xchip, 1), ssem.at[2 * nt + 2 * t + 1], rsem.at[2 * nt + 2 * t + 1],
               ynb).wait_send()


def _ring_sems(nt):
    return [pltpu.SemaphoreType.DMA((7 * nt,)), pltpu.SemaphoreType.DMA((7 * nt,))]


def _gather_sems(nt):
    return [pltpu.SemaphoreType.DMA((6 * nt,)), pltpu.SemaphoreType.DMA((6 * nt,))]


def _pair_copies(ins, outs, ssem, rsem):
    x, y, c = _place()
    copies = []
    for t in range(len(ins)):
        hs = ins[t].shape[1] // 2
        copies.append(_rcopy(ins[t].at[:, pl.ds((1 - c) * hs, hs), :], outs[t], ssem.at[t], rsem.at[t], (x, y, 1 - c)))
    return copies


def _pair_shapes(grads):
    return [_big((g.shape[0], g.shape[1] // 2, g.shape[2]), g.dtype) for g in grads]


def _pair_sems(nt):
    return [pltpu.SemaphoreType.DMA((nt,)), pltpu.SemaphoreType.DMA((nt,))]


def _pair_swap(grads, name):
    nt = len(grads)
    hbm = pl.BlockSpec(memory_space=pl.ANY)

    def body(*refs):
        copies = _pair_copies(refs[:nt], refs[nt:2 * nt], refs[2 * nt], refs[2 * nt + 1])
        for cp in copies:
            cp.start()
        for cp in copies:
            cp.wait_recv()
        for cp in copies:
            cp.wait_send()

    return pl.pallas_call(
        body, name=name, out_shape=_pair_shapes(grads), in_specs=[hbm] * nt, out_specs=[hbm] * nt,
        scratch_shapes=_pair_sems(nt),
    )(*[_hbm(g) for g in grads])


def _xchg_copies(ins, outs, ssem, rsem):
    x, y, c = _place()
    copies = []
    for k, (fx, fy) in enumerate(OTHER_CHIPS):
        peer = (_flip(x, fx), _flip(y, fy), c)
        for t in range(len(ins)):
            copies.append(_rcopy(ins[t].at[k], outs[t].at[k], ssem.at[t * 3 + k], rsem.at[t * 3 + k], peer))
    return copies


def _xchg_sems(nt):
    return [pltpu.SemaphoreType.DMA((3 * nt,)), pltpu.SemaphoreType.DMA((3 * nt,))]


def _final_exchange(fulls, stats):
    nt = len(fulls)
    rows, cols = stats.shape
    hbm = pl.BlockSpec(memory_space=pl.ANY)
    vm = pl.BlockSpec(memory_space=pltpu.VMEM)

    def body(*refs):
        ins, s_ref = refs[:nt], refs[nt]
        outs, g_ref = refs[nt + 1:2 * nt + 1], refs[2 * nt + 1]
        hssem, hrsem, ssem, rsem = refs[2 * nt + 2:]
        x, y, c = _place()
        me, sibling = (x, y, c), (x, y, 1 - c)
        halves = []
        for t in range(nt):
            hs = ins[t].shape[0] // 2
            mine = pl.ds(c * hs, hs)
            cp = _rcopy(ins[t].at[mine, :], outs[t].at[mine, :], hssem.at[t], hrsem.at[t], sibling)
            cp.start()
            halves.append(cp)

        chips = [(_flip(x, fx), _flip(y, fy)) for fx, fy in OTHER_CHIPS]

        def blk(px, py, pc):
            return g_ref.at[4 * px + 2 * py + pc]

        def copy(k, block, to, src=None):
            return _rcopy(blk(*block) if src is None else src, blk(*block), ssem.at[k], rsem.at[k], to)

        g_ref[4 * x + 2 * y + c] = s_ref[...]
        first = [copy(0, me, sibling, src=s_ref)]
        first += [copy(1 + j, me, (*chip, c), src=s_ref) for j, chip in enumerate(chips)]
        for cp in first:
            cp.start()
        passed = [copy(4 + j, (*chip, c), sibling) for j, chip in enumerate(chips)]
        for j, chip in enumerate(chips):
            copy(1 + j, (*chip, c), me).wait_recv()
            passed[j].start()
        copy(0, sibling, me).wait_recv()
        for j, chip in enumerate(chips):
            copy(4 + j, (*chip, 1 - c), me).wait_recv()
        for t in range(nt):
            hs = ins[t].shape[0] // 2
            other = pl.ds((1 - c) * hs, hs)
            _rcopy(ins[t].at[other, :], outs[t].at[other, :], hssem.at[t], hrsem.at[t], sibling).wait_recv()
        for cp in first + passed + halves:
            cp.wait_send()

    outs = pl.pallas_call(
        body, name="final_exchange",
        out_shape=[_big(f.shape, F32) for f in fulls] + [jax.ShapeDtypeStruct((N_DEV, rows, cols), F32)],
        in_specs=[hbm] * nt + [vm], out_specs=[hbm] * nt + [vm],
        input_output_aliases={t: t for t in range(nt)},
        scratch_shapes=[pltpu.SemaphoreType.DMA((nt,)), pltpu.SemaphoreType.DMA((nt,)),
                        pltpu.SemaphoreType.DMA((7,)), pltpu.SemaphoreType.DMA((7,))],
        compiler_params=_cp(None, 32),
    )(*[_hbm(f) for f in fulls], stats)
    return outs[:nt], outs[nt]


def _row_tile(rows):
    return min(rows, 512)


def _pair_sum(place, grad, got, name):
    nb, hs, cols = got.shape
    tr = _row_tile(hs)
    nr = hs // tr

    def body(pl_ref, g_ref, o_ref, pf_ref, pb_ref):
        j = pl.program_id(1)
        s = g_ref[0] + o_ref[0].astype(F32)

        @pl.when(j == 0)
        def _():
            pf_ref[...] = s

        @pl.when(j > 0)
        def _():
            pb_ref[0] = s.astype(BF16)

    gs = pltpu.PrefetchScalarGridSpec(
        num_scalar_prefetch=1, grid=(nr, nb),
        in_specs=[pl.BlockSpec((1, tr, cols), lambda i, j, p: (p[1] ^ j, p[0] * nr + i, 0)),
                  pl.BlockSpec((1, tr, cols), lambda i, j, p: (p[1] ^ j, i, 0))],
        out_specs=[pl.BlockSpec((tr, cols), lambda i, j, p: (i, 0)),
                   pl.BlockSpec((1, tr, cols), lambda i, j, p: (jnp.maximum(j - 1, 0), i, 0))])
    return pl.pallas_call(
        body, name=name, grid_spec=gs,
        out_shape=[_big((hs, cols), F32), _big((nb - 1, hs, cols), BF16)],
        compiler_params=_cp(("arbitrary", "arbitrary"), 32),
    )(place, *_hbm(grad, got))


def _chip_sum(place, pair_f, got_b, name):
    nb, hs, cols = got_b.shape
    tr = _row_tile(hs)
    nr = hs // tr

    def body(pl_ref, pf_ref, gb_ref, o_ref):
        acc = pf_ref[...]
        for k in range(nb):
            acc = acc + gb_ref[k].astype(F32)
        o_ref[...] = acc

    gs = pltpu.PrefetchScalarGridSpec(
        num_scalar_prefetch=1, grid=(nr,),
        in_specs=[pl.BlockSpec((tr, cols), lambda i, p: (i, 0)),
                  pl.BlockSpec((nb, tr, cols), lambda i, p: (0, i, 0))],
        out_specs=pl.BlockSpec((tr, cols), lambda i, p: (p[0] * nr + i, 0)))
    return pl.pallas_call(
        body, name=name, grid_spec=gs,
        out_shape=_big((2 * hs, cols), F32),
        compiler_params=_cp(("arbitrary",), 32),
    )(place, *_hbm(pair_f, got_b))


def _adam_math(w, g, m, v):
    m2 = ADAM_B1 * m + (1.0 - ADAM_B1) * g
    v2 = ADAM_B2 * v + (1.0 - ADAM_B2) * (g * g)
    m_hat = m2 / (1.0 - ADAM_B1 ** ADAM_STEP)
    v_hat = v2 / (1.0 - ADAM_B2 ** ADAM_STEP)
    delta = -ADAM_LR * (m_hat / (jnp.sqrt(v_hat) + ADAM_EPS) + ADAM_WD * w)
    return delta, m2, v2


def _adam_big(w, g, m, v, name):
    rows, cols = w.shape
    tr = _row_tile(rows)

    def body(w_ref, g_ref, m_ref, v_ref, d_out, m_out, v_out):
        d, m2, v2 = _adam_math(w_ref[...], g_ref[...], m_ref[...], v_ref[...])
        d_out[...] = d
        m_out[...] = m2
        v_out[...] = v2

    spec = pl.BlockSpec((tr, cols), lambda i: (i, 0))
    return pl.pallas_call(
        body, name=name, grid=(rows // tr,), in_specs=[spec] * 4, out_specs=[spec] * 3,
        out_shape=[_big(w.shape, F32)] * 3,
        compiler_params=_cp(("arbitrary",), 48),
    )(*_hbm(w, g, m, v))


def _ada_grad_adam(c8t, dmod_sh, w, m, v):
    rows, cols = w.shape
    tr = _row_tile(rows) // 2

    def body(ct_ref, dm_ref, w_ref, m_ref, v_ref, g_out, d_out, m_out, v_out):
        g = None
        for b in range(N_DEV):
            term = ct_ref[:, b:b + 1] * dm_ref[b:b + 1, :]
            g = term if g is None else g + term
        d, m2, v2 = _adam_math(w_ref[...], g, m_ref[...], v_ref[...])
        g_out[...] = g
        d_out[...] = d
        m_out[...] = m2
        v_out[...] = v2

    spec = pl.BlockSpec((tr, cols), lambda i: (i, 0))
    return pl.pallas_call(
        body, name="ada_grad_adam", grid=(rows // tr,),
        in_specs=[pl.BlockSpec((tr, N_DEV), lambda i: (i, 0)), _full((N_DEV, cols)), spec, spec, spec],
        out_specs=[spec] * 4, out_shape=[_big(w.shape, F32)] * 4,
        compiler_params=_cp(("arbitrary",), 32),
    )(c8t, dmod_sh, *_hbm(w, m, v))


def _tok_tile(t):
    return min(TOK_TILE, t)


def _mix_in_fwd(chip1, x, modr, g_pre, b_in4, w_in_buf, w_out_buf):
    T = x.shape[0]
    tm = min(MIXIN_TILE, T)
    nt = T // tm
    nb = IN_COLS // N_CHIPS

    def body(ch_ref, x_ref, mod_ref, g_ref, b_ref, win_in, wout_in, p_ref, h_ref, win_ref, wout_ref,
             h_all, wblk, lsem, is_sem, ir_sem, os_sem, or_sem):
        k, i = pl.program_id(0), pl.program_id(1)
        chip = ch_ref[0]

        def load_block(blk):
            cp = pltpu.make_async_copy(win_ref.at[blk], wblk, lsem)
            cp.start()
            cp.wait()

        @pl.when((k == 0) & (i == 0))
        def _():
            _gather_start([win_ref], is_sem, ir_sem, relations=(0, 1))
            load_block(chip)

        for r in range(N_CHIPS - 1):
            @pl.when((k == r + 1) & (i == 0))
            def _(r=r):
                _gather_arrive([win_ref], r, is_sem, ir_sem)
                if r == 0:
                    _gather_start([win_ref], is_sem, ir_sem, relations=(2,))
                if r == 1:
                    _gather_start([wout_ref], os_sem, or_sem)
                load_block(chip ^ (r + 1))

        rows = pl.ds(pl.multiple_of(i * tm, tm), tm)

        @pl.when(k == 0)
        def _():
            xv = x_ref[...]
            rstd = lax.rsqrt(jnp.mean(xv * xv, axis=-1, keepdims=True) + RMS_EPS)
            h = (xv * rstd) * g_ref[...] * (1.0 + mod_ref[1:2, :]) + mod_ref[0:1, :]
            hb = h.astype(BF16)
            h_ref[...] = hb
            h_all[rows, :] = hb

        p_ref[...] = _dot(h_all[rows, :], wblk[...]) + b_ref[chip ^ k]

        @pl.when((k == N_CHIPS - 1) & (i == nt - 1))
        def _():
            _gather_sends_done([win_ref], is_sem, ir_sem)
            _gather_finish([wout_ref], os_sem, or_sem)

    hbm = pl.BlockSpec(memory_space=pl.ANY)
    first_pass = lambda k, i, ch: (jnp.where(k == 0, i, nt - 1), 0)
    gs = pltpu.PrefetchScalarGridSpec(
        num_scalar_prefetch=1, grid=(N_CHIPS, nt),
        in_specs=[pl.BlockSpec((tm, D_MODEL), first_pass), pl.BlockSpec((6, D_MODEL), lambda k, i, ch: (0, 0)),
                  pl.BlockSpec((1, D_MODEL), lambda k, i, ch: (0, 0)),
                  pl.BlockSpec((N_CHIPS, 1, nb), lambda k, i, ch: (0, 0, 0)), hbm, hbm],
        out_specs=[pl.BlockSpec((tm, nb), lambda k, i, ch: (i, ch[0] ^ k)), pl.BlockSpec((tm, D_MODEL), first_pass),
                   hbm, hbm],
        scratch_shapes=[pltpu.VMEM((T, D_MODEL), BF16), pltpu.VMEM((D_MODEL, nb), BF16), pltpu.SemaphoreType.DMA]
        + _gather_sems(1) + _gather_sems(1))
    return pl.pallas_call(
        body, name="mix_in_fwd", grid_spec=gs,
        out_shape=[_big((T, IN_COLS), F32), _big((T, D_MODEL), BF16), _big(w_in_buf.shape, BF16),
                   _big(w_out_buf.shape, BF16)],
        input_output_aliases={5: 2, 6: 3},
        compiler_params=_cp(("arbitrary", "arbitrary"), 48),
    )(chip1, _hbm(x), modr, g_pre, b_in4, _hbm(w_in_buf), _hbm(w_out_buf))


def _mixers_fwd(p, wdw, vecs, lbl, w_out, x, modr, g_post, g_ffn, gbufs):
    T = p.shape[0]
    tm = _tok_tile(T)
    nt = T // tm
    nch = tm // CHUNK
    ng = len(gbufs)
    n_in, n_out = 12, 7

    def body(*refs):
        (p_ref, wdw_ref, bdw_ref, gain_ref, bias_ref, gout_ref, lbl_ref, wout_ref, x_ref, mod_ref, gp_ref,
         gf_ref) = refs[:n_in]
        cat_ref, ys_ref, o_ref, st_ref, y_ref, x1_ref, h2_ref = refs[n_in + ng:n_in + ng + n_out]
        gout_bufs = refs[n_in + ng + n_out:n_in + 2 * ng + n_out]
        (ubuf, state, qt_s, kt_s, kh_s, v_s, egl_s, lower_s, gmat_s, gssem,
         grsem) = refs[n_in + 2 * ng + n_out:]
        i = pl.program_id(0)

        @pl.when(i == 0)
        def _():
            _ring_start(gout_bufs, gssem, grsem)
            lower_s[...], _, _ = _chunk_masks(tm)
            gmat_s[...] = _gn_matrix()
            state[...] = jnp.zeros(state.shape, F32)
            ubuf[0:HALO, :] = jnp.zeros((HALO, CONV_CH), F32)
            ubuf[HALO + tm:HALO + tm + SUB, :] = jnp.zeros((SUB, CONV_CH), F32)

        @pl.when(i > 0)
        def _():
            ubuf[0:HALO, :] = ubuf[tm:tm + HALO, :]

        ubuf[HALO:HALO + tm, :] = p_ref[:, 0:CONV_CH] * _sig(p_ref[:, CONV_CH:2 * CONV_CH])
        for r in range(tm // CONV_ROWS):
            rows = slice(r * CONV_ROWS, (r + 1) * CONV_ROWS)
            for lb_ in range(CONV_CH // LANE):
                lanes = slice(lb_ * LANE, (lb_ + 1) * LANE)
                ys_ref[rows, lanes] = bdw_ref[:, lanes] + _tap_conv(ubuf, wdw_ref, r * CONV_ROWS, CONV_FWD_TAPS, lanes)
        gmat = gmat_s[...]
        yv = ys_ref[...]
        d = yv - _gmean(yv, gmat)
        rs = lax.rsqrt(_gmean(d * d, gmat) + GN_EPS)
        z = d * rs * gain_ref[...] + bias_ref[...]
        cat_ref[:, 0:CONV_CH] = (z * _sig(z)).astype(BF16)

        lb, _ = _lower_bound(lbl_ref)
        o0 = 2 * CONV_CH
        pr = _hgrn_prep(p_ref[:, o0:o0 + HGRN_W], p_ref[:, o0 + HGRN_W:o0 + 2 * HGRN_W], lb, lower_s[...])
        qt_s[...] = pr["qt"].astype(BF16)
        kt_s[...] = pr["kt"].astype(BF16)
        kh_s[...] = pr["kh"].astype(BF16)
        v_s[...] = p_ref[:, o0 + 2 * HGRN_W:o0 + 3 * HGRN_W].astype(BF16)
        egl_s[...] = jnp.exp(pr["Gl"])
        tri = _tri()

        def chunk(ci, carry):
            r0 = pl.multiple_of(ci * CHUNK, CHUNK)
            rows = pl.ds(r0, CHUNK)
            for h in range(N_HEADS):
                ls = pl.ds(h * HEAD_D, HEAD_D)
                qc, kc, hc, vc = qt_s[rows, ls], kt_s[rows, ls], kh_s[rows, ls], v_s[rows, ls]
                s0 = state[h]
                s0b = s0.astype(BF16)
                st_ref[ci, h] = s0
                att =jnp.where(tri, _dot_nt(qc, kc), 0.0).astype(BF16)
                o_ref[rows, ls] = _dot(att, vc) + _dot_nt(qc, s0b)
                state[h] = s0 * egl_s[pl.ds(r0, 1), ls] + _dot_tn(vc, hc)
            return carry

        lax.fori_loop(0, nch, chunk, 0, unroll=min(CHUNK_UNROLL, nch))
        for h in range(N_HEADS):
            sl = slice(h * HEAD_D, (h + 1) * HEAD_D)
            oh = o_ref[:, sl]
            gh = p_ref[:, o0 + 3 * HGRN_W + h * HEAD_D:o0 + 3 * HGRN_W + (h + 1) * HEAD_D]
            rsh = lax.rsqrt(jnp.mean(oh * oh, axis=-1, keepdims=True) + RMS_EPS)
            hg = (oh * rsh) * gout_ref[:, sl] * (gh * _sig(gh))
            cat_ref[:, CONV_CH + h * HEAD_D:CONV_CH + (h + 1) * HEAD_D] = hg.astype(BF16)

        yv = _dot(cat_ref[...], wout_ref[...])
        y_ref[...] = yv
        rsy = lax.rsqrt(jnp.mean(yv * yv, axis=-1, keepdims=True) + RMS_EPS)
        x1 = x_ref[...] + mod_ref[2:3, :] * ((yv * rsy) * gp_ref[...])
        x1_ref[...] = x1
        rs1 = lax.rsqrt(jnp.mean(x1 * x1, axis=-1, keepdims=True) + RMS_EPS)
        h2 = (x1 * rs1) * gf_ref[...] * (1.0 + mod_ref[4:5, :]) + mod_ref[3:4, :]
        h2_ref[...] = h2.astype(BF16)

        @pl.when(i == min(nt - 1, nt // 2 + 1))
        def _():
            _ring_forward(gout_bufs, gssem, grsem)

        @pl.when(i == nt - 1)
        def _():
            _ring_finish(gout_bufs, gssem, grsem)

    tile = lambda cols: pl.BlockSpec((tm, cols), lambda i: (i, 0))
    hbm = pl.BlockSpec(memory_space=pl.ANY)
    outs = pl.pallas_call(
        body, name="mixers_fwd", grid=(nt,),
        in_specs=[tile(IN_COLS), _full((HALO, CONV_CH))] + [_full((1, CONV_CH))] * 4 + [_full((2, HGRN_W))]
        + [_full((D_MODEL, D_MODEL)), tile(D_MODEL), _full((6, D_MODEL)), _full((1, D_MODEL)), _full((1, D_MODEL))]
        + [hbm] * ng,
        out_specs=[tile(D_MODEL), tile(CONV_CH), tile(HGRN_W),
                   pl.BlockSpec((nch, N_HEADS, HEAD_D, HEAD_D), lambda i: (i, 0, 0, 0)),
                   tile(D_MODEL), tile(D_MODEL), tile(D_MODEL)] + [hbm] * ng,
        out_shape=[_big((T, D_MODEL), BF16), _big((T, CONV_CH), F32), _big((T, HGRN_W), F32),
                   _big((T // CHUNK, N_HEADS, HEAD_D, HEAD_D), F32), _big((T, D_MODEL), F32),
                   _big((T, D_MODEL), F32), _big((T, D_MODEL), BF16)] + [_big(b.shape, BF16) for b in gbufs],
        input_output_aliases={n_in + t: n_out + t for t in range(ng)},
        scratch_shapes=[pltpu.VMEM((tm + HALO + SUB, CONV_CH), F32), pltpu.VMEM((N_HEADS, HEAD_D, HEAD_D), F32),
                        pltpu.VMEM((tm, HGRN_W), BF16), pltpu.VMEM((tm, HGRN_W), BF16),
                        pltpu.VMEM((tm, HGRN_W), BF16), pltpu.VMEM((tm, HGRN_W), BF16),
                        pltpu.VMEM((tm, HGRN_W), F32), pltpu.VMEM((tm, tm), BF16),
                        pltpu.VMEM((CONV_CH, CONV_CH), BF16)] + _ring_sems(ng),
        compiler_params=_cp(("arbitrary",), 56),
    )(_hbm(p), wdw, *vecs, lbl, *_hbm(w_out, x), modr, g_post, g_ffn, *[_hbm(b) for b in gbufs])
    return outs[:n_out], outs[n_out:]


def _row_chains(rows, n=2):
    step = rows // n
    return [slice(k * step, (k + 1) * step) for k in range(n)]


def _ffn_blocks():
    return D_FF // FFN_BLOCK, (D_FF // N_CHIPS) // FFN_BLOCK


def _ffn_fwd(h2, w_up_g, w_down, x1, target, modr, g_post):
    T = h2.shape[0]
    tm = min(FFN_TILE, T)
    fb = FFN_BLOCK
    nj, per = _ffn_blocks()

    def body(h_ref, wu_ref, wd_ref, x1_ref, t_ref, mod_ref, g_ref, r_ref, dy2_ref, dx2_ref, st_ref, acc):
        i, j = pl.program_id(0), pl.program_id(1)

        @pl.when((i == 0) & (j == 0))
        def _():
            st_ref[...] = jnp.zeros(st_ref.shape, F32)

        @pl.when(j == 0)
        def _():
            acc[...] = jnp.zeros(acc.shape, F32)

        for rows in _row_chains(tm):
            ra = jnp.maximum(_dot(h_ref[rows, :], wu_ref[0]), 0.0)
            rb = (ra * ra).astype(BF16)
            r_ref[rows, :] = rb
            acc[rows, :] += _dot(rb, wd_ref[...])

        @pl.when(j == nj - 1)
        def _():
            y2 = acc[...]
            rs = lax.rsqrt(jnp.mean(y2 * y2, axis=-1, keepdims=True) + RMS_EPS)
            nh = y2 * rs
            gp = g_ref[...]
            err = x1_ref[...] + mod_ref[5:6, :] * (nh * gp) - t_ref[...]
            dx2 = err * (1.0 / D_MODEL)
            dx2_ref[...] = dx2
            st_ref[0:1, :] += _colsum(err * err)
            st_ref[1:2, :] += _colsum(dx2 * (nh * gp))
            dn = dx2 * mod_ref[5:6, :]
            st_ref[2:3, :] += _colsum(dn * nh)
            dy2_ref[...] = _rms_bwd(dn * gp, nh, rs).astype(BF16)

    tile = pl.BlockSpec((tm, D_MODEL), lambda i, j: (i, 0))
    return pl.pallas_call(
        body, name="ffn_fwd", grid=(T // tm, nj),
        in_specs=[tile, pl.BlockSpec((1, D_MODEL, fb), lambda i, j: (j // per, 0, j % per)),
                  pl.BlockSpec((fb, D_MODEL), lambda i, j: (j, 0)), tile, tile,
                  _full((6, D_MODEL)), _full((1, D_MODEL))],
        out_specs=[pl.BlockSpec((tm, fb), lambda i, j: (i, j)), tile, tile, _full((8, D_MODEL))],
        out_shape=[_big((T, D_FF), BF16), _big((T, D_MODEL), BF16), _big((T, D_MODEL), F32),
                   jax.ShapeDtypeStruct((8, D_MODEL), F32)],
        scratch_shapes=[pltpu.VMEM((tm, D_MODEL), F32)],
        compiler_params=_cp(("arbitrary", "arbitrary"), 56),
    )(*_hbm(h2, w_up_g, w_down, x1, target), modr, g_post)


def _rms_bwd(dxn, xn, rs):
    return rs * (dxn - xn * jnp.mean(dxn * xn, axis=-1, keepdims=True))


def _ffn_bwd(dy2, r, x1, dx2, w_up_g, w_down, modr, g_ffn):
    T = dx2.shape[0]
    tm = min(FFN_TILE, T)
    fb = FFN_BLOCK
    nj, per = _ffn_blocks()

    def body(dy2_ref, r_ref, x1_ref, dx2_ref, wu_ref, wd_ref, mod_ref, gf_ref, da_ref, dx1_ref, st_ref, dh_s):
        i, j = pl.program_id(0), pl.program_id(1)

        @pl.when((i == 0) & (j == 0))
        def _():
            st_ref[...] = jnp.zeros(st_ref.shape, F32)

        @pl.when(j == 0)
        def _():
            dh_s[...] = jnp.zeros(dh_s.shape, F32)

        for rows in _row_chains(tm):
            ra = jnp.sqrt(r_ref[rows, :].astype(F32))
            da = (_dot_nt(dy2_ref[rows, :], wd_ref[...]) * (2.0 * ra)).astype(BF16)
            da_ref[rows, :] = da
            dh_s[rows, :] += _dot_nt(da, wu_ref[0])

        @pl.when(j == nj - 1)
        def _():
            dh = dh_s[...]
            x1v = x1_ref[...]
            rs1 = lax.rsqrt(jnp.mean(x1v * x1v, axis=-1, keepdims=True) + RMS_EPS)
            xn = x1v * rs1
            st_ref[0:1, :] += _colsum(dh)
            st_ref[1:2, :] += _colsum(dh * (xn * gf_ref[...]))
            dsc = dh * (1.0 + mod_ref[4:5, :])
            st_ref[2:3, :] += _colsum(dsc * xn)
            dx1_ref[...] = dx2_ref[...] + _rms_bwd(dsc * gf_ref[...], xn, rs1)

    tile = pl.BlockSpec((tm, D_MODEL), lambda i, j: (i, 0))
    ftile = pl.BlockSpec((tm, fb), lambda i, j: (i, j))
    return pl.pallas_call(
        body, name="ffn_bwd", grid=(T // tm, nj),
        in_specs=[tile, ftile, tile, tile, pl.BlockSpec((1, D_MODEL, fb), lambda i, j: (j // per, 0, j % per)),
                  pl.BlockSpec((fb, D_MODEL), lambda i, j: (j, 0)), _full((6, D_MODEL)), _full((1, D_MODEL))],
        out_specs=[ftile, tile, _full((8, D_MODEL))],
        out_shape=[_big((T, D_FF), BF16), _big((T, D_MODEL), F32), jax.ShapeDtypeStruct((8, D_MODEL), F32)],
        scratch_shapes=[pltpu.VMEM((tm, D_MODEL), F32)],
        compiler_params=_cp(("arbitrary", "arbitrary"), 56),
    )(*_hbm(dy2, r, x1, dx2, w_up_g, w_down), modr, g_ffn)


def _mix_out_bwd(dx1, y, cat, w_out, modr, g_post, swap):
    T = dx1.shape[0]
    tm = _tok_tile(T)
    nt = T // tm
    ns = len(swap)

    def body(*refs):
        dx1_ref, y_ref, cat_ref, w_ref, mod_ref, gp_ref = refs[:6]
        s_ins = refs[6:6 + ns]
        dcat_ref, st_ref, gw_ref = refs[6 + ns:9 + ns]
        s_outs = refs[9 + ns:9 + 2 * ns]
        gacc, gsem, pssem, prsem = refs[9 + 2 * ns:]
        i = pl.program_id(0)

        @pl.when(i == 0)
        def _():
            for cp in _pair_copies(s_ins, s_outs, pssem, prsem):
                cp.start()
            st_ref[...] = jnp.zeros(st_ref.shape, F32)
            gacc[...] = jnp.zeros(gacc.shape, F32)

        dxv, yv = dx1_ref[...], y_ref[...]
        rs = lax.rsqrt(jnp.mean(yv * yv, axis=-1, keepdims=True) + RMS_EPS)
        nh = yv * rs
        st_ref[0:1, :] += _colsum(dxv * (nh * gp_ref[...]))
        dn = dxv * mod_ref[2:3, :]
        st_ref[1:2, :] += _colsum(dn * nh)
        dy = _rms_bwd(dn * gp_ref[...], nh, rs).astype(BF16)
        dcat_ref[...] = _dot_nt(dy, w_ref[...])
        for cols in _row_chains(D_MODEL):
            gacc[:, cols] += _dot_tn(cat_ref[...], dy[:, cols])

        @pl.when(i == nt - 1)
        def _():
            out = pltpu.make_async_copy(gacc, gw_ref, gsem)
            out.start()
            copies = _pair_copies(s_ins, s_outs, pssem, prsem)
            for cp in copies:
                cp.wait_recv()
            for cp in copies:
                cp.wait_send()
            out.wait()

    tile = pl.BlockSpec((tm, D_MODEL), lambda i: (i, 0))
    hbm = pl.BlockSpec(memory_space=pl.ANY)
    outs = pl.pallas_call(
        body, name="mix_out_bwd", grid=(nt,),
        in_specs=[tile, tile, tile, _full((D_MODEL, D_MODEL)), _full((6, D_MODEL)), _full((1, D_MODEL))]
        + [hbm] * ns,
        out_specs=[tile, _full((8, D_MODEL)), hbm] + [hbm] * ns,
        out_shape=[_big((T, D_MODEL), F32), jax.ShapeDtypeStruct((8, D_MODEL), F32), _big((D_MODEL, D_MODEL), F32)]
        + _pair_shapes(swap),
        scratch_shapes=[pltpu.VMEM((D_MODEL, D_MODEL), F32), pltpu.SemaphoreType.DMA] + _pair_sems(ns),
        compiler_params=_cp(("arbitrary",), 48),
    )(*_hbm(dx1, y, cat, w_out), modr, g_post, *[_hbm(g) for g in swap])
    return outs[:3], outs[3:]


def _mixers_bwd(p, dcat, ys, o, states, h1, wdw, vecs, lbl, pairs_b):
    T = p.shape[0]
    tm = min(MIXB_TILE, T)
    nt = T // tm
    nch = tm // CHUNK
    hpt = tm // HALO
    nx = len(pairs_b)
    nb = IN_COLS // N_CHIPS

    def body(*refs):
        (p_ref, ph_ref, dcat_ref, ys_ref, o_ref, st_ref, h1_ref, wdw_ref, bdw_ref, gain_ref, bias_ref, gout_ref,
         lbl_ref) = refs[:13]
        x_ins = refs[13:13 + nx]
        dp_ref, sb_ref, s5_ref, dw_ref = refs[13 + nx:17 + nx]
        x_outs = refs[17 + nx:17 + 2 * nx]
        gin_ref, ginb_ref = refs[17 + 2 * nx], refs[18 + 2 * nx]
        (ubuf, dybuf, carry, dstate, qt_s, kt_s, kh_s, v_s, do_s, egl_s, dqt_s, dkt_s, dkh_s, dv_s, dgl_s,
         dsh, dw8, dshift, lower_s, upper_s, same_s, gmat_s, gacc, gacc_b, dp_prev, h1_prev, gsem, gbsem, xssem,
         xrsem) = refs[19 + 2 * nx:]
        i = pl.program_id(0)
        tile_idx = nt - 1 - i

        @pl.when(i == 0)
        def _():
            for cp in _xchg_copies(x_ins, x_outs, xssem, xrsem):
                cp.start()
            gacc[...] = jnp.zeros(gacc.shape, F32)
            dstate[...] = jnp.zeros(dstate.shape, F32)
            carry[...] = jnp.zeros(carry.shape, F32)
            sb_ref[...] = jnp.zeros(sb_ref.shape, F32)
            s5_ref[...] = jnp.zeros(s5_ref.shape, F32)
            dw_ref[...] = jnp.zeros(dw_ref.shape, F32)
            dw8[...] = jnp.zeros(dw8.shape, F32)
            lower_s[...], upper_s[...], same_s[...] = _chunk_masks(tm)
            gmat_s[...] = _gn_matrix()
            dsh[0:SUB, :] = jnp.zeros((SUB, CONV_CH), F32)
            dsh[SUB + tm:2 * SUB + tm, :] = jnp.zeros((SUB, CONV_CH), F32)
            ubuf[HALO + tm:HALO + tm + SUB, :] = jnp.zeros((SUB, CONV_CH), F32)
            dp_prev[...] = jnp.zeros(dp_prev.shape, BF16)
            h1_prev[...] = jnp.zeros(h1_prev.shape, BF16)

        n_pieces = (tm // CONV_ROWS) * (CONV_CH // LANE)
        per_block = n_pieces // N_CHIPS
        prow = D_MODEL // per_block

        def w_in_grad_piece(k):
            j, part = k // per_block, k % per_block
            rows_k = slice(part * prow, (part + 1) * prow)
            gacc[j, rows_k, :] += _dot_tn(h1_prev[:, rows_k], dp_prev[:, j * nb:(j + 1) * nb])

        uh = ph_ref[:, 0:CONV_CH] * _sig(ph_ref[:, CONV_CH:2 * CONV_CH])
        ubuf[0:HALO, :] = jnp.where(tile_idx > 0, uh, 0.0)
        ubuf[HALO:HALO + tm, :] = p_ref[:, 0:CONV_CH] * _sig(p_ref[:, CONV_CH:2 * CONV_CH])
        gmat = gmat_s[...]
        gain = gain_ref[...]
        yv = ys_ref[...]
        d = yv - _gmean(yv, gmat)
        rs = lax.rsqrt(_gmean(d * d, gmat) + GN_EPS)
        yn = d * rs
        z = yn * gain + bias_ref[...]
        sz = _sig(z)
        dz = dcat_ref[:, 0:CONV_CH] * (sz * (1.0 + z * (1.0 - sz)))
        dyn = dz * gain
        dyc = rs * (dyn - _gmean(dyn, gmat) - yn * _gmean(dyn * yn, gmat))
        s5_ref[0:1, :] += _colsum(dyc)
        s5_ref[1:2, :] += _colsum(dz * yn)
        s5_ref[2:3, :] += _colsum(dz)
        dybuf[tm:tm + HALO, :] = carry[...]
        dybuf[0:tm, :] = dyc
        dsh[SUB:SUB + tm, :] = dyc
        carry[...] = dyc[0:HALO, :]
        for b in range(SUB):
            dshift[...] = dsh[SUB - b:2 * SUB - b + tm, :]
            for j, off in CONV_FWD_TAPS:
                if off % SUB == b:
                    prod = dshift[...] * ubuf[off - b:off - b + tm + SUB, :]
                    dw8[j] += jnp.sum(prod.reshape((tm + SUB) // SUB, SUB, CONV_CH), axis=0)
        for r in range(tm // CONV_ROWS):
            rows = slice(r * CONV_ROWS, (r + 1) * CONV_ROWS)
            for lb_ in range(CONV_CH // LANE):
                lanes = slice(lb_ * LANE, (lb_ + 1) * LANE)
                glanes = slice(CONV_CH + lb_ * LANE, CONV_CH + (lb_ + 1) * LANE)
                w_in_grad_piece(r * (CONV_CH // LANE) + lb_)
                acc = _tap_conv(dybuf, wdw_ref, r * CONV_ROWS, CONV_BWD_TAPS, lanes)
                val = p_ref[rows, lanes]
                sg = _sig(p_ref[rows, glanes])
                dval = acc * sg
                dgate = acc * val * (sg * (1.0 - sg))
                dp_ref[rows, lanes] = dval.astype(BF16)
                dp_ref[rows, glanes] = dgate.astype(BF16)
                sb_ref[0:1, lanes] += _colsum(dval)
                sb_ref[0:1, glanes] += _colsum(dgate)

        o0 = 2 * CONV_CH
        for h in range(N_HEADS):
            sl = slice(h * HEAD_D, (h + 1) * HEAD_D)
            gsl = slice(o0 + 3 * HGRN_W + h * HEAD_D, o0 + 3 * HGRN_W + (h + 1) * HEAD_D)
            oh = o_ref[:, sl]
            gh = p_ref[:, gsl]
            dh = dcat_ref[:, CONV_CH + h * HEAD_D:CONV_CH + (h + 1) * HEAD_D]
            gout = gout_ref[:, sl]
            rsh = lax.rsqrt(jnp.mean(oh * oh, axis=-1, keepdims=True) + RMS_EPS)
            on = oh * rsh
            sgg = _sig(gh)
            dgh = dh * (on * gout) * (sgg * (1.0 + gh * (1.0 - sgg)))
            dm = dh * (gh * sgg)
            s5_ref[3:4, sl] += _colsum(dm * on)
            do_s[:, sl] = _rms_bwd(dm * gout, on, rsh).astype(BF16)
            dp_ref[:, gsl] = dgh.astype(BF16)
            sb_ref[2:3, CONV_CH + h * HEAD_D:CONV_CH + (h + 1) * HEAD_D] += _colsum(dgh)

        lb, _ = _lower_bound(lbl_ref)
        pq = p_ref[:, o0:o0 + HGRN_W]
        pr = _hgrn_prep(pq, p_ref[:, o0 + HGRN_W:o0 + 2 * HGRN_W], lb, lower_s[...])
        qt_s[...] = pr["qt"].astype(BF16)
        kt_s[...] = pr["kt"].astype(BF16)
        kh_s[...] = pr["kh"].astype(BF16)
        v_s[...] = p_ref[:, o0 + 2 * HGRN_W:o0 + 3 * HGRN_W].astype(BF16)
        egl_s[...] = jnp.exp(pr["Gl"])
        tri = _tri()

        def chunk(it, c_):
            ci = nch - 1 - it
            r0 = pl.multiple_of(ci * CHUNK, CHUNK)
            rows = pl.ds(r0, CHUNK)
            for h in range(N_HEADS):
                ls = pl.ds(h * HEAD_D, HEAD_D)
                qc, kc, hc, vc = qt_s[rows, ls], kt_s[rows, ls], kh_s[rows, ls], v_s[rows, ls]
                dob = do_s[rows, ls]
                s0 = st_ref[ci, h]
                s0b = s0.astype(BF16)
                ds1 = dstate[h]
                ds1b = ds1.astype(BF16)
                egl = egl_s[pl.ds(r0, 1), ls]
                att = jnp.where(tri, _dot_nt(qc, kc), 0.0).astype(BF16)
                datt = jnp.where(tri, _dot_nt(dob, vc), 0.0).astype(BF16)
                dv_s[rows, ls] = _dot_tn(att, dob) + _dot_nt(hc, ds1b)
                dqt_s[rows, ls] = _dot(datt, kc) + _dot(dob, s0b)
                dkt_s[rows, ls] = _dot_tn(datt, qc)
                dkh_s[rows, ls] = _dot(vc, ds1b)
                dgl = egl * _colsum(ds1 * s0)
                dgl_s[rows, ls] = jnp.broadcast_to(dgl, (CHUNK, HEAD_D))
                dstate[h] = ds1 * egl + _dot_tn(dob, qc)
            return c_

        lax.fori_loop(0, nch, chunk, 0, unroll=min(CHUNK_UNROLL, nch))
        dqt, dkt, dkh = dqt_s[...], dkt_s[...], dkh_s[...]
        dk = dkt * pr["enG"] + dkh * pr["eGlG"]
        khk = dkh * kh_s[...].astype(F32)
        dG = dqt * qt_s[...].astype(F32) - dkt * kt_s[...].astype(F32) - khk
        dlogf = _mm3(upper_s[...], dG) + _mm3(same_s[...], khk) + dgl_s[...]
        df = dlogf / pr["f"] - dk
        sf, sq = pr["sf"], pr["sq"]
        s5_ref[4:5, :] += _colsum(df * (1.0 - sf))
        dfl = df * (1.0 - lb) * (sf * (1.0 - sf))
        dq = (dqt * pr["eG"]) * (sq * (1.0 + pq * (1.0 - sq)))
        dvv = dv_s[...]
        dp_ref[:, o0:o0 + HGRN_W] = dq.astype(BF16)
        dp_ref[:, o0 + HGRN_W:o0 + 2 * HGRN_W] = dfl.astype(BF16)
        dp_ref[:, o0 + 2 * HGRN_W:o0 + 3 * HGRN_W] = dvv.astype(BF16)
        sb_ref[1:2, 0:HGRN_W] += _colsum(dq)
        sb_ref[1:2, HGRN_W:2 * HGRN_W] += _colsum(dfl)
        sb_ref[2:3, 0:HGRN_W] += _colsum(dvv)

        dp_prev[...] = dp_ref[...]
        h1_prev[...] = h1_ref[...]

        @pl.when(i == nt - 1)
        def _():
            for k in range(n_pieces):
                w_in_grad_piece(k)
            out = pltpu.make_async_copy(gacc, gin_ref, gsem)
            out.start()
            for j in range(N_CHIPS):
                gacc_b[j] = gacc[j].astype(BF16)
            out_b = pltpu.make_async_copy(gacc_b, ginb_ref, gbsem)
            out_b.start()
            for j in range(CONV_K):
                dw_ref[j:j + 1, :] = _colsum(dw8[j])
            copies = _xchg_copies(x_ins, x_outs, xssem, xrsem)
            for cp in copies:
                cp.wait_recv()
            for cp in copies:
                cp.wait_send()
            out.wait()
            out_b.wait()

    rev = lambda cols: pl.BlockSpec((tm, cols), lambda i: (nt - 1 - i, 0))
    halo = pl.BlockSpec((HALO, 2 * CONV_CH), lambda i: (jnp.maximum((nt - 1 - i) * hpt - 1, 0), 0))
    wide = lambda n: pltpu.VMEM((tm, HGRN_W), n)
    hbm = pl.BlockSpec(memory_space=pl.ANY)
    outs = pl.pallas_call(
        body, name="mixers_bwd", grid=(nt,),
        in_specs=[rev(IN_COLS), halo, rev(D_MODEL), rev(CONV_CH), rev(HGRN_W),
                  pl.BlockSpec((nch, N_HEADS, HEAD_D, HEAD_D), lambda i: (nt - 1 - i, 0, 0, 0)), rev(D_MODEL),
                  _full((HALO, CONV_CH))] + [_full((1, CONV_CH))] * 4 + [_full((2, HGRN_W))] + [hbm] * nx,
        out_specs=[rev(IN_COLS), _full((8, D_MODEL)), _full((8, CONV_CH)), _full((HALO, CONV_CH))]
        + [hbm] * (nx + 2),
        out_shape=[_big((T, IN_COLS), BF16), jax.ShapeDtypeStruct((8, D_MODEL), F32),
                   jax.ShapeDtypeStruct((8, CONV_CH), F32), jax.ShapeDtypeStruct((HALO, CONV_CH), F32)]
        + [_big(pb.shape, BF16) for pb in pairs_b]
        + [_big((N_CHIPS, D_MODEL, nb), F32), _big((N_CHIPS, D_MODEL, nb), BF16)],
        scratch_shapes=[pltpu.VMEM((tm + HALO + SUB, CONV_CH), F32), pltpu.VMEM((tm + HALO, CONV_CH), F32),
                        pltpu.VMEM((HALO, CONV_CH), F32), pltpu.VMEM((N_HEADS, HEAD_D, HEAD_D), F32),
                        wide(BF16), wide(BF16), wide(BF16), wide(BF16), wide(BF16),
                        wide(F32), wide(F32), wide(F32), wide(F32), wide(F32), wide(F32),
                        pltpu.VMEM((tm + 2 * SUB, CONV_CH), F32), pltpu.VMEM((HALO, SUB, CONV_CH), F32),
                        pltpu.VMEM((tm + SUB, CONV_CH), F32), pltpu.VMEM((tm, tm), BF16), pltpu.VMEM((tm, tm), BF16),
                        pltpu.VMEM((tm, tm), BF16), pltpu.VMEM((CONV_CH, CONV_CH), BF16),
                        pltpu.VMEM((N_CHIPS, D_MODEL, nb), F32), pltpu.VMEM((N_CHIPS, D_MODEL, nb), BF16),
                        pltpu.VMEM((tm, IN_COLS), BF16), pltpu.VMEM((tm, D_MODEL), BF16),
                        pltpu.SemaphoreType.DMA, pltpu.SemaphoreType.DMA]
        + _xchg_sems(nx),
        compiler_params=_cp(("arbitrary",), 56),
    )(*_hbm(p, p, dcat, ys, o, states, h1), wdw, *vecs, lbl, *[_hbm(pb) for pb in pairs_b])
    return outs[:4], outs[4:4 + nx], (outs[4 + nx], outs[5 + nx])


def _mix_in_bwd(dp, w_in_g, x, dx1, modr, g_pre, pairs_b):
    T = x.shape[0]
    tm = _tok_tile(T)
    nt = T // tm
    nb = IN_COLS // N_CHIPS
    nx = len(pairs_b)

    def body(*refs):
        dp_ref, w_ref, x_ref, dx1_ref, mod_ref, g_ref = refs[:6]
        x_ins = refs[6:6 + nx]
        gx_ref, st_ref = refs[6 + nx:8 + nx]
        x_outs = refs[8 + nx:8 + 2 * nx]
        xssem, xrsem = refs[8 + 2 * nx:]
        i = pl.program_id(0)

        @pl.when(i == 0)
        def _():
            for cp in _xchg_copies(x_ins, x_outs, xssem, xrsem):
                cp.start()
            st_ref[...] = jnp.zeros(st_ref.shape, F32)

        dh = None
        for j in range(N_CHIPS):
            part = _dot_nt(dp_ref[:, j * nb:(j + 1) * nb], w_ref[j])
            dh = part if dh is None else dh + part
        xv = x_ref[...]
        rs = lax.rsqrt(jnp.mean(xv * xv, axis=-1, keepdims=True) + RMS_EPS)
        xn = xv * rs
        st_ref[0:1, :] += _colsum(dh)
        st_ref[1:2, :] += _colsum(dh * (xn * g_ref[...]))
        dsc = dh * (1.0 + mod_ref[1:2, :])
        st_ref[2:3, :] += _colsum(dsc * xn)
        gx_ref[...] = dx1_ref[...] + _rms_bwd(dsc * g_ref[...], xn, rs)

        @pl.when(i == nt - 1)
        def _():
            copies = _xchg_copies(x_ins, x_outs, xssem, xrsem)
            for cp in copies:
                cp.wait_recv()
            for cp in copies:
                cp.wait_send()

    tile = pl.BlockSpec((tm, D_MODEL), lambda i: (i, 0))
    hbm = pl.BlockSpec(memory_space=pl.ANY)
    outs = pl.pallas_call(
        body, name="mix_in_bwd", grid=(nt,),
        in_specs=[pl.BlockSpec((tm, IN_COLS), lambda i: (i, 0)), _full((N_CHIPS, D_MODEL, nb)), tile, tile,
                  _full((6, D_MODEL)), _full((1, D_MODEL))] + [hbm] * nx,
        out_specs=[tile, _full((8, D_MODEL))] + [hbm] * nx,
        out_shape=[_big((T, D_MODEL), F32), jax.ShapeDtypeStruct((8, D_MODEL), F32)]
        + [_big(pb.shape, BF16) for pb in pairs_b],
        scratch_shapes=_xchg_sems(nx),
        compiler_params=_cp(("arbitrary",), 48),
    )(*_hbm(dp, w_in_g, x, dx1), modr, g_pre, *[_hbm(pb) for pb in pairs_b])
    return outs[:2], outs[2:]


def _weight_grad(a, b, a_blocked, b_blocked, name):
    T = a.shape[0]
    tt = min(GRAD_TILE, T)
    nt = T // tt
    ka = a.shape[1] // N_CHIPS if a_blocked else a.shape[1]
    nb = b.shape[1] // N_CHIPS if b_blocked else b.shape[1]

    def body(a_ref, b_ref, o_ref, ob_ref):
        t = pl.program_id(1)

        @pl.when(t == 0)
        def _():
            o_ref[...] = jnp.zeros(o_ref.shape, F32)

        for cols in _row_chains(nb):
            o_ref[0, :, cols] += _dot_tn(a_ref[...], b_ref[:, cols])

        @pl.when(t == nt - 1)
        def _():
            ob_ref[0] = o_ref[0].astype(BF16)

    blk = pl.BlockSpec((1, ka, nb), lambda j, t: (j, 0, 0))
    return pl.pallas_call(
        body, name=name, grid=(N_CHIPS, nt),
        in_specs=[pl.BlockSpec((tt, ka), (lambda j, t: (t, j)) if a_blocked else (lambda j, t: (t, 0))),
                  pl.BlockSpec((tt, nb), (lambda j, t: (t, j)) if b_blocked else (lambda j, t: (t, 0)))],
        out_specs=[blk, blk],
        out_shape=[_big((N_CHIPS, ka, nb), F32), _big((N_CHIPS, ka, nb), BF16)],
        compiler_params=_cp(("arbitrary", "arbitrary"), 48),
    )(*_hbm(a, b))


R_LOSS = 0
R_FFN = 8
R_OUT = 16
R_IN = 24
R_BIN = 32
R_512 = 40
R_DW = 48
N_STAT_ROWS = 80
MOD_ROWS = (R_IN + 0, R_IN + 1, R_OUT + 0, R_FFN + 0, R_FFN + 1, R_LOSS + 1)


def _small_update(gath, params):
    names = ["b_ada", "lb_logits", "g_pre_mix", "b_in", "b_dw", "gn_gain", "gn_bias", "g_hgrn_out", "g_post_mix",
             "g_pre_ffn", "g_post_ffn"]
    flat = []
    for n in names:
        flat += list(params[n])
    n_in = 1 + len(flat)

    def body(*refs):
        g_ref = refs[0]
        prm = {n: refs[1 + 3 * k:4 + 3 * k] for k, n in enumerate(names)}
        outs = refs[n_in:]
        loss_ref, dmod_ref, dwdw_ref = outs[0], outs[1], outs[2]
        res = {n: outs[3 + 4 * k:7 + 4 * k] for k, n in enumerate(names)}
        red = g_ref[0]
        for dev in range(1, N_DEV):
            red = red + g_ref[dev]
        loss_ref[...] = jnp.broadcast_to(
            (0.5 / D_MODEL) * jnp.sum(red[R_LOSS:R_LOSS + 1, :], axis=-1, keepdims=True), loss_ref.shape)
        for dev in range(N_DEV):
            for k, r in enumerate(MOD_ROWS):
                dmod_ref[dev:dev + 1, k * D_MODEL:(k + 1) * D_MODEL] = g_ref[dev, r:r + 1, :]
        dwdw_ref[...] = red[R_DW:R_DW + HALO, 0:CONV_CH]

        def finish(name, pieces):
            w_ref, m_ref, v_ref = prm[name]
            g_out, d_out, m_out, v_out = res[name]
            for rsl, lsl, g in pieces:
                d, m2, v2 = _adam_math(w_ref[rsl, lsl], g, m_ref[rsl, lsl], v_ref[rsl, lsl])
                g_out[rsl, lsl] = g
                d_out[rsl, lsl] = d
                m_out[rsl, lsl] = m2
                v_out[rsl, lsl] = v2

        one = slice(0, 1)
        row = lambda r: red[r:r + 1, :]
        half = lambda r: red[r:r + 1, 0:CONV_CH]
        finish("b_ada", [(one, slice(k * D_MODEL, (k + 1) * D_MODEL), row(r)) for k, r in enumerate(MOD_ROWS)])
        finish("b_in", [(one, slice(k * D_MODEL, (k + 1) * D_MODEL), row(R_BIN + k)) for k in range(3)])
        finish("g_pre_mix", [(one, slice(None), row(R_IN + 2))])
        finish("g_post_mix", [(one, slice(None), row(R_OUT + 1))])
        finish("g_pre_ffn", [(one, slice(None), row(R_FFN + 2))])
        finish("g_post_ffn", [(one, slice(None), row(R_LOSS + 2))])
        finish("b_dw", [(one, slice(None), half(R_512 + 0))])
        finish("gn_gain", [(one, slice(None), half(R_512 + 1))])
        finish("gn_bias", [(one, slice(None), half(R_512 + 2))])
        finish("g_hgrn_out", [(one, slice(None), half(R_512 + 3))])
        s0, s1 = _lower_bound(prm["lb_logits"][0])
        dlb = half(R_512 + 4)
        finish("lb_logits", [(slice(0, 1), slice(None), dlb * s0 * (1.0 - s0)),
                             (slice(1, 2), slice(None), -dlb * s0 * s1)])

    vm = pl.BlockSpec(memory_space=pltpu.VMEM)
    out_shape = [jax.ShapeDtypeStruct((8, 128), F32), jax.ShapeDtypeStruct((N_DEV, 6 * D_MODEL), F32),
                 jax.ShapeDtypeStruct((HALO, CONV_CH), F32)]
    for n in names:
        out_shape += [jax.ShapeDtypeStruct(params[n][0].shape, F32)] * 4
    outs = pl.pallas_call(
        body, name="small_update", out_shape=out_shape,
        in_specs=[vm] * n_in, out_specs=[vm] * len(out_shape),
        compiler_params=_cp(None, 32),
    )(gath, *flat)
    return outs[0], outs[1], outs[2], {n: outs[3 + 4 * k:7 + 4 * k] for k, n in enumerate(names)}


def _wdw_adam(w, g, m, v):
    def body(w_ref, g_ref, m_ref, v_ref, d_out, m_out, v_out):
        d, m2, v2 = _adam_math(w_ref[...], g_ref[...], m_ref[...], v_ref[...])
        d_out[...] = d
        m_out[...] = m2
        v_out[...] = v2

    vm = pl.BlockSpec(memory_space=pltpu.VMEM)
    return pl.pallas_call(
        body, name="wdw_adam", out_shape=[jax.ShapeDtypeStruct(w.shape, F32)] * 3,
        in_specs=[vm] * 4, out_specs=[vm] * 3, compiler_params=_cp(None, 16),
    )(w, g, m, v)


def kernel(x, c, w_ada, b_ada, lb_logits, g_pre_mix, w_in, b_in, w_dw, b_dw, gn_gain, gn_bias, g_hgrn_out, w_out, g_post_mix, g_pre_ffn, w_up, w_down, g_post_ffn, loss_target, m_w_ada, m_b_ada, m_lb_logits, m_g_pre_mix, m_w_in, m_b_in, m_w_dw, m_b_dw, m_gn_gain, m_gn_bias, m_g_hgrn_out, m_w_out, m_g_post_mix, m_g_pre_ffn, m_w_up, m_w_down, m_g_post_ffn, v_w_ada, v_b_ada, v_lb_logits, v_g_pre_mix, v_w_in, v_b_in, v_w_dw, v_b_dw, v_gn_gain, v_gn_bias, v_g_hgrn_out, v_w_out, v_g_post_mix, v_g_pre_ffn, v_w_up, v_w_down, v_g_post_ffn):
    ax, ay, ac = lax.axis_index("x"), lax.axis_index("y"), lax.axis_index("c")
    chip = 2 * ax + ay
    T = x.shape[1]
    xs, tgt = x[0], loss_target[0]
    ada_cols = w_ada.shape[2]

    b_sh = lax.dynamic_slice_in_dim(b_ada, chip * ada_cols, ada_cols, axis=1)
    wdw_pad = jnp.pad(w_dw[0], ((0, HALO - CONV_K), (0, 0)))
    chip1 = jnp.reshape(chip, (1,)).astype(jnp.int32)
    place = jnp.stack([ac, chip]).astype(jnp.int32)
    _, c8, modg, wdwg = _ada_exchange(c, w_ada[0], b_sh, wdw_pad)
    modr = modg.reshape(6, D_MODEL)
    wdw_all = jnp.transpose(wdwg, (1, 0, 2)).reshape(HALO, CONV_CH)
    bufs = {t: _cast_own(chip1, w[0], "cast_" + t)
            for w, t in ((w_in, "w_in"), (w_out, "w_out"), (w_up, "w_up"), (w_down, "w_down"))}
    vec = (b_dw, gn_gain, gn_bias, g_hgrn_out)

    p, h1, w_in_g, w_out_g = _mix_in_fwd(chip1, xs, modr, g_pre_mix, b_in.reshape(N_CHIPS, 1, IN_COLS // N_CHIPS),
                                          bufs["w_in"], bufs["w_out"])
    w_out_f = w_out_g.reshape(D_MODEL, D_MODEL)
    (cat, ys, o, states, y, x1, h2), (w_up_g, w_down_g) = _mixers_fwd(
        p, wdw_all, vec, lb_logits, w_out_f, xs, modr, g_post_mix, g_pre_ffn, [bufs["w_up"], bufs["w_down"]])
    w_down_f = w_down_g.reshape(D_FF, D_MODEL)
    r, dy2, dx2, st_loss = _ffn_fwd(h2, w_up_g, w_down_f, x1, tgt, modr, g_post_ffn)

    def pair_sums(grads, got, tags):
        return [_pair_sum(place, g, o_, "pair_sum_" + t) for (g, _), o_, t in zip(grads, got, tags)]

    da, dx1, st_ffn = _ffn_bwd(dy2, r, x1, dx2, w_up_g, w_down_f, modr, g_pre_ffn)
    g_up = _weight_grad(h2, da, False, True, "grad_w_up")
    g_down = _weight_grad(r, dy2, True, False, "grad_w_down")
    (dcat, st_out, g_out), got_ud = _mix_out_bwd(dx1, y, cat, w_out_f, modr, g_post_mix, [g_up[1], g_down[1]])
    g_out = g_out.reshape(N_CHIPS, D_MODEL // N_CHIPS, D_MODEL)
    early = pair_sums([g_up, g_down], got_ud, ["w_up", "w_down"])
    (dp, st_bin, st_512, dwdw), got_early, g_in = _mixers_bwd(p, dcat, ys, o, states, h1, wdw_all, vec, lb_logits,
                                                              [pb for _, pb in early])
    late = pair_sums([g_in, (g_out, None)], _pair_swap([g_in[1], g_out], "pair_swap_late"), ["w_in", "w_out"])
    (grad_x, st_in), got_late = _mix_in_bwd(dp, w_in_g, xs, dx1, modr, g_pre_mix, [pb for _, pb in late])
    fulls = [_chip_sum(place, pf, gb, "chip_sum_" + t)
             for (pf, _), gb, t in zip(late + early, list(got_late) + list(got_early), ["w_in", "w_out", "w_up", "w_down"])]

    pad_lanes = lambda s: jnp.pad(s, ((0, 0), (0, D_MODEL - s.shape[1])))
    stats = jnp.concatenate([st_loss, st_ffn, st_out, st_in, st_bin, pad_lanes(st_512), pad_lanes(dwdw)], axis=0)
    (g_w_in, g_w_out, g_w_up, g_w_down), gath = _final_exchange(fulls, stats)
    small = {"b_ada": (b_ada, m_b_ada, v_b_ada), "lb_logits": (lb_logits, m_lb_logits, v_lb_logits),
             "g_pre_mix": (g_pre_mix, m_g_pre_mix, v_g_pre_mix), "b_in": (b_in, m_b_in, v_b_in),
             "b_dw": (b_dw, m_b_dw, v_b_dw), "gn_gain": (gn_gain, m_gn_gain, v_gn_gain),
             "gn_bias": (gn_bias, m_gn_bias, v_gn_bias), "g_hgrn_out": (g_hgrn_out, m_g_hgrn_out, v_g_hgrn_out),
             "g_post_mix": (g_post_mix, m_g_post_mix, v_g_post_mix), "g_pre_ffn": (g_pre_ffn, m_g_pre_ffn, v_g_pre_ffn),
             "g_post_ffn": (g_post_ffn, m_g_post_ffn, v_g_post_ffn)}
    loss_t, dmod_all, dwdw_sum, sres = _small_update(gath, small)
    loss = loss_t[0, 0]

    res = dict(sres)
    dmod_sh = lax.dynamic_slice_in_dim(dmod_all, chip * ada_cols, ada_cols, axis=1)
    res["w_ada"] = [t[None] for t in _ada_grad_adam(jnp.transpose(c8), dmod_sh, w_ada[0], m_w_ada[0], v_w_ada[0])]
    g_wdw = lax.dynamic_slice_in_dim(dwdw_sum, chip * HEAD_D, HEAD_D, axis=1)[:CONV_K][None]
    res["w_dw"] = [g_wdw] + list(_wdw_adam(w_dw, g_wdw, m_w_dw, v_w_dw))
    for name, g, w, m, v in (("w_in", g_w_in, w_in, m_w_in, v_w_in), ("w_out", g_w_out, w_out, m_w_out, v_w_out),
                             ("w_up", g_w_up, w_up, m_w_up, v_w_up), ("w_down", g_w_down, w_down, m_w_down, v_w_down)):
        d, m2, v2 = _adam_big(w[0], g, m[0], v[0], "adam_" + name)
        res[name] = [g[None], d[None], m2[None], v2[None]]

    order = ["w_ada", "b_ada", "lb_logits", "g_pre_mix", "w_in", "b_in", "w_dw", "b_dw", "gn_gain", "gn_bias",
             "g_hgrn_out", "w_out", "g_post_mix", "g_pre_ffn", "w_up", "w_down", "g_post_ffn"]
    out = [loss, grad_x[None]]
    for k in range(4):
        out += [res[n][k] for n in order]
    return tuple(out)
```

```python
import jax
import jax.numpy as jnp
from jax import lax
from jax.experimental import pallas as pl
from jax.experimental.pallas import tpu as pltpu

F32, BF16 = jnp.float32, jnp.bfloat16
D_MODEL = 1024
CONV_CH = 512
HGRN_W = 512
N_HEADS = 4
HEAD_D = 128
CONV_K = 31
GN_GROUP = 64
GN_SHIFT = 6
IN_COLS = 3072
D_FF = 4096
CHUNK = 64
CHUNK_SHIFT = 6
N_CHIPS = 4
N_DEV = 8
RMS_EPS = 1e-6
GN_EPS = 1e-5
ADAM_LR, ADAM_B1, ADAM_B2, ADAM_EPS, ADAM_WD, ADAM_STEP = 0.001, 0.9, 0.999, 1e-08, 0.01, 10
TOK_TILE = 512
MIXIN_TILE = 1024
MIXB_TILE = 256
FFN_TILE = 1024
FFN_BLOCK = 512
GRAD_TILE = 2048
HALO = 32
SUB = 8
LANE = 128
CONV_ROWS = 128
CHUNK_UNROLL = 8
MIB = 1 << 20
MESH = pl.DeviceIdType.MESH
OTHER_CHIPS = ((0, 1), (1, 0), (1, 1))


def _cp(sem=None, vmem_mib=48):
    return pltpu.CompilerParams(dimension_semantics=sem, vmem_limit_bytes=vmem_mib * MIB)


def _dot(a, b):
    return jnp.dot(a, b, preferred_element_type=F32)


def _dot_nt(a, b):
    return lax.dot_general(a, b, (((1,), (1,)), ((), ())), preferred_element_type=F32)


def _dot_tn(a, b):
    return lax.dot_general(a, b, (((0,), (0,)), ((), ())), preferred_element_type=F32)


def _sig(v):
    return 0.5 * jnp.tanh(0.5 * v) + 0.5


def _colsum(v):
    return jnp.sum(v, axis=0, keepdims=True)


def _flip(v, b):
    return 1 - v if b else v


def _rcopy(src, dst, ssem, rsem, dev):
    return pltpu.make_async_remote_copy(src_ref=src, dst_ref=dst, send_sem=ssem, recv_sem=rsem,
                                        device_id=dev, device_id_type=MESH)


def _place():
    return lax.axis_index("x"), lax.axis_index("y"), lax.axis_index("c")


def _full(shape):
    return pl.BlockSpec(shape, lambda *_: (0,) * len(shape))


def _big(shape, dtype):
    return pltpu.HBM(shape, dtype)


def _hbm(*arrays):
    out = [pltpu.with_memory_space_constraint(a, pltpu.HBM) for a in arrays]
    return out[0] if len(out) == 1 else out


def _split2(v):
    hi = v.astype(BF16)
    lo = (v - hi.astype(F32)).astype(BF16)
    return hi, lo


def _split3(v):
    h1 = v.astype(BF16)
    r1 = v - h1.astype(F32)
    h2 = r1.astype(BF16)
    h3 = (r1 - h2.astype(F32)).astype(BF16)
    return h1, h2, h3


def _mm3(mat, v):
    h1, h2, h3 = _split3(v)
    return _dot(mat, h1) + _dot(mat, h2) + _dot(mat, h3)


def _gn_matrix():
    r = lax.broadcasted_iota(jnp.int32, (CONV_CH, CONV_CH), 0) >> GN_SHIFT
    c = lax.broadcasted_iota(jnp.int32, (CONV_CH, CONV_CH), 1) >> GN_SHIFT
    return jnp.where(r == c, 1.0 / GN_GROUP, 0.0).astype(BF16)


def _gmean(v, gmat):
    hi, lo = _split2(v)
    return _dot(hi, gmat) + _dot(lo, gmat)


def _chunk_masks(tm):
    r = lax.broadcasted_iota(jnp.int32, (tm, tm), 0)
    c = lax.broadcasted_iota(jnp.int32, (tm, tm), 1)
    same = (r >> CHUNK_SHIFT) == (c >> CHUNK_SHIFT)
    one = lambda m: jnp.where(m, 1.0, 0.0).astype(BF16)
    return one(same & (c <= r)), one(same & (c >= r)), one(same)


def _tri():
    return lax.broadcasted_iota(jnp.int32, (CHUNK, CHUNK), 0) >= lax.broadcasted_iota(jnp.int32, (CHUNK, CHUNK), 1)


def _lower_bound(lbl_ref):
    l0, l1 = lbl_ref[0:1, :], lbl_ref[1:2, :]
    mx = jnp.maximum(l0, l1)
    e0, e1 = jnp.exp(l0 - mx), jnp.exp(l1 - mx)
    return e0 / (e0 + e1), e1 / (e0 + e1)


CONV_FWD_TAPS = tuple((j, HALO - (CONV_K - 1) + j) for j in range(CONV_K))
CONV_BWD_TAPS = tuple((j, (CONV_K - 1) - j) for j in range(CONV_K))


def _tap_conv(src_ref, w_ref, row0, taps, lanes):
    acc = None
    for b in range(SUB):
        pb = None
        for j, off in taps:
            if off % SUB == b:
                lo = row0 + off - b
                term = w_ref[j:j + 1, lanes] * src_ref[lo:lo + CONV_ROWS + SUB, lanes]
                pb = term if pb is None else pb + term
        if pb is not None:
            sh = pb[b:b + CONV_ROWS, :]
            acc = sh if acc is None else acc + sh
    return acc


def _hgrn_prep(pq, pf, lb, lower):
    sq = _sig(pq)
    qf = pq * sq
    sf = _sig(pf)
    f = lb + (1.0 - lb) * sf
    logf = jnp.log(f)
    k = 1.0 - f
    G = _mm3(lower, logf)
    rows, cols = G.shape
    g3 = G.reshape(rows // CHUNK, CHUNK, cols)
    Gl = jnp.broadcast_to(g3[:, CHUNK - 1:CHUNK, :], g3.shape).reshape(rows, cols)
    eG, enG, eGlG = jnp.exp(G), jnp.exp(-G), jnp.exp(Gl - G)
    return dict(sq=sq, sf=sf, f=f, Gl=Gl, eG=eG, enG=enG, eGlG=eGlG, qt=qf * eG, kt=k * enG, kh=k * eGlG)


def _ada_steps(c_ref, w_ref, b_ref, wdw_ref, call_s, c8_ref, modg_s, wdwg_s, rows_s, sa, ra, sw, rw, sm, rm):
    x, y, c = _place()
    me = 4 * x + 2 * y + c
    chip = 2 * x + y
    cv = c_ref[...]
    call_s[me] = cv * _sig(cv)
    wdwg_s[chip] = wdw_ref[...]
    sends = []
    for m in range(1, N_DEV):
        peer = (_flip(x, m >> 2), _flip(y, (m >> 1) & 1), _flip(c, m & 1))
        cp = _rcopy(call_s.at[me], call_s.at[me], sa.at[m - 1], ra.at[m - 1], peer)
        cp.start()
        sends.append(cp)
    for k, (fx, fy) in enumerate(OTHER_CHIPS):
        peer = (_flip(x, fx), _flip(y, fy), c)
        cp = _rcopy(wdwg_s.at[chip], wdwg_s.at[chip], sw.at[k], rw.at[k], peer)
        cp.start()
        sends.append(cp)
    for m in range(1, N_DEV):
        peer = (_flip(x, m >> 2), _flip(y, (m >> 1) & 1), _flip(c, m & 1))
        pid = 4 * peer[0] + 2 * peer[1] + peer[2]
        _rcopy(call_s.at[pid], call_s.at[pid], sa.at[m - 1], ra.at[m - 1], peer).wait_recv()
    for b in range(N_DEV):
        c8_ref[b:b + 1, :] = call_s[b]
    mod_all = _dot(c8_ref[...].astype(BF16), w_ref[...].astype(BF16)) + b_ref[...]
    for b in range(N_DEV):
        rows_s[b] = mod_all[b:b + 1, :]
    modg_s[chip] = rows_s[me]
    for k, (fx, fy) in enumerate(OTHER_CHIPS):
        peer = (_flip(x, fx), _flip(y, fy), c)
        pid = 4 * peer[0] + 2 * peer[1] + peer[2]
        cp = _rcopy(rows_s.at[pid], modg_s.at[chip], sm.at[k], rm.at[k], peer)
        cp.start()
        sends.append(cp)
    for k, (fx, fy) in enumerate(OTHER_CHIPS):
        peer = (_flip(x, fx), _flip(y, fy), c)
        pchip = 2 * peer[0] + peer[1]
        _rcopy(rows_s.at[0], modg_s.at[pchip], sm.at[k], rm.at[k], peer).wait_recv()
        _rcopy(wdwg_s.at[pchip], wdwg_s.at[pchip], sw.at[k], rw.at[k], peer).wait_recv()
    for cp in sends:
        cp.wait_send()


def _ada_scratch(ncol):
    return [pltpu.VMEM((N_DEV, 1, D_MODEL), F32), pltpu.VMEM((N_CHIPS, 1, ncol), F32),
            pltpu.VMEM((N_CHIPS, HALO, HEAD_D), F32), pltpu.VMEM((N_DEV, 1, ncol), F32),
            pltpu.SemaphoreType.DMA((N_DEV - 1,)), pltpu.SemaphoreType.DMA((N_DEV - 1,)),
            pltpu.SemaphoreType.DMA((3,)), pltpu.SemaphoreType.DMA((3,)),
            pltpu.SemaphoreType.DMA((3,)), pltpu.SemaphoreType.DMA((3,))]


def _cast_own(chip1, shard, name):
    rows, cols = shard.shape
    tr = _row_tile(rows)

    def body(ch_ref, s_ref, o_ref):
        o_ref[0] = s_ref[...].astype(BF16)

    gs = pltpu.PrefetchScalarGridSpec(
        num_scalar_prefetch=1, grid=(rows // tr,),
        in_specs=[pl.BlockSpec((tr, cols), lambda i, ch: (i, 0))],
        out_specs=pl.BlockSpec((1, tr, cols), lambda i, ch: (ch[0], i, 0)))
    return pl.pallas_call(
        body, name=name, grid_spec=gs, out_shape=_big((N_CHIPS, rows, cols), BF16),
        compiler_params=_cp(("arbitrary",), 32),
    )(chip1, _hbm(shard))


def _slab(buf, ch, core):
    hs = buf.shape[1] // 2
    return buf.at[ch, pl.ds(core * hs, hs), :]


def _gather_start(bufs, ssem, rsem, relations=(0, 1, 2)):
    x, y, c = _place()
    chip = 2 * x + y
    for k in relations:
        fx, fy = OTHER_CHIPS[k]
        peer = (_flip(x, fx), _flip(y, fy), c)
        for t, buf in enumerate(bufs):
            _rcopy(_slab(buf, chip, c), _slab(buf, chip, c), ssem.at[t * 3 + k], rsem.at[t * 3 + k], peer).start()


def _gather_pass_on(bufs, ssem, rsem):
    nt = len(bufs)
    x, y, c = _place()
    sibling = (x, y, 1 - c)
    for k, (fx, fy) in enumerate(OTHER_CHIPS):
        peer = (_flip(x, fx), _flip(y, fy), c)
        pchip = 2 * peer[0] + peer[1]
        for t, buf in enumerate(bufs):
            _rcopy(_slab(buf, pchip, c), _slab(buf, pchip, c), ssem.at[t * 3 + k], rsem.at[t * 3 + k], peer).wait_recv()
            _rcopy(_slab(buf, pchip, c), _slab(buf, pchip, c), ssem.at[3 * nt + t * 3 + k],
                   rsem.at[3 * nt + t * 3 + k], sibling).start()


def _gather_drain(bufs, ssem, rsem):
    nt = len(bufs)
    x, y, c = _place()
    chip = 2 * x + y
    sibling = (x, y, 1 - c)
    for k, (fx, fy) in enumerate(OTHER_CHIPS):
        peer = (_flip(x, fx), _flip(y, fy), c)
        pchip = 2 * peer[0] + peer[1]
        for t, buf in enumerate(bufs):
            _rcopy(_slab(buf, pchip, 1 - c), _slab(buf, pchip, 1 - c), ssem.at[3 * nt + t * 3 + k],
                   rsem.at[3 * nt + t * 3 + k], sibling).wait_recv()
            _rcopy(_slab(buf, chip, c), _slab(buf, chip, c), ssem.at[t * 3 + k], rsem.at[t * 3 + k], peer).wait_send()
            _rcopy(_slab(buf, pchip, c), _slab(buf, pchip, c), ssem.at[3 * nt + t * 3 + k],
                   rsem.at[3 * nt + t * 3 + k], sibling).wait_send()


def _gather_finish(bufs, ssem, rsem):
    _gather_pass_on(bufs, ssem, rsem)
    _gather_drain(bufs, ssem, rsem)


def _gather_arrive(bufs, k, ssem, rsem):
    nt = len(bufs)
    x, y, c = _place()
    fx, fy = OTHER_CHIPS[k]
    peer = (_flip(x, fx), _flip(y, fy), c)
    pchip = 2 * peer[0] + peer[1]
    for t, buf in enumerate(bufs):
        _rcopy(_slab(buf, pchip, c), _slab(buf, pchip, c), ssem.at[t * 3 + k], rsem.at[t * 3 + k], peer).wait_recv()
        _rcopy(_slab(buf, pchip, c), _slab(buf, pchip, c), ssem.at[3 * nt + t * 3 + k],
               rsem.at[3 * nt + t * 3 + k], (x, y, 1 - c)).start()
    for t, buf in enumerate(bufs):
        _rcopy(_slab(buf, pchip, 1 - c), _slab(buf, pchip, 1 - c), ssem.at[3 * nt + t * 3 + k],
               rsem.at[3 * nt + t * 3 + k], (x, y, 1 - c)).wait_recv()


def _gather_sends_done(bufs, ssem, rsem):
    nt = len(bufs)
    x, y, c = _place()
    chip = 2 * x + y
    for k, (fx, fy) in enumerate(OTHER_CHIPS):
        peer = (_flip(x, fx), _flip(y, fy), c)
        pchip = 2 * peer[0] + peer[1]
        for t, buf in enumerate(bufs):
            _rcopy(_slab(buf, chip, c), _slab(buf, chip, c), ssem.at[t * 3 + k], rsem.at[t * 3 + k], peer).wait_send()
            _rcopy(_slab(buf, pchip, c), _slab(buf, pchip, c), ssem.at[3 * nt + t * 3 + k],
                   rsem.at[3 * nt + t * 3 + k], (x, y, 1 - c)).wait_send()


def _ring_parts(buf):
    x, y, c = _place()
    ynb, xnb = (x, 1 - y, c), (1 - x, y, c)
    ychip, xchip, dchip = 2 * x + (1 - y), 2 * (1 - x) + y, 2 * (1 - x) + (1 - y)
    hs = buf.shape[1] // 2

    def piece(ch, q):
        return buf.at[ch, pl.ds(c * hs + q * (hs // 2), hs // 2), :]

    return ynb, xnb, ychip, xchip, dchip, piece


def _ring_start(bufs, ssem, rsem):
    x, y, c = _place()
    chip = 2 * x + y
    for t, buf in enumerate(bufs):
        ynb, xnb, _, _, _, _ = _ring_parts(buf)
        _rcopy(_slab(buf, chip, c), _slab(buf, chip, c), ssem.at[2 * t], rsem.at[2 * t], ynb).start()
        _rcopy(_slab(buf, chip, c), _slab(buf, chip, c), ssem.at[2 * t + 1], rsem.at[2 * t + 1], xnb).start()


def _ring_forward(bufs, ssem, rsem):
    nt = len(bufs)
    _, _, c = _place()
    for t, buf in enumerate(bufs):
        ynb, xnb, ychip, xchip, _, piece = _ring_parts(buf)
        _rcopy(_slab(buf, ychip, c), _slab(buf, ychip, c), ssem.at[2 * t], rsem.at[2 * t], ynb).wait_recv()
        _rcopy(piece(ychip, 0), piece(ychip, 0), ssem.at[2 * nt + 2 * t], rsem.at[2 * nt + 2 * t], xnb).start()
        _rcopy(_slab(buf, xchip, c), _slab(buf, xchip, c), ssem.at[2 * t + 1], rsem.at[2 * t + 1], xnb).wait_recv()
        _rcopy(piece(xchip, 1), piece(xchip, 1), ssem.at[2 * nt + 2 * t + 1], rsem.at[2 * nt + 2 * t + 1], ynb).start()


def _ring_finish(bufs, ssem, rsem):
    nt = len(bufs)
    x, y, c = _place()
    chip = 2 * x + y
    sibling = (x, y, 1 - c)
    for t, buf in enumerate(bufs):
        ynb, xnb, ychip, xchip, dchip, piece = _ring_parts(buf)
        _rcopy(piece(dchip, 0), piece(dchip, 0), ssem.at[2 * nt + 2 * t], rsem.at[2 * nt + 2 * t], xnb).wait_recv()
        _rcopy(piece(dchip, 1), piece(dchip, 1), ssem.at[2 * nt + 2 * t + 1], rsem.at[2 * nt + 2 * t + 1],
               ynb).wait_recv()
        for k, ch in enumerate((ychip, xchip, dchip)):
            _rcopy(_slab(buf, ch, c), _slab(buf, ch, c), ssem.at[4 * nt + 3 * t + k], rsem.at[4 * nt + 3 * t + k],
                   sibling).start()
    for t, buf in enumerate(bufs):
        ynb, xnb, ychip, xchip, dchip, piece = _ring_parts(buf)
        for k, ch in enumerate((ychip, xchip, dchip)):
            _rcopy(_slab(buf, ch, 1 - c), _slab(buf, ch, 1 - c), ssem.at[4 * nt + 3 * t + k],
                   rsem.at[4 * nt + 3 * t + k], sibling).wait_recv()
            _rcopy(_slab(buf, ch, c), _slab(buf, ch, c), ssem.at[4 * nt + 3 * t + k], rsem.at[4 * nt + 3 * t + k],
                   sibling).wait_send()
        _rcopy(_slab(buf, chip, c), _slab(buf, chip, c), ssem.at[2 * t], rsem.at[2 * t], ynb).wait_send()
        _rcopy(_slab(buf, chip, c), _slab(buf, chip, c), ssem.at[2 * t + 1], rsem.at[2 * t + 1], xnb).wait_send()
        _rcopy(piece(ychip, 0), piece(ychip, 0), ssem.at[2 * nt + 2 * t], rsem.at[2 * nt + 2 * t], xnb).wait_send()
        _rcopy(piece(xchip, 1), piece(xchip, 1), ssem.at[2 * nt + 2 * t + 1], rsem.at[2 * nt + 2 * t + 1],
               ynb).wait_send()


def _ring_sems(nt):
    return [pltpu.SemaphoreType.DMA((7 * nt,)), pltpu.SemaphoreType.DMA((7 * nt,))]


def _gather_sems(nt):
    return [pltpu.SemaphoreType.DMA((6 * nt,)), pltpu.SemaphoreType.DMA((6 * nt,))]


def _pair_copies(ins, outs, ssem, rsem):
    x, y, c = _place()
    copies = []
    for t in range(len(ins)):
        hs = ins[t].shape[1] // 2
        copies.append(_rcopy(ins[t].at[:, pl.ds((1 - c) * hs, hs), :], outs[t], ssem.at[t], rsem.at[t], (x, y, 1 - c)))
    return copies


def _pair_shapes(grads):
    return [_big((g.shape[0], g.shape[1] // 2, g.shape[2]), g.dtype) for g in grads]


def _pair_sems(nt):
    return [pltpu.SemaphoreType.DMA((nt,)), pltpu.SemaphoreType.DMA((nt,))]


def _pair_swap(grads, name):
    nt = len(grads)
    hbm = pl.BlockSpec(memory_space=pl.ANY)

    def body(*refs):
        copies = _pair_copies(refs[:nt], refs[nt:2 * nt], refs[2 * nt], refs[2 * nt + 1])
        for cp in copies:
            cp.start()
        for cp in copies:
            cp.wait_recv()
        for cp in copies:
            cp.wait_send()

    return pl.pallas_call(
        body, name=name, out_shape=_pair_shapes(grads), in_specs=[hbm] * nt, out_specs=[hbm] * nt,
        scratch_shapes=_pair_sems(nt),
    )(*[_hbm(g) for g in grads])


def _xchg_copies(ins, outs, ssem, rsem):
    x, y, c = _place()
    copies = []
    for k, (fx, fy) in enumerate(OTHER_CHIPS):
        peer = (_flip(x, fx), _flip(y, fy), c)
        for t in range(len(ins)):
            copies.append(_rcopy(ins[t].at[k], outs[t].at[k], ssem.at[t * 3 + k], rsem.at[t * 3 + k], peer))
    return copies


def _xchg_sems(nt):
    return [pltpu.SemaphoreType.DMA((3 * nt,)), pltpu.SemaphoreType.DMA((3 * nt,))]


def _final_exchange(fulls, stats):
    nt = len(fulls)
    rows, cols = stats.shape
    hbm = pl.BlockSpec(memory_space=pl.ANY)
    vm = pl.BlockSpec(memory_space=pltpu.VMEM)

    def body(*refs):
        ins, s_ref = refs[:nt], refs[nt]
        outs, g_ref = refs[nt + 1:2 * nt + 1], refs[2 * nt + 1]
        hssem, hrsem, ssem, rsem = refs[2 * nt + 2:]
        x, y, c = _place()
        me, sibling = (x, y, c), (x, y, 1 - c)
        halves = []
        for t in range(nt):
            hs = ins[t].shape[0] // 2
            mine = pl.ds(c * hs, hs)
            cp = _rcopy(ins[t].at[mine, :], outs[t].at[mine, :], hssem.at[t], hrsem.at[t], sibling)
            cp.start()
            halves.append(cp)

        chips = [(_flip(x, fx), _flip(y, fy)) for fx, fy in OTHER_CHIPS]

        def blk(px, py, pc):
            return g_ref.at[4 * px + 2 * py + pc]

        def copy(k, block, to, src=None):
            return _rcopy(blk(*block) if src is None else src, blk(*block), ssem.at[k], rsem.at[k], to)

        g_ref[4 * x + 2 * y + c] = s_ref[...]
        first = [copy(0, me, sibling, src=s_ref)]
        first += [copy(1 + j, me, (*chip, c), src=s_ref) for j, chip in enumerate(chips)]
        for cp in first:
            cp.start()
        passed = [copy(4 + j, (*chip, c), sibling) for j, chip in enumerate(chips)]
        for j, chip in enumerate(chips):
            copy(1 + j, (*chip, c), me).wait_recv()
            passed[j].start()
        copy(0, sibling, me).wait_recv()
        for j, chip in enumerate(chips):
            copy(4 + j, (*chip, 1 - c), me).wait_recv()
        for t in range(nt):
            hs = ins[t].shape[0] // 2
            other = pl.ds((1 - c) * hs, hs)
            _rcopy(ins[t].at[other, :], outs[t].at[other, :], hssem.at[t], hrsem.at[t], sibling).wait_recv()
        for cp in first + passed + halves:
            cp.wait_send()

    outs = pl.pallas_call(
        body, name="final_exchange",
        out_shape=[_big(f.shape, F32) for f in fulls] + [jax.ShapeDtypeStruct((N_DEV, rows, cols), F32)],
        in_specs=[hbm] * nt + [vm], out_specs=[hbm] * nt + [vm],
        input_output_aliases={t: t for t in range(nt)},
        scratch_shapes=[pltpu.SemaphoreType.DMA((nt,)), pltpu.SemaphoreType.DMA((nt,)),
                        pltpu.SemaphoreType.DMA((7,)), pltpu.SemaphoreType.DMA((7,))],
        compiler_params=_cp(None, 32),
    )(*[_hbm(f) for f in fulls], stats)
    return outs[:nt], outs[nt]


def _row_tile(rows):
    return min(rows, 512)


def _pair_sum(place, grad, got, name):
    nb, hs, cols = got.shape
    tr = _row_tile(hs)
    nr = hs // tr

    def body(pl_ref, g_ref, o_ref, pf_ref, pb_ref):
        j = pl.program_id(1)
        s = g_ref[0] + o_ref[0].astype(F32)

        @pl.when(j == 0)
        def _():
            pf_ref[...] = s

        @pl.when(j > 0)
        def _():
            pb_ref[0] = s.astype(BF16)

    gs = pltpu.PrefetchScalarGridSpec(
        num_scalar_prefetch=1, grid=(nr, nb),
        in_specs=[pl.BlockSpec((1, tr, cols), lambda i, j, p: (p[1] ^ j, p[0] * nr + i, 0)),
                  pl.BlockSpec((1, tr, cols), lambda i, j, p: (p[1] ^ j, i, 0))],
        out_specs=[pl.BlockSpec((tr, cols), lambda i, j, p: (i, 0)),
                   pl.BlockSpec((1, tr, cols), lambda i, j, p: (jnp.maximum(j - 1, 0), i, 0))])
    return pl.pallas_call(
        body, name=name, grid_spec=gs,
        out_shape=[_big((hs, cols), F32), _big((nb - 1, hs, cols), BF16)],
        compiler_params=_cp(("arbitrary", "arbitrary"), 32),
    )(place, *_hbm(grad, got))


def _chip_sum(place, pair_f, got_b, name):
    nb, hs, cols = got_b.shape
    tr = _row_tile(hs)
    nr = hs // tr

    def body(pl_ref, pf_ref, gb_ref, o_ref):
        acc = pf_ref[...]
        for k in range(nb):
            acc = acc + gb_ref[k].astype(F32)
        o_ref[...] = acc

    gs = pltpu.PrefetchScalarGridSpec(
        num_scalar_prefetch=1, grid=(nr,),
        in_specs=[pl.BlockSpec((tr, cols), lambda i, p: (i, 0)),
                  pl.BlockSpec((nb, tr, cols), lambda i, p: (0, i, 0))],
        out_specs=pl.BlockSpec((tr, cols), lambda i, p: (p[0] * nr + i, 0)))
    return pl.pallas_call(
        body, name=name, grid_spec=gs,
        out_shape=_big((2 * hs, cols), F32),
        compiler_params=_cp(("arbitrary",), 32),
    )(place, *_hbm(pair_f, got_b))


def _adam_math(w, g, m, v):
    m2 = ADAM_B1 * m + (1.0 - ADAM_B1) * g
    v2 = ADAM_B2 * v + (1.0 - ADAM_B2) * (g * g)
    m_hat = m2 / (1.0 - ADAM_B1 ** ADAM_STEP)
    v_hat = v2 / (1.0 - ADAM_B2 ** ADAM_STEP)
    delta = -ADAM_LR * (m_hat / (jnp.sqrt(v_hat) + ADAM_EPS) + ADAM_WD * w)
    return delta, m2, v2


def _adam_big(w, g, m, v, name):
    rows, cols = w.shape
    tr = _row_tile(rows)

    def body(w_ref, g_ref, m_ref, v_ref, d_out, m_out, v_out):
        d, m2, v2 = _adam_math(w_ref[...], g_ref[...], m_ref[...], v_ref[...])
        d_out[...] = d
        m_out[...] = m2
        v_out[...] = v2

    spec = pl.BlockSpec((tr, cols), lambda i: (i, 0))
    return pl.pallas_call(
        body, name=name, grid=(rows // tr,), in_specs=[spec] * 4, out_specs=[spec] * 3,
        out_shape=[_big(w.shape, F32)] * 3,
        compiler_params=_cp(("arbitrary",), 48),
    )(*_hbm(w, g, m, v))


def _ada_grad_adam(c8t, dmod_sh, w, m, v):
    rows, cols = w.shape
    tr = _row_tile(rows) // 2

    def body(ct_ref, dm_ref, w_ref, m_ref, v_ref, g_out, d_out, m_out, v_out):
        g = None
        for b in range(N_DEV):
            term = ct_ref[:, b:b + 1] * dm_ref[b:b + 1, :]
            g = term if g is None else g + term
        d, m2, v2 = _adam_math(w_ref[...], g, m_ref[...], v_ref[...])
        g_out[...] = g
        d_out[...] = d
        m_out[...] = m2
        v_out[...] = v2

    spec = pl.BlockSpec((tr, cols), lambda i: (i, 0))
    return pl.pallas_call(
        body, name="ada_grad_adam", grid=(rows // tr,),
        in_specs=[pl.BlockSpec((tr, N_DEV), lambda i: (i, 0)), _full((N_DEV, cols)), spec, spec, spec],
        out_specs=[spec] * 4, out_shape=[_big(w.shape, F32)] * 4,
        compiler_params=_cp(("arbitrary",), 32),
    )(c8t, dmod_sh, *_hbm(w, m, v))


def _tok_tile(t):
    return min(TOK_TILE, t)


def _mix_in_fwd(chip1, x, g_pre, b_in4, c_row, b_sh, wdw_pad, w_ada, w_in_buf, w_out_buf):
    T = x.shape[0]
    tm = min(MIXIN_TILE, T)
    nt = T // tm
    nb = IN_COLS // N_CHIPS
    ncol = w_ada.shape[1]

    def body(ch_ref, x_ref, g_ref, b_ref, c_ref, bsh_ref, wdw_ref, wada_hbm, win_in, wout_in,
             p_ref, h_ref, mod_ref, c8_ref, wdwg_ref, win_ref, wout_ref,
             h_all, wblk, lsem, wada_v, wsem, call_s, modg_s, wdwg_s, rows_s, sa, ra, sw, rw, sm, rm,
             is_sem, ir_sem, os_sem, or_sem):
        k, i = pl.program_id(0), pl.program_id(1)
        chip = ch_ref[0]

        def load_block(blk):
            cp = pltpu.make_async_copy(win_ref.at[blk], wblk, lsem)
            cp.start()
            cp.wait()

        @pl.when((k == 0) & (i == 0))
        def _():
            _gather_start([win_ref], is_sem, ir_sem, relations=(0, 1))
            wcp = pltpu.make_async_copy(wada_hbm, wada_v, wsem)
            wcp.start()
            load_block(chip)
            wcp.wait()
            _ada_steps(c_ref, wada_v, bsh_ref, wdw_ref, call_s, c8_ref, modg_s, wdwg_s, rows_s, sa, ra, sw, rw, sm, rm)
            wdwg_ref[...] = wdwg_s[...]
            m = [modg_s[j] for j in range(N_CHIPS)]
            half = D_MODEL // 2
            six = [m[0][:, :D_MODEL], jnp.concatenate([m[0][:, D_MODEL:], m[1][:, :half]], axis=1), m[1][:, half:],
                   m[2][:, :D_MODEL], jnp.concatenate([m[2][:, D_MODEL:], m[3][:, :half]], axis=1), m[3][:, half:]]
            for r_, row in enumerate(six):
                mod_ref[r_:r_ + 1, :] = row

        for r in range(N_CHIPS - 1):
            @pl.when((k == r + 1) & (i == 0))
            def _(r=r):
                _gather_arrive([win_ref], r, is_sem, ir_sem)
                if r == 0:
                    _gather_start([win_ref], is_sem, ir_sem, relations=(2,))
                if r == 1:
                    _gather_start([wout_ref], os_sem, or_sem)
                load_block(chip ^ (r + 1))

        rows = pl.ds(pl.multiple_of(i * tm, tm), tm)

        @pl.when(k == 0)
        def _():
            xv = x_ref[...]
            rstd = lax.rsqrt(jnp.mean(xv * xv, axis=-1, keepdims=True) + RMS_EPS)
            h = (xv * rstd) * g_ref[...] * (1.0 + mod_ref[1:2, :]) + mod_ref[0:1, :]
            hb = h.astype(BF16)
            h_ref[...] = hb
            h_all[rows, :] = hb

        p_ref[...] = _dot(h_all[rows, :], wblk[...]) + b_ref[chip ^ k]

        @pl.when((k == N_CHIPS - 1) & (i == nt - 1))
        def _():
            _gather_sends_done([win_ref], is_sem, ir_sem)
            _gather_finish([wout_ref], os_sem, or_sem)

    hbm = pl.BlockSpec(memory_space=pl.ANY)
    first_pass = lambda k, i, ch: (jnp.where(k == 0, i, nt - 1), 0)
    whole = lambda shape: pl.BlockSpec(shape, lambda k, i, ch: (0,) * len(shape))
    gs = pltpu.PrefetchScalarGridSpec(
        num_scalar_prefetch=1, grid=(N_CHIPS, nt),
        in_specs=[pl.BlockSpec((tm, D_MODEL), first_pass), whole((1, D_MODEL)), whole((N_CHIPS, 1, nb)),
                  whole((1, D_MODEL)), whole((1, ncol)), whole((HALO, HEAD_D)), hbm, hbm, hbm],
        out_specs=[pl.BlockSpec((tm, nb), lambda k, i, ch: (i, ch[0] ^ k)), pl.BlockSpec((tm, D_MODEL), first_pass),
                   whole((6, D_MODEL)), whole((N_DEV, D_MODEL)), whole((N_CHIPS, HALO, HEAD_D)), hbm, hbm],
        scratch_shapes=[pltpu.VMEM((T, D_MODEL), BF16), pltpu.VMEM((D_MODEL, nb), BF16), pltpu.SemaphoreType.DMA,
                        pltpu.VMEM(w_ada.shape, F32), pltpu.SemaphoreType.DMA]
        + _ada_scratch(ncol) + _gather_sems(1) + _gather_sems(1))
    return pl.pallas_call(
        body, name="mix_in_fwd", grid_spec=gs,
        out_shape=[_big((T, IN_COLS), F32), _big((T, D_MODEL), BF16), jax.ShapeDtypeStruct((6, D_MODEL), F32),
                   jax.ShapeDtypeStruct((N_DEV, D_MODEL), F32), jax.ShapeDtypeStruct((N_CHIPS, HALO, HEAD_D), F32),
                   _big(w_in_buf.shape, BF16), _big(w_out_buf.shape, BF16)],
        input_output_aliases={8: 5, 9: 6},
        compiler_params=_cp(("arbitrary", "arbitrary"), 56),
    )(chip1, _hbm(x), g_pre, b_in4, c_row, b_sh, wdw_pad, _hbm(w_ada), _hbm(w_in_buf), _hbm(w_out_buf))


def _mixers_fwd(p, wdw, vecs, lbl, w_out, x, modr, g_post, g_ffn, gbufs):
    T = p.shape[0]
    tm = _tok_tile(T)
    nt = T // tm
    nch = tm // CHUNK
    ng = len(gbufs)
    n_in, n_out = 12, 7

    def body(*refs):
        (p_ref, wdw_ref, bdw_ref, gain_ref, bias_ref, gout_ref, lbl_ref, wout_ref, x_ref, mod_ref, gp_ref,
         gf_ref) = refs[:n_in]
        cat_ref, ys_ref, o_ref, st_ref, y_ref, x1_ref, h2_ref = refs[n_in + ng:n_in + ng + n_out]
        gout_bufs = refs[n_in + ng + n_out:n_in + 2 * ng + n_out]
        (ubuf, state, qt_s, kt_s, kh_s, v_s, egl_s, lower_s, gmat_s, gssem,
         grsem) = refs[n_in + 2 * ng + n_out:]
        i = pl.program_id(0)

        @pl.when(i == 0)
        def _():
            _ring_start(gout_bufs, gssem, grsem)
            lower_s[...], _, _ = _chunk_masks(tm)
            gmat_s[...] = _gn_matrix()
            state[...] = jnp.zeros(state.shape, F32)
            ubuf[0:HALO, :] = jnp.zeros((HALO, CONV_CH), F32)
            ubuf[HALO + tm:HALO + tm + SUB, :] = jnp.zeros((SUB, CONV_CH), F32)

        @pl.when(i > 0)
        def _():
            ubuf[0:HALO, :] = ubuf[tm:tm + HALO, :]

        ubuf[HALO:HALO + tm, :] = p_ref[:, 0:CONV_CH] * _sig(p_ref[:, CONV_CH:2 * CONV_CH])
        for r in range(tm // CONV_ROWS):
            rows = slice(r * CONV_ROWS, (r + 1) * CONV_ROWS)
            for lb_ in range(CONV_CH // LANE):
                lanes = slice(lb_ * LANE, (lb_ + 1) * LANE)
                ys_ref[rows, lanes] = bdw_ref[:, lanes] + _tap_conv(ubuf, wdw_ref, r * CONV_ROWS, CONV_FWD_TAPS, lanes)
        gmat = gmat_s[...]
        yv = ys_ref[...]
        d = yv - _gmean(yv, gmat)
        rs = lax.rsqrt(_gmean(d * d, gmat) + GN_EPS)
        z = d * rs * gain_ref[...] + bias_ref[...]
        cat_ref[:, 0:CONV_CH] = (z * _sig(z)).astype(BF16)

        lb, _ = _lower_bound(lbl_ref)
        o0 = 2 * CONV_CH
        pr = _hgrn_prep(p_ref[:, o0:o0 + HGRN_W], p_ref[:, o0 + HGRN_W:o0 + 2 * HGRN_W], lb, lower_s[...])
        qt_s[...] = pr["qt"].astype(BF16)
        kt_s[...] = pr["kt"].astype(BF16)
        kh_s[...] = pr["kh"].astype(BF16)
        v_s[...] = p_ref[:, o0 + 2 * HGRN_W:o0 + 3 * HGRN_W].astype(BF16)
        egl_s[...] = jnp.exp(pr["Gl"])
        tri = _tri()

        def chunk(ci, carry):
            r0 = pl.multiple_of(ci * CHUNK, CHUNK)
            rows = pl.ds(r0, CHUNK)
            for h in range(N_HEADS):
                ls = pl.ds(h * HEAD_D, HEAD_D)
                qc, kc, hc, vc = qt_s[rows, ls], kt_s[rows, ls], kh_s[rows, ls], v_s[rows, ls]
                s0 = state[h]
                s0b = s0.astype(BF16)
                st_ref[ci, h] = s0
                att =jnp.where(tri, _dot_nt(qc, kc), 0.0).astype(BF16)
                o_ref[rows, ls] = _dot(att, vc) + _dot_nt(qc, s0b)
                state[h] = s0 * egl_s[pl.ds(r0, 1), ls] + _dot_tn(vc, hc)
            return carry

        lax.fori_loop(0, nch, chunk, 0, unroll=min(CHUNK_UNROLL, nch))
        for h in range(N_HEADS):
            sl = slice(h * HEAD_D, (h + 1) * HEAD_D)
            oh = o_ref[:, sl]
            gh = p_ref[:, o0 + 3 * HGRN_W + h * HEAD_D:o0 + 3 * HGRN_W + (h + 1) * HEAD_D]
            rsh = lax.rsqrt(jnp.mean(oh * oh, axis=-1, keepdims=True) + RMS_EPS)
            hg = (oh * rsh) * gout_ref[:, sl] * (gh * _sig(gh))
            cat_ref[:, CONV_CH + h * HEAD_D:CONV_CH + (h + 1) * HEAD_D] = hg.astype(BF16)

        yv = _dot(cat_ref[...], wout_ref[...])
        y_ref[...] = yv
        rsy = lax.rsqrt(jnp.mean(yv * yv, axis=-1, keepdims=True) + RMS_EPS)
        x1 = x_ref[...] + mod_ref[2:3, :] * ((yv * rsy) * gp_ref[...])
        x1_ref[...] = x1
        rs1 = lax.rsqrt(jnp.mean(x1 * x1, axis=-1, keepdims=True) + RMS_EPS)
        h2 = (x1 * rs1) * gf_ref[...] * (1.0 + mod_ref[4:5, :]) + mod_ref[3:4, :]
        h2_ref[...] = h2.astype(BF16)

        @pl.when(i == min(nt - 1, nt // 2 + 1))
        def _():
            _ring_forward(gout_bufs, gssem, grsem)

        @pl.when(i == nt - 1)
        def _():
            _ring_finish(gout_bufs, gssem, grsem)

    tile = lambda cols: pl.BlockSpec((tm, cols), lambda i: (i, 0))
    hbm = pl.BlockSpec(memory_space=pl.ANY)
    outs = pl.pallas_call(
        body, name="mixers_fwd", grid=(nt,),
        in_specs=[tile(IN_COLS), _full((HALO, CONV_CH))] + [_full((1, CONV_CH))] * 4 + [_full((2, HGRN_W))]
        + [_full((D_MODEL, D_MODEL)), tile(D_MODEL), _full((6, D_MODEL)), _full((1, D_MODEL)), _full((1, D_MODEL))]
        + [hbm] * ng,
        out_specs=[tile(D_MODEL), tile(CONV_CH), tile(HGRN_W),
                   pl.BlockSpec((nch, N_HEADS, HEAD_D, HEAD_D), lambda i: (i, 0, 0, 0)),
                   tile(D_MODEL), tile(D_MODEL), tile(D_MODEL)] + [hbm] * ng,
        out_shape=[_big((T, D_MODEL), BF16), _big((T, CONV_CH), F32), _big((T, HGRN_W), F32),
                   _big((T // CHUNK, N_HEADS, HEAD_D, HEAD_D), F32), _big((T, D_MODEL), F32),
                   _big((T, D_MODEL), F32), _big((T, D_MODEL), BF16)] + [_big(b.shape, BF16) for b in gbufs],
        input_output_aliases={n_in + t: n_out + t for t in range(ng)},
        scratch_shapes=[pltpu.VMEM((tm + HALO + SUB, CONV_CH), F32), pltpu.VMEM((N_HEADS, HEAD_D, HEAD_D), F32),
                        pltpu.VMEM((tm, HGRN_W), BF16), pltpu.VMEM((tm, HGRN_W), BF16),
                        pltpu.VMEM((tm, HGRN_W), BF16), pltpu.VMEM((tm, HGRN_W), BF16),
                        pltpu.VMEM((tm, HGRN_W), F32), pltpu.VMEM((tm, tm), BF16),
                        pltpu.VMEM((CONV_CH, CONV_CH), BF16)] + _ring_sems(ng),
        compiler_params=_cp(("arbitrary",), 56),
    )(_hbm(p), wdw, *vecs, lbl, *_hbm(w_out, x), modr, g_post, g_ffn, *[_hbm(b) for b in gbufs])
    return outs[:n_out], outs[n_out:]


def _row_chains(rows, n=2):
    step = rows // n
    return [slice(k * step, (k + 1) * step) for k in range(n)]


def _ffn_blocks():
    return D_FF // FFN_BLOCK, (D_FF // N_CHIPS) // FFN_BLOCK


def _ffn_fwd(h2, w_up_g, w_down, x1, target, modr, g_post):
    T = h2.shape[0]
    tm = min(FFN_TILE, T)
    fb = FFN_BLOCK
    nj, per = _ffn_blocks()

    def body(h_ref, wu_ref, wd_ref, x1_ref, t_ref, mod_ref, g_ref, r_ref, dy2_ref, dx2_ref, st_ref, acc):
        i, j = pl.program_id(0), pl.program_id(1)

        @pl.when((i == 0) & (j == 0))
        def _():
            st_ref[...] = jnp.zeros(st_ref.shape, F32)

        @pl.when(j == 0)
        def _():
            acc[...] = jnp.zeros(acc.shape, F32)

        for rows in _row_chains(tm):
            ra = jnp.maximum(_dot(h_ref[rows, :], wu_ref[0]), 0.0)
            rb = (ra * ra).astype(BF16)
            r_ref[rows, :] = rb
            acc[rows, :] += _dot(rb, wd_ref[...])

        @pl.when(j == nj - 1)
        def _():
            y2 = acc[...]
            rs = lax.rsqrt(jnp.mean(y2 * y2, axis=-1, keepdims=True) + RMS_EPS)
            nh = y2 * rs
            gp = g_ref[...]
            err = x1_ref[...] + mod_ref[5:6, :] * (nh * gp) - t_ref[...]
            dx2 = err * (1.0 / D_MODEL)
            dx2_ref[...] = dx2
            st_ref[0:1, :] += _colsum(err * err)
            st_ref[1:2, :] += _colsum(dx2 * (nh * gp))
            dn = dx2 * mod_ref[5:6, :]
            st_ref[2:3, :] += _colsum(dn * nh)
            dy2_ref[...] = _rms_bwd(dn * gp, nh, rs).astype(BF16)

    tile = pl.BlockSpec((tm, D_MODEL), lambda i, j: (i, 0))
    return pl.pallas_call(
        body, name="ffn_fwd", grid=(T // tm, nj),
        in_specs=[tile, pl.BlockSpec((1, D_MODEL, fb), lambda i, j: (j // per, 0, j % per)),
                  pl.BlockSpec((fb, D_MODEL), lambda i, j: (j, 0)), tile, tile,
                  _full((6, D_MODEL)), _full((1, D_MODEL))],
        out_specs=[pl.BlockSpec((tm, fb), lambda i, j: (i, j)), tile, tile, _full((8, D_MODEL))],
        out_shape=[_big((T, D_FF), BF16), _big((T, D_MODEL), BF16), _big((T, D_MODEL), F32),
                   jax.ShapeDtypeStruct((8, D_MODEL), F32)],
        scratch_shapes=[pltpu.VMEM((tm, D_MODEL), F32)],
        compiler_params=_cp(("arbitrary", "arbitrary"), 56),
    )(*_hbm(h2, w_up_g, w_down, x1, target), modr, g_post)


def _rms_bwd(dxn, xn, rs):
    return rs * (dxn - xn * jnp.mean(dxn * xn, axis=-1, keepdims=True))


def _ffn_bwd(dy2, r, x1, dx2, w_up_g, w_down, modr, g_ffn):
    T = dx2.shape[0]
    tm = min(FFN_TILE, T)
    fb = FFN_BLOCK
    nj, per = _ffn_blocks()

    def body(dy2_ref, r_ref, x1_ref, dx2_ref, wu_ref, wd_ref, mod_ref, gf_ref, da_ref, dx1_ref, st_ref, dh_s):
        i, j = pl.program_id(0), pl.program_id(1)

        @pl.when((i == 0) & (j == 0))
        def _():
            st_ref[...] = jnp.zeros(st_ref.shape, F32)

        @pl.when(j == 0)
        def _():
            dh_s[...] = jnp.zeros(dh_s.shape, F32)

        for rows in _row_chains(tm):
            ra = jnp.sqrt(r_ref[rows, :].astype(F32))
            da = (_dot_nt(dy2_ref[rows, :], wd_ref[...]) * (2.0 * ra)).astype(BF16)
            da_ref[rows, :] = da
            dh_s[rows, :] += _dot_nt(da, wu_ref[0])

        @pl.when(j == nj - 1)
        def _():
            dh = dh_s[...]
            x1v = x1_ref[...]
            rs1 = lax.rsqrt(jnp.mean(x1v * x1v, axis=-1, keepdims=True) + RMS_EPS)
            xn = x1v * rs1
            st_ref[0:1, :] += _colsum(dh)
            st_ref[1:2, :] += _colsum(dh * (xn * gf_ref[...]))
            dsc = dh * (1.0 + mod_ref[4:5, :])
            st_ref[2:3, :] += _colsum(dsc * xn)
            dx1_ref[...] = dx2_ref[...] + _rms_bwd(dsc * gf_ref[...], xn, rs1)

    tile = pl.BlockSpec((tm, D_MODEL), lambda i, j: (i, 0))
    ftile = pl.BlockSpec((tm, fb), lambda i, j: (i, j))
    return pl.pallas_call(
        body, name="ffn_bwd", grid=(T // tm, nj),
        in_specs=[tile, ftile, tile, tile, pl.BlockSpec((1, D_MODEL, fb), lambda i, j: (j // per, 0, j % per)),
                  pl.BlockSpec((fb, D_MODEL), lambda i, j: (j, 0)), _full((6, D_MODEL)), _full((1, D_MODEL))],
        out_specs=[ftile, tile, _full((8, D_MODEL))],
        out_shape=[_big((T, D_FF), BF16), _big((T, D_MODEL), F32), jax.ShapeDtypeStruct((8, D_MODEL), F32)],
        scratch_shapes=[pltpu.VMEM((tm, D_MODEL), F32)],
        compiler_params=_cp(("arbitrary", "arbitrary"), 56),
    )(*_hbm(dy2, r, x1, dx2, w_up_g, w_down), modr, g_ffn)


def _mix_out_bwd(dx1, y, cat, w_out, modr, g_post, swap):
    T = dx1.shape[0]
    tm = _tok_tile(T)
    nt = T // tm
    ns = len(swap)

    def body(*refs):
        dx1_ref, y_ref, cat_ref, w_ref, mod_ref, gp_ref = refs[:6]
        s_ins = refs[6:6 + ns]
        dcat_ref, st_ref, gw_ref = refs[6 + ns:9 + ns]
        s_outs = refs[9 + ns:9 + 2 * ns]
        gacc, gsem, pssem, prsem = refs[9 + 2 * ns:]
        i = pl.program_id(0)

        @pl.when(i == 0)
        def _():
            for cp in _pair_copies(s_ins, s_outs, pssem, prsem):
                cp.start()
            st_ref[...] = jnp.zeros(st_ref.shape, F32)
            gacc[...] = jnp.zeros(gacc.shape, F32)

        dxv, yv = dx1_ref[...], y_ref[...]
        rs = lax.rsqrt(jnp.mean(yv * yv, axis=-1, keepdims=True) + RMS_EPS)
        nh = yv * rs
        st_ref[0:1, :] += _colsum(dxv * (nh * gp_ref[...]))
        dn = dxv * mod_ref[2:3, :]
        st_ref[1:2, :] += _colsum(dn * nh)
        dy = _rms_bwd(dn * gp_ref[...], nh, rs).astype(BF16)
        dcat_ref[...] = _dot_nt(dy, w_ref[...])
        for cols in _row_chains(D_MODEL):
            gacc[:, cols] += _dot_tn(cat_ref[...], dy[:, cols])

        @pl.when(i == nt - 1)
        def _():
            out = pltpu.make_async_copy(gacc, gw_ref, gsem)
            out.start()
            copies = _pair_copies(s_ins, s_outs, pssem, prsem)
            for cp in copies:
                cp.wait_recv()
            for cp in copies:
                cp.wait_send()
            out.wait()

    tile = pl.BlockSpec((tm, D_MODEL), lambda i: (i, 0))
    hbm = pl.BlockSpec(memory_space=pl.ANY)
    outs = pl.pallas_call(
        body, name="mix_out_bwd", grid=(nt,),
        in_specs=[tile, tile, tile, _full((D_MODEL, D_MODEL)), _full((6, D_MODEL)), _full((1, D_MODEL))]
        + [hbm] * ns,
        out_specs=[tile, _full((8, D_MODEL)), hbm] + [hbm] * ns,
        out_shape=[_big((T, D_MODEL), F32), jax.ShapeDtypeStruct((8, D_MODEL), F32), _big((D_MODEL, D_MODEL), F32)]
        + _pair_shapes(swap),
        scratch_shapes=[pltpu.VMEM((D_MODEL, D_MODEL), F32), pltpu.SemaphoreType.DMA] + _pair_sems(ns),
        compiler_params=_cp(("arbitrary",), 48),
    )(*_hbm(dx1, y, cat, w_out), modr, g_post, *[_hbm(g) for g in swap])
    return outs[:3], outs[3:]


def _mixers_bwd(p, dcat, ys, o, states, h1, wdw, vecs, lbl, pairs_b):
    T = p.shape[0]
    tm = min(MIXB_TILE, T)
    nt = T // tm
    nch = tm // CHUNK
    hpt = tm // HALO
    nx = len(pairs_b)
    nb = IN_COLS // N_CHIPS

    def body(*refs):
        (p_ref, ph_ref, dcat_ref, ys_ref, o_ref, st_ref, h1_ref, wdw_ref, bdw_ref, gain_ref, bias_ref, gout_ref,
         lbl_ref) = refs[:13]
        x_ins = refs[13:13 + nx]
        dp_ref, sb_ref, s5_ref, dw_ref = refs[13 + nx:17 + nx]
        x_outs = refs[17 + nx:17 + 2 * nx]
        gin_ref, ginb_ref = refs[17 + 2 * nx], refs[18 + 2 * nx]
        (ubuf, dybuf, carry, dstate, qt_s, kt_s, kh_s, v_s, do_s, egl_s, dqt_s, dkt_s, dkh_s, dv_s, dgl_s,
         dsh, dw8, dshift, lower_s, upper_s, same_s, gmat_s, gacc, gacc_b, dp_prev, h1_prev, gsem, gbsem, xssem,
         xrsem) = refs[19 + 2 * nx:]
        i = pl.program_id(0)
        tile_idx = nt - 1 - i

        @pl.when(i == 0)
        def _():
            for cp in _xchg_copies(x_ins, x_outs, xssem, xrsem):
                cp.start()
            gacc[...] = jnp.zeros(gacc.shape, F32)
            dstate[...] = jnp.zeros(dstate.shape, F32)
            carry[...] = jnp.zeros(carry.shape, F32)
            sb_ref[...] = jnp.zeros(sb_ref.shape, F32)
            s5_ref[...] = jnp.zeros(s5_ref.shape, F32)
            dw_ref[...] = jnp.zeros(dw_ref.shape, F32)
            dw8[...] = jnp.zeros(dw8.shape, F32)
            lower_s[...], upper_s[...], same_s[...] = _chunk_masks(tm)
            gmat_s[...] = _gn_matrix()
            dsh[0:SUB, :] = jnp.zeros((SUB, CONV_CH), F32)
            dsh[SUB + tm:2 * SUB + tm, :] = jnp.zeros((SUB, CONV_CH), F32)
            ubuf[HALO + tm:HALO + tm + SUB, :] = jnp.zeros((SUB, CONV_CH), F32)
            dp_prev[...] = jnp.zeros(dp_prev.shape, BF16)
            h1_prev[...] = jnp.zeros(h1_prev.shape, BF16)

        n_pieces = (tm // CONV_ROWS) * (CONV_CH // LANE)
        per_block = n_pieces // N_CHIPS
        prow = D_MODEL // per_block

        def w_in_grad_piece(k):
            j, part = k // per_block, k % per_block
            rows_k = slice(part * prow, (part + 1) * prow)
            gacc[j, rows_k, :] += _dot_tn(h1_prev[:, rows_k], dp_prev[:, j * nb:(j + 1) * nb])

        uh = ph_ref[:, 0:CONV_CH] * _sig(ph_ref[:, CONV_CH:2 * CONV_CH])
        ubuf[0:HALO, :] = jnp.where(tile_idx > 0, uh, 0.0)
        ubuf[HALO:HALO + tm, :] = p_ref[:, 0:CONV_CH] * _sig(p_ref[:, CONV_CH:2 * CONV_CH])
        gmat = gmat_s[...]
        gain = gain_ref[...]
        yv = ys_ref[...]
        d = yv - _gmean(yv, gmat)
        rs = lax.rsqrt(_gmean(d * d, gmat) + GN_EPS)
        yn = d * rs
        z = yn * gain + bias_ref[...]
        sz = _sig(z)
        dz = dcat_ref[:, 0:CONV_CH] * (sz * (1.0 + z * (1.0 - sz)))
        dyn = dz * gain
        dyc = rs * (dyn - _gmean(dyn, gmat) - yn * _gmean(dyn * yn, gmat))
        s5_ref[0:1, :] += _colsum(dyc)
        s5_ref[1:2, :] += _colsum(dz * yn)
        s5_ref[2:3, :] += _colsum(dz)
        dybuf[tm:tm + HALO, :] = carry[...]
        dybuf[0:tm, :] = dyc
        dsh[SUB:SUB + tm, :] = dyc
        carry[...] = dyc[0:HALO, :]
        for b in range(SUB):
            dshift[...] = dsh[SUB - b:2 * SUB - b + tm, :]
            for j, off in CONV_FWD_TAPS:
                if off % SUB == b:
                    prod = dshift[...] * ubuf[off - b:off - b + tm + SUB, :]
                    dw8[j] += jnp.sum(prod.reshape((tm + SUB) // SUB, SUB, CONV_CH), axis=0)
        for r in range(tm // CONV_ROWS):
            rows = slice(r * CONV_ROWS, (r + 1) * CONV_ROWS)
            for lb_ in range(CONV_CH // LANE):
                lanes = slice(lb_ * LANE, (lb_ + 1) * LANE)
                glanes = slice(CONV_CH + lb_ * LANE, CONV_CH + (lb_ + 1) * LANE)
                w_in_grad_piece(r * (CONV_CH // LANE) + lb_)
                acc = _tap_conv(dybuf, wdw_ref, r * CONV_ROWS, CONV_BWD_TAPS, lanes)
                val = p_ref[rows, lanes]
                sg = _sig(p_ref[rows, glanes])
                dval = acc * sg
                dgate = acc * val * (sg * (1.0 - sg))
                dp_ref[rows, lanes] = dval.astype(BF16)
                dp_ref[rows, glanes] = dgate.astype(BF16)
                sb_ref[0:1, lanes] += _colsum(dval)
                sb_ref[0:1, glanes] += _colsum(dgate)

        o0 = 2 * CONV_CH
        for h in range(N_HEADS):
            sl = slice(h * HEAD_D, (h + 1) * HEAD_D)
            gsl = slice(o0 + 3 * HGRN_W + h * HEAD_D, o0 + 3 * HGRN_W + (h + 1) * HEAD_D)
            oh = o_ref[:, sl]
            gh = p_ref[:, gsl]
            dh = dcat_ref[:, CONV_CH + h * HEAD_D:CONV_CH + (h + 1) * HEAD_D]
            gout = gout_ref[:, sl]
            rsh = lax.rsqrt(jnp.mean(oh * oh, axis=-1, keepdims=True) + RMS_EPS)
            on = oh * rsh
            sgg = _sig(gh)
            dgh = dh * (on * gout) * (sgg * (1.0 + gh * (1.0 - sgg)))
            dm = dh * (gh * sgg)
            s5_ref[3:4, sl] += _colsum(dm * on)
            do_s[:, sl] = _rms_bwd(dm * gout, on, rsh).astype(BF16)
            dp_ref[:, gsl] = dgh.astype(BF16)
            sb_ref[2:3, CONV_CH + h * HEAD_D:CONV_CH + (h + 1) * HEAD_D] += _colsum(dgh)

        lb, _ = _lower_bound(lbl_ref)
        pq = p_ref[:, o0:o0 + HGRN_W]
        pr = _hgrn_prep(pq, p_ref[:, o0 + HGRN_W:o0 + 2 * HGRN_W], lb, lower_s[...])
        qt_s[...] = pr["qt"].astype(BF16)
        kt_s[...] = pr["kt"].astype(BF16)
        kh_s[...] = pr["kh"].astype(BF16)
        v_s[...] = p_ref[:, o0 + 2 * HGRN_W:o0 + 3 * HGRN_W].astype(BF16)
        egl_s[...] = jnp.exp(pr["Gl"])
        tri = _tri()

        def chunk(it, c_):
            ci = nch - 1 - it
            r0 = pl.multiple_of(ci * CHUNK, CHUNK)
            rows = pl.ds(r0, CHUNK)
            for h in range(N_HEADS):
                ls = pl.ds(h * HEAD_D, HEAD_D)
                qc, kc, hc, vc = qt_s[rows, ls], kt_s[rows, ls], kh_s[rows, ls], v_s[rows, ls]
                dob = do_s[rows, ls]
                s0 = st_ref[ci, h]
                s0b = s0.astype(BF16)
                ds1 = dstate[h]
                ds1b = ds1.astype(BF16)
                egl = egl_s[pl.ds(r0, 1), ls]
                att = jnp.where(tri, _dot_nt(qc, kc), 0.0).astype(BF16)
                datt = jnp.where(tri, _dot_nt(dob, vc), 0.0).astype(BF16)
                dv_s[rows, ls] = _dot_tn(att, dob) + _dot_nt(hc, ds1b)
                dqt_s[rows, ls] = _dot(datt, kc) + _dot(dob, s0b)
                dkt_s[rows, ls] = _dot_tn(datt, qc)
                dkh_s[rows, ls] = _dot(vc, ds1b)
                dgl = egl * _colsum(ds1 * s0)
                dgl_s[rows, ls] = jnp.broadcast_to(dgl, (CHUNK, HEAD_D))
                dstate[h] = ds1 * egl + _dot_tn(dob, qc)
            return c_

        lax.fori_loop(0, nch, chunk, 0, unroll=min(CHUNK_UNROLL, nch))
        dqt, dkt, dkh = dqt_s[...], dkt_s[...], dkh_s[...]
        dk = dkt * pr["enG"] + dkh * pr["eGlG"]
        khk = dkh * kh_s[...].astype(F32)
        dG = dqt * qt_s[...].astype(F32) - dkt * kt_s[...].astype(F32) - khk
        dlogf = _mm3(upper_s[...], dG) + _mm3(same_s[...], khk) + dgl_s[...]
        df = dlogf / pr["f"] - dk
        sf, sq = pr["sf"], pr["sq"]
        s5_ref[4:5, :] += _colsum(df * (1.0 - sf))
        dfl = df * (1.0 - lb) * (sf * (1.0 - sf))
        dq = (dqt * pr["eG"]) * (sq * (1.0 + pq * (1.0 - sq)))
        dvv = dv_s[...]
        dp_ref[:, o0:o0 + HGRN_W] = dq.astype(BF16)
        dp_ref[:, o0 + HGRN_W:o0 + 2 * HGRN_W] = dfl.astype(BF16)
        dp_ref[:, o0 + 2 * HGRN_W:o0 + 3 * HGRN_W] = dvv.astype(BF16)
        sb_ref[1:2, 0:HGRN_W] += _colsum(dq)
        sb_ref[1:2, HGRN_W:2 * HGRN_W] += _colsum(dfl)
        sb_ref[2:3, 0:HGRN_W] += _colsum(dvv)

        dp_prev[...] = dp_ref[...]
        h1_prev[...] = h1_ref[...]

        @pl.when(i == nt - 1)
        def _():
            for k in range(n_pieces):
                w_in_grad_piece(k)
            out = pltpu.make_async_copy(gacc, gin_ref, gsem)
            out.start()
            for j in range(N_CHIPS):
                gacc_b[j] = gacc[j].astype(BF16)
            out_b = pltpu.make_async_copy(gacc_b, ginb_ref, gbsem)
            out_b.start()
            for j in range(CONV_K):
                dw_ref[j:j + 1, :] = _colsum(dw8[j])
            copies = _xchg_copies(x_ins, x_outs, xssem, xrsem)
            for cp in copies:
                cp.wait_recv()
            for cp in copies:
                cp.wait_send()
            out.wait()
            out_b.wait()

    rev = lambda cols: pl.BlockSpec((tm, cols), lambda i: (nt - 1 - i, 0))
    halo = pl.BlockSpec((HALO, 2 * CONV_CH), lambda i: (jnp.maximum((nt - 1 - i) * hpt - 1, 0), 0))
    wide = lambda n: pltpu.VMEM((tm, HGRN_W), n)
    hbm = pl.BlockSpec(memory_space=pl.ANY)
    outs = pl.pallas_call(
        body, name="mixers_bwd", grid=(nt,),
        in_specs=[rev(IN_COLS), halo, rev(D_MODEL), rev(CONV_CH), rev(HGRN_W),
                  pl.BlockSpec((nch, N_HEADS, HEAD_D, HEAD_D), lambda i: (nt - 1 - i, 0, 0, 0)), rev(D_MODEL),
                  _full((HALO, CONV_CH))] + [_full((1, CONV_CH))] * 4 + [_full((2, HGRN_W))] + [hbm] * nx,
        out_specs=[rev(IN_COLS), _full((8, D_MODEL)), _full((8, CONV_CH)), _full((HALO, CONV_CH))]
        + [hbm] * (nx + 2),
        out_shape=[_big((T, IN_COLS), BF16), jax.ShapeDtypeStruct((8, D_MODEL), F32),
                   jax.ShapeDtypeStruct((8, CONV_CH), F32), jax.ShapeDtypeStruct((HALO, CONV_CH), F32)]
        + [_big(pb.shape, BF16) for pb in pairs_b]
        + [_big((N_CHIPS, D_MODEL, nb), F32), _big((N_CHIPS, D_MODEL, nb), BF16)],
        scratch_shapes=[pltpu.VMEM((tm + HALO + SUB, CONV_CH), F32), pltpu.VMEM((tm + HALO, CONV_CH), F32),
                        pltpu.VMEM((HALO, CONV_CH), F32), pltpu.VMEM((N_HEADS, HEAD_D, HEAD_D), F32),
                        wide(BF16), wide(BF16), wide(BF16), wide(BF16), wide(BF16),
                        wide(F32), wide(F32), wide(F32), wide(F32), wide(F32), wide(F32),
                        pltpu.VMEM((tm + 2 * SUB, CONV_CH), F32), pltpu.VMEM((HALO, SUB, CONV_CH), F32),
                        pltpu.VMEM((tm + SUB, CONV_CH), F32), pltpu.VMEM((tm, tm), BF16), pltpu.VMEM((tm, tm), BF16),
                        pltpu.VMEM((tm, tm), BF16), pltpu.VMEM((CONV_CH, CONV_CH), BF16),
                        pltpu.VMEM((N_CHIPS, D_MODEL, nb), F32), pltpu.VMEM((N_CHIPS, D_MODEL, nb), BF16),
                        pltpu.VMEM((tm, IN_COLS), BF16), pltpu.VMEM((tm, D_MODEL), BF16),
                        pltpu.SemaphoreType.DMA, pltpu.SemaphoreType.DMA]
        + _xchg_sems(nx),
        compiler_params=_cp(("arbitrary",), 56),
    )(*_hbm(p, p, dcat, ys, o, states, h1), wdw, *vecs, lbl, *[_hbm(pb) for pb in pairs_b])
    return outs[:4], outs[4:4 + nx], (outs[4 + nx], outs[5 + nx])


def _mix_in_bwd(dp, w_in_g, x, dx1, modr, g_pre, pairs_b):
    T = x.shape[0]
    tm = _tok_tile(T)
    nt = T // tm
    nb = IN_COLS // N_CHIPS
    nx = len(pairs_b)

    def body(*refs):
        dp_ref, w_ref, x_ref, dx1_ref, mod_ref, g_ref = refs[:6]
        x_ins = refs[6:6 + nx]
        gx_ref, st_ref = refs[6 + nx:8 + nx]
        x_outs = refs[8 + nx:8 + 2 * nx]
        xssem, xrsem = refs[8 + 2 * nx:]
        i = pl.program_id(0)

        @pl.when(i == 0)
        def _():
            for cp in _xchg_copies(x_ins, x_outs, xssem, xrsem):
                cp.start()
            st_ref[...] = jnp.zeros(st_ref.shape, F32)

        dh = None
        for j in range(N_CHIPS):
            part = _dot_nt(dp_ref[:, j * nb:(j + 1) * nb], w_ref[j])
            dh = part if dh is None else dh + part
        xv = x_ref[...]
        rs = lax.rsqrt(jnp.mean(xv * xv, axis=-1, keepdims=True) + RMS_EPS)
        xn = xv * rs
        st_ref[0:1, :] += _colsum(dh)
        st_ref[1:2, :] += _colsum(dh * (xn * g_ref[...]))
        dsc = dh * (1.0 + mod_ref[1:2, :])
        st_ref[2:3, :] += _colsum(dsc * xn)
        gx_ref[...] = dx1_ref[...] + _rms_bwd(dsc * g_ref[...], xn, rs)

        @pl.when(i == nt - 1)
        def _():
            copies = _xchg_copies(x_ins, x_outs, xssem, xrsem)
            for cp in copies:
                cp.wait_recv()
            for cp in copies:
                cp.wait_send()

    tile = pl.BlockSpec((tm, D_MODEL), lambda i: (i, 0))
    hbm = pl.BlockSpec(memory_space=pl.ANY)
    outs = pl.pallas_call(
        body, name="mix_in_bwd", grid=(nt,),
        in_specs=[pl.BlockSpec((tm, IN_COLS), lambda i: (i, 0)), _full((N_CHIPS, D_MODEL, nb)), tile, tile,
                  _full((6, D_MODEL)), _full((1, D_MODEL))] + [hbm] * nx,
        out_specs=[tile, _full((8, D_MODEL))] + [hbm] * nx,
        out_shape=[_big((T, D_MODEL), F32), jax.ShapeDtypeStruct((8, D_MODEL), F32)]
        + [_big(pb.shape, BF16) for pb in pairs_b],
        scratch_shapes=_xchg_sems(nx),
        compiler_params=_cp(("arbitrary",), 48),
    )(*_hbm(dp, w_in_g, x, dx1), modr, g_pre, *[_hbm(pb) for pb in pairs_b])
    return outs[:2], outs[2:]


def _weight_grad(a, b, a_blocked, b_blocked, name):
    T = a.shape[0]
    tt = min(GRAD_TILE, T)
    nt = T // tt
    ka = a.shape[1] // N_CHIPS if a_blocked else a.shape[1]
    nb = b.shape[1] // N_CHIPS if b_blocked else b.shape[1]

    def body(a_ref, b_ref, o_ref, ob_ref):
        t = pl.program_id(1)

        @pl.when(t == 0)
        def _():
            o_ref[...] = jnp.zeros(o_ref.shape, F32)

        for cols in _row_chains(nb):
            o_ref[0, :, cols] += _dot_tn(a_ref[...], b_ref[:, cols])

        @pl.when(t == nt - 1)
        def _():
            ob_ref[0] = o_ref[0].astype(BF16)

    blk = pl.BlockSpec((1, ka, nb), lambda j, t: (j, 0, 0))
    return pl.pallas_call(
        body, name=name, grid=(N_CHIPS, nt),
        in_specs=[pl.BlockSpec((tt, ka), (lambda j, t: (t, j)) if a_blocked else (lambda j, t: (t, 0))),
                  pl.BlockSpec((tt, nb), (lambda j, t: (t, j)) if b_blocked else (lambda j, t: (t, 0)))],
        out_specs=[blk, blk],
        out_shape=[_big((N_CHIPS, ka, nb), F32), _big((N_CHIPS, ka, nb), BF16)],
        compiler_params=_cp(("arbitrary", "arbitrary"), 48),
    )(*_hbm(a, b))


R_LOSS = 0
R_FFN = 8
R_OUT = 16
R_IN = 24
R_BIN = 32
R_512 = 40
R_DW = 48
N_STAT_ROWS = 80
MOD_ROWS = (R_IN + 0, R_IN + 1, R_OUT + 0, R_FFN + 0, R_FFN + 1, R_LOSS + 1)


def _small_update(gath, params):
    names = ["b_ada", "lb_logits", "g_pre_mix", "b_in", "b_dw", "gn_gain", "gn_bias", "g_hgrn_out", "g_post_mix",
             "g_pre_ffn", "g_post_ffn"]
    flat = []
    for n in names:
        flat += list(params[n])
    n_in = 1 + len(flat)

    def body(*refs):
        g_ref = refs[0]
        prm = {n: refs[1 + 3 * k:4 + 3 * k] for k, n in enumerate(names)}
        outs = refs[n_in:]
        loss_ref, dmod_ref, dwdw_ref = outs[0], outs[1], outs[2]
        res = {n: outs[3 + 4 * k:7 + 4 * k] for k, n in enumerate(names)}
        red = g_ref[0]
        for dev in range(1, N_DEV):
            red = red + g_ref[dev]
        loss_ref[...] = jnp.broadcast_to(
            (0.5 / D_MODEL) * jnp.sum(red[R_LOSS:R_LOSS + 1, :], axis=-1, keepdims=True), loss_ref.shape)
        for dev in range(N_DEV):
            for k, r in enumerate(MOD_ROWS):
                dmod_ref[dev:dev + 1, k * D_MODEL:(k + 1) * D_MODEL] = g_ref[dev, r:r + 1, :]
        dwdw_ref[...] = red[R_DW:R_DW + HALO, 0:CONV_CH]

        def finish(name, pieces):
            w_ref, m_ref, v_ref = prm[name]
            g_out, d_out, m_out, v_out = res[name]
            for rsl, lsl, g in pieces:
                d, m2, v2 = _adam_math(w_ref[rsl, lsl], g, m_ref[rsl, lsl], v_ref[rsl, lsl])
                g_out[rsl, lsl] = g
                d_out[rsl, lsl] = d
                m_out[rsl, lsl] = m2
                v_out[rsl, lsl] = v2

        one = slice(0, 1)
        row = lambda r: red[r:r + 1, :]
        half = lambda r: red[r:r + 1, 0:CONV_CH]
        finish("b_ada", [(one, slice(k * D_MODEL, (k + 1) * D_MODEL), row(r)) for k, r in enumerate(MOD_ROWS)])
        finish("b_in", [(one, slice(k * D_MODEL, (k + 1) * D_MODEL), row(R_BIN + k)) for k in range(3)])
        finish("g_pre_mix", [(one, slice(None), row(R_IN + 2))])
        finish("g_post_mix", [(one, slice(None), row(R_OUT + 1))])
        finish("g_pre_ffn", [(one, slice(None), row(R_FFN + 2))])
        finish("g_post_ffn", [(one, slice(None), row(R_LOSS + 2))])
        finish("b_dw", [(one, slice(None), half(R_512 + 0))])
        finish("gn_gain", [(one, slice(None), half(R_512 + 1))])
        finish("gn_bias", [(one, slice(None), half(R_512 + 2))])
        finish("g_hgrn_out", [(one, slice(None), half(R_512 + 3))])
        s0, s1 = _lower_bound(prm["lb_logits"][0])
        dlb = half(R_512 + 4)
        finish("lb_logits", [(slice(0, 1), slice(None), dlb * s0 * (1.0 - s0)),
                             (slice(1, 2), slice(None), -dlb * s0 * s1)])

    vm = pl.BlockSpec(memory_space=pltpu.VMEM)
    out_shape = [jax.ShapeDtypeStruct((8, 128), F32), jax.ShapeDtypeStruct((N_DEV, 6 * D_MODEL), F32),
                 jax.ShapeDtypeStruct((HALO, CONV_CH), F32)]
    for n in names:
        out_shape += [jax.ShapeDtypeStruct(params[n][0].shape, F32)] * 4
    outs = pl.pallas_call(
        body, name="small_update", out_shape=out_shape,
        in_specs=[vm] * n_in, out_specs=[vm] * len(out_shape),
        compiler_params=_cp(None, 32),
    )(gath, *flat)
    return outs[0], outs[1], outs[2], {n: outs[3 + 4 * k:7 + 4 * k] for k, n in enumerate(names)}


def _wdw_adam(w, g, m, v):
    def body(w_ref, g_ref, m_ref, v_ref, d_out, m_out, v_out):
        d, m2, v2 = _adam_math(w_ref[...], g_ref[...], m_ref[...], v_ref[...])
        d_out[...] = d
        m_out[...] = m2
        v_out[...] = v2

    vm = pl.BlockSpec(memory_space=pltpu.VMEM)
    return pl.pallas_call(
        body, name="wdw_adam", out_shape=[jax.ShapeDtypeStruct(w.shape, F32)] * 3,
        in_specs=[vm] * 4, out_specs=[vm] * 3, compiler_params=_cp(None, 16),
    )(w, g, m, v)


def kernel(x, c, w_ada, b_ada, lb_logits, g_pre_mix, w_in, b_in, w_dw, b_dw, gn_gain, gn_bias, g_hgrn_out, w_out, g_post_mix, g_pre_ffn, w_up, w_down, g_post_ffn, loss_target, m_w_ada, m_b_ada, m_lb_logits, m_g_pre_mix, m_w_in, m_b_in, m_w_dw, m_b_dw, m_gn_gain, m_gn_bias, m_g_hgrn_out, m_w_out, m_g_post_mix, m_g_pre_ffn, m_w_up, m_w_down, m_g_post_ffn, v_w_ada, v_b_ada, v_lb_logits, v_g_pre_mix, v_w_in, v_b_in, v_w_dw, v_b_dw, v_gn_gain, v_gn_bias, v_g_hgrn_out, v_w_out, v_g_post_mix, v_g_pre_ffn, v_w_up, v_w_down, v_g_post_ffn):
    ax, ay, ac = lax.axis_index("x"), lax.axis_index("y"), lax.axis_index("c")
    chip = 2 * ax + ay
    T = x.shape[1]
    xs, tgt = x[0], loss_target[0]
    ada_cols = w_ada.shape[2]

    b_sh = lax.dynamic_slice_in_dim(b_ada, chip * ada_cols, ada_cols, axis=1)
    wdw_pad = jnp.pad(w_dw[0], ((0, HALO - CONV_K), (0, 0)))
    chip1 = jnp.reshape(chip, (1,)).astype(jnp.int32)
    place = jnp.stack([ac, chip]).astype(jnp.int32)
    bufs = {t: _cast_own(chip1, w[0], "cast_" + t)
            for w, t in ((w_in, "w_in"), (w_out, "w_out"), (w_up, "w_up"), (w_down, "w_down"))}
    vec = (b_dw, gn_gain, gn_bias, g_hgrn_out)

    p, h1, modr, c8, wdwg, w_in_g, w_out_g = _mix_in_fwd(
        chip1, xs, g_pre_mix, b_in.reshape(N_CHIPS, 1, IN_COLS // N_CHIPS), c, b_sh, wdw_pad, w_ada[0],
        bufs["w_in"], bufs["w_out"])
    wdw_all = jnp.transpose(wdwg, (1, 0, 2)).reshape(HALO, CONV_CH)
    w_out_f = w_out_g.reshape(D_MODEL, D_MODEL)
    (cat, ys, o, states, y, x1, h2), (w_up_g, w_down_g) = _mixers_fwd(
        p, wdw_all, vec, lb_logits, w_out_f, xs, modr, g_post_mix, g_pre_ffn, [bufs["w_up"], bufs["w_down"]])
    w_down_f = w_down_g.reshape(D_FF, D_MODEL)
    r, dy2, dx2, st_loss = _ffn_fwd(h2, w_up_g, w_down_f, x1, tgt, modr, g_post_ffn)

    def pair_sums(grads, got, tags):
        return [_pair_sum(place, g, o_, "pair_sum_" + t) for (g, _), o_, t in zip(grads, got, tags)]

    da, dx1, st_ffn = _ffn_bwd(dy2, r, x1, dx2, w_up_g, w_down_f, modr, g_pre_ffn)
    g_up = _weight_grad(h2, da, False, True, "grad_w_up")
    g_down = _weight_grad(r, dy2, True, False, "grad_w_down")
    (dcat, st_out, g_out), got_ud = _mix_out_bwd(dx1, y, cat, w_out_f, modr, g_post_mix, [g_up[1], g_down[1]])
    g_out = g_out.reshape(N_CHIPS, D_MODEL // N_CHIPS, D_MODEL)
    got_o = _pair_swap([g_out], "pair_swap_w_out")
    early = pair_sums([(g_out, None), g_up, g_down], list(got_o) + list(got_ud), ["w_out", "w_up", "w_down"])
    (dp, st_bin, st_512, dwdw), got_early, g_in = _mixers_bwd(p, dcat, ys, o, states, h1, wdw_all, vec, lb_logits,
                                                              [pb for _, pb in early])
    late = pair_sums([g_in], _pair_swap([g_in[1]], "pair_swap_w_in"), ["w_in"])
    (grad_x, st_in), got_late = _mix_in_bwd(dp, w_in_g, xs, dx1, modr, g_pre_mix, [late[0][1]])
    fulls = [_chip_sum(place, pf, gb, "chip_sum_" + t)
             for (pf, _), gb, t in zip(late + early, list(got_late) + list(got_early), ["w_in", "w_out", "w_up", "w_down"])]

    pad_lanes = lambda s: jnp.pad(s, ((0, 0), (0, D_MODEL - s.shape[1])))
    stats = jnp.concatenate([st_loss, st_ffn, st_out, st_in, st_bin, pad_lanes(st_512), pad_lanes(dwdw)], axis=0)
    (g_w_in, g_w_out, g_w_up, g_w_down), gath = _final_exchange(fulls, stats)
    small = {"b_ada": (b_ada, m_b_ada, v_b_ada), "lb_logits": (lb_logits, m_lb_logits, v_lb_logits),
             "g_pre_mix": (g_pre_mix, m_g_pre_mix, v_g_pre_mix), "b_in": (b_in, m_b_in, v_b_in),
             "b_dw": (b_dw, m_b_dw, v_b_dw), "gn_gain": (gn_gain, m_gn_gain, v_gn_gain),
             "gn_bias": (gn_bias, m_gn_bias, v_gn_bias), "g_hgrn_out": (g_hgrn_out, m_g_hgrn_out, v_g_hgrn_out),
             "g_post_mix": (g_post_mix, m_g_post_mix, v_g_post_mix), "g_pre_ffn": (g_pre_ffn, m_g_pre_ffn, v_g_pre_ffn),
             "g_post_ffn": (g_post_ffn, m_g_post_ffn, v_g_post_ffn)}
    loss_t, dmod_all, dwdw_sum, sres = _small_update(gath, small)
    loss = loss_t[0, 0]

    res = dict(sres)
    dmod_sh = lax.dynamic_slice_in_dim(dmod_all, chip * ada_cols, ada_cols, axis=1)
    res["w_ada"] = [t[None] for t in _ada_grad_adam(jnp.transpose(c8), dmod_sh, w_ada[0], m_w_ada[0], v_w_ada[0])]
    g_wdw = lax.dynamic_slice_in_dim(dwdw_sum, chip * HEAD_D, HEAD_D, axis=1)[:CONV_K][None]
    res["w_dw"] = [g_wdw] + list(_wdw_adam(w_dw, g_wdw, m_w_dw, v_w_dw))
    for name, g, w, m, v in (("w_in", g_w_in, w_in, m_w_in, v_w_in), ("w_out", g_w_out, w_out, m_w_out, v_w_out),
                             ("w_up", g_w_up, w_up, m_w_up, v_w_up), ("w_down", g_w_down, w_down, m_w_down, v_w_down)):
        d, m2, v2 = _adam_big(w[0], g, m[0], v[0], "adam_" + name)
        res[name] = [g[None], d[None], m2[None], v2[None]]

    order = ["w_ada", "b_ada", "lb_logits", "g_pre_mix", "w_in", "b_in", "w_dw", "b_dw", "gn_gain", "gn_bias",
             "g_hgrn_out", "w_out", "g_post_mix", "g_pre_ffn", "w_up", "w_down", "g_post_ffn"]
    out = [loss, grad_x[None]]
    for k in range(4):
        out += [res[n][k] for n in order]
    return tuple(out)
```

```python
import jax
import jax.numpy as jnp
from jax import lax
from jax.experimental import pallas as pl
from jax.experimental.pallas import tpu as pltpu

F32, BF16 = jnp.float32, jnp.bfloat16
D_MODEL = 1024
CONV_CH = 512
HGRN_W = 512
N_HEADS = 4
HEAD_D = 128
CONV_K = 31
GN_GROUP = 64
GN_SHIFT = 6
IN_COLS = 3072
D_FF = 4096
CHUNK = 64
CHUNK_SHIFT = 6
N_CHIPS = 4
N_DEV = 8
RMS_EPS = 1e-6
GN_EPS = 1e-5
ADAM_LR, ADAM_B1, ADAM_B2, ADAM_EPS, ADAM_WD, ADAM_STEP = 0.001, 0.9, 0.999, 1e-08, 0.01, 10
TOK_TILE = 512
MIXIN_TILE = 1024
MIXB_TILE = 256
FFN_TILE = 1024
FFN_BLOCK = 512
GRAD_TILE = 2048
HALO = 32
SUB = 8
LANE = 128
CONV_ROWS = 128
CHUNK_UNROLL = 8
MIB = 1 << 20
MESH = pl.DeviceIdType.MESH
OTHER_CHIPS = ((0, 1), (1, 0), (1, 1))


def _cp(sem=None, vmem_mib=48):
    return pltpu.CompilerParams(dimension_semantics=sem, vmem_limit_bytes=vmem_mib * MIB)


def _dot(a, b):
    return jnp.dot(a, b, preferred_element_type=F32)


def _dot_nt(a, b):
    return lax.dot_general(a, b, (((1,), (1,)), ((), ())), preferred_element_type=F32)


def _dot_tn(a, b):
    return lax.dot_general(a, b, (((0,), (0,)), ((), ())), preferred_element_type=F32)


def _sig(v):
    return 0.5 * jnp.tanh(0.5 * v) + 0.5


def _colsum(v):
    return jnp.sum(v, axis=0, keepdims=True)


def _flip(v, b):
    return 1 - v if b else v


def _rcopy(src, dst, ssem, rsem, dev):
    return pltpu.make_async_remote_copy(src_ref=src, dst_ref=dst, send_sem=ssem, recv_sem=rsem,
                                        device_id=dev, device_id_type=MESH)


def _place():
    return lax.axis_index("x"), lax.axis_index("y"), lax.axis_index("c")


def _full(shape):
    return pl.BlockSpec(shape, lambda *_: (0,) * len(shape))


def _big(shape, dtype):
    return pltpu.HBM(shape, dtype)


def _hbm(*arrays):
    out = [pltpu.with_memory_space_constraint(a, pltpu.HBM) for a in arrays]
    return out[0] if len(out) == 1 else out


def _split2(v):
    hi = v.astype(BF16)
    lo = (v - hi.astype(F32)).astype(BF16)
    return hi, lo


def _split3(v):
    h1 = v.astype(BF16)
    r1 = v - h1.astype(F32)
    h2 = r1.astype(BF16)
    h3 = (r1 - h2.astype(F32)).astype(BF16)
    return h1, h2, h3


def _mm3(mat, v):
    h1, h2, h3 = _split3(v)
    return _dot(mat, h1) + _dot(mat, h2) + _dot(mat, h3)


def _gn_matrix():
    r = lax.broadcasted_iota(jnp.int32, (CONV_CH, CONV_CH), 0) >> GN_SHIFT
    c = lax.broadcasted_iota(jnp.int32, (CONV_CH, CONV_CH), 1) >> GN_SHIFT
    return jnp.where(r == c, 1.0 / GN_GROUP, 0.0).astype(BF16)


def _gmean(v, gmat):
    hi, lo = _split2(v)
    return _dot(hi, gmat) + _dot(lo, gmat)


def _chunk_masks(tm):
    r = lax.broadcasted_iota(jnp.int32, (tm, tm), 0)
    c = lax.broadcasted_iota(jnp.int32, (tm, tm), 1)
    same = (r >> CHUNK_SHIFT) == (c >> CHUNK_SHIFT)
    one = lambda m: jnp.where(m, 1.0, 0.0).astype(BF16)
    return one(same & (c <= r)), one(same & (c >= r)), one(same)


def _tri():
    return lax.broadcasted_iota(jnp.int32, (CHUNK, CHUNK), 0) >= lax.broadcasted_iota(jnp.int32, (CHUNK, CHUNK), 1)


def _lower_bound(lbl_ref):
    l0, l1 = lbl_ref[0:1, :], lbl_ref[1:2, :]
    mx = jnp.maximum(l0, l1)
    e0, e1 = jnp.exp(l0 - mx), jnp.exp(l1 - mx)
    return e0 / (e0 + e1), e1 / (e0 + e1)


CONV_FWD_TAPS = tuple((j, HALO - (CONV_K - 1) + j) for j in range(CONV_K))
CONV_BWD_TAPS = tuple((j, (CONV_K - 1) - j) for j in range(CONV_K))


def _tap_conv(src_ref, w_ref, row0, taps, lanes):
    acc = None
    for b in range(SUB):
        pb = None
        for j, off in taps:
            if off % SUB == b:
                lo = row0 + off - b
                term = w_ref[j:j + 1, lanes] * src_ref[lo:lo + CONV_ROWS + SUB, lanes]
                pb = term if pb is None else pb + term
        if pb is not None:
            sh = pb[b:b + CONV_ROWS, :]
            acc = sh if acc is None else acc + sh
    return acc


def _hgrn_prep(pq, pf, lb, lower):
    sq = _sig(pq)
    qf = pq * sq
    sf = _sig(pf)
    f = lb + (1.0 - lb) * sf
    logf = jnp.log(f)
    k = 1.0 - f
    G = _mm3(lower, logf)
    rows, cols = G.shape
    g3 = G.reshape(rows // CHUNK, CHUNK, cols)
    Gl = jnp.broadcast_to(g3[:, CHUNK - 1:CHUNK, :], g3.shape).reshape(rows, cols)
    eG, enG, eGlG = jnp.exp(G), jnp.exp(-G), jnp.exp(Gl - G)
    return dict(sq=sq, sf=sf, f=f, Gl=Gl, eG=eG, enG=enG, eGlG=eGlG, qt=qf * eG, kt=k * enG, kh=k * eGlG)


def _ada_exchange(c_row, w_ada, b_sh, wdw_pad):
    ncol = w_ada.shape[1]

    def body(c_ref, w_ref, b_ref, wdw_ref, call_ref, c8_ref, modg_ref, wdwg_ref, rows_s, sa, ra, sw, rw, sm, rm):
        x, y, c = _place()
        me = 4 * x + 2 * y + c
        chip = 2 * x + y
        cv = c_ref[...]
        call_ref[me] = cv * _sig(cv)
        wdwg_ref[chip] = wdw_ref[...]
        sends = []
        for m in range(1, N_DEV):
            peer = (_flip(x, m >> 2), _flip(y, (m >> 1) & 1), _flip(c, m & 1))
            cp = _rcopy(call_ref.at[me], call_ref.at[me], sa.at[m - 1], ra.at[m - 1], peer)
            cp.start()
            sends.append(cp)
        for k, (fx, fy) in enumerate(OTHER_CHIPS):
            peer = (_flip(x, fx), _flip(y, fy), c)
            cp = _rcopy(wdwg_ref.at[chip], wdwg_ref.at[chip], sw.at[k], rw.at[k], peer)
            cp.start()
            sends.append(cp)
        for m in range(1, N_DEV):
            peer = (_flip(x, m >> 2), _flip(y, (m >> 1) & 1), _flip(c, m & 1))
            pid = 4 * peer[0] + 2 * peer[1] + peer[2]
            _rcopy(call_ref.at[pid], call_ref.at[pid], sa.at[m - 1], ra.at[m - 1], peer).wait_recv()
        for b in range(N_DEV):
            c8_ref[b:b + 1, :] = call_ref[b]
        mod_all = _dot(c8_ref[...].astype(BF16), w_ref[...].astype(BF16)) + b_ref[...]
        for b in range(N_DEV):
            rows_s[b] = mod_all[b:b + 1, :]
        modg_ref[chip] = rows_s[me]
        for k, (fx, fy) in enumerate(OTHER_CHIPS):
            peer = (_flip(x, fx), _flip(y, fy), c)
            pid = 4 * peer[0] + 2 * peer[1] + peer[2]
            cp = _rcopy(rows_s.at[pid], modg_ref.at[chip], sm.at[k], rm.at[k], peer)
            cp.start()
            sends.append(cp)
        for k, (fx, fy) in enumerate(OTHER_CHIPS):
            peer = (_flip(x, fx), _flip(y, fy), c)
            pchip = 2 * peer[0] + peer[1]
            _rcopy(rows_s.at[0], modg_ref.at[pchip], sm.at[k], rm.at[k], peer).wait_recv()
            _rcopy(wdwg_ref.at[pchip], wdwg_ref.at[pchip], sw.at[k], rw.at[k], peer).wait_recv()
        for cp in sends:
            cp.wait_send()

    vm = pl.BlockSpec(memory_space=pltpu.VMEM)
    return pl.pallas_call(
        body, name="ada_exchange",
        out_shape=[jax.ShapeDtypeStruct((N_DEV, 1, D_MODEL), F32), jax.ShapeDtypeStruct((N_DEV, D_MODEL), F32),
                   jax.ShapeDtypeStruct((N_CHIPS, 1, ncol), F32), jax.ShapeDtypeStruct((N_CHIPS, HALO, HEAD_D), F32)],
        in_specs=[vm] * 4, out_specs=[vm] * 4,
        scratch_shapes=[pltpu.VMEM((N_DEV, 1, ncol), F32),
                        pltpu.SemaphoreType.DMA((N_DEV - 1,)), pltpu.SemaphoreType.DMA((N_DEV - 1,)),
                        pltpu.SemaphoreType.DMA((3,)), pltpu.SemaphoreType.DMA((3,)),
                        pltpu.SemaphoreType.DMA((3,)), pltpu.SemaphoreType.DMA((3,))],
        compiler_params=_cp(None, 32),
    )(c_row, w_ada, b_sh, wdw_pad)


def _cast_own(chip1, shard, name):
    rows, cols = shard.shape
    tr = _row_tile(rows)

    def body(ch_ref, s_ref, o_ref):
        o_ref[0] = s_ref[...].astype(BF16)

    gs = pltpu.PrefetchScalarGridSpec(
        num_scalar_prefetch=1, grid=(rows // tr,),
        in_specs=[pl.BlockSpec((tr, cols), lambda i, ch: (i, 0))],
        out_specs=pl.BlockSpec((1, tr, cols), lambda i, ch: (ch[0], i, 0)))
    return pl.pallas_call(
        body, name=name, grid_spec=gs, out_shape=_big((N_CHIPS, rows, cols), BF16),
        compiler_params=_cp(("arbitrary",), 32),
    )(chip1, _hbm(shard))


def _slab(buf, ch, core):
    hs = buf.shape[1] // 2
    return buf.at[ch, pl.ds(core * hs, hs), :]


def _gather_start(bufs, ssem, rsem, relations=(0, 1, 2)):
    x, y, c = _place()
    chip = 2 * x + y
    for k in relations:
        fx, fy = OTHER_CHIPS[k]
        peer = (_flip(x, fx), _flip(y, fy), c)
        for t, buf in enumerate(bufs):
            _rcopy(_slab(buf, chip, c), _slab(buf, chip, c), ssem.at[t * 3 + k], rsem.at[t * 3 + k], peer).start()


def _gather_pass_on(bufs, ssem, rsem):
    nt = len(bufs)
    x, y, c = _place()
    sibling = (x, y, 1 - c)
    for k, (fx, fy) in enumerate(OTHER_CHIPS):
        peer = (_flip(x, fx), _flip(y, fy), c)
        pchip = 2 * peer[0] + peer[1]
        for t, buf in enumerate(bufs):
            _rcopy(_slab(buf, pchip, c), _slab(buf, pchip, c), ssem.at[t * 3 + k], rsem.at[t * 3 + k], peer).wait_recv()
            _rcopy(_slab(buf, pchip, c), _slab(buf, pchip, c), ssem.at[3 * nt + t * 3 + k],
                   rsem.at[3 * nt + t * 3 + k], sibling).start()


def _gather_drain(bufs, ssem, rsem):
    nt = len(bufs)
    x, y, c = _place()
    chip = 2 * x + y
    sibling = (x, y, 1 - c)
    for k, (fx, fy) in enumerate(OTHER_CHIPS):
        peer = (_flip(x, fx), _flip(y, fy), c)
        pchip = 2 * peer[0] + peer[1]
        for t, buf in enumerate(bufs):
            _rcopy(_slab(buf, pchip, 1 - c), _slab(buf, pchip, 1 - c), ssem.at[3 * nt + t * 3 + k],
                   rsem.at[3 * nt + t * 3 + k], sibling).wait_recv()
            _rcopy(_slab(buf, chip, c), _slab(buf, chip, c), ssem.at[t * 3 + k], rsem.at[t * 3 + k], peer).wait_send()
            _rcopy(_slab(buf, pchip, c), _slab(buf, pchip, c), ssem.at[3 * nt + t * 3 + k],
                   rsem.at[3 * nt + t * 3 + k], sibling).wait_send()


def _gather_finish(bufs, ssem, rsem):
    _gather_pass_on(bufs, ssem, rsem)
    _gather_drain(bufs, ssem, rsem)


def _gather_arrive(bufs, k, ssem, rsem):
    nt = len(bufs)
    x, y, c = _place()
    fx, fy = OTHER_CHIPS[k]
    peer = (_flip(x, fx), _flip(y, fy), c)
    pchip = 2 * peer[0] + peer[1]
    for t, buf in enumerate(bufs):
        _rcopy(_slab(buf, pchip, c), _slab(buf, pchip, c), ssem.at[t * 3 + k], rsem.at[t * 3 + k], peer).wait_recv()
        _rcopy(_slab(buf, pchip, c), _slab(buf, pchip, c), ssem.at[3 * nt + t * 3 + k],
               rsem.at[3 * nt + t * 3 + k], (x, y, 1 - c)).start()
    for t, buf in enumerate(bufs):
        _rcopy(_slab(buf, pchip, 1 - c), _slab(buf, pchip, 1 - c), ssem.at[3 * nt + t * 3 + k],
               rsem.at[3 * nt + t * 3 + k], (x, y, 1 - c)).wait_recv()


def _gather_sends_done(bufs, ssem, rsem):
    nt = len(bufs)
    x, y, c = _place()
    chip = 2 * x + y
    for k, (fx, fy) in enumerate(OTHER_CHIPS):
        peer = (_flip(x, fx), _flip(y, fy), c)
        pchip = 2 * peer[0] + peer[1]
        for t, buf in enumerate(bufs):
            _rcopy(_slab(buf, chip, c), _slab(buf, chip, c), ssem.at[t * 3 + k], rsem.at[t * 3 + k], peer).wait_send()
            _rcopy(_slab(buf, pchip, c), _slab(buf, pchip, c), ssem.at[3 * nt + t * 3 + k],
                   rsem.at[3 * nt + t * 3 + k], (x, y, 1 - c)).wait_send()


def _ring_parts(buf):
    x, y, c = _place()
    ynb, xnb = (x, 1 - y, c), (1 - x, y, c)
    ychip, xchip, dchip = 2 * x + (1 - y), 2 * (1 - x) + y, 2 * (1 - x) + (1 - y)
    hs = buf.shape[1] // 2

    def piece(ch, q):
        return buf.at[ch, pl.ds(c * hs + q * (hs // 2), hs // 2), :]

    return ynb, xnb, ychip, xchip, dchip, piece


def _ring_start(bufs, ssem, rsem):
    x, y, c = _place()
    chip = 2 * x + y
    for t, buf in enumerate(bufs):
        ynb, xnb, _, _, _, _ = _ring_parts(buf)
        _rcopy(_slab(buf, chip, c), _slab(buf, chip, c), ssem.at[2 * t], rsem.at[2 * t], ynb).start()
        _rcopy(_slab(buf, chip, c), _slab(buf, chip, c), ssem.at[2 * t + 1], rsem.at[2 * t + 1], xnb).start()


def _ring_forward(bufs, ssem, rsem):
    nt = len(bufs)
    _, _, c = _place()
    for t, buf in enumerate(bufs):
        ynb, xnb, ychip, xchip, _, piece = _ring_parts(buf)
        _rcopy(_slab(buf, ychip, c), _slab(buf, ychip, c), ssem.at[2 * t], rsem.at[2 * t], ynb).wait_recv()
        _rcopy(piece(ychip, 0), piece(ychip, 0), ssem.at[2 * nt + 2 * t], rsem.at[2 * nt + 2 * t], xnb).start()
        _rcopy(_slab(buf, xchip, c), _slab(buf, xchip, c), ssem.at[2 * t + 1], rsem.at[2 * t + 1], xnb).wait_recv()
        _rcopy(piece(xchip, 1), piece(xchip, 1), ssem.at[2 * nt + 2 * t + 1], rsem.at[2 * nt + 2 * t + 1], ynb).start()


def _ring_finish(bufs, ssem, rsem):
    nt = len(bufs)
    x, y, c = _place()
    chip = 2 * x + y
    sibling = (x, y, 1 - c)
    for t, buf in enumerate(bufs):
        ynb, xnb, ychip, xchip, dchip, piece = _ring_parts(buf)
        _rcopy(piece(dchip, 0), piece(dchip, 0), ssem.at[2 * nt + 2 * t], rsem.at[2 * nt + 2 * t], xnb).wait_recv()
        _rcopy(piece(dchip, 1), piece(dchip, 1), ssem.at[2 * nt + 2 * t + 1], rsem.at[2 * nt + 2 * t + 1],
               ynb).wait_recv()
        for k, ch in enumerate((ychip, xchip, dchip)):
            _rcopy(_slab(buf, ch, c), _slab(buf, ch, c), ssem.at[4 * nt + 3 * t + k], rsem.at[4 * nt + 3 * t + k],
                   sibling).start()
    for t, buf in enumerate(bufs):
        ynb, xnb, ychip, xchip, dchip, piece = _ring_parts(buf)
        for k, ch in enumerate((ychip, xchip, dchip)):
            _rcopy(_slab(buf, ch, 1 - c), _slab(buf, ch, 1 - c), ssem.at[4 * nt + 3 * t + k],
                   rsem.at[4 * nt + 3 * t + k], sibling).wait_recv()
            _rcopy(_slab(buf, ch, c), _slab(buf, ch, c), ssem.at[4 * nt + 3 * t + k], rsem.at[4 * nt + 3 * t + k],
                   sibling).wait_send()
        _rcopy(_slab(buf, chip, c), _slab(buf, chip, c), ssem.at[2 * t], rsem.at[2 * t], ynb).wait_send()
        _rcopy(_slab(buf, chip, c), _slab(buf, chip, c), ssem.at[2 * t + 1], rsem.at[2 * t + 1], xnb).wait_send()
        _rcopy(piece(ychip, 0), piece(ychip, 0), ssem.at[2 * nt + 2 * t], rsem.at[2 * nt + 2 * t], xnb).wait_send()
        _rcopy(piece(xchip, 1), piece(xchip, 1), ssem.at[2 * nt + 2 * t + 1], rsem.at[2 * nt + 2 * t + 1],
               ynb).wait_send()


def _ring_sems(nt):
    return [pltpu.SemaphoreType.DMA((7 * nt,)), pltpu.SemaphoreType.DMA((7 * nt,))]


def _gather_sems(nt):
    return [pltpu.SemaphoreType.DMA((6 * nt,)), pltpu.SemaphoreType.DMA((6 * nt,))]


def _pair_copies(ins, outs, ssem, rsem):
    x, y, c = _place()
    copies = []
    for t in range(len(ins)):
        hs = ins[t].shape[1] // 2
        copies.append(_rcopy(ins[t].at[:, pl.ds((1 - c) * hs, hs), :], outs[t], ssem.at[t], rsem.at[t], (x, y, 1 - c)))
    return copies


def _pair_shapes(grads):
    return [_big((g.shape[0], g.shape[1] // 2, g.shape[2]), g.dtype) for g in grads]


def _pair_sems(nt):
    return [pltpu.SemaphoreType.DMA((nt,)), pltpu.SemaphoreType.DMA((nt,))]


def _pair_swap(grads, name):
    nt = len(grads)
    hbm = pl.BlockSpec(memory_space=pl.ANY)

    def body(*refs):
        copies = _pair_copies(refs[:nt], refs[nt:2 * nt], refs[2 * nt], refs[2 * nt + 1])
        for cp in copies:
            cp.start()
        for cp in copies:
            cp.wait_recv()
        for cp in copies:
            cp.wait_send()

    return pl.pallas_call(
        body, name=name, out_shape=_pair_shapes(grads), in_specs=[hbm] * nt, out_specs=[hbm] * nt,
        scratch_shapes=_pair_sems(nt),
    )(*[_hbm(g) for g in grads])


def _xchg_copies(ins, outs, ssem, rsem):
    x, y, c = _place()
    copies = []
    for k, (fx, fy) in enumerate(OTHER_CHIPS):
        peer = (_flip(x, fx), _flip(y, fy), c)
        for t in range(len(ins)):
            copies.append(_rcopy(ins[t].at[k], outs[t].at[k], ssem.at[t * 3 + k], rsem.at[t * 3 + k], peer))
    return copies


def _xchg_sems(nt):
    return [pltpu.SemaphoreType.DMA((3 * nt,)), pltpu.SemaphoreType.DMA((3 * nt,))]


def _final_exchange(fulls, stats):
    nt = len(fulls)
    rows, cols = stats.shape
    hbm = pl.BlockSpec(memory_space=pl.ANY)
    vm = pl.BlockSpec(memory_space=pltpu.VMEM)

    def body(*refs):
        ins, s_ref = refs[:nt], refs[nt]
        outs, g_ref = refs[nt + 1:2 * nt + 1], refs[2 * nt + 1]
        hssem, hrsem, ssem, rsem = refs[2 * nt + 2:]
        x, y, c = _place()
        me, sibling = (x, y, c), (x, y, 1 - c)
        halves = []
        for t in range(nt):
            hs = ins[t].shape[0] // 2
            mine = pl.ds(c * hs, hs)
            cp = _rcopy(ins[t].at[mine, :], outs[t].at[mine, :], hssem.at[t], hrsem.at[t], sibling)
            cp.start()
            halves.append(cp)

        chips = [(_flip(x, fx), _flip(y, fy)) for fx, fy in OTHER_CHIPS]

        def blk(px, py, pc):
            return g_ref.at[4 * px + 2 * py + pc]

        def copy(k, block, to, src=None):
            return _rcopy(blk(*block) if src is None else src, blk(*block), ssem.at[k], rsem.at[k], to)

        g_ref[4 * x + 2 * y + c] = s_ref[...]
        first = [copy(0, me, sibling, src=s_ref)]
        first += [copy(1 + j, me, (*chip, c), src=s_ref) for j, chip in enumerate(chips)]
        for cp in first:
            cp.start()
        passed = [copy(4 + j, (*chip, c), sibling) for j, chip in enumerate(chips)]
        for j, chip in enumerate(chips):
            copy(1 + j, (*chip, c), me).wait_recv()
            passed[j].start()
        copy(0, sibling, me).wait_recv()
        for j, chip in enumerate(chips):
            copy(4 + j, (*chip, 1 - c), me).wait_recv()
        for t in range(nt):
            hs = ins[t].shape[0] // 2
            other = pl.ds((1 - c) * hs, hs)
            _rcopy(ins[t].at[other, :], outs[t].at[other, :], hssem.at[t], hrsem.at[t], sibling).wait_recv()
        for cp in first + passed + halves:
            cp.wait_send()

    outs = pl.pallas_call(
        body, name="final_exchange",
        out_shape=[_big(f.shape, F32) for f in fulls] + [jax.ShapeDtypeStruct((N_DEV, rows, cols), F32)],
        in_specs=[hbm] * nt + [vm], out_specs=[hbm] * nt + [vm],
        input_output_aliases={t: t for t in range(nt)},
        scratch_shapes=[pltpu.SemaphoreType.DMA((nt,)), pltpu.SemaphoreType.DMA((nt,)),
                        pltpu.SemaphoreType.DMA((7,)), pltpu.SemaphoreType.DMA((7,))],
        compiler_params=_cp(None, 32),
    )(*[_hbm(f) for f in fulls], stats)
    return outs[:nt], outs[nt]


def _row_tile(rows):
    return min(rows, 512)


def _pair_sum(place, grad, got, name):
    nb, hs, cols = got.shape
    tr = _row_tile(hs)
    nr = hs // tr

    def body(pl_ref, g_ref, o_ref, pf_ref, pb_ref):
        j = pl.program_id(1)
        s = g_ref[0] + o_ref[0].astype(F32)

        @pl.when(j == 0)
        def _():
            pf_ref[...] = s

        @pl.when(j > 0)
        def _():
            pb_ref[0] = s.astype(BF16)

    gs = pltpu.PrefetchScalarGridSpec(
        num_scalar_prefetch=1, grid=(nr, nb),
        in_specs=[pl.BlockSpec((1, tr, cols), lambda i, j, p: (p[1] ^ j, p[0] * nr + i, 0)),
                  pl.BlockSpec((1, tr, cols), lambda i, j, p: (p[1] ^ j, i, 0))],
        out_specs=[pl.BlockSpec((tr, cols), lambda i, j, p: (i, 0)),
                   pl.BlockSpec((1, tr, cols), lambda i, j, p: (jnp.maximum(j - 1, 0), i, 0))])
    return pl.pallas_call(
        body, name=name, grid_spec=gs,
        out_shape=[_big((hs, cols), F32), _big((nb - 1, hs, cols), BF16)],
        compiler_params=_cp(("arbitrary", "arbitrary"), 32),
    )(place, *_hbm(grad, got))


def _chip_sum(place, pair_f, got_b, name):
    nb, hs, cols = got_b.shape
    tr = _row_tile(hs)
    nr = hs // tr

    def body(pl_ref, pf_ref, gb_ref, o_ref):
        acc = pf_ref[...]
        for k in range(nb):
            acc = acc + gb_ref[k].astype(F32)
        o_ref[...] = acc

    gs = pltpu.PrefetchScalarGridSpec(
        num_scalar_prefetch=1, grid=(nr,),
        in_specs=[pl.BlockSpec((tr, cols), lambda i, p: (i, 0)),
                  pl.BlockSpec((nb, tr, cols), lambda i, p: (0, i, 0))],
        out_specs=pl.BlockSpec((tr, cols), lambda i, p: (p[0] * nr + i, 0)))
    return pl.pallas_call(
        body, name=name, grid_spec=gs,
        out_shape=_big((2 * hs, cols), F32),
        compiler_params=_cp(("arbitrary",), 32),
    )(place, *_hbm(pair_f, got_b))


def _adam_math(w, g, m, v):
    m2 = ADAM_B1 * m + (1.0 - ADAM_B1) * g
    v2 = ADAM_B2 * v + (1.0 - ADAM_B2) * (g * g)
    m_hat = m2 / (1.0 - ADAM_B1 ** ADAM_STEP)
    v_hat = v2 / (1.0 - ADAM_B2 ** ADAM_STEP)
    delta = -ADAM_LR * (m_hat / (jnp.sqrt(v_hat) + ADAM_EPS) + ADAM_WD * w)
    return delta, m2, v2


def _adam_big(w, g, m, v, name):
    rows, cols = w.shape
    tr = _row_tile(rows)

    def body(w_ref, g_ref, m_ref, v_ref, d_out, m_out, v_out):
        d, m2, v2 = _adam_math(w_ref[...], g_ref[...], m_ref[...], v_ref[...])
        d_out[...] = d
        m_out[...] = m2
        v_out[...] = v2

    spec = pl.BlockSpec((tr, cols), lambda i: (i, 0))
    return pl.pallas_call(
        body, name=name, grid=(rows // tr,), in_specs=[spec] * 4, out_specs=[spec] * 3,
        out_shape=[_big(w.shape, F32)] * 3,
        compiler_params=_cp(("arbitrary",), 48),
    )(*_hbm(w, g, m, v))


def _ada_grad_adam(c8t, dmod_sh, w, m, v):
    rows, cols = w.shape
    tr = _row_tile(rows) // 2

    def body(ct_ref, dm_ref, w_ref, m_ref, v_ref, g_out, d_out, m_out, v_out):
        g = None
        for b in range(N_DEV):
            term = ct_ref[:, b:b + 1] * dm_ref[b:b + 1, :]
            g = term if g is None else g + term
        d, m2, v2 = _adam_math(w_ref[...], g, m_ref[...], v_ref[...])
        g_out[...] = g
        d_out[...] = d
        m_out[...] = m2
        v_out[...] = v2

    spec = pl.BlockSpec((tr, cols), lambda i: (i, 0))
    return pl.pallas_call(
        body, name="ada_grad_adam", grid=(rows // tr,),
        in_specs=[pl.BlockSpec((tr, N_DEV), lambda i: (i, 0)), _full((N_DEV, cols)), spec, spec, spec],
        out_specs=[spec] * 4, out_shape=[_big(w.shape, F32)] * 4,
        compiler_params=_cp(("arbitrary",), 32),
    )(c8t, dmod_sh, *_hbm(w, m, v))


def _tok_tile(t):
    return min(TOK_TILE, t)


def _mix_in_fwd(chip1, x, modr, g_pre, b_in4, w_in_buf, w_out_buf):
    T = x.shape[0]
    tm = min(MIXIN_TILE, T)
    nt = T // tm
    nb = IN_COLS // N_CHIPS

    def body(ch_ref, x_ref, mod_ref, g_ref, b_ref, win_in, wout_in, p_ref, h_ref, win_ref, wout_ref,
             h_all, wblk, lsem, is_sem, ir_sem, os_sem, or_sem):
        k, i = pl.program_id(0), pl.program_id(1)
        chip = ch_ref[0]

        def load_block(blk):
            cp = pltpu.make_async_copy(win_ref.at[blk], wblk, lsem)
            cp.start()
            cp.wait()

        @pl.when((k == 0) & (i == 0))
        def _():
            _gather_start([win_ref], is_sem, ir_sem, relations=(0, 1))
            load_block(chip)

        for r in range(N_CHIPS - 1):
            @pl.when((k == r + 1) & (i == 0))
            def _(r=r):
                _gather_arrive([win_ref], r, is_sem, ir_sem)
                if r == 0:
                    _gather_start([win_ref], is_sem, ir_sem, relations=(2,))
                if r == 1:
                    _gather_start([wout_ref], os_sem, or_sem)
                load_block(chip ^ (r + 1))

        rows = pl.ds(pl.multiple_of(i * tm, tm), tm)

        @pl.when(k == 0)
        def _():
            xv = x_ref[...]
            rstd = lax.rsqrt(jnp.mean(xv * xv, axis=-1, keepdims=True) + RMS_EPS)
            h = (xv * rstd) * g_ref[...] * (1.0 + mod_ref[1:2, :]) + mod_ref[0:1, :]
            hb = h.astype(BF16)
            h_ref[...] = hb
            h_all[rows, :] = hb

        p_ref[...] = _dot(h_all[rows, :], wblk[...]) + b_ref[chip ^ k]

        @pl.when((k == N_CHIPS - 1) & (i == nt - 1))
        def _():
            _gather_sends_done([win_ref], is_sem, ir_sem)
            _gather_finish([wout_ref], os_sem, or_sem)

    hbm = pl.BlockSpec(memory_space=pl.ANY)
    first_pass = lambda k, i, ch: (jnp.where(k == 0, i, nt - 1), 0)
    gs = pltpu.PrefetchScalarGridSpec(
        num_scalar_prefetch=1, grid=(N_CHIPS, nt),
        in_specs=[pl.BlockSpec((tm, D_MODEL), first_pass), pl.BlockSpec((6, D_MODEL), lambda k, i, ch: (0, 0)),
                  pl.BlockSpec((1, D_MODEL), lambda k, i, ch: (0, 0)),
                  pl.BlockSpec((N_CHIPS, 1, nb), lambda k, i, ch: (0, 0, 0)), hbm, hbm],
        out_specs=[pl.BlockSpec((tm, nb), lambda k, i, ch: (i, ch[0] ^ k)), pl.BlockSpec((tm, D_MODEL), first_pass),
                   hbm, hbm],
        scratch_shapes=[pltpu.VMEM((T, D_MODEL), BF16), pltpu.VMEM((D_MODEL, nb), BF16), pltpu.SemaphoreType.DMA]
        + _gather_sems(1) + _gather_sems(1))
    return pl.pallas_call(
        body, name="mix_in_fwd", grid_spec=gs,
        out_shape=[_big((T, IN_COLS), F32), _big((T, D_MODEL), BF16), _big(w_in_buf.shape, BF16),
                   _big(w_out_buf.shape, BF16)],
        input_output_aliases={5: 2, 6: 3},
        compiler_params=_cp(("arbitrary", "arbitrary"), 48),
    )(chip1, _hbm(x), modr, g_pre, b_in4, _hbm(w_in_buf), _hbm(w_out_buf))


def _mixers_fwd(p, wdw, vecs, lbl, w_out, x, modr, g_post, g_ffn, gbufs):
    T = p.shape[0]
    tm = _tok_tile(T)
    nt = T // tm
    nch = tm // CHUNK
    ng = len(gbufs)
    n_in, n_out = 12, 7

    def body(*refs):
        (p_ref, wdw_ref, bdw_ref, gain_ref, bias_ref, gout_ref, lbl_ref, wout_ref, x_ref, mod_ref, gp_ref,
         gf_ref) = refs[:n_in]
        cat_ref, ys_ref, o_ref, st_ref, y_ref, x1_ref, h2_ref = refs[n_in + ng:n_in + ng + n_out]
        gout_bufs = refs[n_in + ng + n_out:n_in + 2 * ng + n_out]
        (ubuf, state, qt_s, kt_s, kh_s, v_s, egl_s, lower_s, gmat_s, gssem,
         grsem) = refs[n_in + 2 * ng + n_out:]
        i = pl.program_id(0)

        @pl.when(i == 0)
        def _():
            _ring_start(gout_bufs, gssem, grsem)
            lower_s[...], _, _ = _chunk_masks(tm)
            gmat_s[...] = _gn_matrix()
            state[...] = jnp.zeros(state.shape, F32)
            ubuf[0:HALO, :] = jnp.zeros((HALO, CONV_CH), F32)
            ubuf[HALO + tm:HALO + tm + SUB, :] = jnp.zeros((SUB, CONV_CH), F32)

        @pl.when(i > 0)
        def _():
            ubuf[0:HALO, :] = ubuf[tm:tm + HALO, :]

        ubuf[HALO:HALO + tm, :] = p_ref[:, 0:CONV_CH] * _sig(p_ref[:, CONV_CH:2 * CONV_CH])
        for r in range(tm // CONV_ROWS):
            rows = slice(r * CONV_ROWS, (r + 1) * CONV_ROWS)
            for lb_ in range(CONV_CH // LANE):
                lanes = slice(lb_ * LANE, (lb_ + 1) * LANE)
                ys_ref[rows, lanes] = bdw_ref[:, lanes] + _tap_conv(ubuf, wdw_ref, r * CONV_ROWS, CONV_FWD_TAPS, lanes)
        gmat = gmat_s[...]
        yv = ys_ref[...]
        d = yv - _gmean(yv, gmat)
        rs = lax.rsqrt(_gmean(d * d, gmat) + GN_EPS)
        z = d * rs * gain_ref[...] + bias_ref[...]
        cat_ref[:, 0:CONV_CH] = (z * _sig(z)).astype(BF16)

        lb, _ = _lower_bound(lbl_ref)
        o0 = 2 * CONV_CH
        pr = _hgrn_prep(p_ref[:, o0:o0 + HGRN_W], p_ref[:, o0 + HGRN_W:o0 + 2 * HGRN_W], lb, lower_s[...])
        qt_s[...] = pr["qt"].astype(BF16)
        kt_s[...] = pr["kt"].astype(BF16)
        kh_s[...] = pr["kh"].astype(BF16)
        v_s[...] = p_ref[:, o0 + 2 * HGRN_W:o0 + 3 * HGRN_W].astype(BF16)
        egl_s[...] = jnp.exp(pr["Gl"])
        tri = _tri()

        def chunk(ci, carry):
            r0 = pl.multiple_of(ci * CHUNK, CHUNK)
            rows = pl.ds(r0, CHUNK)
            for h in range(N_HEADS):
                ls = pl.ds(h * HEAD_D, HEAD_D)
                qc, kc, hc, vc = qt_s[rows, ls], kt_s[rows, ls], kh_s[rows, ls], v_s[rows, ls]
                s0 = state[h]
                s0b = s0.astype(BF16)
                st_ref[ci, h] = s0
                att =jnp.where(tri, _dot_nt(qc, kc), 0.0).astype(BF16)
                o_ref[rows, ls] = _dot(att, vc) + _dot_nt(qc, s0b)
                state[h] = s0 * egl_s[pl.ds(r0, 1), ls] + _dot_tn(vc, hc)
            return carry

        lax.fori_loop(0, nch, chunk, 0, unroll=min(CHUNK_UNROLL, nch))
        for h in range(N_HEADS):
            sl = slice(h * HEAD_D, (h + 1) * HEAD_D)
            oh = o_ref[:, sl]
            gh = p_ref[:, o0 + 3 * HGRN_W + h * HEAD_D:o0 + 3 * HGRN_W + (h + 1) * HEAD_D]
            rsh = lax.rsqrt(jnp.mean(oh * oh, axis=-1, keepdims=True) + RMS_EPS)
            hg = (oh * rsh) * gout_ref[:, sl] * (gh * _sig(gh))
            cat_ref[:, CONV_CH + h * HEAD_D:CONV_CH + (h + 1) * HEAD_D] = hg.astype(BF16)

        yv = _dot(cat_ref[...], wout_ref[...])
        y_ref[...] = yv
        rsy = lax.rsqrt(jnp.mean(yv * yv, axis=-1, keepdims=True) + RMS_EPS)
        x1 = x_ref[...] + mod_ref[2:3, :] * ((yv * rsy) * gp_ref[...])
        x1_ref[...] = x1
        rs1 = lax.rsqrt(jnp.mean(x1 * x1, axis=-1, keepdims=True) + RMS_EPS)
        h2 = (x1 * rs1) * gf_ref[...] * (1.0 + mod_ref[4:5, :]) + mod_ref[3:4, :]
        h2_ref[...] = h2.astype(BF16)

        @pl.when(i == min(nt - 1, nt // 2 + 1))
        def _():
            _ring_forward(gout_bufs, gssem, grsem)

        @pl.when(i == nt - 1)
        def _():
            _ring_finish(gout_bufs, gssem, grsem)

    tile = lambda cols: pl.BlockSpec((tm, cols), lambda i: (i, 0))
    hbm = pl.BlockSpec(memory_space=pl.ANY)
    outs = pl.pallas_call(
        body, name="mixers_fwd", grid=(nt,),
        in_specs=[tile(IN_COLS), _full((HALO, CONV_CH))] + [_full((1, CONV_CH))] * 4 + [_full((2, HGRN_W))]
        + [_full((D_MODEL, D_MODEL)), tile(D_MODEL), _full((6, D_MODEL)), _full((1, D_MODEL)), _full((1, D_MODEL))]
        + [hbm] * ng,
        out_specs=[tile(D_MODEL), tile(CONV_CH), tile(HGRN_W),
                   pl.BlockSpec((nch, N_HEADS, HEAD_D, HEAD_D), lambda i: (i, 0, 0, 0)),
                   tile(D_MODEL), tile(D_MODEL), tile(D_MODEL)] + [hbm] * ng,
        out_shape=[_big((T, D_MODEL), BF16), _big((T, CONV_CH), F32), _big((T, HGRN_W), F32),
                   _big((T // CHUNK, N_HEADS, HEAD_D, HEAD_D), F32), _big((T, D_MODEL), F32),
                   _big((T, D_MODEL), F32), _big((T, D_MODEL), BF16)] + [_big(b.shape, BF16) for b in gbufs],
        input_output_aliases={n_in + t: n_out + t for t in range(ng)},
        scratch_shapes=[pltpu.VMEM((tm + HALO + SUB, CONV_CH), F32), pltpu.VMEM((N_HEADS, HEAD_D, HEAD_D), F32),
                        pltpu.VMEM((tm, HGRN_W), BF16), pltpu.VMEM((tm, HGRN_W), BF16),
                        pltpu.VMEM((tm, HGRN_W), BF16), pltpu.VMEM((tm, HGRN_W), BF16),
                        pltpu.VMEM((tm, HGRN_W), F32), pltpu.VMEM((tm, tm), BF16),
                        pltpu.VMEM((CONV_CH, CONV_CH), BF16)] + _ring_sems(ng),
        compiler_params=_cp(("arbitrary",), 56),
    )(_hbm(p), wdw, *vecs, lbl, *_hbm(w_out, x), modr, g_post, g_ffn, *[_hbm(b) for b in gbufs])
    return outs[:n_out], outs[n_out:]


def _row_chains(rows, n=2):
    step = rows // n
    return [slice(k * step, (k + 1) * step) for k in range(n)]


def _ffn_blocks():
    return D_FF // FFN_BLOCK, (D_FF // N_CHIPS) // FFN_BLOCK


def _ffn_fwd(h2, w_up_g, w_down, x1, target, modr, g_post):
    T = h2.shape[0]
    tm = min(FFN_TILE, T)
    fb = FFN_BLOCK
    nj, per = _ffn_blocks()

    def body(h_ref, wu_ref, wd_ref, x1_ref, t_ref, mod_ref, g_ref, r_ref, dy2_ref, dx2_ref, st_ref, acc):
        i, j = pl.program_id(0), pl.program_id(1)

        @pl.when((i == 0) & (j == 0))
        def _():
            st_ref[...] = jnp.zeros(st_ref.shape, F32)

        @pl.when(j == 0)
        def _():
            acc[...] = jnp.zeros(acc.shape, F32)

        for rows in _row_chains(tm):
            ra = jnp.maximum(_dot(h_ref[rows, :], wu_ref[0]), 0.0)
            rb = (ra * ra).astype(BF16)
            r_ref[rows, :] = rb
            acc[rows, :] += _dot(rb, wd_ref[...])

        @pl.when(j == nj - 1)
        def _():
            y2 = acc[...]
            rs = lax.rsqrt(jnp.mean(y2 * y2, axis=-1, keepdims=True) + RMS_EPS)
            nh = y2 * rs
            gp = g_ref[...]
            err = x1_ref[...] + mod_ref[5:6, :] * (nh * gp) - t_ref[...]
            dx2 = err * (1.0 / D_MODEL)
            dx2_ref[...] = dx2
            st_ref[0:1, :] += _colsum(err * err)
            st_ref[1:2, :] += _colsum(dx2 * (nh * gp))
            dn = dx2 * mod_ref[5:6, :]
            st_ref[2:3, :] += _colsum(dn * nh)
            dy2_ref[...] = _rms_bwd(dn * gp, nh, rs).astype(BF16)

    tile = pl.BlockSpec((tm, D_MODEL), lambda i, j: (i, 0))
    return pl.pallas_call(
        body, name="ffn_fwd", grid=(T // tm, nj),
        in_specs=[tile, pl.BlockSpec((1, D_MODEL, fb), lambda i, j: (j // per, 0, j % per)),
                  pl.BlockSpec((fb, D_MODEL), lambda i, j: (j, 0)), tile, tile,
                  _full((6, D_MODEL)), _full((1, D_MODEL))],
        out_specs=[pl.BlockSpec((tm, fb), lambda i, j: (i, j)), tile, tile, _full((8, D_MODEL))],
        out_shape=[_big((T, D_FF), BF16), _big((T, D_MODEL), BF16), _big((T, D_MODEL), F32),
                   jax.ShapeDtypeStruct((8, D_MODEL), F32)],
        scratch_shapes=[pltpu.VMEM((tm, D_MODEL), F32)],
        compiler_params=_cp(("arbitrary", "arbitrary"), 56),
    )(*_hbm(h2, w_up_g, w_down, x1, target), modr, g_post)


def _rms_bwd(dxn, xn, rs):
    return rs * (dxn - xn * jnp.mean(dxn * xn, axis=-1, keepdims=True))


def _ffn_bwd(dy2, r, x1, dx2, w_up_g, w_down, modr, g_ffn):
    T = dx2.shape[0]
    tm = min(FFN_TILE, T)
    fb = FFN_BLOCK
    nj, per = _ffn_blocks()

    def body(dy2_ref, r_ref, x1_ref, dx2_ref, wu_ref, wd_ref, mod_ref, gf_ref, da_ref, dx1_ref, st_ref, dh_s):
        i, j = pl.program_id(0), pl.program_id(1)

        @pl.when((i == 0) & (j == 0))
        def _():
            st_ref[...] = jnp.zeros(st_ref.shape, F32)

        @pl.when(j == 0)
        def _():
            dh_s[...] = jnp.zeros(dh_s.shape, F32)

        for rows in _row_chains(tm):
            ra = jnp.sqrt(r_ref[rows, :].astype(F32))
            da = (_dot_nt(dy2_ref[rows, :], wd_ref[...]) * (2.0 * ra)).astype(BF16)
            da_ref[rows, :] = da
            dh_s[rows, :] += _dot_nt(da, wu_ref[0])

        @pl.when(j == nj - 1)
        def _():
            dh = dh_s[...]
            x1v = x1_ref[...]
            rs1 = lax.rsqrt(jnp.mean(x1v * x1v, axis=-1, keepdims=True) + RMS_EPS)
            xn = x1v * rs1
            st_ref[0:1, :] += _colsum(dh)
            st_ref[1:2, :] += _colsum(dh * (xn * gf_ref[...]))
            dsc = dh * (1.0 + mod_ref[4:5, :])
            st_ref[2:3, :] += _colsum(dsc * xn)
            dx1_ref[...] = dx2_ref[...] + _rms_bwd(dsc * gf_ref[...], xn, rs1)

    tile = pl.BlockSpec((tm, D_MODEL), lambda i, j: (i, 0))
    ftile = pl.BlockSpec((tm, fb), lambda i, j: (i, j))
    return pl.pallas_call(
        body, name="ffn_bwd", grid=(T // tm, nj),
        in_specs=[tile, ftile, tile, tile, pl.BlockSpec((1, D_MODEL, fb), lambda i, j: (j // per, 0, j % per)),
                  pl.BlockSpec((fb, D_MODEL), lambda i, j: (j, 0)), _full((6, D_MODEL)), _full((1, D_MODEL))],
        out_specs=[ftile, tile, _full((8, D_MODEL))],
        out_shape=[_big((T, D_FF), BF16), _big((T, D_MODEL), F32), jax.ShapeDtypeStruct((8, D_MODEL), F32)],
        scratch_shapes=[pltpu.VMEM((tm, D_MODEL), F32)],
        compiler_params=_cp(("arbitrary", "arbitrary"), 56),
    )(*_hbm(dy2, r, x1, dx2, w_up_g, w_down), modr, g_ffn)


def _mix_out_bwd(dx1, y, cat, w_out, modr, g_post, swap):
    T = dx1.shape[0]
    tm = _tok_tile(T)
    nt = T // tm
    ns = len(swap)

    def body(*refs):
        dx1_ref, y_ref, cat_ref, w_ref, mod_ref, gp_ref = refs[:6]
        s_ins = refs[6:6 + ns]
        dcat_ref, st_ref, gw_ref = refs[6 + ns:9 + ns]
        s_outs = refs[9 + ns:9 + 2 * ns]
        gacc, gsem, pssem, prsem = refs[9 + 2 * ns:]
        i = pl.program_id(0)

        @pl.when(i == 0)
        def _():
            for cp in _pair_copies(s_ins, s_outs, pssem, prsem):
                cp.start()
            st_ref[...] = jnp.zeros(st_ref.shape, F32)
            gacc[...] = jnp.zeros(gacc.shape, F32)

        dxv, yv = dx1_ref[...], y_ref[...]
        rs = lax.rsqrt(jnp.mean(yv * yv, axis=-1, keepdims=True) + RMS_EPS)
        nh = yv * rs
        st_ref[0:1, :] += _colsum(dxv * (nh * gp_ref[...]))
        dn = dxv * mod_ref[2:3, :]
        st_ref[1:2, :] += _colsum(dn * nh)
        dy = _rms_bwd(dn * gp_ref[...], nh, rs).astype(BF16)
        dcat_ref[...] = _dot_nt(dy, w_ref[...])
        for cols in _row_chains(D_MODEL):
            gacc[:, cols] += _dot_tn(cat_ref[...], dy[:, cols])

        @pl.when(i == nt - 1)
        def _():
            out = pltpu.make_async_copy(gacc, gw_ref, gsem)
            out.start()
            copies = _pair_copies(s_ins, s_outs, pssem, prsem)
            for cp in copies:
                cp.wait_recv()
            for cp in copies:
                cp.wait_send()
            out.wait()

    tile = pl.BlockSpec((tm, D_MODEL), lambda i: (i, 0))
    hbm = pl.BlockSpec(memory_space=pl.ANY)
    outs = pl.pallas_call(
        body, name="mix_out_bwd", grid=(nt,),
        in_specs=[tile, tile, tile, _full((D_MODEL, D_MODEL)), _full((6, D_MODEL)), _full((1, D_MODEL))]
        + [hbm] * ns,
        out_specs=[tile, _full((8, D_MODEL)), hbm] + [hbm] * ns,
        out_shape=[_big((T, D_MODEL), F32), jax.ShapeDtypeStruct((8, D_MODEL), F32), _big((D_MODEL, D_MODEL), F32)]
        + _pair_shapes(swap),
        scratch_shapes=[pltpu.VMEM((D_MODEL, D_MODEL), F32), pltpu.SemaphoreType.DMA] + _pair_sems(ns),
        compiler_params=_cp(("arbitrary",), 48),
    )(*_hbm(dx1, y, cat, w_out), modr, g_post, *[_hbm(g) for g in swap])
    return outs[:3], outs[3:]


def _mixers_bwd(p, dcat, ys, o, states, h1, wdw, vecs, lbl, pairs_b):
    T = p.shape[0]
    tm = min(MIXB_TILE, T)
    nt = T // tm
    nch = tm // CHUNK
    hpt = tm // HALO
    nx = len(pairs_b)
    nb = IN_COLS // N_CHIPS

    def body(*refs):
        (p_ref, ph_ref, dcat_ref, ys_ref, o_ref, st_ref, h1_ref, wdw_ref, bdw_ref, gain_ref, bias_ref, gout_ref,
         lbl_ref) = refs[:13]
        x_ins = refs[13:13 + nx]
        dp_ref, sb_ref, s5_ref, dw_ref = refs[13 + nx:17 + nx]
        x_outs = refs[17 + nx:17 + 2 * nx]
        gin_ref, ginb_ref = refs[17 + 2 * nx], refs[18 + 2 * nx]
        (ubuf, dybuf, carry, dstate, qt_s, kt_s, kh_s, v_s, do_s, egl_s, dqt_s, dkt_s, dkh_s, dv_s, dgl_s,
         dsh, dw8, dshift, lower_s, upper_s, same_s, gmat_s, gacc, gacc_b, dp_prev, h1_prev, gsem, gbsem, xssem,
         xrsem) = refs[19 + 2 * nx:]
        i = pl.program_id(0)
        tile_idx = nt - 1 - i

        @pl.when(i == 0)
        def _():
            for cp in _xchg_copies(x_ins, x_outs, xssem, xrsem):
                cp.start()
            gacc[...] = jnp.zeros(gacc.shape, F32)
            dstate[...] = jnp.zeros(dstate.shape, F32)
            carry[...] = jnp.zeros(carry.shape, F32)
            sb_ref[...] = jnp.zeros(sb_ref.shape, F32)
            s5_ref[...] = jnp.zeros(s5_ref.shape, F32)
            dw_ref[...] = jnp.zeros(dw_ref.shape, F32)
            dw8[...] = jnp.zeros(dw8.shape, F32)
            lower_s[...], upper_s[...], same_s[...] = _chunk_masks(tm)
            gmat_s[...] = _gn_matrix()
            dsh[0:SUB, :] = jnp.zeros((SUB, CONV_CH), F32)
            dsh[SUB + tm:2 * SUB + tm, :] = jnp.zeros((SUB, CONV_CH), F32)
            ubuf[HALO + tm:HALO + tm + SUB, :] = jnp.zeros((SUB, CONV_CH), F32)
            dp_prev[...] = jnp.zeros(dp_prev.shape, BF16)
            h1_prev[...] = jnp.zeros(h1_prev.shape, BF16)

        n_pieces = (tm // CONV_ROWS) * (CONV_CH // LANE)
        per_block = n_pieces // N_CHIPS
        prow = D_MODEL // per_block

        def w_in_grad_piece(k):
            j, part = k // per_block, k % per_block
            rows_k = slice(part * prow, (part + 1) * prow)
            gacc[j, rows_k, :] += _dot_tn(h1_prev[:, rows_k], dp_prev[:, j * nb:(j + 1) * nb])

        uh = ph_ref[:, 0:CONV_CH] * _sig(ph_ref[:, CONV_CH:2 * CONV_CH])
        ubuf[0:HALO, :] = jnp.where(tile_idx > 0, uh, 0.0)
        ubuf[HALO:HALO + tm, :] = p_ref[:, 0:CONV_CH] * _sig(p_ref[:, CONV_CH:2 * CONV_CH])
        gmat = gmat_s[...]
        gain = gain_ref[...]
        yv = ys_ref[...]
        d = yv - _gmean(yv, gmat)
        rs = lax.rsqrt(_gmean(d * d, gmat) + GN_EPS)
        yn = d * rs
        z = yn * gain + bias_ref[...]
        sz = _sig(z)
        dz = dcat_ref[:, 0:CONV_CH] * (sz * (1.0 + z * (1.0 - sz)))
        dyn = dz * gain
        dyc = rs * (dyn - _gmean(dyn, gmat) - yn * _gmean(dyn * yn, gmat))
        s5_ref[0:1, :] += _colsum(dyc)
        s5_ref[1:2, :] += _colsum(dz * yn)
        s5_ref[2:3, :] += _colsum(dz)
        dybuf[tm:tm + HALO, :] = carry[...]
        dybuf[0:tm, :] = dyc
        dsh[SUB:SUB + tm, :] = dyc
        carry[...] = dyc[0:HALO, :]
        for b in range(SUB):
            dshift[...] = dsh[SUB - b:2 * SUB - b + tm, :]
            for j, off in CONV_FWD_TAPS:
                if off % SUB == b:
                    prod = dshift[...] * ubuf[off - b:off - b + tm + SUB, :]
                    dw8[j] += jnp.sum(prod.reshape((tm + SUB) // SUB, SUB, CONV_CH), axis=0)
        for r in range(tm // CONV_ROWS):
            rows = slice(r * CONV_ROWS, (r + 1) * CONV_ROWS)
            for lb_ in range(CONV_CH // LANE):
                lanes = slice(lb_ * LANE, (lb_ + 1) * LANE)
                glanes = slice(CONV_CH + lb_ * LANE, CONV_CH + (lb_ + 1) * LANE)
                w_in_grad_piece(r * (CONV_CH // LANE) + lb_)
                acc = _tap_conv(dybuf, wdw_ref, r * CONV_ROWS, CONV_BWD_TAPS, lanes)
                val = p_ref[rows, lanes]
                sg = _sig(p_ref[rows, glanes])
                dval = acc * sg
                dgate = acc * val * (sg * (1.0 - sg))
                dp_ref[rows, lanes] = dval.astype(BF16)
                dp_ref[rows, glanes] = dgate.astype(BF16)
                sb_ref[0:1, lanes] += _colsum(dval)
                sb_ref[0:1, glanes] += _colsum(dgate)

        o0 = 2 * CONV_CH
        for h in range(N_HEADS):
            sl = slice(h * HEAD_D, (h + 1) * HEAD_D)
            gsl = slice(o0 + 3 * HGRN_W + h * HEAD_D, o0 + 3 * HGRN_W + (h + 1) * HEAD_D)
            oh = o_ref[:, sl]
            gh = p_ref[:, gsl]
            dh = dcat_ref[:, CONV_CH + h * HEAD_D:CONV_CH + (h + 1) * HEAD_D]
            gout = gout_ref[:, sl]
            rsh = lax.rsqrt(jnp.mean(oh * oh, axis=-1, keepdims=True) + RMS_EPS)
            on = oh * rsh
            sgg = _sig(gh)
            dgh = dh * (on * gout) * (sgg * (1.0 + gh * (1.0 - sgg)))
            dm = dh * (gh * sgg)
            s5_ref[3:4, sl] += _colsum(dm * on)
            do_s[:, sl] = _rms_bwd(dm * gout, on, rsh).astype(BF16)
            dp_ref[:, gsl] = dgh.astype(BF16)
            sb_ref[2:3, CONV_CH + h * HEAD_D:CONV_CH + (h + 1) * HEAD_D] += _colsum(dgh)

        lb, _ = _lower_bound(lbl_ref)
        pq = p_ref[:, o0:o0 + HGRN_W]
        pr = _hgrn_prep(pq, p_ref[:, o0 + HGRN_W:o0 + 2 * HGRN_W], lb, lower_s[...])
        qt_s[...] = pr["qt"].astype(BF16)
        kt_s[...] = pr["kt"].astype(BF16)
        kh_s[...] = pr["kh"].astype(BF16)
        v_s[...] = p_ref[:, o0 + 2 * HGRN_W:o0 + 3 * HGRN_W].astype(BF16)
        egl_s[...] = jnp.exp(pr["Gl"])
        tri = _tri()

        def chunk(it, c_):
            ci = nch - 1 - it
            r0 = pl.multiple_of(ci * CHUNK, CHUNK)
            rows = pl.ds(r0, CHUNK)
            for h in range(N_HEADS):
                ls = pl.ds(h * HEAD_D, HEAD_D)
                qc, kc, hc, vc = qt_s[rows, ls], kt_s[rows, ls], kh_s[rows, ls], v_s[rows, ls]
                dob = do_s[rows, ls]
                s0 = st_ref[ci, h]
                s0b = s0.astype(BF16)
                ds1 = dstate[h]
                ds1b = ds1.astype(BF16)
                egl = egl_s[pl.ds(r0, 1), ls]
                att = jnp.where(tri, _dot_nt(qc, kc), 0.0).astype(BF16)
                datt = jnp.where(tri, _dot_nt(dob, vc), 0.0).astype(BF16)
                dv_s[rows, ls] = _dot_tn(att, dob) + _dot_nt(hc, ds1b)
                dqt_s[rows, ls] = _dot(datt, kc) + _dot(dob, s0b)
                dkt_s[rows, ls] = _dot_tn(datt, qc)
                dkh_s[rows, ls] = _dot(vc, ds1b)
                dgl = egl * _colsum(ds1 * s0)
                dgl_s[rows, ls] = jnp.broadcast_to(dgl, (CHUNK, HEAD_D))
                dstate[h] = ds1 * egl + _dot_tn(dob, qc)
            return c_

        lax.fori_loop(0, nch, chunk, 0, unroll=min(CHUNK_UNROLL, nch))
        dqt, dkt, dkh = dqt_s[...], dkt_s[...], dkh_s[...]
        dk = dkt * pr["enG"] + dkh * pr["eGlG"]
        khk = dkh * kh_s[...].astype(F32)
        dG = dqt * qt_s[...].astype(F32) - dkt * kt_s[...].astype(F32) - khk
        dlogf = _mm3(upper_s[...], dG) + _mm3(same_s[...], khk) + dgl_s[...]
        df = dlogf / pr["f"] - dk
        sf, sq = pr["sf"], pr["sq"]
        s5_ref[4:5, :] += _colsum(df * (1.0 - sf))
        dfl = df * (1.0 - lb) * (sf * (1.0 - sf))
        dq = (dqt * pr["eG"]) * (sq * (1.0 + pq * (1.0 - sq)))
        dvv = dv_s[...]
        dp_ref[:, o0:o0 + HGRN_W] = dq.astype(BF16)
        dp_ref[:, o0 + HGRN_W:o0 + 2 * HGRN_W] = dfl.astype(BF16)
        dp_ref[:, o0 + 2 * HGRN_W:o0 + 3 * HGRN_W] = dvv.astype(BF16)
        sb_ref[1:2, 0:HGRN_W] += _colsum(dq)
        sb_ref[1:2, HGRN_W:2 * HGRN_W] += _colsum(dfl)
        sb_ref[2:3, 0:HGRN_W] += _colsum(dvv)

        dp_prev[...] = dp_ref[...]
        h1_prev[...] = h1_ref[...]

        @pl.when(i == nt - 1)
        def _():
            for k in range(n_pieces):
                w_in_grad_piece(k)
            out = pltpu.make_async_copy(gacc, gin_ref, gsem)
            out.start()
            for j in range(N_CHIPS):
                gacc_b[j] = gacc[j].astype(BF16)
            out_b = pltpu.make_async_copy(gacc_b, ginb_ref, gbsem)
            out_b.start()
            for j in range(CONV_K):
                dw_ref[j:j + 1, :] = _colsum(dw8[j])
            copies = _xchg_copies(x_ins, x_outs, xssem, xrsem)
            for cp in copies:
                cp.wait_recv()
            for cp in copies:
                cp.wait_send()
            out.wait()
            out_b.wait()

    rev = lambda cols: pl.BlockSpec((tm, cols), lambda i: (nt - 1 - i, 0))
    halo = pl.BlockSpec((HALO, 2 * CONV_CH), lambda i: (jnp.maximum((nt - 1 - i) * hpt - 1, 0), 0))
    wide = lambda n: pltpu.VMEM((tm, HGRN_W), n)
    hbm = pl.BlockSpec(memory_space=pl.ANY)
    outs = pl.pallas_call(
        body, name="mixers_bwd", grid=(nt,),
        in_specs=[rev(IN_COLS), halo, rev(D_MODEL), rev(CONV_CH), rev(HGRN_W),
                  pl.BlockSpec((nch, N_HEADS, HEAD_D, HEAD_D), lambda i: (nt - 1 - i, 0, 0, 0)), rev(D_MODEL),
                  _full((HALO, CONV_CH))] + [_full((1, CONV_CH))] * 4 + [_full((2, HGRN_W))] + [hbm] * nx,
        out_specs=[rev(IN_COLS), _full((8, D_MODEL)), _full((8, CONV_CH)), _full((HALO, CONV_CH))]
        + [hbm] * (nx + 2),
        out_shape=[_big((T, IN_COLS), BF16), jax.ShapeDtypeStruct((8, D_MODEL), F32),
                   jax.ShapeDtypeStruct((8, CONV_CH), F32), jax.ShapeDtypeStruct((HALO, CONV_CH), F32)]
        + [_big(pb.shape, BF16) for pb in pairs_b]
        + [_big((N_CHIPS, D_MODEL, nb), F32), _big((N_CHIPS, D_MODEL, nb), BF16)],
        scratch_shapes=[pltpu.VMEM((tm + HALO + SUB, CONV_CH), F32), pltpu.VMEM((tm + HALO, CONV_CH), F32),
                        pltpu.VMEM((HALO, CONV_CH), F32), pltpu.VMEM((N_HEADS, HEAD_D, HEAD_D), F32),
                        wide(BF16), wide(BF16), wide(BF16), wide(BF16), wide(BF16),
                        wide(F32), wide(F32), wide(F32), wide(F32), wide(F32), wide(F32),
                        pltpu.VMEM((tm + 2 * SUB, CONV_CH), F32), pltpu.VMEM((HALO, SUB, CONV_CH), F32),
                        pltpu.VMEM((tm + SUB, CONV_CH), F32), pltpu.VMEM((tm, tm), BF16), pltpu.VMEM((tm, tm), BF16),
                        pltpu.VMEM((tm, tm), BF16), pltpu.VMEM((CONV_CH, CONV_CH), BF16),
                        pltpu.VMEM((N_CHIPS, D_MODEL, nb), F32), pltpu.VMEM((N_CHIPS, D_MODEL, nb), BF16),
                        pltpu.VMEM((tm, IN_COLS), BF16), pltpu.VMEM((tm, D_MODEL), BF16),
                        pltpu.SemaphoreType.DMA, pltpu.SemaphoreType.DMA]
        + _xchg_sems(nx),
        compiler_params=_cp(("arbitrary",), 56),
    )(*_hbm(p, p, dcat, ys, o, states, h1), wdw, *vecs, lbl, *[_hbm(pb) for pb in pairs_b])
    return outs[:4], outs[4:4 + nx], (outs[4 + nx], outs[5 + nx])


def _mix_in_bwd(dp, w_in_g, x, dx1, modr, g_pre, pairs_b):
    T = x.shape[0]
    tm = _tok_tile(T)
    nt = T // tm
    nb = IN_COLS // N_CHIPS
    nx = len(pairs_b)

    def body(*refs):
        dp_ref, w_ref, x_ref, dx1_ref, mod_ref, g_ref = refs[:6]
        x_ins = refs[6:6 + nx]
        gx_ref, st_ref = refs[6 + nx:8 + nx]
        x_outs = refs[8 + nx:8 + 2 * nx]
        xssem, xrsem = refs[8 + 2 * nx:]
        i = pl.program_id(0)

        @pl.when(i == 0)
        def _():
            for cp in _xchg_copies(x_ins, x_outs, xssem, xrsem):
                cp.start()
            st_ref[...] = jnp.zeros(st_ref.shape, F32)

        dh = None
        for j in range(N_CHIPS):
            part = _dot_nt(dp_ref[:, j * nb:(j + 1) * nb], w_ref[j])
            dh = part if dh is None else dh + part
        xv = x_ref[...]
        rs = lax.rsqrt(jnp.mean(xv * xv, axis=-1, keepdims=True) + RMS_EPS)
        xn = xv * rs
        st_ref[0:1, :] += _colsum(dh)
        st_ref[1:2, :] += _colsum(dh * (xn * g_ref[...]))
        dsc = dh * (1.0 + mod_ref[1:2, :])
        st_ref[2:3, :] += _colsum(dsc * xn)
        gx_ref[...] = dx1_ref[...] + _rms_bwd(dsc * g_ref[...], xn, rs)

        @pl.when(i == nt - 1)
        def _():
            copies = _xchg_copies(x_ins, x_outs, xssem, xrsem)
            for cp in copies:
                cp.wait_recv()
            for cp in copies:
                cp.wait_send()

    tile = pl.BlockSpec((tm, D_MODEL), lambda i: (i, 0))
    hbm = pl.BlockSpec(memory_space=pl.ANY)
    outs = pl.pallas_call(
        body, name="mix_in_bwd", grid=(nt,),
        in_specs=[pl.BlockSpec((tm, IN_COLS), lambda i: (i, 0)), _full((N_CHIPS, D_MODEL, nb)), tile, tile,
                  _full((6, D_MODEL)), _full((1, D_MODEL))] + [hbm] * nx,
        out_specs=[tile, _full((8, D_MODEL))] + [hbm] * nx,
        out_shape=[_big((T, D_MODEL), F32), jax.ShapeDtypeStruct((8, D_MODEL), F32)]
        + [_big(pb.shape, BF16) for pb in pairs_b],
        scratch_shapes=_xchg_sems(nx),
        compiler_params=_cp(("arbitrary",), 48),
    )(*_hbm(dp, w_in_g, x, dx1), modr, g_pre, *[_hbm(pb) for pb in pairs_b])
    return outs[:2], outs[2:]


def _weight_grad(a, b, a_blocked, b_blocked, name):
    T = a.shape[0]
    tt = min(GRAD_TILE, T)
    nt = T // tt
    ka = a.shape[1] // N_CHIPS if a_blocked else a.shape[1]
    nb = b.shape[1] // N_CHIPS if b_blocked else b.shape[1]

    def body(a_ref, b_ref, o_ref, ob_ref):
        t = pl.program_id(1)

        @pl.when(t == 0)
        def _():
            o_ref[...] = jnp.zeros(o_ref.shape, F32)

        for cols in _row_chains(nb):
            o_ref[0, :, cols] += _dot_tn(a_ref[...], b_ref[:, cols])

        @pl.when(t == nt - 1)
        def _():
            ob_ref[0] = o_ref[0].astype(BF16)

    blk = pl.BlockSpec((1, ka, nb), lambda j, t: (j, 0, 0))
    return pl.pallas_call(
        body, name=name, grid=(N_CHIPS, nt),
        in_specs=[pl.BlockSpec((tt, ka), (lambda j, t: (t, j)) if a_blocked else (lambda j, t: (t, 0))),
                  pl.BlockSpec((tt, nb), (lambda j, t: (t, j)) if b_blocked else (lambda j, t: (t, 0)))],
        out_specs=[blk, blk],
        out_shape=[_big((N_CHIPS, ka, nb), F32), _big((N_CHIPS, ka, nb), BF16)],
        compiler_params=_cp(("arbitrary", "arbitrary"), 48),
    )(*_hbm(a, b))


R_LOSS = 0
R_FFN = 8
R_OUT = 16
R_IN = 24
R_BIN = 32
R_512 = 40
R_DW = 48
N_STAT_ROWS = 80
MOD_ROWS = (R_IN + 0, R_IN + 1, R_OUT + 0, R_FFN + 0, R_FFN + 1, R_LOSS + 1)


def _small_update(gath, params):
    names = ["b_ada", "lb_logits", "g_pre_mix", "b_in", "b_dw", "gn_gain", "gn_bias", "g_hgrn_out", "g_post_mix",
             "g_pre_ffn", "g_post_ffn"]
    flat = []
    for n in names:
        flat += list(params[n])
    n_in = 1 + len(flat)

    def body(*refs):
        g_ref = refs[0]
        prm = {n: refs[1 + 3 * k:4 + 3 * k] for k, n in enumerate(names)}
        outs = refs[n_in:]
        loss_ref, dmod_ref, dwdw_ref = outs[0], outs[1], outs[2]
        res = {n: outs[3 + 4 * k:7 + 4 * k] for k, n in enumerate(names)}
        red = g_ref[0]
        for dev in range(1, N_DEV):
            red = red + g_ref[dev]
        loss_ref[...] = jnp.broadcast_to(
            (0.5 / D_MODEL) * jnp.sum(red[R_LOSS:R_LOSS + 1, :], axis=-1, keepdims=True), loss_ref.shape)
        for dev in range(N_DEV):
            for k, r in enumerate(MOD_ROWS):
                dmod_ref[dev:dev + 1, k * D_MODEL:(k + 1) * D_MODEL] = g_ref[dev, r:r + 1, :]
        dwdw_ref[...] = red[R_DW:R_DW + HALO, 0:CONV_CH]

        def finish(name, pieces):
            w_ref, m_ref, v_ref = prm[name]
            g_out, d_out, m_out, v_out = res[name]
            for rsl, lsl, g in pieces:
                d, m2, v2 = _adam_math(w_ref[rsl, lsl], g, m_ref[rsl, lsl], v_ref[rsl, lsl])
                g_out[rsl, lsl] = g
                d_out[rsl, lsl] = d
                m_out[rsl, lsl] = m2
                v_out[rsl, lsl] = v2

        one = slice(0, 1)
        row = lambda r: red[r:r + 1, :]
        half = lambda r: red[r:r + 1, 0:CONV_CH]
        finish("b_ada", [(one, slice(k * D_MODEL, (k + 1) * D_MODEL), row(r)) for k, r in enumerate(MOD_ROWS)])
        finish("b_in", [(one, slice(k * D_MODEL, (k + 1) * D_MODEL), row(R_BIN + k)) for k in range(3)])
        finish("g_pre_mix", [(one, slice(None), row(R_IN + 2))])
        finish("g_post_mix", [(one, slice(None), row(R_OUT + 1))])
        finish("g_pre_ffn", [(one, slice(None), row(R_FFN + 2))])
        finish("g_post_ffn", [(one, slice(None), row(R_LOSS + 2))])
        finish("b_dw", [(one, slice(None), half(R_512 + 0))])
        finish("gn_gain", [(one, slice(None), half(R_512 + 1))])
        finish("gn_bias", [(one, slice(None), half(R_512 + 2))])
        finish("g_hgrn_out", [(one, slice(None), half(R_512 + 3))])
        s0, s1 = _lower_bound(prm["lb_logits"][0])
        dlb = half(R_512 + 4)
        finish("lb_logits", [(slice(0, 1), slice(None), dlb * s0 * (1.0 - s0)),
                             (slice(1, 2), slice(None), -dlb * s0 * s1)])

    vm = pl.BlockSpec(memory_space=pltpu.VMEM)
    out_shape = [jax.ShapeDtypeStruct((8, 128), F32), jax.ShapeDtypeStruct((N_DEV, 6 * D_MODEL), F32),
                 jax.ShapeDtypeStruct((HALO, CONV_CH), F32)]
    for n in names:
        out_shape += [jax.ShapeDtypeStruct(params[n][0].shape, F32)] * 4
    outs = pl.pallas_call(
        body, name="small_update", out_shape=out_shape,
        in_specs=[vm] * n_in, out_specs=[vm] * len(out_shape),
        compiler_params=_cp(None, 32),
    )(gath, *flat)
    return outs[0], outs[1], outs[2], {n: outs[3 + 4 * k:7 + 4 * k] for k, n in enumerate(names)}


def _wdw_adam(w, g, m, v):
    def body(w_ref, g_ref, m_ref, v_ref, d_out, m_out, v_out):
        d, m2, v2 = _adam_math(w_ref[...], g_ref[...], m_ref[...], v_ref[...])
        d_out[...] = d
        m_out[...] = m2
        v_out[...] = v2

    vm = pl.BlockSpec(memory_space=pltpu.VMEM)
    return pl.pallas_call(
        body, name="wdw_adam", out_shape=[jax.ShapeDtypeStruct(w.shape, F32)] * 3,
        in_specs=[vm] * 4, out_specs=[vm] * 3, compiler_params=_cp(None, 16),
    )(w, g, m, v)


def kernel(x, c, w_ada, b_ada, lb_logits, g_pre_mix, w_in, b_in, w_dw, b_dw, gn_gain, gn_bias, g_hgrn_out, w_out, g_post_mix, g_pre_ffn, w_up, w_down, g_post_ffn, loss_target, m_w_ada, m_b_ada, m_lb_logits, m_g_pre_mix, m_w_in, m_b_in, m_w_dw, m_b_dw, m_gn_gain, m_gn_bias, m_g_hgrn_out, m_w_out, m_g_post_mix, m_g_pre_ffn, m_w_up, m_w_down, m_g_post_ffn, v_w_ada, v_b_ada, v_lb_logits, v_g_pre_mix, v_w_in, v_b_in, v_w_dw, v_b_dw, v_gn_gain, v_gn_bias, v_g_hgrn_out, v_w_out, v_g_post_mix, v_g_pre_ffn, v_w_up, v_w_down, v_g_post_ffn):
    ax, ay, ac = lax.axis_index("x"), lax.axis_index("y"), lax.axis_index("c")
    chip = 2 * ax + ay
    T = x.shape[1]
    xs, tgt = x[0], loss_target[0]
    ada_cols = w_ada.shape[2]

    b_sh = lax.dynamic_slice_in_dim(b_ada, chip * ada_cols, ada_cols, axis=1)
    wdw_pad = jnp.pad(w_dw[0], ((0, HALO - CONV_K), (0, 0)))
    chip1 = jnp.reshape(chip, (1,)).astype(jnp.int32)
    place = jnp.stack([ac, chip]).astype(jnp.int32)
    _, c8, modg, wdwg = _ada_exchange(c, w_ada[0], b_sh, wdw_pad)
    modr = modg.reshape(6, D_MODEL)
    wdw_all = jnp.transpose(wdwg, (1, 0, 2)).reshape(HALO, CONV_CH)
    bufs = {t: _cast_own(chip1, w[0], "cast_" + t)
            for w, t in ((w_in, "w_in"), (w_out, "w_out"), (w_up, "w_up"), (w_down, "w_down"))}
    vec = (b_dw, gn_gain, gn_bias, g_hgrn_out)

    p, h1, w_in_g, w_out_g = _mix_in_fwd(chip1, xs, modr, g_pre_mix, b_in.reshape(N_CHIPS, 1, IN_COLS // N_CHIPS),
                                          bufs["w_in"], bufs["w_out"])
    w_out_f = w_out_g.reshape(D_MODEL, D_MODEL)
    (cat, ys, o, states, y, x1, h2), (w_up_g, w_down_g) = _mixers_fwd(
        p, wdw_all, vec, lb_logits, w_out_f, xs, modr, g_post_mix, g_pre_ffn, [bufs["w_up"], bufs["w_down"]])
    w_down_f = w_down_g.reshape(D_FF, D_MODEL)
    r, dy2, dx2, st_loss = _ffn_fwd(h2, w_up_g, w_down_f, x1, tgt, modr, g_post_ffn)

    def pair_sums(grads, got, tags):
        return [_pair_sum(place, g, o_, "pair_sum_" + t) for (g, _), o_, t in zip(grads, got, tags)]

    da, dx1, st_ffn = _ffn_bwd(dy2, r, x1, dx2, w_up_g, w_down_f, modr, g_pre_ffn)
    g_up = _weight_grad(h2, da, False, True, "grad_w_up")
    g_down = _weight_grad(r, dy2, True, False, "grad_w_down")
    (dcat, st_out, g_out), got_ud = _mix_out_bwd(dx1, y, cat, w_out_f, modr, g_post_mix, [g_up[1], g_down[1]])
    g_out = g_out.reshape(N_CHIPS, D_MODEL // N_CHIPS, D_MODEL)
    got_o = _pair_swap([g_out], "pair_swap_w_out")
    early = pair_sums([(g_out, None), g_up, g_down], list(got_o) + list(got_ud), ["w_out", "w_up", "w_down"])
    (dp, st_bin, st_512, dwdw), got_early, g_in = _mixers_bwd(p, dcat, ys, o, states, h1, wdw_all, vec, lb_logits,
                                                              [pb for _, pb in early])
    late = pair_sums([g_in], _pair_swap([g_in[1]], "pair_swap_w_in"), ["w_in"])
    (grad_x, st_in), got_late = _mix_in_bwd(dp, w_in_g, xs, dx1, modr, g_pre_mix, [late[0][1]])
    fulls = [_chip_sum(place, pf, gb, "chip_sum_" + t)
             for (pf, _), gb, t in zip(late + early, list(got_late) + list(got_early), ["w_in", "w_out", "w_up", "w_down"])]

    pad_lanes = lambda s: jnp.pad(s, ((0, 0), (0, D_MODEL - s.shape[1])))
    stats = jnp.concatenate([st_loss, st_ffn, st_out, st_in, st_bin, pad_lanes(st_512), pad_lanes(dwdw)], axis=0)
    (g_w_in, g_w_out, g_w_up, g_w_down), gath = _final_exchange(fulls, stats)
    small = {"b_ada": (b_ada, m_b_ada, v_b_ada), "lb_logits": (lb_logits, m_lb_logits, v_lb_logits),
             "g_pre_mix": (g_pre_mix, m_g_pre_mix, v_g_pre_mix), "b_in": (b_in, m_b_in, v_b_in),
             "b_dw": (b_dw, m_b_dw, v_b_dw), "gn_gain": (gn_gain, m_gn_gain, v_gn_gain),
             "gn_bias": (gn_bias, m_gn_bias, v_gn_bias), "g_hgrn_out": (g_hgrn_out, m_g_hgrn_out, v_g_hgrn_out),
             "g_post_mix": (g_post_mix, m_g_post_mix, v_g_post_mix), "g_pre_ffn": (g_pre_ffn, m_g_pre_ffn, v_g_pre_ffn),
             "g_post_ffn": (g_post_ffn, m_g_post_ffn, v_g_post_ffn)}
    loss_t, dmod_all, dwdw_sum, sres = _small_update(gath, small)
    loss = loss_t[0, 0]

    res = dict(sres)
    dmod_sh = lax.dynamic_slice_in_dim(dmod_all, chip * ada_cols, ada_cols, axis=1)
    res["w_ada"] = [t[None] for t in _ada_grad_adam(jnp.transpose(c8), dmod_sh, w_ada[0], m_w_ada[0], v_w_ada[0])]
    g_wdw = lax.dynamic_slice_in_dim(dwdw_sum, chip * HEAD_D, HEAD_D, axis=1)[:CONV_K][None]
    res["w_dw"] = [g_wdw] + list(_wdw_adam(w_dw, g_wdw, m_w_dw, v_w_dw))
    for name, g, w, m, v in (("w_in", g_w_in, w_in, m_w_in, v_w_in), ("w_out", g_w_out, w_out, m_w_out, v_w_out),
                             ("w_up", g_w_up, w_up, m_w_up, v_w_up), ("w_down", g_w_down, w_down, m_w_down, v_w_down)):
        d, m2, v2 = _adam_big(w[0], g, m[0], v[0], "adam_" + name)
        res[name] = [g[None], d[None], m2[None], v2[None]]

    order = ["w_ada", "b_ada", "lb_logits", "g_pre_mix", "w_in", "b_in", "w_dw", "b_dw", "gn_gain", "gn_bias",
             "g_hgrn_out", "w_out", "g_post_mix", "g_pre_ffn", "w_up", "w_down", "g_post_ffn"]
    out = [loss, grad_x[None]]
    for k in range(4):
        out += [res[n][k] for n in order]
    return tuple(out)
```

```python
import jax
import jax.numpy as jnp
from jax import lax
from jax.experimental import pallas as pl
from jax.experimental.pallas import tpu as pltpu

F32, BF16 = jnp.float32, jnp.bfloat16
D_MODEL = 1024
CONV_CH = 512
HGRN_W = 512
N_HEADS = 4
HEAD_D = 128
CONV_K = 31
GN_GROUP = 64
GN_SHIFT = 6
IN_COLS = 3072
D_FF = 4096
CHUNK = 64
CHUNK_SHIFT = 6
N_CHIPS = 4
N_DEV = 8
RMS_EPS = 1e-6
GN_EPS = 1e-5
ADAM_LR, ADAM_B1, ADAM_B2, ADAM_EPS, ADAM_WD, ADAM_STEP = 0.001, 0.9, 0.999, 1e-08, 0.01, 10
TOK_TILE = 512
MIXIN_TILE = 1024
MIXB_TILE = 256
FFN_TILE = 1024
FFN_BLOCK = 512
GRAD_TILE = 2048
HALO = 32
SUB = 8
LANE = 128
CONV_ROWS = 128
CHUNK_UNROLL = 8
MIB = 1 << 20
MESH = pl.DeviceIdType.MESH
OTHER_CHIPS = ((0, 1), (1, 0), (1, 1))


def _cp(sem=None, vmem_mib=48):
    return pltpu.CompilerParams(dimension_semantics=sem, vmem_limit_bytes=vmem_mib * MIB)


def _dot(a, b):
    return jnp.dot(a, b, preferred_element_type=F32)


def _dot_nt(a, b):
    return lax.dot_general(a, b, (((1,), (1,)), ((), ())), preferred_element_type=F32)


def _dot_tn(a, b):
    return lax.dot_general(a, b, (((0,), (0,)), ((), ())), preferred_element_type=F32)


def _sig(v):
    return 0.5 * jnp.tanh(0.5 * v) + 0.5


def _colsum(v):
    return jnp.sum(v, axis=0, keepdims=True)


def _flip(v, b):
    return 1 - v if b else v


def _rcopy(src, dst, ssem, rsem, dev):
    return pltpu.make_async_remote_copy(src_ref=src, dst_ref=dst, send_sem=ssem, recv_sem=rsem,
                                        device_id=dev, device_id_type=MESH)


def _place():
    return lax.axis_index("x"), lax.axis_index("y"), lax.axis_index("c")


def _full(shape):
    return pl.BlockSpec(shape, lambda *_: (0,) * len(shape))


def _big(shape, dtype):
    return pltpu.HBM(shape, dtype)


def _hbm(*arrays):
    out = [pltpu.with_memory_space_constraint(a, pltpu.HBM) for a in arrays]
    return out[0] if len(out) == 1 else out


def _split2(v):
    hi = v.astype(BF16)
    lo = (v - hi.astype(F32)).astype(BF16)
    return hi, lo


def _split3(v):
    h1 = v.astype(BF16)
    r1 = v - h1.astype(F32)
    h2 = r1.astype(BF16)
    h3 = (r1 - h2.astype(F32)).astype(BF16)
    return h1, h2, h3


def _mm3(mat, v):
    h1, h2, h3 = _split3(v)
    return _dot(mat, h1) + _dot(mat, h2) + _dot(mat, h3)


def _gn_matrix():
    r = lax.broadcasted_iota(jnp.int32, (CONV_CH, CONV_CH), 0) >> GN_SHIFT
    c = lax.broadcasted_iota(jnp.int32, (CONV_CH, CONV_CH), 1) >> GN_SHIFT
    return jnp.where(r == c, 1.0 / GN_GROUP, 0.0).astype(BF16)


def _gmean(v, gmat):
    hi, lo = _split2(v)
    return _dot(hi, gmat) + _dot(lo, gmat)


def _chunk_masks(tm):
    r = lax.broadcasted_iota(jnp.int32, (tm, tm), 0)
    c = lax.broadcasted_iota(jnp.int32, (tm, tm), 1)
    same = (r >> CHUNK_SHIFT) == (c >> CHUNK_SHIFT)
    one = lambda m: jnp.where(m, 1.0, 0.0).astype(BF16)
    return one(same & (c <= r)), one(same & (c >= r)), one(same)


def _tri():
    return lax.broadcasted_iota(jnp.int32, (CHUNK, CHUNK), 0) >= lax.broadcasted_iota(jnp.int32, (CHUNK, CHUNK), 1)


def _lower_bound(lbl_ref):
    l0, l1 = lbl_ref[0:1, :], lbl_ref[1:2, :]
    mx = jnp.maximum(l0, l1)
    e0, e1 = jnp.exp(l0 - mx), jnp.exp(l1 - mx)
    return e0 / (e0 + e1), e1 / (e0 + e1)


CONV_FWD_TAPS = tuple((j, HALO - (CONV_K - 1) + j) for j in range(CONV_K))
CONV_BWD_TAPS = tuple((j, (CONV_K - 1) - j) for j in range(CONV_K))


def _tap_conv(src_ref, w_ref, row0, taps, lanes):
    acc = None
    for b in range(SUB):
        pb = None
        for j, off in taps:
            if off % SUB == b:
                lo = row0 + off - b
                term = w_ref[j:j + 1, lanes] * src_ref[lo:lo + CONV_ROWS + SUB, lanes]
                pb = term if pb is None else pb + term
        if pb is not None:
            sh = pb[b:b + CONV_ROWS, :]
            acc = sh if acc is None else acc + sh
    return acc


def _hgrn_prep(pq, pf, lb, lower):
    sq = _sig(pq)
    qf = pq * sq
    sf = _sig(pf)
    f = lb + (1.0 - lb) * sf
    logf = jnp.log(f)
    k = 1.0 - f
    G = _mm3(lower, logf)
    rows, cols = G.shape
    g3 = G.reshape(rows // CHUNK, CHUNK, cols)
    Gl = jnp.broadcast_to(g3[:, CHUNK - 1:CHUNK, :], g3.shape).reshape(rows, cols)
    eG, enG, eGlG = jnp.exp(G), jnp.exp(-G), jnp.exp(Gl - G)
    return dict(sq=sq, sf=sf, f=f, Gl=Gl, eG=eG, enG=enG, eGlG=eGlG, qt=qf * eG, kt=k * enG, kh=k * eGlG)


def _ada_exchange(c_row, w_ada, b_sh, wdw_pad):
    ncol = w_ada.shape[1]

    def body(c_ref, w_ref, b_ref, wdw_ref, call_ref, c8_ref, modg_ref, wdwg_ref, rows_s, sa, ra, sw, rw, sm, rm):
        x, y, c = _place()
        me = 4 * x + 2 * y + c
        chip = 2 * x + y
        cv = c_ref[...]
        call_ref[me] = cv * _sig(cv)
        wdwg_ref[chip] = wdw_ref[...]
        sends = []
        for m in range(1, N_DEV):
            peer = (_flip(x, m >> 2), _flip(y, (m >> 1) & 1), _flip(c, m & 1))
            cp = _rcopy(call_ref.at[me], call_ref.at[me], sa.at[m - 1], ra.at[m - 1], peer)
            cp.start()
            sends.append(cp)
        for k, (fx, fy) in enumerate(OTHER_CHIPS):
            peer = (_flip(x, fx), _flip(y, fy), c)
            cp = _rcopy(wdwg_ref.at[chip], wdwg_ref.at[chip], sw.at[k], rw.at[k], peer)
            cp.start()
            sends.append(cp)
        for m in range(1, N_DEV):
            peer = (_flip(x, m >> 2), _flip(y, (m >> 1) & 1), _flip(c, m & 1))
            pid = 4 * peer[0] + 2 * peer[1] + peer[2]
            _rcopy(call_ref.at[pid], call_ref.at[pid], sa.at[m - 1], ra.at[m - 1], peer).wait_recv()
        for b in range(N_DEV):
            c8_ref[b:b + 1, :] = call_ref[b]
        mod_all = _dot(c8_ref[...].astype(BF16), w_ref[...].astype(BF16)) + b_ref[...]
        for b in range(N_DEV):
            rows_s[b] = mod_all[b:b + 1, :]
        modg_ref[chip] = rows_s[me]
        for k, (fx, fy) in enumerate(OTHER_CHIPS):
            peer = (_flip(x, fx), _flip(y, fy), c)
            pid = 4 * peer[0] + 2 * peer[1] + peer[2]
            cp = _rcopy(rows_s.at[pid], modg_ref.at[chip], sm.at[k], rm.at[k], peer)
            cp.start()
            sends.append(cp)
        for k, (fx, fy) in enumerate(OTHER_CHIPS):
            peer = (_flip(x, fx), _flip(y, fy), c)
            pchip = 2 * peer[0] + peer[1]
            _rcopy(rows_s.at[0], modg_ref.at[pchip], sm.at[k], rm.at[k], peer).wait_recv()
            _rcopy(wdwg_ref.at[pchip], wdwg_ref.at[pchip], sw.at[k], rw.at[k], peer).wait_recv()
        for cp in sends:
            cp.wait_send()

    vm = pl.BlockSpec(memory_space=pltpu.VMEM)
    return pl.pallas_call(
        body, name="ada_exchange",
        out_shape=[jax.ShapeDtypeStruct((N_DEV, 1, D_MODEL), F32), jax.ShapeDtypeStruct((N_DEV, D_MODEL), F32),
                   jax.ShapeDtypeStruct((N_CHIPS, 1, ncol), F32), jax.ShapeDtypeStruct((N_CHIPS, HALO, HEAD_D), F32)],
        in_specs=[vm] * 4, out_specs=[vm] * 4,
        scratch_shapes=[pltpu.VMEM((N_DEV, 1, ncol), F32),
                        pltpu.SemaphoreType.DMA((N_DEV - 1,)), pltpu.SemaphoreType.DMA((N_DEV - 1,)),
                        pltpu.SemaphoreType.DMA((3,)), pltpu.SemaphoreType.DMA((3,)),
                        pltpu.SemaphoreType.DMA((3,)), pltpu.SemaphoreType.DMA((3,))],
        compiler_params=_cp(None, 32),
    )(c_row, w_ada, b_sh, wdw_pad)


def _cast_own(chip1, shard, name):
    rows, cols = shard.shape
    tr = _row_tile(rows)

    def body(ch_ref, s_ref, o_ref):
        o_ref[0] = s_ref[...].astype(BF16)

    gs = pltpu.PrefetchScalarGridSpec(
        num_scalar_prefetch=1, grid=(rows // tr,),
        in_specs=[pl.BlockSpec((tr, cols), lambda i, ch: (i, 0))],
        out_specs=pl.BlockSpec((1, tr, cols), lambda i, ch: (ch[0], i, 0)))
    return pl.pallas_call(
        body, name=name, grid_spec=gs, out_shape=_big((N_CHIPS, rows, cols), BF16),
        compiler_params=_cp(("arbitrary",), 32),
    )(chip1, _hbm(shard))


def _slab(buf, ch, core):
    hs = buf.shape[1] // 2
    return buf.at[ch, pl.ds(core * hs, hs), :]


def _gather_start(bufs, ssem, rsem, relations=(0, 1, 2)):
    x, y, c = _place()
    chip = 2 * x + y
    for k in relations:
        fx, fy = OTHER_CHIPS[k]
        peer = (_flip(x, fx), _flip(y, fy), c)
        for t, buf in enumerate(bufs):
            _rcopy(_slab(buf, chip, c), _slab(buf, chip, c), ssem.at[t * 3 + k], rsem.at[t * 3 + k], peer).start()


def _gather_pass_on(bufs, ssem, rsem):
    nt = len(bufs)
    x, y, c = _place()
    sibling = (x, y, 1 - c)
    for k, (fx, fy) in enumerate(OTHER_CHIPS):
        peer = (_flip(x, fx), _flip(y, fy), c)
        pchip = 2 * peer[0] + peer[1]
        for t, buf in enumerate(bufs):
            _rcopy(_slab(buf, pchip, c), _slab(buf, pchip, c), ssem.at[t * 3 + k], rsem.at[t * 3 + k], peer).wait_recv()
            _rcopy(_slab(buf, pchip, c), _slab(buf, pchip, c), ssem.at[3 * nt + t * 3 + k],
                   rsem.at[3 * nt + t * 3 + k], sibling).start()


def _gather_drain(bufs, ssem, rsem):
    nt = len(bufs)
    x, y, c = _place()
    chip = 2 * x + y
    sibling = (x, y, 1 - c)
    for k, (fx, fy) in enumerate(OTHER_CHIPS):
        peer = (_flip(x, fx), _flip(y, fy), c)
        pchip = 2 * peer[0] + peer[1]
        for t, buf in enumerate(bufs):
            _rcopy(_slab(buf, pchip, 1 - c), _slab(buf, pchip, 1 - c), ssem.at[3 * nt + t * 3 + k],
                   rsem.at[3 * nt + t * 3 + k], sibling).wait_recv()
            _rcopy(_slab(buf, chip, c), _slab(buf, chip, c), ssem.at[t * 3 + k], rsem.at[t * 3 + k], peer).wait_send()
            _rcopy(_slab(buf, pchip, c), _slab(buf, pchip, c), ssem.at[3 * nt + t * 3 + k],
                   rsem.at[3 * nt + t * 3 + k], sibling).wait_send()


def _gather_finish(bufs, ssem, rsem):
    _gather_pass_on(bufs, ssem, rsem)
    _gather_drain(bufs, ssem, rsem)


def _gather_arrive(bufs, k, ssem, rsem):
    nt = len(bufs)
    x, y, c = _place()
    fx, fy = OTHER_CHIPS[k]
    peer = (_flip(x, fx), _flip(y, fy), c)
    pchip = 2 * peer[0] + peer[1]
    for t, buf in enumerate(bufs):
        _rcopy(_slab(buf, pchip, c), _slab(buf, pchip, c), ssem.at[t * 3 + k], rsem.at[t * 3 + k], peer).wait_recv()
        _rcopy(_slab(buf, pchip, c), _slab(buf, pchip, c), ssem.at[3 * nt + t * 3 + k],
               rsem.at[3 * nt + t * 3 + k], (x, y, 1 - c)).start()
    for t, buf in enumerate(bufs):
        _rcopy(_slab(buf, pchip, 1 - c), _slab(buf, pchip, 1 - c), ssem.at[3 * nt + t * 3 + k],
               rsem.at[3 * nt + t * 3 + k], (x, y, 1 - c)).wait_recv()


def _gather_sends_done(bufs, ssem, rsem):
    nt = len(bufs)
    x, y, c = _place()
    chip = 2 * x + y
    for k, (fx, fy) in enumerate(OTHER_CHIPS):
        peer = (_flip(x, fx), _flip(y, fy), c)
        pchip = 2 * peer[0] + peer[1]
        for t, buf in enumerate(bufs):
            _rcopy(_slab(buf, chip, c), _slab(buf, chip, c), ssem.at[t * 3 + k], rsem.at[t * 3 + k], peer).wait_send()
            _rcopy(_slab(buf, pchip, c), _slab(buf, pchip, c), ssem.at[3 * nt + t * 3 + k],
                   rsem.at[3 * nt + t * 3 + k], (x, y, 1 - c)).wait_send()


def _ring_parts(buf):
    x, y, c = _place()
    ynb, xnb = (x, 1 - y, c), (1 - x, y, c)
    ychip, xchip, dchip = 2 * x + (1 - y), 2 * (1 - x) + y, 2 * (1 - x) + (1 - y)
    hs = buf.shape[1] // 2

    def piece(ch, q):
        return buf.at[ch, pl.ds(c * hs + q * (hs // 2), hs // 2), :]

    return ynb, xnb, ychip, xchip, dchip, piece


def _ring_start(bufs, ssem, rsem):
    x, y, c = _place()
    chip = 2 * x + y
    for t, buf in enumerate(bufs):
        ynb, xnb, _, _, _, _ = _ring_parts(buf)
        _rcopy(_slab(buf, chip, c), _slab(buf, chip, c), ssem.at[2 * t], rsem.at[2 * t], ynb).start()
        _rcopy(_slab(buf, chip, c), _slab(buf, chip, c), ssem.at[2 * t + 1], rsem.at[2 * t + 1], xnb).start()


def _ring_forward(bufs, ssem, rsem):
    nt = len(bufs)
    _, _, c = _place()
    for t, buf in enumerate(bufs):
        ynb, xnb, ychip, xchip, _, piece = _ring_parts(buf)
        _rcopy(_slab(buf, ychip, c), _slab(buf, ychip, c), ssem.at[2 * t], rsem.at[2 * t], ynb).wait_recv()
        _rcopy(piece(ychip, 0), piece(ychip, 0), ssem.at[2 * nt + 2 * t], rsem.at[2 * nt + 2 * t], xnb).start()
        _rcopy(_slab(buf, xchip, c), _slab(buf, xchip, c), ssem.at[2 * t + 1], rsem.at[2 * t + 1], xnb).wait_recv()
        _rcopy(piece(xchip, 1), piece(xchip, 1), ssem.at[2 * nt + 2 * t + 1], rsem.at[2 * nt + 2 * t + 1], ynb).start()


def _ring_finish(bufs, ssem, rsem):
    nt = len(bufs)
    x, y, c = _place()
    chip = 2 * x + y
    sibling = (x, y, 1 - c)
    for t, buf in enumerate(bufs):
        ynb, xnb, ychip, xchip, dchip, piece = _ring_parts(buf)
        _rcopy(piece(dchip, 0), piece(dchip, 0), ssem.at[2 * nt + 2 * t], rsem.at[2 * nt + 2 * t], xnb).wait_recv()
        _rcopy(piece(dchip, 1), piece(dchip, 1), ssem.at[2 * nt + 2 * t + 1], rsem.at[2 * nt + 2 * t + 1],
               ynb).wait_recv()
        for k, ch in enumerate((ychip, xchip, dchip)):
            _rcopy(_slab(buf, ch, c), _slab(buf, ch, c), ssem.at[4 * nt + 3 * t + k], rsem.at[4 * nt + 3 * t + k],
                   sibling).start()
    for t, buf in enumerate(bufs):
        ynb, xnb, ychip, xchip, dchip, piece = _ring_parts(buf)
        for k, ch in enumerate((ychip, xchip, dchip)):
            _rcopy(_slab(buf, ch, 1 - c), _slab(buf, ch, 1 - c), ssem.at[4 * nt + 3 * t + k],
                   rsem.at[4 * nt + 3 * t + k], sibling).wait_recv()
            _rcopy(_slab(buf, ch, c), _slab(buf, ch, c), ssem.at[4 * nt + 3 * t + k], rsem.at[4 * nt + 3 * t + k],
                   sibling).wait_send()
        _rcopy(_slab(buf, chip, c), _slab(buf, chip, c), ssem.at[2 * t], rsem.at[2 * t], ynb).wait_send()
        _rcopy(_slab(buf, chip, c), _slab(buf, chip, c), ssem.at[2 * t + 1], rsem.at[2 * t + 1], xnb).wait_send()
        _rcopy(piece(ychip, 0), piece(ychip, 0), ssem.at[2 * nt + 2 * t], rsem.at[2 * nt + 2 * t], xnb).wait_send()
        _rcopy(piece(xchip, 1), piece(xchip, 1), ssem.at[2 * nt + 2 * t + 1], rsem.at[2 * nt + 2 * t + 1],
               ynb).wait_send()


def _ring_sems(nt):
    return [pltpu.SemaphoreType.DMA((7 * nt,)), pltpu.SemaphoreType.DMA((7 * nt,))]


def _gather_sems(nt):
    return [pltpu.SemaphoreType.DMA((6 * nt,)), pltpu.SemaphoreType.DMA((6 * nt,))]


def _pair_copies(ins, outs, ssem, rsem):
    x, y, c = _place()
    copies = []
    for t in range(len(ins)):
        hs = ins[t].shape[1] // 2
        copies.append(_rcopy(ins[t].at[:, pl.ds((1 - c) * hs, hs), :], outs[t], ssem.at[t], rsem.at[t], (x, y, 1 - c)))
    return copies


def _pair_shapes(grads):
    return [_big((g.shape[0], g.shape[1] // 2, g.shape[2]), g.dtype) for g in grads]


def _pair_sems(nt):
    return [pltpu.SemaphoreType.DMA((nt,)), pltpu.SemaphoreType.DMA((nt,))]


def _pair_swap(grads, name):
    nt = len(grads)
    hbm = pl.BlockSpec(memory_space=pl.ANY)

    def body(*refs):
        copies = _pair_copies(refs[:nt], refs[nt:2 * nt], refs[2 * nt], refs[2 * nt + 1])
        for cp in copies:
            cp.start()
        for cp in copies:
            cp.wait_recv()
        for cp in copies:
            cp.wait_send()

    return pl.pallas_call(
        body, name=name, out_shape=_pair_shapes(grads), in_specs=[hbm] * nt, out_specs=[hbm] * nt,
        scratch_shapes=_pair_sems(nt),
    )(*[_hbm(g) for g in grads])


def _xchg_copies(ins, outs, ssem, rsem):
    x, y, c = _place()
    copies = []
    for k, (fx, fy) in enumerate(OTHER_CHIPS):
        peer = (_flip(x, fx), _flip(y, fy), c)
        for t in range(len(ins)):
            copies.append(_rcopy(ins[t].at[k], outs[t].at[k], ssem.at[t * 3 + k], rsem.at[t * 3 + k], peer))
    return copies


def _xchg_sems(nt):
    return [pltpu.SemaphoreType.DMA((3 * nt,)), pltpu.SemaphoreType.DMA((3 * nt,))]


def _final_exchange(fulls, stats):
    nt = len(fulls)
    rows, cols = stats.shape
    hbm = pl.BlockSpec(memory_space=pl.ANY)
    vm = pl.BlockSpec(memory_space=pltpu.VMEM)

    def body(*refs):
        ins, s_ref = refs[:nt], refs[nt]
        outs, g_ref = refs[nt + 1:2 * nt + 1], refs[2 * nt + 1]
        hssem, hrsem, ssem, rsem = refs[2 * nt + 2:]
        x, y, c = _place()
        me, sibling = (x, y, c), (x, y, 1 - c)
        halves = []
        for t in range(nt):
            hs = ins[t].shape[0] // 2
            mine = pl.ds(c * hs, hs)
            cp = _rcopy(ins[t].at[mine, :], outs[t].at[mine, :], hssem.at[t], hrsem.at[t], sibling)
            cp.start()
            halves.append(cp)

        chips = [(_flip(x, fx), _flip(y, fy)) for fx, fy in OTHER_CHIPS]

        def blk(px, py, pc):
            return g_ref.at[4 * px + 2 * py + pc]

        def copy(k, block, to, src=None):
            return _rcopy(blk(*block) if src is None else src, blk(*block), ssem.at[k], rsem.at[k], to)

        g_ref[4 * x + 2 * y + c] = s_ref[...]
        first = [copy(0, me, sibling, src=s_ref)]
        first += [copy(1 + j, me, (*chip, c), src=s_ref) for j, chip in enumerate(chips)]
        for cp in first:
            cp.start()
        passed = [copy(4 + j, (*chip, c), sibling) for j, chip in enumerate(chips)]
        for j, chip in enumerate(chips):
            copy(1 + j, (*chip, c), me).wait_recv()
            passed[j].start()
        copy(0, sibling, me).wait_recv()
        for j, chip in enumerate(chips):
            copy(4 + j, (*chip, 1 - c), me).wait_recv()
        for t in range(nt):
            hs = ins[t].shape[0] // 2
            other = pl.ds((1 - c) * hs, hs)
            _rcopy(ins[t].at[other, :], outs[t].at[other, :], hssem.at[t], hrsem.at[t], sibling).wait_recv()
        for cp in first + passed + halves:
            cp.wait_send()

    outs = pl.pallas_call(
        body, name="final_exchange",
        out_shape=[_big(f.shape, F32) for f in fulls] + [jax.ShapeDtypeStruct((N_DEV, rows, cols), F32)],
        in_specs=[hbm] * nt + [vm], out_specs=[hbm] * nt + [vm],
        input_output_aliases={t: t for t in range(nt)},
        scratch_shapes=[pltpu.SemaphoreType.DMA((nt,)), pltpu.SemaphoreType.DMA((nt,)),
                        pltpu.SemaphoreType.DMA((7,)), pltpu.SemaphoreType.DMA((7,))],
        compiler_params=_cp(None, 32),
    )(*[_hbm(f) for f in fulls], stats)
    return outs[:nt], outs[nt]


def _row_tile(rows):
    return min(rows, 512)


def _pair_sum(place, grad, got, name):
    nb, hs, cols = got.shape
    tr = _row_tile(hs)
    nr = hs // tr

    def body(pl_ref, g_ref, o_ref, pf_ref, pb_ref):
        j = pl.program_id(1)
        s = g_ref[0] + o_ref[0].astype(F32)

        @pl.when(j == 0)
        def _():
            pf_ref[...] = s

        @pl.when(j > 0)
        def _():
            pb_ref[0] = s.astype(BF16)

    gs = pltpu.PrefetchScalarGridSpec(
        num_scalar_prefetch=1, grid=(nr, nb),
        in_specs=[pl.BlockSpec((1, tr, cols), lambda i, j, p: (p[1] ^ j, p[0] * nr + i, 0)),
                  pl.BlockSpec((1, tr, cols), lambda i, j, p: (p[1] ^ j, i, 0))],
        out_specs=[pl.BlockSpec((tr, cols), lambda i, j, p: (i, 0)),
                   pl.BlockSpec((1, tr, cols), lambda i, j, p: (jnp.maximum(j - 1, 0), i, 0))])
    return pl.pallas_call(
        body, name=name, grid_spec=gs,
        out_shape=[_big((hs, cols), F32), _big((nb - 1, hs, cols), BF16)],
        compiler_params=_cp(("arbitrary", "arbitrary"), 32),
    )(place, *_hbm(grad, got))


def _chip_sum(place, pair_f, got_b, name):
    nb, hs, cols = got_b.shape
    tr = _row_tile(hs)
    nr = hs // tr

    def body(pl_ref, pf_ref, gb_ref, o_ref):
        acc = pf_ref[...]
        for k in range(nb):
            acc = acc + gb_ref[k].astype(F32)
        o_ref[...] = acc

    gs = pltpu.PrefetchScalarGridSpec(
        num_scalar_prefetch=1, grid=(nr,),
        in_specs=[pl.BlockSpec((tr, cols), lambda i, p: (i, 0)),
                  pl.BlockSpec((nb, tr, cols), lambda i, p: (0, i, 0))],
        out_specs=pl.BlockSpec((tr, cols), lambda i, p: (p[0] * nr + i, 0)))
    return pl.pallas_call(
        body, name=name, grid_spec=gs,
        out_shape=_big((2 * hs, cols), F32),
        compiler_params=_cp(("arbitrary",), 32),
    )(place, *_hbm(pair_f, got_b))


def _adam_math(w, g, m, v):
    m2 = ADAM_B1 * m + (1.0 - ADAM_B1) * g
    v2 = ADAM_B2 * v + (1.0 - ADAM_B2) * (g * g)
    m_hat = m2 / (1.0 - ADAM_B1 ** ADAM_STEP)
    v_hat = v2 / (1.0 - ADAM_B2 ** ADAM_STEP)
    delta = -ADAM_LR * (m_hat / (jnp.sqrt(v_hat) + ADAM_EPS) + ADAM_WD * w)
    return delta, m2, v2


def _adam_big(w, g, m, v, name):
    rows, cols = w.shape
    tr = _row_tile(rows)

    def body(w_ref, g_ref, m_ref, v_ref, d_out, m_out, v_out):
        d, m2, v2 = _adam_math(w_ref[...], g_ref[...], m_ref[...], v_ref[...])
        d_out[...] = d
        m_out[...] = m2
        v_out[...] = v2

    spec = pl.BlockSpec((tr, cols), lambda i: (i, 0))
    return pl.pallas_call(
        body, name=name, grid=(rows // tr,), in_specs=[spec] * 4, out_specs=[spec] * 3,
        out_shape=[_big(w.shape, F32)] * 3,
        compiler_params=_cp(("arbitrary",), 48),
    )(*_hbm(w, g, m, v))


def _ada_grad_adam(c8t, dmod_sh, w, m, v):
    rows, cols = w.shape
    tr = _row_tile(rows) // 2

    def body(ct_ref, dm_ref, w_ref, m_ref, v_ref, g_out, d_out, m_out, v_out):
        g = None
        for b in range(N_DEV):
            term = ct_ref[:, b:b + 1] * dm_ref[b:b + 1, :]
            g = term if g is None else g + term
        d, m2, v2 = _adam_math(w_ref[...], g, m_ref[...], v_ref[...])
        g_out[...] = g
        d_out[...] = d
        m_out[...] = m2
        v_out[...] = v2

    spec = pl.BlockSpec((tr, cols), lambda i: (i, 0))
    return pl.pallas_call(
        body, name="ada_grad_adam", grid=(rows // tr,),
        in_specs=[pl.BlockSpec((tr, N_DEV), lambda i: (i, 0)), _full((N_DEV, cols)), spec, spec, spec],
        out_specs=[spec] * 4, out_shape=[_big(w.shape, F32)] * 4,
        compiler_params=_cp(("arbitrary",), 32),
    )(c8t, dmod_sh, *_hbm(w, m, v))


def _tok_tile(t):
    return min(TOK_TILE, t)


def _mix_in_fwd(chip1, x, modr, g_pre, b_in4, later, w_in_buf, w_out_buf):
    T = x.shape[0]
    tm = min(MIXIN_TILE, T)
    nt = T // tm
    nb = IN_COLS // N_CHIPS
    nl = len(later)

    def body(*refs):
        ch_ref, x_ref, mod_ref, g_ref, b_ref = refs[:5]
        l_ins = refs[5:5 + nl]
        p_ref, h_ref = refs[7 + nl:9 + nl]
        l_outs = refs[9 + nl:9 + 2 * nl]
        win_ref, wout_ref = refs[9 + 2 * nl:11 + 2 * nl]
        h_all, wblk, lsem, is_sem, ir_sem, os_sem, or_sem = refs[11 + 2 * nl:]
        k, i = pl.program_id(0), pl.program_id(1)
        chip = ch_ref[0]

        @pl.when(k == 0)
        def _():
            for src, dst in zip(l_ins, l_outs):
                dst[0] = src[...].astype(BF16)

        def load_block(blk):
            cp = pltpu.make_async_copy(win_ref.at[blk], wblk, lsem)
            cp.start()
            cp.wait()

        @pl.when((k == 0) & (i == 0))
        def _():
            _gather_start([win_ref], is_sem, ir_sem, relations=(0, 1))
            load_block(chip)

        for r in range(N_CHIPS - 1):
            @pl.when((k == r + 1) & (i == 0))
            def _(r=r):
                _gather_arrive([win_ref], r, is_sem, ir_sem)
                if r == 0:
                    _gather_start([win_ref], is_sem, ir_sem, relations=(2,))
                if r == 1:
                    _gather_start([wout_ref], os_sem, or_sem)
                load_block(chip ^ (r + 1))

        rows = pl.ds(pl.multiple_of(i * tm, tm), tm)

        @pl.when(k == 0)
        def _():
            xv = x_ref[...]
            rstd = lax.rsqrt(jnp.mean(xv * xv, axis=-1, keepdims=True) + RMS_EPS)
            h = (xv * rstd) * g_ref[...] * (1.0 + mod_ref[1:2, :]) + mod_ref[0:1, :]
            hb = h.astype(BF16)
            h_ref[...] = hb
            h_all[rows, :] = hb

        p_ref[...] = _dot(h_all[rows, :], wblk[...]) + b_ref[chip ^ k]

        @pl.when((k == N_CHIPS - 1) & (i == nt - 1))
        def _():
            _gather_sends_done([win_ref], is_sem, ir_sem)
            _gather_finish([wout_ref], os_sem, or_sem)

    hbm = pl.BlockSpec(memory_space=pl.ANY)
    first_pass = lambda k, i, ch: (jnp.where(k == 0, i, nt - 1), 0)
    own_slot = lambda k, i, ch: (ch[0], jnp.where(k == 0, i, nt - 1), 0)
    gs = pltpu.PrefetchScalarGridSpec(
        num_scalar_prefetch=1, grid=(N_CHIPS, nt),
        in_specs=[pl.BlockSpec((tm, D_MODEL), first_pass), pl.BlockSpec((6, D_MODEL), lambda k, i, ch: (0, 0)),
                  pl.BlockSpec((1, D_MODEL), lambda k, i, ch: (0, 0)),
                  pl.BlockSpec((N_CHIPS, 1, nb), lambda k, i, ch: (0, 0, 0))]
        + [pl.BlockSpec((w.shape[0] // nt, w.shape[1]), first_pass) for w in later] + [hbm, hbm],
        out_specs=[pl.BlockSpec((tm, nb), lambda k, i, ch: (i, ch[0] ^ k)), pl.BlockSpec((tm, D_MODEL), first_pass)]
        + [pl.BlockSpec((1, w.shape[0] // nt, w.shape[1]), own_slot) for w in later] + [hbm, hbm],
        scratch_shapes=[pltpu.VMEM((T, D_MODEL), BF16), pltpu.VMEM((D_MODEL, nb), BF16), pltpu.SemaphoreType.DMA]
        + _gather_sems(1) + _gather_sems(1))
    outs = pl.pallas_call(
        body, name="mix_in_fwd", grid_spec=gs,
        out_shape=[_big((T, IN_COLS), F32), _big((T, D_MODEL), BF16)]
        + [_big((N_CHIPS,) + w.shape, BF16) for w in later]
        + [_big(w_in_buf.shape, BF16), _big(w_out_buf.shape, BF16)],
        input_output_aliases={5 + nl: 2 + nl, 6 + nl: 3 + nl},
        compiler_params=_cp(("arbitrary", "arbitrary"), 48),
    )(chip1, _hbm(x), modr, g_pre, b_in4, *[_hbm(w) for w in later], _hbm(w_in_buf), _hbm(w_out_buf))
    return outs[0], outs[1], outs[2:2 + nl], outs[2 + nl], outs[3 + nl]


def _mixers_fwd(p, wdw, vecs, lbl, w_out, x, modr, g_post, g_ffn, gbufs):
    T = p.shape[0]
    tm = _tok_tile(T)
    nt = T // tm
    nch = tm // CHUNK
    ng = len(gbufs)
    n_in, n_out = 12, 7

    def body(*refs):
        (p_ref, wdw_ref, bdw_ref, gain_ref, bias_ref, gout_ref, lbl_ref, wout_ref, x_ref, mod_ref, gp_ref,
         gf_ref) = refs[:n_in]
        cat_ref, ys_ref, o_ref, st_ref, y_ref, x1_ref, h2_ref = refs[n_in + ng:n_in + ng + n_out]
        gout_bufs = refs[n_in + ng + n_out:n_in + 2 * ng + n_out]
        (ubuf, state, qt_s, kt_s, kh_s, v_s, egl_s, lower_s, gmat_s, gssem,
         grsem) = refs[n_in + 2 * ng + n_out:]
        i = pl.program_id(0)

        @pl.when(i == 0)
        def _():
            _ring_start(gout_bufs, gssem, grsem)
            lower_s[...], _, _ = _chunk_masks(tm)
            gmat_s[...] = _gn_matrix()
            state[...] = jnp.zeros(state.shape, F32)
            ubuf[0:HALO, :] = jnp.zeros((HALO, CONV_CH), F32)
            ubuf[HALO + tm:HALO + tm + SUB, :] = jnp.zeros((SUB, CONV_CH), F32)

        @pl.when(i > 0)
        def _():
            ubuf[0:HALO, :] = ubuf[tm:tm + HALO, :]

        ubuf[HALO:HALO + tm, :] = p_ref[:, 0:CONV_CH] * _sig(p_ref[:, CONV_CH:2 * CONV_CH])
        for r in range(tm // CONV_ROWS):
            rows = slice(r * CONV_ROWS, (r + 1) * CONV_ROWS)
            for lb_ in range(CONV_CH // LANE):
                lanes = slice(lb_ * LANE, (lb_ + 1) * LANE)
                ys_ref[rows, lanes] = bdw_ref[:, lanes] + _tap_conv(ubuf, wdw_ref, r * CONV_ROWS, CONV_FWD_TAPS, lanes)
        gmat = gmat_s[...]
        yv = ys_ref[...]
        d = yv - _gmean(yv, gmat)
        rs = lax.rsqrt(_gmean(d * d, gmat) + GN_EPS)
        z = d * rs * gain_ref[...] + bias_ref[...]
        cat_ref[:, 0:CONV_CH] = (z * _sig(z)).astype(BF16)

        lb, _ = _lower_bound(lbl_ref)
        o0 = 2 * CONV_CH
        pr = _hgrn_prep(p_ref[:, o0:o0 + HGRN_W], p_ref[:, o0 + HGRN_W:o0 + 2 * HGRN_W], lb, lower_s[...])
        qt_s[...] = pr["qt"].astype(BF16)
        kt_s[...] = pr["kt"].astype(BF16)
        kh_s[...] = pr["kh"].astype(BF16)
        v_s[...] = p_ref[:, o0 + 2 * HGRN_W:o0 + 3 * HGRN_W].astype(BF16)
        egl_s[...] = jnp.exp(pr["Gl"])
        tri = _tri()

        def chunk(ci, carry):
            r0 = pl.multiple_of(ci * CHUNK, CHUNK)
            rows = pl.ds(r0, CHUNK)
            for h in range(N_HEADS):
                ls = pl.ds(h * HEAD_D, HEAD_D)
                qc, kc, hc, vc = qt_s[rows, ls], kt_s[rows, ls], kh_s[rows, ls], v_s[rows, ls]
                s0 = state[h]
                s0b = s0.astype(BF16)
                st_ref[ci, h] = s0
                att =jnp.where(tri, _dot_nt(qc, kc), 0.0).astype(BF16)
                o_ref[rows, ls] = _dot(att, vc) + _dot_nt(qc, s0b)
                state[h] = s0 * egl_s[pl.ds(r0, 1), ls] + _dot_tn(vc, hc)
            return carry

        lax.fori_loop(0, nch, chunk, 0, unroll=min(CHUNK_UNROLL, nch))
        for h in range(N_HEADS):
            sl = slice(h * HEAD_D, (h + 1) * HEAD_D)
            oh = o_ref[:, sl]
            gh = p_ref[:, o0 + 3 * HGRN_W + h * HEAD_D:o0 + 3 * HGRN_W + (h + 1) * HEAD_D]
            rsh = lax.rsqrt(jnp.mean(oh * oh, axis=-1, keepdims=True) + RMS_EPS)
            hg = (oh * rsh) * gout_ref[:, sl] * (gh * _sig(gh))
            cat_ref[:, CONV_CH + h * HEAD_D:CONV_CH + (h + 1) * HEAD_D] = hg.astype(BF16)

        yv = _dot(cat_ref[...], wout_ref[...])
        y_ref[...] = yv
        rsy = lax.rsqrt(jnp.mean(yv * yv, axis=-1, keepdims=True) + RMS_EPS)
        x1 = x_ref[...] + mod_ref[2:3, :] * ((yv * rsy) * gp_ref[...])
        x1_ref[...] = x1
        rs1 = lax.rsqrt(jnp.mean(x1 * x1, axis=-1, keepdims=True) + RMS_EPS)
        h2 = (x1 * rs1) * gf_ref[...] * (1.0 + mod_ref[4:5, :]) + mod_ref[3:4, :]
        h2_ref[...] = h2.astype(BF16)

        @pl.when(i == min(nt - 1, nt // 2 + 1))
        def _():
            _ring_forward(gout_bufs, gssem, grsem)

        @pl.when(i == nt - 1)
        def _():
            _ring_finish(gout_bufs, gssem, grsem)

    tile = lambda cols: pl.BlockSpec((tm, cols), lambda i: (i, 0))
    hbm = pl.BlockSpec(memory_space=pl.ANY)
    outs = pl.pallas_call(
        body, name="mixers_fwd", grid=(nt,),
        in_specs=[tile(IN_COLS), _full((HALO, CONV_CH))] + [_full((1, CONV_CH))] * 4 + [_full((2, HGRN_W))]
        + [_full((D_MODEL, D_MODEL)), tile(D_MODEL), _full((6, D_MODEL)), _full((1, D_MODEL)), _full((1, D_MODEL))]
        + [hbm] * ng,
        out_specs=[tile(D_MODEL), tile(CONV_CH), tile(HGRN_W),
                   pl.BlockSpec((nch, N_HEADS, HEAD_D, HEAD_D), lambda i: (i, 0, 0, 0)),
                   tile(D_MODEL), tile(D_MODEL), tile(D_MODEL)] + [hbm] * ng,
        out_shape=[_big((T, D_MODEL), BF16), _big((T, CONV_CH), F32), _big((T, HGRN_W), F32),
                   _big((T // CHUNK, N_HEADS, HEAD_D, HEAD_D), F32), _big((T, D_MODEL), F32),
                   _big((T, D_MODEL), F32), _big((T, D_MODEL), BF16)] + [_big(b.shape, BF16) for b in gbufs],
        input_output_aliases={n_in + t: n_out + t for t in range(ng)},
        scratch_shapes=[pltpu.VMEM((tm + HALO + SUB, CONV_CH), F32), pltpu.VMEM((N_HEADS, HEAD_D, HEAD_D), F32),
                        pltpu.VMEM((tm, HGRN_W), BF16), pltpu.VMEM((tm, HGRN_W), BF16),
                        pltpu.VMEM((tm, HGRN_W), BF16), pltpu.VMEM((tm, HGRN_W), BF16),
                        pltpu.VMEM((tm, HGRN_W), F32), pltpu.VMEM((tm, tm), BF16),
                        pltpu.VMEM((CONV_CH, CONV_CH), BF16)] + _ring_sems(ng),
        compiler_params=_cp(("arbitrary",), 56),
    )(_hbm(p), wdw, *vecs, lbl, *_hbm(w_out, x), modr, g_post, g_ffn, *[_hbm(b) for b in gbufs])
    return outs[:n_out], outs[n_out:]


def _row_chains(rows, n=2):
    step = rows // n
    return [slice(k * step, (k + 1) * step) for k in range(n)]


def _ffn_blocks():
    return D_FF // FFN_BLOCK, (D_FF // N_CHIPS) // FFN_BLOCK


def _ffn_fwd(h2, w_up_g, w_down, x1, target, modr, g_post):
    T = h2.shape[0]
    tm = min(FFN_TILE, T)
    fb = FFN_BLOCK
    nj, per = _ffn_blocks()

    def body(h_ref, wu_ref, wd_ref, x1_ref, t_ref, mod_ref, g_ref, r_ref, dy2_ref, dx2_ref, st_ref, acc):
        i, j = pl.program_id(0), pl.program_id(1)

        @pl.when((i == 0) & (j == 0))
        def _():
            st_ref[...] = jnp.zeros(st_ref.shape, F32)

        @pl.when(j == 0)
        def _():
            acc[...] = jnp.zeros(acc.shape, F32)

        for rows in _row_chains(tm):
            ra = jnp.maximum(_dot(h_ref[rows, :], wu_ref[0]), 0.0)
            rb = (ra * ra).astype(BF16)
            r_ref[rows, :] = rb
            acc[rows, :] += _dot(rb, wd_ref[...])

        @pl.when(j == nj - 1)
        def _():
            y2 = acc[...]
            rs = lax.rsqrt(jnp.mean(y2 * y2, axis=-1, keepdims=True) + RMS_EPS)
            nh = y2 * rs
            gp = g_ref[...]
            err = x1_ref[...] + mod_ref[5:6, :] * (nh * gp) - t_ref[...]
            dx2 = err * (1.0 / D_MODEL)
            dx2_ref[...] = dx2
            st_ref[0:1, :] += _colsum(err * err)
            st_ref[1:2, :] += _colsum(dx2 * (nh * gp))
            dn = dx2 * mod_ref[5:6, :]
            st_ref[2:3, :] += _colsum(dn * nh)
            dy2_ref[...] = _rms_bwd(dn * gp, nh, rs).astype(BF16)

    tile = pl.BlockSpec((tm, D_MODEL), lambda i, j: (i, 0))
    return pl.pallas_call(
        body, name="ffn_fwd", grid=(T // tm, nj),
        in_specs=[tile, pl.BlockSpec((1, D_MODEL, fb), lambda i, j: (j // per, 0, j % per)),
                  pl.BlockSpec((fb, D_MODEL), lambda i, j: (j, 0)), tile, tile,
                  _full((6, D_MODEL)), _full((1, D_MODEL))],
        out_specs=[pl.BlockSpec((tm, fb), lambda i, j: (i, j)), tile, tile, _full((8, D_MODEL))],
        out_shape=[_big((T, D_FF), BF16), _big((T, D_MODEL), BF16), _big((T, D_MODEL), F32),
                   jax.ShapeDtypeStruct((8, D_MODEL), F32)],
        scratch_shapes=[pltpu.VMEM((tm, D_MODEL), F32)],
        compiler_params=_cp(("arbitrary", "arbitrary"), 56),
    )(*_hbm(h2, w_up_g, w_down, x1, target), modr, g_post)


def _rms_bwd(dxn, xn, rs):
    return rs * (dxn - xn * jnp.mean(dxn * xn, axis=-1, keepdims=True))


def _ffn_bwd(dy2, r, x1, dx2, w_up_g, w_down, modr, g_ffn):
    T = dx2.shape[0]
    tm = min(FFN_TILE, T)
    fb = FFN_BLOCK
    nj, per = _ffn_blocks()

    def body(dy2_ref, r_ref, x1_ref, dx2_ref, wu_ref, wd_ref, mod_ref, gf_ref, da_ref, dx1_ref, st_ref, dh_s):
        i, j = pl.program_id(0), pl.program_id(1)

        @pl.when((i == 0) & (j == 0))
        def _():
            st_ref[...] = jnp.zeros(st_ref.shape, F32)

        @pl.when(j == 0)
        def _():
            dh_s[...] = jnp.zeros(dh_s.shape, F32)

        for rows in _row_chains(tm):
            ra = jnp.sqrt(r_ref[rows, :].astype(F32))
            da = (_dot_nt(dy2_ref[rows, :], wd_ref[...]) * (2.0 * ra)).astype(BF16)
            da_ref[rows, :] = da
            dh_s[rows, :] += _dot_nt(da, wu_ref[0])

        @pl.when(j == nj - 1)
        def _():
            dh = dh_s[...]
            x1v = x1_ref[...]
            rs1 = lax.rsqrt(jnp.mean(x1v * x1v, axis=-1, keepdims=True) + RMS_EPS)
            xn = x1v * rs1
            st_ref[0:1, :] += _colsum(dh)
            st_ref[1:2, :] += _colsum(dh * (xn * gf_ref[...]))
            dsc = dh * (1.0 + mod_ref[4:5, :])
            st_ref[2:3, :] += _colsum(dsc * xn)
            dx1_ref[...] = dx2_ref[...] + _rms_bwd(dsc * gf_ref[...], xn, rs1)

    tile = pl.BlockSpec((tm, D_MODEL), lambda i, j: (i, 0))
    ftile = pl.BlockSpec((tm, fb), lambda i, j: (i, j))
    return pl.pallas_call(
        body, name="ffn_bwd", grid=(T // tm, nj),
        in_specs=[tile, ftile, tile, tile, pl.BlockSpec((1, D_MODEL, fb), lambda i, j: (j // per, 0, j % per)),
                  pl.BlockSpec((fb, D_MODEL), lambda i, j: (j, 0)), _full((6, D_MODEL)), _full((1, D_MODEL))],
        out_specs=[ftile, tile, _full((8, D_MODEL))],
        out_shape=[_big((T, D_FF), BF16), _big((T, D_MODEL), F32), jax.ShapeDtypeStruct((8, D_MODEL), F32)],
        scratch_shapes=[pltpu.VMEM((tm, D_MODEL), F32)],
        compiler_params=_cp(("arbitrary", "arbitrary"), 56),
    )(*_hbm(dy2, r, x1, dx2, w_up_g, w_down), modr, g_ffn)


def _mix_out_bwd(dx1, y, cat, w_out, modr, g_post, swap):
    T = dx1.shape[0]
    tm = _tok_tile(T)
    nt = T // tm
    ns = len(swap)

    def body(*refs):
        dx1_ref, y_ref, cat_ref, w_ref, mod_ref, gp_ref = refs[:6]
        s_ins = refs[6:6 + ns]
        dcat_ref, st_ref, gw_ref = refs[6 + ns:9 + ns]
        s_outs = refs[9 + ns:9 + 2 * ns]
        gacc, gsem, pssem, prsem = refs[9 + 2 * ns:]
        i = pl.program_id(0)

        @pl.when(i == 0)
        def _():
            for cp in _pair_copies(s_ins, s_outs, pssem, prsem):
                cp.start()
            st_ref[...] = jnp.zeros(st_ref.shape, F32)
            gacc[...] = jnp.zeros(gacc.shape, F32)

        dxv, yv = dx1_ref[...], y_ref[...]
        rs = lax.rsqrt(jnp.mean(yv * yv, axis=-1, keepdims=True) + RMS_EPS)
        nh = yv * rs
        st_ref[0:1, :] += _colsum(dxv * (nh * gp_ref[...]))
        dn = dxv * mod_ref[2:3, :]
        st_ref[1:2, :] += _colsum(dn * nh)
        dy = _rms_bwd(dn * gp_ref[...], nh, rs).astype(BF16)
        dcat_ref[...] = _dot_nt(dy, w_ref[...])
        for cols in _row_chains(D_MODEL):
            gacc[:, cols] += _dot_tn(cat_ref[...], dy[:, cols])

        @pl.when(i == nt - 1)
        def _():
            out = pltpu.make_async_copy(gacc, gw_ref, gsem)
            out.start()
            copies = _pair_copies(s_ins, s_outs, pssem, prsem)
            for cp in copies:
                cp.wait_recv()
            for cp in copies:
                cp.wait_send()
            out.wait()

    tile = pl.BlockSpec((tm, D_MODEL), lambda i: (i, 0))
    hbm = pl.BlockSpec(memory_space=pl.ANY)
    outs = pl.pallas_call(
        body, name="mix_out_bwd", grid=(nt,),
        in_specs=[tile, tile, tile, _full((D_MODEL, D_MODEL)), _full((6, D_MODEL)), _full((1, D_MODEL))]
        + [hbm] * ns,
        out_specs=[tile, _full((8, D_MODEL)), hbm] + [hbm] * ns,
        out_shape=[_big((T, D_MODEL), F32), jax.ShapeDtypeStruct((8, D_MODEL), F32), _big((D_MODEL, D_MODEL), F32)]
        + _pair_shapes(swap),
        scratch_shapes=[pltpu.VMEM((D_MODEL, D_MODEL), F32), pltpu.SemaphoreType.DMA] + _pair_sems(ns),
        compiler_params=_cp(("arbitrary",), 48),
    )(*_hbm(dx1, y, cat, w_out), modr, g_post, *[_hbm(g) for g in swap])
    return outs[:3], outs[3:]


def _mixers_bwd(p, dcat, ys, o, states, h1, wdw, vecs, lbl, pairs_b):
    T = p.shape[0]
    tm = min(MIXB_TILE, T)
    nt = T // tm
    nch = tm // CHUNK
    hpt = tm // HALO
    nx = len(pairs_b)
    nb = IN_COLS // N_CHIPS

    def body(*refs):
        (p_ref, ph_ref, dcat_ref, ys_ref, o_ref, st_ref, h1_ref, wdw_ref, bdw_ref, gain_ref, bias_ref, gout_ref,
         lbl_ref) = refs[:13]
        x_ins = refs[13:13 + nx]
        dp_ref, sb_ref, s5_ref, dw_ref = refs[13 + nx:17 + nx]
        x_outs = refs[17 + nx:17 + 2 * nx]
        gin_ref, ginb_ref = refs[17 + 2 * nx], refs[18 + 2 * nx]
        (ubuf, dybuf, carry, dstate, qt_s, kt_s, kh_s, v_s, do_s, egl_s, dqt_s, dkt_s, dkh_s, dv_s, dgl_s,
         dsh, dw8, dshift, lower_s, upper_s, same_s, gmat_s, gacc, gacc_b, dp_prev, h1_prev, gsem, gbsem, xssem,
         xrsem) = refs[19 + 2 * nx:]
        i = pl.program_id(0)
        tile_idx = nt - 1 - i

        @pl.when(i == 0)
        def _():
            for cp in _xchg_copies(x_ins, x_outs, xssem, xrsem):
                cp.start()
            gacc[...] = jnp.zeros(gacc.shape, F32)
            dstate[...] = jnp.zeros(dstate.shape, F32)
            carry[...] = jnp.zeros(carry.shape, F32)
            sb_ref[...] = jnp.zeros(sb_ref.shape, F32)
            s5_ref[...] = jnp.zeros(s5_ref.shape, F32)
            dw_ref[...] = jnp.zeros(dw_ref.shape, F32)
            dw8[...] = jnp.zeros(dw8.shape, F32)
            lower_s[...], upper_s[...], same_s[...] = _chunk_masks(tm)
            gmat_s[...] = _gn_matrix()
            dsh[0:SUB, :] = jnp.zeros((SUB, CONV_CH), F32)
            dsh[SUB + tm:2 * SUB + tm, :] = jnp.zeros((SUB, CONV_CH), F32)
            ubuf[HALO + tm:HALO + tm + SUB, :] = jnp.zeros((SUB, CONV_CH), F32)
            dp_prev[...] = jnp.zeros(dp_prev.shape, BF16)
            h1_prev[...] = jnp.zeros(h1_prev.shape, BF16)

        n_pieces = (tm // CONV_ROWS) * (CONV_CH // LANE)
        per_block = n_pieces // N_CHIPS
        prow = D_MODEL // per_block

        def w_in_grad_piece(k):
            j, part = k // per_block, k % per_block
            rows_k = slice(part * prow, (part + 1) * prow)
            gacc[j, rows_k, :] += _dot_tn(h1_prev[:, rows_k], dp_prev[:, j * nb:(j + 1) * nb])

        uh = ph_ref[:, 0:CONV_CH] * _sig(ph_ref[:, CONV_CH:2 * CONV_CH])
        ubuf[0:HALO, :] = jnp.where(tile_idx > 0, uh, 0.0)
        ubuf[HALO:HALO + tm, :] = p_ref[:, 0:CONV_CH] * _sig(p_ref[:, CONV_CH:2 * CONV_CH])
        gmat = gmat_s[...]
        gain = gain_ref[...]
        yv = ys_ref[...]
        d = yv - _gmean(yv, gmat)
        rs = lax.rsqrt(_gmean(d * d, gmat) + GN_EPS)
        yn = d * rs
        z = yn * gain + bias_ref[...]
        sz = _sig(z)
        dz = dcat_ref[:, 0:CONV_CH] * (sz * (1.0 + z * (1.0 - sz)))
        dyn = dz * gain
        dyc = rs * (dyn - _gmean(dyn, gmat) - yn * _gmean(dyn * yn, gmat))
        s5_ref[0:1, :] += _colsum(dyc)
        s5_ref[1:2, :] += _colsum(dz * yn)
        s5_ref[2:3, :] += _colsum(dz)
        dybuf[tm:tm + HALO, :] = carry[...]
        dybuf[0:tm, :] = dyc
        dsh[SUB:SUB + tm, :] = dyc
        carry[...] = dyc[0:HALO, :]
        for b in range(SUB):
            dshift[...] = dsh[SUB - b:2 * SUB - b + tm, :]
            for j, off in CONV_FWD_TAPS:
                if off % SUB == b:
                    prod = dshift[...] * ubuf[off - b:off - b + tm + SUB, :]
                    dw8[j] += jnp.sum(prod.reshape((tm + SUB) // SUB, SUB, CONV_CH), axis=0)
        for r in range(tm // CONV_ROWS):
            rows = slice(r * CONV_ROWS, (r + 1) * CONV_ROWS)
            for lb_ in range(CONV_CH // LANE):
                lanes = slice(lb_ * LANE, (lb_ + 1) * LANE)
                glanes = slice(CONV_CH + lb_ * LANE, CONV_CH + (lb_ + 1) * LANE)
                w_in_grad_piece(r * (CONV_CH // LANE) + lb_)
                acc = _tap_conv(dybuf, wdw_ref, r * CONV_ROWS, CONV_BWD_TAPS, lanes)
                val = p_ref[rows, lanes]
                sg = _sig(p_ref[rows, glanes])
                dval = acc * sg
                dgate = acc * val * (sg * (1.0 - sg))
                dp_ref[rows, lanes] = dval.astype(BF16)
                dp_ref[rows, glanes] = dgate.astype(BF16)
                sb_ref[0:1, lanes] += _colsum(dval)
                sb_ref[0:1, glanes] += _colsum(dgate)

        o0 = 2 * CONV_CH
        for h in range(N_HEADS):
            sl = slice(h * HEAD_D, (h + 1) * HEAD_D)
            gsl = slice(o0 + 3 * HGRN_W + h * HEAD_D, o0 + 3 * HGRN_W + (h + 1) * HEAD_D)
            oh = o_ref[:, sl]
            gh = p_ref[:, gsl]
            dh = dcat_ref[:, CONV_CH + h * HEAD_D:CONV_CH + (h + 1) * HEAD_D]
            gout = gout_ref[:, sl]
            rsh = lax.rsqrt(jnp.mean(oh * oh, axis=-1, keepdims=True) + RMS_EPS)
            on = oh * rsh
            sgg = _sig(gh)
            dgh = dh * (on * gout) * (sgg * (1.0 + gh * (1.0 - sgg)))
            dm = dh * (gh * sgg)
            s5_ref[3:4, sl] += _colsum(dm * on)
            do_s[:, sl] = _rms_bwd(dm * gout, on, rsh).astype(BF16)
            dp_ref[:, gsl] = dgh.astype(BF16)
            sb_ref[2:3, CONV_CH + h * HEAD_D:CONV_CH + (h + 1) * HEAD_D] += _colsum(dgh)

        lb, _ = _lower_bound(lbl_ref)
        pq = p_ref[:, o0:o0 + HGRN_W]
        pr = _hgrn_prep(pq, p_ref[:, o0 + HGRN_W:o0 + 2 * HGRN_W], lb, lower_s[...])
        qt_s[...] = pr["qt"].astype(BF16)
        kt_s[...] = pr["kt"].astype(BF16)
        kh_s[...] = pr["kh"].astype(BF16)
        v_s[...] = p_ref[:, o0 + 2 * HGRN_W:o0 + 3 * HGRN_W].astype(BF16)
        egl_s[...] = jnp.exp(pr["Gl"])
        tri = _tri()

        def chunk(it, c_):
            ci = nch - 1 - it
            r0 = pl.multiple_of(ci * CHUNK, CHUNK)
            rows = pl.ds(r0, CHUNK)
            for h in range(N_HEADS):
                ls = pl.ds(h * HEAD_D, HEAD_D)
                qc, kc, hc, vc = qt_s[rows, ls], kt_s[rows, ls], kh_s[rows, ls], v_s[rows, ls]
                dob = do_s[rows, ls]
                s0 = st_ref[ci, h]
                s0b = s0.astype(BF16)
                ds1 = dstate[h]
                ds1b = ds1.astype(BF16)
                egl = egl_s[pl.ds(r0, 1), ls]
                att = jnp.where(tri, _dot_nt(qc, kc), 0.0).astype(BF16)
                datt = jnp.where(tri, _dot_nt(dob, vc), 0.0).astype(BF16)
                dv_s[rows, ls] = _dot_tn(att, dob) + _dot_nt(hc, ds1b)
                dqt_s[rows, ls] = _dot(datt, kc) + _dot(dob, s0b)
                dkt_s[rows, ls] = _dot_tn(datt, qc)
                dkh_s[rows, ls] = _dot(vc, ds1b)
                dgl = egl * _colsum(ds1 * s0)
                dgl_s[rows, ls] = jnp.broadcast_to(dgl, (CHUNK, HEAD_D))
                dstate[h] = ds1 * egl + _dot_tn(dob, qc)
            return c_

        lax.fori_loop(0, nch, chunk, 0, unroll=min(CHUNK_UNROLL, nch))
        dqt, dkt, dkh = dqt_s[...], dkt_s[...], dkh_s[...]
        dk = dkt * pr["enG"] + dkh * pr["eGlG"]
        khk = dkh * kh_s[...].astype(F32)
        dG = dqt * qt_s[...].astype(F32) - dkt * kt_s[...].astype(F32) - khk
        dlogf = _mm3(upper_s[...], dG) + _mm3(same_s[...], khk) + dgl_s[...]
        df = dlogf / pr["f"] - dk
        sf, sq = pr["sf"], pr["sq"]
        s5_ref[4:5, :] += _colsum(df * (1.0 - sf))
        dfl = df * (1.0 - lb) * (sf * (1.0 - sf))
        dq = (dqt * pr["eG"]) * (sq * (1.0 + pq * (1.0 - sq)))
        dvv = dv_s[...]
        dp_ref[:, o0:o0 + HGRN_W] = dq.astype(BF16)
        dp_ref[:, o0 + HGRN_W:o0 + 2 * HGRN_W] = dfl.astype(BF16)
        dp_ref[:, o0 + 2 * HGRN_W:o0 + 3 * HGRN_W] = dvv.astype(BF16)
        sb_ref[1:2, 0:HGRN_W] += _colsum(dq)
        sb_ref[1:2, HGRN_W:2 * HGRN_W] += _colsum(dfl)
        sb_ref[2:3, 0:HGRN_W] += _colsum(dvv)

        dp_prev[...] = dp_ref[...]
        h1_prev[...] = h1_ref[...]

        @pl.when(i == nt - 1)
        def _():
            for k in range(n_pieces):
                w_in_grad_piece(k)
            out = pltpu.make_async_copy(gacc, gin_ref, gsem)
            out.start()
            for j in range(N_CHIPS):
                gacc_b[j] = gacc[j].astype(BF16)
            out_b = pltpu.make_async_copy(gacc_b, ginb_ref, gbsem)
            out_b.start()
            for j in range(CONV_K):
                dw_ref[j:j + 1, :] = _colsum(dw8[j])
            copies = _xchg_copies(x_ins, x_outs, xssem, xrsem)
            for cp in copies:
                cp.wait_recv()
            for cp in copies:
                cp.wait_send()
            out.wait()
            out_b.wait()

    rev = lambda cols: pl.BlockSpec((tm, cols), lambda i: (nt - 1 - i, 0))
    halo = pl.BlockSpec((HALO, 2 * CONV_CH), lambda i: (jnp.maximum((nt - 1 - i) * hpt - 1, 0), 0))
    wide = lambda n: pltpu.VMEM((tm, HGRN_W), n)
    hbm = pl.BlockSpec(memory_space=pl.ANY)
    outs = pl.pallas_call(
        body, name="mixers_bwd", grid=(nt,),
        in_specs=[rev(IN_COLS), halo, rev(D_MODEL), rev(CONV_CH), rev(HGRN_W),
                  pl.BlockSpec((nch, N_HEADS, HEAD_D, HEAD_D), lambda i: (nt - 1 - i, 0, 0, 0)), rev(D_MODEL),
                  _full((HALO, CONV_CH))] + [_full((1, CONV_CH))] * 4 + [_full((2, HGRN_W))] + [hbm] * nx,
        out_specs=[rev(IN_COLS), _full((8, D_MODEL)), _full((8, CONV_CH)), _full((HALO, CONV_CH))]
        + [hbm] * (nx + 2),
        out_shape=[_big((T, IN_COLS), BF16), jax.ShapeDtypeStruct((8, D_MODEL), F32),
                   jax.ShapeDtypeStruct((8, CONV_CH), F32), jax.ShapeDtypeStruct((HALO, CONV_CH), F32)]
        + [_big(pb.shape, BF16) for pb in pairs_b]
        + [_big((N_CHIPS, D_MODEL, nb), F32), _big((N_CHIPS, D_MODEL, nb), BF16)],
        scratch_shapes=[pltpu.VMEM((tm + HALO + SUB, CONV_CH), F32), pltpu.VMEM((tm + HALO, CONV_CH), F32),
                        pltpu.VMEM((HALO, CONV_CH), F32), pltpu.VMEM((N_HEADS, HEAD_D, HEAD_D), F32),
                        wide(BF16), wide(BF16), wide(BF16), wide(BF16), wide(BF16),
                        wide(F32), wide(F32), wide(F32), wide(F32), wide(F32), wide(F32),
                        pltpu.VMEM((tm + 2 * SUB, CONV_CH), F32), pltpu.VMEM((HALO, SUB, CONV_CH), F32),
                        pltpu.VMEM((tm + SUB, CONV_CH), F32), pltpu.VMEM((tm, tm), BF16), pltpu.VMEM((tm, tm), BF16),
                        pltpu.VMEM((tm, tm), BF16), pltpu.VMEM((CONV_CH, CONV_CH), BF16),
                        pltpu.VMEM((N_CHIPS, D_MODEL, nb), F32), pltpu.VMEM((N_CHIPS, D_MODEL, nb), BF16),
                        pltpu.VMEM((tm, IN_COLS), BF16), pltpu.VMEM((tm, D_MODEL), BF16),
                        pltpu.SemaphoreType.DMA, pltpu.SemaphoreType.DMA]
        + _xchg_sems(nx),
        compiler_params=_cp(("arbitrary",), 56),
    )(*_hbm(p, p, dcat, ys, o, states, h1), wdw, *vecs, lbl, *[_hbm(pb) for pb in pairs_b])
    return outs[:4], outs[4:4 + nx], (outs[4 + nx], outs[5 + nx])


def _mix_in_bwd(dp, w_in_g, x, dx1, modr, g_pre, pairs_b):
    T = x.shape[0]
    tm = _tok_tile(T)
    nt = T // tm
    nb = IN_COLS // N_CHIPS
    nx = len(pairs_b)

    def body(*refs):
        dp_ref, w_ref, x_ref, dx1_ref, mod_ref, g_ref = refs[:6]
        x_ins = refs[6:6 + nx]
        gx_ref, st_ref = refs[6 + nx:8 + nx]
        x_outs = refs[8 + nx:8 + 2 * nx]
        xssem, xrsem = refs[8 + 2 * nx:]
        i = pl.program_id(0)

        @pl.when(i == 0)
        def _():
            for cp in _xchg_copies(x_ins, x_outs, xssem, xrsem):
                cp.start()
            st_ref[...] = jnp.zeros(st_ref.shape, F32)

        dh = None
        for j in range(N_CHIPS):
            part = _dot_nt(dp_ref[:, j * nb:(j + 1) * nb], w_ref[j])
            dh = part if dh is None else dh + part
        xv = x_ref[...]
        rs = lax.rsqrt(jnp.mean(xv * xv, axis=-1, keepdims=True) + RMS_EPS)
        xn = xv * rs
        st_ref[0:1, :] += _colsum(dh)
        st_ref[1:2, :] += _colsum(dh * (xn * g_ref[...]))
        dsc = dh * (1.0 + mod_ref[1:2, :])
        st_ref[2:3, :] += _colsum(dsc * xn)
        gx_ref[...] = dx1_ref[...] + _rms_bwd(dsc * g_ref[...], xn, rs)

        @pl.when(i == nt - 1)
        def _():
            copies = _xchg_copies(x_ins, x_outs, xssem, xrsem)
            for cp in copies:
                cp.wait_recv()
            for cp in copies:
                cp.wait_send()

    tile = pl.BlockSpec((tm, D_MODEL), lambda i: (i, 0))
    hbm = pl.BlockSpec(memory_space=pl.ANY)
    outs = pl.pallas_call(
        body, name="mix_in_bwd", grid=(nt,),
        in_specs=[pl.BlockSpec((tm, IN_COLS), lambda i: (i, 0)), _full((N_CHIPS, D_MODEL, nb)), tile, tile,
                  _full((6, D_MODEL)), _full((1, D_MODEL))] + [hbm] * nx,
        out_specs=[tile, _full((8, D_MODEL))] + [hbm] * nx,
        out_shape=[_big((T, D_MODEL), F32), jax.ShapeDtypeStruct((8, D_MODEL), F32)]
        + [_big(pb.shape, BF16) for pb in pairs_b],
        scratch_shapes=_xchg_sems(nx),
        compiler_params=_cp(("arbitrary",), 48),
    )(*_hbm(dp, w_in_g, x, dx1), modr, g_pre, *[_hbm(pb) for pb in pairs_b])
    return outs[:2], outs[2:]


def _weight_grad(a, b, a_blocked, b_blocked, name):
    T = a.shape[0]
    tt = min(GRAD_TILE, T)
    nt = T // tt
    ka = a.shape[1] // N_CHIPS if a_blocked else a.shape[1]
    nb = b.shape[1] // N_CHIPS if b_blocked else b.shape[1]

    def body(a_ref, b_ref, o_ref, ob_ref):
        t = pl.program_id(1)

        @pl.when(t == 0)
        def _():
            o_ref[...] = jnp.zeros(o_ref.shape, F32)

        for cols in _row_chains(nb):
            o_ref[0, :, cols] += _dot_tn(a_ref[...], b_ref[:, cols])

        @pl.when(t == nt - 1)
        def _():
            ob_ref[0] = o_ref[0].astype(BF16)

    blk = pl.BlockSpec((1, ka, nb), lambda j, t: (j, 0, 0))
    return pl.pallas_call(
        body, name=name, grid=(N_CHIPS, nt),
        in_specs=[pl.BlockSpec((tt, ka), (lambda j, t: (t, j)) if a_blocked else (lambda j, t: (t, 0))),
                  pl.BlockSpec((tt, nb), (lambda j, t: (t, j)) if b_blocked else (lambda j, t: (t, 0)))],
        out_specs=[blk, blk],
        out_shape=[_big((N_CHIPS, ka, nb), F32), _big((N_CHIPS, ka, nb), BF16)],
        compiler_params=_cp(("arbitrary", "arbitrary"), 48),
    )(*_hbm(a, b))


R_LOSS = 0
R_FFN = 8
R_OUT = 16
R_IN = 24
R_BIN = 32
R_512 = 40
R_DW = 48
N_STAT_ROWS = 80
MOD_ROWS = (R_IN + 0, R_IN + 1, R_OUT + 0, R_FFN + 0, R_FFN + 1, R_LOSS + 1)


def _small_update(gath, params):
    names = ["b_ada", "lb_logits", "g_pre_mix", "b_in", "b_dw", "gn_gain", "gn_bias", "g_hgrn_out", "g_post_mix",
             "g_pre_ffn", "g_post_ffn"]
    flat = []
    for n in names:
        flat += list(params[n])
    n_in = 1 + len(flat)

    def body(*refs):
        g_ref = refs[0]
        prm = {n: refs[1 + 3 * k:4 + 3 * k] for k, n in enumerate(names)}
        outs = refs[n_in:]
        loss_ref, dmod_ref, dwdw_ref = outs[0], outs[1], outs[2]
        res = {n: outs[3 + 4 * k:7 + 4 * k] for k, n in enumerate(names)}
        red = g_ref[0]
        for dev in range(1, N_DEV):
            red = red + g_ref[dev]
        loss_ref[...] = jnp.broadcast_to(
            (0.5 / D_MODEL) * jnp.sum(red[R_LOSS:R_LOSS + 1, :], axis=-1, keepdims=True), loss_ref.shape)
        for dev in range(N_DEV):
            for k, r in enumerate(MOD_ROWS):
                dmod_ref[dev:dev + 1, k * D_MODEL:(k + 1) * D_MODEL] = g_ref[dev, r:r + 1, :]
        dwdw_ref[...] = red[R_DW:R_DW + HALO, 0:CONV_CH]

        def finish(name, pieces):
            w_ref, m_ref, v_ref = prm[name]
            g_out, d_out, m_out, v_out = res[name]
            for rsl, lsl, g in pieces:
                d, m2, v2 = _adam_math(w_ref[rsl, lsl], g, m_ref[rsl, lsl], v_ref[rsl, lsl])
                g_out[rsl, lsl] = g
                d_out[rsl, lsl] = d
                m_out[rsl, lsl] = m2
                v_out[rsl, lsl] = v2

        one = slice(0, 1)
        row = lambda r: red[r:r + 1, :]
        half = lambda r: red[r:r + 1, 0:CONV_CH]
        finish("b_ada", [(one, slice(k * D_MODEL, (k + 1) * D_MODEL), row(r)) for k, r in enumerate(MOD_ROWS)])
        finish("b_in", [(one, slice(k * D_MODEL, (k + 1) * D_MODEL), row(R_BIN + k)) for k in range(3)])
        finish("g_pre_mix", [(one, slice(None), row(R_IN + 2))])
        finish("g_post_mix", [(one, slice(None), row(R_OUT + 1))])
        finish("g_pre_ffn", [(one, slice(None), row(R_FFN + 2))])
        finish("g_post_ffn", [(one, slice(None), row(R_LOSS + 2))])
        finish("b_dw", [(one, slice(None), half(R_512 + 0))])
        finish("gn_gain", [(one, slice(None), half(R_512 + 1))])
        finish("gn_bias", [(one, slice(None), half(R_512 + 2))])
        finish("g_hgrn_out", [(one, slice(None), half(R_512 + 3))])
        s0, s1 = _lower_bound(prm["lb_logits"][0])
        dlb = half(R_512 + 4)
        finish("lb_logits", [(slice(0, 1), slice(None), dlb * s0 * (1.0 - s0)),
                             (slice(1, 2), slice(None), -dlb * s0 * s1)])

    vm = pl.BlockSpec(memory_space=pltpu.VMEM)
    out_shape = [jax.ShapeDtypeStruct((8, 128), F32), jax.ShapeDtypeStruct((N_DEV, 6 * D_MODEL), F32),
                 jax.ShapeDtypeStruct((HALO, CONV_CH), F32)]
    for n in names:
        out_shape += [jax.ShapeDtypeStruct(params[n][0].shape, F32)] * 4
    outs = pl.pallas_call(
        body, name="small_update", out_shape=out_shape,
        in_specs=[vm] * n_in, out_specs=[vm] * len(out_shape),
        compiler_params=_cp(None, 32),
    )(gath, *flat)
    return outs[0], outs[1], outs[2], {n: outs[3 + 4 * k:7 + 4 * k] for k, n in enumerate(names)}


def _wdw_adam(w, g, m, v):
    def body(w_ref, g_ref, m_ref, v_ref, d_out, m_out, v_out):
        d, m2, v2 = _adam_math(w_ref[...], g_ref[...], m_ref[...], v_ref[...])
        d_out[...] = d
        m_out[...] = m2
        v_out[...] = v2

    vm = pl.BlockSpec(memory_space=pltpu.VMEM)
    return pl.pallas_call(
        body, name="wdw_adam", out_shape=[jax.ShapeDtypeStruct(w.shape, F32)] * 3,
        in_specs=[vm] * 4, out_specs=[vm] * 3, compiler_params=_cp(None, 16),
    )(w, g, m, v)


def kernel(x, c, w_ada, b_ada, lb_logits, g_pre_mix, w_in, b_in, w_dw, b_dw, gn_gain, gn_bias, g_hgrn_out, w_out, g_post_mix, g_pre_ffn, w_up, w_down, g_post_ffn, loss_target, m_w_ada, m_b_ada, m_lb_logits, m_g_pre_mix, m_w_in, m_b_in, m_w_dw, m_b_dw, m_gn_gain, m_gn_bias, m_g_hgrn_out, m_w_out, m_g_post_mix, m_g_pre_ffn, m_w_up, m_w_down, m_g_post_ffn, v_w_ada, v_b_ada, v_lb_logits, v_g_pre_mix, v_w_in, v_b_in, v_w_dw, v_b_dw, v_gn_gain, v_gn_bias, v_g_hgrn_out, v_w_out, v_g_post_mix, v_g_pre_ffn, v_w_up, v_w_down, v_g_post_ffn):
    ax, ay, ac = lax.axis_index("x"), lax.axis_index("y"), lax.axis_index("c")
    chip = 2 * ax + ay
    T = x.shape[1]
    xs, tgt = x[0], loss_target[0]
    ada_cols = w_ada.shape[2]

    b_sh = lax.dynamic_slice_in_dim(b_ada, chip * ada_cols, ada_cols, axis=1)
    wdw_pad = jnp.pad(w_dw[0], ((0, HALO - CONV_K), (0, 0)))
    chip1 = jnp.reshape(chip, (1,)).astype(jnp.int32)
    place = jnp.stack([ac, chip]).astype(jnp.int32)
    _, c8, modg, wdwg = _ada_exchange(c, w_ada[0], b_sh, wdw_pad)
    modr = modg.reshape(6, D_MODEL)
    wdw_all = jnp.transpose(wdwg, (1, 0, 2)).reshape(HALO, CONV_CH)
    vec = (b_dw, gn_gain, gn_bias, g_hgrn_out)

    p, h1, (up_buf, down_buf), w_in_g, w_out_g = _mix_in_fwd(
        chip1, xs, modr, g_pre_mix, b_in.reshape(N_CHIPS, 1, IN_COLS // N_CHIPS), [w_up[0], w_down[0]],
        _cast_own(chip1, w_in[0], "cast_w_in"), _cast_own(chip1, w_out[0], "cast_w_out"))
    w_out_f = w_out_g.reshape(D_MODEL, D_MODEL)
    (cat, ys, o, states, y, x1, h2), (w_up_g, w_down_g) = _mixers_fwd(
        p, wdw_all, vec, lb_logits, w_out_f, xs, modr, g_post_mix, g_pre_ffn, [up_buf, down_buf])
    w_down_f = w_down_g.reshape(D_FF, D_MODEL)
    r, dy2, dx2, st_loss = _ffn_fwd(h2, w_up_g, w_down_f, x1, tgt, modr, g_post_ffn)

    def pair_sums(grads, got, tags):
        return [_pair_sum(place, g, o_, "pair_sum_" + t) for (g, _), o_, t in zip(grads, got, tags)]

    da, dx1, st_ffn = _ffn_bwd(dy2, r, x1, dx2, w_up_g, w_down_f, modr, g_pre_ffn)
    g_up = _weight_grad(h2, da, False, True, "grad_w_up")
    g_down = _weight_grad(r, dy2, True, False, "grad_w_down")
    (dcat, st_out, g_out), got_ud = _mix_out_bwd(dx1, y, cat, w_out_f, modr, g_post_mix, [g_up[1], g_down[1]])
    g_out = g_out.reshape(N_CHIPS, D_MODEL // N_CHIPS, D_MODEL)
    got_o = _pair_swap([g_out], "pair_swap_w_out")
    early = pair_sums([(g_out, None), g_up, g_down], list(got_o) + list(got_ud), ["w_out", "w_up", "w_down"])
    (dp, st_bin, st_512, dwdw), got_early, g_in = _mixers_bwd(p, dcat, ys, o, states, h1, wdw_all, vec, lb_logits,
                                                              [pb for _, pb in early])
    late = pair_sums([g_in], _pair_swap([g_in[1]], "pair_swap_w_in"), ["w_in"])
    (grad_x, st_in), got_late = _mix_in_bwd(dp, w_in_g, xs, dx1, modr, g_pre_mix, [late[0][1]])
    fulls = [_chip_sum(place, pf, gb, "chip_sum_" + t)
             for (pf, _), gb, t in zip(late + early, list(got_late) + list(got_early), ["w_in", "w_out", "w_up", "w_down"])]

    pad_lanes = lambda s: jnp.pad(s, ((0, 0), (0, D_MODEL - s.shape[1])))
    stats = jnp.concatenate([st_loss, st_ffn, st_out, st_in, st_bin, pad_lanes(st_512), pad_lanes(dwdw)], axis=0)
    (g_w_in, g_w_out, g_w_up, g_w_down), gath = _final_exchange(fulls, stats)
    small = {"b_ada": (b_ada, m_b_ada, v_b_ada), "lb_logits": (lb_logits, m_lb_logits, v_lb_logits),
             "g_pre_mix": (g_pre_mix, m_g_pre_mix, v_g_pre_mix), "b_in": (b_in, m_b_in, v_b_in),
             "b_dw": (b_dw, m_b_dw, v_b_dw), "gn_gain": (gn_gain, m_gn_gain, v_gn_gain),
             "gn_bias": (gn_bias, m_gn_bias, v_gn_bias), "g_hgrn_out": (g_hgrn_out, m_g_hgrn_out, v_g_hgrn_out),
             "g_post_mix": (g_post_mix, m_g_post_mix, v_g_post_mix), "g_pre_ffn": (g_pre_ffn, m_g_pre_ffn, v_g_pre_ffn),
             "g_post_ffn": (g_post_ffn, m_g_post_ffn, v_g_post_ffn)}
    loss_t, dmod_all, dwdw_sum, sres = _small_update(gath, small)
    loss = loss_t[0, 0]

    res = dict(sres)
    dmod_sh = lax.dynamic_slice_in_dim(dmod_all, chip * ada_cols, ada_cols, axis=1)
    res["w_ada"] = [t[None] for t in _ada_grad_adam(jnp.transpose(c8), dmod_sh, w_ada[0], m_w_ada[0], v_w_ada[0])]
    g_wdw = lax.dynamic_slice_in_dim(dwdw_sum, chip * HEAD_D, HEAD_D, axis=1)[:CONV_K][None]
    res["w_dw"] = [g_wdw] + list(_wdw_adam(w_dw, g_wdw, m_w_dw, v_w_dw))
    for name, g, w, m, v in (("w_in", g_w_in, w_in, m_w_in, v_w_in), ("w_out", g_w_out, w_out, m_w_out, v_w_out),
                             ("w_up", g_w_up, w_up, m_w_up, v_w_up), ("w_down", g_w_down, w_down, m_w_down, v_w_down)):
        d, m2, v2 = _adam_big(w[0], g, m[0], v[0], "adam_" + name)
        res[name] = [g[None], d[None], m2[None], v2[None]]

    order = ["w_ada", "b_ada", "lb_logits", "g_pre_mix", "w_in", "b_in", "w_dw", "b_dw", "gn_gain", "gn_bias",
             "g_hgrn_out", "w_out", "g_post_mix", "g_pre_ffn", "w_up", "w_down", "g_post_ffn"]
    out = [loss, grad_x[None]]
    for k in range(4):
        out += [res[n][k] for n in order]
    return tuple(out)
```

```python
import jax
import jax.numpy as jnp
from jax import lax
from jax.experimental import pallas as pl
from jax.experimental.pallas import tpu as pltpu

F32, BF16 = jnp.float32, jnp.bfloat16
D_MODEL = 1024
CONV_CH = 512
HGRN_W = 512
N_HEADS = 4
HEAD_D = 128
CONV_K = 31
GN_GROUP = 64
GN_SHIFT = 6
IN_COLS = 3072
D_FF = 4096
CHUNK = 64
CHUNK_SHIFT = 6
N_CHIPS = 4
N_DEV = 8
RMS_EPS = 1e-6
GN_EPS = 1e-5
ADAM_LR, ADAM_B1, ADAM_B2, ADAM_EPS, ADAM_WD, ADAM_STEP = 0.001, 0.9, 0.999, 1e-08, 0.01, 10
TOK_TILE = 512
MIXIN_TILE = 1024
MIXB_TILE = 256
FFN_TILE = 1024
FFN_BLOCK = 512
GRAD_TILE = 2048
HALO = 32
SUB = 8
LANE = 128
CONV_ROWS = 128
CHUNK_UNROLL = 8
MIB = 1 << 20
MESH = pl.DeviceIdType.MESH
OTHER_CHIPS = ((0, 1), (1, 0), (1, 1))


def _cp(sem=None, vmem_mib=48):
    return pltpu.CompilerParams(dimension_semantics=sem, vmem_limit_bytes=vmem_mib * MIB)


def _dot(a, b):
    return jnp.dot(a, b, preferred_element_type=F32)


def _dot_nt(a, b):
    return lax.dot_general(a, b, (((1,), (1,)), ((), ())), preferred_element_type=F32)


def _dot_tn(a, b):
    return lax.dot_general(a, b, (((0,), (0,)), ((), ())), preferred_element_type=F32)


def _sig(v):
    return 0.5 * jnp.tanh(0.5 * v) + 0.5


def _colsum(v):
    return jnp.sum(v, axis=0, keepdims=True)


def _flip(v, b):
    return 1 - v if b else v


def _rcopy(src, dst, ssem, rsem, dev):
    return pltpu.make_async_remote_copy(src_ref=src, dst_ref=dst, send_sem=ssem, recv_sem=rsem,
                                        device_id=dev, device_id_type=MESH)


def _place():
    return lax.axis_index("x"), lax.axis_index("y"), lax.axis_index("c")


def _full(shape):
    return pl.BlockSpec(shape, lambda *_: (0,) * len(shape))


def _big(shape, dtype):
    return pltpu.HBM(shape, dtype)


def _hbm(*arrays):
    out = [pltpu.with_memory_space_constraint(a, pltpu.HBM) for a in arrays]
    return out[0] if len(out) == 1 else out


def _split2(v):
    hi = v.astype(BF16)
    lo = (v - hi.astype(F32)).astype(BF16)
    return hi, lo


def _split3(v):
    h1 = v.astype(BF16)
    r1 = v - h1.astype(F32)
    h2 = r1.astype(BF16)
    h3 = (r1 - h2.astype(F32)).astype(BF16)
    return h1, h2, h3


def _mm3(mat, v):
    h1, h2, h3 = _split3(v)
    return _dot(mat, h1) + _dot(mat, h2) + _dot(mat, h3)


def _gn_matrix():
    r = lax.broadcasted_iota(jnp.int32, (CONV_CH, CONV_CH), 0) >> GN_SHIFT
    c = lax.broadcasted_iota(jnp.int32, (CONV_CH, CONV_CH), 1) >> GN_SHIFT
    return jnp.where(r == c, 1.0 / GN_GROUP, 0.0).astype(BF16)


def _gmean(v, gmat):
    hi, lo = _split2(v)
    return _dot(hi, gmat) + _dot(lo, gmat)


def _chunk_masks(tm):
    r = lax.broadcasted_iota(jnp.int32, (tm, tm), 0)
    c = lax.broadcasted_iota(jnp.int32, (tm, tm), 1)
    same = (r >> CHUNK_SHIFT) == (c >> CHUNK_SHIFT)
    one = lambda m: jnp.where(m, 1.0, 0.0).astype(BF16)
    return one(same & (c <= r)), one(same & (c >= r)), one(same)


def _tri():
    return lax.broadcasted_iota(jnp.int32, (CHUNK, CHUNK), 0) >= lax.broadcasted_iota(jnp.int32, (CHUNK, CHUNK), 1)


def _lower_bound(lbl_ref):
    l0, l1 = lbl_ref[0:1, :], lbl_ref[1:2, :]
    mx = jnp.maximum(l0, l1)
    e0, e1 = jnp.exp(l0 - mx), jnp.exp(l1 - mx)
    return e0 / (e0 + e1), e1 / (e0 + e1)


CONV_FWD_TAPS = tuple((j, HALO - (CONV_K - 1) + j) for j in range(CONV_K))
CONV_BWD_TAPS = tuple((j, (CONV_K - 1) - j) for j in range(CONV_K))


def _tap_conv(src_ref, w_ref, row0, taps, lanes):
    acc = None
    for b in range(SUB):
        pb = None
        for j, off in taps:
            if off % SUB == b:
                lo = row0 + off - b
                term = w_ref[j:j + 1, lanes] * src_ref[lo:lo + CONV_ROWS + SUB, lanes]
                pb = term if pb is None else pb + term
        if pb is not None:
            sh = pb[b:b + CONV_ROWS, :]
            acc = sh if acc is None else acc + sh
    return acc


def _hgrn_prep(pq, pf, lb, lower):
    sq = _sig(pq)
    qf = pq * sq
    sf = _sig(pf)
    f = lb + (1.0 - lb) * sf
    logf = jnp.log(f)
    k = 1.0 - f
    G = _mm3(lower, logf)
    rows, cols = G.shape
    g3 = G.reshape(rows // CHUNK, CHUNK, cols)
    Gl = jnp.broadcast_to(g3[:, CHUNK - 1:CHUNK, :], g3.shape).reshape(rows, cols)
    eG, enG, eGlG = jnp.exp(G), jnp.exp(-G), jnp.exp(Gl - G)
    return dict(sq=sq, sf=sf, f=f, Gl=Gl, eG=eG, enG=enG, eGlG=eGlG, qt=qf * eG, kt=k * enG, kh=k * eGlG)


def _ada_exchange(c_row, w_ada, b_sh, wdw_pad):
    ncol = w_ada.shape[1]

    def body(c_ref, w_ref, b_ref, wdw_ref, call_ref, c8_ref, modg_ref, wdwg_ref, rows_s, sa, ra, sw, rw, sm, rm):
        x, y, c = _place()
        me = 4 * x + 2 * y + c
        chip = 2 * x + y
        cv = c_ref[...]
        call_ref[me] = cv * _sig(cv)
        wdwg_ref[chip] = wdw_ref[...]
        sends = []
        for m in range(1, N_DEV):
            peer = (_flip(x, m >> 2), _flip(y, (m >> 1) & 1), _flip(c, m & 1))
            cp = _rcopy(call_ref.at[me], call_ref.at[me], sa.at[m - 1], ra.at[m - 1], peer)
            cp.start()
            sends.append(cp)
        for k, (fx, fy) in enumerate(OTHER_CHIPS):
            peer = (_flip(x, fx), _flip(y, fy), c)
            cp = _rcopy(wdwg_ref.at[chip], wdwg_ref.at[chip], sw.at[k], rw.at[k], peer)
            cp.start()
            sends.append(cp)
        for m in range(1, N_DEV):
            peer = (_flip(x, m >> 2), _flip(y, (m >> 1) & 1), _flip(c, m & 1))
            pid = 4 * peer[0] + 2 * peer[1] + peer[2]
            _rcopy(call_ref.at[pid], call_ref.at[pid], sa.at[m - 1], ra.at[m - 1], peer).wait_recv()
        for b in range(N_DEV):
            c8_ref[b:b + 1, :] = call_ref[b]
        mod_all = _dot(c8_ref[...].astype(BF16), w_ref[...].astype(BF16)) + b_ref[...]
        for b in range(N_DEV):
            rows_s[b] = mod_all[b:b + 1, :]
        modg_ref[chip] = rows_s[me]
        for k, (fx, fy) in enumerate(OTHER_CHIPS):
            peer = (_flip(x, fx), _flip(y, fy), c)
            pid = 4 * peer[0] + 2 * peer[1] + peer[2]
            cp = _rcopy(rows_s.at[pid], modg_ref.at[chip], sm.at[k], rm.at[k], peer)
            cp.start()
            sends.append(cp)
        for k, (fx, fy) in enumerate(OTHER_CHIPS):
            peer = (_flip(x, fx), _flip(y, fy), c)
            pchip = 2 * peer[0] + peer[1]
            _rcopy(rows_s.at[0], modg_ref.at[pchip], sm.at[k], rm.at[k], peer).wait_recv()
            _rcopy(wdwg_ref.at[pchip], wdwg_ref.at[pchip], sw.at[k], rw.at[k], peer).wait_recv()
        for cp in sends:
            cp.wait_send()

    vm = pl.BlockSpec(memory_space=pltpu.VMEM)
    return pl.pallas_call(
        body, name="ada_exchange",
        out_shape=[jax.ShapeDtypeStruct((N_DEV, 1, D_MODEL), F32), jax.ShapeDtypeStruct((N_DEV, D_MODEL), F32),
                   jax.ShapeDtypeStruct((N_CHIPS, 1, ncol), F32), jax.ShapeDtypeStruct((N_CHIPS, HALO, HEAD_D), F32)],
        in_specs=[vm] * 4, out_specs=[vm] * 4,
        scratch_shapes=[pltpu.VMEM((N_DEV, 1, ncol), F32),
                        pltpu.SemaphoreType.DMA((N_DEV - 1,)), pltpu.SemaphoreType.DMA((N_DEV - 1,)),
                        pltpu.SemaphoreType.DMA((3,)), pltpu.SemaphoreType.DMA((3,)),
                        pltpu.SemaphoreType.DMA((3,)), pltpu.SemaphoreType.DMA((3,))],
        compiler_params=_cp(None, 32),
    )(c_row, w_ada, b_sh, wdw_pad)


def _cast_own(chip1, shard, name):
    rows, cols = shard.shape
    tr = _row_tile(rows)

    def body(ch_ref, s_ref, o_ref):
        o_ref[0] = s_ref[...].astype(BF16)

    gs = pltpu.PrefetchScalarGridSpec(
        num_scalar_prefetch=1, grid=(rows // tr,),
        in_specs=[pl.BlockSpec((tr, cols), lambda i, ch: (i, 0))],
        out_specs=pl.BlockSpec((1, tr, cols), lambda i, ch: (ch[0], i, 0)))
    return pl.pallas_call(
        body, name=name, grid_spec=gs, out_shape=_big((N_CHIPS, rows, cols), BF16),
        compiler_params=_cp(("arbitrary",), 32),
    )(chip1, _hbm(shard))


def _slab(buf, ch, core):
    hs = buf.shape[1] // 2
    return buf.at[ch, pl.ds(core * hs, hs), :]


def _gather_start(bufs, ssem, rsem, relations=(0, 1, 2)):
    x, y, c = _place()
    chip = 2 * x + y
    for k in relations:
        fx, fy = OTHER_CHIPS[k]
        peer = (_flip(x, fx), _flip(y, fy), c)
        for t, buf in enumerate(bufs):
            _rcopy(_slab(buf, chip, c), _slab(buf, chip, c), ssem.at[t * 3 + k], rsem.at[t * 3 + k], peer).start()


def _gather_pass_on(bufs, ssem, rsem):
    nt = len(bufs)
    x, y, c = _place()
    sibling = (x, y, 1 - c)
    for k, (fx, fy) in enumerate(OTHER_CHIPS):
        peer = (_flip(x, fx), _flip(y, fy), c)
        pchip = 2 * peer[0] + peer[1]
        for t, buf in enumerate(bufs):
            _rcopy(_slab(buf, pchip, c), _slab(buf, pchip, c), ssem.at[t * 3 + k], rsem.at[t * 3 + k], peer).wait_recv()
            _rcopy(_slab(buf, pchip, c), _slab(buf, pchip, c), ssem.at[3 * nt + t * 3 + k],
                   rsem.at[3 * nt + t * 3 + k], sibling).start()


def _gather_drain(bufs, ssem, rsem):
    nt = len(bufs)
    x, y, c = _place()
    chip = 2 * x + y
    sibling = (x, y, 1 - c)
    for k, (fx, fy) in enumerate(OTHER_CHIPS):
        peer = (_flip(x, fx), _flip(y, fy), c)
        pchip = 2 * peer[0] + peer[1]
        for t, buf in enumerate(bufs):
            _rcopy(_slab(buf, pchip, 1 - c), _slab(buf, pchip, 1 - c), ssem.at[3 * nt + t * 3 + k],
                   rsem.at[3 * nt + t * 3 + k], sibling).wait_recv()
            _rcopy(_slab(buf, chip, c), _slab(buf, chip, c), ssem.at[t * 3 + k], rsem.at[t * 3 + k], peer).wait_send()
            _rcopy(_slab(buf, pchip, c), _slab(buf, pchip, c), ssem.at[3 * nt + t * 3 + k],
                   rsem.at[3 * nt + t * 3 + k], sibling).wait_send()


def _gather_finish(bufs, ssem, rsem):
    _gather_pass_on(bufs, ssem, rsem)
    _gather_drain(bufs, ssem, rsem)


def _gather_arrive(bufs, k, ssem, rsem):
    nt = len(bufs)
    x, y, c = _place()
    fx, fy = OTHER_CHIPS[k]
    peer = (_flip(x, fx), _flip(y, fy), c)
    pchip = 2 * peer[0] + peer[1]
    for t, buf in enumerate(bufs):
        _rcopy(_slab(buf, pchip, c), _slab(buf, pchip, c), ssem.at[t * 3 + k], rsem.at[t * 3 + k], peer).wait_recv()
        _rcopy(_slab(buf, pchip, c), _slab(buf, pchip, c), ssem.at[3 * nt + t * 3 + k],
               rsem.at[3 * nt + t * 3 + k], (x, y, 1 - c)).start()
    for t, buf in enumerate(bufs):
        _rcopy(_slab(buf, pchip, 1 - c), _slab(buf, pchip, 1 - c), ssem.at[3 * nt + t * 3 + k],
               rsem.at[3 * nt + t * 3 + k], (x, y, 1 - c)).wait_recv()


def _gather_sends_done(bufs, ssem, rsem):
    nt = len(bufs)
    x, y, c = _place()
    chip = 2 * x + y
    for k, (fx, fy) in enumerate(OTHER_CHIPS):
        peer = (_flip(x, fx), _flip(y, fy), c)
        pchip = 2 * peer[0] + peer[1]
        for t, buf in enumerate(bufs):
            _rcopy(_slab(buf, chip, c), _slab(buf, chip, c), ssem.at[t * 3 + k], rsem.at[t * 3 + k], peer).wait_send()
            _rcopy(_slab(buf, pchip, c), _slab(buf, pchip, c), ssem.at[3 * nt + t * 3 + k],
                   rsem.at[3 * nt + t * 3 + k], (x, y, 1 - c)).wait_send()


def _ring_parts(buf):
    x, y, c = _place()
    ynb, xnb = (x, 1 - y, c), (1 - x, y, c)
    ychip, xchip, dchip = 2 * x + (1 - y), 2 * (1 - x) + y, 2 * (1 - x) + (1 - y)
    hs = buf.shape[1] // 2

    def piece(ch, q):
        return buf.at[ch, pl.ds(c * hs + q * (hs // 2), hs // 2), :]

    return ynb, xnb, ychip, xchip, dchip, piece


def _ring_start(bufs, ssem, rsem):
    x, y, c = _place()
    chip = 2 * x + y
    for t, buf in enumerate(bufs):
        ynb, xnb, _, _, _, _ = _ring_parts(buf)
        _rcopy(_slab(buf, chip, c), _slab(buf, chip, c), ssem.at[2 * t], rsem.at[2 * t], ynb).start()
        _rcopy(_slab(buf, chip, c), _slab(buf, chip, c), ssem.at[2 * t + 1], rsem.at[2 * t + 1], xnb).start()


def _ring_forward(bufs, ssem, rsem):
    nt = len(bufs)
    _, _, c = _place()
    for t, buf in enumerate(bufs):
        ynb, xnb, ychip, xchip, _, piece = _ring_parts(buf)
        _rcopy(_slab(buf, ychip, c), _slab(buf, ychip, c), ssem.at[2 * t], rsem.at[2 * t], ynb).wait_recv()
        _rcopy(piece(ychip, 0), piece(ychip, 0), ssem.at[2 * nt + 2 * t], rsem.at[2 * nt + 2 * t], xnb).start()
        _rcopy(_slab(buf, xchip, c), _slab(buf, xchip, c), ssem.at[2 * t + 1], rsem.at[2 * t + 1], xnb).wait_recv()
        _rcopy(piece(xchip, 1), piece(xchip, 1), ssem.at[2 * nt + 2 * t + 1], rsem.at[2 * nt + 2 * t + 1], ynb).start()


def _ring_finish(bufs, ssem, rsem):
    nt = len(bufs)
    x, y, c = _place()
    chip = 2 * x + y
    sibling = (x, y, 1 - c)
    for t, buf in enumerate(bufs):
        ynb, xnb, ychip, xchip, dchip, piece = _ring_parts(buf)
        _rcopy(piece(dchip, 0), piece(dchip, 0), ssem.at[2 * nt + 2 * t], rsem.at[2 * nt + 2 * t], xnb).wait_recv()
        _rcopy(piece(dchip, 1), piece(dchip, 1), ssem.at[2 * nt + 2 * t + 1], rsem.at[2 * nt + 2 * t + 1],
               ynb).wait_recv()
        for k, ch in enumerate((ychip, xchip, dchip)):
            _rcopy(_slab(buf, ch, c), _slab(buf, ch, c), ssem.at[4 * nt + 3 * t + k], rsem.at[4 * nt + 3 * t + k],
                   sibling).start()
    for t, buf in enumerate(bufs):
        ynb, xnb, ychip, xchip, dchip, piece = _ring_parts(buf)
        for k, ch in enumerate((ychip, xchip, dchip)):
            _rcopy(_slab(buf, ch, 1 - c), _slab(buf, ch, 1 - c), ssem.at[4 * nt + 3 * t + k],
                   rsem.at[4 * nt + 3 * t + k], sibling).wait_recv()
            _rcopy(_slab(buf, ch, c), _slab(buf, ch, c), ssem.at[4 * nt + 3 * t + k], rsem.at[4 * nt + 3 * t + k],
                   sibling).wait_send()
        _rcopy(_slab(buf, chip, c), _slab(buf, chip, c), ssem.at[2 * t], rsem.at[2 * t], ynb).wait_send()
        _rcopy(_slab(buf, chip, c), _slab(buf, chip, c), ssem.at[2 * t + 1], rsem.at[2 * t + 1], xnb).wait_send()
        _rcopy(piece(ychip, 0), piece(ychip, 0), ssem.at[2 * nt + 2 * t], rsem.at[2 * nt + 2 * t], xnb).wait_send()
        _rcopy(piece(xchip, 1), piece(xchip, 1), ssem.at[2 * nt + 2 * t + 1], rsem.at[2 * nt + 2 * t + 1],
               ynb).wait_send()


def _ring_sems(nt):
    return [pltpu.SemaphoreType.DMA((7 * nt,)), pltpu.SemaphoreType.DMA((7 * nt,))]


def _gather_sems(nt):
    return [pltpu.SemaphoreType.DMA((6 * nt,)), pltpu.SemaphoreType.DMA((6 * nt,))]


def _pair_copies(ins, outs, ssem, rsem):
    x, y, c = _place()
    copies = []
    for t in range(len(ins)):
        hs = ins[t].shape[1] // 2
        copies.append(_rcopy(ins[t].at[:, pl.ds((1 - c) * hs, hs), :], outs[t], ssem.at[t], rsem.at[t], (x, y, 1 - c)))
    return copies


def _pair_shapes(grads):
    return [_big((g.shape[0], g.shape[1] // 2, g.shape[2]), g.dtype) for g in grads]


def _pair_sems(nt):
    return [pltpu.SemaphoreType.DMA((nt,)), pltpu.SemaphoreType.DMA((nt,))]


def _pair_swap(grads, name):
    nt = len(grads)
    hbm = pl.BlockSpec(memory_space=pl.ANY)

    def body(*refs):
        copies = _pair_copies(refs[:nt], refs[nt:2 * nt], refs[2 * nt], refs[2 * nt + 1])
        for cp in copies:
            cp.start()
        for cp in copies:
            cp.wait_recv()
        for cp in copies:
            cp.wait_send()

    return pl.pallas_call(
        body, name=name, out_shape=_pair_shapes(grads), in_specs=[hbm] * nt, out_specs=[hbm] * nt,
        scratch_shapes=_pair_sems(nt),
    )(*[_hbm(g) for g in grads])


def _xchg_copies(ins, outs, ssem, rsem):
    x, y, c = _place()
    copies = []
    for k, (fx, fy) in enumerate(OTHER_CHIPS):
        peer = (_flip(x, fx), _flip(y, fy), c)
        for t in range(len(ins)):
            copies.append(_rcopy(ins[t].at[k], outs[t].at[k], ssem.at[t * 3 + k], rsem.at[t * 3 + k], peer))
    return copies


def _xchg_sems(nt):
    return [pltpu.SemaphoreType.DMA((3 * nt,)), pltpu.SemaphoreType.DMA((3 * nt,))]


def _final_exchange(fulls, stats):
    nt = len(fulls)
    rows, cols = stats.shape
    hbm = pl.BlockSpec(memory_space=pl.ANY)
    vm = pl.BlockSpec(memory_space=pltpu.VMEM)

    def body(*refs):
        ins, s_ref = refs[:nt], refs[nt]
        outs, g_ref = refs[nt + 1:2 * nt + 1], refs[2 * nt + 1]
        hssem, hrsem, ssem, rsem = refs[2 * nt + 2:]
        x, y, c = _place()
        me, sibling = (x, y, c), (x, y, 1 - c)
        halves = []
        for t in range(nt):
            hs = ins[t].shape[0] // 2
            mine = pl.ds(c * hs, hs)
            cp = _rcopy(ins[t].at[mine, :], outs[t].at[mine, :], hssem.at[t], hrsem.at[t], sibling)
            cp.start()
            halves.append(cp)

        chips = [(_flip(x, fx), _flip(y, fy)) for fx, fy in OTHER_CHIPS]

        def blk(px, py, pc):
            return g_ref.at[4 * px + 2 * py + pc]

        def copy(k, block, to, src=None):
            return _rcopy(blk(*block) if src is None else src, blk(*block), ssem.at[k], rsem.at[k], to)

        g_ref[4 * x + 2 * y + c] = s_ref[...]
        first = [copy(0, me, sibling, src=s_ref)]
        first += [copy(1 + j, me, (*chip, c), src=s_ref) for j, chip in enumerate(chips)]
        for cp in first:
            cp.start()
        passed = [copy(4 + j, (*chip, c), sibling) for j, chip in enumerate(chips)]
        for j, chip in enumerate(chips):
            copy(1 + j, (*chip, c), me).wait_recv()
            passed[j].start()
        copy(0, sibling, me).wait_recv()
        for j, chip in enumerate(chips):
            copy(4 + j, (*chip, 1 - c), me).wait_recv()
        for t in range(nt):
            hs = ins[t].shape[0] // 2
            other = pl.ds((1 - c) * hs, hs)
            _rcopy(ins[t].at[other, :], outs[t].at[other, :], hssem.at[t], hrsem.at[t], sibling).wait_recv()
        for cp in first + passed + halves:
            cp.wait_send()

    outs = pl.pallas_call(
        body, name="final_exchange",
        out_shape=[_big(f.shape, F32) for f in fulls] + [jax.ShapeDtypeStruct((N_DEV, rows, cols), F32)],
        in_specs=[hbm] * nt + [vm], out_specs=[hbm] * nt + [vm],
        input_output_aliases={t: t for t in range(nt)},
        scratch_shapes=[pltpu.SemaphoreType.DMA((nt,)), pltpu.SemaphoreType.DMA((nt,)),
                        pltpu.SemaphoreType.DMA((7,)), pltpu.SemaphoreType.DMA((7,))],
        compiler_params=_cp(None, 32),
    )(*[_hbm(f) for f in fulls], stats)
    return outs[:nt], outs[nt]


def _row_tile(rows):
    return min(rows, 512)


def _pair_sum(place, grad, got, name):
    nb, hs, cols = got.shape
    tr = _row_tile(hs)
    nr = hs // tr

    def body(pl_ref, g_ref, o_ref, pf_ref, pb_ref):
        j = pl.program_id(1)
        s = g_ref[0] + o_ref[0].astype(F32)

        @pl.when(j == 0)
        def _():
            pf_ref[...] = s

        @pl.when(j > 0)
        def _():
            pb_ref[0] = s.astype(BF16)

    gs = pltpu.PrefetchScalarGridSpec(
        num_scalar_prefetch=1, grid=(nr, nb),
        in_specs=[pl.BlockSpec((1, tr, cols), lambda i, j, p: (p[1] ^ j, p[0] * nr + i, 0)),
                  pl.BlockSpec((1, tr, cols), lambda i, j, p: (p[1] ^ j, i, 0))],
        out_specs=[pl.BlockSpec((tr, cols), lambda i, j, p: (i, 0)),
                   pl.BlockSpec((1, tr, cols), lambda i, j, p: (jnp.maximum(j - 1, 0), i, 0))])
    return pl.pallas_call(
        body, name=name, grid_spec=gs,
        out_shape=[_big((hs, cols), F32), _big((nb - 1, hs, cols), BF16)],
        compiler_params=_cp(("arbitrary", "arbitrary"), 32),
    )(place, *_hbm(grad, got))


def _chip_sum(place, pair_f, got_b, name):
    nb, hs, cols = got_b.shape
    tr = _row_tile(hs)
    nr = hs // tr

    def body(pl_ref, pf_ref, gb_ref, o_ref):
        acc = pf_ref[...]
        for k in range(nb):
            acc = acc + gb_ref[k].astype(F32)
        o_ref[...] = acc

    gs = pltpu.PrefetchScalarGridSpec(
        num_scalar_prefetch=1, grid=(nr,),
        in_specs=[pl.BlockSpec((tr, cols), lambda i, p: (i, 0)),
                  pl.BlockSpec((nb, tr, cols), lambda i, p: (0, i, 0))],
        out_specs=pl.BlockSpec((tr, cols), lambda i, p: (p[0] * nr + i, 0)))
    return pl.pallas_call(
        body, name=name, grid_spec=gs,
        out_shape=_big((2 * hs, cols), F32),
        compiler_params=_cp(("arbitrary",), 32),
    )(place, *_hbm(pair_f, got_b))


def _adam_math(w, g, m, v):
    m2 = ADAM_B1 * m + (1.0 - ADAM_B1) * g
    v2 = ADAM_B2 * v + (1.0 - ADAM_B2) * (g * g)
    m_hat = m2 / (1.0 - ADAM_B1 ** ADAM_STEP)
    v_hat = v2 / (1.0 - ADAM_B2 ** ADAM_STEP)
    delta = -ADAM_LR * (m_hat / (jnp.sqrt(v_hat) + ADAM_EPS) + ADAM_WD * w)
    return delta, m2, v2


def _adam_big(w, g, m, v, name):
    rows, cols = w.shape
    tr = _row_tile(rows)

    def body(w_ref, g_ref, m_ref, v_ref, d_out, m_out, v_out):
        d, m2, v2 = _adam_math(w_ref[...], g_ref[...], m_ref[...], v_ref[...])
        d_out[...] = d
        m_out[...] = m2
        v_out[...] = v2

    spec = pl.BlockSpec((tr, cols), lambda i: (i, 0))
    return pl.pallas_call(
        body, name=name, grid=(rows // tr,), in_specs=[spec] * 4, out_specs=[spec] * 3,
        out_shape=[_big(w.shape, F32)] * 3,
        compiler_params=_cp(("arbitrary",), 48),
    )(*_hbm(w, g, m, v))


def _ada_grad_adam(c8t, dmod_sh, w, m, v):
    rows, cols = w.shape
    tr = _row_tile(rows) // 2

    def body(ct_ref, dm_ref, w_ref, m_ref, v_ref, g_out, d_out, m_out, v_out):
        g = None
        for b in range(N_DEV):
            term = ct_ref[:, b:b + 1] * dm_ref[b:b + 1, :]
            g = term if g is None else g + term
        d, m2, v2 = _adam_math(w_ref[...], g, m_ref[...], v_ref[...])
        g_out[...] = g
        d_out[...] = d
        m_out[...] = m2
        v_out[...] = v2

    spec = pl.BlockSpec((tr, cols), lambda i: (i, 0))
    return pl.pallas_call(
        body, name="ada_grad_adam", grid=(rows // tr,),
        in_specs=[pl.BlockSpec((tr, N_DEV), lambda i: (i, 0)), _full((N_DEV, cols)), spec, spec, spec],
        out_specs=[spec] * 4, out_shape=[_big(w.shape, F32)] * 4,
        compiler_params=_cp(("arbitrary",), 32),
    )(c8t, dmod_sh, *_hbm(w, m, v))


def _tok_tile(t):
    return min(TOK_TILE, t)


def _mix_in_fwd(chip1, x, modr, g_pre, b_in4, later, w_in_buf, w_out_buf):
    T = x.shape[0]
    tm = min(MIXIN_TILE, T)
    nt = T // tm
    nb = IN_COLS // N_CHIPS
    nl = len(later)

    def body(*refs):
        ch_ref, x_ref, mod_ref, g_ref, b_ref = refs[:5]
        l_ins = refs[5:5 + nl]
        p_ref, h_ref = refs[7 + nl:9 + nl]
        l_outs = refs[9 + nl:9 + 2 * nl]
        win_ref, wout_ref = refs[9 + 2 * nl:11 + 2 * nl]
        h_all, wblk, lsem, is_sem, ir_sem, os_sem, or_sem = refs[11 + 2 * nl:]
        k, i = pl.program_id(0), pl.program_id(1)
        chip = ch_ref[0]

        @pl.when(k == 0)
        def _():
            for src, dst in zip(l_ins, l_outs):
                dst[0] = src[...].astype(BF16)

        def load_block(blk):
            cp = pltpu.make_async_copy(win_ref.at[blk], wblk, lsem)
            cp.start()
            cp.wait()

        @pl.when((k == 0) & (i == 0))
        def _():
            _gather_start([win_ref], is_sem, ir_sem, relations=(0, 1))
            load_block(chip)

        for r in range(N_CHIPS - 1):
            @pl.when((k == r + 1) & (i == 0))
            def _(r=r):
                _gather_arrive([win_ref], r, is_sem, ir_sem)
                if r == 0:
                    _gather_start([win_ref], is_sem, ir_sem, relations=(2,))
                if r == 1:
                    _gather_start([wout_ref], os_sem, or_sem)
                load_block(chip ^ (r + 1))

        rows = pl.ds(pl.multiple_of(i * tm, tm), tm)

        @pl.when(k == 0)
        def _():
            xv = x_ref[...]
            rstd = lax.rsqrt(jnp.mean(xv * xv, axis=-1, keepdims=True) + RMS_EPS)
            h = (xv * rstd) * g_ref[...] * (1.0 + mod_ref[1:2, :]) + mod_ref[0:1, :]
            hb = h.astype(BF16)
            h_ref[...] = hb
            h_all[rows, :] = hb

        p_ref[...] = _dot(h_all[rows, :], wblk[...]) + b_ref[chip ^ k]

        @pl.when((k == N_CHIPS - 1) & (i == nt - 1))
        def _():
            _gather_sends_done([win_ref], is_sem, ir_sem)
            _gather_finish([wout_ref], os_sem, or_sem)

    hbm = pl.BlockSpec(memory_space=pl.ANY)
    first_pass = lambda k, i, ch: (jnp.where(k == 0, i, nt - 1), 0)
    own_slot = lambda k, i, ch: (ch[0], jnp.where(k == 0, i, nt - 1), 0)
    gs = pltpu.PrefetchScalarGridSpec(
        num_scalar_prefetch=1, grid=(N_CHIPS, nt),
        in_specs=[pl.BlockSpec((tm, D_MODEL), first_pass), pl.BlockSpec((6, D_MODEL), lambda k, i, ch: (0, 0)),
                  pl.BlockSpec((1, D_MODEL), lambda k, i, ch: (0, 0)),
                  pl.BlockSpec((N_CHIPS, 1, nb), lambda k, i, ch: (0, 0, 0))]
        + [pl.BlockSpec((w.shape[0] // nt, w.shape[1]), first_pass) for w in later] + [hbm, hbm],
        out_specs=[pl.BlockSpec((tm, nb), lambda k, i, ch: (i, ch[0] ^ k)), pl.BlockSpec((tm, D_MODEL), first_pass)]
        + [pl.BlockSpec((1, w.shape[0] // nt, w.shape[1]), own_slot) for w in later] + [hbm, hbm],
        scratch_shapes=[pltpu.VMEM((T, D_MODEL), BF16), pltpu.VMEM((D_MODEL, nb), BF16), pltpu.SemaphoreType.DMA]
        + _gather_sems(1) + _gather_sems(1))
    outs = pl.pallas_call(
        body, name="mix_in_fwd", grid_spec=gs,
        out_shape=[_big((T, IN_COLS), F32), _big((T, D_MODEL), BF16)]
        + [_big((N_CHIPS,) + w.shape, BF16) for w in later]
        + [_big(w_in_buf.shape, BF16), _big(w_out_buf.shape, BF16)],
        input_output_aliases={5 + nl: 2 + nl, 6 + nl: 3 + nl},
        compiler_params=_cp(("arbitrary", "arbitrary"), 48),
    )(chip1, _hbm(x), modr, g_pre, b_in4, *[_hbm(w) for w in later], _hbm(w_in_buf), _hbm(w_out_buf))
    return outs[0], outs[1], outs[2:2 + nl], outs[2 + nl], outs[3 + nl]


def _mixers_fwd(p, wdw, vecs, lbl, w_out, x, modr, g_post, g_ffn, gbufs):
    T = p.shape[0]
    tm = _tok_tile(T)
    nt = T // tm
    nch = tm // CHUNK
    ng = len(gbufs)
    n_in, n_out = 12, 7

    def body(*refs):
        (p_ref, wdw_ref, bdw_ref, gain_ref, bias_ref, gout_ref, lbl_ref, wout_ref, x_ref, mod_ref, gp_ref,
         gf_ref) = refs[:n_in]
        cat_ref, ys_ref, o_ref, st_ref, y_ref, x1_ref, h2_ref = refs[n_in + ng:n_in + ng + n_out]
        gout_bufs = refs[n_in + ng + n_out:n_in + 2 * ng + n_out]
        (ubuf, state, qt_s, kt_s, kh_s, v_s, egl_s, lower_s, gmat_s, gssem,
         grsem) = refs[n_in + 2 * ng + n_out:]
        i = pl.program_id(0)

        @pl.when(i == 0)
        def _():
            _ring_start(gout_bufs, gssem, grsem)
            lower_s[...], _, _ = _chunk_masks(tm)
            gmat_s[...] = _gn_matrix()
            state[...] = jnp.zeros(state.shape, F32)
            ubuf[0:HALO, :] = jnp.zeros((HALO, CONV_CH), F32)
            ubuf[HALO + tm:HALO + tm + SUB, :] = jnp.zeros((SUB, CONV_CH), F32)

        @pl.when(i > 0)
        def _():
            ubuf[0:HALO, :] = ubuf[tm:tm + HALO, :]

        ubuf[HALO:HALO + tm, :] = p_ref[:, 0:CONV_CH] * _sig(p_ref[:, CONV_CH:2 * CONV_CH])
        for r in range(tm // CONV_ROWS):
            rows = slice(r * CONV_ROWS, (r + 1) * CONV_ROWS)
            for lb_ in range(CONV_CH // LANE):
                lanes = slice(lb_ * LANE, (lb_ + 1) * LANE)
                ys_ref[rows, lanes] = bdw_ref[:, lanes] + _tap_conv(ubuf, wdw_ref, r * CONV_ROWS, CONV_FWD_TAPS, lanes)
        gmat = gmat_s[...]
        yv = ys_ref[...]
        d = yv - _gmean(yv, gmat)
        rs = lax.rsqrt(_gmean(d * d, gmat) + GN_EPS)
        z = d * rs * gain_ref[...] + bias_ref[...]
        cat_ref[:, 0:CONV_CH] = (z * _sig(z)).astype(BF16)

        lb, _ = _lower_bound(lbl_ref)
        o0 = 2 * CONV_CH
        pr = _hgrn_prep(p_ref[:, o0:o0 + HGRN_W], p_ref[:, o0 + HGRN_W:o0 + 2 * HGRN_W], lb, lower_s[...])
        qt_s[...] = pr["qt"].astype(BF16)
        kt_s[...] = pr["kt"].astype(BF16)
        kh_s[...] = pr["kh"].astype(BF16)
        v_s[...] = p_ref[:, o0 + 2 * HGRN_W:o0 + 3 * HGRN_W].astype(BF16)
        egl_s[...] = jnp.exp(pr["Gl"])
        tri = _tri()

        def chunk(ci, carry):
            r0 = pl.multiple_of(ci * CHUNK, CHUNK)
            rows = pl.ds(r0, CHUNK)
            for h in range(N_HEADS):
                ls = pl.ds(h * HEAD_D, HEAD_D)
                qc, kc, hc, vc = qt_s[rows, ls], kt_s[rows, ls], kh_s[rows, ls], v_s[rows, ls]
                s0 = state[h]
                s0b = s0.astype(BF16)
                st_ref[ci, h] = s0
                att =jnp.where(tri, _dot_nt(qc, kc), 0.0).astype(BF16)
                o_ref[rows, ls] = _dot(att, vc) + _dot_nt(qc, s0b)
                state[h] = s0 * egl_s[pl.ds(r0, 1), ls] + _dot_tn(vc, hc)
            return carry

        lax.fori_loop(0, nch, chunk, 0, unroll=min(CHUNK_UNROLL, nch))
        for h in range(N_HEADS):
            sl = slice(h * HEAD_D, (h + 1) * HEAD_D)
            oh = o_ref[:, sl]
            gh = p_ref[:, o0 + 3 * HGRN_W + h * HEAD_D:o0 + 3 * HGRN_W + (h + 1) * HEAD_D]
            rsh = lax.rsqrt(jnp.mean(oh * oh, axis=-1, keepdims=True) + RMS_EPS)
            hg = (oh * rsh) * gout_ref[:, sl] * (gh * _sig(gh))
            cat_ref[:, CONV_CH + h * HEAD_D:CONV_CH + (h + 1) * HEAD_D] = hg.astype(BF16)

        yv = _dot(cat_ref[...], wout_ref[...])
        y_ref[...] = yv
        rsy = lax.rsqrt(jnp.mean(yv * yv, axis=-1, keepdims=True) + RMS_EPS)
        x1 = x_ref[...] + (yv * rsy) * (mod_ref[2:3, :] * gp_ref[...])
        x1_ref[...] = x1
        rs1 = lax.rsqrt(jnp.mean(x1 * x1, axis=-1, keepdims=True) + RMS_EPS)
        h2 = (x1 * rs1) * (gf_ref[...] * (1.0 + mod_ref[4:5, :])) + mod_ref[3:4, :]
        h2_ref[...] = h2.astype(BF16)

        @pl.when(i == min(nt - 1, nt // 2 + 1))
        def _():
            _ring_forward(gout_bufs, gssem, grsem)

        @pl.when(i == nt - 1)
        def _():
            _ring_finish(gout_bufs, gssem, grsem)

    tile = lambda cols: pl.BlockSpec((tm, cols), lambda i: (i, 0))
    hbm = pl.BlockSpec(memory_space=pl.ANY)
    outs = pl.pallas_call(
        body, name="mixers_fwd", grid=(nt,),
        in_specs=[tile(IN_COLS), _full((HALO, CONV_CH))] + [_full((1, CONV_CH))] * 4 + [_full((2, HGRN_W))]
        + [_full((D_MODEL, D_MODEL)), tile(D_MODEL), _full((6, D_MODEL)), _full((1, D_MODEL)), _full((1, D_MODEL))]
        + [hbm] * ng,
        out_specs=[tile(D_MODEL), tile(CONV_CH), tile(HGRN_W),
                   pl.BlockSpec((nch, N_HEADS, HEAD_D, HEAD_D), lambda i: (i, 0, 0, 0)),
                   tile(D_MODEL), tile(D_MODEL), tile(D_MODEL)] + [hbm] * ng,
        out_shape=[_big((T, D_MODEL), BF16), _big((T, CONV_CH), F32), _big((T, HGRN_W), F32),
                   _big((T // CHUNK, N_HEADS, HEAD_D, HEAD_D), F32), _big((T, D_MODEL), F32),
                   _big((T, D_MODEL), F32), _big((T, D_MODEL), BF16)] + [_big(b.shape, BF16) for b in gbufs],
        input_output_aliases={n_in + t: n_out + t for t in range(ng)},
        scratch_shapes=[pltpu.VMEM((tm + HALO + SUB, CONV_CH), F32), pltpu.VMEM((N_HEADS, HEAD_D, HEAD_D), F32),
                        pltpu.VMEM((tm, HGRN_W), BF16), pltpu.VMEM((tm, HGRN_W), BF16),
                        pltpu.VMEM((tm, HGRN_W), BF16), pltpu.VMEM((tm, HGRN_W), BF16),
                        pltpu.VMEM((tm, HGRN_W), F32), pltpu.VMEM((tm, tm), BF16),
                        pltpu.VMEM((CONV_CH, CONV_CH), BF16)] + _ring_sems(ng),
        compiler_params=_cp(("arbitrary",), 56),
    )(_hbm(p), wdw, *vecs, lbl, *_hbm(w_out, x), modr, g_post, g_ffn, *[_hbm(b) for b in gbufs])
    return outs[:n_out], outs[n_out:]


def _row_chains(rows, n=2):
    step = rows // n
    return [slice(k * step, (k + 1) * step) for k in range(n)]


def _ffn_blocks():
    return D_FF // FFN_BLOCK, (D_FF // N_CHIPS) // FFN_BLOCK


def _ffn_fwd(h2, w_up_g, w_down, x1, target, modr, g_post):
    T = h2.shape[0]
    tm = min(FFN_TILE, T)
    fb = FFN_BLOCK
    nj, per = _ffn_blocks()

    def body(h_ref, wu_ref, wd_ref, x1_ref, t_ref, mod_ref, g_ref, r_ref, dy2_ref, dx2_ref, st_ref, acc):
        i, j = pl.program_id(0), pl.program_id(1)

        @pl.when((i == 0) & (j == 0))
        def _():
            st_ref[...] = jnp.zeros(st_ref.shape, F32)

        @pl.when(j == 0)
        def _():
            acc[...] = jnp.zeros(acc.shape, F32)

        for rows in _row_chains(tm):
            ra = jnp.maximum(_dot(h_ref[rows, :], wu_ref[0]), 0.0)
            rb = (ra * ra).astype(BF16)
            r_ref[rows, :] = rb
            acc[rows, :] += _dot(rb, wd_ref[...])

        @pl.when(j == nj - 1)
        def _():
            y2 = acc[...]
            rs = lax.rsqrt(jnp.mean(y2 * y2, axis=-1, keepdims=True) + RMS_EPS)
            nh = y2 * rs
            gp, gt = g_ref[...], mod_ref[5:6, :]
            gate_gain = gt * gp
            err = x1_ref[...] + nh * gate_gain - t_ref[...]
            dx2 = err * (1.0 / D_MODEL)
            dx2_ref[...] = dx2
            st_ref[0:1, :] += _colsum(err * err)
            s_dn = _colsum(dx2 * nh)
            st_ref[1:2, :] += s_dn * gp
            st_ref[2:3, :] += s_dn * gt
            dy2_ref[...] = _rms_bwd(dx2 * gate_gain, nh, rs).astype(BF16)

    tile = pl.BlockSpec((tm, D_MODEL), lambda i, j: (i, 0))
    return pl.pallas_call(
        body, name="ffn_fwd", grid=(T // tm, nj),
        in_specs=[tile, pl.BlockSpec((1, D_MODEL, fb), lambda i, j: (j // per, 0, j % per)),
                  pl.BlockSpec((fb, D_MODEL), lambda i, j: (j, 0)), tile, tile,
                  _full((6, D_MODEL)), _full((1, D_MODEL))],
        out_specs=[pl.BlockSpec((tm, fb), lambda i, j: (i, j)), tile, tile, _full((8, D_MODEL))],
        out_shape=[_big((T, D_FF), BF16), _big((T, D_MODEL), BF16), _big((T, D_MODEL), F32),
                   jax.ShapeDtypeStruct((8, D_MODEL), F32)],
        scratch_shapes=[pltpu.VMEM((tm, D_MODEL), F32)],
        compiler_params=_cp(("arbitrary", "arbitrary"), 56),
    )(*_hbm(h2, w_up_g, w_down, x1, target), modr, g_post)


def _rms_bwd(dxn, xn, rs):
    return rs * (dxn - xn * jnp.mean(dxn * xn, axis=-1, keepdims=True))


def _ffn_bwd(dy2, r, x1, dx2, w_up_g, w_down, modr, g_ffn):
    T = dx2.shape[0]
    tm = min(FFN_TILE, T)
    fb = FFN_BLOCK
    nj, per = _ffn_blocks()

    def body(dy2_ref, r_ref, x1_ref, dx2_ref, wu_ref, wd_ref, mod_ref, gf_ref, da_ref, dx1_ref, st_ref, dh_s):
        i, j = pl.program_id(0), pl.program_id(1)

        @pl.when((i == 0) & (j == 0))
        def _():
            st_ref[...] = jnp.zeros(st_ref.shape, F32)

        @pl.when(j == 0)
        def _():
            dh_s[...] = jnp.zeros(dh_s.shape, F32)

        for rows in _row_chains(tm):
            ra = jnp.sqrt(r_ref[rows, :].astype(F32))
            da = (_dot_nt(dy2_ref[rows, :], wd_ref[...]) * (2.0 * ra)).astype(BF16)
            da_ref[rows, :] = da
            dh_s[rows, :] += _dot_nt(da, wu_ref[0])

        @pl.when(j == nj - 1)
        def _():
            dh = dh_s[...]
            x1v = x1_ref[...]
            rs1 = lax.rsqrt(jnp.mean(x1v * x1v, axis=-1, keepdims=True) + RMS_EPS)
            xn = x1v * rs1
            gf, scale1 = gf_ref[...], 1.0 + mod_ref[4:5, :]
            st_ref[0:1, :] += _colsum(dh)
            s_dh = _colsum(dh * xn)
            st_ref[1:2, :] += s_dh * gf
            st_ref[2:3, :] += s_dh * scale1
            dx1_ref[...] = dx2_ref[...] + _rms_bwd(dh * (scale1 * gf), xn, rs1)

    tile = pl.BlockSpec((tm, D_MODEL), lambda i, j: (i, 0))
    ftile = pl.BlockSpec((tm, fb), lambda i, j: (i, j))
    return pl.pallas_call(
        body, name="ffn_bwd", grid=(T // tm, nj),
        in_specs=[tile, ftile, tile, tile, pl.BlockSpec((1, D_MODEL, fb), lambda i, j: (j // per, 0, j % per)),
                  pl.BlockSpec((fb, D_MODEL), lambda i, j: (j, 0)), _full((6, D_MODEL)), _full((1, D_MODEL))],
        out_specs=[ftile, tile, _full((8, D_MODEL))],
        out_shape=[_big((T, D_FF), BF16), _big((T, D_MODEL), F32), jax.ShapeDtypeStruct((8, D_MODEL), F32)],
        scratch_shapes=[pltpu.VMEM((tm, D_MODEL), F32)],
        compiler_params=_cp(("arbitrary", "arbitrary"), 56),
    )(*_hbm(dy2, r, x1, dx2, w_up_g, w_down), modr, g_ffn)


def _mix_out_bwd(dx1, y, cat, w_out, modr, g_post, swap):
    T = dx1.shape[0]
    tm = _tok_tile(T)
    nt = T // tm
    ns = len(swap)

    def body(*refs):
        dx1_ref, y_ref, cat_ref, w_ref, mod_ref, gp_ref = refs[:6]
        s_ins = refs[6:6 + ns]
        dcat_ref, st_ref, gw_ref = refs[6 + ns:9 + ns]
        s_outs = refs[9 + ns:9 + 2 * ns]
        gacc, gsem, pssem, prsem = refs[9 + 2 * ns:]
        i = pl.program_id(0)

        @pl.when(i == 0)
        def _():
            for cp in _pair_copies(s_ins, s_outs, pssem, prsem):
                cp.start()
            st_ref[...] = jnp.zeros(st_ref.shape, F32)
            gacc[...] = jnp.zeros(gacc.shape, F32)

        dxv, yv = dx1_ref[...], y_ref[...]
        rs = lax.rsqrt(jnp.mean(yv * yv, axis=-1, keepdims=True) + RMS_EPS)
        nh = yv * rs
        st_ref[0:1, :] += _colsum(dxv * (nh * gp_ref[...]))
        dn = dxv * mod_ref[2:3, :]
        st_ref[1:2, :] += _colsum(dn * nh)
        dy = _rms_bwd(dn * gp_ref[...], nh, rs).astype(BF16)
        dcat_ref[...] = _dot_nt(dy, w_ref[...])
        for cols in _row_chains(D_MODEL):
            gacc[:, cols] += _dot_tn(cat_ref[...], dy[:, cols])

        @pl.when(i == nt - 1)
        def _():
            out = pltpu.make_async_copy(gacc, gw_ref, gsem)
            out.start()
            copies = _pair_copies(s_ins, s_outs, pssem, prsem)
            for cp in copies:
                cp.wait_recv()
            for cp in copies:
                cp.wait_send()
            out.wait()

    tile = pl.BlockSpec((tm, D_MODEL), lambda i: (i, 0))
    hbm = pl.BlockSpec(memory_space=pl.ANY)
    outs = pl.pallas_call(
        body, name="mix_out_bwd", grid=(nt,),
        in_specs=[tile, tile, tile, _full((D_MODEL, D_MODEL)), _full((6, D_MODEL)), _full((1, D_MODEL))]
        + [hbm] * ns,
        out_specs=[tile, _full((8, D_MODEL)), hbm] + [hbm] * ns,
        out_shape=[_big((T, D_MODEL), F32), jax.ShapeDtypeStruct((8, D_MODEL), F32), _big((D_MODEL, D_MODEL), F32)]
        + _pair_shapes(swap),
        scratch_shapes=[pltpu.VMEM((D_MODEL, D_MODEL), F32), pltpu.SemaphoreType.DMA] + _pair_sems(ns),
        compiler_params=_cp(("arbitrary",), 48),
    )(*_hbm(dx1, y, cat, w_out), modr, g_post, *[_hbm(g) for g in swap])
    return outs[:3], outs[3:]


def _mixers_bwd(p, dcat, ys, o, states, h1, wdw, vecs, lbl, pairs_b):
    T = p.shape[0]
    tm = min(MIXB_TILE, T)
    nt = T // tm
    nch = tm // CHUNK
    hpt = tm // HALO
    nx = len(pairs_b)
    nb = IN_COLS // N_CHIPS

    def body(*refs):
        (p_ref, ph_ref, dcat_ref, ys_ref, o_ref, st_ref, h1_ref, wdw_ref, bdw_ref, gain_ref, bias_ref, gout_ref,
         lbl_ref) = refs[:13]
        x_ins = refs[13:13 + nx]
        dp_ref, sb_ref, s5_ref, dw_ref = refs[13 + nx:17 + nx]
        x_outs = refs[17 + nx:17 + 2 * nx]
        gin_ref, ginb_ref = refs[17 + 2 * nx], refs[18 + 2 * nx]
        (ubuf, dybuf, carry, dstate, qt_s, kt_s, kh_s, v_s, do_s, egl_s, dqt_s, dkt_s, dkh_s, dv_s, dgl_s,
         dsh, dw8, dshift, lower_s, upper_s, same_s, gmat_s, gacc, gacc_b, dp_prev, h1_prev, gsem, gbsem, xssem,
         xrsem) = refs[19 + 2 * nx:]
        i = pl.program_id(0)
        tile_idx = nt - 1 - i

        @pl.when(i == 0)
        def _():
            for cp in _xchg_copies(x_ins, x_outs, xssem, xrsem):
                cp.start()
            gacc[...] = jnp.zeros(gacc.shape, F32)
            dstate[...] = jnp.zeros(dstate.shape, F32)
            carry[...] = jnp.zeros(carry.shape, F32)
            sb_ref[...] = jnp.zeros(sb_ref.shape, F32)
            s5_ref[...] = jnp.zeros(s5_ref.shape, F32)
            dw_ref[...] = jnp.zeros(dw_ref.shape, F32)
            dw8[...] = jnp.zeros(dw8.shape, F32)
            lower_s[...], upper_s[...], same_s[...] = _chunk_masks(tm)
            gmat_s[...] = _gn_matrix()
            dsh[0:SUB, :] = jnp.zeros((SUB, CONV_CH), F32)
            dsh[SUB + tm:2 * SUB + tm, :] = jnp.zeros((SUB, CONV_CH), F32)
            ubuf[HALO + tm:HALO + tm + SUB, :] = jnp.zeros((SUB, CONV_CH), F32)
            dp_prev[...] = jnp.zeros(dp_prev.shape, BF16)
            h1_prev[...] = jnp.zeros(h1_prev.shape, BF16)

        n_pieces = (tm // CONV_ROWS) * (CONV_CH // LANE)
        per_block = n_pieces // N_CHIPS
        prow = D_MODEL // per_block

        def w_in_grad_piece(k):
            j, part = k // per_block, k % per_block
            rows_k = slice(part * prow, (part + 1) * prow)
            gacc[j, rows_k, :] += _dot_tn(h1_prev[:, rows_k], dp_prev[:, j * nb:(j + 1) * nb])

        uh = ph_ref[:, 0:CONV_CH] * _sig(ph_ref[:, CONV_CH:2 * CONV_CH])
        ubuf[0:HALO, :] = jnp.where(tile_idx > 0, uh, 0.0)
        ubuf[HALO:HALO + tm, :] = p_ref[:, 0:CONV_CH] * _sig(p_ref[:, CONV_CH:2 * CONV_CH])
        gmat = gmat_s[...]
        gain = gain_ref[...]
        yv = ys_ref[...]
        d = yv - _gmean(yv, gmat)
        rs = lax.rsqrt(_gmean(d * d, gmat) + GN_EPS)
        yn = d * rs
        z = yn * gain + bias_ref[...]
        sz = _sig(z)
        dz = dcat_ref[:, 0:CONV_CH] * (sz * (1.0 + z * (1.0 - sz)))
        dyn = dz * gain
        dyc = rs * (dyn - _gmean(dyn, gmat) - yn * _gmean(dyn * yn, gmat))
        s5_ref[0:1, :] += _colsum(dyc)
        s5_ref[1:2, :] += _colsum(dz * yn)
        s5_ref[2:3, :] += _colsum(dz)
        dybuf[tm:tm + HALO, :] = carry[...]
        dybuf[0:tm, :] = dyc
        dsh[SUB:SUB + tm, :] = dyc
        carry[...] = dyc[0:HALO, :]
        for b in range(SUB):
            dshift[...] = dsh[SUB - b:2 * SUB - b + tm, :]
            for j, off in CONV_FWD_TAPS:
                if off % SUB == b:
                    prod = dshift[...] * ubuf[off - b:off - b + tm + SUB, :]
                    dw8[j] += jnp.sum(prod.reshape((tm + SUB) // SUB, SUB, CONV_CH), axis=0)
        for r in range(tm // CONV_ROWS):
            rows = slice(r * CONV_ROWS, (r + 1) * CONV_ROWS)
            for lb_ in range(CONV_CH // LANE):
                lanes = slice(lb_ * LANE, (lb_ + 1) * LANE)
                glanes = slice(CONV_CH + lb_ * LANE, CONV_CH + (lb_ + 1) * LANE)
                w_in_grad_piece(r * (CONV_CH // LANE) + lb_)
                acc = _tap_conv(dybuf, wdw_ref, r * CONV_ROWS, CONV_BWD_TAPS, lanes)
                val = p_ref[rows, lanes]
                sg = _sig(p_ref[rows, glanes])
                dval = acc * sg
                dgate = acc * val * (sg * (1.0 - sg))
                dp_ref[rows, lanes] = dval.astype(BF16)
                dp_ref[rows, glanes] = dgate.astype(BF16)
                sb_ref[0:1, lanes] += _colsum(dval)
                sb_ref[0:1, glanes] += _colsum(dgate)

        o0 = 2 * CONV_CH
        for h in range(N_HEADS):
            sl = slice(h * HEAD_D, (h + 1) * HEAD_D)
            gsl = slice(o0 + 3 * HGRN_W + h * HEAD_D, o0 + 3 * HGRN_W + (h + 1) * HEAD_D)
            oh = o_ref[:, sl]
            gh = p_ref[:, gsl]
            dh = dcat_ref[:, CONV_CH + h * HEAD_D:CONV_CH + (h + 1) * HEAD_D]
            gout = gout_ref[:, sl]
            rsh = lax.rsqrt(jnp.mean(oh * oh, axis=-1, keepdims=True) + RMS_EPS)
            on = oh * rsh
            sgg = _sig(gh)
            dgh = dh * (on * gout) * (sgg * (1.0 + gh * (1.0 - sgg)))
            dm = dh * (gh * sgg)
            s5_ref[3:4, sl] += _colsum(dm * on)
            do_s[:, sl] = _rms_bwd(dm * gout, on, rsh).astype(BF16)
            dp_ref[:, gsl] = dgh.astype(BF16)
            sb_ref[2:3, CONV_CH + h * HEAD_D:CONV_CH + (h + 1) * HEAD_D] += _colsum(dgh)

        lb, _ = _lower_bound(lbl_ref)
        pq = p_ref[:, o0:o0 + HGRN_W]
        pr = _hgrn_prep(pq, p_ref[:, o0 + HGRN_W:o0 + 2 * HGRN_W], lb, lower_s[...])
        qt_s[...] = pr["qt"].astype(BF16)
        kt_s[...] = pr["kt"].astype(BF16)
        kh_s[...] = pr["kh"].astype(BF16)
        v_s[...] = p_ref[:, o0 + 2 * HGRN_W:o0 + 3 * HGRN_W].astype(BF16)
        egl_s[...] = jnp.exp(pr["Gl"])
        tri = _tri()

        def chunk(it, c_):
            ci = nch - 1 - it
            r0 = pl.multiple_of(ci * CHUNK, CHUNK)
            rows = pl.ds(r0, CHUNK)
            for h in range(N_HEADS):
                ls = pl.ds(h * HEAD_D, HEAD_D)
                qc, kc, hc, vc = qt_s[rows, ls], kt_s[rows, ls], kh_s[rows, ls], v_s[rows, ls]
                dob = do_s[rows, ls]
                s0 = st_ref[ci, h]
                s0b = s0.astype(BF16)
                ds1 = dstate[h]
                ds1b = ds1.astype(BF16)
                egl = egl_s[pl.ds(r0, 1), ls]
                att = jnp.where(tri, _dot_nt(qc, kc), 0.0).astype(BF16)
                datt = jnp.where(tri, _dot_nt(dob, vc), 0.0).astype(BF16)
                dv_s[rows, ls] = _dot_tn(att, dob) + _dot_nt(hc, ds1b)
                dqt_s[rows, ls] = _dot(datt, kc) + _dot(dob, s0b)
                dkt_s[rows, ls] = _dot_tn(datt, qc)
                dkh_s[rows, ls] = _dot(vc, ds1b)
                dgl = egl * _colsum(ds1 * s0)
                dgl_s[rows, ls] = jnp.broadcast_to(dgl, (CHUNK, HEAD_D))
                dstate[h] = ds1 * egl + _dot_tn(dob, qc)
            return c_

        lax.fori_loop(0, nch, chunk, 0, unroll=min(CHUNK_UNROLL, nch))
        dqt, dkt, dkh = dqt_s[...], dkt_s[...], dkh_s[...]
        dk = dkt * pr["enG"] + dkh * pr["eGlG"]
        khk = dkh * kh_s[...].astype(F32)
        dG = dqt * qt_s[...].astype(F32) - dkt * kt_s[...].astype(F32) - khk
        dlogf = _mm3(upper_s[...], dG) + _mm3(same_s[...], khk) + dgl_s[...]
        df = dlogf / pr["f"] - dk
        sf, sq = pr["sf"], pr["sq"]
        s5_ref[4:5, :] += _colsum(df * (1.0 - sf))
        dfl = df * (1.0 - lb) * (sf * (1.0 - sf))
        dq = (dqt * pr["eG"]) * (sq * (1.0 + pq * (1.0 - sq)))
        dvv = dv_s[...]
        dp_ref[:, o0:o0 + HGRN_W] = dq.astype(BF16)
        dp_ref[:, o0 + HGRN_W:o0 + 2 * HGRN_W] = dfl.astype(BF16)
        dp_ref[:, o0 + 2 * HGRN_W:o0 + 3 * HGRN_W] = dvv.astype(BF16)
        sb_ref[1:2, 0:HGRN_W] += _colsum(dq)
        sb_ref[1:2, HGRN_W:2 * HGRN_W] += _colsum(dfl)
        sb_ref[2:3, 0:HGRN_W] += _colsum(dvv)

        dp_prev[...] = dp_ref[...]
        h1_prev[...] = h1_ref[...]

        @pl.when(i == nt - 1)
        def _():
            for k in range(n_pieces):
                w_in_grad_piece(k)
            out = pltpu.make_async_copy(gacc, gin_ref, gsem)
            out.start()
            for j in range(N_CHIPS):
                gacc_b[j] = gacc[j].astype(BF16)
            out_b = pltpu.make_async_copy(gacc_b, ginb_ref, gbsem)
            out_b.start()
            for j in range(CONV_K):
                dw_ref[j:j + 1, :] = _colsum(dw8[j])
            copies = _xchg_copies(x_ins, x_outs, xssem, xrsem)
            for cp in copies:
                cp.wait_recv()
            for cp in copies:
                cp.wait_send()
            out.wait()
            out_b.wait()

    rev = lambda cols: pl.BlockSpec((tm, cols), lambda i: (nt - 1 - i, 0))
    halo = pl.BlockSpec((HALO, 2 * CONV_CH), lambda i: (jnp.maximum((nt - 1 - i) * hpt - 1, 0), 0))
    wide = lambda n: pltpu.VMEM((tm, HGRN_W), n)
    hbm = pl.BlockSpec(memory_space=pl.ANY)
    outs = pl.pallas_call(
        body, name="mixers_bwd", grid=(nt,),
        in_specs=[rev(IN_COLS), halo, rev(D_MODEL), rev(CONV_CH), rev(HGRN_W),
                  pl.BlockSpec((nch, N_HEADS, HEAD_D, HEAD_D), lambda i: (nt - 1 - i, 0, 0, 0)), rev(D_MODEL),
                  _full((HALO, CONV_CH))] + [_full((1, CONV_CH))] * 4 + [_full((2, HGRN_W))] + [hbm] * nx,
        out_specs=[rev(IN_COLS), _full((8, D_MODEL)), _full((8, CONV_CH)), _full((HALO, CONV_CH))]
        + [hbm] * (nx + 2),
        out_shape=[_big((T, IN_COLS), BF16), jax.ShapeDtypeStruct((8, D_MODEL), F32),
                   jax.ShapeDtypeStruct((8, CONV_CH), F32), jax.ShapeDtypeStruct((HALO, CONV_CH), F32)]
        + [_big(pb.shape, BF16) for pb in pairs_b]
        + [_big((N_CHIPS, D_MODEL, nb), F32), _big((N_CHIPS, D_MODEL, nb), BF16)],
        scratch_shapes=[pltpu.VMEM((tm + HALO + SUB, CONV_CH), F32), pltpu.VMEM((tm + HALO, CONV_CH), F32),
                        pltpu.VMEM((HALO, CONV_CH), F32), pltpu.VMEM((N_HEADS, HEAD_D, HEAD_D), F32),
                        wide(BF16), wide(BF16), wide(BF16), wide(BF16), wide(BF16),
                        wide(F32), wide(F32), wide(F32), wide(F32), wide(F32), wide(F32),
                        pltpu.VMEM((tm + 2 * SUB, CONV_CH), F32), pltpu.VMEM((HALO, SUB, CONV_CH), F32),
                        pltpu.VMEM((tm + SUB, CONV_CH), F32), pltpu.VMEM((tm, tm), BF16), pltpu.VMEM((tm, tm), BF16),
                        pltpu.VMEM((tm, tm), BF16), pltpu.VMEM((CONV_CH, CONV_CH), BF16),
                        pltpu.VMEM((N_CHIPS, D_MODEL, nb), F32), pltpu.VMEM((N_CHIPS, D_MODEL, nb), BF16),
                        pltpu.VMEM((tm, IN_COLS), BF16), pltpu.VMEM((tm, D_MODEL), BF16),
                        pltpu.SemaphoreType.DMA, pltpu.SemaphoreType.DMA]
        + _xchg_sems(nx),
        compiler_params=_cp(("arbitrary",), 56),
    )(*_hbm(p, p, dcat, ys, o, states, h1), wdw, *vecs, lbl, *[_hbm(pb) for pb in pairs_b])
    return outs[:4], outs[4:4 + nx], (outs[4 + nx], outs[5 + nx])


def _mix_in_bwd(dp, w_in_g, x, dx1, modr, g_pre, pairs_b):
    T = x.shape[0]
    tm = _tok_tile(T)
    nt = T // tm
    nb = IN_COLS // N_CHIPS
    nx = len(pairs_b)

    def body(*refs):
        dp_ref, w_ref, x_ref, dx1_ref, mod_ref, g_ref = refs[:6]
        x_ins = refs[6:6 + nx]
        gx_ref, st_ref = refs[6 + nx:8 + nx]
        x_outs = refs[8 + nx:8 + 2 * nx]
        xssem, xrsem = refs[8 + 2 * nx:]
        i = pl.program_id(0)

        @pl.when(i == 0)
        def _():
            for cp in _xchg_copies(x_ins, x_outs, xssem, xrsem):
                cp.start()
            st_ref[...] = jnp.zeros(st_ref.shape, F32)

        dh = None
        for j in range(N_CHIPS):
            part = _dot_nt(dp_ref[:, j * nb:(j + 1) * nb], w_ref[j])
            dh = part if dh is None else dh + part
        xv = x_ref[...]
        rs = lax.rsqrt(jnp.mean(xv * xv, axis=-1, keepdims=True) + RMS_EPS)
        xn = xv * rs
        st_ref[0:1, :] += _colsum(dh)
        st_ref[1:2, :] += _colsum(dh * (xn * g_ref[...]))
        dsc = dh * (1.0 + mod_ref[1:2, :])
        st_ref[2:3, :] += _colsum(dsc * xn)
        gx_ref[...] = dx1_ref[...] + _rms_bwd(dsc * g_ref[...], xn, rs)

        @pl.when(i == nt - 1)
        def _():
            copies = _xchg_copies(x_ins, x_outs, xssem, xrsem)
            for cp in copies:
                cp.wait_recv()
            for cp in copies:
                cp.wait_send()

    tile = pl.BlockSpec((tm, D_MODEL), lambda i: (i, 0))
    hbm = pl.BlockSpec(memory_space=pl.ANY)
    outs = pl.pallas_call(
        body, name="mix_in_bwd", grid=(nt,),
        in_specs=[pl.BlockSpec((tm, IN_COLS), lambda i: (i, 0)), _full((N_CHIPS, D_MODEL, nb)), tile, tile,
                  _full((6, D_MODEL)), _full((1, D_MODEL))] + [hbm] * nx,
        out_specs=[tile, _full((8, D_MODEL))] + [hbm] * nx,
        out_shape=[_big((T, D_MODEL), F32), jax.ShapeDtypeStruct((8, D_MODEL), F32)]
        + [_big(pb.shape, BF16) for pb in pairs_b],
        scratch_shapes=_xchg_sems(nx),
        compiler_params=_cp(("arbitrary",), 48),
    )(*_hbm(dp, w_in_g, x, dx1), modr, g_pre, *[_hbm(pb) for pb in pairs_b])
    return outs[:2], outs[2:]


def _weight_grad(a, b, a_blocked, b_blocked, name):
    T = a.shape[0]
    tt = min(GRAD_TILE, T)
    nt = T // tt
    ka = a.shape[1] // N_CHIPS if a_blocked else a.shape[1]
    nb = b.shape[1] // N_CHIPS if b_blocked else b.shape[1]

    def body(a_ref, b_ref, o_ref, ob_ref):
        t = pl.program_id(1)

        @pl.when(t == 0)
        def _():
            o_ref[...] = jnp.zeros(o_ref.shape, F32)

        for cols in _row_chains(nb):
            o_ref[0, :, cols] += _dot_tn(a_ref[...], b_ref[:, cols])

        @pl.when(t == nt - 1)
        def _():
            ob_ref[0] = o_ref[0].astype(BF16)

    blk = pl.BlockSpec((1, ka, nb), lambda j, t: (j, 0, 0))
    return pl.pallas_call(
        body, name=name, grid=(N_CHIPS, nt),
        in_specs=[pl.BlockSpec((tt, ka), (lambda j, t: (t, j)) if a_blocked else (lambda j, t: (t, 0))),
                  pl.BlockSpec((tt, nb), (lambda j, t: (t, j)) if b_blocked else (lambda j, t: (t, 0)))],
        out_specs=[blk, blk],
        out_shape=[_big((N_CHIPS, ka, nb), F32), _big((N_CHIPS, ka, nb), BF16)],
        compiler_params=_cp(("arbitrary", "arbitrary"), 48),
    )(*_hbm(a, b))


R_LOSS = 0
R_FFN = 8
R_OUT = 16
R_IN = 24
R_BIN = 32
R_512 = 40
R_DW = 48
N_STAT_ROWS = 80
MOD_ROWS = (R_IN + 0, R_IN + 1, R_OUT + 0, R_FFN + 0, R_FFN + 1, R_LOSS + 1)


def _small_update(gath, params):
    names = ["b_ada", "lb_logits", "g_pre_mix", "b_in", "b_dw", "gn_gain", "gn_bias", "g_hgrn_out", "g_post_mix",
             "g_pre_ffn", "g_post_ffn"]
    flat = []
    for n in names:
        flat += list(params[n])
    n_in = 1 + len(flat)

    def body(*refs):
        g_ref = refs[0]
        prm = {n: refs[1 + 3 * k:4 + 3 * k] for k, n in enumerate(names)}
        outs = refs[n_in:]
        loss_ref, dmod_ref, dwdw_ref = outs[0], outs[1], outs[2]
        res = {n: outs[3 + 4 * k:7 + 4 * k] for k, n in enumerate(names)}
        red = g_ref[0]
        for dev in range(1, N_DEV):
            red = red + g_ref[dev]
        loss_ref[...] = jnp.broadcast_to(
            (0.5 / D_MODEL) * jnp.sum(red[R_LOSS:R_LOSS + 1, :], axis=-1, keepdims=True), loss_ref.shape)
        for dev in range(N_DEV):
            for k, r in enumerate(MOD_ROWS):
                dmod_ref[dev:dev + 1, k * D_MODEL:(k + 1) * D_MODEL] = g_ref[dev, r:r + 1, :]
        dwdw_ref[...] = red[R_DW:R_DW + HALO, 0:CONV_CH]

        def finish(name, pieces):
            w_ref, m_ref, v_ref = prm[name]
            g_out, d_out, m_out, v_out = res[name]
            for rsl, lsl, g in pieces:
                d, m2, v2 = _adam_math(w_ref[rsl, lsl], g, m_ref[rsl, lsl], v_ref[rsl, lsl])
                g_out[rsl, lsl] = g
                d_out[rsl, lsl] = d
                m_out[rsl, lsl] = m2
                v_out[rsl, lsl] = v2

        one = slice(0, 1)
        row = lambda r: red[r:r + 1, :]
        half = lambda r: red[r:r + 1, 0:CONV_CH]
        finish("b_ada", [(one, slice(k * D_MODEL, (k + 1) * D_MODEL), row(r)) for k, r in enumerate(MOD_ROWS)])
        finish("b_in", [(one, slice(k * D_MODEL, (k + 1) * D_MODEL), row(R_BIN + k)) for k in range(3)])
        finish("g_pre_mix", [(one, slice(None), row(R_IN + 2))])
        finish("g_post_mix", [(one, slice(None), row(R_OUT + 1))])
        finish("g_pre_ffn", [(one, slice(None), row(R_FFN + 2))])
        finish("g_post_ffn", [(one, slice(None), row(R_LOSS + 2))])
        finish("b_dw", [(one, slice(None), half(R_512 + 0))])
        finish("gn_gain", [(one, slice(None), half(R_512 + 1))])
        finish("gn_bias", [(one, slice(None), half(R_512 + 2))])
        finish("g_hgrn_out", [(one, slice(None), half(R_512 + 3))])
        s0, s1 = _lower_bound(prm["lb_logits"][0])
        dlb = half(R_512 + 4)
        finish("lb_logits", [(slice(0, 1), slice(None), dlb * s0 * (1.0 - s0)),
                             (slice(1, 2), slice(None), -dlb * s0 * s1)])

    vm = pl.BlockSpec(memory_space=pltpu.VMEM)
    out_shape = [jax.ShapeDtypeStruct((8, 128), F32), jax.ShapeDtypeStruct((N_DEV, 6 * D_MODEL), F32),
                 jax.ShapeDtypeStruct((HALO, CONV_CH), F32)]
    for n in names:
        out_shape += [jax.ShapeDtypeStruct(params[n][0].shape, F32)] * 4
    outs = pl.pallas_call(
        body, name="small_update", out_shape=out_shape,
        in_specs=[vm] * n_in, out_specs=[vm] * len(out_shape),
        compiler_params=_cp(None, 32),
    )(gath, *flat)
    return outs[0], outs[1], outs[2], {n: outs[3 + 4 * k:7 + 4 * k] for k, n in enumerate(names)}


def _wdw_adam(w, g, m, v):
    def body(w_ref, g_ref, m_ref, v_ref, d_out, m_out, v_out):
        d, m2, v2 = _adam_math(w_ref[...], g_ref[...], m_ref[...], v_ref[...])
        d_out[...] = d
        m_out[...] = m2
        v_out[...] = v2

    vm = pl.BlockSpec(memory_space=pltpu.VMEM)
    return pl.pallas_call(
        body, name="wdw_adam", out_shape=[jax.ShapeDtypeStruct(w.shape, F32)] * 3,
        in_specs=[vm] * 4, out_specs=[vm] * 3, compiler_params=_cp(None, 16),
    )(w, g, m, v)


def kernel(x, c, w_ada, b_ada, lb_logits, g_pre_mix, w_in, b_in, w_dw, b_dw, gn_gain, gn_bias, g_hgrn_out, w_out, g_post_mix, g_pre_ffn, w_up, w_down, g_post_ffn, loss_target, m_w_ada, m_b_ada, m_lb_logits, m_g_pre_mix, m_w_in, m_b_in, m_w_dw, m_b_dw, m_gn_gain, m_gn_bias, m_g_hgrn_out, m_w_out, m_g_post_mix, m_g_pre_ffn, m_w_up, m_w_down, m_g_post_ffn, v_w_ada, v_b_ada, v_lb_logits, v_g_pre_mix, v_w_in, v_b_in, v_w_dw, v_b_dw, v_gn_gain, v_gn_bias, v_g_hgrn_out, v_w_out, v_g_post_mix, v_g_pre_ffn, v_w_up, v_w_down, v_g_post_ffn):
    ax, ay, ac = lax.axis_index("x"), lax.axis_index("y"), lax.axis_index("c")
    chip = 2 * ax + ay
    T = x.shape[1]
    xs, tgt = x[0], loss_target[0]
    ada_cols = w_ada.shape[2]

    b_sh = lax.dynamic_slice_in_dim(b_ada, chip * ada_cols, ada_cols, axis=1)
    wdw_pad = jnp.pad(w_dw[0], ((0, HALO - CONV_K), (0, 0)))
    chip1 = jnp.reshape(chip, (1,)).astype(jnp.int32)
    place = jnp.stack([ac, chip]).astype(jnp.int32)
    _, c8, modg, wdwg = _ada_exchange(c, w_ada[0], b_sh, wdw_pad)
    modr = modg.reshape(6, D_MODEL)
    wdw_all = jnp.transpose(wdwg, (1, 0, 2)).reshape(HALO, CONV_CH)
    vec = (b_dw, gn_gain, gn_bias, g_hgrn_out)

    p, h1, (up_buf, down_buf), w_in_g, w_out_g = _mix_in_fwd(
        chip1, xs, modr, g_pre_mix, b_in.reshape(N_CHIPS, 1, IN_COLS // N_CHIPS), [w_up[0], w_down[0]],
        _cast_own(chip1, w_in[0], "cast_w_in"), _cast_own(chip1, w_out[0], "cast_w_out"))
    w_out_f = w_out_g.reshape(D_MODEL, D_MODEL)
    (cat, ys, o, states, y, x1, h2), (w_up_g, w_down_g) = _mixers_fwd(
        p, wdw_all, vec, lb_logits, w_out_f, xs, modr, g_post_mix, g_pre_ffn, [up_buf, down_buf])
    w_down_f = w_down_g.reshape(D_FF, D_MODEL)
    r, dy2, dx2, st_loss = _ffn_fwd(h2, w_up_g, w_down_f, x1, tgt, modr, g_post_ffn)

    def pair_sums(grads, got, tags):
        return [_pair_sum(place, g, o_, "pair_sum_" + t) for (g, _), o_, t in zip(grads, got, tags)]

    da, dx1, st_ffn = _ffn_bwd(dy2, r, x1, dx2, w_up_g, w_down_f, modr, g_pre_ffn)
    g_up = _weight_grad(h2, da, False, True, "grad_w_up")
    g_down = _weight_grad(r, dy2, True, False, "grad_w_down")
    (dcat, st_out, g_out), got_ud = _mix_out_bwd(dx1, y, cat, w_out_f, modr, g_post_mix, [g_up[1], g_down[1]])
    g_out = g_out.reshape(N_CHIPS, D_MODEL // N_CHIPS, D_MODEL)
    got_o = _pair_swap([g_out], "pair_swap_w_out")
    early = pair_sums([(g_out, None), g_up, g_down], list(got_o) + list(got_ud), ["w_out", "w_up", "w_down"])
    (dp, st_bin, st_512, dwdw), got_early, g_in = _mixers_bwd(p, dcat, ys, o, states, h1, wdw_all, vec, lb_logits,
                                                              [pb for _, pb in early])
    late = pair_sums([g_in], _pair_swap([g_in[1]], "pair_swap_w_in"), ["w_in"])
    (grad_x, st_in), got_late = _mix_in_bwd(dp, w_in_g, xs, dx1, modr, g_pre_mix, [late[0][1]])
    fulls = [_chip_sum(place, pf, gb, "chip_sum_" + t)
             for (pf, _), gb, t in zip(late + early, list(got_late) + list(got_early), ["w_in", "w_out", "w_up", "w_down"])]

    pad_lanes = lambda s: jnp.pad(s, ((0, 0), (0, D_MODEL - s.shape[1])))
    stats = jnp.concatenate([st_loss, st_ffn, st_out, st_in, st_bin, pad_lanes(st_512), pad_lanes(dwdw)], axis=0)
    (g_w_in, g_w_out, g_w_up, g_w_down), gath = _final_exchange(fulls, stats)
    small = {"b_ada": (b_ada, m_b_ada, v_b_ada), "lb_logits": (lb_logits, m_lb_logits, v_lb_logits),
             "g_pre_mix": (g_pre_mix, m_g_pre_mix, v_g_pre_mix), "b_in": (b_in, m_b_in, v_b_in),
             "b_dw": (b_dw, m_b_dw, v_b_dw), "gn_gain": (gn_gain, m_gn_gain, v_gn_gain),
             "gn_bias": (gn_bias, m_gn_bias, v_gn_bias), "g_hgrn_out": (g_hgrn_out, m_g_hgrn_out, v_g_hgrn_out),
             "g_post_mix": (g_post_mix, m_g_post_mix, v_g_post_mix), "g_pre_ffn": (g_pre_ffn, m_g_pre_ffn, v_g_pre_ffn),
             "g_post_ffn": (g_post_ffn, m_g_post_ffn, v_g_post_ffn)}
    loss_t, dmod_all, dwdw_sum, sres = _small_update(gath, small)
    loss = loss_t[0, 0]

    res = dict(sres)
    dmod_sh = lax.dynamic_slice_in_dim(dmod_all, chip * ada_cols, ada_cols, axis=1)
    res["w_ada"] = [t[None] for t in _ada_grad_adam(jnp.transpose(c8), dmod_sh, w_ada[0], m_w_ada[0], v_w_ada[0])]
    g_wdw = lax.dynamic_slice_in_dim(dwdw_sum, chip * HEAD_D, HEAD_D, axis=1)[:CONV_K][None]
    res["w_dw"] = [g_wdw] + list(_wdw_adam(w_dw, g_wdw, m_w_dw, v_w_dw))
    for name, g, w, m, v in (("w_in", g_w_in, w_in, m_w_in, v_w_in), ("w_out", g_w_out, w_out, m_w_out, v_w_out),
                             ("w_up", g_w_up, w_up, m_w_up, v_w_up), ("w_down", g_w_down, w_down, m_w_down, v_w_down)):
        d, m2, v2 = _adam_big(w[0], g, m[0], v[0], "adam_" + name)
        res[name] = [g[None], d[None], m2[None], v2[None]]

    order = ["w_ada", "b_ada", "lb_logits", "g_pre_mix", "w_in", "b_in", "w_dw", "b_dw", "gn_gain", "gn_bias",
             "g_hgrn_out", "w_out", "g_post_mix", "g_pre_ffn", "w_up", "w_down", "g_post_ffn"]
    out = [loss, grad_x[None]]
    for k in range(4):
        out += [res[n][k] for n in order]
    return tuple(out)
```

```python
import jax
import jax.numpy as jnp
from jax import lax
from jax.experimental import pallas as pl
from jax.experimental.pallas import tpu as pltpu

F32, BF16 = jnp.float32, jnp.bfloat16
D_MODEL = 1024
CONV_CH = 512
HGRN_W = 512
N_HEADS = 4
HEAD_D = 128
CONV_K = 31
GN_GROUP = 64
GN_SHIFT = 6
IN_COLS = 3072
D_FF = 4096
CHUNK = 64
CHUNK_SHIFT = 6
N_CHIPS = 4
N_DEV = 8
RMS_EPS = 1e-6
GN_EPS = 1e-5
ADAM_LR, ADAM_B1, ADAM_B2, ADAM_EPS, ADAM_WD, ADAM_STEP = 0.001, 0.9, 0.999, 1e-08, 0.01, 10
TOK_TILE = 512
MIXIN_TILE = 1024
MIXB_TILE = 256
FFN_TILE = 1024
FFN_BLOCK = 512
GRAD_TILE = 4096
HALO = 32
SUB = 8
LANE = 128
CONV_ROWS = 128
CHUNK_UNROLL = 8
MIB = 1 << 20
MESH = pl.DeviceIdType.MESH
OTHER_CHIPS = ((0, 1), (1, 0), (1, 1))


def _cp(sem=None, vmem_mib=48):
    return pltpu.CompilerParams(dimension_semantics=sem, vmem_limit_bytes=vmem_mib * MIB)


def _dot(a, b):
    return jnp.dot(a, b, preferred_element_type=F32)


def _dot_nt(a, b):
    return lax.dot_general(a, b, (((1,), (1,)), ((), ())), preferred_element_type=F32)


def _dot_tn(a, b):
    return lax.dot_general(a, b, (((0,), (0,)), ((), ())), preferred_element_type=F32)


def _sig(v):
    return 0.5 * jnp.tanh(0.5 * v) + 0.5


def _colsum(v):
    return jnp.sum(v, axis=0, keepdims=True)


def _flip(v, b):
    return 1 - v if b else v


def _rcopy(src, dst, ssem, rsem, dev):
    return pltpu.make_async_remote_copy(src_ref=src, dst_ref=dst, send_sem=ssem, recv_sem=rsem,
                                        device_id=dev, device_id_type=MESH)


def _place():
    return lax.axis_index("x"), lax.axis_index("y"), lax.axis_index("c")


def _full(shape):
    return pl.BlockSpec(shape, lambda *_: (0,) * len(shape))


def _big(shape, dtype):
    return pltpu.HBM(shape, dtype)


def _hbm(*arrays):
    out = [pltpu.with_memory_space_constraint(a, pltpu.HBM) for a in arrays]
    return out[0] if len(out) == 1 else out


def _split2(v):
    hi = v.astype(BF16)
    lo = (v - hi.astype(F32)).astype(BF16)
    return hi, lo


def _split3(v):
    h1 = v.astype(BF16)
    r1 = v - h1.astype(F32)
    h2 = r1.astype(BF16)
    h3 = (r1 - h2.astype(F32)).astype(BF16)
    return h1, h2, h3


def _mm3(mat, v):
    h1, h2, h3 = _split3(v)
    return _dot(mat, h1) + _dot(mat, h2) + _dot(mat, h3)


def _gn_matrix():
    r = lax.broadcasted_iota(jnp.int32, (CONV_CH, CONV_CH), 0) >> GN_SHIFT
    c = lax.broadcasted_iota(jnp.int32, (CONV_CH, CONV_CH), 1) >> GN_SHIFT
    return jnp.where(r == c, 1.0 / GN_GROUP, 0.0).astype(BF16)


def _gmean(v, gmat):
    hi, lo = _split2(v)
    return _dot(hi, gmat) + _dot(lo, gmat)


def _chunk_masks(tm):
    r = lax.broadcasted_iota(jnp.int32, (tm, tm), 0)
    c = lax.broadcasted_iota(jnp.int32, (tm, tm), 1)
    same = (r >> CHUNK_SHIFT) == (c >> CHUNK_SHIFT)
    one = lambda m: jnp.where(m, 1.0, 0.0).astype(BF16)
    return one(same & (c <= r)), one(same & (c >= r)), one(same)


def _tri():
    return lax.broadcasted_iota(jnp.int32, (CHUNK, CHUNK), 0) >= lax.broadcasted_iota(jnp.int32, (CHUNK, CHUNK), 1)


def _lower_bound(lbl_ref):
    l0, l1 = lbl_ref[0:1, :], lbl_ref[1:2, :]
    mx = jnp.maximum(l0, l1)
    e0, e1 = jnp.exp(l0 - mx), jnp.exp(l1 - mx)
    return e0 / (e0 + e1), e1 / (e0 + e1)


CONV_FWD_TAPS = tuple((j, HALO - (CONV_K - 1) + j) for j in range(CONV_K))
CONV_BWD_TAPS = tuple((j, (CONV_K - 1) - j) for j in range(CONV_K))


def _tap_conv(src_ref, w_ref, row0, taps, lanes):
    acc = None
    for b in range(SUB):
        pb = None
        for j, off in taps:
            if off % SUB == b:
                lo = row0 + off - b
                term = w_ref[j:j + 1, lanes] * src_ref[lo:lo + CONV_ROWS + SUB, lanes]
                pb = term if pb is None else pb + term
        if pb is not None:
            sh = pb[b:b + CONV_ROWS, :]
            acc = sh if acc is None else acc + sh
    return acc


def _hgrn_prep(pq, pf, lb, lower):
    sq = _sig(pq)
    qf = pq * sq
    sf = _sig(pf)
    f = lb + (1.0 - lb) * sf
    logf = jnp.log(f)
    k = 1.0 - f
    G = _mm3(lower, logf)
    rows, cols = G.shape
    g3 = G.reshape(rows // CHUNK, CHUNK, cols)
    Gl = jnp.broadcast_to(g3[:, CHUNK - 1:CHUNK, :], g3.shape).reshape(rows, cols)
    eG, enG, eGlG = jnp.exp(G), jnp.exp(-G), jnp.exp(Gl - G)
    return dict(sq=sq, sf=sf, f=f, Gl=Gl, eG=eG, enG=enG, eGlG=eGlG, qt=qf * eG, kt=k * enG, kh=k * eGlG)


def _ada_exchange(c_row, w_ada, b_sh, wdw_pad):
    ncol = w_ada.shape[1]

    def body(c_ref, w_ref, b_ref, wdw_ref, call_ref, c8_ref, modg_ref, wdwg_ref, rows_s, sa, ra, sw, rw, sm, rm):
        x, y, c = _place()
        me = 4 * x + 2 * y + c
        chip = 2 * x + y
        cv = c_ref[...]
        call_ref[me] = cv * _sig(cv)
        wdwg_ref[chip] = wdw_ref[...]
        sends = []
        for m in range(1, N_DEV):
            peer = (_flip(x, m >> 2), _flip(y, (m >> 1) & 1), _flip(c, m & 1))
            cp = _rcopy(call_ref.at[me], call_ref.at[me], sa.at[m - 1], ra.at[m - 1], peer)
            cp.start()
            sends.append(cp)
        for k, (fx, fy) in enumerate(OTHER_CHIPS):
            peer = (_flip(x, fx), _flip(y, fy), c)
            cp = _rcopy(wdwg_ref.at[chip], wdwg_ref.at[chip], sw.at[k], rw.at[k], peer)
            cp.start()
            sends.append(cp)
        for m in range(1, N_DEV):
            peer = (_flip(x, m >> 2), _flip(y, (m >> 1) & 1), _flip(c, m & 1))
            pid = 4 * peer[0] + 2 * peer[1] + peer[2]
            _rcopy(call_ref.at[pid], call_ref.at[pid], sa.at[m - 1], ra.at[m - 1], peer).wait_recv()
        for b in range(N_DEV):
            c8_ref[b:b + 1, :] = call_ref[b]
        mod_all = _dot(c8_ref[...].astype(BF16), w_ref[...].astype(BF16)) + b_ref[...]
        for b in range(N_DEV):
            rows_s[b] = mod_all[b:b + 1, :]
        modg_ref[chip] = rows_s[me]
        for k, (fx, fy) in enumerate(OTHER_CHIPS):
            peer = (_flip(x, fx), _flip(y, fy), c)
            pid = 4 * peer[0] + 2 * peer[1] + peer[2]
            cp = _rcopy(rows_s.at[pid], modg_ref.at[chip], sm.at[k], rm.at[k], peer)
            cp.start()
            sends.append(cp)
        for k, (fx, fy) in enumerate(OTHER_CHIPS):
            peer = (_flip(x, fx), _flip(y, fy), c)
            pchip = 2 * peer[0] + peer[1]
            _rcopy(rows_s.at[0], modg_ref.at[pchip], sm.at[k], rm.at[k], peer).wait_recv()
            _rcopy(wdwg_ref.at[pchip], wdwg_ref.at[pchip], sw.at[k], rw.at[k], peer).wait_recv()
        for cp in sends:
            cp.wait_send()

    vm = pl.BlockSpec(memory_space=pltpu.VMEM)
    return pl.pallas_call(
        body, name="ada_exchange",
        out_shape=[jax.ShapeDtypeStruct((N_DEV, 1, D_MODEL), F32), jax.ShapeDtypeStruct((N_DEV, D_MODEL), F32),
                   jax.ShapeDtypeStruct((N_CHIPS, 1, ncol), F32), jax.ShapeDtypeStruct((N_CHIPS, HALO, HEAD_D), F32)],
        in_specs=[vm] * 4, out_specs=[vm] * 4,
        scratch_shapes=[pltpu.VMEM((N_DEV, 1, ncol), F32),
                        pltpu.SemaphoreType.DMA((N_DEV - 1,)), pltpu.SemaphoreType.DMA((N_DEV - 1,)),
                        pltpu.SemaphoreType.DMA((3,)), pltpu.SemaphoreType.DMA((3,)),
                        pltpu.SemaphoreType.DMA((3,)), pltpu.SemaphoreType.DMA((3,))],
        compiler_params=_cp(None, 32),
    )(c_row, w_ada, b_sh, wdw_pad)


def _cast_own(chip1, shard, name):
    rows, cols = shard.shape
    tr = _row_tile(rows)

    def body(ch_ref, s_ref, o_ref):
        o_ref[0] = s_ref[...].astype(BF16)

    gs = pltpu.PrefetchScalarGridSpec(
        num_scalar_prefetch=1, grid=(rows // tr,),
        in_specs=[pl.BlockSpec((tr, cols), lambda i, ch: (i, 0))],
        out_specs=pl.BlockSpec((1, tr, cols), lambda i, ch: (ch[0], i, 0)))
    return pl.pallas_call(
        body, name=name, grid_spec=gs, out_shape=_big((N_CHIPS, rows, cols), BF16),
        compiler_params=_cp(("arbitrary",), 32),
    )(chip1, _hbm(shard))


def _slab(buf, ch, core):
    hs = buf.shape[1] // 2
    return buf.at[ch, pl.ds(core * hs, hs), :]


def _gather_start(bufs, ssem, rsem, relations=(0, 1, 2)):
    x, y, c = _place()
    chip = 2 * x + y
    for k in relations:
        fx, fy = OTHER_CHIPS[k]
        peer = (_flip(x, fx), _flip(y, fy), c)
        for t, buf in enumerate(bufs):
            _rcopy(_slab(buf, chip, c), _slab(buf, chip, c), ssem.at[t * 3 + k], rsem.at[t * 3 + k], peer).start()


def _gather_pass_on(bufs, ssem, rsem):
    nt = len(bufs)
    x, y, c = _place()
    sibling = (x, y, 1 - c)
    for k, (fx, fy) in enumerate(OTHER_CHIPS):
        peer = (_flip(x, fx), _flip(y, fy), c)
        pchip = 2 * peer[0] + peer[1]
        for t, buf in enumerate(bufs):
            _rcopy(_slab(buf, pchip, c), _slab(buf, pchip, c), ssem.at[t * 3 + k], rsem.at[t * 3 + k], peer).wait_recv()
            _rcopy(_slab(buf, pchip, c), _slab(buf, pchip, c), ssem.at[3 * nt + t * 3 + k],
                   rsem.at[3 * nt + t * 3 + k], sibling).start()


def _gather_drain(bufs, ssem, rsem):
    nt = len(bufs)
    x, y, c = _place()
    chip = 2 * x + y
    sibling = (x, y, 1 - c)
    for k, (fx, fy) in enumerate(OTHER_CHIPS):
        peer = (_flip(x, fx), _flip(y, fy), c)
        pchip = 2 * peer[0] + peer[1]
        for t, buf in enumerate(bufs):
            _rcopy(_slab(buf, pchip, 1 - c), _slab(buf, pchip, 1 - c), ssem.at[3 * nt + t * 3 + k],
                   rsem.at[3 * nt + t * 3 + k], sibling).wait_recv()
            _rcopy(_slab(buf, chip, c), _slab(buf, chip, c), ssem.at[t * 3 + k], rsem.at[t * 3 + k], peer).wait_send()
            _rcopy(_slab(buf, pchip, c), _slab(buf, pchip, c), ssem.at[3 * nt + t * 3 + k],
                   rsem.at[3 * nt + t * 3 + k], sibling).wait_send()


def _gather_finish(bufs, ssem, rsem):
    _gather_pass_on(bufs, ssem, rsem)
    _gather_drain(bufs, ssem, rsem)


def _gather_arrive(bufs, k, ssem, rsem):
    nt = len(bufs)
    x, y, c = _place()
    fx, fy = OTHER_CHIPS[k]
    peer = (_flip(x, fx), _flip(y, fy), c)
    pchip = 2 * peer[0] + peer[1]
    for t, buf in enumerate(bufs):
        _rcopy(_slab(buf, pchip, c), _slab(buf, pchip, c), ssem.at[t * 3 + k], rsem.at[t * 3 + k], peer).wait_recv()
        _rcopy(_slab(buf, pchip, c), _slab(buf, pchip, c), ssem.at[3 * nt + t * 3 + k],
               rsem.at[3 * nt + t * 3 + k], (x, y, 1 - c)).start()
    for t, buf in enumerate(bufs):
        _rcopy(_slab(buf, pchip, 1 - c), _slab(buf, pchip, 1 - c), ssem.at[3 * nt + t * 3 + k],
               rsem.at[3 * nt + t * 3 + k], (x, y, 1 - c)).wait_recv()


def _gather_sends_done(bufs, ssem, rsem):
    nt = len(bufs)
    x, y, c = _place()
    chip = 2 * x + y
    for k, (fx, fy) in enumerate(OTHER_CHIPS):
        peer = (_flip(x, fx), _flip(y, fy), c)
        pchip = 2 * peer[0] + peer[1]
        for t, buf in enumerate(bufs):
            _rcopy(_slab(buf, chip, c), _slab(buf, chip, c), ssem.at[t * 3 + k], rsem.at[t * 3 + k], peer).wait_send()
            _rcopy(_slab(buf, pchip, c), _slab(buf, pchip, c), ssem.at[3 * nt + t * 3 + k],
                   rsem.at[3 * nt + t * 3 + k], (x, y, 1 - c)).wait_send()


def _ring_parts(buf):
    x, y, c = _place()
    ynb, xnb = (x, 1 - y, c), (1 - x, y, c)
    ychip, xchip, dchip = 2 * x + (1 - y), 2 * (1 - x) + y, 2 * (1 - x) + (1 - y)
    hs = buf.shape[1] // 2

    def piece(ch, q):
        return buf.at[ch, pl.ds(c * hs + q * (hs // 2), hs // 2), :]

    return ynb, xnb, ychip, xchip, dchip, piece


def _ring_start(bufs, ssem, rsem):
    x, y, c = _place()
    chip = 2 * x + y
    for t, buf in enumerate(bufs):
        ynb, xnb, _, _, _, _ = _ring_parts(buf)
        _rcopy(_slab(buf, chip, c), _slab(buf, chip, c), ssem.at[2 * t], rsem.at[2 * t], ynb).start()
        _rcopy(_slab(buf, chip, c), _slab(buf, chip, c), ssem.at[2 * t + 1], rsem.at[2 * t + 1], xnb).start()


def _ring_forward(bufs, ssem, rsem):
    nt = len(bufs)
    _, _, c = _place()
    for t, buf in enumerate(bufs):
        ynb, xnb, ychip, xchip, _, piece = _ring_parts(buf)
        _rcopy(_slab(buf, ychip, c), _slab(buf, ychip, c), ssem.at[2 * t], rsem.at[2 * t], ynb).wait_recv()
        _rcopy(piece(ychip, 0), piece(ychip, 0), ssem.at[2 * nt + 2 * t], rsem.at[2 * nt + 2 * t], xnb).start()
        _rcopy(_slab(buf, xchip, c), _slab(buf, xchip, c), ssem.at[2 * t + 1], rsem.at[2 * t + 1], xnb).wait_recv()
        _rcopy(piece(xchip, 1), piece(xchip, 1), ssem.at[2 * nt + 2 * t + 1], rsem.at[2 * nt + 2 * t + 1], ynb).start()


def _ring_finish(bufs, ssem, rsem):
    nt = len(bufs)
    x, y, c = _place()
    chip = 2 * x + y
    sibling = (x, y, 1 - c)
    for t, buf in enumerate(bufs):
        ynb, xnb, ychip, xchip, dchip, piece = _ring_parts(buf)
        _rcopy(piece(dchip, 0), piece(dchip, 0), ssem.at[2 * nt + 2 * t], rsem.at[2 * nt + 2 * t], xnb).wait_recv()
        _rcopy(piece(dchip, 1), piece(dchip, 1), ssem.at[2 * nt + 2 * t + 1], rsem.at[2 * nt + 2 * t + 1],
               ynb).wait_recv()
        for k, ch in enumerate((ychip, xchip, dchip)):
            _rcopy(_slab(buf, ch, c), _slab(buf, ch, c), ssem.at[4 * nt + 3 * t + k], rsem.at[4 * nt + 3 * t + k],
                   sibling).start()
    for t, buf in enumerate(bufs):
        ynb, xnb, ychip, xchip, dchip, piece = _ring_parts(buf)
        for k, ch in enumerate((ychip, xchip, dchip)):
            _rcopy(_slab(buf, ch, 1 - c), _slab(buf, ch, 1 - c), ssem.at[4 * nt + 3 * t + k],
                   rsem.at[4 * nt + 3 * t + k], sibling).wait_recv()
            _rcopy(_slab(buf, ch, c), _slab(buf, ch, c), ssem.at[4 * nt + 3 * t + k], rsem.at[4 * nt + 3 * t + k],
                   sibling).wait_send()
        _rcopy(_slab(buf, chip, c), _slab(buf, chip, c), ssem.at[2 * t], rsem.at[2 * t], ynb).wait_send()
        _rcopy(_slab(buf, chip, c), _slab(buf, chip, c), ssem.at[2 * t + 1], rsem.at[2 * t + 1], xnb).wait_send()
        _rcopy(piece(ychip, 0), piece(ychip, 0), ssem.at[2 * nt + 2 * t], rsem.at[2 * nt + 2 * t], xnb).wait_send()
        _rcopy(piece(xchip, 1), piece(xchip, 1), ssem.at[2 * nt + 2 * t + 1], rsem.at[2 * nt + 2 * t + 1],
               ynb).wait_send()


def _ring_sems(nt):
    return [pltpu.SemaphoreType.DMA((7 * nt,)), pltpu.SemaphoreType.DMA((7 * nt,))]


def _gather_sems(nt):
    return [pltpu.SemaphoreType.DMA((6 * nt,)), pltpu.SemaphoreType.DMA((6 * nt,))]


def _pair_copies(ins, outs, ssem, rsem):
    x, y, c = _place()
    copies = []
    for t in range(len(ins)):
        hs = ins[t].shape[1] // 2
        copies.append(_rcopy(ins[t].at[:, pl.ds((1 - c) * hs, hs), :], outs[t], ssem.at[t], rsem.at[t], (x, y, 1 - c)))
    return copies


def _pair_shapes(grads):
    return [_big((g.shape[0], g.shape[1] // 2, g.shape[2]), g.dtype) for g in grads]


def _pair_sems(nt):
    return [pltpu.SemaphoreType.DMA((nt,)), pltpu.SemaphoreType.DMA((nt,))]


def _pair_swap(grads, name):
    nt = len(grads)
    hbm = pl.BlockSpec(memory_space=pl.ANY)

    def body(*refs):
        copies = _pair_copies(refs[:nt], refs[nt:2 * nt], refs[2 * nt], refs[2 * nt + 1])
        for cp in copies:
            cp.start()
        for cp in copies:
            cp.wait_recv()
        for cp in copies:
            cp.wait_send()

    return pl.pallas_call(
        body, name=name, out_shape=_pair_shapes(grads), in_specs=[hbm] * nt, out_specs=[hbm] * nt,
        scratch_shapes=_pair_sems(nt),
    )(*[_hbm(g) for g in grads])


def _xchg_copies(ins, outs, ssem, rsem):
    x, y, c = _place()
    copies = []
    for k, (fx, fy) in enumerate(OTHER_CHIPS):
        peer = (_flip(x, fx), _flip(y, fy), c)
        for t in range(len(ins)):
            copies.append(_rcopy(ins[t].at[k], outs[t].at[k], ssem.at[t * 3 + k], rsem.at[t * 3 + k], peer))
    return copies


def _xchg_sems(nt):
    return [pltpu.SemaphoreType.DMA((3 * nt,)), pltpu.SemaphoreType.DMA((3 * nt,))]


def _final_exchange(fulls, stats):
    nt = len(fulls)
    rows, cols = stats.shape
    hbm = pl.BlockSpec(memory_space=pl.ANY)
    vm = pl.BlockSpec(memory_space=pltpu.VMEM)

    def body(*refs):
        ins, s_ref = refs[:nt], refs[nt]
        outs, g_ref = refs[nt + 1:2 * nt + 1], refs[2 * nt + 1]
        hssem, hrsem, ssem, rsem = refs[2 * nt + 2:]
        x, y, c = _place()
        me, sibling = (x, y, c), (x, y, 1 - c)
        halves = []
        for t in range(nt):
            hs = ins[t].shape[0] // 2
            mine = pl.ds(c * hs, hs)
            cp = _rcopy(ins[t].at[mine, :], outs[t].at[mine, :], hssem.at[t], hrsem.at[t], sibling)
            cp.start()
            halves.append(cp)

        chips = [(_flip(x, fx), _flip(y, fy)) for fx, fy in OTHER_CHIPS]

        def blk(px, py, pc):
            return g_ref.at[4 * px + 2 * py + pc]

        def copy(k, block, to, src=None):
            return _rcopy(blk(*block) if src is None else src, blk(*block), ssem.at[k], rsem.at[k], to)

        g_ref[4 * x + 2 * y + c] = s_ref[...]
        first = [copy(0, me, sibling, src=s_ref)]
        first += [copy(1 + j, me, (*chip, c), src=s_ref) for j, chip in enumerate(chips)]
        for cp in first:
            cp.start()
        passed = [copy(4 + j, (*chip, c), sibling) for j, chip in enumerate(chips)]
        for j, chip in enumerate(chips):
            copy(1 + j, (*chip, c), me).wait_recv()
            passed[j].start()
        copy(0, sibling, me).wait_recv()
        for j, chip in enumerate(chips):
            copy(4 + j, (*chip, 1 - c), me).wait_recv()
        for t in range(nt):
            hs = ins[t].shape[0] // 2
            other = pl.ds((1 - c) * hs, hs)
            _rcopy(ins[t].at[other, :], outs[t].at[other, :], hssem.at[t], hrsem.at[t], sibling).wait_recv()
        for cp in first + passed + halves:
            cp.wait_send()

    outs = pl.pallas_call(
        body, name="final_exchange",
        out_shape=[_big(f.shape, F32) for f in fulls] + [jax.ShapeDtypeStruct((N_DEV, rows, cols), F32)],
        in_specs=[hbm] * nt + [vm], out_specs=[hbm] * nt + [vm],
        input_output_aliases={t: t for t in range(nt)},
        scratch_shapes=[pltpu.SemaphoreType.DMA((nt,)), pltpu.SemaphoreType.DMA((nt,)),
                        pltpu.SemaphoreType.DMA((7,)), pltpu.SemaphoreType.DMA((7,))],
        compiler_params=_cp(None, 32),
    )(*[_hbm(f) for f in fulls], stats)
    return outs[:nt], outs[nt]


def _row_tile(rows):
    return min(rows, 512)


def _pair_sum(place, grad, got, name):
    nb, hs, cols = got.shape
    tr = _row_tile(hs)
    nr = hs // tr

    def body(pl_ref, g_ref, o_ref, pf_ref, pb_ref):
        j = pl.program_id(1)
        s = g_ref[0] + o_ref[0].astype(F32)

        @pl.when(j == 0)
        def _():
            pf_ref[...] = s

        @pl.when(j > 0)
        def _():
            pb_ref[0] = s.astype(BF16)

    gs = pltpu.PrefetchScalarGridSpec(
        num_scalar_prefetch=1, grid=(nr, nb),
        in_specs=[pl.BlockSpec((1, tr, cols), lambda i, j, p: (p[1] ^ j, p[0] * nr + i, 0)),
                  pl.BlockSpec((1, tr, cols), lambda i, j, p: (p[1] ^ j, i, 0))],
        out_specs=[pl.BlockSpec((tr, cols), lambda i, j, p: (i, 0)),
                   pl.BlockSpec((1, tr, cols), lambda i, j, p: (jnp.maximum(j - 1, 0), i, 0))])
    return pl.pallas_call(
        body, name=name, grid_spec=gs,
        out_shape=[_big((hs, cols), F32), _big((nb - 1, hs, cols), BF16)],
        compiler_params=_cp(("arbitrary", "arbitrary"), 32),
    )(place, *_hbm(grad, got))


def _chip_sum(place, pair_f, got_b, name):
    nb, hs, cols = got_b.shape
    tr = _row_tile(hs)
    nr = hs // tr

    def body(pl_ref, pf_ref, gb_ref, o_ref):
        acc = pf_ref[...]
        for k in range(nb):
            acc = acc + gb_ref[k].astype(F32)
        o_ref[...] = acc

    gs = pltpu.PrefetchScalarGridSpec(
        num_scalar_prefetch=1, grid=(nr,),
        in_specs=[pl.BlockSpec((tr, cols), lambda i, p: (i, 0)),
                  pl.BlockSpec((nb, tr, cols), lambda i, p: (0, i, 0))],
        out_specs=pl.BlockSpec((tr, cols), lambda i, p: (p[0] * nr + i, 0)))
    return pl.pallas_call(
        body, name=name, grid_spec=gs,
        out_shape=_big((2 * hs, cols), F32),
        compiler_params=_cp(("arbitrary",), 32),
    )(place, *_hbm(pair_f, got_b))


def _adam_math(w, g, m, v):
    m2 = ADAM_B1 * m + (1.0 - ADAM_B1) * g
    v2 = ADAM_B2 * v + (1.0 - ADAM_B2) * (g * g)
    m_hat = m2 / (1.0 - ADAM_B1 ** ADAM_STEP)
    v_hat = v2 / (1.0 - ADAM_B2 ** ADAM_STEP)
    delta = -ADAM_LR * (m_hat / (jnp.sqrt(v_hat) + ADAM_EPS) + ADAM_WD * w)
    return delta, m2, v2


def _adam_big(w, g, m, v, name):
    rows, cols = w.shape
    tr = _row_tile(rows)

    def body(w_ref, g_ref, m_ref, v_ref, d_out, m_out, v_out):
        d, m2, v2 = _adam_math(w_ref[...], g_ref[...], m_ref[...], v_ref[...])
        d_out[...] = d
        m_out[...] = m2
        v_out[...] = v2

    spec = pl.BlockSpec((tr, cols), lambda i: (i, 0))
    return pl.pallas_call(
        body, name=name, grid=(rows // tr,), in_specs=[spec] * 4, out_specs=[spec] * 3,
        out_shape=[_big(w.shape, F32)] * 3,
        compiler_params=_cp(("arbitrary",), 48),
    )(*_hbm(w, g, m, v))


def _ada_grad_adam(c8t, dmod_sh, w, m, v):
    rows, cols = w.shape
    tr = _row_tile(rows) // 2

    def body(ct_ref, dm_ref, w_ref, m_ref, v_ref, g_out, d_out, m_out, v_out):
        g = None
        for b in range(N_DEV):
            term = ct_ref[:, b:b + 1] * dm_ref[b:b + 1, :]
            g = term if g is None else g + term
        d, m2, v2 = _adam_math(w_ref[...], g, m_ref[...], v_ref[...])
        g_out[...] = g
        d_out[...] = d
        m_out[...] = m2
        v_out[...] = v2

    spec = pl.BlockSpec((tr, cols), lambda i: (i, 0))
    return pl.pallas_call(
        body, name="ada_grad_adam", grid=(rows // tr,),
        in_specs=[pl.BlockSpec((tr, N_DEV), lambda i: (i, 0)), _full((N_DEV, cols)), spec, spec, spec],
        out_specs=[spec] * 4, out_shape=[_big(w.shape, F32)] * 4,
        compiler_params=_cp(("arbitrary",), 32),
    )(c8t, dmod_sh, *_hbm(w, m, v))


def _tok_tile(t):
    return min(TOK_TILE, t)


def _mix_in_fwd(chip1, x, modr, g_pre, b_in4, later, w_in_buf, w_out_buf):
    T = x.shape[0]
    tm = min(MIXIN_TILE, T)
    nt = T // tm
    nb = IN_COLS // N_CHIPS
    nl = len(later)

    def body(*refs):
        ch_ref, x_ref, mod_ref, g_ref, b_ref = refs[:5]
        l_ins = refs[5:5 + nl]
        p_ref, h_ref = refs[7 + nl:9 + nl]
        l_outs = refs[9 + nl:9 + 2 * nl]
        win_ref, wout_ref = refs[9 + 2 * nl:11 + 2 * nl]
        h_all, wblk, lsem, is_sem, ir_sem, os_sem, or_sem = refs[11 + 2 * nl:]
        k, i = pl.program_id(0), pl.program_id(1)
        chip = ch_ref[0]

        @pl.when(k == 0)
        def _():
            for src, dst in zip(l_ins, l_outs):
                dst[0] = src[...].astype(BF16)

        def load_block(blk):
            cp = pltpu.make_async_copy(win_ref.at[blk], wblk, lsem)
            cp.start()
            cp.wait()

        @pl.when((k == 0) & (i == 0))
        def _():
            _gather_start([win_ref], is_sem, ir_sem, relations=(0, 1))
            load_block(chip)

        for r in range(N_CHIPS - 1):
            @pl.when((k == r + 1) & (i == 0))
            def _(r=r):
                _gather_arrive([win_ref], r, is_sem, ir_sem)
                if r == 0:
                    _gather_start([win_ref], is_sem, ir_sem, relations=(2,))
                if r == 1:
                    _gather_start([wout_ref], os_sem, or_sem)
                load_block(chip ^ (r + 1))

        rows = pl.ds(pl.multiple_of(i * tm, tm), tm)

        @pl.when(k == 0)
        def _():
            xv = x_ref[...]
            rstd = lax.rsqrt(jnp.mean(xv * xv, axis=-1, keepdims=True) + RMS_EPS)
            h = (xv * rstd) * g_ref[...] * (1.0 + mod_ref[1:2, :]) + mod_ref[0:1, :]
            hb = h.astype(BF16)
            h_ref[...] = hb
            h_all[rows, :] = hb

        p_ref[...] = _dot(h_all[rows, :], wblk[...]) + b_ref[chip ^ k]

        @pl.when((k == N_CHIPS - 1) & (i == nt - 1))
        def _():
            _gather_sends_done([win_ref], is_sem, ir_sem)
            _gather_finish([wout_ref], os_sem, or_sem)

    hbm = pl.BlockSpec(memory_space=pl.ANY)
    first_pass = lambda k, i, ch: (jnp.where(k == 0, i, nt - 1), 0)
    own_slot = lambda k, i, ch: (ch[0], jnp.where(k == 0, i, nt - 1), 0)
    gs = pltpu.PrefetchScalarGridSpec(
        num_scalar_prefetch=1, grid=(N_CHIPS, nt),
        in_specs=[pl.BlockSpec((tm, D_MODEL), first_pass), pl.BlockSpec((6, D_MODEL), lambda k, i, ch: (0, 0)),
                  pl.BlockSpec((1, D_MODEL), lambda k, i, ch: (0, 0)),
                  pl.BlockSpec((N_CHIPS, 1, nb), lambda k, i, ch: (0, 0, 0))]
        + [pl.BlockSpec((w.shape[0] // nt, w.shape[1]), first_pass) for w in later] + [hbm, hbm],
        out_specs=[pl.BlockSpec((tm, nb), lambda k, i, ch: (i, ch[0] ^ k)), pl.BlockSpec((tm, D_MODEL), first_pass)]
        + [pl.BlockSpec((1, w.shape[0] // nt, w.shape[1]), own_slot) for w in later] + [hbm, hbm],
        scratch_shapes=[pltpu.VMEM((T, D_MODEL), BF16), pltpu.VMEM((D_MODEL, nb), BF16), pltpu.SemaphoreType.DMA]
        + _gather_sems(1) + _gather_sems(1))
    outs = pl.pallas_call(
        body, name="mix_in_fwd", grid_spec=gs,
        out_shape=[_big((T, IN_COLS), F32), _big((T, D_MODEL), BF16)]
        + [_big((N_CHIPS,) + w.shape, BF16) for w in later]
        + [_big(w_in_buf.shape, BF16), _big(w_out_buf.shape, BF16)],
        input_output_aliases={5 + nl: 2 + nl, 6 + nl: 3 + nl},
        compiler_params=_cp(("arbitrary", "arbitrary"), 48),
    )(chip1, _hbm(x), modr, g_pre, b_in4, *[_hbm(w) for w in later], _hbm(w_in_buf), _hbm(w_out_buf))
    return outs[0], outs[1], outs[2:2 + nl], outs[2 + nl], outs[3 + nl]


def _mixers_fwd(p, wdw, vecs, lbl, w_out, x, modr, g_post, g_ffn, gbufs):
    T = p.shape[0]
    tm = _tok_tile(T)
    nt = T // tm
    nch = tm // CHUNK
    ng = len(gbufs)
    n_in, n_out = 12, 7

    def body(*refs):
        (p_ref, wdw_ref, bdw_ref, gain_ref, bias_ref, gout_ref, lbl_ref, wout_ref, x_ref, mod_ref, gp_ref,
         gf_ref) = refs[:n_in]
        cat_ref, ys_ref, o_ref, st_ref, y_ref, x1_ref, h2_ref = refs[n_in + ng:n_in + ng + n_out]
        gout_bufs = refs[n_in + ng + n_out:n_in + 2 * ng + n_out]
        (ubuf, state, qt_s, kt_s, kh_s, v_s, egl_s, lower_s, gmat_s, gssem,
         grsem) = refs[n_in + 2 * ng + n_out:]
        i = pl.program_id(0)

        @pl.when(i == 0)
        def _():
            _ring_start(gout_bufs, gssem, grsem)
            lower_s[...], _, _ = _chunk_masks(tm)
            gmat_s[...] = _gn_matrix()
            state[...] = jnp.zeros(state.shape, F32)
            ubuf[0:HALO, :] = jnp.zeros((HALO, CONV_CH), F32)
            ubuf[HALO + tm:HALO + tm + SUB, :] = jnp.zeros((SUB, CONV_CH), F32)

        @pl.when(i > 0)
        def _():
            ubuf[0:HALO, :] = ubuf[tm:tm + HALO, :]

        ubuf[HALO:HALO + tm, :] = p_ref[:, 0:CONV_CH] * _sig(p_ref[:, CONV_CH:2 * CONV_CH])
        for r in range(tm // CONV_ROWS):
            rows = slice(r * CONV_ROWS, (r + 1) * CONV_ROWS)
            for lb_ in range(CONV_CH // LANE):
                lanes = slice(lb_ * LANE, (lb_ + 1) * LANE)
                ys_ref[rows, lanes] = bdw_ref[:, lanes] + _tap_conv(ubuf, wdw_ref, r * CONV_ROWS, CONV_FWD_TAPS, lanes)
        gmat = gmat_s[...]
        yv = ys_ref[...]
        d = yv - _gmean(yv, gmat)
        rs = lax.rsqrt(_gmean(d * d, gmat) + GN_EPS)
        z = d * rs * gain_ref[...] + bias_ref[...]
        cat_ref[:, 0:CONV_CH] = (z * _sig(z)).astype(BF16)

        lb, _ = _lower_bound(lbl_ref)
        o0 = 2 * CONV_CH
        pr = _hgrn_prep(p_ref[:, o0:o0 + HGRN_W], p_ref[:, o0 + HGRN_W:o0 + 2 * HGRN_W], lb, lower_s[...])
        qt_s[...] = pr["qt"].astype(BF16)
        kt_s[...] = pr["kt"].astype(BF16)
        kh_s[...] = pr["kh"].astype(BF16)
        v_s[...] = p_ref[:, o0 + 2 * HGRN_W:o0 + 3 * HGRN_W].astype(BF16)
        egl_s[...] = jnp.exp(pr["Gl"])
        tri = _tri()

        def chunk(ci, carry):
            r0 = pl.multiple_of(ci * CHUNK, CHUNK)
            rows = pl.ds(r0, CHUNK)
            for h in range(N_HEADS):
                ls = pl.ds(h * HEAD_D, HEAD_D)
                qc, kc, hc, vc = qt_s[rows, ls], kt_s[rows, ls], kh_s[rows, ls], v_s[rows, ls]
                s0 = state[h]
                s0b = s0.astype(BF16)
                st_ref[ci, h] = s0
                att =jnp.where(tri, _dot_nt(qc, kc), 0.0).astype(BF16)
                o_ref[rows, ls] = _dot(att, vc) + _dot_nt(qc, s0b)
                state[h] = s0 * egl_s[pl.ds(r0, 1), ls] + _dot_tn(vc, hc)
            return carry

        lax.fori_loop(0, nch, chunk, 0, unroll=min(CHUNK_UNROLL, nch))
        for h in range(N_HEADS):
            sl = slice(h * HEAD_D, (h + 1) * HEAD_D)
            oh = o_ref[:, sl]
            gh = p_ref[:, o0 + 3 * HGRN_W + h * HEAD_D:o0 + 3 * HGRN_W + (h + 1) * HEAD_D]
            rsh = lax.rsqrt(jnp.mean(oh * oh, axis=-1, keepdims=True) + RMS_EPS)
            hg = (oh * rsh) * gout_ref[:, sl] * (gh * _sig(gh))
            cat_ref[:, CONV_CH + h * HEAD_D:CONV_CH + (h + 1) * HEAD_D] = hg.astype(BF16)

        yv = _dot(cat_ref[...], wout_ref[...])
        y_ref[...] = yv
        rsy = lax.rsqrt(jnp.mean(yv * yv, axis=-1, keepdims=True) + RMS_EPS)
        x1 = x_ref[...] + (yv * rsy) * (mod_ref[2:3, :] * gp_ref[...])
        x1_ref[...] = x1
        rs1 = lax.rsqrt(jnp.mean(x1 * x1, axis=-1, keepdims=True) + RMS_EPS)
        h2 = (x1 * rs1) * (gf_ref[...] * (1.0 + mod_ref[4:5, :])) + mod_ref[3:4, :]
        h2_ref[...] = h2.astype(BF16)

        @pl.when(i == min(nt - 1, nt // 2 + 1))
        def _():
            _ring_forward(gout_bufs, gssem, grsem)

        @pl.when(i == nt - 1)
        def _():
            _ring_finish(gout_bufs, gssem, grsem)

    tile = lambda cols: pl.BlockSpec((tm, cols), lambda i: (i, 0))
    hbm = pl.BlockSpec(memory_space=pl.ANY)
    outs = pl.pallas_call(
        body, name="mixers_fwd", grid=(nt,),
        in_specs=[tile(IN_COLS), _full((HALO, CONV_CH))] + [_full((1, CONV_CH))] * 4 + [_full((2, HGRN_W))]
        + [_full((D_MODEL, D_MODEL)), tile(D_MODEL), _full((6, D_MODEL)), _full((1, D_MODEL)), _full((1, D_MODEL))]
        + [hbm] * ng,
        out_specs=[tile(D_MODEL), tile(CONV_CH), tile(HGRN_W),
                   pl.BlockSpec((nch, N_HEADS, HEAD_D, HEAD_D), lambda i: (i, 0, 0, 0)),
                   tile(D_MODEL), tile(D_MODEL), tile(D_MODEL)] + [hbm] * ng,
        out_shape=[_big((T, D_MODEL), BF16), _big((T, CONV_CH), F32), _big((T, HGRN_W), F32),
                   _big((T // CHUNK, N_HEADS, HEAD_D, HEAD_D), F32), _big((T, D_MODEL), F32),
                   _big((T, D_MODEL), F32), _big((T, D_MODEL), BF16)] + [_big(b.shape, BF16) for b in gbufs],
        input_output_aliases={n_in + t: n_out + t for t in range(ng)},
        scratch_shapes=[pltpu.VMEM((tm + HALO + SUB, CONV_CH), F32), pltpu.VMEM((N_HEADS, HEAD_D, HEAD_D), F32),
                        pltpu.VMEM((tm, HGRN_W), BF16), pltpu.VMEM((tm, HGRN_W), BF16),
                        pltpu.VMEM((tm, HGRN_W), BF16), pltpu.VMEM((tm, HGRN_W), BF16),
                        pltpu.VMEM((tm, HGRN_W), F32), pltpu.VMEM((tm, tm), BF16),
                        pltpu.VMEM((CONV_CH, CONV_CH), BF16)] + _ring_sems(ng),
        compiler_params=_cp(("arbitrary",), 56),
    )(_hbm(p), wdw, *vecs, lbl, *_hbm(w_out, x), modr, g_post, g_ffn, *[_hbm(b) for b in gbufs])
    return outs[:n_out], outs[n_out:]


def _row_chains(rows, n=2):
    step = rows // n
    return [slice(k * step, (k + 1) * step) for k in range(n)]


def _ffn_blocks():
    return D_FF // FFN_BLOCK, (D_FF // N_CHIPS) // FFN_BLOCK


def _ffn_fwd(h2, w_up_g, w_down, x1, target, modr, g_post):
    T = h2.shape[0]
    tm = min(FFN_TILE, T)
    fb = FFN_BLOCK
    nj, per = _ffn_blocks()

    def body(h_ref, wu_ref, wd_ref, x1_ref, t_ref, mod_ref, g_ref, r_ref, dy2_ref, dx2_ref, st_ref, acc):
        i, j = pl.program_id(0), pl.program_id(1)

        @pl.when((i == 0) & (j == 0))
        def _():
            st_ref[...] = jnp.zeros(st_ref.shape, F32)

        @pl.when(j == 0)
        def _():
            acc[...] = jnp.zeros(acc.shape, F32)

        for rows in _row_chains(tm):
            ra = jnp.maximum(_dot(h_ref[rows, :], wu_ref[0]), 0.0)
            rb = (ra * ra).astype(BF16)
            r_ref[rows, :] = rb
            acc[rows, :] += _dot(rb, wd_ref[...])

        @pl.when(j == nj - 1)
        def _():
            y2 = acc[...]
            rs = lax.rsqrt(jnp.mean(y2 * y2, axis=-1, keepdims=True) + RMS_EPS)
            nh = y2 * rs
            gp, gt = g_ref[...], mod_ref[5:6, :]
            gate_gain = gt * gp
            err = x1_ref[...] + nh * gate_gain - t_ref[...]
            dx2 = err * (1.0 / D_MODEL)
            dx2_ref[...] = dx2
            st_ref[0:1, :] += _colsum(err * err)
            s_dn = _colsum(dx2 * nh)
            st_ref[1:2, :] += s_dn * gp
            st_ref[2:3, :] += s_dn * gt
            dy2_ref[...] = _rms_bwd(dx2 * gate_gain, nh, rs).astype(BF16)

    tile = pl.BlockSpec((tm, D_MODEL), lambda i, j: (i, 0))
    return pl.pallas_call(
        body, name="ffn_fwd", grid=(T // tm, nj),
        in_specs=[tile, pl.BlockSpec((1, D_MODEL, fb), lambda i, j: (j // per, 0, j % per)),
                  pl.BlockSpec((fb, D_MODEL), lambda i, j: (j, 0)), tile, tile,
                  _full((6, D_MODEL)), _full((1, D_MODEL))],
        out_specs=[pl.BlockSpec((tm, fb), lambda i, j: (i, j)), tile, tile, _full((8, D_MODEL))],
        out_shape=[_big((T, D_FF), BF16), _big((T, D_MODEL), BF16), _big((T, D_MODEL), F32),
                   jax.ShapeDtypeStruct((8, D_MODEL), F32)],
        scratch_shapes=[pltpu.VMEM((tm, D_MODEL), F32)],
        compiler_params=_cp(("arbitrary", "arbitrary"), 56),
    )(*_hbm(h2, w_up_g, w_down, x1, target), modr, g_post)


def _rms_bwd(dxn, xn, rs):
    return rs * (dxn - xn * jnp.mean(dxn * xn, axis=-1, keepdims=True))


def _ffn_bwd(dy2, r, x1, dx2, w_up_g, w_down, modr, g_ffn):
    T = dx2.shape[0]
    tm = min(FFN_TILE, T)
    fb = FFN_BLOCK
    nj, per = _ffn_blocks()

    def body(dy2_ref, r_ref, x1_ref, dx2_ref, wu_ref, wd_ref, mod_ref, gf_ref, da_ref, dx1_ref, st_ref, dh_s):
        i, j = pl.program_id(0), pl.program_id(1)

        @pl.when((i == 0) & (j == 0))
        def _():
            st_ref[...] = jnp.zeros(st_ref.shape, F32)

        @pl.when(j == 0)
        def _():
            dh_s[...] = jnp.zeros(dh_s.shape, F32)

        for rows in _row_chains(tm):
            ra = jnp.sqrt(r_ref[rows, :].astype(F32))
            da = (_dot_nt(dy2_ref[rows, :], wd_ref[...]) * (2.0 * ra)).astype(BF16)
            da_ref[rows, :] = da
            dh_s[rows, :] += _dot_nt(da, wu_ref[0])

        @pl.when(j == nj - 1)
        def _():
            dh = dh_s[...]
            x1v = x1_ref[...]
            rs1 = lax.rsqrt(jnp.mean(x1v * x1v, axis=-1, keepdims=True) + RMS_EPS)
            xn = x1v * rs1
            gf, scale1 = gf_ref[...], 1.0 + mod_ref[4:5, :]
            st_ref[0:1, :] += _colsum(dh)
            s_dh = _colsum(dh * xn)
            st_ref[1:2, :] += s_dh * gf
            st_ref[2:3, :] += s_dh * scale1
            dx1_ref[...] = dx2_ref[...] + _rms_bwd(dh * (scale1 * gf), xn, rs1)

    tile = pl.BlockSpec((tm, D_MODEL), lambda i, j: (i, 0))
    ftile = pl.BlockSpec((tm, fb), lambda i, j: (i, j))
    return pl.pallas_call(
        body, name="ffn_bwd", grid=(T // tm, nj),
        in_specs=[tile, ftile, tile, tile, pl.BlockSpec((1, D_MODEL, fb), lambda i, j: (j // per, 0, j % per)),
                  pl.BlockSpec((fb, D_MODEL), lambda i, j: (j, 0)), _full((6, D_MODEL)), _full((1, D_MODEL))],
        out_specs=[ftile, tile, _full((8, D_MODEL))],
        out_shape=[_big((T, D_FF), BF16), _big((T, D_MODEL), F32), jax.ShapeDtypeStruct((8, D_MODEL), F32)],
        scratch_shapes=[pltpu.VMEM((tm, D_MODEL), F32)],
        compiler_params=_cp(("arbitrary", "arbitrary"), 56),
    )(*_hbm(dy2, r, x1, dx2, w_up_g, w_down), modr, g_ffn)


def _mix_out_bwd(dx1, y, cat, w_out, modr, g_post, swap):
    T = dx1.shape[0]
    tm = _tok_tile(T)
    nt = T // tm
    ns = len(swap)

    def body(*refs):
        dx1_ref, y_ref, cat_ref, w_ref, mod_ref, gp_ref = refs[:6]
        s_ins = refs[6:6 + ns]
        dcat_ref, st_ref, gw_ref = refs[6 + ns:9 + ns]
        s_outs = refs[9 + ns:9 + 2 * ns]
        gacc, gsem, pssem, prsem = refs[9 + 2 * ns:]
        i = pl.program_id(0)

        @pl.when(i == 0)
        def _():
            for cp in _pair_copies(s_ins, s_outs, pssem, prsem):
                cp.start()
            st_ref[...] = jnp.zeros(st_ref.shape, F32)
            gacc[...] = jnp.zeros(gacc.shape, F32)

        dxv, yv = dx1_ref[...], y_ref[...]
        rs = lax.rsqrt(jnp.mean(yv * yv, axis=-1, keepdims=True) + RMS_EPS)
        nh = yv * rs
        st_ref[0:1, :] += _colsum(dxv * (nh * gp_ref[...]))
        dn = dxv * mod_ref[2:3, :]
        st_ref[1:2, :] += _colsum(dn * nh)
        dy = _rms_bwd(dn * gp_ref[...], nh, rs).astype(BF16)
        dcat_ref[...] = _dot_nt(dy, w_ref[...])
        for cols in _row_chains(D_MODEL):
            gacc[:, cols] += _dot_tn(cat_ref[...], dy[:, cols])

        @pl.when(i == nt - 1)
        def _():
            out = pltpu.make_async_copy(gacc, gw_ref, gsem)
            out.start()
            copies = _pair_copies(s_ins, s_outs, pssem, prsem)
            for cp in copies:
                cp.wait_recv()
            for cp in copies:
                cp.wait_send()
            out.wait()

    tile = pl.BlockSpec((tm, D_MODEL), lambda i: (i, 0))
    hbm = pl.BlockSpec(memory_space=pl.ANY)
    outs = pl.pallas_call(
        body, name="mix_out_bwd", grid=(nt,),
        in_specs=[tile, tile, tile, _full((D_MODEL, D_MODEL)), _full((6, D_MODEL)), _full((1, D_MODEL))]
        + [hbm] * ns,
        out_specs=[tile, _full((8, D_MODEL)), hbm] + [hbm] * ns,
        out_shape=[_big((T, D_MODEL), F32), jax.ShapeDtypeStruct((8, D_MODEL), F32), _big((D_MODEL, D_MODEL), F32)]
        + _pair_shapes(swap),
        scratch_shapes=[pltpu.VMEM((D_MODEL, D_MODEL), F32), pltpu.SemaphoreType.DMA] + _pair_sems(ns),
        compiler_params=_cp(("arbitrary",), 48),
    )(*_hbm(dx1, y, cat, w_out), modr, g_post, *[_hbm(g) for g in swap])
    return outs[:3], outs[3:]


def _mixers_bwd(p, dcat, ys, o, states, h1, wdw, vecs, lbl, pairs_b):
    T = p.shape[0]
    tm = min(MIXB_TILE, T)
    nt = T // tm
    nch = tm // CHUNK
    hpt = tm // HALO
    nx = len(pairs_b)
    nb = IN_COLS // N_CHIPS

    def body(*refs):
        (p_ref, ph_ref, dcat_ref, ys_ref, o_ref, st_ref, h1_ref, wdw_ref, bdw_ref, gain_ref, bias_ref, gout_ref,
         lbl_ref) = refs[:13]
        x_ins = refs[13:13 + nx]
        dp_ref, sb_ref, s5_ref, dw_ref = refs[13 + nx:17 + nx]
        x_outs = refs[17 + nx:17 + 2 * nx]
        gin_ref, ginb_ref = refs[17 + 2 * nx], refs[18 + 2 * nx]
        (ubuf, dybuf, carry, dstate, qt_s, kt_s, kh_s, v_s, do_s, egl_s, dqt_s, dkt_s, dkh_s, dv_s, dgl_s,
         dsh, dw8, dshift, lower_s, upper_s, same_s, gmat_s, gacc, gacc_b, dp_prev, h1_prev, gsem, gbsem, xssem,
         xrsem) = refs[19 + 2 * nx:]
        i = pl.program_id(0)
        tile_idx = nt - 1 - i

        @pl.when(i == 0)
        def _():
            for cp in _xchg_copies(x_ins, x_outs, xssem, xrsem):
                cp.start()
            gacc[...] = jnp.zeros(gacc.shape, F32)
            dstate[...] = jnp.zeros(dstate.shape, F32)
            carry[...] = jnp.zeros(carry.shape, F32)
            sb_ref[...] = jnp.zeros(sb_ref.shape, F32)
            s5_ref[...] = jnp.zeros(s5_ref.shape, F32)
            dw_ref[...] = jnp.zeros(dw_ref.shape, F32)
            dw8[...] = jnp.zeros(dw8.shape, F32)
            lower_s[...], upper_s[...], same_s[...] = _chunk_masks(tm)
            gmat_s[...] = _gn_matrix()
            dsh[0:SUB, :] = jnp.zeros((SUB, CONV_CH), F32)
            dsh[SUB + tm:2 * SUB + tm, :] = jnp.zeros((SUB, CONV_CH), F32)
            ubuf[HALO + tm:HALO + tm + SUB, :] = jnp.zeros((SUB, CONV_CH), F32)
            dp_prev[...] = jnp.zeros(dp_prev.shape, BF16)
            h1_prev[...] = jnp.zeros(h1_prev.shape, BF16)

        n_pieces = (tm // CONV_ROWS) * (CONV_CH // LANE)
        per_block = n_pieces // N_CHIPS
        prow = D_MODEL // per_block

        def w_in_grad_piece(k):
            j, part = k // per_block, k % per_block
            rows_k = slice(part * prow, (part + 1) * prow)
            gacc[j, rows_k, :] += _dot_tn(h1_prev[:, rows_k], dp_prev[:, j * nb:(j + 1) * nb])

        uh = ph_ref[:, 0:CONV_CH] * _sig(ph_ref[:, CONV_CH:2 * CONV_CH])
        ubuf[0:HALO, :] = jnp.where(tile_idx > 0, uh, 0.0)
        ubuf[HALO:HALO + tm, :] = p_ref[:, 0:CONV_CH] * _sig(p_ref[:, CONV_CH:2 * CONV_CH])
        gmat = gmat_s[...]
        gain = gain_ref[...]
        yv = ys_ref[...]
        d = yv - _gmean(yv, gmat)
        rs = lax.rsqrt(_gmean(d * d, gmat) + GN_EPS)
        yn = d * rs
        z = yn * gain + bias_ref[...]
        sz = _sig(z)
        dz = dcat_ref[:, 0:CONV_CH] * (sz * (1.0 + z * (1.0 - sz)))
        dyn = dz * gain
        dyc = rs * (dyn - _gmean(dyn, gmat) - yn * _gmean(dyn * yn, gmat))
        s5_ref[0:1, :] += _colsum(dyc)
        s5_ref[1:2, :] += _colsum(dz * yn)
        s5_ref[2:3, :] += _colsum(dz)
        dybuf[tm:tm + HALO, :] = carry[...]
        dybuf[0:tm, :] = dyc
        dsh[SUB:SUB + tm, :] = dyc
        carry[...] = dyc[0:HALO, :]
        for b in range(SUB):
            dshift[...] = dsh[SUB - b:2 * SUB - b + tm, :]
            for j, off in CONV_FWD_TAPS:
                if off % SUB == b:
                    prod = dshift[...] * ubuf[off - b:off - b + tm + SUB, :]
                    dw8[j] += jnp.sum(prod.reshape((tm + SUB) // SUB, SUB, CONV_CH), axis=0)
        for r in range(tm // CONV_ROWS):
            rows = slice(r * CONV_ROWS, (r + 1) * CONV_ROWS)
            for lb_ in range(CONV_CH // LANE):
                lanes = slice(lb_ * LANE, (lb_ + 1) * LANE)
                glanes = slice(CONV_CH + lb_ * LANE, CONV_CH + (lb_ + 1) * LANE)
                w_in_grad_piece(r * (CONV_CH // LANE) + lb_)
                acc = _tap_conv(dybuf, wdw_ref, r * CONV_ROWS, CONV_BWD_TAPS, lanes)
                val = p_ref[rows, lanes]
                sg = _sig(p_ref[rows, glanes])
                dval = acc * sg
                dgate = acc * val * (sg * (1.0 - sg))
                dp_ref[rows, lanes] = dval.astype(BF16)
                dp_ref[rows, glanes] = dgate.astype(BF16)
                sb_ref[0:1, lanes] += _colsum(dval)
                sb_ref[0:1, glanes] += _colsum(dgate)

        o0 = 2 * CONV_CH
        for h in range(N_HEADS):
            sl = slice(h * HEAD_D, (h + 1) * HEAD_D)
            gsl = slice(o0 + 3 * HGRN_W + h * HEAD_D, o0 + 3 * HGRN_W + (h + 1) * HEAD_D)
            oh = o_ref[:, sl]
            gh = p_ref[:, gsl]
            dh = dcat_ref[:, CONV_CH + h * HEAD_D:CONV_CH + (h + 1) * HEAD_D]
            gout = gout_ref[:, sl]
            rsh = lax.rsqrt(jnp.mean(oh * oh, axis=-1, keepdims=True) + RMS_EPS)
            on = oh * rsh
            sgg = _sig(gh)
            dgh = dh * (on * gout) * (sgg * (1.0 + gh * (1.0 - sgg)))
            dm = dh * (gh * sgg)
            s5_ref[3:4, sl] += _colsum(dm * on)
            do_s[:, sl] = _rms_bwd(dm * gout, on, rsh).astype(BF16)
            dp_ref[:, gsl] = dgh.astype(BF16)
            sb_ref[2:3, CONV_CH + h * HEAD_D:CONV_CH + (h + 1) * HEAD_D] += _colsum(dgh)

        lb, _ = _lower_bound(lbl_ref)
        pq = p_ref[:, o0:o0 + HGRN_W]
        pr = _hgrn_prep(pq, p_ref[:, o0 + HGRN_W:o0 + 2 * HGRN_W], lb, lower_s[...])
        qt_s[...] = pr["qt"].astype(BF16)
        kt_s[...] = pr["kt"].astype(BF16)
        kh_s[...] = pr["kh"].astype(BF16)
        v_s[...] = p_ref[:, o0 + 2 * HGRN_W:o0 + 3 * HGRN_W].astype(BF16)
        egl_s[...] = jnp.exp(pr["Gl"])
        tri = _tri()

        def chunk(it, c_):
            ci = nch - 1 - it
            r0 = pl.multiple_of(ci * CHUNK, CHUNK)
            rows = pl.ds(r0, CHUNK)
            for h in range(N_HEADS):
                ls = pl.ds(h * HEAD_D, HEAD_D)
                qc, kc, hc, vc = qt_s[rows, ls], kt_s[rows, ls], kh_s[rows, ls], v_s[rows, ls]
                dob = do_s[rows, ls]
                s0 = st_ref[ci, h]
                s0b = s0.astype(BF16)
                ds1 = dstate[h]
                ds1b = ds1.astype(BF16)
                egl = egl_s[pl.ds(r0, 1), ls]
                att = jnp.where(tri, _dot_nt(qc, kc), 0.0).astype(BF16)
                datt = jnp.where(tri, _dot_nt(dob, vc), 0.0).astype(BF16)
                dv_s[rows, ls] = _dot_tn(att, dob) + _dot_nt(hc, ds1b)
                dqt_s[rows, ls] = _dot(datt, kc) + _dot(dob, s0b)
                dkt_s[rows, ls] = _dot_tn(datt, qc)
                dkh_s[rows, ls] = _dot(vc, ds1b)
                dgl = egl * _colsum(ds1 * s0)
                dgl_s[rows, ls] = jnp.broadcast_to(dgl, (CHUNK, HEAD_D))
                dstate[h] = ds1 * egl + _dot_tn(dob, qc)
            return c_

        lax.fori_loop(0, nch, chunk, 0, unroll=min(CHUNK_UNROLL, nch))
        dqt, dkt, dkh = dqt_s[...], dkt_s[...], dkh_s[...]
        dk = dkt * pr["enG"] + dkh * pr["eGlG"]
        khk = dkh * kh_s[...].astype(F32)
        dG = dqt * qt_s[...].astype(F32) - dkt * kt_s[...].astype(F32) - khk
        dlogf = _mm3(upper_s[...], dG) + _mm3(same_s[...], khk) + dgl_s[...]
        df = dlogf / pr["f"] - dk
        sf, sq = pr["sf"], pr["sq"]
        s5_ref[4:5, :] += _colsum(df * (1.0 - sf))
        dfl = df * (1.0 - lb) * (sf * (1.0 - sf))
        dq = (dqt * pr["eG"]) * (sq * (1.0 + pq * (1.0 - sq)))
        dvv = dv_s[...]
        dp_ref[:, o0:o0 + HGRN_W] = dq.astype(BF16)
        dp_ref[:, o0 + HGRN_W:o0 + 2 * HGRN_W] = dfl.astype(BF16)
        dp_ref[:, o0 + 2 * HGRN_W:o0 + 3 * HGRN_W] = dvv.astype(BF16)
        sb_ref[1:2, 0:HGRN_W] += _colsum(dq)
        sb_ref[1:2, HGRN_W:2 * HGRN_W] += _colsum(dfl)
        sb_ref[2:3, 0:HGRN_W] += _colsum(dvv)

        dp_prev[...] = dp_ref[...]
        h1_prev[...] = h1_ref[...]

        @pl.when(i == nt - 1)
        def _():
            for k in range(n_pieces):
                w_in_grad_piece(k)
            out = pltpu.make_async_copy(gacc, gin_ref, gsem)
            out.start()
            for j in range(N_CHIPS):
                gacc_b[j] = gacc[j].astype(BF16)
            out_b = pltpu.make_async_copy(gacc_b, ginb_ref, gbsem)
            out_b.start()
            for j in range(CONV_K):
                dw_ref[j:j + 1, :] = _colsum(dw8[j])
            copies = _xchg_copies(x_ins, x_outs, xssem, xrsem)
            for cp in copies:
                cp.wait_recv()
            for cp in copies:
                cp.wait_send()
            out.wait()
            out_b.wait()

    rev = lambda cols: pl.BlockSpec((tm, cols), lambda i: (nt - 1 - i, 0))
    halo = pl.BlockSpec((HALO, 2 * CONV_CH), lambda i: (jnp.maximum((nt - 1 - i) * hpt - 1, 0), 0))
    wide = lambda n: pltpu.VMEM((tm, HGRN_W), n)
    hbm = pl.BlockSpec(memory_space=pl.ANY)
    outs = pl.pallas_call(
        body, name="mixers_bwd", grid=(nt,),
        in_specs=[rev(IN_COLS), halo, rev(D_MODEL), rev(CONV_CH), rev(HGRN_W),
                  pl.BlockSpec((nch, N_HEADS, HEAD_D, HEAD_D), lambda i: (nt - 1 - i, 0, 0, 0)), rev(D_MODEL),
                  _full((HALO, CONV_CH))] + [_full((1, CONV_CH))] * 4 + [_full((2, HGRN_W))] + [hbm] * nx,
        out_specs=[rev(IN_COLS), _full((8, D_MODEL)), _full((8, CONV_CH)), _full((HALO, CONV_CH))]
        + [hbm] * (nx + 2),
        out_shape=[_big((T, IN_COLS), BF16), jax.ShapeDtypeStruct((8, D_MODEL), F32),
                   jax.ShapeDtypeStruct((8, CONV_CH), F32), jax.ShapeDtypeStruct((HALO, CONV_CH), F32)]
        + [_big(pb.shape, BF16) for pb in pairs_b]
        + [_big((N_CHIPS, D_MODEL, nb), F32), _big((N_CHIPS, D_MODEL, nb), BF16)],
        scratch_shapes=[pltpu.VMEM((tm + HALO + SUB, CONV_CH), F32), pltpu.VMEM((tm + HALO, CONV_CH), F32),
                        pltpu.VMEM((HALO, CONV_CH), F32), pltpu.VMEM((N_HEADS, HEAD_D, HEAD_D), F32),
                        wide(BF16), wide(BF16), wide(BF16), wide(BF16), wide(BF16),
                        wide(F32), wide(F32), wide(F32), wide(F32), wide(F32), wide(F32),
                        pltpu.VMEM((tm + 2 * SUB, CONV_CH), F32), pltpu.VMEM((HALO, SUB, CONV_CH), F32),
                        pltpu.VMEM((tm + SUB, CONV_CH), F32), pltpu.VMEM((tm, tm), BF16), pltpu.VMEM((tm, tm), BF16),
                        pltpu.VMEM((tm, tm), BF16), pltpu.VMEM((CONV_CH, CONV_CH), BF16),
                        pltpu.VMEM((N_CHIPS, D_MODEL, nb), F32), pltpu.VMEM((N_CHIPS, D_MODEL, nb), BF16),
                        pltpu.VMEM((tm, IN_COLS), BF16), pltpu.VMEM((tm, D_MODEL), BF16),
                        pltpu.SemaphoreType.DMA, pltpu.SemaphoreType.DMA]
        + _xchg_sems(nx),
        compiler_params=_cp(("arbitrary",), 56),
    )(*_hbm(p, p, dcat, ys, o, states, h1), wdw, *vecs, lbl, *[_hbm(pb) for pb in pairs_b])
    return outs[:4], outs[4:4 + nx], (outs[4 + nx], outs[5 + nx])


def _mix_in_bwd(dp, w_in_g, x, dx1, modr, g_pre, pairs_b):
    T = x.shape[0]
    tm = _tok_tile(T)
    nt = T // tm
    nb = IN_COLS // N_CHIPS
    nx = len(pairs_b)

    def body(*refs):
        dp_ref, w_ref, x_ref, dx1_ref, mod_ref, g_ref = refs[:6]
        x_ins = refs[6:6 + nx]
        gx_ref, st_ref = refs[6 + nx:8 + nx]
        x_outs = refs[8 + nx:8 + 2 * nx]
        xssem, xrsem = refs[8 + 2 * nx:]
        i = pl.program_id(0)

        @pl.when(i == 0)
        def _():
            for cp in _xchg_copies(x_ins, x_outs, xssem, xrsem):
                cp.start()
            st_ref[...] = jnp.zeros(st_ref.shape, F32)

        dh = None
        for j in range(N_CHIPS):
            part = _dot_nt(dp_ref[:, j * nb:(j + 1) * nb], w_ref[j])
            dh = part if dh is None else dh + part
        xv = x_ref[...]
        rs = lax.rsqrt(jnp.mean(xv * xv, axis=-1, keepdims=True) + RMS_EPS)
        xn = xv * rs
        st_ref[0:1, :] += _colsum(dh)
        st_ref[1:2, :] += _colsum(dh * (xn * g_ref[...]))
        dsc = dh * (1.0 + mod_ref[1:2, :])
        st_ref[2:3, :] += _colsum(dsc * xn)
        gx_ref[...] = dx1_ref[...] + _rms_bwd(dsc * g_ref[...], xn, rs)

        @pl.when(i == nt - 1)
        def _():
            copies = _xchg_copies(x_ins, x_outs, xssem, xrsem)
            for cp in copies:
                cp.wait_recv()
            for cp in copies:
                cp.wait_send()

    tile = pl.BlockSpec((tm, D_MODEL), lambda i: (i, 0))
    hbm = pl.BlockSpec(memory_space=pl.ANY)
    outs = pl.pallas_call(
        body, name="mix_in_bwd", grid=(nt,),
        in_specs=[pl.BlockSpec((tm, IN_COLS), lambda i: (i, 0)), _full((N_CHIPS, D_MODEL, nb)), tile, tile,
                  _full((6, D_MODEL)), _full((1, D_MODEL))] + [hbm] * nx,
        out_specs=[tile, _full((8, D_MODEL))] + [hbm] * nx,
        out_shape=[_big((T, D_MODEL), F32), jax.ShapeDtypeStruct((8, D_MODEL), F32)]
        + [_big(pb.shape, BF16) for pb in pairs_b],
        scratch_shapes=_xchg_sems(nx),
        compiler_params=_cp(("arbitrary",), 48),
    )(*_hbm(dp, w_in_g, x, dx1), modr, g_pre, *[_hbm(pb) for pb in pairs_b])
    return outs[:2], outs[2:]


def _weight_grad(a, b, a_blocked, b_blocked, name):
    T = a.shape[0]
    tt = min(GRAD_TILE, T)
    nt = T // tt
    ka = a.shape[1] // N_CHIPS if a_blocked else a.shape[1]
    nb = b.shape[1] // N_CHIPS if b_blocked else b.shape[1]

    def body(a_ref, b_ref, o_ref, ob_ref):
        t = pl.program_id(1)

        @pl.when(t == 0)
        def _():
            o_ref[...] = jnp.zeros(o_ref.shape, F32)

        for cols in _row_chains(nb):
            o_ref[0, :, cols] += _dot_tn(a_ref[...], b_ref[:, cols])

        @pl.when(t == nt - 1)
        def _():
            ob_ref[0] = o_ref[0].astype(BF16)

    blk = pl.BlockSpec((1, ka, nb), lambda j, t: (j, 0, 0))
    return pl.pallas_call(
        body, name=name, grid=(N_CHIPS, nt),
        in_specs=[pl.BlockSpec((tt, ka), (lambda j, t: (t, j)) if a_blocked else (lambda j, t: (t, 0))),
                  pl.BlockSpec((tt, nb), (lambda j, t: (t, j)) if b_blocked else (lambda j, t: (t, 0)))],
        out_specs=[blk, blk],
        out_shape=[_big((N_CHIPS, ka, nb), F32), _big((N_CHIPS, ka, nb), BF16)],
        compiler_params=_cp(("arbitrary", "arbitrary"), 48),
    )(*_hbm(a, b))


R_LOSS = 0
R_FFN = 8
R_OUT = 16
R_IN = 24
R_BIN = 32
R_512 = 40
R_DW = 48
N_STAT_ROWS = 80
MOD_ROWS = (R_IN + 0, R_IN + 1, R_OUT + 0, R_FFN + 0, R_FFN + 1, R_LOSS + 1)


def _small_update(gath, params):
    names = ["b_ada", "lb_logits", "g_pre_mix", "b_in", "b_dw", "gn_gain", "gn_bias", "g_hgrn_out", "g_post_mix",
             "g_pre_ffn", "g_post_ffn"]
    flat = []
    for n in names:
        flat += list(params[n])
    n_in = 1 + len(flat)

    def body(*refs):
        g_ref = refs[0]
        prm = {n: refs[1 + 3 * k:4 + 3 * k] for k, n in enumerate(names)}
        outs = refs[n_in:]
        loss_ref, dmod_ref, dwdw_ref = outs[0], outs[1], outs[2]
        res = {n: outs[3 + 4 * k:7 + 4 * k] for k, n in enumerate(names)}
        red = g_ref[0]
        for dev in range(1, N_DEV):
            red = red + g_ref[dev]
        loss_ref[...] = jnp.broadcast_to(
            (0.5 / D_MODEL) * jnp.sum(red[R_LOSS:R_LOSS + 1, :], axis=-1, keepdims=True), loss_ref.shape)
        for dev in range(N_DEV):
            for k, r in enumerate(MOD_ROWS):
                dmod_ref[dev:dev + 1, k * D_MODEL:(k + 1) * D_MODEL] = g_ref[dev, r:r + 1, :]
        dwdw_ref[...] = red[R_DW:R_DW + HALO, 0:CONV_CH]

        def finish(name, pieces):
            w_ref, m_ref, v_ref = prm[name]
            g_out, d_out, m_out, v_out = res[name]
            for rsl, lsl, g in pieces:
                d, m2, v2 = _adam_math(w_ref[rsl, lsl], g, m_ref[rsl, lsl], v_ref[rsl, lsl])
                g_out[rsl, lsl] = g
                d_out[rsl, lsl] = d
                m_out[rsl, lsl] = m2
                v_out[rsl, lsl] = v2

        one = slice(0, 1)
        row = lambda r: red[r:r + 1, :]
        half = lambda r: red[r:r + 1, 0:CONV_CH]
        finish("b_ada", [(one, slice(k * D_MODEL, (k + 1) * D_MODEL), row(r)) for k, r in enumerate(MOD_ROWS)])
        finish("b_in", [(one, slice(k * D_MODEL, (k + 1) * D_MODEL), row(R_BIN + k)) for k in range(3)])
        finish("g_pre_mix", [(one, slice(None), row(R_IN + 2))])
        finish("g_post_mix", [(one, slice(None), row(R_OUT + 1))])
        finish("g_pre_ffn", [(one, slice(None), row(R_FFN + 2))])
        finish("g_post_ffn", [(one, slice(None), row(R_LOSS + 2))])
        finish("b_dw", [(one, slice(None), half(R_512 + 0))])
        finish("gn_gain", [(one, slice(None), half(R_512 + 1))])
        finish("gn_bias", [(one, slice(None), half(R_512 + 2))])
        finish("g_hgrn_out", [(one, slice(None), half(R_512 + 3))])
        s0, s1 = _lower_bound(prm["lb_logits"][0])
        dlb = half(R_512 + 4)
        finish("lb_logits", [(slice(0, 1), slice(None), dlb * s0 * (1.0 - s0)),
                             (slice(1, 2), slice(None), -dlb * s0 * s1)])

    vm = pl.BlockSpec(memory_space=pltpu.VMEM)
    out_shape = [jax.ShapeDtypeStruct((8, 128), F32), jax.ShapeDtypeStruct((N_DEV, 6 * D_MODEL), F32),
                 jax.ShapeDtypeStruct((HALO, CONV_CH), F32)]
    for n in names:
        out_shape += [jax.ShapeDtypeStruct(params[n][0].shape, F32)] * 4
    outs = pl.pallas_call(
        body, name="small_update", out_shape=out_shape,
        in_specs=[vm] * n_in, out_specs=[vm] * len(out_shape),
        compiler_params=_cp(None, 32),
    )(gath, *flat)
    return outs[0], outs[1], outs[2], {n: outs[3 + 4 * k:7 + 4 * k] for k, n in enumerate(names)}


def _wdw_adam(w, g, m, v):
    def body(w_ref, g_ref, m_ref, v_ref, d_out, m_out, v_out):
        d, m2, v2 = _adam_math(w_ref[...], g_ref[...], m_ref[...], v_ref[...])
        d_out[...] = d
        m_out[...] = m2
        v_out[...] = v2

    vm = pl.BlockSpec(memory_space=pltpu.VMEM)
    return pl.pallas_call(
        body, name="wdw_adam", out_shape=[jax.ShapeDtypeStruct(w.shape, F32)] * 3,
        in_specs=[vm] * 4, out_specs=[vm] * 3, compiler_params=_cp(None, 16),
    )(w, g, m, v)


def kernel(x, c, w_ada, b_ada, lb_logits, g_pre_mix, w_in, b_in, w_dw, b_dw, gn_gain, gn_bias, g_hgrn_out, w_out, g_post_mix, g_pre_ffn, w_up, w_down, g_post_ffn, loss_target, m_w_ada, m_b_ada, m_lb_logits, m_g_pre_mix, m_w_in, m_b_in, m_w_dw, m_b_dw, m_gn_gain, m_gn_bias, m_g_hgrn_out, m_w_out, m_g_post_mix, m_g_pre_ffn, m_w_up, m_w_down, m_g_post_ffn, v_w_ada, v_b_ada, v_lb_logits, v_g_pre_mix, v_w_in, v_b_in, v_w_dw, v_b_dw, v_gn_gain, v_gn_bias, v_g_hgrn_out, v_w_out, v_g_post_mix, v_g_pre_ffn, v_w_up, v_w_down, v_g_post_ffn):
    ax, ay, ac = lax.axis_index("x"), lax.axis_index("y"), lax.axis_index("c")
    chip = 2 * ax + ay
    T = x.shape[1]
    xs, tgt = x[0], loss_target[0]
    ada_cols = w_ada.shape[2]

    b_sh = lax.dynamic_slice_in_dim(b_ada, chip * ada_cols, ada_cols, axis=1)
    wdw_pad = jnp.pad(w_dw[0], ((0, HALO - CONV_K), (0, 0)))
    chip1 = jnp.reshape(chip, (1,)).astype(jnp.int32)
    place = jnp.stack([ac, chip]).astype(jnp.int32)
    _, c8, modg, wdwg = _ada_exchange(c, w_ada[0], b_sh, wdw_pad)
    modr = modg.reshape(6, D_MODEL)
    wdw_all = jnp.transpose(wdwg, (1, 0, 2)).reshape(HALO, CONV_CH)
    vec = (b_dw, gn_gain, gn_bias, g_hgrn_out)

    p, h1, (up_buf, down_buf), w_in_g, w_out_g = _mix_in_fwd(
        chip1, xs, modr, g_pre_mix, b_in.reshape(N_CHIPS, 1, IN_COLS // N_CHIPS), [w_up[0], w_down[0]],
        _cast_own(chip1, w_in[0], "cast_w_in"), _cast_own(chip1, w_out[0], "cast_w_out"))
    w_out_f = w_out_g.reshape(D_MODEL, D_MODEL)
    (cat, ys, o, states, y, x1, h2), (w_up_g, w_down_g) = _mixers_fwd(
        p, wdw_all, vec, lb_logits, w_out_f, xs, modr, g_post_mix, g_pre_ffn, [up_buf, down_buf])
    w_down_f = w_down_g.reshape(D_FF, D_MODEL)
    r, dy2, dx2, st_loss = _ffn_fwd(h2, w_up_g, w_down_f, x1, tgt, modr, g_post_ffn)

    def pair_sums(grads, got, tags):
        return [_pair_sum(place, g, o_, "pair_sum_" + t) for (g, _), o_, t in zip(grads, got, tags)]

    da, dx1, st_ffn = _ffn_bwd(dy2, r, x1, dx2, w_up_g, w_down_f, modr, g_pre_ffn)
    g_up = _weight_grad(h2, da, False, True, "grad_w_up")
    g_down = _weight_grad(r, dy2, True, False, "grad_w_down")
    (dcat, st_out, g_out), got_ud = _mix_out_bwd(dx1, y, cat, w_out_f, modr, g_post_mix, [g_up[1], g_down[1]])
    g_out = g_out.reshape(N_CHIPS, D_MODEL // N_CHIPS, D_MODEL)
    got_o = _pair_swap([g_out], "pair_swap_w_out")
    early = pair_sums([(g_out, None), g_up, g_down], list(got_o) + list(got_ud), ["w_out", "w_up", "w_down"])
    (dp, st_bin, st_512, dwdw), got_early, g_in = _mixers_bwd(p, dcat, ys, o, states, h1, wdw_all, vec, lb_logits,
                                                              [pb for _, pb in early])
    late = pair_sums([g_in], _pair_swap([g_in[1]], "pair_swap_w_in"), ["w_in"])
    (grad_x, st_in), got_late = _mix_in_bwd(dp, w_in_g, xs, dx1, modr, g_pre_mix, [late[0][1]])
    fulls = [_chip_sum(place, pf, gb, "chip_sum_" + t)
             for (pf, _), gb, t in zip(late + early, list(got_late) + list(got_early), ["w_in", "w_out", "w_up", "w_down"])]

    pad_lanes = lambda s: jnp.pad(s, ((0, 0), (0, D_MODEL - s.shape[1])))
    stats = jnp.concatenate([st_loss, st_ffn, st_out, st_in, st_bin, pad_lanes(st_512), pad_lanes(dwdw)], axis=0)
    (g_w_in, g_w_out, g_w_up, g_w_down), gath = _final_exchange(fulls, stats)
    small = {"b_ada": (b_ada, m_b_ada, v_b_ada), "lb_logits": (lb_logits, m_lb_logits, v_lb_logits),
             "g_pre_mix": (g_pre_mix, m_g_pre_mix, v_g_pre_mix), "b_in": (b_in, m_b_in, v_b_in),
             "b_dw": (b_dw, m_b_dw, v_b_dw), "gn_gain": (gn_gain, m_gn_gain, v_gn_gain),
             "gn_bias": (gn_bias, m_gn_bias, v_gn_bias), "g_hgrn_out": (g_hgrn_out, m_g_hgrn_out, v_g_hgrn_out),
             "g_post_mix": (g_post_mix, m_g_post_mix, v_g_post_mix), "g_pre_ffn": (g_pre_ffn, m_g_pre_ffn, v_g_pre_ffn),
             "g_post_ffn": (g_post_ffn, m_g_post_ffn, v_g_post_ffn)}
    loss_t, dmod_all, dwdw_sum, sres = _small_update(gath, small)
    loss = loss_t[0, 0]

    res = dict(sres)
    dmod_sh = lax.dynamic_slice_in_dim(dmod_all, chip * ada_cols, ada_cols, axis=1)
    res["w_ada"] = [t[None] for t in _ada_grad_adam(jnp.transpose(c8), dmod_sh, w_ada[0], m_w_ada[0], v_w_ada[0])]
    g_wdw = lax.dynamic_slice_in_dim(dwdw_sum, chip * HEAD_D, HEAD_D, axis=1)[:CONV_K][None]
    res["w_dw"] = [g_wdw] + list(_wdw_adam(w_dw, g_wdw, m_w_dw, v_w_dw))
    for name, g, w, m, v in (("w_in", g_w_in, w_in, m_w_in, v_w_in), ("w_out", g_w_out, w_out, m_w_out, v_w_out),
                             ("w_up", g_w_up, w_up, m_w_up, v_w_up), ("w_down", g_w_down, w_down, m_w_down, v_w_down)):
        d, m2, v2 = _adam_big(w[0], g, m[0], v[0], "adam_" + name)
        res[name] = [g[None], d[None], m2[None], v2[None]]

    order = ["w_ada", "b_ada", "lb_logits", "g_pre_mix", "w_in", "b_in", "w_dw", "b_dw", "gn_gain", "gn_bias",
             "g_hgrn_out", "w_out", "g_post_mix", "g_pre_ffn", "w_up", "w_down", "g_post_ffn"]
    out = [loss, grad_x[None]]
    for k in range(4):
        out += [res[n][k] for n in order]
    return tuple(out)
```

```python
import jax
import jax.numpy as jnp
from jax import lax
from jax.experimental import pallas as pl
from jax.experimental.pallas import tpu as pltpu

F32, BF16 = jnp.float32, jnp.bfloat16
D_MODEL = 1024
CONV_CH = 512
HGRN_W = 512
N_HEADS = 4
HEAD_D = 128
CONV_K = 31
GN_GROUP = 64
GN_SHIFT = 6
IN_COLS = 3072
D_FF = 4096
CHUNK = 64
CHUNK_SHIFT = 6
N_CHIPS = 4
N_DEV = 8
RMS_EPS = 1e-6
GN_EPS = 1e-5
ADAM_LR, ADAM_B1, ADAM_B2, ADAM_EPS, ADAM_WD, ADAM_STEP = 0.001, 0.9, 0.999, 1e-08, 0.01, 10
TOK_TILE = 512
MIXIN_TILE = 1024
MIXB_TILE = 256
FFN_TILE = 1024
FFN_BLOCK = 512
GRAD_TILE = 4096
HALO = 32
SUB = 8
LANE = 128
CONV_ROWS = 128
CHUNK_UNROLL = 8
MIB = 1 << 20
MESH = pl.DeviceIdType.MESH
OTHER_CHIPS = ((0, 1), (1, 0), (1, 1))


def _cp(sem=None, vmem_mib=48):
    return pltpu.CompilerParams(dimension_semantics=sem, vmem_limit_bytes=vmem_mib * MIB)


def _dot(a, b):
    return jnp.dot(a, b, preferred_element_type=F32)


def _dot_nt(a, b):
    return lax.dot_general(a, b, (((1,), (1,)), ((), ())), preferred_element_type=F32)


def _dot_tn(a, b):
    return lax.dot_general(a, b, (((0,), (0,)), ((), ())), preferred_element_type=F32)


def _sig(v):
    return 0.5 * jnp.tanh(0.5 * v) + 0.5


def _colsum(v):
    return jnp.sum(v, axis=0, keepdims=True)


def _flip(v, b):
    return 1 - v if b else v


def _rcopy(src, dst, ssem, rsem, dev):
    return pltpu.make_async_remote_copy(src_ref=src, dst_ref=dst, send_sem=ssem, recv_sem=rsem,
                                        device_id=dev, device_id_type=MESH)


def _place():
    return lax.axis_index("x"), lax.axis_index("y"), lax.axis_index("c")


def _full(shape):
    return pl.BlockSpec(shape, lambda *_: (0,) * len(shape))


def _big(shape, dtype):
    return pltpu.HBM(shape, dtype)


def _hbm(*arrays):
    out = [pltpu.with_memory_space_constraint(a, pltpu.HBM) for a in arrays]
    return out[0] if len(out) == 1 else out


def _split2(v):
    hi = v.astype(BF16)
    lo = (v - hi.astype(F32)).astype(BF16)
    return hi, lo


def _split3(v):
    h1 = v.astype(BF16)
    r1 = v - h1.astype(F32)
    h2 = r1.astype(BF16)
    h3 = (r1 - h2.astype(F32)).astype(BF16)
    return h1, h2, h3


def _mm3(mat, v):
    h1, h2, h3 = _split3(v)
    return _dot(mat, h1) + _dot(mat, h2) + _dot(mat, h3)


def _gn_matrix():
    r = lax.broadcasted_iota(jnp.int32, (CONV_CH, CONV_CH), 0) >> GN_SHIFT
    c = lax.broadcasted_iota(jnp.int32, (CONV_CH, CONV_CH), 1) >> GN_SHIFT
    return jnp.where(r == c, 1.0 / GN_GROUP, 0.0).astype(BF16)


def _gmean(v, gmat):
    hi, lo = _split2(v)
    return _dot(hi, gmat) + _dot(lo, gmat)


def _chunk_masks(tm):
    r = lax.broadcasted_iota(jnp.int32, (tm, tm), 0)
    c = lax.broadcasted_iota(jnp.int32, (tm, tm), 1)
    same = (r >> CHUNK_SHIFT) == (c >> CHUNK_SHIFT)
    one = lambda m: jnp.where(m, 1.0, 0.0).astype(BF16)
    return one(same & (c <= r)), one(same & (c >= r)), one(same)


def _tri():
    return lax.broadcasted_iota(jnp.int32, (CHUNK, CHUNK), 0) >= lax.broadcasted_iota(jnp.int32, (CHUNK, CHUNK), 1)


def _lower_bound(lbl_ref):
    l0, l1 = lbl_ref[0:1, :], lbl_ref[1:2, :]
    mx = jnp.maximum(l0, l1)
    e0, e1 = jnp.exp(l0 - mx), jnp.exp(l1 - mx)
    return e0 / (e0 + e1), e1 / (e0 + e1)


CONV_FWD_TAPS = tuple((j, HALO - (CONV_K - 1) + j) for j in range(CONV_K))
CONV_BWD_TAPS = tuple((j, (CONV_K - 1) - j) for j in range(CONV_K))


def _tap_conv(src_ref, w_ref, row0, taps, lanes):
    acc = None
    for b in range(SUB):
        pb = None
        for j, off in taps:
            if off % SUB == b:
                lo = row0 + off - b
                term = w_ref[j:j + 1, lanes] * src_ref[lo:lo + CONV_ROWS + SUB, lanes]
                pb = term if pb is None else pb + term
        if pb is not None:
            sh = pb[b:b + CONV_ROWS, :]
            acc = sh if acc is None else acc + sh
    return acc


def _hgrn_prep(pq, pf, lb, lower):
    sq = _sig(pq)
    qf = pq * sq
    sf = _sig(pf)
    f = lb + (1.0 - lb) * sf
    logf = jnp.log(f)
    k = 1.0 - f
    G = _mm3(lower, logf)
    rows, cols = G.shape
    g3 = G.reshape(rows // CHUNK, CHUNK, cols)
    Gl = jnp.broadcast_to(g3[:, CHUNK - 1:CHUNK, :], g3.shape).reshape(rows, cols)
    eG, enG, eGlG = jnp.exp(G), jnp.exp(-G), jnp.exp(Gl - G)
    return dict(sq=sq, sf=sf, f=f, Gl=Gl, eG=eG, enG=enG, eGlG=eGlG, qt=qf * eG, kt=k * enG, kh=k * eGlG)


def _ada_exchange(c_row, w_ada, b_sh, wdw_pad):
    ncol = w_ada.shape[1]

    def body(c_ref, w_ref, b_ref, wdw_ref, call_ref, c8_ref, modg_ref, wdwg_ref, rows_s, sa, ra, sw, rw, sm, rm):
        x, y, c = _place()
        me = 4 * x + 2 * y + c
        chip = 2 * x + y
        cv = c_ref[...]
        call_ref[me] = cv * _sig(cv)
        wdwg_ref[chip] = wdw_ref[...]
        sends = []
        for m in range(1, N_DEV):
            peer = (_flip(x, m >> 2), _flip(y, (m >> 1) & 1), _flip(c, m & 1))
            cp = _rcopy(call_ref.at[me], call_ref.at[me], sa.at[m - 1], ra.at[m - 1], peer)
            cp.start()
            sends.append(cp)
        for k, (fx, fy) in enumerate(OTHER_CHIPS):
            peer = (_flip(x, fx), _flip(y, fy), c)
            cp = _rcopy(wdwg_ref.at[chip], wdwg_ref.at[chip], sw.at[k], rw.at[k], peer)
            cp.start()
            sends.append(cp)
        for m in range(1, N_DEV):
            peer = (_flip(x, m >> 2), _flip(y, (m >> 1) & 1), _flip(c, m & 1))
            pid = 4 * peer[0] + 2 * peer[1] + peer[2]
            _rcopy(call_ref.at[pid], call_ref.at[pid], sa.at[m - 1], ra.at[m - 1], peer).wait_recv()
        for b in range(N_DEV):
            c8_ref[b:b + 1, :] = call_ref[b]
        mod_all = _dot(c8_ref[...].astype(BF16), w_ref[...].astype(BF16)) + b_ref[...]
        for b in range(N_DEV):
            rows_s[b] = mod_all[b:b + 1, :]
        modg_ref[chip] = rows_s[me]
        for k, (fx, fy) in enumerate(OTHER_CHIPS):
            peer = (_flip(x, fx), _flip(y, fy), c)
            pid = 4 * peer[0] + 2 * peer[1] + peer[2]
            cp = _rcopy(rows_s.at[pid], modg_ref.at[chip], sm.at[k], rm.at[k], peer)
            cp.start()
            sends.append(cp)
        for k, (fx, fy) in enumerate(OTHER_CHIPS):
            peer = (_flip(x, fx), _flip(y, fy), c)
            pchip = 2 * peer[0] + peer[1]
            _rcopy(rows_s.at[0], modg_ref.at[pchip], sm.at[k], rm.at[k], peer).wait_recv()
            _rcopy(wdwg_ref.at[pchip], wdwg_ref.at[pchip], sw.at[k], rw.at[k], peer).wait_recv()
        for cp in sends:
            cp.wait_send()

    vm = pl.BlockSpec(memory_space=pltpu.VMEM)
    return pl.pallas_call(
        body, name="ada_exchange",
        out_shape=[jax.ShapeDtypeStruct((N_DEV, 1, D_MODEL), F32), jax.ShapeDtypeStruct((N_DEV, D_MODEL), F32),
                   jax.ShapeDtypeStruct((N_CHIPS, 1, ncol), F32), jax.ShapeDtypeStruct((N_CHIPS, HALO, HEAD_D), F32)],
        in_specs=[vm] * 4, out_specs=[vm] * 4,
        scratch_shapes=[pltpu.VMEM((N_DEV, 1, ncol), F32),
                        pltpu.SemaphoreType.DMA((N_DEV - 1,)), pltpu.SemaphoreType.DMA((N_DEV - 1,)),
                        pltpu.SemaphoreType.DMA((3,)), pltpu.SemaphoreType.DMA((3,)),
                        pltpu.SemaphoreType.DMA((3,)), pltpu.SemaphoreType.DMA((3,))],
        compiler_params=_cp(None, 32),
    )(c_row, w_ada, b_sh, wdw_pad)


def _cast_own(chip1, shard, name):
    rows, cols = shard.shape
    tr = _row_tile(rows)

    def body(ch_ref, s_ref, o_ref):
        o_ref[0] = s_ref[...].astype(BF16)

    gs = pltpu.PrefetchScalarGridSpec(
        num_scalar_prefetch=1, grid=(rows // tr,),
        in_specs=[pl.BlockSpec((tr, cols), lambda i, ch: (i, 0))],
        out_specs=pl.BlockSpec((1, tr, cols), lambda i, ch: (ch[0], i, 0)))
    return pl.pallas_call(
        body, name=name, grid_spec=gs, out_shape=_big((N_CHIPS, rows, cols), BF16),
        compiler_params=_cp(("arbitrary",), 32),
    )(chip1, _hbm(shard))


def _slab(buf, ch, core):
    hs = buf.shape[1] // 2
    return buf.at[ch, pl.ds(core * hs, hs), :]


def _gather_start(bufs, ssem, rsem, relations=(0, 1, 2)):
    x, y, c = _place()
    chip = 2 * x + y
    for k in relations:
        fx, fy = OTHER_CHIPS[k]
        peer = (_flip(x, fx), _flip(y, fy), c)
        for t, buf in enumerate(bufs):
            _rcopy(_slab(buf, chip, c), _slab(buf, chip, c), ssem.at[t * 3 + k], rsem.at[t * 3 + k], peer).start()


def _gather_pass_on(bufs, ssem, rsem):
    nt = len(bufs)
    x, y, c = _place()
    sibling = (x, y, 1 - c)
    for k, (fx, fy) in enumerate(OTHER_CHIPS):
        peer = (_flip(x, fx), _flip(y, fy), c)
        pchip = 2 * peer[0] + peer[1]
        for t, buf in enumerate(bufs):
            _rcopy(_slab(buf, pchip, c), _slab(buf, pchip, c), ssem.at[t * 3 + k], rsem.at[t * 3 + k], peer).wait_recv()
            _rcopy(_slab(buf, pchip, c), _slab(buf, pchip, c), ssem.at[3 * nt + t * 3 + k],
                   rsem.at[3 * nt + t * 3 + k], sibling).start()


def _gather_drain(bufs, ssem, rsem):
    nt = len(bufs)
    x, y, c = _place()
    chip = 2 * x + y
    sibling = (x, y, 1 - c)
    for k, (fx, fy) in enumerate(OTHER_CHIPS):
        peer = (_flip(x, fx), _flip(y, fy), c)
        pchip = 2 * peer[0] + peer[1]
        for t, buf in enumerate(bufs):
            _rcopy(_slab(buf, pchip, 1 - c), _slab(buf, pchip, 1 - c), ssem.at[3 * nt + t * 3 + k],
                   rsem.at[3 * nt + t * 3 + k], sibling).wait_recv()
            _rcopy(_slab(buf, chip, c), _slab(buf, chip, c), ssem.at[t * 3 + k], rsem.at[t * 3 + k], peer).wait_send()
            _rcopy(_slab(buf, pchip, c), _slab(buf, pchip, c), ssem.at[3 * nt + t * 3 + k],
                   rsem.at[3 * nt + t * 3 + k], sibling).wait_send()


def _gather_finish(bufs, ssem, rsem):
    _gather_pass_on(bufs, ssem, rsem)
    _gather_drain(bufs, ssem, rsem)


def _gather_arrive(bufs, k, ssem, rsem):
    nt = len(bufs)
    x, y, c = _place()
    fx, fy = OTHER_CHIPS[k]
    peer = (_flip(x, fx), _flip(y, fy), c)
    pchip = 2 * peer[0] + peer[1]
    for t, buf in enumerate(bufs):
        _rcopy(_slab(buf, pchip, c), _slab(buf, pchip, c), ssem.at[t * 3 + k], rsem.at[t * 3 + k], peer).wait_recv()
        _rcopy(_slab(buf, pchip, c), _slab(buf, pchip, c), ssem.at[3 * nt + t * 3 + k],
               rsem.at[3 * nt + t * 3 + k], (x, y, 1 - c)).start()
    for t, buf in enumerate(bufs):
        _rcopy(_slab(buf, pchip, 1 - c), _slab(buf, pchip, 1 - c), ssem.at[3 * nt + t * 3 + k],
               rsem.at[3 * nt + t * 3 + k], (x, y, 1 - c)).wait_recv()


def _gather_sends_done(bufs, ssem, rsem):
    nt = len(bufs)
    x, y, c = _place()
    chip = 2 * x + y
    for k, (fx, fy) in enumerate(OTHER_CHIPS):
        peer = (_flip(x, fx), _flip(y, fy), c)
        pchip = 2 * peer[0] + peer[1]
        for t, buf in enumerate(bufs):
            _rcopy(_slab(buf, chip, c), _slab(buf, chip, c), ssem.at[t * 3 + k], rsem.at[t * 3 + k], peer).wait_send()
            _rcopy(_slab(buf, pchip, c), _slab(buf, pchip, c), ssem.at[3 * nt + t * 3 + k],
                   rsem.at[3 * nt + t * 3 + k], (x, y, 1 - c)).wait_send()


def _ring_parts(buf):
    x, y, c = _place()
    ynb, xnb = (x, 1 - y, c), (1 - x, y, c)
    ychip, xchip, dchip = 2 * x + (1 - y), 2 * (1 - x) + y, 2 * (1 - x) + (1 - y)
    hs = buf.shape[1] // 2

    def piece(ch, q):
        return buf.at[ch, pl.ds(c * hs + q * (hs // 2), hs // 2), :]

    return ynb, xnb, ychip, xchip, dchip, piece


def _ring_start(bufs, ssem, rsem):
    x, y, c = _place()
    chip = 2 * x + y
    for t, buf in enumerate(bufs):
        ynb, xnb, _, _, _, _ = _ring_parts(buf)
        _rcopy(_slab(buf, chip, c), _slab(buf, chip, c), ssem.at[2 * t], rsem.at[2 * t], ynb).start()
        _rcopy(_slab(buf, chip, c), _slab(buf, chip, c), ssem.at[2 * t + 1], rsem.at[2 * t + 1], xnb).start()


def _ring_forward(bufs, ssem, rsem):
    nt = len(bufs)
    _, _, c = _place()
    for t, buf in enumerate(bufs):
        ynb, xnb, ychip, xchip, _, piece = _ring_parts(buf)
        _rcopy(_slab(buf, ychip, c), _slab(buf, ychip, c), ssem.at[2 * t], rsem.at[2 * t], ynb).wait_recv()
        _rcopy(piece(ychip, 0), piece(ychip, 0), ssem.at[2 * nt + 2 * t], rsem.at[2 * nt + 2 * t], xnb).start()
        _rcopy(_slab(buf, xchip, c), _slab(buf, xchip, c), ssem.at[2 * t + 1], rsem.at[2 * t + 1], xnb).wait_recv()
        _rcopy(piece(xchip, 1), piece(xchip, 1), ssem.at[2 * nt + 2 * t + 1], rsem.at[2 * nt + 2 * t + 1], ynb).start()


def _ring_finish(bufs, ssem, rsem):
    nt = len(bufs)
    x, y, c = _place()
    chip = 2 * x + y
    sibling = (x, y, 1 - c)
    for t, buf in enumerate(bufs):
        ynb, xnb, ychip, xchip, dchip, piece = _ring_parts(buf)
        _rcopy(piece(dchip, 0), piece(dchip, 0), ssem.at[2 * nt + 2 * t], rsem.at[2 * nt + 2 * t], xnb).wait_recv()
        _rcopy(piece(dchip, 1), piece(dchip, 1), ssem.at[2 * nt + 2 * t + 1], rsem.at[2 * nt + 2 * t + 1],
               ynb).wait_recv()
        for k, ch in enumerate((ychip, xchip, dchip)):
            _rcopy(_slab(buf, ch, c), _slab(buf, ch, c), ssem.at[4 * nt + 3 * t + k], rsem.at[4 * nt + 3 * t + k],
                   sibling).start()
    for t, buf in enumerate(bufs):
        ynb, xnb, ychip, xchip, dchip, piece = _ring_parts(buf)
        for k, ch in enumerate((ychip, xchip, dchip)):
            _rcopy(_slab(buf, ch, 1 - c), _slab(buf, ch, 1 - c), ssem.at[4 * nt + 3 * t + k],
                   rsem.at[4 * nt + 3 * t + k], sibling).wait_recv()
            _rcopy(_slab(buf, ch, c), _slab(buf, ch, c), ssem.at[4 * nt + 3 * t + k], rsem.at[4 * nt + 3 * t + k],
                   sibling).wait_send()
        _rcopy(_slab(buf, chip, c), _slab(buf, chip, c), ssem.at[2 * t], rsem.at[2 * t], ynb).wait_send()
        _rcopy(_slab(buf, chip, c), _slab(buf, chip, c), ssem.at[2 * t + 1], rsem.at[2 * t + 1], xnb).wait_send()
        _rcopy(piece(ychip, 0), piece(ychip, 0), ssem.at[2 * nt + 2 * t], rsem.at[2 * nt + 2 * t], xnb).wait_send()
        _rcopy(piece(xchip, 1), piece(xchip, 1), ssem.at[2 * nt + 2 * t + 1], rsem.at[2 * nt + 2 * t + 1],
               ynb).wait_send()


def _ring_sems(nt):
    return [pltpu.SemaphoreType.DMA((7 * nt,)), pltpu.SemaphoreType.DMA((7 * nt,))]


def _gather_sems(nt):
    return [pltpu.SemaphoreType.DMA((6 * nt,)), pltpu.SemaphoreType.DMA((6 * nt,))]


def _pair_copies(ins, outs, ssem, rsem):
    x, y, c = _place()
    copies = []
    for t in range(len(ins)):
        hs = ins[t].shape[1] // 2
        copies.append(_rcopy(ins[t].at[:, pl.ds((1 - c) * hs, hs), :], outs[t], ssem.at[t], rsem.at[t], (x, y, 1 - c)))
    return copies


def _pair_shapes(grads):
    return [_big((g.shape[0], g.shape[1] // 2, g.shape[2]), g.dtype) for g in grads]


def _pair_sems(nt):
    return [pltpu.SemaphoreType.DMA((nt,)), pltpu.SemaphoreType.DMA((nt,))]


def _pair_swap(grads, name):
    nt = len(grads)
    hbm = pl.BlockSpec(memory_space=pl.ANY)

    def body(*refs):
        copies = _pair_copies(refs[:nt], refs[nt:2 * nt], refs[2 * nt], refs[2 * nt + 1])
        for cp in copies:
            cp.start()
        for cp in copies:
            cp.wait_recv()
        for cp in copies:
            cp.wait_send()

    return pl.pallas_call(
        body, name=name, out_shape=_pair_shapes(grads), in_specs=[hbm] * nt, out_specs=[hbm] * nt,
        scratch_shapes=_pair_sems(nt),
    )(*[_hbm(g) for g in grads])


def _xchg_copies(ins, outs, ssem, rsem):
    x, y, c = _place()
    copies = []
    for k, (fx, fy) in enumerate(OTHER_CHIPS):
        peer = (_flip(x, fx), _flip(y, fy), c)
        for t in range(len(ins)):
            copies.append(_rcopy(ins[t].at[k], outs[t].at[k], ssem.at[t * 3 + k], rsem.at[t * 3 + k], peer))
    return copies


def _xchg_sems(nt):
    return [pltpu.SemaphoreType.DMA((3 * nt,)), pltpu.SemaphoreType.DMA((3 * nt,))]


def _final_exchange(fulls, stats):
    nt = len(fulls)
    rows, cols = stats.shape
    hbm = pl.BlockSpec(memory_space=pl.ANY)
    vm = pl.BlockSpec(memory_space=pltpu.VMEM)

    def body(*refs):
        ins, s_ref = refs[:nt], refs[nt]
        outs, g_ref = refs[nt + 1:2 * nt + 1], refs[2 * nt + 1]
        hssem, hrsem, ssem, rsem = refs[2 * nt + 2:]
        x, y, c = _place()
        me, sibling = (x, y, c), (x, y, 1 - c)
        halves = []
        for t in range(nt):
            hs = ins[t].shape[0] // 2
            mine = pl.ds(c * hs, hs)
            cp = _rcopy(ins[t].at[mine, :], outs[t].at[mine, :], hssem.at[t], hrsem.at[t], sibling)
            cp.start()
            halves.append(cp)

        chips = [(_flip(x, fx), _flip(y, fy)) for fx, fy in OTHER_CHIPS]

        def blk(px, py, pc):
            return g_ref.at[4 * px + 2 * py + pc]

        def copy(k, block, to, src=None):
            return _rcopy(blk(*block) if src is None else src, blk(*block), ssem.at[k], rsem.at[k], to)

        g_ref[4 * x + 2 * y + c] = s_ref[...]
        first = [copy(0, me, sibling, src=s_ref)]
        first += [copy(1 + j, me, (*chip, c), src=s_ref) for j, chip in enumerate(chips)]
        for cp in first:
            cp.start()
        passed = [copy(4 + j, (*chip, c), sibling) for j, chip in enumerate(chips)]
        for j, chip in enumerate(chips):
            copy(1 + j, (*chip, c), me).wait_recv()
            passed[j].start()
        copy(0, sibling, me).wait_recv()
        for j, chip in enumerate(chips):
            copy(4 + j, (*chip, 1 - c), me).wait_recv()
        for t in range(nt):
            hs = ins[t].shape[0] // 2
            other = pl.ds((1 - c) * hs, hs)
            _rcopy(ins[t].at[other, :], outs[t].at[other, :], hssem.at[t], hrsem.at[t], sibling).wait_recv()
        for cp in first + passed + halves:
            cp.wait_send()

    outs = pl.pallas_call(
        body, name="final_exchange",
        out_shape=[_big(f.shape, F32) for f in fulls] + [jax.ShapeDtypeStruct((N_DEV, rows, cols), F32)],
        in_specs=[hbm] * nt + [vm], out_specs=[hbm] * nt + [vm],
        input_output_aliases={t: t for t in range(nt)},
        scratch_shapes=[pltpu.SemaphoreType.DMA((nt,)), pltpu.SemaphoreType.DMA((nt,)),
                        pltpu.SemaphoreType.DMA((7,)), pltpu.SemaphoreType.DMA((7,))],
        compiler_params=_cp(None, 32),
    )(*[_hbm(f) for f in fulls], stats)
    return outs[:nt], outs[nt]


def _row_tile(rows):
    return min(rows, 512)


def _pair_sum(place, grad, got, name):
    nb, hs, cols = got.shape
    tr = _row_tile(hs)
    nr = hs // tr

    def body(pl_ref, g_ref, o_ref, pf_ref, pb_ref):
        j = pl.program_id(1)
        s = g_ref[0] + o_ref[0].astype(F32)

        @pl.when(j == 0)
        def _():
            pf_ref[...] = s

        @pl.when(j > 0)
        def _():
            pb_ref[0] = s.astype(BF16)

    gs = pltpu.PrefetchScalarGridSpec(
        num_scalar_prefetch=1, grid=(nr, nb),
        in_specs=[pl.BlockSpec((1, tr, cols), lambda i, j, p: (p[1] ^ j, p[0] * nr + i, 0)),
                  pl.BlockSpec((1, tr, cols), lambda i, j, p: (p[1] ^ j, i, 0))],
        out_specs=[pl.BlockSpec((tr, cols), lambda i, j, p: (i, 0)),
                   pl.BlockSpec((1, tr, cols), lambda i, j, p: (jnp.maximum(j - 1, 0), i, 0))])
    return pl.pallas_call(
        body, name=name, grid_spec=gs,
        out_shape=[_big((hs, cols), F32), _big((nb - 1, hs, cols), BF16)],
        compiler_params=_cp(("arbitrary", "arbitrary"), 32),
    )(place, *_hbm(grad, got))


def _chip_sum(place, pair_f, got_b, name):
    nb, hs, cols = got_b.shape
    tr = _row_tile(hs)
    nr = hs // tr

    def body(pl_ref, pf_ref, gb_ref, o_ref):
        acc = pf_ref[...]
        for k in range(nb):
            acc = acc + gb_ref[k].astype(F32)
        o_ref[...] = acc

    gs = pltpu.PrefetchScalarGridSpec(
        num_scalar_prefetch=1, grid=(nr,),
        in_specs=[pl.BlockSpec((tr, cols), lambda i, p: (i, 0)),
                  pl.BlockSpec((nb, tr, cols), lambda i, p: (0, i, 0))],
        out_specs=pl.BlockSpec((tr, cols), lambda i, p: (p[0] * nr + i, 0)))
    return pl.pallas_call(
        body, name=name, grid_spec=gs,
        out_shape=_big((2 * hs, cols), F32),
        compiler_params=_cp(("arbitrary",), 32),
    )(place, *_hbm(pair_f, got_b))


def _adam_math(w, g, m, v):
    m2 = ADAM_B1 * m + (1.0 - ADAM_B1) * g
    v2 = ADAM_B2 * v + (1.0 - ADAM_B2) * (g * g)
    m_hat = m2 / (1.0 - ADAM_B1 ** ADAM_STEP)
    v_hat = v2 / (1.0 - ADAM_B2 ** ADAM_STEP)
    delta = -ADAM_LR * (m_hat / (jnp.sqrt(v_hat) + ADAM_EPS) + ADAM_WD * w)
    return delta, m2, v2


def _adam_big(w, g, m, v, name):
    rows, cols = w.shape
    tr = _row_tile(rows)

    def body(w_ref, g_ref, m_ref, v_ref, d_out, m_out, v_out):
        d, m2, v2 = _adam_math(w_ref[...], g_ref[...], m_ref[...], v_ref[...])
        d_out[...] = d
        m_out[...] = m2
        v_out[...] = v2

    spec = pl.BlockSpec((tr, cols), lambda i: (i, 0))
    return pl.pallas_call(
        body, name=name, grid=(rows // tr,), in_specs=[spec] * 4, out_specs=[spec] * 3,
        out_shape=[_big(w.shape, F32)] * 3,
        compiler_params=_cp(("arbitrary",), 48),
    )(*_hbm(w, g, m, v))


def _ada_grad_adam(c8t, dmod_sh, w, m, v):
    rows, cols = w.shape
    tr = _row_tile(rows) // 2

    def body(ct_ref, dm_ref, w_ref, m_ref, v_ref, g_out, d_out, m_out, v_out):
        g = None
        for b in range(N_DEV):
            term = ct_ref[:, b:b + 1] * dm_ref[b:b + 1, :]
            g = term if g is None else g + term
        d, m2, v2 = _adam_math(w_ref[...], g, m_ref[...], v_ref[...])
        g_out[...] = g
        d_out[...] = d
        m_out[...] = m2
        v_out[...] = v2

    spec = pl.BlockSpec((tr, cols), lambda i: (i, 0))
    return pl.pallas_call(
        body, name="ada_grad_adam", grid=(rows // tr,),
        in_specs=[pl.BlockSpec((tr, N_DEV), lambda i: (i, 0)), _full((N_DEV, cols)), spec, spec, spec],
        out_specs=[spec] * 4, out_shape=[_big(w.shape, F32)] * 4,
        compiler_params=_cp(("arbitrary",), 32),
    )(c8t, dmod_sh, *_hbm(w, m, v))


def _tok_tile(t):
    return min(TOK_TILE, t)


def _mix_in_fwd(chip1, x, modr, g_pre, b_in4, later, w_in_buf, w_out_buf):
    T = x.shape[0]
    tm = min(MIXIN_TILE, T)
    nt = T // tm
    nb = IN_COLS // N_CHIPS
    nl = len(later)

    def body(*refs):
        ch_ref, x_ref, mod_ref, g_ref, b_ref = refs[:5]
        l_ins = refs[5:5 + nl]
        p_ref, h_ref = refs[7 + nl:9 + nl]
        l_outs = refs[9 + nl:9 + 2 * nl]
        win_ref, wout_ref = refs[9 + 2 * nl:11 + 2 * nl]
        h_all, wblk, lsem, is_sem, ir_sem, os_sem, or_sem = refs[11 + 2 * nl:]
        k, i = pl.program_id(0), pl.program_id(1)
        chip = ch_ref[0]

        @pl.when(k == 0)
        def _():
            for src, dst in zip(l_ins, l_outs):
                dst[0] = src[...].astype(BF16)

        def load_block(blk):
            cp = pltpu.make_async_copy(win_ref.at[blk], wblk, lsem)
            cp.start()
            cp.wait()

        @pl.when((k == 0) & (i == 0))
        def _():
            _gather_start([win_ref], is_sem, ir_sem, relations=(0, 1))
            load_block(chip)

        for r in range(N_CHIPS - 1):
            @pl.when((k == r + 1) & (i == 0))
            def _(r=r):
                _gather_arrive([win_ref], r, is_sem, ir_sem)
                if r == 0:
                    _gather_start([win_ref], is_sem, ir_sem, relations=(2,))
                if r == 1:
                    _gather_start([wout_ref], os_sem, or_sem)
                load_block(chip ^ (r + 1))

        rows = pl.ds(pl.multiple_of(i * tm, tm), tm)

        @pl.when(k == 0)
        def _():
            xv = x_ref[...]
            rstd = lax.rsqrt(jnp.mean(xv * xv, axis=-1, keepdims=True) + RMS_EPS)
            h = (xv * rstd) * g_ref[...] * (1.0 + mod_ref[1:2, :]) + mod_ref[0:1, :]
            hb = h.astype(BF16)
            h_ref[...] = hb
            h_all[rows, :] = hb

        p_ref[...] = _dot(h_all[rows, :], wblk[...]) + b_ref[chip ^ k]

        @pl.when((k == N_CHIPS - 1) & (i == nt - 1))
        def _():
            _gather_sends_done([win_ref], is_sem, ir_sem)
            _gather_finish([wout_ref], os_sem, or_sem)

    hbm = pl.BlockSpec(memory_space=pl.ANY)
    first_pass = lambda k, i, ch: (jnp.where(k == 0, i, nt - 1), 0)
    own_slot = lambda k, i, ch: (ch[0], jnp.where(k == 0, i, nt - 1), 0)
    gs = pltpu.PrefetchScalarGridSpec(
        num_scalar_prefetch=1, grid=(N_CHIPS, nt),
        in_specs=[pl.BlockSpec((tm, D_MODEL), first_pass), pl.BlockSpec((6, D_MODEL), lambda k, i, ch: (0, 0)),
                  pl.BlockSpec((1, D_MODEL), lambda k, i, ch: (0, 0)),
                  pl.BlockSpec((N_CHIPS, 1, nb), lambda k, i, ch: (0, 0, 0))]
        + [pl.BlockSpec((w.shape[0] // nt, w.shape[1]), first_pass) for w in later] + [hbm, hbm],
        out_specs=[pl.BlockSpec((tm, nb), lambda k, i, ch: (i, ch[0] ^ k)), pl.BlockSpec((tm, D_MODEL), first_pass)]
        + [pl.BlockSpec((1, w.shape[0] // nt, w.shape[1]), own_slot) for w in later] + [hbm, hbm],
        scratch_shapes=[pltpu.VMEM((T, D_MODEL), BF16), pltpu.VMEM((D_MODEL, nb), BF16), pltpu.SemaphoreType.DMA]
        + _gather_sems(1) + _gather_sems(1))
    outs = pl.pallas_call(
        body, name="mix_in_fwd", grid_spec=gs,
        out_shape=[_big((T, IN_COLS), F32), _big((T, D_MODEL), BF16)]
        + [_big((N_CHIPS,) + w.shape, BF16) for w in later]
        + [_big(w_in_buf.shape, BF16), _big(w_out_buf.shape, BF16)],
        input_output_aliases={5 + nl: 2 + nl, 6 + nl: 3 + nl},
        compiler_params=_cp(("arbitrary", "arbitrary"), 48),
    )(chip1, _hbm(x), modr, g_pre, b_in4, *[_hbm(w) for w in later], _hbm(w_in_buf), _hbm(w_out_buf))
    return outs[0], outs[1], outs[2:2 + nl], outs[2 + nl], outs[3 + nl]


def _mixers_fwd(p, wdw, vecs, lbl, w_out, x, modr, g_post, g_ffn, gbufs):
    T = p.shape[0]
    tm = _tok_tile(T)
    nt = T // tm
    nch = tm // CHUNK
    ng = len(gbufs)
    n_in, n_out = 12, 7

    def body(*refs):
        (p_ref, wdw_ref, bdw_ref, gain_ref, bias_ref, gout_ref, lbl_ref, wout_ref, x_ref, mod_ref, gp_ref,
         gf_ref) = refs[:n_in]
        cat_ref, ys_ref, o_ref, st_ref, y_ref, x1_ref, h2_ref = refs[n_in + ng:n_in + ng + n_out]
        gout_bufs = refs[n_in + ng + n_out:n_in + 2 * ng + n_out]
        (ubuf, state, qt_s, kt_s, kh_s, v_s, egl_s, lower_s, gmat_s, gssem,
         grsem) = refs[n_in + 2 * ng + n_out:]
        i = pl.program_id(0)

        @pl.when(i == 0)
        def _():
            _ring_start(gout_bufs, gssem, grsem)
            lower_s[...], _, _ = _chunk_masks(tm)
            gmat_s[...] = _gn_matrix()
            state[...] = jnp.zeros(state.shape, F32)
            ubuf[0:HALO, :] = jnp.zeros((HALO, CONV_CH), F32)
            ubuf[HALO + tm:HALO + tm + SUB, :] = jnp.zeros((SUB, CONV_CH), F32)

        @pl.when(i > 0)
        def _():
            ubuf[0:HALO, :] = ubuf[tm:tm + HALO, :]

        ubuf[HALO:HALO + tm, :] = p_ref[:, 0:CONV_CH] * _sig(p_ref[:, CONV_CH:2 * CONV_CH])
        for r in range(tm // CONV_ROWS):
            rows = slice(r * CONV_ROWS, (r + 1) * CONV_ROWS)
            for lb_ in range(CONV_CH // LANE):
                lanes = slice(lb_ * LANE, (lb_ + 1) * LANE)
                ys_ref[rows, lanes] = bdw_ref[:, lanes] + _tap_conv(ubuf, wdw_ref, r * CONV_ROWS, CONV_FWD_TAPS, lanes)
        gmat = gmat_s[...]
        yv = ys_ref[...]
        d = yv - _gmean(yv, gmat)
        rs = lax.rsqrt(_gmean(d * d, gmat) + GN_EPS)
        z = d * rs * gain_ref[...] + bias_ref[...]
        cat_ref[:, 0:CONV_CH] = (z * _sig(z)).astype(BF16)

        lb, _ = _lower_bound(lbl_ref)
        o0 = 2 * CONV_CH
        pr = _hgrn_prep(p_ref[:, o0:o0 + HGRN_W], p_ref[:, o0 + HGRN_W:o0 + 2 * HGRN_W], lb, lower_s[...])
        qt_s[...] = pr["qt"].astype(BF16)
        kt_s[...] = pr["kt"].astype(BF16)
        kh_s[...] = pr["kh"].astype(BF16)
        v_s[...] = p_ref[:, o0 + 2 * HGRN_W:o0 + 3 * HGRN_W].astype(BF16)
        egl_s[...] = jnp.exp(pr["Gl"])
        tri = _tri()

        def chunk(ci, carry):
            r0 = pl.multiple_of(ci * CHUNK, CHUNK)
            rows = pl.ds(r0, CHUNK)
            for h in range(N_HEADS):
                ls = pl.ds(h * HEAD_D, HEAD_D)
                qc, kc, hc, vc = qt_s[rows, ls], kt_s[rows, ls], kh_s[rows, ls], v_s[rows, ls]
                s0 = state[h]
                s0b = s0.astype(BF16)
                st_ref[ci, h] = s0
                att =jnp.where(tri, _dot_nt(qc, kc), 0.0).astype(BF16)
                o_ref[rows, ls] = _dot(att, vc) + _dot_nt(qc, s0b)
                state[h] = s0 * egl_s[pl.ds(r0, 1), ls] + _dot_tn(vc, hc)
            return carry

        lax.fori_loop(0, nch, chunk, 0, unroll=min(CHUNK_UNROLL, nch))
        for h in range(N_HEADS):
            sl = slice(h * HEAD_D, (h + 1) * HEAD_D)
            oh = o_ref[:, sl]
            gh = p_ref[:, o0 + 3 * HGRN_W + h * HEAD_D:o0 + 3 * HGRN_W + (h + 1) * HEAD_D]
            rsh = lax.rsqrt(jnp.mean(oh * oh, axis=-1, keepdims=True) + RMS_EPS)
            hg = (oh * rsh) * gout_ref[:, sl] * (gh * _sig(gh))
            cat_ref[:, CONV_CH + h * HEAD_D:CONV_CH + (h + 1) * HEAD_D] = hg.astype(BF16)

        yv = _dot(cat_ref[...], wout_ref[...])
        y_ref[...] = yv
        rsy = lax.rsqrt(jnp.mean(yv * yv, axis=-1, keepdims=True) + RMS_EPS)
        x1 = x_ref[...] + (yv * rsy) * (mod_ref[2:3, :] * gp_ref[...])
        x1_ref[...] = x1
        rs1 = lax.rsqrt(jnp.mean(x1 * x1, axis=-1, keepdims=True) + RMS_EPS)
        h2 = (x1 * rs1) * (gf_ref[...] * (1.0 + mod_ref[4:5, :])) + mod_ref[3:4, :]
        h2_ref[...] = h2.astype(BF16)

        @pl.when(i == min(nt - 1, nt // 2 + 1))
        def _():
            _ring_forward(gout_bufs, gssem, grsem)

        @pl.when(i == nt - 1)
        def _():
            _ring_finish(gout_bufs, gssem, grsem)

    tile = lambda cols: pl.BlockSpec((tm, cols), lambda i: (i, 0))
    hbm = pl.BlockSpec(memory_space=pl.ANY)
    outs = pl.pallas_call(
        body, name="mixers_fwd", grid=(nt,),
        in_specs=[tile(IN_COLS), _full((HALO, CONV_CH))] + [_full((1, CONV_CH))] * 4 + [_full((2, HGRN_W))]
        + [_full((D_MODEL, D_MODEL)), tile(D_MODEL), _full((6, D_MODEL)), _full((1, D_MODEL)), _full((1, D_MODEL))]
        + [hbm] * ng,
        out_specs=[tile(D_MODEL), tile(CONV_CH), tile(HGRN_W),
                   pl.BlockSpec((nch, N_HEADS, HEAD_D, HEAD_D), lambda i: (i, 0, 0, 0)),
                   tile(D_MODEL), tile(D_MODEL), tile(D_MODEL)] + [hbm] * ng,
        out_shape=[_big((T, D_MODEL), BF16), _big((T, CONV_CH), F32), _big((T, HGRN_W), F32),
                   _big((T // CHUNK, N_HEADS, HEAD_D, HEAD_D), F32), _big((T, D_MODEL), F32),
                   _big((T, D_MODEL), F32), _big((T, D_MODEL), BF16)] + [_big(b.shape, BF16) for b in gbufs],
        input_output_aliases={n_in + t: n_out + t for t in range(ng)},
        scratch_shapes=[pltpu.VMEM((tm + HALO + SUB, CONV_CH), F32), pltpu.VMEM((N_HEADS, HEAD_D, HEAD_D), F32),
                        pltpu.VMEM((tm, HGRN_W), BF16), pltpu.VMEM((tm, HGRN_W), BF16),
                        pltpu.VMEM((tm, HGRN_W), BF16), pltpu.VMEM((tm, HGRN_W), BF16),
                        pltpu.VMEM((tm, HGRN_W), F32), pltpu.VMEM((tm, tm), BF16),
                        pltpu.VMEM((CONV_CH, CONV_CH), BF16)] + _ring_sems(ng),
        compiler_params=_cp(("arbitrary",), 56),
    )(_hbm(p), wdw, *vecs, lbl, *_hbm(w_out, x), modr, g_post, g_ffn, *[_hbm(b) for b in gbufs])
    return outs[:n_out], outs[n_out:]


def _row_chains(rows, n=2):
    step = rows // n
    return [slice(k * step, (k + 1) * step) for k in range(n)]


def _ffn_blocks():
    return D_FF // FFN_BLOCK, (D_FF // N_CHIPS) // FFN_BLOCK


def _ffn_fwd(h2, w_up_g, w_down, x1, target, modr, g_post):
    T = h2.shape[0]
    tm = min(FFN_TILE, T)
    fb = FFN_BLOCK
    nj, per = _ffn_blocks()

    def body(h_ref, wu_ref, wd_ref, x1_ref, t_ref, mod_ref, g_ref, r_ref, dy2_ref, dx2_ref, st_ref, wut_ref, wdt_ref,
             acc):
        i, j = pl.program_id(0), pl.program_id(1)

        @pl.when((i == 0) & (j == 0))
        def _():
            st_ref[...] = jnp.zeros(st_ref.shape, F32)

        @pl.when(i == 0)
        def _():
            wut_ref[...] = wu_ref[0].T
            wdt_ref[...] = wd_ref[...].T

        @pl.when(j == 0)
        def _():
            acc[...] = jnp.zeros(acc.shape, F32)

        for rows in _row_chains(tm):
            ra = jnp.maximum(_dot(h_ref[rows, :], wu_ref[0]), 0.0)
            rb = (ra * ra).astype(BF16)
            r_ref[rows, :] = rb
            acc[rows, :] += _dot(rb, wd_ref[...])

        @pl.when(j == nj - 1)
        def _():
            y2 = acc[...]
            rs = lax.rsqrt(jnp.mean(y2 * y2, axis=-1, keepdims=True) + RMS_EPS)
            nh = y2 * rs
            gp, gt = g_ref[...], mod_ref[5:6, :]
            gate_gain = gt * gp
            err = x1_ref[...] + nh * gate_gain - t_ref[...]
            dx2 = err * (1.0 / D_MODEL)
            dx2_ref[...] = dx2
            st_ref[0:1, :] += _colsum(err * err)
            s_dn = _colsum(dx2 * nh)
            st_ref[1:2, :] += s_dn * gp
            st_ref[2:3, :] += s_dn * gt
            dy2_ref[...] = _rms_bwd(dx2 * gate_gain, nh, rs).astype(BF16)

    tile = pl.BlockSpec((tm, D_MODEL), lambda i, j: (i, 0))
    first_tile = lambda i, j: jnp.where(i == 0, j, nj - 1)
    return pl.pallas_call(
        body, name="ffn_fwd", grid=(T // tm, nj),
        in_specs=[tile, pl.BlockSpec((1, D_MODEL, fb), lambda i, j: (j // per, 0, j % per)),
                  pl.BlockSpec((fb, D_MODEL), lambda i, j: (j, 0)), tile, tile,
                  _full((6, D_MODEL)), _full((1, D_MODEL))],
        out_specs=[pl.BlockSpec((tm, fb), lambda i, j: (i, j)), tile, tile, _full((8, D_MODEL)),
                   pl.BlockSpec((fb, D_MODEL), lambda i, j: (first_tile(i, j), 0)),
                   pl.BlockSpec((D_MODEL, fb), lambda i, j: (0, first_tile(i, j)))],
        out_shape=[_big((T, D_FF), BF16), _big((T, D_MODEL), BF16), _big((T, D_MODEL), F32),
                   jax.ShapeDtypeStruct((8, D_MODEL), F32), _big((D_FF, D_MODEL), BF16), _big((D_MODEL, D_FF), BF16)],
        scratch_shapes=[pltpu.VMEM((tm, D_MODEL), F32)],
        compiler_params=_cp(("arbitrary", "arbitrary"), 56),
    )(*_hbm(h2, w_up_g, w_down, x1, target), modr, g_post)


def _rms_bwd(dxn, xn, rs):
    return rs * (dxn - xn * jnp.mean(dxn * xn, axis=-1, keepdims=True))


def _ffn_bwd(dy2, r, x1, dx2, w_up_t, w_down_t, modr, g_ffn):
    T = dx2.shape[0]
    tm = min(FFN_TILE, T)
    fb = FFN_BLOCK
    nj, per = _ffn_blocks()

    def body(dy2_ref, r_ref, x1_ref, dx2_ref, wu_ref, wd_ref, mod_ref, gf_ref, da_ref, dx1_ref, st_ref, dh_s):
        i, j = pl.program_id(0), pl.program_id(1)

        @pl.when((i == 0) & (j == 0))
        def _():
            st_ref[...] = jnp.zeros(st_ref.shape, F32)

        @pl.when(j == 0)
        def _():
            dh_s[...] = jnp.zeros(dh_s.shape, F32)

        for rows in _row_chains(tm):
            ra = jnp.sqrt(r_ref[rows, :].astype(F32))
            da = (_dot(dy2_ref[rows, :], wd_ref[...]) * (2.0 * ra)).astype(BF16)
            da_ref[rows, :] = da
            dh_s[rows, :] += _dot(da, wu_ref[...])

        @pl.when(j == nj - 1)
        def _():
            dh = dh_s[...]
            x1v = x1_ref[...]
            rs1 = lax.rsqrt(jnp.mean(x1v * x1v, axis=-1, keepdims=True) + RMS_EPS)
            xn = x1v * rs1
            gf, scale1 = gf_ref[...], 1.0 + mod_ref[4:5, :]
            st_ref[0:1, :] += _colsum(dh)
            s_dh = _colsum(dh * xn)
            st_ref[1:2, :] += s_dh * gf
            st_ref[2:3, :] += s_dh * scale1
            dx1_ref[...] = dx2_ref[...] + _rms_bwd(dh * (scale1 * gf), xn, rs1)

    tile = pl.BlockSpec((tm, D_MODEL), lambda i, j: (i, 0))
    ftile = pl.BlockSpec((tm, fb), lambda i, j: (i, j))
    return pl.pallas_call(
        body, name="ffn_bwd", grid=(T // tm, nj),
        in_specs=[tile, ftile, tile, tile, pl.BlockSpec((fb, D_MODEL), lambda i, j: (j, 0)),
                  pl.BlockSpec((D_MODEL, fb), lambda i, j: (0, j)), _full((6, D_MODEL)), _full((1, D_MODEL))],
        out_specs=[ftile, tile, _full((8, D_MODEL))],
        out_shape=[_big((T, D_FF), BF16), _big((T, D_MODEL), F32), jax.ShapeDtypeStruct((8, D_MODEL), F32)],
        scratch_shapes=[pltpu.VMEM((tm, D_MODEL), F32)],
        compiler_params=_cp(("arbitrary", "arbitrary"), 56),
    )(*_hbm(dy2, r, x1, dx2, w_up_t, w_down_t), modr, g_ffn)


def _mix_out_bwd(dx1, y, cat, w_out, modr, g_post, swap):
    T = dx1.shape[0]
    tm = _tok_tile(T)
    nt = T // tm
    ns = len(swap)

    def body(*refs):
        dx1_ref, y_ref, cat_ref, w_ref, mod_ref, gp_ref = refs[:6]
        s_ins = refs[6:6 + ns]
        dcat_ref, st_ref, gw_ref = refs[6 + ns:9 + ns]
        s_outs = refs[9 + ns:9 + 2 * ns]
        gacc, gsem, pssem, prsem = refs[9 + 2 * ns:]
        i = pl.program_id(0)

        @pl.when(i == 0)
        def _():
            for cp in _pair_copies(s_ins, s_outs, pssem, prsem):
                cp.start()
            st_ref[...] = jnp.zeros(st_ref.shape, F32)
            gacc[...] = jnp.zeros(gacc.shape, F32)

        dxv, yv = dx1_ref[...], y_ref[...]
        rs = lax.rsqrt(jnp.mean(yv * yv, axis=-1, keepdims=True) + RMS_EPS)
        nh = yv * rs
        st_ref[0:1, :] += _colsum(dxv * (nh * gp_ref[...]))
        dn = dxv * mod_ref[2:3, :]
        st_ref[1:2, :] += _colsum(dn * nh)
        dy = _rms_bwd(dn * gp_ref[...], nh, rs).astype(BF16)
        dcat_ref[...] = _dot_nt(dy, w_ref[...])
        for cols in _row_chains(D_MODEL):
            gacc[:, cols] += _dot_tn(cat_ref[...], dy[:, cols])

        @pl.when(i == nt - 1)
        def _():
            out = pltpu.make_async_copy(gacc, gw_ref, gsem)
            out.start()
            copies = _pair_copies(s_ins, s_outs, pssem, prsem)
            for cp in copies:
                cp.wait_recv()
            for cp in copies:
                cp.wait_send()
            out.wait()

    tile = pl.BlockSpec((tm, D_MODEL), lambda i: (i, 0))
    hbm = pl.BlockSpec(memory_space=pl.ANY)
    outs = pl.pallas_call(
        body, name="mix_out_bwd", grid=(nt,),
        in_specs=[tile, tile, tile, _full((D_MODEL, D_MODEL)), _full((6, D_MODEL)), _full((1, D_MODEL))]
        + [hbm] * ns,
        out_specs=[tile, _full((8, D_MODEL)), hbm] + [hbm] * ns,
        out_shape=[_big((T, D_MODEL), F32), jax.ShapeDtypeStruct((8, D_MODEL), F32), _big((D_MODEL, D_MODEL), F32)]
        + _pair_shapes(swap),
        scratch_shapes=[pltpu.VMEM((D_MODEL, D_MODEL), F32), pltpu.SemaphoreType.DMA] + _pair_sems(ns),
        compiler_params=_cp(("arbitrary",), 48),
    )(*_hbm(dx1, y, cat, w_out), modr, g_post, *[_hbm(g) for g in swap])
    return outs[:3], outs[3:]


def _mixers_bwd(p, dcat, ys, o, states, h1, wdw, vecs, lbl, pairs_b):
    T = p.shape[0]
    tm = min(MIXB_TILE, T)
    nt = T // tm
    nch = tm // CHUNK
    hpt = tm // HALO
    nx = len(pairs_b)
    nb = IN_COLS // N_CHIPS

    def body(*refs):
        (p_ref, ph_ref, dcat_ref, ys_ref, o_ref, st_ref, h1_ref, wdw_ref, bdw_ref, gain_ref, bias_ref, gout_ref,
         lbl_ref) = refs[:13]
        x_ins = refs[13:13 + nx]
        dp_ref, sb_ref, s5_ref, dw_ref = refs[13 + nx:17 + nx]
        x_outs = refs[17 + nx:17 + 2 * nx]
        gin_ref, ginb_ref = refs[17 + 2 * nx], refs[18 + 2 * nx]
        (ubuf, dybuf, carry, dstate, qt_s, kt_s, kh_s, v_s, do_s, egl_s, dqt_s, dkt_s, dkh_s, dv_s, dgl_s,
         dsh, dw8, dshift, lower_s, upper_s, same_s, gmat_s, gacc, gacc_b, dp_prev, h1_prev, gsem, gbsem, xssem,
         xrsem) = refs[19 + 2 * nx:]
        i = pl.program_id(0)
        tile_idx = nt - 1 - i

        @pl.when(i == 0)
        def _():
            for cp in _xchg_copies(x_ins, x_outs, xssem, xrsem):
                cp.start()
            gacc[...] = jnp.zeros(gacc.shape, F32)
            dstate[...] = jnp.zeros(dstate.shape, F32)
            carry[...] = jnp.zeros(carry.shape, F32)
            sb_ref[...] = jnp.zeros(sb_ref.shape, F32)
            s5_ref[...] = jnp.zeros(s5_ref.shape, F32)
            dw_ref[...] = jnp.zeros(dw_ref.shape, F32)
            dw8[...] = jnp.zeros(dw8.shape, F32)
            lower_s[...], upper_s[...], same_s[...] = _chunk_masks(tm)
            gmat_s[...] = _gn_matrix()
            dsh[0:SUB, :] = jnp.zeros((SUB, CONV_CH), F32)
            dsh[SUB + tm:2 * SUB + tm, :] = jnp.zeros((SUB, CONV_CH), F32)
            ubuf[HALO + tm:HALO + tm + SUB, :] = jnp.zeros((SUB, CONV_CH), F32)
            dp_prev[...] = jnp.zeros(dp_prev.shape, BF16)
            h1_prev[...] = jnp.zeros(h1_prev.shape, BF16)

        n_pieces = (tm // CONV_ROWS) * (CONV_CH // LANE)
        per_block = n_pieces // N_CHIPS
        prow = D_MODEL // per_block

        def w_in_grad_piece(k):
            j, part = k // per_block, k % per_block
            rows_k = slice(part * prow, (part + 1) * prow)
            gacc[j, rows_k, :] += _dot_tn(h1_prev[:, rows_k], dp_prev[:, j * nb:(j + 1) * nb])

        uh = ph_ref[:, 0:CONV_CH] * _sig(ph_ref[:, CONV_CH:2 * CONV_CH])
        ubuf[0:HALO, :] = jnp.where(tile_idx > 0, uh, 0.0)
        ubuf[HALO:HALO + tm, :] = p_ref[:, 0:CONV_CH] * _sig(p_ref[:, CONV_CH:2 * CONV_CH])
        gmat = gmat_s[...]
        gain = gain_ref[...]
        yv = ys_ref[...]
        d = yv - _gmean(yv, gmat)
        rs = lax.rsqrt(_gmean(d * d, gmat) + GN_EPS)
        yn = d * rs
        z = yn * gain + bias_ref[...]
        sz = _sig(z)
        dz = dcat_ref[:, 0:CONV_CH] * (sz * (1.0 + z * (1.0 - sz)))
        dyn = dz * gain
        dyc = rs * (dyn - _gmean(dyn, gmat) - yn * _gmean(dyn * yn, gmat))
        s5_ref[0:1, :] += _colsum(dyc)
        s5_ref[1:2, :] += _colsum(dz * yn)
        s5_ref[2:3, :] += _colsum(dz)
        dybuf[tm:tm + HALO, :] = carry[...]
        dybuf[0:tm, :] = dyc
        dsh[SUB:SUB + tm, :] = dyc
        carry[...] = dyc[0:HALO, :]
        for b in range(SUB):
            dshift[...] = dsh[SUB - b:2 * SUB - b + tm, :]
            for j, off in CONV_FWD_TAPS:
                if off % SUB == b:
                    prod = dshift[...] * ubuf[off - b:off - b + tm + SUB, :]
                    dw8[j] += jnp.sum(prod.reshape((tm + SUB) // SUB, SUB, CONV_CH), axis=0)
        for r in range(tm // CONV_ROWS):
            rows = slice(r * CONV_ROWS, (r + 1) * CONV_ROWS)
            for lb_ in range(CONV_CH // LANE):
                lanes = slice(lb_ * LANE, (lb_ + 1) * LANE)
                glanes = slice(CONV_CH + lb_ * LANE, CONV_CH + (lb_ + 1) * LANE)
                w_in_grad_piece(r * (CONV_CH // LANE) + lb_)
                acc = _tap_conv(dybuf, wdw_ref, r * CONV_ROWS, CONV_BWD_TAPS, lanes)
                val = p_ref[rows, lanes]
                sg = _sig(p_ref[rows, glanes])
                dval = acc * sg
                dgate = acc * val * (sg * (1.0 - sg))
                dp_ref[rows, lanes] = dval.astype(BF16)
                dp_ref[rows, glanes] = dgate.astype(BF16)
                sb_ref[0:1, lanes] += _colsum(dval)
                sb_ref[0:1, glanes] += _colsum(dgate)

        o0 = 2 * CONV_CH
        for h in range(N_HEADS):
            sl = slice(h * HEAD_D, (h + 1) * HEAD_D)
            gsl = slice(o0 + 3 * HGRN_W + h * HEAD_D, o0 + 3 * HGRN_W + (h + 1) * HEAD_D)
            oh = o_ref[:, sl]
            gh = p_ref[:, gsl]
            dh = dcat_ref[:, CONV_CH + h * HEAD_D:CONV_CH + (h + 1) * HEAD_D]
            gout = gout_ref[:, sl]
            rsh = lax.rsqrt(jnp.mean(oh * oh, axis=-1, keepdims=True) + RMS_EPS)
            on = oh * rsh
            sgg = _sig(gh)
            dgh = dh * (on * gout) * (sgg * (1.0 + gh * (1.0 - sgg)))
            dm = dh * (gh * sgg)
            s5_ref[3:4, sl] += _colsum(dm * on)
            do_s[:, sl] = _rms_bwd(dm * gout, on, rsh).astype(BF16)
            dp_ref[:, gsl] = dgh.astype(BF16)
            sb_ref[2:3, CONV_CH + h * HEAD_D:CONV_CH + (h + 1) * HEAD_D] += _colsum(dgh)

        lb, _ = _lower_bound(lbl_ref)
        pq = p_ref[:, o0:o0 + HGRN_W]
        pr = _hgrn_prep(pq, p_ref[:, o0 + HGRN_W:o0 + 2 * HGRN_W], lb, lower_s[...])
        qt_s[...] = pr["qt"].astype(BF16)
        kt_s[...] = pr["kt"].astype(BF16)
        kh_s[...] = pr["kh"].astype(BF16)
        v_s[...] = p_ref[:, o0 + 2 * HGRN_W:o0 + 3 * HGRN_W].astype(BF16)
        egl_s[...] = jnp.exp(pr["Gl"])
        tri = _tri()

        def chunk(it, c_):
            ci = nch - 1 - it
            r0 = pl.multiple_of(ci * CHUNK, CHUNK)
            rows = pl.ds(r0, CHUNK)
            for h in range(N_HEADS):
                ls = pl.ds(h * HEAD_D, HEAD_D)
                qc, kc, hc, vc = qt_s[rows, ls], kt_s[rows, ls], kh_s[rows, ls], v_s[rows, ls]
                dob = do_s[rows, ls]
                s0 = st_ref[ci, h]
                s0b = s0.astype(BF16)
                ds1 = dstate[h]
                ds1b = ds1.astype(BF16)
                egl = egl_s[pl.ds(r0, 1), ls]
                att = jnp.where(tri, _dot_nt(qc, kc), 0.0).astype(BF16)
                datt = jnp.where(tri, _dot_nt(dob, vc), 0.0).astype(BF16)
                dv_s[rows, ls] = _dot_tn(att, dob) + _dot_nt(hc, ds1b)
                dqt_s[rows, ls] = _dot(datt, kc) + _dot(dob, s0b)
                dkt_s[rows, ls] = _dot_tn(datt, qc)
                dkh_s[rows, ls] = _dot(vc, ds1b)
                dgl = egl * _colsum(ds1 * s0)
                dgl_s[rows, ls] = jnp.broadcast_to(dgl, (CHUNK, HEAD_D))
                dstate[h] = ds1 * egl + _dot_tn(dob, qc)
            return c_

        lax.fori_loop(0, nch, chunk, 0, unroll=min(CHUNK_UNROLL, nch))
        dqt, dkt, dkh = dqt_s[...], dkt_s[...], dkh_s[...]
        dk = dkt * pr["enG"] + dkh * pr["eGlG"]
        khk = dkh * kh_s[...].astype(F32)
        dG = dqt * qt_s[...].astype(F32) - dkt * kt_s[...].astype(F32) - khk
        dlogf = _mm3(upper_s[...], dG) + _mm3(same_s[...], khk) + dgl_s[...]
        df = dlogf / pr["f"] - dk
        sf, sq = pr["sf"], pr["sq"]
        s5_ref[4:5, :] += _colsum(df * (1.0 - sf))
        dfl = df * (1.0 - lb) * (sf * (1.0 - sf))
        dq = (dqt * pr["eG"]) * (sq * (1.0 + pq * (1.0 - sq)))
        dvv = dv_s[...]
        dp_ref[:, o0:o0 + HGRN_W] = dq.astype(BF16)
        dp_ref[:, o0 + HGRN_W:o0 + 2 * HGRN_W] = dfl.astype(BF16)
        dp_ref[:, o0 + 2 * HGRN_W:o0 + 3 * HGRN_W] = dvv.astype(BF16)
        sb_ref[1:2, 0:HGRN_W] += _colsum(dq)
        sb_ref[1:2, HGRN_W:2 * HGRN_W] += _colsum(dfl)
        sb_ref[2:3, 0:HGRN_W] += _colsum(dvv)

        dp_prev[...] = dp_ref[...]
        h1_prev[...] = h1_ref[...]

        @pl.when(i == nt - 1)
        def _():
            for k in range(n_pieces):
                w_in_grad_piece(k)
            out = pltpu.make_async_copy(gacc, gin_ref, gsem)
            out.start()
            for j in range(N_CHIPS):
                gacc_b[j] = gacc[j].astype(BF16)
            out_b = pltpu.make_async_copy(gacc_b, ginb_ref, gbsem)
            out_b.start()
            for j in range(CONV_K):
                dw_ref[j:j + 1, :] = _colsum(dw8[j])
            copies = _xchg_copies(x_ins, x_outs, xssem, xrsem)
            for cp in copies:
                cp.wait_recv()
            for cp in copies:
                cp.wait_send()
            out.wait()
            out_b.wait()

    rev = lambda cols: pl.BlockSpec((tm, cols), lambda i: (nt - 1 - i, 0))
    halo = pl.BlockSpec((HALO, 2 * CONV_CH), lambda i: (jnp.maximum((nt - 1 - i) * hpt - 1, 0), 0))
    wide = lambda n: pltpu.VMEM((tm, HGRN_W), n)
    hbm = pl.BlockSpec(memory_space=pl.ANY)
    outs = pl.pallas_call(
        body, name="mixers_bwd", grid=(nt,),
        in_specs=[rev(IN_COLS), halo, rev(D_MODEL), rev(CONV_CH), rev(HGRN_W),
                  pl.BlockSpec((nch, N_HEADS, HEAD_D, HEAD_D), lambda i: (nt - 1 - i, 0, 0, 0)), rev(D_MODEL),
                  _full((HALO, CONV_CH))] + [_full((1, CONV_CH))] * 4 + [_full((2, HGRN_W))] + [hbm] * nx,
        out_specs=[rev(IN_COLS), _full((8, D_MODEL)), _full((8, CONV_CH)), _full((HALO, CONV_CH))]
        + [hbm] * (nx + 2),
        out_shape=[_big((T, IN_COLS), BF16), jax.ShapeDtypeStruct((8, D_MODEL), F32),
                   jax.ShapeDtypeStruct((8, CONV_CH), F32), jax.ShapeDtypeStruct((HALO, CONV_CH), F32)]
        + [_big(pb.shape, BF16) for pb in pairs_b]
        + [_big((N_CHIPS, D_MODEL, nb), F32), _big((N_CHIPS, D_MODEL, nb), BF16)],
        scratch_shapes=[pltpu.VMEM((tm + HALO + SUB, CONV_CH), F32), pltpu.VMEM((tm + HALO, CONV_CH), F32),
                        pltpu.VMEM((HALO, CONV_CH), F32), pltpu.VMEM((N_HEADS, HEAD_D, HEAD_D), F32),
                        wide(BF16), wide(BF16), wide(BF16), wide(BF16), wide(BF16),
                        wide(F32), wide(F32), wide(F32), wide(F32), wide(F32), wide(F32),
                        pltpu.VMEM((tm + 2 * SUB, CONV_CH), F32), pltpu.VMEM((HALO, SUB, CONV_CH), F32),
                        pltpu.VMEM((tm + SUB, CONV_CH), F32), pltpu.VMEM((tm, tm), BF16), pltpu.VMEM((tm, tm), BF16),
                        pltpu.VMEM((tm, tm), BF16), pltpu.VMEM((CONV_CH, CONV_CH), BF16),
                        pltpu.VMEM((N_CHIPS, D_MODEL, nb), F32), pltpu.VMEM((N_CHIPS, D_MODEL, nb), BF16),
                        pltpu.VMEM((tm, IN_COLS), BF16), pltpu.VMEM((tm, D_MODEL), BF16),
                        pltpu.SemaphoreType.DMA, pltpu.SemaphoreType.DMA]
        + _xchg_sems(nx),
        compiler_params=_cp(("arbitrary",), 56),
    )(*_hbm(p, p, dcat, ys, o, states, h1), wdw, *vecs, lbl, *[_hbm(pb) for pb in pairs_b])
    return outs[:4], outs[4:4 + nx], (outs[4 + nx], outs[5 + nx])


def _mix_in_bwd(dp, w_in_g, x, dx1, modr, g_pre, pairs_b):
    T = x.shape[0]
    tm = _tok_tile(T)
    nt = T // tm
    nb = IN_COLS // N_CHIPS
    nx = len(pairs_b)

    def body(*refs):
        dp_ref, w_ref, x_ref, dx1_ref, mod_ref, g_ref = refs[:6]
        x_ins = refs[6:6 + nx]
        gx_ref, st_ref = refs[6 + nx:8 + nx]
        x_outs = refs[8 + nx:8 + 2 * nx]
        xssem, xrsem = refs[8 + 2 * nx:]
        i = pl.program_id(0)

        @pl.when(i == 0)
        def _():
            for cp in _xchg_copies(x_ins, x_outs, xssem, xrsem):
                cp.start()
            st_ref[...] = jnp.zeros(st_ref.shape, F32)

        dh = None
        for j in range(N_CHIPS):
            part = _dot_nt(dp_ref[:, j * nb:(j + 1) * nb], w_ref[j])
            dh = part if dh is None else dh + part
        xv = x_ref[...]
        rs = lax.rsqrt(jnp.mean(xv * xv, axis=-1, keepdims=True) + RMS_EPS)
        xn = xv * rs
        st_ref[0:1, :] += _colsum(dh)
        st_ref[1:2, :] += _colsum(dh * (xn * g_ref[...]))
        dsc = dh * (1.0 + mod_ref[1:2, :])
        st_ref[2:3, :] += _colsum(dsc * xn)
        gx_ref[...] = dx1_ref[...] + _rms_bwd(dsc * g_ref[...], xn, rs)

        @pl.when(i == nt - 1)
        def _():
            copies = _xchg_copies(x_ins, x_outs, xssem, xrsem)
            for cp in copies:
                cp.wait_recv()
            for cp in copies:
                cp.wait_send()

    tile = pl.BlockSpec((tm, D_MODEL), lambda i: (i, 0))
    hbm = pl.BlockSpec(memory_space=pl.ANY)
    outs = pl.pallas_call(
        body, name="mix_in_bwd", grid=(nt,),
        in_specs=[pl.BlockSpec((tm, IN_COLS), lambda i: (i, 0)), _full((N_CHIPS, D_MODEL, nb)), tile, tile,
                  _full((6, D_MODEL)), _full((1, D_MODEL))] + [hbm] * nx,
        out_specs=[tile, _full((8, D_MODEL))] + [hbm] * nx,
        out_shape=[_big((T, D_MODEL), F32), jax.ShapeDtypeStruct((8, D_MODEL), F32)]
        + [_big(pb.shape, BF16) for pb in pairs_b],
        scratch_shapes=_xchg_sems(nx),
        compiler_params=_cp(("arbitrary",), 48),
    )(*_hbm(dp, w_in_g, x, dx1), modr, g_pre, *[_hbm(pb) for pb in pairs_b])
    return outs[:2], outs[2:]


def _weight_grad(a, b, a_blocked, b_blocked, name):
    T = a.shape[0]
    tt = min(GRAD_TILE, T)
    nt = T // tt
    ka = a.shape[1] // N_CHIPS if a_blocked else a.shape[1]
    nb = b.shape[1] // N_CHIPS if b_blocked else b.shape[1]

    def body(a_ref, b_ref, o_ref, ob_ref):
        t = pl.program_id(1)

        @pl.when(t == 0)
        def _():
            o_ref[...] = jnp.zeros(o_ref.shape, F32)

        for cols in _row_chains(nb):
            o_ref[0, :, cols] += _dot_tn(a_ref[...], b_ref[:, cols])

        @pl.when(t == nt - 1)
        def _():
            ob_ref[0] = o_ref[0].astype(BF16)

    blk = pl.BlockSpec((1, ka, nb), lambda j, t: (j, 0, 0))
    return pl.pallas_call(
        body, name=name, grid=(N_CHIPS, nt),
        in_specs=[pl.BlockSpec((tt, ka), (lambda j, t: (t, j)) if a_blocked else (lambda j, t: (t, 0))),
                  pl.BlockSpec((tt, nb), (lambda j, t: (t, j)) if b_blocked else (lambda j, t: (t, 0)))],
        out_specs=[blk, blk],
        out_shape=[_big((N_CHIPS, ka, nb), F32), _big((N_CHIPS, ka, nb), BF16)],
        compiler_params=_cp(("arbitrary", "arbitrary"), 48),
    )(*_hbm(a, b))


R_LOSS = 0
R_FFN = 8
R_OUT = 16
R_IN = 24
R_BIN = 32
R_512 = 40
R_DW = 48
N_STAT_ROWS = 80
MOD_ROWS = (R_IN + 0, R_IN + 1, R_OUT + 0, R_FFN + 0, R_FFN + 1, R_LOSS + 1)


def _small_update(gath, params):
    names = ["b_ada", "lb_logits", "g_pre_mix", "b_in", "b_dw", "gn_gain", "gn_bias", "g_hgrn_out", "g_post_mix",
             "g_pre_ffn", "g_post_ffn"]
    flat = []
    for n in names:
        flat += list(params[n])
    n_in = 1 + len(flat)

    def body(*refs):
        g_ref = refs[0]
        prm = {n: refs[1 + 3 * k:4 + 3 * k] for k, n in enumerate(names)}
        outs = refs[n_in:]
        loss_ref, dmod_ref, dwdw_ref = outs[0], outs[1], outs[2]
        res = {n: outs[3 + 4 * k:7 + 4 * k] for k, n in enumerate(names)}
        red = g_ref[0]
        for dev in range(1, N_DEV):
            red = red + g_ref[dev]
        loss_ref[...] = jnp.broadcast_to(
            (0.5 / D_MODEL) * jnp.sum(red[R_LOSS:R_LOSS + 1, :], axis=-1, keepdims=True), loss_ref.shape)
        for dev in range(N_DEV):
            for k, r in enumerate(MOD_ROWS):
                dmod_ref[dev:dev + 1, k * D_MODEL:(k + 1) * D_MODEL] = g_ref[dev, r:r + 1, :]
        dwdw_ref[...] = red[R_DW:R_DW + HALO, 0:CONV_CH]

        def finish(name, pieces):
            w_ref, m_ref, v_ref = prm[name]
            g_out, d_out, m_out, v_out = res[name]
            for rsl, lsl, g in pieces:
                d, m2, v2 = _adam_math(w_ref[rsl, lsl], g, m_ref[rsl, lsl], v_ref[rsl, lsl])
                g_out[rsl, lsl] = g
                d_out[rsl, lsl] = d
                m_out[rsl, lsl] = m2
                v_out[rsl, lsl] = v2

        one = slice(0, 1)
        row = lambda r: red[r:r + 1, :]
        half = lambda r: red[r:r + 1, 0:CONV_CH]
        finish("b_ada", [(one, slice(k * D_MODEL, (k + 1) * D_MODEL), row(r)) for k, r in enumerate(MOD_ROWS)])
        finish("b_in", [(one, slice(k * D_MODEL, (k + 1) * D_MODEL), row(R_BIN + k)) for k in range(3)])
        finish("g_pre_mix", [(one, slice(None), row(R_IN + 2))])
        finish("g_post_mix", [(one, slice(None), row(R_OUT + 1))])
        finish("g_pre_ffn", [(one, slice(None), row(R_FFN + 2))])
        finish("g_post_ffn", [(one, slice(None), row(R_LOSS + 2))])
        finish("b_dw", [(one, slice(None), half(R_512 + 0))])
        finish("gn_gain", [(one, slice(None), half(R_512 + 1))])
        finish("gn_bias", [(one, slice(None), half(R_512 + 2))])
        finish("g_hgrn_out", [(one, slice(None), half(R_512 + 3))])
        s0, s1 = _lower_bound(prm["lb_logits"][0])
        dlb = half(R_512 + 4)
        finish("lb_logits", [(slice(0, 1), slice(None), dlb * s0 * (1.0 - s0)),
                             (slice(1, 2), slice(None), -dlb * s0 * s1)])

    vm = pl.BlockSpec(memory_space=pltpu.VMEM)
    out_shape = [jax.ShapeDtypeStruct((8, 128), F32), jax.ShapeDtypeStruct((N_DEV, 6 * D_MODEL), F32),
                 jax.ShapeDtypeStruct((HALO, CONV_CH), F32)]
    for n in names:
        out_shape += [jax.ShapeDtypeStruct(params[n][0].shape, F32)] * 4
    outs = pl.pallas_call(
        body, name="small_update", out_shape=out_shape,
        in_specs=[vm] * n_in, out_specs=[vm] * len(out_shape),
        compiler_params=_cp(None, 32),
    )(gath, *flat)
    return outs[0], outs[1], outs[2], {n: outs[3 + 4 * k:7 + 4 * k] for k, n in enumerate(names)}


def _wdw_adam(w, g, m, v):
    def body(w_ref, g_ref, m_ref, v_ref, d_out, m_out, v_out):
        d, m2, v2 = _adam_math(w_ref[...], g_ref[...], m_ref[...], v_ref[...])
        d_out[...] = d
        m_out[...] = m2
        v_out[...] = v2

    vm = pl.BlockSpec(memory_space=pltpu.VMEM)
    return pl.pallas_call(
        body, name="wdw_adam", out_shape=[jax.ShapeDtypeStruct(w.shape, F32)] * 3,
        in_specs=[vm] * 4, out_specs=[vm] * 3, compiler_params=_cp(None, 16),
    )(w, g, m, v)


def kernel(x, c, w_ada, b_ada, lb_logits, g_pre_mix, w_in, b_in, w_dw, b_dw, gn_gain, gn_bias, g_hgrn_out, w_out, g_post_mix, g_pre_ffn, w_up, w_down, g_post_ffn, loss_target, m_w_ada, m_b_ada, m_lb_logits, m_g_pre_mix, m_w_in, m_b_in, m_w_dw, m_b_dw, m_gn_gain, m_gn_bias, m_g_hgrn_out, m_w_out, m_g_post_mix, m_g_pre_ffn, m_w_up, m_w_down, m_g_post_ffn, v_w_ada, v_b_ada, v_lb_logits, v_g_pre_mix, v_w_in, v_b_in, v_w_dw, v_b_dw, v_gn_gain, v_gn_bias, v_g_hgrn_out, v_w_out, v_g_post_mix, v_g_pre_ffn, v_w_up, v_w_down, v_g_post_ffn):
    ax, ay, ac = lax.axis_index("x"), lax.axis_index("y"), lax.axis_index("c")
    chip = 2 * ax + ay
    T = x.shape[1]
    xs, tgt = x[0], loss_target[0]
    ada_cols = w_ada.shape[2]

    b_sh = lax.dynamic_slice_in_dim(b_ada, chip * ada_cols, ada_cols, axis=1)
    wdw_pad = jnp.pad(w_dw[0], ((0, HALO - CONV_K), (0, 0)))
    chip1 = jnp.reshape(chip, (1,)).astype(jnp.int32)
    place = jnp.stack([ac, chip]).astype(jnp.int32)
    _, c8, modg, wdwg = _ada_exchange(c, w_ada[0], b_sh, wdw_pad)
    modr = modg.reshape(6, D_MODEL)
    wdw_all = jnp.transpose(wdwg, (1, 0, 2)).reshape(HALO, CONV_CH)
    vec = (b_dw, gn_gain, gn_bias, g_hgrn_out)

    p, h1, (up_buf, down_buf), w_in_g, w_out_g = _mix_in_fwd(
        chip1, xs, modr, g_pre_mix, b_in.reshape(N_CHIPS, 1, IN_COLS // N_CHIPS), [w_up[0], w_down[0]],
        _cast_own(chip1, w_in[0], "cast_w_in"), _cast_own(chip1, w_out[0], "cast_w_out"))
    w_out_f = w_out_g.reshape(D_MODEL, D_MODEL)
    (cat, ys, o, states, y, x1, h2), (w_up_g, w_down_g) = _mixers_fwd(
        p, wdw_all, vec, lb_logits, w_out_f, xs, modr, g_post_mix, g_pre_ffn, [up_buf, down_buf])
    w_down_f = w_down_g.reshape(D_FF, D_MODEL)
    r, dy2, dx2, st_loss, w_up_t, w_down_t = _ffn_fwd(h2, w_up_g, w_down_f, x1, tgt, modr, g_post_ffn)

    def pair_sums(grads, got, tags):
        return [_pair_sum(place, g, o_, "pair_sum_" + t) for (g, _), o_, t in zip(grads, got, tags)]

    da, dx1, st_ffn = _ffn_bwd(dy2, r, x1, dx2, w_up_t, w_down_t, modr, g_pre_ffn)
    g_up = _weight_grad(h2, da, False, True, "grad_w_up")
    g_down = _weight_grad(r, dy2, True, False, "grad_w_down")
    (dcat, st_out, g_out), got_ud = _mix_out_bwd(dx1, y, cat, w_out_f, modr, g_post_mix, [g_up[1], g_down[1]])
    g_out = g_out.reshape(N_CHIPS, D_MODEL // N_CHIPS, D_MODEL)
    got_o = _pair_swap([g_out], "pair_swap_w_out")
    early = pair_sums([(g_out, None), g_up, g_down], list(got_o) + list(got_ud), ["w_out", "w_up", "w_down"])
    (dp, st_bin, st_512, dwdw), got_early, g_in = _mixers_bwd(p, dcat, ys, o, states, h1, wdw_all, vec, lb_logits,
                                                              [pb for _, pb in early])
    late = pair_sums([g_in], _pair_swap([g_in[1]], "pair_swap_w_in"), ["w_in"])
    (grad_x, st_in), got_late = _mix_in_bwd(dp, w_in_g, xs, dx1, modr, g_pre_mix, [late[0][1]])
    fulls = [_chip_sum(place, pf, gb, "chip_sum_" + t)
             for (pf, _), gb, t in zip(late + early, list(got_late) + list(got_early), ["w_in", "w_out", "w_up", "w_down"])]

    pad_lanes = lambda s: jnp.pad(s, ((0, 0), (0, D_MODEL - s.shape[1])))
    stats = jnp.concatenate([st_loss, st_ffn, st_out, st_in, st_bin, pad_lanes(st_512), pad_lanes(dwdw)], axis=0)
    (g_w_in, g_w_out, g_w_up, g_w_down), gath = _final_exchange(fulls, stats)
    small = {"b_ada": (b_ada, m_b_ada, v_b_ada), "lb_logits": (lb_logits, m_lb_logits, v_lb_logits),
             "g_pre_mix": (g_pre_mix, m_g_pre_mix, v_g_pre_mix), "b_in": (b_in, m_b_in, v_b_in),
             "b_dw": (b_dw, m_b_dw, v_b_dw), "gn_gain": (gn_gain, m_gn_gain, v_gn_gain),
             "gn_bias": (gn_bias, m_gn_bias, v_gn_bias), "g_hgrn_out": (g_hgrn_out, m_g_hgrn_out, v_g_hgrn_out),
             "g_post_mix": (g_post_mix, m_g_post_mix, v_g_post_mix), "g_pre_ffn": (g_pre_ffn, m_g_pre_ffn, v_g_pre_ffn),
             "g_post_ffn": (g_post_ffn, m_g_post_ffn, v_g_post_ffn)}
    loss_t, dmod_all, dwdw_sum, sres = _small_update(gath, small)
    loss = loss_t[0, 0]

    res = dict(sres)
    dmod_sh = lax.dynamic_slice_in_dim(dmod_all, chip * ada_cols, ada_cols, axis=1)
    res["w_ada"] = [t[None] for t in _ada_grad_adam(jnp.transpose(c8), dmod_sh, w_ada[0], m_w_ada[0], v_w_ada[0])]
    g_wdw = lax.dynamic_slice_in_dim(dwdw_sum, chip * HEAD_D, HEAD_D, axis=1)[:CONV_K][None]
    res["w_dw"] = [g_wdw] + list(_wdw_adam(w_dw, g_wdw, m_w_dw, v_w_dw))
    for name, g, w, m, v in (("w_in", g_w_in, w_in, m_w_in, v_w_in), ("w_out", g_w_out, w_out, m_w_out, v_w_out),
                             ("w_up", g_w_up, w_up, m_w_up, v_w_up), ("w_down", g_w_down, w_down, m_w_down, v_w_down)):
        d, m2, v2 = _adam_big(w[0], g, m[0], v[0], "adam_" + name)
        res[name] = [g[None], d[None], m2[None], v2[None]]

    order = ["w_ada", "b_ada", "lb_logits", "g_pre_mix", "w_in", "b_in", "w_dw", "b_dw", "gn_gain", "gn_bias",
             "g_hgrn_out", "w_out", "g_post_mix", "g_pre_ffn", "w_up", "w_down", "g_post_ffn"]
    out = [loss, grad_x[None]]
    for k in range(4):
        out += [res[n][k] for n in order]
    return tuple(out)
```

```python
import jax
import jax.numpy as jnp
from jax import lax
from jax.experimental import pallas as pl
from jax.experimental.pallas import tpu as pltpu

F32, BF16 = jnp.float32, jnp.bfloat16
D_MODEL = 1024
CONV_CH = 512
HGRN_W = 512
N_HEADS = 4
HEAD_D = 128
CONV_K = 31
GN_GROUP = 64
GN_SHIFT = 6
IN_COLS = 3072
D_FF = 4096
CHUNK = 64
CHUNK_SHIFT = 6
N_CHIPS = 4
N_DEV = 8
RMS_EPS = 1e-6
GN_EPS = 1e-5
ADAM_LR, ADAM_B1, ADAM_B2, ADAM_EPS, ADAM_WD, ADAM_STEP = 0.001, 0.9, 0.999, 1e-08, 0.01, 10
TOK_TILE = 512
MIXIN_TILE = 1024
MIXB_TILE = 256
FFN_TILE = 1024
FFN_BLOCK = 512
GRAD_TILE = 4096
HALO = 32
SUB = 8
LANE = 128
CONV_ROWS = 128
CHUNK_UNROLL = 8
MIB = 1 << 20
MESH = pl.DeviceIdType.MESH
OTHER_CHIPS = ((0, 1), (1, 0), (1, 1))


def _cp(sem=None, vmem_mib=48):
    return pltpu.CompilerParams(dimension_semantics=sem, vmem_limit_bytes=vmem_mib * MIB)


def _dot(a, b):
    return jnp.dot(a, b, preferred_element_type=F32)


def _dot_nt(a, b):
    return lax.dot_general(a, b, (((1,), (1,)), ((), ())), preferred_element_type=F32)


def _dot_tn(a, b):
    return lax.dot_general(a, b, (((0,), (0,)), ((), ())), preferred_element_type=F32)


def _sig(v):
    return 0.5 * jnp.tanh(0.5 * v) + 0.5


def _colsum(v):
    return jnp.sum(v, axis=0, keepdims=True)


def _flip(v, b):
    return 1 - v if b else v


def _rcopy(src, dst, ssem, rsem, dev):
    return pltpu.make_async_remote_copy(src_ref=src, dst_ref=dst, send_sem=ssem, recv_sem=rsem,
                                        device_id=dev, device_id_type=MESH)


def _place():
    return lax.axis_index("x"), lax.axis_index("y"), lax.axis_index("c")


def _full(shape):
    return pl.BlockSpec(shape, lambda *_: (0,) * len(shape))


def _big(shape, dtype):
    return pltpu.HBM(shape, dtype)


def _hbm(*arrays):
    out = [pltpu.with_memory_space_constraint(a, pltpu.HBM) for a in arrays]
    return out[0] if len(out) == 1 else out


def _split2(v):
    hi = v.astype(BF16)
    lo = (v - hi.astype(F32)).astype(BF16)
    return hi, lo


def _split3(v):
    h1 = v.astype(BF16)
    r1 = v - h1.astype(F32)
    h2 = r1.astype(BF16)
    h3 = (r1 - h2.astype(F32)).astype(BF16)
    return h1, h2, h3


def _mm3(mat, v):
    h1, h2, h3 = _split3(v)
    return _dot(mat, h1) + _dot(mat, h2) + _dot(mat, h3)


def _gn_matrix():
    r = lax.broadcasted_iota(jnp.int32, (CONV_CH, CONV_CH), 0) >> GN_SHIFT
    c = lax.broadcasted_iota(jnp.int32, (CONV_CH, CONV_CH), 1) >> GN_SHIFT
    return jnp.where(r == c, 1.0 / GN_GROUP, 0.0).astype(BF16)


def _gmean(v, gmat):
    hi, lo = _split2(v)
    return _dot(hi, gmat) + _dot(lo, gmat)


def _chunk_masks(tm):
    r = lax.broadcasted_iota(jnp.int32, (tm, tm), 0)
    c = lax.broadcasted_iota(jnp.int32, (tm, tm), 1)
    same = (r >> CHUNK_SHIFT) == (c >> CHUNK_SHIFT)
    one = lambda m: jnp.where(m, 1.0, 0.0).astype(BF16)
    return one(same & (c <= r)), one(same & (c >= r)), one(same)


def _tri():
    return lax.broadcasted_iota(jnp.int32, (CHUNK, CHUNK), 0) >= lax.broadcasted_iota(jnp.int32, (CHUNK, CHUNK), 1)


def _lower_bound(lbl_ref):
    l0, l1 = lbl_ref[0:1, :], lbl_ref[1:2, :]
    mx = jnp.maximum(l0, l1)
    e0, e1 = jnp.exp(l0 - mx), jnp.exp(l1 - mx)
    return e0 / (e0 + e1), e1 / (e0 + e1)


CONV_FWD_TAPS = tuple((j, HALO - (CONV_K - 1) + j) for j in range(CONV_K))
CONV_BWD_TAPS = tuple((j, (CONV_K - 1) - j) for j in range(CONV_K))


def _tap_conv(src_ref, w_ref, row0, taps, lanes):
    acc = None
    for b in range(SUB):
        pb = None
        for j, off in taps:
            if off % SUB == b:
                lo = row0 + off - b
                term = w_ref[j:j + 1, lanes] * src_ref[lo:lo + CONV_ROWS + SUB, lanes]
                pb = term if pb is None else pb + term
        if pb is not None:
            sh = pb[b:b + CONV_ROWS, :]
            acc = sh if acc is None else acc + sh
    return acc


def _hgrn_prep(pq, pf, lb, lower):
    sq = _sig(pq)
    qf = pq * sq
    sf = _sig(pf)
    f = lb + (1.0 - lb) * sf
    logf = jnp.log(f)
    k = 1.0 - f
    G = _mm3(lower, logf)
    rows, cols = G.shape
    g3 = G.reshape(rows // CHUNK, CHUNK, cols)
    Gl = jnp.broadcast_to(g3[:, CHUNK - 1:CHUNK, :], g3.shape).reshape(rows, cols)
    eG, enG, eGlG = jnp.exp(G), jnp.exp(-G), jnp.exp(Gl - G)
    return dict(sq=sq, sf=sf, f=f, Gl=Gl, eG=eG, enG=enG, eGlG=eGlG, qt=qf * eG, kt=k * enG, kh=k * eGlG)


def _ada_exchange(c_row, w_ada, b_sh, wdw_pad):
    ncol = w_ada.shape[1]

    def body(c_ref, w_ref, b_ref, wdw_ref, call_ref, c8_ref, modg_ref, wdwg_ref, rows_s, sa, ra, sw, rw, sm, rm):
        x, y, c = _place()
        me = 4 * x + 2 * y + c
        chip = 2 * x + y
        cv = c_ref[...]
        call_ref[me] = cv * _sig(cv)
        wdwg_ref[chip] = wdw_ref[...]
        sends = []
        for m in range(1, N_DEV):
            peer = (_flip(x, m >> 2), _flip(y, (m >> 1) & 1), _flip(c, m & 1))
            cp = _rcopy(call_ref.at[me], call_ref.at[me], sa.at[m - 1], ra.at[m - 1], peer)
            cp.start()
            sends.append(cp)
        for k, (fx, fy) in enumerate(OTHER_CHIPS):
            peer = (_flip(x, fx), _flip(y, fy), c)
            cp = _rcopy(wdwg_ref.at[chip], wdwg_ref.at[chip], sw.at[k], rw.at[k], peer)
            cp.start()
            sends.append(cp)
        for m in range(1, N_DEV):
            peer = (_flip(x, m >> 2), _flip(y, (m >> 1) & 1), _flip(c, m & 1))
            pid = 4 * peer[0] + 2 * peer[1] + peer[2]
            _rcopy(call_ref.at[pid], call_ref.at[pid], sa.at[m - 1], ra.at[m - 1], peer).wait_recv()
        for b in range(N_DEV):
            c8_ref[b:b + 1, :] = call_ref[b]
        mod_all = _dot(c8_ref[...].astype(BF16), w_ref[...].astype(BF16)) + b_ref[...]
        for b in range(N_DEV):
            rows_s[b] = mod_all[b:b + 1, :]
        modg_ref[chip] = rows_s[me]
        for k, (fx, fy) in enumerate(OTHER_CHIPS):
            peer = (_flip(x, fx), _flip(y, fy), c)
            pid = 4 * peer[0] + 2 * peer[1] + peer[2]
            cp = _rcopy(rows_s.at[pid], modg_ref.at[chip], sm.at[k], rm.at[k], peer)
            cp.start()
            sends.append(cp)
        for k, (fx, fy) in enumerate(OTHER_CHIPS):
            peer = (_flip(x, fx), _flip(y, fy), c)
            pchip = 2 * peer[0] + peer[1]
            _rcopy(rows_s.at[0], modg_ref.at[pchip], sm.at[k], rm.at[k], peer).wait_recv()
            _rcopy(wdwg_ref.at[pchip], wdwg_ref.at[pchip], sw.at[k], rw.at[k], peer).wait_recv()
        for cp in sends:
            cp.wait_send()

    vm = pl.BlockSpec(memory_space=pltpu.VMEM)
    return pl.pallas_call(
        body, name="ada_exchange",
        out_shape=[jax.ShapeDtypeStruct((N_DEV, 1, D_MODEL), F32), jax.ShapeDtypeStruct((N_DEV, D_MODEL), F32),
                   jax.ShapeDtypeStruct((N_CHIPS, 1, ncol), F32), jax.ShapeDtypeStruct((N_CHIPS, HALO, HEAD_D), F32)],
        in_specs=[vm] * 4, out_specs=[vm] * 4,
        scratch_shapes=[pltpu.VMEM((N_DEV, 1, ncol), F32),
                        pltpu.SemaphoreType.DMA((N_DEV - 1,)), pltpu.SemaphoreType.DMA((N_DEV - 1,)),
                        pltpu.SemaphoreType.DMA((3,)), pltpu.SemaphoreType.DMA((3,)),
                        pltpu.SemaphoreType.DMA((3,)), pltpu.SemaphoreType.DMA((3,))],
        compiler_params=_cp(None, 32),
    )(c_row, w_ada, b_sh, wdw_pad)


def _cast_own(chip1, shard, name):
    rows, cols = shard.shape
    tr = _row_tile(rows)

    def body(ch_ref, s_ref, o_ref):
        o_ref[0] = s_ref[...].astype(BF16)

    gs = pltpu.PrefetchScalarGridSpec(
        num_scalar_prefetch=1, grid=(rows // tr,),
        in_specs=[pl.BlockSpec((tr, cols), lambda i, ch: (i, 0))],
        out_specs=pl.BlockSpec((1, tr, cols), lambda i, ch: (ch[0], i, 0)))
    return pl.pallas_call(
        body, name=name, grid_spec=gs, out_shape=_big((N_CHIPS, rows, cols), BF16),
        compiler_params=_cp(("arbitrary",), 32),
    )(chip1, _hbm(shard))


def _slab(buf, ch, core):
    hs = buf.shape[1] // 2
    return buf.at[ch, pl.ds(core * hs, hs), :]


def _gather_start(bufs, ssem, rsem, relations=(0, 1, 2)):
    x, y, c = _place()
    chip = 2 * x + y
    for k in relations:
        fx, fy = OTHER_CHIPS[k]
        peer = (_flip(x, fx), _flip(y, fy), c)
        for t, buf in enumerate(bufs):
            _rcopy(_slab(buf, chip, c), _slab(buf, chip, c), ssem.at[t * 3 + k], rsem.at[t * 3 + k], peer).start()


def _gather_pass_on(bufs, ssem, rsem):
    nt = len(bufs)
    x, y, c = _place()
    sibling = (x, y, 1 - c)
    for k, (fx, fy) in enumerate(OTHER_CHIPS):
        peer = (_flip(x, fx), _flip(y, fy), c)
        pchip = 2 * peer[0] + peer[1]
        for t, buf in enumerate(bufs):
            _rcopy(_slab(buf, pchip, c), _slab(buf, pchip, c), ssem.at[t * 3 + k], rsem.at[t * 3 + k], peer).wait_recv()
            _rcopy(_slab(buf, pchip, c), _slab(buf, pchip, c), ssem.at[3 * nt + t * 3 + k],
                   rsem.at[3 * nt + t * 3 + k], sibling).start()


def _gather_drain(bufs, ssem, rsem):
    nt = len(bufs)
    x, y, c = _place()
    chip = 2 * x + y
    sibling = (x, y, 1 - c)
    for k, (fx, fy) in enumerate(OTHER_CHIPS):
        peer = (_flip(x, fx), _flip(y, fy), c)
        pchip = 2 * peer[0] + peer[1]
        for t, buf in enumerate(bufs):
            _rcopy(_slab(buf, pchip, 1 - c), _slab(buf, pchip, 1 - c), ssem.at[3 * nt + t * 3 + k],
                   rsem.at[3 * nt + t * 3 + k], sibling).wait_recv()
            _rcopy(_slab(buf, chip, c), _slab(buf, chip, c), ssem.at[t * 3 + k], rsem.at[t * 3 + k], peer).wait_send()
            _rcopy(_slab(buf, pchip, c), _slab(buf, pchip, c), ssem.at[3 * nt + t * 3 + k],
                   rsem.at[3 * nt + t * 3 + k], sibling).wait_send()


def _gather_finish(bufs, ssem, rsem):
    _gather_pass_on(bufs, ssem, rsem)
    _gather_drain(bufs, ssem, rsem)


def _gather_arrive(bufs, k, ssem, rsem):
    nt = len(bufs)
    x, y, c = _place()
    fx, fy = OTHER_CHIPS[k]
    peer = (_flip(x, fx), _flip(y, fy), c)
    pchip = 2 * peer[0] + peer[1]
    for t, buf in enumerate(bufs):
        _rcopy(_slab(buf, pchip, c), _slab(buf, pchip, c), ssem.at[t * 3 + k], rsem.at[t * 3 + k], peer).wait_recv()
        _rcopy(_slab(buf, pchip, c), _slab(buf, pchip, c), ssem.at[3 * nt + t * 3 + k],
               rsem.at[3 * nt + t * 3 + k], (x, y, 1 - c)).start()
    for t, buf in enumerate(bufs):
        _rcopy(_slab(buf, pchip, 1 - c), _slab(buf, pchip, 1 - c), ssem.at[3 * nt + t * 3 + k],
               rsem.at[3 * nt + t * 3 + k], (x, y, 1 - c)).wait_recv()


def _gather_sends_done(bufs, ssem, rsem):
    nt = len(bufs)
    x, y, c = _place()
    chip = 2 * x + y
    for k, (fx, fy) in enumerate(OTHER_CHIPS):
        peer = (_flip(x, fx), _flip(y, fy), c)
        pchip = 2 * peer[0] + peer[1]
        for t, buf in enumerate(bufs):
            _rcopy(_slab(buf, chip, c), _slab(buf, chip, c), ssem.at[t * 3 + k], rsem.at[t * 3 + k], peer).wait_send()
            _rcopy(_slab(buf, pchip, c), _slab(buf, pchip, c), ssem.at[3 * nt + t * 3 + k],
                   rsem.at[3 * nt + t * 3 + k], (x, y, 1 - c)).wait_send()


def _ring_parts(buf):
    x, y, c = _place()
    ynb, xnb = (x, 1 - y, c), (1 - x, y, c)
    ychip, xchip, dchip = 2 * x + (1 - y), 2 * (1 - x) + y, 2 * (1 - x) + (1 - y)
    hs = buf.shape[1] // 2

    def piece(ch, q):
        return buf.at[ch, pl.ds(c * hs + q * (hs // 2), hs // 2), :]

    return ynb, xnb, ychip, xchip, dchip, piece


def _ring_start(bufs, ssem, rsem):
    x, y, c = _place()
    chip = 2 * x + y
    for t, buf in enumerate(bufs):
        ynb, xnb, _, _, _, _ = _ring_parts(buf)
        _rcopy(_slab(buf, chip, c), _slab(buf, chip, c), ssem.at[2 * t], rsem.at[2 * t], ynb).start()
        _rcopy(_slab(buf, chip, c), _slab(buf, chip, c), ssem.at[2 * t + 1], rsem.at[2 * t + 1], xnb).start()


def _ring_forward(bufs, ssem, rsem):
    nt = len(bufs)
    _, _, c = _place()
    for t, buf in enumerate(bufs):
        ynb, xnb, ychip, xchip, _, piece = _ring_parts(buf)
        _rcopy(_slab(buf, ychip, c), _slab(buf, ychip, c), ssem.at[2 * t], rsem.at[2 * t], ynb).wait_recv()
        _rcopy(piece(ychip, 0), piece(ychip, 0), ssem.at[2 * nt + 2 * t], rsem.at[2 * nt + 2 * t], xnb).start()
        _rcopy(_slab(buf, xchip, c), _slab(buf, xchip, c), ssem.at[2 * t + 1], rsem.at[2 * t + 1], xnb).wait_recv()
        _rcopy(piece(xchip, 1), piece(xchip, 1), ssem.at[2 * nt + 2 * t + 1], rsem.at[2 * nt + 2 * t + 1], ynb).start()


def _ring_finish(bufs, ssem, rsem):
    nt = len(bufs)
    x, y, c = _place()
    chip = 2 * x + y
    sibling = (x, y, 1 - c)
    for t, buf in enumerate(bufs):
        ynb, xnb, ychip, xchip, dchip, piece = _ring_parts(buf)
        _rcopy(piece(dchip, 0), piece(dchip, 0), ssem.at[2 * nt + 2 * t], rsem.at[2 * nt + 2 * t], xnb).wait_recv()
        _rcopy(piece(dchip, 1), piece(dchip, 1), ssem.at[2 * nt + 2 * t + 1], rsem.at[2 * nt + 2 * t + 1],
               ynb).wait_recv()
        for k, ch in enumerate((ychip, xchip, dchip)):
            _rcopy(_slab(buf, ch, c), _slab(buf, ch, c), ssem.at[4 * nt + 3 * t + k], rsem.at[4 * nt + 3 * t + k],
                   sibling).start()
    for t, buf in enumerate(bufs):
        ynb, xnb, ychip, xchip, dchip, piece = _ring_parts(buf)
        for k, ch in enumerate((ychip, xchip, dchip)):
            _rcopy(_slab(buf, ch, 1 - c), _slab(buf, ch, 1 - c), ssem.at[4 * nt + 3 * t + k],
                   rsem.at[4 * nt + 3 * t + k], sibling).wait_recv()
            _rcopy(_slab(buf, ch, c), _slab(buf, ch, c), ssem.at[4 * nt + 3 * t + k], rsem.at[4 * nt + 3 * t + k],
                   sibling).wait_send()
        _rcopy(_slab(buf, chip, c), _slab(buf, chip, c), ssem.at[2 * t], rsem.at[2 * t], ynb).wait_send()
        _rcopy(_slab(buf, chip, c), _slab(buf, chip, c), ssem.at[2 * t + 1], rsem.at[2 * t + 1], xnb).wait_send()
        _rcopy(piece(ychip, 0), piece(ychip, 0), ssem.at[2 * nt + 2 * t], rsem.at[2 * nt + 2 * t], xnb).wait_send()
        _rcopy(piece(xchip, 1), piece(xchip, 1), ssem.at[2 * nt + 2 * t + 1], rsem.at[2 * nt + 2 * t + 1],
               ynb).wait_send()


def _ring_sems(nt):
    return [pltpu.SemaphoreType.DMA((7 * nt,)), pltpu.SemaphoreType.DMA((7 * nt,))]


def _gather_sems(nt):
    return [pltpu.SemaphoreType.DMA((6 * nt,)), pltpu.SemaphoreType.DMA((6 * nt,))]


def _pair_copies(ins, outs, ssem, rsem):
    x, y, c = _place()
    copies = []
    for t in range(len(ins)):
        hs = ins[t].shape[1] // 2
        copies.append(_rcopy(ins[t].at[:, pl.ds((1 - c) * hs, hs), :], outs[t], ssem.at[t], rsem.at[t], (x, y, 1 - c)))
    return copies


def _pair_shapes(grads):
    return [_big((g.shape[0], g.shape[1] // 2, g.shape[2]), g.dtype) for g in grads]


def _pair_sems(nt):
    return [pltpu.SemaphoreType.DMA((nt,)), pltpu.SemaphoreType.DMA((nt,))]


def _pair_swap(grads, name):
    nt = len(grads)
    hbm = pl.BlockSpec(memory_space=pl.ANY)

    def body(*refs):
        copies = _pair_copies(refs[:nt], refs[nt:2 * nt], refs[2 * nt], refs[2 * nt + 1])
        for cp in copies:
            cp.start()
        for cp in copies:
            cp.wait_recv()
        for cp in copies:
            cp.wait_send()

    return pl.pallas_call(
        body, name=name, out_shape=_pair_shapes(grads), in_specs=[hbm] * nt, out_specs=[hbm] * nt,
        scratch_shapes=_pair_sems(nt),
    )(*[_hbm(g) for g in grads])


def _xchg_copies(ins, outs, ssem, rsem):
    x, y, c = _place()
    copies = []
    for k, (fx, fy) in enumerate(OTHER_CHIPS):
        peer = (_flip(x, fx), _flip(y, fy), c)
        for t in range(len(ins)):
            copies.append(_rcopy(ins[t].at[k], outs[t].at[k], ssem.at[t * 3 + k], rsem.at[t * 3 + k], peer))
    return copies


def _xchg_sems(nt):
    return [pltpu.SemaphoreType.DMA((3 * nt,)), pltpu.SemaphoreType.DMA((3 * nt,))]


def _final_exchange(fulls, stats):
    nt = len(fulls)
    rows, cols = stats.shape
    hbm = pl.BlockSpec(memory_space=pl.ANY)
    vm = pl.BlockSpec(memory_space=pltpu.VMEM)

    def body(*refs):
        ins, s_ref = refs[:nt], refs[nt]
        outs, g_ref = refs[nt + 1:2 * nt + 1], refs[2 * nt + 1]
        hssem, hrsem, ssem, rsem = refs[2 * nt + 2:]
        x, y, c = _place()
        me, sibling = (x, y, c), (x, y, 1 - c)
        halves = []
        for t in range(nt):
            hs = ins[t].shape[0] // 2
            mine = pl.ds(c * hs, hs)
            cp = _rcopy(ins[t].at[mine, :], outs[t].at[mine, :], hssem.at[t], hrsem.at[t], sibling)
            cp.start()
            halves.append(cp)

        chips = [(_flip(x, fx), _flip(y, fy)) for fx, fy in OTHER_CHIPS]

        def blk(px, py, pc):
            return g_ref.at[4 * px + 2 * py + pc]

        def copy(k, block, to, src=None):
            return _rcopy(blk(*block) if src is None else src, blk(*block), ssem.at[k], rsem.at[k], to)

        g_ref[4 * x + 2 * y + c] = s_ref[...]
        first = [copy(0, me, sibling, src=s_ref)]
        first += [copy(1 + j, me, (*chip, c), src=s_ref) for j, chip in enumerate(chips)]
        for cp in first:
            cp.start()
        passed = [copy(4 + j, (*chip, c), sibling) for j, chip in enumerate(chips)]
        for j, chip in enumerate(chips):
            copy(1 + j, (*chip, c), me).wait_recv()
            passed[j].start()
        copy(0, sibling, me).wait_recv()
        for j, chip in enumerate(chips):
            copy(4 + j, (*chip, 1 - c), me).wait_recv()
        for t in range(nt):
            hs = ins[t].shape[0] // 2
            other = pl.ds((1 - c) * hs, hs)
            _rcopy(ins[t].at[other, :], outs[t].at[other, :], hssem.at[t], hrsem.at[t], sibling).wait_recv()
        for cp in first + passed + halves:
            cp.wait_send()

    outs = pl.pallas_call(
        body, name="final_exchange",
        out_shape=[_big(f.shape, F32) for f in fulls] + [jax.ShapeDtypeStruct((N_DEV, rows, cols), F32)],
        in_specs=[hbm] * nt + [vm], out_specs=[hbm] * nt + [vm],
        input_output_aliases={t: t for t in range(nt)},
        scratch_shapes=[pltpu.SemaphoreType.DMA((nt,)), pltpu.SemaphoreType.DMA((nt,)),
                        pltpu.SemaphoreType.DMA((7,)), pltpu.SemaphoreType.DMA((7,))],
        compiler_params=_cp(None, 32),
    )(*[_hbm(f) for f in fulls], stats)
    return outs[:nt], outs[nt]


def _row_tile(rows):
    return min(rows, 512)


def _pair_sum(place, grad, got, name):
    nb, hs, cols = got.shape
    tr = _row_tile(hs)
    nr = hs // tr

    def body(pl_ref, g_ref, o_ref, pf_ref, pb_ref):
        j = pl.program_id(1)
        s = g_ref[0] + o_ref[0].astype(F32)

        @pl.when(j == 0)
        def _():
            pf_ref[...] = s

        @pl.when(j > 0)
        def _():
            pb_ref[0] = s.astype(BF16)

    gs = pltpu.PrefetchScalarGridSpec(
        num_scalar_prefetch=1, grid=(nr, nb),
        in_specs=[pl.BlockSpec((1, tr, cols), lambda i, j, p: (p[1] ^ j, p[0] * nr + i, 0)),
                  pl.BlockSpec((1, tr, cols), lambda i, j, p: (p[1] ^ j, i, 0))],
        out_specs=[pl.BlockSpec((tr, cols), lambda i, j, p: (i, 0)),
                   pl.BlockSpec((1, tr, cols), lambda i, j, p: (jnp.maximum(j - 1, 0), i, 0))])
    return pl.pallas_call(
        body, name=name, grid_spec=gs,
        out_shape=[_big((hs, cols), F32), _big((nb - 1, hs, cols), BF16)],
        compiler_params=_cp(("arbitrary", "arbitrary"), 32),
    )(place, *_hbm(grad, got))


def _chip_sum(place, pair_f, got_b, name):
    nb, hs, cols = got_b.shape
    tr = _row_tile(hs)
    nr = hs // tr

    def body(pl_ref, pf_ref, gb_ref, o_ref):
        acc = pf_ref[...]
        for k in range(nb):
            acc = acc + gb_ref[k].astype(F32)
        o_ref[...] = acc

    gs = pltpu.PrefetchScalarGridSpec(
        num_scalar_prefetch=1, grid=(nr,),
        in_specs=[pl.BlockSpec((tr, cols), lambda i, p: (i, 0)),
                  pl.BlockSpec((nb, tr, cols), lambda i, p: (0, i, 0))],
        out_specs=pl.BlockSpec((tr, cols), lambda i, p: (p[0] * nr + i, 0)))
    return pl.pallas_call(
        body, name=name, grid_spec=gs,
        out_shape=_big((2 * hs, cols), F32),
        compiler_params=_cp(("arbitrary",), 32),
    )(place, *_hbm(pair_f, got_b))


def _adam_math(w, g, m, v):
    m2 = ADAM_B1 * m + (1.0 - ADAM_B1) * g
    v2 = ADAM_B2 * v + (1.0 - ADAM_B2) * (g * g)
    m_hat = m2 / (1.0 - ADAM_B1 ** ADAM_STEP)
    v_hat = v2 / (1.0 - ADAM_B2 ** ADAM_STEP)
    delta = -ADAM_LR * (m_hat / (jnp.sqrt(v_hat) + ADAM_EPS) + ADAM_WD * w)
    return delta, m2, v2


def _adam_big(w, g, m, v, name):
    rows, cols = w.shape
    tr = _row_tile(rows)

    def body(w_ref, g_ref, m_ref, v_ref, d_out, m_out, v_out):
        d, m2, v2 = _adam_math(w_ref[...], g_ref[...], m_ref[...], v_ref[...])
        d_out[...] = d
        m_out[...] = m2
        v_out[...] = v2

    spec = pl.BlockSpec((tr, cols), lambda i: (i, 0))
    return pl.pallas_call(
        body, name=name, grid=(rows // tr,), in_specs=[spec] * 4, out_specs=[spec] * 3,
        out_shape=[_big(w.shape, F32)] * 3,
        compiler_params=_cp(("arbitrary",), 48),
    )(*_hbm(w, g, m, v))


def _ada_grad_adam(c8t, dmod_sh, w, m, v):
    rows, cols = w.shape
    tr = _row_tile(rows) // 2

    def body(ct_ref, dm_ref, w_ref, m_ref, v_ref, g_out, d_out, m_out, v_out):
        g = None
        for b in range(N_DEV):
            term = ct_ref[:, b:b + 1] * dm_ref[b:b + 1, :]
            g = term if g is None else g + term
        d, m2, v2 = _adam_math(w_ref[...], g, m_ref[...], v_ref[...])
        g_out[...] = g
        d_out[...] = d
        m_out[...] = m2
        v_out[...] = v2

    spec = pl.BlockSpec((tr, cols), lambda i: (i, 0))
    return pl.pallas_call(
        body, name="ada_grad_adam", grid=(rows // tr,),
        in_specs=[pl.BlockSpec((tr, N_DEV), lambda i: (i, 0)), _full((N_DEV, cols)), spec, spec, spec],
        out_specs=[spec] * 4, out_shape=[_big(w.shape, F32)] * 4,
        compiler_params=_cp(("arbitrary",), 32),
    )(c8t, dmod_sh, *_hbm(w, m, v))


def _tok_tile(t):
    return min(TOK_TILE, t)


def _mix_in_fwd(chip1, x, modr, g_pre, b_in4, later, w_in_buf, w_out_buf):
    T = x.shape[0]
    tm = min(MIXIN_TILE, T)
    nt = T // tm
    nb = IN_COLS // N_CHIPS
    nl = len(later)

    def body(*refs):
        ch_ref, x_ref, mod_ref, g_ref, b_ref = refs[:5]
        l_ins = refs[5:5 + nl]
        p_ref, h_ref = refs[7 + nl:9 + nl]
        l_outs = refs[9 + nl:9 + 2 * nl]
        win_ref, wout_ref = refs[9 + 2 * nl:11 + 2 * nl]
        h_all, wblk, lsem, is_sem, ir_sem, os_sem, or_sem = refs[11 + 2 * nl:]
        k, i = pl.program_id(0), pl.program_id(1)
        chip = ch_ref[0]

        @pl.when(k == 0)
        def _():
            for src, dst in zip(l_ins, l_outs):
                dst[0] = src[...].astype(BF16)

        def load_block(blk):
            cp = pltpu.make_async_copy(win_ref.at[blk], wblk, lsem)
            cp.start()
            cp.wait()

        @pl.when((k == 0) & (i == 0))
        def _():
            _gather_start([win_ref], is_sem, ir_sem, relations=(0, 1))
            load_block(chip)

        for r in range(N_CHIPS - 1):
            @pl.when((k == r + 1) & (i == 0))
            def _(r=r):
                _gather_arrive([win_ref], r, is_sem, ir_sem)
                if r == 0:
                    _gather_start([win_ref], is_sem, ir_sem, relations=(2,))
                if r == 1:
                    _gather_start([wout_ref], os_sem, or_sem)
                load_block(chip ^ (r + 1))

        rows = pl.ds(pl.multiple_of(i * tm, tm), tm)

        @pl.when(k == 0)
        def _():
            xv = x_ref[...]
            rstd = lax.rsqrt(jnp.mean(xv * xv, axis=-1, keepdims=True) + RMS_EPS)
            h = (xv * rstd) * g_ref[...] * (1.0 + mod_ref[1:2, :]) + mod_ref[0:1, :]
            hb = h.astype(BF16)
            h_ref[...] = hb
            h_all[rows, :] = hb

        p_ref[...] = _dot(h_all[rows, :], wblk[...]) + b_ref[chip ^ k]

        @pl.when((k == N_CHIPS - 1) & (i == nt - 1))
        def _():
            _gather_sends_done([win_ref], is_sem, ir_sem)
            _gather_finish([wout_ref], os_sem, or_sem)

    hbm = pl.BlockSpec(memory_space=pl.ANY)
    first_pass = lambda k, i, ch: (jnp.where(k == 0, i, nt - 1), 0)
    own_slot = lambda k, i, ch: (ch[0], jnp.where(k == 0, i, nt - 1), 0)
    gs = pltpu.PrefetchScalarGridSpec(
        num_scalar_prefetch=1, grid=(N_CHIPS, nt),
        in_specs=[pl.BlockSpec((tm, D_MODEL), first_pass), pl.BlockSpec((6, D_MODEL), lambda k, i, ch: (0, 0)),
                  pl.BlockSpec((1, D_MODEL), lambda k, i, ch: (0, 0)),
                  pl.BlockSpec((N_CHIPS, 1, nb), lambda k, i, ch: (0, 0, 0))]
        + [pl.BlockSpec((w.shape[0] // nt, w.shape[1]), first_pass) for w in later] + [hbm, hbm],
        out_specs=[pl.BlockSpec((tm, nb), lambda k, i, ch: (i, ch[0] ^ k)), pl.BlockSpec((tm, D_MODEL), first_pass)]
        + [pl.BlockSpec((1, w.shape[0] // nt, w.shape[1]), own_slot) for w in later] + [hbm, hbm],
        scratch_shapes=[pltpu.VMEM((T, D_MODEL), BF16), pltpu.VMEM((D_MODEL, nb), BF16), pltpu.SemaphoreType.DMA]
        + _gather_sems(1) + _gather_sems(1))
    outs = pl.pallas_call(
        body, name="mix_in_fwd", grid_spec=gs,
        out_shape=[_big((T, IN_COLS), F32), _big((T, D_MODEL), BF16)]
        + [_big((N_CHIPS,) + w.shape, BF16) for w in later]
        + [_big(w_in_buf.shape, BF16), _big(w_out_buf.shape, BF16)],
        input_output_aliases={5 + nl: 2 + nl, 6 + nl: 3 + nl},
        compiler_params=_cp(("arbitrary", "arbitrary"), 48),
    )(chip1, _hbm(x), modr, g_pre, b_in4, *[_hbm(w) for w in later], _hbm(w_in_buf), _hbm(w_out_buf))
    return outs[0], outs[1], outs[2:2 + nl], outs[2 + nl], outs[3 + nl]


def _mixers_fwd(p, wdw, vecs, lbl, w_out, x, modr, g_post, g_ffn, gbufs):
    T = p.shape[0]
    tm = _tok_tile(T)
    nt = T // tm
    nch = tm // CHUNK
    ng = len(gbufs)
    n_in, n_out = 12, 7

    def body(*refs):
        (p_ref, wdw_ref, bdw_ref, gain_ref, bias_ref, gout_ref, lbl_ref, wout_ref, x_ref, mod_ref, gp_ref,
         gf_ref) = refs[:n_in]
        cat_ref, ys_ref, o_ref, st_ref, y_ref, x1_ref, h2_ref = refs[n_in + ng:n_in + ng + n_out]
        gout_bufs = refs[n_in + ng + n_out:n_in + 2 * ng + n_out]
        (ubuf, state, qt_s, kt_s, kh_s, v_s, egl_s, lower_s, gmat_s, gssem,
         grsem) = refs[n_in + 2 * ng + n_out:]
        i = pl.program_id(0)

        @pl.when(i == 0)
        def _():
            _ring_start(gout_bufs, gssem, grsem)
            lower_s[...], _, _ = _chunk_masks(tm)
            gmat_s[...] = _gn_matrix()
            state[...] = jnp.zeros(state.shape, F32)
            ubuf[0:HALO, :] = jnp.zeros((HALO, CONV_CH), F32)
            ubuf[HALO + tm:HALO + tm + SUB, :] = jnp.zeros((SUB, CONV_CH), F32)

        @pl.when(i > 0)
        def _():
            ubuf[0:HALO, :] = ubuf[tm:tm + HALO, :]

        ubuf[HALO:HALO + tm, :] = p_ref[:, 0:CONV_CH] * _sig(p_ref[:, CONV_CH:2 * CONV_CH])
        for r in range(tm // CONV_ROWS):
            rows = slice(r * CONV_ROWS, (r + 1) * CONV_ROWS)
            for lb_ in range(CONV_CH // LANE):
                lanes = slice(lb_ * LANE, (lb_ + 1) * LANE)
                ys_ref[rows, lanes] = bdw_ref[:, lanes] + _tap_conv(ubuf, wdw_ref, r * CONV_ROWS, CONV_FWD_TAPS, lanes)
        gmat = gmat_s[...]
        yv = ys_ref[...]
        d = yv - _gmean(yv, gmat)
        rs = lax.rsqrt(_gmean(d * d, gmat) + GN_EPS)
        z = d * rs * gain_ref[...] + bias_ref[...]
        cat_ref[:, 0:CONV_CH] = (z * _sig(z)).astype(BF16)

        lb, _ = _lower_bound(lbl_ref)
        o0 = 2 * CONV_CH
        pr = _hgrn_prep(p_ref[:, o0:o0 + HGRN_W], p_ref[:, o0 + HGRN_W:o0 + 2 * HGRN_W], lb, lower_s[...])
        qt_s[...] = pr["qt"].astype(BF16)
        kt_s[...] = pr["kt"].astype(BF16)
        kh_s[...] = pr["kh"].astype(BF16)
        v_s[...] = p_ref[:, o0 + 2 * HGRN_W:o0 + 3 * HGRN_W].astype(BF16)
        egl_s[...] = jnp.exp(pr["Gl"])
        tri = _tri()

        def chunk(ci, carry):
            r0 = pl.multiple_of(ci * CHUNK, CHUNK)
            rows = pl.ds(r0, CHUNK)
            for h in range(N_HEADS):
                ls = pl.ds(h * HEAD_D, HEAD_D)
                qc, kc, hc, vc = qt_s[rows, ls], kt_s[rows, ls], kh_s[rows, ls], v_s[rows, ls]
                s0 = state[h]
                s0b = s0.astype(BF16)
                st_ref[ci, h] = s0
                att =jnp.where(tri, _dot_nt(qc, kc), 0.0).astype(BF16)
                o_ref[rows, ls] = _dot(att, vc) + _dot_nt(qc, s0b)
                state[h] = s0 * egl_s[pl.ds(r0, 1), ls] + _dot_tn(vc, hc)
            return carry

        lax.fori_loop(0, nch, chunk, 0, unroll=min(CHUNK_UNROLL, nch))
        for h in range(N_HEADS):
            sl = slice(h * HEAD_D, (h + 1) * HEAD_D)
            oh = o_ref[:, sl]
            gh = p_ref[:, o0 + 3 * HGRN_W + h * HEAD_D:o0 + 3 * HGRN_W + (h + 1) * HEAD_D]
            rsh = lax.rsqrt(jnp.mean(oh * oh, axis=-1, keepdims=True) + RMS_EPS)
            hg = (oh * rsh) * gout_ref[:, sl] * (gh * _sig(gh))
            cat_ref[:, CONV_CH + h * HEAD_D:CONV_CH + (h + 1) * HEAD_D] = hg.astype(BF16)

        yv = _dot(cat_ref[...], wout_ref[...])
        y_ref[...] = yv
        rsy = lax.rsqrt(jnp.mean(yv * yv, axis=-1, keepdims=True) + RMS_EPS)
        x1 = x_ref[...] + (yv * rsy) * (mod_ref[2:3, :] * gp_ref[...])
        x1_ref[...] = x1
        rs1 = lax.rsqrt(jnp.mean(x1 * x1, axis=-1, keepdims=True) + RMS_EPS)
        h2 = (x1 * rs1) * (gf_ref[...] * (1.0 + mod_ref[4:5, :])) + mod_ref[3:4, :]
        h2_ref[...] = h2.astype(BF16)

        @pl.when(i == min(nt - 1, nt // 2 + 1))
        def _():
            _ring_forward(gout_bufs, gssem, grsem)

        @pl.when(i == nt - 1)
        def _():
            _ring_finish(gout_bufs, gssem, grsem)

    tile = lambda cols: pl.BlockSpec((tm, cols), lambda i: (i, 0))
    hbm = pl.BlockSpec(memory_space=pl.ANY)
    outs = pl.pallas_call(
        body, name="mixers_fwd", grid=(nt,),
        in_specs=[tile(IN_COLS), _full((HALO, CONV_CH))] + [_full((1, CONV_CH))] * 4 + [_full((2, HGRN_W))]
        + [_full((D_MODEL, D_MODEL)), tile(D_MODEL), _full((6, D_MODEL)), _full((1, D_MODEL)), _full((1, D_MODEL))]
        + [hbm] * ng,
        out_specs=[tile(D_MODEL), tile(CONV_CH), tile(HGRN_W),
                   pl.BlockSpec((nch, N_HEADS, HEAD_D, HEAD_D), lambda i: (i, 0, 0, 0)),
                   tile(D_MODEL), tile(D_MODEL), tile(D_MODEL)] + [hbm] * ng,
        out_shape=[_big((T, D_MODEL), BF16), _big((T, CONV_CH), F32), _big((T, HGRN_W), F32),
                   _big((T // CHUNK, N_HEADS, HEAD_D, HEAD_D), F32), _big((T, D_MODEL), F32),
                   _big((T, D_MODEL), F32), _big((T, D_MODEL), BF16)] + [_big(b.shape, BF16) for b in gbufs],
        input_output_aliases={n_in + t: n_out + t for t in range(ng)},
        scratch_shapes=[pltpu.VMEM((tm + HALO + SUB, CONV_CH), F32), pltpu.VMEM((N_HEADS, HEAD_D, HEAD_D), F32),
                        pltpu.VMEM((tm, HGRN_W), BF16), pltpu.VMEM((tm, HGRN_W), BF16),
                        pltpu.VMEM((tm, HGRN_W), BF16), pltpu.VMEM((tm, HGRN_W), BF16),
                        pltpu.VMEM((tm, HGRN_W), F32), pltpu.VMEM((tm, tm), BF16),
                        pltpu.VMEM((CONV_CH, CONV_CH), BF16)] + _ring_sems(ng),
        compiler_params=_cp(("arbitrary",), 56),
    )(_hbm(p), wdw, *vecs, lbl, *_hbm(w_out, x), modr, g_post, g_ffn, *[_hbm(b) for b in gbufs])
    return outs[:n_out], outs[n_out:]


def _row_chains(rows, n=2):
    step = rows // n
    return [slice(k * step, (k + 1) * step) for k in range(n)]


def _ffn_blocks():
    return D_FF // FFN_BLOCK, (D_FF // N_CHIPS) // FFN_BLOCK


def _ffn_fwd(h2, w_up_g, w_down, x1, target, modr, g_post):
    T = h2.shape[0]
    tm = min(FFN_TILE, T)
    fb = FFN_BLOCK
    nj, per = _ffn_blocks()

    def body(h_ref, wu_ref, wd_ref, x1_ref, t_ref, mod_ref, g_ref, r_ref, dy2_ref, dx2_ref, st_ref, acc):
        i, j = pl.program_id(0), pl.program_id(1)

        @pl.when((i == 0) & (j == 0))
        def _():
            st_ref[...] = jnp.zeros(st_ref.shape, F32)

        @pl.when(j == 0)
        def _():
            acc[...] = jnp.zeros(acc.shape, F32)

        for rows in _row_chains(tm):
            ra = jnp.maximum(_dot(h_ref[rows, :], wu_ref[0]), 0.0)
            rb = (ra * ra).astype(BF16)
            r_ref[rows, :] = rb
            acc[rows, :] += _dot(rb, wd_ref[...])

        @pl.when(j == nj - 1)
        def _():
            y2 = acc[...]
            rs = lax.rsqrt(jnp.mean(y2 * y2, axis=-1, keepdims=True) + RMS_EPS)
            nh = y2 * rs
            gp, gt = g_ref[...], mod_ref[5:6, :]
            gate_gain = gt * gp
            err = x1_ref[...] + nh * gate_gain - t_ref[...]
            dx2 = err * (1.0 / D_MODEL)
            dx2_ref[...] = dx2
            st_ref[0:1, :] += _colsum(err * err)
            s_dn = _colsum(dx2 * nh)
            st_ref[1:2, :] += s_dn * gp
            st_ref[2:3, :] += s_dn * gt
            dy2_ref[...] = _rms_bwd(dx2 * gate_gain, nh, rs).astype(BF16)

    tile = pl.BlockSpec((tm, D_MODEL), lambda i, j: (i, 0))
    return pl.pallas_call(
        body, name="ffn_fwd", grid=(T // tm, nj),
        in_specs=[tile, pl.BlockSpec((1, D_MODEL, fb), lambda i, j: (j // per, 0, j % per)),
                  pl.BlockSpec((fb, D_MODEL), lambda i, j: (j, 0)), tile, tile,
                  _full((6, D_MODEL)), _full((1, D_MODEL))],
        out_specs=[pl.BlockSpec((tm, fb), lambda i, j: (i, j)), tile, tile, _full((8, D_MODEL))],
        out_shape=[_big((T, D_FF), BF16), _big((T, D_MODEL), BF16), _big((T, D_MODEL), F32),
                   jax.ShapeDtypeStruct((8, D_MODEL), F32)],
        scratch_shapes=[pltpu.VMEM((tm, D_MODEL), F32)],
        compiler_params=_cp(("arbitrary", "arbitrary"), 56),
    )(*_hbm(h2, w_up_g, w_down, x1, target), modr, g_post)


def _rms_bwd(dxn, xn, rs):
    return rs * (dxn - xn * jnp.mean(dxn * xn, axis=-1, keepdims=True))


def _ffn_bwd(dy2, r, x1, dx2, w_up_g, w_down, modr, g_ffn):
    T = dx2.shape[0]
    tm = min(FFN_TILE, T)
    fb = FFN_BLOCK
    nj, per = _ffn_blocks()

    def body(dy2_ref, r_ref, x1_ref, dx2_ref, wu_ref, wd_ref, mod_ref, gf_ref, da_ref, dx1_ref, st_ref, dh_s):
        i, j = pl.program_id(0), pl.program_id(1)

        @pl.when((i == 0) & (j == 0))
        def _():
            st_ref[...] = jnp.zeros(st_ref.shape, F32)

        @pl.when(j == 0)
        def _():
            dh_s[...] = jnp.zeros(dh_s.shape, F32)

        for rows in _row_chains(tm):
            ra = jnp.sqrt(r_ref[rows, :].astype(F32))
            da = (_dot_nt(dy2_ref[rows, :], wd_ref[...]) * (2.0 * ra)).astype(BF16)
            da_ref[rows, :] = da
            dh_s[rows, :] += _dot_nt(da, wu_ref[0])

        @pl.when(j == nj - 1)
        def _():
            dh = dh_s[...]
            x1v = x1_ref[...]
            rs1 = lax.rsqrt(jnp.mean(x1v * x1v, axis=-1, keepdims=True) + RMS_EPS)
            xn = x1v * rs1
            gf, scale1 = gf_ref[...], 1.0 + mod_ref[4:5, :]
            st_ref[0:1, :] += _colsum(dh)
            s_dh = _colsum(dh * xn)
            st_ref[1:2, :] += s_dh * gf
            st_ref[2:3, :] += s_dh * scale1
            dx1_ref[...] = dx2_ref[...] + _rms_bwd(dh * (scale1 * gf), xn, rs1)

    tile = pl.BlockSpec((tm, D_MODEL), lambda i, j: (i, 0))
    ftile = pl.BlockSpec((tm, fb), lambda i, j: (i, j))
    return pl.pallas_call(
        body, name="ffn_bwd", grid=(T // tm, nj),
        in_specs=[tile, ftile, tile, tile, pl.BlockSpec((1, D_MODEL, fb), lambda i, j: (j // per, 0, j % per)),
                  pl.BlockSpec((fb, D_MODEL), lambda i, j: (j, 0)), _full((6, D_MODEL)), _full((1, D_MODEL))],
        out_specs=[ftile, tile, _full((8, D_MODEL))],
        out_shape=[_big((T, D_FF), BF16), _big((T, D_MODEL), F32), jax.ShapeDtypeStruct((8, D_MODEL), F32)],
        scratch_shapes=[pltpu.VMEM((tm, D_MODEL), F32)],
        compiler_params=_cp(("arbitrary", "arbitrary"), 56),
    )(*_hbm(dy2, r, x1, dx2, w_up_g, w_down), modr, g_ffn)


def _mix_out_bwd(dx1, y, cat, w_out, modr, g_post, swap):
    T = dx1.shape[0]
    tm = _tok_tile(T)
    nt = T // tm
    ns = len(swap)

    def body(*refs):
        dx1_ref, y_ref, cat_ref, w_ref, mod_ref, gp_ref = refs[:6]
        s_ins = refs[6:6 + ns]
        dcat_ref, st_ref, gw_ref = refs[6 + ns:9 + ns]
        s_outs = refs[9 + ns:9 + 2 * ns]
        gacc, gsem, pssem, prsem = refs[9 + 2 * ns:]
        i = pl.program_id(0)

        @pl.when(i == 0)
        def _():
            for cp in _pair_copies(s_ins, s_outs, pssem, prsem):
                cp.start()
            st_ref[...] = jnp.zeros(st_ref.shape, F32)
            gacc[...] = jnp.zeros(gacc.shape, F32)

        dxv, yv = dx1_ref[...], y_ref[...]
        rs = lax.rsqrt(jnp.mean(yv * yv, axis=-1, keepdims=True) + RMS_EPS)
        nh = yv * rs
        st_ref[0:1, :] += _colsum(dxv * (nh * gp_ref[...]))
        dn = dxv * mod_ref[2:3, :]
        st_ref[1:2, :] += _colsum(dn * nh)
        dy = _rms_bwd(dn * gp_ref[...], nh, rs).astype(BF16)
        dcat_ref[...] = _dot_nt(dy, w_ref[...])
        for cols in _row_chains(D_MODEL):
            gacc[:, cols] += _dot_tn(cat_ref[...], dy[:, cols])

        @pl.when(i == nt - 1)
        def _():
            out = pltpu.make_async_copy(gacc, gw_ref, gsem)
            out.start()
            copies = _pair_copies(s_ins, s_outs, pssem, prsem)
            for cp in copies:
                cp.wait_recv()
            for cp in copies:
                cp.wait_send()
            out.wait()

    tile = pl.BlockSpec((tm, D_MODEL), lambda i: (i, 0))
    hbm = pl.BlockSpec(memory_space=pl.ANY)
    outs = pl.pallas_call(
        body, name="mix_out_bwd", grid=(nt,),
        in_specs=[tile, tile, tile, _full((D_MODEL, D_MODEL)), _full((6, D_MODEL)), _full((1, D_MODEL))]
        + [hbm] * ns,
        out_specs=[tile, _full((8, D_MODEL)), hbm] + [hbm] * ns,
        out_shape=[_big((T, D_MODEL), F32), jax.ShapeDtypeStruct((8, D_MODEL), F32), _big((D_MODEL, D_MODEL), F32)]
        + _pair_shapes(swap),
        scratch_shapes=[pltpu.VMEM((D_MODEL, D_MODEL), F32), pltpu.SemaphoreType.DMA] + _pair_sems(ns),
        compiler_params=_cp(("arbitrary",), 48),
    )(*_hbm(dx1, y, cat, w_out), modr, g_post, *[_hbm(g) for g in swap])
    return outs[:3], outs[3:]


def _mixers_bwd(p, dcat, ys, o, states, h1, wdw, vecs, lbl, pairs_b):
    T = p.shape[0]
    tm = min(MIXB_TILE, T)
    nt = T // tm
    nch = tm // CHUNK
    hpt = tm // HALO
    nx = len(pairs_b)
    nb = IN_COLS // N_CHIPS

    def body(*refs):
        (p_ref, ph_ref, dcat_ref, ys_ref, o_ref, st_ref, h1_ref, wdw_ref, bdw_ref, gain_ref, bias_ref, gout_ref,
         lbl_ref) = refs[:13]
        x_ins = refs[13:13 + nx]
        dp_ref, sb_ref, s5_ref, dw_ref = refs[13 + nx:17 + nx]
        x_outs = refs[17 + nx:17 + 2 * nx]
        gin_ref, ginb_ref = refs[17 + 2 * nx], refs[18 + 2 * nx]
        (ubuf, dybuf, carry, dstate, qt_s, kt_s, kh_s, v_s, do_s, egl_s, dqt_s, dkt_s, dkh_s, dv_s, dgl_s,
         dsh, dw8, dshift, lower_s, upper_s, same_s, gmat_s, gacc, gacc_b, dp_prev, h1_prev, gsem, gbsem, xssem,
         xrsem) = refs[19 + 2 * nx:]
        i = pl.program_id(0)
        tile_idx = nt - 1 - i

        @pl.when(i == 0)
        def _():
            for cp in _xchg_copies(x_ins, x_outs, xssem, xrsem):
                cp.start()
            gacc[...] = jnp.zeros(gacc.shape, F32)
            dstate[...] = jnp.zeros(dstate.shape, F32)
            carry[...] = jnp.zeros(carry.shape, F32)
            sb_ref[...] = jnp.zeros(sb_ref.shape, F32)
            s5_ref[...] = jnp.zeros(s5_ref.shape, F32)
            dw_ref[...] = jnp.zeros(dw_ref.shape, F32)
            dw8[...] = jnp.zeros(dw8.shape, F32)
            lower_s[...], upper_s[...], same_s[...] = _chunk_masks(tm)
            gmat_s[...] = _gn_matrix()
            dsh[0:SUB, :] = jnp.zeros((SUB, CONV_CH), F32)
            dsh[SUB + tm:2 * SUB + tm, :] = jnp.zeros((SUB, CONV_CH), F32)
            ubuf[HALO + tm:HALO + tm + SUB, :] = jnp.zeros((SUB, CONV_CH), F32)
            dp_prev[...] = jnp.zeros(dp_prev.shape, BF16)
            h1_prev[...] = jnp.zeros(h1_prev.shape, BF16)

        n_pieces = (tm // CONV_ROWS) * (CONV_CH // LANE)
        per_block = n_pieces // N_CHIPS
        prow = D_MODEL // per_block

        def w_in_grad_piece(k):
            j, part = k // per_block, k % per_block
            rows_k = slice(part * prow, (part + 1) * prow)
            gacc[j, rows_k, :] += _dot_tn(h1_prev[:, rows_k], dp_prev[:, j * nb:(j + 1) * nb])

        uh = ph_ref[:, 0:CONV_CH] * _sig(ph_ref[:, CONV_CH:2 * CONV_CH])
        ubuf[0:HALO, :] = jnp.where(tile_idx > 0, uh, 0.0)
        ubuf[HALO:HALO + tm, :] = p_ref[:, 0:CONV_CH] * _sig(p_ref[:, CONV_CH:2 * CONV_CH])
        gmat = gmat_s[...]
        gain = gain_ref[...]
        yv = ys_ref[...]
        d = yv - _gmean(yv, gmat)
        rs = lax.rsqrt(_gmean(d * d, gmat) + GN_EPS)
        yn = d * rs
        z = yn * gain + bias_ref[...]
        sz = _sig(z)
        dz = dcat_ref[:, 0:CONV_CH] * (sz * (1.0 + z * (1.0 - sz)))
        dyn = dz * gain
        dyc = rs * (dyn - _gmean(dyn, gmat) - yn * _gmean(dyn * yn, gmat))
        s5_ref[0:1, :] += _colsum(dyc)
        s5_ref[1:2, :] += _colsum(dz * yn)
        s5_ref[2:3, :] += _colsum(dz)
        dybuf[tm:tm + HALO, :] = carry[...]
        dybuf[0:tm, :] = dyc
        dsh[SUB:SUB + tm, :] = dyc
        carry[...] = dyc[0:HALO, :]
        for b in range(SUB):
            dshift[...] = dsh[SUB - b:2 * SUB - b + tm, :]
            for j, off in CONV_FWD_TAPS:
                if off % SUB == b:
                    prod = dshift[...] * ubuf[off - b:off - b + tm + SUB, :]
                    dw8[j] += jnp.sum(prod.reshape((tm + SUB) // SUB, SUB, CONV_CH), axis=0)
        for r in range(tm // CONV_ROWS):
            rows = slice(r * CONV_ROWS, (r + 1) * CONV_ROWS)
            for lb_ in range(CONV_CH // LANE):
                lanes = slice(lb_ * LANE, (lb_ + 1) * LANE)
                glanes = slice(CONV_CH + lb_ * LANE, CONV_CH + (lb_ + 1) * LANE)
                w_in_grad_piece(r * (CONV_CH // LANE) + lb_)
                acc = _tap_conv(dybuf, wdw_ref, r * CONV_ROWS, CONV_BWD_TAPS, lanes)
                val = p_ref[rows, lanes]
                sg = _sig(p_ref[rows, glanes])
                dval = acc * sg
                dgate = acc * val * (sg * (1.0 - sg))
                dp_ref[rows, lanes] = dval.astype(BF16)
                dp_ref[rows, glanes] = dgate.astype(BF16)
                sb_ref[0:1, lanes] += _colsum(dval)
                sb_ref[0:1, glanes] += _colsum(dgate)

        o0 = 2 * CONV_CH
        for h in range(N_HEADS):
            sl = slice(h * HEAD_D, (h + 1) * HEAD_D)
            gsl = slice(o0 + 3 * HGRN_W + h * HEAD_D, o0 + 3 * HGRN_W + (h + 1) * HEAD_D)
            oh = o_ref[:, sl]
            gh = p_ref[:, gsl]
            dh = dcat_ref[:, CONV_CH + h * HEAD_D:CONV_CH + (h + 1) * HEAD_D]
            gout = gout_ref[:, sl]
            rsh = lax.rsqrt(jnp.mean(oh * oh, axis=-1, keepdims=True) + RMS_EPS)
            on = oh * rsh
            sgg = _sig(gh)
            dgh = dh * (on * gout) * (sgg * (1.0 + gh * (1.0 - sgg)))
            dm = dh * (gh * sgg)
            s5_ref[3:4, sl] += _colsum(dm * on)
            do_s[:, sl] = _rms_bwd(dm * gout, on, rsh).astype(BF16)
            dp_ref[:, gsl] = dgh.astype(BF16)
            sb_ref[2:3, CONV_CH + h * HEAD_D:CONV_CH + (h + 1) * HEAD_D] += _colsum(dgh)

        lb, _ = _lower_bound(lbl_ref)
        pq = p_ref[:, o0:o0 + HGRN_W]
        pr = _hgrn_prep(pq, p_ref[:, o0 + HGRN_W:o0 + 2 * HGRN_W], lb, lower_s[...])
        qt_s[...] = pr["qt"].astype(BF16)
        kt_s[...] = pr["kt"].astype(BF16)
        kh_s[...] = pr["kh"].astype(BF16)
        v_s[...] = p_ref[:, o0 + 2 * HGRN_W:o0 + 3 * HGRN_W].astype(BF16)
        egl_s[...] = jnp.exp(pr["Gl"])
        tri = _tri()

        def chunk(it, c_):
            ci = nch - 1 - it
            r0 = pl.multiple_of(ci * CHUNK, CHUNK)
            rows = pl.ds(r0, CHUNK)
            for h in range(N_HEADS):
                ls = pl.ds(h * HEAD_D, HEAD_D)
                qc, kc, hc, vc = qt_s[rows, ls], kt_s[rows, ls], kh_s[rows, ls], v_s[rows, ls]
                dob = do_s[rows, ls]
                s0 = st_ref[ci, h]
                s0b = s0.astype(BF16)
                ds1 = dstate[h]
                ds1b = ds1.astype(BF16)
                egl = egl_s[pl.ds(r0, 1), ls]
                att = jnp.where(tri, _dot_nt(qc, kc), 0.0).astype(BF16)
                datt = jnp.where(tri, _dot_nt(dob, vc), 0.0).astype(BF16)
                dv_s[rows, ls] = _dot_tn(att, dob) + _dot_nt(hc, ds1b)
                dqt_s[rows, ls] = _dot(datt, kc) + _dot(dob, s0b)
                dkt_s[rows, ls] = _dot_tn(datt, qc)
                dkh_s[rows, ls] = _dot(vc, ds1b)
                dgl = egl * _colsum(ds1 * s0)
                dgl_s[rows, ls] = jnp.broadcast_to(dgl, (CHUNK, HEAD_D))
                dstate[h] = ds1 * egl + _dot_tn(dob, qc)
            return c_

        lax.fori_loop(0, nch, chunk, 0, unroll=min(CHUNK_UNROLL, nch))
        dqt, dkt, dkh = dqt_s[...], dkt_s[...], dkh_s[...]
        dk = dkt * pr["enG"] + dkh * pr["eGlG"]
        khk = dkh * kh_s[...].astype(F32)
        dG = dqt * qt_s[...].astype(F32) - dkt * kt_s[...].astype(F32) - khk
        dlogf = _mm3(upper_s[...], dG) + _mm3(same_s[...], khk) + dgl_s[...]
        df = dlogf / pr["f"] - dk
        sf, sq = pr["sf"], pr["sq"]
        s5_ref[4:5, :] += _colsum(df * (1.0 - sf))
        dfl = df * (1.0 - lb) * (sf * (1.0 - sf))
        dq = (dqt * pr["eG"]) * (sq * (1.0 + pq * (1.0 - sq)))
        dvv = dv_s[...]
        dp_ref[:, o0:o0 + HGRN_W] = dq.astype(BF16)
        dp_ref[:, o0 + HGRN_W:o0 + 2 * HGRN_W] = dfl.astype(BF16)
        dp_ref[:, o0 + 2 * HGRN_W:o0 + 3 * HGRN_W] = dvv.astype(BF16)
        sb_ref[1:2, 0:HGRN_W] += _colsum(dq)
        sb_ref[1:2, HGRN_W:2 * HGRN_W] += _colsum(dfl)
        sb_ref[2:3, 0:HGRN_W] += _colsum(dvv)

        dp_prev[...] = dp_ref[...]
        h1_prev[...] = h1_ref[...]

        @pl.when(i == nt - 1)
        def _():
            for k in range(n_pieces):
                w_in_grad_piece(k)
            out = pltpu.make_async_copy(gacc, gin_ref, gsem)
            out.start()
            for j in range(N_CHIPS):
                gacc_b[j] = gacc[j].astype(BF16)
            out_b = pltpu.make_async_copy(gacc_b, ginb_ref, gbsem)
            out_b.start()
            for j in range(CONV_K):
                dw_ref[j:j + 1, :] = _colsum(dw8[j])
            copies = _xchg_copies(x_ins, x_outs, xssem, xrsem)
            for cp in copies:
                cp.wait_recv()
            for cp in copies:
                cp.wait_send()
            out.wait()
            out_b.wait()

    rev = lambda cols: pl.BlockSpec((tm, cols), lambda i: (nt - 1 - i, 0))
    halo = pl.BlockSpec((HALO, 2 * CONV_CH), lambda i: (jnp.maximum((nt - 1 - i) * hpt - 1, 0), 0))
    wide = lambda n: pltpu.VMEM((tm, HGRN_W), n)
    hbm = pl.BlockSpec(memory_space=pl.ANY)
    outs = pl.pallas_call(
        body, name="mixers_bwd", grid=(nt,),
        in_specs=[rev(IN_COLS), halo, rev(D_MODEL), rev(CONV_CH), rev(HGRN_W),
                  pl.BlockSpec((nch, N_HEADS, HEAD_D, HEAD_D), lambda i: (nt - 1 - i, 0, 0, 0)), rev(D_MODEL),
                  _full((HALO, CONV_CH))] + [_full((1, CONV_CH))] * 4 + [_full((2, HGRN_W))] + [hbm] * nx,
        out_specs=[rev(IN_COLS), _full((8, D_MODEL)), _full((8, CONV_CH)), _full((HALO, CONV_CH))]
        + [hbm] * (nx + 2),
        out_shape=[_big((T, IN_COLS), BF16), jax.ShapeDtypeStruct((8, D_MODEL), F32),
                   jax.ShapeDtypeStruct((8, CONV_CH), F32), jax.ShapeDtypeStruct((HALO, CONV_CH), F32)]
        + [_big(pb.shape, BF16) for pb in pairs_b]
        + [_big((N_CHIPS, D_MODEL, nb), F32), _big((N_CHIPS, D_MODEL, nb), BF16)],
        scratch_shapes=[pltpu.VMEM((tm + HALO + SUB, CONV_CH), F32), pltpu.VMEM((tm + HALO, CONV_CH), F32),
                        pltpu.VMEM((HALO, CONV_CH), F32), pltpu.VMEM((N_HEADS, HEAD_D, HEAD_D), F32),
                        wide(BF16), wide(BF16), wide(BF16), wide(BF16), wide(BF16),
                        wide(F32), wide(F32), wide(F32), wide(F32), wide(F32), wide(F32),
                        pltpu.VMEM((tm + 2 * SUB, CONV_CH), F32), pltpu.VMEM((HALO, SUB, CONV_CH), F32),
                        pltpu.VMEM((tm + SUB, CONV_CH), F32), pltpu.VMEM((tm, tm), BF16), pltpu.VMEM((tm, tm), BF16),
                        pltpu.VMEM((tm, tm), BF16), pltpu.VMEM((CONV_CH, CONV_CH), BF16),
                        pltpu.VMEM((N_CHIPS, D_MODEL, nb), F32), pltpu.VMEM((N_CHIPS, D_MODEL, nb), BF16),
                        pltpu.VMEM((tm, IN_COLS), BF16), pltpu.VMEM((tm, D_MODEL), BF16),
                        pltpu.SemaphoreType.DMA, pltpu.SemaphoreType.DMA]
        + _xchg_sems(nx),
        compiler_params=_cp(("arbitrary",), 56),
    )(*_hbm(p, p, dcat, ys, o, states, h1), wdw, *vecs, lbl, *[_hbm(pb) for pb in pairs_b])
    return outs[:4], outs[4:4 + nx], (outs[4 + nx], outs[5 + nx])


def _mix_in_bwd(dp, w_in_g, x, dx1, modr, g_pre, pairs_b):
    T = x.shape[0]
    tm = _tok_tile(T)
    nt = T // tm
    nb = IN_COLS // N_CHIPS
    nx = len(pairs_b)

    def body(*refs):
        dp_ref, w_ref, x_ref, dx1_ref, mod_ref, g_ref = refs[:6]
        x_ins = refs[6:6 + nx]
        gx_ref, st_ref = refs[6 + nx:8 + nx]
        x_outs = refs[8 + nx:8 + 2 * nx]
        xssem, xrsem = refs[8 + 2 * nx:]
        i = pl.program_id(0)

        @pl.when(i == 0)
        def _():
            for cp in _xchg_copies(x_ins, x_outs, xssem, xrsem):
                cp.start()
            st_ref[...] = jnp.zeros(st_ref.shape, F32)

        g, scale1 = g_ref[...], 1.0 + mod_ref[1:2, :]
        for rows in _row_chains(tm):
            dh = None
            for j in range(N_CHIPS):
                part = _dot_nt(dp_ref[rows, j * nb:(j + 1) * nb], w_ref[j])
                dh = part if dh is None else dh + part
            xv = x_ref[rows, :]
            rs = lax.rsqrt(jnp.mean(xv * xv, axis=-1, keepdims=True) + RMS_EPS)
            xn = xv * rs
            st_ref[0:1, :] += _colsum(dh)
            s_dh = _colsum(dh * xn)
            st_ref[1:2, :] += s_dh * g
            st_ref[2:3, :] += s_dh * scale1
            gx_ref[rows, :] = dx1_ref[rows, :] + _rms_bwd(dh * (scale1 * g), xn, rs)

        @pl.when(i == nt - 1)
        def _():
            copies = _xchg_copies(x_ins, x_outs, xssem, xrsem)
            for cp in copies:
                cp.wait_recv()
            for cp in copies:
                cp.wait_send()

    tile = pl.BlockSpec((tm, D_MODEL), lambda i: (i, 0))
    hbm = pl.BlockSpec(memory_space=pl.ANY)
    outs = pl.pallas_call(
        body, name="mix_in_bwd", grid=(nt,),
        in_specs=[pl.BlockSpec((tm, IN_COLS), lambda i: (i, 0)), _full((N_CHIPS, D_MODEL, nb)), tile, tile,
                  _full((6, D_MODEL)), _full((1, D_MODEL))] + [hbm] * nx,
        out_specs=[tile, _full((8, D_MODEL))] + [hbm] * nx,
        out_shape=[_big((T, D_MODEL), F32), jax.ShapeDtypeStruct((8, D_MODEL), F32)]
        + [_big(pb.shape, BF16) for pb in pairs_b],
        scratch_shapes=_xchg_sems(nx),
        compiler_params=_cp(("arbitrary",), 48),
    )(*_hbm(dp, w_in_g, x, dx1), modr, g_pre, *[_hbm(pb) for pb in pairs_b])
    return outs[:2], outs[2:]


def _weight_grad(a, b, a_blocked, b_blocked, name):
    T = a.shape[0]
    tt = min(GRAD_TILE, T)
    nt = T // tt
    ka = a.shape[1] // N_CHIPS if a_blocked else a.shape[1]
    nb = b.shape[1] // N_CHIPS if b_blocked else b.shape[1]

    def body(a_ref, b_ref, o_ref, ob_ref):
        t = pl.program_id(1)

        @pl.when(t == 0)
        def _():
            o_ref[...] = jnp.zeros(o_ref.shape, F32)

        for cols in _row_chains(nb):
            o_ref[0, :, cols] += _dot_tn(a_ref[...], b_ref[:, cols])

        @pl.when(t == nt - 1)
        def _():
            ob_ref[0] = o_ref[0].astype(BF16)

    blk = pl.BlockSpec((1, ka, nb), lambda j, t: (j, 0, 0))
    return pl.pallas_call(
        body, name=name, grid=(N_CHIPS, nt),
        in_specs=[pl.BlockSpec((tt, ka), (lambda j, t: (t, j)) if a_blocked else (lambda j, t: (t, 0))),
                  pl.BlockSpec((tt, nb), (lambda j, t: (t, j)) if b_blocked else (lambda j, t: (t, 0)))],
        out_specs=[blk, blk],
        out_shape=[_big((N_CHIPS, ka, nb), F32), _big((N_CHIPS, ka, nb), BF16)],
        compiler_params=_cp(("arbitrary", "arbitrary"), 48),
    )(*_hbm(a, b))


R_LOSS = 0
R_FFN = 8
R_OUT = 16
R_IN = 24
R_BIN = 32
R_512 = 40
R_DW = 48
N_STAT_ROWS = 80
MOD_ROWS = (R_IN + 0, R_IN + 1, R_OUT + 0, R_FFN + 0, R_FFN + 1, R_LOSS + 1)


def _small_update(gath, params):
    names = ["b_ada", "lb_logits", "g_pre_mix", "b_in", "b_dw", "gn_gain", "gn_bias", "g_hgrn_out", "g_post_mix",
             "g_pre_ffn", "g_post_ffn"]
    flat = []
    for n in names:
        flat += list(params[n])
    n_in = 1 + len(flat)

    def body(*refs):
        g_ref = refs[0]
        prm = {n: refs[1 + 3 * k:4 + 3 * k] for k, n in enumerate(names)}
        outs = refs[n_in:]
        loss_ref, dmod_ref, dwdw_ref = outs[0], outs[1], outs[2]
        res = {n: outs[3 + 4 * k:7 + 4 * k] for k, n in enumerate(names)}
        red = g_ref[0]
        for dev in range(1, N_DEV):
            red = red + g_ref[dev]
        loss_ref[...] = jnp.broadcast_to(
            (0.5 / D_MODEL) * jnp.sum(red[R_LOSS:R_LOSS + 1, :], axis=-1, keepdims=True), loss_ref.shape)
        for dev in range(N_DEV):
            for k, r in enumerate(MOD_ROWS):
                dmod_ref[dev:dev + 1, k * D_MODEL:(k + 1) * D_MODEL] = g_ref[dev, r:r + 1, :]
        dwdw_ref[...] = red[R_DW:R_DW + HALO, 0:CONV_CH]

        def finish(name, pieces):
            w_ref, m_ref, v_ref = prm[name]
            g_out, d_out, m_out, v_out = res[name]
            for rsl, lsl, g in pieces:
                d, m2, v2 = _adam_math(w_ref[rsl, lsl], g, m_ref[rsl, lsl], v_ref[rsl, lsl])
                g_out[rsl, lsl] = g
                d_out[rsl, lsl] = d
                m_out[rsl, lsl] = m2
                v_out[rsl, lsl] = v2

        one = slice(0, 1)
        row = lambda r: red[r:r + 1, :]
        half = lambda r: red[r:r + 1, 0:CONV_CH]
        finish("b_ada", [(one, slice(k * D_MODEL, (k + 1) * D_MODEL), row(r)) for k, r in enumerate(MOD_ROWS)])
        finish("b_in", [(one, slice(k * D_MODEL, (k + 1) * D_MODEL), row(R_BIN + k)) for k in range(3)])
        finish("g_pre_mix", [(one, slice(None), row(R_IN + 2))])
        finish("g_post_mix", [(one, slice(None), row(R_OUT + 1))])
        finish("g_pre_ffn", [(one, slice(None), row(R_FFN + 2))])
        finish("g_post_ffn", [(one, slice(None), row(R_LOSS + 2))])
        finish("b_dw", [(one, slice(None), half(R_512 + 0))])
        finish("gn_gain", [(one, slice(None), half(R_512 + 1))])
        finish("gn_bias", [(one, slice(None), half(R_512 + 2))])
        finish("g_hgrn_out", [(one, slice(None), half(R_512 + 3))])
        s0, s1 = _lower_bound(prm["lb_logits"][0])
        dlb = half(R_512 + 4)
        finish("lb_logits", [(slice(0, 1), slice(None), dlb * s0 * (1.0 - s0)),
                             (slice(1, 2), slice(None), -dlb * s0 * s1)])

    vm = pl.BlockSpec(memory_space=pltpu.VMEM)
    out_shape = [jax.ShapeDtypeStruct((8, 128), F32), jax.ShapeDtypeStruct((N_DEV, 6 * D_MODEL), F32),
                 jax.ShapeDtypeStruct((HALO, CONV_CH), F32)]
    for n in names:
        out_shape += [jax.ShapeDtypeStruct(params[n][0].shape, F32)] * 4
    outs = pl.pallas_call(
        body, name="small_update", out_shape=out_shape,
        in_specs=[vm] * n_in, out_specs=[vm] * len(out_shape),
        compiler_params=_cp(None, 32),
    )(gath, *flat)
    return outs[0], outs[1], outs[2], {n: outs[3 + 4 * k:7 + 4 * k] for k, n in enumerate(names)}


def _wdw_adam(w, g, m, v):
    def body(w_ref, g_ref, m_ref, v_ref, d_out, m_out, v_out):
        d, m2, v2 = _adam_math(w_ref[...], g_ref[...], m_ref[...], v_ref[...])
        d_out[...] = d
        m_out[...] = m2
        v_out[...] = v2

    vm = pl.BlockSpec(memory_space=pltpu.VMEM)
    return pl.pallas_call(
        body, name="wdw_adam", out_shape=[jax.ShapeDtypeStruct(w.shape, F32)] * 3,
        in_specs=[vm] * 4, out_specs=[vm] * 3, compiler_params=_cp(None, 16),
    )(w, g, m, v)


def kernel(x, c, w_ada, b_ada, lb_logits, g_pre_mix, w_in, b_in, w_dw, b_dw, gn_gain, gn_bias, g_hgrn_out, w_out, g_post_mix, g_pre_ffn, w_up, w_down, g_post_ffn, loss_target, m_w_ada, m_b_ada, m_lb_logits, m_g_pre_mix, m_w_in, m_b_in, m_w_dw, m_b_dw, m_gn_gain, m_gn_bias, m_g_hgrn_out, m_w_out, m_g_post_mix, m_g_pre_ffn, m_w_up, m_w_down, m_g_post_ffn, v_w_ada, v_b_ada, v_lb_logits, v_g_pre_mix, v_w_in, v_b_in, v_w_dw, v_b_dw, v_gn_gain, v_gn_bias, v_g_hgrn_out, v_w_out, v_g_post_mix, v_g_pre_ffn, v_w_up, v_w_down, v_g_post_ffn):
    ax, ay, ac = lax.axis_index("x"), lax.axis_index("y"), lax.axis_index("c")
    chip = 2 * ax + ay
    T = x.shape[1]
    xs, tgt = x[0], loss_target[0]
    ada_cols = w_ada.shape[2]

    b_sh = lax.dynamic_slice_in_dim(b_ada, chip * ada_cols, ada_cols, axis=1)
    wdw_pad = jnp.pad(w_dw[0], ((0, HALO - CONV_K), (0, 0)))
    chip1 = jnp.reshape(chip, (1,)).astype(jnp.int32)
    place = jnp.stack([ac, chip]).astype(jnp.int32)
    _, c8, modg, wdwg = _ada_exchange(c, w_ada[0], b_sh, wdw_pad)
    modr = modg.reshape(6, D_MODEL)
    wdw_all = jnp.transpose(wdwg, (1, 0, 2)).reshape(HALO, CONV_CH)
    vec = (b_dw, gn_gain, gn_bias, g_hgrn_out)

    p, h1, (up_buf, down_buf), w_in_g, w_out_g = _mix_in_fwd(
        chip1, xs, modr, g_pre_mix, b_in.reshape(N_CHIPS, 1, IN_COLS // N_CHIPS), [w_up[0], w_down[0]],
        _cast_own(chip1, w_in[0], "cast_w_in"), _cast_own(chip1, w_out[0], "cast_w_out"))
    w_out_f = w_out_g.reshape(D_MODEL, D_MODEL)
    (cat, ys, o, states, y, x1, h2), (w_up_g, w_down_g) = _mixers_fwd(
        p, wdw_all, vec, lb_logits, w_out_f, xs, modr, g_post_mix, g_pre_ffn, [up_buf, down_buf])
    w_down_f = w_down_g.reshape(D_FF, D_MODEL)
    r, dy2, dx2, st_loss = _ffn_fwd(h2, w_up_g, w_down_f, x1, tgt, modr, g_post_ffn)

    def pair_sums(grads, got, tags):
        return [_pair_sum(place, g, o_, "pair_sum_" + t) for (g, _), o_, t in zip(grads, got, tags)]

    da, dx1, st_ffn = _ffn_bwd(dy2, r, x1, dx2, w_up_g, w_down_f, modr, g_pre_ffn)
    g_up = _weight_grad(h2, da, False, True, "grad_w_up")
    g_down = _weight_grad(r, dy2, True, False, "grad_w_down")
    (dcat, st_out, g_out), got_ud = _mix_out_bwd(dx1, y, cat, w_out_f, modr, g_post_mix, [g_up[1], g_down[1]])
    g_out = g_out.reshape(N_CHIPS, D_MODEL // N_CHIPS, D_MODEL)
    got_o = _pair_swap([g_out], "pair_swap_w_out")
    early = pair_sums([(g_out, None), g_up, g_down], list(got_o) + list(got_ud), ["w_out", "w_up", "w_down"])
    (dp, st_bin, st_512, dwdw), got_early, g_in = _mixers_bwd(p, dcat, ys, o, states, h1, wdw_all, vec, lb_logits,
                                                              [pb for _, pb in early])
    late = pair_sums([g_in], _pair_swap([g_in[1]], "pair_swap_w_in"), ["w_in"])
    (grad_x, st_in), got_late = _mix_in_bwd(dp, w_in_g, xs, dx1, modr, g_pre_mix, [late[0][1]])
    fulls = [_chip_sum(place, pf, gb, "chip_sum_" + t)
             for (pf, _), gb, t in zip(late + early, list(got_late) + list(got_early), ["w_in", "w_out", "w_up", "w_down"])]

    pad_lanes = lambda s: jnp.pad(s, ((0, 0), (0, D_MODEL - s.shape[1])))
    stats = jnp.concatenate([st_loss, st_ffn, st_out, st_in, st_bin, pad_lanes(st_512), pad_lanes(dwdw)], axis=0)
    (g_w_in, g_w_out, g_w_up, g_w_down), gath = _final_exchange(fulls, stats)
    small = {"b_ada": (b_ada, m_b_ada, v_b_ada), "lb_logits": (lb_logits, m_lb_logits, v_lb_logits),
             "g_pre_mix": (g_pre_mix, m_g_pre_mix, v_g_pre_mix), "b_in": (b_in, m_b_in, v_b_in),
             "b_dw": (b_dw, m_b_dw, v_b_dw), "gn_gain": (gn_gain, m_gn_gain, v_gn_gain),
             "gn_bias": (gn_bias, m_gn_bias, v_gn_bias), "g_hgrn_out": (g_hgrn_out, m_g_hgrn_out, v_g_hgrn_out),
             "g_post_mix": (g_post_mix, m_g_post_mix, v_g_post_mix), "g_pre_ffn": (g_pre_ffn, m_g_pre_ffn, v_g_pre_ffn),
             "g_post_ffn": (g_post_ffn, m_g_post_ffn, v_g_post_ffn)}
    loss_t, dmod_all, dwdw_sum, sres = _small_update(gath, small)
    loss = loss_t[0, 0]

    res = dict(sres)
    dmod_sh = lax.dynamic_slice_in_dim(dmod_all, chip * ada_cols, ada_cols, axis=1)
    res["w_ada"] = [t[None] for t in _ada_grad_adam(jnp.transpose(c8), dmod_sh, w_ada[0], m_w_ada[0], v_w_ada[0])]
    g_wdw = lax.dynamic_slice_in_dim(dwdw_sum, chip * HEAD_D, HEAD_D, axis=1)[:CONV_K][None]
    res["w_dw"] = [g_wdw] + list(_wdw_adam(w_dw, g_wdw, m_w_dw, v_w_dw))
    for name, g, w, m, v in (("w_in", g_w_in, w_in, m_w_in, v_w_in), ("w_out", g_w_out, w_out, m_w_out, v_w_out),
                             ("w_up", g_w_up, w_up, m_w_up, v_w_up), ("w_down", g_w_down, w_down, m_w_down, v_w_down)):
        d, m2, v2 = _adam_big(w[0], g, m[0], v[0], "adam_" + name)
        res[name] = [g[None], d[None], m2[None], v2[None]]

    order = ["w_ada", "b_ada", "lb_logits", "g_pre_mix", "w_in", "b_in", "w_dw", "b_dw", "gn_gain", "gn_bias",
             "g_hgrn_out", "w_out", "g_post_mix", "g_pre_ffn", "w_up", "w_down", "g_post_ffn"]
    out = [loss, grad_x[None]]
    for k in range(4):
        out += [res[n][k] for n in order]
    return tuple(out)
```

```python
import jax
import jax.numpy as jnp
from jax import lax
from jax.experimental import pallas as pl
from jax.experimental.pallas import tpu as pltpu

F32, BF16 = jnp.float32, jnp.bfloat16
D_MODEL = 1024
CONV_CH = 512
HGRN_W = 512
N_HEADS = 4
HEAD_D = 128
CONV_K = 31
GN_GROUP = 64
GN_SHIFT = 6
IN_COLS = 3072
D_FF = 4096
CHUNK = 64
CHUNK_SHIFT = 6
N_CHIPS = 4
N_DEV = 8
RMS_EPS = 1e-6
GN_EPS = 1e-5
ADAM_LR, ADAM_B1, ADAM_B2, ADAM_EPS, ADAM_WD, ADAM_STEP = 0.001, 0.9, 0.999, 1e-08, 0.01, 10
TOK_TILE = 512
MIXIN_TILE = 1024
MIXB_TILE = 256
FFN_TILE = 1024
FFN_BLOCK = 512
GRAD_TILE = 4096
HALO = 32
SUB = 8
LANE = 128
CONV_ROWS = 128
CHUNK_UNROLL = 8
MIB = 1 << 20
MESH = pl.DeviceIdType.MESH
OTHER_CHIPS = ((0, 1), (1, 0), (1, 1))


def _cp(sem=None, vmem_mib=48):
    return pltpu.CompilerParams(dimension_semantics=sem, vmem_limit_bytes=vmem_mib * MIB)


def _dot(a, b):
    return jnp.dot(a, b, preferred_element_type=F32)


def _dot_nt(a, b):
    return lax.dot_general(a, b, (((1,), (1,)), ((), ())), preferred_element_type=F32)


def _dot_tn(a, b):
    return lax.dot_general(a, b, (((0,), (0,)), ((), ())), preferred_element_type=F32)


def _sig(v):
    return 0.5 * jnp.tanh(0.5 * v) + 0.5


def _colsum(v):
    return jnp.sum(v, axis=0, keepdims=True)


def _flip(v, b):
    return 1 - v if b else v


def _rcopy(src, dst, ssem, rsem, dev):
    return pltpu.make_async_remote_copy(src_ref=src, dst_ref=dst, send_sem=ssem, recv_sem=rsem,
                                        device_id=dev, device_id_type=MESH)


def _place():
    return lax.axis_index("x"), lax.axis_index("y"), lax.axis_index("c")


def _full(shape):
    return pl.BlockSpec(shape, lambda *_: (0,) * len(shape))


def _big(shape, dtype):
    return pltpu.HBM(shape, dtype)


def _hbm(*arrays):
    out = [pltpu.with_memory_space_constraint(a, pltpu.HBM) for a in arrays]
    return out[0] if len(out) == 1 else out


def _split2(v):
    hi = v.astype(BF16)
    lo = (v - hi.astype(F32)).astype(BF16)
    return hi, lo


def _split3(v):
    h1 = v.astype(BF16)
    r1 = v - h1.astype(F32)
    h2 = r1.astype(BF16)
    h3 = (r1 - h2.astype(F32)).astype(BF16)
    return h1, h2, h3


def _mm3(mat, v):
    h1, h2, h3 = _split3(v)
    return _dot(mat, h1) + _dot(mat, h2) + _dot(mat, h3)


def _gn_matrix():
    r = lax.broadcasted_iota(jnp.int32, (CONV_CH, CONV_CH), 0) >> GN_SHIFT
    c = lax.broadcasted_iota(jnp.int32, (CONV_CH, CONV_CH), 1) >> GN_SHIFT
    return jnp.where(r == c, 1.0 / GN_GROUP, 0.0).astype(BF16)


def _gmean(v, gmat):
    hi, lo = _split2(v)
    return _dot(hi, gmat) + _dot(lo, gmat)


def _chunk_masks(tm):
    r = lax.broadcasted_iota(jnp.int32, (tm, tm), 0)
    c = lax.broadcasted_iota(jnp.int32, (tm, tm), 1)
    same = (r >> CHUNK_SHIFT) == (c >> CHUNK_SHIFT)
    one = lambda m: jnp.where(m, 1.0, 0.0).astype(BF16)
    return one(same & (c <= r)), one(same & (c >= r)), one(same)


def _tri():
    return lax.broadcasted_iota(jnp.int32, (CHUNK, CHUNK), 0) >= lax.broadcasted_iota(jnp.int32, (CHUNK, CHUNK), 1)


def _lower_bound(lbl_ref):
    l0, l1 = lbl_ref[0:1, :], lbl_ref[1:2, :]
    mx = jnp.maximum(l0, l1)
    e0, e1 = jnp.exp(l0 - mx), jnp.exp(l1 - mx)
    return e0 / (e0 + e1), e1 / (e0 + e1)


CONV_FWD_TAPS = tuple((j, HALO - (CONV_K - 1) + j) for j in range(CONV_K))
CONV_BWD_TAPS = tuple((j, (CONV_K - 1) - j) for j in range(CONV_K))


def _tap_conv(src_ref, w_ref, row0, taps, lanes):
    acc = None
    for b in range(SUB):
        pb = None
        for j, off in taps:
            if off % SUB == b:
                lo = row0 + off - b
                term = w_ref[j:j + 1, lanes] * src_ref[lo:lo + CONV_ROWS + SUB, lanes]
                pb = term if pb is None else pb + term
        if pb is not None:
            sh = pb[b:b + CONV_ROWS, :]
            acc = sh if acc is None else acc + sh
    return acc


def _hgrn_prep(pq, pf, lb, lower):
    sq = _sig(pq)
    qf = pq * sq
    sf = _sig(pf)
    f = lb + (1.0 - lb) * sf
    logf = jnp.log(f)
    k = 1.0 - f
    G = _mm3(lower, logf)
    rows, cols = G.shape
    g3 = G.reshape(rows // CHUNK, CHUNK, cols)
    Gl = jnp.broadcast_to(g3[:, CHUNK - 1:CHUNK, :], g3.shape).reshape(rows, cols)
    eG, enG, eGlG = jnp.exp(G), jnp.exp(-G), jnp.exp(Gl - G)
    return dict(sq=sq, sf=sf, f=f, Gl=Gl, eG=eG, enG=enG, eGlG=eGlG, qt=qf * eG, kt=k * enG, kh=k * eGlG)


def _ada_exchange(c_row, w_ada, b_sh, wdw_pad):
    ncol = w_ada.shape[1]

    def body(c_ref, w_ref, b_ref, wdw_ref, call_ref, c8_ref, modg_ref, wdwg_ref, rows_s, sa, ra, sw, rw, sm, rm):
        x, y, c = _place()
        me = 4 * x + 2 * y + c
        chip = 2 * x + y
        cv = c_ref[...]
        call_ref[me] = cv * _sig(cv)
        wdwg_ref[chip] = wdw_ref[...]
        sends = []
        for m in range(1, N_DEV):
            peer = (_flip(x, m >> 2), _flip(y, (m >> 1) & 1), _flip(c, m & 1))
            cp = _rcopy(call_ref.at[me], call_ref.at[me], sa.at[m - 1], ra.at[m - 1], peer)
            cp.start()
            sends.append(cp)
        for k, (fx, fy) in enumerate(OTHER_CHIPS):
            peer = (_flip(x, fx), _flip(y, fy), c)
            cp = _rcopy(wdwg_ref.at[chip], wdwg_ref.at[chip], sw.at[k], rw.at[k], peer)
            cp.start()
            sends.append(cp)
        for m in range(1, N_DEV):
            peer = (_flip(x, m >> 2), _flip(y, (m >> 1) & 1), _flip(c, m & 1))
            pid = 4 * peer[0] + 2 * peer[1] + peer[2]
            _rcopy(call_ref.at[pid], call_ref.at[pid], sa.at[m - 1], ra.at[m - 1], peer).wait_recv()
        for b in range(N_DEV):
            c8_ref[b:b + 1, :] = call_ref[b]
        mod_all = _dot(c8_ref[...].astype(BF16), w_ref[...].astype(BF16)) + b_ref[...]
        for b in range(N_DEV):
            rows_s[b] = mod_all[b:b + 1, :]
        modg_ref[chip] = rows_s[me]
        for k, (fx, fy) in enumerate(OTHER_CHIPS):
            peer = (_flip(x, fx), _flip(y, fy), c)
            pid = 4 * peer[0] + 2 * peer[1] + peer[2]
            cp = _rcopy(rows_s.at[pid], modg_ref.at[chip], sm.at[k], rm.at[k], peer)
            cp.start()
            sends.append(cp)
        for k, (fx, fy) in enumerate(OTHER_CHIPS):
            peer = (_flip(x, fx), _flip(y, fy), c)
            pchip = 2 * peer[0] + peer[1]
            _rcopy(rows_s.at[0], modg_ref.at[pchip], sm.at[k], rm.at[k], peer).wait_recv()
            _rcopy(wdwg_ref.at[pchip], wdwg_ref.at[pchip], sw.at[k], rw.at[k], peer).wait_recv()
        for cp in sends:
            cp.wait_send()

    vm = pl.BlockSpec(memory_space=pltpu.VMEM)
    return pl.pallas_call(
        body, name="ada_exchange",
        out_shape=[jax.ShapeDtypeStruct((N_DEV, 1, D_MODEL), F32), jax.ShapeDtypeStruct((N_DEV, D_MODEL), F32),
                   jax.ShapeDtypeStruct((N_CHIPS, 1, ncol), F32), jax.ShapeDtypeStruct((N_CHIPS, HALO, HEAD_D), F32)],
        in_specs=[vm] * 4, out_specs=[vm] * 4,
        scratch_shapes=[pltpu.VMEM((N_DEV, 1, ncol), F32),
                        pltpu.SemaphoreType.DMA((N_DEV - 1,)), pltpu.SemaphoreType.DMA((N_DEV - 1,)),
                        pltpu.SemaphoreType.DMA((3,)), pltpu.SemaphoreType.DMA((3,)),
                        pltpu.SemaphoreType.DMA((3,)), pltpu.SemaphoreType.DMA((3,))],
        compiler_params=_cp(None, 32),
    )(c_row, w_ada, b_sh, wdw_pad)


def _cast_own(chip1, shard, name):
    rows, cols = shard.shape
    tr = _row_tile(rows)

    def body(ch_ref, s_ref, o_ref):
        o_ref[0] = s_ref[...].astype(BF16)

    gs = pltpu.PrefetchScalarGridSpec(
        num_scalar_prefetch=1, grid=(rows // tr,),
        in_specs=[pl.BlockSpec((tr, cols), lambda i, ch: (i, 0))],
        out_specs=pl.BlockSpec((1, tr, cols), lambda i, ch: (ch[0], i, 0)))
    return pl.pallas_call(
        body, name=name, grid_spec=gs, out_shape=_big((N_CHIPS, rows, cols), BF16),
        compiler_params=_cp(("arbitrary",), 32),
    )(chip1, _hbm(shard))


def _slab(buf, ch, core):
    hs = buf.shape[1] // 2
    return buf.at[ch, pl.ds(core * hs, hs), :]


def _gather_start(bufs, ssem, rsem, relations=(0, 1, 2)):
    x, y, c = _place()
    chip = 2 * x + y
    for k in relations:
        fx, fy = OTHER_CHIPS[k]
        peer = (_flip(x, fx), _flip(y, fy), c)
        for t, buf in enumerate(bufs):
            _rcopy(_slab(buf, chip, c), _slab(buf, chip, c), ssem.at[t * 3 + k], rsem.at[t * 3 + k], peer).start()


def _gather_pass_on(bufs, ssem, rsem):
    nt = len(bufs)
    x, y, c = _place()
    sibling = (x, y, 1 - c)
    for k, (fx, fy) in enumerate(OTHER_CHIPS):
        peer = (_flip(x, fx), _flip(y, fy), c)
        pchip = 2 * peer[0] + peer[1]
        for t, buf in enumerate(bufs):
            _rcopy(_slab(buf, pchip, c), _slab(buf, pchip, c), ssem.at[t * 3 + k], rsem.at[t * 3 + k], peer).wait_recv()
            _rcopy(_slab(buf, pchip, c), _slab(buf, pchip, c), ssem.at[3 * nt + t * 3 + k],
                   rsem.at[3 * nt + t * 3 + k], sibling).start()


def _gather_drain(bufs, ssem, rsem):
    nt = len(bufs)
    x, y, c = _place()
    chip = 2 * x + y
    sibling = (x, y, 1 - c)
    for k, (fx, fy) in enumerate(OTHER_CHIPS):
        peer = (_flip(x, fx), _flip(y, fy), c)
        pchip = 2 * peer[0] + peer[1]
        for t, buf in enumerate(bufs):
            _rcopy(_slab(buf, pchip, 1 - c), _slab(buf, pchip, 1 - c), ssem.at[3 * nt + t * 3 + k],
                   rsem.at[3 * nt + t * 3 + k], sibling).wait_recv()
            _rcopy(_slab(buf, chip, c), _slab(buf, chip, c), ssem.at[t * 3 + k], rsem.at[t * 3 + k], peer).wait_send()
            _rcopy(_slab(buf, pchip, c), _slab(buf, pchip, c), ssem.at[3 * nt + t * 3 + k],
                   rsem.at[3 * nt + t * 3 + k], sibling).wait_send()


def _gather_finish(bufs, ssem, rsem):
    _gather_pass_on(bufs, ssem, rsem)
    _gather_drain(bufs, ssem, rsem)


def _gather_arrive(bufs, k, ssem, rsem):
    nt = len(bufs)
    x, y, c = _place()
    fx, fy = OTHER_CHIPS[k]
    peer = (_flip(x, fx), _flip(y, fy), c)
    pchip = 2 * peer[0] + peer[1]
    for t, buf in enumerate(bufs):
        _rcopy(_slab(buf, pchip, c), _slab(buf, pchip, c), ssem.at[t * 3 + k], rsem.at[t * 3 + k], peer).wait_recv()
        _rcopy(_slab(buf, pchip, c), _slab(buf, pchip, c), ssem.at[3 * nt + t * 3 + k],
               rsem.at[3 * nt + t * 3 + k], (x, y, 1 - c)).start()
    for t, buf in enumerate(bufs):
        _rcopy(_slab(buf, pchip, 1 - c), _slab(buf, pchip, 1 - c), ssem.at[3 * nt + t * 3 + k],
               rsem.at[3 * nt + t * 3 + k], (x, y, 1 - c)).wait_recv()


def _gather_sends_done(bufs, ssem, rsem):
    nt = len(bufs)
    x, y, c = _place()
    chip = 2 * x + y
    for k, (fx, fy) in enumerate(OTHER_CHIPS):
        peer = (_flip(x, fx), _flip(y, fy), c)
        pchip = 2 * peer[0] + peer[1]
        for t, buf in enumerate(bufs):
            _rcopy(_slab(buf, chip, c), _slab(buf, chip, c), ssem.at[t * 3 + k], rsem.at[t * 3 + k], peer).wait_send()
            _rcopy(_slab(buf, pchip, c), _slab(buf, pchip, c), ssem.at[3 * nt + t * 3 + k],
                   rsem.at[3 * nt + t * 3 + k], (x, y, 1 - c)).wait_send()


def _ring_parts(buf):
    x, y, c = _place()
    ynb, xnb = (x, 1 - y, c), (1 - x, y, c)
    ychip, xchip, dchip = 2 * x + (1 - y), 2 * (1 - x) + y, 2 * (1 - x) + (1 - y)
    hs = buf.shape[1] // 2

    def piece(ch, q):
        return buf.at[ch, pl.ds(c * hs + q * (hs // 2), hs // 2), :]

    return ynb, xnb, ychip, xchip, dchip, piece


def _ring_start(bufs, ssem, rsem):
    x, y, c = _place()
    chip = 2 * x + y
    for t, buf in enumerate(bufs):
        ynb, xnb, _, _, _, _ = _ring_parts(buf)
        _rcopy(_slab(buf, chip, c), _slab(buf, chip, c), ssem.at[2 * t], rsem.at[2 * t], ynb).start()
        _rcopy(_slab(buf, chip, c), _slab(buf, chip, c), ssem.at[2 * t + 1], rsem.at[2 * t + 1], xnb).start()


def _ring_forward(bufs, ssem, rsem):
    nt = len(bufs)
    _, _, c = _place()
    for t, buf in enumerate(bufs):
        ynb, xnb, ychip, xchip, _, piece = _ring_parts(buf)
        _rcopy(_slab(buf, ychip, c), _slab(buf, ychip, c), ssem.at[2 * t], rsem.at[2 * t], ynb).wait_recv()
        _rcopy(piece(ychip, 0), piece(ychip, 0), ssem.at[2 * nt + 2 * t], rsem.at[2 * nt + 2 * t], xnb).start()
        _rcopy(_slab(buf, xchip, c), _slab(buf, xchip, c), ssem.at[2 * t + 1], rsem.at[2 * t + 1], xnb).wait_recv()
        _rcopy(piece(xchip, 1), piece(xchip, 1), ssem.at[2 * nt + 2 * t + 1], rsem.at[2 * nt + 2 * t + 1], ynb).start()


def _ring_finish(bufs, ssem, rsem):
    nt = len(bufs)
    x, y, c = _place()
    chip = 2 * x + y
    sibling = (x, y, 1 - c)
    for t, buf in enumerate(bufs):
        ynb, xnb, ychip, xchip, dchip, piece = _ring_parts(buf)
        _rcopy(piece(dchip, 0), piece(dchip, 0), ssem.at[2 * nt + 2 * t], rsem.at[2 * nt + 2 * t], xnb).wait_recv()
        _rcopy(piece(dchip, 1), piece(dchip, 1), ssem.at[2 * nt + 2 * t + 1], rsem.at[2 * nt + 2 * t + 1],
               ynb).wait_recv()
        for k, ch in enumerate((ychip, xchip, dchip)):
            _rcopy(_slab(buf, ch, c), _slab(buf, ch, c), ssem.at[4 * nt + 3 * t + k], rsem.at[4 * nt + 3 * t + k],
                   sibling).start()
    for t, buf in enumerate(bufs):
        ynb, xnb, ychip, xchip, dchip, piece = _ring_parts(buf)
        for k, ch in enumerate((ychip, xchip, dchip)):
            _rcopy(_slab(buf, ch, 1 - c), _slab(buf, ch, 1 - c), ssem.at[4 * nt + 3 * t + k],
                   rsem.at[4 * nt + 3 * t + k], sibling).wait_recv()
            _rcopy(_slab(buf, ch, c), _slab(buf, ch, c), ssem.at[4 * nt + 3 * t + k], rsem.at[4 * nt + 3 * t + k],
                   sibling).wait_send()
        _rcopy(_slab(buf, chip, c), _slab(buf, chip, c), ssem.at[2 * t], rsem.at[2 * t], ynb).wait_send()
        _rcopy(_slab(buf, chip, c), _slab(buf, chip, c), ssem.at[2 * t + 1], rsem.at[2 * t + 1], xnb).wait_send()
        _rcopy(piece(ychip, 0), piece(ychip, 0), ssem.at[2 * nt + 2 * t], rsem.at[2 * nt + 2 * t], xnb).wait_send()
        _rcopy(piece(xchip, 1), piece(xchip, 1), ssem.at[2 * nt + 2 * t + 1], rsem.at[2 * nt + 2 * t + 1],
               ynb).wait_send()


def _ring_sems(nt):
    return [pltpu.SemaphoreType.DMA((7 * nt,)), pltpu.SemaphoreType.DMA((7 * nt,))]


def _gather_sems(nt):
    return [pltpu.SemaphoreType.DMA((6 * nt,)), pltpu.SemaphoreType.DMA((6 * nt,))]


def _pair_copies(ins, outs, ssem, rsem):
    x, y, c = _place()
    copies = []
    for t in range(len(ins)):
        hs = ins[t].shape[1] // 2
        copies.append(_rcopy(ins[t].at[:, pl.ds((1 - c) * hs, hs), :], outs[t], ssem.at[t], rsem.at[t], (x, y, 1 - c)))
    return copies


def _pair_shapes(grads):
    return [_big((g.shape[0], g.shape[1] // 2, g.shape[2]), g.dtype) for g in grads]


def _pair_sems(nt):
    return [pltpu.SemaphoreType.DMA((nt,)), pltpu.SemaphoreType.DMA((nt,))]


def _pair_swap(grads, name):
    nt = len(grads)
    hbm = pl.BlockSpec(memory_space=pl.ANY)

    def body(*refs):
        copies = _pair_copies(refs[:nt], refs[nt:2 * nt], refs[2 * nt], refs[2 * nt + 1])
        for cp in copies:
            cp.start()
        for cp in copies:
            cp.wait_recv()
        for cp in copies:
            cp.wait_send()

    return pl.pallas_call(
        body, name=name, out_shape=_pair_shapes(grads), in_specs=[hbm] * nt, out_specs=[hbm] * nt,
        scratch_shapes=_pair_sems(nt),
    )(*[_hbm(g) for g in grads])


def _xchg_copies(ins, outs, ssem, rsem):
    x, y, c = _place()
    copies = []
    for k, (fx, fy) in enumerate(OTHER_CHIPS):
        peer = (_flip(x, fx), _flip(y, fy), c)
        for t in range(len(ins)):
            copies.append(_rcopy(ins[t].at[k], outs[t].at[k], ssem.at[t * 3 + k], rsem.at[t * 3 + k], peer))
    return copies


def _xchg_sems(nt):
    return [pltpu.SemaphoreType.DMA((3 * nt,)), pltpu.SemaphoreType.DMA((3 * nt,))]


def _final_exchange(fulls, stats):
    nt = len(fulls)
    rows, cols = stats.shape
    hbm = pl.BlockSpec(memory_space=pl.ANY)
    vm = pl.BlockSpec(memory_space=pltpu.VMEM)

    def body(*refs):
        ins, s_ref = refs[:nt], refs[nt]
        outs, g_ref = refs[nt + 1:2 * nt + 1], refs[2 * nt + 1]
        hssem, hrsem, ssem, rsem = refs[2 * nt + 2:]
        x, y, c = _place()
        me, sibling = (x, y, c), (x, y, 1 - c)
        halves = []
        for t in range(nt):
            hs = ins[t].shape[0] // 2
            mine = pl.ds(c * hs, hs)
            cp = _rcopy(ins[t].at[mine, :], outs[t].at[mine, :], hssem.at[t], hrsem.at[t], sibling)
            cp.start()
            halves.append(cp)

        chips = [(_flip(x, fx), _flip(y, fy)) for fx, fy in OTHER_CHIPS]

        def blk(px, py, pc):
            return g_ref.at[4 * px + 2 * py + pc]

        def copy(k, block, to, src=None):
            return _rcopy(blk(*block) if src is None else src, blk(*block), ssem.at[k], rsem.at[k], to)

        g_ref[4 * x + 2 * y + c] = s_ref[...]
        first = [copy(0, me, sibling, src=s_ref)]
        first += [copy(1 + j, me, (*chip, c), src=s_ref) for j, chip in enumerate(chips)]
        for cp in first:
            cp.start()
        passed = [copy(4 + j, (*chip, c), sibling) for j, chip in enumerate(chips)]
        for j, chip in enumerate(chips):
            copy(1 + j, (*chip, c), me).wait_recv()
            passed[j].start()
        copy(0, sibling, me).wait_recv()
        for j, chip in enumerate(chips):
            copy(4 + j, (*chip, 1 - c), me).wait_recv()
        for t in range(nt):
            hs = ins[t].shape[0] // 2
            other = pl.ds((1 - c) * hs, hs)
            _rcopy(ins[t].at[other, :], outs[t].at[other, :], hssem.at[t], hrsem.at[t], sibling).wait_recv()
        for cp in first + passed + halves:
            cp.wait_send()

    outs = pl.pallas_call(
        body, name="final_exchange",
        out_shape=[_big(f.shape, F32) for f in fulls] + [jax.ShapeDtypeStruct((N_DEV, rows, cols), F32)],
        in_specs=[hbm] * nt + [vm], out_specs=[hbm] * nt + [vm],
        input_output_aliases={t: t for t in range(nt)},
        scratch_shapes=[pltpu.SemaphoreType.DMA((nt,)), pltpu.SemaphoreType.DMA((nt,)),
                        pltpu.SemaphoreType.DMA((7,)), pltpu.SemaphoreType.DMA((7,))],
        compiler_params=_cp(None, 32),
    )(*[_hbm(f) for f in fulls], stats)
    return outs[:nt], outs[nt]


def _row_tile(rows):
    return min(rows, 512)


def _pair_sum(place, grad, got, name):
    nb, hs, cols = got.shape
    tr = _row_tile(hs)
    nr = hs // tr

    def body(pl_ref, g_ref, o_ref, pf_ref, pb_ref):
        j = pl.program_id(1)
        s = g_ref[0] + o_ref[0].astype(F32)

        @pl.when(j == 0)
        def _():
            pf_ref[...] = s

        @pl.when(j > 0)
        def _():
            pb_ref[0] = s.astype(BF16)

    gs = pltpu.PrefetchScalarGridSpec(
        num_scalar_prefetch=1, grid=(nr, nb),
        in_specs=[pl.BlockSpec((1, tr, cols), lambda i, j, p: (p[1] ^ j, p[0] * nr + i, 0)),
                  pl.BlockSpec((1, tr, cols), lambda i, j, p: (p[1] ^ j, i, 0))],
        out_specs=[pl.BlockSpec((tr, cols), lambda i, j, p: (i, 0)),
                   pl.BlockSpec((1, tr, cols), lambda i, j, p: (jnp.maximum(j - 1, 0), i, 0))])
    return pl.pallas_call(
        body, name=name, grid_spec=gs,
        out_shape=[_big((hs, cols), F32), _big((nb - 1, hs, cols), BF16)],
        compiler_params=_cp(("arbitrary", "arbitrary"), 32),
    )(place, *_hbm(grad, got))


def _chip_sum(place, pair_f, got_b, name):
    nb, hs, cols = got_b.shape
    tr = _row_tile(hs)
    nr = hs // tr

    def body(pl_ref, pf_ref, gb_ref, o_ref):
        acc = pf_ref[...]
        for k in range(nb):
            acc = acc + gb_ref[k].astype(F32)
        o_ref[...] = acc

    gs = pltpu.PrefetchScalarGridSpec(
        num_scalar_prefetch=1, grid=(nr,),
        in_specs=[pl.BlockSpec((tr, cols), lambda i, p: (i, 0)),
                  pl.BlockSpec((nb, tr, cols), lambda i, p: (0, i, 0))],
        out_specs=pl.BlockSpec((tr, cols), lambda i, p: (p[0] * nr + i, 0)))
    return pl.pallas_call(
        body, name=name, grid_spec=gs,
        out_shape=_big((2 * hs, cols), F32),
        compiler_params=_cp(("arbitrary",), 32),
    )(place, *_hbm(pair_f, got_b))


def _adam_math(w, g, m, v):
    m2 = ADAM_B1 * m + (1.0 - ADAM_B1) * g
    v2 = ADAM_B2 * v + (1.0 - ADAM_B2) * (g * g)
    m_hat = m2 / (1.0 - ADAM_B1 ** ADAM_STEP)
    v_hat = v2 / (1.0 - ADAM_B2 ** ADAM_STEP)
    delta = -ADAM_LR * (m_hat / (jnp.sqrt(v_hat) + ADAM_EPS) + ADAM_WD * w)
    return delta, m2, v2


def _adam_big(w, g, m, v, name):
    rows, cols = w.shape
    tr = _row_tile(rows)

    def body(w_ref, g_ref, m_ref, v_ref, d_out, m_out, v_out):
        d, m2, v2 = _adam_math(w_ref[...], g_ref[...], m_ref[...], v_ref[...])
        d_out[...] = d
        m_out[...] = m2
        v_out[...] = v2

    spec = pl.BlockSpec((tr, cols), lambda i: (i, 0))
    return pl.pallas_call(
        body, name=name, grid=(rows // tr,), in_specs=[spec] * 4, out_specs=[spec] * 3,
        out_shape=[_big(w.shape, F32)] * 3,
        compiler_params=_cp(("arbitrary",), 48),
    )(*_hbm(w, g, m, v))


def _ada_grad_adam(c8t, dmod_sh, w, m, v):
    rows, cols = w.shape
    tr = _row_tile(rows) // 2

    def body(ct_ref, dm_ref, w_ref, m_ref, v_ref, g_out, d_out, m_out, v_out):
        g = None
        for b in range(N_DEV):
            term = ct_ref[:, b:b + 1] * dm_ref[b:b + 1, :]
            g = term if g is None else g + term
        d, m2, v2 = _adam_math(w_ref[...], g, m_ref[...], v_ref[...])
        g_out[...] = g
        d_out[...] = d
        m_out[...] = m2
        v_out[...] = v2

    spec = pl.BlockSpec((tr, cols), lambda i: (i, 0))
    return pl.pallas_call(
        body, name="ada_grad_adam", grid=(rows // tr,),
        in_specs=[pl.BlockSpec((tr, N_DEV), lambda i: (i, 0)), _full((N_DEV, cols)), spec, spec, spec],
        out_specs=[spec] * 4, out_shape=[_big(w.shape, F32)] * 4,
        compiler_params=_cp(("arbitrary",), 32),
    )(c8t, dmod_sh, *_hbm(w, m, v))


def _tok_tile(t):
    return min(TOK_TILE, t)


def _mix_in_fwd(chip1, x, modr, g_pre, b_in4, later, w_in_buf, w_out_buf):
    T = x.shape[0]
    tm = min(MIXIN_TILE, T)
    nt = T // tm
    nb = IN_COLS // N_CHIPS
    nl = len(later)

    def body(*refs):
        ch_ref, x_ref, mod_ref, g_ref, b_ref = refs[:5]
        l_ins = refs[5:5 + nl]
        p_ref, h_ref = refs[7 + nl:9 + nl]
        l_outs = refs[9 + nl:9 + 2 * nl]
        win_ref, wout_ref = refs[9 + 2 * nl:11 + 2 * nl]
        h_all, wblk, lsem, is_sem, ir_sem, os_sem, or_sem = refs[11 + 2 * nl:]
        k, i = pl.program_id(0), pl.program_id(1)
        chip = ch_ref[0]

        @pl.when(k == 0)
        def _():
            for src, dst in zip(l_ins, l_outs):
                dst[0] = src[...].astype(BF16)

        def load_block(blk):
            cp = pltpu.make_async_copy(win_ref.at[blk], wblk, lsem)
            cp.start()
            cp.wait()

        @pl.when((k == 0) & (i == 0))
        def _():
            _gather_start([win_ref], is_sem, ir_sem, relations=(0, 1))
            load_block(chip)

        for r in range(N_CHIPS - 1):
            @pl.when((k == r + 1) & (i == 0))
            def _(r=r):
                _gather_arrive([win_ref], r, is_sem, ir_sem)
                if r == 0:
                    _gather_start([win_ref], is_sem, ir_sem, relations=(2,))
                if r == 1:
                    _gather_start([wout_ref], os_sem, or_sem)
                load_block(chip ^ (r + 1))

        rows = pl.ds(pl.multiple_of(i * tm, tm), tm)

        @pl.when(k == 0)
        def _():
            xv = x_ref[...]
            rstd = lax.rsqrt(jnp.mean(xv * xv, axis=-1, keepdims=True) + RMS_EPS)
            h = (xv * rstd) * g_ref[...] * (1.0 + mod_ref[1:2, :]) + mod_ref[0:1, :]
            hb = h.astype(BF16)
            h_ref[...] = hb
            h_all[rows, :] = hb

        p_ref[...] = _dot(h_all[rows, :], wblk[...]) + b_ref[chip ^ k]

        @pl.when((k == N_CHIPS - 1) & (i == nt - 1))
        def _():
            _gather_sends_done([win_ref], is_sem, ir_sem)
            _gather_finish([wout_ref], os_sem, or_sem)

    hbm = pl.BlockSpec(memory_space=pl.ANY)
    first_pass = lambda k, i, ch: (jnp.where(k == 0, i, nt - 1), 0)
    own_slot = lambda k, i, ch: (ch[0], jnp.where(k == 0, i, nt - 1), 0)
    gs = pltpu.PrefetchScalarGridSpec(
        num_scalar_prefetch=1, grid=(N_CHIPS, nt),
        in_specs=[pl.BlockSpec((tm, D_MODEL), first_pass), pl.BlockSpec((6, D_MODEL), lambda k, i, ch: (0, 0)),
                  pl.BlockSpec((1, D_MODEL), lambda k, i, ch: (0, 0)),
                  pl.BlockSpec((N_CHIPS, 1, nb), lambda k, i, ch: (0, 0, 0))]
        + [pl.BlockSpec((w.shape[0] // nt, w.shape[1]), first_pass) for w in later] + [hbm, hbm],
        out_specs=[pl.BlockSpec((tm, nb), lambda k, i, ch: (i, ch[0] ^ k)), pl.BlockSpec((tm, D_MODEL), first_pass)]
        + [pl.BlockSpec((1, w.shape[0] // nt, w.shape[1]), own_slot) for w in later] + [hbm, hbm],
        scratch_shapes=[pltpu.VMEM((T, D_MODEL), BF16), pltpu.VMEM((D_MODEL, nb), BF16), pltpu.SemaphoreType.DMA]
        + _gather_sems(1) + _gather_sems(1))
    outs = pl.pallas_call(
        body, name="mix_in_fwd", grid_spec=gs,
        out_shape=[_big((T, IN_COLS), F32), _big((T, D_MODEL), BF16)]
        + [_big((N_CHIPS,) + w.shape, BF16) for w in later]
        + [_big(w_in_buf.shape, BF16), _big(w_out_buf.shape, BF16)],
        input_output_aliases={5 + nl: 2 + nl, 6 + nl: 3 + nl},
        compiler_params=_cp(("arbitrary", "arbitrary"), 48),
    )(chip1, _hbm(x), modr, g_pre, b_in4, *[_hbm(w) for w in later], _hbm(w_in_buf), _hbm(w_out_buf))
    return outs[0], outs[1], outs[2:2 + nl], outs[2 + nl], outs[3 + nl]


def _mixers_fwd(p, wdw, vecs, lbl, w_out, x, modr, g_post, g_ffn, gbufs):
    T = p.shape[0]
    tm = _tok_tile(T)
    nt = T // tm
    nch = tm // CHUNK
    ng = len(gbufs)
    n_in, n_out = 12, 7

    def body(*refs):
        (p_ref, wdw_ref, bdw_ref, gain_ref, bias_ref, gout_ref, lbl_ref, wout_ref, x_ref, mod_ref, gp_ref,
         gf_ref) = refs[:n_in]
        cat_ref, ys_ref, o_ref, st_ref, y_ref, x1_ref, h2_ref = refs[n_in + ng:n_in + ng + n_out]
        gout_bufs = refs[n_in + ng + n_out:n_in + 2 * ng + n_out]
        (ubuf, state, qt_s, kt_s, kh_s, v_s, egl_s, lower_s, gmat_s, gssem,
         grsem) = refs[n_in + 2 * ng + n_out:]
        i = pl.program_id(0)

        @pl.when(i == 0)
        def _():
            _ring_start(gout_bufs, gssem, grsem)
            lower_s[...], _, _ = _chunk_masks(tm)
            gmat_s[...] = _gn_matrix()
            state[...] = jnp.zeros(state.shape, F32)
            ubuf[0:HALO, :] = jnp.zeros((HALO, CONV_CH), F32)
            ubuf[HALO + tm:HALO + tm + SUB, :] = jnp.zeros((SUB, CONV_CH), F32)

        @pl.when(i > 0)
        def _():
            ubuf[0:HALO, :] = ubuf[tm:tm + HALO, :]

        ubuf[HALO:HALO + tm, :] = p_ref[:, 0:CONV_CH] * _sig(p_ref[:, CONV_CH:2 * CONV_CH])
        for r in range(tm // CONV_ROWS):
            rows = slice(r * CONV_ROWS, (r + 1) * CONV_ROWS)
            for lb_ in range(CONV_CH // LANE):
                lanes = slice(lb_ * LANE, (lb_ + 1) * LANE)
                ys_ref[rows, lanes] = bdw_ref[:, lanes] + _tap_conv(ubuf, wdw_ref, r * CONV_ROWS, CONV_FWD_TAPS, lanes)
        gmat = gmat_s[...]
        yv = ys_ref[...]
        d = yv - _gmean(yv, gmat)
        rs = lax.rsqrt(_gmean(d * d, gmat) + GN_EPS)
        z = d * rs * gain_ref[...] + bias_ref[...]
        cat_ref[:, 0:CONV_CH] = (z * _sig(z)).astype(BF16)

        lb, _ = _lower_bound(lbl_ref)
        o0 = 2 * CONV_CH
        pr = _hgrn_prep(p_ref[:, o0:o0 + HGRN_W], p_ref[:, o0 + HGRN_W:o0 + 2 * HGRN_W], lb, lower_s[...])
        qt_s[...] = pr["qt"].astype(BF16)
        kt_s[...] = pr["kt"].astype(BF16)
        kh_s[...] = pr["kh"].astype(BF16)
        v_s[...] = p_ref[:, o0 + 2 * HGRN_W:o0 + 3 * HGRN_W].astype(BF16)
        egl_s[...] = jnp.exp(pr["Gl"])
        tri = _tri()

        def chunk(ci, carry):
            r0 = pl.multiple_of(ci * CHUNK, CHUNK)
            rows = pl.ds(r0, CHUNK)
            for h in range(N_HEADS):
                ls = pl.ds(h * HEAD_D, HEAD_D)
                qc, kc, hc, vc = qt_s[rows, ls], kt_s[rows, ls], kh_s[rows, ls], v_s[rows, ls]
                s0 = state[h]
                s0b = s0.astype(BF16)
                st_ref[ci, h] = s0
                att =jnp.where(tri, _dot_nt(qc, kc), 0.0).astype(BF16)
                o_ref[rows, ls] = _dot(att, vc) + _dot_nt(qc, s0b)
                state[h] = s0 * egl_s[pl.ds(r0, 1), ls] + _dot_tn(vc, hc)
            return carry

        lax.fori_loop(0, nch, chunk, 0, unroll=min(CHUNK_UNROLL, nch))
        for h in range(N_HEADS):
            sl = slice(h * HEAD_D, (h + 1) * HEAD_D)
            oh = o_ref[:, sl]
            gh = p_ref[:, o0 + 3 * HGRN_W + h * HEAD_D:o0 + 3 * HGRN_W + (h + 1) * HEAD_D]
            rsh = lax.rsqrt(jnp.mean(oh * oh, axis=-1, keepdims=True) + RMS_EPS)
            hg = (oh * rsh) * gout_ref[:, sl] * (gh * _sig(gh))
            cat_ref[:, CONV_CH + h * HEAD_D:CONV_CH + (h + 1) * HEAD_D] = hg.astype(BF16)

        gate_gain = mod_ref[2:3, :] * gp_ref[...]
        gain_scale = gf_ref[...] * (1.0 + mod_ref[4:5, :])
        for rows in _row_chains(tm):
            yv = _dot(cat_ref[rows, :], wout_ref[...])
            y_ref[rows, :] = yv
            rsy = lax.rsqrt(jnp.mean(yv * yv, axis=-1, keepdims=True) + RMS_EPS)
            x1 = x_ref[rows, :] + (yv * rsy) * gate_gain
            x1_ref[rows, :] = x1
            rs1 = lax.rsqrt(jnp.mean(x1 * x1, axis=-1, keepdims=True) + RMS_EPS)
            h2_ref[rows, :] = ((x1 * rs1) * gain_scale + mod_ref[3:4, :]).astype(BF16)

        @pl.when(i == min(nt - 1, nt // 2 + 1))
        def _():
            _ring_forward(gout_bufs, gssem, grsem)

        @pl.when(i == nt - 1)
        def _():
            _ring_finish(gout_bufs, gssem, grsem)

    tile = lambda cols: pl.BlockSpec((tm, cols), lambda i: (i, 0))
    hbm = pl.BlockSpec(memory_space=pl.ANY)
    outs = pl.pallas_call(
        body, name="mixers_fwd", grid=(nt,),
        in_specs=[tile(IN_COLS), _full((HALO, CONV_CH))] + [_full((1, CONV_CH))] * 4 + [_full((2, HGRN_W))]
        + [_full((D_MODEL, D_MODEL)), tile(D_MODEL), _full((6, D_MODEL)), _full((1, D_MODEL)), _full((1, D_MODEL))]
        + [hbm] * ng,
        out_specs=[tile(D_MODEL), tile(CONV_CH), tile(HGRN_W),
                   pl.BlockSpec((nch, N_HEADS, HEAD_D, HEAD_D), lambda i: (i, 0, 0, 0)),
                   tile(D_MODEL), tile(D_MODEL), tile(D_MODEL)] + [hbm] * ng,
        out_shape=[_big((T, D_MODEL), BF16), _big((T, CONV_CH), F32), _big((T, HGRN_W), F32),
                   _big((T // CHUNK, N_HEADS, HEAD_D, HEAD_D), F32), _big((T, D_MODEL), F32),
                   _big((T, D_MODEL), F32), _big((T, D_MODEL), BF16)] + [_big(b.shape, BF16) for b in gbufs],
        input_output_aliases={n_in + t: n_out + t for t in range(ng)},
        scratch_shapes=[pltpu.VMEM((tm + HALO + SUB, CONV_CH), F32), pltpu.VMEM((N_HEADS, HEAD_D, HEAD_D), F32),
                        pltpu.VMEM((tm, HGRN_W), BF16), pltpu.VMEM((tm, HGRN_W), BF16),
                        pltpu.VMEM((tm, HGRN_W), BF16), pltpu.VMEM((tm, HGRN_W), BF16),
                        pltpu.VMEM((tm, HGRN_W), F32), pltpu.VMEM((tm, tm), BF16),
                        pltpu.VMEM((CONV_CH, CONV_CH), BF16)] + _ring_sems(ng),
        compiler_params=_cp(("arbitrary",), 56),
    )(_hbm(p), wdw, *vecs, lbl, *_hbm(w_out, x), modr, g_post, g_ffn, *[_hbm(b) for b in gbufs])
    return outs[:n_out], outs[n_out:]


def _row_chains(rows, n=2):
    step = rows // n
    return [slice(k * step, (k + 1) * step) for k in range(n)]


def _ffn_blocks():
    return D_FF // FFN_BLOCK, (D_FF // N_CHIPS) // FFN_BLOCK


def _ffn_fwd(h2, w_up_g, w_down, x1, target, modr, g_post):
    T = h2.shape[0]
    tm = min(FFN_TILE, T)
    fb = FFN_BLOCK
    nj, per = _ffn_blocks()

    def body(h_ref, wu_ref, wd_ref, x1_ref, t_ref, mod_ref, g_ref, r_ref, dy2_ref, dx2_ref, st_ref, acc):
        i, j = pl.program_id(0), pl.program_id(1)

        @pl.when((i == 0) & (j == 0))
        def _():
            st_ref[...] = jnp.zeros(st_ref.shape, F32)

        @pl.when(j == 0)
        def _():
            acc[...] = jnp.zeros(acc.shape, F32)

        for rows in _row_chains(tm):
            ra = jnp.maximum(_dot(h_ref[rows, :], wu_ref[0]), 0.0)
            rb = (ra * ra).astype(BF16)
            r_ref[rows, :] = rb
            acc[rows, :] += _dot(rb, wd_ref[...])

        @pl.when(j == nj - 1)
        def _():
            y2 = acc[...]
            rs = lax.rsqrt(jnp.mean(y2 * y2, axis=-1, keepdims=True) + RMS_EPS)
            nh = y2 * rs
            gp, gt = g_ref[...], mod_ref[5:6, :]
            gate_gain = gt * gp
            err = x1_ref[...] + nh * gate_gain - t_ref[...]
            dx2 = err * (1.0 / D_MODEL)
            dx2_ref[...] = dx2
            st_ref[0:1, :] += _colsum(err * err)
            s_dn = _colsum(dx2 * nh)
            st_ref[1:2, :] += s_dn * gp
            st_ref[2:3, :] += s_dn * gt
            dy2_ref[...] = _rms_bwd(dx2 * gate_gain, nh, rs).astype(BF16)

    tile = pl.BlockSpec((tm, D_MODEL), lambda i, j: (i, 0))
    return pl.pallas_call(
        body, name="ffn_fwd", grid=(T // tm, nj),
        in_specs=[tile, pl.BlockSpec((1, D_MODEL, fb), lambda i, j: (j // per, 0, j % per)),
                  pl.BlockSpec((fb, D_MODEL), lambda i, j: (j, 0)), tile, tile,
                  _full((6, D_MODEL)), _full((1, D_MODEL))],
        out_specs=[pl.BlockSpec((tm, fb), lambda i, j: (i, j)), tile, tile, _full((8, D_MODEL))],
        out_shape=[_big((T, D_FF), BF16), _big((T, D_MODEL), BF16), _big((T, D_MODEL), F32),
                   jax.ShapeDtypeStruct((8, D_MODEL), F32)],
        scratch_shapes=[pltpu.VMEM((tm, D_MODEL), F32)],
        compiler_params=_cp(("arbitrary", "arbitrary"), 56),
    )(*_hbm(h2, w_up_g, w_down, x1, target), modr, g_post)


def _rms_bwd(dxn, xn, rs):
    return rs * (dxn - xn * jnp.mean(dxn * xn, axis=-1, keepdims=True))


def _ffn_bwd(dy2, r, x1, dx2, w_up_g, w_down, modr, g_ffn):
    T = dx2.shape[0]
    tm = min(FFN_TILE, T)
    fb = FFN_BLOCK
    nj, per = _ffn_blocks()

    def body(dy2_ref, r_ref, x1_ref, dx2_ref, wu_ref, wd_ref, mod_ref, gf_ref, da_ref, dx1_ref, st_ref, dh_s):
        i, j = pl.program_id(0), pl.program_id(1)

        @pl.when((i == 0) & (j == 0))
        def _():
            st_ref[...] = jnp.zeros(st_ref.shape, F32)

        @pl.when(j == 0)
        def _():
            dh_s[...] = jnp.zeros(dh_s.shape, F32)

        for rows in _row_chains(tm):
            ra = jnp.sqrt(r_ref[rows, :].astype(F32))
            da = (_dot_nt(dy2_ref[rows, :], wd_ref[...]) * (2.0 * ra)).astype(BF16)
            da_ref[rows, :] = da
            dh_s[rows, :] += _dot_nt(da, wu_ref[0])

        @pl.when(j == nj - 1)
        def _():
            dh = dh_s[...]
            x1v = x1_ref[...]
            rs1 = lax.rsqrt(jnp.mean(x1v * x1v, axis=-1, keepdims=True) + RMS_EPS)
            xn = x1v * rs1
            gf, scale1 = gf_ref[...], 1.0 + mod_ref[4:5, :]
            st_ref[0:1, :] += _colsum(dh)
            s_dh = _colsum(dh * xn)
            st_ref[1:2, :] += s_dh * gf
            st_ref[2:3, :] += s_dh * scale1
            dx1_ref[...] = dx2_ref[...] + _rms_bwd(dh * (scale1 * gf), xn, rs1)

    tile = pl.BlockSpec((tm, D_MODEL), lambda i, j: (i, 0))
    ftile = pl.BlockSpec((tm, fb), lambda i, j: (i, j))
    return pl.pallas_call(
        body, name="ffn_bwd", grid=(T // tm, nj),
        in_specs=[tile, ftile, tile, tile, pl.BlockSpec((1, D_MODEL, fb), lambda i, j: (j // per, 0, j % per)),
                  pl.BlockSpec((fb, D_MODEL), lambda i, j: (j, 0)), _full((6, D_MODEL)), _full((1, D_MODEL))],
        out_specs=[ftile, tile, _full((8, D_MODEL))],
        out_shape=[_big((T, D_FF), BF16), _big((T, D_MODEL), F32), jax.ShapeDtypeStruct((8, D_MODEL), F32)],
        scratch_shapes=[pltpu.VMEM((tm, D_MODEL), F32)],
        compiler_params=_cp(("arbitrary", "arbitrary"), 56),
    )(*_hbm(dy2, r, x1, dx2, w_up_g, w_down), modr, g_ffn)


def _mix_out_bwd(dx1, y, cat, w_out, modr, g_post, swap):
    T = dx1.shape[0]
    tm = _tok_tile(T)
    nt = T // tm
    ns = len(swap)

    def body(*refs):
        dx1_ref, y_ref, cat_ref, w_ref, mod_ref, gp_ref = refs[:6]
        s_ins = refs[6:6 + ns]
        dcat_ref, st_ref, gw_ref = refs[6 + ns:9 + ns]
        s_outs = refs[9 + ns:9 + 2 * ns]
        gacc, gsem, pssem, prsem = refs[9 + 2 * ns:]
        i = pl.program_id(0)

        @pl.when(i == 0)
        def _():
            for cp in _pair_copies(s_ins, s_outs, pssem, prsem):
                cp.start()
            st_ref[...] = jnp.zeros(st_ref.shape, F32)
            gacc[...] = jnp.zeros(gacc.shape, F32)

        dxv, yv = dx1_ref[...], y_ref[...]
        rs = lax.rsqrt(jnp.mean(yv * yv, axis=-1, keepdims=True) + RMS_EPS)
        nh = yv * rs
        st_ref[0:1, :] += _colsum(dxv * (nh * gp_ref[...]))
        dn = dxv * mod_ref[2:3, :]
        st_ref[1:2, :] += _colsum(dn * nh)
        dy = _rms_bwd(dn * gp_ref[...], nh, rs).astype(BF16)
        dcat_ref[...] = _dot_nt(dy, w_ref[...])
        for cols in _row_chains(D_MODEL):
            gacc[:, cols] += _dot_tn(cat_ref[...], dy[:, cols])

        @pl.when(i == nt - 1)
        def _():
            out = pltpu.make_async_copy(gacc, gw_ref, gsem)
            out.start()
            copies = _pair_copies(s_ins, s_outs, pssem, prsem)
            for cp in copies:
                cp.wait_recv()
            for cp in copies:
                cp.wait_send()
            out.wait()

    tile = pl.BlockSpec((tm, D_MODEL), lambda i: (i, 0))
    hbm = pl.BlockSpec(memory_space=pl.ANY)
    outs = pl.pallas_call(
        body, name="mix_out_bwd", grid=(nt,),
        in_specs=[tile, tile, tile, _full((D_MODEL, D_MODEL)), _full((6, D_MODEL)), _full((1, D_MODEL))]
        + [hbm] * ns,
        out_specs=[tile, _full((8, D_MODEL)), hbm] + [hbm] * ns,
        out_shape=[_big((T, D_MODEL), F32), jax.ShapeDtypeStruct((8, D_MODEL), F32), _big((D_MODEL, D_MODEL), F32)]
        + _pair_shapes(swap),
        scratch_shapes=[pltpu.VMEM((D_MODEL, D_MODEL), F32), pltpu.SemaphoreType.DMA] + _pair_sems(ns),
        compiler_params=_cp(("arbitrary",), 48),
    )(*_hbm(dx1, y, cat, w_out), modr, g_post, *[_hbm(g) for g in swap])
    return outs[:3], outs[3:]


def _mixers_bwd(p, dcat, ys, o, states, h1, wdw, vecs, lbl, pairs_b):
    T = p.shape[0]
    tm = min(MIXB_TILE, T)
    nt = T // tm
    nch = tm // CHUNK
    hpt = tm // HALO
    nx = len(pairs_b)
    nb = IN_COLS // N_CHIPS

    def body(*refs):
        (p_ref, ph_ref, dcat_ref, ys_ref, o_ref, st_ref, h1_ref, wdw_ref, bdw_ref, gain_ref, bias_ref, gout_ref,
         lbl_ref) = refs[:13]
        x_ins = refs[13:13 + nx]
        dp_ref, sb_ref, s5_ref, dw_ref = refs[13 + nx:17 + nx]
        x_outs = refs[17 + nx:17 + 2 * nx]
        gin_ref, ginb_ref = refs[17 + 2 * nx], refs[18 + 2 * nx]
        (ubuf, dybuf, carry, dstate, qt_s, kt_s, kh_s, v_s, do_s, egl_s, dqt_s, dkt_s, dkh_s, dv_s, dgl_s,
         dsh, dw8, dshift, lower_s, upper_s, same_s, gmat_s, gacc, gacc_b, dp_prev, h1_prev, gsem, gbsem, xssem,
         xrsem) = refs[19 + 2 * nx:]
        i = pl.program_id(0)
        tile_idx = nt - 1 - i

        @pl.when(i == 0)
        def _():
            for cp in _xchg_copies(x_ins, x_outs, xssem, xrsem):
                cp.start()
            gacc[...] = jnp.zeros(gacc.shape, F32)
            dstate[...] = jnp.zeros(dstate.shape, F32)
            carry[...] = jnp.zeros(carry.shape, F32)
            sb_ref[...] = jnp.zeros(sb_ref.shape, F32)
            s5_ref[...] = jnp.zeros(s5_ref.shape, F32)
            dw_ref[...] = jnp.zeros(dw_ref.shape, F32)
            dw8[...] = jnp.zeros(dw8.shape, F32)
            lower_s[...], upper_s[...], same_s[...] = _chunk_masks(tm)
            gmat_s[...] = _gn_matrix()
            dsh[0:SUB, :] = jnp.zeros((SUB, CONV_CH), F32)
            dsh[SUB + tm:2 * SUB + tm, :] = jnp.zeros((SUB, CONV_CH), F32)
            ubuf[HALO + tm:HALO + tm + SUB, :] = jnp.zeros((SUB, CONV_CH), F32)
            dp_prev[...] = jnp.zeros(dp_prev.shape, BF16)
            h1_prev[...] = jnp.zeros(h1_prev.shape, BF16)

        n_pieces = (tm // CONV_ROWS) * (CONV_CH // LANE)
        per_block = n_pieces // N_CHIPS
        prow = D_MODEL // per_block

        def w_in_grad_piece(k):
            j, part = k // per_block, k % per_block
            rows_k = slice(part * prow, (part + 1) * prow)
            gacc[j, rows_k, :] += _dot_tn(h1_prev[:, rows_k], dp_prev[:, j * nb:(j + 1) * nb])

        uh = ph_ref[:, 0:CONV_CH] * _sig(ph_ref[:, CONV_CH:2 * CONV_CH])
        ubuf[0:HALO, :] = jnp.where(tile_idx > 0, uh, 0.0)
        ubuf[HALO:HALO + tm, :] = p_ref[:, 0:CONV_CH] * _sig(p_ref[:, CONV_CH:2 * CONV_CH])
        gmat = gmat_s[...]
        gain = gain_ref[...]
        yv = ys_ref[...]
        d = yv - _gmean(yv, gmat)
        rs = lax.rsqrt(_gmean(d * d, gmat) + GN_EPS)
        yn = d * rs
        z = yn * gain + bias_ref[...]
        sz = _sig(z)
        dz = dcat_ref[:, 0:CONV_CH] * (sz * (1.0 + z * (1.0 - sz)))
        dyn = dz * gain
        dyc = rs * (dyn - _gmean(dyn, gmat) - yn * _gmean(dyn * yn, gmat))
        s5_ref[0:1, :] += _colsum(dyc)
        s5_ref[1:2, :] += _colsum(dz * yn)
        s5_ref[2:3, :] += _colsum(dz)
        dybuf[tm:tm + HALO, :] = carry[...]
        dybuf[0:tm, :] = dyc
        dsh[SUB:SUB + tm, :] = dyc
        carry[...] = dyc[0:HALO, :]
        for b in range(SUB):
            dshift[...] = dsh[SUB - b:2 * SUB - b + tm, :]
            for j, off in CONV_FWD_TAPS:
                if off % SUB == b:
                    prod = dshift[...] * ubuf[off - b:off - b + tm + SUB, :]
                    dw8[j] += jnp.sum(prod.reshape((tm + SUB) // SUB, SUB, CONV_CH), axis=0)
        for r in range(tm // CONV_ROWS):
            rows = slice(r * CONV_ROWS, (r + 1) * CONV_ROWS)
            for lb_ in range(CONV_CH // LANE):
                lanes = slice(lb_ * LANE, (lb_ + 1) * LANE)
                glanes = slice(CONV_CH + lb_ * LANE, CONV_CH + (lb_ + 1) * LANE)
                w_in_grad_piece(r * (CONV_CH // LANE) + lb_)
                acc = _tap_conv(dybuf, wdw_ref, r * CONV_ROWS, CONV_BWD_TAPS, lanes)
                val = p_ref[rows, lanes]
                sg = _sig(p_ref[rows, glanes])
                dval = acc * sg
                dgate = acc * val * (sg * (1.0 - sg))
                dp_ref[rows, lanes] = dval.astype(BF16)
                dp_ref[rows, glanes] = dgate.astype(BF16)
                sb_ref[0:1, lanes] += _colsum(dval)
                sb_ref[0:1, glanes] += _colsum(dgate)

        o0 = 2 * CONV_CH
        for h in range(N_HEADS):
            sl = slice(h * HEAD_D, (h + 1) * HEAD_D)
            gsl = slice(o0 + 3 * HGRN_W + h * HEAD_D, o0 + 3 * HGRN_W + (h + 1) * HEAD_D)
            oh = o_ref[:, sl]
            gh = p_ref[:, gsl]
            dh = dcat_ref[:, CONV_CH + h * HEAD_D:CONV_CH + (h + 1) * HEAD_D]
            gout = gout_ref[:, sl]
            rsh = lax.rsqrt(jnp.mean(oh * oh, axis=-1, keepdims=True) + RMS_EPS)
            on = oh * rsh
            sgg = _sig(gh)
            dgh = dh * (on * gout) * (sgg * (1.0 + gh * (1.0 - sgg)))
            dm = dh * (gh * sgg)
            s5_ref[3:4, sl] += _colsum(dm * on)
            do_s[:, sl] = _rms_bwd(dm * gout, on, rsh).astype(BF16)
            dp_ref[:, gsl] = dgh.astype(BF16)
            sb_ref[2:3, CONV_CH + h * HEAD_D:CONV_CH + (h + 1) * HEAD_D] += _colsum(dgh)

        lb, _ = _lower_bound(lbl_ref)
        pq = p_ref[:, o0:o0 + HGRN_W]
        pr = _hgrn_prep(pq, p_ref[:, o0 + HGRN_W:o0 + 2 * HGRN_W], lb, lower_s[...])
        qt_s[...] = pr["qt"].astype(BF16)
        kt_s[...] = pr["kt"].astype(BF16)
        kh_s[...] = pr["kh"].astype(BF16)
        v_s[...] = p_ref[:, o0 + 2 * HGRN_W:o0 + 3 * HGRN_W].astype(BF16)
        egl_s[...] = jnp.exp(pr["Gl"])
        tri = _tri()

        def chunk(it, c_):
            ci = nch - 1 - it
            r0 = pl.multiple_of(ci * CHUNK, CHUNK)
            rows = pl.ds(r0, CHUNK)
            for h in range(N_HEADS):
                ls = pl.ds(h * HEAD_D, HEAD_D)
                qc, kc, hc, vc = qt_s[rows, ls], kt_s[rows, ls], kh_s[rows, ls], v_s[rows, ls]
                dob = do_s[rows, ls]
                s0 = st_ref[ci, h]
                s0b = s0.astype(BF16)
                ds1 = dstate[h]
                ds1b = ds1.astype(BF16)
                egl = egl_s[pl.ds(r0, 1), ls]
                att = jnp.where(tri, _dot_nt(qc, kc), 0.0).astype(BF16)
                datt = jnp.where(tri, _dot_nt(dob, vc), 0.0).astype(BF16)
                dv_s[rows, ls] = _dot_tn(att, dob) + _dot_nt(hc, ds1b)
                dqt_s[rows, ls] = _dot(datt, kc) + _dot(dob, s0b)
                dkt_s[rows, ls] = _dot_tn(datt, qc)
                dkh_s[rows, ls] = _dot(vc, ds1b)
                dgl = egl * _colsum(ds1 * s0)
                dgl_s[rows, ls] = jnp.broadcast_to(dgl, (CHUNK, HEAD_D))
                dstate[h] = ds1 * egl + _dot_tn(dob, qc)
            return c_

        lax.fori_loop(0, nch, chunk, 0, unroll=min(CHUNK_UNROLL, nch))
        dqt, dkt, dkh = dqt_s[...], dkt_s[...], dkh_s[...]
        dk = dkt * pr["enG"] + dkh * pr["eGlG"]
        khk = dkh * kh_s[...].astype(F32)
        dG = dqt * qt_s[...].astype(F32) - dkt * kt_s[...].astype(F32) - khk
        dlogf = _mm3(upper_s[...], dG) + _mm3(same_s[...], khk) + dgl_s[...]
        df = dlogf / pr["f"] - dk
        sf, sq = pr["sf"], pr["sq"]
        s5_ref[4:5, :] += _colsum(df * (1.0 - sf))
        dfl = df * (1.0 - lb) * (sf * (1.0 - sf))
        dq = (dqt * pr["eG"]) * (sq * (1.0 + pq * (1.0 - sq)))
        dvv = dv_s[...]
        dp_ref[:, o0:o0 + HGRN_W] = dq.astype(BF16)
        dp_ref[:, o0 + HGRN_W:o0 + 2 * HGRN_W] = dfl.astype(BF16)
        dp_ref[:, o0 + 2 * HGRN_W:o0 + 3 * HGRN_W] = dvv.astype(BF16)
        sb_ref[1:2, 0:HGRN_W] += _colsum(dq)
        sb_ref[1:2, HGRN_W:2 * HGRN_W] += _colsum(dfl)
        sb_ref[2:3, 0:HGRN_W] += _colsum(dvv)

        dp_prev[...] = dp_ref[...]
        h1_prev[...] = h1_ref[...]

        @pl.when(i == nt - 1)
        def _():
            for k in range(n_pieces):
                w_in_grad_piece(k)
            out = pltpu.make_async_copy(gacc, gin_ref, gsem)
            out.start()
            for j in range(N_CHIPS):
                gacc_b[j] = gacc[j].astype(BF16)
            out_b = pltpu.make_async_copy(gacc_b, ginb_ref, gbsem)
            out_b.start()
            for j in range(CONV_K):
                dw_ref[j:j + 1, :] = _colsum(dw8[j])
            copies = _xchg_copies(x_ins, x_outs, xssem, xrsem)
            for cp in copies:
                cp.wait_recv()
            for cp in copies:
                cp.wait_send()
            out.wait()
            out_b.wait()

    rev = lambda cols: pl.BlockSpec((tm, cols), lambda i: (nt - 1 - i, 0))
    halo = pl.BlockSpec((HALO, 2 * CONV_CH), lambda i: (jnp.maximum((nt - 1 - i) * hpt - 1, 0), 0))
    wide = lambda n: pltpu.VMEM((tm, HGRN_W), n)
    hbm = pl.BlockSpec(memory_space=pl.ANY)
    outs = pl.pallas_call(
        body, name="mixers_bwd", grid=(nt,),
        in_specs=[rev(IN_COLS), halo, rev(D_MODEL), rev(CONV_CH), rev(HGRN_W),
                  pl.BlockSpec((nch, N_HEADS, HEAD_D, HEAD_D), lambda i: (nt - 1 - i, 0, 0, 0)), rev(D_MODEL),
                  _full((HALO, CONV_CH))] + [_full((1, CONV_CH))] * 4 + [_full((2, HGRN_W))] + [hbm] * nx,
        out_specs=[rev(IN_COLS), _full((8, D_MODEL)), _full((8, CONV_CH)), _full((HALO, CONV_CH))]
        + [hbm] * (nx + 2),
        out_shape=[_big((T, IN_COLS), BF16), jax.ShapeDtypeStruct((8, D_MODEL), F32),
                   jax.ShapeDtypeStruct((8, CONV_CH), F32), jax.ShapeDtypeStruct((HALO, CONV_CH), F32)]
        + [_big(pb.shape, BF16) for pb in pairs_b]
        + [_big((N_CHIPS, D_MODEL, nb), F32), _big((N_CHIPS, D_MODEL, nb), BF16)],
        scratch_shapes=[pltpu.VMEM((tm + HALO + SUB, CONV_CH), F32), pltpu.VMEM((tm + HALO, CONV_CH), F32),
                        pltpu.VMEM((HALO, CONV_CH), F32), pltpu.VMEM((N_HEADS, HEAD_D, HEAD_D), F32),
                        wide(BF16), wide(BF16), wide(BF16), wide(BF16), wide(BF16),
                        wide(F32), wide(F32), wide(F32), wide(F32), wide(F32), wide(F32),
                        pltpu.VMEM((tm + 2 * SUB, CONV_CH), F32), pltpu.VMEM((HALO, SUB, CONV_CH), F32),
                        pltpu.VMEM((tm + SUB, CONV_CH), F32), pltpu.VMEM((tm, tm), BF16), pltpu.VMEM((tm, tm), BF16),
                        pltpu.VMEM((tm, tm), BF16), pltpu.VMEM((CONV_CH, CONV_CH), BF16),
                        pltpu.VMEM((N_CHIPS, D_MODEL, nb), F32), pltpu.VMEM((N_CHIPS, D_MODEL, nb), BF16),
                        pltpu.VMEM((tm, IN_COLS), BF16), pltpu.VMEM((tm, D_MODEL), BF16),
                        pltpu.SemaphoreType.DMA, pltpu.SemaphoreType.DMA]
        + _xchg_sems(nx),
        compiler_params=_cp(("arbitrary",), 56),
    )(*_hbm(p, p, dcat, ys, o, states, h1), wdw, *vecs, lbl, *[_hbm(pb) for pb in pairs_b])
    return outs[:4], outs[4:4 + nx], (outs[4 + nx], outs[5 + nx])


def _mix_in_bwd(dp, w_in_g, x, dx1, modr, g_pre, pairs_b):
    T = x.shape[0]
    tm = _tok_tile(T)
    nt = T // tm
    nb = IN_COLS // N_CHIPS
    nx = len(pairs_b)

    def body(*refs):
        dp_ref, w_ref, x_ref, dx1_ref, mod_ref, g_ref = refs[:6]
        x_ins = refs[6:6 + nx]
        gx_ref, st_ref = refs[6 + nx:8 + nx]
        x_outs = refs[8 + nx:8 + 2 * nx]
        xssem, xrsem = refs[8 + 2 * nx:]
        i = pl.program_id(0)

        @pl.when(i == 0)
        def _():
            for cp in _xchg_copies(x_ins, x_outs, xssem, xrsem):
                cp.start()
            st_ref[...] = jnp.zeros(st_ref.shape, F32)

        g, scale1 = g_ref[...], 1.0 + mod_ref[1:2, :]
        for rows in _row_chains(tm):
            dh = None
            for j in range(N_CHIPS):
                part = _dot_nt(dp_ref[rows, j * nb:(j + 1) * nb], w_ref[j])
                dh = part if dh is None else dh + part
            xv = x_ref[rows, :]
            rs = lax.rsqrt(jnp.mean(xv * xv, axis=-1, keepdims=True) + RMS_EPS)
            xn = xv * rs
            st_ref[0:1, :] += _colsum(dh)
            s_dh = _colsum(dh * xn)
            st_ref[1:2, :] += s_dh * g
            st_ref[2:3, :] += s_dh * scale1
            gx_ref[rows, :] = dx1_ref[rows, :] + _rms_bwd(dh * (scale1 * g), xn, rs)

        @pl.when(i == nt - 1)
        def _():
            copies = _xchg_copies(x_ins, x_outs, xssem, xrsem)
            for cp in copies:
                cp.wait_recv()
            for cp in copies:
                cp.wait_send()

    tile = pl.BlockSpec((tm, D_MODEL), lambda i: (i, 0))
    hbm = pl.BlockSpec(memory_space=pl.ANY)
    outs = pl.pallas_call(
        body, name="mix_in_bwd", grid=(nt,),
        in_specs=[pl.BlockSpec((tm, IN_COLS), lambda i: (i, 0)), _full((N_CHIPS, D_MODEL, nb)), tile, tile,
                  _full((6, D_MODEL)), _full((1, D_MODEL))] + [hbm] * nx,
        out_specs=[tile, _full((8, D_MODEL))] + [hbm] * nx,
        out_shape=[_big((T, D_MODEL), F32), jax.ShapeDtypeStruct((8, D_MODEL), F32)]
        + [_big(pb.shape, BF16) for pb in pairs_b],
        scratch_shapes=_xchg_sems(nx),
        compiler_params=_cp(("arbitrary",), 48),
    )(*_hbm(dp, w_in_g, x, dx1), modr, g_pre, *[_hbm(pb) for pb in pairs_b])
    return outs[:2], outs[2:]


def _weight_grad(a, b, a_blocked, b_blocked, name):
    T = a.shape[0]
    tt = min(GRAD_TILE, T)
    nt = T // tt
    ka = a.shape[1] // N_CHIPS if a_blocked else a.shape[1]
    nb = b.shape[1] // N_CHIPS if b_blocked else b.shape[1]

    def body(a_ref, b_ref, o_ref, ob_ref):
        t = pl.program_id(1)

        @pl.when(t == 0)
        def _():
            o_ref[...] = jnp.zeros(o_ref.shape, F32)

        for cols in _row_chains(nb):
            o_ref[0, :, cols] += _dot_tn(a_ref[...], b_ref[:, cols])

        @pl.when(t == nt - 1)
        def _():
            ob_ref[0] = o_ref[0].astype(BF16)

    blk = pl.BlockSpec((1, ka, nb), lambda j, t: (j, 0, 0))
    return pl.pallas_call(
        body, name=name, grid=(N_CHIPS, nt),
        in_specs=[pl.BlockSpec((tt, ka), (lambda j, t: (t, j)) if a_blocked else (lambda j, t: (t, 0))),
                  pl.BlockSpec((tt, nb), (lambda j, t: (t, j)) if b_blocked else (lambda j, t: (t, 0)))],
        out_specs=[blk, blk],
        out_shape=[_big((N_CHIPS, ka, nb), F32), _big((N_CHIPS, ka, nb), BF16)],
        compiler_params=_cp(("arbitrary", "arbitrary"), 48),
    )(*_hbm(a, b))


R_LOSS = 0
R_FFN = 8
R_OUT = 16
R_IN = 24
R_BIN = 32
R_512 = 40
R_DW = 48
N_STAT_ROWS = 80
MOD_ROWS = (R_IN + 0, R_IN + 1, R_OUT + 0, R_FFN + 0, R_FFN + 1, R_LOSS + 1)


def _small_update(gath, params):
    names = ["b_ada", "lb_logits", "g_pre_mix", "b_in", "b_dw", "gn_gain", "gn_bias", "g_hgrn_out", "g_post_mix",
             "g_pre_ffn", "g_post_ffn"]
    flat = []
    for n in names:
        flat += list(params[n])
    n_in = 1 + len(flat)

    def body(*refs):
        g_ref = refs[0]
        prm = {n: refs[1 + 3 * k:4 + 3 * k] for k, n in enumerate(names)}
        outs = refs[n_in:]
        loss_ref, dmod_ref, dwdw_ref = outs[0], outs[1], outs[2]
        res = {n: outs[3 + 4 * k:7 + 4 * k] for k, n in enumerate(names)}
        red = g_ref[0]
        for dev in range(1, N_DEV):
            red = red + g_ref[dev]
        loss_ref[...] = jnp.broadcast_to(
            (0.5 / D_MODEL) * jnp.sum(red[R_LOSS:R_LOSS + 1, :], axis=-1, keepdims=True), loss_ref.shape)
        for dev in range(N_DEV):
            for k, r in enumerate(MOD_ROWS):
                dmod_ref[dev:dev + 1, k * D_MODEL:(k + 1) * D_MODEL] = g_ref[dev, r:r + 1, :]
        dwdw_ref[...] = red[R_DW:R_DW + HALO, 0:CONV_CH]

        def finish(name, pieces):
            w_ref, m_ref, v_ref = prm[name]
            g_out, d_out, m_out, v_out = res[name]
            for rsl, lsl, g in pieces:
                d, m2, v2 = _adam_math(w_ref[rsl, lsl], g, m_ref[rsl, lsl], v_ref[rsl, lsl])
                g_out[rsl, lsl] = g
                d_out[rsl, lsl] = d
                m_out[rsl, lsl] = m2
                v_out[rsl, lsl] = v2

        one = slice(0, 1)
        row = lambda r: red[r:r + 1, :]
        half = lambda r: red[r:r + 1, 0:CONV_CH]
        finish("b_ada", [(one, slice(k * D_MODEL, (k + 1) * D_MODEL), row(r)) for k, r in enumerate(MOD_ROWS)])
        finish("b_in", [(one, slice(k * D_MODEL, (k + 1) * D_MODEL), row(R_BIN + k)) for k in range(3)])
        finish("g_pre_mix", [(one, slice(None), row(R_IN + 2))])
        finish("g_post_mix", [(one, slice(None), row(R_OUT + 1))])
        finish("g_pre_ffn", [(one, slice(None), row(R_FFN + 2))])
        finish("g_post_ffn", [(one, slice(None), row(R_LOSS + 2))])
        finish("b_dw", [(one, slice(None), half(R_512 + 0))])
        finish("gn_gain", [(one, slice(None), half(R_512 + 1))])
        finish("gn_bias", [(one, slice(None), half(R_512 + 2))])
        finish("g_hgrn_out", [(one, slice(None), half(R_512 + 3))])
        s0, s1 = _lower_bound(prm["lb_logits"][0])
        dlb = half(R_512 + 4)
        finish("lb_logits", [(slice(0, 1), slice(None), dlb * s0 * (1.0 - s0)),
                             (slice(1, 2), slice(None), -dlb * s0 * s1)])

    vm = pl.BlockSpec(memory_space=pltpu.VMEM)
    out_shape = [jax.ShapeDtypeStruct((8, 128), F32), jax.ShapeDtypeStruct((N_DEV, 6 * D_MODEL), F32),
                 jax.ShapeDtypeStruct((HALO, CONV_CH), F32)]
    for n in names:
        out_shape += [jax.ShapeDtypeStruct(params[n][0].shape, F32)] * 4
    outs = pl.pallas_call(
        body, name="small_update", out_shape=out_shape,
        in_specs=[vm] * n_in, out_specs=[vm] * len(out_shape),
        compiler_params=_cp(None, 32),
    )(gath, *flat)
    return outs[0], outs[1], outs[2], {n: outs[3 + 4 * k:7 + 4 * k] for k, n in enumerate(names)}


def _wdw_adam(w, g, m, v):
    def body(w_ref, g_ref, m_ref, v_ref, d_out, m_out, v_out):
        d, m2, v2 = _adam_math(w_ref[...], g_ref[...], m_ref[...], v_ref[...])
        d_out[...] = d
        m_out[...] = m2
        v_out[...] = v2

    vm = pl.BlockSpec(memory_space=pltpu.VMEM)
    return pl.pallas_call(
        body, name="wdw_adam", out_shape=[jax.ShapeDtypeStruct(w.shape, F32)] * 3,
        in_specs=[vm] * 4, out_specs=[vm] * 3, compiler_params=_cp(None, 16),
    )(w, g, m, v)


def kernel(x, c, w_ada, b_ada, lb_logits, g_pre_mix, w_in, b_in, w_dw, b_dw, gn_gain, gn_bias, g_hgrn_out, w_out, g_post_mix, g_pre_ffn, w_up, w_down, g_post_ffn, loss_target, m_w_ada, m_b_ada, m_lb_logits, m_g_pre_mix, m_w_in, m_b_in, m_w_dw, m_b_dw, m_gn_gain, m_gn_bias, m_g_hgrn_out, m_w_out, m_g_post_mix, m_g_pre_ffn, m_w_up, m_w_down, m_g_post_ffn, v_w_ada, v_b_ada, v_lb_logits, v_g_pre_mix, v_w_in, v_b_in, v_w_dw, v_b_dw, v_gn_gain, v_gn_bias, v_g_hgrn_out, v_w_out, v_g_post_mix, v_g_pre_ffn, v_w_up, v_w_down, v_g_post_ffn):
    ax, ay, ac = lax.axis_index("x"), lax.axis_index("y"), lax.axis_index("c")
    chip = 2 * ax + ay
    T = x.shape[1]
    xs, tgt = x[0], loss_target[0]
    ada_cols = w_ada.shape[2]

    b_sh = lax.dynamic_slice_in_dim(b_ada, chip * ada_cols, ada_cols, axis=1)
    wdw_pad = jnp.pad(w_dw[0], ((0, HALO - CONV_K), (0, 0)))
    chip1 = jnp.reshape(chip, (1,)).astype(jnp.int32)
    place = jnp.stack([ac, chip]).astype(jnp.int32)
    _, c8, modg, wdwg = _ada_exchange(c, w_ada[0], b_sh, wdw_pad)
    modr = modg.reshape(6, D_MODEL)
    wdw_all = jnp.transpose(wdwg, (1, 0, 2)).reshape(HALO, CONV_CH)
    vec = (b_dw, gn_gain, gn_bias, g_hgrn_out)

    p, h1, (up_buf, down_buf), w_in_g, w_out_g = _mix_in_fwd(
        chip1, xs, modr, g_pre_mix, b_in.reshape(N_CHIPS, 1, IN_COLS // N_CHIPS), [w_up[0], w_down[0]],
        _cast_own(chip1, w_in[0], "cast_w_in"), _cast_own(chip1, w_out[0], "cast_w_out"))
    w_out_f = w_out_g.reshape(D_MODEL, D_MODEL)
    (cat, ys, o, states, y, x1, h2), (w_up_g, w_down_g) = _mixers_fwd(
        p, wdw_all, vec, lb_logits, w_out_f, xs, modr, g_post_mix, g_pre_ffn, [up_buf, down_buf])
    w_down_f = w_down_g.reshape(D_FF, D_MODEL)
    r, dy2, dx2, st_loss = _ffn_fwd(h2, w_up_g, w_down_f, x1, tgt, modr, g_post_ffn)

    def pair_sums(grads, got, tags):
        return [_pair_sum(place, g, o_, "pair_sum_" + t) for (g, _), o_, t in zip(grads, got, tags)]

    da, dx1, st_ffn = _ffn_bwd(dy2, r, x1, dx2, w_up_g, w_down_f, modr, g_pre_ffn)
    g_up = _weight_grad(h2, da, False, True, "grad_w_up")
    g_down = _weight_grad(r, dy2, True, False, "grad_w_down")
    (dcat, st_out, g_out), got_ud = _mix_out_bwd(dx1, y, cat, w_out_f, modr, g_post_mix, [g_up[1], g_down[1]])
    g_out = g_out.reshape(N_CHIPS, D_MODEL // N_CHIPS, D_MODEL)
    got_o = _pair_swap([g_out], "pair_swap_w_out")
    early = pair_sums([(g_out, None), g_up, g_down], list(got_o) + list(got_ud), ["w_out", "w_up", "w_down"])
    (dp, st_bin, st_512, dwdw), got_early, g_in = _mixers_bwd(p, dcat, ys, o, states, h1, wdw_all, vec, lb_logits,
                                                              [pb for _, pb in early])
    late = pair_sums([g_in], _pair_swap([g_in[1]], "pair_swap_w_in"), ["w_in"])
    (grad_x, st_in), got_late = _mix_in_bwd(dp, w_in_g, xs, dx1, modr, g_pre_mix, [late[0][1]])
    fulls = [_chip_sum(place, pf, gb, "chip_sum_" + t)
             for (pf, _), gb, t in zip(late + early, list(got_late) + list(got_early), ["w_in", "w_out", "w_up", "w_down"])]

    pad_lanes = lambda s: jnp.pad(s, ((0, 0), (0, D_MODEL - s.shape[1])))
    stats = jnp.concatenate([st_loss, st_ffn, st_out, st_in, st_bin, pad_lanes(st_512), pad_lanes(dwdw)], axis=0)
    (g_w_in, g_w_out, g_w_up, g_w_down), gath = _final_exchange(fulls, stats)
    small = {"b_ada": (b_ada, m_b_ada, v_b_ada), "lb_logits": (lb_logits, m_lb_logits, v_lb_logits),
             "g_pre_mix": (g_pre_mix, m_g_pre_mix, v_g_pre_mix), "b_in": (b_in, m_b_in, v_b_in),
             "b_dw": (b_dw, m_b_dw, v_b_dw), "gn_gain": (gn_gain, m_gn_gain, v_gn_gain),
             "gn_bias": (gn_bias, m_gn_bias, v_gn_bias), "g_hgrn_out": (g_hgrn_out, m_g_hgrn_out, v_g_hgrn_out),
             "g_post_mix": (g_post_mix, m_g_post_mix, v_g_post_mix), "g_pre_ffn": (g_pre_ffn, m_g_pre_ffn, v_g_pre_ffn),
             "g_post_ffn": (g_post_ffn, m_g_post_ffn, v_g_post_ffn)}
    loss_t, dmod_all, dwdw_sum, sres = _small_update(gath, small)
    loss = loss_t[0, 0]

    res = dict(sres)
    dmod_sh = lax.dynamic_slice_in_dim(dmod_all, chip * ada_cols, ada_cols, axis=1)
    res["w_ada"] = [t[None] for t in _ada_grad_adam(jnp.transpose(c8), dmod_sh, w_ada[0], m_w_ada[0], v_w_ada[0])]
    g_wdw = lax.dynamic_slice_in_dim(dwdw_sum, chip * HEAD_D, HEAD_D, axis=1)[:CONV_K][None]
    res["w_dw"] = [g_wdw] + list(_wdw_adam(w_dw, g_wdw, m_w_dw, v_w_dw))
    for name, g, w, m, v in (("w_in", g_w_in, w_in, m_w_in, v_w_in), ("w_out", g_w_out, w_out, m_w_out, v_w_out),
                             ("w_up", g_w_up, w_up, m_w_up, v_w_up), ("w_down", g_w_down, w_down, m_w_down, v_w_down)):
        d, m2, v2 = _adam_big(w[0], g, m[0], v[0], "adam_" + name)
        res[name] = [g[None], d[None], m2[None], v2[None]]

    order = ["w_ada", "b_ada", "lb_logits", "g_pre_mix", "w_in", "b_in", "w_dw", "b_dw", "gn_gain", "gn_bias",
             "g_hgrn_out", "w_out", "g_post_mix", "g_pre_ffn", "w_up", "w_down", "g_post_ffn"]
    out = [loss, grad_x[None]]
    for k in range(4):
        out += [res[n][k] for n in order]
    return tuple(out)
```

```python
import jax
import jax.numpy as jnp
from jax import lax
from jax.experimental import pallas as pl
from jax.experimental.pallas import tpu as pltpu

F32, BF16 = jnp.float32, jnp.bfloat16
D_MODEL = 1024
CONV_CH = 512
HGRN_W = 512
N_HEADS = 4
HEAD_D = 128
CONV_K = 31
GN_GROUP = 64
GN_SHIFT = 6
IN_COLS = 3072
D_FF = 4096
CHUNK = 64
CHUNK_SHIFT = 6
N_CHIPS = 4
N_DEV = 8
RMS_EPS = 1e-6
GN_EPS = 1e-5
ADAM_LR, ADAM_B1, ADAM_B2, ADAM_EPS, ADAM_WD, ADAM_STEP = 0.001, 0.9, 0.999, 1e-08, 0.01, 10
TOK_TILE = 512
MIXIN_TILE = 1024
MIXB_TILE = 256
FFN_TILE = 1024
FFN_BLOCK = 512
GRAD_TILE = 4096
HALO = 32
SUB = 8
LANE = 128
CONV_ROWS = 128
CHUNK_UNROLL = 8
MIB = 1 << 20
MESH = pl.DeviceIdType.MESH
OTHER_CHIPS = ((0, 1), (1, 0), (1, 1))


def _cp(sem=None, vmem_mib=48):
    return pltpu.CompilerParams(dimension_semantics=sem, vmem_limit_bytes=vmem_mib * MIB)


def _dot(a, b):
    return jnp.dot(a, b, preferred_element_type=F32)


def _dot_nt(a, b):
    return lax.dot_general(a, b, (((1,), (1,)), ((), ())), preferred_element_type=F32)


def _dot_tn(a, b):
    return lax.dot_general(a, b, (((0,), (0,)), ((), ())), preferred_element_type=F32)


def _sig(v):
    return 0.5 * jnp.tanh(0.5 * v) + 0.5


def _colsum(v):
    return jnp.sum(v, axis=0, keepdims=True)


def _flip(v, b):
    return 1 - v if b else v


def _rcopy(src, dst, ssem, rsem, dev):
    return pltpu.make_async_remote_copy(src_ref=src, dst_ref=dst, send_sem=ssem, recv_sem=rsem,
                                        device_id=dev, device_id_type=MESH)


def _place():
    return lax.axis_index("x"), lax.axis_index("y"), lax.axis_index("c")


def _full(shape):
    return pl.BlockSpec(shape, lambda *_: (0,) * len(shape))


def _big(shape, dtype):
    return pltpu.HBM(shape, dtype)


def _hbm(*arrays):
    out = [pltpu.with_memory_space_constraint(a, pltpu.HBM) for a in arrays]
    return out[0] if len(out) == 1 else out


def _split2(v):
    hi = v.astype(BF16)
    lo = (v - hi.astype(F32)).astype(BF16)
    return hi, lo


def _split3(v):
    h1 = v.astype(BF16)
    r1 = v - h1.astype(F32)
    h2 = r1.astype(BF16)
    h3 = (r1 - h2.astype(F32)).astype(BF16)
    return h1, h2, h3


def _mm3(mat, v):
    h1, h2, h3 = _split3(v)
    return _dot(mat, h1) + _dot(mat, h2) + _dot(mat, h3)


def _gn_matrix():
    r = lax.broadcasted_iota(jnp.int32, (CONV_CH, CONV_CH), 0) >> GN_SHIFT
    c = lax.broadcasted_iota(jnp.int32, (CONV_CH, CONV_CH), 1) >> GN_SHIFT
    return jnp.where(r == c, 1.0 / GN_GROUP, 0.0).astype(BF16)


def _gmean(v, gmat):
    hi, lo = _split2(v)
    return _dot(hi, gmat) + _dot(lo, gmat)


def _chunk_masks(tm):
    r = lax.broadcasted_iota(jnp.int32, (tm, tm), 0)
    c = lax.broadcasted_iota(jnp.int32, (tm, tm), 1)
    same = (r >> CHUNK_SHIFT) == (c >> CHUNK_SHIFT)
    one = lambda m: jnp.where(m, 1.0, 0.0).astype(BF16)
    return one(same & (c <= r)), one(same & (c >= r)), one(same)


def _tri():
    return lax.broadcasted_iota(jnp.int32, (CHUNK, CHUNK), 0) >= lax.broadcasted_iota(jnp.int32, (CHUNK, CHUNK), 1)


def _lower_bound(lbl_ref):
    l0, l1 = lbl_ref[0:1, :], lbl_ref[1:2, :]
    mx = jnp.maximum(l0, l1)
    e0, e1 = jnp.exp(l0 - mx), jnp.exp(l1 - mx)
    return e0 / (e0 + e1), e1 / (e0 + e1)


CONV_FWD_TAPS = tuple((j, HALO - (CONV_K - 1) + j) for j in range(CONV_K))
CONV_BWD_TAPS = tuple((j, (CONV_K - 1) - j) for j in range(CONV_K))


def _tap_conv(src_ref, w_ref, row0, taps, lanes):
    acc = None
    for b in range(SUB):
        pb = None
        for j, off in taps:
            if off % SUB == b:
                lo = row0 + off - b
                term = w_ref[j:j + 1, lanes] * src_ref[lo:lo + CONV_ROWS + SUB, lanes]
                pb = term if pb is None else pb + term
        if pb is not None:
            sh = pb[b:b + CONV_ROWS, :]
            acc = sh if acc is None else acc + sh
    return acc


def _hgrn_prep(pq, pf, lb, lower):
    sq = _sig(pq)
    qf = pq * sq
    sf = _sig(pf)
    f = lb + (1.0 - lb) * sf
    logf = jnp.log(f)
    k = 1.0 - f
    G = _mm3(lower, logf)
    rows, cols = G.shape
    g3 = G.reshape(rows // CHUNK, CHUNK, cols)
    Gl = jnp.broadcast_to(g3[:, CHUNK - 1:CHUNK, :], g3.shape).reshape(rows, cols)
    eG, enG, eGlG = jnp.exp(G), jnp.exp(-G), jnp.exp(Gl - G)
    return dict(sq=sq, sf=sf, f=f, Gl=Gl, eG=eG, enG=enG, eGlG=eGlG, qt=qf * eG, kt=k * enG, kh=k * eGlG)


def _ada_exchange(c_row, w_ada, b_sh, wdw_pad):
    ncol = w_ada.shape[1]

    def body(c_ref, w_ref, b_ref, wdw_ref, call_ref, c8_ref, modg_ref, wdwg_ref, rows_s, sa, ra, sw, rw, sm, rm):
        x, y, c = _place()
        me = 4 * x + 2 * y + c
        chip = 2 * x + y
        cv = c_ref[...]
        call_ref[me] = cv * _sig(cv)
        wdwg_ref[chip] = wdw_ref[...]
        sends = []
        for m in range(1, N_DEV):
            peer = (_flip(x, m >> 2), _flip(y, (m >> 1) & 1), _flip(c, m & 1))
            cp = _rcopy(call_ref.at[me], call_ref.at[me], sa.at[m - 1], ra.at[m - 1], peer)
            cp.start()
            sends.append(cp)
        for k, (fx, fy) in enumerate(OTHER_CHIPS):
            peer = (_flip(x, fx), _flip(y, fy), c)
            cp = _rcopy(wdwg_ref.at[chip], wdwg_ref.at[chip], sw.at[k], rw.at[k], peer)
            cp.start()
            sends.append(cp)
        for m in range(1, N_DEV):
            peer = (_flip(x, m >> 2), _flip(y, (m >> 1) & 1), _flip(c, m & 1))
            pid = 4 * peer[0] + 2 * peer[1] + peer[2]
            _rcopy(call_ref.at[pid], call_ref.at[pid], sa.at[m - 1], ra.at[m - 1], peer).wait_recv()
        for b in range(N_DEV):
            c8_ref[b:b + 1, :] = call_ref[b]
        mod_all = _dot(c8_ref[...].astype(BF16), w_ref[...].astype(BF16)) + b_ref[...]
        for b in range(N_DEV):
            rows_s[b] = mod_all[b:b + 1, :]
        modg_ref[chip] = rows_s[me]
        for k, (fx, fy) in enumerate(OTHER_CHIPS):
            peer = (_flip(x, fx), _flip(y, fy), c)
            pid = 4 * peer[0] + 2 * peer[1] + peer[2]
            cp = _rcopy(rows_s.at[pid], modg_ref.at[chip], sm.at[k], rm.at[k], peer)
            cp.start()
            sends.append(cp)
        for k, (fx, fy) in enumerate(OTHER_CHIPS):
            peer = (_flip(x, fx), _flip(y, fy), c)
            pchip = 2 * peer[0] + peer[1]
            _rcopy(rows_s.at[0], modg_ref.at[pchip], sm.at[k], rm.at[k], peer).wait_recv()
            _rcopy(wdwg_ref.at[pchip], wdwg_ref.at[pchip], sw.at[k], rw.at[k], peer).wait_recv()
        for cp in sends:
            cp.wait_send()

    vm = pl.BlockSpec(memory_space=pltpu.VMEM)
    return pl.pallas_call(
        body, name="ada_exchange",
        out_shape=[jax.ShapeDtypeStruct((N_DEV, 1, D_MODEL), F32), jax.ShapeDtypeStruct((N_DEV, D_MODEL), F32),
                   jax.ShapeDtypeStruct((N_CHIPS, 1, ncol), F32), jax.ShapeDtypeStruct((N_CHIPS, HALO, HEAD_D), F32)],
        in_specs=[vm] * 4, out_specs=[vm] * 4,
        scratch_shapes=[pltpu.VMEM((N_DEV, 1, ncol), F32),
                        pltpu.SemaphoreType.DMA((N_DEV - 1,)), pltpu.SemaphoreType.DMA((N_DEV - 1,)),
                        pltpu.SemaphoreType.DMA((3,)), pltpu.SemaphoreType.DMA((3,)),
                        pltpu.SemaphoreType.DMA((3,)), pltpu.SemaphoreType.DMA((3,))],
        compiler_params=_cp(None, 32),
    )(c_row, w_ada, b_sh, wdw_pad)


def _cast_own(chip1, shard, name):
    rows, cols = shard.shape
    tr = _row_tile(rows)

    def body(ch_ref, s_ref, o_ref):
        o_ref[0] = s_ref[...].astype(BF16)

    gs = pltpu.PrefetchScalarGridSpec(
        num_scalar_prefetch=1, grid=(rows // tr,),
        in_specs=[pl.BlockSpec((tr, cols), lambda i, ch: (i, 0))],
        out_specs=pl.BlockSpec((1, tr, cols), lambda i, ch: (ch[0], i, 0)))
    return pl.pallas_call(
        body, name=name, grid_spec=gs, out_shape=_big((N_CHIPS, rows, cols), BF16),
        compiler_params=_cp(("arbitrary",), 32),
    )(chip1, _hbm(shard))


def _slab(buf, ch, core):
    hs = buf.shape[1] // 2
    return buf.at[ch, pl.ds(core * hs, hs), :]


def _gather_start(bufs, ssem, rsem, relations=(0, 1, 2)):
    x, y, c = _place()
    chip = 2 * x + y
    for k in relations:
        fx, fy = OTHER_CHIPS[k]
        peer = (_flip(x, fx), _flip(y, fy), c)
        for t, buf in enumerate(bufs):
            _rcopy(_slab(buf, chip, c), _slab(buf, chip, c), ssem.at[t * 3 + k], rsem.at[t * 3 + k], peer).start()


def _gather_pass_on(bufs, ssem, rsem):
    nt = len(bufs)
    x, y, c = _place()
    sibling = (x, y, 1 - c)
    for k, (fx, fy) in enumerate(OTHER_CHIPS):
        peer = (_flip(x, fx), _flip(y, fy), c)
        pchip = 2 * peer[0] + peer[1]
        for t, buf in enumerate(bufs):
            _rcopy(_slab(buf, pchip, c), _slab(buf, pchip, c), ssem.at[t * 3 + k], rsem.at[t * 3 + k], peer).wait_recv()
            _rcopy(_slab(buf, pchip, c), _slab(buf, pchip, c), ssem.at[3 * nt + t * 3 + k],
                   rsem.at[3 * nt + t * 3 + k], sibling).start()


def _gather_drain(bufs, ssem, rsem):
    nt = len(bufs)
    x, y, c = _place()
    chip = 2 * x + y
    sibling = (x, y, 1 - c)
    for k, (fx, fy) in enumerate(OTHER_CHIPS):
        peer = (_flip(x, fx), _flip(y, fy), c)
        pchip = 2 * peer[0] + peer[1]
        for t, buf in enumerate(bufs):
            _rcopy(_slab(buf, pchip, 1 - c), _slab(buf, pchip, 1 - c), ssem.at[3 * nt + t * 3 + k],
                   rsem.at[3 * nt + t * 3 + k], sibling).wait_recv()
            _rcopy(_slab(buf, chip, c), _slab(buf, chip, c), ssem.at[t * 3 + k], rsem.at[t * 3 + k], peer).wait_send()
            _rcopy(_slab(buf, pchip, c), _slab(buf, pchip, c), ssem.at[3 * nt + t * 3 + k],
                   rsem.at[3 * nt + t * 3 + k], sibling).wait_send()


def _gather_finish(bufs, ssem, rsem):
    _gather_pass_on(bufs, ssem, rsem)
    _gather_drain(bufs, ssem, rsem)


def _gather_arrive(bufs, k, ssem, rsem):
    nt = len(bufs)
    x, y, c = _place()
    fx, fy = OTHER_CHIPS[k]
    peer = (_flip(x, fx), _flip(y, fy), c)
    pchip = 2 * peer[0] + peer[1]
    for t, buf in enumerate(bufs):
        _rcopy(_slab(buf, pchip, c), _slab(buf, pchip, c), ssem.at[t * 3 + k], rsem.at[t * 3 + k], peer).wait_recv()
        _rcopy(_slab(buf, pchip, c), _slab(buf, pchip, c), ssem.at[3 * nt + t * 3 + k],
               rsem.at[3 * nt + t * 3 + k], (x, y, 1 - c)).start()
    for t, buf in enumerate(bufs):
        _rcopy(_slab(buf, pchip, 1 - c), _slab(buf, pchip, 1 - c), ssem.at[3 * nt + t * 3 + k],
               rsem.at[3 * nt + t * 3 + k], (x, y, 1 - c)).wait_recv()


def _gather_sends_done(bufs, ssem, rsem):
    nt = len(bufs)
    x, y, c = _place()
    chip = 2 * x + y
    for k, (fx, fy) in enumerate(OTHER_CHIPS):
        peer = (_flip(x, fx), _flip(y, fy), c)
        pchip = 2 * peer[0] + peer[1]
        for t, buf in enumerate(bufs):
            _rcopy(_slab(buf, chip, c), _slab(buf, chip, c), ssem.at[t * 3 + k], rsem.at[t * 3 + k], peer).wait_send()
            _rcopy(_slab(buf, pchip, c), _slab(buf, pchip, c), ssem.at[3 * nt + t * 3 + k],
                   rsem.at[3 * nt + t * 3 + k], (x, y, 1 - c)).wait_send()


def _ring_parts(buf):
    x, y, c = _place()
    ynb, xnb = (x, 1 - y, c), (1 - x, y, c)
    ychip, xchip, dchip = 2 * x + (1 - y), 2 * (1 - x) + y, 2 * (1 - x) + (1 - y)
    hs = buf.shape[1] // 2

    def piece(ch, q):
        return buf.at[ch, pl.ds(c * hs + q * (hs // 2), hs // 2), :]

    return ynb, xnb, ychip, xchip, dchip, piece


def _ring_start(bufs, ssem, rsem):
    x, y, c = _place()
    chip = 2 * x + y
    for t, buf in enumerate(bufs):
        ynb, xnb, _, _, _, _ = _ring_parts(buf)
        _rcopy(_slab(buf, chip, c), _slab(buf, chip, c), ssem.at[2 * t], rsem.at[2 * t], ynb).start()
        _rcopy(_slab(buf, chip, c), _slab(buf, chip, c), ssem.at[2 * t + 1], rsem.at[2 * t + 1], xnb).start()


def _ring_forward(bufs, ssem, rsem):
    nt = len(bufs)
    _, _, c = _place()
    for t, buf in enumerate(bufs):
        ynb, xnb, ychip, xchip, _, piece = _ring_parts(buf)
        _rcopy(_slab(buf, ychip, c), _slab(buf, ychip, c), ssem.at[2 * t], rsem.at[2 * t], ynb).wait_recv()
        _rcopy(piece(ychip, 0), piece(ychip, 0), ssem.at[2 * nt + 2 * t], rsem.at[2 * nt + 2 * t], xnb).start()
        _rcopy(_slab(buf, xchip, c), _slab(buf, xchip, c), ssem.at[2 * t + 1], rsem.at[2 * t + 1], xnb).wait_recv()
        _rcopy(piece(xchip, 1), piece(xchip, 1), ssem.at[2 * nt + 2 * t + 1], rsem.at[2 * nt + 2 * t + 1], ynb).start()


def _ring_finish(bufs, ssem, rsem):
    nt = len(bufs)
    x, y, c = _place()
    chip = 2 * x + y
    sibling = (x, y, 1 - c)
    for t, buf in enumerate(bufs):
        ynb, xnb, ychip, xchip, dchip, piece = _ring_parts(buf)
        _rcopy(piece(dchip, 0), piece(dchip, 0), ssem.at[2 * nt + 2 * t], rsem.at[2 * nt + 2 * t], xnb).wait_recv()
        _rcopy(piece(dchip, 1), piece(dchip, 1), ssem.at[2 * nt + 2 * t + 1], rsem.at[2 * nt + 2 * t + 1],
               ynb).wait_recv()
        for k, ch in enumerate((ychip, xchip, dchip)):
            _rcopy(_slab(buf, ch, c), _slab(buf, ch, c), ssem.at[4 * nt + 3 * t + k], rsem.at[4 * nt + 3 * t + k],
                   sibling).start()
    for t, buf in enumerate(bufs):
        ynb, xnb, ychip, xchip, dchip, piece = _ring_parts(buf)
        for k, ch in enumerate((ychip, xchip, dchip)):
            _rcopy(_slab(buf, ch, 1 - c), _slab(buf, ch, 1 - c), ssem.at[4 * nt + 3 * t + k],
                   rsem.at[4 * nt + 3 * t + k], sibling).wait_recv()
            _rcopy(_slab(buf, ch, c), _slab(buf, ch, c), ssem.at[4 * nt + 3 * t + k], rsem.at[4 * nt + 3 * t + k],
                   sibling).wait_send()
        _rcopy(_slab(buf, chip, c), _slab(buf, chip, c), ssem.at[2 * t], rsem.at[2 * t], ynb).wait_send()
        _rcopy(_slab(buf, chip, c), _slab(buf, chip, c), ssem.at[2 * t + 1], rsem.at[2 * t + 1], xnb).wait_send()
        _rcopy(piece(ychip, 0), piece(ychip, 0), ssem.at[2 * nt + 2 * t], rsem.at[2 * nt + 2 * t], xnb).wait_send()
        _rcopy(piece(xchip, 1), piece(xchip, 1), ssem.at[2 * nt + 2 * t + 1], rsem.at[2 * nt + 2 * t + 1],
               ynb).wait_send()


def _ring_sems(nt):
    return [pltpu.SemaphoreType.DMA((7 * nt,)), pltpu.SemaphoreType.DMA((7 * nt,))]


def _gather_sems(nt):
    return [pltpu.SemaphoreType.DMA((6 * nt,)), pltpu.SemaphoreType.DMA((6 * nt,))]


def _pair_copies(ins, outs, ssem, rsem):
    x, y, c = _place()
    copies = []
    for t in range(len(ins)):
        hs = ins[t].shape[1] // 2
        copies.append(_rcopy(ins[t].at[:, pl.ds((1 - c) * hs, hs), :], outs[t], ssem.at[t], rsem.at[t], (x, y, 1 - c)))
    return copies


def _pair_shapes(grads):
    return [_big((g.shape[0], g.shape[1] // 2, g.shape[2]), g.dtype) for g in grads]


def _pair_sems(nt):
    return [pltpu.SemaphoreType.DMA((nt,)), pltpu.SemaphoreType.DMA((nt,))]


def _pair_swap(grads, name):
    nt = len(grads)
    hbm = pl.BlockSpec(memory_space=pl.ANY)

    def body(*refs):
        copies = _pair_copies(refs[:nt], refs[nt:2 * nt], refs[2 * nt], refs[2 * nt + 1])
        for cp in copies:
            cp.start()
        for cp in copies:
            cp.wait_recv()
        for cp in copies:
            cp.wait_send()

    return pl.pallas_call(
        body, name=name, out_shape=_pair_shapes(grads), in_specs=[hbm] * nt, out_specs=[hbm] * nt,
        scratch_shapes=_pair_sems(nt),
    )(*[_hbm(g) for g in grads])


def _xchg_copies(ins, outs, ssem, rsem):
    x, y, c = _place()
    copies = []
    for k, (fx, fy) in enumerate(OTHER_CHIPS):
        peer = (_flip(x, fx), _flip(y, fy), c)
        for t in range(len(ins)):
            copies.append(_rcopy(ins[t].at[k], outs[t].at[k], ssem.at[t * 3 + k], rsem.at[t * 3 + k], peer))
    return copies


def _xchg_sems(nt):
    return [pltpu.SemaphoreType.DMA((3 * nt,)), pltpu.SemaphoreType.DMA((3 * nt,))]


def _final_exchange(fulls, stats):
    nt = len(fulls)
    rows, cols = stats.shape
    hbm = pl.BlockSpec(memory_space=pl.ANY)
    vm = pl.BlockSpec(memory_space=pltpu.VMEM)

    def body(*refs):
        ins, s_ref = refs[:nt], refs[nt]
        outs, g_ref = refs[nt + 1:2 * nt + 1], refs[2 * nt + 1]
        hssem, hrsem, ssem, rsem = refs[2 * nt + 2:]
        x, y, c = _place()
        me, sibling = (x, y, c), (x, y, 1 - c)
        halves = []
        for t in range(nt):
            hs = ins[t].shape[0] // 2
            mine = pl.ds(c * hs, hs)
            cp = _rcopy(ins[t].at[mine, :], outs[t].at[mine, :], hssem.at[t], hrsem.at[t], sibling)
            cp.start()
            halves.append(cp)

        chips = [(_flip(x, fx), _flip(y, fy)) for fx, fy in OTHER_CHIPS]

        def blk(px, py, pc):
            return g_ref.at[4 * px + 2 * py + pc]

        def copy(k, block, to, src=None):
            return _rcopy(blk(*block) if src is None else src, blk(*block), ssem.at[k], rsem.at[k], to)

        g_ref[4 * x + 2 * y + c] = s_ref[...]
        first = [copy(0, me, sibling, src=s_ref)]
        first += [copy(1 + j, me, (*chip, c), src=s_ref) for j, chip in enumerate(chips)]
        for cp in first:
            cp.start()
        passed = [copy(4 + j, (*chip, c), sibling) for j, chip in enumerate(chips)]
        for j, chip in enumerate(chips):
            copy(1 + j, (*chip, c), me).wait_recv()
            passed[j].start()
        copy(0, sibling, me).wait_recv()
        for j, chip in enumerate(chips):
            copy(4 + j, (*chip, 1 - c), me).wait_recv()
        for t in range(nt):
            hs = ins[t].shape[0] // 2
            other = pl.ds((1 - c) * hs, hs)
            _rcopy(ins[t].at[other, :], outs[t].at[other, :], hssem.at[t], hrsem.at[t], sibling).wait_recv()
        for cp in first + passed + halves:
            cp.wait_send()

    outs = pl.pallas_call(
        body, name="final_exchange",
        out_shape=[_big(f.shape, F32) for f in fulls] + [jax.ShapeDtypeStruct((N_DEV, rows, cols), F32)],
        in_specs=[hbm] * nt + [vm], out_specs=[hbm] * nt + [vm],
        input_output_aliases={t: t for t in range(nt)},
        scratch_shapes=[pltpu.SemaphoreType.DMA((nt,)), pltpu.SemaphoreType.DMA((nt,)),
                        pltpu.SemaphoreType.DMA((7,)), pltpu.SemaphoreType.DMA((7,))],
        compiler_params=_cp(None, 32),
    )(*[_hbm(f) for f in fulls], stats)
    return outs[:nt], outs[nt]


def _row_tile(rows):
    return min(rows, 512)


def _pair_sum(place, grad, got, name):
    nb, hs, cols = got.shape
    tr = _row_tile(hs)
    nr = hs // tr

    def body(pl_ref, g_ref, o_ref, pf_ref, pb_ref):
        j = pl.program_id(1)
        s = g_ref[0] + o_ref[0].astype(F32)

        @pl.when(j == 0)
        def _():
            pf_ref[...] = s

        @pl.when(j > 0)
        def _():
            pb_ref[0] = s.astype(BF16)

    gs = pltpu.PrefetchScalarGridSpec(
        num_scalar_prefetch=1, grid=(nr, nb),
        in_specs=[pl.BlockSpec((1, tr, cols), lambda i, j, p: (p[1] ^ j, p[0] * nr + i, 0)),
                  pl.BlockSpec((1, tr, cols), lambda i, j, p: (p[1] ^ j, i, 0))],
        out_specs=[pl.BlockSpec((tr, cols), lambda i, j, p: (i, 0)),
                   pl.BlockSpec((1, tr, cols), lambda i, j, p: (jnp.maximum(j - 1, 0), i, 0))])
    return pl.pallas_call(
        body, name=name, grid_spec=gs,
        out_shape=[_big((hs, cols), F32), _big((nb - 1, hs, cols), BF16)],
        compiler_params=_cp(("arbitrary", "arbitrary"), 32),
    )(place, *_hbm(grad, got))


def _chip_sum(place, pair_f, got_b, name):
    nb, hs, cols = got_b.shape
    tr = _row_tile(hs)
    nr = hs // tr

    def body(pl_ref, pf_ref, gb_ref, o_ref):
        acc = pf_ref[...]
        for k in range(nb):
            acc = acc + gb_ref[k].astype(F32)
        o_ref[...] = acc

    gs = pltpu.PrefetchScalarGridSpec(
        num_scalar_prefetch=1, grid=(nr,),
        in_specs=[pl.BlockSpec((tr, cols), lambda i, p: (i, 0)),
                  pl.BlockSpec((nb, tr, cols), lambda i, p: (0, i, 0))],
        out_specs=pl.BlockSpec((tr, cols), lambda i, p: (p[0] * nr + i, 0)))
    return pl.pallas_call(
        body, name=name, grid_spec=gs,
        out_shape=_big((2 * hs, cols), F32),
        compiler_params=_cp(("arbitrary",), 32),
    )(place, *_hbm(pair_f, got_b))


def _adam_math(w, g, m, v):
    m2 = ADAM_B1 * m + (1.0 - ADAM_B1) * g
    v2 = ADAM_B2 * v + (1.0 - ADAM_B2) * (g * g)
    m_hat = m2 / (1.0 - ADAM_B1 ** ADAM_STEP)
    v_hat = v2 / (1.0 - ADAM_B2 ** ADAM_STEP)
    delta = -ADAM_LR * (m_hat / (jnp.sqrt(v_hat) + ADAM_EPS) + ADAM_WD * w)
    return delta, m2, v2


def _adam_big(w, g, m, v, name):
    rows, cols = w.shape
    tr = _row_tile(rows)

    def body(w_ref, g_ref, m_ref, v_ref, d_out, m_out, v_out):
        d, m2, v2 = _adam_math(w_ref[...], g_ref[...], m_ref[...], v_ref[...])
        d_out[...] = d
        m_out[...] = m2
        v_out[...] = v2

    spec = pl.BlockSpec((tr, cols), lambda i: (i, 0))
    return pl.pallas_call(
        body, name=name, grid=(rows // tr,), in_specs=[spec] * 4, out_specs=[spec] * 3,
        out_shape=[_big(w.shape, F32)] * 3,
        compiler_params=_cp(("arbitrary",), 48),
    )(*_hbm(w, g, m, v))


def _ada_grad_adam(c8t, dmod_sh, w, m, v):
    rows, cols = w.shape
    tr = _row_tile(rows) // 2

    def body(ct_ref, dm_ref, w_ref, m_ref, v_ref, g_out, d_out, m_out, v_out):
        g = None
        for b in range(N_DEV):
            term = ct_ref[:, b:b + 1] * dm_ref[b:b + 1, :]
            g = term if g is None else g + term
        d, m2, v2 = _adam_math(w_ref[...], g, m_ref[...], v_ref[...])
        g_out[...] = g
        d_out[...] = d
        m_out[...] = m2
        v_out[...] = v2

    spec = pl.BlockSpec((tr, cols), lambda i: (i, 0))
    return pl.pallas_call(
        body, name="ada_grad_adam", grid=(rows // tr,),
        in_specs=[pl.BlockSpec((tr, N_DEV), lambda i: (i, 0)), _full((N_DEV, cols)), spec, spec, spec],
        out_specs=[spec] * 4, out_shape=[_big(w.shape, F32)] * 4,
        compiler_params=_cp(("arbitrary",), 32),
    )(c8t, dmod_sh, *_hbm(w, m, v))


def _tok_tile(t):
    return min(TOK_TILE, t)


def _mix_in_fwd(chip1, x, modr, g_pre, b_in4, later, w_in_buf, w_out_buf):
    T = x.shape[0]
    tm = min(MIXIN_TILE, T)
    nt = T // tm
    nb = IN_COLS // N_CHIPS
    nl = len(later)

    def body(*refs):
        ch_ref, x_ref, mod_ref, g_ref, b_ref = refs[:5]
        l_ins = refs[5:5 + nl]
        p_ref, h_ref = refs[7 + nl:9 + nl]
        l_outs = refs[9 + nl:9 + 2 * nl]
        win_ref, wout_ref = refs[9 + 2 * nl:11 + 2 * nl]
        h_all, wblk, lsem, is_sem, ir_sem, os_sem, or_sem = refs[11 + 2 * nl:]
        k, i = pl.program_id(0), pl.program_id(1)
        chip = ch_ref[0]

        @pl.when(k == 0)
        def _():
            for src, dst in zip(l_ins, l_outs):
                dst[0] = src[...].astype(BF16)

        def load_block(blk):
            cp = pltpu.make_async_copy(win_ref.at[blk], wblk, lsem)
            cp.start()
            cp.wait()

        @pl.when((k == 0) & (i == 0))
        def _():
            _gather_start([win_ref], is_sem, ir_sem, relations=(0, 1))
            load_block(chip)

        for r in range(N_CHIPS - 1):
            @pl.when((k == r + 1) & (i == 0))
            def _(r=r):
                _gather_arrive([win_ref], r, is_sem, ir_sem)
                if r == 0:
                    _gather_start([win_ref], is_sem, ir_sem, relations=(2,))
                if r == 1:
                    _gather_start([wout_ref], os_sem, or_sem)
                load_block(chip ^ (r + 1))

        rows = pl.ds(pl.multiple_of(i * tm, tm), tm)

        @pl.when(k == 0)
        def _():
            xv = x_ref[...]
            rstd = lax.rsqrt(jnp.mean(xv * xv, axis=-1, keepdims=True) + RMS_EPS)
            h = (xv * rstd) * g_ref[...] * (1.0 + mod_ref[1:2, :]) + mod_ref[0:1, :]
            hb = h.astype(BF16)
            h_ref[...] = hb
            h_all[rows, :] = hb

        p_ref[...] = _dot(h_all[rows, :], wblk[...]) + b_ref[chip ^ k]

        @pl.when((k == N_CHIPS - 1) & (i == nt - 1))
        def _():
            _gather_sends_done([win_ref], is_sem, ir_sem)
            _gather_finish([wout_ref], os_sem, or_sem)

    hbm = pl.BlockSpec(memory_space=pl.ANY)
    first_pass = lambda k, i, ch: (jnp.where(k == 0, i, nt - 1), 0)
    own_slot = lambda k, i, ch: (ch[0], jnp.where(k == 0, i, nt - 1), 0)
    gs = pltpu.PrefetchScalarGridSpec(
        num_scalar_prefetch=1, grid=(N_CHIPS, nt),
        in_specs=[pl.BlockSpec((tm, D_MODEL), first_pass), pl.BlockSpec((6, D_MODEL), lambda k, i, ch: (0, 0)),
                  pl.BlockSpec((1, D_MODEL), lambda k, i, ch: (0, 0)),
                  pl.BlockSpec((N_CHIPS, 1, nb), lambda k, i, ch: (0, 0, 0))]
        + [pl.BlockSpec((w.shape[0] // nt, w.shape[1]), first_pass) for w in later] + [hbm, hbm],
        out_specs=[pl.BlockSpec((tm, nb), lambda k, i, ch: (i, ch[0] ^ k)), pl.BlockSpec((tm, D_MODEL), first_pass)]
        + [pl.BlockSpec((1, w.shape[0] // nt, w.shape[1]), own_slot) for w in later] + [hbm, hbm],
        scratch_shapes=[pltpu.VMEM((T, D_MODEL), BF16), pltpu.VMEM((D_MODEL, nb), BF16), pltpu.SemaphoreType.DMA]
        + _gather_sems(1) + _gather_sems(1))
    outs = pl.pallas_call(
        body, name="mix_in_fwd", grid_spec=gs,
        out_shape=[_big((T, IN_COLS), F32), _big((T, D_MODEL), BF16)]
        + [_big((N_CHIPS,) + w.shape, BF16) for w in later]
        + [_big(w_in_buf.shape, BF16), _big(w_out_buf.shape, BF16)],
        input_output_aliases={5 + nl: 2 + nl, 6 + nl: 3 + nl},
        compiler_params=_cp(("arbitrary", "arbitrary"), 48),
    )(chip1, _hbm(x), modr, g_pre, b_in4, *[_hbm(w) for w in later], _hbm(w_in_buf), _hbm(w_out_buf))
    return outs[0], outs[1], outs[2:2 + nl], outs[2 + nl], outs[3 + nl]


def _mixers_fwd(p, wdw, vecs, lbl, w_out, x, modr, g_post, g_ffn, gbufs):
    T = p.shape[0]
    tm = _tok_tile(T)
    nt = T // tm
    nch = tm // CHUNK
    ng = len(gbufs)
    n_in, n_out = 12, 7

    def body(*refs):
        (p_ref, wdw_ref, bdw_ref, gain_ref, bias_ref, gout_ref, lbl_ref, wout_ref, x_ref, mod_ref, gp_ref,
         gf_ref) = refs[:n_in]
        cat_ref, ys_ref, o_ref, st_ref, y_ref, x1_ref, h2_ref = refs[n_in + ng:n_in + ng + n_out]
        gout_bufs = refs[n_in + ng + n_out:n_in + 2 * ng + n_out]
        (ubuf, state, qt_s, kt_s, kh_s, v_s, egl_s, lower_s, gmat_s, gssem,
         grsem) = refs[n_in + 2 * ng + n_out:]
        i = pl.program_id(0)

        @pl.when(i == 0)
        def _():
            _ring_start(gout_bufs, gssem, grsem)
            lower_s[...], _, _ = _chunk_masks(tm)
            gmat_s[...] = _gn_matrix()
            state[...] = jnp.zeros(state.shape, F32)
            ubuf[0:HALO, :] = jnp.zeros((HALO, CONV_CH), F32)
            ubuf[HALO + tm:HALO + tm + SUB, :] = jnp.zeros((SUB, CONV_CH), F32)

        @pl.when(i > 0)
        def _():
            ubuf[0:HALO, :] = ubuf[tm:tm + HALO, :]

        ubuf[HALO:HALO + tm, :] = p_ref[:, 0:CONV_CH] * _sig(p_ref[:, CONV_CH:2 * CONV_CH])
        for r in range(tm // CONV_ROWS):
            rows = slice(r * CONV_ROWS, (r + 1) * CONV_ROWS)
            for lb_ in range(CONV_CH // LANE):
                lanes = slice(lb_ * LANE, (lb_ + 1) * LANE)
                ys_ref[rows, lanes] = bdw_ref[:, lanes] + _tap_conv(ubuf, wdw_ref, r * CONV_ROWS, CONV_FWD_TAPS, lanes)
        gmat = gmat_s[...]
        yv = ys_ref[...]
        d = yv - _gmean(yv, gmat)
        rs = lax.rsqrt(_gmean(d * d, gmat) + GN_EPS)
        z = d * rs * gain_ref[...] + bias_ref[...]
        cat_ref[:, 0:CONV_CH] = (z * _sig(z)).astype(BF16)

        lb, _ = _lower_bound(lbl_ref)
        o0 = 2 * CONV_CH
        pr = _hgrn_prep(p_ref[:, o0:o0 + HGRN_W], p_ref[:, o0 + HGRN_W:o0 + 2 * HGRN_W], lb, lower_s[...])
        qt_s[...] = pr["qt"].astype(BF16)
        kt_s[...] = pr["kt"].astype(BF16)
        kh_s[...] = pr["kh"].astype(BF16)
        v_s[...] = p_ref[:, o0 + 2 * HGRN_W:o0 + 3 * HGRN_W].astype(BF16)
        egl_s[...] = jnp.exp(pr["Gl"])
        tri = _tri()

        def chunk(ci, carry):
            r0 = pl.multiple_of(ci * CHUNK, CHUNK)
            rows = pl.ds(r0, CHUNK)
            for h in range(N_HEADS):
                ls = pl.ds(h * HEAD_D, HEAD_D)
                qc, kc, hc, vc = qt_s[rows, ls], kt_s[rows, ls], kh_s[rows, ls], v_s[rows, ls]
                s0 = state[h]
                s0b = s0.astype(BF16)
                st_ref[ci, h] = s0
                att =jnp.where(tri, _dot_nt(qc, kc), 0.0).astype(BF16)
                o_ref[rows, ls] = _dot(att, vc) + _dot_nt(qc, s0b)
                state[h] = s0 * egl_s[pl.ds(r0, 1), ls] + _dot_tn(vc, hc)
            return carry

        lax.fori_loop(0, nch, chunk, 0, unroll=min(CHUNK_UNROLL, nch))
        for h in range(N_HEADS):
            sl = slice(h * HEAD_D, (h + 1) * HEAD_D)
            oh = o_ref[:, sl]
            gh = p_ref[:, o0 + 3 * HGRN_W + h * HEAD_D:o0 + 3 * HGRN_W + (h + 1) * HEAD_D]
            rsh = lax.rsqrt(jnp.mean(oh * oh, axis=-1, keepdims=True) + RMS_EPS)
            hg = (oh * rsh) * gout_ref[:, sl] * (gh * _sig(gh))
            cat_ref[:, CONV_CH + h * HEAD_D:CONV_CH + (h + 1) * HEAD_D] = hg.astype(BF16)

        gate_gain = mod_ref[2:3, :] * gp_ref[...]
        gain_scale = gf_ref[...] * (1.0 + mod_ref[4:5, :])
        for rows in _row_chains(tm):
            yv = _dot(cat_ref[rows, :], wout_ref[...])
            y_ref[rows, :] = yv
            rsy = lax.rsqrt(jnp.mean(yv * yv, axis=-1, keepdims=True) + RMS_EPS)
            x1 = x_ref[rows, :] + (yv * rsy) * gate_gain
            x1_ref[rows, :] = x1
            rs1 = lax.rsqrt(jnp.mean(x1 * x1, axis=-1, keepdims=True) + RMS_EPS)
            h2_ref[rows, :] = ((x1 * rs1) * gain_scale + mod_ref[3:4, :]).astype(BF16)

        @pl.when(i == min(nt - 1, nt // 2 + 1))
        def _():
            _ring_forward(gout_bufs, gssem, grsem)

        @pl.when(i == nt - 1)
        def _():
            _ring_finish(gout_bufs, gssem, grsem)

    tile = lambda cols: pl.BlockSpec((tm, cols), lambda i: (i, 0))
    hbm = pl.BlockSpec(memory_space=pl.ANY)
    outs = pl.pallas_call(
        body, name="mixers_fwd", grid=(nt,),
        in_specs=[tile(IN_COLS), _full((HALO, CONV_CH))] + [_full((1, CONV_CH))] * 4 + [_full((2, HGRN_W))]
        + [_full((D_MODEL, D_MODEL)), tile(D_MODEL), _full((6, D_MODEL)), _full((1, D_MODEL)), _full((1, D_MODEL))]
        + [hbm] * ng,
        out_specs=[tile(D_MODEL), tile(CONV_CH), tile(HGRN_W),
                   pl.BlockSpec((nch, N_HEADS, HEAD_D, HEAD_D), lambda i: (i, 0, 0, 0)),
                   tile(D_MODEL), tile(D_MODEL), tile(D_MODEL)] + [hbm] * ng,
        out_shape=[_big((T, D_MODEL), BF16), _big((T, CONV_CH), F32), _big((T, HGRN_W), F32),
                   _big((T // CHUNK, N_HEADS, HEAD_D, HEAD_D), F32), _big((T, D_MODEL), F32),
                   _big((T, D_MODEL), F32), _big((T, D_MODEL), BF16)] + [_big(b.shape, BF16) for b in gbufs],
        input_output_aliases={n_in + t: n_out + t for t in range(ng)},
        scratch_shapes=[pltpu.VMEM((tm + HALO + SUB, CONV_CH), F32), pltpu.VMEM((N_HEADS, HEAD_D, HEAD_D), F32),
                        pltpu.VMEM((tm, HGRN_W), BF16), pltpu.VMEM((tm, HGRN_W), BF16),
                        pltpu.VMEM((tm, HGRN_W), BF16), pltpu.VMEM((tm, HGRN_W), BF16),
                        pltpu.VMEM((tm, HGRN_W), F32), pltpu.VMEM((tm, tm), BF16),
                        pltpu.VMEM((CONV_CH, CONV_CH), BF16)] + _ring_sems(ng),
        compiler_params=_cp(("arbitrary",), 56),
    )(_hbm(p), wdw, *vecs, lbl, *_hbm(w_out, x), modr, g_post, g_ffn, *[_hbm(b) for b in gbufs])
    return outs[:n_out], outs[n_out:]


def _row_chains(rows, n=2):
    step = rows // n
    return [slice(k * step, (k + 1) * step) for k in range(n)]


def _ffn_blocks():
    return D_FF // FFN_BLOCK, (D_FF // N_CHIPS) // FFN_BLOCK


def _ffn_fwd(h2, w_up_g, w_down, x1, target, modr, g_post):
    T = h2.shape[0]
    tm = min(FFN_TILE, T)
    fb = FFN_BLOCK
    nj, per = _ffn_blocks()

    def body(h_ref, wu_ref, wd_ref, x1_ref, t_ref, mod_ref, g_ref, r_ref, dy2_ref, dx2_ref, st_ref, acc):
        i, j = pl.program_id(0), pl.program_id(1)

        @pl.when((i == 0) & (j == 0))
        def _():
            st_ref[...] = jnp.zeros(st_ref.shape, F32)

        @pl.when(j == 0)
        def _():
            acc[...] = jnp.zeros(acc.shape, F32)

        for rows in _row_chains(tm):
            ra = jnp.maximum(_dot(h_ref[rows, :], wu_ref[0]), 0.0)
            rb = (ra * ra).astype(BF16)
            r_ref[rows, :] = rb
            acc[rows, :] += _dot(rb, wd_ref[...])

        @pl.when(j == nj - 1)
        def _():
            y2 = acc[...]
            rs = lax.rsqrt(jnp.mean(y2 * y2, axis=-1, keepdims=True) + RMS_EPS)
            nh = y2 * rs
            gp, gt = g_ref[...], mod_ref[5:6, :]
            gate_gain = gt * gp
            err = x1_ref[...] + nh * gate_gain - t_ref[...]
            dx2 = err * (1.0 / D_MODEL)
            dx2_ref[...] = dx2
            st_ref[0:1, :] += _colsum(err * err)
            s_dn = _colsum(dx2 * nh)
            st_ref[1:2, :] += s_dn * gp
            st_ref[2:3, :] += s_dn * gt
            dy2_ref[...] = _rms_bwd(dx2 * gate_gain, nh, rs).astype(BF16)

    tile = pl.BlockSpec((tm, D_MODEL), lambda i, j: (i, 0))
    return pl.pallas_call(
        body, name="ffn_fwd", grid=(T // tm, nj),
        in_specs=[tile, pl.BlockSpec((1, D_MODEL, fb), lambda i, j: (j // per, 0, j % per)),
                  pl.BlockSpec((fb, D_MODEL), lambda i, j: (j, 0)), tile, tile,
                  _full((6, D_MODEL)), _full((1, D_MODEL))],
        out_specs=[pl.BlockSpec((tm, fb), lambda i, j: (i, j)), tile, tile, _full((8, D_MODEL))],
        out_shape=[_big((T, D_FF), BF16), _big((T, D_MODEL), BF16), _big((T, D_MODEL), F32),
                   jax.ShapeDtypeStruct((8, D_MODEL), F32)],
        scratch_shapes=[pltpu.VMEM((tm, D_MODEL), F32)],
        compiler_params=_cp(("arbitrary", "arbitrary"), 56),
    )(*_hbm(h2, w_up_g, w_down, x1, target), modr, g_post)


def _rms_bwd(dxn, xn, rs):
    return rs * (dxn - xn * jnp.mean(dxn * xn, axis=-1, keepdims=True))


def _ffn_bwd(dy2, r, x1, dx2, w_up_g, w_down, modr, g_ffn):
    T = dx2.shape[0]
    tm = min(FFN_TILE, T)
    fb = FFN_BLOCK
    nj, per = _ffn_blocks()

    def body(dy2_ref, r_ref, x1_ref, dx2_ref, wu_ref, wd_ref, mod_ref, gf_ref, da_ref, dx1_ref, st_ref, dh_s):
        i, j = pl.program_id(0), pl.program_id(1)

        @pl.when((i == 0) & (j == 0))
        def _():
            st_ref[...] = jnp.zeros(st_ref.shape, F32)

        @pl.when(j == 0)
        def _():
            dh_s[...] = jnp.zeros(dh_s.shape, F32)

        for rows in _row_chains(tm):
            ra = jnp.sqrt(r_ref[rows, :].astype(F32))
            da = (_dot_nt(dy2_ref[rows, :], wd_ref[...]) * (2.0 * ra)).astype(BF16)
            da_ref[rows, :] = da
            dh_s[rows, :] += _dot_nt(da, wu_ref[0])

        @pl.when(j == nj - 1)
        def _():
            dh = dh_s[...]
            x1v = x1_ref[...]
            rs1 = lax.rsqrt(jnp.mean(x1v * x1v, axis=-1, keepdims=True) + RMS_EPS)
            xn = x1v * rs1
            gf, scale1 = gf_ref[...], 1.0 + mod_ref[4:5, :]
            st_ref[0:1, :] += _colsum(dh)
            s_dh = _colsum(dh * xn)
            st_ref[1:2, :] += s_dh * gf
            st_ref[2:3, :] += s_dh * scale1
            dx1_ref[...] = dx2_ref[...] + _rms_bwd(dh * (scale1 * gf), xn, rs1)

    tile = pl.BlockSpec((tm, D_MODEL), lambda i, j: (i, 0))
    ftile = pl.BlockSpec((tm, fb), lambda i, j: (i, j))
    return pl.pallas_call(
        body, name="ffn_bwd", grid=(T // tm, nj),
        in_specs=[tile, ftile, tile, tile, pl.BlockSpec((1, D_MODEL, fb), lambda i, j: (j // per, 0, j % per)),
                  pl.BlockSpec((fb, D_MODEL), lambda i, j: (j, 0)), _full((6, D_MODEL)), _full((1, D_MODEL))],
        out_specs=[ftile, tile, _full((8, D_MODEL))],
        out_shape=[_big((T, D_FF), BF16), _big((T, D_MODEL), F32), jax.ShapeDtypeStruct((8, D_MODEL), F32)],
        scratch_shapes=[pltpu.VMEM((tm, D_MODEL), F32)],
        compiler_params=_cp(("arbitrary", "arbitrary"), 56),
    )(*_hbm(dy2, r, x1, dx2, w_up_g, w_down), modr, g_ffn)


def _mix_out_bwd(dx1, y, cat, w_out, modr, g_post, swap):
    T = dx1.shape[0]
    tm = _tok_tile(T)
    nt = T // tm
    ns = len(swap)

    def body(*refs):
        dx1_ref, y_ref, cat_ref, w_ref, mod_ref, gp_ref = refs[:6]
        s_ins = refs[6:6 + ns]
        dcat_ref, st_ref, gw_ref = refs[6 + ns:9 + ns]
        s_outs = refs[9 + ns:9 + 2 * ns]
        gacc, gsem, pssem, prsem = refs[9 + 2 * ns:]
        i = pl.program_id(0)

        @pl.when(i == 0)
        def _():
            for cp in _pair_copies(s_ins, s_outs, pssem, prsem):
                cp.start()
            st_ref[...] = jnp.zeros(st_ref.shape, F32)
            gacc[...] = jnp.zeros(gacc.shape, F32)

        dxv, yv = dx1_ref[...], y_ref[...]
        rs = lax.rsqrt(jnp.mean(yv * yv, axis=-1, keepdims=True) + RMS_EPS)
        nh = yv * rs
        st_ref[0:1, :] += _colsum(dxv * (nh * gp_ref[...]))
        dn = dxv * mod_ref[2:3, :]
        st_ref[1:2, :] += _colsum(dn * nh)
        dy = _rms_bwd(dn * gp_ref[...], nh, rs).astype(BF16)
        dcat_ref[...] = _dot_nt(dy, w_ref[...])
        for cols in _row_chains(D_MODEL):
            gacc[:, cols] += _dot_tn(cat_ref[...], dy[:, cols])

        @pl.when(i == nt - 1)
        def _():
            out = pltpu.make_async_copy(gacc, gw_ref, gsem)
            out.start()
            copies = _pair_copies(s_ins, s_outs, pssem, prsem)
            for cp in copies:
                cp.wait_recv()
            for cp in copies:
                cp.wait_send()
            out.wait()

    tile = pl.BlockSpec((tm, D_MODEL), lambda i: (i, 0))
    hbm = pl.BlockSpec(memory_space=pl.ANY)
    outs = pl.pallas_call(
        body, name="mix_out_bwd", grid=(nt,),
        in_specs=[tile, tile, tile, _full((D_MODEL, D_MODEL)), _full((6, D_MODEL)), _full((1, D_MODEL))]
        + [hbm] * ns,
        out_specs=[tile, _full((8, D_MODEL)), hbm] + [hbm] * ns,
        out_shape=[_big((T, D_MODEL), F32), jax.ShapeDtypeStruct((8, D_MODEL), F32), _big((D_MODEL, D_MODEL), F32)]
        + _pair_shapes(swap),
        scratch_shapes=[pltpu.VMEM((D_MODEL, D_MODEL), F32), pltpu.SemaphoreType.DMA] + _pair_sems(ns),
        compiler_params=_cp(("arbitrary",), 48),
    )(*_hbm(dx1, y, cat, w_out), modr, g_post, *[_hbm(g) for g in swap])
    return outs[:3], outs[3:]


def _mixers_bwd(p, dcat, ys, o, states, h1, wdw, vecs, lbl, pairs_b):
    T = p.shape[0]
    tm = min(MIXB_TILE, T)
    nt = T // tm
    nch = tm // CHUNK
    hpt = tm // HALO
    nx = len(pairs_b)
    nb = IN_COLS // N_CHIPS

    def body(*refs):
        (p_ref, ph_ref, dcat_ref, ys_ref, o_ref, st_ref, h1_ref, wdw_ref, bdw_ref, gain_ref, bias_ref, gout_ref,
         lbl_ref) = refs[:13]
        x_ins = refs[13:13 + nx]
        dp_ref, sb_ref, s5_ref, dw_ref = refs[13 + nx:17 + nx]
        x_outs = refs[17 + nx:17 + 2 * nx]
        gin_ref, ginb_ref = refs[17 + 2 * nx], refs[18 + 2 * nx]
        (ubuf, dybuf, carry, dstate, qt_s, kt_s, kh_s, v_s, do_s, egl_s, dqt_s, dkt_s, dkh_s, dv_s, dgl_s,
         dsh, dw8, dshift, lower_s, upper_s, same_s, gmat_s, gacc, gacc_b, dp_prev, h1_prev, gsem, gbsem, xssem,
         xrsem) = refs[19 + 2 * nx:]
        i = pl.program_id(0)
        tile_idx = nt - 1 - i

        @pl.when(i == 0)
        def _():
            for cp in _xchg_copies(x_ins, x_outs, xssem, xrsem):
                cp.start()
            gacc[...] = jnp.zeros(gacc.shape, F32)
            dstate[...] = jnp.zeros(dstate.shape, F32)
            carry[...] = jnp.zeros(carry.shape, F32)
            sb_ref[...] = jnp.zeros(sb_ref.shape, F32)
            s5_ref[...] = jnp.zeros(s5_ref.shape, F32)
            dw_ref[...] = jnp.zeros(dw_ref.shape, F32)
            dw8[...] = jnp.zeros(dw8.shape, F32)
            lower_s[...], upper_s[...], same_s[...] = _chunk_masks(tm)
            gmat_s[...] = _gn_matrix()
            dsh[0:SUB, :] = jnp.zeros((SUB, CONV_CH), F32)
            dsh[SUB + tm:2 * SUB + tm, :] = jnp.zeros((SUB, CONV_CH), F32)
            ubuf[HALO + tm:HALO + tm + SUB, :] = jnp.zeros((SUB, CONV_CH), F32)
            dp_prev[...] = jnp.zeros(dp_prev.shape, BF16)
            h1_prev[...] = jnp.zeros(h1_prev.shape, BF16)

        n_pieces = (tm // CONV_ROWS) * (CONV_CH // LANE)
        per_block = n_pieces // N_CHIPS
        prow = D_MODEL // per_block

        def w_in_grad_piece(k):
            j, part = k // per_block, k % per_block
            rows_k = slice(part * prow, (part + 1) * prow)
            gacc[j, rows_k, :] += _dot_tn(h1_prev[:, rows_k], dp_prev[:, j * nb:(j + 1) * nb])

        uh = ph_ref[:, 0:CONV_CH] * _sig(ph_ref[:, CONV_CH:2 * CONV_CH])
        ubuf[0:HALO, :] = jnp.where(tile_idx > 0, uh, 0.0)
        ubuf[HALO:HALO + tm, :] = p_ref[:, 0:CONV_CH] * _sig(p_ref[:, CONV_CH:2 * CONV_CH])
        gmat = gmat_s[...]
        gain = gain_ref[...]
        yv = ys_ref[...]
        d = yv - _gmean(yv, gmat)
        rs = lax.rsqrt(_gmean(d * d, gmat) + GN_EPS)
        yn = d * rs
        z = yn * gain + bias_ref[...]
        sz = _sig(z)
        dz = dcat_ref[:, 0:CONV_CH] * (sz * (1.0 + z * (1.0 - sz)))
        dyn = dz * gain
        dyc = rs * (dyn - _gmean(dyn, gmat) - yn * _gmean(dyn * yn, gmat))
        s5_ref[0:1, :] += _colsum(dyc)
        s5_ref[1:2, :] += _colsum(dz * yn)
        s5_ref[2:3, :] += _colsum(dz)
        dybuf[tm:tm + HALO, :] = carry[...]
        dybuf[0:tm, :] = dyc
        dsh[SUB:SUB + tm, :] = dyc
        carry[...] = dyc[0:HALO, :]
        for b in range(SUB):
            dshift[...] = dsh[SUB - b:2 * SUB - b + tm, :]
            for j, off in CONV_FWD_TAPS:
                if off % SUB == b:
                    prod = dshift[...] * ubuf[off - b:off - b + tm + SUB, :]
                    dw8[j] += jnp.sum(prod.reshape((tm + SUB) // SUB, SUB, CONV_CH), axis=0)
        for r in range(tm // CONV_ROWS):
            rows = slice(r * CONV_ROWS, (r + 1) * CONV_ROWS)
            for lb_ in range(CONV_CH // LANE):
                lanes = slice(lb_ * LANE, (lb_ + 1) * LANE)
                glanes = slice(CONV_CH + lb_ * LANE, CONV_CH + (lb_ + 1) * LANE)
                w_in_grad_piece(r * (CONV_CH // LANE) + lb_)
                acc = _tap_conv(dybuf, wdw_ref, r * CONV_ROWS, CONV_BWD_TAPS, lanes)
                val = p_ref[rows, lanes]
                sg = _sig(p_ref[rows, glanes])
                dval = acc * sg
                dgate = acc * val * (sg * (1.0 - sg))
                dp_ref[rows, lanes] = dval.astype(BF16)
                dp_ref[rows, glanes] = dgate.astype(BF16)
                sb_ref[0:1, lanes] += _colsum(dval)
                sb_ref[0:1, glanes] += _colsum(dgate)

        o0 = 2 * CONV_CH
        for h in range(N_HEADS):
            sl = slice(h * HEAD_D, (h + 1) * HEAD_D)
            gsl = slice(o0 + 3 * HGRN_W + h * HEAD_D, o0 + 3 * HGRN_W + (h + 1) * HEAD_D)
            oh = o_ref[:, sl]
            gh = p_ref[:, gsl]
            dh = dcat_ref[:, CONV_CH + h * HEAD_D:CONV_CH + (h + 1) * HEAD_D]
            gout = gout_ref[:, sl]
            rsh = lax.rsqrt(jnp.mean(oh * oh, axis=-1, keepdims=True) + RMS_EPS)
            on = oh * rsh
            sgg = _sig(gh)
            dgh = dh * (on * gout) * (sgg * (1.0 + gh * (1.0 - sgg)))
            dm = dh * (gh * sgg)
            s5_ref[3:4, sl] += _colsum(dm * on)
            do_s[:, sl] = _rms_bwd(dm * gout, on, rsh).astype(BF16)
            dp_ref[:, gsl] = dgh.astype(BF16)
            sb_ref[2:3, CONV_CH + h * HEAD_D:CONV_CH + (h + 1) * HEAD_D] += _colsum(dgh)

        lb, _ = _lower_bound(lbl_ref)
        pq = p_ref[:, o0:o0 + HGRN_W]
        pr = _hgrn_prep(pq, p_ref[:, o0 + HGRN_W:o0 + 2 * HGRN_W], lb, lower_s[...])
        qt_s[...] = pr["qt"].astype(BF16)
        kt_s[...] = pr["kt"].astype(BF16)
        kh_s[...] = pr["kh"].astype(BF16)
        v_s[...] = p_ref[:, o0 + 2 * HGRN_W:o0 + 3 * HGRN_W].astype(BF16)
        egl_s[...] = jnp.exp(pr["Gl"])
        tri = _tri()

        def chunk(it, c_):
            ci = nch - 1 - it
            r0 = pl.multiple_of(ci * CHUNK, CHUNK)
            rows = pl.ds(r0, CHUNK)
            for h in range(N_HEADS):
                ls = pl.ds(h * HEAD_D, HEAD_D)
                qc, kc, hc, vc = qt_s[rows, ls], kt_s[rows, ls], kh_s[rows, ls], v_s[rows, ls]
                dob = do_s[rows, ls]
                s0 = st_ref[ci, h]
                s0b = s0.astype(BF16)
                ds1 = dstate[h]
                ds1b = ds1.astype(BF16)
                egl = egl_s[pl.ds(r0, 1), ls]
                att = jnp.where(tri, _dot_nt(qc, kc), 0.0).astype(BF16)
                datt = jnp.where(tri, _dot_nt(dob, vc), 0.0).astype(BF16)
                dv_s[rows, ls] = _dot_tn(att, dob) + _dot_nt(hc, ds1b)
                dqt_s[rows, ls] = _dot(datt, kc) + _dot(dob, s0b)
                dkt_s[rows, ls] = _dot_tn(datt, qc)
                dkh_s[rows, ls] = _dot(vc, ds1b)
                dgl = egl * _colsum(ds1 * s0)
                dgl_s[rows, ls] = jnp.broadcast_to(dgl, (CHUNK, HEAD_D))
                dstate[h] = ds1 * egl + _dot_tn(dob, qc)
            return c_

        lax.fori_loop(0, nch, chunk, 0, unroll=min(CHUNK_UNROLL, nch))
        dqt, dkt, dkh = dqt_s[...], dkt_s[...], dkh_s[...]
        dk = dkt * pr["enG"] + dkh * pr["eGlG"]
        khk = dkh * kh_s[...].astype(F32)
        dG = dqt * qt_s[...].astype(F32) - dkt * kt_s[...].astype(F32) - khk
        dlogf = _mm3(upper_s[...], dG) + _mm3(same_s[...], khk) + dgl_s[...]
        df = dlogf / pr["f"] - dk
        sf, sq = pr["sf"], pr["sq"]
        s5_ref[4:5, :] += _colsum(df * (1.0 - sf))
        dfl = df * (1.0 - lb) * (sf * (1.0 - sf))
        dq = (dqt * pr["eG"]) * (sq * (1.0 + pq * (1.0 - sq)))
        dvv = dv_s[...]
        dp_ref[:, o0:o0 + HGRN_W] = dq.astype(BF16)
        dp_ref[:, o0 + HGRN_W:o0 + 2 * HGRN_W] = dfl.astype(BF16)
        dp_ref[:, o0 + 2 * HGRN_W:o0 + 3 * HGRN_W] = dvv.astype(BF16)
        sb_ref[1:2, 0:HGRN_W] += _colsum(dq)
        sb_ref[1:2, HGRN_W:2 * HGRN_W] += _colsum(dfl)
        sb_ref[2:3, 0:HGRN_W] += _colsum(dvv)

        dp_prev[...] = dp_ref[...]
        h1_prev[...] = h1_ref[...]

        @pl.when(i == nt - 1)
        def _():
            for k in range(n_pieces):
                w_in_grad_piece(k)
            out = pltpu.make_async_copy(gacc, gin_ref, gsem)
            out.start()
            for j in range(N_CHIPS):
                gacc_b[j] = gacc[j].astype(BF16)
            out_b = pltpu.make_async_copy(gacc_b, ginb_ref, gbsem)
            out_b.start()
            for j in range(CONV_K):
                dw_ref[j:j + 1, :] = _colsum(dw8[j])
            copies = _xchg_copies(x_ins, x_outs, xssem, xrsem)
            for cp in copies:
                cp.wait_recv()
            for cp in copies:
                cp.wait_send()
            out.wait()
            out_b.wait()

    rev = lambda cols: pl.BlockSpec((tm, cols), lambda i: (nt - 1 - i, 0))
    halo = pl.BlockSpec((HALO, 2 * CONV_CH), lambda i: (jnp.maximum((nt - 1 - i) * hpt - 1, 0), 0))
    wide = lambda n: pltpu.VMEM((tm, HGRN_W), n)
    hbm = pl.BlockSpec(memory_space=pl.ANY)
    outs = pl.pallas_call(
        body, name="mixers_bwd", grid=(nt,),
        in_specs=[rev(IN_COLS), halo, rev(D_MODEL), rev(CONV_CH), rev(HGRN_W),
                  pl.BlockSpec((nch, N_HEADS, HEAD_D, HEAD_D), lambda i: (nt - 1 - i, 0, 0, 0)), rev(D_MODEL),
                  _full((HALO, CONV_CH))] + [_full((1, CONV_CH))] * 4 + [_full((2, HGRN_W))] + [hbm] * nx,
        out_specs=[rev(IN_COLS), _full((8, D_MODEL)), _full((8, CONV_CH)), _full((HALO, CONV_CH))]
        + [hbm] * (nx + 2),
        out_shape=[_big((T, IN_COLS), BF16), jax.ShapeDtypeStruct((8, D_MODEL), F32),
                   jax.ShapeDtypeStruct((8, CONV_CH), F32), jax.ShapeDtypeStruct((HALO, CONV_CH), F32)]
        + [_big(pb.shape, BF16) for pb in pairs_b]
        + [_big((N_CHIPS, D_MODEL, nb), F32), _big((N_CHIPS, D_MODEL, nb), BF16)],
        scratch_shapes=[pltpu.VMEM((tm + HALO + SUB, CONV_CH), F32), pltpu.VMEM((tm + HALO, CONV_CH), F32),
                        pltpu.VMEM((HALO, CONV_CH), F32), pltpu.VMEM((N_HEADS, HEAD_D, HEAD_D), F32),
                        wide(BF16), wide(BF16), wide(BF16), wide(BF16), wide(BF16),
                        wide(F32), wide(F32), wide(F32), wide(F32), wide(F32), wide(F32),
                        pltpu.VMEM((tm + 2 * SUB, CONV_CH), F32), pltpu.VMEM((HALO, SUB, CONV_CH), F32),
                        pltpu.VMEM((tm + SUB, CONV_CH), F32), pltpu.VMEM((tm, tm), BF16), pltpu.VMEM((tm, tm), BF16),
                        pltpu.VMEM((tm, tm), BF16), pltpu.VMEM((CONV_CH, CONV_CH), BF16),
                        pltpu.VMEM((N_CHIPS, D_MODEL, nb), F32), pltpu.VMEM((N_CHIPS, D_MODEL, nb), BF16),
                        pltpu.VMEM((tm, IN_COLS), BF16), pltpu.VMEM((tm, D_MODEL), BF16),
                        pltpu.SemaphoreType.DMA, pltpu.SemaphoreType.DMA]
        + _xchg_sems(nx),
        compiler_params=_cp(("arbitrary",), 56),
    )(*_hbm(p, p, dcat, ys, o, states, h1), wdw, *vecs, lbl, *[_hbm(pb) for pb in pairs_b])
    return outs[:4], outs[4:4 + nx], (outs[4 + nx], outs[5 + nx])


def _mix_in_bwd(dp, w_in_g, x, dx1, modr, g_pre, pairs_b):
    T = x.shape[0]
    tm = _tok_tile(T)
    nt = T // tm
    nb = IN_COLS // N_CHIPS
    nx = len(pairs_b)

    def body(*refs):
        dp_ref, w_ref, x_ref, dx1_ref, mod_ref, g_ref = refs[:6]
        x_ins = refs[6:6 + nx]
        gx_ref, st_ref = refs[6 + nx:8 + nx]
        x_outs = refs[8 + nx:8 + 2 * nx]
        xssem, xrsem = refs[8 + 2 * nx:]
        i = pl.program_id(0)

        @pl.when(i == 0)
        def _():
            for cp in _xchg_copies(x_ins, x_outs, xssem, xrsem):
                cp.start()
            st_ref[...] = jnp.zeros(st_ref.shape, F32)

        g, scale1 = g_ref[...], 1.0 + mod_ref[1:2, :]
        for rows in _row_chains(tm):
            dh = None
            for j in range(N_CHIPS):
                part = _dot_nt(dp_ref[rows, j * nb:(j + 1) * nb], w_ref[j])
                dh = part if dh is None else dh + part
            xv = x_ref[rows, :]
            rs = lax.rsqrt(jnp.mean(xv * xv, axis=-1, keepdims=True) + RMS_EPS)
            xn = xv * rs
            st_ref[0:1, :] += _colsum(dh)
            s_dh = _colsum(dh * xn)
            st_ref[1:2, :] += s_dh * g
            st_ref[2:3, :] += s_dh * scale1
            gx_ref[rows, :] = dx1_ref[rows, :] + _rms_bwd(dh * (scale1 * g), xn, rs)

        @pl.when(i == nt - 1)
        def _():
            copies = _xchg_copies(x_ins, x_outs, xssem, xrsem)
            for cp in copies:
                cp.wait_recv()
            for cp in copies:
                cp.wait_send()

    tile = pl.BlockSpec((tm, D_MODEL), lambda i: (i, 0))
    hbm = pl.BlockSpec(memory_space=pl.ANY)
    outs = pl.pallas_call(
        body, name="mix_in_bwd", grid=(nt,),
        in_specs=[pl.BlockSpec((tm, IN_COLS), lambda i: (i, 0)), _full((N_CHIPS, D_MODEL, nb)), tile, tile,
                  _full((6, D_MODEL)), _full((1, D_MODEL))] + [hbm] * nx,
        out_specs=[tile, _full((8, D_MODEL))] + [hbm] * nx,
        out_shape=[_big((T, D_MODEL), F32), jax.ShapeDtypeStruct((8, D_MODEL), F32)]
        + [_big(pb.shape, BF16) for pb in pairs_b],
        scratch_shapes=_xchg_sems(nx),
        compiler_params=_cp(("arbitrary",), 48),
    )(*_hbm(dp, w_in_g, x, dx1), modr, g_pre, *[_hbm(pb) for pb in pairs_b])
    return outs[:2], outs[2:]


def _weight_grad(a, b, a_blocked, b_blocked, name):
    T = a.shape[0]
    tt = min(GRAD_TILE, T)
    nt = T // tt
    ka = a.shape[1] // N_CHIPS if a_blocked else a.shape[1]
    nb = b.shape[1] // N_CHIPS if b_blocked else b.shape[1]

    def body(a_ref, b_ref, o_ref, ob_ref):
        t = pl.program_id(1)
        if nt == 1:
            for cols in _row_chains(nb):
                prod = _dot_tn(a_ref[...], b_ref[:, cols])
                o_ref[0, :, cols] = prod
                ob_ref[0, :, cols] = prod.astype(BF16)
            return

        @pl.when(t == 0)
        def _():
            o_ref[...] = jnp.zeros(o_ref.shape, F32)

        for cols in _row_chains(nb):
            o_ref[0, :, cols] += _dot_tn(a_ref[...], b_ref[:, cols])

        @pl.when(t == nt - 1)
        def _():
            ob_ref[0] = o_ref[0].astype(BF16)

    blk = pl.BlockSpec((1, ka, nb), lambda j, t: (j, 0, 0))
    return pl.pallas_call(
        body, name=name, grid=(N_CHIPS, nt),
        in_specs=[pl.BlockSpec((tt, ka), (lambda j, t: (t, j)) if a_blocked else (lambda j, t: (t, 0))),
                  pl.BlockSpec((tt, nb), (lambda j, t: (t, j)) if b_blocked else (lambda j, t: (t, 0)))],
        out_specs=[blk, blk],
        out_shape=[_big((N_CHIPS, ka, nb), F32), _big((N_CHIPS, ka, nb), BF16)],
        compiler_params=_cp(("arbitrary", "arbitrary"), 48),
    )(*_hbm(a, b))


R_LOSS = 0
R_FFN = 8
R_OUT = 16
R_IN = 24
R_BIN = 32
R_512 = 40
R_DW = 48
N_STAT_ROWS = 80
MOD_ROWS = (R_IN + 0, R_IN + 1, R_OUT + 0, R_FFN + 0, R_FFN + 1, R_LOSS + 1)


def _small_update(gath, params):
    names = ["b_ada", "lb_logits", "g_pre_mix", "b_in", "b_dw", "gn_gain", "gn_bias", "g_hgrn_out", "g_post_mix",
             "g_pre_ffn", "g_post_ffn"]
    flat = []
    for n in names:
        flat += list(params[n])
    n_in = 1 + len(flat)

    def body(*refs):
        g_ref = refs[0]
        prm = {n: refs[1 + 3 * k:4 + 3 * k] for k, n in enumerate(names)}
        outs = refs[n_in:]
        loss_ref, dmod_ref, dwdw_ref = outs[0], outs[1], outs[2]
        res = {n: outs[3 + 4 * k:7 + 4 * k] for k, n in enumerate(names)}
        red = g_ref[0]
        for dev in range(1, N_DEV):
            red = red + g_ref[dev]
        loss_ref[...] = jnp.broadcast_to(
            (0.5 / D_MODEL) * jnp.sum(red[R_LOSS:R_LOSS + 1, :], axis=-1, keepdims=True), loss_ref.shape)
        for dev in range(N_DEV):
            for k, r in enumerate(MOD_ROWS):
                dmod_ref[dev:dev + 1, k * D_MODEL:(k + 1) * D_MODEL] = g_ref[dev, r:r + 1, :]
        dwdw_ref[...] = red[R_DW:R_DW + HALO, 0:CONV_CH]

        def finish(name, pieces):
            w_ref, m_ref, v_ref = prm[name]
            g_out, d_out, m_out, v_out = res[name]
            for rsl, lsl, g in pieces:
                d, m2, v2 = _adam_math(w_ref[rsl, lsl], g, m_ref[rsl, lsl], v_ref[rsl, lsl])
                g_out[rsl, lsl] = g
                d_out[rsl, lsl] = d
                m_out[rsl, lsl] = m2
                v_out[rsl, lsl] = v2

        one = slice(0, 1)
        row = lambda r: red[r:r + 1, :]
        half = lambda r: red[r:r + 1, 0:CONV_CH]
        finish("b_ada", [(one, slice(k * D_MODEL, (k + 1) * D_MODEL), row(r)) for k, r in enumerate(MOD_ROWS)])
        finish("b_in", [(one, slice(k * D_MODEL, (k + 1) * D_MODEL), row(R_BIN + k)) for k in range(3)])
        finish("g_pre_mix", [(one, slice(None), row(R_IN + 2))])
        finish("g_post_mix", [(one, slice(None), row(R_OUT + 1))])
        finish("g_pre_ffn", [(one, slice(None), row(R_FFN + 2))])
        finish("g_post_ffn", [(one, slice(None), row(R_LOSS + 2))])
        finish("b_dw", [(one, slice(None), half(R_512 + 0))])
        finish("gn_gain", [(one, slice(None), half(R_512 + 1))])
        finish("gn_bias", [(one, slice(None), half(R_512 + 2))])
        finish("g_hgrn_out", [(one, slice(None), half(R_512 + 3))])
        s0, s1 = _lower_bound(prm["lb_logits"][0])
        dlb = half(R_512 + 4)
        finish("lb_logits", [(slice(0, 1), slice(None), dlb * s0 * (1.0 - s0)),
                             (slice(1, 2), slice(None), -dlb * s0 * s1)])

    vm = pl.BlockSpec(memory_space=pltpu.VMEM)
    out_shape = [jax.ShapeDtypeStruct((8, 128), F32), jax.ShapeDtypeStruct((N_DEV, 6 * D_MODEL), F32),
                 jax.ShapeDtypeStruct((HALO, CONV_CH), F32)]
    for n in names:
        out_shape += [jax.ShapeDtypeStruct(params[n][0].shape, F32)] * 4
    outs = pl.pallas_call(
        body, name="small_update", out_shape=out_shape,
        in_specs=[vm] * n_in, out_specs=[vm] * len(out_shape),
        compiler_params=_cp(None, 32),
    )(gath, *flat)
    return outs[0], outs[1], outs[2], {n: outs[3 + 4 * k:7 + 4 * k] for k, n in enumerate(names)}


def _wdw_adam(w, g, m, v):
    def body(w_ref, g_ref, m_ref, v_ref, d_out, m_out, v_out):
        d, m2, v2 = _adam_math(w_ref[...], g_ref[...], m_ref[...], v_ref[...])
        d_out[...] = d
        m_out[...] = m2
        v_out[...] = v2

    vm = pl.BlockSpec(memory_space=pltpu.VMEM)
    return pl.pallas_call(
        body, name="wdw_adam", out_shape=[jax.ShapeDtypeStruct(w.shape, F32)] * 3,
        in_specs=[vm] * 4, out_specs=[vm] * 3, compiler_params=_cp(None, 16),
    )(w, g, m, v)


def kernel(x, c, w_ada, b_ada, lb_logits, g_pre_mix, w_in, b_in, w_dw, b_dw, gn_gain, gn_bias, g_hgrn_out, w_out, g_post_mix, g_pre_ffn, w_up, w_down, g_post_ffn, loss_target, m_w_ada, m_b_ada, m_lb_logits, m_g_pre_mix, m_w_in, m_b_in, m_w_dw, m_b_dw, m_gn_gain, m_gn_bias, m_g_hgrn_out, m_w_out, m_g_post_mix, m_g_pre_ffn, m_w_up, m_w_down, m_g_post_ffn, v_w_ada, v_b_ada, v_lb_logits, v_g_pre_mix, v_w_in, v_b_in, v_w_dw, v_b_dw, v_gn_gain, v_gn_bias, v_g_hgrn_out, v_w_out, v_g_post_mix, v_g_pre_ffn, v_w_up, v_w_down, v_g_post_ffn):
    ax, ay, ac = lax.axis_index("x"), lax.axis_index("y"), lax.axis_index("c")
    chip = 2 * ax + ay
    T = x.shape[1]
    xs, tgt = x[0], loss_target[0]
    ada_cols = w_ada.shape[2]

    b_sh = lax.dynamic_slice_in_dim(b_ada, chip * ada_cols, ada_cols, axis=1)
    wdw_pad = jnp.pad(w_dw[0], ((0, HALO - CONV_K), (0, 0)))
    chip1 = jnp.reshape(chip, (1,)).astype(jnp.int32)
    place = jnp.stack([ac, chip]).astype(jnp.int32)
    _, c8, modg, wdwg = _ada_exchange(c, w_ada[0], b_sh, wdw_pad)
    modr = modg.reshape(6, D_MODEL)
    wdw_all = jnp.transpose(wdwg, (1, 0, 2)).reshape(HALO, CONV_CH)
    vec = (b_dw, gn_gain, gn_bias, g_hgrn_out)

    p, h1, (up_buf, down_buf), w_in_g, w_out_g = _mix_in_fwd(
        chip1, xs, modr, g_pre_mix, b_in.reshape(N_CHIPS, 1, IN_COLS // N_CHIPS), [w_up[0], w_down[0]],
        _cast_own(chip1, w_in[0], "cast_w_in"), _cast_own(chip1, w_out[0], "cast_w_out"))
    w_out_f = w_out_g.reshape(D_MODEL, D_MODEL)
    (cat, ys, o, states, y, x1, h2), (w_up_g, w_down_g) = _mixers_fwd(
        p, wdw_all, vec, lb_logits, w_out_f, xs, modr, g_post_mix, g_pre_ffn, [up_buf, down_buf])
    w_down_f = w_down_g.reshape(D_FF, D_MODEL)
    r, dy2, dx2, st_loss = _ffn_fwd(h2, w_up_g, w_down_f, x1, tgt, modr, g_post_ffn)

    def pair_sums(grads, got, tags):
        return [_pair_sum(place, g, o_, "pair_sum_" + t) for (g, _), o_, t in zip(grads, got, tags)]

    da, dx1, st_ffn = _ffn_bwd(dy2, r, x1, dx2, w_up_g, w_down_f, modr, g_pre_ffn)
    g_up = _weight_grad(h2, da, False, True, "grad_w_up")
    g_down = _weight_grad(r, dy2, True, False, "grad_w_down")
    (dcat, st_out, g_out), got_ud = _mix_out_bwd(dx1, y, cat, w_out_f, modr, g_post_mix, [g_up[1], g_down[1]])
    g_out = g_out.reshape(N_CHIPS, D_MODEL // N_CHIPS, D_MODEL)
    got_o = _pair_swap([g_out], "pair_swap_w_out")
    early = pair_sums([(g_out, None), g_up, g_down], list(got_o) + list(got_ud), ["w_out", "w_up", "w_down"])
    (dp, st_bin, st_512, dwdw), got_early, g_in = _mixers_bwd(p, dcat, ys, o, states, h1, wdw_all, vec, lb_logits,
                                                              [pb for _, pb in early])
    late = pair_sums([g_in], _pair_swap([g_in[1]], "pair_swap_w_in"), ["w_in"])
    (grad_x, st_in), got_late = _mix_in_bwd(dp, w_in_g, xs, dx1, modr, g_pre_mix, [late[0][1]])
    fulls = [_chip_sum(place, pf, gb, "chip_sum_" + t)
             for (pf, _), gb, t in zip(late + early, list(got_late) + list(got_early), ["w_in", "w_out", "w_up", "w_down"])]

    pad_lanes = lambda s: jnp.pad(s, ((0, 0), (0, D_MODEL - s.shape[1])))
    stats = jnp.concatenate([st_loss, st_ffn, st_out, st_in, st_bin, pad_lanes(st_512), pad_lanes(dwdw)], axis=0)
    (g_w_in, g_w_out, g_w_up, g_w_down), gath = _final_exchange(fulls, stats)
    small = {"b_ada": (b_ada, m_b_ada, v_b_ada), "lb_logits": (lb_logits, m_lb_logits, v_lb_logits),
             "g_pre_mix": (g_pre_mix, m_g_pre_mix, v_g_pre_mix), "b_in": (b_in, m_b_in, v_b_in),
             "b_dw": (b_dw, m_b_dw, v_b_dw), "gn_gain": (gn_gain, m_gn_gain, v_gn_gain),
             "gn_bias": (gn_bias, m_gn_bias, v_gn_bias), "g_hgrn_out": (g_hgrn_out, m_g_hgrn_out, v_g_hgrn_out),
             "g_post_mix": (g_post_mix, m_g_post_mix, v_g_post_mix), "g_pre_ffn": (g_pre_ffn, m_g_pre_ffn, v_g_pre_ffn),
             "g_post_ffn": (g_post_ffn, m_g_post_ffn, v_g_post_ffn)}
    loss_t, dmod_all, dwdw_sum, sres = _small_update(gath, small)
    loss = loss_t[0, 0]

    res = dict(sres)
    dmod_sh = lax.dynamic_slice_in_dim(dmod_all, chip * ada_cols, ada_cols, axis=1)
    res["w_ada"] = [t[None] for t in _ada_grad_adam(jnp.transpose(c8), dmod_sh, w_ada[0], m_w_ada[0], v_w_ada[0])]
    g_wdw = lax.dynamic_slice_in_dim(dwdw_sum, chip * HEAD_D, HEAD_D, axis=1)[:CONV_K][None]
    res["w_dw"] = [g_wdw] + list(_wdw_adam(w_dw, g_wdw, m_w_dw, v_w_dw))
    for name, g, w, m, v in (("w_in", g_w_in, w_in, m_w_in, v_w_in), ("w_out", g_w_out, w_out, m_w_out, v_w_out),
                             ("w_up", g_w_up, w_up, m_w_up, v_w_up), ("w_down", g_w_down, w_down, m_w_down, v_w_down)):
        d, m2, v2 = _adam_big(w[0], g, m[0], v[0], "adam_" + name)
        res[name] = [g[None], d[None], m2[None], v2[None]]

    order = ["w_ada", "b_ada", "lb_logits", "g_pre_mix", "w_in", "b_in", "w_dw", "b_dw", "gn_gain", "gn_bias",
             "g_hgrn_out", "w_out", "g_post_mix", "g_pre_ffn", "w_up", "w_down", "g_post_ffn"]
    out = [loss, grad_x[None]]
    for k in range(4):
        out += [res[n][k] for n in order]
    return tuple(out)
```

```python
import jax
import jax.numpy as jnp
from jax import lax
from jax.experimental import pallas as pl
from jax.experimental.pallas import tpu as pltpu

F32, BF16 = jnp.float32, jnp.bfloat16
D_MODEL = 1024
CONV_CH = 512
HGRN_W = 512
N_HEADS = 4
HEAD_D = 128
CONV_K = 31
GN_GROUP = 64
GN_SHIFT = 6
IN_COLS = 3072
D_FF = 4096
CHUNK = 64
CHUNK_SHIFT = 6
N_CHIPS = 4
N_DEV = 8
RMS_EPS = 1e-6
GN_EPS = 1e-5
ADAM_LR, ADAM_B1, ADAM_B2, ADAM_EPS, ADAM_WD, ADAM_STEP = 0.001, 0.9, 0.999, 1e-08, 0.01, 10
TOK_TILE = 512
MIXIN_TILE = 1024
MIXB_TILE = 256
FFN_TILE = 1024
FFN_BLOCK = 512
GRAD_TILE = 4096
HALO = 32
SUB = 8
LANE = 128
CONV_ROWS = 128
CHUNK_UNROLL = 8
MIB = 1 << 20
MESH = pl.DeviceIdType.MESH
OTHER_CHIPS = ((0, 1), (1, 0), (1, 1))


def _cp(sem=None, vmem_mib=48):
    return pltpu.CompilerParams(dimension_semantics=sem, vmem_limit_bytes=vmem_mib * MIB)


def _dot(a, b):
    return jnp.dot(a, b, preferred_element_type=F32)


def _dot_nt(a, b):
    return lax.dot_general(a, b, (((1,), (1,)), ((), ())), preferred_element_type=F32)


def _dot_tn(a, b):
    return lax.dot_general(a, b, (((0,), (0,)), ((), ())), preferred_element_type=F32)


def _sig(v):
    return 0.5 * jnp.tanh(0.5 * v) + 0.5


def _colsum(v):
    return jnp.sum(v, axis=0, keepdims=True)


def _flip(v, b):
    return 1 - v if b else v


def _rcopy(src, dst, ssem, rsem, dev):
    return pltpu.make_async_remote_copy(src_ref=src, dst_ref=dst, send_sem=ssem, recv_sem=rsem,
                                        device_id=dev, device_id_type=MESH)


def _place():
    return lax.axis_index("x"), lax.axis_index("y"), lax.axis_index("c")


def _full(shape):
    return pl.BlockSpec(shape, lambda *_: (0,) * len(shape))


def _big(shape, dtype):
    return pltpu.HBM(shape, dtype)


def _hbm(*arrays):
    out = [pltpu.with_memory_space_constraint(a, pltpu.HBM) for a in arrays]
    return out[0] if len(out) == 1 else out


def _split2(v):
    hi = v.astype(BF16)
    lo = (v - hi.astype(F32)).astype(BF16)
    return hi, lo


def _split3(v):
    h1 = v.astype(BF16)
    r1 = v - h1.astype(F32)
    h2 = r1.astype(BF16)
    h3 = (r1 - h2.astype(F32)).astype(BF16)
    return h1, h2, h3


def _mm3(mat, v):
    h1, h2, h3 = _split3(v)
    return _dot(mat, h1) + _dot(mat, h2) + _dot(mat, h3)


def _gn_matrix():
    r = lax.broadcasted_iota(jnp.int32, (CONV_CH, CONV_CH), 0) >> GN_SHIFT
    c = lax.broadcasted_iota(jnp.int32, (CONV_CH, CONV_CH), 1) >> GN_SHIFT
    return jnp.where(r == c, 1.0 / GN_GROUP, 0.0).astype(BF16)


def _gmean(v, gmat):
    hi, lo = _split2(v)
    return _dot(hi, gmat) + _dot(lo, gmat)


def _chunk_masks(tm):
    r = lax.broadcasted_iota(jnp.int32, (tm, tm), 0)
    c = lax.broadcasted_iota(jnp.int32, (tm, tm), 1)
    same = (r >> CHUNK_SHIFT) == (c >> CHUNK_SHIFT)
    one = lambda m: jnp.where(m, 1.0, 0.0).astype(BF16)
    return one(same & (c <= r)), one(same & (c >= r)), one(same)


def _tri():
    return lax.broadcasted_iota(jnp.int32, (CHUNK, CHUNK), 0) >= lax.broadcasted_iota(jnp.int32, (CHUNK, CHUNK), 1)


def _lower_bound(lbl_ref):
    l0, l1 = lbl_ref[0:1, :], lbl_ref[1:2, :]
    mx = jnp.maximum(l0, l1)
    e0, e1 = jnp.exp(l0 - mx), jnp.exp(l1 - mx)
    return e0 / (e0 + e1), e1 / (e0 + e1)


CONV_FWD_TAPS = tuple((j, HALO - (CONV_K - 1) + j) for j in range(CONV_K))
CONV_BWD_TAPS = tuple((j, (CONV_K - 1) - j) for j in range(CONV_K))


def _tap_conv(src_ref, w_ref, row0, taps, lanes):
    acc = None
    for b in range(SUB):
        pb = None
        for j, off in taps:
            if off % SUB == b:
                lo = row0 + off - b
                term = w_ref[j:j + 1, lanes] * src_ref[lo:lo + CONV_ROWS + SUB, lanes]
                pb = term if pb is None else pb + term
        if pb is not None:
            sh = pb[b:b + CONV_ROWS, :]
            acc = sh if acc is None else acc + sh
    return acc


def _hgrn_prep(pq, pf, lb, lower):
    sq = _sig(pq)
    qf = pq * sq
    sf = _sig(pf)
    f = lb + (1.0 - lb) * sf
    logf = jnp.log(f)
    k = 1.0 - f
    G = _mm3(lower, logf)
    rows, cols = G.shape
    g3 = G.reshape(rows // CHUNK, CHUNK, cols)
    Gl = jnp.broadcast_to(g3[:, CHUNK - 1:CHUNK, :], g3.shape).reshape(rows, cols)
    eG, enG, eGlG = jnp.exp(G), jnp.exp(-G), jnp.exp(Gl - G)
    return dict(sq=sq, sf=sf, f=f, Gl=Gl, eG=eG, enG=enG, eGlG=eGlG, qt=qf * eG, kt=k * enG, kh=k * eGlG)


def _ada_exchange(c_row, w_ada, b_sh, wdw_pad):
    ncol = w_ada.shape[1]

    def body(c_ref, w_ref, b_ref, wdw_ref, call_ref, c8_ref, modg_ref, wdwg_ref, rows_s, sa, ra, sw, rw, sm, rm):
        x, y, c = _place()
        me = 4 * x + 2 * y + c
        chip = 2 * x + y
        cv = c_ref[...]
        call_ref[me] = cv * _sig(cv)
        wdwg_ref[chip] = wdw_ref[...]
        sends = []
        for m in range(1, N_DEV):
            peer = (_flip(x, m >> 2), _flip(y, (m >> 1) & 1), _flip(c, m & 1))
            cp = _rcopy(call_ref.at[me], call_ref.at[me], sa.at[m - 1], ra.at[m - 1], peer)
            cp.start()
            sends.append(cp)
        for k, (fx, fy) in enumerate(OTHER_CHIPS):
            peer = (_flip(x, fx), _flip(y, fy), c)
            cp = _rcopy(wdwg_ref.at[chip], wdwg_ref.at[chip], sw.at[k], rw.at[k], peer)
            cp.start()
            sends.append(cp)
        for m in range(1, N_DEV):
            peer = (_flip(x, m >> 2), _flip(y, (m >> 1) & 1), _flip(c, m & 1))
            pid = 4 * peer[0] + 2 * peer[1] + peer[2]
            _rcopy(call_ref.at[pid], call_ref.at[pid], sa.at[m - 1], ra.at[m - 1], peer).wait_recv()
        for b in range(N_DEV):
            c8_ref[b:b + 1, :] = call_ref[b]
        mod_all = _dot(c8_ref[...].astype(BF16), w_ref[...].astype(BF16)) + b_ref[...]
        for b in range(N_DEV):
            rows_s[b] = mod_all[b:b + 1, :]
        modg_ref[chip] = rows_s[me]
        for k, (fx, fy) in enumerate(OTHER_CHIPS):
            peer = (_flip(x, fx), _flip(y, fy), c)
            pid = 4 * peer[0] + 2 * peer[1] + peer[2]
            cp = _rcopy(rows_s.at[pid], modg_ref.at[chip], sm.at[k], rm.at[k], peer)
            cp.start()
            sends.append(cp)
        for k, (fx, fy) in enumerate(OTHER_CHIPS):
            peer = (_flip(x, fx), _flip(y, fy), c)
            pchip = 2 * peer[0] + peer[1]
            _rcopy(rows_s.at[0], modg_ref.at[pchip], sm.at[k], rm.at[k], peer).wait_recv()
            _rcopy(wdwg_ref.at[pchip], wdwg_ref.at[pchip], sw.at[k], rw.at[k], peer).wait_recv()
        for cp in sends:
            cp.wait_send()

    vm = pl.BlockSpec(memory_space=pltpu.VMEM)
    return pl.pallas_call(
        body, name="ada_exchange",
        out_shape=[jax.ShapeDtypeStruct((N_DEV, 1, D_MODEL), F32), jax.ShapeDtypeStruct((N_DEV, D_MODEL), F32),
                   jax.ShapeDtypeStruct((N_CHIPS, 1, ncol), F32), jax.ShapeDtypeStruct((N_CHIPS, HALO, HEAD_D), F32)],
        in_specs=[vm] * 4, out_specs=[vm] * 4,
        scratch_shapes=[pltpu.VMEM((N_DEV, 1, ncol), F32),
                        pltpu.SemaphoreType.DMA((N_DEV - 1,)), pltpu.SemaphoreType.DMA((N_DEV - 1,)),
                        pltpu.SemaphoreType.DMA((3,)), pltpu.SemaphoreType.DMA((3,)),
                        pltpu.SemaphoreType.DMA((3,)), pltpu.SemaphoreType.DMA((3,))],
        compiler_params=_cp(None, 32),
    )(c_row, w_ada, b_sh, wdw_pad)


def _cast_own(chip1, shard, name):
    rows, cols = shard.shape
    tr = _row_tile(rows)

    def body(ch_ref, s_ref, o_ref):
        o_ref[0] = s_ref[...].astype(BF16)

    gs = pltpu.PrefetchScalarGridSpec(
        num_scalar_prefetch=1, grid=(rows // tr,),
        in_specs=[pl.BlockSpec((tr, cols), lambda i, ch: (i, 0))],
        out_specs=pl.BlockSpec((1, tr, cols), lambda i, ch: (ch[0], i, 0)))
    return pl.pallas_call(
        body, name=name, grid_spec=gs, out_shape=_big((N_CHIPS, rows, cols), BF16),
        compiler_params=_cp(("arbitrary",), 32),
    )(chip1, _hbm(shard))


def _slab(buf, ch, core):
    hs = buf.shape[1] // 2
    return buf.at[ch, pl.ds(core * hs, hs), :]


def _gather_start(bufs, ssem, rsem, relations=(0, 1, 2)):
    x, y, c = _place()
    chip = 2 * x + y
    for k in relations:
        fx, fy = OTHER_CHIPS[k]
        peer = (_flip(x, fx), _flip(y, fy), c)
        for t, buf in enumerate(bufs):
            _rcopy(_slab(buf, chip, c), _slab(buf, chip, c), ssem.at[t * 3 + k], rsem.at[t * 3 + k], peer).start()


def _gather_pass_on(bufs, ssem, rsem):
    nt = len(bufs)
    x, y, c = _place()
    sibling = (x, y, 1 - c)
    for k, (fx, fy) in enumerate(OTHER_CHIPS):
        peer = (_flip(x, fx), _flip(y, fy), c)
        pchip = 2 * peer[0] + peer[1]
        for t, buf in enumerate(bufs):
            _rcopy(_slab(buf, pchip, c), _slab(buf, pchip, c), ssem.at[t * 3 + k], rsem.at[t * 3 + k], peer).wait_recv()
            _rcopy(_slab(buf, pchip, c), _slab(buf, pchip, c), ssem.at[3 * nt + t * 3 + k],
                   rsem.at[3 * nt + t * 3 + k], sibling).start()


def _gather_drain(bufs, ssem, rsem):
    nt = len(bufs)
    x, y, c = _place()
    chip = 2 * x + y
    sibling = (x, y, 1 - c)
    for k, (fx, fy) in enumerate(OTHER_CHIPS):
        peer = (_flip(x, fx), _flip(y, fy), c)
        pchip = 2 * peer[0] + peer[1]
        for t, buf in enumerate(bufs):
            _rcopy(_slab(buf, pchip, 1 - c), _slab(buf, pchip, 1 - c), ssem.at[3 * nt + t * 3 + k],
                   rsem.at[3 * nt + t * 3 + k], sibling).wait_recv()
            _rcopy(_slab(buf, chip, c), _slab(buf, chip, c), ssem.at[t * 3 + k], rsem.at[t * 3 + k], peer).wait_send()
            _rcopy(_slab(buf, pchip, c), _slab(buf, pchip, c), ssem.at[3 * nt + t * 3 + k],
                   rsem.at[3 * nt + t * 3 + k], sibling).wait_send()


def _gather_finish(bufs, ssem, rsem):
    _gather_pass_on(bufs, ssem, rsem)
    _gather_drain(bufs, ssem, rsem)


def _gather_arrive(bufs, k, ssem, rsem):
    nt = len(bufs)
    x, y, c = _place()
    fx, fy = OTHER_CHIPS[k]
    peer = (_flip(x, fx), _flip(y, fy), c)
    pchip = 2 * peer[0] + peer[1]
    for t, buf in enumerate(bufs):
        _rcopy(_slab(buf, pchip, c), _slab(buf, pchip, c), ssem.at[t * 3 + k], rsem.at[t * 3 + k], peer).wait_recv()
        _rcopy(_slab(buf, pchip, c), _slab(buf, pchip, c), ssem.at[3 * nt + t * 3 + k],
               rsem.at[3 * nt + t * 3 + k], (x, y, 1 - c)).start()
    for t, buf in enumerate(bufs):
        _rcopy(_slab(buf, pchip, 1 - c), _slab(buf, pchip, 1 - c), ssem.at[3 * nt + t * 3 + k],
               rsem.at[3 * nt + t * 3 + k], (x, y, 1 - c)).wait_recv()


def _gather_sends_done(bufs, ssem, rsem):
    nt = len(bufs)
    x, y, c = _place()
    chip = 2 * x + y
    for k, (fx, fy) in enumerate(OTHER_CHIPS):
        peer = (_flip(x, fx), _flip(y, fy), c)
        pchip = 2 * peer[0] + peer[1]
        for t, buf in enumerate(bufs):
            _rcopy(_slab(buf, chip, c), _slab(buf, chip, c), ssem.at[t * 3 + k], rsem.at[t * 3 + k], peer).wait_send()
            _rcopy(_slab(buf, pchip, c), _slab(buf, pchip, c), ssem.at[3 * nt + t * 3 + k],
                   rsem.at[3 * nt + t * 3 + k], (x, y, 1 - c)).wait_send()


def _ring_parts(buf):
    x, y, c = _place()
    ynb, xnb = (x, 1 - y, c), (1 - x, y, c)
    ychip, xchip, dchip = 2 * x + (1 - y), 2 * (1 - x) + y, 2 * (1 - x) + (1 - y)
    hs = buf.shape[1] // 2

    def piece(ch, q):
        return buf.at[ch, pl.ds(c * hs + q * (hs // 2), hs // 2), :]

    return ynb, xnb, ychip, xchip, dchip, piece


def _ring_start(bufs, ssem, rsem):
    x, y, c = _place()
    chip = 2 * x + y
    for t, buf in enumerate(bufs):
        ynb, xnb, _, _, _, _ = _ring_parts(buf)
        _rcopy(_slab(buf, chip, c), _slab(buf, chip, c), ssem.at[2 * t], rsem.at[2 * t], ynb).start()
        _rcopy(_slab(buf, chip, c), _slab(buf, chip, c), ssem.at[2 * t + 1], rsem.at[2 * t + 1], xnb).start()


def _ring_forward(bufs, ssem, rsem):
    nt = len(bufs)
    _, _, c = _place()
    for t, buf in enumerate(bufs):
        ynb, xnb, ychip, xchip, _, piece = _ring_parts(buf)
        _rcopy(_slab(buf, ychip, c), _slab(buf, ychip, c), ssem.at[2 * t], rsem.at[2 * t], ynb).wait_recv()
        _rcopy(piece(ychip, 0), piece(ychip, 0), ssem.at[2 * nt + 2 * t], rsem.at[2 * nt + 2 * t], xnb).start()
        _rcopy(_slab(buf, xchip, c), _slab(buf, xchip, c), ssem.at[2 * t + 1], rsem.at[2 * t + 1], xnb).wait_recv()
        _rcopy(piece(xchip, 1), piece(xchip, 1), ssem.at[2 * nt + 2 * t + 1], rsem.at[2 * nt + 2 * t + 1], ynb).start()


def _ring_finish(bufs, ssem, rsem):
    nt = len(bufs)
    x, y, c = _place()
    chip = 2 * x + y
    sibling = (x, y, 1 - c)
    for t, buf in enumerate(bufs):
        ynb, xnb, ychip, xchip, dchip, piece = _ring_parts(buf)
        _rcopy(piece(dchip, 0), piece(dchip, 0), ssem.at[2 * nt + 2 * t], rsem.at[2 * nt + 2 * t], xnb).wait_recv()
        _rcopy(piece(dchip, 1), piece(dchip, 1), ssem.at[2 * nt + 2 * t + 1], rsem.at[2 * nt + 2 * t + 1],
               ynb).wait_recv()
        for k, ch in enumerate((ychip, xchip, dchip)):
            _rcopy(_slab(buf, ch, c), _slab(buf, ch, c), ssem.at[4 * nt + 3 * t + k], rsem.at[4 * nt + 3 * t + k],
                   sibling).start()
    for t, buf in enumerate(bufs):
        ynb, xnb, ychip, xchip, dchip, piece = _ring_parts(buf)
        for k, ch in enumerate((ychip, xchip, dchip)):
            _rcopy(_slab(buf, ch, 1 - c), _slab(buf, ch, 1 - c), ssem.at[4 * nt + 3 * t + k],
                   rsem.at[4 * nt + 3 * t + k], sibling).wait_recv()
            _rcopy(_slab(buf, ch, c), _slab(buf, ch, c), ssem.at[4 * nt + 3 * t + k], rsem.at[4 * nt + 3 * t + k],
                   sibling).wait_send()
        _rcopy(_slab(buf, chip, c), _slab(buf, chip, c), ssem.at[2 * t], rsem.at[2 * t], ynb).wait_send()
        _rcopy(_slab(buf, chip, c), _slab(buf, chip, c), ssem.at[2 * t + 1], rsem.at[2 * t + 1], xnb).wait_send()
        _rcopy(piece(ychip, 0), piece(ychip, 0), ssem.at[2 * nt + 2 * t], rsem.at[2 * nt + 2 * t], xnb).wait_send()
        _rcopy(piece(xchip, 1), piece(xchip, 1), ssem.at[2 * nt + 2 * t + 1], rsem.at[2 * nt + 2 * t + 1],
               ynb).wait_send()


def _ring_sems(nt):
    return [pltpu.SemaphoreType.DMA((7 * nt,)), pltpu.SemaphoreType.DMA((7 * nt,))]


def _gather_sems(nt):
    return [pltpu.SemaphoreType.DMA((6 * nt,)), pltpu.SemaphoreType.DMA((6 * nt,))]


def _pair_copies(ins, outs, ssem, rsem):
    x, y, c = _place()
    copies = []
    for t in range(len(ins)):
        hs = ins[t].shape[1] // 2
        copies.append(_rcopy(ins[t].at[:, pl.ds((1 - c) * hs, hs), :], outs[t], ssem.at[t], rsem.at[t], (x, y, 1 - c)))
    return copies


def _pair_shapes(grads):
    return [_big((g.shape[0], g.shape[1] // 2, g.shape[2]), g.dtype) for g in grads]


def _pair_sems(nt):
    return [pltpu.SemaphoreType.DMA((nt,)), pltpu.SemaphoreType.DMA((nt,))]


def _pair_swap(grads, name):
    nt = len(grads)
    hbm = pl.BlockSpec(memory_space=pl.ANY)

    def body(*refs):
        copies = _pair_copies(refs[:nt], refs[nt:2 * nt], refs[2 * nt], refs[2 * nt + 1])
        for cp in copies:
            cp.start()
        for cp in copies:
            cp.wait_recv()
        for cp in copies:
            cp.wait_send()

    return pl.pallas_call(
        body, name=name, out_shape=_pair_shapes(grads), in_specs=[hbm] * nt, out_specs=[hbm] * nt,
        scratch_shapes=_pair_sems(nt),
    )(*[_hbm(g) for g in grads])


def _xchg_copies(ins, outs, ssem, rsem):
    x, y, c = _place()
    copies = []
    for k, (fx, fy) in enumerate(OTHER_CHIPS):
        peer = (_flip(x, fx), _flip(y, fy), c)
        for t in range(len(ins)):
            copies.append(_rcopy(ins[t].at[k], outs[t].at[k], ssem.at[t * 3 + k], rsem.at[t * 3 + k], peer))
    return copies


def _xchg_sems(nt):
    return [pltpu.SemaphoreType.DMA((3 * nt,)), pltpu.SemaphoreType.DMA((3 * nt,))]


def _final_exchange(fulls, stats):
    nt = len(fulls)
    rows, cols = stats.shape
    hbm = pl.BlockSpec(memory_space=pl.ANY)
    vm = pl.BlockSpec(memory_space=pltpu.VMEM)

    def body(*refs):
        ins, s_ref = refs[:nt], refs[nt]
        outs, g_ref = refs[nt + 1:2 * nt + 1], refs[2 * nt + 1]
        hssem, hrsem, ssem, rsem = refs[2 * nt + 2:]
        x, y, c = _place()
        me, sibling = (x, y, c), (x, y, 1 - c)
        halves = []
        for t in range(nt):
            hs = ins[t].shape[0] // 2
            mine = pl.ds(c * hs, hs)
            cp = _rcopy(ins[t].at[mine, :], outs[t].at[mine, :], hssem.at[t], hrsem.at[t], sibling)
            cp.start()
            halves.append(cp)

        chips = [(_flip(x, fx), _flip(y, fy)) for fx, fy in OTHER_CHIPS]

        def blk(px, py, pc):
            return g_ref.at[4 * px + 2 * py + pc]

        def copy(k, block, to, src=None):
            return _rcopy(blk(*block) if src is None else src, blk(*block), ssem.at[k], rsem.at[k], to)

        g_ref[4 * x + 2 * y + c] = s_ref[...]
        first = [copy(0, me, sibling, src=s_ref)]
        first += [copy(1 + j, me, (*chip, c), src=s_ref) for j, chip in enumerate(chips)]
        for cp in first:
            cp.start()
        passed = [copy(4 + j, (*chip, c), sibling) for j, chip in enumerate(chips)]
        for j, chip in enumerate(chips):
            copy(1 + j, (*chip, c), me).wait_recv()
            passed[j].start()
        copy(0, sibling, me).wait_recv()
        for j, chip in enumerate(chips):
            copy(4 + j, (*chip, 1 - c), me).wait_recv()
        for t in range(nt):
            hs = ins[t].shape[0] // 2
            other = pl.ds((1 - c) * hs, hs)
            _rcopy(ins[t].at[other, :], outs[t].at[other, :], hssem.at[t], hrsem.at[t], sibling).wait_recv()
        for cp in first + passed + halves:
            cp.wait_send()

    outs = pl.pallas_call(
        body, name="final_exchange",
        out_shape=[_big(f.shape, F32) for f in fulls] + [jax.ShapeDtypeStruct((N_DEV, rows, cols), F32)],
        in_specs=[hbm] * nt + [vm], out_specs=[hbm] * nt + [vm],
        input_output_aliases={t: t for t in range(nt)},
        scratch_shapes=[pltpu.SemaphoreType.DMA((nt,)), pltpu.SemaphoreType.DMA((nt,)),
                        pltpu.SemaphoreType.DMA((7,)), pltpu.SemaphoreType.DMA((7,))],
        compiler_params=_cp(None, 32),
    )(*[_hbm(f) for f in fulls], stats)
    return outs[:nt], outs[nt]


def _row_tile(rows):
    return min(rows, 512)


def _pair_sum(place, grad, got, name):
    nb, hs, cols = got.shape
    tr = _row_tile(hs)
    nr = hs // tr

    def body(pl_ref, g_ref, o_ref, pf_ref, pb_ref):
        j = pl.program_id(1)
        s = g_ref[0] + o_ref[0].astype(F32)

        @pl.when(j == 0)
        def _():
            pf_ref[...] = s

        @pl.when(j > 0)
        def _():
            pb_ref[0] = s.astype(BF16)

    gs = pltpu.PrefetchScalarGridSpec(
        num_scalar_prefetch=1, grid=(nr, nb),
        in_specs=[pl.BlockSpec((1, tr, cols), lambda i, j, p: (p[1] ^ j, p[0] * nr + i, 0)),
                  pl.BlockSpec((1, tr, cols), lambda i, j, p: (p[1] ^ j, i, 0))],
        out_specs=[pl.BlockSpec((tr, cols), lambda i, j, p: (i, 0)),
                   pl.BlockSpec((1, tr, cols), lambda i, j, p: (jnp.maximum(j - 1, 0), i, 0))])
    return pl.pallas_call(
        body, name=name, grid_spec=gs,
        out_shape=[_big((hs, cols), F32), _big((nb - 1, hs, cols), BF16)],
        compiler_params=_cp(("arbitrary", "arbitrary"), 32),
    )(place, *_hbm(grad, got))


def _chip_sum(place, pair_f, got_b, name):
    nb, hs, cols = got_b.shape
    tr = _row_tile(hs)
    nr = hs // tr

    def body(pl_ref, pf_ref, gb_ref, o_ref):
        acc = pf_ref[...]
        for k in range(nb):
            acc = acc + gb_ref[k].astype(F32)
        o_ref[...] = acc

    gs = pltpu.PrefetchScalarGridSpec(
        num_scalar_prefetch=1, grid=(nr,),
        in_specs=[pl.BlockSpec((tr, cols), lambda i, p: (i, 0)),
                  pl.BlockSpec((nb, tr, cols), lambda i, p: (0, i, 0))],
        out_specs=pl.BlockSpec((tr, cols), lambda i, p: (p[0] * nr + i, 0)))
    return pl.pallas_call(
        body, name=name, grid_spec=gs,
        out_shape=_big((2 * hs, cols), F32),
        compiler_params=_cp(("arbitrary",), 32),
    )(place, *_hbm(pair_f, got_b))


def _adam_math(w, g, m, v):
    m2 = ADAM_B1 * m + (1.0 - ADAM_B1) * g
    v2 = ADAM_B2 * v + (1.0 - ADAM_B2) * (g * g)
    m_hat = m2 / (1.0 - ADAM_B1 ** ADAM_STEP)
    v_hat = v2 / (1.0 - ADAM_B2 ** ADAM_STEP)
    delta = -ADAM_LR * (m_hat / (jnp.sqrt(v_hat) + ADAM_EPS) + ADAM_WD * w)
    return delta, m2, v2


def _adam_big(w, g, m, v, name):
    rows, cols = w.shape
    tr = _row_tile(rows)

    def body(w_ref, g_ref, m_ref, v_ref, d_out, m_out, v_out):
        d, m2, v2 = _adam_math(w_ref[...], g_ref[...], m_ref[...], v_ref[...])
        d_out[...] = d
        m_out[...] = m2
        v_out[...] = v2

    spec = pl.BlockSpec((tr, cols), lambda i: (i, 0))
    return pl.pallas_call(
        body, name=name, grid=(rows // tr,), in_specs=[spec] * 4, out_specs=[spec] * 3,
        out_shape=[_big(w.shape, F32)] * 3,
        compiler_params=_cp(("arbitrary",), 48),
    )(*_hbm(w, g, m, v))


def _ada_grad_adam(c8t, dmod_sh, w, m, v):
    rows, cols = w.shape
    tr = _row_tile(rows) // 2

    def body(ct_ref, dm_ref, w_ref, m_ref, v_ref, g_out, d_out, m_out, v_out):
        g = None
        for b in range(N_DEV):
            term = ct_ref[:, b:b + 1] * dm_ref[b:b + 1, :]
            g = term if g is None else g + term
        d, m2, v2 = _adam_math(w_ref[...], g, m_ref[...], v_ref[...])
        g_out[...] = g
        d_out[...] = d
        m_out[...] = m2
        v_out[...] = v2

    spec = pl.BlockSpec((tr, cols), lambda i: (i, 0))
    return pl.pallas_call(
        body, name="ada_grad_adam", grid=(rows // tr,),
        in_specs=[pl.BlockSpec((tr, N_DEV), lambda i: (i, 0)), _full((N_DEV, cols)), spec, spec, spec],
        out_specs=[spec] * 4, out_shape=[_big(w.shape, F32)] * 4,
        compiler_params=_cp(("arbitrary",), 32),
    )(c8t, dmod_sh, *_hbm(w, m, v))


def _tok_tile(t):
    return min(TOK_TILE, t)


def _mix_in_fwd(chip1, x, modr, g_pre, b_in4, later, w_in_buf, w_out_buf):
    T = x.shape[0]
    tm = min(MIXIN_TILE, T)
    nt = T // tm
    nb = IN_COLS // N_CHIPS
    nl = len(later)

    def body(*refs):
        ch_ref, x_ref, mod_ref, g_ref, b_ref = refs[:5]
        l_ins = refs[5:5 + nl]
        p_ref, h_ref = refs[7 + nl:9 + nl]
        l_outs = refs[9 + nl:9 + 2 * nl]
        win_ref, wout_ref = refs[9 + 2 * nl:11 + 2 * nl]
        h_all, wblk, lsem, is_sem, ir_sem, os_sem, or_sem = refs[11 + 2 * nl:]
        k, i = pl.program_id(0), pl.program_id(1)
        chip = ch_ref[0]

        @pl.when(k == 0)
        def _():
            for src, dst in zip(l_ins, l_outs):
                dst[0] = src[...].astype(BF16)

        def load_block(blk):
            cp = pltpu.make_async_copy(win_ref.at[blk], wblk, lsem)
            cp.start()
            cp.wait()

        @pl.when((k == 0) & (i == 0))
        def _():
            _gather_start([win_ref], is_sem, ir_sem, relations=(0, 1))
            load_block(chip)

        for r in range(N_CHIPS - 1):
            @pl.when((k == r + 1) & (i == 0))
            def _(r=r):
                _gather_arrive([win_ref], r, is_sem, ir_sem)
                if r == 0:
                    _gather_start([win_ref], is_sem, ir_sem, relations=(2,))
                if r == 1:
                    _gather_start([wout_ref], os_sem, or_sem)
                load_block(chip ^ (r + 1))

        rows = pl.ds(pl.multiple_of(i * tm, tm), tm)

        @pl.when(k == 0)
        def _():
            xv = x_ref[...]
            rstd = lax.rsqrt(jnp.mean(xv * xv, axis=-1, keepdims=True) + RMS_EPS)
            h = (xv * rstd) * g_ref[...] * (1.0 + mod_ref[1:2, :]) + mod_ref[0:1, :]
            hb = h.astype(BF16)
            h_ref[...] = hb
            h_all[rows, :] = hb

        p_ref[...] = _dot(h_all[rows, :], wblk[...]) + b_ref[chip ^ k]

        @pl.when((k == N_CHIPS - 1) & (i == nt - 1))
        def _():
            _gather_sends_done([win_ref], is_sem, ir_sem)
            _gather_finish([wout_ref], os_sem, or_sem)

    hbm = pl.BlockSpec(memory_space=pl.ANY)
    first_pass = lambda k, i, ch: (jnp.where(k == 0, i, nt - 1), 0)
    own_slot = lambda k, i, ch: (ch[0], jnp.where(k == 0, i, nt - 1), 0)
    gs = pltpu.PrefetchScalarGridSpec(
        num_scalar_prefetch=1, grid=(N_CHIPS, nt),
        in_specs=[pl.BlockSpec((tm, D_MODEL), first_pass), pl.BlockSpec((6, D_MODEL), lambda k, i, ch: (0, 0)),
                  pl.BlockSpec((1, D_MODEL), lambda k, i, ch: (0, 0)),
                  pl.BlockSpec((N_CHIPS, 1, nb), lambda k, i, ch: (0, 0, 0))]
        + [pl.BlockSpec((w.shape[0] // nt, w.shape[1]), first_pass) for w in later] + [hbm, hbm],
        out_specs=[pl.BlockSpec((tm, nb), lambda k, i, ch: (i, ch[0] ^ k)), pl.BlockSpec((tm, D_MODEL), first_pass)]
        + [pl.BlockSpec((1, w.shape[0] // nt, w.shape[1]), own_slot) for w in later] + [hbm, hbm],
        scratch_shapes=[pltpu.VMEM((T, D_MODEL), BF16), pltpu.VMEM((D_MODEL, nb), BF16), pltpu.SemaphoreType.DMA]
        + _gather_sems(1) + _gather_sems(1))
    outs = pl.pallas_call(
        body, name="mix_in_fwd", grid_spec=gs,
        out_shape=[_big((T, IN_COLS), F32), _big((T, D_MODEL), BF16)]
        + [_big((N_CHIPS,) + w.shape, BF16) for w in later]
        + [_big(w_in_buf.shape, BF16), _big(w_out_buf.shape, BF16)],
        input_output_aliases={5 + nl: 2 + nl, 6 + nl: 3 + nl},
        compiler_params=_cp(("arbitrary", "arbitrary"), 48),
    )(chip1, _hbm(x), modr, g_pre, b_in4, *[_hbm(w) for w in later], _hbm(w_in_buf), _hbm(w_out_buf))
    return outs[0], outs[1], outs[2:2 + nl], outs[2 + nl], outs[3 + nl]


def _mixers_fwd(p, wdw, vecs, lbl, w_out, x, modr, g_post, g_ffn, gbufs):
    T = p.shape[0]
    tm = _tok_tile(T)
    nt = T // tm
    nch = tm // CHUNK
    ng = len(gbufs)
    n_in, n_out = 12, 7

    def body(*refs):
        (p_ref, wdw_ref, bdw_ref, gain_ref, bias_ref, gout_ref, lbl_ref, wout_ref, x_ref, mod_ref, gp_ref,
         gf_ref) = refs[:n_in]
        cat_ref, ys_ref, o_ref, st_ref, y_ref, x1_ref, h2_ref = refs[n_in + ng:n_in + ng + n_out]
        gout_bufs = refs[n_in + ng + n_out:n_in + 2 * ng + n_out]
        (ubuf, state, qt_s, kt_s, kh_s, v_s, egl_s, lower_s, gmat_s, gssem,
         grsem) = refs[n_in + 2 * ng + n_out:]
        i = pl.program_id(0)

        @pl.when(i == 0)
        def _():
            _ring_start(gout_bufs, gssem, grsem)
            lower_s[...], _, _ = _chunk_masks(tm)
            gmat_s[...] = _gn_matrix()
            state[...] = jnp.zeros(state.shape, F32)
            ubuf[0:HALO, :] = jnp.zeros((HALO, CONV_CH), F32)
            ubuf[HALO + tm:HALO + tm + SUB, :] = jnp.zeros((SUB, CONV_CH), F32)

        @pl.when(i > 0)
        def _():
            ubuf[0:HALO, :] = ubuf[tm:tm + HALO, :]

        ubuf[HALO:HALO + tm, :] = p_ref[:, 0:CONV_CH] * _sig(p_ref[:, CONV_CH:2 * CONV_CH])
        for r in range(tm // CONV_ROWS):
            rows = slice(r * CONV_ROWS, (r + 1) * CONV_ROWS)
            for lb_ in range(CONV_CH // LANE):
                lanes = slice(lb_ * LANE, (lb_ + 1) * LANE)
                ys_ref[rows, lanes] = bdw_ref[:, lanes] + _tap_conv(ubuf, wdw_ref, r * CONV_ROWS, CONV_FWD_TAPS, lanes)
        gmat = gmat_s[...]
        yv = ys_ref[...]
        d = yv - _gmean(yv, gmat)
        rs = lax.rsqrt(_gmean(d * d, gmat) + GN_EPS)
        z = d * rs * gain_ref[...] + bias_ref[...]
        cat_ref[:, 0:CONV_CH] = (z * _sig(z)).astype(BF16)

        lb, _ = _lower_bound(lbl_ref)
        o0 = 2 * CONV_CH
        pr = _hgrn_prep(p_ref[:, o0:o0 + HGRN_W], p_ref[:, o0 + HGRN_W:o0 + 2 * HGRN_W], lb, lower_s[...])
        qt_s[...] = pr["qt"].astype(BF16)
        kt_s[...] = pr["kt"].astype(BF16)
        kh_s[...] = pr["kh"].astype(BF16)
        v_s[...] = p_ref[:, o0 + 2 * HGRN_W:o0 + 3 * HGRN_W].astype(BF16)
        egl_s[...] = jnp.exp(pr["Gl"])
        tri = _tri()

        def chunk(ci, carry):
            r0 = pl.multiple_of(ci * CHUNK, CHUNK)
            rows = pl.ds(r0, CHUNK)
            for h in range(N_HEADS):
                ls = pl.ds(h * HEAD_D, HEAD_D)
                qc, kc, hc, vc = qt_s[rows, ls], kt_s[rows, ls], kh_s[rows, ls], v_s[rows, ls]
                s0 = state[h]
                s0b = s0.astype(BF16)
                st_ref[ci, h] = s0
                att =jnp.where(tri, _dot_nt(qc, kc), 0.0).astype(BF16)
                o_ref[rows, ls] = _dot(att, vc) + _dot_nt(qc, s0b)
                state[h] = s0 * egl_s[pl.ds(r0, 1), ls] + _dot_tn(vc, hc)
            return carry

        lax.fori_loop(0, nch, chunk, 0, unroll=min(CHUNK_UNROLL, nch))
        for h in range(N_HEADS):
            sl = slice(h * HEAD_D, (h + 1) * HEAD_D)
            oh = o_ref[:, sl]
            gh = p_ref[:, o0 + 3 * HGRN_W + h * HEAD_D:o0 + 3 * HGRN_W + (h + 1) * HEAD_D]
            rsh = lax.rsqrt(jnp.mean(oh * oh, axis=-1, keepdims=True) + RMS_EPS)
            hg = (oh * rsh) * gout_ref[:, sl] * (gh * _sig(gh))
            cat_ref[:, CONV_CH + h * HEAD_D:CONV_CH + (h + 1) * HEAD_D] = hg.astype(BF16)

        gate_gain = mod_ref[2:3, :] * gp_ref[...]
        gain_scale = gf_ref[...] * (1.0 + mod_ref[4:5, :])
        for rows in _row_chains(tm):
            yv = _dot(cat_ref[rows, :], wout_ref[...])
            y_ref[rows, :] = yv
            rsy = lax.rsqrt(jnp.mean(yv * yv, axis=-1, keepdims=True) + RMS_EPS)
            x1 = x_ref[rows, :] + (yv * rsy) * gate_gain
            x1_ref[rows, :] = x1
            rs1 = lax.rsqrt(jnp.mean(x1 * x1, axis=-1, keepdims=True) + RMS_EPS)
            h2_ref[rows, :] = ((x1 * rs1) * gain_scale + mod_ref[3:4, :]).astype(BF16)

        @pl.when(i == min(nt - 1, nt // 2 + 1))
        def _():
            _ring_forward(gout_bufs, gssem, grsem)

        @pl.when(i == nt - 1)
        def _():
            _ring_finish(gout_bufs, gssem, grsem)

    tile = lambda cols: pl.BlockSpec((tm, cols), lambda i: (i, 0))
    hbm = pl.BlockSpec(memory_space=pl.ANY)
    outs = pl.pallas_call(
        body, name="mixers_fwd", grid=(nt,),
        in_specs=[tile(IN_COLS), _full((HALO, CONV_CH))] + [_full((1, CONV_CH))] * 4 + [_full((2, HGRN_W))]
        + [_full((D_MODEL, D_MODEL)), tile(D_MODEL), _full((6, D_MODEL)), _full((1, D_MODEL)), _full((1, D_MODEL))]
        + [hbm] * ng,
        out_specs=[tile(D_MODEL), tile(CONV_CH), tile(HGRN_W),
                   pl.BlockSpec((nch, N_HEADS, HEAD_D, HEAD_D), lambda i: (i, 0, 0, 0)),
                   tile(D_MODEL), tile(D_MODEL), tile(D_MODEL)] + [hbm] * ng,
        out_shape=[_big((T, D_MODEL), BF16), _big((T, CONV_CH), F32), _big((T, HGRN_W), F32),
                   _big((T // CHUNK, N_HEADS, HEAD_D, HEAD_D), F32), _big((T, D_MODEL), F32),
                   _big((T, D_MODEL), F32), _big((T, D_MODEL), BF16)] + [_big(b.shape, BF16) for b in gbufs],
        input_output_aliases={n_in + t: n_out + t for t in range(ng)},
        scratch_shapes=[pltpu.VMEM((tm + HALO + SUB, CONV_CH), F32), pltpu.VMEM((N_HEADS, HEAD_D, HEAD_D), F32),
                        pltpu.VMEM((tm, HGRN_W), BF16), pltpu.VMEM((tm, HGRN_W), BF16),
                        pltpu.VMEM((tm, HGRN_W), BF16), pltpu.VMEM((tm, HGRN_W), BF16),
                        pltpu.VMEM((tm, HGRN_W), F32), pltpu.VMEM((tm, tm), BF16),
                        pltpu.VMEM((CONV_CH, CONV_CH), BF16)] + _ring_sems(ng),
        compiler_params=_cp(("arbitrary",), 56),
    )(_hbm(p), wdw, *vecs, lbl, *_hbm(w_out, x), modr, g_post, g_ffn, *[_hbm(b) for b in gbufs])
    return outs[:n_out], outs[n_out:]


def _row_chains(rows, n=2):
    step = rows // n
    return [slice(k * step, (k + 1) * step) for k in range(n)]


def _ffn_blocks():
    return D_FF // FFN_BLOCK, (D_FF // N_CHIPS) // FFN_BLOCK


def _ffn_fwd(h2, w_up_g, w_down, x1, target, modr, g_post):
    T = h2.shape[0]
    tm = min(FFN_TILE, T)
    fb = FFN_BLOCK
    nj, per = _ffn_blocks()

    def body(h_ref, wu_ref, wd_ref, x1_ref, t_ref, mod_ref, g_ref, r_ref, dy2_ref, dx2_ref, st_ref, acc):
        i, j = pl.program_id(0), pl.program_id(1)

        @pl.when((i == 0) & (j == 0))
        def _():
            st_ref[...] = jnp.zeros(st_ref.shape, F32)

        @pl.when(j == 0)
        def _():
            acc[...] = jnp.zeros(acc.shape, F32)

        for rows in _row_chains(tm):
            ra = jnp.maximum(_dot(h_ref[rows, :], wu_ref[0]), 0.0)
            rb = (ra * ra).astype(BF16)
            r_ref[rows, :] = rb
            acc[rows, :] += _dot(rb, wd_ref[...])

        @pl.when(j == nj - 1)
        def _():
            y2 = acc[...]
            rs = lax.rsqrt(jnp.mean(y2 * y2, axis=-1, keepdims=True) + RMS_EPS)
            nh = y2 * rs
            gp, gt = g_ref[...], mod_ref[5:6, :]
            gate_gain = gt * gp
            err = x1_ref[...] + nh * gate_gain - t_ref[...]
            dx2 = err * (1.0 / D_MODEL)
            dx2_ref[...] = dx2
            st_ref[0:1, :] += _colsum(err * err)
            s_dn = _colsum(dx2 * nh)
            st_ref[1:2, :] += s_dn * gp
            st_ref[2:3, :] += s_dn * gt
            dy2_ref[...] = _rms_bwd(dx2 * gate_gain, nh, rs).astype(BF16)

    tile = pl.BlockSpec((tm, D_MODEL), lambda i, j: (i, 0))
    return pl.pallas_call(
        body, name="ffn_fwd", grid=(T // tm, nj),
        in_specs=[tile, pl.BlockSpec((1, D_MODEL, fb), lambda i, j: (j // per, 0, j % per)),
                  pl.BlockSpec((fb, D_MODEL), lambda i, j: (j, 0)), tile, tile,
                  _full((6, D_MODEL)), _full((1, D_MODEL))],
        out_specs=[pl.BlockSpec((tm, fb), lambda i, j: (i, j)), tile, tile, _full((8, D_MODEL))],
        out_shape=[_big((T, D_FF), BF16), _big((T, D_MODEL), BF16), _big((T, D_MODEL), F32),
                   jax.ShapeDtypeStruct((8, D_MODEL), F32)],
        scratch_shapes=[pltpu.VMEM((tm, D_MODEL), F32)],
        compiler_params=_cp(("arbitrary", "arbitrary"), 56),
    )(*_hbm(h2, w_up_g, w_down, x1, target), modr, g_post)


def _rms_bwd(dxn, xn, rs):
    return rs * (dxn - xn * jnp.mean(dxn * xn, axis=-1, keepdims=True))


def _ffn_bwd(dy2, r, x1, dx2, w_up_g, w_down, modr, g_ffn):
    T = dx2.shape[0]
    tm = min(FFN_TILE, T)
    fb = FFN_BLOCK
    nj, per = _ffn_blocks()

    def body(dy2_ref, r_ref, x1_ref, dx2_ref, wu_ref, wd_ref, mod_ref, gf_ref, da_ref, dx1_ref, st_ref, dh_s):
        i, j = pl.program_id(0), pl.program_id(1)

        @pl.when((i == 0) & (j == 0))
        def _():
            st_ref[...] = jnp.zeros(st_ref.shape, F32)

        @pl.when(j == 0)
        def _():
            dh_s[...] = jnp.zeros(dh_s.shape, F32)

        for rows in _row_chains(tm):
            ra = jnp.sqrt(r_ref[rows, :].astype(F32))
            da = (_dot_nt(dy2_ref[rows, :], wd_ref[...]) * (2.0 * ra)).astype(BF16)
            da_ref[rows, :] = da
            dh_s[rows, :] += _dot_nt(da, wu_ref[0])

        @pl.when(j == nj - 1)
        def _():
            dh = dh_s[...]
            x1v = x1_ref[...]
            rs1 = lax.rsqrt(jnp.mean(x1v * x1v, axis=-1, keepdims=True) + RMS_EPS)
            xn = x1v * rs1
            gf, scale1 = gf_ref[...], 1.0 + mod_ref[4:5, :]
            st_ref[0:1, :] += _colsum(dh)
            s_dh = _colsum(dh * xn)
            st_ref[1:2, :] += s_dh * gf
            st_ref[2:3, :] += s_dh * scale1
            dx1_ref[...] = dx2_ref[...] + _rms_bwd(dh * (scale1 * gf), xn, rs1)

    tile = pl.BlockSpec((tm, D_MODEL), lambda i, j: (i, 0))
    ftile = pl.BlockSpec((tm, fb), lambda i, j: (i, j))
    return pl.pallas_call(
        body, name="ffn_bwd", grid=(T // tm, nj),
        in_specs=[tile, ftile, tile, tile, pl.BlockSpec((1, D_MODEL, fb), lambda i, j: (j // per, 0, j % per)),
                  pl.BlockSpec((fb, D_MODEL), lambda i, j: (j, 0)), _full((6, D_MODEL)), _full((1, D_MODEL))],
        out_specs=[ftile, tile, _full((8, D_MODEL))],
        out_shape=[_big((T, D_FF), BF16), _big((T, D_MODEL), F32), jax.ShapeDtypeStruct((8, D_MODEL), F32)],
        scratch_shapes=[pltpu.VMEM((tm, D_MODEL), F32)],
        compiler_params=_cp(("arbitrary", "arbitrary"), 56),
    )(*_hbm(dy2, r, x1, dx2, w_up_g, w_down), modr, g_ffn)


def _mix_out_bwd(dx1, y, cat, w_out, modr, g_post, swap):
    T = dx1.shape[0]
    tm = _tok_tile(T)
    nt = T // tm
    ns = len(swap)

    def body(*refs):
        dx1_ref, y_ref, cat_ref, w_ref, mod_ref, gp_ref = refs[:6]
        s_ins = refs[6:6 + ns]
        dcat_ref, st_ref, gw_ref = refs[6 + ns:9 + ns]
        s_outs = refs[9 + ns:9 + 2 * ns]
        gacc, gsem, pssem, prsem = refs[9 + 2 * ns:]
        i = pl.program_id(0)

        @pl.when(i == 0)
        def _():
            for cp in _pair_copies(s_ins, s_outs, pssem, prsem):
                cp.start()
            st_ref[...] = jnp.zeros(st_ref.shape, F32)
            gacc[...] = jnp.zeros(gacc.shape, F32)

        dxv, yv = dx1_ref[...], y_ref[...]
        rs = lax.rsqrt(jnp.mean(yv * yv, axis=-1, keepdims=True) + RMS_EPS)
        nh = yv * rs
        st_ref[0:1, :] += _colsum(dxv * (nh * gp_ref[...]))
        dn = dxv * mod_ref[2:3, :]
        st_ref[1:2, :] += _colsum(dn * nh)
        dy = _rms_bwd(dn * gp_ref[...], nh, rs).astype(BF16)
        dcat_ref[...] = _dot_nt(dy, w_ref[...])
        for cols in _row_chains(D_MODEL):
            gacc[:, cols] += _dot_tn(cat_ref[...], dy[:, cols])

        @pl.when(i == nt - 1)
        def _():
            out = pltpu.make_async_copy(gacc, gw_ref, gsem)
            out.start()
            copies = _pair_copies(s_ins, s_outs, pssem, prsem)
            for cp in copies:
                cp.wait_recv()
            for cp in copies:
                cp.wait_send()
            out.wait()

    tile = pl.BlockSpec((tm, D_MODEL), lambda i: (i, 0))
    hbm = pl.BlockSpec(memory_space=pl.ANY)
    outs = pl.pallas_call(
        body, name="mix_out_bwd", grid=(nt,),
        in_specs=[tile, tile, tile, _full((D_MODEL, D_MODEL)), _full((6, D_MODEL)), _full((1, D_MODEL))]
        + [hbm] * ns,
        out_specs=[tile, _full((8, D_MODEL)), hbm] + [hbm] * ns,
        out_shape=[_big((T, D_MODEL), F32), jax.ShapeDtypeStruct((8, D_MODEL), F32), _big((D_MODEL, D_MODEL), F32)]
        + _pair_shapes(swap),
        scratch_shapes=[pltpu.VMEM((D_MODEL, D_MODEL), F32), pltpu.SemaphoreType.DMA] + _pair_sems(ns),
        compiler_params=_cp(("arbitrary",), 48),
    )(*_hbm(dx1, y, cat, w_out), modr, g_post, *[_hbm(g) for g in swap])
    return outs[:3], outs[3:]


def _mixers_bwd(p, dcat, ys, o, states, h1, wdw, vecs, lbl, pairs_b):
    T = p.shape[0]
    tm = min(MIXB_TILE, T)
    nt = T // tm
    nch = tm // CHUNK
    hpt = tm // HALO
    nx = len(pairs_b)
    nb = IN_COLS // N_CHIPS

    def body(*refs):
        (p_ref, ph_ref, dcat_ref, ys_ref, o_ref, st_ref, h1_ref, wdw_ref, bdw_ref, gain_ref, bias_ref, gout_ref,
         lbl_ref) = refs[:13]
        x_ins = refs[13:13 + nx]
        dp_ref, sb_ref, s5_ref, dw_ref = refs[13 + nx:17 + nx]
        x_outs = refs[17 + nx:17 + 2 * nx]
        gin_ref, ginb_ref = refs[17 + 2 * nx], refs[18 + 2 * nx]
        (ubuf, dybuf, carry, dstate, qt_s, kt_s, kh_s, v_s, do_s, egl_s, dqt_s, dkt_s, dkh_s, dv_s, dgl_s,
         dsh, dw8, dshift, lower_s, upper_s, same_s, gmat_s, gacc, gacc_b, dp_prev, h1_prev, gsem, gbsem, xssem,
         xrsem) = refs[19 + 2 * nx:]
        i = pl.program_id(0)
        tile_idx = nt - 1 - i

        @pl.when(i == 0)
        def _():
            for cp in _xchg_copies(x_ins, x_outs, xssem, xrsem):
                cp.start()
            gacc[...] = jnp.zeros(gacc.shape, F32)
            dstate[...] = jnp.zeros(dstate.shape, F32)
            carry[...] = jnp.zeros(carry.shape, F32)
            sb_ref[...] = jnp.zeros(sb_ref.shape, F32)
            s5_ref[...] = jnp.zeros(s5_ref.shape, F32)
            dw_ref[...] = jnp.zeros(dw_ref.shape, F32)
            dw8[...] = jnp.zeros(dw8.shape, F32)
            lower_s[...], upper_s[...], same_s[...] = _chunk_masks(tm)
            gmat_s[...] = _gn_matrix()
            dsh[0:SUB, :] = jnp.zeros((SUB, CONV_CH), F32)
            dsh[SUB + tm:2 * SUB + tm, :] = jnp.zeros((SUB, CONV_CH), F32)
            ubuf[HALO + tm:HALO + tm + SUB, :] = jnp.zeros((SUB, CONV_CH), F32)
            dp_prev[...] = jnp.zeros(dp_prev.shape, BF16)
            h1_prev[...] = jnp.zeros(h1_prev.shape, BF16)

        n_pieces = (tm // CONV_ROWS) * (CONV_CH // LANE)
        per_block = n_pieces // N_CHIPS
        prow = D_MODEL // per_block

        def w_in_grad_piece(k):
            j, part = k // per_block, k % per_block
            rows_k = slice(part * prow, (part + 1) * prow)
            gacc[j, rows_k, :] += _dot_tn(h1_prev[:, rows_k], dp_prev[:, j * nb:(j + 1) * nb])

        uh = ph_ref[:, 0:CONV_CH] * _sig(ph_ref[:, CONV_CH:2 * CONV_CH])
        ubuf[0:HALO, :] = jnp.where(tile_idx > 0, uh, 0.0)
        ubuf[HALO:HALO + tm, :] = p_ref[:, 0:CONV_CH] * _sig(p_ref[:, CONV_CH:2 * CONV_CH])
        gmat = gmat_s[...]
        gain = gain_ref[...]
        yv = ys_ref[...]
        d = yv - _gmean(yv, gmat)
        rs = lax.rsqrt(_gmean(d * d, gmat) + GN_EPS)
        yn = d * rs
        z = yn * gain + bias_ref[...]
        sz = _sig(z)
        dz = dcat_ref[:, 0:CONV_CH] * (sz * (1.0 + z * (1.0 - sz)))
        dyn = dz * gain
        dyc = rs * (dyn - _gmean(dyn, gmat) - yn * _gmean(dyn * yn, gmat))
        s5_ref[0:1, :] += _colsum(dyc)
        s5_ref[1:2, :] += _colsum(dz * yn)
        s5_ref[2:3, :] += _colsum(dz)
        dybuf[tm:tm + HALO, :] = carry[...]
        dybuf[0:tm, :] = dyc
        dsh[SUB:SUB + tm, :] = dyc
        carry[...] = dyc[0:HALO, :]
        for b in range(SUB):
            dshift[...] = dsh[SUB - b:2 * SUB - b + tm, :]
            for j, off in CONV_FWD_TAPS:
                if off % SUB == b:
                    prod = dshift[...] * ubuf[off - b:off - b + tm + SUB, :]
                    dw8[j] += jnp.sum(prod.reshape((tm + SUB) // SUB, SUB, CONV_CH), axis=0)
        for r in range(tm // CONV_ROWS):
            rows = slice(r * CONV_ROWS, (r + 1) * CONV_ROWS)
            for lb_ in range(CONV_CH // LANE):
                lanes = slice(lb_ * LANE, (lb_ + 1) * LANE)
                glanes = slice(CONV_CH + lb_ * LANE, CONV_CH + (lb_ + 1) * LANE)
                w_in_grad_piece(r * (CONV_CH // LANE) + lb_)
                acc = _tap_conv(dybuf, wdw_ref, r * CONV_ROWS, CONV_BWD_TAPS, lanes)
                val = p_ref[rows, lanes]
                sg = _sig(p_ref[rows, glanes])
                dval = acc * sg
                dgate = acc * val * (sg * (1.0 - sg))
                dp_ref[rows, lanes] = dval.astype(BF16)
                dp_ref[rows, glanes] = dgate.astype(BF16)
                sb_ref[0:1, lanes] += _colsum(dval)
                sb_ref[0:1, glanes] += _colsum(dgate)

        o0 = 2 * CONV_CH
        for h in range(N_HEADS):
            sl = slice(h * HEAD_D, (h + 1) * HEAD_D)
            gsl = slice(o0 + 3 * HGRN_W + h * HEAD_D, o0 + 3 * HGRN_W + (h + 1) * HEAD_D)
            oh = o_ref[:, sl]
            gh = p_ref[:, gsl]
            dh = dcat_ref[:, CONV_CH + h * HEAD_D:CONV_CH + (h + 1) * HEAD_D]
            gout = gout_ref[:, sl]
            rsh = lax.rsqrt(jnp.mean(oh * oh, axis=-1, keepdims=True) + RMS_EPS)
            on = oh * rsh
            sgg = _sig(gh)
            dgh = dh * (on * gout) * (sgg * (1.0 + gh * (1.0 - sgg)))
            dm = dh * (gh * sgg)
            s5_ref[3:4, sl] += _colsum(dm * on)
            do_s[:, sl] = _rms_bwd(dm * gout, on, rsh).astype(BF16)
            dp_ref[:, gsl] = dgh.astype(BF16)
            sb_ref[2:3, CONV_CH + h * HEAD_D:CONV_CH + (h + 1) * HEAD_D] += _colsum(dgh)

        lb, _ = _lower_bound(lbl_ref)
        pq = p_ref[:, o0:o0 + HGRN_W]
        pr = _hgrn_prep(pq, p_ref[:, o0 + HGRN_W:o0 + 2 * HGRN_W], lb, lower_s[...])
        qt_s[...] = pr["qt"].astype(BF16)
        kt_s[...] = pr["kt"].astype(BF16)
        kh_s[...] = pr["kh"].astype(BF16)
        v_s[...] = p_ref[:, o0 + 2 * HGRN_W:o0 + 3 * HGRN_W].astype(BF16)
        egl_s[...] = jnp.exp(pr["Gl"])
        tri = _tri()

        def chunk(it, c_):
            ci = nch - 1 - it
            r0 = pl.multiple_of(ci * CHUNK, CHUNK)
            rows = pl.ds(r0, CHUNK)
            for h in range(N_HEADS):
                ls = pl.ds(h * HEAD_D, HEAD_D)
                qc, kc, hc, vc = qt_s[rows, ls], kt_s[rows, ls], kh_s[rows, ls], v_s[rows, ls]
                dob = do_s[rows, ls]
                s0 = st_ref[ci, h]
                s0b = s0.astype(BF16)
                ds1 = dstate[h]
                ds1b = ds1.astype(BF16)
                egl = egl_s[pl.ds(r0, 1), ls]
                att = jnp.where(tri, _dot_nt(qc, kc), 0.0).astype(BF16)
                datt = jnp.where(tri, _dot_nt(dob, vc), 0.0).astype(BF16)
                dv_s[rows, ls] = _dot_tn(att, dob) + _dot_nt(hc, ds1b)
                dqt_s[rows, ls] = _dot(datt, kc) + _dot(dob, s0b)
                dkt_s[rows, ls] = _dot_tn(datt, qc)
                dkh_s[rows, ls] = _dot(vc, ds1b)
                dgl = egl * _colsum(ds1 * s0)
                dgl_s[rows, ls] = jnp.broadcast_to(dgl, (CHUNK, HEAD_D))
                dstate[h] = ds1 * egl + _dot_tn(dob, qc)
            return c_

        lax.fori_loop(0, nch, chunk, 0, unroll=min(CHUNK_UNROLL, nch))
        for hl in range(2):
            ls = slice(hl * (HGRN_W // 2), (hl + 1) * (HGRN_W // 2))
            dqt, dkt, dkh = dqt_s[:, ls], dkt_s[:, ls], dkh_s[:, ls]
            dk = dkt * pr["enG"][:, ls] + dkh * pr["eGlG"][:, ls]
            khk = dkh * kh_s[:, ls].astype(F32)
            dG = dqt * qt_s[:, ls].astype(F32) - dkt * kt_s[:, ls].astype(F32) - khk
            dlogf = _mm3(upper_s[...], dG) + _mm3(same_s[...], khk) + dgl_s[:, ls]
            df = dlogf / pr["f"][:, ls] - dk
            sf, sq = pr["sf"][:, ls], pr["sq"][:, ls]
            s5_ref[4:5, ls] += _colsum(df * (1.0 - sf))
            dfl = df * (1.0 - lb[:, ls]) * (sf * (1.0 - sf))
            dq = (dqt * pr["eG"][:, ls]) * (sq * (1.0 + pq[:, ls] * (1.0 - sq)))
            dvv = dv_s[:, ls]
            c0 = o0 + hl * (HGRN_W // 2)
            dp_ref[:, c0:c0 + HGRN_W // 2] = dq.astype(BF16)
            dp_ref[:, c0 + HGRN_W:c0 + HGRN_W + HGRN_W // 2] = dfl.astype(BF16)
            dp_ref[:, c0 + 2 * HGRN_W:c0 + 2 * HGRN_W + HGRN_W // 2] = dvv.astype(BF16)
            sb_ref[1:2, ls] += _colsum(dq)
            sb_ref[1:2, HGRN_W + hl * (HGRN_W // 2):HGRN_W + (hl + 1) * (HGRN_W // 2)] += _colsum(dfl)
            sb_ref[2:3, ls] += _colsum(dvv)

        dp_prev[...] = dp_ref[...]
        h1_prev[...] = h1_ref[...]

        @pl.when(i == nt - 1)
        def _():
            for k in range(n_pieces):
                w_in_grad_piece(k)
            out = pltpu.make_async_copy(gacc, gin_ref, gsem)
            out.start()
            for j in range(N_CHIPS):
                gacc_b[j] = gacc[j].astype(BF16)
            out_b = pltpu.make_async_copy(gacc_b, ginb_ref, gbsem)
            out_b.start()
            for j in range(CONV_K):
                dw_ref[j:j + 1, :] = _colsum(dw8[j])
            copies = _xchg_copies(x_ins, x_outs, xssem, xrsem)
            for cp in copies:
                cp.wait_recv()
            for cp in copies:
                cp.wait_send()
            out.wait()
            out_b.wait()

    rev = lambda cols: pl.BlockSpec((tm, cols), lambda i: (nt - 1 - i, 0))
    halo = pl.BlockSpec((HALO, 2 * CONV_CH), lambda i: (jnp.maximum((nt - 1 - i) * hpt - 1, 0), 0))
    wide = lambda n: pltpu.VMEM((tm, HGRN_W), n)
    hbm = pl.BlockSpec(memory_space=pl.ANY)
    outs = pl.pallas_call(
        body, name="mixers_bwd", grid=(nt,),
        in_specs=[rev(IN_COLS), halo, rev(D_MODEL), rev(CONV_CH), rev(HGRN_W),
                  pl.BlockSpec((nch, N_HEADS, HEAD_D, HEAD_D), lambda i: (nt - 1 - i, 0, 0, 0)), rev(D_MODEL),
                  _full((HALO, CONV_CH))] + [_full((1, CONV_CH))] * 4 + [_full((2, HGRN_W))] + [hbm] * nx,
        out_specs=[rev(IN_COLS), _full((8, D_MODEL)), _full((8, CONV_CH)), _full((HALO, CONV_CH))]
        + [hbm] * (nx + 2),
        out_shape=[_big((T, IN_COLS), BF16), jax.ShapeDtypeStruct((8, D_MODEL), F32),
                   jax.ShapeDtypeStruct((8, CONV_CH), F32), jax.ShapeDtypeStruct((HALO, CONV_CH), F32)]
        + [_big(pb.shape, BF16) for pb in pairs_b]
        + [_big((N_CHIPS, D_MODEL, nb), F32), _big((N_CHIPS, D_MODEL, nb), BF16)],
        scratch_shapes=[pltpu.VMEM((tm + HALO + SUB, CONV_CH), F32), pltpu.VMEM((tm + HALO, CONV_CH), F32),
                        pltpu.VMEM((HALO, CONV_CH), F32), pltpu.VMEM((N_HEADS, HEAD_D, HEAD_D), F32),
                        wide(BF16), wide(BF16), wide(BF16), wide(BF16), wide(BF16),
                        wide(F32), wide(F32), wide(F32), wide(F32), wide(F32), wide(F32),
                        pltpu.VMEM((tm + 2 * SUB, CONV_CH), F32), pltpu.VMEM((HALO, SUB, CONV_CH), F32),
                        pltpu.VMEM((tm + SUB, CONV_CH), F32), pltpu.VMEM((tm, tm), BF16), pltpu.VMEM((tm, tm), BF16),
                        pltpu.VMEM((tm, tm), BF16), pltpu.VMEM((CONV_CH, CONV_CH), BF16),
                        pltpu.VMEM((N_CHIPS, D_MODEL, nb), F32), pltpu.VMEM((N_CHIPS, D_MODEL, nb), BF16),
                        pltpu.VMEM((tm, IN_COLS), BF16), pltpu.VMEM((tm, D_MODEL), BF16),
                        pltpu.SemaphoreType.DMA, pltpu.SemaphoreType.DMA]
        + _xchg_sems(nx),
        compiler_params=_cp(("arbitrary",), 56),
    )(*_hbm(p, p, dcat, ys, o, states, h1), wdw, *vecs, lbl, *[_hbm(pb) for pb in pairs_b])
    return outs[:4], outs[4:4 + nx], (outs[4 + nx], outs[5 + nx])


def _mix_in_bwd(dp, w_in_g, x, dx1, modr, g_pre, pairs_b):
    T = x.shape[0]
    tm = _tok_tile(T)
    nt = T // tm
    nb = IN_COLS // N_CHIPS
    nx = len(pairs_b)

    def body(*refs):
        dp_ref, w_ref, x_ref, dx1_ref, mod_ref, g_ref = refs[:6]
        x_ins = refs[6:6 + nx]
        gx_ref, st_ref = refs[6 + nx:8 + nx]
        x_outs = refs[8 + nx:8 + 2 * nx]
        xssem, xrsem = refs[8 + 2 * nx:]
        i = pl.program_id(0)

        @pl.when(i == 0)
        def _():
            for cp in _xchg_copies(x_ins, x_outs, xssem, xrsem):
                cp.start()
            st_ref[...] = jnp.zeros(st_ref.shape, F32)

        g, scale1 = g_ref[...], 1.0 + mod_ref[1:2, :]
        for rows in _row_chains(tm):
            dh = None
            for j in range(N_CHIPS):
                part = _dot_nt(dp_ref[rows, j * nb:(j + 1) * nb], w_ref[j])
                dh = part if dh is None else dh + part
            xv = x_ref[rows, :]
            rs = lax.rsqrt(jnp.mean(xv * xv, axis=-1, keepdims=True) + RMS_EPS)
            xn = xv * rs
            st_ref[0:1, :] += _colsum(dh)
            s_dh = _colsum(dh * xn)
            st_ref[1:2, :] += s_dh * g
            st_ref[2:3, :] += s_dh * scale1
            gx_ref[rows, :] = dx1_ref[rows, :] + _rms_bwd(dh * (scale1 * g), xn, rs)

        @pl.when(i == nt - 1)
        def _():
            copies = _xchg_copies(x_ins, x_outs, xssem, xrsem)
            for cp in copies:
                cp.wait_recv()
            for cp in copies:
                cp.wait_send()

    tile = pl.BlockSpec((tm, D_MODEL), lambda i: (i, 0))
    hbm = pl.BlockSpec(memory_space=pl.ANY)
    outs = pl.pallas_call(
        body, name="mix_in_bwd", grid=(nt,),
        in_specs=[pl.BlockSpec((tm, IN_COLS), lambda i: (i, 0)), _full((N_CHIPS, D_MODEL, nb)), tile, tile,
                  _full((6, D_MODEL)), _full((1, D_MODEL))] + [hbm] * nx,
        out_specs=[tile, _full((8, D_MODEL))] + [hbm] * nx,
        out_shape=[_big((T, D_MODEL), F32), jax.ShapeDtypeStruct((8, D_MODEL), F32)]
        + [_big(pb.shape, BF16) for pb in pairs_b],
        scratch_shapes=_xchg_sems(nx),
        compiler_params=_cp(("arbitrary",), 48),
    )(*_hbm(dp, w_in_g, x, dx1), modr, g_pre, *[_hbm(pb) for pb in pairs_b])
    return outs[:2], outs[2:]


def _weight_grad(a, b, a_blocked, b_blocked, name):
    T = a.shape[0]
    tt = min(GRAD_TILE, T)
    nt = T // tt
    ka = a.shape[1] // N_CHIPS if a_blocked else a.shape[1]
    nb = b.shape[1] // N_CHIPS if b_blocked else b.shape[1]

    def body(a_ref, b_ref, o_ref, ob_ref):
        t = pl.program_id(1)

        @pl.when(t == 0)
        def _():
            o_ref[...] = jnp.zeros(o_ref.shape, F32)

        for cols in _row_chains(nb):
            o_ref[0, :, cols] += _dot_tn(a_ref[...], b_ref[:, cols])

        @pl.when(t == nt - 1)
        def _():
            ob_ref[0] = o_ref[0].astype(BF16)

    blk = pl.BlockSpec((1, ka, nb), lambda j, t: (j, 0, 0))
    return pl.pallas_call(
        body, name=name, grid=(N_CHIPS, nt),
        in_specs=[pl.BlockSpec((tt, ka), (lambda j, t: (t, j)) if a_blocked else (lambda j, t: (t, 0))),
                  pl.BlockSpec((tt, nb), (lambda j, t: (t, j)) if b_blocked else (lambda j, t: (t, 0)))],
        out_specs=[blk, blk],
        out_shape=[_big((N_CHIPS, ka, nb), F32), _big((N_CHIPS, ka, nb), BF16)],
        compiler_params=_cp(("arbitrary", "arbitrary"), 48),
    )(*_hbm(a, b))


R_LOSS = 0
R_FFN = 8
R_OUT = 16
R_IN = 24
R_BIN = 32
R_512 = 40
R_DW = 48
N_STAT_ROWS = 80
MOD_ROWS = (R_IN + 0, R_IN + 1, R_OUT + 0, R_FFN + 0, R_FFN + 1, R_LOSS + 1)


def _small_update(gath, params):
    names = ["b_ada", "lb_logits", "g_pre_mix", "b_in", "b_dw", "gn_gain", "gn_bias", "g_hgrn_out", "g_post_mix",
             "g_pre_ffn", "g_post_ffn"]
    flat = []
    for n in names:
        flat += list(params[n])
    n_in = 1 + len(flat)

    def body(*refs):
        g_ref = refs[0]
        prm = {n: refs[1 + 3 * k:4 + 3 * k] for k, n in enumerate(names)}
        outs = refs[n_in:]
        loss_ref, dmod_ref, dwdw_ref = outs[0], outs[1], outs[2]
        res = {n: outs[3 + 4 * k:7 + 4 * k] for k, n in enumerate(names)}
        red = g_ref[0]
        for dev in range(1, N_DEV):
            red = red + g_ref[dev]
        loss_ref[...] = jnp.broadcast_to(
            (0.5 / D_MODEL) * jnp.sum(red[R_LOSS:R_LOSS + 1, :], axis=-1, keepdims=True), loss_ref.shape)
        for dev in range(N_DEV):
            for k, r in enumerate(MOD_ROWS):
                dmod_ref[dev:dev + 1, k * D_MODEL:(k + 1) * D_MODEL] = g_ref[dev, r:r + 1, :]
        dwdw_ref[...] = red[R_DW:R_DW + HALO, 0:CONV_CH]

        def finish(name, pieces):
            w_ref, m_ref, v_ref = prm[name]
            g_out, d_out, m_out, v_out = res[name]
            for rsl, lsl, g in pieces:
                d, m2, v2 = _adam_math(w_ref[rsl, lsl], g, m_ref[rsl, lsl], v_ref[rsl, lsl])
                g_out[rsl, lsl] = g
                d_out[rsl, lsl] = d
                m_out[rsl, lsl] = m2
                v_out[rsl, lsl] = v2

        one = slice(0, 1)
        row = lambda r: red[r:r + 1, :]
        half = lambda r: red[r:r + 1, 0:CONV_CH]
        finish("b_ada", [(one, slice(k * D_MODEL, (k + 1) * D_MODEL), row(r)) for k, r in enumerate(MOD_ROWS)])
        finish("b_in", [(one, slice(k * D_MODEL, (k + 1) * D_MODEL), row(R_BIN + k)) for k in range(3)])
        finish("g_pre_mix", [(one, slice(None), row(R_IN + 2))])
        finish("g_post_mix", [(one, slice(None), row(R_OUT + 1))])
        finish("g_pre_ffn", [(one, slice(None), row(R_FFN + 2))])
        finish("g_post_ffn", [(one, slice(None), row(R_LOSS + 2))])
        finish("b_dw", [(one, slice(None), half(R_512 + 0))])
        finish("gn_gain", [(one, slice(None), half(R_512 + 1))])
        finish("gn_bias", [(one, slice(None), half(R_512 + 2))])
        finish("g_hgrn_out", [(one, slice(None), half(R_512 + 3))])
        s0, s1 = _lower_bound(prm["lb_logits"][0])
        dlb = half(R_512 + 4)
        finish("lb_logits", [(slice(0, 1), slice(None), dlb * s0 * (1.0 - s0)),
                             (slice(1, 2), slice(None), -dlb * s0 * s1)])

    vm = pl.BlockSpec(memory_space=pltpu.VMEM)
    out_shape = [jax.ShapeDtypeStruct((8, 128), F32), jax.ShapeDtypeStruct((N_DEV, 6 * D_MODEL), F32),
                 jax.ShapeDtypeStruct((HALO, CONV_CH), F32)]
    for n in names:
        out_shape += [jax.ShapeDtypeStruct(params[n][0].shape, F32)] * 4
    outs = pl.pallas_call(
        body, name="small_update", out_shape=out_shape,
        in_specs=[vm] * n_in, out_specs=[vm] * len(out_shape),
        compiler_params=_cp(None, 32),
    )(gath, *flat)
    return outs[0], outs[1], outs[2], {n: outs[3 + 4 * k:7 + 4 * k] for k, n in enumerate(names)}


def _wdw_adam(w, g, m, v):
    def body(w_ref, g_ref, m_ref, v_ref, d_out, m_out, v_out):
        d, m2, v2 = _adam_math(w_ref[...], g_ref[...], m_ref[...], v_ref[...])
        d_out[...] = d
        m_out[...] = m2
        v_out[...] = v2

    vm = pl.BlockSpec(memory_space=pltpu.VMEM)
    return pl.pallas_call(
        body, name="wdw_adam", out_shape=[jax.ShapeDtypeStruct(w.shape, F32)] * 3,
        in_specs=[vm] * 4, out_specs=[vm] * 3, compiler_params=_cp(None, 16),
    )(w, g, m, v)


def kernel(x, c, w_ada, b_ada, lb_logits, g_pre_mix, w_in, b_in, w_dw, b_dw, gn_gain, gn_bias, g_hgrn_out, w_out, g_post_mix, g_pre_ffn, w_up, w_down, g_post_ffn, loss_target, m_w_ada, m_b_ada, m_lb_logits, m_g_pre_mix, m_w_in, m_b_in, m_w_dw, m_b_dw, m_gn_gain, m_gn_bias, m_g_hgrn_out, m_w_out, m_g_post_mix, m_g_pre_ffn, m_w_up, m_w_down, m_g_post_ffn, v_w_ada, v_b_ada, v_lb_logits, v_g_pre_mix, v_w_in, v_b_in, v_w_dw, v_b_dw, v_gn_gain, v_gn_bias, v_g_hgrn_out, v_w_out, v_g_post_mix, v_g_pre_ffn, v_w_up, v_w_down, v_g_post_ffn):
    ax, ay, ac = lax.axis_index("x"), lax.axis_index("y"), lax.axis_index("c")
    chip = 2 * ax + ay
    T = x.shape[1]
    xs, tgt = x[0], loss_target[0]
    ada_cols = w_ada.shape[2]

    b_sh = lax.dynamic_slice_in_dim(b_ada, chip * ada_cols, ada_cols, axis=1)
    wdw_pad = jnp.pad(w_dw[0], ((0, HALO - CONV_K), (0, 0)))
    chip1 = jnp.reshape(chip, (1,)).astype(jnp.int32)
    place = jnp.stack([ac, chip]).astype(jnp.int32)
    _, c8, modg, wdwg = _ada_exchange(c, w_ada[0], b_sh, wdw_pad)
    modr = modg.reshape(6, D_MODEL)
    wdw_all = jnp.transpose(wdwg, (1, 0, 2)).reshape(HALO, CONV_CH)
    vec = (b_dw, gn_gain, gn_bias, g_hgrn_out)

    p, h1, (up_buf, down_buf), w_in_g, w_out_g = _mix_in_fwd(
        chip1, xs, modr, g_pre_mix, b_in.reshape(N_CHIPS, 1, IN_COLS // N_CHIPS), [w_up[0], w_down[0]],
        _cast_own(chip1, w_in[0], "cast_w_in"), _cast_own(chip1, w_out[0], "cast_w_out"))
    w_out_f = w_out_g.reshape(D_MODEL, D_MODEL)
    (cat, ys, o, states, y, x1, h2), (w_up_g, w_down_g) = _mixers_fwd(
        p, wdw_all, vec, lb_logits, w_out_f, xs, modr, g_post_mix, g_pre_ffn, [up_buf, down_buf])
    w_down_f = w_down_g.reshape(D_FF, D_MODEL)
    r, dy2, dx2, st_loss = _ffn_fwd(h2, w_up_g, w_down_f, x1, tgt, modr, g_post_ffn)

    def pair_sums(grads, got, tags):
        return [_pair_sum(place, g, o_, "pair_sum_" + t) for (g, _), o_, t in zip(grads, got, tags)]

    da, dx1, st_ffn = _ffn_bwd(dy2, r, x1, dx2, w_up_g, w_down_f, modr, g_pre_ffn)
    g_up = _weight_grad(h2, da, False, True, "grad_w_up")
    g_down = _weight_grad(r, dy2, True, False, "grad_w_down")
    (dcat, st_out, g_out), got_ud = _mix_out_bwd(dx1, y, cat, w_out_f, modr, g_post_mix, [g_up[1], g_down[1]])
    g_out = g_out.reshape(N_CHIPS, D_MODEL // N_CHIPS, D_MODEL)
    got_o = _pair_swap([g_out], "pair_swap_w_out")
    early = pair_sums([(g_out, None), g_up, g_down], list(got_o) + list(got_ud), ["w_out", "w_up", "w_down"])
    (dp, st_bin, st_512, dwdw), got_early, g_in = _mixers_bwd(p, dcat, ys, o, states, h1, wdw_all, vec, lb_logits,
                                                              [pb for _, pb in early])
    late = pair_sums([g_in], _pair_swap([g_in[1]], "pair_swap_w_in"), ["w_in"])
    (grad_x, st_in), got_late = _mix_in_bwd(dp, w_in_g, xs, dx1, modr, g_pre_mix, [late[0][1]])
    fulls = [_chip_sum(place, pf, gb, "chip_sum_" + t)
             for (pf, _), gb, t in zip(late + early, list(got_late) + list(got_early), ["w_in", "w_out", "w_up", "w_down"])]

    pad_lanes = lambda s: jnp.pad(s, ((0, 0), (0, D_MODEL - s.shape[1])))
    stats = jnp.concatenate([st_loss, st_ffn, st_out, st_in, st_bin, pad_lanes(st_512), pad_lanes(dwdw)], axis=0)
    (g_w_in, g_w_out, g_w_up, g_w_down), gath = _final_exchange(fulls, stats)
    small = {"b_ada": (b_ada, m_b_ada, v_b_ada), "lb_logits": (lb_logits, m_lb_logits, v_lb_logits),
             "g_pre_mix": (g_pre_mix, m_g_pre_mix, v_g_pre_mix), "b_in": (b_in, m_b_in, v_b_in),
             "b_dw": (b_dw, m_b_dw, v_b_dw), "gn_gain": (gn_gain, m_gn_gain, v_gn_gain),
             "gn_bias": (gn_bias, m_gn_bias, v_gn_bias), "g_hgrn_out": (g_hgrn_out, m_g_hgrn_out, v_g_hgrn_out),
             "g_post_mix": (g_post_mix, m_g_post_mix, v_g_post_mix), "g_pre_ffn": (g_pre_ffn, m_g_pre_ffn, v_g_pre_ffn),
             "g_post_ffn": (g_post_ffn, m_g_post_ffn, v_g_post_ffn)}
    loss_t, dmod_all, dwdw_sum, sres = _small_update(gath, small)
    loss = loss_t[0, 0]

    res = dict(sres)
    dmod_sh = lax.dynamic_slice_in_dim(dmod_all, chip * ada_cols, ada_cols, axis=1)
    res["w_ada"] = [t[None] for t in _ada_grad_adam(jnp.transpose(c8), dmod_sh, w_ada[0], m_w_ada[0], v_w_ada[0])]
    g_wdw = lax.dynamic_slice_in_dim(dwdw_sum, chip * HEAD_D, HEAD_D, axis=1)[:CONV_K][None]
    res["w_dw"] = [g_wdw] + list(_wdw_adam(w_dw, g_wdw, m_w_dw, v_w_dw))
    for name, g, w, m, v in (("w_in", g_w_in, w_in, m_w_in, v_w_in), ("w_out", g_w_out, w_out, m_w_out, v_w_out),
                             ("w_up", g_w_up, w_up, m_w_up, v_w_up), ("w_down", g_w_down, w_down, m_w_down, v_w_down)):
        d, m2, v2 = _adam_big(w[0], g, m[0], v[0], "adam_" + name)
        res[name] = [g[None], d[None], m2[None], v2[None]]

    order = ["w_ada", "b_ada", "lb_logits", "g_pre_mix", "w_in", "b_in", "w_dw", "b_dw", "gn_gain", "gn_bias",
             "g_hgrn_out", "w_out", "g_post_mix", "g_pre_ffn", "w_up", "w_down", "g_post_ffn"]
    out = [loss, grad_x[None]]
    for k in range(4):
        out += [res[n][k] for n in order]
    return tuple(out)
```
